```python
import math
import jax, jax.numpy as jnp
from jax import lax
import numpy as np

D_MODEL = 1024
BATCH = 8
SEQ = 4096
DEPTH = 2

N_MIXERS = 2
N_SGU_LAYERS = (DEPTH + 1) // 2
N_SWA_LAYERS = DEPTH // 2

CHUNK = 128
SGU_WIDTH = 2 * D_MODEL
SGU_GROUPS = 16
SGU_GROUP_DIM = SGU_WIDTH // SGU_GROUPS

HEAD_DIM = 64
N_HEADS = D_MODEL // HEAD_DIM
N_KV_HEADS = 4
KV_GROUP = N_HEADS // N_KV_HEADS
WINDOW = 128
BLOCK = WINDOW

REL_BUCKETS = 32
REL_MAX_DIST = 128

D_FF = 2816
CONV_WIDTH = 3

EPS = 1e-6

kernel_name = "hybrid_sgu_swa_convffn"


def rms_norm(x, gain):
    xf = x.astype(jnp.float32)
    y = xf * lax.rsqrt(jnp.mean(xf * xf, axis=-1, keepdims=True) + EPS)
    return (y * gain.astype(jnp.float32)).astype(x.dtype)


def _rel_buckets():
    q = np.arange(BLOCK)[:, None] + BLOCK
    k = np.arange(2 * BLOCK)[None, :]
    dist = q - k
    n = np.maximum(dist, 0)
    max_exact = REL_BUCKETS // 2
    large = max_exact + (np.log(np.maximum(n, 1).astype(np.float32) / max_exact)
                         / math.log(REL_MAX_DIST / max_exact)
                         * (REL_BUCKETS - max_exact)).astype(np.int32)
    large = np.minimum(large, REL_BUCKETS - 1)
    bucket = np.where(n < max_exact, n, large).astype(np.int32)
    return dist, bucket


def sgu_mixer(h, w_in, v_gain, w_s, b_s, w_out):
    B, S, _ = h.shape
    nc = S // CHUNK
    z = jax.nn.gelu(h @ w_in, approximate=False)
    u, v = jnp.split(z, 2, axis=-1)
    v = rms_norm(v, v_gain).reshape(B, nc, CHUNK, SGU_GROUPS, SGU_GROUP_DIM)
    w_causal = jnp.tril(w_s)
    s = jnp.einsum('gts,bcsgd->bctgd', w_causal, v) + b_s.T[:, :, None]
    return (u * s.reshape(B, S, SGU_WIDTH)) @ w_out


def swa_mixer(h, w_qkv, q_gain, k_gain, sinks, w_o, rel_bias):
    B, S, _ = h.shape
    nb = S // BLOCK
    qkv = h @ w_qkv
    q, k, v = jnp.split(qkv, [N_HEADS * HEAD_DIM, (N_HEADS + N_KV_HEADS) * HEAD_DIM], axis=-1)
    q = rms_norm(q.reshape(B, S, N_HEADS, HEAD_DIM), q_gain)
    k = rms_norm(k.reshape(B, S, N_KV_HEADS, HEAD_DIM), k_gain)
    v = v.reshape(B, S, N_KV_HEADS, HEAD_DIM)
    q = q.reshape(B, nb, BLOCK, N_KV_HEADS, KV_GROUP, HEAD_DIM)

    def band(t):
        t = t.reshape(B, nb, BLOCK, N_KV_HEADS, HEAD_DIM)
        prev = jnp.pad(t, ((0, 0), (1, 0), (0, 0), (0, 0), (0, 0)))[:, :-1]
        return jnp.concatenate([prev, t], axis=2)

    kb, vb = band(k), band(v)
    scores = jnp.einsum('bnqhgd,bnkhd->bnhgqk', q, kb).astype(jnp.float32) * (HEAD_DIM ** -0.5)

    dist, bucket = _rel_buckets()
    bias = rel_bias.astype(jnp.float32)[bucket]
    bias = jnp.transpose(bias, (2, 0, 1)).reshape(N_KV_HEADS, KV_GROUP, BLOCK, 2 * BLOCK)
    scores = scores + bias

    blk = np.arange(nb)[:, None, None]
    kk = np.arange(2 * BLOCK)[None, None, :]
    valid = ((dist >= 0) & (dist < WINDOW))[None] & ((blk > 0) | (kk >= BLOCK))
    scores = jnp.where(valid[None, :, None, None], scores, -jnp.inf)

    sink = sinks.astype(jnp.float32).reshape(N_KV_HEADS, KV_GROUP)[None, None, :, :, None, None]
    m = jnp.maximum(jnp.max(scores, axis=-1, keepdims=True), sink)
    p = jnp.exp(scores - m)
    denom = jnp.sum(p, axis=-1, keepdims=True) + jnp.exp(sink - m)
    p = (p / denom).astype(vb.dtype)
    o = jnp.einsum('bnhgqk,bnkhd->bnqhgd', p, vb).reshape(B, S, N_HEADS * HEAD_DIM)
    return o @ w_o


def conv_glu_ffn(h, w_up, conv_w, conv_b, w_down):
    a = h @ w_up
    C = a.shape[-1]
    c = lax.conv_general_dilated(a, conv_w[:, None, :].astype(a.dtype), window_strides=(1,),
                                 padding=[(CONV_WIDTH - 1, 0)],
                                 dimension_numbers=('NWC', 'WIO', 'NWC'),
                                 feature_group_count=C) + conv_b
    g, val = jnp.split(c, 2, axis=-1)
    return (jax.nn.silu(g) * val) @ w_down


def _fwd_setup_inputs(seed: int = 0) -> dict:
    key = jax.random.key(seed)
    ks = jax.random.split(key, 20)
    f32 = jnp.float32
    nrm = lambda k, shape, scale: jax.random.normal(k, shape, f32) * scale
    return {
        "x": nrm(ks[0], (BATCH, SEQ, D_MODEL), 1.0),
        "mix_norm": 1.0 + nrm(ks[1], (DEPTH, D_MODEL), 0.05),
        "ffn_norm": 1.0 + nrm(ks[2], (DEPTH, D_MODEL), 0.05),
        "sgu_w_in": nrm(ks[3], (N_SGU_LAYERS, D_MODEL, 2 * SGU_WIDTH), D_MODEL ** -0.5),
        "sgu_v_gain": 1.0 + nrm(ks[4], (N_SGU_LAYERS, SGU_WIDTH), 0.05),
        "sgu_w_s": nrm(ks[5], (N_SGU_LAYERS, SGU_GROUPS, CHUNK, CHUNK), CHUNK ** -0.5),
        "sgu_b_s": 1.0 + nrm(ks[6], (N_SGU_LAYERS, SGU_GROUPS, CHUNK), 0.05),
        "sgu_w_out": nrm(ks[7], (N_SGU_LAYERS, SGU_WIDTH, D_MODEL), SGU_WIDTH ** -0.5),
        "attn_w_qkv": nrm(ks[8], (N_SWA_LAYERS, D_MODEL, (N_HEADS + 2 * N_KV_HEADS) * HEAD_DIM), D_MODEL ** -0.5),
        "attn_q_gain": 1.0 + nrm(ks[9], (N_SWA_LAYERS, HEAD_DIM), 0.05),
        "attn_k_gain": 1.0 + nrm(ks[10], (N_SWA_LAYERS, HEAD_DIM), 0.05),
        "attn_sinks": nrm(ks[11], (N_SWA_LAYERS, N_HEADS), 0.5),
        "attn_w_o": nrm(ks[12], (N_SWA_LAYERS, N_HEADS * HEAD_DIM, D_MODEL), (N_HEADS * HEAD_DIM) ** -0.5),
        "rel_bias": nrm(ks[13], (REL_BUCKETS, N_HEADS), 0.5),
        "ffn_w_up": nrm(ks[14], (DEPTH, D_MODEL, 2 * D_FF), D_MODEL ** -0.5),
        "ffn_conv_w": nrm(ks[15], (DEPTH, CONV_WIDTH, 2 * D_FF), CONV_WIDTH ** -0.5),
        "ffn_conv_b": nrm(ks[16], (DEPTH, 2 * D_FF), 0.02),
        "ffn_w_down": nrm(ks[17], (DEPTH, D_FF, D_MODEL), D_FF ** -0.5),
    }


def _fwd_reference(x, mix_norm, ffn_norm, sgu_w_in, sgu_v_gain, sgu_w_s, sgu_b_s, sgu_w_out,
              attn_w_qkv, attn_q_gain, attn_k_gain, attn_sinks, attn_w_o, rel_bias,
              ffn_w_up, ffn_conv_w, ffn_conv_b, ffn_w_down):
    h = x
    for i in range(DEPTH):
        j = i // N_MIXERS
        hn = rms_norm(h, mix_norm[i])
        if i % N_MIXERS == 0:
            h = h + sgu_mixer(hn, sgu_w_in[j], sgu_v_gain[j], sgu_w_s[j], sgu_b_s[j], sgu_w_out[j])
        else:
            h = h + swa_mixer(hn, attn_w_qkv[j], attn_q_gain[j], attn_k_gain[j], attn_sinks[j],
                              attn_w_o[j], rel_bias)
        h = h + conv_glu_ffn(rms_norm(h, ffn_norm[i]), ffn_w_up[i], ffn_conv_w[i], ffn_conv_b[i], ffn_w_down[i])
    return h


import jax as _jax
import jax.numpy as _jnp

TWIN_FORMAT = 'train_step'
FWD_PARAMS = ['x', 'mix_norm', 'ffn_norm', 'sgu_w_in', 'sgu_v_gain', 'sgu_w_s', 'sgu_b_s', 'sgu_w_out', 'attn_w_qkv', 'attn_q_gain', 'attn_k_gain', 'attn_sinks', 'attn_w_o', 'rel_bias', 'ffn_w_up', 'ffn_conv_w', 'ffn_conv_b', 'ffn_w_down']
TWIN_WEIGHTS = ['mix_norm', 'ffn_norm', 'sgu_w_in', 'sgu_v_gain', 'sgu_w_s', 'sgu_b_s', 'sgu_w_out', 'attn_w_qkv', 'attn_q_gain', 'attn_k_gain', 'attn_sinks', 'attn_w_o', 'rel_bias', 'ffn_w_up', 'ffn_conv_w', 'ffn_conv_b', 'ffn_w_down']
TWIN_DIFF_INPUT = 'x'
TWIN_INPUTS = ['x', 'mix_norm', 'ffn_norm', 'sgu_w_in', 'sgu_v_gain', 'sgu_w_s', 'sgu_b_s', 'sgu_w_out', 'attn_w_qkv', 'attn_q_gain', 'attn_k_gain', 'attn_sinks', 'attn_w_o', 'rel_bias', 'ffn_w_up', 'ffn_conv_w', 'ffn_conv_b', 'ffn_w_down', 'loss_target', 'm_mix_norm', 'm_ffn_norm', 'm_sgu_w_in', 'm_sgu_v_gain', 'm_sgu_w_s', 'm_sgu_b_s', 'm_sgu_w_out', 'm_attn_w_qkv', 'm_attn_q_gain', 'm_attn_k_gain', 'm_attn_sinks', 'm_attn_w_o', 'm_rel_bias', 'm_ffn_w_up', 'm_ffn_conv_w', 'm_ffn_conv_b', 'm_ffn_w_down', 'v_mix_norm', 'v_ffn_norm', 'v_sgu_w_in', 'v_sgu_v_gain', 'v_sgu_w_s', 'v_sgu_b_s', 'v_sgu_w_out', 'v_attn_w_qkv', 'v_attn_q_gain', 'v_attn_k_gain', 'v_attn_sinks', 'v_attn_w_o', 'v_rel_bias', 'v_ffn_w_up', 'v_ffn_conv_w', 'v_ffn_conv_b', 'v_ffn_w_down']
TWIN_OUTPUTS = ['loss', 'grad_x', 'grad_mix_norm', 'grad_ffn_norm', 'grad_sgu_w_in', 'grad_sgu_v_gain', 'grad_sgu_w_s', 'grad_sgu_b_s', 'grad_sgu_w_out', 'grad_attn_w_qkv', 'grad_attn_q_gain', 'grad_attn_k_gain', 'grad_attn_sinks', 'grad_attn_w_o', 'grad_rel_bias', 'grad_ffn_w_up', 'grad_ffn_conv_w', 'grad_ffn_conv_b', 'grad_ffn_w_down', 'delta_mix_norm', 'delta_ffn_norm', 'delta_sgu_w_in', 'delta_sgu_v_gain', 'delta_sgu_w_s', 'delta_sgu_b_s', 'delta_sgu_w_out', 'delta_attn_w_qkv', 'delta_attn_q_gain', 'delta_attn_k_gain', 'delta_attn_sinks', 'delta_attn_w_o', 'delta_rel_bias', 'delta_ffn_w_up', 'delta_ffn_conv_w', 'delta_ffn_conv_b', 'delta_ffn_w_down', 'new_m_mix_norm', 'new_m_ffn_norm', 'new_m_sgu_w_in', 'new_m_sgu_v_gain', 'new_m_sgu_w_s', 'new_m_sgu_b_s', 'new_m_sgu_w_out', 'new_m_attn_w_qkv', 'new_m_attn_q_gain', 'new_m_attn_k_gain', 'new_m_attn_sinks', 'new_m_attn_w_o', 'new_m_rel_bias', 'new_m_ffn_w_up', 'new_m_ffn_conv_w', 'new_m_ffn_conv_b', 'new_m_ffn_w_down', 'new_v_mix_norm', 'new_v_ffn_norm', 'new_v_sgu_w_in', 'new_v_sgu_v_gain', 'new_v_sgu_w_s', 'new_v_sgu_b_s', 'new_v_sgu_w_out', 'new_v_attn_w_qkv', 'new_v_attn_q_gain', 'new_v_attn_k_gain', 'new_v_attn_sinks', 'new_v_attn_w_o', 'new_v_rel_bias', 'new_v_ffn_w_up', 'new_v_ffn_conv_w', 'new_v_ffn_conv_b', 'new_v_ffn_w_down']
TWIN_LEAF_KINDS = {'loss': 'loss', 'grad_x': 'grad_x', 'grad_mix_norm': 'grad_w', 'grad_ffn_norm': 'grad_w', 'grad_sgu_w_in': 'grad_w', 'grad_sgu_v_gain': 'grad_w', 'grad_sgu_w_s': 'grad_w', 'grad_sgu_b_s': 'grad_w', 'grad_sgu_w_out': 'grad_w', 'grad_attn_w_qkv': 'grad_w', 'grad_attn_q_gain': 'grad_w', 'grad_attn_k_gain': 'grad_w', 'grad_attn_sinks': 'grad_w', 'grad_attn_w_o': 'grad_w', 'grad_rel_bias': 'grad_w', 'grad_ffn_w_up': 'grad_w', 'grad_ffn_conv_w': 'grad_w', 'grad_ffn_conv_b': 'grad_w', 'grad_ffn_w_down': 'grad_w', 'delta_mix_norm': 'delta_w', 'delta_ffn_norm': 'delta_w', 'delta_sgu_w_in': 'delta_w', 'delta_sgu_v_gain': 'delta_w', 'delta_sgu_w_s': 'delta_w', 'delta_sgu_b_s': 'delta_w', 'delta_sgu_w_out': 'delta_w', 'delta_attn_w_qkv': 'delta_w', 'delta_attn_q_gain': 'delta_w', 'delta_attn_k_gain': 'delta_w', 'delta_attn_sinks': 'delta_w', 'delta_attn_w_o': 'delta_w', 'delta_rel_bias': 'delta_w', 'delta_ffn_w_up': 'delta_w', 'delta_ffn_conv_w': 'delta_w', 'delta_ffn_conv_b': 'delta_w', 'delta_ffn_w_down': 'delta_w', 'new_m_mix_norm': 'new_m', 'new_m_ffn_norm': 'new_m', 'new_m_sgu_w_in': 'new_m', 'new_m_sgu_v_gain': 'new_m', 'new_m_sgu_w_s': 'new_m', 'new_m_sgu_b_s': 'new_m', 'new_m_sgu_w_out': 'new_m', 'new_m_attn_w_qkv': 'new_m', 'new_m_attn_q_gain': 'new_m', 'new_m_attn_k_gain': 'new_m', 'new_m_attn_sinks': 'new_m', 'new_m_attn_w_o': 'new_m', 'new_m_rel_bias': 'new_m', 'new_m_ffn_w_up': 'new_m', 'new_m_ffn_conv_w': 'new_m', 'new_m_ffn_conv_b': 'new_m', 'new_m_ffn_w_down': 'new_m', 'new_v_mix_norm': 'new_v', 'new_v_ffn_norm': 'new_v', 'new_v_sgu_w_in': 'new_v', 'new_v_sgu_v_gain': 'new_v', 'new_v_sgu_w_s': 'new_v', 'new_v_sgu_b_s': 'new_v', 'new_v_sgu_w_out': 'new_v', 'new_v_attn_w_qkv': 'new_v', 'new_v_attn_q_gain': 'new_v', 'new_v_attn_k_gain': 'new_v', 'new_v_attn_sinks': 'new_v', 'new_v_attn_w_o': 'new_v', 'new_v_rel_bias': 'new_v', 'new_v_ffn_w_up': 'new_v', 'new_v_ffn_conv_w': 'new_v', 'new_v_ffn_conv_b': 'new_v', 'new_v_ffn_w_down': 'new_v'}


def _forward(args):
    return _fwd_reference(*[args[k] for k in FWD_PARAMS])


def _output_shape():
    out = _jax.eval_shape(lambda: _forward(_fwd_setup_inputs(0)))
    return out.shape, out.dtype

N_MICROBATCH = 1
ADAM_LR = 0.001
ADAM_B1 = 0.9
ADAM_B2 = 0.999
ADAM_EPS = 1e-08
ADAM_WD = 0.01
ADAM_STEP = 10
PER_EXAMPLE_BATCH_AXIS = {'x': 0, 'loss_target': 0}
SHARED_INPUTS = []
_WEIGHT_DTYPES = {'mix_norm': _jnp.float32, 'ffn_norm': _jnp.float32, 'sgu_w_in': _jnp.float32, 'sgu_v_gain': _jnp.float32, 'sgu_w_s': _jnp.float32, 'sgu_b_s': _jnp.float32, 'sgu_w_out': _jnp.float32, 'attn_w_qkv': _jnp.float32, 'attn_q_gain': _jnp.float32, 'attn_k_gain': _jnp.float32, 'attn_sinks': _jnp.float32, 'attn_w_o': _jnp.float32, 'rel_bias': _jnp.float32, 'ffn_w_up': _jnp.float32, 'ffn_conv_w': _jnp.float32, 'ffn_conv_b': _jnp.float32, 'ffn_w_down': _jnp.float32}
MOMENT_SCALE = {'mix_norm': 1.748721e+01, 'ffn_norm': 2.658096e+01, 'sgu_w_in': 3.930121e-01, 'sgu_v_gain': 3.127073e+00, 'sgu_w_s': 2.186672e+00, 'sgu_b_s': 7.452295e+00, 'sgu_w_out': 6.633380e+00, 'attn_w_qkv': 2.817945e+00, 'attn_q_gain': 6.224910e+00, 'attn_k_gain': 6.193285e+00, 'attn_sinks': 9.691114e-01, 'attn_w_o': 2.601957e+00, 'rel_bias': 6.050231e-01, 'ffn_w_up': 1.179959e+00, 'ffn_conv_w': 4.090881e+00, 'ffn_conv_b': 4.141866e+00, 'ffn_w_down': 9.397110e-01}


def _to_microbatches(a, axis):
    t = _jnp.moveaxis(a, axis, 0)
    t = t.reshape((N_MICROBATCH, t.shape[0] // N_MICROBATCH) + t.shape[1:])
    return _jnp.moveaxis(t, 1, axis + 1)


def setup_inputs(seed: int = 0) -> dict:
    inp = _fwd_setup_inputs(seed)
    key = _jax.random.fold_in(_jax.random.key(seed), 7919)
    shape, _ = _output_shape()
    out = dict(inp)
    out["loss_target"] = _jax.random.normal(_jax.random.fold_in(key, 0), shape, _jnp.float32)
    for i, name in enumerate(TWIN_WEIGHTS):
        w = inp[name].astype(_jnp.float32)
        if MOMENT_SCALE is None:
            s = _jnp.sqrt(_jnp.mean(_jnp.square(w)) + 1e-30)
        else:
            s = MOMENT_SCALE[name]
        km, kv = _jax.random.split(_jax.random.fold_in(key, i + 1))
        out[name] = w
        out["m_" + name] = s * _jax.random.normal(km, w.shape, _jnp.float32)
        out["v_" + name] = (s * s) * _jax.random.uniform(kv, w.shape, _jnp.float32, 0.5, 1.5)
    if N_MICROBATCH > 1:
        for name, axis in PER_EXAMPLE_BATCH_AXIS.items():
            out[name] = _to_microbatches(out[name], axis)
    return {'x': out['x'], 'mix_norm': out['mix_norm'], 'ffn_norm': out['ffn_norm'], 'sgu_w_in': out['sgu_w_in'], 'sgu_v_gain': out['sgu_v_gain'], 'sgu_w_s': out['sgu_w_s'], 'sgu_b_s': out['sgu_b_s'], 'sgu_w_out': out['sgu_w_out'], 'attn_w_qkv': out['attn_w_qkv'], 'attn_q_gain': out['attn_q_gain'], 'attn_k_gain': out['attn_k_gain'], 'attn_sinks': out['attn_sinks'], 'attn_w_o': out['attn_w_o'], 'rel_bias': out['rel_bias'], 'ffn_w_up': out['ffn_w_up'], 'ffn_conv_w': out['ffn_conv_w'], 'ffn_conv_b': out['ffn_conv_b'], 'ffn_w_down': out['ffn_w_down'], 'loss_target': out['loss_target'], 'm_mix_norm': out['m_mix_norm'], 'm_ffn_norm': out['m_ffn_norm'], 'm_sgu_w_in': out['m_sgu_w_in'], 'm_sgu_v_gain': out['m_sgu_v_gain'], 'm_sgu_w_s': out['m_sgu_w_s'], 'm_sgu_b_s': out['m_sgu_b_s'], 'm_sgu_w_out': out['m_sgu_w_out'], 'm_attn_w_qkv': out['m_attn_w_qkv'], 'm_attn_q_gain': out['m_attn_q_gain'], 'm_attn_k_gain': out['m_attn_k_gain'], 'm_attn_sinks': out['m_attn_sinks'], 'm_attn_w_o': out['m_attn_w_o'], 'm_rel_bias': out['m_rel_bias'], 'm_ffn_w_up': out['m_ffn_w_up'], 'm_ffn_conv_w': out['m_ffn_conv_w'], 'm_ffn_conv_b': out['m_ffn_conv_b'], 'm_ffn_w_down': out['m_ffn_w_down'], 'v_mix_norm': out['v_mix_norm'], 'v_ffn_norm': out['v_ffn_norm'], 'v_sgu_w_in': out['v_sgu_w_in'], 'v_sgu_v_gain': out['v_sgu_v_gain'], 'v_sgu_w_s': out['v_sgu_w_s'], 'v_sgu_b_s': out['v_sgu_b_s'], 'v_sgu_w_out': out['v_sgu_w_out'], 'v_attn_w_qkv': out['v_attn_w_qkv'], 'v_attn_q_gain': out['v_attn_q_gain'], 'v_attn_k_gain': out['v_attn_k_gain'], 'v_attn_sinks': out['v_attn_sinks'], 'v_attn_w_o': out['v_attn_w_o'], 'v_rel_bias': out['v_rel_bias'], 'v_ffn_w_up': out['v_ffn_w_up'], 'v_ffn_conv_w': out['v_ffn_conv_w'], 'v_ffn_conv_b': out['v_ffn_conv_b'], 'v_ffn_w_down': out['v_ffn_w_down']}


def _loss(weights, diff, rest, loss_target):
    with _jax.named_scope("forward"):
        args = {**rest, TWIN_DIFF_INPUT: diff, **{k: w.astype(_WEIGHT_DTYPES[k]) for k, w in weights.items()}}
        y = _forward(args)
    with _jax.named_scope("loss_head"):
        err = _jnp.square(y.astype(_jnp.float32) - loss_target)
        return 0.5 * _jnp.sum(_jnp.mean(err, axis=-1)) if err.ndim else 0.5 * err


def _adamw(w, g, m, v):
    m = ADAM_B1 * m + (1.0 - ADAM_B1) * g
    v = ADAM_B2 * v + (1.0 - ADAM_B2) * _jnp.square(g)
    m_hat = m / (1.0 - ADAM_B1 ** ADAM_STEP)
    v_hat = v / (1.0 - ADAM_B2 ** ADAM_STEP)
    delta = -ADAM_LR * (m_hat / (_jnp.sqrt(v_hat) + ADAM_EPS) + ADAM_WD * w)
    return delta, m, v


def reference(x, mix_norm, ffn_norm, sgu_w_in, sgu_v_gain, sgu_w_s, sgu_b_s, sgu_w_out, attn_w_qkv, attn_q_gain, attn_k_gain, attn_sinks, attn_w_o, rel_bias, ffn_w_up, ffn_conv_w, ffn_conv_b, ffn_w_down, loss_target, m_mix_norm, m_ffn_norm, m_sgu_w_in, m_sgu_v_gain, m_sgu_w_s, m_sgu_b_s, m_sgu_w_out, m_attn_w_qkv, m_attn_q_gain, m_attn_k_gain, m_attn_sinks, m_attn_w_o, m_rel_bias, m_ffn_w_up, m_ffn_conv_w, m_ffn_conv_b, m_ffn_w_down, v_mix_norm, v_ffn_norm, v_sgu_w_in, v_sgu_v_gain, v_sgu_w_s, v_sgu_b_s, v_sgu_w_out, v_attn_w_qkv, v_attn_q_gain, v_attn_k_gain, v_attn_sinks, v_attn_w_o, v_rel_bias, v_ffn_w_up, v_ffn_conv_w, v_ffn_conv_b, v_ffn_w_down):
    given = dict(x=x, mix_norm=mix_norm, ffn_norm=ffn_norm, sgu_w_in=sgu_w_in, sgu_v_gain=sgu_v_gain, sgu_w_s=sgu_w_s, sgu_b_s=sgu_b_s, sgu_w_out=sgu_w_out, attn_w_qkv=attn_w_qkv, attn_q_gain=attn_q_gain, attn_k_gain=attn_k_gain, attn_sinks=attn_sinks, attn_w_o=attn_w_o, rel_bias=rel_bias, ffn_w_up=ffn_w_up, ffn_conv_w=ffn_conv_w, ffn_conv_b=ffn_conv_b, ffn_w_down=ffn_w_down, loss_target=loss_target, m_mix_norm=m_mix_norm, m_ffn_norm=m_ffn_norm, m_sgu_w_in=m_sgu_w_in, m_sgu_v_gain=m_sgu_v_gain, m_sgu_w_s=m_sgu_w_s, m_sgu_b_s=m_sgu_b_s, m_sgu_w_out=m_sgu_w_out, m_attn_w_qkv=m_attn_w_qkv, m_attn_q_gain=m_attn_q_gain, m_attn_k_gain=m_attn_k_gain, m_attn_sinks=m_attn_sinks, m_attn_w_o=m_attn_w_o, m_rel_bias=m_rel_bias, m_ffn_w_up=m_ffn_w_up, m_ffn_conv_w=m_ffn_conv_w, m_ffn_conv_b=m_ffn_conv_b, m_ffn_w_down=m_ffn_w_down, v_mix_norm=v_mix_norm, v_ffn_norm=v_ffn_norm, v_sgu_w_in=v_sgu_w_in, v_sgu_v_gain=v_sgu_v_gain, v_sgu_w_s=v_sgu_w_s, v_sgu_b_s=v_sgu_b_s, v_sgu_w_out=v_sgu_w_out, v_attn_w_qkv=v_attn_w_qkv, v_attn_q_gain=v_attn_q_gain, v_attn_k_gain=v_attn_k_gain, v_attn_sinks=v_attn_sinks, v_attn_w_o=v_attn_w_o, v_rel_bias=v_rel_bias, v_ffn_w_up=v_ffn_w_up, v_ffn_conv_w=v_ffn_conv_w, v_ffn_conv_b=v_ffn_conv_b, v_ffn_w_down=v_ffn_w_down)
    weights = {n: given[n] for n in TWIN_WEIGHTS}
    shared = {n: given[n] for n in SHARED_INPUTS}
    per_example = {n: given[n] for n in ['x']}
    grad_fn = _jax.value_and_grad(_loss, argnums=(0, 1))

    def one_microbatch(ex, loss_target):
        ex = dict(ex)
        diff = ex.pop(TWIN_DIFF_INPUT)
        return grad_fn(weights, diff, {**shared, **ex}, loss_target)

    if N_MICROBATCH == 1:
        loss, (grad_w, grad_x) = one_microbatch(per_example, given["loss_target"])
    else:
        def body(carry, xs):
            loss_sum, grad_sum = carry
            l_k, (gw_k, gx_k) = one_microbatch(xs[0], xs[1])
            with _jax.named_scope("update"):
                return (loss_sum + l_k, _jax.tree.map(_jnp.add, grad_sum, gw_k)), gx_k

        init = (_jnp.zeros((), _jnp.float32), _jax.tree.map(_jnp.zeros_like, weights))
        (loss, grad_w), grad_x = _jax.lax.scan(body, init, (per_example, given["loss_target"]))
    with _jax.named_scope("update"):
        delta_w, new_m, new_v = {}, {}, {}
        for n in TWIN_WEIGHTS:
            delta_w[n], new_m[n], new_v[n] = _adamw(weights[n], grad_w[n], given["m_" + n], given["v_" + n])
    return (loss, grad_x, *[grad_w[n] for n in TWIN_WEIGHTS], *[delta_w[n] for n in TWIN_WEIGHTS],
            *[new_m[n] for n in TWIN_WEIGHTS], *[new_v[n] for n in TWIN_WEIGHTS])
```

```python
import functools
import math

import numpy as np
import jax
import jax.numpy as jnp
from jax import lax
from jax.experimental import pallas as pl
from jax.experimental.pallas import tpu as pltpu

F32 = jnp.float32
BF16 = jnp.bfloat16
S = jax.ShapeDtypeStruct

D = 1024
CHUNK = 128
SGU_W = 2048
SGU_G = 16
HD = 64
NH = 16
NKV = 4
KVG = 4
D_FF = 2816
REL_BUCKETS = 32
REL_MAX_DIST = 128
EPS = 1e-6
N_DEV = 8
MESH = pl.DeviceIdType.MESH

ADAM_LR = 0.001
ADAM_B1 = 0.9
ADAM_B2 = 0.999
ADAM_EPS = 1e-08
ADAM_WD = 0.01
ADAM_STEP = 10

ROW_TILE = 512
HALO = 8


def _tm(t):
    return min(ROW_TILE, t)


def _cp(*sem):
    return pltpu.CompilerParams(dimension_semantics=sem)


def _dot(a, b):
    return jnp.dot(a, b, preferred_element_type=F32)


def _dot_nt(a, b):
    return lax.dot_general(a, b, (((1,), (1,)), ((), ())), preferred_element_type=F32)


def _dot_tn(a, b):
    return lax.dot_general(a, b, (((0,), (0,)), ((), ())), preferred_element_type=F32)


def _gelu(x):
    return 0.5 * x * (1.0 + lax.erf(x * (2.0 ** -0.5)))


def _gelu_grad(x):
    return 0.5 * (1.0 + lax.erf(x * (2.0 ** -0.5))) + x * jnp.exp(-0.5 * x * x) * (1.0 / math.sqrt(2.0 * math.pi))


def _sigmoid(x):
    return 1.0 / (1.0 + jnp.exp(-x))


def _rstd(x):
    return lax.rsqrt(jnp.mean(x * x, axis=-1, keepdims=True) + EPS)


def _rel_tables():
    q = np.arange(CHUNK)[:, None] + CHUNK
    k = np.arange(2 * CHUNK)[None, :]
    dist = q - k
    n = np.maximum(dist, 0)
    max_exact = REL_BUCKETS // 2
    large = max_exact + (np.log(np.maximum(n, 1).astype(np.float32) / max_exact)
                         / math.log(REL_MAX_DIST / max_exact) * (REL_BUCKETS - max_exact)).astype(np.int32)
    large = np.minimum(large, REL_BUCKETS - 1)
    return np.where(n < max_exact, n, large).astype(np.int32)


def _rmsnorm(x, gain, name):
    t = x.shape[0]
    tm = _tm(t)

    def body(x_ref, g_ref, o_ref):
        xv = x_ref[...]
        o_ref[...] = (xv * _rstd(xv) * g_ref[...]).astype(BF16)

    return pl.pallas_call(
        body, grid=(t // tm,), name=name,
        in_specs=[pl.BlockSpec((tm, D), lambda i: (i, 0)), pl.BlockSpec((1, D), lambda i: (0, 0))],
        out_specs=pl.BlockSpec((tm, D), lambda i: (i, 0)),
        out_shape=S((t, D), BF16), compiler_params=_cp("parallel"))(x, gain)


def _mm_slot(hn, wg, name):
    t, k = hn.shape
    ns, _, n = wg.shape
    tm = _tm(t)

    def body(a_ref, w_ref, o_ref):
        o_ref[...] = _dot(a_ref[...], w_ref[...])

    return pl.pallas_call(
        body, grid=(t // tm, ns), name=name,
        in_specs=[pl.BlockSpec((tm, k), lambda i, j: (i, 0)), pl.BlockSpec((None, k, n), lambda i, j: (j, 0, 0))],
        out_specs=pl.BlockSpec((None, tm, n), lambda i, j: (j, i, 0)),
        out_shape=S((ns, t, n), F32), compiler_params=_cp("parallel", "parallel"))(hn, wg)


def _conv3(a, prev, cw, cb, tm):
    ext = jnp.concatenate([prev, a], axis=0)
    a1 = ext[HALO - 1:HALO - 1 + tm]
    a2 = ext[HALO - 2:HALO - 2 + tm]
    return cw[2:3] * a + cw[1:2] * a1 + cw[0:1] * a2 + cb, a1, a2


def _ffn_up(hn, wg, cw, cb, name):
    t, k = hn.shape
    n = wg.shape[-1]
    nh = wg.shape[0] // 2
    tm = _tm(t)

    def body(a_ref, wg_ref, wv_ref, cwg_ref, cwv_ref, cbg_ref, cbv_ref, ag_ref, av_ref, act_ref, carry):
        i = pl.program_id(1)

        @pl.when(i == 0)
        def _():
            carry[...] = jnp.zeros_like(carry)

        a = a_ref[...]
        ag = _dot(a, wg_ref[...])
        av = _dot(a, wv_ref[...])
        ag_ref[...] = ag
        av_ref[...] = av
        cg, _, _ = _conv3(ag, carry[0], cwg_ref[...], cbg_ref[...], tm)
        cv, _, _ = _conv3(av, carry[1], cwv_ref[...], cbv_ref[...], tm)
        carry[0] = ag[tm - HALO:]
        carry[1] = av[tm - HALO:]
        act_ref[...] = (cg * _sigmoid(cg) * cv).astype(BF16)

    wspec = lambda off: pl.BlockSpec((None, k, n), lambda j, i: (j + off, 0, 0))
    cwspec = lambda off: pl.BlockSpec((None, 3, n), lambda j, i: (j + off, 0, 0))
    cbspec = lambda off: pl.BlockSpec((None, 1, n), lambda j, i: (j + off, 0, 0))
    ospec = pl.BlockSpec((None, tm, n), lambda j, i: (j, i, 0))
    return pl.pallas_call(
        body, grid=(nh, t // tm), name=name,
        in_specs=[pl.BlockSpec((tm, k), lambda j, i: (i, 0)), wspec(0), wspec(nh), cwspec(0), cwspec(nh), cbspec(0), cbspec(nh)],
        out_specs=[ospec, ospec, ospec],
        out_shape=[S((nh, t, n), F32), S((nh, t, n), F32), S((nh, t, n), BF16)],
        scratch_shapes=[pltpu.VMEM((2, HALO, n), F32)],
        compiler_params=_cp("parallel", "arbitrary"))(hn, wg, wg, cw, cw, cb, cb)


def _tril_mask():
    r = lax.broadcasted_iota(jnp.int32, (CHUNK, CHUNK), 0)
    c = lax.broadcasted_iota(jnp.int32, (CHUNK, CHUNK), 1)
    return r >= c


def _sgu_gate_fwd(a_s, vgain, ws, bst, name):
    t = a_s.shape[1]
    sw = a_s.shape[2]
    gps = sw // CHUNK

    def body(a_ref, vg_ref, ws_ref, b_ref, o_ref):
        v = _gelu(jnp.concatenate([a_ref[4 + s] for s in range(4)], axis=1))
        vn = (v * _rstd(v) * vg_ref[...]).astype(BF16)
        tri = _tril_mask()
        for g in range(SGU_G):
            w = jnp.where(tri, ws_ref[g], 0.0).astype(BF16)
            sg = _dot(w, vn[:, g * CHUNK:(g + 1) * CHUNK]) + b_ref[:, g:g + 1]
            lo = (g % gps) * CHUNK
            u = _gelu(a_ref[g // gps, :, lo:lo + CHUNK])
            o_ref[g // gps, :, lo:lo + CHUNK] = (u * sg).astype(BF16)

    return pl.pallas_call(
        body, grid=(t // CHUNK,), name=name,
        in_specs=[pl.BlockSpec((8, CHUNK, sw), lambda n: (0, n, 0)), pl.BlockSpec((1, SGU_W), lambda n: (0, 0)),
                  pl.BlockSpec((SGU_G, CHUNK, CHUNK), lambda n: (0, 0, 0)), pl.BlockSpec((CHUNK, SGU_G), lambda n: (0, 0))],
        out_specs=pl.BlockSpec((4, CHUNK, sw), lambda n: (0, n, 0)),
        out_shape=S((4, t, sw), BF16), compiler_params=_cp("parallel"))(a_s, vgain, ws, bst)


def _resid_mm(a_s, w, resid, extra, mode, name):
    nk, t, kc = a_s.shape
    tm = _tm(t)
    ni = t // tm

    def body(a_ref, w_ref, r_ref, e_ref, o1_ref, o2_ref, acc):
        j = pl.program_id(1)

        @pl.when(j == 0)
        def _():
            acc[...] = r_ref[...]

        acc[...] += _dot(a_ref[...], w_ref[...])

        @pl.when(j == nk - 1)
        def _():
            h = acc[...]
            if mode == "norm":
                o1_ref[...] = h
                o2_ref[...] = (h * _rstd(h) * e_ref[...]).astype(BF16)
            else:
                err = h - e_ref[...]
                o1_ref[...] = err * (1.0 / D)
                o2_ref[...] = jnp.full(o2_ref.shape, jnp.sum(err * err), F32)

    row = pl.BlockSpec((tm, D), lambda i, j: (i, 0))
    if mode == "norm":
        e_spec, o2_spec, o2_shape = pl.BlockSpec((1, D), lambda i, j: (0, 0)), row, S((t, D), BF16)
    else:
        e_spec, o2_spec, o2_shape = row, pl.BlockSpec((None, 8, 128), lambda i, j: (i, 0, 0)), S((ni, 8, 128), F32)
    return pl.pallas_call(
        body, grid=(ni, nk), name=name,
        in_specs=[pl.BlockSpec((None, tm, kc), lambda i, j: (j, i, 0)), pl.BlockSpec((kc, D), lambda i, j: (j, 0)), row, e_spec],
        out_specs=[row, o2_spec], out_shape=[S((t, D), F32), o2_shape],
        scratch_shapes=[pltpu.VMEM((tm, D), F32)],
        compiler_params=_cp("parallel", "arbitrary"))(a_s, w, resid, extra)


def _relbias_fwd(rel_bias_t, bucket_row, name):
    nb = bucket_row.shape[1]

    def body(rb_ref, bk_ref, o_ref):
        onehot = (lax.broadcasted_iota(jnp.int32, (REL_BUCKETS, nb), 0) == bk_ref[...]).astype(F32)
        o_ref[...] = jnp.dot(rb_ref[...], onehot, precision=lax.Precision.HIGHEST, preferred_element_type=F32)

    return pl.pallas_call(body, out_shape=S((NH, nb), F32), name=name)(rel_bias_t, bucket_row)


def _relbias_bwd(dbias, bucket_row, name):
    nb = bucket_row.shape[1]

    def body(db_ref, bk_ref, o_ref):
        onehot = (lax.broadcasted_iota(jnp.int32, (REL_BUCKETS, nb), 0) == bk_ref[...]).astype(F32)
        o_ref[...] = lax.dot_general(db_ref[...], onehot, (((1,), (1,)), ((), ())),
                                     precision=lax.Precision.HIGHEST, preferred_element_type=F32)

    return pl.pallas_call(body, out_shape=S((NH, REL_BUCKETS), F32), name=name)(dbias, bucket_row)


QKV_SLOT = 192


def _head(ref, col):
    return ref[col // QKV_SLOT, :, col % QKV_SLOT:col % QKV_SLOT + HD]


def _attn_valid(n):
    qi = lax.broadcasted_iota(jnp.int32, (CHUNK, 2 * CHUNK), 0)
    kj = lax.broadcasted_iota(jnp.int32, (CHUNK, 2 * CHUNK), 1)
    dist = qi + CHUNK - kj
    return (dist >= 0) & (dist < CHUNK) & ((n > 0) | (kj >= CHUNK))


def _attn_probs(qn, kn, bias, valid, sink):
    s = _dot_nt(qn, kn) * (HD ** -0.5) + bias
    s = jnp.where(valid, s, -jnp.inf)
    m = jnp.maximum(jnp.max(s, axis=-1, keepdims=True), sink)
    p = jnp.exp(s - m)
    psink = jnp.exp(sink - m)
    inv = 1.0 / (jnp.sum(p, axis=-1, keepdims=True) + psink)
    return p * inv, psink * inv


def _attn_fwd(qkv_s, qg, kg, sinks, bias, name):
    t = qkv_s.shape[1]

    def body(cur_ref, prev_ref, qg_ref, kg_ref, sink_ref, bias_ref, o_ref):
        n = pl.program_id(0)
        valid = _attn_valid(n)
        for h in range(NKV):
            k = jnp.concatenate([_head(prev_ref, D + HD * h), _head(cur_ref, D + HD * h)], axis=0)
            v = jnp.concatenate([_head(prev_ref, D + HD * (NKV + h)), _head(cur_ref, D + HD * (NKV + h))], axis=0)
            kn = (k * _rstd(k) * kg_ref[...]).astype(BF16)
            vb = v.astype(BF16)
            outs = []
            for g in range(KVG):
                hq = KVG * h + g
                q = _head(cur_ref, HD * hq)
                qn = (q * _rstd(q) * qg_ref[...]).astype(BF16)
                p, _ = _attn_probs(qn, kn, bias_ref[hq], valid, sink_ref[hq])
                outs.append(_dot(p.astype(BF16), vb))
            o_ref[0, :, KVG * HD * h:KVG * HD * (h + 1)] = jnp.concatenate(outs, axis=1).astype(BF16)

    blk = lambda f: pl.BlockSpec((8, CHUNK, QKV_SLOT), f)
    return pl.pallas_call(
        body, grid=(t // CHUNK,), name=name,
        in_specs=[blk(lambda n: (0, n, 0)), blk(lambda n: (0, jnp.maximum(n - 1, 0), 0)),
                  pl.BlockSpec((1, HD), lambda n: (0, 0)), pl.BlockSpec((1, HD), lambda n: (0, 0)),
                  pl.BlockSpec(memory_space=pltpu.SMEM), pl.BlockSpec((NH, CHUNK, 2 * CHUNK), lambda n: (0, 0, 0))],
        out_specs=pl.BlockSpec((1, CHUNK, D), lambda n: (0, n, 0)),
        out_shape=S((1, t, D), BF16), compiler_params=_cp("parallel"))(qkv_s, qkv_s, qg, kg, sinks, bias)


def _dx_rows(dh, w, kc, out_dtype, name):
    t = dh.shape[0]
    nk = w.shape[0] // kc
    tm = _tm(t)

    def body(d_ref, w_ref, o_ref):
        o_ref[...] = _dot_nt(d_ref[...].astype(BF16), w_ref[...]).astype(out_dtype)

    return pl.pallas_call(
        body, grid=(t // tm, nk), name=name,
        in_specs=[pl.BlockSpec((tm, D), lambda i, j: (i, 0)), pl.BlockSpec((kc, D), lambda i, j: (j, 0))],
        out_specs=pl.BlockSpec((None, tm, kc), lambda i, j: (j, i, 0)),
        out_shape=S((nk, t, kc), out_dtype), compiler_params=_cp("parallel", "parallel"))(dh, w)


def _ffn_down_bwd(dh, w, a_g, a_v, cw, cb, name):
    nh, t, n = a_g.shape
    tm = _tm(t)

    def body(d_ref, w_ref, ag_ref, av_ref, cwg_ref, cwv_ref, cbg_ref, cbv_ref,
             dcg_ref, dcv_ref, dwg_ref, dwv_ref, dbg_ref, dbv_ref, carry):
        i = pl.program_id(1)

        @pl.when(i == 0)
        def _():
            carry[...] = jnp.zeros_like(carry)
            dwg_ref[...] = jnp.zeros_like(dwg_ref)
            dwv_ref[...] = jnp.zeros_like(dwv_ref)
            dbg_ref[...] = jnp.zeros_like(dbg_ref)
            dbv_ref[...] = jnp.zeros_like(dbv_ref)

        dact = _dot_nt(d_ref[...].astype(BF16), w_ref[...])
        ag = ag_ref[...]
        av = av_ref[...]
        cg, ag1, ag2 = _conv3(ag, carry[0], cwg_ref[...], cbg_ref[...], tm)
        cv, av1, av2 = _conv3(av, carry[1], cwv_ref[...], cbv_ref[...], tm)
        carry[0] = ag[tm - HALO:]
        carry[1] = av[tm - HALO:]
        sg = _sigmoid(cg)
        dcg = dact * cv * (sg * (1.0 + cg * (1.0 - sg)))
        dcv = dact * (cg * sg)
        dcg_ref[...] = dcg
        dcv_ref[...] = dcv
        rsum = lambda x: jnp.sum(x, axis=0, keepdims=True)
        dwg_ref[...] += jnp.concatenate([rsum(dcg * ag2), rsum(dcg * ag1), rsum(dcg * ag)], axis=0)
        dwv_ref[...] += jnp.concatenate([rsum(dcv * av2), rsum(dcv * av1), rsum(dcv * av)], axis=0)
        dbg_ref[...] += rsum(dcg)
        dbv_ref[...] += rsum(dcv)

    aspec = pl.BlockSpec((None, tm, n), lambda j, i: (j, i, 0))
    cwspec = lambda off: pl.BlockSpec((None, 3, n), lambda j, i: (j + off, 0, 0))
    cbspec = lambda off: pl.BlockSpec((None, 1, n), lambda j, i: (j + off, 0, 0))
    cwo = pl.BlockSpec((None, 3, n), lambda j, i: (j, 0, 0))
    cbo = pl.BlockSpec((None, 1, n), lambda j, i: (j, 0, 0))
    outs = pl.pallas_call(
        body, grid=(nh, t // tm), name=name,
        in_specs=[pl.BlockSpec((tm, D), lambda j, i: (i, 0)), pl.BlockSpec((n, D), lambda j, i: (j, 0)), aspec, aspec,
                  cwspec(0), cwspec(nh), cbspec(0), cbspec(nh)],
        out_specs=[aspec, aspec, cwo, cwo, cbo, cbo],
        out_shape=[S((nh, t, n), F32), S((nh, t, n), F32), S((nh, 3, n), F32), S((nh, 3, n), F32),
                   S((nh, 1, n), F32), S((nh, 1, n), F32)],
        scratch_shapes=[pltpu.VMEM((2, HALO, n), F32)],
        compiler_params=_cp("parallel", "arbitrary"))(dh, w, a_g, a_v, cw, cw, cb, cb)
    return outs


def _conv_bwd(dc, cw, off, into, name):
    ns, t, n = dc.shape
    tm = _tm(t)
    ni = t // tm

    def body(dc_ref, cw_ref, *rest):
        o_ref, carry = rest[-2:]
        i = pl.program_id(1)

        @pl.when(i == 0)
        def _():
            carry[...] = jnp.zeros_like(carry)

        x = dc_ref[...]
        ext = jnp.concatenate([x, carry[...]], axis=0)
        cwv = cw_ref[...]
        o_ref[...] = (cwv[2:3] * x + cwv[1:2] * ext[1:1 + tm] + cwv[0:1] * ext[2:2 + tm]).astype(BF16)
        carry[...] = x[:HALO]

    in_specs = [pl.BlockSpec((None, tm, n), lambda j, i: (j, ni - 1 - i, 0)), pl.BlockSpec((None, 3, n), lambda j, i: (j + off, 0, 0))]
    args = [dc, cw]
    if into is not None:
        in_specs.append(pl.BlockSpec(memory_space=pl.ANY))
        args.append(into)
    return pl.pallas_call(
        body, grid=(ns, ni), name=name, in_specs=in_specs,
        out_specs=pl.BlockSpec((None, tm, n), lambda j, i: (j + off, ni - 1 - i, 0)), out_shape=S((2 * ns, t, n), BF16),
        input_output_aliases={} if into is None else {2: 0},
        scratch_shapes=[pltpu.VMEM((HALO, n), F32)],
        compiler_params=_cp("parallel", "arbitrary"))(*args)


def _dw_slot(hn, dy_s, name):
    t, k = hn.shape
    ns, _, n = dy_s.shape
    tm = _tm(t)
    ni = t // tm

    def body(a_ref, b_ref, o_ref, ob_ref, at_ref):
        j = pl.program_id(0)
        i = pl.program_id(1)

        @pl.when(j == 0)
        def _():
            at_ref[i] = a_ref[...].T

        contrib = _dot(at_ref[i], b_ref[...])

        @pl.when(i == 0)
        def _():
            o_ref[...] = contrib

        @pl.when(i > 0)
        def _():
            o_ref[...] += contrib

        @pl.when(i == ni - 1)
        def _():
            ob_ref[...] = o_ref[...].astype(BF16)

    ospec = pl.BlockSpec((None, k, n), lambda j, i: (j, 0, 0))
    return pl.pallas_call(
        body, grid=(ns, ni), name=name,
        in_specs=[pl.BlockSpec((tm, k), lambda j, i: (jnp.where(j == 0, i, ni - 1), 0)),
                  pl.BlockSpec((None, tm, n), lambda j, i: (j, i, 0))],
        out_specs=[ospec, ospec], out_shape=[S((ns, k, n), F32), S((ns, k, n), BF16)],
        scratch_shapes=[pltpu.VMEM((ni, k, tm), BF16)],
        compiler_params=_cp("arbitrary", "arbitrary"))(hn, dy_s)


def _dw_rows(a_s, dh, name):
    nk, t, kc = a_s.shape
    tm = _tm(t)
    ni = t // tm

    def body(a_ref, d_ref, o_ref, ob_ref):
        i = pl.program_id(1)
        contrib = _dot_tn(a_ref[...], d_ref[...].astype(BF16))

        @pl.when(i == 0)
        def _():
            o_ref[...] = contrib

        @pl.when(i > 0)
        def _():
            o_ref[...] += contrib

        @pl.when(i == ni - 1)
        def _():
            ob_ref[...] = o_ref[...].astype(BF16)

    ospec = pl.BlockSpec((kc, D), lambda j, i: (j, 0))
    return pl.pallas_call(
        body, grid=(nk, ni), name=name,
        in_specs=[pl.BlockSpec((None, tm, kc), lambda j, i: (j, i, 0)), pl.BlockSpec((tm, D), lambda j, i: (i, 0))],
        out_specs=[ospec, ospec], out_shape=[S((nk * kc, D), F32), S((nk * kc, D), BF16)],
        compiler_params=_cp("parallel", "arbitrary"))(a_s, dh)


def _dx_slot_normbwd(dy_s, wg, h, gain, dh_in, name):
    ns, t, n = dy_s.shape
    tm = _tm(t)

    def body(dy_ref, w_ref, h_ref, g_ref, di_ref, o_ref, dg_ref, acc):
        i = pl.program_id(0)
        j = pl.program_id(1)

        @pl.when((i == 0) & (j == 0))
        def _():
            dg_ref[...] = jnp.zeros_like(dg_ref)

        contrib = _dot_nt(dy_ref[...], w_ref[...])

        @pl.when(j == 0)
        def _():
            acc[...] = contrib

        @pl.when(j > 0)
        def _():
            acc[...] += contrib

        @pl.when(j == ns - 1)
        def _():
            g = acc[...]
            hv = h_ref[...]
            r = _rstd(hv)
            gg = g * g_ref[...]
            o_ref[...] = di_ref[...] + r * gg - hv * (r * r * r * jnp.mean(gg * hv, axis=-1, keepdims=True))
            dg_ref[...] += jnp.sum(g * hv * r, axis=0, keepdims=True)

    row = pl.BlockSpec((tm, D), lambda i, j: (i, 0))
    vec = pl.BlockSpec((1, D), lambda i, j: (0, 0))
    return pl.pallas_call(
        body, grid=(t // tm, ns), name=name,
        in_specs=[pl.BlockSpec((None, tm, n), lambda i, j: (j, i, 0)), pl.BlockSpec((None, D, n), lambda i, j: (j, 0, 0)),
                  row, vec, row],
        out_specs=[row, vec], out_shape=[S((t, D), F32), S((1, D), F32)],
        scratch_shapes=[pltpu.VMEM((tm, D), F32)],
        compiler_params=_cp("arbitrary", "arbitrary"))(dy_s, wg, h, gain, dh_in)


def _sgu_gate_bwd(a_s, dg_s, vgain, ws, bst, name):
    t = a_s.shape[1]
    sw = a_s.shape[2]
    gps = sw // CHUNK

    def body(a_ref, dg_ref, vg_ref, ws_ref, b_ref, da_ref, dws_ref, dbt_ref, dvg_ref, dvn_ref):
        n = pl.program_id(0)

        @pl.when(n == 0)
        def _():
            dws_ref[...] = jnp.zeros_like(dws_ref)
            dbt_ref[...] = jnp.zeros_like(dbt_ref)
            dvg_ref[...] = jnp.zeros_like(dvg_ref)

        vpre = jnp.concatenate([a_ref[4 + s] for s in range(4)], axis=1)
        v = _gelu(vpre)
        r = _rstd(v)
        vhat = v * r
        vn = (vhat * vg_ref[...]).astype(BF16)
        tri = _tril_mask()
        lane = lax.broadcasted_iota(jnp.int32, (CHUNK, CHUNK), 1)
        dbt = jnp.zeros((CHUNK, CHUNK), F32)
        for g in range(SGU_G):
            w = jnp.where(tri, ws_ref[g], 0.0).astype(BF16)
            vng = vn[:, g * CHUNK:(g + 1) * CHUNK]
            sg = _dot(w, vng) + b_ref[:, g:g + 1]
            lo = (g % gps) * CHUNK
            upre = a_ref[g // gps, :, lo:lo + CHUNK]
            dgate = dg_ref[g // gps, :, lo:lo + CHUNK]
            da_ref[g // gps, :, lo:lo + CHUNK] = (dgate * sg * _gelu_grad(upre)).astype(BF16)
            ds = dgate * _gelu(upre)
            dsb = ds.astype(BF16)
            dvn_ref[:, g * CHUNK:(g + 1) * CHUNK] = _dot_tn(w, dsb)
            dws_ref[g] += jnp.where(tri, _dot_nt(dsb, vng), 0.0)
            dbt = dbt + jnp.where(lane == g, jnp.sum(ds, axis=-1, keepdims=True), 0.0)
        dbt_ref[...] += dbt
        dvn = dvn_ref[...]
        dvg_ref[...] += jnp.sum(dvn * vhat, axis=0, keepdims=True)
        gg = dvn * vg_ref[...]
        dv = r * gg - v * (r * r * r * jnp.mean(gg * v, axis=-1, keepdims=True))
        dav = (dv * _gelu_grad(vpre)).astype(BF16)
        for s in range(4):
            da_ref[4 + s] = dav[:, s * sw:(s + 1) * sw]

    return pl.pallas_call(
        body, grid=(t // CHUNK,), name=name,
        in_specs=[pl.BlockSpec((8, CHUNK, sw), lambda n: (0, n, 0)), pl.BlockSpec((4, CHUNK, sw), lambda n: (0, n, 0)),
                  pl.BlockSpec((1, SGU_W), lambda n: (0, 0)), pl.BlockSpec((SGU_G, CHUNK, CHUNK), lambda n: (0, 0, 0)),
                  pl.BlockSpec((CHUNK, SGU_G), lambda n: (0, 0))],
        out_specs=[pl.BlockSpec((8, CHUNK, sw), lambda n: (0, n, 0)), pl.BlockSpec((SGU_G, CHUNK, CHUNK), lambda n: (0, 0, 0)),
                   pl.BlockSpec((CHUNK, CHUNK), lambda n: (0, 0)), pl.BlockSpec((1, SGU_W), lambda n: (0, 0))],
        out_shape=[S((8, t, sw), BF16), S((SGU_G, CHUNK, CHUNK), F32), S((CHUNK, CHUNK), F32), S((1, SGU_W), F32)],
        scratch_shapes=[pltpu.VMEM((CHUNK, SGU_W), F32)],
        compiler_params=_cp("arbitrary"))(a_s, dg_s, vgain, ws, bst)


def _attn_bwd(qkv_s, do, qg, kg, sinks, bias, name):
    t = qkv_s.shape[1]
    nb = t // CHUNK

    def body(cur_ref, prev_ref, do_ref, qg_ref, kg_ref, sink_ref, bias_ref,
             o_ref, dqg_ref, dkg_ref, dsk_ref, dbias_ref, carry, top):
        n = pl.program_id(0)

        @pl.when(n == 0)
        def _():
            carry[...] = jnp.zeros_like(carry)
            dqg_ref[...] = jnp.zeros_like(dqg_ref)
            dkg_ref[...] = jnp.zeros_like(dkg_ref)
            dsk_ref[...] = jnp.zeros_like(dsk_ref)
            dbias_ref[...] = jnp.zeros_like(dbias_ref)

        @pl.when(n < nb)
        def _():
            valid = _attn_valid(n)
            top[...] = jnp.zeros_like(top)
            new = [[None] * 3 for _ in range(8)]
            lane = lax.broadcasted_iota(jnp.int32, (1, CHUNK), 1)
            dsk = jnp.zeros((1, CHUNK), F32)
            dqg = jnp.zeros((1, HD), F32)
            dkg = jnp.zeros((1, HD), F32)

            def place(col, val):
                new[col // QKV_SLOT][(col % QKV_SLOT) // HD] = val

            for h in range(NKV):
                kcol = D + HD * h
                vcol = D + HD * (NKV + h)
                k = jnp.concatenate([_head(prev_ref, kcol), _head(cur_ref, kcol)], axis=0)
                v = jnp.concatenate([_head(prev_ref, vcol), _head(cur_ref, vcol)], axis=0)
                rk = _rstd(k)
                khat = k * rk
                kn = (khat * kg_ref[...]).astype(BF16)
                vb = v.astype(BF16)
                dkn = jnp.zeros((2 * CHUNK, HD), F32)
                dv = jnp.zeros((2 * CHUNK, HD), F32)
                for g in range(KVG):
                    hq = KVG * h + g
                    q = _head(cur_ref, HD * hq)
                    rq = _rstd(q)
                    qhat = q * rq
                    qn = (qhat * qg_ref[...]).astype(BF16)
                    sink = sink_ref[hq]
                    p, psink = _attn_probs(qn, kn, bias_ref[hq], valid, sink)
                    doh = do_ref[0, :, HD * hq:HD * (hq + 1)]
                    dp = _dot_nt(doh, vb)
                    dsum = jnp.sum(p * dp, axis=-1, keepdims=True)
                    ds = p * (dp - dsum)
                    dsk = dsk + jnp.where(lane == hq, jnp.sum(-psink * dsum), 0.0)
                    dbias_ref[hq] += ds
                    dv = dv + _dot_tn(p.astype(BF16), doh)
                    dsc = (ds * (HD ** -0.5)).astype(BF16)
                    dqn = _dot(dsc, kn)
                    dkn = dkn + _dot_tn(dsc, qn)
                    dqg = dqg + jnp.sum(dqn * qhat, axis=0, keepdims=True)
                    gq = dqn * qg_ref[...]
                    place(HD * hq, rq * gq - q * (rq * rq * rq * jnp.mean(gq * q, axis=-1, keepdims=True)))
                dkg = dkg + jnp.sum(dkn * khat, axis=0, keepdims=True)
                gk = dkn * kg_ref[...]
                dk = rk * gk - k * (rk * rk * rk * jnp.mean(gk * k, axis=-1, keepdims=True))
                place(kcol, dk[CHUNK:])
                place(vcol, dv[CHUNK:])
                top[kcol // QKV_SLOT, :, kcol % QKV_SLOT:kcol % QKV_SLOT + HD] = dk[:CHUNK]
                top[vcol // QKV_SLOT, :, vcol % QKV_SLOT:vcol % QKV_SLOT + HD] = dv[:CHUNK]
            dqg_ref[...] += dqg
            dkg_ref[...] += dkg
            dsk_ref[...] += dsk
            o_ref[...] = (carry[...] + top[...]).astype(BF16)
            for s in range(8):
                carry[s] = jnp.concatenate(new[s], axis=1)

        @pl.when(n == nb)
        def _():
            o_ref[...] = carry[...].astype(BF16)

    blk = lambda f: pl.BlockSpec((8, CHUNK, QKV_SLOT), f)
    cur = lambda n: (0, jnp.minimum(n, nb - 1), 0)
    prev = lambda n: (0, jnp.clip(n - 1, 0, nb - 1), 0)
    small = lambda w: pl.BlockSpec((1, w), lambda n: (0, 0))
    return pl.pallas_call(
        body, grid=(nb + 1,), name=name,
        in_specs=[blk(cur), blk(prev), pl.BlockSpec((1, CHUNK, D), cur), small(HD), small(HD),
                  pl.BlockSpec(memory_space=pltpu.SMEM), pl.BlockSpec((NH, CHUNK, 2 * CHUNK), lambda n: (0, 0, 0))],
        out_specs=[blk(lambda n: (0, jnp.maximum(n - 1, 0), 0)), small(HD), small(HD), small(CHUNK),
                   pl.BlockSpec((NH, CHUNK, 2 * CHUNK), lambda n: (0, 0, 0))],
        out_shape=[S((8, t, QKV_SLOT), BF16), S((1, HD), F32), S((1, HD), F32), S((1, CHUNK), F32),
                   S((NH, CHUNK, 2 * CHUNK), F32)],
        scratch_shapes=[pltpu.VMEM((8, CHUNK, QKV_SLOT), F32), pltpu.VMEM((8, CHUNK, QKV_SLOT), F32)],
        compiler_params=_cp("arbitrary"))(qkv_s, qkv_s, do, qg, kg, sinks, bias)


def _local_step(x, target, rep, wg):
    t = x.shape[0]
    bucket_row = jnp.asarray(_rel_tables().reshape(1, -1))
    bias = _relbias_fwd(rep["rel_bias"].T, bucket_row, "relbias_fwd").reshape(NH, CHUNK, 2 * CHUNK)
    bst = rep["sgu_b_s"][0].T
    ws = rep["sgu_w_s"][0]
    vgain = rep["sgu_v_gain"]
    qg, kg, sinks = rep["attn_q_gain"], rep["attn_k_gain"], rep["attn_sinks"][0]
    w_in, w_out = wg["sgu_w_in"], wg["sgu_w_out"].reshape(SGU_W, D)
    w_qkv, w_o = wg["attn_w_qkv"], wg["attn_w_o"].reshape(D, D)
    w_up = [wg["ffn_w_up%d" % l] for l in range(2)]
    w_down = [wg["ffn_w_down%d" % l].reshape(D_FF, D) for l in range(2)]
    cw = [wg["ffn_conv_w"][:, 3 * l:3 * l + 3] for l in range(2)]
    cb = [rep["ffn_conv_b"][l].reshape(8, 1, -1) for l in range(2)]
    mixg = [rep["mix_norm"][l:l + 1] for l in range(2)]
    ffng = [rep["ffn_norm"][l:l + 1] for l in range(2)]

    hn0 = _rmsnorm(x, mixg[0], "norm0")
    a0 = _mm_slot(hn0, w_in, "sgu_in")
    gated = _sgu_gate_fwd(a0, vgain, ws, bst, "sgu_gate")
    h1, hn1 = _resid_mm(gated, w_out, x, ffng[0], "norm", "sgu_out")
    ag0, av0, act0 = _ffn_up(hn1, w_up[0], cw[0], cb[0], "ffn0_up")
    h2, hn2 = _resid_mm(act0, w_down[0], h1, mixg[1], "norm", "ffn0_down")
    qkv = _mm_slot(hn2, w_qkv, "qkv")
    o = _attn_fwd(qkv, qg, kg, sinks, bias, "attn")
    h3, hn3 = _resid_mm(o, w_o, h2, ffng[1], "norm", "attn_out")
    ag1, av1, act1 = _ffn_up(hn3, w_up[1], cw[1], cb[1], "ffn1_up")
    dy, sq = _resid_mm(act1, w_down[1], h3, target, "loss", "ffn1_down_loss")
    loss = (0.5 / D) * jnp.sum(sq[:, 0, 0])

    def ffn_bwd(dh, h_in, hn, a_g, a_v, act, l, tag):
        dcg, dcv, dcwg, dcwv, dcbg, dcbv = _ffn_down_bwd(dh, w_down[l], a_g, a_v, cw[l], cb[l], tag + "_down_bwd")
        g_down = _dw_rows(act, dh, tag + "_dw_down")
        da = _conv_bwd(dcv, cw[l], 4, _conv_bwd(dcg, cw[l], 0, None, tag + "_conv_bwd_g"), tag + "_conv_bwd_v")
        g_up = _dw_slot(hn, da, tag + "_dw_up")
        dh_new, dgain = _dx_slot_normbwd(da, w_up[l], h_in, ffng[l], dh, tag + "_dx_up")
        g_cw = jnp.concatenate([dcwg, dcwv], axis=0)
        g_cb = jnp.concatenate([dcbg, dcbv], axis=0).reshape(-1)
        return dh_new, dgain, g_up, g_down, g_cw, g_cb

    dh, d_ffng1, g_up1, g_down1, g_cw1, g_cb1 = ffn_bwd(dy, h3, hn3, ag1, av1, act1, 1, "ffn1")
    do = _dx_rows(dh, w_o, D, BF16, "attn_do")
    g_wo = _dw_rows(o, dh, "dw_o")
    dqkv, d_qg, d_kg, d_sk, d_bias = _attn_bwd(qkv, do, qg, kg, sinks, bias, "attn_bwd")
    g_qkv = _dw_slot(hn2, dqkv, "dw_qkv")
    dh, d_mixg1 = _dx_slot_normbwd(dqkv, w_qkv, h2, mixg[1], dh, "dx_qkv")
    d_relb = _relbias_bwd(d_bias.reshape(NH, -1), bucket_row, "relbias_bwd").T
    dh, d_ffng0, g_up0, g_down0, g_cw0, g_cb0 = ffn_bwd(dh, h1, hn1, ag0, av0, act0, 0, "ffn0")
    dgated = _dx_rows(dh, w_out, SGU_W // 4, F32, "sgu_dgated")
    g_wout = _dw_rows(gated, dh, "dw_sgu_out")
    da0, d_ws, d_bst, d_vgain = _sgu_gate_bwd(a0, dgated, vgain, ws, bst, "sgu_gate_bwd")
    g_win = _dw_slot(hn0, da0, "dw_sgu_in")
    grad_x, d_mixg0 = _dx_slot_normbwd(da0, w_in, x, mixg[0], dh, "dx_sgu_in")

    g_rep = {
        "mix_norm": jnp.concatenate([d_mixg0, d_mixg1], axis=0),
        "ffn_norm": jnp.concatenate([d_ffng0, d_ffng1], axis=0),
        "sgu_v_gain": d_vgain,
        "sgu_w_s": d_ws[None],
        "sgu_b_s": d_bst[:, :SGU_G].T[None],
        "attn_q_gain": d_qg,
        "attn_k_gain": d_kg,
        "attn_sinks": d_sk[:, :NH],
        "rel_bias": d_relb,
        "ffn_conv_b": jnp.stack([g_cb0, g_cb1], axis=0),
    }
    rows = lambda pair: tuple(g.reshape(N_DEV, -1, D) for g in pair)
    g_cw = jnp.concatenate([g_cw0, g_cw1], axis=1)
    g_sh = {
        "sgu_w_in": g_win, "sgu_w_out": rows(g_wout), "attn_w_qkv": g_qkv, "attn_w_o": rows(g_wo),
        "ffn_w_up0": g_up0, "ffn_w_up1": g_up1, "ffn_w_down0": rows(g_down0), "ffn_w_down1": rows(g_down1),
        "ffn_conv_w": (g_cw, g_cw.astype(BF16)),
    }
    return loss, grad_x, g_rep, g_sh


ANY = pl.BlockSpec(memory_space=pl.ANY)


def _place():
    x, y, c = lax.axis_index("x"), lax.axis_index("y"), lax.axis_index("c")
    return x, y, c, [(1 - x, y), (x, 1 - y), (1 - x, 1 - y)]


def _allgather(xs, name):
    nt = len(xs)

    def body(*refs):
        x_refs, o_refs = refs[:nt], refs[nt:2 * nt]
        send_sems, recv_sems, local_sems = refs[2 * nt:]
        x, y, c, chips = _place()
        me, sibling = (x, y, c), (x, y, 1 - c)

        def copy(t, k, block, to, src=None):
            px, py, pc = block
            dst = o_refs[t].at[4 * px + 2 * py + pc]
            return pltpu.make_async_remote_copy(
                src_ref=dst if src is None else src, dst_ref=dst, send_sem=send_sems.at[t, k], recv_sem=recv_sems.at[t, k],
                device_id=to, device_id_type=MESH)

        mine = [pltpu.make_async_copy(x_refs[t], o_refs[t].at[4 * x + 2 * y + c], local_sems.at[t]) for t in range(nt)]
        for cp in mine:
            cp.start()
        first = []
        for t in range(nt):
            first.append(copy(t, 0, me, sibling, src=x_refs[t]))
            first += [copy(t, 1 + j, me, (*chip, c), src=x_refs[t]) for j, chip in enumerate(chips)]
        for cp in first:
            cp.start()
        passed = []
        for j, chip in enumerate(chips):
            for t in range(nt):
                copy(t, 1 + j, (*chip, c), me).wait_recv()
                fwd = copy(t, 4 + j, (*chip, c), sibling)
                fwd.start()
                passed.append(fwd)
        for t in range(nt):
            copy(t, 0, sibling, me).wait_recv()
            for j, chip in enumerate(chips):
                copy(t, 4 + j, (*chip, 1 - c), me).wait_recv()
        for cp in first + passed:
            cp.wait_send()
        for cp in mine:
            cp.wait()

    return pl.pallas_call(
        body, name=name, in_specs=[ANY] * nt, out_specs=[ANY] * nt,
        out_shape=[S((N_DEV,) + a.shape, a.dtype) for a in xs],
        scratch_shapes=[pltpu.SemaphoreType.DMA((nt, 7)), pltpu.SemaphoreType.DMA((nt, 7)), pltpu.SemaphoreType.DMA((nt,))],
        compiler_params=pltpu.CompilerParams(has_side_effects=True))(*xs)


def _rs_sibling(gs, name):
    nt = len(gs)

    def body(*refs):
        g_refs, o_refs = refs[:nt], refs[nt:2 * nt]
        send_sems, recv_sems = refs[2 * nt:]
        x, y, c, _ = _place()
        copies = []
        for t in range(nt):
            for k in range(4):
                copies.append(pltpu.make_async_remote_copy(
                    src_ref=g_refs[t].at[2 * k + (1 - c)], dst_ref=o_refs[t].at[k],
                    send_sem=send_sems.at[t, k], recv_sem=recv_sems.at[t, k], device_id=(x, y, 1 - c), device_id_type=MESH))
        for cp in copies:
            cp.start()
        for cp in copies:
            cp.wait_recv()
        for cp in copies:
            cp.wait_send()

    return pl.pallas_call(
        body, name=name, in_specs=[ANY] * nt, out_specs=[ANY] * nt,
        out_shape=[S((4,) + g.shape[1:], g.dtype) for g in gs],
        scratch_shapes=[pltpu.SemaphoreType.DMA((nt, 4)), pltpu.SemaphoreType.DMA((nt, 4))],
        compiler_params=pltpu.CompilerParams(has_side_effects=True))(*gs)


def _rs_chips(ps, name):
    nt = len(ps)

    def body(*refs):
        p_refs, o_refs = refs[:nt], refs[nt:2 * nt]
        send_sems, recv_sems = refs[2 * nt:]
        x, y, c, chips = _place()
        copies = []
        for t in range(nt):
            for j, (px, py) in enumerate(chips):
                copies.append(pltpu.make_async_remote_copy(
                    src_ref=p_refs[t].at[2 * px + py], dst_ref=o_refs[t].at[j],
                    send_sem=send_sems.at[t, j], recv_sem=recv_sems.at[t, j], device_id=(px, py, c), device_id_type=MESH))
        for cp in copies:
            cp.start()
        for cp in copies:
            cp.wait_recv()
        for cp in copies:
            cp.wait_send()

    return pl.pallas_call(
        body, name=name, in_specs=[ANY] * nt, out_specs=[ANY] * nt,
        out_shape=[S((3,) + p.shape[1:], p.dtype) for p in ps],
        scratch_shapes=[pltpu.SemaphoreType.DMA((nt, 3)), pltpu.SemaphoreType.DMA((nt, 3))],
        compiler_params=pltpu.CompilerParams(has_side_effects=True))(*ps)


def _row_tile(r):
    tr = r if r <= ROW_TILE or r % ROW_TILE else ROW_TILE
    assert r % tr == 0
    return tr


def _rs_partial(g32, sib, place, name):
    _, r, cdim = g32.shape
    tr = _row_tile(r)

    def body(place_ref, g_ref, s_ref, p_ref, own_ref):
        k = pl.program_id(1)
        tot = g_ref[...] + s_ref[...].astype(F32)
        p_ref[...] = tot.astype(BF16)

        @pl.when(k == place_ref[1])
        def _():
            own_ref[...] = tot

    grid_spec = pltpu.PrefetchScalarGridSpec(
        num_scalar_prefetch=1, grid=(r // tr, 4),
        in_specs=[pl.BlockSpec((None, None, tr, cdim), lambda i, k, pr: (k, pr[0], i, 0)),
                  pl.BlockSpec((None, tr, cdim), lambda i, k, pr: (k, i, 0))],
        out_specs=[pl.BlockSpec((None, tr, cdim), lambda i, k, pr: (k, i, 0)), pl.BlockSpec((tr, cdim), lambda i, k, pr: (i, 0))])
    return pl.pallas_call(
        body, grid_spec=grid_spec, name=name,
        out_shape=[S((4, r, cdim), BF16), S((r, cdim), F32)],
        compiler_params=_cp("parallel", "arbitrary"))(place, g32.reshape(4, 2, r, cdim), sib)


def _adamw_math(w, g, m, v):
    m = ADAM_B1 * m + (1.0 - ADAM_B1) * g
    v = ADAM_B2 * v + (1.0 - ADAM_B2) * (g * g)
    m_hat = m / (1.0 - ADAM_B1 ** ADAM_STEP)
    v_hat = v / (1.0 - ADAM_B2 ** ADAM_STEP)
    delta = -ADAM_LR * (m_hat / (jnp.sqrt(v_hat) + ADAM_EPS) + ADAM_WD * w)
    return delta, m, v


def _adamw_shard(own, recv, w, m, v, name):
    r, cdim = own.shape
    tr = _row_tile(r)

    def body(own_ref, recv_ref, w_ref, m_ref, v_ref, g_out, d_out, m_out, v_out):
        g = own_ref[...] + recv_ref[0].astype(F32) + recv_ref[1].astype(F32) + recv_ref[2].astype(F32)
        g_out[...] = g
        d_out[...], m_out[...], v_out[...] = _adamw_math(w_ref[...], g, m_ref[...], v_ref[...])

    row = pl.BlockSpec((tr, cdim), lambda i: (i, 0))
    return pl.pallas_call(
        body, grid=(r // tr,), name=name,
        in_specs=[row, pl.BlockSpec((3, tr, cdim), lambda i: (0, i, 0)), row, row, row],
        out_specs=[row] * 4, out_shape=[S((r, cdim), F32)] * 4, compiler_params=_cp("parallel"))(own, recv, w, m, v)


def _adamw_replicated(gall, w, m, v, name):
    _, r, cdim = gall.shape
    tr = r // 3 if r % 24 == 0 else r

    def body(g_ref, w_ref, m_ref, v_ref, g_out, d_out, m_out, v_out):
        g = g_ref[0]
        for s in range(1, N_DEV):
            g = g + g_ref[s]
        g_out[...] = g
        d_out[...], m_out[...], v_out[...] = _adamw_math(w_ref[...], g, m_ref[...], v_ref[...])

    row = pl.BlockSpec((tr, cdim), lambda i: (i, 0))
    return pl.pallas_call(
        body, grid=(r // tr,), name=name,
        in_specs=[pl.BlockSpec((N_DEV, tr, cdim), lambda i: (0, i, 0)), row, row, row],
        out_specs=[row] * 4, out_shape=[S((r, cdim), F32)] * 4, compiler_params=_cp("parallel"))(gall, w, m, v)


REPLICATED = ["mix_norm", "ffn_norm", "sgu_v_gain", "sgu_w_s", "sgu_b_s", "attn_q_gain", "attn_k_gain", "attn_sinks", "rel_bias",
              "ffn_conv_b"]
WEIGHTS = ["mix_norm", "ffn_norm", "sgu_w_in", "sgu_v_gain", "sgu_w_s", "sgu_b_s", "sgu_w_out", "attn_w_qkv", "attn_q_gain",
           "attn_k_gain", "attn_sinks", "attn_w_o", "rel_bias", "ffn_w_up", "ffn_conv_w", "ffn_conv_b", "ffn_w_down"]
PACK_UNIT = 1024


def _shard_views(w):
    return {
        "sgu_w_in": w["sgu_w_in"][0], "sgu_w_out": w["sgu_w_out"][0], "attn_w_qkv": w["attn_w_qkv"][0], "attn_w_o": w["attn_w_o"][0],
        "ffn_w_up0": w["ffn_w_up"][0], "ffn_w_up1": w["ffn_w_up"][1], "ffn_w_down0": w["ffn_w_down"][0], "ffn_w_down1": w["ffn_w_down"][1],
        "ffn_conv_w": w["ffn_conv_w"].reshape(6, -1),
    }


def _unshard_views(d):
    return {
        "sgu_w_in": d["sgu_w_in"][None], "sgu_w_out": d["sgu_w_out"][None], "attn_w_qkv": d["attn_w_qkv"][None], "attn_w_o": d["attn_w_o"][None],
        "ffn_w_up": jnp.stack([d["ffn_w_up0"], d["ffn_w_up1"]]), "ffn_w_down": jnp.stack([d["ffn_w_down0"], d["ffn_w_down1"]]),
        "ffn_conv_w": d["ffn_conv_w"].reshape(2, 3, -1),
    }


def _pack(d):
    parts = []
    for n in REPLICATED:
        flat = d[n].reshape(-1)
        parts.append(jnp.pad(flat, (0, -flat.shape[0] % PACK_UNIT)))
    return jnp.concatenate(parts).reshape(-1, 128)


def _unpack(p, like):
    flat = p.reshape(-1)
    out, off = {}, 0
    for n in REPLICATED:
        size = math.prod(like[n].shape)
        out[n] = flat[off:off + size].reshape(like[n].shape)
        off += size + (-size % PACK_UNIT)
    return out


def kernel(x, mix_norm, ffn_norm, sgu_w_in, sgu_v_gain, sgu_w_s, sgu_b_s, sgu_w_out, attn_w_qkv, attn_q_gain, attn_k_gain, attn_sinks, attn_w_o, rel_bias, ffn_w_up, ffn_conv_w, ffn_conv_b, ffn_w_down, loss_target, m_mix_norm, m_ffn_norm, m_sgu_w_in, m_sgu_v_gain, m_sgu_w_s, m_sgu_b_s, m_sgu_w_out, m_attn_w_qkv, m_attn_q_gain, m_attn_k_gain, m_attn_sinks, m_attn_w_o, m_rel_bias, m_ffn_w_up, m_ffn_conv_w, m_ffn_conv_b, m_ffn_w_down, v_mix_norm, v_ffn_norm, v_sgu_w_in, v_sgu_v_gain, v_sgu_w_s, v_sgu_b_s, v_sgu_w_out, v_attn_w_qkv, v_attn_q_gain, v_attn_k_gain, v_attn_sinks, v_attn_w_o, v_rel_bias, v_ffn_w_up, v_ffn_conv_w, v_ffn_conv_b, v_ffn_w_down):
    w = dict(zip(WEIGHTS, (mix_norm, ffn_norm, sgu_w_in, sgu_v_gain, sgu_w_s, sgu_b_s, sgu_w_out, attn_w_qkv, attn_q_gain, attn_k_gain,
                           attn_sinks, attn_w_o, rel_bias, ffn_w_up, ffn_conv_w, ffn_conv_b, ffn_w_down)))
    m = dict(zip(WEIGHTS, (m_mix_norm, m_ffn_norm, m_sgu_w_in, m_sgu_v_gain, m_sgu_w_s, m_sgu_b_s, m_sgu_w_out, m_attn_w_qkv, m_attn_q_gain,
                           m_attn_k_gain, m_attn_sinks, m_attn_w_o, m_rel_bias, m_ffn_w_up, m_ffn_conv_w, m_ffn_conv_b, m_ffn_w_down)))
    v = dict(zip(WEIGHTS, (v_mix_norm, v_ffn_norm, v_sgu_w_in, v_sgu_v_gain, v_sgu_w_s, v_sgu_b_s, v_sgu_w_out, v_attn_w_qkv, v_attn_q_gain,
                           v_attn_k_gain, v_attn_sinks, v_attn_w_o, v_rel_bias, v_ffn_w_up, v_ffn_conv_w, v_ffn_conv_b, v_ffn_w_down)))
    rep = {n: w[n] for n in REPLICATED}

    ws, ms, vs = _shard_views(w), _shard_views(m), _shard_views(v)
    names = list(ws)
    send = [ws[n] if n == "ffn_conv_w" else ws[n].astype(BF16) for n in names]
    wg = dict(zip(names, _allgather(send, "gather_weights")))

    loss, grad_x, g_rep, g_sh = _local_step(x[0], loss_target[0], rep, wg)
    loss = lax.psum(loss, ("x", "y", "c"))

    xi, yi, ci = lax.axis_index("x"), lax.axis_index("y"), lax.axis_index("c")
    place = jnp.stack([ci, 2 * xi + yi]).astype(jnp.int32)
    sib = _rs_sibling([g_sh[n][1] for n in names], "rs_sibling")
    parts = [_rs_partial(g_sh[n][0], s, place, "rs_partial_" + n) for n, s in zip(names, sib)]
    recv = _rs_chips([p[0] for p in parts], "rs_chips")
    res = [_adamw_shard(p[1], r, ws[n], ms[n], vs[n], "adamw_" + n) for n, p, r in zip(names, parts, recv)]
    out = [_unshard_views({n: r[i] for n, r in zip(names, res)}) for i in range(4)]

    gall = _allgather([_pack(g_rep)], "gather_small_grads")[0]
    small = _adamw_replicated(gall, _pack(rep), _pack({n: m[n] for n in REPLICATED}), _pack({n: v[n] for n in REPLICATED}), "adamw_small")
    for i in range(4):
        out[i].update(_unpack(small[i], rep))

    return (loss, grad_x[None], *[out[0][n] for n in WEIGHTS], *[out[1][n] for n in WEIGHTS],
            *[out[2][n] for n in WEIGHTS], *[out[3][n] for n in WEIGHTS])
```

```python
import functools
import math

import numpy as np
import jax
import jax.numpy as jnp
from jax import lax
from jax.experimental import pallas as pl
from jax.experimental.pallas import tpu as pltpu

F32 = jnp.float32
BF16 = jnp.bfloat16
S = jax.ShapeDtypeStruct

D = 1024
CHUNK = 128
SGU_W = 2048
SGU_G = 16
HD = 64
NH = 16
NKV = 4
KVG = 4
D_FF = 2816
REL_BUCKETS = 32
REL_MAX_DIST = 128
EPS = 1e-6
N_DEV = 8
MESH = pl.DeviceIdType.MESH

ADAM_LR = 0.001
ADAM_B1 = 0.9
ADAM_B2 = 0.999
ADAM_EPS = 1e-08
ADAM_WD = 0.01
ADAM_STEP = 10

ROW_TILE = 512
HALO = 8


def _tm(t):
    return min(ROW_TILE, t)


def _cp(*sem):
    return pltpu.CompilerParams(dimension_semantics=sem)


def _dot(a, b):
    return jnp.dot(a, b, preferred_element_type=F32)


def _dot_nt(a, b):
    return lax.dot_general(a, b, (((1,), (1,)), ((), ())), preferred_element_type=F32)


def _dot_tn(a, b):
    return lax.dot_general(a, b, (((0,), (0,)), ((), ())), preferred_element_type=F32)


def _gelu(x):
    return 0.5 * x * (1.0 + lax.erf(x * (2.0 ** -0.5)))


def _gelu_grad(x):
    return 0.5 * (1.0 + lax.erf(x * (2.0 ** -0.5))) + x * jnp.exp(-0.5 * x * x) * (1.0 / math.sqrt(2.0 * math.pi))


def _sigmoid(x):
    return 1.0 / (1.0 + jnp.exp(-x))


def _rstd(x):
    return lax.rsqrt(jnp.mean(x * x, axis=-1, keepdims=True) + EPS)


def _rel_tables():
    q = np.arange(CHUNK)[:, None] + CHUNK
    k = np.arange(2 * CHUNK)[None, :]
    dist = q - k
    n = np.maximum(dist, 0)
    max_exact = REL_BUCKETS // 2
    large = max_exact + (np.log(np.maximum(n, 1).astype(np.float32) / max_exact)
                         / math.log(REL_MAX_DIST / max_exact) * (REL_BUCKETS - max_exact)).astype(np.int32)
    large = np.minimum(large, REL_BUCKETS - 1)
    return np.where(n < max_exact, n, large).astype(np.int32)


def _rmsnorm(x, gain, name):
    t = x.shape[0]
    tm = _tm(t)

    def body(x_ref, g_ref, o_ref):
        xv = x_ref[...]
        o_ref[...] = (xv * _rstd(xv) * g_ref[...]).astype(BF16)

    return pl.pallas_call(
        body, grid=(t // tm,), name=name,
        in_specs=[pl.BlockSpec((tm, D), lambda i: (i, 0)), pl.BlockSpec((1, D), lambda i: (0, 0))],
        out_specs=pl.BlockSpec((tm, D), lambda i: (i, 0)),
        out_shape=S((t, D), BF16), compiler_params=_cp("parallel"))(x, gain)


def _resident(shape):
    zeros = (0,) * len(shape)
    return pl.BlockSpec(shape, lambda *_: zeros, pipeline_mode=pl.Buffered(1))


def _mm_slot(hn, wg, out_dtype, name):
    t, k = hn.shape
    ns, _, n = wg.shape
    tm = _tm(t)

    def body(a_ref, w_ref, o_ref):
        a = a_ref[...]
        for s in range(ns):
            o_ref[s] = _dot(a, w_ref[s]).astype(out_dtype)

    return pl.pallas_call(
        body, grid=(t // tm,), name=name,
        in_specs=[pl.BlockSpec((tm, k), lambda i: (i, 0)), _resident(wg.shape)],
        out_specs=pl.BlockSpec((ns, tm, n), lambda i: (0, i, 0)),
        out_shape=S((ns, t, n), out_dtype), compiler_params=_cp("parallel"))(hn, wg)


def _conv3(a, prev, cw, cb, tm):
    ext = jnp.concatenate([prev, a], axis=0)
    a1 = ext[HALO - 1:HALO - 1 + tm]
    a2 = ext[HALO - 2:HALO - 2 + tm]
    return cw[2:3] * a + cw[1:2] * a1 + cw[0:1] * a2 + cb, a1, a2


def _ffn_fwd(hn, h, wup, wdown, cw, cb, extra, mode, name):
    t, k = hn.shape
    n = wup.shape[-1]
    nh = wup.shape[0] // 2
    tm = _tm(t)
    ni = t // tm

    def body(a_ref, h_ref, wu_ref, wd_ref, cw_ref, cb_ref, e_ref, ag_ref, av_ref, o1_ref, o2_ref, carry):
        i = pl.program_id(0)

        @pl.when(i == 0)
        def _():
            carry[...] = jnp.zeros_like(carry)

        a = a_ref[...]
        acc = h_ref[...]
        for j in range(nh):
            ag = _dot(a, wu_ref[j])
            av = _dot(a, wu_ref[nh + j])
            ag_ref[j] = ag.astype(BF16)
            av_ref[j] = av.astype(BF16)
            cg, _, _ = _conv3(ag, carry[j], cw_ref[j], cb_ref[j], tm)
            cv, _, _ = _conv3(av, carry[nh + j], cw_ref[nh + j], cb_ref[nh + j], tm)
            carry[j] = ag[tm - HALO:]
            carry[nh + j] = av[tm - HALO:]
            act = (cg * _sigmoid(cg) * cv).astype(BF16)
            acc = acc + _dot(act, wd_ref[j * n:(j + 1) * n, :])
        if mode == "norm":
            o1_ref[...] = acc
            o2_ref[...] = (acc * _rstd(acc) * e_ref[...]).astype(BF16)
        else:
            err = acc - e_ref[...]
            o1_ref[...] = err * (1.0 / D)
            o2_ref[...] = jnp.full(o2_ref.shape, jnp.sum(err * err), F32)

    row = pl.BlockSpec((tm, D), lambda i: (i, 0))
    if mode == "norm":
        e_spec, o2_spec, o2_shape = pl.BlockSpec((1, D), lambda i: (0, 0)), row, S((t, D), BF16)
    else:
        e_spec, o2_spec, o2_shape = row, pl.BlockSpec((None, 8, 128), lambda i: (i, 0, 0)), S((ni, 8, 128), F32)
    aspec = pl.BlockSpec((nh, tm, n), lambda i: (0, i, 0))
    return pl.pallas_call(
        body, grid=(ni,), name=name,
        in_specs=[pl.BlockSpec((tm, k), lambda i: (i, 0)), row, _resident(wup.shape), _resident(wdown.shape),
                  _resident(cw.shape), _resident(cb.shape), e_spec],
        out_specs=[aspec, aspec, row, o2_spec],
        out_shape=[S((nh, t, n), BF16), S((nh, t, n), BF16), S((t, D), F32), o2_shape],
        scratch_shapes=[pltpu.VMEM((2 * nh, HALO, n), F32)],
        compiler_params=_cp("arbitrary"))(hn, h, wup, wdown, cw, cb, extra)


def _tril_mask():
    r = lax.broadcasted_iota(jnp.int32, (CHUNK, CHUNK), 0)
    c = lax.broadcasted_iota(jnp.int32, (CHUNK, CHUNK), 1)
    return r >= c


def _sgu_gate_fwd(a_s, vgain, ws, bst, name):
    t = a_s.shape[1]
    sw = a_s.shape[2]
    gps = sw // CHUNK

    def body(a_ref, vg_ref, ws_ref, b_ref, o_ref):
        v = _gelu(jnp.concatenate([a_ref[4 + s].astype(F32) for s in range(4)], axis=1))
        vn = (v * _rstd(v) * vg_ref[...]).astype(BF16)
        tri = _tril_mask()
        for g in range(SGU_G):
            w = jnp.where(tri, ws_ref[g], 0.0).astype(BF16)
            sg = _dot(w, vn[:, g * CHUNK:(g + 1) * CHUNK]) + b_ref[:, g:g + 1]
            lo = (g % gps) * CHUNK
            u = _gelu(a_ref[g // gps, :, lo:lo + CHUNK].astype(F32))
            o_ref[g // gps, :, lo:lo + CHUNK] = (u * sg).astype(BF16)

    return pl.pallas_call(
        body, grid=(t // CHUNK,), name=name,
        in_specs=[pl.BlockSpec((8, CHUNK, sw), lambda n: (0, n, 0)), pl.BlockSpec((1, SGU_W), lambda n: (0, 0)),
                  pl.BlockSpec((SGU_G, CHUNK, CHUNK), lambda n: (0, 0, 0)), pl.BlockSpec((CHUNK, SGU_G), lambda n: (0, 0))],
        out_specs=pl.BlockSpec((4, CHUNK, sw), lambda n: (0, n, 0)),
        out_shape=S((4, t, sw), BF16), compiler_params=_cp("parallel"))(a_s, vgain, ws, bst)


def _resid_mm(a_s, w, resid, extra, mode, name):
    nk, t, kc = a_s.shape
    tm = _tm(t)
    ni = t // tm

    def body(a_ref, w_ref, r_ref, e_ref, o1_ref, o2_ref):
        h = r_ref[...]
        for j in range(nk):
            h = h + _dot(a_ref[j], w_ref[j * kc:(j + 1) * kc, :])
        if mode == "norm":
            o1_ref[...] = h
            o2_ref[...] = (h * _rstd(h) * e_ref[...]).astype(BF16)
        else:
            err = h - e_ref[...]
            o1_ref[...] = err * (1.0 / D)
            o2_ref[...] = jnp.full(o2_ref.shape, jnp.sum(err * err), F32)

    row = pl.BlockSpec((tm, D), lambda i: (i, 0))
    if mode == "norm":
        e_spec, o2_spec, o2_shape = pl.BlockSpec((1, D), lambda i: (0, 0)), row, S((t, D), BF16)
    else:
        e_spec, o2_spec, o2_shape = row, pl.BlockSpec((None, 8, 128), lambda i: (i, 0, 0)), S((ni, 8, 128), F32)
    return pl.pallas_call(
        body, grid=(ni,), name=name,
        in_specs=[pl.BlockSpec((nk, tm, kc), lambda i: (0, i, 0)), _resident(w.shape), row, e_spec],
        out_specs=[row, o2_spec], out_shape=[S((t, D), F32), o2_shape],
        compiler_params=_cp("parallel"))(a_s, w, resid, extra)


def _relbias_fwd(rel_bias_t, bucket_row, name):
    nb = bucket_row.shape[1]

    def body(rb_ref, bk_ref, o_ref):
        onehot = (lax.broadcasted_iota(jnp.int32, (REL_BUCKETS, nb), 0) == bk_ref[...]).astype(F32)
        o_ref[...] = jnp.dot(rb_ref[...], onehot, precision=lax.Precision.HIGHEST, preferred_element_type=F32)

    return pl.pallas_call(body, out_shape=S((NH, nb), F32), name=name)(rel_bias_t, bucket_row)


def _relbias_bwd(dbias, bucket_row, name):
    nb = bucket_row.shape[1]

    def body(db_ref, bk_ref, o_ref):
        onehot = (lax.broadcasted_iota(jnp.int32, (REL_BUCKETS, nb), 0) == bk_ref[...]).astype(F32)
        o_ref[...] = lax.dot_general(db_ref[...], onehot, (((1,), (1,)), ((), ())),
                                     precision=lax.Precision.HIGHEST, preferred_element_type=F32)

    return pl.pallas_call(body, out_shape=S((NH, REL_BUCKETS), F32), name=name)(dbias, bucket_row)


QKV_SLOT = 192


def _head(ref, col):
    return ref[col // QKV_SLOT, :, col % QKV_SLOT:col % QKV_SLOT + HD]


def _attn_valid(n):
    qi = lax.broadcasted_iota(jnp.int32, (CHUNK, 2 * CHUNK), 0)
    kj = lax.broadcasted_iota(jnp.int32, (CHUNK, 2 * CHUNK), 1)
    dist = qi + CHUNK - kj
    return (dist >= 0) & (dist < CHUNK) & ((n > 0) | (kj >= CHUNK))


def _attn_probs(qn, kn, bias, valid, sink):
    s = _dot_nt(qn, kn) * (HD ** -0.5) + bias
    s = jnp.where(valid, s, -jnp.inf)
    m = jnp.maximum(jnp.max(s, axis=-1, keepdims=True), sink)
    p = jnp.exp(s - m)
    psink = jnp.exp(sink - m)
    inv = 1.0 / (jnp.sum(p, axis=-1, keepdims=True) + psink)
    return p * inv, psink * inv


def _attn_fwd(qkv_s, qg, kg, sinks, bias, name):
    t = qkv_s.shape[1]

    def body(cur_ref, prev_ref, qg_ref, kg_ref, sink_ref, bias_ref, o_ref):
        n = pl.program_id(0)
        valid = _attn_valid(n)
        for h in range(NKV):
            k = jnp.concatenate([_head(prev_ref, D + HD * h), _head(cur_ref, D + HD * h)], axis=0)
            v = jnp.concatenate([_head(prev_ref, D + HD * (NKV + h)), _head(cur_ref, D + HD * (NKV + h))], axis=0)
            kn = (k * _rstd(k) * kg_ref[...]).astype(BF16)
            vb = v.astype(BF16)
            outs = []
            for g in range(KVG):
                hq = KVG * h + g
                q = _head(cur_ref, HD * hq)
                qn = (q * _rstd(q) * qg_ref[...]).astype(BF16)
                p, _ = _attn_probs(qn, kn, bias_ref[hq], valid, sink_ref[hq])
                outs.append(_dot(p.astype(BF16), vb))
            o_ref[0, :, KVG * HD * h:KVG * HD * (h + 1)] = jnp.concatenate(outs, axis=1).astype(BF16)

    blk = lambda f: pl.BlockSpec((8, CHUNK, QKV_SLOT), f)
    return pl.pallas_call(
        body, grid=(t // CHUNK,), name=name,
        in_specs=[blk(lambda n: (0, n, 0)), blk(lambda n: (0, jnp.maximum(n - 1, 0), 0)),
                  pl.BlockSpec((1, HD), lambda n: (0, 0)), pl.BlockSpec((1, HD), lambda n: (0, 0)),
                  pl.BlockSpec(memory_space=pltpu.SMEM), pl.BlockSpec((NH, CHUNK, 2 * CHUNK), lambda n: (0, 0, 0))],
        out_specs=pl.BlockSpec((1, CHUNK, D), lambda n: (0, n, 0)),
        out_shape=S((1, t, D), BF16), compiler_params=_cp("parallel"))(qkv_s, qkv_s, qg, kg, sinks, bias)


def _dx_rows(dh, w, kc, out_dtype, name):
    t = dh.shape[0]
    nk = w.shape[0] // kc
    tm = _tm(t)

    def body(d_ref, w_ref, o_ref):
        dhb = d_ref[...].astype(BF16)
        for j in range(nk):
            o_ref[j] = _dot_nt(dhb, w_ref[j * kc:(j + 1) * kc, :]).astype(out_dtype)

    return pl.pallas_call(
        body, grid=(t // tm,), name=name,
        in_specs=[pl.BlockSpec((tm, D), lambda i: (i, 0)), _resident(w.shape)],
        out_specs=pl.BlockSpec((nk, tm, kc), lambda i: (0, i, 0)),
        out_shape=S((nk, t, kc), out_dtype), compiler_params=_cp("parallel"))(dh, w)


BWD1_ROWS = 256


def _ffn_bwd1(dh, a_g, a_v, wdown, cw, cb, name):
    nh, t, n = a_g.shape
    tm = min(BWD1_ROWS, t)
    ni = t // tm

    def body(d_ref, ag_ref, av_ref, wd_ref, cw_ref, cb_ref, dc_ref, dw_hbm, dwb_hbm, dcw_ref, dcb_ref, carry, acc, stage):
        i = pl.program_id(0)

        @pl.when(i == 0)
        def _():
            carry[...] = jnp.zeros_like(carry)
            acc[...] = jnp.zeros_like(acc)
            dcw_ref[...] = jnp.zeros_like(dcw_ref)
            dcb_ref[...] = jnp.zeros_like(dcb_ref)

        dhb = d_ref[...].astype(BF16)
        rsum = lambda x: jnp.sum(x, axis=0, keepdims=True)
        for j in range(nh):
            dact = _dot_nt(dhb, wd_ref[j * n:(j + 1) * n, :])
            ag = ag_ref[j].astype(F32)
            av = av_ref[j].astype(F32)
            cg, ag1, ag2 = _conv3(ag, carry[j], cw_ref[j], cb_ref[j], tm)
            cv, av1, av2 = _conv3(av, carry[nh + j], cw_ref[nh + j], cb_ref[nh + j], tm)
            carry[j] = ag[tm - HALO:]
            carry[nh + j] = av[tm - HALO:]
            sg = _sigmoid(cg)
            gs = cg * sg
            acc[j * n:(j + 1) * n, :] += _dot_tn((gs * cv).astype(BF16), dhb)
            dcg = dact * cv * (sg * (1.0 + cg * (1.0 - sg)))
            dcv = dact * gs
            dc_ref[j] = dcg.astype(BF16)
            dc_ref[nh + j] = dcv.astype(BF16)
            dcw_ref[j] += jnp.concatenate([rsum(dcg * ag2), rsum(dcg * ag1), rsum(dcg * ag)], axis=0)
            dcw_ref[nh + j] += jnp.concatenate([rsum(dcv * av2), rsum(dcv * av1), rsum(dcv * av)], axis=0)
            dcb_ref[j] += rsum(dcg)
            dcb_ref[nh + j] += rsum(dcv)

        @pl.when(i == ni - 1)
        def _():
            pltpu.sync_copy(acc, dw_hbm)
            for j in range(nh):
                stage[...] = acc[j * n:(j + 1) * n, :].astype(BF16)
                pltpu.sync_copy(stage, dwb_hbm.at[pl.ds(j * n, n), :])

    aspec = pl.BlockSpec((nh, tm, n), lambda i: (0, i, 0))
    return pl.pallas_call(
        body, grid=(ni,), name=name,
        in_specs=[pl.BlockSpec((tm, D), lambda i: (i, 0)), aspec, aspec, _resident(wdown.shape), _resident(cw.shape), _resident(cb.shape)],
        out_specs=[pl.BlockSpec((2 * nh, tm, n), lambda i: (0, i, 0)), ANY, ANY,
                   pl.BlockSpec(cw.shape, lambda i: (0, 0, 0)), pl.BlockSpec(cb.shape, lambda i: (0, 0, 0))],
        out_shape=[S((2 * nh, t, n), BF16), S(wdown.shape, F32), S(wdown.shape, BF16), S(cw.shape, F32), S(cb.shape, F32)],
        scratch_shapes=[pltpu.VMEM((2 * nh, HALO, n), F32), pltpu.VMEM(wdown.shape, F32), pltpu.VMEM((n, D), BF16)],
        compiler_params=_cp("arbitrary"))(dh, a_g, a_v, wdown, cw, cb)


def _ffn_bwd2(dc, wup, cw, h, gain, dh_in, name):
    ns, t, n = dc.shape
    tm = _tm(t)
    ni = t // tm

    def body(dc_ref, wu_ref, cw_ref, h_ref, g_ref, di_ref, da_ref, o_ref, dg_ref, carry):
        i = pl.program_id(0)

        @pl.when(i == 0)
        def _():
            carry[...] = jnp.zeros_like(carry)
            dg_ref[...] = jnp.zeros_like(dg_ref)

        acc = jnp.zeros((tm, D), F32)
        for s in range(ns):
            x = dc_ref[s].astype(F32)
            ext = jnp.concatenate([x, carry[s]], axis=0)
            cwv = cw_ref[s]
            da = (cwv[2:3] * x + cwv[1:2] * ext[1:1 + tm] + cwv[0:1] * ext[2:2 + tm]).astype(BF16)
            carry[s] = x[:HALO]
            da_ref[s] = da
            acc = acc + _dot_nt(da, wu_ref[s])
        hv = h_ref[...]
        r = _rstd(hv)
        gg = acc * g_ref[...]
        o_ref[...] = di_ref[...] + r * gg - hv * (r * r * r * jnp.mean(gg * hv, axis=-1, keepdims=True))
        dg_ref[...] += jnp.sum(acc * hv * r, axis=0, keepdims=True)

    slab = pl.BlockSpec((ns, tm, n), lambda i: (0, ni - 1 - i, 0))
    row = pl.BlockSpec((tm, D), lambda i: (ni - 1 - i, 0))
    vec = pl.BlockSpec((1, D), lambda i: (0, 0))
    return pl.pallas_call(
        body, grid=(ni,), name=name,
        in_specs=[slab, _resident(wup.shape), _resident(cw.shape), row, vec, row],
        out_specs=[slab, row, vec], out_shape=[S((ns, t, n), BF16), S((t, D), F32), S((1, D), F32)],
        scratch_shapes=[pltpu.VMEM((ns, HALO, n), F32)],
        compiler_params=_cp("arbitrary"))(dc, wup, cw, h, gain, dh_in)


def _dw_slot(hn, dy_s, name):
    t, k = hn.shape
    ns, _, n = dy_s.shape
    tm = _tm(t)
    ni = t // tm

    def body(a_ref, b_ref, o_ref, ob_ref, at_ref):
        j = pl.program_id(0)
        i = pl.program_id(1)

        @pl.when(j == 0)
        def _():
            at_ref[i] = a_ref[...].T

        contrib = _dot(at_ref[i], b_ref[...])

        @pl.when(i == 0)
        def _():
            o_ref[...] = contrib

        @pl.when(i > 0)
        def _():
            o_ref[...] += contrib

        @pl.when(i == ni - 1)
        def _():
            ob_ref[...] = o_ref[...].astype(BF16)

    ospec = pl.BlockSpec((None, k, n), lambda j, i: (j, 0, 0))
    return pl.pallas_call(
        body, grid=(ns, ni), name=name,
        in_specs=[pl.BlockSpec((tm, k), lambda j, i: (jnp.where(j == 0, i, ni - 1), 0)),
                  pl.BlockSpec((None, tm, n), lambda j, i: (j, i, 0))],
        out_specs=[ospec, ospec], out_shape=[S((ns, k, n), F32), S((ns, k, n), BF16)],
        scratch_shapes=[pltpu.VMEM((ni, k, tm), BF16)],
        compiler_params=_cp("arbitrary", "arbitrary"))(hn, dy_s)


def _dw_rows(a_s, dh, name):
    nk, t, kc = a_s.shape
    tm = _tm(t)
    ni = t // tm

    def body(a_ref, d_ref, o_ref, ob_ref):
        i = pl.program_id(0)
        dhb = d_ref[...].astype(BF16)

        @pl.when(i == 0)
        def _():
            o_ref[...] = jnp.zeros_like(o_ref)

        for j in range(nk):
            o_ref[j * kc:(j + 1) * kc, :] += _dot_tn(a_ref[j], dhb)

        @pl.when(i == ni - 1)
        def _():
            ob_ref[...] = o_ref[...].astype(BF16)

    ospec = pl.BlockSpec((nk * kc, D), lambda i: (0, 0))
    return pl.pallas_call(
        body, grid=(ni,), name=name,
        in_specs=[pl.BlockSpec((nk, tm, kc), lambda i: (0, i, 0)), pl.BlockSpec((tm, D), lambda i: (i, 0))],
        out_specs=[ospec, ospec], out_shape=[S((nk * kc, D), F32), S((nk * kc, D), BF16)],
        compiler_params=_cp("arbitrary"))(a_s, dh)


def _dx_slot_normbwd(dy_s, wg, h, gain, dh_in, name):
    ns, t, n = dy_s.shape
    tm = _tm(t)

    def body(dy_ref, w_ref, h_ref, g_ref, di_ref, o_ref, dg_ref):
        i = pl.program_id(0)

        @pl.when(i == 0)
        def _():
            dg_ref[...] = jnp.zeros_like(dg_ref)

        g = _dot_nt(dy_ref[0], w_ref[0])
        for s in range(1, ns):
            g = g + _dot_nt(dy_ref[s], w_ref[s])
        hv = h_ref[...]
        r = _rstd(hv)
        gg = g * g_ref[...]
        o_ref[...] = di_ref[...] + r * gg - hv * (r * r * r * jnp.mean(gg * hv, axis=-1, keepdims=True))
        dg_ref[...] += jnp.sum(g * hv * r, axis=0, keepdims=True)

    row = pl.BlockSpec((tm, D), lambda i: (i, 0))
    vec = pl.BlockSpec((1, D), lambda i: (0, 0))
    return pl.pallas_call(
        body, grid=(t // tm,), name=name,
        in_specs=[pl.BlockSpec((ns, tm, n), lambda i: (0, i, 0)), _resident(wg.shape), row, vec, row],
        out_specs=[row, vec], out_shape=[S((t, D), F32), S((1, D), F32)],
        compiler_params=_cp("arbitrary"))(dy_s, wg, h, gain, dh_in)


def _sgu_gate_bwd(a_s, dg_s, vgain, ws, bst, name):
    t = a_s.shape[1]
    sw = a_s.shape[2]
    gps = sw // CHUNK

    def body(a_ref, dg_ref, vg_ref, ws_ref, b_ref, da_ref, dws_ref, dbt_ref, dvg_ref, dvn_ref):
        n = pl.program_id(0)

        @pl.when(n == 0)
        def _():
            dws_ref[...] = jnp.zeros_like(dws_ref)
            dbt_ref[...] = jnp.zeros_like(dbt_ref)
            dvg_ref[...] = jnp.zeros_like(dvg_ref)

        vpre = jnp.concatenate([a_ref[4 + s].astype(F32) for s in range(4)], axis=1)
        v = _gelu(vpre)
        r = _rstd(v)
        vhat = v * r
        vn = (vhat * vg_ref[...]).astype(BF16)
        tri = _tril_mask()
        lane = lax.broadcasted_iota(jnp.int32, (CHUNK, CHUNK), 1)
        dbt = jnp.zeros((CHUNK, CHUNK), F32)
        for g in range(SGU_G):
            w = jnp.where(tri, ws_ref[g], 0.0).astype(BF16)
            vng = vn[:, g * CHUNK:(g + 1) * CHUNK]
            sg = _dot(w, vng) + b_ref[:, g:g + 1]
            lo = (g % gps) * CHUNK
            upre = a_ref[g // gps, :, lo:lo + CHUNK].astype(F32)
            dgate = dg_ref[g // gps, :, lo:lo + CHUNK].astype(F32)
            da_ref[g // gps, :, lo:lo + CHUNK] = (dgate * sg * _gelu_grad(upre)).astype(BF16)
            ds = dgate * _gelu(upre)
            dsb = ds.astype(BF16)
            dvn_ref[:, g * CHUNK:(g + 1) * CHUNK] = _dot_tn(w, dsb)
            dws_ref[g] += jnp.where(tri, _dot_nt(dsb, vng), 0.0)
            dbt = dbt + jnp.where(lane == g, jnp.sum(ds, axis=-1, keepdims=True), 0.0)
        dbt_ref[...] += dbt
        dvn = dvn_ref[...]
        dvg_ref[...] += jnp.sum(dvn * vhat, axis=0, keepdims=True)
        gg = dvn * vg_ref[...]
        dv = r * gg - v * (r * r * r * jnp.mean(gg * v, axis=-1, keepdims=True))
        dav = (dv * _gelu_grad(vpre)).astype(BF16)
        for s in range(4):
            da_ref[4 + s] = dav[:, s * sw:(s + 1) * sw]

    return pl.pallas_call(
        body, grid=(t // CHUNK,), name=name,
        in_specs=[pl.BlockSpec((8, CHUNK, sw), lambda n: (0, n, 0)), pl.BlockSpec((4, CHUNK, sw), lambda n: (0, n, 0)),
                  pl.BlockSpec((1, SGU_W), lambda n: (0, 0)), pl.BlockSpec((SGU_G, CHUNK, CHUNK), lambda n: (0, 0, 0)),
                  pl.BlockSpec((CHUNK, SGU_G), lambda n: (0, 0))],
        out_specs=[pl.BlockSpec((8, CHUNK, sw), lambda n: (0, n, 0)), pl.BlockSpec((SGU_G, CHUNK, CHUNK), lambda n: (0, 0, 0)),
                   pl.BlockSpec((CHUNK, CHUNK), lambda n: (0, 0)), pl.BlockSpec((1, SGU_W), lambda n: (0, 0))],
        out_shape=[S((8, t, sw), BF16), S((SGU_G, CHUNK, CHUNK), F32), S((CHUNK, CHUNK), F32), S((1, SGU_W), F32)],
        scratch_shapes=[pltpu.VMEM((CHUNK, SGU_W), F32)],
        compiler_params=_cp("arbitrary"))(a_s, dg_s, vgain, ws, bst)


def _attn_bwd(qkv_s, do, qg, kg, sinks, bias, name):
    t = qkv_s.shape[1]
    nb = t // CHUNK

    def body(cur_ref, prev_ref, do_ref, qg_ref, kg_ref, sink_ref, bias_ref,
             o_ref, dqg_ref, dkg_ref, dsk_ref, dbias_ref, carry, top):
        n = pl.program_id(0)

        @pl.when(n == 0)
        def _():
            carry[...] = jnp.zeros_like(carry)
            dqg_ref[...] = jnp.zeros_like(dqg_ref)
            dkg_ref[...] = jnp.zeros_like(dkg_ref)
            dsk_ref[...] = jnp.zeros_like(dsk_ref)
            dbias_ref[...] = jnp.zeros_like(dbias_ref)

        @pl.when(n < nb)
        def _():
            valid = _attn_valid(n)
            top[...] = jnp.zeros_like(top)
            new = [[None] * 3 for _ in range(8)]
            lane = lax.broadcasted_iota(jnp.int32, (1, CHUNK), 1)
            dsk = jnp.zeros((1, CHUNK), F32)
            dqg = jnp.zeros((1, HD), F32)
            dkg = jnp.zeros((1, HD), F32)

            def place(col, val):
                new[col // QKV_SLOT][(col % QKV_SLOT) // HD] = val

            for h in range(NKV):
                kcol = D + HD * h
                vcol = D + HD * (NKV + h)
                k = jnp.concatenate([_head(prev_ref, kcol), _head(cur_ref, kcol)], axis=0)
                v = jnp.concatenate([_head(prev_ref, vcol), _head(cur_ref, vcol)], axis=0)
                rk = _rstd(k)
                khat = k * rk
                kn = (khat * kg_ref[...]).astype(BF16)
                vb = v.astype(BF16)
                dkn = jnp.zeros((2 * CHUNK, HD), F32)
                dv = jnp.zeros((2 * CHUNK, HD), F32)
                for g in range(KVG):
                    hq = KVG * h + g
                    q = _head(cur_ref, HD * hq)
                    rq = _rstd(q)
                    qhat = q * rq
                    qn = (qhat * qg_ref[...]).astype(BF16)
                    sink = sink_ref[hq]
                    p, psink = _attn_probs(qn, kn, bias_ref[hq], valid, sink)
                    doh = do_ref[0, :, HD * hq:HD * (hq + 1)]
                    dp = _dot_nt(doh, vb)
                    dsum = jnp.sum(p * dp, axis=-1, keepdims=True)
                    ds = p * (dp - dsum)
                    dsk = dsk + jnp.where(lane == hq, jnp.sum(-psink * dsum), 0.0)
                    dbias_ref[hq] += ds
                    dv = dv + _dot_tn(p.astype(BF16), doh)
                    dsc = (ds * (HD ** -0.5)).astype(BF16)
                    dqn = _dot(dsc, kn)
                    dkn = dkn + _dot_tn(dsc, qn)
                    dqg = dqg + jnp.sum(dqn * qhat, axis=0, keepdims=True)
                    gq = dqn * qg_ref[...]
                    place(HD * hq, rq * gq - q * (rq * rq * rq * jnp.mean(gq * q, axis=-1, keepdims=True)))
                dkg = dkg + jnp.sum(dkn * khat, axis=0, keepdims=True)
                gk = dkn * kg_ref[...]
                dk = rk * gk - k * (rk * rk * rk * jnp.mean(gk * k, axis=-1, keepdims=True))
                place(kcol, dk[CHUNK:])
                place(vcol, dv[CHUNK:])
                top[kcol // QKV_SLOT, :, kcol % QKV_SLOT:kcol % QKV_SLOT + HD] = dk[:CHUNK]
                top[vcol // QKV_SLOT, :, vcol % QKV_SLOT:vcol % QKV_SLOT + HD] = dv[:CHUNK]
            dqg_ref[...] += dqg
            dkg_ref[...] += dkg
            dsk_ref[...] += dsk
            o_ref[...] = (carry[...] + top[...]).astype(BF16)
            for s in range(8):
                carry[s] = jnp.concatenate(new[s], axis=1)

        @pl.when(n == nb)
        def _():
            o_ref[...] = carry[...].astype(BF16)

    blk = lambda f: pl.BlockSpec((8, CHUNK, QKV_SLOT), f)
    cur = lambda n: (0, jnp.minimum(n, nb - 1), 0)
    prev = lambda n: (0, jnp.clip(n - 1, 0, nb - 1), 0)
    small = lambda w: pl.BlockSpec((1, w), lambda n: (0, 0))
    return pl.pallas_call(
        body, grid=(nb + 1,), name=name,
        in_specs=[blk(cur), blk(prev), pl.BlockSpec((1, CHUNK, D), cur), small(HD), small(HD),
                  pl.BlockSpec(memory_space=pltpu.SMEM), pl.BlockSpec((NH, CHUNK, 2 * CHUNK), lambda n: (0, 0, 0))],
        out_specs=[blk(lambda n: (0, jnp.maximum(n - 1, 0), 0)), small(HD), small(HD), small(CHUNK),
                   pl.BlockSpec((NH, CHUNK, 2 * CHUNK), lambda n: (0, 0, 0))],
        out_shape=[S((8, t, QKV_SLOT), BF16), S((1, HD), F32), S((1, HD), F32), S((1, CHUNK), F32),
                   S((NH, CHUNK, 2 * CHUNK), F32)],
        scratch_shapes=[pltpu.VMEM((8, CHUNK, QKV_SLOT), F32), pltpu.VMEM((8, CHUNK, QKV_SLOT), F32)],
        compiler_params=_cp("arbitrary"))(qkv_s, qkv_s, do, qg, kg, sinks, bias)


def _local_step(x, target, rep, wg):
    t = x.shape[0]
    bucket_row = jnp.asarray(_rel_tables().reshape(1, -1))
    bias = _relbias_fwd(rep["rel_bias"].T, bucket_row, "relbias_fwd").reshape(NH, CHUNK, 2 * CHUNK)
    bst = rep["sgu_b_s"][0].T
    ws = rep["sgu_w_s"][0]
    vgain = rep["sgu_v_gain"]
    qg, kg, sinks = rep["attn_q_gain"], rep["attn_k_gain"], rep["attn_sinks"][0]
    w_in, w_out = wg["sgu_w_in"], wg["sgu_w_out"].reshape(SGU_W, D)
    w_qkv, w_o = wg["attn_w_qkv"], wg["attn_w_o"].reshape(D, D)
    w_up = [wg["ffn_w_up%d" % l] for l in range(2)]
    w_down = [wg["ffn_w_down%d" % l].reshape(D_FF, D) for l in range(2)]
    cw = [wg["ffn_conv_w"][:, 3 * l:3 * l + 3] for l in range(2)]
    cb = [rep["ffn_conv_b"][l].reshape(8, 1, -1) for l in range(2)]
    mixg = [rep["mix_norm"][l:l + 1] for l in range(2)]
    ffng = [rep["ffn_norm"][l:l + 1] for l in range(2)]

    hn0 = _rmsnorm(x, mixg[0], "norm0")
    a0 = _mm_slot(hn0, w_in, BF16, "sgu_in")
    gated = _sgu_gate_fwd(a0, vgain, ws, bst, "sgu_gate")
    h1, hn1 = _resid_mm(gated, w_out, x, ffng[0], "norm", "sgu_out")
    ag0, av0, h2, hn2 = _ffn_fwd(hn1, h1, w_up[0], w_down[0], cw[0], cb[0], mixg[1], "norm", "ffn0_fwd")
    qkv = _mm_slot(hn2, w_qkv, F32, "qkv")
    o = _attn_fwd(qkv, qg, kg, sinks, bias, "attn")
    h3, hn3 = _resid_mm(o, w_o, h2, ffng[1], "norm", "attn_out")
    ag1, av1, dy, sq = _ffn_fwd(hn3, h3, w_up[1], w_down[1], cw[1], cb[1], target, "loss", "ffn1_fwd_loss")
    loss = (0.5 / D) * jnp.sum(sq[:, 0, 0])

    def ffn_bwd(dh, h_in, hn, a_g, a_v, l, tag):
        dc, g_down, g_down_b, g_cw, g_cb = _ffn_bwd1(dh, a_g, a_v, w_down[l], cw[l], cb[l], tag + "_bwd1")
        da, dh_new, dgain = _ffn_bwd2(dc, w_up[l], cw[l], h_in, ffng[l], dh, tag + "_bwd2")
        g_up = _dw_slot(hn, da, tag + "_dw_up")
        return dh_new, dgain, g_up, (g_down, g_down_b), g_cw, g_cb.reshape(-1)

    dh, d_ffng1, g_up1, g_down1, g_cw1, g_cb1 = ffn_bwd(dy, h3, hn3, ag1, av1, 1, "ffn1")
    do = _dx_rows(dh, w_o, D, BF16, "attn_do")
    g_wo = _dw_rows(o, dh, "dw_o")
    dqkv, d_qg, d_kg, d_sk, d_bias = _attn_bwd(qkv, do, qg, kg, sinks, bias, "attn_bwd")
    g_qkv = _dw_slot(hn2, dqkv, "dw_qkv")
    dh, d_mixg1 = _dx_slot_normbwd(dqkv, w_qkv, h2, mixg[1], dh, "dx_qkv")
    d_relb = _relbias_bwd(d_bias.reshape(NH, -1), bucket_row, "relbias_bwd").T
    dh, d_ffng0, g_up0, g_down0, g_cw0, g_cb0 = ffn_bwd(dh, h1, hn1, ag0, av0, 0, "ffn0")
    dgated = _dx_rows(dh, w_out, SGU_W // 4, BF16, "sgu_dgated")
    g_wout = _dw_rows(gated, dh, "dw_sgu_out")
    da0, d_ws, d_bst, d_vgain = _sgu_gate_bwd(a0, dgated, vgain, ws, bst, "sgu_gate_bwd")
    g_win = _dw_slot(hn0, da0, "dw_sgu_in")
    grad_x, d_mixg0 = _dx_slot_normbwd(da0, w_in, x, mixg[0], dh, "dx_sgu_in")

    g_rep = {
        "mix_norm": jnp.concatenate([d_mixg0, d_mixg1], axis=0),
        "ffn_norm": jnp.concatenate([d_ffng0, d_ffng1], axis=0),
        "sgu_v_gain": d_vgain,
        "sgu_w_s": d_ws[None],
        "sgu_b_s": d_bst[:, :SGU_G].T[None],
        "attn_q_gain": d_qg,
        "attn_k_gain": d_kg,
        "attn_sinks": d_sk[:, :NH],
        "rel_bias": d_relb,
        "ffn_conv_b": jnp.stack([g_cb0, g_cb1], axis=0),
    }
    rows = lambda pair: tuple(g.reshape(N_DEV, -1, D) for g in pair)
    g_cw = jnp.concatenate([g_cw0, g_cw1], axis=1)
    g_sh = {
        "sgu_w_in": g_win, "sgu_w_out": rows(g_wout), "attn_w_qkv": g_qkv, "attn_w_o": rows(g_wo),
        "ffn_w_up0": g_up0, "ffn_w_up1": g_up1, "ffn_w_down0": rows(g_down0), "ffn_w_down1": rows(g_down1),
        "ffn_conv_w": (g_cw, g_cw.astype(BF16)),
    }
    return loss, grad_x, g_rep, g_sh


ANY = pl.BlockSpec(memory_space=pl.ANY)


def _place():
    x, y, c = lax.axis_index("x"), lax.axis_index("y"), lax.axis_index("c")
    return x, y, c, [(1 - x, y), (x, 1 - y), (1 - x, 1 - y)]


def _allgather(xs, name):
    nt = len(xs)

    def body(*refs):
        x_refs, o_refs = refs[:nt], refs[nt:2 * nt]
        send_sems, recv_sems, local_sems = refs[2 * nt:]
        x, y, c, chips = _place()
        me, sibling = (x, y, c), (x, y, 1 - c)

        def copy(t, k, block, to, src=None):
            px, py, pc = block
            dst = o_refs[t].at[4 * px + 2 * py + pc]
            return pltpu.make_async_remote_copy(
                src_ref=dst if src is None else src, dst_ref=dst, send_sem=send_sems.at[t, k], recv_sem=recv_sems.at[t, k],
                device_id=to, device_id_type=MESH)

        mine = [pltpu.make_async_copy(x_refs[t], o_refs[t].at[4 * x + 2 * y + c], local_sems.at[t]) for t in range(nt)]
        for cp in mine:
            cp.start()
        first = []
        for t in range(nt):
            first.append(copy(t, 0, me, sibling, src=x_refs[t]))
            first += [copy(t, 1 + j, me, (*chip, c), src=x_refs[t]) for j, chip in enumerate(chips)]
        for cp in first:
            cp.start()
        passed = []
        for j, chip in enumerate(chips):
            for t in range(nt):
                copy(t, 1 + j, (*chip, c), me).wait_recv()
                fwd = copy(t, 4 + j, (*chip, c), sibling)
                fwd.start()
                passed.append(fwd)
        for t in range(nt):
            copy(t, 0, sibling, me).wait_recv()
            for j, chip in enumerate(chips):
                copy(t, 4 + j, (*chip, 1 - c), me).wait_recv()
        for cp in first + passed:
            cp.wait_send()
        for cp in mine:
            cp.wait()

    return pl.pallas_call(
        body, name=name, in_specs=[ANY] * nt, out_specs=[ANY] * nt,
        out_shape=[S((N_DEV,) + a.shape, a.dtype) for a in xs],
        scratch_shapes=[pltpu.SemaphoreType.DMA((nt, 7)), pltpu.SemaphoreType.DMA((nt, 7)), pltpu.SemaphoreType.DMA((nt,))],
        compiler_params=pltpu.CompilerParams(has_side_effects=True))(*xs)


def _rs_sibling(gs, name):
    nt = len(gs)

    def body(*refs):
        g_refs, o_refs = refs[:nt], refs[nt:2 * nt]
        send_sems, recv_sems = refs[2 * nt:]
        x, y, c, _ = _place()
        copies = []
        for t in range(nt):
            for k in range(4):
                copies.append(pltpu.make_async_remote_copy(
                    src_ref=g_refs[t].at[2 * k + (1 - c)], dst_ref=o_refs[t].at[k],
                    send_sem=send_sems.at[t, k], recv_sem=recv_sems.at[t, k], device_id=(x, y, 1 - c), device_id_type=MESH))
        for cp in copies:
            cp.start()
        for cp in copies:
            cp.wait_recv()
        for cp in copies:
            cp.wait_send()

    return pl.pallas_call(
        body, name=name, in_specs=[ANY] * nt, out_specs=[ANY] * nt,
        out_shape=[S((4,) + g.shape[1:], g.dtype) for g in gs],
        scratch_shapes=[pltpu.SemaphoreType.DMA((nt, 4)), pltpu.SemaphoreType.DMA((nt, 4))],
        compiler_params=pltpu.CompilerParams(has_side_effects=True))(*gs)


def _rs_chips(ps, name):
    nt = len(ps)

    def body(*refs):
        p_refs, o_refs = refs[:nt], refs[nt:2 * nt]
        send_sems, recv_sems = refs[2 * nt:]
        x, y, c, chips = _place()
        copies = []
        for t in range(nt):
            for j, (px, py) in enumerate(chips):
                copies.append(pltpu.make_async_remote_copy(
                    src_ref=p_refs[t].at[2 * px + py], dst_ref=o_refs[t].at[j],
                    send_sem=send_sems.at[t, j], recv_sem=recv_sems.at[t, j], device_id=(px, py, c), device_id_type=MESH))
        for cp in copies:
            cp.start()
        for cp in copies:
            cp.wait_recv()
        for cp in copies:
            cp.wait_send()

    return pl.pallas_call(
        body, name=name, in_specs=[ANY] * nt, out_specs=[ANY] * nt,
        out_shape=[S((3,) + p.shape[1:], p.dtype) for p in ps],
        scratch_shapes=[pltpu.SemaphoreType.DMA((nt, 3)), pltpu.SemaphoreType.DMA((nt, 3))],
        compiler_params=pltpu.CompilerParams(has_side_effects=True))(*ps)


def _row_tile(r):
    tr = r if r <= ROW_TILE or r % ROW_TILE else ROW_TILE
    assert r % tr == 0
    return tr


def _rs_partial(g32, sib, place, name):
    _, r, cdim = g32.shape
    tr = _row_tile(r)

    def body(place_ref, g_ref, s_ref, p_ref, own_ref):
        k = pl.program_id(1)
        tot = g_ref[...] + s_ref[...].astype(F32)
        p_ref[...] = tot.astype(BF16)

        @pl.when(k == place_ref[1])
        def _():
            own_ref[...] = tot

    grid_spec = pltpu.PrefetchScalarGridSpec(
        num_scalar_prefetch=1, grid=(r // tr, 4),
        in_specs=[pl.BlockSpec((None, None, tr, cdim), lambda i, k, pr: (k, pr[0], i, 0)),
                  pl.BlockSpec((None, tr, cdim), lambda i, k, pr: (k, i, 0))],
        out_specs=[pl.BlockSpec((None, tr, cdim), lambda i, k, pr: (k, i, 0)), pl.BlockSpec((tr, cdim), lambda i, k, pr: (i, 0))])
    return pl.pallas_call(
        body, grid_spec=grid_spec, name=name,
        out_shape=[S((4, r, cdim), BF16), S((r, cdim), F32)],
        compiler_params=_cp("parallel", "arbitrary"))(place, g32.reshape(4, 2, r, cdim), sib)


def _adamw_math(w, g, m, v):
    m = ADAM_B1 * m + (1.0 - ADAM_B1) * g
    v = ADAM_B2 * v + (1.0 - ADAM_B2) * (g * g)
    m_hat = m / (1.0 - ADAM_B1 ** ADAM_STEP)
    v_hat = v / (1.0 - ADAM_B2 ** ADAM_STEP)
    delta = -ADAM_LR * (m_hat / (jnp.sqrt(v_hat) + ADAM_EPS) + ADAM_WD * w)
    return delta, m, v


def _adamw_shard(own, recv, w, m, v, name):
    r, cdim = own.shape
    tr = _row_tile(r)

    def body(own_ref, recv_ref, w_ref, m_ref, v_ref, g_out, d_out, m_out, v_out):
        g = own_ref[...] + recv_ref[0].astype(F32) + recv_ref[1].astype(F32) + recv_ref[2].astype(F32)
        g_out[...] = g
        d_out[...], m_out[...], v_out[...] = _adamw_math(w_ref[...], g, m_ref[...], v_ref[...])

    row = pl.BlockSpec((tr, cdim), lambda i: (i, 0))
    return pl.pallas_call(
        body, grid=(r // tr,), name=name,
        in_specs=[row, pl.BlockSpec((3, tr, cdim), lambda i: (0, i, 0)), row, row, row],
        out_specs=[row] * 4, out_shape=[S((r, cdim), F32)] * 4, compiler_params=_cp("parallel"))(own, recv, w, m, v)


def _adamw_replicated(gall, w, m, v, name):
    _, r, cdim = gall.shape
    tr = r // 3 if r % 24 == 0 else r

    def body(g_ref, w_ref, m_ref, v_ref, g_out, d_out, m_out, v_out):
        g = g_ref[0]
        for s in range(1, N_DEV):
            g = g + g_ref[s]
        g_out[...] = g
        d_out[...], m_out[...], v_out[...] = _adamw_math(w_ref[...], g, m_ref[...], v_ref[...])

    row = pl.BlockSpec((tr, cdim), lambda i: (i, 0))
    return pl.pallas_call(
        body, grid=(r // tr,), name=name,
        in_specs=[pl.BlockSpec((N_DEV, tr, cdim), lambda i: (0, i, 0)), row, row, row],
        out_specs=[row] * 4, out_shape=[S((r, cdim), F32)] * 4, compiler_params=_cp("parallel"))(gall, w, m, v)


REPLICATED = ["mix_norm", "ffn_norm", "sgu_v_gain", "sgu_w_s", "sgu_b_s", "attn_q_gain", "attn_k_gain", "attn_sinks", "rel_bias",
              "ffn_conv_b"]
WEIGHTS = ["mix_norm", "ffn_norm", "sgu_w_in", "sgu_v_gain", "sgu_w_s", "sgu_b_s", "sgu_w_out", "attn_w_qkv", "attn_q_gain",
           "attn_k_gain", "attn_sinks", "attn_w_o", "rel_bias", "ffn_w_up", "ffn_conv_w", "ffn_conv_b", "ffn_w_down"]
PACK_UNIT = 1024


def _shard_views(w):
    return {
        "sgu_w_in": w["sgu_w_in"][0], "sgu_w_out": w["sgu_w_out"][0], "attn_w_qkv": w["attn_w_qkv"][0], "attn_w_o": w["attn_w_o"][0],
        "ffn_w_up0": w["ffn_w_up"][0], "ffn_w_up1": w["ffn_w_up"][1], "ffn_w_down0": w["ffn_w_down"][0], "ffn_w_down1": w["ffn_w_down"][1],
        "ffn_conv_w": w["ffn_conv_w"].reshape(6, -1),
    }


def _unshard_views(d):
    return {
        "sgu_w_in": d["sgu_w_in"][None], "sgu_w_out": d["sgu_w_out"][None], "attn_w_qkv": d["attn_w_qkv"][None], "attn_w_o": d["attn_w_o"][None],
        "ffn_w_up": jnp.stack([d["ffn_w_up0"], d["ffn_w_up1"]]), "ffn_w_down": jnp.stack([d["ffn_w_down0"], d["ffn_w_down1"]]),
        "ffn_conv_w": d["ffn_conv_w"].reshape(2, 3, -1),
    }


def _pack(d):
    parts = []
    for n in REPLICATED:
        flat = d[n].reshape(-1)
        parts.append(jnp.pad(flat, (0, -flat.shape[0] % PACK_UNIT)))
    return jnp.concatenate(parts).reshape(-1, 128)


def _unpack(p, like):
    flat = p.reshape(-1)
    out, off = {}, 0
    for n in REPLICATED:
        size = math.prod(like[n].shape)
        out[n] = flat[off:off + size].reshape(like[n].shape)
        off += size + (-size % PACK_UNIT)
    return out


def kernel(x, mix_norm, ffn_norm, sgu_w_in, sgu_v_gain, sgu_w_s, sgu_b_s, sgu_w_out, attn_w_qkv, attn_q_gain, attn_k_gain, attn_sinks, attn_w_o, rel_bias, ffn_w_up, ffn_conv_w, ffn_conv_b, ffn_w_down, loss_target, m_mix_norm, m_ffn_norm, m_sgu_w_in, m_sgu_v_gain, m_sgu_w_s, m_sgu_b_s, m_sgu_w_out, m_attn_w_qkv, m_attn_q_gain, m_attn_k_gain, m_attn_sinks, m_attn_w_o, m_rel_bias, m_ffn_w_up, m_ffn_conv_w, m_ffn_conv_b, m_ffn_w_down, v_mix_norm, v_ffn_norm, v_sgu_w_in, v_sgu_v_gain, v_sgu_w_s, v_sgu_b_s, v_sgu_w_out, v_attn_w_qkv, v_attn_q_gain, v_attn_k_gain, v_attn_sinks, v_attn_w_o, v_rel_bias, v_ffn_w_up, v_ffn_conv_w, v_ffn_conv_b, v_ffn_w_down):
    w = dict(zip(WEIGHTS, (mix_norm, ffn_norm, sgu_w_in, sgu_v_gain, sgu_w_s, sgu_b_s, sgu_w_out, attn_w_qkv, attn_q_gain, attn_k_gain,
                           attn_sinks, attn_w_o, rel_bias, ffn_w_up, ffn_conv_w, ffn_conv_b, ffn_w_down)))
    m = dict(zip(WEIGHTS, (m_mix_norm, m_ffn_norm, m_sgu_w_in, m_sgu_v_gain, m_sgu_w_s, m_sgu_b_s, m_sgu_w_out, m_attn_w_qkv, m_attn_q_gain,
                           m_attn_k_gain, m_attn_sinks, m_attn_w_o, m_rel_bias, m_ffn_w_up, m_ffn_conv_w, m_ffn_conv_b, m_ffn_w_down)))
    v = dict(zip(WEIGHTS, (v_mix_norm, v_ffn_norm, v_sgu_w_in, v_sgu_v_gain, v_sgu_w_s, v_sgu_b_s, v_sgu_w_out, v_attn_w_qkv, v_attn_q_gain,
                           v_attn_k_gain, v_attn_sinks, v_attn_w_o, v_rel_bias, v_ffn_w_up, v_ffn_conv_w, v_ffn_conv_b, v_ffn_w_down)))
    rep = {n: w[n] for n in REPLICATED}

    ws, ms, vs = _shard_views(w), _shard_views(m), _shard_views(v)
    names = list(ws)
    send = [ws[n] if n == "ffn_conv_w" else ws[n].astype(BF16) for n in names]
    wg = dict(zip(names, _allgather(send, "gather_weights")))

    loss, grad_x, g_rep, g_sh = _local_step(x[0], loss_target[0], rep, wg)
    loss = lax.psum(loss, ("x", "y", "c"))

    xi, yi, ci = lax.axis_index("x"), lax.axis_index("y"), lax.axis_index("c")
    place = jnp.stack([ci, 2 * xi + yi]).astype(jnp.int32)
    sib = _rs_sibling([g_sh[n][1] for n in names], "rs_sibling")
    parts = [_rs_partial(g_sh[n][0], s, place, "rs_partial_" + n) for n, s in zip(names, sib)]
    recv = _rs_chips([p[0] for p in parts], "rs_chips")
    res = [_adamw_shard(p[1], r, ws[n], ms[n], vs[n], "adamw_" + n) for n, p, r in zip(names, parts, recv)]
    out = [_unshard_views({n: r[i] for n, r in zip(names, res)}) for i in range(4)]

    gall = _allgather([_pack(g_rep)], "gather_small_grads")[0]
    small = _adamw_replicated(gall, _pack(rep), _pack({n: m[n] for n in REPLICATED}), _pack({n: v[n] for n in REPLICATED}), "adamw_small")
    for i in range(4):
        out[i].update(_unpack(small[i], rep))

    return (loss, grad_x[None], *[out[0][n] for n in WEIGHTS], *[out[1][n] for n in WEIGHTS],
            *[out[2][n] for n in WEIGHTS], *[out[3][n] for n in WEIGHTS])
```

```python
import functools
import math

import numpy as np
import jax
import jax.numpy as jnp
from jax import lax
from jax.experimental import pallas as pl
from jax.experimental.pallas import tpu as pltpu

F32 = jnp.float32
BF16 = jnp.bfloat16
S = jax.ShapeDtypeStruct

D = 1024
CHUNK = 128
SGU_W = 2048
SGU_G = 16
HD = 64
NH = 16
NKV = 4
KVG = 4
D_FF = 2816
REL_BUCKETS = 32
REL_MAX_DIST = 128
EPS = 1e-6
N_DEV = 8
MESH = pl.DeviceIdType.MESH

ADAM_LR = 0.001
ADAM_B1 = 0.9
ADAM_B2 = 0.999
ADAM_EPS = 1e-08
ADAM_WD = 0.01
ADAM_STEP = 10

ROW_TILE = 512
HALO = 8


def _tm(t):
    return min(ROW_TILE, t)


def _cp(*sem):
    return pltpu.CompilerParams(dimension_semantics=sem)


ANY = pl.BlockSpec(memory_space=pl.ANY)


def _place():
    x, y, c = lax.axis_index("x"), lax.axis_index("y"), lax.axis_index("c")
    return x, y, c, [(1 - x, y), (x, 1 - y), (1 - x, 1 - y)]


class _Comm:
    SEMS = {"ag1": 5, "ag2": 3, "rs1": 4, "rs2": 3}

    def __init__(self):
        self.inputs, self.out_shapes, self.aliases, self.ops, self.n_sems = [], [], {}, [], 0

    def add(self, kind, arr):
        lead = {"ag1": N_DEV, "ag2": None, "rs1": 4, "rs2": 3}[kind]
        shape = arr.shape if lead is None else (lead,) + arr.shape[(0 if kind == "ag1" else 1):]
        if kind == "ag2":
            self.aliases[len(self.inputs)] = len(self.out_shapes)
        self.ops.append((kind, len(self.inputs), len(self.out_shapes), self.n_sems))
        self.inputs.append(arr)
        self.out_shapes.append(S(shape, arr.dtype))
        self.n_sems += self.SEMS[kind]
        return len(self.out_shapes) - 1

    def _copies(self, ins, outs, send, recv):
        x, y, c, chips = _place()
        me, sibling = (x, y, c), (x, y, 1 - c)
        slot = lambda px, py, pc: 4 * px + 2 * py + pc
        sends, recvs, local = [], [], []

        def rc(src, dst, k, to):
            return pltpu.make_async_remote_copy(src_ref=src, dst_ref=dst, send_sem=send.at[k], recv_sem=recv.at[k],
                                                device_id=to, device_id_type=MESH)

        for kind, ii, oi, b in self.ops:
            src, dst = ins[ii], outs[oi]
            if kind == "ag1":
                mine = dst.at[slot(*me)]
                sends.append(rc(src, mine, b, sibling))
                recvs.append(rc(src, dst.at[slot(x, y, 1 - c)], b, me))
                for j, chip in enumerate(chips):
                    sends.append(rc(src, mine, b + 1 + j, (*chip, c)))
                    recvs.append(rc(src, dst.at[slot(*chip, c)], b + 1 + j, me))
                local.append(pltpu.make_async_copy(src, mine, send.at[b + 4]))
            elif kind == "ag2":
                for j, chip in enumerate(chips):
                    sends.append(rc(dst.at[slot(*chip, c)], dst.at[slot(*chip, c)], b + j, sibling))
                    recvs.append(rc(dst.at[slot(*chip, 1 - c)], dst.at[slot(*chip, 1 - c)], b + j, me))
            elif kind == "rs1":
                for k in range(4):
                    sends.append(rc(src.at[2 * k + (1 - c)], dst.at[k], b + k, sibling))
                    recvs.append(rc(src.at[2 * k + c], dst.at[k], b + k, me))
            else:
                for j, (px, py) in enumerate(chips):
                    sends.append(rc(src.at[2 * px + py], dst.at[j], b + j, (px, py, c)))
                    recvs.append(rc(src.at[2 * px + py], dst.at[j], b + j, me))
        return sends, recvs, local

    def start(self, ins, outs, send, recv):
        sends, _, local = self._copies(ins, outs, send, recv)
        for cp in local + sends:
            cp.start()

    def finish(self, ins, outs, send, recv):
        sends, recvs, local = self._copies(ins, outs, send, recv)
        for cp in recvs:
            cp.wait_recv()
        for cp in sends:
            cp.wait_send()
        for cp in local:
            cp.wait()


def _run(body, args, hook, *, grid, in_specs, out_specs, out_shape, name, semantics, scratch_shapes=()):
    comm = hook() if hook is not None else None
    if comm is None:
        return pl.pallas_call(body, grid=grid, in_specs=in_specs, out_specs=out_specs, out_shape=out_shape, name=name,
                              scratch_shapes=list(scratch_shapes), compiler_params=_cp(*semantics))(*args)
    single = not isinstance(out_shape, (list, tuple))
    out_shapes = [out_shape] if single else list(out_shape)
    out_specs_l = [out_specs] if single else list(out_specs)
    n_in, n_out, n_scr, ci, co = len(args), len(out_shapes), len(scratch_shapes), len(comm.inputs), len(comm.out_shapes)

    def wrapped(*refs):
        ins, cins = refs[:n_in], refs[n_in:n_in + ci]
        outs, couts = refs[n_in + ci:n_in + ci + n_out], refs[n_in + ci + n_out:n_in + ci + n_out + co]
        scr = refs[n_in + ci + n_out + co:n_in + ci + n_out + co + n_scr]
        send, recv = refs[-2:]
        first = functools.reduce(lambda a, b: a & b, [pl.program_id(a) == 0 for a in range(len(grid))])
        last = functools.reduce(lambda a, b: a & b, [pl.program_id(a) == g - 1 for a, g in enumerate(grid)])

        @pl.when(first)
        def _():
            comm.start(cins, couts, send, recv)

        body(*ins, *outs, *scr)

        @pl.when(last)
        def _():
            comm.finish(cins, couts, send, recv)

    res = pl.pallas_call(
        wrapped, grid=grid, in_specs=list(in_specs) + [ANY] * ci, out_specs=out_specs_l + [ANY] * co,
        out_shape=out_shapes + comm.out_shapes, name=name,
        scratch_shapes=list(scratch_shapes) + [pltpu.SemaphoreType.DMA((comm.n_sems,)), pltpu.SemaphoreType.DMA((comm.n_sems,))],
        input_output_aliases={n_in + k: n_out + v for k, v in comm.aliases.items()},
        compiler_params=pltpu.CompilerParams(dimension_semantics=("arbitrary",) * len(grid), has_side_effects=True))(*args, *comm.inputs)
    hook(res[n_out:])
    return res[0] if single else list(res[:n_out])


def _dot(a, b):
    return jnp.dot(a, b, preferred_element_type=F32)


def _dot_nt(a, b):
    return lax.dot_general(a, b, (((1,), (1,)), ((), ())), preferred_element_type=F32)


def _dot_tn(a, b):
    return lax.dot_general(a, b, (((0,), (0,)), ((), ())), preferred_element_type=F32)


def _gelu(x):
    return 0.5 * x * (1.0 + lax.erf(x * (2.0 ** -0.5)))


def _gelu_grad(x):
    return 0.5 * (1.0 + lax.erf(x * (2.0 ** -0.5))) + x * jnp.exp(-0.5 * x * x) * (1.0 / math.sqrt(2.0 * math.pi))


def _sigmoid(x):
    return 1.0 / (1.0 + jnp.exp(-x))


def _rstd(x):
    return lax.rsqrt(jnp.mean(x * x, axis=-1, keepdims=True) + EPS)


def _rel_tables():
    q = np.arange(CHUNK)[:, None] + CHUNK
    k = np.arange(2 * CHUNK)[None, :]
    dist = q - k
    n = np.maximum(dist, 0)
    max_exact = REL_BUCKETS // 2
    large = max_exact + (np.log(np.maximum(n, 1).astype(np.float32) / max_exact)
                         / math.log(REL_MAX_DIST / max_exact) * (REL_BUCKETS - max_exact)).astype(np.int32)
    large = np.minimum(large, REL_BUCKETS - 1)
    return np.where(n < max_exact, n, large).astype(np.int32)


def _rmsnorm(x, gain, name):
    t = x.shape[0]
    tm = _tm(t)

    def body(x_ref, g_ref, o_ref):
        xv = x_ref[...]
        o_ref[...] = (xv * _rstd(xv) * g_ref[...]).astype(BF16)

    return pl.pallas_call(
        body, grid=(t // tm,), name=name,
        in_specs=[pl.BlockSpec((tm, D), lambda i: (i, 0)), pl.BlockSpec((1, D), lambda i: (0, 0))],
        out_specs=pl.BlockSpec((tm, D), lambda i: (i, 0)),
        out_shape=S((t, D), BF16), compiler_params=_cp("parallel"))(x, gain)


def _resident(shape):
    zeros = (0,) * len(shape)
    return pl.BlockSpec(shape, lambda *_: zeros, pipeline_mode=pl.Buffered(1))


def _mm_slot(hn, wg, out_dtype, name, hook=None):
    t, k = hn.shape
    ns, _, n = wg.shape
    tm = _tm(t)

    def body(a_ref, w_ref, o_ref):
        a = a_ref[...]
        for s in range(ns):
            o_ref[s] = _dot(a, w_ref[s]).astype(out_dtype)

    return _run(
        body, [hn, wg], hook, grid=(t // tm,), name=name, semantics=("parallel",),
        in_specs=[pl.BlockSpec((tm, k), lambda i: (i, 0)), _resident(wg.shape)],
        out_specs=pl.BlockSpec((ns, tm, n), lambda i: (0, i, 0)), out_shape=S((ns, t, n), out_dtype))


def _conv3(a, prev, cw, cb, tm):
    ext = jnp.concatenate([prev, a], axis=0)
    a1 = ext[HALO - 1:HALO - 1 + tm]
    a2 = ext[HALO - 2:HALO - 2 + tm]
    return cw[2:3] * a + cw[1:2] * a1 + cw[0:1] * a2 + cb, a1, a2


def _ffn_fwd(hn, h, wup, wdown, cw, cb, extra, mode, name, hook=None):
    t, k = hn.shape
    n = wup.shape[-1]
    nh = wup.shape[0] // 2
    tm = _tm(t)
    ni = t // tm

    def body(a_ref, h_ref, wu_ref, wd_ref, cw_ref, cb_ref, e_ref, ag_ref, av_ref, o1_ref, o2_ref, carry):
        i = pl.program_id(0)

        @pl.when(i == 0)
        def _():
            carry[...] = jnp.zeros_like(carry)

        a = a_ref[...]
        acc = h_ref[...]
        for j in range(nh):
            ag = _dot(a, wu_ref[j])
            av = _dot(a, wu_ref[nh + j])
            ag_ref[j] = ag.astype(BF16)
            av_ref[j] = av.astype(BF16)
            cg, _, _ = _conv3(ag, carry[j], cw_ref[j], cb_ref[j], tm)
            cv, _, _ = _conv3(av, carry[nh + j], cw_ref[nh + j], cb_ref[nh + j], tm)
            carry[j] = ag[tm - HALO:]
            carry[nh + j] = av[tm - HALO:]
            act = (cg * _sigmoid(cg) * cv).astype(BF16)
            acc = acc + _dot(act, wd_ref[j * n:(j + 1) * n, :])
        if mode == "norm":
            o1_ref[...] = acc
            o2_ref[...] = (acc * _rstd(acc) * e_ref[...]).astype(BF16)
        else:
            err = acc - e_ref[...]
            o1_ref[...] = err * (1.0 / D)
            o2_ref[...] = jnp.full(o2_ref.shape, jnp.sum(err * err), F32)

    row = pl.BlockSpec((tm, D), lambda i: (i, 0))
    if mode == "norm":
        e_spec, o2_spec, o2_shape = pl.BlockSpec((1, D), lambda i: (0, 0)), row, S((t, D), BF16)
    else:
        e_spec, o2_spec, o2_shape = row, pl.BlockSpec((None, 8, 128), lambda i: (i, 0, 0)), S((ni, 8, 128), F32)
    aspec = pl.BlockSpec((nh, tm, n), lambda i: (0, i, 0))
    return _run(
        body, [hn, h, wup, wdown, cw, cb, extra], hook, grid=(ni,), name=name, semantics=("arbitrary",),
        in_specs=[pl.BlockSpec((tm, k), lambda i: (i, 0)), row, _resident(wup.shape), _resident(wdown.shape),
                  _resident(cw.shape), _resident(cb.shape), e_spec],
        out_specs=[aspec, aspec, row, o2_spec],
        out_shape=[S((nh, t, n), BF16), S((nh, t, n), BF16), S((t, D), F32), o2_shape],
        scratch_shapes=[pltpu.VMEM((2 * nh, HALO, n), F32)])


def _tril_mask():
    r = lax.broadcasted_iota(jnp.int32, (CHUNK, CHUNK), 0)
    c = lax.broadcasted_iota(jnp.int32, (CHUNK, CHUNK), 1)
    return r >= c


def _sgu_gate_fwd(a_s, vgain, ws, bst, name, hook=None):
    t = a_s.shape[1]
    sw = a_s.shape[2]
    gps = sw // CHUNK

    def body(a_ref, vg_ref, ws_ref, b_ref, o_ref):
        v = _gelu(jnp.concatenate([a_ref[4 + s].astype(F32) for s in range(4)], axis=1))
        vn = (v * _rstd(v) * vg_ref[...]).astype(BF16)
        tri = _tril_mask()
        for g in range(SGU_G):
            w = jnp.where(tri, ws_ref[g], 0.0).astype(BF16)
            sg = _dot(w, vn[:, g * CHUNK:(g + 1) * CHUNK]) + b_ref[:, g:g + 1]
            lo = (g % gps) * CHUNK
            u = _gelu(a_ref[g // gps, :, lo:lo + CHUNK].astype(F32))
            o_ref[g // gps, :, lo:lo + CHUNK] = (u * sg).astype(BF16)

    return _run(
        body, [a_s, vgain, ws, bst], hook, grid=(t // CHUNK,), name=name, semantics=("parallel",),
        in_specs=[pl.BlockSpec((8, CHUNK, sw), lambda n: (0, n, 0)), pl.BlockSpec((1, SGU_W), lambda n: (0, 0)),
                  pl.BlockSpec((SGU_G, CHUNK, CHUNK), lambda n: (0, 0, 0)), pl.BlockSpec((CHUNK, SGU_G), lambda n: (0, 0))],
        out_specs=pl.BlockSpec((4, CHUNK, sw), lambda n: (0, n, 0)), out_shape=S((4, t, sw), BF16))


def _resid_mm(a_s, w, resid, extra, mode, name, hook=None):
    nk, t, kc = a_s.shape
    tm = _tm(t)
    ni = t // tm

    def body(a_ref, w_ref, r_ref, e_ref, o1_ref, o2_ref):
        h = r_ref[...]
        for j in range(nk):
            h = h + _dot(a_ref[j], w_ref[j * kc:(j + 1) * kc, :])
        if mode == "norm":
            o1_ref[...] = h
            o2_ref[...] = (h * _rstd(h) * e_ref[...]).astype(BF16)
        else:
            err = h - e_ref[...]
            o1_ref[...] = err * (1.0 / D)
            o2_ref[...] = jnp.full(o2_ref.shape, jnp.sum(err * err), F32)

    row = pl.BlockSpec((tm, D), lambda i: (i, 0))
    if mode == "norm":
        e_spec, o2_spec, o2_shape = pl.BlockSpec((1, D), lambda i: (0, 0)), row, S((t, D), BF16)
    else:
        e_spec, o2_spec, o2_shape = row, pl.BlockSpec((None, 8, 128), lambda i: (i, 0, 0)), S((ni, 8, 128), F32)
    return _run(
        body, [a_s, w, resid, extra], hook, grid=(ni,), name=name, semantics=("parallel",),
        in_specs=[pl.BlockSpec((nk, tm, kc), lambda i: (0, i, 0)), _resident(w.shape), row, e_spec],
        out_specs=[row, o2_spec], out_shape=[S((t, D), F32), o2_shape])


def _relbias_fwd(rel_bias_t, bucket_row, name):
    nb = bucket_row.shape[1]

    def body(rb_ref, bk_ref, o_ref):
        onehot = (lax.broadcasted_iota(jnp.int32, (REL_BUCKETS, nb), 0) == bk_ref[...]).astype(F32)
        o_ref[...] = jnp.dot(rb_ref[...], onehot, precision=lax.Precision.HIGHEST, preferred_element_type=F32)

    return pl.pallas_call(body, out_shape=S((NH, nb), F32), name=name)(rel_bias_t, bucket_row)


def _relbias_bwd(dbias, bucket_row, name):
    nb = bucket_row.shape[1]

    def body(db_ref, bk_ref, o_ref):
        onehot = (lax.broadcasted_iota(jnp.int32, (REL_BUCKETS, nb), 0) == bk_ref[...]).astype(F32)
        o_ref[...] = lax.dot_general(db_ref[...], onehot, (((1,), (1,)), ((), ())),
                                     precision=lax.Precision.HIGHEST, preferred_element_type=F32)

    return pl.pallas_call(body, out_shape=S((NH, REL_BUCKETS), F32), name=name)(dbias, bucket_row)


QKV_SLOT = 192


def _head(ref, col):
    return ref[col // QKV_SLOT, :, col % QKV_SLOT:col % QKV_SLOT + HD]


def _attn_valid(n):
    qi = lax.broadcasted_iota(jnp.int32, (CHUNK, 2 * CHUNK), 0)
    kj = lax.broadcasted_iota(jnp.int32, (CHUNK, 2 * CHUNK), 1)
    dist = qi + CHUNK - kj
    return (dist >= 0) & (dist < CHUNK) & ((n > 0) | (kj >= CHUNK))


def _attn_probs(qn, kn, bias, valid, sink):
    s = _dot_nt(qn, kn) * (HD ** -0.5) + bias
    s = jnp.where(valid, s, -jnp.inf)
    m = jnp.maximum(jnp.max(s, axis=-1, keepdims=True), sink)
    p = jnp.exp(s - m)
    psink = jnp.exp(sink - m)
    inv = 1.0 / (jnp.sum(p, axis=-1, keepdims=True) + psink)
    return p * inv, psink * inv


def _attn_fwd(qkv_s, qg, kg, sinks, bias, name, hook=None):
    t = qkv_s.shape[1]

    def body(cur_ref, prev_ref, qg_ref, kg_ref, sink_ref, bias_ref, o_ref):
        n = pl.program_id(0)
        valid = _attn_valid(n)
        for h in range(NKV):
            k = jnp.concatenate([_head(prev_ref, D + HD * h), _head(cur_ref, D + HD * h)], axis=0)
            v = jnp.concatenate([_head(prev_ref, D + HD * (NKV + h)), _head(cur_ref, D + HD * (NKV + h))], axis=0)
            kn = (k * _rstd(k) * kg_ref[...]).astype(BF16)
            vb = v.astype(BF16)
            outs = []
            for g in range(KVG):
                hq = KVG * h + g
                q = _head(cur_ref, HD * hq)
                qn = (q * _rstd(q) * qg_ref[...]).astype(BF16)
                p, _ = _attn_probs(qn, kn, bias_ref[hq], valid, sink_ref[hq])
                outs.append(_dot(p.astype(BF16), vb))
            o_ref[0, :, KVG * HD * h:KVG * HD * (h + 1)] = jnp.concatenate(outs, axis=1).astype(BF16)

    blk = lambda f: pl.BlockSpec((8, CHUNK, QKV_SLOT), f)
    return _run(
        body, [qkv_s, qkv_s, qg, kg, sinks, bias], hook, grid=(t // CHUNK,), name=name, semantics=("parallel",),
        in_specs=[blk(lambda n: (0, n, 0)), blk(lambda n: (0, jnp.maximum(n - 1, 0), 0)),
                  pl.BlockSpec((1, HD), lambda n: (0, 0)), pl.BlockSpec((1, HD), lambda n: (0, 0)),
                  pl.BlockSpec(memory_space=pltpu.SMEM), pl.BlockSpec((NH, CHUNK, 2 * CHUNK), lambda n: (0, 0, 0))],
        out_specs=pl.BlockSpec((1, CHUNK, D), lambda n: (0, n, 0)), out_shape=S((1, t, D), BF16))


def _dx_rows(dh, w, kc, out_dtype, name, hook=None):
    t = dh.shape[0]
    nk = w.shape[0] // kc
    tm = _tm(t)

    def body(d_ref, w_ref, o_ref):
        dhb = d_ref[...].astype(BF16)
        for j in range(nk):
            o_ref[j] = _dot_nt(dhb, w_ref[j * kc:(j + 1) * kc, :]).astype(out_dtype)

    return _run(
        body, [dh, w], hook, grid=(t // tm,), name=name, semantics=("parallel",),
        in_specs=[pl.BlockSpec((tm, D), lambda i: (i, 0)), _resident(w.shape)],
        out_specs=pl.BlockSpec((nk, tm, kc), lambda i: (0, i, 0)), out_shape=S((nk, t, kc), out_dtype))


BWD1_ROWS = 256


def _ffn_bwd1(dh, a_g, a_v, wdown, cw, cb, name, hook=None):
    nh, t, n = a_g.shape
    tm = min(BWD1_ROWS, t)
    ni = t // tm

    def body(d_ref, ag_ref, av_ref, wd_ref, cw_ref, cb_ref, dc_ref, dw_hbm, dwb_hbm, dcw_ref, dcb_ref, carry, acc, stage):
        i = pl.program_id(0)

        @pl.when(i == 0)
        def _():
            carry[...] = jnp.zeros_like(carry)
            acc[...] = jnp.zeros_like(acc)
            dcw_ref[...] = jnp.zeros_like(dcw_ref)
            dcb_ref[...] = jnp.zeros_like(dcb_ref)

        dhb = d_ref[...].astype(BF16)
        rsum = lambda x: jnp.sum(x, axis=0, keepdims=True)
        for j in range(nh):
            dact = _dot_nt(dhb, wd_ref[j * n:(j + 1) * n, :])
            ag = ag_ref[j].astype(F32)
            av = av_ref[j].astype(F32)
            cg, ag1, ag2 = _conv3(ag, carry[j], cw_ref[j], cb_ref[j], tm)
            cv, av1, av2 = _conv3(av, carry[nh + j], cw_ref[nh + j], cb_ref[nh + j], tm)
            carry[j] = ag[tm - HALO:]
            carry[nh + j] = av[tm - HALO:]
            sg = _sigmoid(cg)
            gs = cg * sg
            acc[j * n:(j + 1) * n, :] += _dot_tn((gs * cv).astype(BF16), dhb)
            dcg = dact * cv * (sg * (1.0 + cg * (1.0 - sg)))
            dcv = dact * gs
            dc_ref[j] = dcg.astype(BF16)
            dc_ref[nh + j] = dcv.astype(BF16)
            dcw_ref[j] += jnp.concatenate([rsum(dcg * ag2), rsum(dcg * ag1), rsum(dcg * ag)], axis=0)
            dcw_ref[nh + j] += jnp.concatenate([rsum(dcv * av2), rsum(dcv * av1), rsum(dcv * av)], axis=0)
            dcb_ref[j] += rsum(dcg)
            dcb_ref[nh + j] += rsum(dcv)

        @pl.when(i == ni - 1)
        def _():
            pltpu.sync_copy(acc, dw_hbm)
            for j in range(nh):
                stage[...] = acc[j * n:(j + 1) * n, :].astype(BF16)
                pltpu.sync_copy(stage, dwb_hbm.at[pl.ds(j * n, n), :])

    aspec = pl.BlockSpec((nh, tm, n), lambda i: (0, i, 0))
    return _run(
        body, [dh, a_g, a_v, wdown, cw, cb], hook, grid=(ni,), name=name, semantics=("arbitrary",),
        in_specs=[pl.BlockSpec((tm, D), lambda i: (i, 0)), aspec, aspec, _resident(wdown.shape), _resident(cw.shape), _resident(cb.shape)],
        out_specs=[pl.BlockSpec((2 * nh, tm, n), lambda i: (0, i, 0)), ANY, ANY,
                   pl.BlockSpec(cw.shape, lambda i: (0, 0, 0)), pl.BlockSpec(cb.shape, lambda i: (0, 0, 0))],
        out_shape=[S((2 * nh, t, n), BF16), S(wdown.shape, F32), S(wdown.shape, BF16), S(cw.shape, F32), S(cb.shape, F32)],
        scratch_shapes=[pltpu.VMEM((2 * nh, HALO, n), F32), pltpu.VMEM(wdown.shape, F32), pltpu.VMEM((n, D), BF16)])


def _ffn_bwd2(dc, wup, cw, h, gain, dh_in, name, hook=None):
    ns, t, n = dc.shape
    tm = _tm(t)
    ni = t // tm

    def body(dc_ref, wu_ref, cw_ref, h_ref, g_ref, di_ref, da_ref, o_ref, dg_ref, carry):
        i = pl.program_id(0)

        @pl.when(i == 0)
        def _():
            carry[...] = jnp.zeros_like(carry)
            dg_ref[...] = jnp.zeros_like(dg_ref)

        acc = jnp.zeros((tm, D), F32)
        for s in range(ns):
            x = dc_ref[s].astype(F32)
            ext = jnp.concatenate([x, carry[s]], axis=0)
            cwv = cw_ref[s]
            da = (cwv[2:3] * x + cwv[1:2] * ext[1:1 + tm] + cwv[0:1] * ext[2:2 + tm]).astype(BF16)
            carry[s] = x[:HALO]
            da_ref[s] = da
            acc = acc + _dot_nt(da, wu_ref[s])
        hv = h_ref[...]
        r = _rstd(hv)
        gg = acc * g_ref[...]
        o_ref[...] = di_ref[...] + r * gg - hv * (r * r * r * jnp.mean(gg * hv, axis=-1, keepdims=True))
        dg_ref[...] += jnp.sum(acc * hv * r, axis=0, keepdims=True)

    slab = pl.BlockSpec((ns, tm, n), lambda i: (0, ni - 1 - i, 0))
    row = pl.BlockSpec((tm, D), lambda i: (ni - 1 - i, 0))
    vec = pl.BlockSpec((1, D), lambda i: (0, 0))
    return _run(
        body, [dc, wup, cw, h, gain, dh_in], hook, grid=(ni,), name=name, semantics=("arbitrary",),
        in_specs=[slab, _resident(wup.shape), _resident(cw.shape), row, vec, row],
        out_specs=[slab, row, vec], out_shape=[S((ns, t, n), BF16), S((t, D), F32), S((1, D), F32)],
        scratch_shapes=[pltpu.VMEM((ns, HALO, n), F32)])


def _dw_slot(hn, dy_s, name, hook=None):
    t, k = hn.shape
    ns, _, n = dy_s.shape
    tm = _tm(t)

    def body(a_ref, b_ref, o_ref, ob_ref, at_ref):
        @pl.when(pl.program_id(0) == 0)
        def _():
            for i in range(t // tm):
                at_ref[:, i * tm:(i + 1) * tm] = a_ref[i * tm:(i + 1) * tm, :].T

        acc = _dot(at_ref[...], b_ref[...])
        o_ref[...] = acc
        ob_ref[...] = acc.astype(BF16)

    ospec = pl.BlockSpec((None, k, n), lambda j: (j, 0, 0))
    return _run(
        body, [hn, dy_s], hook, grid=(ns,), name=name, semantics=("arbitrary",),
        in_specs=[_resident(hn.shape), pl.BlockSpec((None, t, n), lambda j: (j, 0, 0))],
        out_specs=[ospec, ospec], out_shape=[S((ns, k, n), F32), S((ns, k, n), BF16)],
        scratch_shapes=[pltpu.VMEM((k, t), BF16)])


def _dw_rows(a_s, dh, name, hook=None):
    nk, t, kc = a_s.shape
    tm = _tm(t)
    ni = t // tm

    def body(a_ref, d_ref, o_ref, ob_ref):
        i = pl.program_id(0)
        dhb = d_ref[...].astype(BF16)

        @pl.when(i == 0)
        def _():
            o_ref[...] = jnp.zeros_like(o_ref)

        for j in range(nk):
            o_ref[j * kc:(j + 1) * kc, :] += _dot_tn(a_ref[j], dhb)

        @pl.when(i == ni - 1)
        def _():
            ob_ref[...] = o_ref[...].astype(BF16)

    ospec = pl.BlockSpec((nk * kc, D), lambda i: (0, 0))
    return _run(
        body, [a_s, dh], hook, grid=(ni,), name=name, semantics=("arbitrary",),
        in_specs=[pl.BlockSpec((nk, tm, kc), lambda i: (0, i, 0)), pl.BlockSpec((tm, D), lambda i: (i, 0))],
        out_specs=[ospec, ospec], out_shape=[S((nk * kc, D), F32), S((nk * kc, D), BF16)])


def _dx_slot_normbwd(dy_s, wg, h, gain, dh_in, name, hook=None):
    ns, t, n = dy_s.shape
    tm = _tm(t)

    def body(dy_ref, w_ref, h_ref, g_ref, di_ref, o_ref, dg_ref):
        i = pl.program_id(0)

        @pl.when(i == 0)
        def _():
            dg_ref[...] = jnp.zeros_like(dg_ref)

        g = _dot_nt(dy_ref[0], w_ref[0])
        for s in range(1, ns):
            g = g + _dot_nt(dy_ref[s], w_ref[s])
        hv = h_ref[...]
        r = _rstd(hv)
        gg = g * g_ref[...]
        o_ref[...] = di_ref[...] + r * gg - hv * (r * r * r * jnp.mean(gg * hv, axis=-1, keepdims=True))
        dg_ref[...] += jnp.sum(g * hv * r, axis=0, keepdims=True)

    row = pl.BlockSpec((tm, D), lambda i: (i, 0))
    vec = pl.BlockSpec((1, D), lambda i: (0, 0))
    return _run(
        body, [dy_s, wg, h, gain, dh_in], hook, grid=(t // tm,), name=name, semantics=("arbitrary",),
        in_specs=[pl.BlockSpec((ns, tm, n), lambda i: (0, i, 0)), _resident(wg.shape), row, vec, row],
        out_specs=[row, vec], out_shape=[S((t, D), F32), S((1, D), F32)])


def _sgu_gate_bwd(a_s, dg_s, vgain, ws, bst, name, hook=None):
    t = a_s.shape[1]
    sw = a_s.shape[2]
    gps = sw // CHUNK

    def body(a_ref, dg_ref, vg_ref, ws_ref, b_ref, da_ref, dws_ref, dbt_ref, dvg_ref, dvn_ref):
        n = pl.program_id(0)

        @pl.when(n == 0)
        def _():
            dws_ref[...] = jnp.zeros_like(dws_ref)
            dbt_ref[...] = jnp.zeros_like(dbt_ref)
            dvg_ref[...] = jnp.zeros_like(dvg_ref)

        vpre = jnp.concatenate([a_ref[4 + s].astype(F32) for s in range(4)], axis=1)
        v = _gelu(vpre)
        r = _rstd(v)
        vhat = v * r
        vn = (vhat * vg_ref[...]).astype(BF16)
        tri = _tril_mask()
        lane = lax.broadcasted_iota(jnp.int32, (CHUNK, CHUNK), 1)
        dbt = jnp.zeros((CHUNK, CHUNK), F32)
        for g in range(SGU_G):
            w = jnp.where(tri, ws_ref[g], 0.0).astype(BF16)
            vng = vn[:, g * CHUNK:(g + 1) * CHUNK]
            sg = _dot(w, vng) + b_ref[:, g:g + 1]
            lo = (g % gps) * CHUNK
            upre = a_ref[g // gps, :, lo:lo + CHUNK].astype(F32)
            dgate = dg_ref[g // gps, :, lo:lo + CHUNK].astype(F32)
            da_ref[g // gps, :, lo:lo + CHUNK] = (dgate * sg * _gelu_grad(upre)).astype(BF16)
            ds = dgate * _gelu(upre)
            dsb = ds.astype(BF16)
            dvn_ref[:, g * CHUNK:(g + 1) * CHUNK] = _dot_tn(w, dsb)
            dws_ref[g] += jnp.where(tri, _dot_nt(dsb, vng), 0.0)
            dbt = dbt + jnp.where(lane == g, jnp.sum(ds, axis=-1, keepdims=True), 0.0)
        dbt_ref[...] += dbt
        dvn = dvn_ref[...]
        dvg_ref[...] += jnp.sum(dvn * vhat, axis=0, keepdims=True)
        gg = dvn * vg_ref[...]
        dv = r * gg - v * (r * r * r * jnp.mean(gg * v, axis=-1, keepdims=True))
        dav = (dv * _gelu_grad(vpre)).astype(BF16)
        for s in range(4):
            da_ref[4 + s] = dav[:, s * sw:(s + 1) * sw]

    return _run(
        body, [a_s, dg_s, vgain, ws, bst], hook, grid=(t // CHUNK,), name=name, semantics=("arbitrary",),
        in_specs=[pl.BlockSpec((8, CHUNK, sw), lambda n: (0, n, 0)), pl.BlockSpec((4, CHUNK, sw), lambda n: (0, n, 0)),
                  pl.BlockSpec((1, SGU_W), lambda n: (0, 0)), pl.BlockSpec((SGU_G, CHUNK, CHUNK), lambda n: (0, 0, 0)),
                  pl.BlockSpec((CHUNK, SGU_G), lambda n: (0, 0))],
        out_specs=[pl.BlockSpec((8, CHUNK, sw), lambda n: (0, n, 0)), pl.BlockSpec((SGU_G, CHUNK, CHUNK), lambda n: (0, 0, 0)),
                   pl.BlockSpec((CHUNK, CHUNK), lambda n: (0, 0)), pl.BlockSpec((1, SGU_W), lambda n: (0, 0))],
        out_shape=[S((8, t, sw), BF16), S((SGU_G, CHUNK, CHUNK), F32), S((CHUNK, CHUNK), F32), S((1, SGU_W), F32)],
        scratch_shapes=[pltpu.VMEM((CHUNK, SGU_W), F32)])


def _attn_bwd(qkv_s, do, qg, kg, sinks, bias, name, hook=None):
    t = qkv_s.shape[1]
    nb = t // CHUNK

    def body(cur_ref, prev_ref, do_ref, qg_ref, kg_ref, sink_ref, bias_ref,
             o_ref, dqg_ref, dkg_ref, dsk_ref, dbias_ref, carry, top):
        n = pl.program_id(0)

        @pl.when(n == 0)
        def _():
            carry[...] = jnp.zeros_like(carry)
            dqg_ref[...] = jnp.zeros_like(dqg_ref)
            dkg_ref[...] = jnp.zeros_like(dkg_ref)
            dsk_ref[...] = jnp.zeros_like(dsk_ref)
            dbias_ref[...] = jnp.zeros_like(dbias_ref)

        @pl.when(n < nb)
        def _():
            valid = _attn_valid(n)
            top[...] = jnp.zeros_like(top)
            new = [[None] * 3 for _ in range(8)]
            lane = lax.broadcasted_iota(jnp.int32, (1, CHUNK), 1)
            dsk = jnp.zeros((1, CHUNK), F32)
            dqg = jnp.zeros((1, HD), F32)
            dkg = jnp.zeros((1, HD), F32)

            def place(col, val):
                new[col // QKV_SLOT][(col % QKV_SLOT) // HD] = val

            for h in range(NKV):
                kcol = D + HD * h
                vcol = D + HD * (NKV + h)
                k = jnp.concatenate([_head(prev_ref, kcol), _head(cur_ref, kcol)], axis=0)
                v = jnp.concatenate([_head(prev_ref, vcol), _head(cur_ref, vcol)], axis=0)
                rk = _rstd(k)
                khat = k * rk
                kn = (khat * kg_ref[...]).astype(BF16)
                vb = v.astype(BF16)
                dkn = jnp.zeros((2 * CHUNK, HD), F32)
                dv = jnp.zeros((2 * CHUNK, HD), F32)
                for g in range(KVG):
                    hq = KVG * h + g
                    q = _head(cur_ref, HD * hq)
                    rq = _rstd(q)
                    qhat = q * rq
                    qn = (qhat * qg_ref[...]).astype(BF16)
                    sink = sink_ref[hq]
                    p, psink = _attn_probs(qn, kn, bias_ref[hq], valid, sink)
                    doh = do_ref[0, :, HD * hq:HD * (hq + 1)]
                    dp = _dot_nt(doh, vb)
                    dsum = jnp.sum(p * dp, axis=-1, keepdims=True)
                    ds = p * (dp - dsum)
                    dsk = dsk + jnp.where(lane == hq, jnp.sum(-psink * dsum), 0.0)
                    dbias_ref[hq] += ds
                    dv = dv + _dot_tn(p.astype(BF16), doh)
                    dsc = (ds * (HD ** -0.5)).astype(BF16)
                    dqn = _dot(dsc, kn)
                    dkn = dkn + _dot_tn(dsc, qn)
                    dqg = dqg + jnp.sum(dqn * qhat, axis=0, keepdims=True)
                    gq = dqn * qg_ref[...]
                    place(HD * hq, rq * gq - q * (rq * rq * rq * jnp.mean(gq * q, axis=-1, keepdims=True)))
                dkg = dkg + jnp.sum(dkn * khat, axis=0, keepdims=True)
                gk = dkn * kg_ref[...]
                dk = rk * gk - k * (rk * rk * rk * jnp.mean(gk * k, axis=-1, keepdims=True))
                place(kcol, dk[CHUNK:])
                place(vcol, dv[CHUNK:])
                top[kcol // QKV_SLOT, :, kcol % QKV_SLOT:kcol % QKV_SLOT + HD] = dk[:CHUNK]
                top[vcol // QKV_SLOT, :, vcol % QKV_SLOT:vcol % QKV_SLOT + HD] = dv[:CHUNK]
            dqg_ref[...] += dqg
            dkg_ref[...] += dkg
            dsk_ref[...] += dsk
            o_ref[...] = (carry[...] + top[...]).astype(BF16)
            for s in range(8):
                carry[s] = jnp.concatenate(new[s], axis=1)

        @pl.when(n == nb)
        def _():
            o_ref[...] = carry[...].astype(BF16)

    blk = lambda f: pl.BlockSpec((8, CHUNK, QKV_SLOT), f)
    cur = lambda n: (0, jnp.minimum(n, nb - 1), 0)
    prev = lambda n: (0, jnp.clip(n - 1, 0, nb - 1), 0)
    small = lambda w: pl.BlockSpec((1, w), lambda n: (0, 0))
    return _run(
        body, [qkv_s, qkv_s, do, qg, kg, sinks, bias], hook, grid=(nb + 1,), name=name, semantics=("arbitrary",),
        in_specs=[blk(cur), blk(prev), pl.BlockSpec((1, CHUNK, D), cur), small(HD), small(HD),
                  pl.BlockSpec(memory_space=pltpu.SMEM), pl.BlockSpec((NH, CHUNK, 2 * CHUNK), lambda n: (0, 0, 0))],
        out_specs=[blk(lambda n: (0, jnp.maximum(n - 1, 0), 0)), small(HD), small(HD), small(CHUNK),
                   pl.BlockSpec((NH, CHUNK, 2 * CHUNK), lambda n: (0, 0, 0))],
        out_shape=[S((8, t, QKV_SLOT), BF16), S((1, HD), F32), S((1, HD), F32), S((1, CHUNK), F32),
                   S((NH, CHUNK, 2 * CHUNK), F32)],
        scratch_shapes=[pltpu.VMEM((8, CHUNK, QKV_SLOT), F32), pltpu.VMEM((8, CHUNK, QKV_SLOT), F32)])


class _Plain:
    def __init__(self, wg):
        self.full, self.grads = wg, {}

    def w(self, n):
        return self.full[n]

    def hook(self, host):
        return None

    def grad(self, n, pair):
        self.grads[n] = pair


def _local_step(x, target, rep, sch):
    bucket_row = jnp.asarray(_rel_tables().reshape(1, -1))
    bias = _relbias_fwd(rep["rel_bias"].T, bucket_row, "relbias_fwd").reshape(NH, CHUNK, 2 * CHUNK)
    bst = rep["sgu_b_s"][0].T
    ws = rep["sgu_w_s"][0]
    vgain = rep["sgu_v_gain"]
    qg, kg, sinks = rep["attn_q_gain"], rep["attn_k_gain"], rep["attn_sinks"][0]
    w_down = lambda l: sch.w("ffn_w_down%d" % l).reshape(D_FF, D)
    w_up = lambda l: sch.w("ffn_w_up%d" % l)
    cw = [sch.w("ffn_conv_w")[:, 3 * l:3 * l + 3] for l in range(2)]
    cb = [rep["ffn_conv_b"][l].reshape(8, 1, -1) for l in range(2)]
    mixg = [rep["mix_norm"][l:l + 1] for l in range(2)]
    ffng = [rep["ffn_norm"][l:l + 1] for l in range(2)]
    rows = lambda pair: tuple(g.reshape(N_DEV, -1, D) for g in pair)
    hk = sch.hook

    hn0 = _rmsnorm(x, mixg[0], "norm0")
    a0 = _mm_slot(hn0, sch.w("sgu_w_in"), BF16, "sgu_in", hk("sgu_in"))
    gated = _sgu_gate_fwd(a0, vgain, ws, bst, "sgu_gate", hk("sgu_gate"))
    h1, hn1 = _resid_mm(gated, sch.w("sgu_w_out").reshape(SGU_W, D), x, ffng[0], "norm", "sgu_out", hk("sgu_out"))
    ag0, av0, h2, hn2 = _ffn_fwd(hn1, h1, w_up(0), w_down(0), cw[0], cb[0], mixg[1], "norm", "ffn0_fwd", hk("ffn0_fwd"))
    qkv = _mm_slot(hn2, sch.w("attn_w_qkv"), F32, "qkv", hk("qkv"))
    o = _attn_fwd(qkv, qg, kg, sinks, bias, "attn", hk("attn"))
    h3, hn3 = _resid_mm(o, sch.w("attn_w_o").reshape(D, D), h2, ffng[1], "norm", "attn_out", hk("attn_out"))
    ag1, av1, dy, sq = _ffn_fwd(hn3, h3, w_up(1), w_down(1), cw[1], cb[1], target, "loss", "ffn1_fwd_loss", hk("ffn1_fwd_loss"))
    loss = (0.5 / D) * jnp.sum(sq[:, 0, 0])

    def ffn_bwd(dh, h_in, hn, a_g, a_v, l, tag):
        dc, g_down, g_down_b, g_cw, g_cb = _ffn_bwd1(dh, a_g, a_v, w_down(l), cw[l], cb[l], tag + "_bwd1", hk(tag + "_bwd1"))
        sch.grad("ffn_w_down%d" % l, rows((g_down, g_down_b)))
        da, dh_new, dgain = _ffn_bwd2(dc, w_up(l), cw[l], h_in, ffng[l], dh, tag + "_bwd2", hk(tag + "_bwd2"))
        sch.grad("ffn_w_up%d" % l, _dw_slot(hn, da, tag + "_dw_up", hk(tag + "_dw_up")))
        return dh_new, dgain, g_cw, g_cb.reshape(-1)

    dh, d_ffng1, g_cw1, g_cb1 = ffn_bwd(dy, h3, hn3, ag1, av1, 1, "ffn1")
    do = _dx_rows(dh, sch.w("attn_w_o").reshape(D, D), D, BF16, "attn_do", hk("attn_do"))
    sch.grad("attn_w_o", rows(_dw_rows(o, dh, "dw_o", hk("dw_o"))))
    dqkv, d_qg, d_kg, d_sk, d_bias = _attn_bwd(qkv, do, qg, kg, sinks, bias, "attn_bwd", hk("attn_bwd"))
    sch.grad("attn_w_qkv", _dw_slot(hn2, dqkv, "dw_qkv", hk("dw_qkv")))
    dh, d_mixg1 = _dx_slot_normbwd(dqkv, sch.w("attn_w_qkv"), h2, mixg[1], dh, "dx_qkv", hk("dx_qkv"))
    d_relb = _relbias_bwd(d_bias.reshape(NH, -1), bucket_row, "relbias_bwd").T
    dh, d_ffng0, g_cw0, g_cb0 = ffn_bwd(dh, h1, hn1, ag0, av0, 0, "ffn0")
    g_cw = jnp.concatenate([g_cw0, g_cw1], axis=1)
    sch.grad("ffn_conv_w", (g_cw, g_cw.astype(BF16)))
    dgated = _dx_rows(dh, sch.w("sgu_w_out").reshape(SGU_W, D), SGU_W // 4, BF16, "sgu_dgated", hk("sgu_dgated"))
    sch.grad("sgu_w_out", rows(_dw_rows(gated, dh, "dw_sgu_out", hk("dw_sgu_out"))))
    da0, d_ws, d_bst, d_vgain = _sgu_gate_bwd(a0, dgated, vgain, ws, bst, "sgu_gate_bwd", hk("sgu_gate_bwd"))
    sch.grad("sgu_w_in", _dw_slot(hn0, da0, "dw_sgu_in", hk("dw_sgu_in")))
    grad_x, d_mixg0 = _dx_slot_normbwd(da0, sch.w("sgu_w_in"), x, mixg[0], dh, "dx_sgu_in", hk("dx_sgu_in"))

    g_rep = {
        "mix_norm": jnp.concatenate([d_mixg0, d_mixg1], axis=0),
        "ffn_norm": jnp.concatenate([d_ffng0, d_ffng1], axis=0),
        "sgu_v_gain": d_vgain,
        "sgu_w_s": d_ws[None],
        "sgu_b_s": d_bst[:, :SGU_G].T[None],
        "attn_q_gain": d_qg,
        "attn_k_gain": d_kg,
        "attn_sinks": d_sk[:, :NH],
        "rel_bias": d_relb,
        "ffn_conv_b": jnp.stack([g_cb0, g_cb1], axis=0),
    }
    return loss, grad_x, g_rep


def _allgather(xs, name):
    nt = len(xs)

    def body(*refs):
        x_refs, o_refs = refs[:nt], refs[nt:2 * nt]
        send_sems, recv_sems, local_sems = refs[2 * nt:]
        x, y, c, chips = _place()
        me, sibling = (x, y, c), (x, y, 1 - c)

        def copy(t, k, block, to, src=None):
            px, py, pc = block
            dst = o_refs[t].at[4 * px + 2 * py + pc]
            return pltpu.make_async_remote_copy(
                src_ref=dst if src is None else src, dst_ref=dst, send_sem=send_sems.at[t, k], recv_sem=recv_sems.at[t, k],
                device_id=to, device_id_type=MESH)

        mine = [pltpu.make_async_copy(x_refs[t], o_refs[t].at[4 * x + 2 * y + c], local_sems.at[t]) for t in range(nt)]
        for cp in mine:
            cp.start()
        first = []
        for t in range(nt):
            first.append(copy(t, 0, me, sibling, src=x_refs[t]))
            first += [copy(t, 1 + j, me, (*chip, c), src=x_refs[t]) for j, chip in enumerate(chips)]
        for cp in first:
            cp.start()
        passed = []
        for j, chip in enumerate(chips):
            for t in range(nt):
                copy(t, 1 + j, (*chip, c), me).wait_recv()
                fwd = copy(t, 4 + j, (*chip, c), sibling)
                fwd.start()
                passed.append(fwd)
        for t in range(nt):
            copy(t, 0, sibling, me).wait_recv()
            for j, chip in enumerate(chips):
                copy(t, 4 + j, (*chip, 1 - c), me).wait_recv()
        for cp in first + passed:
            cp.wait_send()
        for cp in mine:
            cp.wait()

    return pl.pallas_call(
        body, name=name, in_specs=[ANY] * nt, out_specs=[ANY] * nt,
        out_shape=[S((N_DEV,) + a.shape, a.dtype) for a in xs],
        scratch_shapes=[pltpu.SemaphoreType.DMA((nt, 7)), pltpu.SemaphoreType.DMA((nt, 7)), pltpu.SemaphoreType.DMA((nt,))],
        compiler_params=pltpu.CompilerParams(has_side_effects=True))(*xs)


def _rs_sibling(gs, name):
    nt = len(gs)

    def body(*refs):
        g_refs, o_refs = refs[:nt], refs[nt:2 * nt]
        send_sems, recv_sems = refs[2 * nt:]
        x, y, c, _ = _place()
        copies = []
        for t in range(nt):
            for k in range(4):
                copies.append(pltpu.make_async_remote_copy(
                    src_ref=g_refs[t].at[2 * k + (1 - c)], dst_ref=o_refs[t].at[k],
                    send_sem=send_sems.at[t, k], recv_sem=recv_sems.at[t, k], device_id=(x, y, 1 - c), device_id_type=MESH))
        for cp in copies:
            cp.start()
        for cp in copies:
            cp.wait_recv()
        for cp in copies:
            cp.wait_send()

    return pl.pallas_call(
        body, name=name, in_specs=[ANY] * nt, out_specs=[ANY] * nt,
        out_shape=[S((4,) + g.shape[1:], g.dtype) for g in gs],
        scratch_shapes=[pltpu.SemaphoreType.DMA((nt, 4)), pltpu.SemaphoreType.DMA((nt, 4))],
        compiler_params=pltpu.CompilerParams(has_side_effects=True))(*gs)


def _rs_chips(ps, name):
    nt = len(ps)

    def body(*refs):
        p_refs, o_refs = refs[:nt], refs[nt:2 * nt]
        send_sems, recv_sems = refs[2 * nt:]
        x, y, c, chips = _place()
        copies = []
        for t in range(nt):
            for j, (px, py) in enumerate(chips):
                copies.append(pltpu.make_async_remote_copy(
                    src_ref=p_refs[t].at[2 * px + py], dst_ref=o_refs[t].at[j],
                    send_sem=send_sems.at[t, j], recv_sem=recv_sems.at[t, j], device_id=(px, py, c), device_id_type=MESH))
        for cp in copies:
            cp.start()
        for cp in copies:
            cp.wait_recv()
        for cp in copies:
            cp.wait_send()

    return pl.pallas_call(
        body, name=name, in_specs=[ANY] * nt, out_specs=[ANY] * nt,
        out_shape=[S((3,) + p.shape[1:], p.dtype) for p in ps],
        scratch_shapes=[pltpu.SemaphoreType.DMA((nt, 3)), pltpu.SemaphoreType.DMA((nt, 3))],
        compiler_params=pltpu.CompilerParams(has_side_effects=True))(*ps)


def _row_tile(r):
    tr = r if r <= ROW_TILE or r % ROW_TILE else ROW_TILE
    assert r % tr == 0
    return tr


def _rs_partial(g32, sib, place, name):
    _, r, cdim = g32.shape
    tr = _row_tile(r)

    def body(place_ref, g_ref, s_ref, p_ref, own_ref):
        k = pl.program_id(1)
        tot = g_ref[...] + s_ref[...].astype(F32)
        p_ref[...] = tot.astype(BF16)

        @pl.when(k == place_ref[1])
        def _():
            own_ref[...] = tot

    grid_spec = pltpu.PrefetchScalarGridSpec(
        num_scalar_prefetch=1, grid=(r // tr, 4),
        in_specs=[pl.BlockSpec((None, None, tr, cdim), lambda i, k, pr: (k, pr[0], i, 0)),
                  pl.BlockSpec((None, tr, cdim), lambda i, k, pr: (k, i, 0))],
        out_specs=[pl.BlockSpec((None, tr, cdim), lambda i, k, pr: (k, i, 0)), pl.BlockSpec((tr, cdim), lambda i, k, pr: (i, 0))])
    return pl.pallas_call(
        body, grid_spec=grid_spec, name=name,
        out_shape=[S((4, r, cdim), BF16), S((r, cdim), F32)],
        compiler_params=_cp("parallel", "arbitrary"))(place, g32.reshape(4, 2, r, cdim), sib)


def _adamw_math(w, g, m, v):
    m = ADAM_B1 * m + (1.0 - ADAM_B1) * g
    v = ADAM_B2 * v + (1.0 - ADAM_B2) * (g * g)
    m_hat = m / (1.0 - ADAM_B1 ** ADAM_STEP)
    v_hat = v / (1.0 - ADAM_B2 ** ADAM_STEP)
    delta = -ADAM_LR * (m_hat / (jnp.sqrt(v_hat) + ADAM_EPS) + ADAM_WD * w)
    return delta, m, v


def _adamw_shard(own, recv, w, m, v, name):
    r, cdim = own.shape
    tr = _row_tile(r)

    def body(own_ref, recv_ref, w_ref, m_ref, v_ref, g_out, d_out, m_out, v_out):
        g = own_ref[...] + recv_ref[0].astype(F32) + recv_ref[1].astype(F32) + recv_ref[2].astype(F32)
        g_out[...] = g
        d_out[...], m_out[...], v_out[...] = _adamw_math(w_ref[...], g, m_ref[...], v_ref[...])

    row = pl.BlockSpec((tr, cdim), lambda i: (i, 0))
    return pl.pallas_call(
        body, grid=(r // tr,), name=name,
        in_specs=[row, pl.BlockSpec((3, tr, cdim), lambda i: (0, i, 0)), row, row, row],
        out_specs=[row] * 4, out_shape=[S((r, cdim), F32)] * 4, compiler_params=_cp("parallel"))(own, recv, w, m, v)


def _adamw_replicated(gall, w, m, v, name):
    _, r, cdim = gall.shape
    tr = r // 3 if r % 24 == 0 else r

    def body(g_ref, w_ref, m_ref, v_ref, g_out, d_out, m_out, v_out):
        g = g_ref[0]
        for s in range(1, N_DEV):
            g = g + g_ref[s]
        g_out[...] = g
        d_out[...], m_out[...], v_out[...] = _adamw_math(w_ref[...], g, m_ref[...], v_ref[...])

    row = pl.BlockSpec((tr, cdim), lambda i: (i, 0))
    return pl.pallas_call(
        body, grid=(r // tr,), name=name,
        in_specs=[pl.BlockSpec((N_DEV, tr, cdim), lambda i: (0, i, 0)), row, row, row],
        out_specs=[row] * 4, out_shape=[S((r, cdim), F32)] * 4, compiler_params=_cp("parallel"))(gall, w, m, v)


REPLICATED = ["mix_norm", "ffn_norm", "sgu_v_gain", "sgu_w_s", "sgu_b_s", "attn_q_gain", "attn_k_gain", "attn_sinks", "rel_bias",
              "ffn_conv_b"]
WEIGHTS = ["mix_norm", "ffn_norm", "sgu_w_in", "sgu_v_gain", "sgu_w_s", "sgu_b_s", "sgu_w_out", "attn_w_qkv", "attn_q_gain",
           "attn_k_gain", "attn_sinks", "attn_w_o", "rel_bias", "ffn_w_up", "ffn_conv_w", "ffn_conv_b", "ffn_w_down"]
PACK_UNIT = 1024

GATHER_FIRST = ["sgu_w_in", "sgu_w_out", "ffn_conv_w"]
SCATTER_LAST = ["sgu_w_in", "ffn_conv_w"]
PLAN = {
    "sgu_in": [("ag1", "ffn_w_up0")],
    "sgu_gate": [("ag2", "ffn_w_up0"), ("ag1", "ffn_w_down0")],
    "sgu_out": [("ag2", "ffn_w_down0"), ("ag1", "attn_w_qkv")],
    "ffn0_fwd": [("ag2", "attn_w_qkv"), ("ag1", "attn_w_o"), ("ag1", "ffn_w_up1")],
    "qkv": [("ag2", "attn_w_o"), ("ag2", "ffn_w_up1")],
    "attn": [("ag1", "ffn_w_down1")],
    "attn_out": [("ag2", "ffn_w_down1")],
    "ffn1_bwd2": [("rs1", "ffn_w_down1")],
    "ffn1_dw_up": [("rs2", "ffn_w_down1")],
    "attn_do": [("rs1", "ffn_w_up1")],
    "attn_bwd": [("rs2", "ffn_w_up1"), ("rs1", "attn_w_o")],
    "dw_qkv": [("rs2", "attn_w_o")],
    "dx_qkv": [("rs1", "attn_w_qkv")],
    "ffn0_bwd1": [("rs2", "attn_w_qkv")],
    "ffn0_bwd2": [("rs1", "ffn_w_down0")],
    "ffn0_dw_up": [("rs2", "ffn_w_down0")],
    "sgu_dgated": [("rs1", "ffn_w_up0")],
    "sgu_gate_bwd": [("rs2", "ffn_w_up0"), ("rs1", "sgu_w_out")],
    "dw_sgu_in": [("rs2", "sgu_w_out")],
    "dx_sgu_in": [("rs1", "sgu_w_in"), ("rs1", "ffn_conv_w")],
}


class _Overlap:
    def __init__(self, shard, place):
        self.shard, self.place = shard, place
        self.part, self.full = {}, {}
        self.grads, self.sib, self.own, self.recv = {}, {}, {}, {}

    def w(self, n):
        return self.full[n]

    def grad(self, n, pair):
        self.grads[n] = pair

    def chip_sums(self, n):
        sums, self.own[n] = _rs_partial(self.grads[n][0], self.sib.pop(n), self.place, "rs_partial_" + n)
        return sums

    def hook(self, host):
        ops = PLAN.get(host)
        if not ops:
            return None
        where = {"ag1": self.part, "ag2": self.full, "rs1": self.sib, "rs2": self.recv}
        idx = []

        def hook(results=None):
            if results is not None:
                for (kind, n), i in zip(ops, idx):
                    where[kind][n] = results[i]
                return None
            comm = _Comm()
            for kind, n in ops:
                arr = {"ag1": lambda: self.shard[n], "ag2": lambda: self.part.pop(n), "rs1": lambda: self.grads[n][1],
                       "rs2": lambda: self.chip_sums(n)}[kind]()
                idx.append(comm.add(kind, arr))
            return comm

        return hook


def _shard_views(w):
    return {
        "sgu_w_in": w["sgu_w_in"][0], "sgu_w_out": w["sgu_w_out"][0], "attn_w_qkv": w["attn_w_qkv"][0], "attn_w_o": w["attn_w_o"][0],
        "ffn_w_up0": w["ffn_w_up"][0], "ffn_w_up1": w["ffn_w_up"][1], "ffn_w_down0": w["ffn_w_down"][0], "ffn_w_down1": w["ffn_w_down"][1],
        "ffn_conv_w": w["ffn_conv_w"].reshape(6, -1),
    }


def _unshard_views(d):
    return {
        "sgu_w_in": d["sgu_w_in"][None], "sgu_w_out": d["sgu_w_out"][None], "attn_w_qkv": d["attn_w_qkv"][None], "attn_w_o": d["attn_w_o"][None],
        "ffn_w_up": jnp.stack([d["ffn_w_up0"], d["ffn_w_up1"]]), "ffn_w_down": jnp.stack([d["ffn_w_down0"], d["ffn_w_down1"]]),
        "ffn_conv_w": d["ffn_conv_w"].reshape(2, 3, -1),
    }


def _pack(d):
    parts = []
    for n in REPLICATED:
        flat = d[n].reshape(-1)
        parts.append(jnp.pad(flat, (0, -flat.shape[0] % PACK_UNIT)))
    return jnp.concatenate(parts).reshape(-1, 128)


def _unpack(p, like):
    flat = p.reshape(-1)
    out, off = {}, 0
    for n in REPLICATED:
        size = math.prod(like[n].shape)
        out[n] = flat[off:off + size].reshape(like[n].shape)
        off += size + (-size % PACK_UNIT)
    return out


def kernel(x, mix_norm, ffn_norm, sgu_w_in, sgu_v_gain, sgu_w_s, sgu_b_s, sgu_w_out, attn_w_qkv, attn_q_gain, attn_k_gain, attn_sinks, attn_w_o, rel_bias, ffn_w_up, ffn_conv_w, ffn_conv_b, ffn_w_down, loss_target, m_mix_norm, m_ffn_norm, m_sgu_w_in, m_sgu_v_gain, m_sgu_w_s, m_sgu_b_s, m_sgu_w_out, m_attn_w_qkv, m_attn_q_gain, m_attn_k_gain, m_attn_sinks, m_attn_w_o, m_rel_bias, m_ffn_w_up, m_ffn_conv_w, m_ffn_conv_b, m_ffn_w_down, v_mix_norm, v_ffn_norm, v_sgu_w_in, v_sgu_v_gain, v_sgu_w_s, v_sgu_b_s, v_sgu_w_out, v_attn_w_qkv, v_attn_q_gain, v_attn_k_gain, v_attn_sinks, v_attn_w_o, v_rel_bias, v_ffn_w_up, v_ffn_conv_w, v_ffn_conv_b, v_ffn_w_down):
    w = dict(zip(WEIGHTS, (mix_norm, ffn_norm, sgu_w_in, sgu_v_gain, sgu_w_s, sgu_b_s, sgu_w_out, attn_w_qkv, attn_q_gain, attn_k_gain,
                           attn_sinks, attn_w_o, rel_bias, ffn_w_up, ffn_conv_w, ffn_conv_b, ffn_w_down)))
    m = dict(zip(WEIGHTS, (m_mix_norm, m_ffn_norm, m_sgu_w_in, m_sgu_v_gain, m_sgu_w_s, m_sgu_b_s, m_sgu_w_out, m_attn_w_qkv, m_attn_q_gain,
                           m_attn_k_gain, m_attn_sinks, m_attn_w_o, m_rel_bias, m_ffn_w_up, m_ffn_conv_w, m_ffn_conv_b, m_ffn_w_down)))
    v = dict(zip(WEIGHTS, (v_mix_norm, v_ffn_norm, v_sgu_w_in, v_sgu_v_gain, v_sgu_w_s, v_sgu_b_s, v_sgu_w_out, v_attn_w_qkv, v_attn_q_gain,
                           v_attn_k_gain, v_attn_sinks, v_attn_w_o, v_rel_bias, v_ffn_w_up, v_ffn_conv_w, v_ffn_conv_b, v_ffn_w_down)))
    rep = {n: w[n] for n in REPLICATED}

    ws, ms, vs = _shard_views(w), _shard_views(m), _shard_views(v)
    names = list(ws)
    xi, yi, ci = lax.axis_index("x"), lax.axis_index("y"), lax.axis_index("c")
    place = jnp.stack([ci, 2 * xi + yi]).astype(jnp.int32)
    sch = _Overlap({n: ws[n] if n == "ffn_conv_w" else ws[n].astype(BF16) for n in names}, place)
    sch.full.update(zip(GATHER_FIRST, _allgather([sch.shard[n] for n in GATHER_FIRST], "gather_first")))

    loss, grad_x, g_rep = _local_step(x[0], loss_target[0], rep, sch)
    loss = lax.psum(loss, ("x", "y", "c"))

    sch.recv.update(zip(SCATTER_LAST, _rs_chips([sch.chip_sums(n) for n in SCATTER_LAST], "rs_chips_last")))
    res = [_adamw_shard(sch.own[n], sch.recv[n], ws[n], ms[n], vs[n], "adamw_" + n) for n in names]
    out = [_unshard_views({n: r[i] for n, r in zip(names, res)}) for i in range(4)]

    gall = _allgather([_pack(g_rep)], "gather_small_grads")[0]
    small = _adamw_replicated(gall, _pack(rep), _pack({n: m[n] for n in REPLICATED}), _pack({n: v[n] for n in REPLICATED}), "adamw_small")
    for i in range(4):
        out[i].update(_unpack(small[i], rep))

    return (loss, grad_x[None], *[out[0][n] for n in WEIGHTS], *[out[1][n] for n in WEIGHTS],
            *[out[2][n] for n in WEIGHTS], *[out[3][n] for n in WEIGHTS])
```

```python
import functools
import math

import numpy as np
import jax
import jax.numpy as jnp
from jax import lax
from jax.experimental import pallas as pl
from jax.experimental.pallas import tpu as pltpu

F32 = jnp.float32
BF16 = jnp.bfloat16
S = jax.ShapeDtypeStruct

D = 1024
CHUNK = 128
SGU_W = 2048
SGU_G = 16
HD = 64
NH = 16
NKV = 4
KVG = 4
D_FF = 2816
REL_BUCKETS = 32
REL_MAX_DIST = 128
EPS = 1e-6
N_DEV = 8
MESH = pl.DeviceIdType.MESH

ADAM_LR = 0.001
ADAM_B1 = 0.9
ADAM_B2 = 0.999
ADAM_EPS = 1e-08
ADAM_WD = 0.01
ADAM_STEP = 10

ROW_TILE = 512
HALO = 8


def _tm(t):
    return min(ROW_TILE, t)


def _cp(*sem):
    return pltpu.CompilerParams(dimension_semantics=sem)


ANY = pl.BlockSpec(memory_space=pl.ANY)


def _place():
    x, y, c = lax.axis_index("x"), lax.axis_index("y"), lax.axis_index("c")
    return x, y, c, [(1 - x, y), (x, 1 - y), (1 - x, 1 - y)]


class _Comm:
    SEMS = {"ag1": 5, "ag2": 3, "rs1": 4, "rs2": 3}

    def __init__(self):
        self.inputs, self.out_shapes, self.aliases, self.ops, self.n_sems = [], [], {}, [], 0

    def add(self, kind, arr):
        lead = {"ag1": N_DEV, "ag2": None, "rs1": 4, "rs2": 3}[kind]
        shape = arr.shape if lead is None else (lead,) + arr.shape[(0 if kind == "ag1" else 1):]
        if kind == "ag2":
            self.aliases[len(self.inputs)] = len(self.out_shapes)
        self.ops.append((kind, len(self.inputs), len(self.out_shapes), self.n_sems))
        self.inputs.append(arr)
        self.out_shapes.append(S(shape, arr.dtype))
        self.n_sems += self.SEMS[kind]
        return len(self.out_shapes) - 1

    def _copies(self, ins, outs, send, recv):
        x, y, c, chips = _place()
        me, sibling = (x, y, c), (x, y, 1 - c)
        slot = lambda px, py, pc: 4 * px + 2 * py + pc
        sends, recvs, local = [], [], []

        def rc(src, dst, k, to):
            return lambda: pltpu.make_async_remote_copy(src_ref=src(), dst_ref=dst(), send_sem=send.at[k], recv_sem=recv.at[k],
                                                        device_id=to, device_id_type=MESH)

        for kind, ii, oi, b in self.ops:
            src, dst = ins[ii], outs[oi]
            at = lambda ref, i: (lambda: ref.at[i])
            if kind == "ag1":
                whole, mine = (lambda s=src: s), at(dst, slot(*me))
                sends.append(rc(whole, mine, b, sibling))
                recvs.append(rc(whole, at(dst, slot(x, y, 1 - c)), b, me))
                for j, chip in enumerate(chips):
                    sends.append(rc(whole, mine, b + 1 + j, (*chip, c)))
                    recvs.append(rc(whole, at(dst, slot(*chip, c)), b + 1 + j, me))
                local.append(lambda s=src, m=mine, k=b + 4: pltpu.make_async_copy(s, m(), send.at[k]))
            elif kind == "ag2":
                for j, chip in enumerate(chips):
                    sends.append(rc(at(dst, slot(*chip, c)), at(dst, slot(*chip, c)), b + j, sibling))
                    recvs.append(rc(at(dst, slot(*chip, 1 - c)), at(dst, slot(*chip, 1 - c)), b + j, me))
            elif kind == "rs1":
                for k in range(4):
                    sends.append(rc(at(src, 2 * k + (1 - c)), at(dst, k), b + k, sibling))
                    recvs.append(rc(at(src, 2 * k + c), at(dst, k), b + k, me))
            else:
                for j, (px, py) in enumerate(chips):
                    sends.append(rc(at(src, 2 * px + py), at(dst, j), b + j, (px, py, c)))
                    recvs.append(rc(at(src, 2 * px + py), at(dst, j), b + j, me))
        return sends, recvs, local

    def start(self, ins, outs, send, recv):
        sends, _, local = self._copies(ins, outs, send, recv)
        for make in local + sends:
            make().start()

    def finish(self, ins, outs, send, recv):
        sends, recvs, local = self._copies(ins, outs, send, recv)
        for make in recvs:
            make().wait_recv()
        for make in sends:
            make().wait_send()
        for make in local:
            make().wait()


def _run(body, args, hook, *, grid, in_specs, out_specs, out_shape, name, semantics, scratch_shapes=()):
    comm = hook() if hook is not None else None
    if comm is None:
        return pl.pallas_call(body, grid=grid, in_specs=in_specs, out_specs=out_specs, out_shape=out_shape, name=name,
                              scratch_shapes=list(scratch_shapes), compiler_params=_cp(*semantics))(*args)
    single = not isinstance(out_shape, (list, tuple))
    out_shapes = [out_shape] if single else list(out_shape)
    out_specs_l = [out_specs] if single else list(out_specs)
    n_in, n_out, n_scr, ci, co = len(args), len(out_shapes), len(scratch_shapes), len(comm.inputs), len(comm.out_shapes)

    def wrapped(*refs):
        ins, cins = refs[:n_in], refs[n_in:n_in + ci]
        outs, couts = refs[n_in + ci:n_in + ci + n_out], refs[n_in + ci + n_out:n_in + ci + n_out + co]
        scr = refs[n_in + ci + n_out + co:n_in + ci + n_out + co + n_scr]
        send, recv = refs[-2:]
        first = functools.reduce(lambda a, b: a & b, [pl.program_id(a) == 0 for a in range(len(grid))])
        last = functools.reduce(lambda a, b: a & b, [pl.program_id(a) == g - 1 for a, g in enumerate(grid)])

        @pl.when(first)
        def _():
            comm.start(cins, couts, send, recv)

        body(*ins, *outs, *scr)

        @pl.when(last)
        def _():
            comm.finish(cins, couts, send, recv)

    res = pl.pallas_call(
        wrapped, grid=grid, in_specs=list(in_specs) + [ANY] * ci, out_specs=out_specs_l + [ANY] * co,
        out_shape=out_shapes + comm.out_shapes, name=name,
        scratch_shapes=list(scratch_shapes) + [pltpu.SemaphoreType.DMA((comm.n_sems,)), pltpu.SemaphoreType.DMA((comm.n_sems,))],
        input_output_aliases={n_in + k: n_out + v for k, v in comm.aliases.items()},
        compiler_params=pltpu.CompilerParams(dimension_semantics=("arbitrary",) * len(grid), has_side_effects=True))(*args, *comm.inputs)
    hook(res[n_out:])
    return res[0] if single else list(res[:n_out])


def _dot(a, b):
    return jnp.dot(a, b, preferred_element_type=F32)


def _dot_nt(a, b):
    return lax.dot_general(a, b, (((1,), (1,)), ((), ())), preferred_element_type=F32)


def _dot_tn(a, b):
    return lax.dot_general(a, b, (((0,), (0,)), ((), ())), preferred_element_type=F32)


def _gelu(x):
    return 0.5 * x * (1.0 + lax.erf(x * (2.0 ** -0.5)))


def _gelu_grad(x):
    return 0.5 * (1.0 + lax.erf(x * (2.0 ** -0.5))) + x * jnp.exp(-0.5 * x * x) * (1.0 / math.sqrt(2.0 * math.pi))


def _sigmoid(x):
    return 1.0 / (1.0 + jnp.exp(-x))


def _rstd(x):
    return lax.rsqrt(jnp.mean(x * x, axis=-1, keepdims=True) + EPS)


def _rel_tables():
    q = np.arange(CHUNK)[:, None] + CHUNK
    k = np.arange(2 * CHUNK)[None, :]
    dist = q - k
    n = np.maximum(dist, 0)
    max_exact = REL_BUCKETS // 2
    large = max_exact + (np.log(np.maximum(n, 1).astype(np.float32) / max_exact)
                         / math.log(REL_MAX_DIST / max_exact) * (REL_BUCKETS - max_exact)).astype(np.int32)
    large = np.minimum(large, REL_BUCKETS - 1)
    return np.where(n < max_exact, n, large).astype(np.int32)


def _rmsnorm(x, gain, name):
    t = x.shape[0]
    tm = _tm(t)

    def body(x_ref, g_ref, o_ref):
        xv = x_ref[...]
        o_ref[...] = (xv * _rstd(xv) * g_ref[...]).astype(BF16)

    return pl.pallas_call(
        body, grid=(t // tm,), name=name,
        in_specs=[pl.BlockSpec((tm, D), lambda i: (i, 0)), pl.BlockSpec((1, D), lambda i: (0, 0))],
        out_specs=pl.BlockSpec((tm, D), lambda i: (i, 0)),
        out_shape=S((t, D), BF16), compiler_params=_cp("parallel"))(x, gain)


def _resident(shape):
    zeros = (0,) * len(shape)
    return pl.BlockSpec(shape, lambda *_: zeros, pipeline_mode=pl.Buffered(1))


def _mm_slot(hn, wg, out_dtype, name, hook=None):
    t, k = hn.shape
    ns, _, n = wg.shape
    tm = _tm(t)

    def body(a_ref, w_ref, o_ref):
        a = a_ref[...]
        for s in range(ns):
            o_ref[s] = _dot(a, w_ref[s]).astype(out_dtype)

    return _run(
        body, [hn, wg], hook, grid=(t // tm,), name=name, semantics=("parallel",),
        in_specs=[pl.BlockSpec((tm, k), lambda i: (i, 0)), _resident(wg.shape)],
        out_specs=pl.BlockSpec((ns, tm, n), lambda i: (0, i, 0)), out_shape=S((ns, t, n), out_dtype))


def _conv3(a, prev, cw, cb, tm):
    ext = jnp.concatenate([prev, a], axis=0)
    a1 = ext[HALO - 1:HALO - 1 + tm]
    a2 = ext[HALO - 2:HALO - 2 + tm]
    return cw[2:3] * a + cw[1:2] * a1 + cw[0:1] * a2 + cb, a1, a2


def _ffn_fwd(hn, h, wup, wdown, cw, cb, extra, mode, name, hook=None):
    t, k = hn.shape
    n = wup.shape[-1]
    nh = wup.shape[0] // 2
    tm = _tm(t)
    ni = t // tm

    def body(a_ref, h_ref, wu_ref, wd_ref, cw_ref, cb_ref, e_ref, ag_ref, av_ref, o1_ref, o2_ref, carry):
        i = pl.program_id(0)

        @pl.when(i == 0)
        def _():
            carry[...] = jnp.zeros_like(carry)

        a = a_ref[...]
        acc = h_ref[...]
        for j in range(nh):
            ag = _dot(a, wu_ref[j])
            av = _dot(a, wu_ref[nh + j])
            ag_ref[j] = ag.astype(BF16)
            av_ref[j] = av.astype(BF16)
            cg, _, _ = _conv3(ag, carry[j], cw_ref[j], cb_ref[j], tm)
            cv, _, _ = _conv3(av, carry[nh + j], cw_ref[nh + j], cb_ref[nh + j], tm)
            carry[j] = ag[tm - HALO:]
            carry[nh + j] = av[tm - HALO:]
            act = (cg * _sigmoid(cg) * cv).astype(BF16)
            acc = acc + _dot(act, wd_ref[j * n:(j + 1) * n, :])
        if mode == "norm":
            o1_ref[...] = acc
            o2_ref[...] = (acc * _rstd(acc) * e_ref[...]).astype(BF16)
        else:
            err = acc - e_ref[...]
            o1_ref[...] = err * (1.0 / D)
            o2_ref[...] = jnp.full(o2_ref.shape, jnp.sum(err * err), F32)

    row = pl.BlockSpec((tm, D), lambda i: (i, 0))
    if mode == "norm":
        e_spec, o2_spec, o2_shape = pl.BlockSpec((1, D), lambda i: (0, 0)), row, S((t, D), BF16)
    else:
        e_spec, o2_spec, o2_shape = row, pl.BlockSpec((None, 8, 128), lambda i: (i, 0, 0)), S((ni, 8, 128), F32)
    aspec = pl.BlockSpec((nh, tm, n), lambda i: (0, i, 0))
    return _run(
        body, [hn, h, wup, wdown, cw, cb, extra], hook, grid=(ni,), name=name, semantics=("arbitrary",),
        in_specs=[pl.BlockSpec((tm, k), lambda i: (i, 0)), row, _resident(wup.shape), _resident(wdown.shape),
                  _resident(cw.shape), _resident(cb.shape), e_spec],
        out_specs=[aspec, aspec, row, o2_spec],
        out_shape=[S((nh, t, n), BF16), S((nh, t, n), BF16), S((t, D), F32), o2_shape],
        scratch_shapes=[pltpu.VMEM((2 * nh, HALO, n), F32)])


def _tril_mask():
    r = lax.broadcasted_iota(jnp.int32, (CHUNK, CHUNK), 0)
    c = lax.broadcasted_iota(jnp.int32, (CHUNK, CHUNK), 1)
    return r >= c


def _sgu_gate_fwd(a_s, vgain, ws, bst, name, hook=None):
    t = a_s.shape[1]
    sw = a_s.shape[2]
    gps = sw // CHUNK

    def body(a_ref, vg_ref, ws_ref, b_ref, o_ref):
        v = _gelu(jnp.concatenate([a_ref[4 + s].astype(F32) for s in range(4)], axis=1))
        vn = (v * _rstd(v) * vg_ref[...]).astype(BF16)
        tri = _tril_mask()
        for g in range(SGU_G):
            w = jnp.where(tri, ws_ref[g], 0.0).astype(BF16)
            sg = _dot(w, vn[:, g * CHUNK:(g + 1) * CHUNK]) + b_ref[:, g:g + 1]
            lo = (g % gps) * CHUNK
            u = _gelu(a_ref[g // gps, :, lo:lo + CHUNK].astype(F32))
            o_ref[g // gps, :, lo:lo + CHUNK] = (u * sg).astype(BF16)

    return _run(
        body, [a_s, vgain, ws, bst], hook, grid=(t // CHUNK,), name=name, semantics=("parallel",),
        in_specs=[pl.BlockSpec((8, CHUNK, sw), lambda n: (0, n, 0)), pl.BlockSpec((1, SGU_W), lambda n: (0, 0)),
                  pl.BlockSpec((SGU_G, CHUNK, CHUNK), lambda n: (0, 0, 0)), pl.BlockSpec((CHUNK, SGU_G), lambda n: (0, 0))],
        out_specs=pl.BlockSpec((4, CHUNK, sw), lambda n: (0, n, 0)), out_shape=S((4, t, sw), BF16))


def _resid_mm(a_s, w, resid, extra, mode, name, hook=None):
    nk, t, kc = a_s.shape
    tm = _tm(t)
    ni = t // tm

    def body(a_ref, w_ref, r_ref, e_ref, o1_ref, o2_ref):
        h = r_ref[...]
        for j in range(nk):
            h = h + _dot(a_ref[j], w_ref[j * kc:(j + 1) * kc, :])
        if mode == "norm":
            o1_ref[...] = h
            o2_ref[...] = (h * _rstd(h) * e_ref[...]).astype(BF16)
        else:
            err = h - e_ref[...]
            o1_ref[...] = err * (1.0 / D)
            o2_ref[...] = jnp.full(o2_ref.shape, jnp.sum(err * err), F32)

    row = pl.BlockSpec((tm, D), lambda i: (i, 0))
    if mode == "norm":
        e_spec, o2_spec, o2_shape = pl.BlockSpec((1, D), lambda i: (0, 0)), row, S((t, D), BF16)
    else:
        e_spec, o2_spec, o2_shape = row, pl.BlockSpec((None, 8, 128), lambda i: (i, 0, 0)), S((ni, 8, 128), F32)
    return _run(
        body, [a_s, w, resid, extra], hook, grid=(ni,), name=name, semantics=("parallel",),
        in_specs=[pl.BlockSpec((nk, tm, kc), lambda i: (0, i, 0)), _resident(w.shape), row, e_spec],
        out_specs=[row, o2_spec], out_shape=[S((t, D), F32), o2_shape])


def _relbias_fwd(rel_bias_t, bucket_row, name):
    nb = bucket_row.shape[1]

    def body(rb_ref, bk_ref, o_ref):
        onehot = (lax.broadcasted_iota(jnp.int32, (REL_BUCKETS, nb), 0) == bk_ref[...]).astype(F32)
        o_ref[...] = jnp.dot(rb_ref[...], onehot, precision=lax.Precision.HIGHEST, preferred_element_type=F32)

    return pl.pallas_call(body, out_shape=S((NH, nb), F32), name=name)(rel_bias_t, bucket_row)


def _relbias_bwd(dbias, bucket_row, name):
    nb = bucket_row.shape[1]

    def body(db_ref, bk_ref, o_ref):
        onehot = (lax.broadcasted_iota(jnp.int32, (REL_BUCKETS, nb), 0) == bk_ref[...]).astype(F32)
        o_ref[...] = lax.dot_general(db_ref[...], onehot, (((1,), (1,)), ((), ())),
                                     precision=lax.Precision.HIGHEST, preferred_element_type=F32)

    return pl.pallas_call(body, out_shape=S((NH, REL_BUCKETS), F32), name=name)(dbias, bucket_row)


QKV_SLOT = 192


def _head(ref, col):
    return ref[col // QKV_SLOT, :, col % QKV_SLOT:col % QKV_SLOT + HD]


def _attn_valid(n):
    qi = lax.broadcasted_iota(jnp.int32, (KVG * CHUNK, 2 * CHUNK), 0) & (CHUNK - 1)
    kj = lax.broadcasted_iota(jnp.int32, (KVG * CHUNK, 2 * CHUNK), 1)
    dist = qi + CHUNK - kj
    return (dist >= 0) & (dist < CHUNK) & ((n > 0) | (kj >= CHUNK))


def _group_q(cur_ref, h):
    return jnp.concatenate([_head(cur_ref, HD * (KVG * h + g)) for g in range(KVG)], axis=0)


def _group_sinks(sink_ref, h):
    return jnp.concatenate([jnp.full((CHUNK, 1), sink_ref[KVG * h + g], F32) for g in range(KVG)], axis=0)


def _attn_probs(qn, kn, bias, valid, sink):
    s = _dot_nt(qn, kn) * (HD ** -0.5) + bias
    s = jnp.where(valid, s, -jnp.inf)
    m = jnp.maximum(jnp.max(s, axis=-1, keepdims=True), sink)
    p = jnp.exp(s - m)
    psink = jnp.exp(sink - m)
    inv = 1.0 / (jnp.sum(p, axis=-1, keepdims=True) + psink)
    return p * inv, psink * inv


def _attn_fwd(qkv_s, qg, kg, sinks, bias, name, hook=None):
    t = qkv_s.shape[1]

    def body(cur_ref, prev_ref, qg_ref, kg_ref, sink_ref, bias_ref, o_ref):
        n = pl.program_id(0)
        valid = _attn_valid(n)
        for h in range(NKV):
            k = jnp.concatenate([_head(prev_ref, D + HD * h), _head(cur_ref, D + HD * h)], axis=0)
            v = jnp.concatenate([_head(prev_ref, D + HD * (NKV + h)), _head(cur_ref, D + HD * (NKV + h))], axis=0)
            kn = (k * _rstd(k) * kg_ref[...]).astype(BF16)
            vb = v.astype(BF16)
            q = _group_q(cur_ref, h)
            qn = (q * _rstd(q) * qg_ref[...]).astype(BF16)
            bias = bias_ref[KVG * h:KVG * (h + 1)].reshape(KVG * CHUNK, 2 * CHUNK)
            p, _ = _attn_probs(qn, kn, bias, valid, _group_sinks(sink_ref, h))
            o = _dot(p.astype(BF16), vb)
            o_ref[0, :, KVG * HD * h:KVG * HD * (h + 1)] = jnp.concatenate(
                [o[g * CHUNK:(g + 1) * CHUNK] for g in range(KVG)], axis=1).astype(BF16)

    blk = lambda f: pl.BlockSpec((8, CHUNK, QKV_SLOT), f)
    return _run(
        body, [qkv_s, qkv_s, qg, kg, sinks, bias], hook, grid=(t // CHUNK,), name=name, semantics=("parallel",),
        in_specs=[blk(lambda n: (0, n, 0)), blk(lambda n: (0, jnp.maximum(n - 1, 0), 0)),
                  pl.BlockSpec((1, HD), lambda n: (0, 0)), pl.BlockSpec((1, HD), lambda n: (0, 0)),
                  pl.BlockSpec(memory_space=pltpu.SMEM), pl.BlockSpec((NH, CHUNK, 2 * CHUNK), lambda n: (0, 0, 0))],
        out_specs=pl.BlockSpec((1, CHUNK, D), lambda n: (0, n, 0)), out_shape=S((1, t, D), BF16))


def _dx_rows(dh, w, kc, out_dtype, name, hook=None):
    t = dh.shape[0]
    nk = w.shape[0] // kc
    tm = _tm(t)

    def body(d_ref, w_ref, o_ref):
        dhb = d_ref[...].astype(BF16)
        for j in range(nk):
            o_ref[j] = _dot_nt(dhb, w_ref[j * kc:(j + 1) * kc, :]).astype(out_dtype)

    return _run(
        body, [dh, w], hook, grid=(t // tm,), name=name, semantics=("parallel",),
        in_specs=[pl.BlockSpec((tm, D), lambda i: (i, 0)), _resident(w.shape)],
        out_specs=pl.BlockSpec((nk, tm, kc), lambda i: (0, i, 0)), out_shape=S((nk, t, kc), out_dtype))


BWD1_ROWS = 256


def _ffn_bwd1(dh, a_g, a_v, wdown, cw, cb, name, hook=None):
    nh, t, n = a_g.shape
    tm = min(BWD1_ROWS, t)
    ni = t // tm

    def body(d_ref, ag_ref, av_ref, wd_ref, cw_ref, cb_ref, dc_ref, dw_hbm, dwb_hbm, dcw_ref, dcb_ref, carry, acc, stage):
        i = pl.program_id(0)

        @pl.when(i == 0)
        def _():
            carry[...] = jnp.zeros_like(carry)
            acc[...] = jnp.zeros_like(acc)
            dcw_ref[...] = jnp.zeros_like(dcw_ref)
            dcb_ref[...] = jnp.zeros_like(dcb_ref)

        dhb = d_ref[...].astype(BF16)
        rsum = lambda x: jnp.sum(x, axis=0, keepdims=True)
        for j in range(nh):
            dact = _dot_nt(dhb, wd_ref[j * n:(j + 1) * n, :])
            ag = ag_ref[j].astype(F32)
            av = av_ref[j].astype(F32)
            cg, ag1, ag2 = _conv3(ag, carry[j], cw_ref[j], cb_ref[j], tm)
            cv, av1, av2 = _conv3(av, carry[nh + j], cw_ref[nh + j], cb_ref[nh + j], tm)
            carry[j] = ag[tm - HALO:]
            carry[nh + j] = av[tm - HALO:]
            sg = _sigmoid(cg)
            gs = cg * sg
            acc[j * n:(j + 1) * n, :] += _dot_tn((gs * cv).astype(BF16), dhb)
            dcg = dact * cv * (sg * (1.0 + cg * (1.0 - sg)))
            dcv = dact * gs
            dc_ref[j] = dcg.astype(BF16)
            dc_ref[nh + j] = dcv.astype(BF16)
            dcw_ref[j] += jnp.concatenate([rsum(dcg * ag2), rsum(dcg * ag1), rsum(dcg * ag)], axis=0)
            dcw_ref[nh + j] += jnp.concatenate([rsum(dcv * av2), rsum(dcv * av1), rsum(dcv * av)], axis=0)
            dcb_ref[j] += rsum(dcg)
            dcb_ref[nh + j] += rsum(dcv)

        @pl.when(i == ni - 1)
        def _():
            pltpu.sync_copy(acc, dw_hbm)
            for j in range(nh):
                stage[...] = acc[j * n:(j + 1) * n, :].astype(BF16)
                pltpu.sync_copy(stage, dwb_hbm.at[pl.ds(j * n, n), :])

    aspec = pl.BlockSpec((nh, tm, n), lambda i: (0, i, 0))
    return _run(
        body, [dh, a_g, a_v, wdown, cw, cb], hook, grid=(ni,), name=name, semantics=("arbitrary",),
        in_specs=[pl.BlockSpec((tm, D), lambda i: (i, 0)), aspec, aspec, _resident(wdown.shape), _resident(cw.shape), _resident(cb.shape)],
        out_specs=[pl.BlockSpec((2 * nh, tm, n), lambda i: (0, i, 0)), ANY, ANY,
                   pl.BlockSpec(cw.shape, lambda i: (0, 0, 0)), pl.BlockSpec(cb.shape, lambda i: (0, 0, 0))],
        out_shape=[S((2 * nh, t, n), BF16), S(wdown.shape, F32), S(wdown.shape, BF16), S(cw.shape, F32), S(cb.shape, F32)],
        scratch_shapes=[pltpu.VMEM((2 * nh, HALO, n), F32), pltpu.VMEM(wdown.shape, F32), pltpu.VMEM((n, D), BF16)])


def _ffn_bwd2(dc, wup, cw, h, gain, dh_in, name, hook=None):
    ns, t, n = dc.shape
    tm = _tm(t)
    ni = t // tm

    def body(dc_ref, wu_ref, cw_ref, h_ref, g_ref, di_ref, da_ref, o_ref, dg_ref, carry):
        i = pl.program_id(0)

        @pl.when(i == 0)
        def _():
            carry[...] = jnp.zeros_like(carry)
            dg_ref[...] = jnp.zeros_like(dg_ref)

        acc = jnp.zeros((tm, D), F32)
        for s in range(ns):
            x = dc_ref[s].astype(F32)
            ext = jnp.concatenate([x, carry[s]], axis=0)
            cwv = cw_ref[s]
            da = (cwv[2:3] * x + cwv[1:2] * ext[1:1 + tm] + cwv[0:1] * ext[2:2 + tm]).astype(BF16)
            carry[s] = x[:HALO]
            da_ref[s] = da
            acc = acc + _dot_nt(da, wu_ref[s])
        hv = h_ref[...]
        r = _rstd(hv)
        gg = acc * g_ref[...]
        o_ref[...] = di_ref[...] + r * gg - hv * (r * r * r * jnp.mean(gg * hv, axis=-1, keepdims=True))
        dg_ref[...] += jnp.sum(acc * hv * r, axis=0, keepdims=True)

    slab = pl.BlockSpec((ns, tm, n), lambda i: (0, ni - 1 - i, 0))
    row = pl.BlockSpec((tm, D), lambda i: (ni - 1 - i, 0))
    vec = pl.BlockSpec((1, D), lambda i: (0, 0))
    return _run(
        body, [dc, wup, cw, h, gain, dh_in], hook, grid=(ni,), name=name, semantics=("arbitrary",),
        in_specs=[slab, _resident(wup.shape), _resident(cw.shape), row, vec, row],
        out_specs=[slab, row, vec], out_shape=[S((ns, t, n), BF16), S((t, D), F32), S((1, D), F32)],
        scratch_shapes=[pltpu.VMEM((ns, HALO, n), F32)])


def _dw_slot(hn, dy_s, name, hook=None):
    t, k = hn.shape
    ns, _, n = dy_s.shape
    tm = _tm(t)

    def body(a_ref, b_ref, o_ref, ob_ref, at_ref):
        @pl.when(pl.program_id(0) == 0)
        def _():
            for i in range(t // tm):
                at_ref[:, i * tm:(i + 1) * tm] = a_ref[i * tm:(i + 1) * tm, :].T

        acc = _dot(at_ref[...], b_ref[...])
        o_ref[...] = acc
        ob_ref[...] = acc.astype(BF16)

    ospec = pl.BlockSpec((None, k, n), lambda j: (j, 0, 0))
    return _run(
        body, [hn, dy_s], hook, grid=(ns,), name=name, semantics=("arbitrary",),
        in_specs=[_resident(hn.shape), pl.BlockSpec((None, t, n), lambda j: (j, 0, 0))],
        out_specs=[ospec, ospec], out_shape=[S((ns, k, n), F32), S((ns, k, n), BF16)],
        scratch_shapes=[pltpu.VMEM((k, t), BF16)])


def _dw_rows(a_s, dh, name, hook=None):
    nk, t, kc = a_s.shape
    tm = _tm(t)
    ni = t // tm

    def body(a_ref, d_ref, o_ref, ob_ref):
        i = pl.program_id(0)
        dhb = d_ref[...].astype(BF16)

        @pl.when(i == 0)
        def _():
            o_ref[...] = jnp.zeros_like(o_ref)

        for j in range(nk):
            o_ref[j * kc:(j + 1) * kc, :] += _dot_tn(a_ref[j], dhb)

        @pl.when(i == ni - 1)
        def _():
            ob_ref[...] = o_ref[...].astype(BF16)

    ospec = pl.BlockSpec((nk * kc, D), lambda i: (0, 0))
    return _run(
        body, [a_s, dh], hook, grid=(ni,), name=name, semantics=("arbitrary",),
        in_specs=[pl.BlockSpec((nk, tm, kc), lambda i: (0, i, 0)), pl.BlockSpec((tm, D), lambda i: (i, 0))],
        out_specs=[ospec, ospec], out_shape=[S((nk * kc, D), F32), S((nk * kc, D), BF16)])


def _dx_slot_normbwd(dy_s, wg, h, gain, dh_in, name, hook=None):
    ns, t, n = dy_s.shape
    tm = _tm(t)

    def body(dy_ref, w_ref, h_ref, g_ref, di_ref, o_ref, dg_ref):
        i = pl.program_id(0)

        @pl.when(i == 0)
        def _():
            dg_ref[...] = jnp.zeros_like(dg_ref)

        g = _dot_nt(dy_ref[0], w_ref[0])
        for s in range(1, ns):
            g = g + _dot_nt(dy_ref[s], w_ref[s])
        hv = h_ref[...]
        r = _rstd(hv)
        gg = g * g_ref[...]
        o_ref[...] = di_ref[...] + r * gg - hv * (r * r * r * jnp.mean(gg * hv, axis=-1, keepdims=True))
        dg_ref[...] += jnp.sum(g * hv * r, axis=0, keepdims=True)

    row = pl.BlockSpec((tm, D), lambda i: (i, 0))
    vec = pl.BlockSpec((1, D), lambda i: (0, 0))
    return _run(
        body, [dy_s, wg, h, gain, dh_in], hook, grid=(t // tm,), name=name, semantics=("arbitrary",),
        in_specs=[pl.BlockSpec((ns, tm, n), lambda i: (0, i, 0)), _resident(wg.shape), row, vec, row],
        out_specs=[row, vec], out_shape=[S((t, D), F32), S((1, D), F32)])


def _sgu_gate_bwd(a_s, dg_s, vgain, ws, bst, name, hook=None):
    t = a_s.shape[1]
    sw = a_s.shape[2]
    gps = sw // CHUNK

    def body(a_ref, dg_ref, vg_ref, ws_ref, b_ref, da_ref, dws_ref, dbt_ref, dvg_ref, dvn_ref):
        n = pl.program_id(0)

        @pl.when(n == 0)
        def _():
            dws_ref[...] = jnp.zeros_like(dws_ref)
            dbt_ref[...] = jnp.zeros_like(dbt_ref)
            dvg_ref[...] = jnp.zeros_like(dvg_ref)

        vpre = jnp.concatenate([a_ref[4 + s].astype(F32) for s in range(4)], axis=1)
        v = _gelu(vpre)
        r = _rstd(v)
        vhat = v * r
        vn = (vhat * vg_ref[...]).astype(BF16)
        tri = _tril_mask()
        lane = lax.broadcasted_iota(jnp.int32, (CHUNK, CHUNK), 1)
        dbt = jnp.zeros((CHUNK, CHUNK), F32)
        for g in range(SGU_G):
            w = jnp.where(tri, ws_ref[g], 0.0).astype(BF16)
            vng = vn[:, g * CHUNK:(g + 1) * CHUNK]
            sg = _dot(w, vng) + b_ref[:, g:g + 1]
            lo = (g % gps) * CHUNK
            upre = a_ref[g // gps, :, lo:lo + CHUNK].astype(F32)
            dgate = dg_ref[g // gps, :, lo:lo + CHUNK].astype(F32)
            da_ref[g // gps, :, lo:lo + CHUNK] = (dgate * sg * _gelu_grad(upre)).astype(BF16)
            ds = dgate * _gelu(upre)
            dsb = ds.astype(BF16)
            dvn_ref[:, g * CHUNK:(g + 1) * CHUNK] = _dot_tn(w, dsb)
            dws_ref[g] += jnp.where(tri, _dot_nt(dsb, vng), 0.0)
            dbt = dbt + jnp.where(lane == g, jnp.sum(ds, axis=-1, keepdims=True), 0.0)
        dbt_ref[...] += dbt
        dvn = dvn_ref[...]
        dvg_ref[...] += jnp.sum(dvn * vhat, axis=0, keepdims=True)
        gg = dvn * vg_ref[...]
        dv = r * gg - v * (r * r * r * jnp.mean(gg * v, axis=-1, keepdims=True))
        dav = (dv * _gelu_grad(vpre)).astype(BF16)
        for s in range(4):
            da_ref[4 + s] = dav[:, s * sw:(s + 1) * sw]

    return _run(
        body, [a_s, dg_s, vgain, ws, bst], hook, grid=(t // CHUNK,), name=name, semantics=("arbitrary",),
        in_specs=[pl.BlockSpec((8, CHUNK, sw), lambda n: (0, n, 0)), pl.BlockSpec((4, CHUNK, sw), lambda n: (0, n, 0)),
                  pl.BlockSpec((1, SGU_W), lambda n: (0, 0)), pl.BlockSpec((SGU_G, CHUNK, CHUNK), lambda n: (0, 0, 0)),
                  pl.BlockSpec((CHUNK, SGU_G), lambda n: (0, 0))],
        out_specs=[pl.BlockSpec((8, CHUNK, sw), lambda n: (0, n, 0)), pl.BlockSpec((SGU_G, CHUNK, CHUNK), lambda n: (0, 0, 0)),
                   pl.BlockSpec((CHUNK, CHUNK), lambda n: (0, 0)), pl.BlockSpec((1, SGU_W), lambda n: (0, 0))],
        out_shape=[S((8, t, sw), BF16), S((SGU_G, CHUNK, CHUNK), F32), S((CHUNK, CHUNK), F32), S((1, SGU_W), F32)],
        scratch_shapes=[pltpu.VMEM((CHUNK, SGU_W), F32)])


def _attn_bwd(qkv_s, do, qg, kg, sinks, bias, name, hook=None):
    t = qkv_s.shape[1]
    nb = t // CHUNK

    def body(cur_ref, prev_ref, do_ref, qg_ref, kg_ref, sink_ref, bias_ref,
             o_ref, dqg_ref, dkg_ref, dsk_ref, dbias_ref, carry, top):
        n = pl.program_id(0)

        @pl.when(n == 0)
        def _():
            carry[...] = jnp.zeros_like(carry)
            dqg_ref[...] = jnp.zeros_like(dqg_ref)
            dkg_ref[...] = jnp.zeros_like(dkg_ref)
            dsk_ref[...] = jnp.zeros_like(dsk_ref)
            dbias_ref[...] = jnp.zeros_like(dbias_ref)

        @pl.when(n < nb)
        def _():
            valid = _attn_valid(n)
            top[...] = jnp.zeros_like(top)
            new = [[None] * 3 for _ in range(8)]
            lane = lax.broadcasted_iota(jnp.int32, (1, CHUNK), 1)
            dsk = jnp.zeros((1, CHUNK), F32)
            dqg = jnp.zeros((1, HD), F32)
            dkg = jnp.zeros((1, HD), F32)

            def place(col, val):
                new[col // QKV_SLOT][(col % QKV_SLOT) // HD] = val

            for h in range(NKV):
                kcol = D + HD * h
                vcol = D + HD * (NKV + h)
                k = jnp.concatenate([_head(prev_ref, kcol), _head(cur_ref, kcol)], axis=0)
                v = jnp.concatenate([_head(prev_ref, vcol), _head(cur_ref, vcol)], axis=0)
                rk = _rstd(k)
                khat = k * rk
                kn = (khat * kg_ref[...]).astype(BF16)
                vb = v.astype(BF16)
                q = _group_q(cur_ref, h)
                rq = _rstd(q)
                qhat = q * rq
                qn = (qhat * qg_ref[...]).astype(BF16)
                bias = bias_ref[KVG * h:KVG * (h + 1)].reshape(KVG * CHUNK, 2 * CHUNK)
                p, psink = _attn_probs(qn, kn, bias, valid, _group_sinks(sink_ref, h))
                doh = jnp.concatenate([do_ref[0, :, HD * (KVG * h + g):HD * (KVG * h + g + 1)] for g in range(KVG)], axis=0)
                dp = _dot_nt(doh, vb)
                dsum = jnp.sum(p * dp, axis=-1, keepdims=True)
                ds = p * (dp - dsum)
                dsink = psink * dsum
                for g in range(KVG):
                    dsk = dsk - jnp.where(lane == KVG * h + g, jnp.sum(dsink[g * CHUNK:(g + 1) * CHUNK]), 0.0)
                dbias_ref[KVG * h:KVG * (h + 1)] += ds.reshape(KVG, CHUNK, 2 * CHUNK)
                dv = _dot_tn(p.astype(BF16), doh)
                dsc = (ds * (HD ** -0.5)).astype(BF16)
                dqn = _dot(dsc, kn)
                dkn = _dot_tn(dsc, qn)
                dqg = dqg + jnp.sum(dqn * qhat, axis=0, keepdims=True)
                gq = dqn * qg_ref[...]
                dq = rq * gq - q * (rq * rq * rq * jnp.mean(gq * q, axis=-1, keepdims=True))
                for g in range(KVG):
                    place(HD * (KVG * h + g), dq[g * CHUNK:(g + 1) * CHUNK])
                dkg = dkg + jnp.sum(dkn * khat, axis=0, keepdims=True)
                gk = dkn * kg_ref[...]
                dk = rk * gk - k * (rk * rk * rk * jnp.mean(gk * k, axis=-1, keepdims=True))
                place(kcol, dk[CHUNK:])
                place(vcol, dv[CHUNK:])
                top[kcol // QKV_SLOT, :, kcol % QKV_SLOT:kcol % QKV_SLOT + HD] = dk[:CHUNK]
                top[vcol // QKV_SLOT, :, vcol % QKV_SLOT:vcol % QKV_SLOT + HD] = dv[:CHUNK]
            dqg_ref[...] += dqg
            dkg_ref[...] += dkg
            dsk_ref[...] += dsk
            o_ref[...] = (carry[...] + top[...]).astype(BF16)
            for s in range(8):
                carry[s] = jnp.concatenate(new[s], axis=1)

        @pl.when(n == nb)
        def _():
            o_ref[...] = carry[...].astype(BF16)

    blk = lambda f: pl.BlockSpec((8, CHUNK, QKV_SLOT), f)
    cur = lambda n: (0, jnp.minimum(n, nb - 1), 0)
    prev = lambda n: (0, jnp.clip(n - 1, 0, nb - 1), 0)
    small = lambda w: pl.BlockSpec((1, w), lambda n: (0, 0))
    return _run(
        body, [qkv_s, qkv_s, do, qg, kg, sinks, bias], hook, grid=(nb + 1,), name=name, semantics=("arbitrary",),
        in_specs=[blk(cur), blk(prev), pl.BlockSpec((1, CHUNK, D), cur), small(HD), small(HD),
                  pl.BlockSpec(memory_space=pltpu.SMEM), pl.BlockSpec((NH, CHUNK, 2 * CHUNK), lambda n: (0, 0, 0))],
        out_specs=[blk(lambda n: (0, jnp.maximum(n - 1, 0), 0)), small(HD), small(HD), small(CHUNK),
                   pl.BlockSpec((NH, CHUNK, 2 * CHUNK), lambda n: (0, 0, 0))],
        out_shape=[S((8, t, QKV_SLOT), BF16), S((1, HD), F32), S((1, HD), F32), S((1, CHUNK), F32),
                   S((NH, CHUNK, 2 * CHUNK), F32)],
        scratch_shapes=[pltpu.VMEM((8, CHUNK, QKV_SLOT), F32), pltpu.VMEM((8, CHUNK, QKV_SLOT), F32)])


class _Plain:
    def __init__(self, wg):
        self.full, self.grads = wg, {}

    def w(self, n):
        return self.full[n]

    def hook(self, host):
        return None

    def grad(self, n, pair):
        self.grads[n] = pair

    def small(self, g_rep):
        pass


def _local_step(x, target, rep, sch):
    bucket_row = jnp.asarray(_rel_tables().reshape(1, -1))
    bias = _relbias_fwd(rep["rel_bias"].T, bucket_row, "relbias_fwd").reshape(NH, CHUNK, 2 * CHUNK)
    bst = rep["sgu_b_s"][0].T
    ws = rep["sgu_w_s"][0]
    vgain = rep["sgu_v_gain"]
    qg, kg, sinks = rep["attn_q_gain"], rep["attn_k_gain"], rep["attn_sinks"][0]
    w_down = lambda l: sch.w("ffn_w_down%d" % l).reshape(D_FF, D)
    w_up = lambda l: sch.w("ffn_w_up%d" % l)
    cw = [sch.w("ffn_conv_w")[:, 3 * l:3 * l + 3] for l in range(2)]
    cb = [rep["ffn_conv_b"][l].reshape(8, 1, -1) for l in range(2)]
    mixg = [rep["mix_norm"][l:l + 1] for l in range(2)]
    ffng = [rep["ffn_norm"][l:l + 1] for l in range(2)]
    rows = lambda pair: tuple(g.reshape(N_DEV, -1, D) for g in pair)
    hk = sch.hook

    hn0 = _rmsnorm(x, mixg[0], "norm0")
    a0 = _mm_slot(hn0, sch.w("sgu_w_in"), BF16, "sgu_in", hk("sgu_in"))
    gated = _sgu_gate_fwd(a0, vgain, ws, bst, "sgu_gate", hk("sgu_gate"))
    h1, hn1 = _resid_mm(gated, sch.w("sgu_w_out").reshape(SGU_W, D), x, ffng[0], "norm", "sgu_out", hk("sgu_out"))
    ag0, av0, h2, hn2 = _ffn_fwd(hn1, h1, w_up(0), w_down(0), cw[0], cb[0], mixg[1], "norm", "ffn0_fwd", hk("ffn0_fwd"))
    qkv = _mm_slot(hn2, sch.w("attn_w_qkv"), F32, "qkv", hk("qkv"))
    o = _attn_fwd(qkv, qg, kg, sinks, bias, "attn", hk("attn"))
    h3, hn3 = _resid_mm(o, sch.w("attn_w_o").reshape(D, D), h2, ffng[1], "norm", "attn_out", hk("attn_out"))
    ag1, av1, dy, sq = _ffn_fwd(hn3, h3, w_up(1), w_down(1), cw[1], cb[1], target, "loss", "ffn1_fwd_loss", hk("ffn1_fwd_loss"))
    loss = (0.5 / D) * jnp.sum(sq[:, 0, 0])

    def ffn_bwd(dh, h_in, hn, a_g, a_v, l, tag):
        dc, g_down, g_down_b, g_cw, g_cb = _ffn_bwd1(dh, a_g, a_v, w_down(l), cw[l], cb[l], tag + "_bwd1", hk(tag + "_bwd1"))
        sch.grad("ffn_w_down%d" % l, rows((g_down, g_down_b)))
        da, dh_new, dgain = _ffn_bwd2(dc, w_up(l), cw[l], h_in, ffng[l], dh, tag + "_bwd2", hk(tag + "_bwd2"))
        sch.grad("ffn_w_up%d" % l, _dw_slot(hn, da, tag + "_dw_up", hk(tag + "_dw_up")))
        return dh_new, dgain, g_cw, g_cb.reshape(-1)

    dh, d_ffng1, g_cw1, g_cb1 = ffn_bwd(dy, h3, hn3, ag1, av1, 1, "ffn1")
    do = _dx_rows(dh, sch.w("attn_w_o").reshape(D, D), D, BF16, "attn_do", hk("attn_do"))
    sch.grad("attn_w_o", rows(_dw_rows(o, dh, "dw_o", hk("dw_o"))))
    dqkv, d_qg, d_kg, d_sk, d_bias = _attn_bwd(qkv, do, qg, kg, sinks, bias, "attn_bwd", hk("attn_bwd"))
    sch.grad("attn_w_qkv", _dw_slot(hn2, dqkv, "dw_qkv", hk("dw_qkv")))
    dh, d_mixg1 = _dx_slot_normbwd(dqkv, sch.w("attn_w_qkv"), h2, mixg[1], dh, "dx_qkv", hk("dx_qkv"))
    d_relb = _relbias_bwd(d_bias.reshape(NH, -1), bucket_row, "relbias_bwd").T
    dh, d_ffng0, g_cw0, g_cb0 = ffn_bwd(dh, h1, hn1, ag0, av0, 0, "ffn0")
    g_cw = jnp.concatenate([g_cw0, g_cw1], axis=1)
    sch.grad("ffn_conv_w", (g_cw, g_cw.astype(BF16)))
    dgated = _dx_rows(dh, sch.w("sgu_w_out").reshape(SGU_W, D), SGU_W // 4, BF16, "sgu_dgated", hk("sgu_dgated"))
    sch.grad("sgu_w_out", rows(_dw_rows(gated, dh, "dw_sgu_out", hk("dw_sgu_out"))))
    da0, d_ws, d_bst, d_vgain = _sgu_gate_bwd(a0, dgated, vgain, ws, bst, "sgu_gate_bwd", hk("sgu_gate_bwd"))
    grad_x, d_mixg0 = _dx_slot_normbwd(da0, sch.w("sgu_w_in"), x, mixg[0], dh, "dx_sgu_in")
    g_rep = {
        "mix_norm": jnp.concatenate([d_mixg0, d_mixg1], axis=0),
        "ffn_norm": jnp.concatenate([d_ffng0, d_ffng1], axis=0),
        "sgu_v_gain": d_vgain,
        "sgu_w_s": d_ws[None],
        "sgu_b_s": d_bst[:, :SGU_G].T[None],
        "attn_q_gain": d_qg,
        "attn_k_gain": d_kg,
        "attn_sinks": d_sk[:, :NH],
        "rel_bias": d_relb,
        "ffn_conv_b": jnp.stack([g_cb0, g_cb1], axis=0),
    }
    sch.small(g_rep)
    sch.grad("sgu_w_in", _dw_slot(hn0, da0, "dw_sgu_in", hk("dw_sgu_in")))
    return loss, grad_x, g_rep


def _allgather(xs, name):
    nt = len(xs)

    def body(*refs):
        x_refs, o_refs = refs[:nt], refs[nt:2 * nt]
        send_sems, recv_sems, local_sems = refs[2 * nt:]
        x, y, c, chips = _place()
        me, sibling = (x, y, c), (x, y, 1 - c)

        def copy(t, k, block, to, src=None):
            px, py, pc = block
            dst = o_refs[t].at[4 * px + 2 * py + pc]
            return pltpu.make_async_remote_copy(
                src_ref=dst if src is None else src, dst_ref=dst, send_sem=send_sems.at[t, k], recv_sem=recv_sems.at[t, k],
                device_id=to, device_id_type=MESH)

        mine = [pltpu.make_async_copy(x_refs[t], o_refs[t].at[4 * x + 2 * y + c], local_sems.at[t]) for t in range(nt)]
        for cp in mine:
            cp.start()
        first = []
        for t in range(nt):
            first.append(copy(t, 0, me, sibling, src=x_refs[t]))
            first += [copy(t, 1 + j, me, (*chip, c), src=x_refs[t]) for j, chip in enumerate(chips)]
        for cp in first:
            cp.start()
        passed = []
        for j, chip in enumerate(chips):
            for t in range(nt):
                copy(t, 1 + j, (*chip, c), me).wait_recv()
                fwd = copy(t, 4 + j, (*chip, c), sibling)
                fwd.start()
                passed.append(fwd)
        for t in range(nt):
            copy(t, 0, sibling, me).wait_recv()
            for j, chip in enumerate(chips):
                copy(t, 4 + j, (*chip, 1 - c), me).wait_recv()
        for cp in first + passed:
            cp.wait_send()
        for cp in mine:
            cp.wait()

    return pl.pallas_call(
        body, name=name, in_specs=[ANY] * nt, out_specs=[ANY] * nt,
        out_shape=[S((N_DEV,) + a.shape, a.dtype) for a in xs],
        scratch_shapes=[pltpu.SemaphoreType.DMA((nt, 7)), pltpu.SemaphoreType.DMA((nt, 7)), pltpu.SemaphoreType.DMA((nt,))],
        compiler_params=pltpu.CompilerParams(has_side_effects=True))(*xs)


def _exchange(hook, name):
    comm = hook()
    ci, co = len(comm.inputs), len(comm.out_shapes)

    def body(*refs):
        cins, couts = refs[:ci], refs[ci:ci + co]
        send, recv = refs[-2:]
        comm.start(cins, couts, send, recv)
        comm.finish(cins, couts, send, recv)

    res = pl.pallas_call(
        body, name=name, in_specs=[ANY] * ci, out_specs=[ANY] * co, out_shape=comm.out_shapes,
        scratch_shapes=[pltpu.SemaphoreType.DMA((comm.n_sems,)), pltpu.SemaphoreType.DMA((comm.n_sems,))],
        input_output_aliases=dict(comm.aliases),
        compiler_params=pltpu.CompilerParams(has_side_effects=True))(*comm.inputs)
    hook(res)


def _row_tile(r):
    tr = r if r <= ROW_TILE or r % ROW_TILE else ROW_TILE
    assert r % tr == 0
    return tr


def _rs_partial(g32, sib, place, name):
    _, r, cdim = g32.shape
    tr = _row_tile(r)

    def body(place_ref, g_ref, s_ref, p_ref, own_ref):
        k = pl.program_id(1)
        tot = g_ref[...] + s_ref[...].astype(F32)
        p_ref[...] = tot.astype(BF16)

        @pl.when(k == place_ref[1])
        def _():
            own_ref[...] = tot

    grid_spec = pltpu.PrefetchScalarGridSpec(
        num_scalar_prefetch=1, grid=(r // tr, 4),
        in_specs=[pl.BlockSpec((None, None, tr, cdim), lambda i, k, pr: (k, pr[0], i, 0)),
                  pl.BlockSpec((None, tr, cdim), lambda i, k, pr: (k, i, 0))],
        out_specs=[pl.BlockSpec((None, tr, cdim), lambda i, k, pr: (k, i, 0)), pl.BlockSpec((tr, cdim), lambda i, k, pr: (i, 0))])
    return pl.pallas_call(
        body, grid_spec=grid_spec, name=name,
        out_shape=[S((4, r, cdim), BF16), S((r, cdim), F32)],
        compiler_params=_cp("parallel", "arbitrary"))(place, g32.reshape(4, 2, r, cdim), sib)


def _adamw_math(w, g, m, v):
    m = ADAM_B1 * m + (1.0 - ADAM_B1) * g
    v = ADAM_B2 * v + (1.0 - ADAM_B2) * (g * g)
    m_hat = m / (1.0 - ADAM_B1 ** ADAM_STEP)
    v_hat = v / (1.0 - ADAM_B2 ** ADAM_STEP)
    delta = -ADAM_LR * (m_hat / (jnp.sqrt(v_hat) + ADAM_EPS) + ADAM_WD * w)
    return delta, m, v


def _adamw_shard(owns, recvs, w, m, v, name):
    nl, r, cdim = w.shape
    tr = _row_tile(r)
    nr = r // tr

    def body(*refs):
        own_refs, recv_refs = refs[:nl], refs[nl:2 * nl]
        w_ref, m_ref, v_ref, g_out, d_out, m_out, v_out = refs[2 * nl:]
        layer = pl.program_id(0)
        g = None
        for l in range(nl):
            gl = own_refs[l][...] + recv_refs[l][0].astype(F32) + recv_refs[l][1].astype(F32) + recv_refs[l][2].astype(F32)
            g = gl if g is None else jnp.where(layer == l, gl, g)
        g_out[...] = g
        d_out[...], m_out[...], v_out[...] = _adamw_math(w_ref[...], g, m_ref[...], v_ref[...])

    park = lambda l: (lambda layer, i: (jnp.where(layer == l, i, jnp.where(layer < l, 0, nr - 1)), 0))
    park3 = lambda l: (lambda layer, i: (0, jnp.where(layer == l, i, jnp.where(layer < l, 0, nr - 1)), 0))
    row = pl.BlockSpec((None, tr, cdim), lambda layer, i: (layer, i, 0))
    return pl.pallas_call(
        body, grid=(nl, nr), name=name,
        in_specs=[pl.BlockSpec((tr, cdim), park(l)) for l in range(nl)] + [pl.BlockSpec((3, tr, cdim), park3(l)) for l in range(nl)]
        + [row, row, row],
        out_specs=[row] * 4, out_shape=[S((nl, r, cdim), F32)] * 4,
        compiler_params=_cp("arbitrary", "arbitrary"))(*owns, *recvs, w, m, v)


def _adamw_small(galls, ws, ms, vs, name):
    n = len(galls)

    def body(*refs):
        g_refs, w_refs, m_refs, v_refs, outs = refs[:n], refs[n:2 * n], refs[2 * n:3 * n], refs[3 * n:4 * n], refs[4 * n:]
        for i in range(n):
            g = g_refs[i][0]
            for s in range(1, N_DEV):
                g = g + g_refs[i][s]
            outs[i][...] = g
            outs[n + i][...], outs[2 * n + i][...], outs[3 * n + i][...] = _adamw_math(w_refs[i][...], g, m_refs[i][...], v_refs[i][...])

    res = pl.pallas_call(body, out_shape=[S(a.shape, F32) for a in ws] * 4, name=name)(*galls, *ws, *ms, *vs)
    return [res[k * n:(k + 1) * n] for k in range(4)]


REPLICATED = ["mix_norm", "ffn_norm", "sgu_v_gain", "sgu_w_s", "sgu_b_s", "attn_q_gain", "attn_k_gain", "attn_sinks", "rel_bias",
              "ffn_conv_b"]
WEIGHTS = ["mix_norm", "ffn_norm", "sgu_w_in", "sgu_v_gain", "sgu_w_s", "sgu_b_s", "sgu_w_out", "attn_w_qkv", "attn_q_gain",
           "attn_k_gain", "attn_sinks", "attn_w_o", "rel_bias", "ffn_w_up", "ffn_conv_w", "ffn_conv_b", "ffn_w_down"]
SMALL = ["g_" + n for n in REPLICATED]

GATHER_FIRST = ["sgu_w_in", "sgu_w_out", "ffn_conv_w"]
PLAN = {
    "sgu_in": [("ag1", "ffn_w_up0")],
    "sgu_gate": [("ag2", "ffn_w_up0"), ("ag1", "ffn_w_down0")],
    "sgu_out": [("ag2", "ffn_w_down0"), ("ag1", "attn_w_qkv")],
    "ffn0_fwd": [("ag2", "attn_w_qkv"), ("ag1", "attn_w_o"), ("ag1", "ffn_w_up1")],
    "qkv": [("ag2", "attn_w_o"), ("ag2", "ffn_w_up1")],
    "attn": [("ag1", "ffn_w_down1")],
    "attn_out": [("ag2", "ffn_w_down1")],
    "ffn1_bwd2": [("rs1", "ffn_w_down1")],
    "ffn1_dw_up": [("rs2", "ffn_w_down1")],
    "attn_do": [("rs1", "ffn_w_up1")],
    "attn_bwd": [("rs2", "ffn_w_up1"), ("rs1", "attn_w_o")],
    "dw_qkv": [("rs2", "attn_w_o")],
    "dx_qkv": [("rs1", "attn_w_qkv")],
    "ffn0_bwd1": [("rs2", "attn_w_qkv")],
    "ffn0_bwd2": [("rs1", "ffn_w_down0")],
    "ffn0_dw_up": [("rs2", "ffn_w_down0")],
    "sgu_dgated": [("rs1", "ffn_w_up0")],
    "sgu_gate_bwd": [("rs2", "ffn_w_up0"), ("rs1", "sgu_w_out")],
    "dw_sgu_in": [("rs2", "sgu_w_out")] + [("ag1", n) for n in SMALL],
    "last_a": [("rs1", "sgu_w_in"), ("rs1", "ffn_conv_w")] + [("ag2", n) for n in SMALL],
    "last_b": [("rs2", "sgu_w_in"), ("rs2", "ffn_conv_w")],
}


class _Overlap:
    def __init__(self, shard, place):
        self.shard, self.place = shard, place
        self.part, self.full = {}, {}
        self.grads, self.sib, self.own, self.recv = {}, {}, {}, {}

    def w(self, n):
        return self.full[n]

    def grad(self, n, pair):
        self.grads[n] = pair

    def small(self, g_rep):
        self.shard.update(("g_" + n, a) for n, a in _views2d(g_rep).items())

    def chip_sums(self, n):
        sums, self.own[n] = _rs_partial(self.grads[n][0], self.sib.pop(n), self.place, "rs_partial_" + n)
        return sums

    def hook(self, host):
        ops = PLAN.get(host)
        if not ops:
            return None
        where = {"ag1": self.part, "ag2": self.full, "rs1": self.sib, "rs2": self.recv}
        idx = []

        def hook(results=None):
            if results is not None:
                for (kind, n), i in zip(ops, idx):
                    where[kind][n] = results[i]
                return None
            comm = _Comm()
            for kind, n in ops:
                arr = {"ag1": lambda: self.shard[n], "ag2": lambda: self.part.pop(n), "rs1": lambda: self.grads[n][1],
                       "rs2": lambda: self.chip_sums(n)}[kind]()
                idx.append(comm.add(kind, arr))
            return comm

        return hook


SHARDED = {
    "sgu_w_in": ["sgu_w_in"], "sgu_w_out": ["sgu_w_out"], "attn_w_qkv": ["attn_w_qkv"], "attn_w_o": ["attn_w_o"],
    "ffn_w_up": ["ffn_w_up0", "ffn_w_up1"], "ffn_w_down": ["ffn_w_down0", "ffn_w_down1"], "ffn_conv_w": ["ffn_conv_w"],
}


def _send_views(w):
    out = {"ffn_conv_w": w["ffn_conv_w"].reshape(6, -1)}
    for name, parts in SHARDED.items():
        if name != "ffn_conv_w":
            out.update((p, w[name][l].astype(BF16)) for l, p in enumerate(parts))
    return out


def _views2d(d):
    return {n: d[n].reshape(-1, d[n].shape[-1]) for n in REPLICATED}


def kernel(x, mix_norm, ffn_norm, sgu_w_in, sgu_v_gain, sgu_w_s, sgu_b_s, sgu_w_out, attn_w_qkv, attn_q_gain, attn_k_gain, attn_sinks, attn_w_o, rel_bias, ffn_w_up, ffn_conv_w, ffn_conv_b, ffn_w_down, loss_target, m_mix_norm, m_ffn_norm, m_sgu_w_in, m_sgu_v_gain, m_sgu_w_s, m_sgu_b_s, m_sgu_w_out, m_attn_w_qkv, m_attn_q_gain, m_attn_k_gain, m_attn_sinks, m_attn_w_o, m_rel_bias, m_ffn_w_up, m_ffn_conv_w, m_ffn_conv_b, m_ffn_w_down, v_mix_norm, v_ffn_norm, v_sgu_w_in, v_sgu_v_gain, v_sgu_w_s, v_sgu_b_s, v_sgu_w_out, v_attn_w_qkv, v_attn_q_gain, v_attn_k_gain, v_attn_sinks, v_attn_w_o, v_rel_bias, v_ffn_w_up, v_ffn_conv_w, v_ffn_conv_b, v_ffn_w_down):
    w = dict(zip(WEIGHTS, (mix_norm, ffn_norm, sgu_w_in, sgu_v_gain, sgu_w_s, sgu_b_s, sgu_w_out, attn_w_qkv, attn_q_gain, attn_k_gain,
                           attn_sinks, attn_w_o, rel_bias, ffn_w_up, ffn_conv_w, ffn_conv_b, ffn_w_down)))
    m = dict(zip(WEIGHTS, (m_mix_norm, m_ffn_norm, m_sgu_w_in, m_sgu_v_gain, m_sgu_w_s, m_sgu_b_s, m_sgu_w_out, m_attn_w_qkv, m_attn_q_gain,
                           m_attn_k_gain, m_attn_sinks, m_attn_w_o, m_rel_bias, m_ffn_w_up, m_ffn_conv_w, m_ffn_conv_b, m_ffn_w_down)))
    v = dict(zip(WEIGHTS, (v_mix_norm, v_ffn_norm, v_sgu_w_in, v_sgu_v_gain, v_sgu_w_s, v_sgu_b_s, v_sgu_w_out, v_attn_w_qkv, v_attn_q_gain,
                           v_attn_k_gain, v_attn_sinks, v_attn_w_o, v_rel_bias, v_ffn_w_up, v_ffn_conv_w, v_ffn_conv_b, v_ffn_w_down)))
    rep = {n: w[n] for n in REPLICATED}

    xi, yi, ci = lax.axis_index("x"), lax.axis_index("y"), lax.axis_index("c")
    place = jnp.stack([ci, 2 * xi + yi]).astype(jnp.int32)
    sch = _Overlap(_send_views(w), place)
    sch.full.update(zip(GATHER_FIRST, _allgather([sch.shard[n] for n in GATHER_FIRST], "gather_first")))

    loss, grad_x, g_rep = _local_step(x[0], loss_target[0], rep, sch)
    loss = lax.psum(loss, ("x", "y", "c"))
    _exchange(sch.hook("last_a"), "last_a")
    _exchange(sch.hook("last_b"), "last_b")

    out = [{}, {}, {}, {}]
    for name, parts in SHARDED.items():
        shape = w[name].shape
        as3d = lambda a: a.reshape(len(parts), -1, shape[-1])
        res = _adamw_shard([sch.own[p] for p in parts], [sch.recv[p] for p in parts], as3d(w[name]), as3d(m[name]), as3d(v[name]),
                           "adamw_" + name)
        for o, r in zip(out, res):
            o[name] = r.reshape(shape)
    small = _adamw_small([sch.full[n] for n in SMALL], *[list(_views2d(d).values()) for d in (rep, m, v)], "adamw_small")
    for o, res in zip(out, small):
        o.update((n, r.reshape(w[n].shape)) for n, r in zip(REPLICATED, res))

    return (loss, grad_x[None], *[out[0][n] for n in WEIGHTS], *[out[1][n] for n in WEIGHTS],
            *[out[2][n] for n in WEIGHTS], *[out[3][n] for n in WEIGHTS])
```

```python
import functools
import math

import numpy as np
import jax
import jax.numpy as jnp
from jax import lax
from jax.experimental import pallas as pl
from jax.experimental.pallas import tpu as pltpu

F32 = jnp.float32
BF16 = jnp.bfloat16
S = jax.ShapeDtypeStruct

D = 1024
CHUNK = 128
SGU_W = 2048
SGU_G = 16
HD = 64
NH = 16
NKV = 4
KVG = 4
D_FF = 2816
REL_BUCKETS = 32
REL_MAX_DIST = 128
EPS = 1e-6
N_DEV = 8
MESH = pl.DeviceIdType.MESH

ADAM_LR = 0.001
ADAM_B1 = 0.9
ADAM_B2 = 0.999
ADAM_EPS = 1e-08
ADAM_WD = 0.01
ADAM_STEP = 10

ROW_TILE = 512
HALO = 8


def _tm(t):
    return min(ROW_TILE, t)


def _cp(*sem):
    return pltpu.CompilerParams(dimension_semantics=sem)


ANY = pl.BlockSpec(memory_space=pl.ANY)


def _place():
    x, y, c = lax.axis_index("x"), lax.axis_index("y"), lax.axis_index("c")
    return x, y, c, [(1 - x, y), (x, 1 - y), (1 - x, 1 - y)]


class _Comm:
    SEMS = {"ag1": 5, "ag2": 3, "rs1": 4, "rs2": 3}

    def __init__(self):
        self.inputs, self.out_shapes, self.aliases, self.ops, self.n_sems = [], [], {}, [], 0

    def add(self, kind, arr):
        lead = {"ag1": N_DEV, "ag2": None, "rs1": 4, "rs2": 3}[kind]
        shape = arr.shape if lead is None else (lead,) + arr.shape[(0 if kind == "ag1" else 1):]
        if kind == "ag2":
            self.aliases[len(self.inputs)] = len(self.out_shapes)
        self.ops.append((kind, len(self.inputs), len(self.out_shapes), self.n_sems))
        self.inputs.append(arr)
        self.out_shapes.append(S(shape, arr.dtype))
        self.n_sems += self.SEMS[kind]
        return len(self.out_shapes) - 1

    def _copies(self, ins, outs, send, recv):
        x, y, c, chips = _place()
        me, sibling = (x, y, c), (x, y, 1 - c)
        slot = lambda px, py, pc: 4 * px + 2 * py + pc
        sends, recvs, local = [], [], []

        def rc(src, dst, k, to):
            return lambda: pltpu.make_async_remote_copy(src_ref=src(), dst_ref=dst(), send_sem=send.at[k], recv_sem=recv.at[k],
                                                        device_id=to, device_id_type=MESH)

        for kind, ii, oi, b in self.ops:
            src, dst = ins[ii], outs[oi]
            at = lambda ref, i: (lambda: ref.at[i])
            if kind == "ag1":
                whole, mine = (lambda s=src: s), at(dst, slot(*me))
                sends.append(rc(whole, mine, b, sibling))
                recvs.append(rc(whole, at(dst, slot(x, y, 1 - c)), b, me))
                for j, chip in enumerate(chips):
                    sends.append(rc(whole, mine, b + 1 + j, (*chip, c)))
                    recvs.append(rc(whole, at(dst, slot(*chip, c)), b + 1 + j, me))
                local.append(lambda s=src, m=mine, k=b + 4: pltpu.make_async_copy(s, m(), send.at[k]))
            elif kind == "ag2":
                for j, chip in enumerate(chips):
                    sends.append(rc(at(dst, slot(*chip, c)), at(dst, slot(*chip, c)), b + j, sibling))
                    recvs.append(rc(at(dst, slot(*chip, 1 - c)), at(dst, slot(*chip, 1 - c)), b + j, me))
            elif kind == "rs1":
                for k in range(4):
                    sends.append(rc(at(src, 2 * k + (1 - c)), at(dst, k), b + k, sibling))
                    recvs.append(rc(at(src, 2 * k + c), at(dst, k), b + k, me))
            else:
                for j, (px, py) in enumerate(chips):
                    sends.append(rc(at(src, 2 * px + py), at(dst, j), b + j, (px, py, c)))
                    recvs.append(rc(at(src, 2 * px + py), at(dst, j), b + j, me))
        return sends, recvs, local

    def start(self, ins, outs, send, recv):
        sends, _, local = self._copies(ins, outs, send, recv)
        for make in local + sends:
            make().start()

    def finish(self, ins, outs, send, recv):
        sends, recvs, local = self._copies(ins, outs, send, recv)
        for make in recvs:
            make().wait_recv()
        for make in sends:
            make().wait_send()
        for make in local:
            make().wait()


def _run(body, args, hook, *, grid, in_specs, out_specs, out_shape, name, semantics, scratch_shapes=()):
    comm = hook() if hook is not None else None
    if comm is None:
        return pl.pallas_call(body, grid=grid, in_specs=in_specs, out_specs=out_specs, out_shape=out_shape, name=name,
                              scratch_shapes=list(scratch_shapes), compiler_params=_cp(*semantics))(*args)
    single = not isinstance(out_shape, (list, tuple))
    out_shapes = [out_shape] if single else list(out_shape)
    out_specs_l = [out_specs] if single else list(out_specs)
    n_in, n_out, n_scr, ci, co = len(args), len(out_shapes), len(scratch_shapes), len(comm.inputs), len(comm.out_shapes)

    def wrapped(*refs):
        ins, cins = refs[:n_in], refs[n_in:n_in + ci]
        outs, couts = refs[n_in + ci:n_in + ci + n_out], refs[n_in + ci + n_out:n_in + ci + n_out + co]
        scr = refs[n_in + ci + n_out + co:n_in + ci + n_out + co + n_scr]
        send, recv = refs[-2:]
        first = functools.reduce(lambda a, b: a & b, [pl.program_id(a) == 0 for a in range(len(grid))])
        last = functools.reduce(lambda a, b: a & b, [pl.program_id(a) == g - 1 for a, g in enumerate(grid)])

        @pl.when(first)
        def _():
            comm.start(cins, couts, send, recv)

        body(*ins, *outs, *scr)

        @pl.when(last)
        def _():
            comm.finish(cins, couts, send, recv)

    res = pl.pallas_call(
        wrapped, grid=grid, in_specs=list(in_specs) + [ANY] * ci, out_specs=out_specs_l + [ANY] * co,
        out_shape=out_shapes + comm.out_shapes, name=name,
        scratch_shapes=list(scratch_shapes) + [pltpu.SemaphoreType.DMA((comm.n_sems,)), pltpu.SemaphoreType.DMA((comm.n_sems,))],
        input_output_aliases={n_in + k: n_out + v for k, v in comm.aliases.items()},
        compiler_params=pltpu.CompilerParams(dimension_semantics=("arbitrary",) * len(grid), has_side_effects=True))(*args, *comm.inputs)
    hook(res[n_out:])
    return res[0] if single else list(res[:n_out])


def _dot(a, b):
    return jnp.dot(a, b, preferred_element_type=F32)


def _dot_nt(a, b):
    return lax.dot_general(a, b, (((1,), (1,)), ((), ())), preferred_element_type=F32)


def _dot_tn(a, b):
    return lax.dot_general(a, b, (((0,), (0,)), ((), ())), preferred_element_type=F32)


def _gelu(x):
    return 0.5 * x * (1.0 + lax.erf(x * (2.0 ** -0.5)))


def _gelu_grad(x):
    return 0.5 * (1.0 + lax.erf(x * (2.0 ** -0.5))) + x * jnp.exp(-0.5 * x * x) * (1.0 / math.sqrt(2.0 * math.pi))


def _sigmoid(x):
    return 1.0 / (1.0 + jnp.exp(-x))


def _rstd(x):
    return lax.rsqrt(jnp.mean(x * x, axis=-1, keepdims=True) + EPS)


def _rel_tables():
    q = np.arange(CHUNK)[:, None] + CHUNK
    k = np.arange(2 * CHUNK)[None, :]
    dist = q - k
    n = np.maximum(dist, 0)
    max_exact = REL_BUCKETS // 2
    large = max_exact + (np.log(np.maximum(n, 1).astype(np.float32) / max_exact)
                         / math.log(REL_MAX_DIST / max_exact) * (REL_BUCKETS - max_exact)).astype(np.int32)
    large = np.minimum(large, REL_BUCKETS - 1)
    return np.where(n < max_exact, n, large).astype(np.int32)


def _rmsnorm(x, gain, name):
    t = x.shape[0]
    tm = _tm(t)

    def body(x_ref, g_ref, o_ref):
        xv = x_ref[...]
        o_ref[...] = (xv * _rstd(xv) * g_ref[...]).astype(BF16)

    return pl.pallas_call(
        body, grid=(t // tm,), name=name,
        in_specs=[pl.BlockSpec((tm, D), lambda i: (i, 0)), pl.BlockSpec((1, D), lambda i: (0, 0))],
        out_specs=pl.BlockSpec((tm, D), lambda i: (i, 0)),
        out_shape=S((t, D), BF16), compiler_params=_cp("parallel"))(x, gain)


def _resident(shape):
    zeros = (0,) * len(shape)
    return pl.BlockSpec(shape, lambda *_: zeros, pipeline_mode=pl.Buffered(1))


def _mm_slot(hn, wg, out_dtype, name, hook=None):
    t, k = hn.shape
    ns, _, n = wg.shape
    tm = _tm(t)

    def body(a_ref, w_ref, o_ref):
        a = a_ref[...]
        for s in range(ns):
            o_ref[s] = _dot(a, w_ref[s]).astype(out_dtype)

    return _run(
        body, [hn, wg], hook, grid=(t // tm,), name=name, semantics=("parallel",),
        in_specs=[pl.BlockSpec((tm, k), lambda i: (i, 0)), _resident(wg.shape)],
        out_specs=pl.BlockSpec((ns, tm, n), lambda i: (0, i, 0)), out_shape=S((ns, t, n), out_dtype))


def _conv3(a, prev, cw, cb, tm, keep=None):
    ext = jnp.concatenate([prev, a], axis=0)
    a1 = ext[HALO - 1:HALO - 1 + tm]
    a2 = ext[HALO - 2:HALO - 2 + tm]
    if keep is not None:
        keep[0] = a1
        keep[1] = a2
        a1, a2 = keep[0], keep[1]
    return cw[2:3] * a + cw[1:2] * a1 + cw[0:1] * a2 + cb, a1, a2


def _ffn_fwd(hn, h, wup, wdown, cw, cb, extra, mode, name, hook=None):
    t, k = hn.shape
    n = wup.shape[-1]
    nh = wup.shape[0] // 2
    tm = _tm(t)
    ni = t // tm

    def body(a_ref, h_ref, wu_ref, wd_ref, cw_ref, cb_ref, e_ref, ag_ref, av_ref, o1_ref, o2_ref, carry):
        i = pl.program_id(0)

        @pl.when(i == 0)
        def _():
            carry[...] = jnp.zeros_like(carry)

        a = a_ref[...]
        acc = h_ref[...]
        for j in range(nh):
            ag = _dot(a, wu_ref[j])
            av = _dot(a, wu_ref[nh + j])
            ag_ref[j] = ag.astype(BF16)
            av_ref[j] = av.astype(BF16)
            cg, _, _ = _conv3(ag, carry[j], cw_ref[j], cb_ref[j], tm)
            cv, _, _ = _conv3(av, carry[nh + j], cw_ref[nh + j], cb_ref[nh + j], tm)
            carry[j] = ag[tm - HALO:]
            carry[nh + j] = av[tm - HALO:]
            act = (cg * _sigmoid(cg) * cv).astype(BF16)
            acc = acc + _dot(act, wd_ref[j * n:(j + 1) * n, :])
        if mode == "norm":
            o1_ref[...] = acc
            o2_ref[...] = (acc * _rstd(acc) * e_ref[...]).astype(BF16)
        else:
            err = acc - e_ref[...]
            o1_ref[...] = err * (1.0 / D)
            o2_ref[...] = jnp.full(o2_ref.shape, jnp.sum(err * err), F32)

    row = pl.BlockSpec((tm, D), lambda i: (i, 0))
    if mode == "norm":
        e_spec, o2_spec, o2_shape = pl.BlockSpec((1, D), lambda i: (0, 0)), row, S((t, D), BF16)
    else:
        e_spec, o2_spec, o2_shape = row, pl.BlockSpec((None, 8, 128), lambda i: (i, 0, 0)), S((ni, 8, 128), F32)
    aspec = pl.BlockSpec((nh, tm, n), lambda i: (0, i, 0))
    return _run(
        body, [hn, h, wup, wdown, cw, cb, extra], hook, grid=(ni,), name=name, semantics=("arbitrary",),
        in_specs=[pl.BlockSpec((tm, k), lambda i: (i, 0)), row, _resident(wup.shape), _resident(wdown.shape),
                  _resident(cw.shape), _resident(cb.shape), e_spec],
        out_specs=[aspec, aspec, row, o2_spec],
        out_shape=[S((nh, t, n), BF16), S((nh, t, n), BF16), S((t, D), F32), o2_shape],
        scratch_shapes=[pltpu.VMEM((2 * nh, HALO, n), F32)])


def _tril_mask():
    r = lax.broadcasted_iota(jnp.int32, (CHUNK, CHUNK), 0)
    c = lax.broadcasted_iota(jnp.int32, (CHUNK, CHUNK), 1)
    return r >= c


def _sgu_gate_fwd(a_s, vgain, ws, bst, name, hook=None):
    t = a_s.shape[1]
    sw = a_s.shape[2]
    gps = sw // CHUNK

    def body(a_ref, vg_ref, ws_ref, b_ref, o_ref):
        v = _gelu(jnp.concatenate([a_ref[4 + s].astype(F32) for s in range(4)], axis=1))
        vn = (v * _rstd(v) * vg_ref[...]).astype(BF16)
        tri = _tril_mask()
        for g in range(SGU_G):
            w = jnp.where(tri, ws_ref[g], 0.0).astype(BF16)
            sg = _dot(w, vn[:, g * CHUNK:(g + 1) * CHUNK]) + b_ref[:, g:g + 1]
            lo = (g % gps) * CHUNK
            u = _gelu(a_ref[g // gps, :, lo:lo + CHUNK].astype(F32))
            o_ref[g // gps, :, lo:lo + CHUNK] = (u * sg).astype(BF16)

    return _run(
        body, [a_s, vgain, ws, bst], hook, grid=(t // CHUNK,), name=name, semantics=("parallel",),
        in_specs=[pl.BlockSpec((8, CHUNK, sw), lambda n: (0, n, 0)), pl.BlockSpec((1, SGU_W), lambda n: (0, 0)),
                  pl.BlockSpec((SGU_G, CHUNK, CHUNK), lambda n: (0, 0, 0)), pl.BlockSpec((CHUNK, SGU_G), lambda n: (0, 0))],
        out_specs=pl.BlockSpec((4, CHUNK, sw), lambda n: (0, n, 0)), out_shape=S((4, t, sw), BF16))


def _resid_mm(a_s, w, resid, extra, mode, name, hook=None):
    nk, t, kc = a_s.shape
    tm = _tm(t)
    ni = t // tm

    def body(a_ref, w_ref, r_ref, e_ref, o1_ref, o2_ref):
        h = r_ref[...]
        for j in range(nk):
            h = h + _dot(a_ref[j], w_ref[j * kc:(j + 1) * kc, :])
        if mode == "norm":
            o1_ref[...] = h
            o2_ref[...] = (h * _rstd(h) * e_ref[...]).astype(BF16)
        else:
            err = h - e_ref[...]
            o1_ref[...] = err * (1.0 / D)
            o2_ref[...] = jnp.full(o2_ref.shape, jnp.sum(err * err), F32)

    row = pl.BlockSpec((tm, D), lambda i: (i, 0))
    if mode == "norm":
        e_spec, o2_spec, o2_shape = pl.BlockSpec((1, D), lambda i: (0, 0)), row, S((t, D), BF16)
    else:
        e_spec, o2_spec, o2_shape = row, pl.BlockSpec((None, 8, 128), lambda i: (i, 0, 0)), S((ni, 8, 128), F32)
    return _run(
        body, [a_s, w, resid, extra], hook, grid=(ni,), name=name, semantics=("parallel",),
        in_specs=[pl.BlockSpec((nk, tm, kc), lambda i: (0, i, 0)), _resident(w.shape), row, e_spec],
        out_specs=[row, o2_spec], out_shape=[S((t, D), F32), o2_shape])


def _relbias_fwd(rel_bias_t, bucket_row, name):
    nb = bucket_row.shape[1]

    def body(rb_ref, bk_ref, o_ref):
        onehot = (lax.broadcasted_iota(jnp.int32, (REL_BUCKETS, nb), 0) == bk_ref[...]).astype(F32)
        o_ref[...] = jnp.dot(rb_ref[...], onehot, precision=lax.Precision.HIGHEST, preferred_element_type=F32)

    return pl.pallas_call(body, out_shape=S((NH, nb), F32), name=name)(rel_bias_t, bucket_row)


def _relbias_bwd(dbias, bucket_row, name):
    nb = bucket_row.shape[1]

    def body(db_ref, bk_ref, o_ref):
        onehot = (lax.broadcasted_iota(jnp.int32, (REL_BUCKETS, nb), 0) == bk_ref[...]).astype(F32)
        o_ref[...] = lax.dot_general(db_ref[...], onehot, (((1,), (1,)), ((), ())),
                                     precision=lax.Precision.HIGHEST, preferred_element_type=F32)

    return pl.pallas_call(body, out_shape=S((NH, REL_BUCKETS), F32), name=name)(dbias, bucket_row)


QKV_SLOT = 192


def _head(ref, col):
    return ref[col // QKV_SLOT, :, col % QKV_SLOT:col % QKV_SLOT + HD]


def _attn_valid(n):
    qi = lax.broadcasted_iota(jnp.int32, (CHUNK, 2 * CHUNK), 0)
    kj = lax.broadcasted_iota(jnp.int32, (CHUNK, 2 * CHUNK), 1)
    dist = qi + CHUNK - kj
    return (dist >= 0) & (dist < CHUNK) & ((n > 0) | (kj >= CHUNK))


def _attn_probs(qn, kn, bias, valid, sink):
    s = _dot_nt(qn, kn) * (HD ** -0.5) + bias
    s = jnp.where(valid, s, -jnp.inf)
    m = jnp.maximum(jnp.max(s, axis=-1, keepdims=True), sink)
    p = jnp.exp(s - m)
    psink = jnp.exp(sink - m)
    inv = 1.0 / (jnp.sum(p, axis=-1, keepdims=True) + psink)
    return p * inv, psink * inv


def _attn_fwd(qkv_s, qg, kg, sinks, bias, name, hook=None):
    t = qkv_s.shape[1]

    def body(cur_ref, prev_ref, qg_ref, kg_ref, sink_ref, bias_ref, o_ref):
        n = pl.program_id(0)
        valid = _attn_valid(n)
        for h in range(NKV):
            k = jnp.concatenate([_head(prev_ref, D + HD * h), _head(cur_ref, D + HD * h)], axis=0)
            v = jnp.concatenate([_head(prev_ref, D + HD * (NKV + h)), _head(cur_ref, D + HD * (NKV + h))], axis=0)
            kn = (k * _rstd(k) * kg_ref[...]).astype(BF16)
            vb = v.astype(BF16)
            outs = []
            for g in range(KVG):
                hq = KVG * h + g
                q = _head(cur_ref, HD * hq)
                qn = (q * _rstd(q) * qg_ref[...]).astype(BF16)
                p, _ = _attn_probs(qn, kn, bias_ref[hq], valid, sink_ref[hq])
                outs.append(_dot(p.astype(BF16), vb))
            o_ref[0, :, KVG * HD * h:KVG * HD * (h + 1)] = jnp.concatenate(outs, axis=1).astype(BF16)

    blk = lambda f: pl.BlockSpec((8, CHUNK, QKV_SLOT), f)
    return _run(
        body, [qkv_s, qkv_s, qg, kg, sinks, bias], hook, grid=(t // CHUNK,), name=name, semantics=("parallel",),
        in_specs=[blk(lambda n: (0, n, 0)), blk(lambda n: (0, jnp.maximum(n - 1, 0), 0)),
                  pl.BlockSpec((1, HD), lambda n: (0, 0)), pl.BlockSpec((1, HD), lambda n: (0, 0)),
                  pl.BlockSpec(memory_space=pltpu.SMEM), pl.BlockSpec((NH, CHUNK, 2 * CHUNK), lambda n: (0, 0, 0))],
        out_specs=pl.BlockSpec((1, CHUNK, D), lambda n: (0, n, 0)), out_shape=S((1, t, D), BF16))


def _dx_rows(dh, w, kc, out_dtype, name, hook=None):
    t = dh.shape[0]
    nk = w.shape[0] // kc
    tm = _tm(t)

    def body(d_ref, w_ref, o_ref):
        dhb = d_ref[...].astype(BF16)
        for j in range(nk):
            o_ref[j] = _dot_nt(dhb, w_ref[j * kc:(j + 1) * kc, :]).astype(out_dtype)

    return _run(
        body, [dh, w], hook, grid=(t // tm,), name=name, semantics=("parallel",),
        in_specs=[pl.BlockSpec((tm, D), lambda i: (i, 0)), _resident(w.shape)],
        out_specs=pl.BlockSpec((nk, tm, kc), lambda i: (0, i, 0)), out_shape=S((nk, t, kc), out_dtype))


BWD1_ROWS = 256


def _ffn_bwd1(dh, a_g, a_v, wdown, cw, cb, name, hook=None):
    nh, t, n = a_g.shape
    tm = min(BWD1_ROWS, t)
    ni = t // tm

    def body(d_ref, ag_ref, av_ref, wd_ref, cw_ref, cb_ref, dc_ref, dw_hbm, dwb_hbm, dcw_ref, dcb_ref, carry, acc, stage, keep):
        i = pl.program_id(0)

        @pl.when(i == 0)
        def _():
            carry[...] = jnp.zeros_like(carry)
            acc[...] = jnp.zeros_like(acc)
            dcw_ref[...] = jnp.zeros_like(dcw_ref)
            dcb_ref[...] = jnp.zeros_like(dcb_ref)

        dhb = d_ref[...].astype(BF16)
        rsum = lambda x: jnp.sum(x, axis=0, keepdims=True)
        for j in range(nh):
            dact = _dot_nt(dhb, wd_ref[j * n:(j + 1) * n, :])
            ag = ag_ref[j].astype(F32)
            av = av_ref[j].astype(F32)
            cg, ag1, ag2 = _conv3(ag, carry[j], cw_ref[j], cb_ref[j], tm, keep.at[0])
            cv, av1, av2 = _conv3(av, carry[nh + j], cw_ref[nh + j], cb_ref[nh + j], tm, keep.at[1])
            carry[j] = ag[tm - HALO:]
            carry[nh + j] = av[tm - HALO:]
            sg = _sigmoid(cg)
            gs = cg * sg
            acc[j * n:(j + 1) * n, :] += _dot_tn((gs * cv).astype(BF16), dhb)
            dcg = dact * cv * (sg + gs * (1.0 - sg))
            dcv = dact * gs
            dc_ref[j] = dcg.astype(BF16)
            dc_ref[nh + j] = dcv.astype(BF16)
            dcw_ref[j] += jnp.concatenate([rsum(dcg * ag2), rsum(dcg * ag1), rsum(dcg * ag)], axis=0)
            dcw_ref[nh + j] += jnp.concatenate([rsum(dcv * av2), rsum(dcv * av1), rsum(dcv * av)], axis=0)
            dcb_ref[j] += rsum(dcg)
            dcb_ref[nh + j] += rsum(dcv)

        @pl.when(i == ni - 1)
        def _():
            pltpu.sync_copy(acc, dw_hbm)
            for j in range(nh):
                stage[...] = acc[j * n:(j + 1) * n, :].astype(BF16)
                pltpu.sync_copy(stage, dwb_hbm.at[pl.ds(j * n, n), :])

    aspec = pl.BlockSpec((nh, tm, n), lambda i: (0, i, 0))
    return _run(
        body, [dh, a_g, a_v, wdown, cw, cb], hook, grid=(ni,), name=name, semantics=("arbitrary",),
        in_specs=[pl.BlockSpec((tm, D), lambda i: (i, 0)), aspec, aspec, _resident(wdown.shape), _resident(cw.shape), _resident(cb.shape)],
        out_specs=[pl.BlockSpec((2 * nh, tm, n), lambda i: (0, i, 0)), ANY, ANY,
                   pl.BlockSpec(cw.shape, lambda i: (0, 0, 0)), pl.BlockSpec(cb.shape, lambda i: (0, 0, 0))],
        out_shape=[S((2 * nh, t, n), BF16), S(wdown.shape, F32), S(wdown.shape, BF16), S(cw.shape, F32), S(cb.shape, F32)],
        scratch_shapes=[pltpu.VMEM((2 * nh, HALO, n), F32), pltpu.VMEM(wdown.shape, F32), pltpu.VMEM((n, D), BF16),
                        pltpu.VMEM((2, 2, tm, n), F32)])


def _ffn_bwd2(dc, wup, cw, h, gain, dh_in, name, hook=None):
    ns, t, n = dc.shape
    tm = _tm(t)
    ni = t // tm

    def body(dc_ref, wu_ref, cw_ref, h_ref, g_ref, di_ref, da_ref, o_ref, dg_ref, carry):
        i = pl.program_id(0)

        @pl.when(i == 0)
        def _():
            carry[...] = jnp.zeros_like(carry)
            dg_ref[...] = jnp.zeros_like(dg_ref)

        acc = jnp.zeros((tm, D), F32)
        for s in range(ns):
            x = dc_ref[s].astype(F32)
            ext = jnp.concatenate([x, carry[s]], axis=0)
            cwv = cw_ref[s]
            da = (cwv[2:3] * x + cwv[1:2] * ext[1:1 + tm] + cwv[0:1] * ext[2:2 + tm]).astype(BF16)
            carry[s] = x[:HALO]
            da_ref[s] = da
            acc = acc + _dot_nt(da, wu_ref[s])
        hv = h_ref[...]
        r = _rstd(hv)
        gg = acc * g_ref[...]
        o_ref[...] = di_ref[...] + r * gg - hv * (r * r * r * jnp.mean(gg * hv, axis=-1, keepdims=True))
        dg_ref[...] += jnp.sum(acc * hv * r, axis=0, keepdims=True)

    slab = pl.BlockSpec((ns, tm, n), lambda i: (0, ni - 1 - i, 0))
    row = pl.BlockSpec((tm, D), lambda i: (ni - 1 - i, 0))
    vec = pl.BlockSpec((1, D), lambda i: (0, 0))
    return _run(
        body, [dc, wup, cw, h, gain, dh_in], hook, grid=(ni,), name=name, semantics=("arbitrary",),
        in_specs=[slab, _resident(wup.shape), _resident(cw.shape), row, vec, row],
        out_specs=[slab, row, vec], out_shape=[S((ns, t, n), BF16), S((t, D), F32), S((1, D), F32)],
        scratch_shapes=[pltpu.VMEM((ns, HALO, n), F32)])


def _dw_slot(hn, dy_s, name, hook=None):
    t, k = hn.shape
    ns, _, n = dy_s.shape
    tm = _tm(t)

    def body(a_ref, b_ref, o_ref, ob_ref, at_ref):
        @pl.when(pl.program_id(0) == 0)
        def _():
            for i in range(t // tm):
                at_ref[:, i * tm:(i + 1) * tm] = a_ref[i * tm:(i + 1) * tm, :].T

        acc = _dot(at_ref[...], b_ref[...])
        o_ref[...] = acc
        ob_ref[...] = acc.astype(BF16)

    ospec = pl.BlockSpec((None, k, n), lambda j: (j, 0, 0))
    return _run(
        body, [hn, dy_s], hook, grid=(ns,), name=name, semantics=("arbitrary",),
        in_specs=[_resident(hn.shape), pl.BlockSpec((None, t, n), lambda j: (j, 0, 0))],
        out_specs=[ospec, ospec], out_shape=[S((ns, k, n), F32), S((ns, k, n), BF16)],
        scratch_shapes=[pltpu.VMEM((k, t), BF16)])


def _dw_rows(a_s, dh, name, hook=None):
    nk, t, kc = a_s.shape
    tm = _tm(t)
    ni = t // tm

    def body(a_ref, d_ref, o_ref, ob_ref):
        i = pl.program_id(0)
        dhb = d_ref[...].astype(BF16)

        @pl.when(i == 0)
        def _():
            o_ref[...] = jnp.zeros_like(o_ref)

        for j in range(nk):
            o_ref[j * kc:(j + 1) * kc, :] += _dot_tn(a_ref[j], dhb)

        @pl.when(i == ni - 1)
        def _():
            ob_ref[...] = o_ref[...].astype(BF16)

    ospec = pl.BlockSpec((nk * kc, D), lambda i: (0, 0))
    return _run(
        body, [a_s, dh], hook, grid=(ni,), name=name, semantics=("arbitrary",),
        in_specs=[pl.BlockSpec((nk, tm, kc), lambda i: (0, i, 0)), pl.BlockSpec((tm, D), lambda i: (i, 0))],
        out_specs=[ospec, ospec], out_shape=[S((nk * kc, D), F32), S((nk * kc, D), BF16)])


def _dx_slot_normbwd(dy_s, wg, h, gain, dh_in, name, hook=None):
    ns, t, n = dy_s.shape
    tm = _tm(t)

    def body(dy_ref, w_ref, h_ref, g_ref, di_ref, o_ref, dg_ref):
        i = pl.program_id(0)

        @pl.when(i == 0)
        def _():
            dg_ref[...] = jnp.zeros_like(dg_ref)

        g = _dot_nt(dy_ref[0], w_ref[0])
        for s in range(1, ns):
            g = g + _dot_nt(dy_ref[s], w_ref[s])
        hv = h_ref[...]
        r = _rstd(hv)
        gg = g * g_ref[...]
        o_ref[...] = di_ref[...] + r * gg - hv * (r * r * r * jnp.mean(gg * hv, axis=-1, keepdims=True))
        dg_ref[...] += jnp.sum(g * hv * r, axis=0, keepdims=True)

    row = pl.BlockSpec((tm, D), lambda i: (i, 0))
    vec = pl.BlockSpec((1, D), lambda i: (0, 0))
    return _run(
        body, [dy_s, wg, h, gain, dh_in], hook, grid=(t // tm,), name=name, semantics=("arbitrary",),
        in_specs=[pl.BlockSpec((ns, tm, n), lambda i: (0, i, 0)), _resident(wg.shape), row, vec, row],
        out_specs=[row, vec], out_shape=[S((t, D), F32), S((1, D), F32)])


def _sgu_gate_bwd(a_s, dg_s, vgain, ws, bst, name, hook=None):
    t = a_s.shape[1]
    sw = a_s.shape[2]
    gps = sw // CHUNK

    def body(a_ref, dg_ref, vg_ref, ws_ref, b_ref, da_ref, dws_ref, dbt_ref, dvg_ref, dvn_ref):
        n = pl.program_id(0)

        @pl.when(n == 0)
        def _():
            dws_ref[...] = jnp.zeros_like(dws_ref)
            dbt_ref[...] = jnp.zeros_like(dbt_ref)
            dvg_ref[...] = jnp.zeros_like(dvg_ref)

        vpre = jnp.concatenate([a_ref[4 + s].astype(F32) for s in range(4)], axis=1)
        v = _gelu(vpre)
        r = _rstd(v)
        vhat = v * r
        vn = (vhat * vg_ref[...]).astype(BF16)
        tri = _tril_mask()
        lane = lax.broadcasted_iota(jnp.int32, (CHUNK, CHUNK), 1)
        dbt = jnp.zeros((CHUNK, CHUNK), F32)
        for g in range(SGU_G):
            w = jnp.where(tri, ws_ref[g], 0.0).astype(BF16)
            vng = vn[:, g * CHUNK:(g + 1) * CHUNK]
            sg = _dot(w, vng) + b_ref[:, g:g + 1]
            lo = (g % gps) * CHUNK
            upre = a_ref[g // gps, :, lo:lo + CHUNK].astype(F32)
            dgate = dg_ref[g // gps, :, lo:lo + CHUNK].astype(F32)
            da_ref[g // gps, :, lo:lo + CHUNK] = (dgate * sg * _gelu_grad(upre)).astype(BF16)
            ds = dgate * _gelu(upre)
            dsb = ds.astype(BF16)
            dvn_ref[:, g * CHUNK:(g + 1) * CHUNK] = _dot_tn(w, dsb)
            dws_ref[g] += jnp.where(tri, _dot_nt(dsb, vng), 0.0)
            dbt = dbt + jnp.where(lane == g, jnp.sum(ds, axis=-1, keepdims=True), 0.0)
        dbt_ref[...] += dbt
        dvn = dvn_ref[...]
        dvg_ref[...] += jnp.sum(dvn * vhat, axis=0, keepdims=True)
        gg = dvn * vg_ref[...]
        dv = r * gg - v * (r * r * r * jnp.mean(gg * v, axis=-1, keepdims=True))
        dav = (dv * _gelu_grad(vpre)).astype(BF16)
        for s in range(4):
            da_ref[4 + s] = dav[:, s * sw:(s + 1) * sw]

    return _run(
        body, [a_s, dg_s, vgain, ws, bst], hook, grid=(t // CHUNK,), name=name, semantics=("arbitrary",),
        in_specs=[pl.BlockSpec((8, CHUNK, sw), lambda n: (0, n, 0)), pl.BlockSpec((4, CHUNK, sw), lambda n: (0, n, 0)),
                  pl.BlockSpec((1, SGU_W), lambda n: (0, 0)), pl.BlockSpec((SGU_G, CHUNK, CHUNK), lambda n: (0, 0, 0)),
                  pl.BlockSpec((CHUNK, SGU_G), lambda n: (0, 0))],
        out_specs=[pl.BlockSpec((8, CHUNK, sw), lambda n: (0, n, 0)), pl.BlockSpec((SGU_G, CHUNK, CHUNK), lambda n: (0, 0, 0)),
                   pl.BlockSpec((CHUNK, CHUNK), lambda n: (0, 0)), pl.BlockSpec((1, SGU_W), lambda n: (0, 0))],
        out_shape=[S((8, t, sw), BF16), S((SGU_G, CHUNK, CHUNK), F32), S((CHUNK, CHUNK), F32), S((1, SGU_W), F32)],
        scratch_shapes=[pltpu.VMEM((CHUNK, SGU_W), F32)])


def _attn_bwd(qkv_s, do, qg, kg, sinks, bias, name, hook=None):
    t = qkv_s.shape[1]
    nb = t // CHUNK

    def body(cur_ref, prev_ref, do_ref, qg_ref, kg_ref, sink_ref, bias_ref,
             o_ref, dqg_ref, dkg_ref, dsk_ref, dbias_ref, carry, top):
        n = pl.program_id(0)

        @pl.when(n == 0)
        def _():
            carry[...] = jnp.zeros_like(carry)
            dqg_ref[...] = jnp.zeros_like(dqg_ref)
            dkg_ref[...] = jnp.zeros_like(dkg_ref)
            dsk_ref[...] = jnp.zeros_like(dsk_ref)
            dbias_ref[...] = jnp.zeros_like(dbias_ref)

        @pl.when(n < nb)
        def _():
            valid = _attn_valid(n)
            top[...] = jnp.zeros_like(top)
            new = [[None] * 3 for _ in range(8)]
            lane = lax.broadcasted_iota(jnp.int32, (1, CHUNK), 1)
            dsk = jnp.zeros((1, CHUNK), F32)
            dqg = jnp.zeros((1, HD), F32)
            dkg = jnp.zeros((1, HD), F32)

            def place(col, val):
                new[col // QKV_SLOT][(col % QKV_SLOT) // HD] = val

            for h in range(NKV):
                kcol = D + HD * h
                vcol = D + HD * (NKV + h)
                k = jnp.concatenate([_head(prev_ref, kcol), _head(cur_ref, kcol)], axis=0)
                v = jnp.concatenate([_head(prev_ref, vcol), _head(cur_ref, vcol)], axis=0)
                rk = _rstd(k)
                khat = k * rk
                kn = (khat * kg_ref[...]).astype(BF16)
                vb = v.astype(BF16)
                dkn = jnp.zeros((2 * CHUNK, HD), F32)
                dv = jnp.zeros((2 * CHUNK, HD), F32)
                for g in range(KVG):
                    hq = KVG * h + g
                    q = _head(cur_ref, HD * hq)
                    rq = _rstd(q)
                    qhat = q * rq
                    qn = (qhat * qg_ref[...]).astype(BF16)
                    p, psink = _attn_probs(qn, kn, bias_ref[hq], valid, sink_ref[hq])
                    doh = do_ref[0, :, HD * hq:HD * (hq + 1)]
                    dp = _dot_nt(doh, vb)
                    dsum = jnp.sum(p * dp, axis=-1, keepdims=True)
                    ds = p * (dp - dsum)
                    dsk = dsk + jnp.where(lane == hq, jnp.sum(-psink * dsum), 0.0)
                    dbias_ref[hq] += ds
                    dv = dv + _dot_tn(p.astype(BF16), doh)
                    dsc = (ds * (HD ** -0.5)).astype(BF16)
                    dqn = _dot(dsc, kn)
                    dkn = dkn + _dot_tn(dsc, qn)
                    dqg = dqg + jnp.sum(dqn * qhat, axis=0, keepdims=True)
                    gq = dqn * qg_ref[...]
                    place(HD * hq, rq * gq - q * (rq * rq * rq * jnp.mean(gq * q, axis=-1, keepdims=True)))
                dkg = dkg + jnp.sum(dkn * khat, axis=0, keepdims=True)
                gk = dkn * kg_ref[...]
                dk = rk * gk - k * (rk * rk * rk * jnp.mean(gk * k, axis=-1, keepdims=True))
                place(kcol, dk[CHUNK:])
                place(vcol, dv[CHUNK:])
                top[kcol // QKV_SLOT, :, kcol % QKV_SLOT:kcol % QKV_SLOT + HD] = dk[:CHUNK]
                top[vcol // QKV_SLOT, :, vcol % QKV_SLOT:vcol % QKV_SLOT + HD] = dv[:CHUNK]
            dqg_ref[...] += dqg
            dkg_ref[...] += dkg
            dsk_ref[...] += dsk
            o_ref[...] = (carry[...] + top[...]).astype(BF16)
            for s in range(8):
                carry[s] = jnp.concatenate(new[s], axis=1)

        @pl.when(n == nb)
        def _():
            o_ref[...] = carry[...].astype(BF16)

    blk = lambda f: pl.BlockSpec((8, CHUNK, QKV_SLOT), f)
    cur = lambda n: (0, jnp.minimum(n, nb - 1), 0)
    prev = lambda n: (0, jnp.clip(n - 1, 0, nb - 1), 0)
    small = lambda w: pl.BlockSpec((1, w), lambda n: (0, 0))
    return _run(
        body, [qkv_s, qkv_s, do, qg, kg, sinks, bias], hook, grid=(nb + 1,), name=name, semantics=("arbitrary",),
        in_specs=[blk(cur), blk(prev), pl.BlockSpec((1, CHUNK, D), cur), small(HD), small(HD),
                  pl.BlockSpec(memory_space=pltpu.SMEM), pl.BlockSpec((NH, CHUNK, 2 * CHUNK), lambda n: (0, 0, 0))],
        out_specs=[blk(lambda n: (0, jnp.maximum(n - 1, 0), 0)), small(HD), small(HD), small(CHUNK),
                   pl.BlockSpec((NH, CHUNK, 2 * CHUNK), lambda n: (0, 0, 0))],
        out_shape=[S((8, t, QKV_SLOT), BF16), S((1, HD), F32), S((1, HD), F32), S((1, CHUNK), F32),
                   S((NH, CHUNK, 2 * CHUNK), F32)],
        scratch_shapes=[pltpu.VMEM((8, CHUNK, QKV_SLOT), F32), pltpu.VMEM((8, CHUNK, QKV_SLOT), F32)])


class _Plain:
    def __init__(self, wg):
        self.full, self.grads = wg, {}

    def w(self, n):
        return self.full[n]

    def hook(self, host):
        return None

    def grad(self, n, pair):
        self.grads[n] = pair

    def small(self, g_rep):
        pass


def _local_step(x, target, rep, sch):
    bucket_row = jnp.asarray(_rel_tables().reshape(1, -1))
    bias = _relbias_fwd(rep["rel_bias"].T, bucket_row, "relbias_fwd").reshape(NH, CHUNK, 2 * CHUNK)
    bst = rep["sgu_b_s"][0].T
    ws = rep["sgu_w_s"][0]
    vgain = rep["sgu_v_gain"]
    qg, kg, sinks = rep["attn_q_gain"], rep["attn_k_gain"], rep["attn_sinks"][0]
    w_down = lambda l: sch.w("ffn_w_down%d" % l).reshape(D_FF, D)
    w_up = lambda l: sch.w("ffn_w_up%d" % l)
    cw = [sch.w("ffn_conv_w")[:, 3 * l:3 * l + 3] for l in range(2)]
    cb = [rep["ffn_conv_b"][l].reshape(8, 1, -1) for l in range(2)]
    mixg = [rep["mix_norm"][l:l + 1] for l in range(2)]
    ffng = [rep["ffn_norm"][l:l + 1] for l in range(2)]
    rows = lambda pair: tuple(g.reshape(N_DEV, -1, D) for g in pair)
    hk = sch.hook

    hn0 = _rmsnorm(x, mixg[0], "norm0")
    a0 = _mm_slot(hn0, sch.w("sgu_w_in"), BF16, "sgu_in", hk("sgu_in"))
    gated = _sgu_gate_fwd(a0, vgain, ws, bst, "sgu_gate", hk("sgu_gate"))
    h1, hn1 = _resid_mm(gated, sch.w("sgu_w_out").reshape(SGU_W, D), x, ffng[0], "norm", "sgu_out", hk("sgu_out"))
    ag0, av0, h2, hn2 = _ffn_fwd(hn1, h1, w_up(0), w_down(0), cw[0], cb[0], mixg[1], "norm", "ffn0_fwd", hk("ffn0_fwd"))
    qkv = _mm_slot(hn2, sch.w("attn_w_qkv"), F32, "qkv", hk("qkv"))
    o = _attn_fwd(qkv, qg, kg, sinks, bias, "attn", hk("attn"))
    h3, hn3 = _resid_mm(o, sch.w("attn_w_o").reshape(D, D), h2, ffng[1], "norm", "attn_out", hk("attn_out"))
    ag1, av1, dy, sq = _ffn_fwd(hn3, h3, w_up(1), w_down(1), cw[1], cb[1], target, "loss", "ffn1_fwd_loss", hk("ffn1_fwd_loss"))
    loss = (0.5 / D) * jnp.sum(sq[:, 0, 0])

    def ffn_bwd(dh, h_in, hn, a_g, a_v, l, tag):
        dc, g_down, g_down_b, g_cw, g_cb = _ffn_bwd1(dh, a_g, a_v, w_down(l), cw[l], cb[l], tag + "_bwd1", hk(tag + "_bwd1"))
        sch.grad("ffn_w_down%d" % l, rows((g_down, g_down_b)))
        da, dh_new, dgain = _ffn_bwd2(dc, w_up(l), cw[l], h_in, ffng[l], dh, tag + "_bwd2", hk(tag + "_bwd2"))
        sch.grad("ffn_w_up%d" % l, _dw_slot(hn, da, tag + "_dw_up", hk(tag + "_dw_up")))
        return dh_new, dgain, g_cw, g_cb.reshape(-1)

    dh, d_ffng1, g_cw1, g_cb1 = ffn_bwd(dy, h3, hn3, ag1, av1, 1, "ffn1")
    do = _dx_rows(dh, sch.w("attn_w_o").reshape(D, D), D, BF16, "attn_do", hk("attn_do"))
    sch.grad("attn_w_o", rows(_dw_rows(o, dh, "dw_o", hk("dw_o"))))
    dqkv, d_qg, d_kg, d_sk, d_bias = _attn_bwd(qkv, do, qg, kg, sinks, bias, "attn_bwd", hk("attn_bwd"))
    sch.grad("attn_w_qkv", _dw_slot(hn2, dqkv, "dw_qkv", hk("dw_qkv")))
    dh, d_mixg1 = _dx_slot_normbwd(dqkv, sch.w("attn_w_qkv"), h2, mixg[1], dh, "dx_qkv", hk("dx_qkv"))
    d_relb = _relbias_bwd(d_bias.reshape(NH, -1), bucket_row, "relbias_bwd").T
    dh, d_ffng0, g_cw0, g_cb0 = ffn_bwd(dh, h1, hn1, ag0, av0, 0, "ffn0")
    g_cw = jnp.concatenate([g_cw0, g_cw1], axis=1)
    sch.grad("ffn_conv_w", (g_cw, g_cw.astype(BF16)))
    dgated = _dx_rows(dh, sch.w("sgu_w_out").reshape(SGU_W, D), SGU_W // 4, BF16, "sgu_dgated", hk("sgu_dgated"))
    sch.grad("sgu_w_out", rows(_dw_rows(gated, dh, "dw_sgu_out", hk("dw_sgu_out"))))
    da0, d_ws, d_bst, d_vgain = _sgu_gate_bwd(a0, dgated, vgain, ws, bst, "sgu_gate_bwd", hk("sgu_gate_bwd"))
    grad_x, d_mixg0 = _dx_slot_normbwd(da0, sch.w("sgu_w_in"), x, mixg[0], dh, "dx_sgu_in")
    g_rep = {
        "mix_norm": jnp.concatenate([d_mixg0, d_mixg1], axis=0),
        "ffn_norm": jnp.concatenate([d_ffng0, d_ffng1], axis=0),
        "sgu_v_gain": d_vgain,
        "sgu_w_s": d_ws[None],
        "sgu_b_s": d_bst[:, :SGU_G].T[None],
        "attn_q_gain": d_qg,
        "attn_k_gain": d_kg,
        "attn_sinks": d_sk[:, :NH],
        "rel_bias": d_relb,
        "ffn_conv_b": jnp.stack([g_cb0, g_cb1], axis=0),
    }
    sch.small(g_rep)
    sch.grad("sgu_w_in", _dw_slot(hn0, da0, "dw_sgu_in", hk("dw_sgu_in")))
    return loss, grad_x, g_rep


def _allgather(xs, name):
    nt = len(xs)

    def body(*refs):
        x_refs, o_refs = refs[:nt], refs[nt:2 * nt]
        send_sems, recv_sems, local_sems = refs[2 * nt:]
        x, y, c, chips = _place()
        me, sibling = (x, y, c), (x, y, 1 - c)

        def copy(t, k, block, to, src=None):
            px, py, pc = block
            dst = o_refs[t].at[4 * px + 2 * py + pc]
            return pltpu.make_async_remote_copy(
                src_ref=dst if src is None else src, dst_ref=dst, send_sem=send_sems.at[t, k], recv_sem=recv_sems.at[t, k],
                device_id=to, device_id_type=MESH)

        mine = [pltpu.make_async_copy(x_refs[t], o_refs[t].at[4 * x + 2 * y + c], local_sems.at[t]) for t in range(nt)]
        for cp in mine:
            cp.start()
        first = []
        for t in range(nt):
            first.append(copy(t, 0, me, sibling, src=x_refs[t]))
            first += [copy(t, 1 + j, me, (*chip, c), src=x_refs[t]) for j, chip in enumerate(chips)]
        for cp in first:
            cp.start()
        passed = []
        for j, chip in enumerate(chips):
            for t in range(nt):
                copy(t, 1 + j, (*chip, c), me).wait_recv()
                fwd = copy(t, 4 + j, (*chip, c), sibling)
                fwd.start()
                passed.append(fwd)
        for t in range(nt):
            copy(t, 0, sibling, me).wait_recv()
            for j, chip in enumerate(chips):
                copy(t, 4 + j, (*chip, 1 - c), me).wait_recv()
        for cp in first + passed:
            cp.wait_send()
        for cp in mine:
            cp.wait()

    return pl.pallas_call(
        body, name=name, in_specs=[ANY] * nt, out_specs=[ANY] * nt,
        out_shape=[S((N_DEV,) + a.shape, a.dtype) for a in xs],
        scratch_shapes=[pltpu.SemaphoreType.DMA((nt, 7)), pltpu.SemaphoreType.DMA((nt, 7)), pltpu.SemaphoreType.DMA((nt,))],
        compiler_params=pltpu.CompilerParams(has_side_effects=True))(*xs)


def _exchange(hook, name):
    comm = hook()
    ci, co = len(comm.inputs), len(comm.out_shapes)

    def body(*refs):
        cins, couts = refs[:ci], refs[ci:ci + co]
        send, recv = refs[-2:]
        comm.start(cins, couts, send, recv)
        comm.finish(cins, couts, send, recv)

    res = pl.pallas_call(
        body, name=name, in_specs=[ANY] * ci, out_specs=[ANY] * co, out_shape=comm.out_shapes,
        scratch_shapes=[pltpu.SemaphoreType.DMA((comm.n_sems,)), pltpu.SemaphoreType.DMA((comm.n_sems,))],
        input_output_aliases=dict(comm.aliases),
        compiler_params=pltpu.CompilerParams(has_side_effects=True))(*comm.inputs)
    hook(res)


def _row_tile(r):
    tr = r if r <= ROW_TILE or r % ROW_TILE else ROW_TILE
    assert r % tr == 0
    return tr


def _rs_partial(g32, sib, place, name):
    _, r, cdim = g32.shape
    tr = _row_tile(r)

    def body(place_ref, g_ref, s_ref, p_ref, own_ref):
        k = pl.program_id(1)
        tot = g_ref[...] + s_ref[...].astype(F32)
        p_ref[...] = tot.astype(BF16)

        @pl.when(k == place_ref[1])
        def _():
            own_ref[...] = tot

    grid_spec = pltpu.PrefetchScalarGridSpec(
        num_scalar_prefetch=1, grid=(r // tr, 4),
        in_specs=[pl.BlockSpec((None, None, tr, cdim), lambda i, k, pr: (k, pr[0], i, 0)),
                  pl.BlockSpec((None, tr, cdim), lambda i, k, pr: (k, i, 0))],
        out_specs=[pl.BlockSpec((None, tr, cdim), lambda i, k, pr: (k, i, 0)), pl.BlockSpec((tr, cdim), lambda i, k, pr: (i, 0))])
    return pl.pallas_call(
        body, grid_spec=grid_spec, name=name,
        out_shape=[S((4, r, cdim), BF16), S((r, cdim), F32)],
        compiler_params=_cp("parallel", "arbitrary"))(place, g32.reshape(4, 2, r, cdim), sib)


def _adamw_math(w, g, m, v):
    m = ADAM_B1 * m + (1.0 - ADAM_B1) * g
    v = ADAM_B2 * v + (1.0 - ADAM_B2) * (g * g)
    m_hat = m / (1.0 - ADAM_B1 ** ADAM_STEP)
    v_hat = v / (1.0 - ADAM_B2 ** ADAM_STEP)
    delta = -ADAM_LR * (m_hat / (jnp.sqrt(v_hat) + ADAM_EPS) + ADAM_WD * w)
    return delta, m, v


def _adamw_shard(owns, recvs, w, m, v, name):
    nl, r, cdim = w.shape
    tr = _row_tile(r)
    nr = r // tr

    def body(*refs):
        own_refs, recv_refs = refs[:nl], refs[nl:2 * nl]
        w_ref, m_ref, v_ref, g_out, d_out, m_out, v_out = refs[2 * nl:]
        layer = pl.program_id(0)
        g = None
        for l in range(nl):
            gl = own_refs[l][...] + recv_refs[l][0].astype(F32) + recv_refs[l][1].astype(F32) + recv_refs[l][2].astype(F32)
            g = gl if g is None else jnp.where(layer == l, gl, g)
        g_out[...] = g
        d_out[...], m_out[...], v_out[...] = _adamw_math(w_ref[...], g, m_ref[...], v_ref[...])

    park = lambda l: (lambda layer, i: (jnp.where(layer == l, i, jnp.where(layer < l, 0, nr - 1)), 0))
    park3 = lambda l: (lambda layer, i: (0, jnp.where(layer == l, i, jnp.where(layer < l, 0, nr - 1)), 0))
    row = pl.BlockSpec((None, tr, cdim), lambda layer, i: (layer, i, 0))
    return pl.pallas_call(
        body, grid=(nl, nr), name=name,
        in_specs=[pl.BlockSpec((tr, cdim), park(l)) for l in range(nl)] + [pl.BlockSpec((3, tr, cdim), park3(l)) for l in range(nl)]
        + [row, row, row],
        out_specs=[row] * 4, out_shape=[S((nl, r, cdim), F32)] * 4,
        compiler_params=_cp("arbitrary", "arbitrary"))(*owns, *recvs, w, m, v)


def _adamw_small(galls, ws, ms, vs, name):
    n = len(galls)

    def body(*refs):
        g_refs, w_refs, m_refs, v_refs, outs = refs[:n], refs[n:2 * n], refs[2 * n:3 * n], refs[3 * n:4 * n], refs[4 * n:]
        for i in range(n):
            g = g_refs[i][0]
            for s in range(1, N_DEV):
                g = g + g_refs[i][s]
            outs[i][...] = g
            outs[n + i][...], outs[2 * n + i][...], outs[3 * n + i][...] = _adamw_math(w_refs[i][...], g, m_refs[i][...], v_refs[i][...])

    res = pl.pallas_call(body, out_shape=[S(a.shape, F32) for a in ws] * 4, name=name)(*galls, *ws, *ms, *vs)
    return [res[k * n:(k + 1) * n] for k in range(4)]


REPLICATED = ["mix_norm", "ffn_norm", "sgu_v_gain", "sgu_w_s", "sgu_b_s", "attn_q_gain", "attn_k_gain", "attn_sinks", "rel_bias",
              "ffn_conv_b"]
WEIGHTS = ["mix_norm", "ffn_norm", "sgu_w_in", "sgu_v_gain", "sgu_w_s", "sgu_b_s", "sgu_w_out", "attn_w_qkv", "attn_q_gain",
           "attn_k_gain", "attn_sinks", "attn_w_o", "rel_bias", "ffn_w_up", "ffn_conv_w", "ffn_conv_b", "ffn_w_down"]
SMALL = ["g_" + n for n in REPLICATED]

GATHER_FIRST = ["sgu_w_in", "sgu_w_out", "ffn_conv_w"]
PLAN = {
    "sgu_in": [("ag1", "ffn_w_up0")],
    "sgu_gate": [("ag2", "ffn_w_up0"), ("ag1", "ffn_w_down0")],
    "sgu_out": [("ag2", "ffn_w_down0"), ("ag1", "attn_w_qkv")],
    "ffn0_fwd": [("ag2", "attn_w_qkv"), ("ag1", "attn_w_o"), ("ag1", "ffn_w_up1")],
    "qkv": [("ag2", "attn_w_o"), ("ag2", "ffn_w_up1")],
    "attn": [("ag1", "ffn_w_down1")],
    "attn_out": [("ag2", "ffn_w_down1")],
    "ffn1_bwd2": [("rs1", "ffn_w_down1")],
    "ffn1_dw_up": [("rs2", "ffn_w_down1")],
    "attn_do": [("rs1", "ffn_w_up1")],
    "attn_bwd": [("rs2", "ffn_w_up1"), ("rs1", "attn_w_o")],
    "dw_qkv": [("rs2", "attn_w_o")],
    "dx_qkv": [("rs1", "attn_w_qkv")],
    "ffn0_bwd1": [("rs2", "attn_w_qkv")],
    "ffn0_bwd2": [("rs1", "ffn_w_down0")],
    "ffn0_dw_up": [("rs2", "ffn_w_down0")],
    "sgu_dgated": [("rs1", "ffn_w_up0")],
    "sgu_gate_bwd": [("rs2", "ffn_w_up0"), ("rs1", "sgu_w_out")],
    "dw_sgu_in": [("rs2", "sgu_w_out")] + [("ag1", n) for n in SMALL],
    "last_a": [("rs1", "sgu_w_in"), ("rs1", "ffn_conv_w")] + [("ag2", n) for n in SMALL],
    "last_b": [("rs2", "sgu_w_in"), ("rs2", "ffn_conv_w")],
}


class _Overlap:
    def __init__(self, shard, place):
        self.shard, self.place = shard, place
        self.part, self.full = {}, {}
        self.grads, self.sib, self.own, self.recv = {}, {}, {}, {}

    def w(self, n):
        return self.full[n]

    def grad(self, n, pair):
        self.grads[n] = pair

    def small(self, g_rep):
        self.shard.update(("g_" + n, a) for n, a in _views2d(g_rep).items())

    def chip_sums(self, n):
        sums, self.own[n] = _rs_partial(self.grads[n][0], self.sib.pop(n), self.place, "rs_partial_" + n)
        return sums

    def hook(self, host):
        ops = PLAN.get(host)
        if not ops:
            return None
        where = {"ag1": self.part, "ag2": self.full, "rs1": self.sib, "rs2": self.recv}
        idx = []

        def hook(results=None):
            if results is not None:
                for (kind, n), i in zip(ops, idx):
                    where[kind][n] = results[i]
                return None
            comm = _Comm()
            for kind, n in ops:
                arr = {"ag1": lambda: self.shard[n], "ag2": lambda: self.part.pop(n), "rs1": lambda: self.grads[n][1],
                       "rs2": lambda: self.chip_sums(n)}[kind]()
                idx.append(comm.add(kind, arr))
            return comm

        return hook


SHARDED = {
    "sgu_w_in": ["sgu_w_in"], "sgu_w_out": ["sgu_w_out"], "attn_w_qkv": ["attn_w_qkv"], "attn_w_o": ["attn_w_o"],
    "ffn_w_up": ["ffn_w_up0", "ffn_w_up1"], "ffn_w_down": ["ffn_w_down0", "ffn_w_down1"], "ffn_conv_w": ["ffn_conv_w"],
}


def _send_views(w):
    out = {"ffn_conv_w": w["ffn_conv_w"].reshape(6, -1)}
    for name, parts in SHARDED.items():
        if name != "ffn_conv_w":
            out.update((p, w[name][l].astype(BF16)) for l, p in enumerate(parts))
    return out


def _views2d(d):
    return {n: d[n].reshape(-1, d[n].shape[-1]) for n in REPLICATED}


def kernel(x, mix_norm, ffn_norm, sgu_w_in, sgu_v_gain, sgu_w_s, sgu_b_s, sgu_w_out, attn_w_qkv, attn_q_gain, attn_k_gain, attn_sinks, attn_w_o, rel_bias, ffn_w_up, ffn_conv_w, ffn_conv_b, ffn_w_down, loss_target, m_mix_norm, m_ffn_norm, m_sgu_w_in, m_sgu_v_gain, m_sgu_w_s, m_sgu_b_s, m_sgu_w_out, m_attn_w_qkv, m_attn_q_gain, m_attn_k_gain, m_attn_sinks, m_attn_w_o, m_rel_bias, m_ffn_w_up, m_ffn_conv_w, m_ffn_conv_b, m_ffn_w_down, v_mix_norm, v_ffn_norm, v_sgu_w_in, v_sgu_v_gain, v_sgu_w_s, v_sgu_b_s, v_sgu_w_out, v_attn_w_qkv, v_attn_q_gain, v_attn_k_gain, v_attn_sinks, v_attn_w_o, v_rel_bias, v_ffn_w_up, v_ffn_conv_w, v_ffn_conv_b, v_ffn_w_down):
    w = dict(zip(WEIGHTS, (mix_norm, ffn_norm, sgu_w_in, sgu_v_gain, sgu_w_s, sgu_b_s, sgu_w_out, attn_w_qkv, attn_q_gain, attn_k_gain,
                           attn_sinks, attn_w_o, rel_bias, ffn_w_up, ffn_conv_w, ffn_conv_b, ffn_w_down)))
    m = dict(zip(WEIGHTS, (m_mix_norm, m_ffn_norm, m_sgu_w_in, m_sgu_v_gain, m_sgu_w_s, m_sgu_b_s, m_sgu_w_out, m_attn_w_qkv, m_attn_q_gain,
                           m_attn_k_gain, m_attn_sinks, m_attn_w_o, m_rel_bias, m_ffn_w_up, m_ffn_conv_w, m_ffn_conv_b, m_ffn_w_down)))
    v = dict(zip(WEIGHTS, (v_mix_norm, v_ffn_norm, v_sgu_w_in, v_sgu_v_gain, v_sgu_w_s, v_sgu_b_s, v_sgu_w_out, v_attn_w_qkv, v_attn_q_gain,
                           v_attn_k_gain, v_attn_sinks, v_attn_w_o, v_rel_bias, v_ffn_w_up, v_ffn_conv_w, v_ffn_conv_b, v_ffn_w_down)))
    rep = {n: w[n] for n in REPLICATED}

    xi, yi, ci = lax.axis_index("x"), lax.axis_index("y"), lax.axis_index("c")
    place = jnp.stack([ci, 2 * xi + yi]).astype(jnp.int32)
    sch = _Overlap(_send_views(w), place)
    sch.full.update(zip(GATHER_FIRST, _allgather([sch.shard[n] for n in GATHER_FIRST], "gather_first")))

    loss, grad_x, g_rep = _local_step(x[0], loss_target[0], rep, sch)
    loss = lax.psum(loss, ("x", "y", "c"))
    _exchange(sch.hook("last_a"), "last_a")
    _exchange(sch.hook("last_b"), "last_b")

    out = [{}, {}, {}, {}]
    for name, parts in SHARDED.items():
        shape = w[name].shape
        as3d = lambda a: a.reshape(len(parts), -1, shape[-1])
        res = _adamw_shard([sch.own[p] for p in parts], [sch.recv[p] for p in parts], as3d(w[name]), as3d(m[name]), as3d(v[name]),
                           "adamw_" + name)
        for o, r in zip(out, res):
            o[name] = r.reshape(shape)
    small = _adamw_small([sch.full[n] for n in SMALL], *[list(_views2d(d).values()) for d in (rep, m, v)], "adamw_small")
    for o, res in zip(out, small):
        o.update((n, r.reshape(w[n].shape)) for n, r in zip(REPLICATED, res))

    return (loss, grad_x[None], *[out[0][n] for n in WEIGHTS], *[out[1][n] for n in WEIGHTS],
            *[out[2][n] for n in WEIGHTS], *[out[3][n] for n in WEIGHTS])
```

```python
import functools
import math

import numpy as np
import jax
import jax.numpy as jnp
from jax import lax
from jax.experimental import pallas as pl
from jax.experimental.pallas import tpu as pltpu

F32 = jnp.float32
BF16 = jnp.bfloat16
S = jax.ShapeDtypeStruct

D = 1024
CHUNK = 128
SGU_W = 2048
SGU_G = 16
HD = 64
NH = 16
NKV = 4
KVG = 4
D_FF = 2816
REL_BUCKETS = 32
REL_MAX_DIST = 128
EPS = 1e-6
N_DEV = 8
MESH = pl.DeviceIdType.MESH

ADAM_LR = 0.001
ADAM_B1 = 0.9
ADAM_B2 = 0.999
ADAM_EPS = 1e-08
ADAM_WD = 0.01
ADAM_STEP = 10

ROW_TILE = 512
HALO = 8


def _tm(t):
    return min(ROW_TILE, t)


def _cp(*sem):
    return pltpu.CompilerParams(dimension_semantics=sem)


ANY = pl.BlockSpec(memory_space=pl.ANY)


def _place():
    x, y, c = lax.axis_index("x"), lax.axis_index("y"), lax.axis_index("c")
    return x, y, c, [(1 - x, y), (x, 1 - y), (1 - x, 1 - y)]


class _Comm:
    SEMS = {"ag1": 5, "ag2": 3, "rs1": 4, "rs2": 3}

    def __init__(self):
        self.inputs, self.out_shapes, self.aliases, self.ops, self.n_sems = [], [], {}, [], 0

    def add(self, kind, arr):
        lead = {"ag1": N_DEV, "ag2": None, "rs1": 4, "rs2": 3}[kind]
        shape = arr.shape if lead is None else (lead,) + arr.shape[(0 if kind == "ag1" else 1):]
        if kind == "ag2":
            self.aliases[len(self.inputs)] = len(self.out_shapes)
        self.ops.append((kind, len(self.inputs), len(self.out_shapes), self.n_sems))
        self.inputs.append(arr)
        self.out_shapes.append(S(shape, arr.dtype))
        self.n_sems += self.SEMS[kind]
        return len(self.out_shapes) - 1

    def _copies(self, ins, outs, send, recv):
        x, y, c, chips = _place()
        me, sibling = (x, y, c), (x, y, 1 - c)
        slot = lambda px, py, pc: 4 * px + 2 * py + pc
        sends, recvs, local = [], [], []

        def rc(src, dst, k, to):
            return lambda: pltpu.make_async_remote_copy(src_ref=src(), dst_ref=dst(), send_sem=send.at[k], recv_sem=recv.at[k],
                                                        device_id=to, device_id_type=MESH)

        for kind, ii, oi, b in self.ops:
            src, dst = ins[ii], outs[oi]
            at = lambda ref, i: (lambda: ref.at[i])
            if kind == "ag1":
                whole, mine = (lambda s=src: s), at(dst, slot(*me))
                sends.append(rc(whole, mine, b, sibling))
                recvs.append(rc(whole, at(dst, slot(x, y, 1 - c)), b, me))
                for j, chip in enumerate(chips):
                    sends.append(rc(whole, mine, b + 1 + j, (*chip, c)))
                    recvs.append(rc(whole, at(dst, slot(*chip, c)), b + 1 + j, me))
                local.append(lambda s=src, m=mine, k=b + 4: pltpu.make_async_copy(s, m(), send.at[k]))
            elif kind == "ag2":
                for j, chip in enumerate(chips):
                    sends.append(rc(at(dst, slot(*chip, c)), at(dst, slot(*chip, c)), b + j, sibling))
                    recvs.append(rc(at(dst, slot(*chip, 1 - c)), at(dst, slot(*chip, 1 - c)), b + j, me))
            elif kind == "rs1":
                for k in range(4):
                    sends.append(rc(at(src, 2 * k + (1 - c)), at(dst, k), b + k, sibling))
                    recvs.append(rc(at(src, 2 * k + c), at(dst, k), b + k, me))
            else:
                for j, (px, py) in enumerate(chips):
                    sends.append(rc(at(src, 2 * px + py), at(dst, j), b + j, (px, py, c)))
                    recvs.append(rc(at(src, 2 * px + py), at(dst, j), b + j, me))
        return sends, recvs, local

    def start(self, ins, outs, send, recv):
        sends, _, local = self._copies(ins, outs, send, recv)
        for make in local + sends:
            make().start()

    def finish(self, ins, outs, send, recv):
        sends, recvs, local = self._copies(ins, outs, send, recv)
        for make in recvs:
            make().wait_recv()
        for make in sends:
            make().wait_send()
        for make in local:
            make().wait()


def _run(body, args, hook, *, grid, in_specs, out_specs, out_shape, name, semantics, scratch_shapes=()):
    comm = hook() if hook is not None else None
    if comm is None:
        return pl.pallas_call(body, grid=grid, in_specs=in_specs, out_specs=out_specs, out_shape=out_shape, name=name,
                              scratch_shapes=list(scratch_shapes), compiler_params=_cp(*semantics))(*args)
    single = not isinstance(out_shape, (list, tuple))
    out_shapes = [out_shape] if single else list(out_shape)
    out_specs_l = [out_specs] if single else list(out_specs)
    n_in, n_out, n_scr, ci, co = len(args), len(out_shapes), len(scratch_shapes), len(comm.inputs), len(comm.out_shapes)

    def wrapped(*refs):
        ins, cins = refs[:n_in], refs[n_in:n_in + ci]
        outs, couts = refs[n_in + ci:n_in + ci + n_out], refs[n_in + ci + n_out:n_in + ci + n_out + co]
        scr = refs[n_in + ci + n_out + co:n_in + ci + n_out + co + n_scr]
        send, recv = refs[-2:]
        first = functools.reduce(lambda a, b: a & b, [pl.program_id(a) == 0 for a in range(len(grid))])
        last = functools.reduce(lambda a, b: a & b, [pl.program_id(a) == g - 1 for a, g in enumerate(grid)])

        @pl.when(first)
        def _():
            comm.start(cins, couts, send, recv)

        body(*ins, *outs, *scr)

        @pl.when(last)
        def _():
            comm.finish(cins, couts, send, recv)

    res = pl.pallas_call(
        wrapped, grid=grid, in_specs=list(in_specs) + [ANY] * ci, out_specs=out_specs_l + [ANY] * co,
        out_shape=out_shapes + comm.out_shapes, name=name,
        scratch_shapes=list(scratch_shapes) + [pltpu.SemaphoreType.DMA((comm.n_sems,)), pltpu.SemaphoreType.DMA((comm.n_sems,))],
        input_output_aliases={n_in + k: n_out + v for k, v in comm.aliases.items()},
        compiler_params=pltpu.CompilerParams(dimension_semantics=("arbitrary",) * len(grid), has_side_effects=True))(*args, *comm.inputs)
    hook(res[n_out:])
    return res[0] if single else list(res[:n_out])


def _dot(a, b):
    return jnp.dot(a, b, preferred_element_type=F32)


def _dot_nt(a, b):
    return lax.dot_general(a, b, (((1,), (1,)), ((), ())), preferred_element_type=F32)


def _dot_tn(a, b):
    return lax.dot_general(a, b, (((0,), (0,)), ((), ())), preferred_element_type=F32)


def _gelu(x):
    return 0.5 * x * (1.0 + lax.erf(x * (2.0 ** -0.5)))


def _gelu_grad(x):
    return 0.5 * (1.0 + lax.erf(x * (2.0 ** -0.5))) + x * jnp.exp(-0.5 * x * x) * (1.0 / math.sqrt(2.0 * math.pi))


def _sigmoid(x):
    return 1.0 / (1.0 + jnp.exp(-x))


def _rstd(x):
    return lax.rsqrt(jnp.mean(x * x, axis=-1, keepdims=True) + EPS)


def _rel_tables():
    q = np.arange(CHUNK)[:, None] + CHUNK
    k = np.arange(2 * CHUNK)[None, :]
    dist = q - k
    n = np.maximum(dist, 0)
    max_exact = REL_BUCKETS // 2
    large = max_exact + (np.log(np.maximum(n, 1).astype(np.float32) / max_exact)
                         / math.log(REL_MAX_DIST / max_exact) * (REL_BUCKETS - max_exact)).astype(np.int32)
    large = np.minimum(large, REL_BUCKETS - 1)
    return np.where(n < max_exact, n, large).astype(np.int32)


def _rmsnorm(x, gain, name):
    t = x.shape[0]
    tm = _tm(t)

    def body(x_ref, g_ref, o_ref):
        xv = x_ref[...]
        o_ref[...] = (xv * _rstd(xv) * g_ref[...]).astype(BF16)

    return pl.pallas_call(
        body, grid=(t // tm,), name=name,
        in_specs=[pl.BlockSpec((tm, D), lambda i: (i, 0)), pl.BlockSpec((1, D), lambda i: (0, 0))],
        out_specs=pl.BlockSpec((tm, D), lambda i: (i, 0)),
        out_shape=S((t, D), BF16), compiler_params=_cp("parallel"))(x, gain)


def _resident(shape):
    zeros = (0,) * len(shape)
    return pl.BlockSpec(shape, lambda *_: zeros, pipeline_mode=pl.Buffered(1))


def _mm_slot(hn, wg, out_dtype, name, hook=None):
    t, k = hn.shape
    ns, _, n = wg.shape
    tm = _tm(t)

    def body(a_ref, w_ref, o_ref):
        a = a_ref[...]
        for s in range(ns):
            o_ref[s] = _dot(a, w_ref[s]).astype(out_dtype)

    return _run(
        body, [hn, wg], hook, grid=(t // tm,), name=name, semantics=("parallel",),
        in_specs=[pl.BlockSpec((tm, k), lambda i: (i, 0)), _resident(wg.shape)],
        out_specs=pl.BlockSpec((ns, tm, n), lambda i: (0, i, 0)), out_shape=S((ns, t, n), out_dtype))


def _mm_t(hn, wt, name, hook=None):
    t, k = hn.shape
    ns, n, _ = wt.shape
    tm = _tm(t)

    def body(a_ref, w_ref, o_ref):
        a = a_ref[...]
        for s in range(ns):
            o_ref[s * n:(s + 1) * n, :] = _dot_nt(w_ref[s], a)

    return _run(
        body, [hn, wt], hook, grid=(t // tm,), name=name, semantics=("parallel",),
        in_specs=[pl.BlockSpec((tm, k), lambda i: (i, 0)), _resident(wt.shape)],
        out_specs=pl.BlockSpec((ns * n, tm), lambda i: (0, i)), out_shape=S((ns * n, t), F32))


def _conv3(a, prev, cw, cb, tm, keep=None):
    ext = jnp.concatenate([prev, a], axis=0)
    a1 = ext[HALO - 1:HALO - 1 + tm]
    a2 = ext[HALO - 2:HALO - 2 + tm]
    if keep is not None:
        keep[0] = a1
        keep[1] = a2
        a1, a2 = keep[0], keep[1]
    return cw[2:3] * a + cw[1:2] * a1 + cw[0:1] * a2 + cb, a1, a2


def _ffn_fwd(hn, h, wup, wdown, cw, cb, extra, mode, name, hook=None):
    t, k = hn.shape
    n = wup.shape[-1]
    nh = wup.shape[0] // 2
    tm = _tm(t)
    ni = t // tm

    def body(a_ref, h_ref, wu_ref, wd_ref, cw_ref, cb_ref, e_ref, ag_ref, av_ref, o1_ref, o2_ref, carry):
        i = pl.program_id(0)

        @pl.when(i == 0)
        def _():
            carry[...] = jnp.zeros_like(carry)

        a = a_ref[...]
        acc = h_ref[...]
        for j in range(nh):
            ag = _dot(a, wu_ref[j])
            av = _dot(a, wu_ref[nh + j])
            ag_ref[j] = ag.astype(BF16)
            av_ref[j] = av.astype(BF16)
            cg, _, _ = _conv3(ag, carry[j], cw_ref[j], cb_ref[j], tm)
            cv, _, _ = _conv3(av, carry[nh + j], cw_ref[nh + j], cb_ref[nh + j], tm)
            carry[j] = ag[tm - HALO:]
            carry[nh + j] = av[tm - HALO:]
            act = (cg * _sigmoid(cg) * cv).astype(BF16)
            acc = acc + _dot(act, wd_ref[j * n:(j + 1) * n, :])
        if mode == "norm":
            o1_ref[...] = acc
            o2_ref[...] = (acc * _rstd(acc) * e_ref[...]).astype(BF16)
        else:
            err = acc - e_ref[...]
            o1_ref[...] = err * (1.0 / D)
            o2_ref[...] = jnp.full(o2_ref.shape, jnp.sum(err * err), F32)

    row = pl.BlockSpec((tm, D), lambda i: (i, 0))
    if mode == "norm":
        e_spec, o2_spec, o2_shape = pl.BlockSpec((1, D), lambda i: (0, 0)), row, S((t, D), BF16)
    else:
        e_spec, o2_spec, o2_shape = row, pl.BlockSpec((None, 8, 128), lambda i: (i, 0, 0)), S((ni, 8, 128), F32)
    aspec = pl.BlockSpec((nh, tm, n), lambda i: (0, i, 0))
    return _run(
        body, [hn, h, wup, wdown, cw, cb, extra], hook, grid=(ni,), name=name, semantics=("arbitrary",),
        in_specs=[pl.BlockSpec((tm, k), lambda i: (i, 0)), row, _resident(wup.shape), _resident(wdown.shape),
                  _resident(cw.shape), _resident(cb.shape), e_spec],
        out_specs=[aspec, aspec, row, o2_spec],
        out_shape=[S((nh, t, n), BF16), S((nh, t, n), BF16), S((t, D), F32), o2_shape],
        scratch_shapes=[pltpu.VMEM((2 * nh, HALO, n), F32)])


def _tril_mask():
    r = lax.broadcasted_iota(jnp.int32, (CHUNK, CHUNK), 0)
    c = lax.broadcasted_iota(jnp.int32, (CHUNK, CHUNK), 1)
    return r >= c


def _sgu_gate_fwd(a_s, vgain, ws, bst, name, hook=None):
    t = a_s.shape[1]
    sw = a_s.shape[2]
    gps = sw // CHUNK

    def body(a_ref, vg_ref, ws_ref, b_ref, o_ref):
        v = _gelu(jnp.concatenate([a_ref[4 + s].astype(F32) for s in range(4)], axis=1))
        vn = (v * _rstd(v) * vg_ref[...]).astype(BF16)
        tri = _tril_mask()
        for g in range(SGU_G):
            w = jnp.where(tri, ws_ref[g], 0.0).astype(BF16)
            sg = _dot(w, vn[:, g * CHUNK:(g + 1) * CHUNK]) + b_ref[:, g:g + 1]
            lo = (g % gps) * CHUNK
            u = _gelu(a_ref[g // gps, :, lo:lo + CHUNK].astype(F32))
            o_ref[g // gps, :, lo:lo + CHUNK] = (u * sg).astype(BF16)

    return _run(
        body, [a_s, vgain, ws, bst], hook, grid=(t // CHUNK,), name=name, semantics=("parallel",),
        in_specs=[pl.BlockSpec((8, CHUNK, sw), lambda n: (0, n, 0)), pl.BlockSpec((1, SGU_W), lambda n: (0, 0)),
                  pl.BlockSpec((SGU_G, CHUNK, CHUNK), lambda n: (0, 0, 0)), pl.BlockSpec((CHUNK, SGU_G), lambda n: (0, 0))],
        out_specs=pl.BlockSpec((4, CHUNK, sw), lambda n: (0, n, 0)), out_shape=S((4, t, sw), BF16))


def _resid_mm(a_s, w, resid, extra, mode, name, hook=None, fm=False):
    nk, t, kc = (1, a_s.shape[1], a_s.shape[0]) if fm else a_s.shape
    tm = _tm(t)
    ni = t // tm

    def body(a_ref, w_ref, r_ref, e_ref, o1_ref, o2_ref):
        h = r_ref[...]
        if fm:
            h = h + _dot_tn(a_ref[...], w_ref[...])
        for j in range(0 if fm else nk):
            h = h + _dot(a_ref[j], w_ref[j * kc:(j + 1) * kc, :])
        if mode == "norm":
            o1_ref[...] = h
            o2_ref[...] = (h * _rstd(h) * e_ref[...]).astype(BF16)
        else:
            err = h - e_ref[...]
            o1_ref[...] = err * (1.0 / D)
            o2_ref[...] = jnp.full(o2_ref.shape, jnp.sum(err * err), F32)

    row = pl.BlockSpec((tm, D), lambda i: (i, 0))
    if mode == "norm":
        e_spec, o2_spec, o2_shape = pl.BlockSpec((1, D), lambda i: (0, 0)), row, S((t, D), BF16)
    else:
        e_spec, o2_spec, o2_shape = row, pl.BlockSpec((None, 8, 128), lambda i: (i, 0, 0)), S((ni, 8, 128), F32)
    return _run(
        body, [a_s, w, resid, extra], hook, grid=(ni,), name=name, semantics=("parallel",),
        in_specs=[pl.BlockSpec((kc, tm), lambda i: (0, i)) if fm else pl.BlockSpec((nk, tm, kc), lambda i: (0, i, 0)),
                  _resident(w.shape), row, e_spec],
        out_specs=[row, o2_spec], out_shape=[S((t, D), F32), o2_shape])


def _relbias_fwd(rel_bias_t, bucket_row, name):
    nb = bucket_row.shape[1]

    def body(rb_ref, bk_ref, o_ref):
        onehot = (lax.broadcasted_iota(jnp.int32, (REL_BUCKETS, nb), 0) == bk_ref[...]).astype(F32)
        o_ref[...] = jnp.dot(rb_ref[...], onehot, precision=lax.Precision.HIGHEST, preferred_element_type=F32)

    return pl.pallas_call(body, out_shape=S((NH, nb), F32), name=name)(rel_bias_t, bucket_row)


def _relbias_bwd(dbias, bucket_row, name):
    nb = bucket_row.shape[1]

    def body(db_ref, bk_ref, o_ref):
        onehot = (lax.broadcasted_iota(jnp.int32, (REL_BUCKETS, nb), 0) == bk_ref[...]).astype(F32)
        o_ref[...] = lax.dot_general(db_ref[...], onehot, (((1,), (1,)), ((), ())),
                                     precision=lax.Precision.HIGHEST, preferred_element_type=F32)

    return pl.pallas_call(body, out_shape=S((NH, REL_BUCKETS), F32), name=name)(dbias, bucket_row)


QKV = D + 2 * NKV * HD
KV0 = D


def _rstd_rows(x):
    return lax.rsqrt(jnp.mean(x * x, axis=0, keepdims=True) + EPS)


def _attn_valid(n):
    kj = lax.broadcasted_iota(jnp.int32, (2 * CHUNK, CHUNK), 0)
    qi = lax.broadcasted_iota(jnp.int32, (2 * CHUNK, CHUNK), 1)
    dist = qi + CHUNK - kj
    return (dist >= 0) & (dist < CHUNK) & ((n > 0) | (kj >= CHUNK))


def _attn_band(cur_ref, prev_ref, row):
    return jnp.concatenate([prev_ref[row - KV0:row - KV0 + HD, :], cur_ref[row:row + HD, :]], axis=1)


def _attn_probs(kn_tok, qn, bias, valid, sink):
    s = _dot(kn_tok, qn) * (HD ** -0.5) + bias
    s = jnp.where(valid, s, -jnp.inf)
    m = jnp.maximum(jnp.max(s, axis=0, keepdims=True), sink)
    p = jnp.exp(s - m)
    psink = jnp.exp(sink - m)
    inv = 1.0 / (jnp.sum(p, axis=0, keepdims=True) + psink)
    return p * inv, psink * inv


def _attn_fwd(qkv_t, qg, kg, sinks, bias, name, hook=None):
    t = qkv_t.shape[1]

    def body(cur_ref, prev_ref, qg_ref, kg_ref, sink_ref, bias_ref, o_ref):
        n = pl.program_id(0)
        valid = _attn_valid(n)
        for h in range(NKV):
            k = _attn_band(cur_ref, prev_ref, KV0 + HD * h)
            v = _attn_band(cur_ref, prev_ref, KV0 + HD * (NKV + h))
            kn_tok = (k * _rstd_rows(k) * kg_ref[...]).astype(BF16).T
            vb = v.astype(BF16)
            heads = range(KVG * h, KVG * (h + 1))
            qs = [cur_ref[HD * hq:HD * (hq + 1), :] for hq in heads]
            qns = [(q * _rstd_rows(q) * qg_ref[...]).astype(BF16) for q in qs]
            ps = [_attn_probs(kn_tok, qn, bias_ref[hq], valid, sink_ref[hq])[0] for qn, hq in zip(qns, heads)]
            for p, hq in zip(ps, heads):
                o_ref[HD * hq:HD * (hq + 1), :] = _dot(vb, p.astype(BF16)).astype(BF16)

    col = pl.BlockSpec((HD, 1), lambda n: (0, 0))
    return _run(
        body, [qkv_t, qkv_t, qg, kg, sinks, bias], hook, grid=(t // CHUNK,), name=name, semantics=("parallel",),
        in_specs=[pl.BlockSpec((QKV, CHUNK), lambda n: (0, n)),
                  pl.BlockSpec((QKV - KV0, CHUNK), lambda n: (KV0 // (QKV - KV0), jnp.maximum(n - 1, 0))),
                  col, col, pl.BlockSpec(memory_space=pltpu.SMEM), pl.BlockSpec((NH, 2 * CHUNK, CHUNK), lambda n: (0, 0, 0))],
        out_specs=pl.BlockSpec((D, CHUNK), lambda n: (0, n)), out_shape=S((D, t), BF16))


def _dx_rows(dh, w, kc, out_dtype, name, hook=None):
    t = dh.shape[0]
    nk = w.shape[0] // kc
    tm = _tm(t)

    def body(d_ref, w_ref, o_ref):
        dhb = d_ref[...].astype(BF16)
        for j in range(nk):
            o_ref[j] = _dot_nt(dhb, w_ref[j * kc:(j + 1) * kc, :]).astype(out_dtype)

    return _run(
        body, [dh, w], hook, grid=(t // tm,), name=name, semantics=("parallel",),
        in_specs=[pl.BlockSpec((tm, D), lambda i: (i, 0)), _resident(w.shape)],
        out_specs=pl.BlockSpec((nk, tm, kc), lambda i: (0, i, 0)), out_shape=S((nk, t, kc), out_dtype))


def _dx_rows_t(dh, w, name, hook=None):
    t = dh.shape[0]
    k = w.shape[0]
    tm = _tm(t)

    def body(d_ref, w_ref, o_ref):
        o_ref[...] = _dot_nt(w_ref[...], d_ref[...].astype(BF16)).astype(BF16)

    return _run(
        body, [dh, w], hook, grid=(t // tm,), name=name, semantics=("parallel",),
        in_specs=[pl.BlockSpec((tm, D), lambda i: (i, 0)), _resident(w.shape)],
        out_specs=pl.BlockSpec((k, tm), lambda i: (0, i)), out_shape=S((k, t), BF16))


BWD1_ROWS = 256


def _ffn_bwd1(dh, a_g, a_v, wdown, cw, cb, name, hook=None):
    nh, t, n = a_g.shape
    tm = min(BWD1_ROWS, t)
    ni = t // tm

    def body(d_ref, ag_ref, av_ref, wd_ref, cw_ref, cb_ref, dc_ref, dw_hbm, dwb_hbm, dcw_ref, dcb_ref, carry, acc, stage, keep):
        i = pl.program_id(0)

        @pl.when(i == 0)
        def _():
            carry[...] = jnp.zeros_like(carry)
            acc[...] = jnp.zeros_like(acc)
            dcw_ref[...] = jnp.zeros_like(dcw_ref)
            dcb_ref[...] = jnp.zeros_like(dcb_ref)

        dhb = d_ref[...].astype(BF16)
        rsum = lambda x: jnp.sum(x, axis=0, keepdims=True)
        for j in range(nh):
            dact = _dot_nt(dhb, wd_ref[j * n:(j + 1) * n, :])
            ag = ag_ref[j].astype(F32)
            av = av_ref[j].astype(F32)
            cg, ag1, ag2 = _conv3(ag, carry[j], cw_ref[j], cb_ref[j], tm, keep.at[0])
            cv, av1, av2 = _conv3(av, carry[nh + j], cw_ref[nh + j], cb_ref[nh + j], tm, keep.at[1])
            carry[j] = ag[tm - HALO:]
            carry[nh + j] = av[tm - HALO:]
            sg = _sigmoid(cg)
            gs = cg * sg
            acc[j * n:(j + 1) * n, :] += _dot_tn((gs * cv).astype(BF16), dhb)
            dcg = dact * cv * (sg + gs * (1.0 - sg))
            dcv = dact * gs
            dc_ref[j] = dcg.astype(BF16)
            dc_ref[nh + j] = dcv.astype(BF16)
            dcw_ref[j] += jnp.concatenate([rsum(dcg * ag2), rsum(dcg * ag1), rsum(dcg * ag)], axis=0)
            dcw_ref[nh + j] += jnp.concatenate([rsum(dcv * av2), rsum(dcv * av1), rsum(dcv * av)], axis=0)
            dcb_ref[j] += rsum(dcg)
            dcb_ref[nh + j] += rsum(dcv)

        @pl.when(i == ni - 1)
        def _():
            pltpu.sync_copy(acc, dw_hbm)
            for j in range(nh):
                stage[...] = acc[j * n:(j + 1) * n, :].astype(BF16)
                pltpu.sync_copy(stage, dwb_hbm.at[pl.ds(j * n, n), :])

    aspec = pl.BlockSpec((nh, tm, n), lambda i: (0, i, 0))
    return _run(
        body, [dh, a_g, a_v, wdown, cw, cb], hook, grid=(ni,), name=name, semantics=("arbitrary",),
        in_specs=[pl.BlockSpec((tm, D), lambda i: (i, 0)), aspec, aspec, _resident(wdown.shape), _resident(cw.shape), _resident(cb.shape)],
        out_specs=[pl.BlockSpec((2 * nh, tm, n), lambda i: (0, i, 0)), ANY, ANY,
                   pl.BlockSpec(cw.shape, lambda i: (0, 0, 0)), pl.BlockSpec(cb.shape, lambda i: (0, 0, 0))],
        out_shape=[S((2 * nh, t, n), BF16), S(wdown.shape, F32), S(wdown.shape, BF16), S(cw.shape, F32), S(cb.shape, F32)],
        scratch_shapes=[pltpu.VMEM((2 * nh, HALO, n), F32), pltpu.VMEM(wdown.shape, F32), pltpu.VMEM((n, D), BF16),
                        pltpu.VMEM((2, 2, tm, n), F32)])


def _ffn_bwd2(dc, wup, cw, h, gain, dh_in, name, hook=None):
    ns, t, n = dc.shape
    tm = _tm(t)
    ni = t // tm

    def body(dc_ref, wu_ref, cw_ref, h_ref, g_ref, di_ref, da_ref, o_ref, dg_ref, carry):
        i = pl.program_id(0)

        @pl.when(i == 0)
        def _():
            carry[...] = jnp.zeros_like(carry)
            dg_ref[...] = jnp.zeros_like(dg_ref)

        acc = jnp.zeros((tm, D), F32)
        for s in range(ns):
            x = dc_ref[s].astype(F32)
            ext = jnp.concatenate([x, carry[s]], axis=0)
            cwv = cw_ref[s]
            da = (cwv[2:3] * x + cwv[1:2] * ext[1:1 + tm] + cwv[0:1] * ext[2:2 + tm]).astype(BF16)
            carry[s] = x[:HALO]
            da_ref[s] = da
            acc = acc + _dot_nt(da, wu_ref[s])
        hv = h_ref[...]
        r = _rstd(hv)
        gg = acc * g_ref[...]
        o_ref[...] = di_ref[...] + r * gg - hv * (r * r * r * jnp.mean(gg * hv, axis=-1, keepdims=True))
        dg_ref[...] += jnp.sum(acc * hv * r, axis=0, keepdims=True)

    slab = pl.BlockSpec((ns, tm, n), lambda i: (0, ni - 1 - i, 0))
    row = pl.BlockSpec((tm, D), lambda i: (ni - 1 - i, 0))
    vec = pl.BlockSpec((1, D), lambda i: (0, 0))
    return _run(
        body, [dc, wup, cw, h, gain, dh_in], hook, grid=(ni,), name=name, semantics=("arbitrary",),
        in_specs=[slab, _resident(wup.shape), _resident(cw.shape), row, vec, row],
        out_specs=[slab, row, vec], out_shape=[S((ns, t, n), BF16), S((t, D), F32), S((1, D), F32)],
        scratch_shapes=[pltpu.VMEM((ns, HALO, n), F32)])


def _dw_slot(hn, dy_s, name, hook=None):
    t, k = hn.shape
    ns, _, n = dy_s.shape
    tm = _tm(t)

    def body(a_ref, b_ref, o_ref, ob_ref, at_ref):
        @pl.when(pl.program_id(0) == 0)
        def _():
            for i in range(t // tm):
                at_ref[:, i * tm:(i + 1) * tm] = a_ref[i * tm:(i + 1) * tm, :].T

        acc = _dot(at_ref[...], b_ref[...])
        o_ref[...] = acc
        ob_ref[...] = acc.astype(BF16)

    ospec = pl.BlockSpec((None, k, n), lambda j: (j, 0, 0))
    return _run(
        body, [hn, dy_s], hook, grid=(ns,), name=name, semantics=("arbitrary",),
        in_specs=[_resident(hn.shape), pl.BlockSpec((None, t, n), lambda j: (j, 0, 0))],
        out_specs=[ospec, ospec], out_shape=[S((ns, k, n), F32), S((ns, k, n), BF16)],
        scratch_shapes=[pltpu.VMEM((k, t), BF16)])


def _dw_rows(a_s, dh, name, hook=None, fm=False):
    nk, t, kc = (1, a_s.shape[1], a_s.shape[0]) if fm else a_s.shape
    tm = _tm(t)
    ni = t // tm

    def body(a_ref, d_ref, o_ref, ob_ref):
        i = pl.program_id(0)
        dhb = d_ref[...].astype(BF16)

        @pl.when(i == 0)
        def _():
            o_ref[...] = jnp.zeros_like(o_ref)

        if fm:
            o_ref[...] += _dot(a_ref[...], dhb)
        for j in range(0 if fm else nk):
            o_ref[j * kc:(j + 1) * kc, :] += _dot_tn(a_ref[j], dhb)

        @pl.when(i == ni - 1)
        def _():
            ob_ref[...] = o_ref[...].astype(BF16)

    ospec = pl.BlockSpec((nk * kc, D), lambda i: (0, 0))
    return _run(
        body, [a_s, dh], hook, grid=(ni,), name=name, semantics=("arbitrary",),
        in_specs=[pl.BlockSpec((kc, tm), lambda i: (0, i)) if fm else pl.BlockSpec((nk, tm, kc), lambda i: (0, i, 0)),
                  pl.BlockSpec((tm, D), lambda i: (i, 0))],
        out_specs=[ospec, ospec], out_shape=[S((nk * kc, D), F32), S((nk * kc, D), BF16)])


def _dx_slot_normbwd(dy_s, wg, h, gain, dh_in, name, hook=None, fm=False):
    ns, t, n = (1, dy_s.shape[1], dy_s.shape[0]) if fm else dy_s.shape
    tm = _tm(t)

    def body(dy_ref, w_ref, h_ref, g_ref, di_ref, o_ref, dg_ref):
        i = pl.program_id(0)

        @pl.when(i == 0)
        def _():
            dg_ref[...] = jnp.zeros_like(dg_ref)

        g = _dot_tn(dy_ref[...], w_ref[...]) if fm else _dot_nt(dy_ref[0], w_ref[0])
        for s in range(1, ns):
            g = g + _dot_nt(dy_ref[s], w_ref[s])
        hv = h_ref[...]
        r = _rstd(hv)
        gg = g * g_ref[...]
        o_ref[...] = di_ref[...] + r * gg - hv * (r * r * r * jnp.mean(gg * hv, axis=-1, keepdims=True))
        dg_ref[...] += jnp.sum(g * hv * r, axis=0, keepdims=True)

    row = pl.BlockSpec((tm, D), lambda i: (i, 0))
    vec = pl.BlockSpec((1, D), lambda i: (0, 0))
    return _run(
        body, [dy_s, wg, h, gain, dh_in], hook, grid=(t // tm,), name=name, semantics=("arbitrary",),
        in_specs=[pl.BlockSpec((n, tm), lambda i: (0, i)) if fm else pl.BlockSpec((ns, tm, n), lambda i: (0, i, 0)),
                  _resident(wg.shape), row, vec, row],
        out_specs=[row, vec], out_shape=[S((t, D), F32), S((1, D), F32)])


def _sgu_gate_bwd(a_s, dg_s, vgain, ws, bst, name, hook=None):
    t = a_s.shape[1]
    sw = a_s.shape[2]
    gps = sw // CHUNK

    def body(a_ref, dg_ref, vg_ref, ws_ref, b_ref, da_ref, dws_ref, dbt_ref, dvg_ref, dvn_ref):
        n = pl.program_id(0)

        @pl.when(n == 0)
        def _():
            dws_ref[...] = jnp.zeros_like(dws_ref)
            dbt_ref[...] = jnp.zeros_like(dbt_ref)
            dvg_ref[...] = jnp.zeros_like(dvg_ref)

        vpre = jnp.concatenate([a_ref[4 + s].astype(F32) for s in range(4)], axis=1)
        v = _gelu(vpre)
        r = _rstd(v)
        vhat = v * r
        vn = (vhat * vg_ref[...]).astype(BF16)
        tri = _tril_mask()
        lane = lax.broadcasted_iota(jnp.int32, (CHUNK, CHUNK), 1)
        dbt = jnp.zeros((CHUNK, CHUNK), F32)
        for g in range(SGU_G):
            w = jnp.where(tri, ws_ref[g], 0.0).astype(BF16)
            vng = vn[:, g * CHUNK:(g + 1) * CHUNK]
            sg = _dot(w, vng) + b_ref[:, g:g + 1]
            lo = (g % gps) * CHUNK
            upre = a_ref[g // gps, :, lo:lo + CHUNK].astype(F32)
            dgate = dg_ref[g // gps, :, lo:lo + CHUNK].astype(F32)
            da_ref[g // gps, :, lo:lo + CHUNK] = (dgate * sg * _gelu_grad(upre)).astype(BF16)
            ds = dgate * _gelu(upre)
            dsb = ds.astype(BF16)
            dvn_ref[:, g * CHUNK:(g + 1) * CHUNK] = _dot_tn(w, dsb)
            dws_ref[g] += jnp.where(tri, _dot_nt(dsb, vng), 0.0)
            dbt = dbt + jnp.where(lane == g, jnp.sum(ds, axis=-1, keepdims=True), 0.0)
        dbt_ref[...] += dbt
        dvn = dvn_ref[...]
        dvg_ref[...] += jnp.sum(dvn * vhat, axis=0, keepdims=True)
        gg = dvn * vg_ref[...]
        dv = r * gg - v * (r * r * r * jnp.mean(gg * v, axis=-1, keepdims=True))
        dav = (dv * _gelu_grad(vpre)).astype(BF16)
        for s in range(4):
            da_ref[4 + s] = dav[:, s * sw:(s + 1) * sw]

    return _run(
        body, [a_s, dg_s, vgain, ws, bst], hook, grid=(t // CHUNK,), name=name, semantics=("arbitrary",),
        in_specs=[pl.BlockSpec((8, CHUNK, sw), lambda n: (0, n, 0)), pl.BlockSpec((4, CHUNK, sw), lambda n: (0, n, 0)),
                  pl.BlockSpec((1, SGU_W), lambda n: (0, 0)), pl.BlockSpec((SGU_G, CHUNK, CHUNK), lambda n: (0, 0, 0)),
                  pl.BlockSpec((CHUNK, SGU_G), lambda n: (0, 0))],
        out_specs=[pl.BlockSpec((8, CHUNK, sw), lambda n: (0, n, 0)), pl.BlockSpec((SGU_G, CHUNK, CHUNK), lambda n: (0, 0, 0)),
                   pl.BlockSpec((CHUNK, CHUNK), lambda n: (0, 0)), pl.BlockSpec((1, SGU_W), lambda n: (0, 0))],
        out_shape=[S((8, t, sw), BF16), S((SGU_G, CHUNK, CHUNK), F32), S((CHUNK, CHUNK), F32), S((1, SGU_W), F32)],
        scratch_shapes=[pltpu.VMEM((CHUNK, SGU_W), F32)])


def _attn_bwd(qkv_t, do_t, qg, kg, sinks, bias, name, hook=None):
    t = qkv_t.shape[1]
    nb = t // CHUNK

    def body(cur_ref, prev_ref, do_ref, qg_ref, kg_ref, sink_ref, bias_ref,
             o_ref, dqg_out, dkg_out, dsk_out, dbias_ref, carry, dqg_ref, dkg_ref, dsk_ref):
        n = pl.program_id(0)

        @pl.when(n == 0)
        def _():
            carry[...] = jnp.zeros_like(carry)
            dqg_ref[...] = jnp.zeros_like(dqg_ref)
            dkg_ref[...] = jnp.zeros_like(dkg_ref)
            dsk_ref[...] = jnp.zeros_like(dsk_ref)
            dbias_ref[...] = jnp.zeros_like(dbias_ref)

        @pl.when(n < nb)
        def _():
            valid = _attn_valid(n)
            o_ref[0:KV0, :] = carry[0:KV0, :].astype(BF16)
            for h in range(NKV):
                krow = KV0 + HD * h
                vrow = KV0 + HD * (NKV + h)
                k = _attn_band(cur_ref, prev_ref, krow)
                v = _attn_band(cur_ref, prev_ref, vrow)
                rk = _rstd_rows(k)
                khat = k * rk
                kn = (khat * kg_ref[...]).astype(BF16)
                kn_tok = kn.T
                vb = v.astype(BF16)
                v_tok = vb.T
                heads = range(KVG * h, KVG * (h + 1))
                qs = [cur_ref[HD * hq:HD * (hq + 1), :] for hq in heads]
                rqs = [_rstd_rows(q) for q in qs]
                qhats = [q * rq for q, rq in zip(qs, rqs)]
                qns = [(qhat * qg_ref[...]).astype(BF16) for qhat in qhats]
                probs = [_attn_probs(kn_tok, qn, bias_ref[hq], valid, sink_ref[hq]) for qn, hq in zip(qns, heads)]
                dohs = [do_ref[HD * hq:HD * (hq + 1), :] for hq in heads]
                dps = [_dot(v_tok, doh) for doh in dohs]
                dsums = [jnp.sum(p * dp, axis=0, keepdims=True) for (p, _), dp in zip(probs, dps)]
                dss = [p * (dp - dsum) for (p, _), dp, dsum in zip(probs, dps, dsums)]
                for hq, (_, psink), dsum, ds in zip(heads, probs, dsums, dss):
                    dsk_ref[hq:hq + 1, :] -= psink * dsum
                    dbias_ref[hq] += ds
                dv = sum(_dot_nt(doh, p.astype(BF16)) for doh, (p, _) in zip(dohs, probs))
                dscs = [(ds * (HD ** -0.5)).astype(BF16) for ds in dss]
                dqns = [_dot(kn, dsc) for dsc in dscs]
                dkn = sum(_dot_nt(qn, dsc) for qn, dsc in zip(qns, dscs))
                dqg_ref[...] += sum(dqn * qhat for dqn, qhat in zip(dqns, qhats))
                for hq, q, rq, dqn in zip(heads, qs, rqs, dqns):
                    gq = dqn * qg_ref[...]
                    carry[HD * hq:HD * (hq + 1), :] = rq * gq - q * (rq * rq * rq * jnp.mean(gq * q, axis=0, keepdims=True))
                dkg_ref[...] += dkn * khat
                gk = dkn * kg_ref[...]
                dk = rk * gk - k * (rk * rk * rk * jnp.mean(gk * k, axis=0, keepdims=True))
                o_ref[krow:krow + HD, :] = (carry[krow:krow + HD, :] + dk[:, :CHUNK]).astype(BF16)
                o_ref[vrow:vrow + HD, :] = (carry[vrow:vrow + HD, :] + dv[:, :CHUNK]).astype(BF16)
                carry[krow:krow + HD, :] = dk[:, CHUNK:]
                carry[vrow:vrow + HD, :] = dv[:, CHUNK:]

        @pl.when(n == nb)
        def _():
            o_ref[...] = carry[...].astype(BF16)
            dqg_out[...] = jnp.sum(dqg_ref[...], axis=1, keepdims=True)
            dkg_out[...] = jnp.sum(dkg_ref[...], axis=1, keepdims=True)
            dsk_out[...] = jnp.sum(dsk_ref[...], axis=1, keepdims=True)

    cur = lambda n: (0, jnp.minimum(n, nb - 1))
    col = pl.BlockSpec((HD, 1), lambda n: (0, 0))
    whole = lambda shape: pl.BlockSpec(shape, lambda n: (0,) * len(shape))
    return _run(
        body, [qkv_t, qkv_t, do_t, qg, kg, sinks, bias], hook, grid=(nb + 1,), name=name, semantics=("arbitrary",),
        in_specs=[pl.BlockSpec((QKV, CHUNK), cur),
                  pl.BlockSpec((QKV - KV0, CHUNK), lambda n: (KV0 // (QKV - KV0), jnp.clip(n - 1, 0, nb - 1))),
                  pl.BlockSpec((D, CHUNK), cur), col, col, pl.BlockSpec(memory_space=pltpu.SMEM), whole((NH, 2 * CHUNK, CHUNK))],
        out_specs=[pl.BlockSpec((QKV, CHUNK), lambda n: (0, jnp.maximum(n - 1, 0))), whole((HD, 1)), whole((HD, 1)),
                   whole((NH, 1)), whole((NH, 2 * CHUNK, CHUNK))],
        out_shape=[S((QKV, t), BF16), S((HD, 1), F32), S((HD, 1), F32), S((NH, 1), F32), S((NH, 2 * CHUNK, CHUNK), F32)],
        scratch_shapes=[pltpu.VMEM((QKV, CHUNK), F32), pltpu.VMEM((HD, CHUNK), F32), pltpu.VMEM((HD, 2 * CHUNK), F32),
                        pltpu.VMEM((NH, CHUNK), F32)])


class _Plain:
    def __init__(self, wg):
        self.full, self.grads = wg, {}

    def w(self, n):
        return self.full[n]

    def hook(self, host):
        return None

    def grad(self, n, pair):
        self.grads[n] = pair

    def small(self, g_rep):
        pass


def _local_step(x, target, rep, sch):
    bucket_row = jnp.asarray(_rel_tables().T.reshape(1, -1))
    bias = _relbias_fwd(rep["rel_bias"].T, bucket_row, "relbias_fwd").reshape(NH, 2 * CHUNK, CHUNK)
    bst = rep["sgu_b_s"][0].T
    ws = rep["sgu_w_s"][0]
    vgain = rep["sgu_v_gain"]
    qg, kg, sinks = rep["attn_q_gain"].reshape(HD, 1), rep["attn_k_gain"].reshape(HD, 1), rep["attn_sinks"][0]
    w_down = lambda l: sch.w("ffn_w_down%d" % l).reshape(D_FF, D)
    w_up = lambda l: sch.w("ffn_w_up%d" % l)
    cw = [sch.w("ffn_conv_w")[:, 3 * l:3 * l + 3] for l in range(2)]
    cb = [rep["ffn_conv_b"][l].reshape(8, 1, -1) for l in range(2)]
    mixg = [rep["mix_norm"][l:l + 1] for l in range(2)]
    ffng = [rep["ffn_norm"][l:l + 1] for l in range(2)]
    rows = lambda pair: tuple(g.reshape(N_DEV, -1, D) for g in pair)
    hk = sch.hook

    hn0 = _rmsnorm(x, mixg[0], "norm0")
    a0 = _mm_slot(hn0, sch.w("sgu_w_in"), BF16, "sgu_in", hk("sgu_in"))
    gated = _sgu_gate_fwd(a0, vgain, ws, bst, "sgu_gate", hk("sgu_gate"))
    h1, hn1 = _resid_mm(gated, sch.w("sgu_w_out").reshape(SGU_W, D), x, ffng[0], "norm", "sgu_out", hk("sgu_out"))
    ag0, av0, h2, hn2 = _ffn_fwd(hn1, h1, w_up(0), w_down(0), cw[0], cb[0], mixg[1], "norm", "ffn0_fwd", hk("ffn0_fwd"))
    qkv = _mm_t(hn2, sch.w("attn_w_qkv"), "qkv", hk("qkv"))
    o = _attn_fwd(qkv, qg, kg, sinks, bias, "attn", hk("attn"))
    h3, hn3 = _resid_mm(o, sch.w("attn_w_o").reshape(D, D), h2, ffng[1], "norm", "attn_out", hk("attn_out"), fm=True)
    ag1, av1, dy, sq = _ffn_fwd(hn3, h3, w_up(1), w_down(1), cw[1], cb[1], target, "loss", "ffn1_fwd_loss", hk("ffn1_fwd_loss"))
    loss = (0.5 / D) * jnp.sum(sq[:, 0, 0])

    def ffn_bwd(dh, h_in, hn, a_g, a_v, l, tag):
        dc, g_down, g_down_b, g_cw, g_cb = _ffn_bwd1(dh, a_g, a_v, w_down(l), cw[l], cb[l], tag + "_bwd1", hk(tag + "_bwd1"))
        sch.grad("ffn_w_down%d" % l, rows((g_down, g_down_b)))
        da, dh_new, dgain = _ffn_bwd2(dc, w_up(l), cw[l], h_in, ffng[l], dh, tag + "_bwd2", hk(tag + "_bwd2"))
        sch.grad("ffn_w_up%d" % l, _dw_slot(hn, da, tag + "_dw_up", hk(tag + "_dw_up")))
        return dh_new, dgain, g_cw, g_cb.reshape(-1)

    dh, d_ffng1, g_cw1, g_cb1 = ffn_bwd(dy, h3, hn3, ag1, av1, 1, "ffn1")
    do = _dx_rows_t(dh, sch.w("attn_w_o").reshape(D, D), "attn_do", hk("attn_do"))
    sch.grad("attn_w_o", rows(_dw_rows(o, dh, "dw_o", hk("dw_o"), fm=True)))
    dqkv, d_qg, d_kg, d_sk, d_bias = _attn_bwd(qkv, do, qg, kg, sinks, bias, "attn_bwd", hk("attn_bwd"))
    sch.grad("attn_w_qkv", tuple(g.reshape(N_DEV, -1, D) for g in _dw_rows(dqkv, hn2, "dw_qkv", hk("dw_qkv"), fm=True)))
    dh, d_mixg1 = _dx_slot_normbwd(dqkv, sch.w("attn_w_qkv").reshape(QKV, D), h2, mixg[1], dh, "dx_qkv", hk("dx_qkv"), fm=True)
    d_relb = _relbias_bwd(d_bias.reshape(NH, -1), bucket_row, "relbias_bwd").T
    dh, d_ffng0, g_cw0, g_cb0 = ffn_bwd(dh, h1, hn1, ag0, av0, 0, "ffn0")
    g_cw = jnp.concatenate([g_cw0, g_cw1], axis=1)
    sch.grad("ffn_conv_w", (g_cw, g_cw.astype(BF16)))
    dgated = _dx_rows(dh, sch.w("sgu_w_out").reshape(SGU_W, D), SGU_W // 4, BF16, "sgu_dgated", hk("sgu_dgated"))
    sch.grad("sgu_w_out", rows(_dw_rows(gated, dh, "dw_sgu_out", hk("dw_sgu_out"))))
    da0, d_ws, d_bst, d_vgain = _sgu_gate_bwd(a0, dgated, vgain, ws, bst, "sgu_gate_bwd", hk("sgu_gate_bwd"))
    grad_x, d_mixg0 = _dx_slot_normbwd(da0, sch.w("sgu_w_in"), x, mixg[0], dh, "dx_sgu_in")
    g_rep = {
        "mix_norm": jnp.concatenate([d_mixg0, d_mixg1], axis=0),
        "ffn_norm": jnp.concatenate([d_ffng0, d_ffng1], axis=0),
        "sgu_v_gain": d_vgain,
        "sgu_w_s": d_ws[None],
        "sgu_b_s": d_bst[:, :SGU_G].T[None],
        "attn_q_gain": d_qg.reshape(1, HD),
        "attn_k_gain": d_kg.reshape(1, HD),
        "attn_sinks": d_sk.reshape(1, NH),
        "rel_bias": d_relb,
        "ffn_conv_b": jnp.stack([g_cb0, g_cb1], axis=0),
    }
    sch.small(g_rep)
    sch.grad("sgu_w_in", _dw_slot(hn0, da0, "dw_sgu_in", hk("dw_sgu_in")))
    return loss, grad_x, g_rep


def _allgather(xs, name):
    nt = len(xs)

    def body(*refs):
        x_refs, o_refs = refs[:nt], refs[nt:2 * nt]
        send_sems, recv_sems, local_sems = refs[2 * nt:]
        x, y, c, chips = _place()
        me, sibling = (x, y, c), (x, y, 1 - c)

        def copy(t, k, block, to, src=None):
            px, py, pc = block
            dst = o_refs[t].at[4 * px + 2 * py + pc]
            return pltpu.make_async_remote_copy(
                src_ref=dst if src is None else src, dst_ref=dst, send_sem=send_sems.at[t, k], recv_sem=recv_sems.at[t, k],
                device_id=to, device_id_type=MESH)

        mine = [pltpu.make_async_copy(x_refs[t], o_refs[t].at[4 * x + 2 * y + c], local_sems.at[t]) for t in range(nt)]
        for cp in mine:
            cp.start()
        first = []
        for t in range(nt):
            first.append(copy(t, 0, me, sibling, src=x_refs[t]))
            first += [copy(t, 1 + j, me, (*chip, c), src=x_refs[t]) for j, chip in enumerate(chips)]
        for cp in first:
            cp.start()
        passed = []
        for j, chip in enumerate(chips):
            for t in range(nt):
                copy(t, 1 + j, (*chip, c), me).wait_recv()
                fwd = copy(t, 4 + j, (*chip, c), sibling)
                fwd.start()
                passed.append(fwd)
        for t in range(nt):
            copy(t, 0, sibling, me).wait_recv()
            for j, chip in enumerate(chips):
                copy(t, 4 + j, (*chip, 1 - c), me).wait_recv()
        for cp in first + passed:
            cp.wait_send()
        for cp in mine:
            cp.wait()

    return pl.pallas_call(
        body, name=name, in_specs=[ANY] * nt, out_specs=[ANY] * nt,
        out_shape=[S((N_DEV,) + a.shape, a.dtype) for a in xs],
        scratch_shapes=[pltpu.SemaphoreType.DMA((nt, 7)), pltpu.SemaphoreType.DMA((nt, 7)), pltpu.SemaphoreType.DMA((nt,))],
        compiler_params=pltpu.CompilerParams(has_side_effects=True))(*xs)


def _exchange(hook, name):
    comm = hook()
    ci, co = len(comm.inputs), len(comm.out_shapes)

    def body(*refs):
        cins, couts = refs[:ci], refs[ci:ci + co]
        send, recv = refs[-2:]
        comm.start(cins, couts, send, recv)
        comm.finish(cins, couts, send, recv)

    res = pl.pallas_call(
        body, name=name, in_specs=[ANY] * ci, out_specs=[ANY] * co, out_shape=comm.out_shapes,
        scratch_shapes=[pltpu.SemaphoreType.DMA((comm.n_sems,)), pltpu.SemaphoreType.DMA((comm.n_sems,))],
        input_output_aliases=dict(comm.aliases),
        compiler_params=pltpu.CompilerParams(has_side_effects=True))(*comm.inputs)
    hook(res)


def _row_tile(r):
    tr = r if r <= ROW_TILE or r % ROW_TILE else ROW_TILE
    assert r % tr == 0
    return tr


def _rs_partial(g32, sib, place, name):
    _, r, cdim = g32.shape
    tr = _row_tile(r)

    def body(place_ref, g_ref, s_ref, p_ref, own_ref):
        k = pl.program_id(1)
        tot = g_ref[...] + s_ref[...].astype(F32)
        p_ref[...] = tot.astype(BF16)

        @pl.when(k == place_ref[1])
        def _():
            own_ref[...] = tot

    grid_spec = pltpu.PrefetchScalarGridSpec(
        num_scalar_prefetch=1, grid=(r // tr, 4),
        in_specs=[pl.BlockSpec((None, None, tr, cdim), lambda i, k, pr: (k, pr[0], i, 0)),
                  pl.BlockSpec((None, tr, cdim), lambda i, k, pr: (k, i, 0))],
        out_specs=[pl.BlockSpec((None, tr, cdim), lambda i, k, pr: (k, i, 0)), pl.BlockSpec((tr, cdim), lambda i, k, pr: (i, 0))])
    return pl.pallas_call(
        body, grid_spec=grid_spec, name=name,
        out_shape=[S((4, r, cdim), BF16), S((r, cdim), F32)],
        compiler_params=_cp("parallel", "arbitrary"))(place, g32.reshape(4, 2, r, cdim), sib)


def _adamw_math(w, g, m, v):
    m = ADAM_B1 * m + (1.0 - ADAM_B1) * g
    v = ADAM_B2 * v + (1.0 - ADAM_B2) * (g * g)
    m_hat = m / (1.0 - ADAM_B1 ** ADAM_STEP)
    v_hat = v / (1.0 - ADAM_B2 ** ADAM_STEP)
    delta = -ADAM_LR * (m_hat / (jnp.sqrt(v_hat) + ADAM_EPS) + ADAM_WD * w)
    return delta, m, v


def _adamw_shard(owns, recvs, w, m, v, name):
    nl, r, cdim = w.shape
    tr = _row_tile(r)
    nr = r // tr

    def body(*refs):
        own_refs, recv_refs = refs[:nl], refs[nl:2 * nl]
        w_ref, m_ref, v_ref, g_out, d_out, m_out, v_out = refs[2 * nl:]
        layer = pl.program_id(0)
        g = None
        for l in range(nl):
            gl = own_refs[l][...] + recv_refs[l][0].astype(F32) + recv_refs[l][1].astype(F32) + recv_refs[l][2].astype(F32)
            g = gl if g is None else jnp.where(layer == l, gl, g)
        g_out[...] = g
        d_out[...], m_out[...], v_out[...] = _adamw_math(w_ref[...], g, m_ref[...], v_ref[...])

    park = lambda l: (lambda layer, i: (jnp.where(layer == l, i, jnp.where(layer < l, 0, nr - 1)), 0))
    park3 = lambda l: (lambda layer, i: (0, jnp.where(layer == l, i, jnp.where(layer < l, 0, nr - 1)), 0))
    row = pl.BlockSpec((None, tr, cdim), lambda layer, i: (layer, i, 0))
    return pl.pallas_call(
        body, grid=(nl, nr), name=name,
        in_specs=[pl.BlockSpec((tr, cdim), park(l)) for l in range(nl)] + [pl.BlockSpec((3, tr, cdim), park3(l)) for l in range(nl)]
        + [row, row, row],
        out_specs=[row] * 4, out_shape=[S((nl, r, cdim), F32)] * 4,
        compiler_params=_cp("arbitrary", "arbitrary"))(*owns, *recvs, w, m, v)


def _adamw_small(galls, ws, ms, vs, name):
    n = len(galls)

    def body(*refs):
        g_refs, w_refs, m_refs, v_refs, outs = refs[:n], refs[n:2 * n], refs[2 * n:3 * n], refs[3 * n:4 * n], refs[4 * n:]
        for i in range(n):
            g = g_refs[i][0]
            for s in range(1, N_DEV):
                g = g + g_refs[i][s]
            outs[i][...] = g
            outs[n + i][...], outs[2 * n + i][...], outs[3 * n + i][...] = _adamw_math(w_refs[i][...], g, m_refs[i][...], v_refs[i][...])

    res = pl.pallas_call(body, out_shape=[S(a.shape, F32) for a in ws] * 4, name=name)(*galls, *ws, *ms, *vs)
    return [res[k * n:(k + 1) * n] for k in range(4)]


REPLICATED = ["mix_norm", "ffn_norm", "sgu_v_gain", "sgu_w_s", "sgu_b_s", "attn_q_gain", "attn_k_gain", "attn_sinks", "rel_bias",
              "ffn_conv_b"]
WEIGHTS = ["mix_norm", "ffn_norm", "sgu_w_in", "sgu_v_gain", "sgu_w_s", "sgu_b_s", "sgu_w_out", "attn_w_qkv", "attn_q_gain",
           "attn_k_gain", "attn_sinks", "attn_w_o", "rel_bias", "ffn_w_up", "ffn_conv_w", "ffn_conv_b", "ffn_w_down"]
SMALL = ["g_" + n for n in REPLICATED]

GATHER_FIRST = ["sgu_w_in", "sgu_w_out", "ffn_conv_w"]
PLAN = {
    "sgu_in": [("ag1", "ffn_w_up0")],
    "sgu_gate": [("ag2", "ffn_w_up0"), ("ag1", "ffn_w_down0")],
    "sgu_out": [("ag2", "ffn_w_down0"), ("ag1", "attn_w_qkv")],
    "ffn0_fwd": [("ag2", "attn_w_qkv"), ("ag1", "attn_w_o"), ("ag1", "ffn_w_up1")],
    "qkv": [("ag2", "attn_w_o"), ("ag2", "ffn_w_up1")],
    "attn": [("ag1", "ffn_w_down1")],
    "attn_out": [("ag2", "ffn_w_down1")],
    "ffn1_bwd2": [("rs1", "ffn_w_down1")],
    "ffn1_dw_up": [("rs2", "ffn_w_down1")],
    "attn_do": [("rs1", "ffn_w_up1")],
    "attn_bwd": [("rs2", "ffn_w_up1"), ("rs1", "attn_w_o")],
    "dw_qkv": [("rs2", "attn_w_o")],
    "dx_qkv": [("rs1", "attn_w_qkv")],
    "ffn0_bwd1": [("rs2", "attn_w_qkv")],
    "ffn0_bwd2": [("rs1", "ffn_w_down0")],
    "ffn0_dw_up": [("rs2", "ffn_w_down0")],
    "sgu_dgated": [("rs1", "ffn_w_up0")],
    "sgu_gate_bwd": [("rs2", "ffn_w_up0"), ("rs1", "sgu_w_out")],
    "dw_sgu_in": [("rs2", "sgu_w_out")] + [("ag1", n) for n in SMALL],
    "last_a": [("rs1", "sgu_w_in"), ("rs1", "ffn_conv_w")] + [("ag2", n) for n in SMALL],
    "last_b": [("rs2", "sgu_w_in"), ("rs2", "ffn_conv_w")],
}


class _Overlap:
    def __init__(self, shard, place):
        self.shard, self.place = shard, place
        self.part, self.full = {}, {}
        self.grads, self.sib, self.own, self.recv = {}, {}, {}, {}

    def w(self, n):
        return self.full[n]

    def grad(self, n, pair):
        self.grads[n] = pair

    def small(self, g_rep):
        self.shard.update(("g_" + n, a) for n, a in _views2d(g_rep).items())

    def chip_sums(self, n):
        sums, self.own[n] = _rs_partial(self.grads[n][0], self.sib.pop(n), self.place, "rs_partial_" + n)
        return sums

    def hook(self, host):
        ops = PLAN.get(host)
        if not ops:
            return None
        where = {"ag1": self.part, "ag2": self.full, "rs1": self.sib, "rs2": self.recv}
        idx = []

        def hook(results=None):
            if results is not None:
                for (kind, n), i in zip(ops, idx):
                    where[kind][n] = results[i]
                return None
            comm = _Comm()
            for kind, n in ops:
                arr = {"ag1": lambda: self.shard[n], "ag2": lambda: self.part.pop(n), "rs1": lambda: self.grads[n][1],
                       "rs2": lambda: self.chip_sums(n)}[kind]()
                idx.append(comm.add(kind, arr))
            return comm

        return hook


TRANSPOSED = {"attn_w_qkv"}
SHARDED = {
    "sgu_w_in": ["sgu_w_in"], "sgu_w_out": ["sgu_w_out"], "attn_w_qkv": ["attn_w_qkv"], "attn_w_o": ["attn_w_o"],
    "ffn_w_up": ["ffn_w_up0", "ffn_w_up1"], "ffn_w_down": ["ffn_w_down0", "ffn_w_down1"], "ffn_conv_w": ["ffn_conv_w"],
}


def _send_views(w):
    out = {"ffn_conv_w": w["ffn_conv_w"].reshape(6, -1)}
    for name, parts in SHARDED.items():
        if name != "ffn_conv_w":
            out.update((p, (w[name][l].T if name in TRANSPOSED else w[name][l]).astype(BF16)) for l, p in enumerate(parts))
    return out


def _views2d(d):
    return {n: d[n].reshape(-1, d[n].shape[-1]) for n in REPLICATED}


def kernel(x, mix_norm, ffn_norm, sgu_w_in, sgu_v_gain, sgu_w_s, sgu_b_s, sgu_w_out, attn_w_qkv, attn_q_gain, attn_k_gain, attn_sinks, attn_w_o, rel_bias, ffn_w_up, ffn_conv_w, ffn_conv_b, ffn_w_down, loss_target, m_mix_norm, m_ffn_norm, m_sgu_w_in, m_sgu_v_gain, m_sgu_w_s, m_sgu_b_s, m_sgu_w_out, m_attn_w_qkv, m_attn_q_gain, m_attn_k_gain, m_attn_sinks, m_attn_w_o, m_rel_bias, m_ffn_w_up, m_ffn_conv_w, m_ffn_conv_b, m_ffn_w_down, v_mix_norm, v_ffn_norm, v_sgu_w_in, v_sgu_v_gain, v_sgu_w_s, v_sgu_b_s, v_sgu_w_out, v_attn_w_qkv, v_attn_q_gain, v_attn_k_gain, v_attn_sinks, v_attn_w_o, v_rel_bias, v_ffn_w_up, v_ffn_conv_w, v_ffn_conv_b, v_ffn_w_down):
    w = dict(zip(WEIGHTS, (mix_norm, ffn_norm, sgu_w_in, sgu_v_gain, sgu_w_s, sgu_b_s, sgu_w_out, attn_w_qkv, attn_q_gain, attn_k_gain,
                           attn_sinks, attn_w_o, rel_bias, ffn_w_up, ffn_conv_w, ffn_conv_b, ffn_w_down)))
    m = dict(zip(WEIGHTS, (m_mix_norm, m_ffn_norm, m_sgu_w_in, m_sgu_v_gain, m_sgu_w_s, m_sgu_b_s, m_sgu_w_out, m_attn_w_qkv, m_attn_q_gain,
                           m_attn_k_gain, m_attn_sinks, m_attn_w_o, m_rel_bias, m_ffn_w_up, m_ffn_conv_w, m_ffn_conv_b, m_ffn_w_down)))
    v = dict(zip(WEIGHTS, (v_mix_norm, v_ffn_norm, v_sgu_w_in, v_sgu_v_gain, v_sgu_w_s, v_sgu_b_s, v_sgu_w_out, v_attn_w_qkv, v_attn_q_gain,
                           v_attn_k_gain, v_attn_sinks, v_attn_w_o, v_rel_bias, v_ffn_w_up, v_ffn_conv_w, v_ffn_conv_b, v_ffn_w_down)))
    rep = {n: w[n] for n in REPLICATED}

    xi, yi, ci = lax.axis_index("x"), lax.axis_index("y"), lax.axis_index("c")
    place = jnp.stack([ci, 2 * xi + yi]).astype(jnp.int32)
    sch = _Overlap(_send_views(w), place)
    sch.full.update(zip(GATHER_FIRST, _allgather([sch.shard[n] for n in GATHER_FIRST], "gather_first")))

    loss, grad_x, g_rep = _local_step(x[0], loss_target[0], rep, sch)
    loss = lax.psum(loss, ("x", "y", "c"))
    _exchange(sch.hook("last_a"), "last_a")
    _exchange(sch.hook("last_b"), "last_b")

    out = [{}, {}, {}, {}]
    for name, parts in SHARDED.items():
        flip = (lambda a: jnp.swapaxes(a, -1, -2)) if name in TRANSPOSED else (lambda a: a)
        shape = flip(w[name]).shape
        as3d = lambda a: flip(a).reshape(len(parts), -1, shape[-1])
        res = _adamw_shard([sch.own[p] for p in parts], [sch.recv[p] for p in parts], as3d(w[name]), as3d(m[name]), as3d(v[name]),
                           "adamw_" + name)
        for o, r in zip(out, res):
            o[name] = flip(r.reshape(shape))
    small = _adamw_small([sch.full[n] for n in SMALL], *[list(_views2d(d).values()) for d in (rep, m, v)], "adamw_small")
    for o, res in zip(out, small):
        o.update((n, r.reshape(w[n].shape)) for n, r in zip(REPLICATED, res))

    return (loss, grad_x[None], *[out[0][n] for n in WEIGHTS], *[out[1][n] for n in WEIGHTS],
            *[out[2][n] for n in WEIGHTS], *[out[3][n] for n in WEIGHTS])
```

```python
import functools
import math

import numpy as np
import jax
import jax.numpy as jnp
from jax import lax
from jax.experimental import pallas as pl
from jax.experimental.pallas import tpu as pltpu

F32 = jnp.float32
BF16 = jnp.bfloat16
S = jax.ShapeDtypeStruct

D = 1024
CHUNK = 128
SGU_W = 2048
SGU_G = 16
HD = 64
NH = 16
NKV = 4
KVG = 4
D_FF = 2816
REL_BUCKETS = 32
REL_MAX_DIST = 128
EPS = 1e-6
N_DEV = 8
MESH = pl.DeviceIdType.MESH

ADAM_LR = 0.001
ADAM_B1 = 0.9
ADAM_B2 = 0.999
ADAM_EPS = 1e-08
ADAM_WD = 0.01
ADAM_STEP = 10

ROW_TILE = 512
HALO = 8


def _tm(t):
    return min(ROW_TILE, t)


def _cp(*sem):
    return pltpu.CompilerParams(dimension_semantics=sem)


ANY = pl.BlockSpec(memory_space=pl.ANY)


def _place():
    x, y, c = lax.axis_index("x"), lax.axis_index("y"), lax.axis_index("c")
    return x, y, c, [(1 - x, y), (x, 1 - y), (1 - x, 1 - y)]


class _Comm:
    SEMS = {"ag1": 5, "ag2": 3, "rs1": 4, "rs2": 3}

    def __init__(self):
        self.inputs, self.out_shapes, self.aliases, self.ops, self.n_sems = [], [], {}, [], 0

    def add(self, kind, arr):
        lead = {"ag1": N_DEV, "ag2": None, "rs1": 4, "rs2": 3}[kind]
        shape = arr.shape if lead is None else (lead,) + arr.shape[(0 if kind == "ag1" else 1):]
        if kind == "ag2":
            self.aliases[len(self.inputs)] = len(self.out_shapes)
        self.ops.append((kind, len(self.inputs), len(self.out_shapes), self.n_sems))
        self.inputs.append(arr)
        self.out_shapes.append(S(shape, arr.dtype))
        self.n_sems += self.SEMS[kind]
        return len(self.out_shapes) - 1

    def _copies(self, ins, outs, send, recv):
        x, y, c, chips = _place()
        me, sibling = (x, y, c), (x, y, 1 - c)
        slot = lambda px, py, pc: 4 * px + 2 * py + pc
        sends, recvs, local = [], [], []

        def rc(src, dst, k, to):
            return lambda: pltpu.make_async_remote_copy(src_ref=src(), dst_ref=dst(), send_sem=send.at[k], recv_sem=recv.at[k],
                                                        device_id=to, device_id_type=MESH)

        for kind, ii, oi, b in self.ops:
            src, dst = ins[ii], outs[oi]
            at = lambda ref, i: (lambda: ref.at[i])
            if kind == "ag1":
                whole, mine = (lambda s=src: s), at(dst, slot(*me))
                sends.append(rc(whole, mine, b, sibling))
                recvs.append(rc(whole, at(dst, slot(x, y, 1 - c)), b, me))
                for j, chip in enumerate(chips):
                    sends.append(rc(whole, mine, b + 1 + j, (*chip, c)))
                    recvs.append(rc(whole, at(dst, slot(*chip, c)), b + 1 + j, me))
                local.append(lambda s=src, m=mine, k=b + 4: pltpu.make_async_copy(s, m(), send.at[k]))
            elif kind == "ag2":
                for j, chip in enumerate(chips):
                    sends.append(rc(at(dst, slot(*chip, c)), at(dst, slot(*chip, c)), b + j, sibling))
                    recvs.append(rc(at(dst, slot(*chip, 1 - c)), at(dst, slot(*chip, 1 - c)), b + j, me))
            elif kind == "rs1":
                for k in range(4):
                    sends.append(rc(at(src, 2 * k + (1 - c)), at(dst, k), b + k, sibling))
                    recvs.append(rc(at(src, 2 * k + c), at(dst, k), b + k, me))
            else:
                for j, (px, py) in enumerate(chips):
                    sends.append(rc(at(src, 2 * px + py), at(dst, j), b + j, (px, py, c)))
                    recvs.append(rc(at(src, 2 * px + py), at(dst, j), b + j, me))
        return sends, recvs, local

    def start(self, ins, outs, send, recv):
        sends, _, local = self._copies(ins, outs, send, recv)
        for make in local + sends:
            make().start()

    def finish(self, ins, outs, send, recv):
        sends, recvs, local = self._copies(ins, outs, send, recv)
        for make in recvs:
            make().wait_recv()
        for make in sends:
            make().wait_send()
        for make in local:
            make().wait()


def _run(body, args, hook, *, grid, in_specs, out_specs, out_shape, name, semantics, scratch_shapes=()):
    comm = hook() if hook is not None else None
    if comm is None:
        return pl.pallas_call(body, grid=grid, in_specs=in_specs, out_specs=out_specs, out_shape=out_shape, name=name,
                              scratch_shapes=list(scratch_shapes), compiler_params=_cp(*semantics))(*args)
    single = not isinstance(out_shape, (list, tuple))
    out_shapes = [out_shape] if single else list(out_shape)
    out_specs_l = [out_specs] if single else list(out_specs)
    n_in, n_out, n_scr, ci, co = len(args), len(out_shapes), len(scratch_shapes), len(comm.inputs), len(comm.out_shapes)

    def wrapped(*refs):
        ins, cins = refs[:n_in], refs[n_in:n_in + ci]
        outs, couts = refs[n_in + ci:n_in + ci + n_out], refs[n_in + ci + n_out:n_in + ci + n_out + co]
        scr = refs[n_in + ci + n_out + co:n_in + ci + n_out + co + n_scr]
        send, recv = refs[-2:]
        first = functools.reduce(lambda a, b: a & b, [pl.program_id(a) == 0 for a in range(len(grid))])
        last = functools.reduce(lambda a, b: a & b, [pl.program_id(a) == g - 1 for a, g in enumerate(grid)])

        @pl.when(first)
        def _():
            comm.start(cins, couts, send, recv)

        body(*ins, *outs, *scr)

        @pl.when(last)
        def _():
            comm.finish(cins, couts, send, recv)

    res = pl.pallas_call(
        wrapped, grid=grid, in_specs=list(in_specs) + [ANY] * ci, out_specs=out_specs_l + [ANY] * co,
        out_shape=out_shapes + comm.out_shapes, name=name,
        scratch_shapes=list(scratch_shapes) + [pltpu.SemaphoreType.DMA((comm.n_sems,)), pltpu.SemaphoreType.DMA((comm.n_sems,))],
        input_output_aliases={n_in + k: n_out + v for k, v in comm.aliases.items()},
        compiler_params=pltpu.CompilerParams(dimension_semantics=("arbitrary",) * len(grid), has_side_effects=True))(*args, *comm.inputs)
    hook(res[n_out:])
    return res[0] if single else list(res[:n_out])


def _dot(a, b):
    return jnp.dot(a, b, preferred_element_type=F32)


def _dot_nt(a, b):
    return lax.dot_general(a, b, (((1,), (1,)), ((), ())), preferred_element_type=F32)


def _dot_tn(a, b):
    return lax.dot_general(a, b, (((0,), (0,)), ((), ())), preferred_element_type=F32)


def _gelu(x):
    return 0.5 * x * (1.0 + lax.erf(x * (2.0 ** -0.5)))


def _gelu_grad(x):
    return 0.5 * (1.0 + lax.erf(x * (2.0 ** -0.5))) + x * jnp.exp(-0.5 * x * x) * (1.0 / math.sqrt(2.0 * math.pi))


def _sigmoid(x):
    return 1.0 / (1.0 + jnp.exp(-x))


def _rstd(x):
    return lax.rsqrt(jnp.mean(x * x, axis=-1, keepdims=True) + EPS)


def _rel_tables():
    q = np.arange(CHUNK)[:, None] + CHUNK
    k = np.arange(2 * CHUNK)[None, :]
    dist = q - k
    n = np.maximum(dist, 0)
    max_exact = REL_BUCKETS // 2
    large = max_exact + (np.log(np.maximum(n, 1).astype(np.float32) / max_exact)
                         / math.log(REL_MAX_DIST / max_exact) * (REL_BUCKETS - max_exact)).astype(np.int32)
    large = np.minimum(large, REL_BUCKETS - 1)
    return np.where(n < max_exact, n, large).astype(np.int32)


def _rmsnorm(x, gain, name):
    t = x.shape[0]
    tm = _tm(t)

    def body(x_ref, g_ref, o_ref):
        xv = x_ref[...]
        o_ref[...] = (xv * _rstd(xv) * g_ref[...]).astype(BF16)

    return pl.pallas_call(
        body, grid=(t // tm,), name=name,
        in_specs=[pl.BlockSpec((tm, D), lambda i: (i, 0)), pl.BlockSpec((1, D), lambda i: (0, 0))],
        out_specs=pl.BlockSpec((tm, D), lambda i: (i, 0)),
        out_shape=S((t, D), BF16), compiler_params=_cp("parallel"))(x, gain)


def _resident(shape):
    zeros = (0,) * len(shape)
    return pl.BlockSpec(shape, lambda *_: zeros, pipeline_mode=pl.Buffered(1))


def _mm_slot(hn, wg, out_dtype, name, hook=None):
    t, k = hn.shape
    ns, _, n = wg.shape
    tm = _tm(t)

    def body(a_ref, w_ref, o_ref):
        a = a_ref[...]
        for s in range(ns):
            o_ref[s] = _dot(a, w_ref[s]).astype(out_dtype)

    return _run(
        body, [hn, wg], hook, grid=(t // tm,), name=name, semantics=("parallel",),
        in_specs=[pl.BlockSpec((tm, k), lambda i: (i, 0)), _resident(wg.shape)],
        out_specs=pl.BlockSpec((ns, tm, n), lambda i: (0, i, 0)), out_shape=S((ns, t, n), out_dtype))


def _mm_t(hn, wt, name, hook=None):
    t, k = hn.shape
    ns, n, _ = wt.shape
    tm = _tm(t)

    def body(a_ref, w_ref, o_ref):
        a = a_ref[...]
        for s in range(ns):
            o_ref[s * n:(s + 1) * n, :] = _dot_nt(w_ref[s], a)

    return _run(
        body, [hn, wt], hook, grid=(t // tm,), name=name, semantics=("parallel",),
        in_specs=[pl.BlockSpec((tm, k), lambda i: (i, 0)), _resident(wt.shape)],
        out_specs=pl.BlockSpec((ns * n, tm), lambda i: (0, i)), out_shape=S((ns * n, t), F32))


def _conv3(a, prev, cw, cb, tm, keep=None):
    ext = jnp.concatenate([prev, a], axis=0)
    a1 = ext[HALO - 1:HALO - 1 + tm]
    a2 = ext[HALO - 2:HALO - 2 + tm]
    if keep is not None:
        keep[0] = a1
        keep[1] = a2
        a1, a2 = keep[0], keep[1]
    return cw[2:3] * a + cw[1:2] * a1 + cw[0:1] * a2 + cb, a1, a2


def _ffn_fwd(hn, h, wup, wdown, cw, cb, extra, mode, name, hook=None):
    t, k = hn.shape
    n = wup.shape[-1]
    nh = wup.shape[0] // 2
    tm = _tm(t)
    ni = t // tm

    def body(a_ref, h_ref, wu_ref, wd_ref, cw_ref, cb_ref, e_ref, ag_ref, av_ref, o1_ref, o2_ref, carry):
        i = pl.program_id(0)

        @pl.when(i == 0)
        def _():
            carry[...] = jnp.zeros_like(carry)

        a = a_ref[...]
        acc = h_ref[...]
        nxt = (_dot(a, wu_ref[0]), _dot(a, wu_ref[nh]))
        for j in range(nh):
            ag, av = nxt
            if j + 1 < nh:
                nxt = (_dot(a, wu_ref[j + 1]), _dot(a, wu_ref[nh + j + 1]))
            ag_ref[j] = ag.astype(BF16)
            av_ref[j] = av.astype(BF16)
            cg, _, _ = _conv3(ag, carry[j], cw_ref[j], cb_ref[j], tm)
            cv, _, _ = _conv3(av, carry[nh + j], cw_ref[nh + j], cb_ref[nh + j], tm)
            carry[j] = ag[tm - HALO:]
            carry[nh + j] = av[tm - HALO:]
            act = (cg * _sigmoid(cg) * cv).astype(BF16)
            acc = acc + _dot(act, wd_ref[j * n:(j + 1) * n, :])
        if mode == "norm":
            o1_ref[...] = acc
            o2_ref[...] = (acc * _rstd(acc) * e_ref[...]).astype(BF16)
        else:
            err = acc - e_ref[...]
            o1_ref[...] = err * (1.0 / D)
            o2_ref[...] = jnp.full(o2_ref.shape, jnp.sum(err * err), F32)

    row = pl.BlockSpec((tm, D), lambda i: (i, 0))
    if mode == "norm":
        e_spec, o2_spec, o2_shape = pl.BlockSpec((1, D), lambda i: (0, 0)), row, S((t, D), BF16)
    else:
        e_spec, o2_spec, o2_shape = row, pl.BlockSpec((None, 8, 128), lambda i: (i, 0, 0)), S((ni, 8, 128), F32)
    aspec = pl.BlockSpec((nh, tm, n), lambda i: (0, i, 0))
    return _run(
        body, [hn, h, wup, wdown, cw, cb, extra], hook, grid=(ni,), name=name, semantics=("arbitrary",),
        in_specs=[pl.BlockSpec((tm, k), lambda i: (i, 0)), row, _resident(wup.shape), _resident(wdown.shape),
                  _resident(cw.shape), _resident(cb.shape), e_spec],
        out_specs=[aspec, aspec, row, o2_spec],
        out_shape=[S((nh, t, n), BF16), S((nh, t, n), BF16), S((t, D), F32), o2_shape],
        scratch_shapes=[pltpu.VMEM((2 * nh, HALO, n), F32)])


def _tril_mask():
    r = lax.broadcasted_iota(jnp.int32, (CHUNK, CHUNK), 0)
    c = lax.broadcasted_iota(jnp.int32, (CHUNK, CHUNK), 1)
    return r >= c


def _sgu_gate_fwd(a_s, vgain, ws, bst, name, hook=None):
    t = a_s.shape[1]
    sw = a_s.shape[2]
    gps = sw // CHUNK

    def body(a_ref, vg_ref, ws_ref, b_ref, o_ref):
        v = _gelu(jnp.concatenate([a_ref[4 + s].astype(F32) for s in range(4)], axis=1))
        vn = (v * _rstd(v) * vg_ref[...]).astype(BF16)
        tri = _tril_mask()
        for g in range(SGU_G):
            w = jnp.where(tri, ws_ref[g], 0.0).astype(BF16)
            sg = _dot(w, vn[:, g * CHUNK:(g + 1) * CHUNK]) + b_ref[:, g:g + 1]
            lo = (g % gps) * CHUNK
            u = _gelu(a_ref[g // gps, :, lo:lo + CHUNK].astype(F32))
            o_ref[g // gps, :, lo:lo + CHUNK] = (u * sg).astype(BF16)

    return _run(
        body, [a_s, vgain, ws, bst], hook, grid=(t // CHUNK,), name=name, semantics=("parallel",),
        in_specs=[pl.BlockSpec((8, CHUNK, sw), lambda n: (0, n, 0)), pl.BlockSpec((1, SGU_W), lambda n: (0, 0)),
                  pl.BlockSpec((SGU_G, CHUNK, CHUNK), lambda n: (0, 0, 0)), pl.BlockSpec((CHUNK, SGU_G), lambda n: (0, 0))],
        out_specs=pl.BlockSpec((4, CHUNK, sw), lambda n: (0, n, 0)), out_shape=S((4, t, sw), BF16))


def _resid_mm(a_s, w, resid, extra, mode, name, hook=None, fm=False):
    nk, t, kc = (1, a_s.shape[1], a_s.shape[0]) if fm else a_s.shape
    tm = _tm(t)
    ni = t // tm

    def body(a_ref, w_ref, r_ref, e_ref, o1_ref, o2_ref):
        h = r_ref[...]
        if fm:
            h = h + _dot_tn(a_ref[...], w_ref[...])
        for j in range(0 if fm else nk):
            h = h + _dot(a_ref[j], w_ref[j * kc:(j + 1) * kc, :])
        if mode == "norm":
            o1_ref[...] = h
            o2_ref[...] = (h * _rstd(h) * e_ref[...]).astype(BF16)
        else:
            err = h - e_ref[...]
            o1_ref[...] = err * (1.0 / D)
            o2_ref[...] = jnp.full(o2_ref.shape, jnp.sum(err * err), F32)

    row = pl.BlockSpec((tm, D), lambda i: (i, 0))
    if mode == "norm":
        e_spec, o2_spec, o2_shape = pl.BlockSpec((1, D), lambda i: (0, 0)), row, S((t, D), BF16)
    else:
        e_spec, o2_spec, o2_shape = row, pl.BlockSpec((None, 8, 128), lambda i: (i, 0, 0)), S((ni, 8, 128), F32)
    return _run(
        body, [a_s, w, resid, extra], hook, grid=(ni,), name=name, semantics=("parallel",),
        in_specs=[pl.BlockSpec((kc, tm), lambda i: (0, i)) if fm else pl.BlockSpec((nk, tm, kc), lambda i: (0, i, 0)),
                  _resident(w.shape), row, e_spec],
        out_specs=[row, o2_spec], out_shape=[S((t, D), F32), o2_shape])


def _relbias_fwd(rel_bias_t, bucket_row, name):
    nb = bucket_row.shape[1]

    def body(rb_ref, bk_ref, o_ref):
        onehot = (lax.broadcasted_iota(jnp.int32, (REL_BUCKETS, nb), 0) == bk_ref[...]).astype(F32)
        o_ref[...] = jnp.dot(rb_ref[...], onehot, precision=lax.Precision.HIGHEST, preferred_element_type=F32)

    return pl.pallas_call(body, out_shape=S((NH, nb), F32), name=name)(rel_bias_t, bucket_row)


def _relbias_bwd(dbias, bucket_row, name):
    nb = bucket_row.shape[1]

    def body(db_ref, bk_ref, o_ref):
        onehot = (lax.broadcasted_iota(jnp.int32, (REL_BUCKETS, nb), 0) == bk_ref[...]).astype(F32)
        o_ref[...] = lax.dot_general(db_ref[...], onehot, (((1,), (1,)), ((), ())),
                                     precision=lax.Precision.HIGHEST, preferred_element_type=F32)

    return pl.pallas_call(body, out_shape=S((NH, REL_BUCKETS), F32), name=name)(dbias, bucket_row)


QKV = D + 2 * NKV * HD
KV0 = D


def _rstd_rows(x):
    return lax.rsqrt(jnp.mean(x * x, axis=0, keepdims=True) + EPS)


def _attn_valid(n):
    kj = lax.broadcasted_iota(jnp.int32, (2 * CHUNK, CHUNK), 0)
    qi = lax.broadcasted_iota(jnp.int32, (2 * CHUNK, CHUNK), 1)
    dist = qi + CHUNK - kj
    return (dist >= 0) & (dist < CHUNK) & ((n > 0) | (kj >= CHUNK))


def _attn_band(cur_ref, prev_ref, row):
    return jnp.concatenate([prev_ref[row - KV0:row - KV0 + HD, :], cur_ref[row:row + HD, :]], axis=1)


def _attn_probs(kn_tok, qn, bias, valid, sink):
    s = _dot(kn_tok, qn) * (HD ** -0.5) + bias
    s = jnp.where(valid, s, -jnp.inf)
    m = jnp.maximum(jnp.max(s, axis=0, keepdims=True), sink)
    p = jnp.exp(s - m)
    psink = jnp.exp(sink - m)
    inv = 1.0 / (jnp.sum(p, axis=0, keepdims=True) + psink)
    return p * inv, psink * inv


def _attn_fwd(qkv_t, qg, kg, sinks, bias, name, hook=None):
    t = qkv_t.shape[1]

    def body(cur_ref, prev_ref, qg_ref, kg_ref, sink_ref, bias_ref, o_ref):
        n = pl.program_id(0)
        valid = _attn_valid(n)
        for h in range(NKV):
            k = _attn_band(cur_ref, prev_ref, KV0 + HD * h)
            v = _attn_band(cur_ref, prev_ref, KV0 + HD * (NKV + h))
            kn_tok = (k * _rstd_rows(k) * kg_ref[...]).astype(BF16).T
            vb = v.astype(BF16)
            heads = range(KVG * h, KVG * (h + 1))
            qs = [cur_ref[HD * hq:HD * (hq + 1), :] for hq in heads]
            qns = [(q * _rstd_rows(q) * qg_ref[...]).astype(BF16) for q in qs]
            ps = [_attn_probs(kn_tok, qn, bias_ref[hq], valid, sink_ref[hq])[0] for qn, hq in zip(qns, heads)]
            for p, hq in zip(ps, heads):
                o_ref[HD * hq:HD * (hq + 1), :] = _dot(vb, p.astype(BF16)).astype(BF16)

    col = pl.BlockSpec((HD, 1), lambda n: (0, 0))
    return _run(
        body, [qkv_t, qkv_t, qg, kg, sinks, bias], hook, grid=(t // CHUNK,), name=name, semantics=("parallel",),
        in_specs=[pl.BlockSpec((QKV, CHUNK), lambda n: (0, n)),
                  pl.BlockSpec((QKV - KV0, CHUNK), lambda n: (KV0 // (QKV - KV0), jnp.maximum(n - 1, 0))),
                  col, col, pl.BlockSpec(memory_space=pltpu.SMEM), pl.BlockSpec((NH, 2 * CHUNK, CHUNK), lambda n: (0, 0, 0))],
        out_specs=pl.BlockSpec((D, CHUNK), lambda n: (0, n)), out_shape=S((D, t), BF16))


def _dx_rows(dh, w, kc, out_dtype, name, hook=None):
    t = dh.shape[0]
    nk = w.shape[0] // kc
    tm = _tm(t)

    def body(d_ref, w_ref, o_ref):
        dhb = d_ref[...].astype(BF16)
        for j in range(nk):
            o_ref[j] = _dot_nt(dhb, w_ref[j * kc:(j + 1) * kc, :]).astype(out_dtype)

    return _run(
        body, [dh, w], hook, grid=(t // tm,), name=name, semantics=("parallel",),
        in_specs=[pl.BlockSpec((tm, D), lambda i: (i, 0)), _resident(w.shape)],
        out_specs=pl.BlockSpec((nk, tm, kc), lambda i: (0, i, 0)), out_shape=S((nk, t, kc), out_dtype))


def _dx_rows_t(dh, w, name, hook=None):
    t = dh.shape[0]
    k = w.shape[0]
    tm = _tm(t)

    def body(d_ref, w_ref, o_ref):
        o_ref[...] = _dot_nt(w_ref[...], d_ref[...].astype(BF16)).astype(BF16)

    return _run(
        body, [dh, w], hook, grid=(t // tm,), name=name, semantics=("parallel",),
        in_specs=[pl.BlockSpec((tm, D), lambda i: (i, 0)), _resident(w.shape)],
        out_specs=pl.BlockSpec((k, tm), lambda i: (0, i)), out_shape=S((k, t), BF16))


BWD1_ROWS = 256


def _ffn_bwd1(dh, a_g, a_v, wdown, cw, cb, name, hook=None):
    nh, t, n = a_g.shape
    tm = min(BWD1_ROWS, t)
    ni = t // tm

    def body(d_ref, ag_ref, av_ref, wd_ref, cw_ref, cb_ref, dc_ref, dw_hbm, dwb_hbm, dcw_ref, dcb_ref, carry, acc, stage, keep):
        i = pl.program_id(0)

        @pl.when(i == 0)
        def _():
            carry[...] = jnp.zeros_like(carry)
            acc[...] = jnp.zeros_like(acc)
            dcw_ref[...] = jnp.zeros_like(dcw_ref)
            dcb_ref[...] = jnp.zeros_like(dcb_ref)

        dhb = d_ref[...].astype(BF16)
        rsum = lambda x: jnp.sum(x, axis=0, keepdims=True)
        for j in range(nh):
            dact = _dot_nt(dhb, wd_ref[j * n:(j + 1) * n, :])
            ag = ag_ref[j].astype(F32)
            av = av_ref[j].astype(F32)
            cg, ag1, ag2 = _conv3(ag, carry[j], cw_ref[j], cb_ref[j], tm, keep.at[0])
            cv, av1, av2 = _conv3(av, carry[nh + j], cw_ref[nh + j], cb_ref[nh + j], tm, keep.at[1])
            carry[j] = ag[tm - HALO:]
            carry[nh + j] = av[tm - HALO:]
            sg = _sigmoid(cg)
            gs = cg * sg
            acc[j * n:(j + 1) * n, :] += _dot_tn((gs * cv).astype(BF16), dhb)
            dcg = dact * cv * (sg + gs * (1.0 - sg))
            dcv = dact * gs
            dc_ref[j] = dcg.astype(BF16)
            dc_ref[nh + j] = dcv.astype(BF16)
            dcw_ref[j] += jnp.concatenate([rsum(dcg * ag2), rsum(dcg * ag1), rsum(dcg * ag)], axis=0)
            dcw_ref[nh + j] += jnp.concatenate([rsum(dcv * av2), rsum(dcv * av1), rsum(dcv * av)], axis=0)
            dcb_ref[j] += rsum(dcg)
            dcb_ref[nh + j] += rsum(dcv)

        @pl.when(i == ni - 1)
        def _():
            pltpu.sync_copy(acc, dw_hbm)
            for j in range(nh):
                stage[...] = acc[j * n:(j + 1) * n, :].astype(BF16)
                pltpu.sync_copy(stage, dwb_hbm.at[pl.ds(j * n, n), :])

    aspec = pl.BlockSpec((nh, tm, n), lambda i: (0, i, 0))
    return _run(
        body, [dh, a_g, a_v, wdown, cw, cb], hook, grid=(ni,), name=name, semantics=("arbitrary",),
        in_specs=[pl.BlockSpec((tm, D), lambda i: (i, 0)), aspec, aspec, _resident(wdown.shape), _resident(cw.shape), _resident(cb.shape)],
        out_specs=[pl.BlockSpec((2 * nh, tm, n), lambda i: (0, i, 0)), ANY, ANY,
                   pl.BlockSpec(cw.shape, lambda i: (0, 0, 0)), pl.BlockSpec(cb.shape, lambda i: (0, 0, 0))],
        out_shape=[S((2 * nh, t, n), BF16), S(wdown.shape, F32), S(wdown.shape, BF16), S(cw.shape, F32), S(cb.shape, F32)],
        scratch_shapes=[pltpu.VMEM((2 * nh, HALO, n), F32), pltpu.VMEM(wdown.shape, F32), pltpu.VMEM((n, D), BF16),
                        pltpu.VMEM((2, 2, tm, n), F32)])


def _ffn_bwd2(dc, wup, cw, h, gain, dh_in, name, hook=None):
    ns, t, n = dc.shape
    tm = _tm(t)
    ni = t // tm

    def body(dc_ref, wu_ref, cw_ref, h_ref, g_ref, di_ref, da_ref, o_ref, dg_ref, carry):
        i = pl.program_id(0)

        @pl.when(i == 0)
        def _():
            carry[...] = jnp.zeros_like(carry)
            dg_ref[...] = jnp.zeros_like(dg_ref)

        acc = jnp.zeros((tm, D), F32)
        for s in range(ns):
            x = dc_ref[s].astype(F32)
            ext = jnp.concatenate([x, carry[s]], axis=0)
            cwv = cw_ref[s]
            da = (cwv[2:3] * x + cwv[1:2] * ext[1:1 + tm] + cwv[0:1] * ext[2:2 + tm]).astype(BF16)
            carry[s] = x[:HALO]
            da_ref[s] = da
            acc = acc + _dot_nt(da, wu_ref[s])
        hv = h_ref[...]
        r = _rstd(hv)
        gg = acc * g_ref[...]
        o_ref[...] = di_ref[...] + r * gg - hv * (r * r * r * jnp.mean(gg * hv, axis=-1, keepdims=True))
        dg_ref[...] += jnp.sum(acc * hv * r, axis=0, keepdims=True)

    slab = pl.BlockSpec((ns, tm, n), lambda i: (0, ni - 1 - i, 0))
    row = pl.BlockSpec((tm, D), lambda i: (ni - 1 - i, 0))
    vec = pl.BlockSpec((1, D), lambda i: (0, 0))
    return _run(
        body, [dc, wup, cw, h, gain, dh_in], hook, grid=(ni,), name=name, semantics=("arbitrary",),
        in_specs=[slab, _resident(wup.shape), _resident(cw.shape), row, vec, row],
        out_specs=[slab, row, vec], out_shape=[S((ns, t, n), BF16), S((t, D), F32), S((1, D), F32)],
        scratch_shapes=[pltpu.VMEM((ns, HALO, n), F32)])


def _dw_slot(hn, dy_s, name, hook=None):
    t, k = hn.shape
    ns, _, n = dy_s.shape
    tm = _tm(t)

    def body(a_ref, b_ref, o_ref, ob_ref, at_ref):
        @pl.when(pl.program_id(0) == 0)
        def _():
            for i in range(t // tm):
                at_ref[:, i * tm:(i + 1) * tm] = a_ref[i * tm:(i + 1) * tm, :].T

        acc = _dot(at_ref[...], b_ref[...])
        o_ref[...] = acc
        ob_ref[...] = acc.astype(BF16)

    ospec = pl.BlockSpec((None, k, n), lambda j: (j, 0, 0))
    return _run(
        body, [hn, dy_s], hook, grid=(ns,), name=name, semantics=("arbitrary",),
        in_specs=[_resident(hn.shape), pl.BlockSpec((None, t, n), lambda j: (j, 0, 0))],
        out_specs=[ospec, ospec], out_shape=[S((ns, k, n), F32), S((ns, k, n), BF16)],
        scratch_shapes=[pltpu.VMEM((k, t), BF16)])


def _dw_rows(a_s, dh, name, hook=None, fm=False):
    nk, t, kc = (1, a_s.shape[1], a_s.shape[0]) if fm else a_s.shape
    tm = _tm(t)
    ni = t // tm

    def body(a_ref, d_ref, o_ref, ob_ref):
        i = pl.program_id(0)
        dhb = d_ref[...].astype(BF16)

        @pl.when(i == 0)
        def _():
            o_ref[...] = jnp.zeros_like(o_ref)

        if fm:
            o_ref[...] += _dot(a_ref[...], dhb)
        for j in range(0 if fm else nk):
            o_ref[j * kc:(j + 1) * kc, :] += _dot_tn(a_ref[j], dhb)

        @pl.when(i == ni - 1)
        def _():
            ob_ref[...] = o_ref[...].astype(BF16)

    ospec = pl.BlockSpec((nk * kc, D), lambda i: (0, 0))
    return _run(
        body, [a_s, dh], hook, grid=(ni,), name=name, semantics=("arbitrary",),
        in_specs=[pl.BlockSpec((kc, tm), lambda i: (0, i)) if fm else pl.BlockSpec((nk, tm, kc), lambda i: (0, i, 0)),
                  pl.BlockSpec((tm, D), lambda i: (i, 0))],
        out_specs=[ospec, ospec], out_shape=[S((nk * kc, D), F32), S((nk * kc, D), BF16)])


def _dx_slot_normbwd(dy_s, wg, h, gain, dh_in, name, hook=None, fm=False):
    ns, t, n = (1, dy_s.shape[1], dy_s.shape[0]) if fm else dy_s.shape
    tm = _tm(t)

    def body(dy_ref, w_ref, h_ref, g_ref, di_ref, o_ref, dg_ref):
        i = pl.program_id(0)

        @pl.when(i == 0)
        def _():
            dg_ref[...] = jnp.zeros_like(dg_ref)

        g = _dot_tn(dy_ref[...], w_ref[...]) if fm else _dot_nt(dy_ref[0], w_ref[0])
        for s in range(1, ns):
            g = g + _dot_nt(dy_ref[s], w_ref[s])
        hv = h_ref[...]
        r = _rstd(hv)
        gg = g * g_ref[...]
        o_ref[...] = di_ref[...] + r * gg - hv * (r * r * r * jnp.mean(gg * hv, axis=-1, keepdims=True))
        dg_ref[...] += jnp.sum(g * hv * r, axis=0, keepdims=True)

    row = pl.BlockSpec((tm, D), lambda i: (i, 0))
    vec = pl.BlockSpec((1, D), lambda i: (0, 0))
    return _run(
        body, [dy_s, wg, h, gain, dh_in], hook, grid=(t // tm,), name=name, semantics=("arbitrary",),
        in_specs=[pl.BlockSpec((n, tm), lambda i: (0, i)) if fm else pl.BlockSpec((ns, tm, n), lambda i: (0, i, 0)),
                  _resident(wg.shape), row, vec, row],
        out_specs=[row, vec], out_shape=[S((t, D), F32), S((1, D), F32)])


def _sgu_gate_bwd(a_s, dg_s, vgain, ws, bst, name, hook=None):
    t = a_s.shape[1]
    sw = a_s.shape[2]
    gps = sw // CHUNK

    def body(a_ref, dg_ref, vg_ref, ws_ref, b_ref, da_ref, dws_ref, dbt_ref, dvg_ref, dvn_ref):
        n = pl.program_id(0)

        @pl.when(n == 0)
        def _():
            dws_ref[...] = jnp.zeros_like(dws_ref)
            dbt_ref[...] = jnp.zeros_like(dbt_ref)
            dvg_ref[...] = jnp.zeros_like(dvg_ref)

        vpre = jnp.concatenate([a_ref[4 + s].astype(F32) for s in range(4)], axis=1)
        v = _gelu(vpre)
        r = _rstd(v)
        vhat = v * r
        vn = (vhat * vg_ref[...]).astype(BF16)
        tri = _tril_mask()
        lane = lax.broadcasted_iota(jnp.int32, (CHUNK, CHUNK), 1)
        dbt = jnp.zeros((CHUNK, CHUNK), F32)
        for g in range(SGU_G):
            w = jnp.where(tri, ws_ref[g], 0.0).astype(BF16)
            vng = vn[:, g * CHUNK:(g + 1) * CHUNK]
            sg = _dot(w, vng) + b_ref[:, g:g + 1]
            lo = (g % gps) * CHUNK
            upre = a_ref[g // gps, :, lo:lo + CHUNK].astype(F32)
            dgate = dg_ref[g // gps, :, lo:lo + CHUNK].astype(F32)
            da_ref[g // gps, :, lo:lo + CHUNK] = (dgate * sg * _gelu_grad(upre)).astype(BF16)
            ds = dgate * _gelu(upre)
            dsb = ds.astype(BF16)
            dvn_ref[:, g * CHUNK:(g + 1) * CHUNK] = _dot_tn(w, dsb)
            dws_ref[g] += jnp.where(tri, _dot_nt(dsb, vng), 0.0)
            dbt = dbt + jnp.where(lane == g, jnp.sum(ds, axis=-1, keepdims=True), 0.0)
        dbt_ref[...] += dbt
        dvn = dvn_ref[...]
        dvg_ref[...] += jnp.sum(dvn * vhat, axis=0, keepdims=True)
        gg = dvn * vg_ref[...]
        dv = r * gg - v * (r * r * r * jnp.mean(gg * v, axis=-1, keepdims=True))
        dav = (dv * _gelu_grad(vpre)).astype(BF16)
        for s in range(4):
            da_ref[4 + s] = dav[:, s * sw:(s + 1) * sw]

    return _run(
        body, [a_s, dg_s, vgain, ws, bst], hook, grid=(t // CHUNK,), name=name, semantics=("arbitrary",),
        in_specs=[pl.BlockSpec((8, CHUNK, sw), lambda n: (0, n, 0)), pl.BlockSpec((4, CHUNK, sw), lambda n: (0, n, 0)),
                  pl.BlockSpec((1, SGU_W), lambda n: (0, 0)), pl.BlockSpec((SGU_G, CHUNK, CHUNK), lambda n: (0, 0, 0)),
                  pl.BlockSpec((CHUNK, SGU_G), lambda n: (0, 0))],
        out_specs=[pl.BlockSpec((8, CHUNK, sw), lambda n: (0, n, 0)), pl.BlockSpec((SGU_G, CHUNK, CHUNK), lambda n: (0, 0, 0)),
                   pl.BlockSpec((CHUNK, CHUNK), lambda n: (0, 0)), pl.BlockSpec((1, SGU_W), lambda n: (0, 0))],
        out_shape=[S((8, t, sw), BF16), S((SGU_G, CHUNK, CHUNK), F32), S((CHUNK, CHUNK), F32), S((1, SGU_W), F32)],
        scratch_shapes=[pltpu.VMEM((CHUNK, SGU_W), F32)])


def _attn_bwd(qkv_t, do_t, qg, kg, sinks, bias, name, hook=None):
    t = qkv_t.shape[1]
    nb = t // CHUNK

    def body(cur_ref, prev_ref, do_ref, qg_ref, kg_ref, sink_ref, bias_ref,
             o_ref, dqg_out, dkg_out, dsk_out, dbias_ref, carry, dqg_ref, dkg_ref, dsk_ref):
        n = pl.program_id(0)

        @pl.when(n == 0)
        def _():
            carry[...] = jnp.zeros_like(carry)
            dqg_ref[...] = jnp.zeros_like(dqg_ref)
            dkg_ref[...] = jnp.zeros_like(dkg_ref)
            dsk_ref[...] = jnp.zeros_like(dsk_ref)
            dbias_ref[...] = jnp.zeros_like(dbias_ref)

        @pl.when(n < nb)
        def _():
            valid = _attn_valid(n)
            o_ref[0:KV0, :] = carry[0:KV0, :].astype(BF16)
            for h in range(NKV):
                krow = KV0 + HD * h
                vrow = KV0 + HD * (NKV + h)
                k = _attn_band(cur_ref, prev_ref, krow)
                v = _attn_band(cur_ref, prev_ref, vrow)
                rk = _rstd_rows(k)
                khat = k * rk
                kn = (khat * kg_ref[...]).astype(BF16)
                kn_tok = kn.T
                vb = v.astype(BF16)
                v_tok = vb.T
                heads = range(KVG * h, KVG * (h + 1))
                qs = [cur_ref[HD * hq:HD * (hq + 1), :] for hq in heads]
                rqs = [_rstd_rows(q) for q in qs]
                qhats = [q * rq for q, rq in zip(qs, rqs)]
                qns = [(qhat * qg_ref[...]).astype(BF16) for qhat in qhats]
                probs = [_attn_probs(kn_tok, qn, bias_ref[hq], valid, sink_ref[hq]) for qn, hq in zip(qns, heads)]
                dohs = [do_ref[HD * hq:HD * (hq + 1), :] for hq in heads]
                dps = [_dot(v_tok, doh) for doh in dohs]
                dsums = [jnp.sum(p * dp, axis=0, keepdims=True) for (p, _), dp in zip(probs, dps)]
                dss = [p * (dp - dsum) for (p, _), dp, dsum in zip(probs, dps, dsums)]
                for hq, (_, psink), dsum, ds in zip(heads, probs, dsums, dss):
                    dsk_ref[hq:hq + 1, :] -= psink * dsum
                    dbias_ref[hq] += ds
                dv = sum(_dot_nt(doh, p.astype(BF16)) for doh, (p, _) in zip(dohs, probs))
                dscs = [(ds * (HD ** -0.5)).astype(BF16) for ds in dss]
                dqns = [_dot(kn, dsc) for dsc in dscs]
                dkn = sum(_dot_nt(qn, dsc) for qn, dsc in zip(qns, dscs))
                dqg_ref[...] += sum(dqn * qhat for dqn, qhat in zip(dqns, qhats))
                for hq, q, rq, dqn in zip(heads, qs, rqs, dqns):
                    gq = dqn * qg_ref[...]
                    carry[HD * hq:HD * (hq + 1), :] = rq * gq - q * (rq * rq * rq * jnp.mean(gq * q, axis=0, keepdims=True))
                dkg_ref[...] += dkn * khat
                gk = dkn * kg_ref[...]
                dk = rk * gk - k * (rk * rk * rk * jnp.mean(gk * k, axis=0, keepdims=True))
                o_ref[krow:krow + HD, :] = (carry[krow:krow + HD, :] + dk[:, :CHUNK]).astype(BF16)
                o_ref[vrow:vrow + HD, :] = (carry[vrow:vrow + HD, :] + dv[:, :CHUNK]).astype(BF16)
                carry[krow:krow + HD, :] = dk[:, CHUNK:]
                carry[vrow:vrow + HD, :] = dv[:, CHUNK:]

        @pl.when(n == nb)
        def _():
            o_ref[...] = carry[...].astype(BF16)
            dqg_out[...] = jnp.sum(dqg_ref[...], axis=1, keepdims=True)
            dkg_out[...] = jnp.sum(dkg_ref[...], axis=1, keepdims=True)
            dsk_out[...] = jnp.sum(dsk_ref[...], axis=1, keepdims=True)

    cur = lambda n: (0, jnp.minimum(n, nb - 1))
    col = pl.BlockSpec((HD, 1), lambda n: (0, 0))
    whole = lambda shape: pl.BlockSpec(shape, lambda n: (0,) * len(shape))
    return _run(
        body, [qkv_t, qkv_t, do_t, qg, kg, sinks, bias], hook, grid=(nb + 1,), name=name, semantics=("arbitrary",),
        in_specs=[pl.BlockSpec((QKV, CHUNK), cur),
                  pl.BlockSpec((QKV - KV0, CHUNK), lambda n: (KV0 // (QKV - KV0), jnp.clip(n - 1, 0, nb - 1))),
                  pl.BlockSpec((D, CHUNK), cur), col, col, pl.BlockSpec(memory_space=pltpu.SMEM), whole((NH, 2 * CHUNK, CHUNK))],
        out_specs=[pl.BlockSpec((QKV, CHUNK), lambda n: (0, jnp.maximum(n - 1, 0))), whole((HD, 1)), whole((HD, 1)),
                   whole((NH, 1)), whole((NH, 2 * CHUNK, CHUNK))],
        out_shape=[S((QKV, t), BF16), S((HD, 1), F32), S((HD, 1), F32), S((NH, 1), F32), S((NH, 2 * CHUNK, CHUNK), F32)],
        scratch_shapes=[pltpu.VMEM((QKV, CHUNK), F32), pltpu.VMEM((HD, CHUNK), F32), pltpu.VMEM((HD, 2 * CHUNK), F32),
                        pltpu.VMEM((NH, CHUNK), F32)])


class _Plain:
    def __init__(self, wg):
        self.full, self.grads = wg, {}

    def w(self, n):
        return self.full[n]

    def hook(self, host):
        return None

    def grad(self, n, pair):
        self.grads[n] = pair

    def small(self, g_rep):
        pass


def _local_step(x, target, rep, sch):
    bucket_row = jnp.asarray(_rel_tables().T.reshape(1, -1))
    bias = _relbias_fwd(rep["rel_bias"].T, bucket_row, "relbias_fwd").reshape(NH, 2 * CHUNK, CHUNK)
    bst = rep["sgu_b_s"][0].T
    ws = rep["sgu_w_s"][0]
    vgain = rep["sgu_v_gain"]
    qg, kg, sinks = rep["attn_q_gain"].reshape(HD, 1), rep["attn_k_gain"].reshape(HD, 1), rep["attn_sinks"][0]
    w_down = lambda l: sch.w("ffn_w_down%d" % l).reshape(D_FF, D)
    w_up = lambda l: sch.w("ffn_w_up%d" % l)
    cw = [sch.w("ffn_conv_w")[:, 3 * l:3 * l + 3] for l in range(2)]
    cb = [rep["ffn_conv_b"][l].reshape(8, 1, -1) for l in range(2)]
    mixg = [rep["mix_norm"][l:l + 1] for l in range(2)]
    ffng = [rep["ffn_norm"][l:l + 1] for l in range(2)]
    rows = lambda pair: tuple(g.reshape(N_DEV, -1, D) for g in pair)
    hk = sch.hook

    hn0 = _rmsnorm(x, mixg[0], "norm0")
    a0 = _mm_slot(hn0, sch.w("sgu_w_in"), BF16, "sgu_in", hk("sgu_in"))
    gated = _sgu_gate_fwd(a0, vgain, ws, bst, "sgu_gate", hk("sgu_gate"))
    h1, hn1 = _resid_mm(gated, sch.w("sgu_w_out").reshape(SGU_W, D), x, ffng[0], "norm", "sgu_out", hk("sgu_out"))
    ag0, av0, h2, hn2 = _ffn_fwd(hn1, h1, w_up(0), w_down(0), cw[0], cb[0], mixg[1], "norm", "ffn0_fwd", hk("ffn0_fwd"))
    qkv = _mm_t(hn2, sch.w("attn_w_qkv"), "qkv", hk("qkv"))
    o = _attn_fwd(qkv, qg, kg, sinks, bias, "attn", hk("attn"))
    h3, hn3 = _resid_mm(o, sch.w("attn_w_o").reshape(D, D), h2, ffng[1], "norm", "attn_out", hk("attn_out"), fm=True)
    ag1, av1, dy, sq = _ffn_fwd(hn3, h3, w_up(1), w_down(1), cw[1], cb[1], target, "loss", "ffn1_fwd_loss", hk("ffn1_fwd_loss"))
    loss = (0.5 / D) * jnp.sum(sq[:, 0, 0])

    def ffn_bwd(dh, h_in, hn, a_g, a_v, l, tag):
        dc, g_down, g_down_b, g_cw, g_cb = _ffn_bwd1(dh, a_g, a_v, w_down(l), cw[l], cb[l], tag + "_bwd1", hk(tag + "_bwd1"))
        sch.grad("ffn_w_down%d" % l, rows((g_down, g_down_b)))
        da, dh_new, dgain = _ffn_bwd2(dc, w_up(l), cw[l], h_in, ffng[l], dh, tag + "_bwd2", hk(tag + "_bwd2"))
        sch.grad("ffn_w_up%d" % l, _dw_slot(hn, da, tag + "_dw_up", hk(tag + "_dw_up")))
        return dh_new, dgain, g_cw, g_cb.reshape(-1)

    dh, d_ffng1, g_cw1, g_cb1 = ffn_bwd(dy, h3, hn3, ag1, av1, 1, "ffn1")
    do = _dx_rows_t(dh, sch.w("attn_w_o").reshape(D, D), "attn_do", hk("attn_do"))
    sch.grad("attn_w_o", rows(_dw_rows(o, dh, "dw_o", hk("dw_o"), fm=True)))
    dqkv, d_qg, d_kg, d_sk, d_bias = _attn_bwd(qkv, do, qg, kg, sinks, bias, "attn_bwd", hk("attn_bwd"))
    sch.grad("attn_w_qkv", tuple(g.reshape(N_DEV, -1, D) for g in _dw_rows(dqkv, hn2, "dw_qkv", hk("dw_qkv"), fm=True)))
    dh, d_mixg1 = _dx_slot_normbwd(dqkv, sch.w("attn_w_qkv").reshape(QKV, D), h2, mixg[1], dh, "dx_qkv", hk("dx_qkv"), fm=True)
    d_relb = _relbias_bwd(d_bias.reshape(NH, -1), bucket_row, "relbias_bwd").T
    dh, d_ffng0, g_cw0, g_cb0 = ffn_bwd(dh, h1, hn1, ag0, av0, 0, "ffn0")
    g_cw = jnp.concatenate([g_cw0, g_cw1], axis=1)
    sch.grad("ffn_conv_w", (g_cw, g_cw.astype(BF16)))
    dgated = _dx_rows(dh, sch.w("sgu_w_out").reshape(SGU_W, D), SGU_W // 4, BF16, "sgu_dgated", hk("sgu_dgated"))
    sch.grad("sgu_w_out", rows(_dw_rows(gated, dh, "dw_sgu_out", hk("dw_sgu_out"))))
    da0, d_ws, d_bst, d_vgain = _sgu_gate_bwd(a0, dgated, vgain, ws, bst, "sgu_gate_bwd", hk("sgu_gate_bwd"))
    grad_x, d_mixg0 = _dx_slot_normbwd(da0, sch.w("sgu_w_in"), x, mixg[0], dh, "dx_sgu_in")
    g_rep = {
        "mix_norm": jnp.concatenate([d_mixg0, d_mixg1], axis=0),
        "ffn_norm": jnp.concatenate([d_ffng0, d_ffng1], axis=0),
        "sgu_v_gain": d_vgain,
        "sgu_w_s": d_ws[None],
        "sgu_b_s": d_bst[:, :SGU_G].T[None],
        "attn_q_gain": d_qg.reshape(1, HD),
        "attn_k_gain": d_kg.reshape(1, HD),
        "attn_sinks": d_sk.reshape(1, NH),
        "rel_bias": d_relb,
        "ffn_conv_b": jnp.stack([g_cb0, g_cb1], axis=0),
    }
    sch.small(g_rep)
    sch.grad("sgu_w_in", _dw_slot(hn0, da0, "dw_sgu_in", hk("dw_sgu_in")))
    return loss, grad_x, g_rep


def _allgather(xs, name):
    nt = len(xs)

    def body(*refs):
        x_refs, o_refs = refs[:nt], refs[nt:2 * nt]
        send_sems, recv_sems, local_sems = refs[2 * nt:]
        x, y, c, chips = _place()
        me, sibling = (x, y, c), (x, y, 1 - c)

        def copy(t, k, block, to, src=None):
            px, py, pc = block
            dst = o_refs[t].at[4 * px + 2 * py + pc]
            return pltpu.make_async_remote_copy(
                src_ref=dst if src is None else src, dst_ref=dst, send_sem=send_sems.at[t, k], recv_sem=recv_sems.at[t, k],
                device_id=to, device_id_type=MESH)

        mine = [pltpu.make_async_copy(x_refs[t], o_refs[t].at[4 * x + 2 * y + c], local_sems.at[t]) for t in range(nt)]
        for cp in mine:
            cp.start()
        first = []
        for t in range(nt):
            first.append(copy(t, 0, me, sibling, src=x_refs[t]))
            first += [copy(t, 1 + j, me, (*chip, c), src=x_refs[t]) for j, chip in enumerate(chips)]
        for cp in first:
            cp.start()
        passed = []
        for j, chip in enumerate(chips):
            for t in range(nt):
                copy(t, 1 + j, (*chip, c), me).wait_recv()
                fwd = copy(t, 4 + j, (*chip, c), sibling)
                fwd.start()
                passed.append(fwd)
        for t in range(nt):
            copy(t, 0, sibling, me).wait_recv()
            for j, chip in enumerate(chips):
                copy(t, 4 + j, (*chip, 1 - c), me).wait_recv()
        for cp in first + passed:
            cp.wait_send()
        for cp in mine:
            cp.wait()

    return pl.pallas_call(
        body, name=name, in_specs=[ANY] * nt, out_specs=[ANY] * nt,
        out_shape=[S((N_DEV,) + a.shape, a.dtype) for a in xs],
        scratch_shapes=[pltpu.SemaphoreType.DMA((nt, 7)), pltpu.SemaphoreType.DMA((nt, 7)), pltpu.SemaphoreType.DMA((nt,))],
        compiler_params=pltpu.CompilerParams(has_side_effects=True))(*xs)


def _exchange(hook, name):
    comm = hook()
    ci, co = len(comm.inputs), len(comm.out_shapes)

    def body(*refs):
        cins, couts = refs[:ci], refs[ci:ci + co]
        send, recv = refs[-2:]
        comm.start(cins, couts, send, recv)
        comm.finish(cins, couts, send, recv)

    res = pl.pallas_call(
        body, name=name, in_specs=[ANY] * ci, out_specs=[ANY] * co, out_shape=comm.out_shapes,
        scratch_shapes=[pltpu.SemaphoreType.DMA((comm.n_sems,)), pltpu.SemaphoreType.DMA((comm.n_sems,))],
        input_output_aliases=dict(comm.aliases),
        compiler_params=pltpu.CompilerParams(has_side_effects=True))(*comm.inputs)
    hook(res)


def _row_tile(r):
    tr = r if r <= ROW_TILE or r % ROW_TILE else ROW_TILE
    assert r % tr == 0
    return tr


def _rs_partial(g32, sib, place, name):
    _, r, cdim = g32.shape
    tr = _row_tile(r)

    def body(place_ref, g_ref, s_ref, p_ref, own_ref):
        k = pl.program_id(1)
        tot = g_ref[...] + s_ref[...].astype(F32)
        p_ref[...] = tot.astype(BF16)

        @pl.when(k == place_ref[1])
        def _():
            own_ref[...] = tot

    grid_spec = pltpu.PrefetchScalarGridSpec(
        num_scalar_prefetch=1, grid=(r // tr, 4),
        in_specs=[pl.BlockSpec((None, None, tr, cdim), lambda i, k, pr: (k, pr[0], i, 0)),
                  pl.BlockSpec((None, tr, cdim), lambda i, k, pr: (k, i, 0))],
        out_specs=[pl.BlockSpec((None, tr, cdim), lambda i, k, pr: (k, i, 0)), pl.BlockSpec((tr, cdim), lambda i, k, pr: (i, 0))])
    return pl.pallas_call(
        body, grid_spec=grid_spec, name=name,
        out_shape=[S((4, r, cdim), BF16), S((r, cdim), F32)],
        compiler_params=_cp("parallel", "arbitrary"))(place, g32.reshape(4, 2, r, cdim), sib)


def _adamw_math(w, g, m, v):
    m = ADAM_B1 * m + (1.0 - ADAM_B1) * g
    v = ADAM_B2 * v + (1.0 - ADAM_B2) * (g * g)
    m_hat = m / (1.0 - ADAM_B1 ** ADAM_STEP)
    v_hat = v / (1.0 - ADAM_B2 ** ADAM_STEP)
    delta = -ADAM_LR * (m_hat / (jnp.sqrt(v_hat) + ADAM_EPS) + ADAM_WD * w)
    return delta, m, v


def _adamw_shard(owns, recvs, w, m, v, name, flipped=False):
    nl = w.shape[0]
    r, cdim = owns[0].shape
    tr = _row_tile(r)
    nr = r // tr

    def body(*refs):
        own_refs, recv_refs = refs[:nl], refs[nl:2 * nl]
        w_ref, m_ref, v_ref, g_out, d_out, m_out, v_out = refs[2 * nl:]
        layer = pl.program_id(0)
        g = None
        for l in range(nl):
            gl = own_refs[l][...] + recv_refs[l][0].astype(F32) + recv_refs[l][1].astype(F32) + recv_refs[l][2].astype(F32)
            g = gl if g is None else jnp.where(layer == l, gl, g)
        if flipped:
            g = g.T
        g_out[...] = g
        d_out[...], m_out[...], v_out[...] = _adamw_math(w_ref[...], g, m_ref[...], v_ref[...])

    park = lambda l: (lambda layer, i: (jnp.where(layer == l, i, jnp.where(layer < l, 0, nr - 1)), 0))
    park3 = lambda l: (lambda layer, i: (0, jnp.where(layer == l, i, jnp.where(layer < l, 0, nr - 1)), 0))
    if flipped:
        row = pl.BlockSpec((None, cdim, tr), lambda layer, i: (layer, 0, i))
    else:
        row = pl.BlockSpec((None, tr, cdim), lambda layer, i: (layer, i, 0))
    return pl.pallas_call(
        body, grid=(nl, nr), name=name,
        in_specs=[pl.BlockSpec((tr, cdim), park(l)) for l in range(nl)] + [pl.BlockSpec((3, tr, cdim), park3(l)) for l in range(nl)]
        + [row, row, row],
        out_specs=[row] * 4, out_shape=[S(w.shape, F32)] * 4,
        compiler_params=_cp("arbitrary", "arbitrary"))(*owns, *recvs, w, m, v)


def _adamw_small(galls, ws, ms, vs, name):
    n = len(galls)

    def body(*refs):
        g_refs, w_refs, m_refs, v_refs, outs = refs[:n], refs[n:2 * n], refs[2 * n:3 * n], refs[3 * n:4 * n], refs[4 * n:]
        for i in range(n):
            g = g_refs[i][0]
            for s in range(1, N_DEV):
                g = g + g_refs[i][s]
            outs[i][...] = g
            outs[n + i][...], outs[2 * n + i][...], outs[3 * n + i][...] = _adamw_math(w_refs[i][...], g, m_refs[i][...], v_refs[i][...])

    res = pl.pallas_call(body, out_shape=[S(a.shape, F32) for a in ws] * 4, name=name)(*galls, *ws, *ms, *vs)
    return [res[k * n:(k + 1) * n] for k in range(4)]


REPLICATED = ["mix_norm", "ffn_norm", "sgu_v_gain", "sgu_w_s", "sgu_b_s", "attn_q_gain", "attn_k_gain", "attn_sinks", "rel_bias",
              "ffn_conv_b"]
WEIGHTS = ["mix_norm", "ffn_norm", "sgu_w_in", "sgu_v_gain", "sgu_w_s", "sgu_b_s", "sgu_w_out", "attn_w_qkv", "attn_q_gain",
           "attn_k_gain", "attn_sinks", "attn_w_o", "rel_bias", "ffn_w_up", "ffn_conv_w", "ffn_conv_b", "ffn_w_down"]
SMALL = ["g_" + n for n in REPLICATED]

GATHER_FIRST = ["sgu_w_in", "ffn_conv_w"]
PLAN = {
    "sgu_in": [("ag1", "sgu_w_out"), ("ag1", "ffn_w_down0")],
    "sgu_gate": [("ag2", "sgu_w_out"), ("ag2", "ffn_w_down0"), ("ag1", "ffn_w_up0")],
    "sgu_out": [("ag2", "ffn_w_up0"), ("ag1", "attn_w_qkv")],
    "ffn0_fwd": [("ag2", "attn_w_qkv"), ("ag1", "attn_w_o"), ("ag1", "ffn_w_up1")],
    "qkv": [("ag2", "attn_w_o"), ("ag2", "ffn_w_up1")],
    "attn": [("ag1", "ffn_w_down1")],
    "attn_out": [("ag2", "ffn_w_down1")],
    "ffn1_bwd2": [("rs1", "ffn_w_down1")],
    "ffn1_dw_up": [("rs2", "ffn_w_down1")],
    "attn_do": [("rs1", "ffn_w_up1")],
    "attn_bwd": [("rs2", "ffn_w_up1"), ("rs1", "attn_w_o")],
    "dw_qkv": [("rs2", "attn_w_o")],
    "dx_qkv": [("rs1", "attn_w_qkv")],
    "ffn0_bwd1": [("rs2", "attn_w_qkv")],
    "ffn0_bwd2": [("rs1", "ffn_w_down0")],
    "ffn0_dw_up": [("rs2", "ffn_w_down0")],
    "sgu_dgated": [("rs1", "ffn_w_up0")],
    "sgu_gate_bwd": [("rs2", "ffn_w_up0"), ("rs1", "sgu_w_out")],
    "dw_sgu_in": [("rs2", "sgu_w_out")] + [("ag1", n) for n in SMALL],
    "last_a": [("rs1", "sgu_w_in"), ("rs1", "ffn_conv_w")] + [("ag2", n) for n in SMALL],
    "last_b": [("rs2", "sgu_w_in"), ("rs2", "ffn_conv_w")],
}


class _Overlap:
    def __init__(self, shard, place):
        self.shard, self.place = shard, place
        self.part, self.full = {}, {}
        self.grads, self.sib, self.own, self.recv = {}, {}, {}, {}

    def w(self, n):
        return self.full[n]

    def grad(self, n, pair):
        self.grads[n] = pair

    def small(self, g_rep):
        self.shard.update(("g_" + n, a) for n, a in _views2d(g_rep).items())

    def chip_sums(self, n):
        sums, self.own[n] = _rs_partial(self.grads[n][0], self.sib.pop(n), self.place, "rs_partial_" + n)
        return sums

    def hook(self, host):
        ops = PLAN.get(host)
        if not ops:
            return None
        where = {"ag1": self.part, "ag2": self.full, "rs1": self.sib, "rs2": self.recv}
        idx = []

        def hook(results=None):
            if results is not None:
                for (kind, n), i in zip(ops, idx):
                    where[kind][n] = results[i]
                return None
            comm = _Comm()
            for kind, n in ops:
                arr = {"ag1": lambda: self.shard[n], "ag2": lambda: self.part.pop(n), "rs1": lambda: self.grads[n][1],
                       "rs2": lambda: self.chip_sums(n)}[kind]()
                idx.append(comm.add(kind, arr))
            return comm

        return hook


TRANSPOSED = {"attn_w_qkv"}
PHYSICAL_T = {"ffn_w_up"}
SHARDED = {
    "sgu_w_in": ["sgu_w_in"], "sgu_w_out": ["sgu_w_out"], "attn_w_qkv": ["attn_w_qkv"], "attn_w_o": ["attn_w_o"],
    "ffn_w_up": ["ffn_w_up0", "ffn_w_up1"], "ffn_w_down": ["ffn_w_down0", "ffn_w_down1"], "ffn_conv_w": ["ffn_conv_w"],
}


def _send_views(w):
    out = {"ffn_conv_w": w["ffn_conv_w"].reshape(6, -1)}
    for name, parts in SHARDED.items():
        if name != "ffn_conv_w":
            out.update((p, (w[name][l].T if name in TRANSPOSED else w[name][l]).astype(BF16)) for l, p in enumerate(parts))
    return out


def _views2d(d):
    return {n: d[n].reshape(-1, d[n].shape[-1]) for n in REPLICATED}


def kernel(x, mix_norm, ffn_norm, sgu_w_in, sgu_v_gain, sgu_w_s, sgu_b_s, sgu_w_out, attn_w_qkv, attn_q_gain, attn_k_gain, attn_sinks, attn_w_o, rel_bias, ffn_w_up, ffn_conv_w, ffn_conv_b, ffn_w_down, loss_target, m_mix_norm, m_ffn_norm, m_sgu_w_in, m_sgu_v_gain, m_sgu_w_s, m_sgu_b_s, m_sgu_w_out, m_attn_w_qkv, m_attn_q_gain, m_attn_k_gain, m_attn_sinks, m_attn_w_o, m_rel_bias, m_ffn_w_up, m_ffn_conv_w, m_ffn_conv_b, m_ffn_w_down, v_mix_norm, v_ffn_norm, v_sgu_w_in, v_sgu_v_gain, v_sgu_w_s, v_sgu_b_s, v_sgu_w_out, v_attn_w_qkv, v_attn_q_gain, v_attn_k_gain, v_attn_sinks, v_attn_w_o, v_rel_bias, v_ffn_w_up, v_ffn_conv_w, v_ffn_conv_b, v_ffn_w_down):
    w = dict(zip(WEIGHTS, (mix_norm, ffn_norm, sgu_w_in, sgu_v_gain, sgu_w_s, sgu_b_s, sgu_w_out, attn_w_qkv, attn_q_gain, attn_k_gain,
                           attn_sinks, attn_w_o, rel_bias, ffn_w_up, ffn_conv_w, ffn_conv_b, ffn_w_down)))
    m = dict(zip(WEIGHTS, (m_mix_norm, m_ffn_norm, m_sgu_w_in, m_sgu_v_gain, m_sgu_w_s, m_sgu_b_s, m_sgu_w_out, m_attn_w_qkv, m_attn_q_gain,
                           m_attn_k_gain, m_attn_sinks, m_attn_w_o, m_rel_bias, m_ffn_w_up, m_ffn_conv_w, m_ffn_conv_b, m_ffn_w_down)))
    v = dict(zip(WEIGHTS, (v_mix_norm, v_ffn_norm, v_sgu_w_in, v_sgu_v_gain, v_sgu_w_s, v_sgu_b_s, v_sgu_w_out, v_attn_w_qkv, v_attn_q_gain,
                           v_attn_k_gain, v_attn_sinks, v_attn_w_o, v_rel_bias, v_ffn_w_up, v_ffn_conv_w, v_ffn_conv_b, v_ffn_w_down)))
    rep = {n: w[n] for n in REPLICATED}

    xi, yi, ci = lax.axis_index("x"), lax.axis_index("y"), lax.axis_index("c")
    place = jnp.stack([ci, 2 * xi + yi]).astype(jnp.int32)
    sch = _Overlap(_send_views(w), place)
    sch.full.update(zip(GATHER_FIRST, _allgather([sch.shard[n] for n in GATHER_FIRST], "gather_first")))

    loss, grad_x, g_rep = _local_step(x[0], loss_target[0], rep, sch)
    loss = lax.psum(loss, ("x", "y", "c"))
    _exchange(sch.hook("last_a"), "last_a")
    _exchange(sch.hook("last_b"), "last_b")

    out = [{}, {}, {}, {}]
    for name, parts in SHARDED.items():
        flip = (lambda a: jnp.swapaxes(a, -1, -2)) if name in TRANSPOSED | PHYSICAL_T else (lambda a: a)
        shape = flip(w[name]).shape
        as3d = lambda a: flip(a).reshape(len(parts), -1, shape[-1])
        res = _adamw_shard([sch.own[p] for p in parts], [sch.recv[p] for p in parts], as3d(w[name]), as3d(m[name]), as3d(v[name]),
                           "adamw_" + name, flipped=name in PHYSICAL_T)
        for o, r in zip(out, res):
            o[name] = flip(r.reshape(shape))
    small = _adamw_small([sch.full[n] for n in SMALL], *[list(_views2d(d).values()) for d in (rep, m, v)], "adamw_small")
    for o, res in zip(out, small):
        o.update((n, r.reshape(w[n].shape)) for n, r in zip(REPLICATED, res))

    return (loss, grad_x[None], *[out[0][n] for n in WEIGHTS], *[out[1][n] for n in WEIGHTS],
            *[out[2][n] for n in WEIGHTS], *[out[3][n] for n in WEIGHTS])
```

```python
import functools
import math

import numpy as np
import jax
import jax.numpy as jnp
from jax import lax
from jax.experimental import pallas as pl
from jax.experimental.pallas import tpu as pltpu

F32 = jnp.float32
BF16 = jnp.bfloat16
S = jax.ShapeDtypeStruct

D = 1024
CHUNK = 128
SGU_W = 2048
SGU_G = 16
HD = 64
NH = 16
NKV = 4
KVG = 4
D_FF = 2816
REL_BUCKETS = 32
REL_MAX_DIST = 128
EPS = 1e-6
N_DEV = 8
MESH = pl.DeviceIdType.MESH

ADAM_LR = 0.001
ADAM_B1 = 0.9
ADAM_B2 = 0.999
ADAM_EPS = 1e-08
ADAM_WD = 0.01
ADAM_STEP = 10

ROW_TILE = 512
HALO = 8
FFN_ROWS = 256


def _tm(t):
    return min(ROW_TILE, t)


def _cp(*sem):
    return pltpu.CompilerParams(dimension_semantics=sem)


ANY = pl.BlockSpec(memory_space=pl.ANY)


def _place():
    x, y, c = lax.axis_index("x"), lax.axis_index("y"), lax.axis_index("c")
    return x, y, c, [(1 - x, y), (x, 1 - y), (1 - x, 1 - y)]


class _Comm:
    SEMS = {"ag1": 5, "ag2": 3, "rs1": 4, "rs2": 3}

    def __init__(self):
        self.inputs, self.out_shapes, self.aliases, self.ops, self.n_sems = [], [], {}, [], 0

    def add(self, kind, arr):
        lead = {"ag1": N_DEV, "ag2": None, "rs1": 4, "rs2": 3}[kind]
        shape = arr.shape if lead is None else (lead,) + arr.shape[(0 if kind == "ag1" else 1):]
        if kind == "ag2":
            self.aliases[len(self.inputs)] = len(self.out_shapes)
        self.ops.append((kind, len(self.inputs), len(self.out_shapes), self.n_sems))
        self.inputs.append(arr)
        self.out_shapes.append(S(shape, arr.dtype))
        self.n_sems += self.SEMS[kind]
        return len(self.out_shapes) - 1

    def _copies(self, ins, outs, send, recv):
        x, y, c, chips = _place()
        me, sibling = (x, y, c), (x, y, 1 - c)
        slot = lambda px, py, pc: 4 * px + 2 * py + pc
        sends, recvs, local = [], [], []

        def rc(src, dst, k, to):
            return lambda: pltpu.make_async_remote_copy(src_ref=src(), dst_ref=dst(), send_sem=send.at[k], recv_sem=recv.at[k],
                                                        device_id=to, device_id_type=MESH)

        for kind, ii, oi, b in self.ops:
            src, dst = ins[ii], outs[oi]
            at = lambda ref, i: (lambda: ref.at[i])
            if kind == "ag1":
                whole, mine = (lambda s=src: s), at(dst, slot(*me))
                sends.append(rc(whole, mine, b, sibling))
                recvs.append(rc(whole, at(dst, slot(x, y, 1 - c)), b, me))
                for j, chip in enumerate(chips):
                    sends.append(rc(whole, mine, b + 1 + j, (*chip, c)))
                    recvs.append(rc(whole, at(dst, slot(*chip, c)), b + 1 + j, me))
                local.append(lambda s=src, m=mine, k=b + 4: pltpu.make_async_copy(s, m(), send.at[k]))
            elif kind == "ag2":
                for j, chip in enumerate(chips):
                    sends.append(rc(at(dst, slot(*chip, c)), at(dst, slot(*chip, c)), b + j, sibling))
                    recvs.append(rc(at(dst, slot(*chip, 1 - c)), at(dst, slot(*chip, 1 - c)), b + j, me))
            elif kind == "rs1":
                for k in range(4):
                    sends.append(rc(at(src, 2 * k + (1 - c)), at(dst, k), b + k, sibling))
                    recvs.append(rc(at(src, 2 * k + c), at(dst, k), b + k, me))
            else:
                for j, (px, py) in enumerate(chips):
                    sends.append(rc(at(src, 2 * px + py), at(dst, j), b + j, (px, py, c)))
                    recvs.append(rc(at(src, 2 * px + py), at(dst, j), b + j, me))
        return sends, recvs, local

    def start(self, ins, outs, send, recv):
        sends, _, local = self._copies(ins, outs, send, recv)
        for make in local + sends:
            make().start()

    def finish(self, ins, outs, send, recv):
        sends, recvs, local = self._copies(ins, outs, send, recv)
        for make in recvs:
            make().wait_recv()
        for make in sends:
            make().wait_send()
        for make in local:
            make().wait()


def _run(body, args, hook, *, grid, in_specs, out_specs, out_shape, name, semantics, scratch_shapes=()):
    comm = hook() if hook is not None else None
    if comm is None:
        return pl.pallas_call(body, grid=grid, in_specs=in_specs, out_specs=out_specs, out_shape=out_shape, name=name,
                              scratch_shapes=list(scratch_shapes), compiler_params=_cp(*semantics))(*args)
    single = not isinstance(out_shape, (list, tuple))
    out_shapes = [out_shape] if single else list(out_shape)
    out_specs_l = [out_specs] if single else list(out_specs)
    n_in, n_out, n_scr, ci, co = len(args), len(out_shapes), len(scratch_shapes), len(comm.inputs), len(comm.out_shapes)

    def wrapped(*refs):
        ins, cins = refs[:n_in], refs[n_in:n_in + ci]
        outs, couts = refs[n_in + ci:n_in + ci + n_out], refs[n_in + ci + n_out:n_in + ci + n_out + co]
        scr = refs[n_in + ci + n_out + co:n_in + ci + n_out + co + n_scr]
        send, recv = refs[-2:]
        first = functools.reduce(lambda a, b: a & b, [pl.program_id(a) == 0 for a in range(len(grid))])
        last = functools.reduce(lambda a, b: a & b, [pl.program_id(a) == g - 1 for a, g in enumerate(grid)])

        @pl.when(first)
        def _():
            comm.start(cins, couts, send, recv)

        body(*ins, *outs, *scr)

        @pl.when(last)
        def _():
            comm.finish(cins, couts, send, recv)

    res = pl.pallas_call(
        wrapped, grid=grid, in_specs=list(in_specs) + [ANY] * ci, out_specs=out_specs_l + [ANY] * co,
        out_shape=out_shapes + comm.out_shapes, name=name,
        scratch_shapes=list(scratch_shapes) + [pltpu.SemaphoreType.DMA((comm.n_sems,)), pltpu.SemaphoreType.DMA((comm.n_sems,))],
        input_output_aliases={n_in + k: n_out + v for k, v in comm.aliases.items()},
        compiler_params=pltpu.CompilerParams(dimension_semantics=("arbitrary",) * len(grid), has_side_effects=True))(*args, *comm.inputs)
    hook(res[n_out:])
    return res[0] if single else list(res[:n_out])


def _dot(a, b):
    return jnp.dot(a, b, preferred_element_type=F32)


def _dot_nt(a, b):
    return lax.dot_general(a, b, (((1,), (1,)), ((), ())), preferred_element_type=F32)


def _dot_tn(a, b):
    return lax.dot_general(a, b, (((0,), (0,)), ((), ())), preferred_element_type=F32)


def _gelu(x):
    return 0.5 * x * (1.0 + lax.erf(x * (2.0 ** -0.5)))


def _gelu_grad(x):
    return 0.5 * (1.0 + lax.erf(x * (2.0 ** -0.5))) + x * jnp.exp(-0.5 * x * x) * (1.0 / math.sqrt(2.0 * math.pi))


def _sigmoid(x):
    return 1.0 / (1.0 + jnp.exp(-x))


def _rstd(x):
    return lax.rsqrt(jnp.mean(x * x, axis=-1, keepdims=True) + EPS)


def _rel_tables():
    q = np.arange(CHUNK)[:, None] + CHUNK
    k = np.arange(2 * CHUNK)[None, :]
    dist = q - k
    n = np.maximum(dist, 0)
    max_exact = REL_BUCKETS // 2
    large = max_exact + (np.log(np.maximum(n, 1).astype(np.float32) / max_exact)
                         / math.log(REL_MAX_DIST / max_exact) * (REL_BUCKETS - max_exact)).astype(np.int32)
    large = np.minimum(large, REL_BUCKETS - 1)
    return np.where(n < max_exact, n, large).astype(np.int32)


def _rmsnorm(x, gain, name):
    t = x.shape[0]
    tm = _tm(t)

    def body(x_ref, g_ref, o_ref):
        xv = x_ref[...]
        o_ref[...] = (xv * _rstd(xv) * g_ref[...]).astype(BF16)

    return pl.pallas_call(
        body, grid=(t // tm,), name=name,
        in_specs=[pl.BlockSpec((tm, D), lambda i: (i, 0)), pl.BlockSpec((1, D), lambda i: (0, 0))],
        out_specs=pl.BlockSpec((tm, D), lambda i: (i, 0)),
        out_shape=S((t, D), BF16), compiler_params=_cp("parallel"))(x, gain)


def _resident(shape):
    zeros = (0,) * len(shape)
    return pl.BlockSpec(shape, lambda *_: zeros, pipeline_mode=pl.Buffered(1))


def _mm_slot(hn, wg, out_dtype, name, hook=None):
    t, k = hn.shape
    ns, _, n = wg.shape
    tm = _tm(t)

    def body(a_ref, w_ref, o_ref):
        a = a_ref[...]
        for s in range(ns):
            o_ref[s] = _dot(a, w_ref[s]).astype(out_dtype)

    return _run(
        body, [hn, wg], hook, grid=(t // tm,), name=name, semantics=("parallel",),
        in_specs=[pl.BlockSpec((tm, k), lambda i: (i, 0)), _resident(wg.shape)],
        out_specs=pl.BlockSpec((ns, tm, n), lambda i: (0, i, 0)), out_shape=S((ns, t, n), out_dtype))


def _mm_t(hn, wt, name, hook=None):
    t, k = hn.shape
    ns, n, _ = wt.shape
    tm = _tm(t)

    def body(a_ref, w_ref, o_ref):
        a = a_ref[...]
        for s in range(ns):
            o_ref[s * n:(s + 1) * n, :] = _dot_nt(w_ref[s], a)

    return _run(
        body, [hn, wt], hook, grid=(t // tm,), name=name, semantics=("parallel",),
        in_specs=[pl.BlockSpec((tm, k), lambda i: (i, 0)), _resident(wt.shape)],
        out_specs=pl.BlockSpec((ns * n, tm), lambda i: (0, i)), out_shape=S((ns * n, t), F32))


def _conv3(a, prev, cw, cb, tm):
    ext = jnp.concatenate([prev, a], axis=0)
    return cw[2:3] * a + cw[1:2] * ext[HALO - 1:HALO - 1 + tm] + cw[0:1] * ext[HALO - 2:HALO - 2 + tm] + cb


def _ffn_fwd(hn, h, wup, wdown, cw, cb, extra, mode, name, hook=None):
    t, k = hn.shape
    n = wup.shape[-1]
    nh = wup.shape[0] // 2
    tm = min(FFN_ROWS, t)
    ni = t // tm

    def body(a_ref, h_ref, wu_ref, wd_ref, cw_ref, cb_ref, e_ref, as_ref, cs_ref, o1_ref, o2_ref, carry):
        i = pl.program_id(0)

        @pl.when(i == 0)
        def _():
            carry[...] = jnp.zeros_like(carry)

        a = a_ref[...]
        acc = h_ref[...]
        nxt = (_dot(a, wu_ref[0]), _dot(a, wu_ref[nh]))
        for j in range(nh):
            ag, av = nxt
            if j + 1 < nh:
                nxt = (_dot(a, wu_ref[j + 1]), _dot(a, wu_ref[nh + j + 1]))
            as_ref[j] = ag.astype(BF16)
            as_ref[nh + j] = av.astype(BF16)
            cg = _conv3(ag, carry[j], cw_ref[j], cb_ref[j], tm)
            cv = _conv3(av, carry[nh + j], cw_ref[nh + j], cb_ref[nh + j], tm)
            carry[j] = ag[tm - HALO:]
            carry[nh + j] = av[tm - HALO:]
            cs_ref[j] = cg.astype(BF16)
            cs_ref[nh + j] = cv.astype(BF16)
            act = (cg * _sigmoid(cg) * cv).astype(BF16)
            acc = acc + _dot(act, wd_ref[j * n:(j + 1) * n, :])
        if mode == "norm":
            o1_ref[...] = acc
            o2_ref[...] = (acc * _rstd(acc) * e_ref[...]).astype(BF16)
        else:
            err = acc - e_ref[...]
            o1_ref[...] = err * (1.0 / D)
            o2_ref[...] = jnp.full(o2_ref.shape, jnp.sum(err * err), F32)

    row = pl.BlockSpec((tm, D), lambda i: (i, 0))
    if mode == "norm":
        e_spec, o2_spec, o2_shape = pl.BlockSpec((1, D), lambda i: (0, 0)), row, S((t, D), BF16)
    else:
        e_spec, o2_spec, o2_shape = row, pl.BlockSpec((None, 8, 128), lambda i: (i, 0, 0)), S((ni, 8, 128), F32)
    aspec = pl.BlockSpec((2 * nh, tm, n), lambda i: (0, i, 0))
    return _run(
        body, [hn, h, wup, wdown, cw, cb, extra], hook, grid=(ni,), name=name, semantics=("arbitrary",),
        in_specs=[pl.BlockSpec((tm, k), lambda i: (i, 0)), row, _resident(wup.shape), _resident(wdown.shape),
                  _resident(cw.shape), _resident(cb.shape), e_spec],
        out_specs=[aspec, aspec, row, o2_spec],
        out_shape=[S((2 * nh, t, n), BF16), S((2 * nh, t, n), BF16), S((t, D), F32), o2_shape],
        scratch_shapes=[pltpu.VMEM((2 * nh, HALO, n), F32)])


def _tril_mask():
    r = lax.broadcasted_iota(jnp.int32, (CHUNK, CHUNK), 0)
    c = lax.broadcasted_iota(jnp.int32, (CHUNK, CHUNK), 1)
    return r >= c


def _sgu_gate_fwd(a_s, vgain, ws, bst, name, hook=None):
    t = a_s.shape[1]
    sw = a_s.shape[2]
    gps = sw // CHUNK

    def body(a_ref, vg_ref, ws_ref, b_ref, o_ref):
        v = _gelu(jnp.concatenate([a_ref[4 + s].astype(F32) for s in range(4)], axis=1))
        vn = (v * _rstd(v) * vg_ref[...]).astype(BF16)
        tri = _tril_mask()
        for g in range(SGU_G):
            w = jnp.where(tri, ws_ref[g], 0.0).astype(BF16)
            sg = _dot(w, vn[:, g * CHUNK:(g + 1) * CHUNK]) + b_ref[:, g:g + 1]
            lo = (g % gps) * CHUNK
            u = _gelu(a_ref[g // gps, :, lo:lo + CHUNK].astype(F32))
            o_ref[g // gps, :, lo:lo + CHUNK] = (u * sg).astype(BF16)

    return _run(
        body, [a_s, vgain, ws, bst], hook, grid=(t // CHUNK,), name=name, semantics=("parallel",),
        in_specs=[pl.BlockSpec((8, CHUNK, sw), lambda n: (0, n, 0)), pl.BlockSpec((1, SGU_W), lambda n: (0, 0)),
                  pl.BlockSpec((SGU_G, CHUNK, CHUNK), lambda n: (0, 0, 0)), pl.BlockSpec((CHUNK, SGU_G), lambda n: (0, 0))],
        out_specs=pl.BlockSpec((4, CHUNK, sw), lambda n: (0, n, 0)), out_shape=S((4, t, sw), BF16))


def _resid_mm(a_s, w, resid, extra, mode, name, hook=None, fm=False):
    nk, t, kc = (1, a_s.shape[1], a_s.shape[0]) if fm else a_s.shape
    tm = _tm(t)
    ni = t // tm

    def body(a_ref, w_ref, r_ref, e_ref, o1_ref, o2_ref):
        h = r_ref[...]
        if fm:
            h = h + _dot_tn(a_ref[...], w_ref[...])
        for j in range(0 if fm else nk):
            h = h + _dot(a_ref[j], w_ref[j * kc:(j + 1) * kc, :])
        if mode == "norm":
            o1_ref[...] = h
            o2_ref[...] = (h * _rstd(h) * e_ref[...]).astype(BF16)
        else:
            err = h - e_ref[...]
            o1_ref[...] = err * (1.0 / D)
            o2_ref[...] = jnp.full(o2_ref.shape, jnp.sum(err * err), F32)

    row = pl.BlockSpec((tm, D), lambda i: (i, 0))
    if mode == "norm":
        e_spec, o2_spec, o2_shape = pl.BlockSpec((1, D), lambda i: (0, 0)), row, S((t, D), BF16)
    else:
        e_spec, o2_spec, o2_shape = row, pl.BlockSpec((None, 8, 128), lambda i: (i, 0, 0)), S((ni, 8, 128), F32)
    return _run(
        body, [a_s, w, resid, extra], hook, grid=(ni,), name=name, semantics=("parallel",),
        in_specs=[pl.BlockSpec((kc, tm), lambda i: (0, i)) if fm else pl.BlockSpec((nk, tm, kc), lambda i: (0, i, 0)),
                  _resident(w.shape), row, e_spec],
        out_specs=[row, o2_spec], out_shape=[S((t, D), F32), o2_shape])


def _relbias_fwd(rel_bias_t, bucket_row, name):
    nb = bucket_row.shape[1]

    def body(rb_ref, bk_ref, o_ref):
        onehot = (lax.broadcasted_iota(jnp.int32, (REL_BUCKETS, nb), 0) == bk_ref[...]).astype(F32)
        o_ref[...] = jnp.dot(rb_ref[...], onehot, precision=lax.Precision.HIGHEST, preferred_element_type=F32)

    return pl.pallas_call(body, out_shape=S((NH, nb), F32), name=name)(rel_bias_t, bucket_row)


def _relbias_bwd(dbias, bucket_row, name):
    nb = bucket_row.shape[1]

    def body(db_ref, bk_ref, o_ref):
        onehot = (lax.broadcasted_iota(jnp.int32, (REL_BUCKETS, nb), 0) == bk_ref[...]).astype(F32)
        o_ref[...] = lax.dot_general(db_ref[...], onehot, (((1,), (1,)), ((), ())),
                                     precision=lax.Precision.HIGHEST, preferred_element_type=F32)

    return pl.pallas_call(body, out_shape=S((NH, REL_BUCKETS), F32), name=name)(dbias, bucket_row)


QKV = D + 2 * NKV * HD
KV0 = D


def _rstd_rows(x):
    return lax.rsqrt(jnp.mean(x * x, axis=0, keepdims=True) + EPS)


def _attn_valid(n):
    kj = lax.broadcasted_iota(jnp.int32, (2 * CHUNK, CHUNK), 0)
    qi = lax.broadcasted_iota(jnp.int32, (2 * CHUNK, CHUNK), 1)
    dist = qi + CHUNK - kj
    return (dist >= 0) & (dist < CHUNK) & ((n > 0) | (kj >= CHUNK))


def _attn_band(cur_ref, prev_ref, row):
    return jnp.concatenate([prev_ref[row - KV0:row - KV0 + HD, :], cur_ref[row:row + HD, :]], axis=1)


def _attn_probs(kn_tok, qn, bias, valid, sink):
    s = _dot(kn_tok, qn) * (HD ** -0.5) + bias
    s = jnp.where(valid, s, -jnp.inf)
    m = jnp.maximum(jnp.max(s, axis=0, keepdims=True), sink)
    p = jnp.exp(s - m)
    psink = jnp.exp(sink - m)
    inv = 1.0 / (jnp.sum(p, axis=0, keepdims=True) + psink)
    return p * inv, psink * inv


def _attn_fwd(qkv_t, qg, kg, sinks, bias, name, hook=None):
    t = qkv_t.shape[1]

    def body(cur_ref, prev_ref, qg_ref, kg_ref, sink_ref, bias_ref, o_ref):
        n = pl.program_id(0)
        valid = _attn_valid(n)
        for h in range(NKV):
            k = _attn_band(cur_ref, prev_ref, KV0 + HD * h)
            v = _attn_band(cur_ref, prev_ref, KV0 + HD * (NKV + h))
            kn_tok = (k * _rstd_rows(k) * kg_ref[...]).astype(BF16).T
            vb = v.astype(BF16)
            heads = range(KVG * h, KVG * (h + 1))
            qs = [cur_ref[HD * hq:HD * (hq + 1), :] for hq in heads]
            qns = [(q * _rstd_rows(q) * qg_ref[...]).astype(BF16) for q in qs]
            ps = [_attn_probs(kn_tok, qn, bias_ref[hq], valid, sink_ref[hq])[0] for qn, hq in zip(qns, heads)]
            for p, hq in zip(ps, heads):
                o_ref[HD * hq:HD * (hq + 1), :] = _dot(vb, p.astype(BF16)).astype(BF16)

    col = pl.BlockSpec((HD, 1), lambda n: (0, 0))
    return _run(
        body, [qkv_t, qkv_t, qg, kg, sinks, bias], hook, grid=(t // CHUNK,), name=name, semantics=("parallel",),
        in_specs=[pl.BlockSpec((QKV, CHUNK), lambda n: (0, n)),
                  pl.BlockSpec((QKV - KV0, CHUNK), lambda n: (KV0 // (QKV - KV0), jnp.maximum(n - 1, 0))),
                  col, col, pl.BlockSpec(memory_space=pltpu.SMEM), pl.BlockSpec((NH, 2 * CHUNK, CHUNK), lambda n: (0, 0, 0))],
        out_specs=pl.BlockSpec((D, CHUNK), lambda n: (0, n)), out_shape=S((D, t), BF16))


def _dx_rows(dh, w, kc, out_dtype, name, hook=None):
    t = dh.shape[0]
    nk = w.shape[0] // kc
    tm = _tm(t)

    def body(d_ref, w_ref, o_ref):
        dhb = d_ref[...].astype(BF16)
        for j in range(nk):
            o_ref[j] = _dot_nt(dhb, w_ref[j * kc:(j + 1) * kc, :]).astype(out_dtype)

    return _run(
        body, [dh, w], hook, grid=(t // tm,), name=name, semantics=("parallel",),
        in_specs=[pl.BlockSpec((tm, D), lambda i: (i, 0)), _resident(w.shape)],
        out_specs=pl.BlockSpec((nk, tm, kc), lambda i: (0, i, 0)), out_shape=S((nk, t, kc), out_dtype))


def _dx_rows_t(dh, w, name, hook=None):
    t = dh.shape[0]
    k = w.shape[0]
    tm = _tm(t)

    def body(d_ref, w_ref, o_ref):
        o_ref[...] = _dot_nt(w_ref[...], d_ref[...].astype(BF16)).astype(BF16)

    return _run(
        body, [dh, w], hook, grid=(t // tm,), name=name, semantics=("parallel",),
        in_specs=[pl.BlockSpec((tm, D), lambda i: (i, 0)), _resident(w.shape)],
        out_specs=pl.BlockSpec((k, tm), lambda i: (0, i)), out_shape=S((k, t), BF16))


def _ffn_bwd1(dh, c, wdown, name, hook=None):
    ns, t, n = c.shape
    nh = ns // 2
    tm = min(FFN_ROWS, t)
    ni = t // tm

    def body(d_ref, c_ref, wd_ref, dc_ref, dw_hbm, dwb_hbm, acc, stage):
        i = pl.program_id(0)

        @pl.when(i == 0)
        def _():
            acc[...] = jnp.zeros_like(acc)

        dhb = d_ref[...].astype(BF16)
        for j in range(nh):
            dact = _dot_nt(dhb, wd_ref[j * n:(j + 1) * n, :])
            cg = c_ref[j].astype(F32)
            cv = c_ref[nh + j].astype(F32)
            sg = _sigmoid(cg)
            gs = cg * sg
            acc[j * n:(j + 1) * n, :] += _dot_tn((gs * cv).astype(BF16), dhb)
            dc_ref[j] = (dact * cv * (sg + gs * (1.0 - sg))).astype(BF16)
            dc_ref[nh + j] = (dact * gs).astype(BF16)

        @pl.when(i == ni - 1)
        def _():
            pltpu.sync_copy(acc, dw_hbm)
            for j in range(nh):
                stage[...] = acc[j * n:(j + 1) * n, :].astype(BF16)
                pltpu.sync_copy(stage, dwb_hbm.at[pl.ds(j * n, n), :])

    slab = pl.BlockSpec((ns, tm, n), lambda i: (0, i, 0))
    return _run(
        body, [dh, c, wdown], hook, grid=(ni,), name=name, semantics=("arbitrary",),
        in_specs=[pl.BlockSpec((tm, D), lambda i: (i, 0)), slab, _resident(wdown.shape)],
        out_specs=[slab, ANY, ANY], out_shape=[S((ns, t, n), BF16), S(wdown.shape, F32), S(wdown.shape, BF16)],
        scratch_shapes=[pltpu.VMEM(wdown.shape, F32), pltpu.VMEM((n, D), BF16)])


def _ffn_bwd2(dc, a, wup, cw, h, gain, dh_in, name, hook=None):
    ns, t, n = dc.shape
    tm = min(FFN_ROWS, t)
    ni = t // tm

    def body(dc_ref, a_ref, wu_ref, cw_ref, h_ref, g_ref, di_ref, da_ref, o_ref, dg_ref, dcw_ref, dcb_ref, carry, keep):
        i = pl.program_id(0)

        @pl.when(i == 0)
        def _():
            carry[...] = jnp.zeros_like(carry)
            dg_ref[...] = jnp.zeros_like(dg_ref)
            dcw_ref[...] = jnp.zeros_like(dcw_ref)
            dcb_ref[...] = jnp.zeros_like(dcb_ref)

        rsum = lambda v: jnp.sum(v, axis=0, keepdims=True)
        acc = jnp.zeros((tm, D), F32)
        for s in range(ns):
            x = dc_ref[s].astype(F32)
            ext = jnp.concatenate([x, carry[s]], axis=0)
            keep[0] = ext[1:1 + tm]
            keep[1] = ext[2:2 + tm]
            x1, x2 = keep[0], keep[1]
            cwv = cw_ref[s]
            da = (cwv[2:3] * x + cwv[1:2] * x1 + cwv[0:1] * x2).astype(BF16)
            carry[s] = x[:HALO]
            da_ref[s] = da
            acc = acc + _dot_nt(da, wu_ref[s])
            av = a_ref[s].astype(F32)
            dcw_ref[s] += jnp.concatenate([rsum(x2 * av), rsum(x1 * av), rsum(x * av)], axis=0)
            dcb_ref[s] += rsum(x)
        hv = h_ref[...]
        r = _rstd(hv)
        gg = acc * g_ref[...]
        o_ref[...] = di_ref[...] + r * gg - hv * (r * r * r * jnp.mean(gg * hv, axis=-1, keepdims=True))
        dg_ref[...] += jnp.sum(acc * hv * r, axis=0, keepdims=True)

    slab = pl.BlockSpec((ns, tm, n), lambda i: (0, ni - 1 - i, 0))
    row = pl.BlockSpec((tm, D), lambda i: (ni - 1 - i, 0))
    vec = pl.BlockSpec((1, D), lambda i: (0, 0))
    whole = lambda shape: pl.BlockSpec(shape, lambda i: (0,) * len(shape))
    return _run(
        body, [dc, a, wup, cw, h, gain, dh_in], hook, grid=(ni,), name=name, semantics=("arbitrary",),
        in_specs=[slab, slab, _resident(wup.shape), _resident(cw.shape), row, vec, row],
        out_specs=[slab, row, vec, whole((ns, 3, n)), whole((ns, 1, n))],
        out_shape=[S((ns, t, n), BF16), S((t, D), F32), S((1, D), F32), S((ns, 3, n), F32), S((ns, 1, n), F32)],
        scratch_shapes=[pltpu.VMEM((ns, HALO, n), F32), pltpu.VMEM((2, tm, n), F32)])


def _dw_slot(hn, dy_s, name, hook=None):
    t, k = hn.shape
    ns, _, n = dy_s.shape
    tm = _tm(t)

    def body(a_ref, b_ref, o_ref, ob_ref, at_ref):
        @pl.when(pl.program_id(0) == 0)
        def _():
            for i in range(t // tm):
                at_ref[:, i * tm:(i + 1) * tm] = a_ref[i * tm:(i + 1) * tm, :].T

        acc = _dot(at_ref[...], b_ref[...])
        o_ref[...] = acc
        ob_ref[...] = acc.astype(BF16)

    ospec = pl.BlockSpec((None, k, n), lambda j: (j, 0, 0))
    return _run(
        body, [hn, dy_s], hook, grid=(ns,), name=name, semantics=("arbitrary",),
        in_specs=[_resident(hn.shape), pl.BlockSpec((None, t, n), lambda j: (j, 0, 0))],
        out_specs=[ospec, ospec], out_shape=[S((ns, k, n), F32), S((ns, k, n), BF16)],
        scratch_shapes=[pltpu.VMEM((k, t), BF16)])


def _dw_rows(a_s, dh, name, hook=None, fm=False):
    nk, t, kc = (1, a_s.shape[1], a_s.shape[0]) if fm else a_s.shape
    tm = _tm(t)
    ni = t // tm

    def body(a_ref, d_ref, o_ref, ob_ref):
        i = pl.program_id(0)
        dhb = d_ref[...].astype(BF16)

        @pl.when(i == 0)
        def _():
            o_ref[...] = jnp.zeros_like(o_ref)

        if fm:
            o_ref[...] += _dot(a_ref[...], dhb)
        for j in range(0 if fm else nk):
            o_ref[j * kc:(j + 1) * kc, :] += _dot_tn(a_ref[j], dhb)

        @pl.when(i == ni - 1)
        def _():
            ob_ref[...] = o_ref[...].astype(BF16)

    ospec = pl.BlockSpec((nk * kc, D), lambda i: (0, 0))
    return _run(
        body, [a_s, dh], hook, grid=(ni,), name=name, semantics=("arbitrary",),
        in_specs=[pl.BlockSpec((kc, tm), lambda i: (0, i)) if fm else pl.BlockSpec((nk, tm, kc), lambda i: (0, i, 0)),
                  pl.BlockSpec((tm, D), lambda i: (i, 0))],
        out_specs=[ospec, ospec], out_shape=[S((nk * kc, D), F32), S((nk * kc, D), BF16)])


def _dx_slot_normbwd(dy_s, wg, h, gain, dh_in, name, hook=None, fm=False):
    ns, t, n = (1, dy_s.shape[1], dy_s.shape[0]) if fm else dy_s.shape
    tm = _tm(t)

    def body(dy_ref, w_ref, h_ref, g_ref, di_ref, o_ref, dg_ref):
        i = pl.program_id(0)

        @pl.when(i == 0)
        def _():
            dg_ref[...] = jnp.zeros_like(dg_ref)

        g = _dot_tn(dy_ref[...], w_ref[...]) if fm else _dot_nt(dy_ref[0], w_ref[0])
        for s in range(1, ns):
            g = g + _dot_nt(dy_ref[s], w_ref[s])
        hv = h_ref[...]
        r = _rstd(hv)
        gg = g * g_ref[...]
        o_ref[...] = di_ref[...] + r * gg - hv * (r * r * r * jnp.mean(gg * hv, axis=-1, keepdims=True))
        dg_ref[...] += jnp.sum(g * hv * r, axis=0, keepdims=True)

    row = pl.BlockSpec((tm, D), lambda i: (i, 0))
    vec = pl.BlockSpec((1, D), lambda i: (0, 0))
    return _run(
        body, [dy_s, wg, h, gain, dh_in], hook, grid=(t // tm,), name=name, semantics=("arbitrary",),
        in_specs=[pl.BlockSpec((n, tm), lambda i: (0, i)) if fm else pl.BlockSpec((ns, tm, n), lambda i: (0, i, 0)),
                  _resident(wg.shape), row, vec, row],
        out_specs=[row, vec], out_shape=[S((t, D), F32), S((1, D), F32)])


def _sgu_gate_bwd(a_s, dg_s, vgain, ws, bst, name, hook=None):
    t = a_s.shape[1]
    sw = a_s.shape[2]
    gps = sw // CHUNK

    def body(a_ref, dg_ref, vg_ref, ws_ref, b_ref, da_ref, dws_ref, dbt_ref, dvg_ref, dvn_ref):
        n = pl.program_id(0)

        @pl.when(n == 0)
        def _():
            dws_ref[...] = jnp.zeros_like(dws_ref)
            dbt_ref[...] = jnp.zeros_like(dbt_ref)
            dvg_ref[...] = jnp.zeros_like(dvg_ref)

        vpre = jnp.concatenate([a_ref[4 + s].astype(F32) for s in range(4)], axis=1)
        v = _gelu(vpre)
        r = _rstd(v)
        vhat = v * r
        vn = (vhat * vg_ref[...]).astype(BF16)
        tri = _tril_mask()
        lane = lax.broadcasted_iota(jnp.int32, (CHUNK, CHUNK), 1)
        dbt = jnp.zeros((CHUNK, CHUNK), F32)
        for g in range(SGU_G):
            w = jnp.where(tri, ws_ref[g], 0.0).astype(BF16)
            vng = vn[:, g * CHUNK:(g + 1) * CHUNK]
            sg = _dot(w, vng) + b_ref[:, g:g + 1]
            lo = (g % gps) * CHUNK
            upre = a_ref[g // gps, :, lo:lo + CHUNK].astype(F32)
            dgate = dg_ref[g // gps, :, lo:lo + CHUNK].astype(F32)
            da_ref[g // gps, :, lo:lo + CHUNK] = (dgate * sg * _gelu_grad(upre)).astype(BF16)
            ds = dgate * _gelu(upre)
            dsb = ds.astype(BF16)
            dvn_ref[:, g * CHUNK:(g + 1) * CHUNK] = _dot_tn(w, dsb)
            dws_ref[g] += jnp.where(tri, _dot_nt(dsb, vng), 0.0)
            dbt = dbt + jnp.where(lane == g, jnp.sum(ds, axis=-1, keepdims=True), 0.0)
        dbt_ref[...] += dbt
        dvn = dvn_ref[...]
        dvg_ref[...] += jnp.sum(dvn * vhat, axis=0, keepdims=True)
        gg = dvn * vg_ref[...]
        dv = r * gg - v * (r * r * r * jnp.mean(gg * v, axis=-1, keepdims=True))
        dav = (dv * _gelu_grad(vpre)).astype(BF16)
        for s in range(4):
            da_ref[4 + s] = dav[:, s * sw:(s + 1) * sw]

    return _run(
        body, [a_s, dg_s, vgain, ws, bst], hook, grid=(t // CHUNK,), name=name, semantics=("arbitrary",),
        in_specs=[pl.BlockSpec((8, CHUNK, sw), lambda n: (0, n, 0)), pl.BlockSpec((4, CHUNK, sw), lambda n: (0, n, 0)),
                  pl.BlockSpec((1, SGU_W), lambda n: (0, 0)), pl.BlockSpec((SGU_G, CHUNK, CHUNK), lambda n: (0, 0, 0)),
                  pl.BlockSpec((CHUNK, SGU_G), lambda n: (0, 0))],
        out_specs=[pl.BlockSpec((8, CHUNK, sw), lambda n: (0, n, 0)), pl.BlockSpec((SGU_G, CHUNK, CHUNK), lambda n: (0, 0, 0)),
                   pl.BlockSpec((CHUNK, CHUNK), lambda n: (0, 0)), pl.BlockSpec((1, SGU_W), lambda n: (0, 0))],
        out_shape=[S((8, t, sw), BF16), S((SGU_G, CHUNK, CHUNK), F32), S((CHUNK, CHUNK), F32), S((1, SGU_W), F32)],
        scratch_shapes=[pltpu.VMEM((CHUNK, SGU_W), F32)])


def _attn_bwd(qkv_t, do_t, qg, kg, sinks, bias, name, hook=None):
    t = qkv_t.shape[1]
    nb = t // CHUNK

    def body(cur_ref, prev_ref, do_ref, qg_ref, kg_ref, sink_ref, bias_ref,
             o_ref, dqg_out, dkg_out, dsk_out, dbias_ref, carry, dqg_ref, dkg_ref, dsk_ref):
        n = pl.program_id(0)

        @pl.when(n == 0)
        def _():
            carry[...] = jnp.zeros_like(carry)
            dqg_ref[...] = jnp.zeros_like(dqg_ref)
            dkg_ref[...] = jnp.zeros_like(dkg_ref)
            dsk_ref[...] = jnp.zeros_like(dsk_ref)
            dbias_ref[...] = jnp.zeros_like(dbias_ref)

        @pl.when(n < nb)
        def _():
            valid = _attn_valid(n)
            o_ref[0:KV0, :] = carry[0:KV0, :].astype(BF16)
            for h in range(NKV):
                krow = KV0 + HD * h
                vrow = KV0 + HD * (NKV + h)
                k = _attn_band(cur_ref, prev_ref, krow)
                v = _attn_band(cur_ref, prev_ref, vrow)
                rk = _rstd_rows(k)
                khat = k * rk
                kn = (khat * kg_ref[...]).astype(BF16)
                kn_tok = kn.T
                vb = v.astype(BF16)
                v_tok = vb.T
                heads = range(KVG * h, KVG * (h + 1))
                qs = [cur_ref[HD * hq:HD * (hq + 1), :] for hq in heads]
                rqs = [_rstd_rows(q) for q in qs]
                qhats = [q * rq for q, rq in zip(qs, rqs)]
                qns = [(qhat * qg_ref[...]).astype(BF16) for qhat in qhats]
                probs = [_attn_probs(kn_tok, qn, bias_ref[hq], valid, sink_ref[hq]) for qn, hq in zip(qns, heads)]
                dohs = [do_ref[HD * hq:HD * (hq + 1), :] for hq in heads]
                dps = [_dot(v_tok, doh) for doh in dohs]
                dsums = [jnp.sum(p * dp, axis=0, keepdims=True) for (p, _), dp in zip(probs, dps)]
                dss = [p * (dp - dsum) for (p, _), dp, dsum in zip(probs, dps, dsums)]
                for hq, (_, psink), dsum, ds in zip(heads, probs, dsums, dss):
                    dsk_ref[hq:hq + 1, :] -= psink * dsum
                    dbias_ref[hq] += ds
                dv = sum(_dot_nt(doh, p.astype(BF16)) for doh, (p, _) in zip(dohs, probs))
                dscs = [(ds * (HD ** -0.5)).astype(BF16) for ds in dss]
                dqns = [_dot(kn, dsc) for dsc in dscs]
                dkn = sum(_dot_nt(qn, dsc) for qn, dsc in zip(qns, dscs))
                dqg_ref[...] += sum(dqn * qhat for dqn, qhat in zip(dqns, qhats))
                for hq, q, rq, dqn in zip(heads, qs, rqs, dqns):
                    gq = dqn * qg_ref[...]
                    carry[HD * hq:HD * (hq + 1), :] = rq * gq - q * (rq * rq * rq * jnp.mean(gq * q, axis=0, keepdims=True))
                dkg_ref[...] += dkn * khat
                gk = dkn * kg_ref[...]
                dk = rk * gk - k * (rk * rk * rk * jnp.mean(gk * k, axis=0, keepdims=True))
                o_ref[krow:krow + HD, :] = (carry[krow:krow + HD, :] + dk[:, :CHUNK]).astype(BF16)
                o_ref[vrow:vrow + HD, :] = (carry[vrow:vrow + HD, :] + dv[:, :CHUNK]).astype(BF16)
                carry[krow:krow + HD, :] = dk[:, CHUNK:]
                carry[vrow:vrow + HD, :] = dv[:, CHUNK:]

        @pl.when(n == nb)
        def _():
            o_ref[...] = carry[...].astype(BF16)
            dqg_out[...] = jnp.sum(dqg_ref[...], axis=1, keepdims=True)
            dkg_out[...] = jnp.sum(dkg_ref[...], axis=1, keepdims=True)
            dsk_out[...] = jnp.sum(dsk_ref[...], axis=1, keepdims=True)

    cur = lambda n: (0, jnp.minimum(n, nb - 1))
    col = pl.BlockSpec((HD, 1), lambda n: (0, 0))
    whole = lambda shape: pl.BlockSpec(shape, lambda n: (0,) * len(shape))
    return _run(
        body, [qkv_t, qkv_t, do_t, qg, kg, sinks, bias], hook, grid=(nb + 1,), name=name, semantics=("arbitrary",),
        in_specs=[pl.BlockSpec((QKV, CHUNK), cur),
                  pl.BlockSpec((QKV - KV0, CHUNK), lambda n: (KV0 // (QKV - KV0), jnp.clip(n - 1, 0, nb - 1))),
                  pl.BlockSpec((D, CHUNK), cur), col, col, pl.BlockSpec(memory_space=pltpu.SMEM), whole((NH, 2 * CHUNK, CHUNK))],
        out_specs=[pl.BlockSpec((QKV, CHUNK), lambda n: (0, jnp.maximum(n - 1, 0))), whole((HD, 1)), whole((HD, 1)),
                   whole((NH, 1)), whole((NH, 2 * CHUNK, CHUNK))],
        out_shape=[S((QKV, t), BF16), S((HD, 1), F32), S((HD, 1), F32), S((NH, 1), F32), S((NH, 2 * CHUNK, CHUNK), F32)],
        scratch_shapes=[pltpu.VMEM((QKV, CHUNK), F32), pltpu.VMEM((HD, CHUNK), F32), pltpu.VMEM((HD, 2 * CHUNK), F32),
                        pltpu.VMEM((NH, CHUNK), F32)])


class _Plain:
    def __init__(self, wg):
        self.full, self.grads = wg, {}

    def w(self, n):
        return self.full[n]

    def hook(self, host):
        return None

    def grad(self, n, pair):
        self.grads[n] = pair

    def small(self, g_rep):
        pass


def _local_step(x, target, rep, sch):
    bucket_row = jnp.asarray(_rel_tables().T.reshape(1, -1))
    bias = _relbias_fwd(rep["rel_bias"].T, bucket_row, "relbias_fwd").reshape(NH, 2 * CHUNK, CHUNK)
    bst = rep["sgu_b_s"][0].T
    ws = rep["sgu_w_s"][0]
    vgain = rep["sgu_v_gain"]
    qg, kg, sinks = rep["attn_q_gain"].reshape(HD, 1), rep["attn_k_gain"].reshape(HD, 1), rep["attn_sinks"][0]
    w_down = lambda l: sch.w("ffn_w_down%d" % l).reshape(D_FF, D)
    w_up = lambda l: sch.w("ffn_w_up%d" % l)
    cw = [sch.w("ffn_conv_w")[:, 3 * l:3 * l + 3] for l in range(2)]
    cb = [rep["ffn_conv_b"][l].reshape(8, 1, -1) for l in range(2)]
    mixg = [rep["mix_norm"][l:l + 1] for l in range(2)]
    ffng = [rep["ffn_norm"][l:l + 1] for l in range(2)]
    rows = lambda pair: tuple(g.reshape(N_DEV, -1, D) for g in pair)
    hk = sch.hook

    hn0 = _rmsnorm(x, mixg[0], "norm0")
    a0 = _mm_slot(hn0, sch.w("sgu_w_in"), BF16, "sgu_in", hk("sgu_in"))
    gated = _sgu_gate_fwd(a0, vgain, ws, bst, "sgu_gate", hk("sgu_gate"))
    h1, hn1 = _resid_mm(gated, sch.w("sgu_w_out").reshape(SGU_W, D), x, ffng[0], "norm", "sgu_out", hk("sgu_out"))
    a_ff0, c_ff0, h2, hn2 = _ffn_fwd(hn1, h1, w_up(0), w_down(0), cw[0], cb[0], mixg[1], "norm", "ffn0_fwd", hk("ffn0_fwd"))
    qkv = _mm_t(hn2, sch.w("attn_w_qkv"), "qkv", hk("qkv"))
    o = _attn_fwd(qkv, qg, kg, sinks, bias, "attn", hk("attn"))
    h3, hn3 = _resid_mm(o, sch.w("attn_w_o").reshape(D, D), h2, ffng[1], "norm", "attn_out", hk("attn_out"), fm=True)
    a_ff1, c_ff1, dy, sq = _ffn_fwd(hn3, h3, w_up(1), w_down(1), cw[1], cb[1], target, "loss", "ffn1_fwd_loss", hk("ffn1_fwd_loss"))
    loss = (0.5 / D) * jnp.sum(sq[:, 0, 0])

    def ffn_bwd(dh, h_in, hn, a, c, l, tag):
        dc, g_down, g_down_b = _ffn_bwd1(dh, c, w_down(l), tag + "_bwd1", hk(tag + "_bwd1"))
        sch.grad("ffn_w_down%d" % l, rows((g_down, g_down_b)))
        da, dh_new, dgain, g_cw, g_cb = _ffn_bwd2(dc, a, w_up(l), cw[l], h_in, ffng[l], dh, tag + "_bwd2", hk(tag + "_bwd2"))
        sch.grad("ffn_w_up%d" % l, _dw_slot(hn, da, tag + "_dw_up", hk(tag + "_dw_up")))
        return dh_new, dgain, g_cw, g_cb.reshape(-1)

    dh, d_ffng1, g_cw1, g_cb1 = ffn_bwd(dy, h3, hn3, a_ff1, c_ff1, 1, "ffn1")
    do = _dx_rows_t(dh, sch.w("attn_w_o").reshape(D, D), "attn_do", hk("attn_do"))
    sch.grad("attn_w_o", rows(_dw_rows(o, dh, "dw_o", hk("dw_o"), fm=True)))
    dqkv, d_qg, d_kg, d_sk, d_bias = _attn_bwd(qkv, do, qg, kg, sinks, bias, "attn_bwd", hk("attn_bwd"))
    sch.grad("attn_w_qkv", tuple(g.reshape(N_DEV, -1, D) for g in _dw_rows(dqkv, hn2, "dw_qkv", hk("dw_qkv"), fm=True)))
    dh, d_mixg1 = _dx_slot_normbwd(dqkv, sch.w("attn_w_qkv").reshape(QKV, D), h2, mixg[1], dh, "dx_qkv", hk("dx_qkv"), fm=True)
    d_relb = _relbias_bwd(d_bias.reshape(NH, -1), bucket_row, "relbias_bwd").T
    dh, d_ffng0, g_cw0, g_cb0 = ffn_bwd(dh, h1, hn1, a_ff0, c_ff0, 0, "ffn0")
    g_cw = jnp.concatenate([g_cw0, g_cw1], axis=1)
    sch.grad("ffn_conv_w", (g_cw, g_cw.astype(BF16)))
    dgated = _dx_rows(dh, sch.w("sgu_w_out").reshape(SGU_W, D), SGU_W // 4, BF16, "sgu_dgated", hk("sgu_dgated"))
    sch.grad("sgu_w_out", rows(_dw_rows(gated, dh, "dw_sgu_out", hk("dw_sgu_out"))))
    da0, d_ws, d_bst, d_vgain = _sgu_gate_bwd(a0, dgated, vgain, ws, bst, "sgu_gate_bwd", hk("sgu_gate_bwd"))
    grad_x, d_mixg0 = _dx_slot_normbwd(da0, sch.w("sgu_w_in"), x, mixg[0], dh, "dx_sgu_in")
    g_rep = {
        "mix_norm": jnp.concatenate([d_mixg0, d_mixg1], axis=0),
        "ffn_norm": jnp.concatenate([d_ffng0, d_ffng1], axis=0),
        "sgu_v_gain": d_vgain,
        "sgu_w_s": d_ws[None],
        "sgu_b_s": d_bst[:, :SGU_G].T[None],
        "attn_q_gain": d_qg.reshape(1, HD),
        "attn_k_gain": d_kg.reshape(1, HD),
        "attn_sinks": d_sk.reshape(1, NH),
        "rel_bias": d_relb,
        "ffn_conv_b": jnp.stack([g_cb0, g_cb1], axis=0),
    }
    sch.small(g_rep)
    sch.grad("sgu_w_in", _dw_slot(hn0, da0, "dw_sgu_in", hk("dw_sgu_in")))
    return loss, grad_x, g_rep


def _allgather(xs, name):
    nt = len(xs)

    def body(*refs):
        x_refs, o_refs = refs[:nt], refs[nt:2 * nt]
        send_sems, recv_sems, local_sems = refs[2 * nt:]
        x, y, c, chips = _place()
        me, sibling = (x, y, c), (x, y, 1 - c)

        def copy(t, k, block, to, src=None):
            px, py, pc = block
            dst = o_refs[t].at[4 * px + 2 * py + pc]
            return pltpu.make_async_remote_copy(
                src_ref=dst if src is None else src, dst_ref=dst, send_sem=send_sems.at[t, k], recv_sem=recv_sems.at[t, k],
                device_id=to, device_id_type=MESH)

        mine = [pltpu.make_async_copy(x_refs[t], o_refs[t].at[4 * x + 2 * y + c], local_sems.at[t]) for t in range(nt)]
        for cp in mine:
            cp.start()
        first = []
        for t in range(nt):
            first.append(copy(t, 0, me, sibling, src=x_refs[t]))
            first += [copy(t, 1 + j, me, (*chip, c), src=x_refs[t]) for j, chip in enumerate(chips)]
        for cp in first:
            cp.start()
        passed = []
        for j, chip in enumerate(chips):
            for t in range(nt):
                copy(t, 1 + j, (*chip, c), me).wait_recv()
                fwd = copy(t, 4 + j, (*chip, c), sibling)
                fwd.start()
                passed.append(fwd)
        for t in range(nt):
            copy(t, 0, sibling, me).wait_recv()
            for j, chip in enumerate(chips):
                copy(t, 4 + j, (*chip, 1 - c), me).wait_recv()
        for cp in first + passed:
            cp.wait_send()
        for cp in mine:
            cp.wait()

    return pl.pallas_call(
        body, name=name, in_specs=[ANY] * nt, out_specs=[ANY] * nt,
        out_shape=[S((N_DEV,) + a.shape, a.dtype) for a in xs],
        scratch_shapes=[pltpu.SemaphoreType.DMA((nt, 7)), pltpu.SemaphoreType.DMA((nt, 7)), pltpu.SemaphoreType.DMA((nt,))],
        compiler_params=pltpu.CompilerParams(has_side_effects=True))(*xs)


def _exchange(hook, name):
    comm = hook()
    ci, co = len(comm.inputs), len(comm.out_shapes)

    def body(*refs):
        cins, couts = refs[:ci], refs[ci:ci + co]
        send, recv = refs[-2:]
        comm.start(cins, couts, send, recv)
        comm.finish(cins, couts, send, recv)

    res = pl.pallas_call(
        body, name=name, in_specs=[ANY] * ci, out_specs=[ANY] * co, out_shape=comm.out_shapes,
        scratch_shapes=[pltpu.SemaphoreType.DMA((comm.n_sems,)), pltpu.SemaphoreType.DMA((comm.n_sems,))],
        input_output_aliases=dict(comm.aliases),
        compiler_params=pltpu.CompilerParams(has_side_effects=True))(*comm.inputs)
    hook(res)


def _row_tile(r):
    tr = r if r <= ROW_TILE or r % ROW_TILE else ROW_TILE
    assert r % tr == 0
    return tr


def _rs_partial(g32, sib, place, name):
    _, r, cdim = g32.shape
    tr = _row_tile(r)

    def body(place_ref, g_ref, s_ref, p_ref, own_ref):
        k = pl.program_id(1)
        tot = g_ref[...] + s_ref[...].astype(F32)
        p_ref[...] = tot.astype(BF16)

        @pl.when(k == place_ref[1])
        def _():
            own_ref[...] = tot

    grid_spec = pltpu.PrefetchScalarGridSpec(
        num_scalar_prefetch=1, grid=(r // tr, 4),
        in_specs=[pl.BlockSpec((None, None, tr, cdim), lambda i, k, pr: (k, pr[0], i, 0)),
                  pl.BlockSpec((None, tr, cdim), lambda i, k, pr: (k, i, 0))],
        out_specs=[pl.BlockSpec((None, tr, cdim), lambda i, k, pr: (k, i, 0)), pl.BlockSpec((tr, cdim), lambda i, k, pr: (i, 0))])
    return pl.pallas_call(
        body, grid_spec=grid_spec, name=name,
        out_shape=[S((4, r, cdim), BF16), S((r, cdim), F32)],
        compiler_params=_cp("parallel", "arbitrary"))(place, g32.reshape(4, 2, r, cdim), sib)


def _adamw_math(w, g, m, v):
    m = ADAM_B1 * m + (1.0 - ADAM_B1) * g
    v = ADAM_B2 * v + (1.0 - ADAM_B2) * (g * g)
    m_hat = m / (1.0 - ADAM_B1 ** ADAM_STEP)
    v_hat = v / (1.0 - ADAM_B2 ** ADAM_STEP)
    delta = -ADAM_LR * (m_hat / (jnp.sqrt(v_hat) + ADAM_EPS) + ADAM_WD * w)
    return delta, m, v


def _adamw_shard(owns, recvs, w, m, v, name, flipped=False):
    nl = w.shape[0]
    r, cdim = owns[0].shape
    tr = _row_tile(r)
    nr = r // tr

    def body(*refs):
        own_refs, recv_refs = refs[:nl], refs[nl:2 * nl]
        w_ref, m_ref, v_ref, g_out, d_out, m_out, v_out = refs[2 * nl:]
        layer = pl.program_id(0)
        g = None
        for l in range(nl):
            gl = own_refs[l][...] + recv_refs[l][0].astype(F32) + recv_refs[l][1].astype(F32) + recv_refs[l][2].astype(F32)
            g = gl if g is None else jnp.where(layer == l, gl, g)
        if flipped:
            g = g.T
        g_out[...] = g
        d_out[...], m_out[...], v_out[...] = _adamw_math(w_ref[...], g, m_ref[...], v_ref[...])

    park = lambda l: (lambda layer, i: (jnp.where(layer == l, i, jnp.where(layer < l, 0, nr - 1)), 0))
    park3 = lambda l: (lambda layer, i: (0, jnp.where(layer == l, i, jnp.where(layer < l, 0, nr - 1)), 0))
    if flipped:
        row = pl.BlockSpec((None, cdim, tr), lambda layer, i: (layer, 0, i))
    else:
        row = pl.BlockSpec((None, tr, cdim), lambda layer, i: (layer, i, 0))
    return pl.pallas_call(
        body, grid=(nl, nr), name=name,
        in_specs=[pl.BlockSpec((tr, cdim), park(l)) for l in range(nl)] + [pl.BlockSpec((3, tr, cdim), park3(l)) for l in range(nl)]
        + [row, row, row],
        out_specs=[row] * 4, out_shape=[S(w.shape, F32)] * 4,
        compiler_params=_cp("arbitrary", "arbitrary"))(*owns, *recvs, w, m, v)


def _adamw_small(galls, ws, ms, vs, name):
    n = len(galls)

    def body(*refs):
        g_refs, w_refs, m_refs, v_refs, outs = refs[:n], refs[n:2 * n], refs[2 * n:3 * n], refs[3 * n:4 * n], refs[4 * n:]
        for i in range(n):
            g = g_refs[i][0]
            for s in range(1, N_DEV):
                g = g + g_refs[i][s]
            outs[i][...] = g
            outs[n + i][...], outs[2 * n + i][...], outs[3 * n + i][...] = _adamw_math(w_refs[i][...], g, m_refs[i][...], v_refs[i][...])

    res = pl.pallas_call(body, out_shape=[S(a.shape, F32) for a in ws] * 4, name=name)(*galls, *ws, *ms, *vs)
    return [res[k * n:(k + 1) * n] for k in range(4)]


REPLICATED = ["mix_norm", "ffn_norm", "sgu_v_gain", "sgu_w_s", "sgu_b_s", "attn_q_gain", "attn_k_gain", "attn_sinks", "rel_bias",
              "ffn_conv_b"]
WEIGHTS = ["mix_norm", "ffn_norm", "sgu_w_in", "sgu_v_gain", "sgu_w_s", "sgu_b_s", "sgu_w_out", "attn_w_qkv", "attn_q_gain",
           "attn_k_gain", "attn_sinks", "attn_w_o", "rel_bias", "ffn_w_up", "ffn_conv_w", "ffn_conv_b", "ffn_w_down"]
SMALL = ["g_" + n for n in REPLICATED]

GATHER_FIRST = ["sgu_w_in", "ffn_conv_w"]
PLAN = {
    "sgu_in": [("ag1", "sgu_w_out"), ("ag1", "ffn_w_down0")],
    "sgu_gate": [("ag2", "sgu_w_out"), ("ag2", "ffn_w_down0"), ("ag1", "ffn_w_up0")],
    "sgu_out": [("ag2", "ffn_w_up0"), ("ag1", "attn_w_qkv")],
    "ffn0_fwd": [("ag2", "attn_w_qkv"), ("ag1", "attn_w_o"), ("ag1", "ffn_w_up1")],
    "qkv": [("ag2", "attn_w_o"), ("ag2", "ffn_w_up1")],
    "attn": [("ag1", "ffn_w_down1")],
    "attn_out": [("ag2", "ffn_w_down1")],
    "ffn1_bwd2": [("rs1", "ffn_w_down1")],
    "ffn1_dw_up": [("rs2", "ffn_w_down1")],
    "attn_do": [("rs1", "ffn_w_up1")],
    "attn_bwd": [("rs2", "ffn_w_up1"), ("rs1", "attn_w_o")],
    "dw_qkv": [("rs2", "attn_w_o")],
    "dx_qkv": [("rs1", "attn_w_qkv")],
    "ffn0_bwd1": [("rs2", "attn_w_qkv")],
    "ffn0_bwd2": [("rs1", "ffn_w_down0")],
    "ffn0_dw_up": [("rs2", "ffn_w_down0")],
    "sgu_dgated": [("rs1", "ffn_w_up0")],
    "sgu_gate_bwd": [("rs2", "ffn_w_up0"), ("rs1", "sgu_w_out")],
    "dw_sgu_in": [("rs2", "sgu_w_out")] + [("ag1", n) for n in SMALL],
    "last_a": [("rs1", "sgu_w_in"), ("rs1", "ffn_conv_w")] + [("ag2", n) for n in SMALL],
    "last_b": [("rs2", "sgu_w_in"), ("rs2", "ffn_conv_w")],
}


class _Overlap:
    def __init__(self, shard, place):
        self.shard, self.place = shard, place
        self.part, self.full = {}, {}
        self.grads, self.sib, self.own, self.recv = {}, {}, {}, {}

    def w(self, n):
        return self.full[n]

    def grad(self, n, pair):
        self.grads[n] = pair

    def small(self, g_rep):
        self.shard.update(("g_" + n, a) for n, a in _views2d(g_rep).items())

    def chip_sums(self, n):
        sums, self.own[n] = _rs_partial(self.grads[n][0], self.sib.pop(n), self.place, "rs_partial_" + n)
        return sums

    def hook(self, host):
        ops = PLAN.get(host)
        if not ops:
            return None
        where = {"ag1": self.part, "ag2": self.full, "rs1": self.sib, "rs2": self.recv}
        idx = []

        def hook(results=None):
            if results is not None:
                for (kind, n), i in zip(ops, idx):
                    where[kind][n] = results[i]
                return None
            comm = _Comm()
            for kind, n in ops:
                arr = {"ag1": lambda: self.shard[n], "ag2": lambda: self.part.pop(n), "rs1": lambda: self.grads[n][1],
                       "rs2": lambda: self.chip_sums(n)}[kind]()
                idx.append(comm.add(kind, arr))
            return comm

        return hook


TRANSPOSED = {"attn_w_qkv"}
PHYSICAL_T = {"ffn_w_up"}
SHARDED = {
    "sgu_w_in": ["sgu_w_in"], "sgu_w_out": ["sgu_w_out"], "attn_w_qkv": ["attn_w_qkv"], "attn_w_o": ["attn_w_o"],
    "ffn_w_up": ["ffn_w_up0", "ffn_w_up1"], "ffn_w_down": ["ffn_w_down0", "ffn_w_down1"], "ffn_conv_w": ["ffn_conv_w"],
}


def _send_views(w):
    out = {"ffn_conv_w": w["ffn_conv_w"].reshape(6, -1)}
    for name, parts in SHARDED.items():
        if name != "ffn_conv_w":
            out.update((p, (w[name][l].T if name in TRANSPOSED else w[name][l]).astype(BF16)) for l, p in enumerate(parts))
    return out


def _views2d(d):
    return {n: d[n].reshape(-1, d[n].shape[-1]) for n in REPLICATED}


def kernel(x, mix_norm, ffn_norm, sgu_w_in, sgu_v_gain, sgu_w_s, sgu_b_s, sgu_w_out, attn_w_qkv, attn_q_gain, attn_k_gain, attn_sinks, attn_w_o, rel_bias, ffn_w_up, ffn_conv_w, ffn_conv_b, ffn_w_down, loss_target, m_mix_norm, m_ffn_norm, m_sgu_w_in, m_sgu_v_gain, m_sgu_w_s, m_sgu_b_s, m_sgu_w_out, m_attn_w_qkv, m_attn_q_gain, m_attn_k_gain, m_attn_sinks, m_attn_w_o, m_rel_bias, m_ffn_w_up, m_ffn_conv_w, m_ffn_conv_b, m_ffn_w_down, v_mix_norm, v_ffn_norm, v_sgu_w_in, v_sgu_v_gain, v_sgu_w_s, v_sgu_b_s, v_sgu_w_out, v_attn_w_qkv, v_attn_q_gain, v_attn_k_gain, v_attn_sinks, v_attn_w_o, v_rel_bias, v_ffn_w_up, v_ffn_conv_w, v_ffn_conv_b, v_ffn_w_down):
    w = dict(zip(WEIGHTS, (mix_norm, ffn_norm, sgu_w_in, sgu_v_gain, sgu_w_s, sgu_b_s, sgu_w_out, attn_w_qkv, attn_q_gain, attn_k_gain,
                           attn_sinks, attn_w_o, rel_bias, ffn_w_up, ffn_conv_w, ffn_conv_b, ffn_w_down)))
    m = dict(zip(WEIGHTS, (m_mix_norm, m_ffn_norm, m_sgu_w_in, m_sgu_v_gain, m_sgu_w_s, m_sgu_b_s, m_sgu_w_out, m_attn_w_qkv, m_attn_q_gain,
                           m_attn_k_gain, m_attn_sinks, m_attn_w_o, m_rel_bias, m_ffn_w_up, m_ffn_conv_w, m_ffn_conv_b, m_ffn_w_down)))
    v = dict(zip(WEIGHTS, (v_mix_norm, v_ffn_norm, v_sgu_w_in, v_sgu_v_gain, v_sgu_w_s, v_sgu_b_s, v_sgu_w_out, v_attn_w_qkv, v_attn_q_gain,
                           v_attn_k_gain, v_attn_sinks, v_attn_w_o, v_rel_bias, v_ffn_w_up, v_ffn_conv_w, v_ffn_conv_b, v_ffn_w_down)))
    rep = {n: w[n] for n in REPLICATED}

    xi, yi, ci = lax.axis_index("x"), lax.axis_index("y"), lax.axis_index("c")
    place = jnp.stack([ci, 2 * xi + yi]).astype(jnp.int32)
    sch = _Overlap(_send_views(w), place)
    sch.full.update(zip(GATHER_FIRST, _allgather([sch.shard[n] for n in GATHER_FIRST], "gather_first")))

    loss, grad_x, g_rep = _local_step(x[0], loss_target[0], rep, sch)
    loss = lax.psum(loss, ("x", "y", "c"))
    _exchange(sch.hook("last_a"), "last_a")
    _exchange(sch.hook("last_b"), "last_b")

    out = [{}, {}, {}, {}]
    for name, parts in SHARDED.items():
        flip = (lambda a: jnp.swapaxes(a, -1, -2)) if name in TRANSPOSED | PHYSICAL_T else (lambda a: a)
        shape = flip(w[name]).shape
        as3d = lambda a: flip(a).reshape(len(parts), -1, shape[-1])
        res = _adamw_shard([sch.own[p] for p in parts], [sch.recv[p] for p in parts], as3d(w[name]), as3d(m[name]), as3d(v[name]),
                           "adamw_" + name, flipped=name in PHYSICAL_T)
        for o, r in zip(out, res):
            o[name] = flip(r.reshape(shape))
    small = _adamw_small([sch.full[n] for n in SMALL], *[list(_views2d(d).values()) for d in (rep, m, v)], "adamw_small")
    for o, res in zip(out, small):
        o.update((n, r.reshape(w[n].shape)) for n, r in zip(REPLICATED, res))

    return (loss, grad_x[None], *[out[0][n] for n in WEIGHTS], *[out[1][n] for n in WEIGHTS],
            *[out[2][n] for n in WEIGHTS], *[out[3][n] for n in WEIGHTS])
```

```python
import functools
import math

import numpy as np
import jax
import jax.numpy as jnp
from jax import lax
from jax.experimental import pallas as pl
from jax.experimental.pallas import tpu as pltpu

F32 = jnp.float32
BF16 = jnp.bfloat16
S = jax.ShapeDtypeStruct

D = 1024
CHUNK = 128
SGU_W = 2048
SGU_G = 16
HD = 64
NH = 16
NKV = 4
KVG = 4
D_FF = 2816
REL_BUCKETS = 32
REL_MAX_DIST = 128
EPS = 1e-6
N_DEV = 8
MESH = pl.DeviceIdType.MESH

ADAM_LR = 0.001
ADAM_B1 = 0.9
ADAM_B2 = 0.999
ADAM_EPS = 1e-08
ADAM_WD = 0.01
ADAM_STEP = 10

ROW_TILE = 512
HALO = 8
FFN_ROWS = 256


def _tm(t):
    return min(ROW_TILE, t)


def _cp(*sem):
    return pltpu.CompilerParams(dimension_semantics=sem)


ANY = pl.BlockSpec(memory_space=pl.ANY)


def _place():
    x, y, c = lax.axis_index("x"), lax.axis_index("y"), lax.axis_index("c")
    return x, y, c, [(1 - x, y), (x, 1 - y), (1 - x, 1 - y)]


class _Comm:
    SEMS = {"ag1": 5, "ag2": 3, "rs1": 4, "rs2": 3}

    def __init__(self):
        self.inputs, self.out_shapes, self.aliases, self.ops, self.n_sems = [], [], {}, [], 0

    def add(self, kind, arr):
        lead = {"ag1": N_DEV, "ag2": None, "rs1": 4, "rs2": 3}[kind]
        shape = arr.shape if lead is None else (lead,) + arr.shape[(0 if kind == "ag1" else 1):]
        if kind == "ag2":
            self.aliases[len(self.inputs)] = len(self.out_shapes)
        self.ops.append((kind, len(self.inputs), len(self.out_shapes), self.n_sems))
        self.inputs.append(arr)
        self.out_shapes.append(S(shape, arr.dtype))
        self.n_sems += self.SEMS[kind]
        return len(self.out_shapes) - 1

    def _copies(self, ins, outs, send, recv):
        x, y, c, chips = _place()
        me, sibling = (x, y, c), (x, y, 1 - c)
        slot = lambda px, py, pc: 4 * px + 2 * py + pc
        sends, recvs, local = [], [], []

        def rc(src, dst, k, to):
            return lambda: pltpu.make_async_remote_copy(src_ref=src(), dst_ref=dst(), send_sem=send.at[k], recv_sem=recv.at[k],
                                                        device_id=to, device_id_type=MESH)

        for kind, ii, oi, b in self.ops:
            src, dst = ins[ii], outs[oi]
            at = lambda ref, i: (lambda: ref.at[i])
            if kind == "ag1":
                whole, mine = (lambda s=src: s), at(dst, slot(*me))
                sends.append(rc(whole, mine, b, sibling))
                recvs.append(rc(whole, at(dst, slot(x, y, 1 - c)), b, me))
                for j, chip in enumerate(chips):
                    sends.append(rc(whole, mine, b + 1 + j, (*chip, c)))
                    recvs.append(rc(whole, at(dst, slot(*chip, c)), b + 1 + j, me))
                local.append(lambda s=src, m=mine, k=b + 4: pltpu.make_async_copy(s, m(), send.at[k]))
            elif kind == "ag2":
                for j, chip in enumerate(chips):
                    sends.append(rc(at(dst, slot(*chip, c)), at(dst, slot(*chip, c)), b + j, sibling))
                    recvs.append(rc(at(dst, slot(*chip, 1 - c)), at(dst, slot(*chip, 1 - c)), b + j, me))
            elif kind == "rs1":
                for k in range(4):
                    sends.append(rc(at(src, 2 * k + (1 - c)), at(dst, k), b + k, sibling))
                    recvs.append(rc(at(src, 2 * k + c), at(dst, k), b + k, me))
            else:
                for j, (px, py) in enumerate(chips):
                    sends.append(rc(at(src, 2 * px + py), at(dst, j), b + j, (px, py, c)))
                    recvs.append(rc(at(src, 2 * px + py), at(dst, j), b + j, me))
        return sends, recvs, local

    def start(self, ins, outs, send, recv):
        sends, _, local = self._copies(ins, outs, send, recv)
        for make in local + sends:
            make().start()

    def finish(self, ins, outs, send, recv):
        sends, recvs, local = self._copies(ins, outs, send, recv)
        for make in recvs:
            make().wait_recv()
        for make in sends:
            make().wait_send()
        for make in local:
            make().wait()


def _run(body, args, hook, *, grid, in_specs, out_specs, out_shape, name, semantics, scratch_shapes=(), aliases=None):
    comm = hook() if hook is not None else None
    aliases = dict(aliases or {})
    if comm is None:
        return pl.pallas_call(body, grid=grid, in_specs=in_specs, out_specs=out_specs, out_shape=out_shape, name=name,
                              scratch_shapes=list(scratch_shapes), input_output_aliases=aliases,
                              compiler_params=_cp(*semantics))(*args)
    single = not isinstance(out_shape, (list, tuple))
    out_shapes = [out_shape] if single else list(out_shape)
    out_specs_l = [out_specs] if single else list(out_specs)
    n_in, n_out, n_scr, ci, co = len(args), len(out_shapes), len(scratch_shapes), len(comm.inputs), len(comm.out_shapes)

    def wrapped(*refs):
        ins, cins = refs[:n_in], refs[n_in:n_in + ci]
        outs, couts = refs[n_in + ci:n_in + ci + n_out], refs[n_in + ci + n_out:n_in + ci + n_out + co]
        scr = refs[n_in + ci + n_out + co:n_in + ci + n_out + co + n_scr]
        send, recv = refs[-2:]
        first = functools.reduce(lambda a, b: a & b, [pl.program_id(a) == 0 for a in range(len(grid))])
        last = functools.reduce(lambda a, b: a & b, [pl.program_id(a) == g - 1 for a, g in enumerate(grid)])

        @pl.when(first)
        def _():
            comm.start(cins, couts, send, recv)

        body(*ins, *outs, *scr)

        @pl.when(last)
        def _():
            comm.finish(cins, couts, send, recv)

    res = pl.pallas_call(
        wrapped, grid=grid, in_specs=list(in_specs) + [ANY] * ci, out_specs=out_specs_l + [ANY] * co,
        out_shape=out_shapes + comm.out_shapes, name=name,
        scratch_shapes=list(scratch_shapes) + [pltpu.SemaphoreType.DMA((comm.n_sems,)), pltpu.SemaphoreType.DMA((comm.n_sems,))],
        input_output_aliases={**aliases, **{n_in + k: n_out + v for k, v in comm.aliases.items()}},
        compiler_params=pltpu.CompilerParams(dimension_semantics=("arbitrary",) * len(grid), has_side_effects=True))(*args, *comm.inputs)
    hook(res[n_out:])
    return res[0] if single else list(res[:n_out])


def _dot(a, b):
    return jnp.dot(a, b, preferred_element_type=F32)


def _dot_nt(a, b):
    return lax.dot_general(a, b, (((1,), (1,)), ((), ())), preferred_element_type=F32)


def _dot_tn(a, b):
    return lax.dot_general(a, b, (((0,), (0,)), ((), ())), preferred_element_type=F32)


def _gelu(x):
    return 0.5 * x * (1.0 + lax.erf(x * (2.0 ** -0.5)))


def _gelu_grad(x):
    return 0.5 * (1.0 + lax.erf(x * (2.0 ** -0.5))) + x * jnp.exp(-0.5 * x * x) * (1.0 / math.sqrt(2.0 * math.pi))


def _sigmoid(x):
    return 1.0 / (1.0 + jnp.exp(-x))


def _rstd(x):
    return lax.rsqrt(jnp.mean(x * x, axis=-1, keepdims=True) + EPS)


def _rel_tables():
    q = np.arange(CHUNK)[:, None] + CHUNK
    k = np.arange(2 * CHUNK)[None, :]
    dist = q - k
    n = np.maximum(dist, 0)
    max_exact = REL_BUCKETS // 2
    large = max_exact + (np.log(np.maximum(n, 1).astype(np.float32) / max_exact)
                         / math.log(REL_MAX_DIST / max_exact) * (REL_BUCKETS - max_exact)).astype(np.int32)
    large = np.minimum(large, REL_BUCKETS - 1)
    return np.where(n < max_exact, n, large).astype(np.int32)


def _rmsnorm(x, gain, name):
    t = x.shape[0]
    tm = _tm(t)

    def body(x_ref, g_ref, o_ref):
        xv = x_ref[...]
        o_ref[...] = (xv * _rstd(xv) * g_ref[...]).astype(BF16)

    return pl.pallas_call(
        body, grid=(t // tm,), name=name,
        in_specs=[pl.BlockSpec((tm, D), lambda i: (i, 0)), pl.BlockSpec((1, D), lambda i: (0, 0))],
        out_specs=pl.BlockSpec((tm, D), lambda i: (i, 0)),
        out_shape=S((t, D), BF16), compiler_params=_cp("parallel"))(x, gain)


def _resident(shape):
    zeros = (0,) * len(shape)
    return pl.BlockSpec(shape, lambda *_: zeros, pipeline_mode=pl.Buffered(1))


def _mm_slot(hn, wg, out_dtype, name, hook=None):
    t, k = hn.shape
    ns, _, n = wg.shape
    tm = _tm(t)

    def body(a_ref, w_ref, o_ref):
        a = a_ref[...]
        for s in range(ns):
            o_ref[s] = _dot(a, w_ref[s]).astype(out_dtype)

    return _run(
        body, [hn, wg], hook, grid=(t // tm,), name=name, semantics=("parallel",),
        in_specs=[pl.BlockSpec((tm, k), lambda i: (i, 0)), _resident(wg.shape)],
        out_specs=pl.BlockSpec((ns, tm, n), lambda i: (0, i, 0)), out_shape=S((ns, t, n), out_dtype))


def _mm_t(hn, wt, name, hook=None):
    t, k = hn.shape
    ns, n, _ = wt.shape
    tm = _tm(t)

    def body(a_ref, w_ref, o_ref):
        a = a_ref[...]
        for s in range(ns):
            o_ref[s * n:(s + 1) * n, :] = _dot_nt(w_ref[s], a)

    return _run(
        body, [hn, wt], hook, grid=(t // tm,), name=name, semantics=("parallel",),
        in_specs=[pl.BlockSpec((tm, k), lambda i: (i, 0)), _resident(wt.shape)],
        out_specs=pl.BlockSpec((ns * n, tm), lambda i: (0, i)), out_shape=S((ns * n, t), F32))


def _conv3(a, prev, cw, cb, tm):
    ext = jnp.concatenate([prev, a], axis=0)
    return cw[2:3] * a + cw[1:2] * ext[HALO - 1:HALO - 1 + tm] + cw[0:1] * ext[HALO - 2:HALO - 2 + tm] + cb


def _ffn_fwd(hn, h, wup, wdown, cw, cb, extra, mode, name, hook=None):
    t, k = hn.shape
    n = wup.shape[-1]
    nh = wup.shape[0] // 2
    tm = min(FFN_ROWS, t)
    ni = t // tm

    def body(a_ref, h_ref, wu_ref, wd_ref, cw_ref, cb_ref, e_ref, as_ref, cs_ref, o1_ref, o2_ref, carry):
        i = pl.program_id(0)

        @pl.when(i == 0)
        def _():
            carry[...] = jnp.zeros_like(carry)

        a = a_ref[...]
        acc = h_ref[...]
        nxt = (_dot(a, wu_ref[0]), _dot(a, wu_ref[nh]))
        for j in range(nh):
            ag, av = nxt
            if j + 1 < nh:
                nxt = (_dot(a, wu_ref[j + 1]), _dot(a, wu_ref[nh + j + 1]))
            as_ref[j] = ag.astype(BF16)
            as_ref[nh + j] = av.astype(BF16)
            cg = _conv3(ag, carry[j], cw_ref[j], cb_ref[j], tm)
            cv = _conv3(av, carry[nh + j], cw_ref[nh + j], cb_ref[nh + j], tm)
            carry[j] = ag[tm - HALO:]
            carry[nh + j] = av[tm - HALO:]
            cs_ref[j] = cg.astype(BF16)
            cs_ref[nh + j] = cv.astype(BF16)
            act = (cg * _sigmoid(cg) * cv).astype(BF16)
            acc = acc + _dot(act, wd_ref[j * n:(j + 1) * n, :])
        if mode == "norm":
            o1_ref[...] = acc
            o2_ref[...] = (acc * _rstd(acc) * e_ref[...]).astype(BF16)
        else:
            err = acc - e_ref[...]
            o1_ref[...] = err * (1.0 / D)
            o2_ref[...] = jnp.full(o2_ref.shape, jnp.sum(err * err), F32)

    row = pl.BlockSpec((tm, D), lambda i: (i, 0))
    if mode == "norm":
        e_spec, o2_spec, o2_shape = pl.BlockSpec((1, D), lambda i: (0, 0)), row, S((t, D), BF16)
    else:
        e_spec, o2_spec, o2_shape = row, pl.BlockSpec((None, 8, 128), lambda i: (i, 0, 0)), S((ni, 8, 128), F32)
    aspec = pl.BlockSpec((2 * nh, tm, n), lambda i: (0, i, 0))
    return _run(
        body, [hn, h, wup, wdown, cw, cb, extra], hook, grid=(ni,), name=name, semantics=("arbitrary",),
        in_specs=[pl.BlockSpec((tm, k), lambda i: (i, 0)), row, _resident(wup.shape), _resident(wdown.shape),
                  _resident(cw.shape), _resident(cb.shape), e_spec],
        out_specs=[aspec, aspec, row, o2_spec],
        out_shape=[S((2 * nh, t, n), BF16), S((2 * nh, t, n), BF16), S((t, D), F32), o2_shape],
        scratch_shapes=[pltpu.VMEM((2 * nh, HALO, n), F32)])


def _tril_mask():
    r = lax.broadcasted_iota(jnp.int32, (CHUNK, CHUNK), 0)
    c = lax.broadcasted_iota(jnp.int32, (CHUNK, CHUNK), 1)
    return r >= c


def _sgu_gate_fwd(a_s, vgain, ws, bst, name, hook=None):
    t = a_s.shape[1]
    sw = a_s.shape[2]
    gps = sw // CHUNK

    def body(a_ref, vg_ref, ws_ref, b_ref, o_ref):
        v = _gelu(jnp.concatenate([a_ref[4 + s].astype(F32) for s in range(4)], axis=1))
        vn = (v * _rstd(v) * vg_ref[...]).astype(BF16)
        tri = _tril_mask()
        for g in range(SGU_G):
            w = jnp.where(tri, ws_ref[g], 0.0).astype(BF16)
            sg = _dot(w, vn[:, g * CHUNK:(g + 1) * CHUNK]) + b_ref[:, g:g + 1]
            lo = (g % gps) * CHUNK
            u = _gelu(a_ref[g // gps, :, lo:lo + CHUNK].astype(F32))
            o_ref[g // gps, :, lo:lo + CHUNK] = (u * sg).astype(BF16)

    return _run(
        body, [a_s, vgain, ws, bst], hook, grid=(t // CHUNK,), name=name, semantics=("parallel",),
        in_specs=[pl.BlockSpec((8, CHUNK, sw), lambda n: (0, n, 0)), pl.BlockSpec((1, SGU_W), lambda n: (0, 0)),
                  pl.BlockSpec((SGU_G, CHUNK, CHUNK), lambda n: (0, 0, 0)), pl.BlockSpec((CHUNK, SGU_G), lambda n: (0, 0))],
        out_specs=pl.BlockSpec((4, CHUNK, sw), lambda n: (0, n, 0)), out_shape=S((4, t, sw), BF16))


def _resid_mm(a_s, w, resid, extra, mode, name, hook=None, fm=False):
    nk, t, kc = (1, a_s.shape[1], a_s.shape[0]) if fm else a_s.shape
    tm = _tm(t)
    ni = t // tm

    def body(a_ref, w_ref, r_ref, e_ref, o1_ref, o2_ref):
        h = r_ref[...]
        if fm:
            h = h + _dot_tn(a_ref[...], w_ref[...])
        for j in range(0 if fm else nk):
            h = h + _dot(a_ref[j], w_ref[j * kc:(j + 1) * kc, :])
        if mode == "norm":
            o1_ref[...] = h
            o2_ref[...] = (h * _rstd(h) * e_ref[...]).astype(BF16)
        else:
            err = h - e_ref[...]
            o1_ref[...] = err * (1.0 / D)
            o2_ref[...] = jnp.full(o2_ref.shape, jnp.sum(err * err), F32)

    row = pl.BlockSpec((tm, D), lambda i: (i, 0))
    if mode == "norm":
        e_spec, o2_spec, o2_shape = pl.BlockSpec((1, D), lambda i: (0, 0)), row, S((t, D), BF16)
    else:
        e_spec, o2_spec, o2_shape = row, pl.BlockSpec((None, 8, 128), lambda i: (i, 0, 0)), S((ni, 8, 128), F32)
    return _run(
        body, [a_s, w, resid, extra], hook, grid=(ni,), name=name, semantics=("parallel",),
        in_specs=[pl.BlockSpec((kc, tm), lambda i: (0, i)) if fm else pl.BlockSpec((nk, tm, kc), lambda i: (0, i, 0)),
                  _resident(w.shape), row, e_spec],
        out_specs=[row, o2_spec], out_shape=[S((t, D), F32), o2_shape])


def _relbias_fwd(rel_bias_t, bucket_row, name):
    nb = bucket_row.shape[1]

    def body(rb_ref, bk_ref, o_ref):
        onehot = (lax.broadcasted_iota(jnp.int32, (REL_BUCKETS, nb), 0) == bk_ref[...]).astype(F32)
        o_ref[...] = jnp.dot(rb_ref[...], onehot, precision=lax.Precision.HIGHEST, preferred_element_type=F32)

    return pl.pallas_call(body, out_shape=S((NH, nb), F32), name=name)(rel_bias_t, bucket_row)


def _relbias_bwd(dbias, bucket_row, name):
    nb = bucket_row.shape[1]

    def body(db_ref, bk_ref, o_ref):
        onehot = (lax.broadcasted_iota(jnp.int32, (REL_BUCKETS, nb), 0) == bk_ref[...]).astype(F32)
        o_ref[...] = lax.dot_general(db_ref[...], onehot, (((1,), (1,)), ((), ())),
                                     precision=lax.Precision.HIGHEST, preferred_element_type=F32)

    return pl.pallas_call(body, out_shape=S((NH, REL_BUCKETS), F32), name=name)(dbias, bucket_row)


QKV = D + 2 * NKV * HD
KV0 = D


def _rstd_rows(x):
    return lax.rsqrt(jnp.mean(x * x, axis=0, keepdims=True) + EPS)


def _attn_valid(n):
    kj = lax.broadcasted_iota(jnp.int32, (2 * CHUNK, CHUNK), 0)
    qi = lax.broadcasted_iota(jnp.int32, (2 * CHUNK, CHUNK), 1)
    dist = qi + CHUNK - kj
    return (dist >= 0) & (dist < CHUNK) & ((n > 0) | (kj >= CHUNK))


def _attn_band(cur_ref, prev_ref, row):
    return jnp.concatenate([prev_ref[row - KV0:row - KV0 + HD, :], cur_ref[row:row + HD, :]], axis=1)


def _attn_probs(kn_tok, qn, bias, valid, sink):
    s = _dot(kn_tok, qn) * (HD ** -0.5) + bias
    s = jnp.where(valid, s, -jnp.inf)
    m = jnp.maximum(jnp.max(s, axis=0, keepdims=True), sink)
    p = jnp.exp(s - m)
    psink = jnp.exp(sink - m)
    inv = 1.0 / (jnp.sum(p, axis=0, keepdims=True) + psink)
    return p * inv, psink * inv


def _attn_fwd(qkv_t, qg, kg, sinks, bias, name, hook=None):
    t = qkv_t.shape[1]

    def body(cur_ref, prev_ref, qg_ref, kg_ref, sink_ref, bias_ref, o_ref):
        n = pl.program_id(0)
        valid = _attn_valid(n)
        for h in range(NKV):
            k = _attn_band(cur_ref, prev_ref, KV0 + HD * h)
            v = _attn_band(cur_ref, prev_ref, KV0 + HD * (NKV + h))
            kn_tok = (k * _rstd_rows(k) * kg_ref[...]).astype(BF16).T
            vb = v.astype(BF16)
            heads = range(KVG * h, KVG * (h + 1))
            qs = [cur_ref[HD * hq:HD * (hq + 1), :] for hq in heads]
            qns = [(q * _rstd_rows(q) * qg_ref[...]).astype(BF16) for q in qs]
            ps = [_attn_probs(kn_tok, qn, bias_ref[hq], valid, sink_ref[hq])[0] for qn, hq in zip(qns, heads)]
            for p, hq in zip(ps, heads):
                o_ref[HD * hq:HD * (hq + 1), :] = _dot(vb, p.astype(BF16)).astype(BF16)

    col = pl.BlockSpec((HD, 1), lambda n: (0, 0))
    return _run(
        body, [qkv_t, qkv_t, qg, kg, sinks, bias], hook, grid=(t // CHUNK,), name=name, semantics=("parallel",),
        in_specs=[pl.BlockSpec((QKV, CHUNK), lambda n: (0, n)),
                  pl.BlockSpec((QKV - KV0, CHUNK), lambda n: (KV0 // (QKV - KV0), jnp.maximum(n - 1, 0))),
                  col, col, pl.BlockSpec(memory_space=pltpu.SMEM), pl.BlockSpec((NH, 2 * CHUNK, CHUNK), lambda n: (0, 0, 0))],
        out_specs=pl.BlockSpec((D, CHUNK), lambda n: (0, n)), out_shape=S((D, t), BF16))


def _dx_rows(dh, w, kc, out_dtype, name, hook=None):
    t = dh.shape[0]
    nk = w.shape[0] // kc
    tm = _tm(t)

    def body(d_ref, w_ref, o_ref):
        dhb = d_ref[...].astype(BF16)
        for j in range(nk):
            o_ref[j] = _dot_nt(dhb, w_ref[j * kc:(j + 1) * kc, :]).astype(out_dtype)

    return _run(
        body, [dh, w], hook, grid=(t // tm,), name=name, semantics=("parallel",),
        in_specs=[pl.BlockSpec((tm, D), lambda i: (i, 0)), _resident(w.shape)],
        out_specs=pl.BlockSpec((nk, tm, kc), lambda i: (0, i, 0)), out_shape=S((nk, t, kc), out_dtype))


def _dx_rows_t(dh, w, name, hook=None):
    t = dh.shape[0]
    k = w.shape[0]
    tm = _tm(t)

    def body(d_ref, w_ref, o_ref):
        o_ref[...] = _dot_nt(w_ref[...], d_ref[...].astype(BF16)).astype(BF16)

    return _run(
        body, [dh, w], hook, grid=(t // tm,), name=name, semantics=("parallel",),
        in_specs=[pl.BlockSpec((tm, D), lambda i: (i, 0)), _resident(w.shape)],
        out_specs=pl.BlockSpec((k, tm), lambda i: (0, i)), out_shape=S((k, t), BF16))


def _ffn_bwd1(dh, c, wdown, name, hook=None):
    ns, t, n = c.shape
    nh = ns // 2
    tm = min(FFN_ROWS, t)
    ni = t // tm

    def body(d_ref, c_ref, wd_ref, dc_ref, dw_hbm, dwb_hbm, acc, stage):
        i = pl.program_id(0)

        @pl.when(i == 0)
        def _():
            acc[...] = jnp.zeros_like(acc)

        dhb = d_ref[...].astype(BF16)
        for j in range(nh):
            dact = _dot_nt(dhb, wd_ref[j * n:(j + 1) * n, :])
            cg = c_ref[j].astype(F32)
            cv = c_ref[nh + j].astype(F32)
            sg = _sigmoid(cg)
            gs = cg * sg
            acc[j * n:(j + 1) * n, :] += _dot_tn((gs * cv).astype(BF16), dhb)
            dc_ref[j] = (dact * cv * (sg + gs * (1.0 - sg))).astype(BF16)
            dc_ref[nh + j] = (dact * gs).astype(BF16)

        @pl.when(i == ni - 1)
        def _():
            pltpu.sync_copy(acc, dw_hbm)
            for j in range(nh):
                stage[...] = acc[j * n:(j + 1) * n, :].astype(BF16)
                pltpu.sync_copy(stage, dwb_hbm.at[pl.ds(j * n, n), :])

    slab = pl.BlockSpec((ns, tm, n), lambda i: (0, i, 0))
    return _run(
        body, [dh, c, wdown], hook, grid=(ni,), name=name, semantics=("arbitrary",),
        in_specs=[pl.BlockSpec((tm, D), lambda i: (i, 0)), slab, _resident(wdown.shape)],
        out_specs=[slab, ANY, ANY], out_shape=[S((ns, t, n), BF16), S(wdown.shape, F32), S(wdown.shape, BF16)],
        scratch_shapes=[pltpu.VMEM(wdown.shape, F32), pltpu.VMEM((n, D), BF16)])


def _ffn_bwd2(dc, a, wup, cw, h, gain, dh_in, name, hook=None):
    ns, t, n = dc.shape
    tm = min(FFN_ROWS, t)
    ni = t // tm

    def body(dc_ref, a_ref, wu_ref, cw_ref, h_ref, g_ref, di_ref, da_ref, o_ref, dg_ref, dcw_ref, dcb_ref, carry, keep):
        i = pl.program_id(0)

        @pl.when(i == 0)
        def _():
            carry[...] = jnp.zeros_like(carry)
            dg_ref[...] = jnp.zeros_like(dg_ref)
            dcw_ref[...] = jnp.zeros_like(dcw_ref)
            dcb_ref[...] = jnp.zeros_like(dcb_ref)

        rsum = lambda v: jnp.sum(v, axis=0, keepdims=True)
        acc = jnp.zeros((tm, D), F32)
        for s in range(ns):
            x = dc_ref[s].astype(F32)
            ext = jnp.concatenate([x, carry[s]], axis=0)
            keep[0] = ext[1:1 + tm]
            keep[1] = ext[2:2 + tm]
            x1, x2 = keep[0], keep[1]
            cwv = cw_ref[s]
            da = (cwv[2:3] * x + cwv[1:2] * x1 + cwv[0:1] * x2).astype(BF16)
            carry[s] = x[:HALO]
            da_ref[s] = da
            acc = acc + _dot_nt(da, wu_ref[s])
            av = a_ref[s].astype(F32)
            dcw_ref[s] += jnp.concatenate([rsum(x2 * av), rsum(x1 * av), rsum(x * av)], axis=0)
            dcb_ref[s] += rsum(x)
        hv = h_ref[...]
        r = _rstd(hv)
        gg = acc * g_ref[...]
        o_ref[...] = di_ref[...] + r * gg - hv * (r * r * r * jnp.mean(gg * hv, axis=-1, keepdims=True))
        dg_ref[...] += jnp.sum(acc * hv * r, axis=0, keepdims=True)

    slab = pl.BlockSpec((ns, tm, n), lambda i: (0, ni - 1 - i, 0))
    row = pl.BlockSpec((tm, D), lambda i: (ni - 1 - i, 0))
    vec = pl.BlockSpec((1, D), lambda i: (0, 0))
    whole = lambda shape: pl.BlockSpec(shape, lambda i: (0,) * len(shape))
    return _run(
        body, [dc, a, wup, cw, h, gain, dh_in], hook, grid=(ni,), name=name, semantics=("arbitrary",),
        in_specs=[slab, slab, _resident(wup.shape), _resident(cw.shape), row, vec, row],
        out_specs=[slab, row, vec, whole((ns, 3, n)), whole((ns, 1, n))],
        out_shape=[S((ns, t, n), BF16), S((t, D), F32), S((1, D), F32), S((ns, 3, n), F32), S((ns, 1, n), F32)],
        scratch_shapes=[pltpu.VMEM((ns, HALO, n), F32), pltpu.VMEM((2, tm, n), F32)])


def _dw_slot(hn, dy_s, name, hook=None):
    t, k = hn.shape
    ns, _, n = dy_s.shape
    tm = _tm(t)

    def body(a_ref, b_ref, o_ref, ob_ref, at_ref):
        @pl.when(pl.program_id(0) == 0)
        def _():
            for i in range(t // tm):
                at_ref[:, i * tm:(i + 1) * tm] = a_ref[i * tm:(i + 1) * tm, :].T

        acc = _dot(at_ref[...], b_ref[...])
        o_ref[...] = acc
        ob_ref[...] = acc.astype(BF16)

    ospec = pl.BlockSpec((None, k, n), lambda j: (j, 0, 0))
    return _run(
        body, [hn, dy_s], hook, grid=(ns,), name=name, semantics=("arbitrary",),
        in_specs=[_resident(hn.shape), pl.BlockSpec((None, t, n), lambda j: (j, 0, 0))],
        out_specs=[ospec, ospec], out_shape=[S((ns, k, n), F32), S((ns, k, n), BF16)],
        scratch_shapes=[pltpu.VMEM((k, t), BF16)])


def _dw_rows(a_s, dh, name, hook=None, fm=False):
    nk, t, kc = (1, a_s.shape[1], a_s.shape[0]) if fm else a_s.shape
    tm = _tm(t)
    ni = t // tm

    def body(a_ref, d_ref, o_ref, ob_ref):
        i = pl.program_id(0)
        dhb = d_ref[...].astype(BF16)

        @pl.when(i == 0)
        def _():
            o_ref[...] = jnp.zeros_like(o_ref)

        if fm:
            o_ref[...] += _dot(a_ref[...], dhb)
        for j in range(0 if fm else nk):
            o_ref[j * kc:(j + 1) * kc, :] += _dot_tn(a_ref[j], dhb)

        @pl.when(i == ni - 1)
        def _():
            ob_ref[...] = o_ref[...].astype(BF16)

    ospec = pl.BlockSpec((nk * kc, D), lambda i: (0, 0))
    return _run(
        body, [a_s, dh], hook, grid=(ni,), name=name, semantics=("arbitrary",),
        in_specs=[pl.BlockSpec((kc, tm), lambda i: (0, i)) if fm else pl.BlockSpec((nk, tm, kc), lambda i: (0, i, 0)),
                  pl.BlockSpec((tm, D), lambda i: (i, 0))],
        out_specs=[ospec, ospec], out_shape=[S((nk * kc, D), F32), S((nk * kc, D), BF16)])


def _dx_slot_normbwd(dy_s, wg, h, gain, dh_in, name, hook=None, fm=False):
    ns, t, n = (1, dy_s.shape[1], dy_s.shape[0]) if fm else dy_s.shape
    tm = _tm(t)

    def body(dy_ref, w_ref, h_ref, g_ref, di_ref, o_ref, dg_ref):
        i = pl.program_id(0)

        @pl.when(i == 0)
        def _():
            dg_ref[...] = jnp.zeros_like(dg_ref)

        g = _dot_tn(dy_ref[...], w_ref[...]) if fm else _dot_nt(dy_ref[0], w_ref[0])
        for s in range(1, ns):
            g = g + _dot_nt(dy_ref[s], w_ref[s])
        hv = h_ref[...]
        r = _rstd(hv)
        gg = g * g_ref[...]
        o_ref[...] = di_ref[...] + r * gg - hv * (r * r * r * jnp.mean(gg * hv, axis=-1, keepdims=True))
        dg_ref[...] += jnp.sum(g * hv * r, axis=0, keepdims=True)

    row = pl.BlockSpec((tm, D), lambda i: (i, 0))
    vec = pl.BlockSpec((1, D), lambda i: (0, 0))
    return _run(
        body, [dy_s, wg, h, gain, dh_in], hook, grid=(t // tm,), name=name, semantics=("arbitrary",),
        in_specs=[pl.BlockSpec((n, tm), lambda i: (0, i)) if fm else pl.BlockSpec((ns, tm, n), lambda i: (0, i, 0)),
                  _resident(wg.shape), row, vec, row],
        out_specs=[row, vec], out_shape=[S((t, D), F32), S((1, D), F32)])


def _sgu_gate_bwd(a_s, dg_s, vgain, ws, bst, name, hook=None):
    t = a_s.shape[1]
    sw = a_s.shape[2]
    gps = sw // CHUNK

    def body(a_ref, dg_ref, vg_ref, ws_ref, b_ref, da_ref, dws_ref, dbt_ref, dvg_ref, dvn_ref):
        n = pl.program_id(0)

        @pl.when(n == 0)
        def _():
            dws_ref[...] = jnp.zeros_like(dws_ref)
            dbt_ref[...] = jnp.zeros_like(dbt_ref)
            dvg_ref[...] = jnp.zeros_like(dvg_ref)

        vpre = jnp.concatenate([a_ref[4 + s].astype(F32) for s in range(4)], axis=1)
        v = _gelu(vpre)
        r = _rstd(v)
        vhat = v * r
        vn = (vhat * vg_ref[...]).astype(BF16)
        tri = _tril_mask()
        lane = lax.broadcasted_iota(jnp.int32, (CHUNK, CHUNK), 1)
        dbt = jnp.zeros((CHUNK, CHUNK), F32)
        for g in range(SGU_G):
            w = jnp.where(tri, ws_ref[g], 0.0).astype(BF16)
            vng = vn[:, g * CHUNK:(g + 1) * CHUNK]
            sg = _dot(w, vng) + b_ref[:, g:g + 1]
            lo = (g % gps) * CHUNK
            upre = a_ref[g // gps, :, lo:lo + CHUNK].astype(F32)
            dgate = dg_ref[g // gps, :, lo:lo + CHUNK].astype(F32)
            da_ref[g // gps, :, lo:lo + CHUNK] = (dgate * sg * _gelu_grad(upre)).astype(BF16)
            ds = dgate * _gelu(upre)
            dsb = ds.astype(BF16)
            dvn_ref[:, g * CHUNK:(g + 1) * CHUNK] = _dot_tn(w, dsb)
            dws_ref[g] += jnp.where(tri, _dot_nt(dsb, vng), 0.0)
            dbt = dbt + jnp.where(lane == g, jnp.sum(ds, axis=-1, keepdims=True), 0.0)
        dbt_ref[...] += dbt
        dvn = dvn_ref[...]
        dvg_ref[...] += jnp.sum(dvn * vhat, axis=0, keepdims=True)
        gg = dvn * vg_ref[...]
        dv = r * gg - v * (r * r * r * jnp.mean(gg * v, axis=-1, keepdims=True))
        dav = (dv * _gelu_grad(vpre)).astype(BF16)
        for s in range(4):
            da_ref[4 + s] = dav[:, s * sw:(s + 1) * sw]

    return _run(
        body, [a_s, dg_s, vgain, ws, bst], hook, grid=(t // CHUNK,), name=name, semantics=("arbitrary",),
        in_specs=[pl.BlockSpec((8, CHUNK, sw), lambda n: (0, n, 0)), pl.BlockSpec((4, CHUNK, sw), lambda n: (0, n, 0)),
                  pl.BlockSpec((1, SGU_W), lambda n: (0, 0)), pl.BlockSpec((SGU_G, CHUNK, CHUNK), lambda n: (0, 0, 0)),
                  pl.BlockSpec((CHUNK, SGU_G), lambda n: (0, 0))],
        out_specs=[pl.BlockSpec((8, CHUNK, sw), lambda n: (0, n, 0)), pl.BlockSpec((SGU_G, CHUNK, CHUNK), lambda n: (0, 0, 0)),
                   pl.BlockSpec((CHUNK, CHUNK), lambda n: (0, 0)), pl.BlockSpec((1, SGU_W), lambda n: (0, 0))],
        out_shape=[S((8, t, sw), BF16), S((SGU_G, CHUNK, CHUNK), F32), S((CHUNK, CHUNK), F32), S((1, SGU_W), F32)],
        scratch_shapes=[pltpu.VMEM((CHUNK, SGU_W), F32)])


def _attn_bwd(qkv_t, do_t, qg, kg, sinks, bias, name, hook=None):
    t = qkv_t.shape[1]
    nb = t // CHUNK

    def body(cur_ref, prev_ref, do_ref, qg_ref, kg_ref, sink_ref, bias_ref,
             o_ref, dqg_out, dkg_out, dsk_out, dbias_ref, carry, dqg_ref, dkg_ref, dsk_ref):
        n = pl.program_id(0)

        @pl.when(n == 0)
        def _():
            carry[...] = jnp.zeros_like(carry)
            dqg_ref[...] = jnp.zeros_like(dqg_ref)
            dkg_ref[...] = jnp.zeros_like(dkg_ref)
            dsk_ref[...] = jnp.zeros_like(dsk_ref)
            dbias_ref[...] = jnp.zeros_like(dbias_ref)

        @pl.when(n < nb)
        def _():
            valid = _attn_valid(n)
            o_ref[0:KV0, :] = carry[0:KV0, :].astype(BF16)
            for h in range(NKV):
                krow = KV0 + HD * h
                vrow = KV0 + HD * (NKV + h)
                k = _attn_band(cur_ref, prev_ref, krow)
                v = _attn_band(cur_ref, prev_ref, vrow)
                rk = _rstd_rows(k)
                khat = k * rk
                kn = (khat * kg_ref[...]).astype(BF16)
                kn_tok = kn.T
                vb = v.astype(BF16)
                v_tok = vb.T
                heads = range(KVG * h, KVG * (h + 1))
                qs = [cur_ref[HD * hq:HD * (hq + 1), :] for hq in heads]
                rqs = [_rstd_rows(q) for q in qs]
                qhats = [q * rq for q, rq in zip(qs, rqs)]
                qns = [(qhat * qg_ref[...]).astype(BF16) for qhat in qhats]
                probs = [_attn_probs(kn_tok, qn, bias_ref[hq], valid, sink_ref[hq]) for qn, hq in zip(qns, heads)]
                dohs = [do_ref[HD * hq:HD * (hq + 1), :] for hq in heads]
                dps = [_dot(v_tok, doh) for doh in dohs]
                dsums = [jnp.sum(p * dp, axis=0, keepdims=True) for (p, _), dp in zip(probs, dps)]
                dss = [p * (dp - dsum) for (p, _), dp, dsum in zip(probs, dps, dsums)]
                for hq, (_, psink), dsum, ds in zip(heads, probs, dsums, dss):
                    dsk_ref[hq:hq + 1, :] -= psink * dsum
                    dbias_ref[hq] += ds
                dv = sum(_dot_nt(doh, p.astype(BF16)) for doh, (p, _) in zip(dohs, probs))
                dscs = [(ds * (HD ** -0.5)).astype(BF16) for ds in dss]
                dqns = [_dot(kn, dsc) for dsc in dscs]
                dkn = sum(_dot_nt(qn, dsc) for qn, dsc in zip(qns, dscs))
                dqg_ref[...] += sum(dqn * qhat for dqn, qhat in zip(dqns, qhats))
                for hq, q, rq, dqn in zip(heads, qs, rqs, dqns):
                    gq = dqn * qg_ref[...]
                    carry[HD * hq:HD * (hq + 1), :] = rq * gq - q * (rq * rq * rq * jnp.mean(gq * q, axis=0, keepdims=True))
                dkg_ref[...] += dkn * khat
                gk = dkn * kg_ref[...]
                dk = rk * gk - k * (rk * rk * rk * jnp.mean(gk * k, axis=0, keepdims=True))
                o_ref[krow:krow + HD, :] = (carry[krow:krow + HD, :] + dk[:, :CHUNK]).astype(BF16)
                o_ref[vrow:vrow + HD, :] = (carry[vrow:vrow + HD, :] + dv[:, :CHUNK]).astype(BF16)
                carry[krow:krow + HD, :] = dk[:, CHUNK:]
                carry[vrow:vrow + HD, :] = dv[:, CHUNK:]

        @pl.when(n == nb)
        def _():
            o_ref[...] = carry[...].astype(BF16)
            dqg_out[...] = jnp.sum(dqg_ref[...], axis=1, keepdims=True)
            dkg_out[...] = jnp.sum(dkg_ref[...], axis=1, keepdims=True)
            dsk_out[...] = jnp.sum(dsk_ref[...], axis=1, keepdims=True)

    cur = lambda n: (0, jnp.minimum(n, nb - 1))
    col = pl.BlockSpec((HD, 1), lambda n: (0, 0))
    whole = lambda shape: pl.BlockSpec(shape, lambda n: (0,) * len(shape))
    return _run(
        body, [qkv_t, qkv_t, do_t, qg, kg, sinks, bias], hook, grid=(nb + 1,), name=name, semantics=("arbitrary",),
        in_specs=[pl.BlockSpec((QKV, CHUNK), cur),
                  pl.BlockSpec((QKV - KV0, CHUNK), lambda n: (KV0 // (QKV - KV0), jnp.clip(n - 1, 0, nb - 1))),
                  pl.BlockSpec((D, CHUNK), cur), col, col, pl.BlockSpec(memory_space=pltpu.SMEM), whole((NH, 2 * CHUNK, CHUNK))],
        out_specs=[pl.BlockSpec((QKV, CHUNK), lambda n: (0, jnp.maximum(n - 1, 0))), whole((HD, 1)), whole((HD, 1)),
                   whole((NH, 1)), whole((NH, 2 * CHUNK, CHUNK))],
        out_shape=[S((QKV, t), BF16), S((HD, 1), F32), S((HD, 1), F32), S((NH, 1), F32), S((NH, 2 * CHUNK, CHUNK), F32)],
        scratch_shapes=[pltpu.VMEM((QKV, CHUNK), F32), pltpu.VMEM((HD, CHUNK), F32), pltpu.VMEM((HD, 2 * CHUNK), F32),
                        pltpu.VMEM((NH, CHUNK), F32)])


class _Plain:
    def __init__(self, wg):
        self.full, self.grads = wg, {}

    def w(self, n):
        return self.full[n]

    def hook(self, host):
        return None

    def grad(self, n, pair):
        self.grads[n] = pair

    def small(self, g_rep):
        pass


def _local_step(x, target, rep, sch):
    bucket_row = jnp.asarray(_rel_tables().T.reshape(1, -1))
    bias = _relbias_fwd(rep["rel_bias"].T, bucket_row, "relbias_fwd").reshape(NH, 2 * CHUNK, CHUNK)
    bst = rep["sgu_b_s"][0].T
    ws = rep["sgu_w_s"][0]
    vgain = rep["sgu_v_gain"]
    qg, kg, sinks = rep["attn_q_gain"].reshape(HD, 1), rep["attn_k_gain"].reshape(HD, 1), rep["attn_sinks"][0]
    w_down = lambda l: sch.w("ffn_w_down%d" % l).reshape(D_FF, D)
    w_up = lambda l: sch.w("ffn_w_up%d" % l)
    cw = [sch.w("ffn_conv_w")[:, 3 * l:3 * l + 3] for l in range(2)]
    cb = [rep["ffn_conv_b"][l].reshape(8, 1, -1) for l in range(2)]
    mixg = [rep["mix_norm"][l:l + 1] for l in range(2)]
    ffng = [rep["ffn_norm"][l:l + 1] for l in range(2)]
    rows = lambda pair: tuple(g.reshape(N_DEV, -1, D) for g in pair)
    hk = sch.hook

    hn0 = _rmsnorm(x, mixg[0], "norm0")
    a0 = _mm_slot(hn0, sch.w("sgu_w_in"), BF16, "sgu_in", hk("sgu_in"))
    gated = _sgu_gate_fwd(a0, vgain, ws, bst, "sgu_gate", hk("sgu_gate"))
    h1, hn1 = _resid_mm(gated, sch.w("sgu_w_out").reshape(SGU_W, D), x, ffng[0], "norm", "sgu_out", hk("sgu_out"))
    a_ff0, c_ff0, h2, hn2 = _ffn_fwd(hn1, h1, w_up(0), w_down(0), cw[0], cb[0], mixg[1], "norm", "ffn0_fwd", hk("ffn0_fwd"))
    qkv = _mm_t(hn2, sch.w("attn_w_qkv"), "qkv", hk("qkv"))
    o = _attn_fwd(qkv, qg, kg, sinks, bias, "attn", hk("attn"))
    h3, hn3 = _resid_mm(o, sch.w("attn_w_o").reshape(D, D), h2, ffng[1], "norm", "attn_out", hk("attn_out"), fm=True)
    a_ff1, c_ff1, dy, sq = _ffn_fwd(hn3, h3, w_up(1), w_down(1), cw[1], cb[1], target, "loss", "ffn1_fwd_loss", hk("ffn1_fwd_loss"))
    loss = (0.5 / D) * jnp.sum(sq[:, 0, 0])

    def ffn_bwd(dh, h_in, hn, a, c, l, tag):
        dc, g_down, g_down_b = _ffn_bwd1(dh, c, w_down(l), tag + "_bwd1", hk(tag + "_bwd1"))
        sch.grad("ffn_w_down%d" % l, rows((g_down, g_down_b)))
        da, dh_new, dgain, g_cw, g_cb = _ffn_bwd2(dc, a, w_up(l), cw[l], h_in, ffng[l], dh, tag + "_bwd2", hk(tag + "_bwd2"))
        sch.grad("ffn_w_up%d" % l, _dw_slot(hn, da, tag + "_dw_up", hk(tag + "_dw_up")))
        return dh_new, dgain, g_cw, g_cb.reshape(-1)

    dh, d_ffng1, g_cw1, g_cb1 = ffn_bwd(dy, h3, hn3, a_ff1, c_ff1, 1, "ffn1")
    do = _dx_rows_t(dh, sch.w("attn_w_o").reshape(D, D), "attn_do", hk("attn_do"))
    sch.grad("attn_w_o", rows(_dw_rows(o, dh, "dw_o", hk("dw_o"), fm=True)))
    dqkv, d_qg, d_kg, d_sk, d_bias = _attn_bwd(qkv, do, qg, kg, sinks, bias, "attn_bwd", hk("attn_bwd"))
    sch.grad("attn_w_qkv", tuple(g.reshape(N_DEV, -1, D) for g in _dw_rows(dqkv, hn2, "dw_qkv", hk("dw_qkv"), fm=True)))
    dh, d_mixg1 = _dx_slot_normbwd(dqkv, sch.w("attn_w_qkv").reshape(QKV, D), h2, mixg[1], dh, "dx_qkv", hk("dx_qkv"), fm=True)
    d_relb = _relbias_bwd(d_bias.reshape(NH, -1), bucket_row, "relbias_bwd").T
    g_rep = {"attn_q_gain": d_qg.reshape(1, HD), "attn_k_gain": d_kg.reshape(1, HD), "attn_sinks": d_sk.reshape(1, NH),
             "rel_bias": d_relb}
    sch.small(g_rep)
    dh, d_ffng0, g_cw0, g_cb0 = ffn_bwd(dh, h1, hn1, a_ff0, c_ff0, 0, "ffn0")
    g_cw = jnp.concatenate([g_cw0, g_cw1], axis=1)
    sch.grad("ffn_conv_w", (g_cw, g_cw.astype(BF16)))
    g_ffn = {"ffn_norm": jnp.concatenate([d_ffng0, d_ffng1], axis=0), "ffn_conv_b": jnp.stack([g_cb0, g_cb1], axis=0)}
    sch.small(g_ffn)
    dgated = _dx_rows(dh, sch.w("sgu_w_out").reshape(SGU_W, D), SGU_W // 4, BF16, "sgu_dgated", hk("sgu_dgated"))
    sch.grad("sgu_w_out", rows(_dw_rows(gated, dh, "dw_sgu_out", hk("dw_sgu_out"))))
    da0, d_ws, d_bst, d_vgain = _sgu_gate_bwd(a0, dgated, vgain, ws, bst, "sgu_gate_bwd", hk("sgu_gate_bwd"))
    grad_x, d_mixg0 = _dx_slot_normbwd(da0, sch.w("sgu_w_in"), x, mixg[0], dh, "dx_sgu_in")
    g_sgu = {"sgu_v_gain": d_vgain, "sgu_w_s": d_ws[None], "sgu_b_s": d_bst[:, :SGU_G].T[None],
             "mix_norm": jnp.concatenate([d_mixg0, d_mixg1], axis=0)}
    sch.small(g_sgu)
    g_rep.update(g_ffn)
    g_rep.update(g_sgu)
    sch.grad("sgu_w_in", _dw_slot(hn0, da0, "dw_sgu_in", hk("dw_sgu_in")))
    return loss, grad_x, g_rep


def _allgather(xs, name):
    nt = len(xs)

    def body(*refs):
        x_refs, o_refs = refs[:nt], refs[nt:2 * nt]
        send_sems, recv_sems, local_sems = refs[2 * nt:]
        x, y, c, chips = _place()
        me, sibling = (x, y, c), (x, y, 1 - c)

        def copy(t, k, block, to, src=None):
            px, py, pc = block
            dst = o_refs[t].at[4 * px + 2 * py + pc]
            return pltpu.make_async_remote_copy(
                src_ref=dst if src is None else src, dst_ref=dst, send_sem=send_sems.at[t, k], recv_sem=recv_sems.at[t, k],
                device_id=to, device_id_type=MESH)

        mine = [pltpu.make_async_copy(x_refs[t], o_refs[t].at[4 * x + 2 * y + c], local_sems.at[t]) for t in range(nt)]
        for cp in mine:
            cp.start()
        first = []
        for t in range(nt):
            first.append(copy(t, 0, me, sibling, src=x_refs[t]))
            first += [copy(t, 1 + j, me, (*chip, c), src=x_refs[t]) for j, chip in enumerate(chips)]
        for cp in first:
            cp.start()
        passed = []
        for j, chip in enumerate(chips):
            for t in range(nt):
                copy(t, 1 + j, (*chip, c), me).wait_recv()
                fwd = copy(t, 4 + j, (*chip, c), sibling)
                fwd.start()
                passed.append(fwd)
        for t in range(nt):
            copy(t, 0, sibling, me).wait_recv()
            for j, chip in enumerate(chips):
                copy(t, 4 + j, (*chip, 1 - c), me).wait_recv()
        for cp in first + passed:
            cp.wait_send()
        for cp in mine:
            cp.wait()

    return pl.pallas_call(
        body, name=name, in_specs=[ANY] * nt, out_specs=[ANY] * nt,
        out_shape=[S((N_DEV,) + a.shape, a.dtype) for a in xs],
        scratch_shapes=[pltpu.SemaphoreType.DMA((nt, 7)), pltpu.SemaphoreType.DMA((nt, 7)), pltpu.SemaphoreType.DMA((nt,))],
        compiler_params=pltpu.CompilerParams(has_side_effects=True))(*xs)


def _exchange(hook, name):
    comm = hook()
    ci, co = len(comm.inputs), len(comm.out_shapes)

    def body(*refs):
        cins, couts = refs[:ci], refs[ci:ci + co]
        send, recv = refs[-2:]
        comm.start(cins, couts, send, recv)
        comm.finish(cins, couts, send, recv)

    res = pl.pallas_call(
        body, name=name, in_specs=[ANY] * ci, out_specs=[ANY] * co, out_shape=comm.out_shapes,
        scratch_shapes=[pltpu.SemaphoreType.DMA((comm.n_sems,)), pltpu.SemaphoreType.DMA((comm.n_sems,))],
        input_output_aliases=dict(comm.aliases),
        compiler_params=pltpu.CompilerParams(has_side_effects=True))(*comm.inputs)
    hook(res)


def _row_tile(r):
    tr = r if r <= ROW_TILE or r % ROW_TILE else ROW_TILE
    assert r % tr == 0
    return tr


def _rs_partial(g32, sib, place, name):
    _, r, cdim = g32.shape
    tr = _row_tile(r)

    def body(place_ref, g_ref, s_ref, p_ref, own_ref):
        k = pl.program_id(1)
        tot = g_ref[...] + s_ref[...].astype(F32)
        p_ref[...] = tot.astype(BF16)

        @pl.when(k == place_ref[1])
        def _():
            own_ref[...] = tot

    grid_spec = pltpu.PrefetchScalarGridSpec(
        num_scalar_prefetch=1, grid=(r // tr, 4),
        in_specs=[pl.BlockSpec((None, None, tr, cdim), lambda i, k, pr: (k, pr[0], i, 0)),
                  pl.BlockSpec((None, tr, cdim), lambda i, k, pr: (k, i, 0))],
        out_specs=[pl.BlockSpec((None, tr, cdim), lambda i, k, pr: (k, i, 0)), pl.BlockSpec((tr, cdim), lambda i, k, pr: (i, 0))])
    return pl.pallas_call(
        body, grid_spec=grid_spec, name=name,
        out_shape=[S((4, r, cdim), BF16), S((r, cdim), F32)],
        compiler_params=_cp("parallel", "arbitrary"))(place, g32.reshape(4, 2, r, cdim), sib)


def _adamw_math(w, g, m, v):
    m = ADAM_B1 * m + (1.0 - ADAM_B1) * g
    v = ADAM_B2 * v + (1.0 - ADAM_B2) * (g * g)
    m_hat = m / (1.0 - ADAM_B1 ** ADAM_STEP)
    v_hat = v / (1.0 - ADAM_B2 ** ADAM_STEP)
    delta = -ADAM_LR * (m_hat / (jnp.sqrt(v_hat) + ADAM_EPS) + ADAM_WD * w)
    return delta, m, v


def _adamw_shard(owns, recvs, w, m, v, name, flipped=False):
    nl = w.shape[0]
    r, cdim = owns[0].shape
    tr = _row_tile(r)
    nr = r // tr

    def body(*refs):
        own_refs, recv_refs = refs[:nl], refs[nl:2 * nl]
        w_ref, m_ref, v_ref, g_out, d_out, m_out, v_out = refs[2 * nl:]
        layer = pl.program_id(0)
        g = None
        for l in range(nl):
            gl = own_refs[l][...] + recv_refs[l][0].astype(F32) + recv_refs[l][1].astype(F32) + recv_refs[l][2].astype(F32)
            g = gl if g is None else jnp.where(layer == l, gl, g)
        if flipped:
            g = g.T
        g_out[...] = g
        d_out[...], m_out[...], v_out[...] = _adamw_math(w_ref[...], g, m_ref[...], v_ref[...])

    park = lambda l: (lambda layer, i: (jnp.where(layer == l, i, jnp.where(layer < l, 0, nr - 1)), 0))
    park3 = lambda l: (lambda layer, i: (0, jnp.where(layer == l, i, jnp.where(layer < l, 0, nr - 1)), 0))
    if flipped:
        row = pl.BlockSpec((None, cdim, tr), lambda layer, i: (layer, 0, i))
    else:
        row = pl.BlockSpec((None, tr, cdim), lambda layer, i: (layer, i, 0))
    return pl.pallas_call(
        body, grid=(nl, nr), name=name,
        in_specs=[pl.BlockSpec((tr, cdim), park(l)) for l in range(nl)] + [pl.BlockSpec((3, tr, cdim), park3(l)) for l in range(nl)]
        + [row, row, row],
        out_specs=[row] * 4, out_shape=[S(w.shape, F32)] * 4,
        compiler_params=_cp("arbitrary", "arbitrary"))(*owns, *recvs, w, m, v)


def _adamw_small(galls, ws, ms, vs, name):
    n = len(galls)

    def body(*refs):
        g_refs, w_refs, m_refs, v_refs, outs = refs[:n], refs[n:2 * n], refs[2 * n:3 * n], refs[3 * n:4 * n], refs[4 * n:]
        for i in range(n):
            g = g_refs[i][0]
            for s in range(1, N_DEV):
                g = g + g_refs[i][s]
            outs[i][...] = g
            outs[n + i][...], outs[2 * n + i][...], outs[3 * n + i][...] = _adamw_math(w_refs[i][...], g, m_refs[i][...], v_refs[i][...])

    res = pl.pallas_call(body, out_shape=[S(a.shape, F32) for a in ws] * 4, name=name)(*galls, *ws, *ms, *vs)
    return [res[k * n:(k + 1) * n] for k in range(4)]


REPLICATED = ["mix_norm", "ffn_norm", "sgu_v_gain", "sgu_w_s", "sgu_b_s", "attn_q_gain", "attn_k_gain", "attn_sinks", "rel_bias",
              "ffn_conv_b"]
WEIGHTS = ["mix_norm", "ffn_norm", "sgu_w_in", "sgu_v_gain", "sgu_w_s", "sgu_b_s", "sgu_w_out", "attn_w_qkv", "attn_q_gain",
           "attn_k_gain", "attn_sinks", "attn_w_o", "rel_bias", "ffn_w_up", "ffn_conv_w", "ffn_conv_b", "ffn_w_down"]
SMALL = ["g_" + n for n in REPLICATED]
SMALL_ATTN = ["g_attn_q_gain", "g_attn_k_gain", "g_attn_sinks", "g_rel_bias"]
SMALL_FFN = ["g_ffn_norm", "g_ffn_conv_b"]
SMALL_LATE = [n for n in SMALL if n not in SMALL_ATTN + SMALL_FFN]

GATHER_FIRST = ["sgu_w_in", "ffn_conv_w"]
PLAN = {
    "sgu_in": [("ag1", "sgu_w_out"), ("ag1", "ffn_w_down0")],
    "sgu_gate": [("ag2", "sgu_w_out"), ("ag2", "ffn_w_down0"), ("ag1", "ffn_w_up0")],
    "sgu_out": [("ag2", "ffn_w_up0"), ("ag1", "attn_w_qkv")],
    "ffn0_fwd": [("ag2", "attn_w_qkv"), ("ag1", "attn_w_o"), ("ag1", "ffn_w_up1")],
    "qkv": [("ag2", "attn_w_o"), ("ag2", "ffn_w_up1")],
    "attn": [("ag1", "ffn_w_down1")],
    "attn_out": [("ag2", "ffn_w_down1")],
    "ffn1_bwd2": [("rs1", "ffn_w_down1")],
    "ffn1_dw_up": [("rs2", "ffn_w_down1")],
    "attn_do": [("rs1", "ffn_w_up1")],
    "attn_bwd": [("rs2", "ffn_w_up1"), ("rs1", "attn_w_o")],
    "dw_qkv": [("rs2", "attn_w_o")],
    "dx_qkv": [("rs1", "attn_w_qkv")],
    "ffn0_bwd1": [("rs2", "attn_w_qkv")] + [("ag1", n) for n in SMALL_ATTN],
    "ffn0_bwd2": [("rs1", "ffn_w_down0")] + [("ag2", n) for n in SMALL_ATTN],
    "ffn0_dw_up": [("rs2", "ffn_w_down0")],
    "sgu_dgated": [("rs1", "ffn_w_up0")] + [("ag1", n) for n in SMALL_FFN],
    "dw_sgu_out": [("ag2", n) for n in SMALL_FFN],
    "sgu_gate_bwd": [("rs2", "ffn_w_up0"), ("rs1", "sgu_w_out")],
    "dw_sgu_in": [("rs2", "sgu_w_out")] + [("ag1", n) for n in SMALL_LATE],
    "last_a": [("rs1", "sgu_w_in"), ("rs1", "ffn_conv_w")] + [("ag2", n) for n in SMALL_LATE],
    "last_b": [("rs2", "sgu_w_in"), ("rs2", "ffn_conv_w")],
}


class _Overlap:
    def __init__(self, shard, place):
        self.shard, self.place = shard, place
        self.part, self.full = {}, {}
        self.grads, self.sib, self.own, self.recv = {}, {}, {}, {}

    def w(self, n):
        return self.full[n]

    def grad(self, n, pair):
        self.grads[n] = pair

    def small(self, g_rep):
        self.shard.update(("g_" + n, a) for n, a in _views2d(g_rep).items())

    def chip_sums(self, n):
        sums, self.own[n] = _rs_partial(self.grads[n][0], self.sib.pop(n), self.place, "rs_partial_" + n)
        return sums

    def hook(self, host):
        ops = PLAN.get(host)
        if not ops:
            return None
        where = {"ag1": self.part, "ag2": self.full, "rs1": self.sib, "rs2": self.recv}
        idx = []

        def hook(results=None):
            if results is not None:
                for (kind, n), i in zip(ops, idx):
                    where[kind][n] = results[i]
                return None
            comm = _Comm()
            for kind, n in ops:
                arr = {"ag1": lambda: self.shard[n], "ag2": lambda: self.part.pop(n), "rs1": lambda: self.grads[n][1],
                       "rs2": lambda: self.chip_sums(n)}[kind]()
                idx.append(comm.add(kind, arr))
            return comm

        return hook


TRANSPOSED = {"attn_w_qkv"}
PHYSICAL_T = {"ffn_w_up"}
SHARDED = {
    "sgu_w_in": ["sgu_w_in"], "sgu_w_out": ["sgu_w_out"], "attn_w_qkv": ["attn_w_qkv"], "attn_w_o": ["attn_w_o"],
    "ffn_w_up": ["ffn_w_up0", "ffn_w_up1"], "ffn_w_down": ["ffn_w_down0", "ffn_w_down1"], "ffn_conv_w": ["ffn_conv_w"],
}


def _send_views(w):
    out = {"ffn_conv_w": w["ffn_conv_w"].reshape(6, -1)}
    for name, parts in SHARDED.items():
        if name != "ffn_conv_w":
            out.update((p, (w[name][l].T if name in TRANSPOSED else w[name][l]).astype(BF16)) for l, p in enumerate(parts))
    return out


def _views2d(d):
    return {n: d[n].reshape(-1, d[n].shape[-1]) for n in REPLICATED if n in d}


def kernel(x, mix_norm, ffn_norm, sgu_w_in, sgu_v_gain, sgu_w_s, sgu_b_s, sgu_w_out, attn_w_qkv, attn_q_gain, attn_k_gain, attn_sinks, attn_w_o, rel_bias, ffn_w_up, ffn_conv_w, ffn_conv_b, ffn_w_down, loss_target, m_mix_norm, m_ffn_norm, m_sgu_w_in, m_sgu_v_gain, m_sgu_w_s, m_sgu_b_s, m_sgu_w_out, m_attn_w_qkv, m_attn_q_gain, m_attn_k_gain, m_attn_sinks, m_attn_w_o, m_rel_bias, m_ffn_w_up, m_ffn_conv_w, m_ffn_conv_b, m_ffn_w_down, v_mix_norm, v_ffn_norm, v_sgu_w_in, v_sgu_v_gain, v_sgu_w_s, v_sgu_b_s, v_sgu_w_out, v_attn_w_qkv, v_attn_q_gain, v_attn_k_gain, v_attn_sinks, v_attn_w_o, v_rel_bias, v_ffn_w_up, v_ffn_conv_w, v_ffn_conv_b, v_ffn_w_down):
    w = dict(zip(WEIGHTS, (mix_norm, ffn_norm, sgu_w_in, sgu_v_gain, sgu_w_s, sgu_b_s, sgu_w_out, attn_w_qkv, attn_q_gain, attn_k_gain,
                           attn_sinks, attn_w_o, rel_bias, ffn_w_up, ffn_conv_w, ffn_conv_b, ffn_w_down)))
    m = dict(zip(WEIGHTS, (m_mix_norm, m_ffn_norm, m_sgu_w_in, m_sgu_v_gain, m_sgu_w_s, m_sgu_b_s, m_sgu_w_out, m_attn_w_qkv, m_attn_q_gain,
                           m_attn_k_gain, m_attn_sinks, m_attn_w_o, m_rel_bias, m_ffn_w_up, m_ffn_conv_w, m_ffn_conv_b, m_ffn_w_down)))
    v = dict(zip(WEIGHTS, (v_mix_norm, v_ffn_norm, v_sgu_w_in, v_sgu_v_gain, v_sgu_w_s, v_sgu_b_s, v_sgu_w_out, v_attn_w_qkv, v_attn_q_gain,
                           v_attn_k_gain, v_attn_sinks, v_attn_w_o, v_rel_bias, v_ffn_w_up, v_ffn_conv_w, v_ffn_conv_b, v_ffn_w_down)))
    rep = {n: w[n] for n in REPLICATED}

    xi, yi, ci = lax.axis_index("x"), lax.axis_index("y"), lax.axis_index("c")
    place = jnp.stack([ci, 2 * xi + yi]).astype(jnp.int32)
    sch = _Overlap(_send_views(w), place)
    sch.full.update(zip(GATHER_FIRST, _allgather([sch.shard[n] for n in GATHER_FIRST], "gather_first")))

    loss, grad_x, g_rep = _local_step(x[0], loss_target[0], rep, sch)
    loss = lax.psum(loss, ("x", "y", "c"))
    _exchange(sch.hook("last_a"), "last_a")
    _exchange(sch.hook("last_b"), "last_b")

    out = [{}, {}, {}, {}]
    for name, parts in SHARDED.items():
        flip = (lambda a: jnp.swapaxes(a, -1, -2)) if name in TRANSPOSED | PHYSICAL_T else (lambda a: a)
        shape = flip(w[name]).shape
        as3d = lambda a: flip(a).reshape(len(parts), -1, shape[-1])
        res = _adamw_shard([sch.own[p] for p in parts], [sch.recv[p] for p in parts], as3d(w[name]), as3d(m[name]), as3d(v[name]),
                           "adamw_" + name, flipped=name in PHYSICAL_T)
        for o, r in zip(out, res):
            o[name] = flip(r.reshape(shape))
    small = _adamw_small([sch.full[n] for n in SMALL], *[list(_views2d(d).values()) for d in (rep, m, v)], "adamw_small")
    for o, res in zip(out, small):
        o.update((n, r.reshape(w[n].shape)) for n, r in zip(REPLICATED, res))

    return (loss, grad_x[None], *[out[0][n] for n in WEIGHTS], *[out[1][n] for n in WEIGHTS],
            *[out[2][n] for n in WEIGHTS], *[out[3][n] for n in WEIGHTS])
```

```python
import functools
import math

import numpy as np
import jax
import jax.numpy as jnp
from jax import lax
from jax.experimental import pallas as pl
from jax.experimental.pallas import tpu as pltpu

F32 = jnp.float32
BF16 = jnp.bfloat16
DH = jnp.bfloat16
S = jax.ShapeDtypeStruct

D = 1024
CHUNK = 128
SGU_W = 2048
SGU_G = 16
HD = 64
NH = 16
NKV = 4
KVG = 4
D_FF = 2816
REL_BUCKETS = 32
REL_MAX_DIST = 128
EPS = 1e-6
N_DEV = 8
MESH = pl.DeviceIdType.MESH

ADAM_LR = 0.001
ADAM_B1 = 0.9
ADAM_B2 = 0.999
ADAM_EPS = 1e-08
ADAM_WD = 0.01
ADAM_STEP = 10

ROW_TILE = 512
HALO = 8
FFN_ROWS = 256


def _tm(t):
    return min(ROW_TILE, t)


def _cp(*sem):
    return pltpu.CompilerParams(dimension_semantics=sem)


ANY = pl.BlockSpec(memory_space=pl.ANY)


def _place():
    x, y, c = lax.axis_index("x"), lax.axis_index("y"), lax.axis_index("c")
    return x, y, c, [(1 - x, y), (x, 1 - y), (1 - x, 1 - y)]


class _Comm:
    SEMS = {"ag1": 5, "ag2": 3, "rs1": 4, "rs2": 3}

    def __init__(self):
        self.inputs, self.out_shapes, self.aliases, self.ops, self.n_sems = [], [], {}, [], 0

    def add(self, kind, arr):
        lead = {"ag1": N_DEV, "ag2": None, "rs1": 4, "rs2": 3}[kind]
        shape = arr.shape if lead is None else (lead,) + arr.shape[(0 if kind == "ag1" else 1):]
        if kind == "ag2":
            self.aliases[len(self.inputs)] = len(self.out_shapes)
        self.ops.append((kind, len(self.inputs), len(self.out_shapes), self.n_sems))
        self.inputs.append(arr)
        self.out_shapes.append(S(shape, arr.dtype))
        self.n_sems += self.SEMS[kind]
        return len(self.out_shapes) - 1

    def _copies(self, ins, outs, send, recv):
        x, y, c, chips = _place()
        me, sibling = (x, y, c), (x, y, 1 - c)
        slot = lambda px, py, pc: 4 * px + 2 * py + pc
        sends, recvs, local = [], [], []

        def rc(src, dst, k, to):
            return lambda: pltpu.make_async_remote_copy(src_ref=src(), dst_ref=dst(), send_sem=send.at[k], recv_sem=recv.at[k],
                                                        device_id=to, device_id_type=MESH)

        for kind, ii, oi, b in self.ops:
            src, dst = ins[ii], outs[oi]
            at = lambda ref, i: (lambda: ref.at[i])
            if kind == "ag1":
                whole, mine = (lambda s=src: s), at(dst, slot(*me))
                sends.append(rc(whole, mine, b, sibling))
                recvs.append(rc(whole, at(dst, slot(x, y, 1 - c)), b, me))
                for j, chip in enumerate(chips):
                    sends.append(rc(whole, mine, b + 1 + j, (*chip, c)))
                    recvs.append(rc(whole, at(dst, slot(*chip, c)), b + 1 + j, me))
                local.append(lambda s=src, m=mine, k=b + 4: pltpu.make_async_copy(s, m(), send.at[k]))
            elif kind == "ag2":
                for j, chip in enumerate(chips):
                    sends.append(rc(at(dst, slot(*chip, c)), at(dst, slot(*chip, c)), b + j, sibling))
                    recvs.append(rc(at(dst, slot(*chip, 1 - c)), at(dst, slot(*chip, 1 - c)), b + j, me))
            elif kind == "rs1":
                for k in range(4):
                    sends.append(rc(at(src, 2 * k + (1 - c)), at(dst, k), b + k, sibling))
                    recvs.append(rc(at(src, 2 * k + c), at(dst, k), b + k, me))
            else:
                for j, (px, py) in enumerate(chips):
                    sends.append(rc(at(src, 2 * px + py), at(dst, j), b + j, (px, py, c)))
                    recvs.append(rc(at(src, 2 * px + py), at(dst, j), b + j, me))
        return sends, recvs, local

    def start(self, ins, outs, send, recv):
        sends, _, local = self._copies(ins, outs, send, recv)
        for make in local + sends:
            make().start()

    def finish(self, ins, outs, send, recv):
        sends, recvs, local = self._copies(ins, outs, send, recv)
        for make in recvs:
            make().wait_recv()
        for make in sends:
            make().wait_send()
        for make in local:
            make().wait()


def _run(body, args, hook, *, grid, in_specs, out_specs, out_shape, name, semantics, scratch_shapes=(), aliases=None):
    comm = hook() if hook is not None else None
    aliases = dict(aliases or {})
    if comm is None:
        return pl.pallas_call(body, grid=grid, in_specs=in_specs, out_specs=out_specs, out_shape=out_shape, name=name,
                              scratch_shapes=list(scratch_shapes), input_output_aliases=aliases,
                              compiler_params=_cp(*semantics))(*args)
    single = not isinstance(out_shape, (list, tuple))
    out_shapes = [out_shape] if single else list(out_shape)
    out_specs_l = [out_specs] if single else list(out_specs)
    n_in, n_out, n_scr, ci, co = len(args), len(out_shapes), len(scratch_shapes), len(comm.inputs), len(comm.out_shapes)

    def wrapped(*refs):
        ins, cins = refs[:n_in], refs[n_in:n_in + ci]
        outs, couts = refs[n_in + ci:n_in + ci + n_out], refs[n_in + ci + n_out:n_in + ci + n_out + co]
        scr = refs[n_in + ci + n_out + co:n_in + ci + n_out + co + n_scr]
        send, recv = refs[-2:]
        first = functools.reduce(lambda a, b: a & b, [pl.program_id(a) == 0 for a in range(len(grid))])
        last = functools.reduce(lambda a, b: a & b, [pl.program_id(a) == g - 1 for a, g in enumerate(grid)])

        @pl.when(first)
        def _():
            comm.start(cins, couts, send, recv)

        body(*ins, *outs, *scr)

        @pl.when(last)
        def _():
            comm.finish(cins, couts, send, recv)

    res = pl.pallas_call(
        wrapped, grid=grid, in_specs=list(in_specs) + [ANY] * ci, out_specs=out_specs_l + [ANY] * co,
        out_shape=out_shapes + comm.out_shapes, name=name,
        scratch_shapes=list(scratch_shapes) + [pltpu.SemaphoreType.DMA((comm.n_sems,)), pltpu.SemaphoreType.DMA((comm.n_sems,))],
        input_output_aliases={**aliases, **{n_in + k: n_out + v for k, v in comm.aliases.items()}},
        compiler_params=pltpu.CompilerParams(dimension_semantics=("arbitrary",) * len(grid), has_side_effects=True))(*args, *comm.inputs)
    hook(res[n_out:])
    return res[0] if single else list(res[:n_out])


def _dot(a, b):
    return jnp.dot(a, b, preferred_element_type=F32)


def _dot_nt(a, b):
    return lax.dot_general(a, b, (((1,), (1,)), ((), ())), preferred_element_type=F32)


def _dot_tn(a, b):
    return lax.dot_general(a, b, (((0,), (0,)), ((), ())), preferred_element_type=F32)


def _gelu(x):
    return 0.5 * x * (1.0 + lax.erf(x * (2.0 ** -0.5)))


def _gelu_grad(x):
    return 0.5 * (1.0 + lax.erf(x * (2.0 ** -0.5))) + x * jnp.exp(-0.5 * x * x) * (1.0 / math.sqrt(2.0 * math.pi))


def _sigmoid(x):
    return 1.0 / (1.0 + jnp.exp(-x))


def _rstd(x):
    return lax.rsqrt(jnp.mean(x * x, axis=-1, keepdims=True) + EPS)


def _rel_tables():
    q = np.arange(CHUNK)[:, None] + CHUNK
    k = np.arange(2 * CHUNK)[None, :]
    dist = q - k
    n = np.maximum(dist, 0)
    max_exact = REL_BUCKETS // 2
    large = max_exact + (np.log(np.maximum(n, 1).astype(np.float32) / max_exact)
                         / math.log(REL_MAX_DIST / max_exact) * (REL_BUCKETS - max_exact)).astype(np.int32)
    large = np.minimum(large, REL_BUCKETS - 1)
    return np.where(n < max_exact, n, large).astype(np.int32)


def _rmsnorm(x, gain, name):
    t = x.shape[0]
    tm = _tm(t)

    def body(x_ref, g_ref, o_ref):
        xv = x_ref[...]
        o_ref[...] = (xv * _rstd(xv) * g_ref[...]).astype(BF16)

    return pl.pallas_call(
        body, grid=(t // tm,), name=name,
        in_specs=[pl.BlockSpec((tm, D), lambda i: (i, 0)), pl.BlockSpec((1, D), lambda i: (0, 0))],
        out_specs=pl.BlockSpec((tm, D), lambda i: (i, 0)),
        out_shape=S((t, D), BF16), compiler_params=_cp("parallel"))(x, gain)


def _resident(shape):
    zeros = (0,) * len(shape)
    return pl.BlockSpec(shape, lambda *_: zeros, pipeline_mode=pl.Buffered(1))


def _mm_slot(hn, wg, out_dtype, name, hook=None):
    t, k = hn.shape
    ns, _, n = wg.shape
    tm = _tm(t)

    def body(a_ref, w_ref, o_ref):
        a = a_ref[...]
        for s in range(ns):
            o_ref[s] = _dot(a, w_ref[s]).astype(out_dtype)

    return _run(
        body, [hn, wg], hook, grid=(t // tm,), name=name, semantics=("parallel",),
        in_specs=[pl.BlockSpec((tm, k), lambda i: (i, 0)), _resident(wg.shape)],
        out_specs=pl.BlockSpec((ns, tm, n), lambda i: (0, i, 0)), out_shape=S((ns, t, n), out_dtype))


def _mm_t(hn, wt, name, hook=None):
    t, k = hn.shape
    ns, n, _ = wt.shape
    tm = _tm(t)

    def body(a_ref, w_ref, o_ref):
        a = a_ref[...]
        for s in range(ns):
            o_ref[s * n:(s + 1) * n, :] = _dot_nt(w_ref[s], a)

    return _run(
        body, [hn, wt], hook, grid=(t // tm,), name=name, semantics=("parallel",),
        in_specs=[pl.BlockSpec((tm, k), lambda i: (i, 0)), _resident(wt.shape)],
        out_specs=pl.BlockSpec((ns * n, tm), lambda i: (0, i)), out_shape=S((ns * n, t), F32))


def _conv3(a, prev, cw, cb, tm):
    ext = jnp.concatenate([prev, a], axis=0)
    return cw[2:3] * a + cw[1:2] * ext[HALO - 1:HALO - 1 + tm] + cw[0:1] * ext[HALO - 2:HALO - 2 + tm] + cb


def _ffn_fwd(hn, h, wup, wdown, cw, cb, extra, mode, name, hook=None):
    t, k = hn.shape
    n = wup.shape[-1]
    nh = wup.shape[0] // 2
    tm = min(FFN_ROWS, t)
    ni = t // tm

    def body(a_ref, h_ref, wu_ref, wd_ref, cw_ref, cb_ref, e_ref, as_ref, cs_ref, o1_ref, o2_ref, carry):
        i = pl.program_id(0)

        @pl.when(i == 0)
        def _():
            carry[...] = jnp.zeros_like(carry)

        a = a_ref[...]
        acc = h_ref[...]
        nxt = (_dot(a, wu_ref[0]), _dot(a, wu_ref[nh]))
        for j in range(nh):
            ag, av = nxt
            if j + 1 < nh:
                nxt = (_dot(a, wu_ref[j + 1]), _dot(a, wu_ref[nh + j + 1]))
            as_ref[j] = ag.astype(BF16)
            as_ref[nh + j] = av.astype(BF16)
            cg = _conv3(ag, carry[j], cw_ref[j], cb_ref[j], tm)
            cv = _conv3(av, carry[nh + j], cw_ref[nh + j], cb_ref[nh + j], tm)
            carry[j] = ag[tm - HALO:]
            carry[nh + j] = av[tm - HALO:]
            cs_ref[j] = cg.astype(BF16)
            cs_ref[nh + j] = cv.astype(BF16)
            act = (cg * _sigmoid(cg) * cv).astype(BF16)
            acc = acc + _dot(act, wd_ref[j * n:(j + 1) * n, :])
        if mode == "norm":
            o1_ref[...] = acc
            o2_ref[...] = (acc * _rstd(acc) * e_ref[...]).astype(BF16)
        else:
            err = acc - e_ref[...]
            o1_ref[...] = (err * (1.0 / D)).astype(o1_ref.dtype)
            o2_ref[...] = jnp.full(o2_ref.shape, jnp.sum(err * err), F32)

    row = pl.BlockSpec((tm, D), lambda i: (i, 0))
    if mode == "norm":
        e_spec, o2_spec, o2_shape = pl.BlockSpec((1, D), lambda i: (0, 0)), row, S((t, D), BF16)
    else:
        e_spec, o2_spec, o2_shape = row, pl.BlockSpec((None, 8, 128), lambda i: (i, 0, 0)), S((ni, 8, 128), F32)
    aspec = pl.BlockSpec((2 * nh, tm, n), lambda i: (0, i, 0))
    return _run(
        body, [hn, h, wup, wdown, cw, cb, extra], hook, grid=(ni,), name=name, semantics=("arbitrary",),
        in_specs=[pl.BlockSpec((tm, k), lambda i: (i, 0)), row, _resident(wup.shape), _resident(wdown.shape),
                  _resident(cw.shape), _resident(cb.shape), e_spec],
        out_specs=[aspec, aspec, row, o2_spec],
        out_shape=[S((2 * nh, t, n), BF16), S((2 * nh, t, n), BF16), S((t, D), F32 if mode == "norm" else DH), o2_shape],
        scratch_shapes=[pltpu.VMEM((2 * nh, HALO, n), F32)])


def _tril_mask():
    r = lax.broadcasted_iota(jnp.int32, (CHUNK, CHUNK), 0)
    c = lax.broadcasted_iota(jnp.int32, (CHUNK, CHUNK), 1)
    return r >= c


def _sgu_gate_fwd(a_s, vgain, ws, bst, name, hook=None):
    t = a_s.shape[1]
    sw = a_s.shape[2]
    gps = sw // CHUNK

    def body(a_ref, vg_ref, ws_ref, b_ref, o_ref):
        v = _gelu(jnp.concatenate([a_ref[4 + s].astype(F32) for s in range(4)], axis=1))
        vn = (v * _rstd(v) * vg_ref[...]).astype(BF16)
        tri = _tril_mask()
        for g in range(SGU_G):
            w = jnp.where(tri, ws_ref[g], 0.0).astype(BF16)
            sg = _dot(w, vn[:, g * CHUNK:(g + 1) * CHUNK]) + b_ref[:, g:g + 1]
            lo = (g % gps) * CHUNK
            u = _gelu(a_ref[g // gps, :, lo:lo + CHUNK].astype(F32))
            o_ref[g // gps, :, lo:lo + CHUNK] = (u * sg).astype(BF16)

    return _run(
        body, [a_s, vgain, ws, bst], hook, grid=(t // CHUNK,), name=name, semantics=("parallel",),
        in_specs=[pl.BlockSpec((8, CHUNK, sw), lambda n: (0, n, 0)), pl.BlockSpec((1, SGU_W), lambda n: (0, 0)),
                  pl.BlockSpec((SGU_G, CHUNK, CHUNK), lambda n: (0, 0, 0)), pl.BlockSpec((CHUNK, SGU_G), lambda n: (0, 0))],
        out_specs=pl.BlockSpec((4, CHUNK, sw), lambda n: (0, n, 0)), out_shape=S((4, t, sw), BF16))


def _resid_mm(a_s, w, resid, extra, mode, name, hook=None, fm=False):
    nk, t, kc = (1, a_s.shape[1], a_s.shape[0]) if fm else a_s.shape
    tm = _tm(t)
    ni = t // tm

    def body(a_ref, w_ref, r_ref, e_ref, o1_ref, o2_ref):
        h = r_ref[...]
        if fm:
            h = h + _dot_tn(a_ref[...], w_ref[...])
        for j in range(0 if fm else nk):
            h = h + _dot(a_ref[j], w_ref[j * kc:(j + 1) * kc, :])
        if mode == "norm":
            o1_ref[...] = h
            o2_ref[...] = (h * _rstd(h) * e_ref[...]).astype(BF16)
        else:
            err = h - e_ref[...]
            o1_ref[...] = (err * (1.0 / D)).astype(o1_ref.dtype)
            o2_ref[...] = jnp.full(o2_ref.shape, jnp.sum(err * err), F32)

    row = pl.BlockSpec((tm, D), lambda i: (i, 0))
    if mode == "norm":
        e_spec, o2_spec, o2_shape = pl.BlockSpec((1, D), lambda i: (0, 0)), row, S((t, D), BF16)
    else:
        e_spec, o2_spec, o2_shape = row, pl.BlockSpec((None, 8, 128), lambda i: (i, 0, 0)), S((ni, 8, 128), F32)
    return _run(
        body, [a_s, w, resid, extra], hook, grid=(ni,), name=name, semantics=("parallel",),
        in_specs=[pl.BlockSpec((kc, tm), lambda i: (0, i)) if fm else pl.BlockSpec((nk, tm, kc), lambda i: (0, i, 0)),
                  _resident(w.shape), row, e_spec],
        out_specs=[row, o2_spec], out_shape=[S((t, D), F32 if mode == "norm" else DH), o2_shape])


def _relbias_fwd(rel_bias_t, bucket_row, name):
    nb = bucket_row.shape[1]

    def body(rb_ref, bk_ref, o_ref):
        onehot = (lax.broadcasted_iota(jnp.int32, (REL_BUCKETS, nb), 0) == bk_ref[...]).astype(F32)
        o_ref[...] = jnp.dot(rb_ref[...], onehot, precision=lax.Precision.HIGHEST, preferred_element_type=F32)

    return pl.pallas_call(body, out_shape=S((NH, nb), F32), name=name)(rel_bias_t, bucket_row)


def _relbias_bwd(dbias, bucket_row, name):
    nb = bucket_row.shape[1]

    def body(db_ref, bk_ref, o_ref):
        onehot = (lax.broadcasted_iota(jnp.int32, (REL_BUCKETS, nb), 0) == bk_ref[...]).astype(F32)
        o_ref[...] = lax.dot_general(db_ref[...], onehot, (((1,), (1,)), ((), ())),
                                     precision=lax.Precision.HIGHEST, preferred_element_type=F32)

    return pl.pallas_call(body, out_shape=S((NH, REL_BUCKETS), F32), name=name)(dbias, bucket_row)


QKV = D + 2 * NKV * HD
KV0 = D


def _rstd_rows(x):
    return lax.rsqrt(jnp.mean(x * x, axis=0, keepdims=True) + EPS)


def _attn_valid(n):
    kj = lax.broadcasted_iota(jnp.int32, (2 * CHUNK, CHUNK), 0)
    qi = lax.broadcasted_iota(jnp.int32, (2 * CHUNK, CHUNK), 1)
    dist = qi + CHUNK - kj
    return (dist >= 0) & (dist < CHUNK) & ((n > 0) | (kj >= CHUNK))


def _attn_band(cur_ref, prev_ref, row):
    return jnp.concatenate([prev_ref[row - KV0:row - KV0 + HD, :], cur_ref[row:row + HD, :]], axis=1)


def _attn_probs(kn_tok, qn, bias, valid, sink):
    s = _dot(kn_tok, qn) * (HD ** -0.5) + bias
    s = jnp.where(valid, s, -jnp.inf)
    m = jnp.maximum(jnp.max(s, axis=0, keepdims=True), sink)
    p = jnp.exp(s - m)
    psink = jnp.exp(sink - m)
    inv = 1.0 / (jnp.sum(p, axis=0, keepdims=True) + psink)
    return p * inv, psink * inv


def _attn_fwd(qkv_t, qg, kg, sinks, bias, name, hook=None):
    t = qkv_t.shape[1]

    def body(cur_ref, prev_ref, qg_ref, kg_ref, sink_ref, bias_ref, o_ref):
        n = pl.program_id(0)
        valid = _attn_valid(n)
        for h in range(NKV):
            k = _attn_band(cur_ref, prev_ref, KV0 + HD * h)
            v = _attn_band(cur_ref, prev_ref, KV0 + HD * (NKV + h))
            kn_tok = (k * _rstd_rows(k) * kg_ref[...]).astype(BF16).T
            vb = v.astype(BF16)
            heads = range(KVG * h, KVG * (h + 1))
            qs = [cur_ref[HD * hq:HD * (hq + 1), :] for hq in heads]
            qns = [(q * _rstd_rows(q) * qg_ref[...]).astype(BF16) for q in qs]
            ps = [_attn_probs(kn_tok, qn, bias_ref[hq], valid, sink_ref[hq])[0] for qn, hq in zip(qns, heads)]
            for p, hq in zip(ps, heads):
                o_ref[HD * hq:HD * (hq + 1), :] = _dot(vb, p.astype(BF16)).astype(BF16)

    col = pl.BlockSpec((HD, 1), lambda n: (0, 0))
    return _run(
        body, [qkv_t, qkv_t, qg, kg, sinks, bias], hook, grid=(t // CHUNK,), name=name, semantics=("parallel",),
        in_specs=[pl.BlockSpec((QKV, CHUNK), lambda n: (0, n)),
                  pl.BlockSpec((QKV - KV0, CHUNK), lambda n: (KV0 // (QKV - KV0), jnp.maximum(n - 1, 0))),
                  col, col, pl.BlockSpec(memory_space=pltpu.SMEM), pl.BlockSpec((NH, 2 * CHUNK, CHUNK), lambda n: (0, 0, 0))],
        out_specs=pl.BlockSpec((D, CHUNK), lambda n: (0, n)), out_shape=S((D, t), BF16))


def _dx_rows(dh, w, kc, out_dtype, name, hook=None):
    t = dh.shape[0]
    nk = w.shape[0] // kc
    tm = _tm(t)

    def body(d_ref, w_ref, o_ref):
        dhb = d_ref[...].astype(BF16)
        for j in range(nk):
            o_ref[j] = _dot_nt(dhb, w_ref[j * kc:(j + 1) * kc, :]).astype(out_dtype)

    return _run(
        body, [dh, w], hook, grid=(t // tm,), name=name, semantics=("parallel",),
        in_specs=[pl.BlockSpec((tm, D), lambda i: (i, 0)), _resident(w.shape)],
        out_specs=pl.BlockSpec((nk, tm, kc), lambda i: (0, i, 0)), out_shape=S((nk, t, kc), out_dtype))


def _dx_rows_t(dh, w, name, hook=None):
    t = dh.shape[0]
    k = w.shape[0]
    tm = _tm(t)

    def body(d_ref, w_ref, o_ref):
        o_ref[...] = _dot_nt(w_ref[...], d_ref[...].astype(BF16)).astype(BF16)

    return _run(
        body, [dh, w], hook, grid=(t // tm,), name=name, semantics=("parallel",),
        in_specs=[pl.BlockSpec((tm, D), lambda i: (i, 0)), _resident(w.shape)],
        out_specs=pl.BlockSpec((k, tm), lambda i: (0, i)), out_shape=S((k, t), BF16))


def _ffn_bwd1(dh, c, wdown, name, hook=None):
    ns, t, n = c.shape
    nh = ns // 2
    tm = min(FFN_ROWS, t)
    ni = t // tm

    def body(d_ref, c_ref, wd_ref, dc_ref, dw_hbm, dwb_hbm, acc, stage):
        i = pl.program_id(0)

        @pl.when(i == 0)
        def _():
            acc[...] = jnp.zeros_like(acc)

        dhb = d_ref[...].astype(BF16)
        for j in range(nh):
            dact = _dot_nt(dhb, wd_ref[j * n:(j + 1) * n, :])
            cg = c_ref[j].astype(F32)
            cv = c_ref[nh + j].astype(F32)
            sg = _sigmoid(cg)
            gs = cg * sg
            acc[j * n:(j + 1) * n, :] += _dot_tn((gs * cv).astype(BF16), dhb)
            dc_ref[j] = (dact * cv * (sg + gs * (1.0 - sg))).astype(BF16)
            dc_ref[nh + j] = (dact * gs).astype(BF16)

        @pl.when(i == ni - 1)
        def _():
            pltpu.sync_copy(acc, dw_hbm)
            for j in range(nh):
                stage[...] = acc[j * n:(j + 1) * n, :].astype(BF16)
                pltpu.sync_copy(stage, dwb_hbm.at[pl.ds(j * n, n), :])

    slab = pl.BlockSpec((ns, tm, n), lambda i: (0, i, 0))
    return _run(
        body, [dh, c, wdown], hook, grid=(ni,), name=name, semantics=("arbitrary",),
        in_specs=[pl.BlockSpec((tm, D), lambda i: (i, 0)), slab, _resident(wdown.shape)],
        out_specs=[slab, ANY, ANY], out_shape=[S((ns, t, n), BF16), S(wdown.shape, F32), S(wdown.shape, BF16)],
        scratch_shapes=[pltpu.VMEM(wdown.shape, F32), pltpu.VMEM((n, D), BF16)])


def _ffn_bwd2(dc, a, wup, cw, h, gain, dh_in, name, hook=None):
    ns, t, n = dc.shape
    tm = min(FFN_ROWS, t)
    ni = t // tm

    def body(dc_ref, a_ref, wu_ref, cw_ref, h_ref, g_ref, di_ref, da_ref, o_ref, dg_ref, dcw_ref, dcb_ref, carry, keep):
        i = pl.program_id(0)

        @pl.when(i == 0)
        def _():
            carry[...] = jnp.zeros_like(carry)
            dg_ref[...] = jnp.zeros_like(dg_ref)
            dcw_ref[...] = jnp.zeros_like(dcw_ref)
            dcb_ref[...] = jnp.zeros_like(dcb_ref)

        rsum = lambda v: jnp.sum(v, axis=0, keepdims=True)
        acc = jnp.zeros((tm, D), F32)
        for s in range(ns):
            x = dc_ref[s].astype(F32)
            ext = jnp.concatenate([x, carry[s]], axis=0)
            keep[0] = ext[1:1 + tm]
            keep[1] = ext[2:2 + tm]
            x1, x2 = keep[0], keep[1]
            cwv = cw_ref[s]
            da = (cwv[2:3] * x + cwv[1:2] * x1 + cwv[0:1] * x2).astype(BF16)
            carry[s] = x[:HALO]
            da_ref[s] = da
            acc = acc + _dot_nt(da, wu_ref[s])
            av = a_ref[s].astype(F32)
            dcw_ref[s] += jnp.concatenate([rsum(x2 * av), rsum(x1 * av), rsum(x * av)], axis=0)
            dcb_ref[s] += rsum(x)
        hv = h_ref[...]
        r = _rstd(hv)
        gg = acc * g_ref[...]
        dh_new = di_ref[...].astype(F32) + r * gg - hv * (r * r * r * jnp.mean(gg * hv, axis=-1, keepdims=True))
        o_ref[...] = dh_new.astype(o_ref.dtype)
        dg_ref[...] += jnp.sum(acc * hv * r, axis=0, keepdims=True)

    slab = pl.BlockSpec((ns, tm, n), lambda i: (0, ni - 1 - i, 0))
    row = pl.BlockSpec((tm, D), lambda i: (ni - 1 - i, 0))
    vec = pl.BlockSpec((1, D), lambda i: (0, 0))
    whole = lambda shape: pl.BlockSpec(shape, lambda i: (0,) * len(shape))
    return _run(
        body, [dc, a, wup, cw, h, gain, dh_in], hook, grid=(ni,), name=name, semantics=("arbitrary",),
        in_specs=[slab, slab, _resident(wup.shape), _resident(cw.shape), row, vec, row],
        out_specs=[slab, row, vec, whole((ns, 3, n)), whole((ns, 1, n))],
        out_shape=[S((ns, t, n), BF16), S((t, D), DH), S((1, D), F32), S((ns, 3, n), F32), S((ns, 1, n), F32)],
        scratch_shapes=[pltpu.VMEM((ns, HALO, n), F32), pltpu.VMEM((2, tm, n), F32)])


def _dw_slot(hn, dy_s, name, hook=None):
    t, k = hn.shape
    ns, _, n = dy_s.shape
    tm = _tm(t)

    def body(a_ref, b_ref, o_ref, ob_ref, at_ref):
        @pl.when(pl.program_id(0) == 0)
        def _():
            for i in range(t // tm):
                at_ref[:, i * tm:(i + 1) * tm] = a_ref[i * tm:(i + 1) * tm, :].T

        acc = _dot(at_ref[...], b_ref[...])
        o_ref[...] = acc
        ob_ref[...] = acc.astype(BF16)

    ospec = pl.BlockSpec((None, k, n), lambda j: (j, 0, 0))
    return _run(
        body, [hn, dy_s], hook, grid=(ns,), name=name, semantics=("arbitrary",),
        in_specs=[_resident(hn.shape), pl.BlockSpec((None, t, n), lambda j: (j, 0, 0))],
        out_specs=[ospec, ospec], out_shape=[S((ns, k, n), F32), S((ns, k, n), BF16)],
        scratch_shapes=[pltpu.VMEM((k, t), BF16)])


def _dw_rows(a_s, dh, name, hook=None, fm=False):
    nk, t, kc = (1, a_s.shape[1], a_s.shape[0]) if fm else a_s.shape
    tm = _tm(t)
    ni = t // tm

    def body(a_ref, d_ref, o_ref, ob_ref):
        i = pl.program_id(0)
        dhb = d_ref[...].astype(BF16)

        @pl.when(i == 0)
        def _():
            o_ref[...] = jnp.zeros_like(o_ref)

        if fm:
            o_ref[...] += _dot(a_ref[...], dhb)
        for j in range(0 if fm else nk):
            o_ref[j * kc:(j + 1) * kc, :] += _dot_tn(a_ref[j], dhb)

        @pl.when(i == ni - 1)
        def _():
            ob_ref[...] = o_ref[...].astype(BF16)

    ospec = pl.BlockSpec((nk * kc, D), lambda i: (0, 0))
    return _run(
        body, [a_s, dh], hook, grid=(ni,), name=name, semantics=("arbitrary",),
        in_specs=[pl.BlockSpec((kc, tm), lambda i: (0, i)) if fm else pl.BlockSpec((nk, tm, kc), lambda i: (0, i, 0)),
                  pl.BlockSpec((tm, D), lambda i: (i, 0))],
        out_specs=[ospec, ospec], out_shape=[S((nk * kc, D), F32), S((nk * kc, D), BF16)])


def _dx_slot_normbwd(dy_s, wg, h, gain, dh_in, name, hook=None, fm=False, out_dtype=F32):
    ns, t, n = (1, dy_s.shape[1], dy_s.shape[0]) if fm else dy_s.shape
    tm = _tm(t)

    def body(dy_ref, w_ref, h_ref, g_ref, di_ref, o_ref, dg_ref):
        i = pl.program_id(0)

        @pl.when(i == 0)
        def _():
            dg_ref[...] = jnp.zeros_like(dg_ref)

        g = _dot_tn(dy_ref[...], w_ref[...]) if fm else _dot_nt(dy_ref[0], w_ref[0])
        for s in range(1, ns):
            g = g + _dot_nt(dy_ref[s], w_ref[s])
        hv = h_ref[...]
        r = _rstd(hv)
        gg = g * g_ref[...]
        dh_new = di_ref[...].astype(F32) + r * gg - hv * (r * r * r * jnp.mean(gg * hv, axis=-1, keepdims=True))
        o_ref[...] = dh_new.astype(o_ref.dtype)
        dg_ref[...] += jnp.sum(g * hv * r, axis=0, keepdims=True)

    row = pl.BlockSpec((tm, D), lambda i: (i, 0))
    vec = pl.BlockSpec((1, D), lambda i: (0, 0))
    return _run(
        body, [dy_s, wg, h, gain, dh_in], hook, grid=(t // tm,), name=name, semantics=("arbitrary",),
        in_specs=[pl.BlockSpec((n, tm), lambda i: (0, i)) if fm else pl.BlockSpec((ns, tm, n), lambda i: (0, i, 0)),
                  _resident(wg.shape), row, vec, row],
        out_specs=[row, vec], out_shape=[S((t, D), out_dtype), S((1, D), F32)])


def _sgu_gate_bwd(a_s, dg_s, vgain, ws, bst, name, hook=None):
    t = a_s.shape[1]
    sw = a_s.shape[2]
    gps = sw // CHUNK

    def body(a_ref, dg_ref, vg_ref, ws_ref, b_ref, da_ref, dws_ref, dbt_ref, dvg_ref, dvn_ref):
        n = pl.program_id(0)

        @pl.when(n == 0)
        def _():
            dws_ref[...] = jnp.zeros_like(dws_ref)
            dbt_ref[...] = jnp.zeros_like(dbt_ref)
            dvg_ref[...] = jnp.zeros_like(dvg_ref)

        vpre = jnp.concatenate([a_ref[4 + s].astype(F32) for s in range(4)], axis=1)
        v = _gelu(vpre)
        r = _rstd(v)
        vhat = v * r
        vn = (vhat * vg_ref[...]).astype(BF16)
        tri = _tril_mask()
        lane = lax.broadcasted_iota(jnp.int32, (CHUNK, CHUNK), 1)
        dbt = jnp.zeros((CHUNK, CHUNK), F32)
        for g in range(SGU_G):
            w = jnp.where(tri, ws_ref[g], 0.0).astype(BF16)
            vng = vn[:, g * CHUNK:(g + 1) * CHUNK]
            sg = _dot(w, vng) + b_ref[:, g:g + 1]
            lo = (g % gps) * CHUNK
            upre = a_ref[g // gps, :, lo:lo + CHUNK].astype(F32)
            dgate = dg_ref[g // gps, :, lo:lo + CHUNK].astype(F32)
            da_ref[g // gps, :, lo:lo + CHUNK] = (dgate * sg * _gelu_grad(upre)).astype(BF16)
            ds = dgate * _gelu(upre)
            dsb = ds.astype(BF16)
            dvn_ref[:, g * CHUNK:(g + 1) * CHUNK] = _dot_tn(w, dsb)
            dws_ref[g] += jnp.where(tri, _dot_nt(dsb, vng), 0.0)
            dbt = dbt + jnp.where(lane == g, jnp.sum(ds, axis=-1, keepdims=True), 0.0)
        dbt_ref[...] += dbt
        dvn = dvn_ref[...]
        dvg_ref[...] += jnp.sum(dvn * vhat, axis=0, keepdims=True)
        gg = dvn * vg_ref[...]
        dv = r * gg - v * (r * r * r * jnp.mean(gg * v, axis=-1, keepdims=True))
        dav = (dv * _gelu_grad(vpre)).astype(BF16)
        for s in range(4):
            da_ref[4 + s] = dav[:, s * sw:(s + 1) * sw]

    return _run(
        body, [a_s, dg_s, vgain, ws, bst], hook, grid=(t // CHUNK,), name=name, semantics=("arbitrary",),
        in_specs=[pl.BlockSpec((8, CHUNK, sw), lambda n: (0, n, 0)), pl.BlockSpec((4, CHUNK, sw), lambda n: (0, n, 0)),
                  pl.BlockSpec((1, SGU_W), lambda n: (0, 0)), pl.BlockSpec((SGU_G, CHUNK, CHUNK), lambda n: (0, 0, 0)),
                  pl.BlockSpec((CHUNK, SGU_G), lambda n: (0, 0))],
        out_specs=[pl.BlockSpec((8, CHUNK, sw), lambda n: (0, n, 0)), pl.BlockSpec((SGU_G, CHUNK, CHUNK), lambda n: (0, 0, 0)),
                   pl.BlockSpec((CHUNK, CHUNK), lambda n: (0, 0)), pl.BlockSpec((1, SGU_W), lambda n: (0, 0))],
        out_shape=[S((8, t, sw), BF16), S((SGU_G, CHUNK, CHUNK), F32), S((CHUNK, CHUNK), F32), S((1, SGU_W), F32)],
        scratch_shapes=[pltpu.VMEM((CHUNK, SGU_W), F32)])


def _attn_bwd(qkv_t, do_t, qg, kg, sinks, bias, name, hook=None):
    t = qkv_t.shape[1]
    nb = t // CHUNK

    def body(cur_ref, prev_ref, do_ref, qg_ref, kg_ref, sink_ref, bias_ref,
             o_ref, dqg_out, dkg_out, dsk_out, dbias_ref, carry, dqg_ref, dkg_ref, dsk_ref):
        n = pl.program_id(0)

        @pl.when(n == 0)
        def _():
            carry[...] = jnp.zeros_like(carry)
            dqg_ref[...] = jnp.zeros_like(dqg_ref)
            dkg_ref[...] = jnp.zeros_like(dkg_ref)
            dsk_ref[...] = jnp.zeros_like(dsk_ref)
            dbias_ref[...] = jnp.zeros_like(dbias_ref)

        @pl.when(n < nb)
        def _():
            valid = _attn_valid(n)
            o_ref[0:KV0, :] = carry[0:KV0, :].astype(BF16)
            for h in range(NKV):
                krow = KV0 + HD * h
                vrow = KV0 + HD * (NKV + h)
                k = _attn_band(cur_ref, prev_ref, krow)
                v = _attn_band(cur_ref, prev_ref, vrow)
                rk = _rstd_rows(k)
                khat = k * rk
                kn = (khat * kg_ref[...]).astype(BF16)
                kn_tok = kn.T
                vb = v.astype(BF16)
                v_tok = vb.T
                heads = range(KVG * h, KVG * (h + 1))
                qs = [cur_ref[HD * hq:HD * (hq + 1), :] for hq in heads]
                rqs = [_rstd_rows(q) for q in qs]
                qhats = [q * rq for q, rq in zip(qs, rqs)]
                qns = [(qhat * qg_ref[...]).astype(BF16) for qhat in qhats]
                probs = [_attn_probs(kn_tok, qn, bias_ref[hq], valid, sink_ref[hq]) for qn, hq in zip(qns, heads)]
                dohs = [do_ref[HD * hq:HD * (hq + 1), :] for hq in heads]
                dps = [_dot(v_tok, doh) for doh in dohs]
                dsums = [jnp.sum(p * dp, axis=0, keepdims=True) for (p, _), dp in zip(probs, dps)]
                dss = [p * (dp - dsum) for (p, _), dp, dsum in zip(probs, dps, dsums)]
                for hq, (_, psink), dsum, ds in zip(heads, probs, dsums, dss):
                    dsk_ref[hq:hq + 1, :] -= psink * dsum
                    dbias_ref[hq] += ds
                dv = sum(_dot_nt(doh, p.astype(BF16)) for doh, (p, _) in zip(dohs, probs))
                dscs = [(ds * (HD ** -0.5)).astype(BF16) for ds in dss]
                dqns = [_dot(kn, dsc) for dsc in dscs]
                dkn = sum(_dot_nt(qn, dsc) for qn, dsc in zip(qns, dscs))
                dqg_ref[...] += sum(dqn * qhat for dqn, qhat in zip(dqns, qhats))
                for hq, q, rq, dqn in zip(heads, qs, rqs, dqns):
                    gq = dqn * qg_ref[...]
                    carry[HD * hq:HD * (hq + 1), :] = rq * gq - q * (rq * rq * rq * jnp.mean(gq * q, axis=0, keepdims=True))
                dkg_ref[...] += dkn * khat
                gk = dkn * kg_ref[...]
                dk = rk * gk - k * (rk * rk * rk * jnp.mean(gk * k, axis=0, keepdims=True))
                o_ref[krow:krow + HD, :] = (carry[krow:krow + HD, :] + dk[:, :CHUNK]).astype(BF16)
                o_ref[vrow:vrow + HD, :] = (carry[vrow:vrow + HD, :] + dv[:, :CHUNK]).astype(BF16)
                carry[krow:krow + HD, :] = dk[:, CHUNK:]
                carry[vrow:vrow + HD, :] = dv[:, CHUNK:]

        @pl.when(n == nb)
        def _():
            o_ref[...] = carry[...].astype(BF16)
            dqg_out[...] = jnp.sum(dqg_ref[...], axis=1, keepdims=True)
            dkg_out[...] = jnp.sum(dkg_ref[...], axis=1, keepdims=True)
            dsk_out[...] = jnp.sum(dsk_ref[...], axis=1, keepdims=True)

    cur = lambda n: (0, jnp.minimum(n, nb - 1))
    col = pl.BlockSpec((HD, 1), lambda n: (0, 0))
    whole = lambda shape: pl.BlockSpec(shape, lambda n: (0,) * len(shape))
    return _run(
        body, [qkv_t, qkv_t, do_t, qg, kg, sinks, bias], hook, grid=(nb + 1,), name=name, semantics=("arbitrary",),
        in_specs=[pl.BlockSpec((QKV, CHUNK), cur),
                  pl.BlockSpec((QKV - KV0, CHUNK), lambda n: (KV0 // (QKV - KV0), jnp.clip(n - 1, 0, nb - 1))),
                  pl.BlockSpec((D, CHUNK), cur), col, col, pl.BlockSpec(memory_space=pltpu.SMEM), whole((NH, 2 * CHUNK, CHUNK))],
        out_specs=[pl.BlockSpec((QKV, CHUNK), lambda n: (0, jnp.maximum(n - 1, 0))), whole((HD, 1)), whole((HD, 1)),
                   whole((NH, 1)), whole((NH, 2 * CHUNK, CHUNK))],
        out_shape=[S((QKV, t), BF16), S((HD, 1), F32), S((HD, 1), F32), S((NH, 1), F32), S((NH, 2 * CHUNK, CHUNK), F32)],
        scratch_shapes=[pltpu.VMEM((QKV, CHUNK), F32), pltpu.VMEM((HD, CHUNK), F32), pltpu.VMEM((HD, 2 * CHUNK), F32),
                        pltpu.VMEM((NH, CHUNK), F32)])


class _Plain:
    def __init__(self, wg):
        self.full, self.grads = wg, {}

    def w(self, n):
        return self.full[n]

    def hook(self, host):
        return None

    def grad(self, n, pair):
        self.grads[n] = pair

    def small(self, g_rep):
        pass


def _local_step(x, target, rep, sch):
    bucket_row = jnp.asarray(_rel_tables().T.reshape(1, -1))
    bias = _relbias_fwd(rep["rel_bias"].T, bucket_row, "relbias_fwd").reshape(NH, 2 * CHUNK, CHUNK)
    bst = rep["sgu_b_s"][0].T
    ws = rep["sgu_w_s"][0]
    vgain = rep["sgu_v_gain"]
    qg, kg, sinks = rep["attn_q_gain"].reshape(HD, 1), rep["attn_k_gain"].reshape(HD, 1), rep["attn_sinks"][0]
    w_down = lambda l: sch.w("ffn_w_down%d" % l).reshape(D_FF, D)
    w_up = lambda l: sch.w("ffn_w_up%d" % l)
    cw = [sch.w("ffn_conv_w")[:, 3 * l:3 * l + 3] for l in range(2)]
    cb = [rep["ffn_conv_b"][l].reshape(8, 1, -1) for l in range(2)]
    mixg = [rep["mix_norm"][l:l + 1] for l in range(2)]
    ffng = [rep["ffn_norm"][l:l + 1] for l in range(2)]
    rows = lambda pair: tuple(g.reshape(N_DEV, -1, D) for g in pair)
    hk = sch.hook

    hn0 = _rmsnorm(x, mixg[0], "norm0")
    a0 = _mm_slot(hn0, sch.w("sgu_w_in"), BF16, "sgu_in", hk("sgu_in"))
    gated = _sgu_gate_fwd(a0, vgain, ws, bst, "sgu_gate", hk("sgu_gate"))
    h1, hn1 = _resid_mm(gated, sch.w("sgu_w_out").reshape(SGU_W, D), x, ffng[0], "norm", "sgu_out", hk("sgu_out"))
    a_ff0, c_ff0, h2, hn2 = _ffn_fwd(hn1, h1, w_up(0), w_down(0), cw[0], cb[0], mixg[1], "norm", "ffn0_fwd", hk("ffn0_fwd"))
    qkv = _mm_t(hn2, sch.w("attn_w_qkv"), "qkv", hk("qkv"))
    o = _attn_fwd(qkv, qg, kg, sinks, bias, "attn", hk("attn"))
    h3, hn3 = _resid_mm(o, sch.w("attn_w_o").reshape(D, D), h2, ffng[1], "norm", "attn_out", hk("attn_out"), fm=True)
    a_ff1, c_ff1, dy, sq = _ffn_fwd(hn3, h3, w_up(1), w_down(1), cw[1], cb[1], target, "loss", "ffn1_fwd_loss", hk("ffn1_fwd_loss"))
    loss = (0.5 / D) * jnp.sum(sq[:, 0, 0])

    def ffn_bwd(dh, h_in, hn, a, c, l, tag):
        dc, g_down, g_down_b = _ffn_bwd1(dh, c, w_down(l), tag + "_bwd1", hk(tag + "_bwd1"))
        sch.grad("ffn_w_down%d" % l, rows((g_down, g_down_b)))
        da, dh_new, dgain, g_cw, g_cb = _ffn_bwd2(dc, a, w_up(l), cw[l], h_in, ffng[l], dh, tag + "_bwd2", hk(tag + "_bwd2"))
        sch.grad("ffn_w_up%d" % l, _dw_slot(hn, da, tag + "_dw_up", hk(tag + "_dw_up")))
        return dh_new, dgain, g_cw, g_cb.reshape(-1)

    dh, d_ffng1, g_cw1, g_cb1 = ffn_bwd(dy, h3, hn3, a_ff1, c_ff1, 1, "ffn1")
    do = _dx_rows_t(dh, sch.w("attn_w_o").reshape(D, D), "attn_do", hk("attn_do"))
    sch.grad("attn_w_o", rows(_dw_rows(o, dh, "dw_o", hk("dw_o"), fm=True)))
    dqkv, d_qg, d_kg, d_sk, d_bias = _attn_bwd(qkv, do, qg, kg, sinks, bias, "attn_bwd", hk("attn_bwd"))
    sch.grad("attn_w_qkv", tuple(g.reshape(N_DEV, -1, D) for g in _dw_rows(dqkv, hn2, "dw_qkv", hk("dw_qkv"), fm=True)))
    dh, d_mixg1 = _dx_slot_normbwd(dqkv, sch.w("attn_w_qkv").reshape(QKV, D), h2, mixg[1], dh, "dx_qkv", hk("dx_qkv"), fm=True,
                                   out_dtype=DH)
    d_relb = _relbias_bwd(d_bias.reshape(NH, -1), bucket_row, "relbias_bwd").T
    g_rep = {"attn_q_gain": d_qg.reshape(1, HD), "attn_k_gain": d_kg.reshape(1, HD), "attn_sinks": d_sk.reshape(1, NH),
             "rel_bias": d_relb}
    sch.small(g_rep)
    dh, d_ffng0, g_cw0, g_cb0 = ffn_bwd(dh, h1, hn1, a_ff0, c_ff0, 0, "ffn0")
    g_cw = jnp.concatenate([g_cw0, g_cw1], axis=1)
    sch.grad("ffn_conv_w", (g_cw, g_cw.astype(BF16)))
    g_ffn = {"ffn_norm": jnp.concatenate([d_ffng0, d_ffng1], axis=0), "ffn_conv_b": jnp.stack([g_cb0, g_cb1], axis=0)}
    sch.small(g_ffn)
    dgated = _dx_rows(dh, sch.w("sgu_w_out").reshape(SGU_W, D), SGU_W // 4, BF16, "sgu_dgated", hk("sgu_dgated"))
    sch.grad("sgu_w_out", rows(_dw_rows(gated, dh, "dw_sgu_out", hk("dw_sgu_out"))))
    da0, d_ws, d_bst, d_vgain = _sgu_gate_bwd(a0, dgated, vgain, ws, bst, "sgu_gate_bwd", hk("sgu_gate_bwd"))
    grad_x, d_mixg0 = _dx_slot_normbwd(da0, sch.w("sgu_w_in"), x, mixg[0], dh, "dx_sgu_in")
    g_sgu = {"sgu_v_gain": d_vgain, "sgu_w_s": d_ws[None], "sgu_b_s": d_bst[:, :SGU_G].T[None],
             "mix_norm": jnp.concatenate([d_mixg0, d_mixg1], axis=0)}
    sch.small(g_sgu)
    g_rep.update(g_ffn)
    g_rep.update(g_sgu)
    sch.grad("sgu_w_in", _dw_slot(hn0, da0, "dw_sgu_in", hk("dw_sgu_in")))
    return loss, grad_x, g_rep


def _allgather(xs, name):
    nt = len(xs)

    def body(*refs):
        x_refs, o_refs = refs[:nt], refs[nt:2 * nt]
        send_sems, recv_sems, local_sems = refs[2 * nt:]
        x, y, c, chips = _place()
        me, sibling = (x, y, c), (x, y, 1 - c)

        def copy(t, k, block, to, src=None):
            px, py, pc = block
            dst = o_refs[t].at[4 * px + 2 * py + pc]
            return pltpu.make_async_remote_copy(
                src_ref=dst if src is None else src, dst_ref=dst, send_sem=send_sems.at[t, k], recv_sem=recv_sems.at[t, k],
                device_id=to, device_id_type=MESH)

        mine = [pltpu.make_async_copy(x_refs[t], o_refs[t].at[4 * x + 2 * y + c], local_sems.at[t]) for t in range(nt)]
        for cp in mine:
            cp.start()
        first = []
        for t in range(nt):
            first.append(copy(t, 0, me, sibling, src=x_refs[t]))
            first += [copy(t, 1 + j, me, (*chip, c), src=x_refs[t]) for j, chip in enumerate(chips)]
        for cp in first:
            cp.start()
        passed = []
        for j, chip in enumerate(chips):
            for t in range(nt):
                copy(t, 1 + j, (*chip, c), me).wait_recv()
                fwd = copy(t, 4 + j, (*chip, c), sibling)
                fwd.start()
                passed.append(fwd)
        for t in range(nt):
            copy(t, 0, sibling, me).wait_recv()
            for j, chip in enumerate(chips):
                copy(t, 4 + j, (*chip, 1 - c), me).wait_recv()
        for cp in first + passed:
            cp.wait_send()
        for cp in mine:
            cp.wait()

    return pl.pallas_call(
        body, name=name, in_specs=[ANY] * nt, out_specs=[ANY] * nt,
        out_shape=[S((N_DEV,) + a.shape, a.dtype) for a in xs],
        scratch_shapes=[pltpu.SemaphoreType.DMA((nt, 7)), pltpu.SemaphoreType.DMA((nt, 7)), pltpu.SemaphoreType.DMA((nt,))],
        compiler_params=pltpu.CompilerParams(has_side_effects=True))(*xs)


def _exchange(hook, name):
    comm = hook()
    ci, co = len(comm.inputs), len(comm.out_shapes)

    def body(*refs):
        cins, couts = refs[:ci], refs[ci:ci + co]
        send, recv = refs[-2:]
        comm.start(cins, couts, send, recv)
        comm.finish(cins, couts, send, recv)

    res = pl.pallas_call(
        body, name=name, in_specs=[ANY] * ci, out_specs=[ANY] * co, out_shape=comm.out_shapes,
        scratch_shapes=[pltpu.SemaphoreType.DMA((comm.n_sems,)), pltpu.SemaphoreType.DMA((comm.n_sems,))],
        input_output_aliases=dict(comm.aliases),
        compiler_params=pltpu.CompilerParams(has_side_effects=True))(*comm.inputs)
    hook(res)


def _row_tile(r):
    tr = r if r <= ROW_TILE or r % ROW_TILE else ROW_TILE
    assert r % tr == 0
    return tr


def _rs_partial(g32, sib, place, name):
    _, r, cdim = g32.shape
    tr = _row_tile(r)

    def body(place_ref, g_ref, s_ref, p_ref, own_ref):
        k = pl.program_id(1)
        tot = g_ref[...] + s_ref[...].astype(F32)
        p_ref[...] = tot.astype(BF16)

        @pl.when(k == place_ref[1])
        def _():
            own_ref[...] = tot

    grid_spec = pltpu.PrefetchScalarGridSpec(
        num_scalar_prefetch=1, grid=(r // tr, 4),
        in_specs=[pl.BlockSpec((None, None, tr, cdim), lambda i, k, pr: (k, pr[0], i, 0)),
                  pl.BlockSpec((None, tr, cdim), lambda i, k, pr: (k, i, 0))],
        out_specs=[pl.BlockSpec((None, tr, cdim), lambda i, k, pr: (k, i, 0)), pl.BlockSpec((tr, cdim), lambda i, k, pr: (i, 0))])
    return pl.pallas_call(
        body, grid_spec=grid_spec, name=name,
        out_shape=[S((4, r, cdim), BF16), S((r, cdim), F32)],
        compiler_params=_cp("parallel", "arbitrary"))(place, g32.reshape(4, 2, r, cdim), sib)


def _adamw_math(w, g, m, v):
    m = ADAM_B1 * m + (1.0 - ADAM_B1) * g
    v = ADAM_B2 * v + (1.0 - ADAM_B2) * (g * g)
    m_hat = m / (1.0 - ADAM_B1 ** ADAM_STEP)
    v_hat = v / (1.0 - ADAM_B2 ** ADAM_STEP)
    delta = -ADAM_LR * (m_hat / (jnp.sqrt(v_hat) + ADAM_EPS) + ADAM_WD * w)
    return delta, m, v


def _adamw_shard(owns, recvs, w, m, v, name, flipped=False):
    nl = w.shape[0]
    r, cdim = owns[0].shape
    tr = _row_tile(r)
    nr = r // tr

    def body(*refs):
        own_refs, recv_refs = refs[:nl], refs[nl:2 * nl]
        w_ref, m_ref, v_ref, g_out, d_out, m_out, v_out = refs[2 * nl:]
        layer = pl.program_id(0)
        g = None
        for l in range(nl):
            gl = own_refs[l][...] + recv_refs[l][0].astype(F32) + recv_refs[l][1].astype(F32) + recv_refs[l][2].astype(F32)
            g = gl if g is None else jnp.where(layer == l, gl, g)
        if flipped:
            g = g.T
        g_out[...] = g
        d_out[...], m_out[...], v_out[...] = _adamw_math(w_ref[...], g, m_ref[...], v_ref[...])

    park = lambda l: (lambda layer, i: (jnp.where(layer == l, i, jnp.where(layer < l, 0, nr - 1)), 0))
    park3 = lambda l: (lambda layer, i: (0, jnp.where(layer == l, i, jnp.where(layer < l, 0, nr - 1)), 0))
    if flipped:
        row = pl.BlockSpec((None, cdim, tr), lambda layer, i: (layer, 0, i))
    else:
        row = pl.BlockSpec((None, tr, cdim), lambda layer, i: (layer, i, 0))
    return pl.pallas_call(
        body, grid=(nl, nr), name=name,
        in_specs=[pl.BlockSpec((tr, cdim), park(l)) for l in range(nl)] + [pl.BlockSpec((3, tr, cdim), park3(l)) for l in range(nl)]
        + [row, row, row],
        out_specs=[row] * 4, out_shape=[S(w.shape, F32)] * 4,
        compiler_params=_cp("arbitrary", "arbitrary"))(*owns, *recvs, w, m, v)


def _adamw_small(galls, ws, ms, vs, name):
    n = len(galls)

    def body(*refs):
        g_refs, w_refs, m_refs, v_refs, outs = refs[:n], refs[n:2 * n], refs[2 * n:3 * n], refs[3 * n:4 * n], refs[4 * n:]
        for i in range(n):
            g = g_refs[i][0].astype(F32)
            for s in range(1, N_DEV):
                g = g + g_refs[i][s].astype(F32)
            outs[i][...] = g
            outs[n + i][...], outs[2 * n + i][...], outs[3 * n + i][...] = _adamw_math(w_refs[i][...], g, m_refs[i][...], v_refs[i][...])

    res = pl.pallas_call(body, out_shape=[S(a.shape, F32) for a in ws] * 4, name=name)(*galls, *ws, *ms, *vs)
    return [res[k * n:(k + 1) * n] for k in range(4)]


REPLICATED = ["mix_norm", "ffn_norm", "sgu_v_gain", "sgu_w_s", "sgu_b_s", "attn_q_gain", "attn_k_gain", "attn_sinks", "rel_bias",
              "ffn_conv_b"]
WEIGHTS = ["mix_norm", "ffn_norm", "sgu_w_in", "sgu_v_gain", "sgu_w_s", "sgu_b_s", "sgu_w_out", "attn_w_qkv", "attn_q_gain",
           "attn_k_gain", "attn_sinks", "attn_w_o", "rel_bias", "ffn_w_up", "ffn_conv_w", "ffn_conv_b", "ffn_w_down"]
SMALL = ["g_" + n for n in REPLICATED]
BF16_TRANSIT = {"sgu_w_s"}
SMALL_ATTN = ["g_attn_q_gain", "g_attn_k_gain", "g_attn_sinks", "g_rel_bias"]
SMALL_FFN = ["g_ffn_norm", "g_ffn_conv_b"]
SMALL_LATE = [n for n in SMALL if n not in SMALL_ATTN + SMALL_FFN]

GATHER_FIRST = ["sgu_w_in", "ffn_conv_w"]
PLAN = {
    "sgu_in": [("ag1", "sgu_w_out"), ("ag1", "ffn_w_down0")],
    "sgu_gate": [("ag2", "sgu_w_out"), ("ag2", "ffn_w_down0"), ("ag1", "ffn_w_up0")],
    "sgu_out": [("ag2", "ffn_w_up0"), ("ag1", "attn_w_qkv")],
    "ffn0_fwd": [("ag2", "attn_w_qkv"), ("ag1", "attn_w_o"), ("ag1", "ffn_w_up1")],
    "qkv": [("ag2", "attn_w_o"), ("ag2", "ffn_w_up1")],
    "attn": [("ag1", "ffn_w_down1")],
    "attn_out": [("ag2", "ffn_w_down1")],
    "ffn1_bwd2": [("rs1", "ffn_w_down1")],
    "ffn1_dw_up": [("rs2", "ffn_w_down1")],
    "attn_do": [("rs1", "ffn_w_up1")],
    "attn_bwd": [("rs2", "ffn_w_up1"), ("rs1", "attn_w_o")],
    "dw_qkv": [("rs2", "attn_w_o")],
    "dx_qkv": [("rs1", "attn_w_qkv")],
    "ffn0_bwd1": [("rs2", "attn_w_qkv")] + [("ag1", n) for n in SMALL_ATTN],
    "ffn0_bwd2": [("rs1", "ffn_w_down0")] + [("ag2", n) for n in SMALL_ATTN],
    "ffn0_dw_up": [("rs2", "ffn_w_down0")],
    "sgu_dgated": [("rs1", "ffn_w_up0")] + [("ag1", n) for n in SMALL_FFN],
    "dw_sgu_out": [("ag2", n) for n in SMALL_FFN],
    "sgu_gate_bwd": [("rs2", "ffn_w_up0"), ("rs1", "sgu_w_out")],
    "dw_sgu_in": [("rs2", "sgu_w_out")] + [("ag1", n) for n in SMALL_LATE],
    "last_a": [("rs1", "sgu_w_in"), ("rs1", "ffn_conv_w")] + [("ag2", n) for n in SMALL_LATE],
    "last_b": [("rs2", "sgu_w_in"), ("rs2", "ffn_conv_w")],
}


class _Overlap:
    def __init__(self, shard, place):
        self.shard, self.place = shard, place
        self.part, self.full = {}, {}
        self.grads, self.sib, self.own, self.recv = {}, {}, {}, {}

    def w(self, n):
        return self.full[n]

    def grad(self, n, pair):
        self.grads[n] = pair

    def small(self, g_rep):
        self.shard.update(("g_" + n, a.astype(BF16) if n in BF16_TRANSIT else a) for n, a in _views2d(g_rep).items())

    def chip_sums(self, n):
        sums, self.own[n] = _rs_partial(self.grads[n][0], self.sib.pop(n), self.place, "rs_partial_" + n)
        return sums

    def hook(self, host):
        ops = PLAN.get(host)
        if not ops:
            return None
        where = {"ag1": self.part, "ag2": self.full, "rs1": self.sib, "rs2": self.recv}
        idx = []

        def hook(results=None):
            if results is not None:
                for (kind, n), i in zip(ops, idx):
                    where[kind][n] = results[i]
                return None
            comm = _Comm()
            for kind, n in ops:
                arr = {"ag1": lambda: self.shard[n], "ag2": lambda: self.part.pop(n), "rs1": lambda: self.grads[n][1],
                       "rs2": lambda: self.chip_sums(n)}[kind]()
                idx.append(comm.add(kind, arr))
            return comm

        return hook


TRANSPOSED = {"attn_w_qkv"}
PHYSICAL_T = {"ffn_w_up"}
SHARDED = {
    "sgu_w_in": ["sgu_w_in"], "sgu_w_out": ["sgu_w_out"], "attn_w_qkv": ["attn_w_qkv"], "attn_w_o": ["attn_w_o"],
    "ffn_w_up": ["ffn_w_up0", "ffn_w_up1"], "ffn_w_down": ["ffn_w_down0", "ffn_w_down1"], "ffn_conv_w": ["ffn_conv_w"],
}


def _send_views(w):
    out = {"ffn_conv_w": w["ffn_conv_w"].reshape(6, -1)}
    for name, parts in SHARDED.items():
        if name != "ffn_conv_w":
            out.update((p, (w[name][l].T if name in TRANSPOSED else w[name][l]).astype(BF16)) for l, p in enumerate(parts))
    return out


def _views2d(d):
    return {n: d[n].reshape(-1, d[n].shape[-1]) for n in REPLICATED if n in d}


def kernel(x, mix_norm, ffn_norm, sgu_w_in, sgu_v_gain, sgu_w_s, sgu_b_s, sgu_w_out, attn_w_qkv, attn_q_gain, attn_k_gain, attn_sinks, attn_w_o, rel_bias, ffn_w_up, ffn_conv_w, ffn_conv_b, ffn_w_down, loss_target, m_mix_norm, m_ffn_norm, m_sgu_w_in, m_sgu_v_gain, m_sgu_w_s, m_sgu_b_s, m_sgu_w_out, m_attn_w_qkv, m_attn_q_gain, m_attn_k_gain, m_attn_sinks, m_attn_w_o, m_rel_bias, m_ffn_w_up, m_ffn_conv_w, m_ffn_conv_b, m_ffn_w_down, v_mix_norm, v_ffn_norm, v_sgu_w_in, v_sgu_v_gain, v_sgu_w_s, v_sgu_b_s, v_sgu_w_out, v_attn_w_qkv, v_attn_q_gain, v_attn_k_gain, v_attn_sinks, v_attn_w_o, v_rel_bias, v_ffn_w_up, v_ffn_conv_w, v_ffn_conv_b, v_ffn_w_down):
    w = dict(zip(WEIGHTS, (mix_norm, ffn_norm, sgu_w_in, sgu_v_gain, sgu_w_s, sgu_b_s, sgu_w_out, attn_w_qkv, attn_q_gain, attn_k_gain,
                           attn_sinks, attn_w_o, rel_bias, ffn_w_up, ffn_conv_w, ffn_conv_b, ffn_w_down)))
    m = dict(zip(WEIGHTS, (m_mix_norm, m_ffn_norm, m_sgu_w_in, m_sgu_v_gain, m_sgu_w_s, m_sgu_b_s, m_sgu_w_out, m_attn_w_qkv, m_attn_q_gain,
                           m_attn_k_gain, m_attn_sinks, m_attn_w_o, m_rel_bias, m_ffn_w_up, m_ffn_conv_w, m_ffn_conv_b, m_ffn_w_down)))
    v = dict(zip(WEIGHTS, (v_mix_norm, v_ffn_norm, v_sgu_w_in, v_sgu_v_gain, v_sgu_w_s, v_sgu_b_s, v_sgu_w_out, v_attn_w_qkv, v_attn_q_gain,
                           v_attn_k_gain, v_attn_sinks, v_attn_w_o, v_rel_bias, v_ffn_w_up, v_ffn_conv_w, v_ffn_conv_b, v_ffn_w_down)))
    rep = {n: w[n] for n in REPLICATED}

    xi, yi, ci = lax.axis_index("x"), lax.axis_index("y"), lax.axis_index("c")
    place = jnp.stack([ci, 2 * xi + yi]).astype(jnp.int32)
    sch = _Overlap(_send_views(w), place)
    sch.full.update(zip(GATHER_FIRST, _allgather([sch.shard[n] for n in GATHER_FIRST], "gather_first")))

    loss, grad_x, g_rep = _local_step(x[0], loss_target[0], rep, sch)
    loss = lax.psum(loss, ("x", "y", "c"))
    _exchange(sch.hook("last_a"), "last_a")
    _exchange(sch.hook("last_b"), "last_b")

    out = [{}, {}, {}, {}]
    for name, parts in SHARDED.items():
        flip = (lambda a: jnp.swapaxes(a, -1, -2)) if name in TRANSPOSED | PHYSICAL_T else (lambda a: a)
        shape = flip(w[name]).shape
        as3d = lambda a: flip(a).reshape(len(parts), -1, shape[-1])
        res = _adamw_shard([sch.own[p] for p in parts], [sch.recv[p] for p in parts], as3d(w[name]), as3d(m[name]), as3d(v[name]),
                           "adamw_" + name, flipped=name in PHYSICAL_T)
        for o, r in zip(out, res):
            o[name] = flip(r.reshape(shape))
    small = _adamw_small([sch.full[n] for n in SMALL], *[list(_views2d(d).values()) for d in (rep, m, v)], "adamw_small")
    for o, res in zip(out, small):
        o.update((n, r.reshape(w[n].shape)) for n, r in zip(REPLICATED, res))

    return (loss, grad_x[None], *[out[0][n] for n in WEIGHTS], *[out[1][n] for n in WEIGHTS],
            *[out[2][n] for n in WEIGHTS], *[out[3][n] for n in WEIGHTS])
```

```python
import functools
import math

import numpy as np
import jax
import jax.numpy as jnp
from jax import lax
from jax.experimental import pallas as pl
from jax.experimental.pallas import tpu as pltpu

F32 = jnp.float32
BF16 = jnp.bfloat16
DH = jnp.bfloat16
S = jax.ShapeDtypeStruct

D = 1024
CHUNK = 128
SGU_W = 2048
SGU_G = 16
HD = 64
NH = 16
NKV = 4
KVG = 4
D_FF = 2816
REL_BUCKETS = 32
REL_MAX_DIST = 128
EPS = 1e-6
N_DEV = 8
MESH = pl.DeviceIdType.MESH

ADAM_LR = 0.001
ADAM_B1 = 0.9
ADAM_B2 = 0.999
ADAM_EPS = 1e-08
ADAM_WD = 0.01
ADAM_STEP = 10

ROW_TILE = 512
HALO = 8
FFN_ROWS = 256


def _tm(t):
    return min(ROW_TILE, t)


def _cp(*sem):
    return pltpu.CompilerParams(dimension_semantics=sem)


ANY = pl.BlockSpec(memory_space=pl.ANY)


def _place():
    x, y, c = lax.axis_index("x"), lax.axis_index("y"), lax.axis_index("c")
    return x, y, c, [(1 - x, y), (x, 1 - y), (1 - x, 1 - y)]


class _Comm:
    SEMS = {"ag1": 5, "ag2": 3, "rs1": 4, "rs2": 3}

    def __init__(self):
        self.inputs, self.out_shapes, self.aliases, self.ops, self.n_sems = [], [], {}, [], 0

    def add(self, kind, arr):
        lead = {"ag1": N_DEV, "ag2": None, "rs1": 4, "rs2": 3}[kind]
        shape = arr.shape if lead is None else (lead,) + arr.shape[(0 if kind == "ag1" else 1):]
        if kind == "ag2":
            self.aliases[len(self.inputs)] = len(self.out_shapes)
        self.ops.append((kind, len(self.inputs), len(self.out_shapes), self.n_sems))
        self.inputs.append(arr)
        self.out_shapes.append(S(shape, arr.dtype))
        self.n_sems += self.SEMS[kind]
        return len(self.out_shapes) - 1

    def _copies(self, ins, outs, send, recv):
        x, y, c, chips = _place()
        me, sibling = (x, y, c), (x, y, 1 - c)
        slot = lambda px, py, pc: 4 * px + 2 * py + pc
        sends, recvs, local = [], [], []

        def rc(src, dst, k, to):
            return lambda: pltpu.make_async_remote_copy(src_ref=src(), dst_ref=dst(), send_sem=send.at[k], recv_sem=recv.at[k],
                                                        device_id=to, device_id_type=MESH)

        for kind, ii, oi, b in self.ops:
            src, dst = ins[ii], outs[oi]
            at = lambda ref, i: (lambda: ref.at[i])
            if kind == "ag1":
                whole, mine = (lambda s=src: s), at(dst, slot(*me))
                sends.append(rc(whole, mine, b, sibling))
                recvs.append(rc(whole, at(dst, slot(x, y, 1 - c)), b, me))
                for j, chip in enumerate(chips):
                    sends.append(rc(whole, mine, b + 1 + j, (*chip, c)))
                    recvs.append(rc(whole, at(dst, slot(*chip, c)), b + 1 + j, me))
                local.append(lambda s=src, m=mine, k=b + 4: pltpu.make_async_copy(s, m(), send.at[k]))
            elif kind == "ag2":
                for j, chip in enumerate(chips):
                    sends.append(rc(at(dst, slot(*chip, c)), at(dst, slot(*chip, c)), b + j, sibling))
                    recvs.append(rc(at(dst, slot(*chip, 1 - c)), at(dst, slot(*chip, 1 - c)), b + j, me))
            elif kind == "rs1":
                for k in range(4):
                    sends.append(rc(at(src, 2 * k + (1 - c)), at(dst, k), b + k, sibling))
                    recvs.append(rc(at(src, 2 * k + c), at(dst, k), b + k, me))
            else:
                for j, (px, py) in enumerate(chips):
                    sends.append(rc(at(src, 2 * px + py), at(dst, j), b + j, (px, py, c)))
                    recvs.append(rc(at(src, 2 * px + py), at(dst, j), b + j, me))
        return sends, recvs, local

    def start(self, ins, outs, send, recv):
        sends, _, local = self._copies(ins, outs, send, recv)
        for make in local + sends:
            make().start()

    def finish(self, ins, outs, send, recv):
        sends, recvs, local = self._copies(ins, outs, send, recv)
        for make in recvs:
            make().wait_recv()
        for make in sends:
            make().wait_send()
        for make in local:
            make().wait()


def _run(body, args, hook, *, grid, in_specs, out_specs, out_shape, name, semantics, scratch_shapes=(), aliases=None):
    comm = hook() if hook is not None else None
    aliases = dict(aliases or {})
    if comm is None:
        return pl.pallas_call(body, grid=grid, in_specs=in_specs, out_specs=out_specs, out_shape=out_shape, name=name,
                              scratch_shapes=list(scratch_shapes), input_output_aliases=aliases,
                              compiler_params=_cp(*semantics))(*args)
    single = not isinstance(out_shape, (list, tuple))
    out_shapes = [out_shape] if single else list(out_shape)
    out_specs_l = [out_specs] if single else list(out_specs)
    n_in, n_out, n_scr, ci, co = len(args), len(out_shapes), len(scratch_shapes), len(comm.inputs), len(comm.out_shapes)

    def wrapped(*refs):
        ins, cins = refs[:n_in], refs[n_in:n_in + ci]
        outs, couts = refs[n_in + ci:n_in + ci + n_out], refs[n_in + ci + n_out:n_in + ci + n_out + co]
        scr = refs[n_in + ci + n_out + co:n_in + ci + n_out + co + n_scr]
        send, recv = refs[-2:]
        first = functools.reduce(lambda a, b: a & b, [pl.program_id(a) == 0 for a in range(len(grid))])
        last = functools.reduce(lambda a, b: a & b, [pl.program_id(a) == g - 1 for a, g in enumerate(grid)])

        @pl.when(first)
        def _():
            comm.start(cins, couts, send, recv)

        body(*ins, *outs, *scr)

        @pl.when(last)
        def _():
            comm.finish(cins, couts, send, recv)

    res = pl.pallas_call(
        wrapped, grid=grid, in_specs=list(in_specs) + [ANY] * ci, out_specs=out_specs_l + [ANY] * co,
        out_shape=out_shapes + comm.out_shapes, name=name,
        scratch_shapes=list(scratch_shapes) + [pltpu.SemaphoreType.DMA((comm.n_sems,)), pltpu.SemaphoreType.DMA((comm.n_sems,))],
        input_output_aliases={**aliases, **{n_in + k: n_out + v for k, v in comm.aliases.items()}},
        compiler_params=pltpu.CompilerParams(dimension_semantics=("arbitrary",) * len(grid), has_side_effects=True))(*args, *comm.inputs)
    hook(res[n_out:])
    return res[0] if single else list(res[:n_out])


def _dot(a, b):
    return jnp.dot(a, b, preferred_element_type=F32)


def _dot_nt(a, b):
    return lax.dot_general(a, b, (((1,), (1,)), ((), ())), preferred_element_type=F32)


def _dot_tn(a, b):
    return lax.dot_general(a, b, (((0,), (0,)), ((), ())), preferred_element_type=F32)


def _gelu(x):
    return 0.5 * x * (1.0 + lax.erf(x * (2.0 ** -0.5)))


def _gelu_grad(x):
    return 0.5 * (1.0 + lax.erf(x * (2.0 ** -0.5))) + x * jnp.exp(-0.5 * x * x) * (1.0 / math.sqrt(2.0 * math.pi))


def _sigmoid(x):
    return 1.0 / (1.0 + jnp.exp(-x))


def _rstd(x):
    return lax.rsqrt(jnp.mean(x * x, axis=-1, keepdims=True) + EPS)


def _rel_tables():
    q = np.arange(CHUNK)[:, None] + CHUNK
    k = np.arange(2 * CHUNK)[None, :]
    dist = q - k
    n = np.maximum(dist, 0)
    max_exact = REL_BUCKETS // 2
    large = max_exact + (np.log(np.maximum(n, 1).astype(np.float32) / max_exact)
                         / math.log(REL_MAX_DIST / max_exact) * (REL_BUCKETS - max_exact)).astype(np.int32)
    large = np.minimum(large, REL_BUCKETS - 1)
    return np.where(n < max_exact, n, large).astype(np.int32)


def _rmsnorm(x, gain, name):
    t = x.shape[0]
    tm = _tm(t)

    def body(x_ref, g_ref, o_ref):
        xv = x_ref[...]
        o_ref[...] = (xv * _rstd(xv) * g_ref[...]).astype(BF16)

    return pl.pallas_call(
        body, grid=(t // tm,), name=name,
        in_specs=[pl.BlockSpec((tm, D), lambda i: (i, 0)), pl.BlockSpec((1, D), lambda i: (0, 0))],
        out_specs=pl.BlockSpec((tm, D), lambda i: (i, 0)),
        out_shape=S((t, D), BF16), compiler_params=_cp("parallel"))(x, gain)


def _resident(shape):
    zeros = (0,) * len(shape)
    return pl.BlockSpec(shape, lambda *_: zeros, pipeline_mode=pl.Buffered(1))


def _mm_slot(hn, wg, out_dtype, name, hook=None):
    t, k = hn.shape
    ns, _, n = wg.shape
    tm = _tm(t)

    def body(a_ref, w_ref, o_ref):
        a = a_ref[...]
        for s in range(ns):
            o_ref[s] = _dot(a, w_ref[s]).astype(out_dtype)

    return _run(
        body, [hn, wg], hook, grid=(t // tm,), name=name, semantics=("parallel",),
        in_specs=[pl.BlockSpec((tm, k), lambda i: (i, 0)), _resident(wg.shape)],
        out_specs=pl.BlockSpec((ns, tm, n), lambda i: (0, i, 0)), out_shape=S((ns, t, n), out_dtype))


def _mm_t(hn, wt, name, hook=None):
    t, k = hn.shape
    ns, n, _ = wt.shape
    tm = _tm(t)

    def body(a_ref, w_ref, o_ref):
        a = a_ref[...]
        for s in range(ns):
            o_ref[s * n:(s + 1) * n, :] = _dot_nt(w_ref[s], a)

    return _run(
        body, [hn, wt], hook, grid=(t // tm,), name=name, semantics=("parallel",),
        in_specs=[pl.BlockSpec((tm, k), lambda i: (i, 0)), _resident(wt.shape)],
        out_specs=pl.BlockSpec((ns * n, tm), lambda i: (0, i)), out_shape=S((ns * n, t), F32))


def _conv3(a, prev, cw, cb, tm):
    ext = jnp.concatenate([prev, a], axis=0)
    return cw[2:3] * a + cw[1:2] * ext[HALO - 1:HALO - 1 + tm] + cw[0:1] * ext[HALO - 2:HALO - 2 + tm] + cb


def _ffn_fwd(hn, h, wup, wdown, cw, cb, extra, mode, name, hook=None):
    t, k = hn.shape
    n = wup.shape[-1]
    nh = wup.shape[0] // 2
    tm = min(FFN_ROWS, t)
    ni = t // tm

    def body(a_ref, h_ref, wu_ref, wd_ref, cw_ref, cb_ref, e_ref, as_ref, cs_ref, o1_ref, o2_ref, carry):
        i = pl.program_id(0)

        @pl.when(i == 0)
        def _():
            carry[...] = jnp.zeros_like(carry)

        a = a_ref[...]
        acc = h_ref[...]
        nxt = (_dot(a, wu_ref[0]), _dot(a, wu_ref[nh]))
        for j in range(nh):
            ag, av = nxt
            if j + 1 < nh:
                nxt = (_dot(a, wu_ref[j + 1]), _dot(a, wu_ref[nh + j + 1]))
            as_ref[j] = ag.astype(BF16)
            as_ref[nh + j] = av.astype(BF16)
            cg = _conv3(ag, carry[j], cw_ref[j], cb_ref[j], tm)
            cv = _conv3(av, carry[nh + j], cw_ref[nh + j], cb_ref[nh + j], tm)
            carry[j] = ag[tm - HALO:]
            carry[nh + j] = av[tm - HALO:]
            cs_ref[j] = cg.astype(BF16)
            cs_ref[nh + j] = cv.astype(BF16)
            act = (cg * _sigmoid(cg) * cv).astype(BF16)
            acc = acc + _dot(act, wd_ref[j * n:(j + 1) * n, :])
        if mode == "norm":
            o1_ref[...] = acc
            o2_ref[...] = (acc * _rstd(acc) * e_ref[...]).astype(BF16)
        else:
            err = acc - e_ref[...]
            o1_ref[...] = (err * (1.0 / D)).astype(o1_ref.dtype)
            o2_ref[...] = jnp.full(o2_ref.shape, jnp.sum(err * err), F32)

    row = pl.BlockSpec((tm, D), lambda i: (i, 0))
    if mode == "norm":
        e_spec, o2_spec, o2_shape = pl.BlockSpec((1, D), lambda i: (0, 0)), row, S((t, D), BF16)
    else:
        e_spec, o2_spec, o2_shape = row, pl.BlockSpec((None, 8, 128), lambda i: (i, 0, 0)), S((ni, 8, 128), F32)
    aspec = pl.BlockSpec((2 * nh, tm, n), lambda i: (0, i, 0))
    return _run(
        body, [hn, h, wup, wdown, cw, cb, extra], hook, grid=(ni,), name=name, semantics=("arbitrary",),
        in_specs=[pl.BlockSpec((tm, k), lambda i: (i, 0)), row, _resident(wup.shape), _resident(wdown.shape),
                  _resident(cw.shape), _resident(cb.shape), e_spec],
        out_specs=[aspec, aspec, row, o2_spec],
        out_shape=[S((2 * nh, t, n), BF16), S((2 * nh, t, n), BF16), S((t, D), F32 if mode == "norm" else DH), o2_shape],
        scratch_shapes=[pltpu.VMEM((2 * nh, HALO, n), F32)])


def _tril_mask():
    r = lax.broadcasted_iota(jnp.int32, (CHUNK, CHUNK), 0)
    c = lax.broadcasted_iota(jnp.int32, (CHUNK, CHUNK), 1)
    return r >= c


def _sgu_gate_fwd(a_s, vgain, ws, bst, name, hook=None):
    t = a_s.shape[1]
    sw = a_s.shape[2]
    gps = sw // CHUNK

    def body(a_ref, vg_ref, ws_ref, b_ref, o_ref):
        v = _gelu(jnp.concatenate([a_ref[4 + s].astype(F32) for s in range(4)], axis=1))
        vn = (v * _rstd(v) * vg_ref[...]).astype(BF16)
        tri = _tril_mask()
        for g in range(SGU_G):
            w = jnp.where(tri, ws_ref[g], 0.0).astype(BF16)
            sg = _dot(w, vn[:, g * CHUNK:(g + 1) * CHUNK]) + b_ref[:, g:g + 1]
            lo = (g % gps) * CHUNK
            u = _gelu(a_ref[g // gps, :, lo:lo + CHUNK].astype(F32))
            o_ref[g // gps, :, lo:lo + CHUNK] = (u * sg).astype(BF16)

    return _run(
        body, [a_s, vgain, ws, bst], hook, grid=(t // CHUNK,), name=name, semantics=("parallel",),
        in_specs=[pl.BlockSpec((8, CHUNK, sw), lambda n: (0, n, 0)), pl.BlockSpec((1, SGU_W), lambda n: (0, 0)),
                  pl.BlockSpec((SGU_G, CHUNK, CHUNK), lambda n: (0, 0, 0)), pl.BlockSpec((CHUNK, SGU_G), lambda n: (0, 0))],
        out_specs=pl.BlockSpec((4, CHUNK, sw), lambda n: (0, n, 0)), out_shape=S((4, t, sw), BF16))


def _resid_mm(a_s, w, resid, extra, mode, name, hook=None, fm=False):
    nk, t, kc = (1, a_s.shape[1], a_s.shape[0]) if fm else a_s.shape
    tm = _tm(t)
    ni = t // tm

    def body(a_ref, w_ref, r_ref, e_ref, o1_ref, o2_ref):
        h = r_ref[...]
        if fm:
            h = h + _dot_tn(a_ref[...], w_ref[...])
        for j in range(0 if fm else nk):
            h = h + _dot(a_ref[j], w_ref[j * kc:(j + 1) * kc, :])
        if mode == "norm":
            o1_ref[...] = h
            o2_ref[...] = (h * _rstd(h) * e_ref[...]).astype(BF16)
        else:
            err = h - e_ref[...]
            o1_ref[...] = (err * (1.0 / D)).astype(o1_ref.dtype)
            o2_ref[...] = jnp.full(o2_ref.shape, jnp.sum(err * err), F32)

    row = pl.BlockSpec((tm, D), lambda i: (i, 0))
    if mode == "norm":
        e_spec, o2_spec, o2_shape = pl.BlockSpec((1, D), lambda i: (0, 0)), row, S((t, D), BF16)
    else:
        e_spec, o2_spec, o2_shape = row, pl.BlockSpec((None, 8, 128), lambda i: (i, 0, 0)), S((ni, 8, 128), F32)
    return _run(
        body, [a_s, w, resid, extra], hook, grid=(ni,), name=name, semantics=("parallel",),
        in_specs=[pl.BlockSpec((kc, tm), lambda i: (0, i)) if fm else pl.BlockSpec((nk, tm, kc), lambda i: (0, i, 0)),
                  _resident(w.shape), row, e_spec],
        out_specs=[row, o2_spec], out_shape=[S((t, D), F32 if mode == "norm" else DH), o2_shape])


def _relbias_fwd(rel_bias_t, bucket_row, name):
    nb = bucket_row.shape[1]

    def body(rb_ref, bk_ref, o_ref):
        onehot = (lax.broadcasted_iota(jnp.int32, (REL_BUCKETS, nb), 0) == bk_ref[...]).astype(F32)
        o_ref[...] = jnp.dot(rb_ref[...], onehot, precision=lax.Precision.HIGHEST, preferred_element_type=F32)

    return pl.pallas_call(body, out_shape=S((NH, nb), F32), name=name)(rel_bias_t, bucket_row)


def _relbias_bwd(dbias, bucket_row, name):
    nb = bucket_row.shape[1]

    def body(db_ref, bk_ref, o_ref):
        onehot = (lax.broadcasted_iota(jnp.int32, (REL_BUCKETS, nb), 0) == bk_ref[...]).astype(F32)
        o_ref[...] = lax.dot_general(db_ref[...], onehot, (((1,), (1,)), ((), ())),
                                     precision=lax.Precision.HIGHEST, preferred_element_type=F32)

    return pl.pallas_call(body, out_shape=S((NH, REL_BUCKETS), F32), name=name)(dbias, bucket_row)


QKV = D + 2 * NKV * HD
KV0 = D


def _rstd_rows(x):
    return lax.rsqrt(jnp.mean(x * x, axis=0, keepdims=True) + EPS)


def _attn_valid(n):
    kj = lax.broadcasted_iota(jnp.int32, (2 * CHUNK, CHUNK), 0)
    qi = lax.broadcasted_iota(jnp.int32, (2 * CHUNK, CHUNK), 1)
    dist = qi + CHUNK - kj
    return (dist >= 0) & (dist < CHUNK) & ((n > 0) | (kj >= CHUNK))


def _attn_band(cur_ref, prev_ref, row):
    return jnp.concatenate([prev_ref[row - KV0:row - KV0 + HD, :], cur_ref[row:row + HD, :]], axis=1)


def _attn_probs(kn_tok, qn, bias, valid, sink):
    s = _dot(kn_tok, qn) * (HD ** -0.5) + bias
    s = jnp.where(valid, s, -jnp.inf)
    m = jnp.maximum(jnp.max(s, axis=0, keepdims=True), sink)
    p = jnp.exp(s - m)
    psink = jnp.exp(sink - m)
    inv = 1.0 / (jnp.sum(p, axis=0, keepdims=True) + psink)
    return p * inv, psink * inv


def _attn_fwd(qkv_t, qg, kg, sinks, bias, name, hook=None):
    t = qkv_t.shape[1]

    def body(cur_ref, prev_ref, qg_ref, kg_ref, sink_ref, bias_ref, o_ref):
        n = pl.program_id(0)
        valid = _attn_valid(n)
        ks = [_attn_band(cur_ref, prev_ref, KV0 + HD * h) for h in range(NKV)]
        kn_toks = [(k * _rstd_rows(k) * kg_ref[...]).astype(BF16).T for k in ks]
        vbs = [_attn_band(cur_ref, prev_ref, KV0 + HD * (NKV + h)).astype(BF16) for h in range(NKV)]
        qs = [cur_ref[HD * hq:HD * (hq + 1), :] for hq in range(NH)]
        qns = [(q * _rstd_rows(q) * qg_ref[...]).astype(BF16) for q in qs]
        ps = [_attn_probs(kn_toks[hq // KVG], qns[hq], bias_ref[hq], valid, sink_ref[hq])[0] for hq in range(NH)]
        for hq in range(NH):
            o_ref[HD * hq:HD * (hq + 1), :] = _dot(vbs[hq // KVG], ps[hq].astype(BF16)).astype(BF16)

    col = pl.BlockSpec((HD, 1), lambda n: (0, 0))
    return _run(
        body, [qkv_t, qkv_t, qg, kg, sinks, bias], hook, grid=(t // CHUNK,), name=name, semantics=("parallel",),
        in_specs=[pl.BlockSpec((QKV, CHUNK), lambda n: (0, n)),
                  pl.BlockSpec((QKV - KV0, CHUNK), lambda n: (KV0 // (QKV - KV0), jnp.maximum(n - 1, 0))),
                  col, col, pl.BlockSpec(memory_space=pltpu.SMEM), pl.BlockSpec((NH, 2 * CHUNK, CHUNK), lambda n: (0, 0, 0))],
        out_specs=pl.BlockSpec((D, CHUNK), lambda n: (0, n)), out_shape=S((D, t), BF16))


def _dx_rows(dh, w, kc, out_dtype, name, hook=None):
    t = dh.shape[0]
    nk = w.shape[0] // kc
    tm = _tm(t)

    def body(d_ref, w_ref, o_ref):
        dhb = d_ref[...].astype(BF16)
        for j in range(nk):
            o_ref[j] = _dot_nt(dhb, w_ref[j * kc:(j + 1) * kc, :]).astype(out_dtype)

    return _run(
        body, [dh, w], hook, grid=(t // tm,), name=name, semantics=("parallel",),
        in_specs=[pl.BlockSpec((tm, D), lambda i: (i, 0)), _resident(w.shape)],
        out_specs=pl.BlockSpec((nk, tm, kc), lambda i: (0, i, 0)), out_shape=S((nk, t, kc), out_dtype))


def _dx_rows_t(dh, w, name, hook=None):
    t = dh.shape[0]
    k = w.shape[0]
    tm = _tm(t)

    def body(d_ref, w_ref, o_ref):
        o_ref[...] = _dot_nt(w_ref[...], d_ref[...].astype(BF16)).astype(BF16)

    return _run(
        body, [dh, w], hook, grid=(t // tm,), name=name, semantics=("parallel",),
        in_specs=[pl.BlockSpec((tm, D), lambda i: (i, 0)), _resident(w.shape)],
        out_specs=pl.BlockSpec((k, tm), lambda i: (0, i)), out_shape=S((k, t), BF16))


def _ffn_bwd1(dh, c, wdown, name, hook=None):
    ns, t, n = c.shape
    nh = ns // 2
    tm = min(FFN_ROWS, t)
    ni = t // tm

    def body(d_ref, c_ref, wd_ref, dc_ref, dw_hbm, dwb_hbm, acc, stage):
        i = pl.program_id(0)

        @pl.when(i == 0)
        def _():
            acc[...] = jnp.zeros_like(acc)

        dhb = d_ref[...].astype(BF16)
        for j in range(nh):
            dact = _dot_nt(dhb, wd_ref[j * n:(j + 1) * n, :])
            cg = c_ref[j].astype(F32)
            cv = c_ref[nh + j].astype(F32)
            sg = _sigmoid(cg)
            gs = cg * sg
            acc[j * n:(j + 1) * n, :] += _dot_tn((gs * cv).astype(BF16), dhb)
            dc_ref[j] = (dact * cv * (sg + gs * (1.0 - sg))).astype(BF16)
            dc_ref[nh + j] = (dact * gs).astype(BF16)

        @pl.when(i == ni - 1)
        def _():
            pltpu.sync_copy(acc, dw_hbm)
            for j in range(nh):
                stage[...] = acc[j * n:(j + 1) * n, :].astype(BF16)
                pltpu.sync_copy(stage, dwb_hbm.at[pl.ds(j * n, n), :])

    slab = pl.BlockSpec((ns, tm, n), lambda i: (0, i, 0))
    return _run(
        body, [dh, c, wdown], hook, grid=(ni,), name=name, semantics=("arbitrary",),
        in_specs=[pl.BlockSpec((tm, D), lambda i: (i, 0)), slab, _resident(wdown.shape)],
        out_specs=[slab, ANY, ANY], out_shape=[S((ns, t, n), BF16), S(wdown.shape, F32), S(wdown.shape, BF16)],
        scratch_shapes=[pltpu.VMEM(wdown.shape, F32), pltpu.VMEM((n, D), BF16)])


def _ffn_bwd2(dc, a, wup, cw, h, gain, dh_in, name, hook=None):
    ns, t, n = dc.shape
    tm = min(FFN_ROWS, t)
    ni = t // tm

    def body(dc_ref, a_ref, wu_ref, cw_ref, h_ref, g_ref, di_ref, da_ref, o_ref, dg_ref, dcw_ref, dcb_ref, carry, keep):
        i = pl.program_id(0)

        @pl.when(i == 0)
        def _():
            carry[...] = jnp.zeros_like(carry)
            dg_ref[...] = jnp.zeros_like(dg_ref)
            dcw_ref[...] = jnp.zeros_like(dcw_ref)
            dcb_ref[...] = jnp.zeros_like(dcb_ref)

        rsum = lambda v: jnp.sum(v, axis=0, keepdims=True)
        acc = jnp.zeros((tm, D), F32)
        for s in range(ns):
            x = dc_ref[s].astype(F32)
            ext = jnp.concatenate([x, carry[s]], axis=0)
            keep[0] = ext[1:1 + tm]
            keep[1] = ext[2:2 + tm]
            x1, x2 = keep[0], keep[1]
            cwv = cw_ref[s]
            da = (cwv[2:3] * x + cwv[1:2] * x1 + cwv[0:1] * x2).astype(BF16)
            carry[s] = x[:HALO]
            da_ref[s] = da
            acc = acc + _dot_nt(da, wu_ref[s])
            av = a_ref[s].astype(F32)
            dcw_ref[s] += jnp.concatenate([rsum(x2 * av), rsum(x1 * av), rsum(x * av)], axis=0)
            dcb_ref[s] += rsum(x)
        hv = h_ref[...]
        r = _rstd(hv)
        gg = acc * g_ref[...]
        dh_new = di_ref[...].astype(F32) + r * gg - hv * (r * r * r * jnp.mean(gg * hv, axis=-1, keepdims=True))
        o_ref[...] = dh_new.astype(o_ref.dtype)
        dg_ref[...] += jnp.sum(acc * hv * r, axis=0, keepdims=True)

    slab = pl.BlockSpec((ns, tm, n), lambda i: (0, ni - 1 - i, 0))
    row = pl.BlockSpec((tm, D), lambda i: (ni - 1 - i, 0))
    vec = pl.BlockSpec((1, D), lambda i: (0, 0))
    whole = lambda shape: pl.BlockSpec(shape, lambda i: (0,) * len(shape))
    return _run(
        body, [dc, a, wup, cw, h, gain, dh_in], hook, grid=(ni,), name=name, semantics=("arbitrary",),
        in_specs=[slab, slab, _resident(wup.shape), _resident(cw.shape), row, vec, row],
        out_specs=[slab, row, vec, whole((ns, 3, n)), whole((ns, 1, n))],
        out_shape=[S((ns, t, n), BF16), S((t, D), DH), S((1, D), F32), S((ns, 3, n), F32), S((ns, 1, n), F32)],
        scratch_shapes=[pltpu.VMEM((ns, HALO, n), F32), pltpu.VMEM((2, tm, n), F32)])


def _dw_slot(hn, dy_s, name, hook=None):
    t, k = hn.shape
    ns, _, n = dy_s.shape
    tm = _tm(t)

    def body(a_ref, b_ref, o_ref, ob_ref, at_ref):
        @pl.when(pl.program_id(0) == 0)
        def _():
            for i in range(t // tm):
                at_ref[:, i * tm:(i + 1) * tm] = a_ref[i * tm:(i + 1) * tm, :].T

        acc = _dot(at_ref[...], b_ref[...])
        o_ref[...] = acc
        ob_ref[...] = acc.astype(BF16)

    ospec = pl.BlockSpec((None, k, n), lambda j: (j, 0, 0))
    return _run(
        body, [hn, dy_s], hook, grid=(ns,), name=name, semantics=("arbitrary",),
        in_specs=[_resident(hn.shape), pl.BlockSpec((None, t, n), lambda j: (j, 0, 0))],
        out_specs=[ospec, ospec], out_shape=[S((ns, k, n), F32), S((ns, k, n), BF16)],
        scratch_shapes=[pltpu.VMEM((k, t), BF16)])


def _dw_rows(a_s, dh, name, hook=None, fm=False):
    nk, t, kc = (1, a_s.shape[1], a_s.shape[0]) if fm else a_s.shape
    tm = _tm(t)
    ni = t // tm

    def body(a_ref, d_ref, o_ref, ob_ref):
        i = pl.program_id(0)
        dhb = d_ref[...].astype(BF16)

        @pl.when(i == 0)
        def _():
            o_ref[...] = jnp.zeros_like(o_ref)

        if fm:
            o_ref[...] += _dot(a_ref[...], dhb)
        for j in range(0 if fm else nk):
            o_ref[j * kc:(j + 1) * kc, :] += _dot_tn(a_ref[j], dhb)

        @pl.when(i == ni - 1)
        def _():
            ob_ref[...] = o_ref[...].astype(BF16)

    ospec = pl.BlockSpec((nk * kc, D), lambda i: (0, 0))
    return _run(
        body, [a_s, dh], hook, grid=(ni,), name=name, semantics=("arbitrary",),
        in_specs=[pl.BlockSpec((kc, tm), lambda i: (0, i)) if fm else pl.BlockSpec((nk, tm, kc), lambda i: (0, i, 0)),
                  pl.BlockSpec((tm, D), lambda i: (i, 0))],
        out_specs=[ospec, ospec], out_shape=[S((nk * kc, D), F32), S((nk * kc, D), BF16)])


def _dx_slot_normbwd(dy_s, wg, h, gain, dh_in, name, hook=None, fm=False, out_dtype=F32):
    ns, t, n = (1, dy_s.shape[1], dy_s.shape[0]) if fm else dy_s.shape
    tm = _tm(t)

    def body(dy_ref, w_ref, h_ref, g_ref, di_ref, o_ref, dg_ref):
        i = pl.program_id(0)

        @pl.when(i == 0)
        def _():
            dg_ref[...] = jnp.zeros_like(dg_ref)

        g = _dot_tn(dy_ref[...], w_ref[...]) if fm else _dot_nt(dy_ref[0], w_ref[0])
        for s in range(1, ns):
            g = g + _dot_nt(dy_ref[s], w_ref[s])
        hv = h_ref[...]
        r = _rstd(hv)
        gg = g * g_ref[...]
        dh_new = di_ref[...].astype(F32) + r * gg - hv * (r * r * r * jnp.mean(gg * hv, axis=-1, keepdims=True))
        o_ref[...] = dh_new.astype(o_ref.dtype)
        dg_ref[...] += jnp.sum(g * hv * r, axis=0, keepdims=True)

    row = pl.BlockSpec((tm, D), lambda i: (i, 0))
    vec = pl.BlockSpec((1, D), lambda i: (0, 0))
    return _run(
        body, [dy_s, wg, h, gain, dh_in], hook, grid=(t // tm,), name=name, semantics=("arbitrary",),
        in_specs=[pl.BlockSpec((n, tm), lambda i: (0, i)) if fm else pl.BlockSpec((ns, tm, n), lambda i: (0, i, 0)),
                  _resident(wg.shape), row, vec, row],
        out_specs=[row, vec], out_shape=[S((t, D), out_dtype), S((1, D), F32)])


def _sgu_gate_bwd(a_s, dg_s, vgain, ws, bst, name, hook=None):
    t = a_s.shape[1]
    sw = a_s.shape[2]
    gps = sw // CHUNK

    def body(a_ref, dg_ref, vg_ref, ws_ref, b_ref, da_ref, dws_ref, dbt_ref, dvg_ref, dvn_ref):
        n = pl.program_id(0)

        @pl.when(n == 0)
        def _():
            dws_ref[...] = jnp.zeros_like(dws_ref)
            dbt_ref[...] = jnp.zeros_like(dbt_ref)
            dvg_ref[...] = jnp.zeros_like(dvg_ref)

        vpre = jnp.concatenate([a_ref[4 + s].astype(F32) for s in range(4)], axis=1)
        v = _gelu(vpre)
        r = _rstd(v)
        vhat = v * r
        vn = (vhat * vg_ref[...]).astype(BF16)
        tri = _tril_mask()
        lane = lax.broadcasted_iota(jnp.int32, (CHUNK, CHUNK), 1)
        dbt = jnp.zeros((CHUNK, CHUNK), F32)
        for g in range(SGU_G):
            w = jnp.where(tri, ws_ref[g], 0.0).astype(BF16)
            vng = vn[:, g * CHUNK:(g + 1) * CHUNK]
            sg = _dot(w, vng) + b_ref[:, g:g + 1]
            lo = (g % gps) * CHUNK
            upre = a_ref[g // gps, :, lo:lo + CHUNK].astype(F32)
            dgate = dg_ref[g // gps, :, lo:lo + CHUNK].astype(F32)
            da_ref[g // gps, :, lo:lo + CHUNK] = (dgate * sg * _gelu_grad(upre)).astype(BF16)
            ds = dgate * _gelu(upre)
            dsb = ds.astype(BF16)
            dvn_ref[:, g * CHUNK:(g + 1) * CHUNK] = _dot_tn(w, dsb)
            dws_ref[g] += jnp.where(tri, _dot_nt(dsb, vng), 0.0)
            dbt = dbt + jnp.where(lane == g, jnp.sum(ds, axis=-1, keepdims=True), 0.0)
        dbt_ref[...] += dbt
        dvn = dvn_ref[...]
        dvg_ref[...] += jnp.sum(dvn * vhat, axis=0, keepdims=True)
        gg = dvn * vg_ref[...]
        dv = r * gg - v * (r * r * r * jnp.mean(gg * v, axis=-1, keepdims=True))
        dav = (dv * _gelu_grad(vpre)).astype(BF16)
        for s in range(4):
            da_ref[4 + s] = dav[:, s * sw:(s + 1) * sw]

    return _run(
        body, [a_s, dg_s, vgain, ws, bst], hook, grid=(t // CHUNK,), name=name, semantics=("arbitrary",),
        in_specs=[pl.BlockSpec((8, CHUNK, sw), lambda n: (0, n, 0)), pl.BlockSpec((4, CHUNK, sw), lambda n: (0, n, 0)),
                  pl.BlockSpec((1, SGU_W), lambda n: (0, 0)), pl.BlockSpec((SGU_G, CHUNK, CHUNK), lambda n: (0, 0, 0)),
                  pl.BlockSpec((CHUNK, SGU_G), lambda n: (0, 0))],
        out_specs=[pl.BlockSpec((8, CHUNK, sw), lambda n: (0, n, 0)), pl.BlockSpec((SGU_G, CHUNK, CHUNK), lambda n: (0, 0, 0)),
                   pl.BlockSpec((CHUNK, CHUNK), lambda n: (0, 0)), pl.BlockSpec((1, SGU_W), lambda n: (0, 0))],
        out_shape=[S((8, t, sw), BF16), S((SGU_G, CHUNK, CHUNK), F32), S((CHUNK, CHUNK), F32), S((1, SGU_W), F32)],
        scratch_shapes=[pltpu.VMEM((CHUNK, SGU_W), F32)])


def _attn_bwd(qkv_t, do_t, qg, kg, sinks, bias, name, hook=None):
    t = qkv_t.shape[1]
    nb = t // CHUNK

    def body(cur_ref, prev_ref, do_ref, qg_ref, kg_ref, sink_ref, bias_ref,
             o_ref, dqg_out, dkg_out, dsk_out, dbias_ref, carry, dqg_ref, dkg_ref, dsk_ref):
        n = pl.program_id(0)

        @pl.when(n == 0)
        def _():
            carry[...] = jnp.zeros_like(carry)
            dqg_ref[...] = jnp.zeros_like(dqg_ref)
            dkg_ref[...] = jnp.zeros_like(dkg_ref)
            dsk_ref[...] = jnp.zeros_like(dsk_ref)
            dbias_ref[...] = jnp.zeros_like(dbias_ref)

        @pl.when(n < nb)
        def _():
            valid = _attn_valid(n)
            o_ref[0:KV0, :] = carry[0:KV0, :].astype(BF16)
            kvs, heads = range(NKV), range(NH)
            group = lambda h: range(KVG * h, KVG * (h + 1))
            ks = [_attn_band(cur_ref, prev_ref, KV0 + HD * h) for h in kvs]
            rks = [_rstd_rows(k) for k in ks]
            khats = [k * rk for k, rk in zip(ks, rks)]
            kns = [(khat * kg_ref[...]).astype(BF16) for khat in khats]
            kn_toks = [kn.T for kn in kns]
            vbs = [_attn_band(cur_ref, prev_ref, KV0 + HD * (NKV + h)).astype(BF16) for h in kvs]
            v_toks = [vb.T for vb in vbs]
            qs = [cur_ref[HD * hq:HD * (hq + 1), :] for hq in heads]
            rqs = [_rstd_rows(q) for q in qs]
            qhats = [q * rq for q, rq in zip(qs, rqs)]
            qns = [(qhat * qg_ref[...]).astype(BF16) for qhat in qhats]
            probs = [_attn_probs(kn_toks[hq // KVG], qns[hq], bias_ref[hq], valid, sink_ref[hq]) for hq in heads]
            dohs = [do_ref[HD * hq:HD * (hq + 1), :] for hq in heads]
            dps = [_dot(v_toks[hq // KVG], dohs[hq]) for hq in heads]
            dsums = [jnp.sum(p * dp, axis=0, keepdims=True) for (p, _), dp in zip(probs, dps)]
            dss = [p * (dp - dsum) for (p, _), dp, dsum in zip(probs, dps, dsums)]
            for hq in heads:
                dsk_ref[hq:hq + 1, :] -= probs[hq][1] * dsums[hq]
                dbias_ref[hq] += dss[hq]
            dvs = [sum(_dot_nt(dohs[hq], probs[hq][0].astype(BF16)) for hq in group(h)) for h in kvs]
            dscs = [(ds * (HD ** -0.5)).astype(BF16) for ds in dss]
            dqns = [_dot(kns[hq // KVG], dscs[hq]) for hq in heads]
            dkns = [sum(_dot_nt(qns[hq], dscs[hq]) for hq in group(h)) for h in kvs]
            dqg_ref[...] += sum(dqn * qhat for dqn, qhat in zip(dqns, qhats))
            for hq in heads:
                gq = dqns[hq] * qg_ref[...]
                carry[HD * hq:HD * (hq + 1), :] = rqs[hq] * gq - qs[hq] * (
                    rqs[hq] * rqs[hq] * rqs[hq] * jnp.mean(gq * qs[hq], axis=0, keepdims=True))
            dkg_ref[...] += sum(dkn * khat for dkn, khat in zip(dkns, khats))
            for h in kvs:
                krow, vrow = KV0 + HD * h, KV0 + HD * (NKV + h)
                gk = dkns[h] * kg_ref[...]
                dk = rks[h] * gk - ks[h] * (rks[h] * rks[h] * rks[h] * jnp.mean(gk * ks[h], axis=0, keepdims=True))
                o_ref[krow:krow + HD, :] = (carry[krow:krow + HD, :] + dk[:, :CHUNK]).astype(BF16)
                o_ref[vrow:vrow + HD, :] = (carry[vrow:vrow + HD, :] + dvs[h][:, :CHUNK]).astype(BF16)
                carry[krow:krow + HD, :] = dk[:, CHUNK:]
                carry[vrow:vrow + HD, :] = dvs[h][:, CHUNK:]

        @pl.when(n == nb)
        def _():
            o_ref[...] = carry[...].astype(BF16)
            dqg_out[...] = jnp.sum(dqg_ref[...], axis=1, keepdims=True)
            dkg_out[...] = jnp.sum(dkg_ref[...], axis=1, keepdims=True)
            dsk_out[...] = jnp.sum(dsk_ref[...], axis=1, keepdims=True)

    cur = lambda n: (0, jnp.minimum(n, nb - 1))
    col = pl.BlockSpec((HD, 1), lambda n: (0, 0))
    whole = lambda shape: pl.BlockSpec(shape, lambda n: (0,) * len(shape))
    return _run(
        body, [qkv_t, qkv_t, do_t, qg, kg, sinks, bias], hook, grid=(nb + 1,), name=name, semantics=("arbitrary",),
        in_specs=[pl.BlockSpec((QKV, CHUNK), cur),
                  pl.BlockSpec((QKV - KV0, CHUNK), lambda n: (KV0 // (QKV - KV0), jnp.clip(n - 1, 0, nb - 1))),
                  pl.BlockSpec((D, CHUNK), cur), col, col, pl.BlockSpec(memory_space=pltpu.SMEM), whole((NH, 2 * CHUNK, CHUNK))],
        out_specs=[pl.BlockSpec((QKV, CHUNK), lambda n: (0, jnp.maximum(n - 1, 0))), whole((HD, 1)), whole((HD, 1)),
                   whole((NH, 1)), whole((NH, 2 * CHUNK, CHUNK))],
        out_shape=[S((QKV, t), BF16), S((HD, 1), F32), S((HD, 1), F32), S((NH, 1), F32), S((NH, 2 * CHUNK, CHUNK), F32)],
        scratch_shapes=[pltpu.VMEM((QKV, CHUNK), F32), pltpu.VMEM((HD, CHUNK), F32), pltpu.VMEM((HD, 2 * CHUNK), F32),
                        pltpu.VMEM((NH, CHUNK), F32)])


class _Plain:
    def __init__(self, wg):
        self.full, self.grads = wg, {}

    def w(self, n):
        return self.full[n]

    def hook(self, host):
        return None

    def grad(self, n, pair):
        self.grads[n] = pair

    def small(self, g_rep):
        pass


def _local_step(x, target, rep, sch):
    bucket_row = jnp.asarray(_rel_tables().T.reshape(1, -1))
    bias = _relbias_fwd(rep["rel_bias"].T, bucket_row, "relbias_fwd").reshape(NH, 2 * CHUNK, CHUNK)
    bst = rep["sgu_b_s"][0].T
    ws = rep["sgu_w_s"][0]
    vgain = rep["sgu_v_gain"]
    qg, kg, sinks = rep["attn_q_gain"].reshape(HD, 1), rep["attn_k_gain"].reshape(HD, 1), rep["attn_sinks"][0]
    w_down = lambda l: sch.w("ffn_w_down%d" % l).reshape(D_FF, D)
    w_up = lambda l: sch.w("ffn_w_up%d" % l)
    cw = [sch.w("ffn_conv_w")[:, 3 * l:3 * l + 3] for l in range(2)]
    cb = [rep["ffn_conv_b"][l].reshape(8, 1, -1) for l in range(2)]
    mixg = [rep["mix_norm"][l:l + 1] for l in range(2)]
    ffng = [rep["ffn_norm"][l:l + 1] for l in range(2)]
    rows = lambda pair: tuple(g.reshape(N_DEV, -1, D) for g in pair)
    hk = sch.hook

    hn0 = _rmsnorm(x, mixg[0], "norm0")
    a0 = _mm_slot(hn0, sch.w("sgu_w_in"), BF16, "sgu_in", hk("sgu_in"))
    gated = _sgu_gate_fwd(a0, vgain, ws, bst, "sgu_gate", hk("sgu_gate"))
    h1, hn1 = _resid_mm(gated, sch.w("sgu_w_out").reshape(SGU_W, D), x, ffng[0], "norm", "sgu_out", hk("sgu_out"))
    a_ff0, c_ff0, h2, hn2 = _ffn_fwd(hn1, h1, w_up(0), w_down(0), cw[0], cb[0], mixg[1], "norm", "ffn0_fwd", hk("ffn0_fwd"))
    qkv = _mm_t(hn2, sch.w("attn_w_qkv"), "qkv", hk("qkv"))
    o = _attn_fwd(qkv, qg, kg, sinks, bias, "attn", hk("attn"))
    h3, hn3 = _resid_mm(o, sch.w("attn_w_o").reshape(D, D), h2, ffng[1], "norm", "attn_out", hk("attn_out"), fm=True)
    a_ff1, c_ff1, dy, sq = _ffn_fwd(hn3, h3, w_up(1), w_down(1), cw[1], cb[1], target, "loss", "ffn1_fwd_loss", hk("ffn1_fwd_loss"))
    loss = (0.5 / D) * jnp.sum(sq[:, 0, 0])

    def ffn_bwd(dh, h_in, hn, a, c, l, tag):
        dc, g_down, g_down_b = _ffn_bwd1(dh, c, w_down(l), tag + "_bwd1", hk(tag + "_bwd1"))
        sch.grad("ffn_w_down%d" % l, rows((g_down, g_down_b)))
        da, dh_new, dgain, g_cw, g_cb = _ffn_bwd2(dc, a, w_up(l), cw[l], h_in, ffng[l], dh, tag + "_bwd2", hk(tag + "_bwd2"))
        sch.grad("ffn_w_up%d" % l, _dw_slot(hn, da, tag + "_dw_up", hk(tag + "_dw_up")))
        return dh_new, dgain, g_cw, g_cb.reshape(-1)

    dh, d_ffng1, g_cw1, g_cb1 = ffn_bwd(dy, h3, hn3, a_ff1, c_ff1, 1, "ffn1")
    do = _dx_rows_t(dh, sch.w("attn_w_o").reshape(D, D), "attn_do", hk("attn_do"))
    sch.grad("attn_w_o", rows(_dw_rows(o, dh, "dw_o", hk("dw_o"), fm=True)))
    dqkv, d_qg, d_kg, d_sk, d_bias = _attn_bwd(qkv, do, qg, kg, sinks, bias, "attn_bwd", hk("attn_bwd"))
    sch.grad("attn_w_qkv", tuple(g.reshape(N_DEV, -1, D) for g in _dw_rows(dqkv, hn2, "dw_qkv", hk("dw_qkv"), fm=True)))
    dh, d_mixg1 = _dx_slot_normbwd(dqkv, sch.w("attn_w_qkv").reshape(QKV, D), h2, mixg[1], dh, "dx_qkv", hk("dx_qkv"), fm=True,
                                   out_dtype=DH)
    d_relb = _relbias_bwd(d_bias.reshape(NH, -1), bucket_row, "relbias_bwd").T
    g_rep = {"attn_q_gain": d_qg.reshape(1, HD), "attn_k_gain": d_kg.reshape(1, HD), "attn_sinks": d_sk.reshape(1, NH),
             "rel_bias": d_relb}
    sch.small(g_rep)
    dh, d_ffng0, g_cw0, g_cb0 = ffn_bwd(dh, h1, hn1, a_ff0, c_ff0, 0, "ffn0")
    g_cw = jnp.concatenate([g_cw0, g_cw1], axis=1)
    sch.grad("ffn_conv_w", (g_cw, g_cw.astype(BF16)))
    g_ffn = {"ffn_norm": jnp.concatenate([d_ffng0, d_ffng1], axis=0), "ffn_conv_b": jnp.stack([g_cb0, g_cb1], axis=0)}
    sch.small(g_ffn)
    dgated = _dx_rows(dh, sch.w("sgu_w_out").reshape(SGU_W, D), SGU_W // 4, BF16, "sgu_dgated", hk("sgu_dgated"))
    sch.grad("sgu_w_out", rows(_dw_rows(gated, dh, "dw_sgu_out", hk("dw_sgu_out"))))
    da0, d_ws, d_bst, d_vgain = _sgu_gate_bwd(a0, dgated, vgain, ws, bst, "sgu_gate_bwd", hk("sgu_gate_bwd"))
    grad_x, d_mixg0 = _dx_slot_normbwd(da0, sch.w("sgu_w_in"), x, mixg[0], dh, "dx_sgu_in")
    g_sgu = {"sgu_v_gain": d_vgain, "sgu_w_s": d_ws[None], "sgu_b_s": d_bst[:, :SGU_G].T[None],
             "mix_norm": jnp.concatenate([d_mixg0, d_mixg1], axis=0)}
    sch.small(g_sgu)
    g_rep.update(g_ffn)
    g_rep.update(g_sgu)
    sch.grad("sgu_w_in", _dw_slot(hn0, da0, "dw_sgu_in", hk("dw_sgu_in")))
    return loss, grad_x, g_rep


def _allgather(xs, name):
    nt = len(xs)

    def body(*refs):
        x_refs, o_refs = refs[:nt], refs[nt:2 * nt]
        send_sems, recv_sems, local_sems = refs[2 * nt:]
        x, y, c, chips = _place()
        me, sibling = (x, y, c), (x, y, 1 - c)

        def copy(t, k, block, to, src=None):
            px, py, pc = block
            dst = o_refs[t].at[4 * px + 2 * py + pc]
            return pltpu.make_async_remote_copy(
                src_ref=dst if src is None else src, dst_ref=dst, send_sem=send_sems.at[t, k], recv_sem=recv_sems.at[t, k],
                device_id=to, device_id_type=MESH)

        mine = [pltpu.make_async_copy(x_refs[t], o_refs[t].at[4 * x + 2 * y + c], local_sems.at[t]) for t in range(nt)]
        for cp in mine:
            cp.start()
        first = []
        for t in range(nt):
            first.append(copy(t, 0, me, sibling, src=x_refs[t]))
            first += [copy(t, 1 + j, me, (*chip, c), src=x_refs[t]) for j, chip in enumerate(chips)]
        for cp in first:
            cp.start()
        passed = []
        for j, chip in enumerate(chips):
            for t in range(nt):
                copy(t, 1 + j, (*chip, c), me).wait_recv()
                fwd = copy(t, 4 + j, (*chip, c), sibling)
                fwd.start()
                passed.append(fwd)
        for t in range(nt):
            copy(t, 0, sibling, me).wait_recv()
            for j, chip in enumerate(chips):
                copy(t, 4 + j, (*chip, 1 - c), me).wait_recv()
        for cp in first + passed:
            cp.wait_send()
        for cp in mine:
            cp.wait()

    return pl.pallas_call(
        body, name=name, in_specs=[ANY] * nt, out_specs=[ANY] * nt,
        out_shape=[S((N_DEV,) + a.shape, a.dtype) for a in xs],
        scratch_shapes=[pltpu.SemaphoreType.DMA((nt, 7)), pltpu.SemaphoreType.DMA((nt, 7)), pltpu.SemaphoreType.DMA((nt,))],
        compiler_params=pltpu.CompilerParams(has_side_effects=True))(*xs)


def _exchange(hook, name):
    comm = hook()
    ci, co = len(comm.inputs), len(comm.out_shapes)

    def body(*refs):
        cins, couts = refs[:ci], refs[ci:ci + co]
        send, recv = refs[-2:]
        comm.start(cins, couts, send, recv)
        comm.finish(cins, couts, send, recv)

    res = pl.pallas_call(
        body, name=name, in_specs=[ANY] * ci, out_specs=[ANY] * co, out_shape=comm.out_shapes,
        scratch_shapes=[pltpu.SemaphoreType.DMA((comm.n_sems,)), pltpu.SemaphoreType.DMA((comm.n_sems,))],
        input_output_aliases=dict(comm.aliases),
        compiler_params=pltpu.CompilerParams(has_side_effects=True))(*comm.inputs)
    hook(res)


def _row_tile(r):
    tr = r if r <= ROW_TILE or r % ROW_TILE else ROW_TILE
    assert r % tr == 0
    return tr


def _rs_partial(g32, sib, place, name):
    _, r, cdim = g32.shape
    tr = _row_tile(r)

    def body(place_ref, g_ref, s_ref, p_ref, own_ref):
        k = pl.program_id(1)
        tot = g_ref[...] + s_ref[...].astype(F32)
        p_ref[...] = tot.astype(BF16)

        @pl.when(k == place_ref[1])
        def _():
            own_ref[...] = tot

    grid_spec = pltpu.PrefetchScalarGridSpec(
        num_scalar_prefetch=1, grid=(r // tr, 4),
        in_specs=[pl.BlockSpec((None, None, tr, cdim), lambda i, k, pr: (k, pr[0], i, 0)),
                  pl.BlockSpec((None, tr, cdim), lambda i, k, pr: (k, i, 0))],
        out_specs=[pl.BlockSpec((None, tr, cdim), lambda i, k, pr: (k, i, 0)), pl.BlockSpec((tr, cdim), lambda i, k, pr: (i, 0))])
    return pl.pallas_call(
        body, grid_spec=grid_spec, name=name,
        out_shape=[S((4, r, cdim), BF16), S((r, cdim), F32)],
        compiler_params=_cp("parallel", "arbitrary"))(place, g32.reshape(4, 2, r, cdim), sib)


def _adamw_math(w, g, m, v):
    m = ADAM_B1 * m + (1.0 - ADAM_B1) * g
    v = ADAM_B2 * v + (1.0 - ADAM_B2) * (g * g)
    m_hat = m / (1.0 - ADAM_B1 ** ADAM_STEP)
    v_hat = v / (1.0 - ADAM_B2 ** ADAM_STEP)
    delta = -ADAM_LR * (m_hat / (jnp.sqrt(v_hat) + ADAM_EPS) + ADAM_WD * w)
    return delta, m, v


def _adamw_shard(owns, recvs, w, m, v, name, flipped=False):
    nl = w.shape[0]
    r, cdim = owns[0].shape
    tr = _row_tile(r)
    nr = r // tr

    def body(*refs):
        own_refs, recv_refs = refs[:nl], refs[nl:2 * nl]
        w_ref, m_ref, v_ref, g_out, d_out, m_out, v_out = refs[2 * nl:]
        layer = pl.program_id(0)
        g = None
        for l in range(nl):
            gl = own_refs[l][...] + recv_refs[l][0].astype(F32) + recv_refs[l][1].astype(F32) + recv_refs[l][2].astype(F32)
            g = gl if g is None else jnp.where(layer == l, gl, g)
        if flipped:
            g = g.T
        g_out[...] = g
        d_out[...], m_out[...], v_out[...] = _adamw_math(w_ref[...], g, m_ref[...], v_ref[...])

    park = lambda l: (lambda layer, i: (jnp.where(layer == l, i, jnp.where(layer < l, 0, nr - 1)), 0))
    park3 = lambda l: (lambda layer, i: (0, jnp.where(layer == l, i, jnp.where(layer < l, 0, nr - 1)), 0))
    if flipped:
        row = pl.BlockSpec((None, cdim, tr), lambda layer, i: (layer, 0, i))
    else:
        row = pl.BlockSpec((None, tr, cdim), lambda layer, i: (layer, i, 0))
    return pl.pallas_call(
        body, grid=(nl, nr), name=name,
        in_specs=[pl.BlockSpec((tr, cdim), park(l)) for l in range(nl)] + [pl.BlockSpec((3, tr, cdim), park3(l)) for l in range(nl)]
        + [row, row, row],
        out_specs=[row] * 4, out_shape=[S(w.shape, F32)] * 4,
        compiler_params=_cp("arbitrary", "arbitrary"))(*owns, *recvs, w, m, v)


def _adamw_small(galls, ws, ms, vs, name):
    n = len(galls)

    def body(*refs):
        g_refs, w_refs, m_refs, v_refs, outs = refs[:n], refs[n:2 * n], refs[2 * n:3 * n], refs[3 * n:4 * n], refs[4 * n:]
        for i in range(n):
            g = g_refs[i][0].astype(F32)
            for s in range(1, N_DEV):
                g = g + g_refs[i][s].astype(F32)
            outs[i][...] = g
            outs[n + i][...], outs[2 * n + i][...], outs[3 * n + i][...] = _adamw_math(w_refs[i][...], g, m_refs[i][...], v_refs[i][...])

    res = pl.pallas_call(body, out_shape=[S(a.shape, F32) for a in ws] * 4, name=name)(*galls, *ws, *ms, *vs)
    return [res[k * n:(k + 1) * n] for k in range(4)]


REPLICATED = ["mix_norm", "ffn_norm", "sgu_v_gain", "sgu_w_s", "sgu_b_s", "attn_q_gain", "attn_k_gain", "attn_sinks", "rel_bias",
              "ffn_conv_b"]
WEIGHTS = ["mix_norm", "ffn_norm", "sgu_w_in", "sgu_v_gain", "sgu_w_s", "sgu_b_s", "sgu_w_out", "attn_w_qkv", "attn_q_gain",
           "attn_k_gain", "attn_sinks", "attn_w_o", "rel_bias", "ffn_w_up", "ffn_conv_w", "ffn_conv_b", "ffn_w_down"]
SMALL = ["g_" + n for n in REPLICATED]
BF16_TRANSIT = {"sgu_w_s"}
SMALL_ATTN = ["g_attn_q_gain", "g_attn_k_gain", "g_attn_sinks", "g_rel_bias"]
SMALL_FFN = ["g_ffn_norm", "g_ffn_conv_b"]
SMALL_LATE = [n for n in SMALL if n not in SMALL_ATTN + SMALL_FFN]

GATHER_FIRST = ["sgu_w_in", "ffn_conv_w"]
PLAN = {
    "sgu_in": [("ag1", "sgu_w_out"), ("ag1", "ffn_w_down0")],
    "sgu_gate": [("ag2", "sgu_w_out"), ("ag2", "ffn_w_down0"), ("ag1", "ffn_w_up0")],
    "sgu_out": [("ag2", "ffn_w_up0"), ("ag1", "attn_w_qkv")],
    "ffn0_fwd": [("ag2", "attn_w_qkv"), ("ag1", "attn_w_o"), ("ag1", "ffn_w_up1")],
    "qkv": [("ag2", "attn_w_o"), ("ag2", "ffn_w_up1")],
    "attn": [("ag1", "ffn_w_down1")],
    "attn_out": [("ag2", "ffn_w_down1")],
    "ffn1_bwd2": [("rs1", "ffn_w_down1")],
    "ffn1_dw_up": [("rs2", "ffn_w_down1")],
    "attn_do": [("rs1", "ffn_w_up1")],
    "attn_bwd": [("rs2", "ffn_w_up1"), ("rs1", "attn_w_o")],
    "dw_qkv": [("rs2", "attn_w_o")],
    "dx_qkv": [("rs1", "attn_w_qkv")],
    "ffn0_bwd1": [("rs2", "attn_w_qkv")] + [("ag1", n) for n in SMALL_ATTN],
    "ffn0_bwd2": [("rs1", "ffn_w_down0")] + [("ag2", n) for n in SMALL_ATTN],
    "ffn0_dw_up": [("rs2", "ffn_w_down0")],
    "sgu_dgated": [("rs1", "ffn_w_up0")] + [("ag1", n) for n in SMALL_FFN],
    "dw_sgu_out": [("ag2", n) for n in SMALL_FFN],
    "sgu_gate_bwd": [("rs2", "ffn_w_up0"), ("rs1", "sgu_w_out")],
    "dw_sgu_in": [("rs2", "sgu_w_out")] + [("ag1", n) for n in SMALL_LATE],
    "last_a": [("rs1", "sgu_w_in"), ("rs1", "ffn_conv_w")] + [("ag2", n) for n in SMALL_LATE],
    "last_b": [("rs2", "sgu_w_in"), ("rs2", "ffn_conv_w")],
}


class _Overlap:
    def __init__(self, shard, place):
        self.shard, self.place = shard, place
        self.part, self.full = {}, {}
        self.grads, self.sib, self.own, self.recv = {}, {}, {}, {}

    def w(self, n):
        return self.full[n]

    def grad(self, n, pair):
        self.grads[n] = pair

    def small(self, g_rep):
        self.shard.update(("g_" + n, a.astype(BF16) if n in BF16_TRANSIT else a) for n, a in _views2d(g_rep).items())

    def chip_sums(self, n):
        sums, self.own[n] = _rs_partial(self.grads[n][0], self.sib.pop(n), self.place, "rs_partial_" + n)
        return sums

    def hook(self, host):
        ops = PLAN.get(host)
        if not ops:
            return None
        where = {"ag1": self.part, "ag2": self.full, "rs1": self.sib, "rs2": self.recv}
        idx = []

        def hook(results=None):
            if results is not None:
                for (kind, n), i in zip(ops, idx):
                    where[kind][n] = results[i]
                return None
            comm = _Comm()
            for kind, n in ops:
                arr = {"ag1": lambda: self.shard[n], "ag2": lambda: self.part.pop(n), "rs1": lambda: self.grads[n][1],
                       "rs2": lambda: self.chip_sums(n)}[kind]()
                idx.append(comm.add(kind, arr))
            return comm

        return hook


TRANSPOSED = {"attn_w_qkv"}
PHYSICAL_T = {"ffn_w_up"}
SHARDED = {
    "sgu_w_in": ["sgu_w_in"], "sgu_w_out": ["sgu_w_out"], "attn_w_qkv": ["attn_w_qkv"], "attn_w_o": ["attn_w_o"],
    "ffn_w_up": ["ffn_w_up0", "ffn_w_up1"], "ffn_w_down": ["ffn_w_down0", "ffn_w_down1"], "ffn_conv_w": ["ffn_conv_w"],
}


def _send_views(w):
    out = {"ffn_conv_w": w["ffn_conv_w"].reshape(6, -1)}
    for name, parts in SHARDED.items():
        if name != "ffn_conv_w":
            out.update((p, (w[name][l].T if name in TRANSPOSED else w[name][l]).astype(BF16)) for l, p in enumerate(parts))
    return out


def _views2d(d):
    return {n: d[n].reshape(-1, d[n].shape[-1]) for n in REPLICATED if n in d}


def kernel(x, mix_norm, ffn_norm, sgu_w_in, sgu_v_gain, sgu_w_s, sgu_b_s, sgu_w_out, attn_w_qkv, attn_q_gain, attn_k_gain, attn_sinks, attn_w_o, rel_bias, ffn_w_up, ffn_conv_w, ffn_conv_b, ffn_w_down, loss_target, m_mix_norm, m_ffn_norm, m_sgu_w_in, m_sgu_v_gain, m_sgu_w_s, m_sgu_b_s, m_sgu_w_out, m_attn_w_qkv, m_attn_q_gain, m_attn_k_gain, m_attn_sinks, m_attn_w_o, m_rel_bias, m_ffn_w_up, m_ffn_conv_w, m_ffn_conv_b, m_ffn_w_down, v_mix_norm, v_ffn_norm, v_sgu_w_in, v_sgu_v_gain, v_sgu_w_s, v_sgu_b_s, v_sgu_w_out, v_attn_w_qkv, v_attn_q_gain, v_attn_k_gain, v_attn_sinks, v_attn_w_o, v_rel_bias, v_ffn_w_up, v_ffn_conv_w, v_ffn_conv_b, v_ffn_w_down):
    w = dict(zip(WEIGHTS, (mix_norm, ffn_norm, sgu_w_in, sgu_v_gain, sgu_w_s, sgu_b_s, sgu_w_out, attn_w_qkv, attn_q_gain, attn_k_gain,
                           attn_sinks, attn_w_o, rel_bias, ffn_w_up, ffn_conv_w, ffn_conv_b, ffn_w_down)))
    m = dict(zip(WEIGHTS, (m_mix_norm, m_ffn_norm, m_sgu_w_in, m_sgu_v_gain, m_sgu_w_s, m_sgu_b_s, m_sgu_w_out, m_attn_w_qkv, m_attn_q_gain,
                           m_attn_k_gain, m_attn_sinks, m_attn_w_o, m_rel_bias, m_ffn_w_up, m_ffn_conv_w, m_ffn_conv_b, m_ffn_w_down)))
    v = dict(zip(WEIGHTS, (v_mix_norm, v_ffn_norm, v_sgu_w_in, v_sgu_v_gain, v_sgu_w_s, v_sgu_b_s, v_sgu_w_out, v_attn_w_qkv, v_attn_q_gain,
                           v_attn_k_gain, v_attn_sinks, v_attn_w_o, v_rel_bias, v_ffn_w_up, v_ffn_conv_w, v_ffn_conv_b, v_ffn_w_down)))
    rep = {n: w[n] for n in REPLICATED}

    xi, yi, ci = lax.axis_index("x"), lax.axis_index("y"), lax.axis_index("c")
    place = jnp.stack([ci, 2 * xi + yi]).astype(jnp.int32)
    sch = _Overlap(_send_views(w), place)
    sch.full.update(zip(GATHER_FIRST, _allgather([sch.shard[n] for n in GATHER_FIRST], "gather_first")))

    loss, grad_x, g_rep = _local_step(x[0], loss_target[0], rep, sch)
    loss = lax.psum(loss, ("x", "y", "c"))
    _exchange(sch.hook("last_a"), "last_a")
    _exchange(sch.hook("last_b"), "last_b")

    out = [{}, {}, {}, {}]
    for name, parts in SHARDED.items():
        flip = (lambda a: jnp.swapaxes(a, -1, -2)) if name in TRANSPOSED | PHYSICAL_T else (lambda a: a)
        shape = flip(w[name]).shape
        as3d = lambda a: flip(a).reshape(len(parts), -1, shape[-1])
        res = _adamw_shard([sch.own[p] for p in parts], [sch.recv[p] for p in parts], as3d(w[name]), as3d(m[name]), as3d(v[name]),
                           "adamw_" + name, flipped=name in PHYSICAL_T)
        for o, r in zip(out, res):
            o[name] = flip(r.reshape(shape))
    small = _adamw_small([sch.full[n] for n in SMALL], *[list(_views2d(d).values()) for d in (rep, m, v)], "adamw_small")
    for o, res in zip(out, small):
        o.update((n, r.reshape(w[n].shape)) for n, r in zip(REPLICATED, res))

    return (loss, grad_x[None], *[out[0][n] for n in WEIGHTS], *[out[1][n] for n in WEIGHTS],
            *[out[2][n] for n in WEIGHTS], *[out[3][n] for n in WEIGHTS])
```

```python
import functools
import math

import numpy as np
import jax
import jax.numpy as jnp
from jax import lax
from jax.experimental import pallas as pl
from jax.experimental.pallas import tpu as pltpu

F32 = jnp.float32
BF16 = jnp.bfloat16
DH = jnp.bfloat16
S = jax.ShapeDtypeStruct

D = 1024
CHUNK = 128
SGU_W = 2048
SGU_G = 16
HD = 64
NH = 16
NKV = 4
KVG = 4
D_FF = 2816
REL_BUCKETS = 32
REL_MAX_DIST = 128
EPS = 1e-6
N_DEV = 8
MESH = pl.DeviceIdType.MESH

ADAM_LR = 0.001
ADAM_B1 = 0.9
ADAM_B2 = 0.999
ADAM_EPS = 1e-08
ADAM_WD = 0.01
ADAM_STEP = 10

ROW_TILE = 512
HALO = 8
FFN_ROWS = 256


def _tm(t):
    return min(ROW_TILE, t)


def _cp(*sem):
    return pltpu.CompilerParams(dimension_semantics=sem)


ANY = pl.BlockSpec(memory_space=pl.ANY)


def _place():
    x, y, c = lax.axis_index("x"), lax.axis_index("y"), lax.axis_index("c")
    return x, y, c, [(1 - x, y), (x, 1 - y), (1 - x, 1 - y)]


class _Comm:
    SEMS = {"ag1": 5, "ag2": 3, "rs1": 4, "rs2": 3}

    def __init__(self):
        self.inputs, self.out_shapes, self.aliases, self.ops, self.n_sems = [], [], {}, [], 0

    def add(self, kind, arr):
        lead = {"ag1": N_DEV, "ag2": None, "rs1": 4, "rs2": 3}[kind]
        shape = arr.shape if lead is None else (lead,) + arr.shape[(0 if kind == "ag1" else 1):]
        if kind == "ag2":
            self.aliases[len(self.inputs)] = len(self.out_shapes)
        self.ops.append((kind, len(self.inputs), len(self.out_shapes), self.n_sems))
        self.inputs.append(arr)
        self.out_shapes.append(S(shape, arr.dtype))
        self.n_sems += self.SEMS[kind]
        return len(self.out_shapes) - 1

    def _copies(self, ins, outs, send, recv):
        x, y, c, chips = _place()
        me, sibling = (x, y, c), (x, y, 1 - c)
        slot = lambda px, py, pc: 4 * px + 2 * py + pc
        sends, recvs, local = [], [], []

        def rc(src, dst, k, to):
            return lambda: pltpu.make_async_remote_copy(src_ref=src(), dst_ref=dst(), send_sem=send.at[k], recv_sem=recv.at[k],
                                                        device_id=to, device_id_type=MESH)

        for kind, ii, oi, b in self.ops:
            src, dst = ins[ii], outs[oi]
            at = lambda ref, i: (lambda: ref.at[i])
            if kind == "ag1":
                whole, mine = (lambda s=src: s), at(dst, slot(*me))
                sends.append(rc(whole, mine, b, sibling))
                recvs.append(rc(whole, at(dst, slot(x, y, 1 - c)), b, me))
                for j, chip in enumerate(chips):
                    sends.append(rc(whole, mine, b + 1 + j, (*chip, c)))
                    recvs.append(rc(whole, at(dst, slot(*chip, c)), b + 1 + j, me))
                local.append(lambda s=src, m=mine, k=b + 4: pltpu.make_async_copy(s, m(), send.at[k]))
            elif kind == "ag2":
                for j, chip in enumerate(chips):
                    sends.append(rc(at(dst, slot(*chip, c)), at(dst, slot(*chip, c)), b + j, sibling))
                    recvs.append(rc(at(dst, slot(*chip, 1 - c)), at(dst, slot(*chip, 1 - c)), b + j, me))
            elif kind == "rs1":
                for k in range(4):
                    sends.append(rc(at(src, 2 * k + (1 - c)), at(dst, k), b + k, sibling))
                    recvs.append(rc(at(src, 2 * k + c), at(dst, k), b + k, me))
            else:
                for j, (px, py) in enumerate(chips):
                    sends.append(rc(at(src, 2 * px + py), at(dst, j), b + j, (px, py, c)))
                    recvs.append(rc(at(src, 2 * px + py), at(dst, j), b + j, me))
        return sends, recvs, local

    def start(self, ins, outs, send, recv):
        sends, _, local = self._copies(ins, outs, send, recv)
        for make in local + sends:
            make().start()

    def finish(self, ins, outs, send, recv):
        sends, recvs, local = self._copies(ins, outs, send, recv)
        for make in recvs:
            make().wait_recv()
        for make in sends:
            make().wait_send()
        for make in local:
            make().wait()


def _run(body, args, hook, *, grid, in_specs, out_specs, out_shape, name, semantics, scratch_shapes=(), aliases=None):
    comm = hook() if hook is not None else None
    aliases = dict(aliases or {})
    if comm is None:
        return pl.pallas_call(body, grid=grid, in_specs=in_specs, out_specs=out_specs, out_shape=out_shape, name=name,
                              scratch_shapes=list(scratch_shapes), input_output_aliases=aliases,
                              compiler_params=_cp(*semantics))(*args)
    single = not isinstance(out_shape, (list, tuple))
    out_shapes = [out_shape] if single else list(out_shape)
    out_specs_l = [out_specs] if single else list(out_specs)
    n_in, n_out, n_scr, ci, co = len(args), len(out_shapes), len(scratch_shapes), len(comm.inputs), len(comm.out_shapes)

    def wrapped(*refs):
        ins, cins = refs[:n_in], refs[n_in:n_in + ci]
        outs, couts = refs[n_in + ci:n_in + ci + n_out], refs[n_in + ci + n_out:n_in + ci + n_out + co]
        scr = refs[n_in + ci + n_out + co:n_in + ci + n_out + co + n_scr]
        send, recv = refs[-2:]
        first = functools.reduce(lambda a, b: a & b, [pl.program_id(a) == 0 for a in range(len(grid))])
        last = functools.reduce(lambda a, b: a & b, [pl.program_id(a) == g - 1 for a, g in enumerate(grid)])

        @pl.when(first)
        def _():
            comm.start(cins, couts, send, recv)

        body(*ins, *outs, *scr)

        @pl.when(last)
        def _():
            comm.finish(cins, couts, send, recv)

    res = pl.pallas_call(
        wrapped, grid=grid, in_specs=list(in_specs) + [ANY] * ci, out_specs=out_specs_l + [ANY] * co,
        out_shape=out_shapes + comm.out_shapes, name=name,
        scratch_shapes=list(scratch_shapes) + [pltpu.SemaphoreType.DMA((comm.n_sems,)), pltpu.SemaphoreType.DMA((comm.n_sems,))],
        input_output_aliases={**aliases, **{n_in + k: n_out + v for k, v in comm.aliases.items()}},
        compiler_params=pltpu.CompilerParams(dimension_semantics=("arbitrary",) * len(grid), has_side_effects=True))(*args, *comm.inputs)
    hook(res[n_out:])
    return res[0] if single else list(res[:n_out])


def _dot(a, b):
    return jnp.dot(a, b, preferred_element_type=F32)


def _dot_nt(a, b):
    return lax.dot_general(a, b, (((1,), (1,)), ((), ())), preferred_element_type=F32)


def _dot_tn(a, b):
    return lax.dot_general(a, b, (((0,), (0,)), ((), ())), preferred_element_type=F32)


def _gelu(x):
    return 0.5 * x * (1.0 + lax.erf(x * (2.0 ** -0.5)))


def _gelu_and_grad(x):
    cdf = 0.5 * (1.0 + lax.erf(x * (2.0 ** -0.5)))
    return x * cdf, cdf + x * jnp.exp(-0.5 * x * x) * (1.0 / math.sqrt(2.0 * math.pi))


def _sigmoid(x):
    return 1.0 / (1.0 + jnp.exp(-x))


def _rstd(x):
    return lax.rsqrt(jnp.mean(x * x, axis=-1, keepdims=True) + EPS)


def _rel_tables():
    q = np.arange(CHUNK)[:, None] + CHUNK
    k = np.arange(2 * CHUNK)[None, :]
    dist = q - k
    n = np.maximum(dist, 0)
    max_exact = REL_BUCKETS // 2
    large = max_exact + (np.log(np.maximum(n, 1).astype(np.float32) / max_exact)
                         / math.log(REL_MAX_DIST / max_exact) * (REL_BUCKETS - max_exact)).astype(np.int32)
    large = np.minimum(large, REL_BUCKETS - 1)
    return np.where(n < max_exact, n, large).astype(np.int32)


def _rmsnorm(x, gain, name):
    t = x.shape[0]
    tm = _tm(t)

    def body(x_ref, g_ref, o_ref):
        xv = x_ref[...]
        o_ref[...] = (xv * _rstd(xv) * g_ref[...]).astype(BF16)

    return pl.pallas_call(
        body, grid=(t // tm,), name=name,
        in_specs=[pl.BlockSpec((tm, D), lambda i: (i, 0)), pl.BlockSpec((1, D), lambda i: (0, 0))],
        out_specs=pl.BlockSpec((tm, D), lambda i: (i, 0)),
        out_shape=S((t, D), BF16), compiler_params=_cp("parallel"))(x, gain)


def _resident(shape):
    zeros = (0,) * len(shape)
    return pl.BlockSpec(shape, lambda *_: zeros, pipeline_mode=pl.Buffered(1))


def _mm_slot(hn, wg, out_dtype, name, hook=None):
    t, k = hn.shape
    ns, _, n = wg.shape
    tm = _tm(t)

    def body(a_ref, w_ref, o_ref):
        a = a_ref[...]
        for s in range(ns):
            o_ref[s] = _dot(a, w_ref[s]).astype(out_dtype)

    return _run(
        body, [hn, wg], hook, grid=(t // tm,), name=name, semantics=("parallel",),
        in_specs=[pl.BlockSpec((tm, k), lambda i: (i, 0)), _resident(wg.shape)],
        out_specs=pl.BlockSpec((ns, tm, n), lambda i: (0, i, 0)), out_shape=S((ns, t, n), out_dtype))


def _mm_t(hn, wt, name, hook=None):
    t, k = hn.shape
    ns, n, _ = wt.shape
    tm = _tm(t)

    def body(a_ref, w_ref, o_ref):
        a = a_ref[...]
        for s in range(ns):
            o_ref[s * n:(s + 1) * n, :] = _dot_nt(w_ref[s], a)

    return _run(
        body, [hn, wt], hook, grid=(t // tm,), name=name, semantics=("parallel",),
        in_specs=[pl.BlockSpec((tm, k), lambda i: (i, 0)), _resident(wt.shape)],
        out_specs=pl.BlockSpec((ns * n, tm), lambda i: (0, i)), out_shape=S((ns * n, t), F32))


def _conv3(a, prev, cw, cb, tm):
    ext = jnp.concatenate([prev, a], axis=0)
    return cw[2:3] * a + cw[1:2] * ext[HALO - 1:HALO - 1 + tm] + cw[0:1] * ext[HALO - 2:HALO - 2 + tm] + cb


def _ffn_fwd(hn, h, wup, wdown, cw, cb, extra, mode, name, hook=None):
    t, k = hn.shape
    n = wup.shape[-1]
    nh = wup.shape[0] // 2
    tm = min(FFN_ROWS, t)
    ni = t // tm

    def body(a_ref, h_ref, wu_ref, wd_ref, cw_ref, cb_ref, e_ref, as_ref, cs_ref, o1_ref, o2_ref, carry):
        i = pl.program_id(0)

        @pl.when(i == 0)
        def _():
            carry[...] = jnp.zeros_like(carry)

        a = a_ref[...]
        acc = h_ref[...]
        nxt = (_dot(a, wu_ref[0]), _dot(a, wu_ref[nh]))
        for j in range(nh):
            ag, av = nxt
            if j + 1 < nh:
                nxt = (_dot(a, wu_ref[j + 1]), _dot(a, wu_ref[nh + j + 1]))
            as_ref[j] = ag.astype(BF16)
            as_ref[nh + j] = av.astype(BF16)
            cg = _conv3(ag, carry[j], cw_ref[j], cb_ref[j], tm)
            cv = _conv3(av, carry[nh + j], cw_ref[nh + j], cb_ref[nh + j], tm)
            carry[j] = ag[tm - HALO:]
            carry[nh + j] = av[tm - HALO:]
            cs_ref[j] = cg.astype(BF16)
            cs_ref[nh + j] = cv.astype(BF16)
            act = (cg * _sigmoid(cg) * cv).astype(BF16)
            acc = acc + _dot(act, wd_ref[j * n:(j + 1) * n, :])
        if mode == "norm":
            o1_ref[...] = acc
            o2_ref[...] = (acc * _rstd(acc) * e_ref[...]).astype(BF16)
        else:
            err = acc - e_ref[...]
            o1_ref[...] = (err * (1.0 / D)).astype(o1_ref.dtype)
            o2_ref[...] = jnp.full(o2_ref.shape, jnp.sum(err * err), F32)

    row = pl.BlockSpec((tm, D), lambda i: (i, 0))
    if mode == "norm":
        e_spec, o2_spec, o2_shape = pl.BlockSpec((1, D), lambda i: (0, 0)), row, S((t, D), BF16)
    else:
        e_spec, o2_spec, o2_shape = row, pl.BlockSpec((None, 8, 128), lambda i: (i, 0, 0)), S((ni, 8, 128), F32)
    aspec = pl.BlockSpec((2 * nh, tm, n), lambda i: (0, i, 0))
    return _run(
        body, [hn, h, wup, wdown, cw, cb, extra], hook, grid=(ni,), name=name, semantics=("arbitrary",),
        in_specs=[pl.BlockSpec((tm, k), lambda i: (i, 0)), row, _resident(wup.shape), _resident(wdown.shape),
                  _resident(cw.shape), _resident(cb.shape), e_spec],
        out_specs=[aspec, aspec, row, o2_spec],
        out_shape=[S((2 * nh, t, n), BF16), S((2 * nh, t, n), BF16), S((t, D), F32 if mode == "norm" else DH), o2_shape],
        scratch_shapes=[pltpu.VMEM((2 * nh, HALO, n), F32)])


def _tril_mask():
    r = lax.broadcasted_iota(jnp.int32, (CHUNK, CHUNK), 0)
    c = lax.broadcasted_iota(jnp.int32, (CHUNK, CHUNK), 1)
    return r >= c


def _sgu_gate_fwd(a_s, vgain, ws, bst, name, hook=None):
    t = a_s.shape[1]
    sw = a_s.shape[2]
    gps = sw // CHUNK

    def body(a_ref, vg_ref, ws_ref, b_ref, o_ref):
        v = _gelu(jnp.concatenate([a_ref[4 + s].astype(F32) for s in range(4)], axis=1))
        vn = (v * _rstd(v) * vg_ref[...]).astype(BF16)
        tri = _tril_mask()
        for g in range(SGU_G):
            w = jnp.where(tri, ws_ref[g], 0.0).astype(BF16)
            sg = _dot(w, vn[:, g * CHUNK:(g + 1) * CHUNK]) + b_ref[:, g:g + 1]
            lo = (g % gps) * CHUNK
            u = _gelu(a_ref[g // gps, :, lo:lo + CHUNK].astype(F32))
            o_ref[g // gps, :, lo:lo + CHUNK] = (u * sg).astype(BF16)

    return _run(
        body, [a_s, vgain, ws, bst], hook, grid=(t // CHUNK,), name=name, semantics=("parallel",),
        in_specs=[pl.BlockSpec((8, CHUNK, sw), lambda n: (0, n, 0)), pl.BlockSpec((1, SGU_W), lambda n: (0, 0)),
                  pl.BlockSpec((SGU_G, CHUNK, CHUNK), lambda n: (0, 0, 0)), pl.BlockSpec((CHUNK, SGU_G), lambda n: (0, 0))],
        out_specs=pl.BlockSpec((4, CHUNK, sw), lambda n: (0, n, 0)), out_shape=S((4, t, sw), BF16))


def _resid_mm(a_s, w, resid, extra, mode, name, hook=None, fm=False):
    nk, t, kc = (1, a_s.shape[1], a_s.shape[0]) if fm else a_s.shape
    tm = _tm(t)
    ni = t // tm

    def body(a_ref, w_ref, r_ref, e_ref, o1_ref, o2_ref):
        h = r_ref[...]
        if fm:
            h = h + _dot_tn(a_ref[...], w_ref[...])
        for j in range(0 if fm else nk):
            h = h + _dot(a_ref[j], w_ref[j * kc:(j + 1) * kc, :])
        if mode == "norm":
            o1_ref[...] = h
            o2_ref[...] = (h * _rstd(h) * e_ref[...]).astype(BF16)
        else:
            err = h - e_ref[...]
            o1_ref[...] = (err * (1.0 / D)).astype(o1_ref.dtype)
            o2_ref[...] = jnp.full(o2_ref.shape, jnp.sum(err * err), F32)

    row = pl.BlockSpec((tm, D), lambda i: (i, 0))
    if mode == "norm":
        e_spec, o2_spec, o2_shape = pl.BlockSpec((1, D), lambda i: (0, 0)), row, S((t, D), BF16)
    else:
        e_spec, o2_spec, o2_shape = row, pl.BlockSpec((None, 8, 128), lambda i: (i, 0, 0)), S((ni, 8, 128), F32)
    return _run(
        body, [a_s, w, resid, extra], hook, grid=(ni,), name=name, semantics=("parallel",),
        in_specs=[pl.BlockSpec((kc, tm), lambda i: (0, i)) if fm else pl.BlockSpec((nk, tm, kc), lambda i: (0, i, 0)),
                  _resident(w.shape), row, e_spec],
        out_specs=[row, o2_spec], out_shape=[S((t, D), F32 if mode == "norm" else DH), o2_shape])


def _relbias_fwd(rel_bias_t, bucket_row, name):
    nb = bucket_row.shape[1]

    def body(rb_ref, bk_ref, o_ref):
        onehot = (lax.broadcasted_iota(jnp.int32, (REL_BUCKETS, nb), 0) == bk_ref[...]).astype(F32)
        o_ref[...] = jnp.dot(rb_ref[...], onehot, precision=lax.Precision.HIGHEST, preferred_element_type=F32)

    return pl.pallas_call(body, out_shape=S((NH, nb), F32), name=name)(rel_bias_t, bucket_row)


def _relbias_bwd(dbias, bucket_row, name):
    nb = bucket_row.shape[1]

    def body(db_ref, bk_ref, o_ref):
        onehot = (lax.broadcasted_iota(jnp.int32, (REL_BUCKETS, nb), 0) == bk_ref[...]).astype(F32)
        o_ref[...] = lax.dot_general(db_ref[...], onehot, (((1,), (1,)), ((), ())),
                                     precision=lax.Precision.HIGHEST, preferred_element_type=F32)

    return pl.pallas_call(body, out_shape=S((NH, REL_BUCKETS), F32), name=name)(dbias, bucket_row)


QKV = D + 2 * NKV * HD
KV0 = D


def _rstd_rows(x):
    return lax.rsqrt(jnp.mean(x * x, axis=0, keepdims=True) + EPS)


def _attn_valid(n):
    kj = lax.broadcasted_iota(jnp.int32, (2 * CHUNK, CHUNK), 0)
    qi = lax.broadcasted_iota(jnp.int32, (2 * CHUNK, CHUNK), 1)
    dist = qi + CHUNK - kj
    return (dist >= 0) & (dist < CHUNK) & ((n > 0) | (kj >= CHUNK))


def _attn_band(cur_ref, prev_ref, row):
    return jnp.concatenate([prev_ref[row - KV0:row - KV0 + HD, :], cur_ref[row:row + HD, :]], axis=1)


def _attn_probs(kn_tok, qn, bias, valid, sink):
    s = _dot(kn_tok, qn) * (HD ** -0.5) + bias
    s = jnp.where(valid, s, -jnp.inf)
    m = jnp.maximum(jnp.max(s, axis=0, keepdims=True), sink)
    p = jnp.exp(s - m)
    psink = jnp.exp(sink - m)
    inv = 1.0 / (jnp.sum(p, axis=0, keepdims=True) + psink)
    return p * inv, psink * inv


def _attn_fwd(qkv_t, qg, kg, sinks, bias, name, hook=None):
    t = qkv_t.shape[1]

    def body(cur_ref, prev_ref, qg_ref, kg_ref, sink_ref, bias_ref, o_ref):
        n = pl.program_id(0)
        valid = _attn_valid(n)
        ks = [_attn_band(cur_ref, prev_ref, KV0 + HD * h) for h in range(NKV)]
        kn_toks = [(k * _rstd_rows(k) * kg_ref[...]).astype(BF16).T for k in ks]
        vbs = [_attn_band(cur_ref, prev_ref, KV0 + HD * (NKV + h)).astype(BF16) for h in range(NKV)]
        qs = [cur_ref[HD * hq:HD * (hq + 1), :] for hq in range(NH)]
        qns = [(q * _rstd_rows(q) * qg_ref[...]).astype(BF16) for q in qs]
        ps = [_attn_probs(kn_toks[hq // KVG], qns[hq], bias_ref[hq], valid, sink_ref[hq])[0] for hq in range(NH)]
        for hq in range(NH):
            o_ref[HD * hq:HD * (hq + 1), :] = _dot(vbs[hq // KVG], ps[hq].astype(BF16)).astype(BF16)

    col = pl.BlockSpec((HD, 1), lambda n: (0, 0))
    return _run(
        body, [qkv_t, qkv_t, qg, kg, sinks, bias], hook, grid=(t // CHUNK,), name=name, semantics=("parallel",),
        in_specs=[pl.BlockSpec((QKV, CHUNK), lambda n: (0, n)),
                  pl.BlockSpec((QKV - KV0, CHUNK), lambda n: (KV0 // (QKV - KV0), jnp.maximum(n - 1, 0))),
                  col, col, pl.BlockSpec(memory_space=pltpu.SMEM), pl.BlockSpec((NH, 2 * CHUNK, CHUNK), lambda n: (0, 0, 0))],
        out_specs=pl.BlockSpec((D, CHUNK), lambda n: (0, n)), out_shape=S((D, t), BF16))


def _dx_rows(dh, w, kc, out_dtype, name, hook=None):
    t = dh.shape[0]
    nk = w.shape[0] // kc
    tm = _tm(t)

    def body(d_ref, w_ref, o_ref):
        dhb = d_ref[...].astype(BF16)
        for j in range(nk):
            o_ref[j] = _dot_nt(dhb, w_ref[j * kc:(j + 1) * kc, :]).astype(out_dtype)

    return _run(
        body, [dh, w], hook, grid=(t // tm,), name=name, semantics=("parallel",),
        in_specs=[pl.BlockSpec((tm, D), lambda i: (i, 0)), _resident(w.shape)],
        out_specs=pl.BlockSpec((nk, tm, kc), lambda i: (0, i, 0)), out_shape=S((nk, t, kc), out_dtype))


def _dx_rows_t(dh, w, name, hook=None):
    t = dh.shape[0]
    k = w.shape[0]
    tm = _tm(t)

    def body(d_ref, w_ref, o_ref):
        o_ref[...] = _dot_nt(w_ref[...], d_ref[...].astype(BF16)).astype(BF16)

    return _run(
        body, [dh, w], hook, grid=(t // tm,), name=name, semantics=("parallel",),
        in_specs=[pl.BlockSpec((tm, D), lambda i: (i, 0)), _resident(w.shape)],
        out_specs=pl.BlockSpec((k, tm), lambda i: (0, i)), out_shape=S((k, t), BF16))


def _ffn_bwd1(dh, c, wdown, name, hook=None):
    ns, t, n = c.shape
    nh = ns // 2
    tm = min(FFN_ROWS, t)
    ni = t // tm

    def body(d_ref, c_ref, wd_ref, dc_ref, dw_hbm, dwb_hbm, acc, stage):
        i = pl.program_id(0)

        @pl.when(i == 0)
        def _():
            acc[...] = jnp.zeros_like(acc)

        dhb = d_ref[...].astype(BF16)
        for j in range(nh):
            dact = _dot_nt(dhb, wd_ref[j * n:(j + 1) * n, :])
            cg = c_ref[j].astype(F32)
            cv = c_ref[nh + j].astype(F32)
            sg = _sigmoid(cg)
            gs = cg * sg
            acc[j * n:(j + 1) * n, :] += _dot_tn((gs * cv).astype(BF16), dhb)
            dc_ref[j] = (dact * cv * (sg + gs * (1.0 - sg))).astype(BF16)
            dc_ref[nh + j] = (dact * gs).astype(BF16)

        @pl.when(i == ni - 1)
        def _():
            pltpu.sync_copy(acc, dw_hbm)
            for j in range(nh):
                stage[...] = acc[j * n:(j + 1) * n, :].astype(BF16)
                pltpu.sync_copy(stage, dwb_hbm.at[pl.ds(j * n, n), :])

    slab = pl.BlockSpec((ns, tm, n), lambda i: (0, i, 0))
    return _run(
        body, [dh, c, wdown], hook, grid=(ni,), name=name, semantics=("arbitrary",),
        in_specs=[pl.BlockSpec((tm, D), lambda i: (i, 0)), slab, _resident(wdown.shape)],
        out_specs=[slab, ANY, ANY], out_shape=[S((ns, t, n), BF16), S(wdown.shape, F32), S(wdown.shape, BF16)],
        scratch_shapes=[pltpu.VMEM(wdown.shape, F32), pltpu.VMEM((n, D), BF16)])


def _ffn_bwd2(dc, a, wup, cw, h, gain, dh_in, name, hook=None):
    ns, t, n = dc.shape
    tm = min(FFN_ROWS, t)
    ni = t // tm

    def body(dc_ref, a_ref, wu_ref, cw_ref, h_ref, g_ref, di_ref, da_ref, o_ref, dg_ref, dcw_ref, dcb_ref, carry, keep):
        i = pl.program_id(0)

        @pl.when(i == 0)
        def _():
            carry[...] = jnp.zeros_like(carry)
            dg_ref[...] = jnp.zeros_like(dg_ref)
            dcw_ref[...] = jnp.zeros_like(dcw_ref)
            dcb_ref[...] = jnp.zeros_like(dcb_ref)

        rsum = lambda v: jnp.sum(v, axis=0, keepdims=True)
        acc = jnp.zeros((tm, D), F32)
        for s in range(ns):
            x = dc_ref[s].astype(F32)
            ext = jnp.concatenate([x, carry[s]], axis=0)
            keep[0] = ext[1:1 + tm]
            keep[1] = ext[2:2 + tm]
            x1, x2 = keep[0], keep[1]
            cwv = cw_ref[s]
            da = (cwv[2:3] * x + cwv[1:2] * x1 + cwv[0:1] * x2).astype(BF16)
            carry[s] = x[:HALO]
            da_ref[s] = da
            acc = acc + _dot_nt(da, wu_ref[s])
            av = a_ref[s].astype(F32)
            dcw_ref[s] += jnp.concatenate([rsum(x2 * av), rsum(x1 * av), rsum(x * av)], axis=0)
            dcb_ref[s] += rsum(x)
        hv = h_ref[...]
        r = _rstd(hv)
        gg = acc * g_ref[...]
        dh_new = di_ref[...].astype(F32) + r * gg - hv * (r * r * r * jnp.mean(gg * hv, axis=-1, keepdims=True))
        o_ref[...] = dh_new.astype(o_ref.dtype)
        dg_ref[...] += jnp.sum(acc * hv * r, axis=0, keepdims=True)

    slab = pl.BlockSpec((ns, tm, n), lambda i: (0, ni - 1 - i, 0))
    row = pl.BlockSpec((tm, D), lambda i: (ni - 1 - i, 0))
    vec = pl.BlockSpec((1, D), lambda i: (0, 0))
    whole = lambda shape: pl.BlockSpec(shape, lambda i: (0,) * len(shape))
    return _run(
        body, [dc, a, wup, cw, h, gain, dh_in], hook, grid=(ni,), name=name, semantics=("arbitrary",),
        in_specs=[slab, slab, _resident(wup.shape), _resident(cw.shape), row, vec, row],
        out_specs=[slab, row, vec, whole((ns, 3, n)), whole((ns, 1, n))],
        out_shape=[S((ns, t, n), BF16), S((t, D), DH), S((1, D), F32), S((ns, 3, n), F32), S((ns, 1, n), F32)],
        scratch_shapes=[pltpu.VMEM((ns, HALO, n), F32), pltpu.VMEM((2, tm, n), F32)])


def _dw_slot(hn, dy_s, name, hook=None):
    t, k = hn.shape
    ns, _, n = dy_s.shape
    tm = _tm(t)

    def body(a_ref, b_ref, o_ref, ob_ref, at_ref):
        @pl.when(pl.program_id(0) == 0)
        def _():
            for i in range(t // tm):
                at_ref[:, i * tm:(i + 1) * tm] = a_ref[i * tm:(i + 1) * tm, :].T

        acc = _dot(at_ref[...], b_ref[...])
        o_ref[...] = acc
        ob_ref[...] = acc.astype(BF16)

    ospec = pl.BlockSpec((None, k, n), lambda j: (j, 0, 0))
    return _run(
        body, [hn, dy_s], hook, grid=(ns,), name=name, semantics=("arbitrary",),
        in_specs=[_resident(hn.shape), pl.BlockSpec((None, t, n), lambda j: (j, 0, 0))],
        out_specs=[ospec, ospec], out_shape=[S((ns, k, n), F32), S((ns, k, n), BF16)],
        scratch_shapes=[pltpu.VMEM((k, t), BF16)])


def _dw_rows(a_s, dh, name, hook=None, fm=False):
    nk, t, kc = (1, a_s.shape[1], a_s.shape[0]) if fm else a_s.shape
    tm = _tm(t)
    ni = t // tm

    def body(a_ref, d_ref, o_ref, ob_ref):
        i = pl.program_id(0)
        dhb = d_ref[...].astype(BF16)

        @pl.when(i == 0)
        def _():
            o_ref[...] = jnp.zeros_like(o_ref)

        if fm:
            o_ref[...] += _dot(a_ref[...], dhb)
        for j in range(0 if fm else nk):
            o_ref[j * kc:(j + 1) * kc, :] += _dot_tn(a_ref[j], dhb)

        @pl.when(i == ni - 1)
        def _():
            ob_ref[...] = o_ref[...].astype(BF16)

    ospec = pl.BlockSpec((nk * kc, D), lambda i: (0, 0))
    return _run(
        body, [a_s, dh], hook, grid=(ni,), name=name, semantics=("arbitrary",),
        in_specs=[pl.BlockSpec((kc, tm), lambda i: (0, i)) if fm else pl.BlockSpec((nk, tm, kc), lambda i: (0, i, 0)),
                  pl.BlockSpec((tm, D), lambda i: (i, 0))],
        out_specs=[ospec, ospec], out_shape=[S((nk * kc, D), F32), S((nk * kc, D), BF16)])


def _dx_slot_normbwd(dy_s, wg, h, gain, dh_in, name, hook=None, fm=False, out_dtype=F32):
    ns, t, n = (1, dy_s.shape[1], dy_s.shape[0]) if fm else dy_s.shape
    tm = _tm(t)

    def body(dy_ref, w_ref, h_ref, g_ref, di_ref, o_ref, dg_ref):
        i = pl.program_id(0)

        @pl.when(i == 0)
        def _():
            dg_ref[...] = jnp.zeros_like(dg_ref)

        g = _dot_tn(dy_ref[...], w_ref[...]) if fm else _dot_nt(dy_ref[0], w_ref[0])
        for s in range(1, ns):
            g = g + _dot_nt(dy_ref[s], w_ref[s])
        hv = h_ref[...]
        r = _rstd(hv)
        gg = g * g_ref[...]
        dh_new = di_ref[...].astype(F32) + r * gg - hv * (r * r * r * jnp.mean(gg * hv, axis=-1, keepdims=True))
        o_ref[...] = dh_new.astype(o_ref.dtype)
        dg_ref[...] += jnp.sum(g * hv * r, axis=0, keepdims=True)

    row = pl.BlockSpec((tm, D), lambda i: (i, 0))
    vec = pl.BlockSpec((1, D), lambda i: (0, 0))
    return _run(
        body, [dy_s, wg, h, gain, dh_in], hook, grid=(t // tm,), name=name, semantics=("arbitrary",),
        in_specs=[pl.BlockSpec((n, tm), lambda i: (0, i)) if fm else pl.BlockSpec((ns, tm, n), lambda i: (0, i, 0)),
                  _resident(wg.shape), row, vec, row],
        out_specs=[row, vec], out_shape=[S((t, D), out_dtype), S((1, D), F32)])


def _sgu_gate_bwd(a_s, dg_s, vgain, ws, bst, name, hook=None):
    t = a_s.shape[1]
    sw = a_s.shape[2]
    gps = sw // CHUNK

    def body(a_ref, dg_ref, vg_ref, ws_ref, b_ref, da_ref, dws_ref, dbt_ref, dvg_ref, dvn_ref):
        n = pl.program_id(0)

        @pl.when(n == 0)
        def _():
            dws_ref[...] = jnp.zeros_like(dws_ref)
            dbt_ref[...] = jnp.zeros_like(dbt_ref)
            dvg_ref[...] = jnp.zeros_like(dvg_ref)

        vpre = jnp.concatenate([a_ref[4 + s].astype(F32) for s in range(4)], axis=1)
        v, v_grad = _gelu_and_grad(vpre)
        r = _rstd(v)
        vhat = v * r
        vn = (vhat * vg_ref[...]).astype(BF16)
        tri = _tril_mask()
        lane = lax.broadcasted_iota(jnp.int32, (CHUNK, CHUNK), 1)
        dbt = jnp.zeros((CHUNK, CHUNK), F32)
        for g in range(SGU_G):
            w = jnp.where(tri, ws_ref[g], 0.0).astype(BF16)
            vng = vn[:, g * CHUNK:(g + 1) * CHUNK]
            sg = _dot(w, vng) + b_ref[:, g:g + 1]
            lo = (g % gps) * CHUNK
            u, u_grad = _gelu_and_grad(a_ref[g // gps, :, lo:lo + CHUNK].astype(F32))
            dgate = dg_ref[g // gps, :, lo:lo + CHUNK].astype(F32)
            da_ref[g // gps, :, lo:lo + CHUNK] = (dgate * sg * u_grad).astype(BF16)
            ds = dgate * u
            dsb = ds.astype(BF16)
            dvn_ref[:, g * CHUNK:(g + 1) * CHUNK] = _dot_tn(w, dsb)
            dws_ref[g] += jnp.where(tri, _dot_nt(dsb, vng), 0.0)
            dbt = dbt + jnp.where(lane == g, jnp.sum(ds, axis=-1, keepdims=True), 0.0)
        dbt_ref[...] += dbt
        dvn = dvn_ref[...]
        dvg_ref[...] += jnp.sum(dvn * vhat, axis=0, keepdims=True)
        gg = dvn * vg_ref[...]
        dv = r * gg - v * (r * r * r * jnp.mean(gg * v, axis=-1, keepdims=True))
        dav = (dv * v_grad).astype(BF16)
        for s in range(4):
            da_ref[4 + s] = dav[:, s * sw:(s + 1) * sw]

    return _run(
        body, [a_s, dg_s, vgain, ws, bst], hook, grid=(t // CHUNK,), name=name, semantics=("arbitrary",),
        in_specs=[pl.BlockSpec((8, CHUNK, sw), lambda n: (0, n, 0)), pl.BlockSpec((4, CHUNK, sw), lambda n: (0, n, 0)),
                  pl.BlockSpec((1, SGU_W), lambda n: (0, 0)), pl.BlockSpec((SGU_G, CHUNK, CHUNK), lambda n: (0, 0, 0)),
                  pl.BlockSpec((CHUNK, SGU_G), lambda n: (0, 0))],
        out_specs=[pl.BlockSpec((8, CHUNK, sw), lambda n: (0, n, 0)), pl.BlockSpec((SGU_G, CHUNK, CHUNK), lambda n: (0, 0, 0)),
                   pl.BlockSpec((CHUNK, CHUNK), lambda n: (0, 0)), pl.BlockSpec((1, SGU_W), lambda n: (0, 0))],
        out_shape=[S((8, t, sw), BF16), S((SGU_G, CHUNK, CHUNK), F32), S((CHUNK, CHUNK), F32), S((1, SGU_W), F32)],
        scratch_shapes=[pltpu.VMEM((CHUNK, SGU_W), F32)])


def _attn_bwd(qkv_t, do_t, qg, kg, sinks, bias, name, hook=None):
    t = qkv_t.shape[1]
    nb = t // CHUNK

    def body(cur_ref, prev_ref, do_ref, qg_ref, kg_ref, sink_ref, bias_ref,
             o_ref, dqg_out, dkg_out, dsk_out, dbias_ref, carry, dqg_ref, dkg_ref, dsk_ref):
        n = pl.program_id(0)

        @pl.when(n == 0)
        def _():
            carry[...] = jnp.zeros_like(carry)
            dqg_ref[...] = jnp.zeros_like(dqg_ref)
            dkg_ref[...] = jnp.zeros_like(dkg_ref)
            dsk_ref[...] = jnp.zeros_like(dsk_ref)
            dbias_ref[...] = jnp.zeros_like(dbias_ref)

        @pl.when(n < nb)
        def _():
            valid = _attn_valid(n)
            o_ref[0:KV0, :] = carry[0:KV0, :].astype(BF16)
            kvs, heads = range(NKV), range(NH)
            group = lambda h: range(KVG * h, KVG * (h + 1))
            ks = [_attn_band(cur_ref, prev_ref, KV0 + HD * h) for h in kvs]
            rks = [_rstd_rows(k) for k in ks]
            khats = [k * rk for k, rk in zip(ks, rks)]
            kns = [(khat * kg_ref[...]).astype(BF16) for khat in khats]
            kn_toks = [kn.T for kn in kns]
            vbs = [_attn_band(cur_ref, prev_ref, KV0 + HD * (NKV + h)).astype(BF16) for h in kvs]
            v_toks = [vb.T for vb in vbs]
            qs = [cur_ref[HD * hq:HD * (hq + 1), :] for hq in heads]
            rqs = [_rstd_rows(q) for q in qs]
            qhats = [q * rq for q, rq in zip(qs, rqs)]
            qns = [(qhat * qg_ref[...]).astype(BF16) for qhat in qhats]
            probs = [_attn_probs(kn_toks[hq // KVG], qns[hq], bias_ref[hq], valid, sink_ref[hq]) for hq in heads]
            dohs = [do_ref[HD * hq:HD * (hq + 1), :] for hq in heads]
            dps = [_dot(v_toks[hq // KVG], dohs[hq]) for hq in heads]
            dsums = [jnp.sum(p * dp, axis=0, keepdims=True) for (p, _), dp in zip(probs, dps)]
            dss = [p * (dp - dsum) for (p, _), dp, dsum in zip(probs, dps, dsums)]
            for hq in heads:
                dsk_ref[hq:hq + 1, :] -= probs[hq][1] * dsums[hq]
                dbias_ref[hq] += dss[hq]
            dvs = [sum(_dot_nt(dohs[hq], probs[hq][0].astype(BF16)) for hq in group(h)) for h in kvs]
            dscs = [(ds * (HD ** -0.5)).astype(BF16) for ds in dss]
            dqns = [_dot(kns[hq // KVG], dscs[hq]) for hq in heads]
            dkns = [sum(_dot_nt(qns[hq], dscs[hq]) for hq in group(h)) for h in kvs]
            dqg_ref[...] += sum(dqn * qhat for dqn, qhat in zip(dqns, qhats))
            for hq in heads:
                gq = dqns[hq] * qg_ref[...]
                carry[HD * hq:HD * (hq + 1), :] = rqs[hq] * gq - qs[hq] * (
                    rqs[hq] * rqs[hq] * rqs[hq] * jnp.mean(gq * qs[hq], axis=0, keepdims=True))
            dkg_ref[...] += sum(dkn * khat for dkn, khat in zip(dkns, khats))
            for h in kvs:
                krow, vrow = KV0 + HD * h, KV0 + HD * (NKV + h)
                gk = dkns[h] * kg_ref[...]
                dk = rks[h] * gk - ks[h] * (rks[h] * rks[h] * rks[h] * jnp.mean(gk * ks[h], axis=0, keepdims=True))
                o_ref[krow:krow + HD, :] = (carry[krow:krow + HD, :] + dk[:, :CHUNK]).astype(BF16)
                o_ref[vrow:vrow + HD, :] = (carry[vrow:vrow + HD, :] + dvs[h][:, :CHUNK]).astype(BF16)
                carry[krow:krow + HD, :] = dk[:, CHUNK:]
                carry[vrow:vrow + HD, :] = dvs[h][:, CHUNK:]

        @pl.when(n == nb)
        def _():
            o_ref[...] = carry[...].astype(BF16)
            dqg_out[...] = jnp.sum(dqg_ref[...], axis=1, keepdims=True)
            dkg_out[...] = jnp.sum(dkg_ref[...], axis=1, keepdims=True)
            dsk_out[...] = jnp.sum(dsk_ref[...], axis=1, keepdims=True)

    cur = lambda n: (0, jnp.minimum(n, nb - 1))
    col = pl.BlockSpec((HD, 1), lambda n: (0, 0))
    whole = lambda shape: pl.BlockSpec(shape, lambda n: (0,) * len(shape))
    return _run(
        body, [qkv_t, qkv_t, do_t, qg, kg, sinks, bias], hook, grid=(nb + 1,), name=name, semantics=("arbitrary",),
        in_specs=[pl.BlockSpec((QKV, CHUNK), cur),
                  pl.BlockSpec((QKV - KV0, CHUNK), lambda n: (KV0 // (QKV - KV0), jnp.clip(n - 1, 0, nb - 1))),
                  pl.BlockSpec((D, CHUNK), cur), col, col, pl.BlockSpec(memory_space=pltpu.SMEM), whole((NH, 2 * CHUNK, CHUNK))],
        out_specs=[pl.BlockSpec((QKV, CHUNK), lambda n: (0, jnp.maximum(n - 1, 0))), whole((HD, 1)), whole((HD, 1)),
                   whole((NH, 1)), whole((NH, 2 * CHUNK, CHUNK))],
        out_shape=[S((QKV, t), BF16), S((HD, 1), F32), S((HD, 1), F32), S((NH, 1), F32), S((NH, 2 * CHUNK, CHUNK), F32)],
        scratch_shapes=[pltpu.VMEM((QKV, CHUNK), F32), pltpu.VMEM((HD, CHUNK), F32), pltpu.VMEM((HD, 2 * CHUNK), F32),
                        pltpu.VMEM((NH, CHUNK), F32)])


class _Plain:
    def __init__(self, wg):
        self.full, self.grads = wg, {}

    def w(self, n):
        return self.full[n]

    def hook(self, host):
        return None

    def grad(self, n, pair):
        self.grads[n] = pair

    def small(self, g_rep):
        pass

    def loss(self, value):
        pass


def _local_step(x, target, rep, sch):
    bucket_row = jnp.asarray(_rel_tables().T.reshape(1, -1))
    bias = _relbias_fwd(rep["rel_bias"].T, bucket_row, "relbias_fwd").reshape(NH, 2 * CHUNK, CHUNK)
    bst = rep["sgu_b_s"][0].T
    ws = rep["sgu_w_s"][0]
    vgain = rep["sgu_v_gain"]
    qg, kg, sinks = rep["attn_q_gain"].reshape(HD, 1), rep["attn_k_gain"].reshape(HD, 1), rep["attn_sinks"][0]
    w_down = lambda l: sch.w("ffn_w_down%d" % l).reshape(D_FF, D)
    w_up = lambda l: sch.w("ffn_w_up%d" % l)
    cw = [sch.w("ffn_conv_w")[:, 3 * l:3 * l + 3] for l in range(2)]
    cb = [rep["ffn_conv_b"][l].reshape(8, 1, -1) for l in range(2)]
    mixg = [rep["mix_norm"][l:l + 1] for l in range(2)]
    ffng = [rep["ffn_norm"][l:l + 1] for l in range(2)]
    rows = lambda pair: tuple(g.reshape(N_DEV, -1, D) for g in pair)
    hk = sch.hook

    hn0 = _rmsnorm(x, mixg[0], "norm0")
    a0 = _mm_slot(hn0, sch.w("sgu_w_in"), BF16, "sgu_in", hk("sgu_in"))
    gated = _sgu_gate_fwd(a0, vgain, ws, bst, "sgu_gate", hk("sgu_gate"))
    h1, hn1 = _resid_mm(gated, sch.w("sgu_w_out").reshape(SGU_W, D), x, ffng[0], "norm", "sgu_out", hk("sgu_out"))
    a_ff0, c_ff0, h2, hn2 = _ffn_fwd(hn1, h1, w_up(0), w_down(0), cw[0], cb[0], mixg[1], "norm", "ffn0_fwd", hk("ffn0_fwd"))
    qkv = _mm_t(hn2, sch.w("attn_w_qkv"), "qkv", hk("qkv"))
    o = _attn_fwd(qkv, qg, kg, sinks, bias, "attn", hk("attn"))
    h3, hn3 = _resid_mm(o, sch.w("attn_w_o").reshape(D, D), h2, ffng[1], "norm", "attn_out", hk("attn_out"), fm=True)
    a_ff1, c_ff1, dy, sq = _ffn_fwd(hn3, h3, w_up(1), w_down(1), cw[1], cb[1], target, "loss", "ffn1_fwd_loss", hk("ffn1_fwd_loss"))
    loss = (0.5 / D) * jnp.sum(sq[:, 0, 0])
    sch.loss(loss)

    def ffn_bwd(dh, h_in, hn, a, c, l, tag):
        dc, g_down, g_down_b = _ffn_bwd1(dh, c, w_down(l), tag + "_bwd1", hk(tag + "_bwd1"))
        sch.grad("ffn_w_down%d" % l, rows((g_down, g_down_b)))
        da, dh_new, dgain, g_cw, g_cb = _ffn_bwd2(dc, a, w_up(l), cw[l], h_in, ffng[l], dh, tag + "_bwd2", hk(tag + "_bwd2"))
        sch.grad("ffn_w_up%d" % l, _dw_slot(hn, da, tag + "_dw_up", hk(tag + "_dw_up")))
        return dh_new, dgain, g_cw, g_cb.reshape(-1)

    dh, d_ffng1, g_cw1, g_cb1 = ffn_bwd(dy, h3, hn3, a_ff1, c_ff1, 1, "ffn1")
    do = _dx_rows_t(dh, sch.w("attn_w_o").reshape(D, D), "attn_do", hk("attn_do"))
    sch.grad("attn_w_o", rows(_dw_rows(o, dh, "dw_o", hk("dw_o"), fm=True)))
    dqkv, d_qg, d_kg, d_sk, d_bias = _attn_bwd(qkv, do, qg, kg, sinks, bias, "attn_bwd", hk("attn_bwd"))
    sch.grad("attn_w_qkv", tuple(g.reshape(N_DEV, -1, D) for g in _dw_rows(dqkv, hn2, "dw_qkv", hk("dw_qkv"), fm=True)))
    dh, d_mixg1 = _dx_slot_normbwd(dqkv, sch.w("attn_w_qkv").reshape(QKV, D), h2, mixg[1], dh, "dx_qkv", hk("dx_qkv"), fm=True,
                                   out_dtype=DH)
    d_relb = _relbias_bwd(d_bias.reshape(NH, -1), bucket_row, "relbias_bwd").T
    g_rep = {"attn_q_gain": d_qg.reshape(1, HD), "attn_k_gain": d_kg.reshape(1, HD), "attn_sinks": d_sk.reshape(1, NH),
             "rel_bias": d_relb}
    sch.small(g_rep)
    dh, d_ffng0, g_cw0, g_cb0 = ffn_bwd(dh, h1, hn1, a_ff0, c_ff0, 0, "ffn0")
    g_cw = jnp.concatenate([g_cw0, g_cw1], axis=1)
    sch.grad("ffn_conv_w", (g_cw, g_cw.astype(BF16)))
    g_ffn = {"ffn_norm": jnp.concatenate([d_ffng0, d_ffng1], axis=0), "ffn_conv_b": jnp.stack([g_cb0, g_cb1], axis=0)}
    sch.small(g_ffn)
    dgated = _dx_rows(dh, sch.w("sgu_w_out").reshape(SGU_W, D), SGU_W // 4, BF16, "sgu_dgated", hk("sgu_dgated"))
    sch.grad("sgu_w_out", rows(_dw_rows(gated, dh, "dw_sgu_out", hk("dw_sgu_out"))))
    da0, d_ws, d_bst, d_vgain = _sgu_gate_bwd(a0, dgated, vgain, ws, bst, "sgu_gate_bwd", hk("sgu_gate_bwd"))
    grad_x, d_mixg0 = _dx_slot_normbwd(da0, sch.w("sgu_w_in"), x, mixg[0], dh, "dx_sgu_in")
    g_sgu = {"sgu_v_gain": d_vgain, "sgu_w_s": d_ws[None], "sgu_b_s": d_bst[:, :SGU_G].T[None],
             "mix_norm": jnp.concatenate([d_mixg0, d_mixg1], axis=0)}
    sch.small(g_sgu)
    g_rep.update(g_ffn)
    g_rep.update(g_sgu)
    sch.grad("sgu_w_in", _dw_slot(hn0, da0, "dw_sgu_in", hk("dw_sgu_in")))
    return loss, grad_x, g_rep


def _allgather(xs, name):
    nt = len(xs)

    def body(*refs):
        x_refs, o_refs = refs[:nt], refs[nt:2 * nt]
        send_sems, recv_sems, local_sems = refs[2 * nt:]
        x, y, c, chips = _place()
        me, sibling = (x, y, c), (x, y, 1 - c)

        def copy(t, k, block, to, src=None):
            px, py, pc = block
            dst = o_refs[t].at[4 * px + 2 * py + pc]
            return pltpu.make_async_remote_copy(
                src_ref=dst if src is None else src, dst_ref=dst, send_sem=send_sems.at[t, k], recv_sem=recv_sems.at[t, k],
                device_id=to, device_id_type=MESH)

        mine = [pltpu.make_async_copy(x_refs[t], o_refs[t].at[4 * x + 2 * y + c], local_sems.at[t]) for t in range(nt)]
        for cp in mine:
            cp.start()
        first = []
        for t in range(nt):
            first.append(copy(t, 0, me, sibling, src=x_refs[t]))
            first += [copy(t, 1 + j, me, (*chip, c), src=x_refs[t]) for j, chip in enumerate(chips)]
        for cp in first:
            cp.start()
        passed = []
        for j, chip in enumerate(chips):
            for t in range(nt):
                copy(t, 1 + j, (*chip, c), me).wait_recv()
                fwd = copy(t, 4 + j, (*chip, c), sibling)
                fwd.start()
                passed.append(fwd)
        for t in range(nt):
            copy(t, 0, sibling, me).wait_recv()
            for j, chip in enumerate(chips):
                copy(t, 4 + j, (*chip, 1 - c), me).wait_recv()
        for cp in first + passed:
            cp.wait_send()
        for cp in mine:
            cp.wait()

    return pl.pallas_call(
        body, name=name, in_specs=[ANY] * nt, out_specs=[ANY] * nt,
        out_shape=[S((N_DEV,) + a.shape, a.dtype) for a in xs],
        scratch_shapes=[pltpu.SemaphoreType.DMA((nt, 7)), pltpu.SemaphoreType.DMA((nt, 7)), pltpu.SemaphoreType.DMA((nt,))],
        compiler_params=pltpu.CompilerParams(has_side_effects=True))(*xs)


def _exchange(hook, name):
    comm = hook()
    ci, co = len(comm.inputs), len(comm.out_shapes)

    def body(*refs):
        cins, couts = refs[:ci], refs[ci:ci + co]
        send, recv = refs[-2:]
        comm.start(cins, couts, send, recv)
        comm.finish(cins, couts, send, recv)

    res = pl.pallas_call(
        body, name=name, in_specs=[ANY] * ci, out_specs=[ANY] * co, out_shape=comm.out_shapes,
        scratch_shapes=[pltpu.SemaphoreType.DMA((comm.n_sems,)), pltpu.SemaphoreType.DMA((comm.n_sems,))],
        input_output_aliases=dict(comm.aliases),
        compiler_params=pltpu.CompilerParams(has_side_effects=True))(*comm.inputs)
    hook(res)


def _row_tile(r):
    tr = r if r <= ROW_TILE or r % ROW_TILE else ROW_TILE
    assert r % tr == 0
    return tr


def _rs_partial(g32, sib, place, name):
    _, r, cdim = g32.shape
    tr = _row_tile(r)

    def body(place_ref, g_ref, s_ref, p_ref, own_ref):
        k = pl.program_id(1)
        tot = g_ref[...] + s_ref[...].astype(F32)
        p_ref[...] = tot.astype(BF16)

        @pl.when(k == place_ref[1])
        def _():
            own_ref[...] = tot

    grid_spec = pltpu.PrefetchScalarGridSpec(
        num_scalar_prefetch=1, grid=(r // tr, 4),
        in_specs=[pl.BlockSpec((None, None, tr, cdim), lambda i, k, pr: (k, pr[0], i, 0)),
                  pl.BlockSpec((None, tr, cdim), lambda i, k, pr: (k, i, 0))],
        out_specs=[pl.BlockSpec((None, tr, cdim), lambda i, k, pr: (k, i, 0)), pl.BlockSpec((tr, cdim), lambda i, k, pr: (i, 0))])
    return pl.pallas_call(
        body, grid_spec=grid_spec, name=name,
        out_shape=[S((4, r, cdim), BF16), S((r, cdim), F32)],
        compiler_params=_cp("parallel", "arbitrary"))(place, g32.reshape(4, 2, r, cdim), sib)


def _adamw_math(w, g, m, v):
    m = ADAM_B1 * m + (1.0 - ADAM_B1) * g
    v = ADAM_B2 * v + (1.0 - ADAM_B2) * (g * g)
    m_hat = m / (1.0 - ADAM_B1 ** ADAM_STEP)
    v_hat = v / (1.0 - ADAM_B2 ** ADAM_STEP)
    delta = -ADAM_LR * (m_hat / (jnp.sqrt(v_hat) + ADAM_EPS) + ADAM_WD * w)
    return delta, m, v


def _adamw_shard(owns, recvs, w, m, v, name, flipped=False):
    nl = w.shape[0]
    r, cdim = owns[0].shape
    tr = _row_tile(r)
    nr = r // tr

    def body(*refs):
        own_refs, recv_refs = refs[:nl], refs[nl:2 * nl]
        w_ref, m_ref, v_ref, g_out, d_out, m_out, v_out = refs[2 * nl:]
        layer = pl.program_id(0)
        g = None
        for l in range(nl):
            gl = own_refs[l][...] + recv_refs[l][0].astype(F32) + recv_refs[l][1].astype(F32) + recv_refs[l][2].astype(F32)
            g = gl if g is None else jnp.where(layer == l, gl, g)
        if flipped:
            g = g.T
        g_out[...] = g
        d_out[...], m_out[...], v_out[...] = _adamw_math(w_ref[...], g, m_ref[...], v_ref[...])

    park = lambda l: (lambda layer, i: (jnp.where(layer == l, i, jnp.where(layer < l, 0, nr - 1)), 0))
    park3 = lambda l: (lambda layer, i: (0, jnp.where(layer == l, i, jnp.where(layer < l, 0, nr - 1)), 0))
    if flipped:
        row = pl.BlockSpec((None, cdim, tr), lambda layer, i: (layer, 0, i))
    else:
        row = pl.BlockSpec((None, tr, cdim), lambda layer, i: (layer, i, 0))
    return pl.pallas_call(
        body, grid=(nl, nr), name=name,
        in_specs=[pl.BlockSpec((tr, cdim), park(l)) for l in range(nl)] + [pl.BlockSpec((3, tr, cdim), park3(l)) for l in range(nl)]
        + [row, row, row],
        out_specs=[row] * 4, out_shape=[S(w.shape, F32)] * 4,
        compiler_params=_cp("arbitrary", "arbitrary"))(*owns, *recvs, w, m, v)


def _adamw_small(galls, ws, ms, vs, name):
    n = len(galls)

    def body(*refs):
        g_refs, w_refs, m_refs, v_refs, outs = refs[:n], refs[n:2 * n], refs[2 * n:3 * n], refs[3 * n:4 * n], refs[4 * n:]
        for i in range(n):
            g = g_refs[i][0].astype(F32)
            for s in range(1, N_DEV):
                g = g + g_refs[i][s].astype(F32)
            outs[i][...] = g
            outs[n + i][...], outs[2 * n + i][...], outs[3 * n + i][...] = _adamw_math(w_refs[i][...], g, m_refs[i][...], v_refs[i][...])

    res = pl.pallas_call(body, out_shape=[S(a.shape, F32) for a in ws] * 4, name=name)(*galls, *ws, *ms, *vs)
    return [res[k * n:(k + 1) * n] for k in range(4)]


REPLICATED = ["mix_norm", "ffn_norm", "sgu_v_gain", "sgu_w_s", "sgu_b_s", "attn_q_gain", "attn_k_gain", "attn_sinks", "rel_bias",
              "ffn_conv_b"]
WEIGHTS = ["mix_norm", "ffn_norm", "sgu_w_in", "sgu_v_gain", "sgu_w_s", "sgu_b_s", "sgu_w_out", "attn_w_qkv", "attn_q_gain",
           "attn_k_gain", "attn_sinks", "attn_w_o", "rel_bias", "ffn_w_up", "ffn_conv_w", "ffn_conv_b", "ffn_w_down"]
SMALL = ["g_" + n for n in REPLICATED]
BF16_TRANSIT = {"sgu_w_s"}
SMALL_ATTN = ["g_attn_q_gain", "g_attn_k_gain", "g_attn_sinks", "g_rel_bias"]
SMALL_FFN = ["g_ffn_norm", "g_ffn_conv_b"]
SMALL_LATE = [n for n in SMALL if n not in SMALL_ATTN + SMALL_FFN]

GATHER_FIRST = ["sgu_w_in", "ffn_conv_w"]
PLAN = {
    "sgu_in": [("ag1", "sgu_w_out"), ("ag1", "ffn_w_down0")],
    "sgu_gate": [("ag2", "sgu_w_out"), ("ag2", "ffn_w_down0"), ("ag1", "ffn_w_up0")],
    "sgu_out": [("ag2", "ffn_w_up0"), ("ag1", "attn_w_qkv")],
    "ffn0_fwd": [("ag2", "attn_w_qkv"), ("ag1", "attn_w_o"), ("ag1", "ffn_w_up1")],
    "qkv": [("ag2", "attn_w_o"), ("ag2", "ffn_w_up1")],
    "attn": [("ag1", "ffn_w_down1")],
    "attn_out": [("ag2", "ffn_w_down1")],
    "ffn1_bwd2": [("rs1", "ffn_w_down1")],
    "ffn1_dw_up": [("rs2", "ffn_w_down1")],
    "attn_do": [("rs1", "ffn_w_up1")],
    "attn_bwd": [("rs2", "ffn_w_up1"), ("rs1", "attn_w_o")],
    "dw_qkv": [("rs2", "attn_w_o")],
    "dx_qkv": [("rs1", "attn_w_qkv")],
    "ffn0_bwd1": [("rs2", "attn_w_qkv")] + [("ag1", n) for n in SMALL_ATTN + ["g_loss"]],
    "ffn0_bwd2": [("rs1", "ffn_w_down0")] + [("ag2", n) for n in SMALL_ATTN + ["g_loss"]],
    "ffn0_dw_up": [("rs2", "ffn_w_down0")],
    "sgu_dgated": [("rs1", "ffn_w_up0")] + [("ag1", n) for n in SMALL_FFN],
    "dw_sgu_out": [("ag2", n) for n in SMALL_FFN],
    "sgu_gate_bwd": [("rs2", "ffn_w_up0"), ("rs1", "sgu_w_out")],
    "dw_sgu_in": [("rs2", "sgu_w_out")] + [("ag1", n) for n in SMALL_LATE],
    "last_a": [("rs1", "sgu_w_in"), ("rs1", "ffn_conv_w")] + [("ag2", n) for n in SMALL_LATE],
    "last_b": [("rs2", "sgu_w_in"), ("rs2", "ffn_conv_w")],
}


class _Overlap:
    def __init__(self, shard, place):
        self.shard, self.place = shard, place
        self.part, self.full = {}, {}
        self.grads, self.sib, self.own, self.recv = {}, {}, {}, {}

    def w(self, n):
        return self.full[n]

    def grad(self, n, pair):
        self.grads[n] = pair

    def small(self, g_rep):
        self.shard.update(("g_" + n, a.astype(BF16) if n in BF16_TRANSIT else a) for n, a in _views2d(g_rep).items())

    def loss(self, value):
        self.shard["g_loss"] = jnp.full((8, 128), value, F32)

    def chip_sums(self, n):
        sums, self.own[n] = _rs_partial(self.grads[n][0], self.sib.pop(n), self.place, "rs_partial_" + n)
        return sums

    def hook(self, host):
        ops = PLAN.get(host)
        if not ops:
            return None
        where = {"ag1": self.part, "ag2": self.full, "rs1": self.sib, "rs2": self.recv}
        idx = []

        def hook(results=None):
            if results is not None:
                for (kind, n), i in zip(ops, idx):
                    where[kind][n] = results[i]
                return None
            comm = _Comm()
            for kind, n in ops:
                arr = {"ag1": lambda: self.shard[n], "ag2": lambda: self.part.pop(n), "rs1": lambda: self.grads[n][1],
                       "rs2": lambda: self.chip_sums(n)}[kind]()
                idx.append(comm.add(kind, arr))
            return comm

        return hook


TRANSPOSED = {"attn_w_qkv"}
PHYSICAL_T = {"ffn_w_up"}
SHARDED = {
    "sgu_w_in": ["sgu_w_in"], "sgu_w_out": ["sgu_w_out"], "attn_w_qkv": ["attn_w_qkv"], "attn_w_o": ["attn_w_o"],
    "ffn_w_up": ["ffn_w_up0", "ffn_w_up1"], "ffn_w_down": ["ffn_w_down0", "ffn_w_down1"], "ffn_conv_w": ["ffn_conv_w"],
}


def _send_views(w):
    out = {"ffn_conv_w": w["ffn_conv_w"].reshape(6, -1)}
    for name, parts in SHARDED.items():
        if name != "ffn_conv_w":
            out.update((p, (w[name][l].T if name in TRANSPOSED else w[name][l]).astype(BF16)) for l, p in enumerate(parts))
    return out


def _views2d(d):
    return {n: d[n].reshape(-1, d[n].shape[-1]) for n in REPLICATED if n in d}


def kernel(x, mix_norm, ffn_norm, sgu_w_in, sgu_v_gain, sgu_w_s, sgu_b_s, sgu_w_out, attn_w_qkv, attn_q_gain, attn_k_gain, attn_sinks, attn_w_o, rel_bias, ffn_w_up, ffn_conv_w, ffn_conv_b, ffn_w_down, loss_target, m_mix_norm, m_ffn_norm, m_sgu_w_in, m_sgu_v_gain, m_sgu_w_s, m_sgu_b_s, m_sgu_w_out, m_attn_w_qkv, m_attn_q_gain, m_attn_k_gain, m_attn_sinks, m_attn_w_o, m_rel_bias, m_ffn_w_up, m_ffn_conv_w, m_ffn_conv_b, m_ffn_w_down, v_mix_norm, v_ffn_norm, v_sgu_w_in, v_sgu_v_gain, v_sgu_w_s, v_sgu_b_s, v_sgu_w_out, v_attn_w_qkv, v_attn_q_gain, v_attn_k_gain, v_attn_sinks, v_attn_w_o, v_rel_bias, v_ffn_w_up, v_ffn_conv_w, v_ffn_conv_b, v_ffn_w_down):
    w = dict(zip(WEIGHTS, (mix_norm, ffn_norm, sgu_w_in, sgu_v_gain, sgu_w_s, sgu_b_s, sgu_w_out, attn_w_qkv, attn_q_gain, attn_k_gain,
                           attn_sinks, attn_w_o, rel_bias, ffn_w_up, ffn_conv_w, ffn_conv_b, ffn_w_down)))
    m = dict(zip(WEIGHTS, (m_mix_norm, m_ffn_norm, m_sgu_w_in, m_sgu_v_gain, m_sgu_w_s, m_sgu_b_s, m_sgu_w_out, m_attn_w_qkv, m_attn_q_gain,
                           m_attn_k_gain, m_attn_sinks, m_attn_w_o, m_rel_bias, m_ffn_w_up, m_ffn_conv_w, m_ffn_conv_b, m_ffn_w_down)))
    v = dict(zip(WEIGHTS, (v_mix_norm, v_ffn_norm, v_sgu_w_in, v_sgu_v_gain, v_sgu_w_s, v_sgu_b_s, v_sgu_w_out, v_attn_w_qkv, v_attn_q_gain,
                           v_attn_k_gain, v_attn_sinks, v_attn_w_o, v_rel_bias, v_ffn_w_up, v_ffn_conv_w, v_ffn_conv_b, v_ffn_w_down)))
    rep = {n: w[n] for n in REPLICATED}

    xi, yi, ci = lax.axis_index("x"), lax.axis_index("y"), lax.axis_index("c")
    place = jnp.stack([ci, 2 * xi + yi]).astype(jnp.int32)
    sch = _Overlap(_send_views(w), place)
    sch.full.update(zip(GATHER_FIRST, _allgather([sch.shard[n] for n in GATHER_FIRST], "gather_first")))

    _, grad_x, g_rep = _local_step(x[0], loss_target[0], rep, sch)
    loss = jnp.sum(sch.full["g_loss"][:, 0, 0])
    _exchange(sch.hook("last_a"), "last_a")
    _exchange(sch.hook("last_b"), "last_b")

    out = [{}, {}, {}, {}]
    for name, parts in SHARDED.items():
        flip = (lambda a: jnp.swapaxes(a, -1, -2)) if name in TRANSPOSED | PHYSICAL_T else (lambda a: a)
        shape = flip(w[name]).shape
        as3d = lambda a: flip(a).reshape(len(parts), -1, shape[-1])
        res = _adamw_shard([sch.own[p] for p in parts], [sch.recv[p] for p in parts], as3d(w[name]), as3d(m[name]), as3d(v[name]),
                           "adamw_" + name, flipped=name in PHYSICAL_T)
        for o, r in zip(out, res):
            o[name] = flip(r.reshape(shape))
    small = _adamw_small([sch.full[n] for n in SMALL], *[list(_views2d(d).values()) for d in (rep, m, v)], "adamw_small")
    for o, res in zip(out, small):
        o.update((n, r.reshape(w[n].shape)) for n, r in zip(REPLICATED, res))

    return (loss, grad_x[None], *[out[0][n] for n in WEIGHTS], *[out[1][n] for n in WEIGHTS],
            *[out[2][n] for n in WEIGHTS], *[out[3][n] for n in WEIGHTS])
```

```python
import functools
import math

import numpy as np
import jax
import jax.numpy as jnp
from jax import lax
from jax.experimental import pallas as pl
from jax.experimental.pallas import tpu as pltpu

F32 = jnp.float32
BF16 = jnp.bfloat16
DH = jnp.bfloat16
S = jax.ShapeDtypeStruct

D = 1024
CHUNK = 128
SGU_W = 2048
SGU_G = 16
HD = 64
NH = 16
NKV = 4
KVG = 4
D_FF = 2816
REL_BUCKETS = 32
REL_MAX_DIST = 128
EPS = 1e-6
N_DEV = 8
MESH = pl.DeviceIdType.MESH

ADAM_LR = 0.001
ADAM_B1 = 0.9
ADAM_B2 = 0.999
ADAM_EPS = 1e-08
ADAM_WD = 0.01
ADAM_STEP = 10

ROW_TILE = 512
HALO = 8
FFN_ROWS = 256


def _tm(t):
    return min(ROW_TILE, t)


def _cp(*sem):
    return pltpu.CompilerParams(dimension_semantics=sem)


ANY = pl.BlockSpec(memory_space=pl.ANY)


def _place():
    x, y, c = lax.axis_index("x"), lax.axis_index("y"), lax.axis_index("c")
    return x, y, c, [(1 - x, y), (x, 1 - y), (1 - x, 1 - y)]


class _Comm:
    SEMS = {"ag1": 5, "ag2": 3, "rs1": 4, "rs2": 3}

    def __init__(self):
        self.inputs, self.out_shapes, self.aliases, self.ops, self.n_sems = [], [], {}, [], 0

    def add(self, kind, arr):
        lead = {"ag1": N_DEV, "ag2": None, "rs1": 4, "rs2": 3}[kind]
        shape = arr.shape if lead is None else (lead,) + arr.shape[(0 if kind == "ag1" else 1):]
        if kind == "ag2":
            self.aliases[len(self.inputs)] = len(self.out_shapes)
        self.ops.append((kind, len(self.inputs), len(self.out_shapes), self.n_sems))
        self.inputs.append(arr)
        self.out_shapes.append(S(shape, arr.dtype))
        self.n_sems += self.SEMS[kind]
        return len(self.out_shapes) - 1

    def _copies(self, ins, outs, send, recv):
        x, y, c, chips = _place()
        me, sibling = (x, y, c), (x, y, 1 - c)
        slot = lambda px, py, pc: 4 * px + 2 * py + pc
        sends, recvs, local = [], [], []

        def rc(src, dst, k, to):
            return lambda: pltpu.make_async_remote_copy(src_ref=src(), dst_ref=dst(), send_sem=send.at[k], recv_sem=recv.at[k],
                                                        device_id=to, device_id_type=MESH)

        for kind, ii, oi, b in self.ops:
            src, dst = ins[ii], outs[oi]
            at = lambda ref, i: (lambda: ref.at[i])
            if kind == "ag1":
                whole, mine = (lambda s=src: s), at(dst, slot(*me))
                sends.append(rc(whole, mine, b, sibling))
                recvs.append(rc(whole, at(dst, slot(x, y, 1 - c)), b, me))
                for j, chip in enumerate(chips):
                    sends.append(rc(whole, mine, b + 1 + j, (*chip, c)))
                    recvs.append(rc(whole, at(dst, slot(*chip, c)), b + 1 + j, me))
                local.append(lambda s=src, m=mine, k=b + 4: pltpu.make_async_copy(s, m(), send.at[k]))
            elif kind == "ag2":
                for j, chip in enumerate(chips):
                    sends.append(rc(at(dst, slot(*chip, c)), at(dst, slot(*chip, c)), b + j, sibling))
                    recvs.append(rc(at(dst, slot(*chip, 1 - c)), at(dst, slot(*chip, 1 - c)), b + j, me))
            elif kind == "rs1":
                for k in range(4):
                    sends.append(rc(at(src, 2 * k + (1 - c)), at(dst, k), b + k, sibling))
                    recvs.append(rc(at(src, 2 * k + c), at(dst, k), b + k, me))
            else:
                for j, (px, py) in enumerate(chips):
                    sends.append(rc(at(src, 2 * px + py), at(dst, j), b + j, (px, py, c)))
                    recvs.append(rc(at(src, 2 * px + py), at(dst, j), b + j, me))
        return sends, recvs, local

    def start(self, ins, outs, send, recv):
        sends, _, local = self._copies(ins, outs, send, recv)
        for make in local + sends:
            make().start()

    def finish(self, ins, outs, send, recv):
        sends, recvs, local = self._copies(ins, outs, send, recv)
        for make in recvs:
            make().wait_recv()
        for make in sends:
            make().wait_send()
        for make in local:
            make().wait()


def _run(body, args, hook, *, grid, in_specs, out_specs, out_shape, name, semantics, scratch_shapes=(), aliases=None):
    comm = hook() if hook is not None else None
    aliases = dict(aliases or {})
    if comm is None:
        return pl.pallas_call(body, grid=grid, in_specs=in_specs, out_specs=out_specs, out_shape=out_shape, name=name,
                              scratch_shapes=list(scratch_shapes), input_output_aliases=aliases,
                              compiler_params=_cp(*semantics))(*args)
    single = not isinstance(out_shape, (list, tuple))
    out_shapes = [out_shape] if single else list(out_shape)
    out_specs_l = [out_specs] if single else list(out_specs)
    n_in, n_out, n_scr, ci, co = len(args), len(out_shapes), len(scratch_shapes), len(comm.inputs), len(comm.out_shapes)

    def wrapped(*refs):
        ins, cins = refs[:n_in], refs[n_in:n_in + ci]
        outs, couts = refs[n_in + ci:n_in + ci + n_out], refs[n_in + ci + n_out:n_in + ci + n_out + co]
        scr = refs[n_in + ci + n_out + co:n_in + ci + n_out + co + n_scr]
        send, recv = refs[-2:]
        first = functools.reduce(lambda a, b: a & b, [pl.program_id(a) == 0 for a in range(len(grid))])
        last = functools.reduce(lambda a, b: a & b, [pl.program_id(a) == g - 1 for a, g in enumerate(grid)])

        @pl.when(first)
        def _():
            comm.start(cins, couts, send, recv)

        body(*ins, *outs, *scr)

        @pl.when(last)
        def _():
            comm.finish(cins, couts, send, recv)

    res = pl.pallas_call(
        wrapped, grid=grid, in_specs=list(in_specs) + [ANY] * ci, out_specs=out_specs_l + [ANY] * co,
        out_shape=out_shapes + comm.out_shapes, name=name,
        scratch_shapes=list(scratch_shapes) + [pltpu.SemaphoreType.DMA((comm.n_sems,)), pltpu.SemaphoreType.DMA((comm.n_sems,))],
        input_output_aliases={**aliases, **{n_in + k: n_out + v for k, v in comm.aliases.items()}},
        compiler_params=pltpu.CompilerParams(dimension_semantics=("arbitrary",) * len(grid), has_side_effects=True))(*args, *comm.inputs)
    hook(res[n_out:])
    return res[0] if single else list(res[:n_out])


def _dot(a, b):
    return jnp.dot(a, b, preferred_element_type=F32)


def _dot_nt(a, b):
    return lax.dot_general(a, b, (((1,), (1,)), ((), ())), preferred_element_type=F32)


def _dot_tn(a, b):
    return lax.dot_general(a, b, (((0,), (0,)), ((), ())), preferred_element_type=F32)


def _gelu(x):
    return 0.5 * x * (1.0 + lax.erf(x * (2.0 ** -0.5)))


def _gelu_and_grad(x):
    cdf = 0.5 * (1.0 + lax.erf(x * (2.0 ** -0.5)))
    return x * cdf, cdf + x * jnp.exp(-0.5 * x * x) * (1.0 / math.sqrt(2.0 * math.pi))


def _sigmoid(x):
    return 1.0 / (1.0 + jnp.exp(-x))


def _rstd(x):
    return lax.rsqrt(jnp.mean(x * x, axis=-1, keepdims=True) + EPS)


def _rel_tables():
    q = np.arange(CHUNK)[:, None] + CHUNK
    k = np.arange(2 * CHUNK)[None, :]
    dist = q - k
    n = np.maximum(dist, 0)
    max_exact = REL_BUCKETS // 2
    large = max_exact + (np.log(np.maximum(n, 1).astype(np.float32) / max_exact)
                         / math.log(REL_MAX_DIST / max_exact) * (REL_BUCKETS - max_exact)).astype(np.int32)
    large = np.minimum(large, REL_BUCKETS - 1)
    return np.where(n < max_exact, n, large).astype(np.int32)


def _rmsnorm(x, gain, name):
    t = x.shape[0]
    tm = _tm(t)

    def body(x_ref, g_ref, o_ref):
        xv = x_ref[...]
        o_ref[...] = (xv * _rstd(xv) * g_ref[...]).astype(BF16)

    return pl.pallas_call(
        body, grid=(t // tm,), name=name,
        in_specs=[pl.BlockSpec((tm, D), lambda i: (i, 0)), pl.BlockSpec((1, D), lambda i: (0, 0))],
        out_specs=pl.BlockSpec((tm, D), lambda i: (i, 0)),
        out_shape=S((t, D), BF16), compiler_params=_cp("parallel"))(x, gain)


def _resident(shape):
    zeros = (0,) * len(shape)
    return pl.BlockSpec(shape, lambda *_: zeros, pipeline_mode=pl.Buffered(1))


def _mm_slot(hn, wg, out_dtype, name, hook=None):
    t, k = hn.shape
    ns, _, n = wg.shape
    tm = _tm(t)

    def body(a_ref, w_ref, o_ref):
        a = a_ref[...]
        for s in range(ns):
            o_ref[s] = _dot(a, w_ref[s]).astype(out_dtype)

    return _run(
        body, [hn, wg], hook, grid=(t // tm,), name=name, semantics=("parallel",),
        in_specs=[pl.BlockSpec((tm, k), lambda i: (i, 0)), _resident(wg.shape)],
        out_specs=pl.BlockSpec((ns, tm, n), lambda i: (0, i, 0)), out_shape=S((ns, t, n), out_dtype))


def _mm_t(hn, wt, name, hook=None):
    t, k = hn.shape
    ns, n, _ = wt.shape
    tm = _tm(t)

    def body(a_ref, w_ref, o_ref):
        a = a_ref[...]
        for s in range(ns):
            o_ref[s * n:(s + 1) * n, :] = _dot_nt(w_ref[s], a)

    return _run(
        body, [hn, wt], hook, grid=(t // tm,), name=name, semantics=("parallel",),
        in_specs=[pl.BlockSpec((tm, k), lambda i: (i, 0)), _resident(wt.shape)],
        out_specs=pl.BlockSpec((ns * n, tm), lambda i: (0, i)), out_shape=S((ns * n, t), F32))


def _conv3(a, prev, cw, cb, tm):
    ext = jnp.concatenate([prev, a], axis=0)
    return cw[2:3] * a + cw[1:2] * ext[HALO - 1:HALO - 1 + tm] + cw[0:1] * ext[HALO - 2:HALO - 2 + tm] + cb


def _ffn_fwd(hn, h, wup, wdown, cw, cb, extra, mode, name, hook=None):
    t, k = hn.shape
    n = wup.shape[-1]
    nh = wup.shape[0] // 2
    tm = min(FFN_ROWS, t)
    ni = t // tm

    def body(a_ref, h_ref, wu_ref, wd_ref, cw_ref, cb_ref, e_ref, as_ref, cs_ref, o1_ref, o2_ref, carry):
        i = pl.program_id(0)

        @pl.when(i == 0)
        def _():
            carry[...] = jnp.zeros_like(carry)

        a = a_ref[...]
        acc = h_ref[...]
        nxt = (_dot(a, wu_ref[0]), _dot(a, wu_ref[nh]))
        for j in range(nh):
            ag, av = nxt
            if j + 1 < nh:
                nxt = (_dot(a, wu_ref[j + 1]), _dot(a, wu_ref[nh + j + 1]))
            as_ref[j] = ag.astype(BF16)
            as_ref[nh + j] = av.astype(BF16)
            cg = _conv3(ag, carry[j], cw_ref[j], cb_ref[j], tm)
            cv = _conv3(av, carry[nh + j], cw_ref[nh + j], cb_ref[nh + j], tm)
            carry[j] = ag[tm - HALO:]
            carry[nh + j] = av[tm - HALO:]
            cs_ref[j] = cg.astype(BF16)
            cs_ref[nh + j] = cv.astype(BF16)
            act = (cg * _sigmoid(cg) * cv).astype(BF16)
            acc = acc + _dot(act, wd_ref[j * n:(j + 1) * n, :])
        if mode == "norm":
            o1_ref[...] = acc
            o2_ref[...] = (acc * _rstd(acc) * e_ref[...]).astype(BF16)
        else:
            err = acc - e_ref[...]
            o1_ref[...] = (err * (1.0 / D)).astype(o1_ref.dtype)
            o2_ref[...] = jnp.full(o2_ref.shape, jnp.sum(err * err), F32)

    row = pl.BlockSpec((tm, D), lambda i: (i, 0))
    if mode == "norm":
        e_spec, o2_spec, o2_shape = pl.BlockSpec((1, D), lambda i: (0, 0)), row, S((t, D), BF16)
    else:
        e_spec, o2_spec, o2_shape = row, pl.BlockSpec((None, 8, 128), lambda i: (i, 0, 0)), S((ni, 8, 128), F32)
    aspec = pl.BlockSpec((2 * nh, tm, n), lambda i: (0, i, 0))
    return _run(
        body, [hn, h, wup, wdown, cw, cb, extra], hook, grid=(ni,), name=name, semantics=("arbitrary",),
        in_specs=[pl.BlockSpec((tm, k), lambda i: (i, 0)), row, _resident(wup.shape), _resident(wdown.shape),
                  _resident(cw.shape), _resident(cb.shape), e_spec],
        out_specs=[aspec, aspec, row, o2_spec],
        out_shape=[S((2 * nh, t, n), BF16), S((2 * nh, t, n), BF16), S((t, D), F32 if mode == "norm" else DH), o2_shape],
        scratch_shapes=[pltpu.VMEM((2 * nh, HALO, n), F32)])


def _tril_mask():
    r = lax.broadcasted_iota(jnp.int32, (CHUNK, CHUNK), 0)
    c = lax.broadcasted_iota(jnp.int32, (CHUNK, CHUNK), 1)
    return r >= c


def _sgu_gate_fwd(a_s, vgain, ws, bst, name, hook=None):
    t = a_s.shape[1]
    sw = a_s.shape[2]
    gps = sw // CHUNK

    def body(a_ref, vg_ref, ws_ref, b_ref, o_ref):
        v = _gelu(jnp.concatenate([a_ref[4 + s].astype(F32) for s in range(4)], axis=1))
        vn = (v * _rstd(v) * vg_ref[...]).astype(BF16)
        tri = _tril_mask()
        for g in range(SGU_G):
            w = jnp.where(tri, ws_ref[g], 0.0).astype(BF16)
            sg = _dot(w, vn[:, g * CHUNK:(g + 1) * CHUNK]) + b_ref[:, g:g + 1]
            lo = (g % gps) * CHUNK
            u = _gelu(a_ref[g // gps, :, lo:lo + CHUNK].astype(F32))
            o_ref[g // gps, :, lo:lo + CHUNK] = (u * sg).astype(BF16)

    return _run(
        body, [a_s, vgain, ws, bst], hook, grid=(t // CHUNK,), name=name, semantics=("parallel",),
        in_specs=[pl.BlockSpec((8, CHUNK, sw), lambda n: (0, n, 0)), pl.BlockSpec((1, SGU_W), lambda n: (0, 0)),
                  pl.BlockSpec((SGU_G, CHUNK, CHUNK), lambda n: (0, 0, 0)), pl.BlockSpec((CHUNK, SGU_G), lambda n: (0, 0))],
        out_specs=pl.BlockSpec((4, CHUNK, sw), lambda n: (0, n, 0)), out_shape=S((4, t, sw), BF16))


def _resid_mm(a_s, w, resid, extra, mode, name, hook=None, fm=False):
    nk, t, kc = (1, a_s.shape[1], a_s.shape[0]) if fm else a_s.shape
    tm = _tm(t)
    ni = t // tm

    def body(a_ref, w_ref, r_ref, e_ref, o1_ref, o2_ref):
        h = r_ref[...]
        if fm:
            h = h + _dot_tn(a_ref[...], w_ref[...])
        for j in range(0 if fm else nk):
            h = h + _dot(a_ref[j], w_ref[j * kc:(j + 1) * kc, :])
        if mode == "norm":
            o1_ref[...] = h
            o2_ref[...] = (h * _rstd(h) * e_ref[...]).astype(BF16)
        else:
            err = h - e_ref[...]
            o1_ref[...] = (err * (1.0 / D)).astype(o1_ref.dtype)
            o2_ref[...] = jnp.full(o2_ref.shape, jnp.sum(err * err), F32)

    row = pl.BlockSpec((tm, D), lambda i: (i, 0))
    if mode == "norm":
        e_spec, o2_spec, o2_shape = pl.BlockSpec((1, D), lambda i: (0, 0)), row, S((t, D), BF16)
    else:
        e_spec, o2_spec, o2_shape = row, pl.BlockSpec((None, 8, 128), lambda i: (i, 0, 0)), S((ni, 8, 128), F32)
    return _run(
        body, [a_s, w, resid, extra], hook, grid=(ni,), name=name, semantics=("parallel",),
        in_specs=[pl.BlockSpec((kc, tm), lambda i: (0, i)) if fm else pl.BlockSpec((nk, tm, kc), lambda i: (0, i, 0)),
                  _resident(w.shape), row, e_spec],
        out_specs=[row, o2_spec], out_shape=[S((t, D), F32 if mode == "norm" else DH), o2_shape])


def _relbias_fwd(rel_bias_t, bucket_row, name):
    nb = bucket_row.shape[1]

    def body(rb_ref, bk_ref, o_ref):
        onehot = (lax.broadcasted_iota(jnp.int32, (REL_BUCKETS, nb), 0) == bk_ref[...]).astype(F32)
        o_ref[...] = jnp.dot(rb_ref[...], onehot, precision=lax.Precision.HIGHEST, preferred_element_type=F32)

    return pl.pallas_call(body, out_shape=S((NH, nb), F32), name=name)(rel_bias_t, bucket_row)


def _relbias_bwd(dbias, bucket_row, name):
    nb = bucket_row.shape[1]

    def body(db_ref, bk_ref, o_ref):
        onehot = (lax.broadcasted_iota(jnp.int32, (REL_BUCKETS, nb), 0) == bk_ref[...]).astype(F32)
        o_ref[...] = lax.dot_general(db_ref[...], onehot, (((1,), (1,)), ((), ())),
                                     precision=lax.Precision.HIGHEST, preferred_element_type=F32)

    return pl.pallas_call(body, out_shape=S((NH, REL_BUCKETS), F32), name=name)(dbias, bucket_row)


QKV = D + 2 * NKV * HD
KV0 = D


def _rstd_rows(x):
    return lax.rsqrt(jnp.mean(x * x, axis=0, keepdims=True) + EPS)


def _attn_valid(n):
    kj = lax.broadcasted_iota(jnp.int32, (2 * CHUNK, CHUNK), 0)
    qi = lax.broadcasted_iota(jnp.int32, (2 * CHUNK, CHUNK), 1)
    dist = qi + CHUNK - kj
    return (dist >= 0) & (dist < CHUNK) & ((n > 0) | (kj >= CHUNK))


def _attn_band(cur_ref, prev_ref, row):
    return jnp.concatenate([prev_ref[row - KV0:row - KV0 + HD, :], cur_ref[row:row + HD, :]], axis=1)


def _attn_probs(kn_tok, qn, bias, valid, sink):
    s = _dot(kn_tok, qn) * (HD ** -0.5) + bias
    s = jnp.where(valid, s, -jnp.inf)
    m = jnp.maximum(jnp.max(s, axis=0, keepdims=True), sink)
    p = jnp.exp(s - m)
    psink = jnp.exp(sink - m)
    inv = 1.0 / (jnp.sum(p, axis=0, keepdims=True) + psink)
    return p * inv, psink * inv


def _attn_fwd(qkv_t, qg, kg, sinks, bias, name, hook=None):
    t = qkv_t.shape[1]

    def body(cur_ref, prev_ref, qg_ref, kg_ref, sink_ref, bias_ref, o_ref):
        n = pl.program_id(0)
        valid = _attn_valid(n)
        ks = [_attn_band(cur_ref, prev_ref, KV0 + HD * h) for h in range(NKV)]
        kn_toks = [(k * _rstd_rows(k) * kg_ref[...]).astype(BF16).T for k in ks]
        vbs = [_attn_band(cur_ref, prev_ref, KV0 + HD * (NKV + h)).astype(BF16) for h in range(NKV)]
        qs = [cur_ref[HD * hq:HD * (hq + 1), :] for hq in range(NH)]
        qns = [(q * _rstd_rows(q) * qg_ref[...]).astype(BF16) for q in qs]
        ps = [_attn_probs(kn_toks[hq // KVG], qns[hq], bias_ref[hq], valid, sink_ref[hq])[0] for hq in range(NH)]
        for hq in range(NH):
            o_ref[HD * hq:HD * (hq + 1), :] = _dot(vbs[hq // KVG], ps[hq].astype(BF16)).astype(BF16)

    col = pl.BlockSpec((HD, 1), lambda n: (0, 0))
    return _run(
        body, [qkv_t, qkv_t, qg, kg, sinks, bias], hook, grid=(t // CHUNK,), name=name, semantics=("parallel",),
        in_specs=[pl.BlockSpec((QKV, CHUNK), lambda n: (0, n)),
                  pl.BlockSpec((QKV - KV0, CHUNK), lambda n: (KV0 // (QKV - KV0), jnp.maximum(n - 1, 0))),
                  col, col, pl.BlockSpec(memory_space=pltpu.SMEM), pl.BlockSpec((NH, 2 * CHUNK, CHUNK), lambda n: (0, 0, 0))],
        out_specs=pl.BlockSpec((D, CHUNK), lambda n: (0, n)), out_shape=S((D, t), BF16))


def _dx_rows(dh, w, kc, out_dtype, name, hook=None):
    t = dh.shape[0]
    nk = w.shape[0] // kc
    tm = _tm(t)

    def body(d_ref, w_ref, o_ref):
        dhb = d_ref[...].astype(BF16)
        for j in range(nk):
            o_ref[j] = _dot_nt(dhb, w_ref[j * kc:(j + 1) * kc, :]).astype(out_dtype)

    return _run(
        body, [dh, w], hook, grid=(t // tm,), name=name, semantics=("parallel",),
        in_specs=[pl.BlockSpec((tm, D), lambda i: (i, 0)), _resident(w.shape)],
        out_specs=pl.BlockSpec((nk, tm, kc), lambda i: (0, i, 0)), out_shape=S((nk, t, kc), out_dtype))


def _dx_rows_t(dh, w, name, hook=None):
    t = dh.shape[0]
    k = w.shape[0]
    tm = _tm(t)

    def body(d_ref, w_ref, o_ref):
        o_ref[...] = _dot_nt(w_ref[...], d_ref[...].astype(BF16)).astype(BF16)

    return _run(
        body, [dh, w], hook, grid=(t // tm,), name=name, semantics=("parallel",),
        in_specs=[pl.BlockSpec((tm, D), lambda i: (i, 0)), _resident(w.shape)],
        out_specs=pl.BlockSpec((k, tm), lambda i: (0, i)), out_shape=S((k, t), BF16))


def _ffn_dw_down(dh, c, name, hook=None):
    ns, t, n = c.shape
    nh = ns // 2
    tm = _tm(t)
    ni = t // tm

    def body(d_ref, c_ref, dw_hbm, dwb_hbm, acc, stage):
        i = pl.program_id(0)

        @pl.when(i == 0)
        def _():
            acc[...] = jnp.zeros_like(acc)

        dhb = d_ref[...].astype(BF16)
        for j in range(nh):
            cg = c_ref[j].astype(F32)
            acc[j * n:(j + 1) * n, :] += _dot_tn((cg * _sigmoid(cg) * c_ref[nh + j].astype(F32)).astype(BF16), dhb)

        @pl.when(i == ni - 1)
        def _():
            pltpu.sync_copy(acc, dw_hbm)
            for j in range(nh):
                stage[...] = acc[j * n:(j + 1) * n, :].astype(BF16)
                pltpu.sync_copy(stage, dwb_hbm.at[pl.ds(j * n, n), :])

    return _run(
        body, [dh, c], hook, grid=(ni,), name=name, semantics=("arbitrary",),
        in_specs=[pl.BlockSpec((tm, D), lambda i: (i, 0)), pl.BlockSpec((ns, tm, n), lambda i: (0, i, 0))],
        out_specs=[ANY, ANY], out_shape=[S((nh * n, D), F32), S((nh * n, D), BF16)],
        scratch_shapes=[pltpu.VMEM((nh * n, D), F32), pltpu.VMEM((n, D), BF16)])


def _ffn_bwd(dh, c, a, wdown, wup, cw, h, gain, name, hook=None):
    ns, t, n = c.shape
    nh = ns // 2
    tm = min(FFN_ROWS, t)
    ni = t // tm

    def body(d_ref, c_ref, a_ref, wd_ref, wu_ref, cw_ref, h_ref, g_ref, da_ref, o_ref, dg_ref, dcw_ref, dcb_ref, carry, keep):
        i = pl.program_id(0)

        @pl.when(i == 0)
        def _():
            carry[...] = jnp.zeros_like(carry)
            dg_ref[...] = jnp.zeros_like(dg_ref)
            dcw_ref[...] = jnp.zeros_like(dcw_ref)
            dcb_ref[...] = jnp.zeros_like(dcb_ref)

        rsum = lambda v: jnp.sum(v, axis=0, keepdims=True)
        dhb = d_ref[...].astype(BF16)
        acc = jnp.zeros((tm, D), F32)
        for j in range(nh):
            dact = _dot_nt(dhb, wd_ref[j * n:(j + 1) * n, :])
            cg = c_ref[j].astype(F32)
            cv = c_ref[nh + j].astype(F32)
            sg = _sigmoid(cg)
            gs = cg * sg
            for s, x in ((j, dact * cv * (sg + gs * (1.0 - sg))), (nh + j, dact * gs)):
                ext = jnp.concatenate([x, carry[s]], axis=0)
                keep[0] = ext[1:1 + tm]
                keep[1] = ext[2:2 + tm]
                x1, x2 = keep[0], keep[1]
                cwv = cw_ref[s]
                da = (cwv[2:3] * x + cwv[1:2] * x1 + cwv[0:1] * x2).astype(BF16)
                carry[s] = x[:HALO]
                da_ref[s] = da
                acc = acc + _dot_nt(da, wu_ref[s])
                av = a_ref[s].astype(F32)
                dcw_ref[s] += jnp.concatenate([rsum(x2 * av), rsum(x1 * av), rsum(x * av)], axis=0)
                dcb_ref[s] += rsum(x)
        hv = h_ref[...]
        r = _rstd(hv)
        gg = acc * g_ref[...]
        dh_new = d_ref[...].astype(F32) + r * gg - hv * (r * r * r * jnp.mean(gg * hv, axis=-1, keepdims=True))
        o_ref[...] = dh_new.astype(o_ref.dtype)
        dg_ref[...] += jnp.sum(acc * hv * r, axis=0, keepdims=True)

    slab = pl.BlockSpec((ns, tm, n), lambda i: (0, ni - 1 - i, 0))
    row = pl.BlockSpec((tm, D), lambda i: (ni - 1 - i, 0))
    vec = pl.BlockSpec((1, D), lambda i: (0, 0))
    whole = lambda shape: pl.BlockSpec(shape, lambda i: (0,) * len(shape))
    return _run(
        body, [dh, c, a, wdown, wup, cw, h, gain], hook, grid=(ni,), name=name, semantics=("arbitrary",),
        in_specs=[row, slab, slab, _resident(wdown.shape), _resident(wup.shape), _resident(cw.shape), row, vec],
        out_specs=[slab, row, vec, whole((ns, 3, n)), whole((ns, 1, n))],
        out_shape=[S((ns, t, n), BF16), S((t, D), DH), S((1, D), F32), S((ns, 3, n), F32), S((ns, 1, n), F32)],
        scratch_shapes=[pltpu.VMEM((ns, HALO, n), F32), pltpu.VMEM((2, tm, n), F32)])


def _dw_slot(hn, dy_s, name, hook=None):
    t, k = hn.shape
    ns, _, n = dy_s.shape
    tm = _tm(t)

    def body(a_ref, b_ref, o_ref, ob_ref, at_ref):
        @pl.when(pl.program_id(0) == 0)
        def _():
            for i in range(t // tm):
                at_ref[:, i * tm:(i + 1) * tm] = a_ref[i * tm:(i + 1) * tm, :].T

        acc = _dot(at_ref[...], b_ref[...])
        o_ref[...] = acc
        ob_ref[...] = acc.astype(BF16)

    ospec = pl.BlockSpec((None, k, n), lambda j: (j, 0, 0))
    return _run(
        body, [hn, dy_s], hook, grid=(ns,), name=name, semantics=("arbitrary",),
        in_specs=[_resident(hn.shape), pl.BlockSpec((None, t, n), lambda j: (j, 0, 0))],
        out_specs=[ospec, ospec], out_shape=[S((ns, k, n), F32), S((ns, k, n), BF16)],
        scratch_shapes=[pltpu.VMEM((k, t), BF16)])


def _dw_rows(a_s, dh, name, hook=None, fm=False):
    nk, t, kc = (1, a_s.shape[1], a_s.shape[0]) if fm else a_s.shape
    tm = _tm(t)
    ni = t // tm

    def body(a_ref, d_ref, o_ref, ob_ref):
        i = pl.program_id(0)
        dhb = d_ref[...].astype(BF16)

        @pl.when(i == 0)
        def _():
            o_ref[...] = jnp.zeros_like(o_ref)

        if fm:
            o_ref[...] += _dot(a_ref[...], dhb)
        for j in range(0 if fm else nk):
            o_ref[j * kc:(j + 1) * kc, :] += _dot_tn(a_ref[j], dhb)

        @pl.when(i == ni - 1)
        def _():
            ob_ref[...] = o_ref[...].astype(BF16)

    ospec = pl.BlockSpec((nk * kc, D), lambda i: (0, 0))
    return _run(
        body, [a_s, dh], hook, grid=(ni,), name=name, semantics=("arbitrary",),
        in_specs=[pl.BlockSpec((kc, tm), lambda i: (0, i)) if fm else pl.BlockSpec((nk, tm, kc), lambda i: (0, i, 0)),
                  pl.BlockSpec((tm, D), lambda i: (i, 0))],
        out_specs=[ospec, ospec], out_shape=[S((nk * kc, D), F32), S((nk * kc, D), BF16)])


def _dx_slot_normbwd(dy_s, wg, h, gain, dh_in, name, hook=None, fm=False, out_dtype=F32):
    ns, t, n = (1, dy_s.shape[1], dy_s.shape[0]) if fm else dy_s.shape
    tm = _tm(t)

    def body(dy_ref, w_ref, h_ref, g_ref, di_ref, o_ref, dg_ref):
        i = pl.program_id(0)

        @pl.when(i == 0)
        def _():
            dg_ref[...] = jnp.zeros_like(dg_ref)

        g = _dot_tn(dy_ref[...], w_ref[...]) if fm else _dot_nt(dy_ref[0], w_ref[0])
        for s in range(1, ns):
            g = g + _dot_nt(dy_ref[s], w_ref[s])
        hv = h_ref[...]
        r = _rstd(hv)
        gg = g * g_ref[...]
        dh_new = di_ref[...].astype(F32) + r * gg - hv * (r * r * r * jnp.mean(gg * hv, axis=-1, keepdims=True))
        o_ref[...] = dh_new.astype(o_ref.dtype)
        dg_ref[...] += jnp.sum(g * hv * r, axis=0, keepdims=True)

    row = pl.BlockSpec((tm, D), lambda i: (i, 0))
    vec = pl.BlockSpec((1, D), lambda i: (0, 0))
    return _run(
        body, [dy_s, wg, h, gain, dh_in], hook, grid=(t // tm,), name=name, semantics=("arbitrary",),
        in_specs=[pl.BlockSpec((n, tm), lambda i: (0, i)) if fm else pl.BlockSpec((ns, tm, n), lambda i: (0, i, 0)),
                  _resident(wg.shape), row, vec, row],
        out_specs=[row, vec], out_shape=[S((t, D), out_dtype), S((1, D), F32)])


def _sgu_gate_bwd(a_s, dg_s, vgain, ws, bst, name, hook=None):
    t = a_s.shape[1]
    sw = a_s.shape[2]
    gps = sw // CHUNK

    def body(a_ref, dg_ref, vg_ref, ws_ref, b_ref, da_ref, dws_ref, dbt_ref, dvg_ref, dvn_ref):
        n = pl.program_id(0)

        @pl.when(n == 0)
        def _():
            dws_ref[...] = jnp.zeros_like(dws_ref)
            dbt_ref[...] = jnp.zeros_like(dbt_ref)
            dvg_ref[...] = jnp.zeros_like(dvg_ref)

        vpre = jnp.concatenate([a_ref[4 + s].astype(F32) for s in range(4)], axis=1)
        v, v_grad = _gelu_and_grad(vpre)
        r = _rstd(v)
        vhat = v * r
        vn = (vhat * vg_ref[...]).astype(BF16)
        tri = _tril_mask()
        lane = lax.broadcasted_iota(jnp.int32, (CHUNK, CHUNK), 1)
        dbt = jnp.zeros((CHUNK, CHUNK), F32)
        for g in range(SGU_G):
            w = jnp.where(tri, ws_ref[g], 0.0).astype(BF16)
            vng = vn[:, g * CHUNK:(g + 1) * CHUNK]
            sg = _dot(w, vng) + b_ref[:, g:g + 1]
            lo = (g % gps) * CHUNK
            u, u_grad = _gelu_and_grad(a_ref[g // gps, :, lo:lo + CHUNK].astype(F32))
            dgate = dg_ref[g // gps, :, lo:lo + CHUNK].astype(F32)
            da_ref[g // gps, :, lo:lo + CHUNK] = (dgate * sg * u_grad).astype(BF16)
            ds = dgate * u
            dsb = ds.astype(BF16)
            dvn_ref[:, g * CHUNK:(g + 1) * CHUNK] = _dot_tn(w, dsb)
            dws_ref[g] += jnp.where(tri, _dot_nt(dsb, vng), 0.0)
            dbt = dbt + jnp.where(lane == g, jnp.sum(ds, axis=-1, keepdims=True), 0.0)
        dbt_ref[...] += dbt
        dvn = dvn_ref[...]
        dvg_ref[...] += jnp.sum(dvn * vhat, axis=0, keepdims=True)
        gg = dvn * vg_ref[...]
        dv = r * gg - v * (r * r * r * jnp.mean(gg * v, axis=-1, keepdims=True))
        dav = (dv * v_grad).astype(BF16)
        for s in range(4):
            da_ref[4 + s] = dav[:, s * sw:(s + 1) * sw]

    return _run(
        body, [a_s, dg_s, vgain, ws, bst], hook, grid=(t // CHUNK,), name=name, semantics=("arbitrary",),
        in_specs=[pl.BlockSpec((8, CHUNK, sw), lambda n: (0, n, 0)), pl.BlockSpec((4, CHUNK, sw), lambda n: (0, n, 0)),
                  pl.BlockSpec((1, SGU_W), lambda n: (0, 0)), pl.BlockSpec((SGU_G, CHUNK, CHUNK), lambda n: (0, 0, 0)),
                  pl.BlockSpec((CHUNK, SGU_G), lambda n: (0, 0))],
        out_specs=[pl.BlockSpec((8, CHUNK, sw), lambda n: (0, n, 0)), pl.BlockSpec((SGU_G, CHUNK, CHUNK), lambda n: (0, 0, 0)),
                   pl.BlockSpec((CHUNK, CHUNK), lambda n: (0, 0)), pl.BlockSpec((1, SGU_W), lambda n: (0, 0))],
        out_shape=[S((8, t, sw), BF16), S((SGU_G, CHUNK, CHUNK), F32), S((CHUNK, CHUNK), F32), S((1, SGU_W), F32)],
        scratch_shapes=[pltpu.VMEM((CHUNK, SGU_W), F32)])


def _attn_bwd(qkv_t, do_t, qg, kg, sinks, bias, name, hook=None):
    t = qkv_t.shape[1]
    nb = t // CHUNK

    def body(cur_ref, prev_ref, do_ref, qg_ref, kg_ref, sink_ref, bias_ref,
             o_ref, dqg_out, dkg_out, dsk_out, dbias_ref, carry, dqg_ref, dkg_ref, dsk_ref):
        n = pl.program_id(0)

        @pl.when(n == 0)
        def _():
            carry[...] = jnp.zeros_like(carry)
            dqg_ref[...] = jnp.zeros_like(dqg_ref)
            dkg_ref[...] = jnp.zeros_like(dkg_ref)
            dsk_ref[...] = jnp.zeros_like(dsk_ref)
            dbias_ref[...] = jnp.zeros_like(dbias_ref)

        @pl.when(n < nb)
        def _():
            valid = _attn_valid(n)
            o_ref[0:KV0, :] = carry[0:KV0, :].astype(BF16)
            kvs, heads = range(NKV), range(NH)
            group = lambda h: range(KVG * h, KVG * (h + 1))
            ks = [_attn_band(cur_ref, prev_ref, KV0 + HD * h) for h in kvs]
            rks = [_rstd_rows(k) for k in ks]
            khats = [k * rk for k, rk in zip(ks, rks)]
            kns = [(khat * kg_ref[...]).astype(BF16) for khat in khats]
            kn_toks = [kn.T for kn in kns]
            vbs = [_attn_band(cur_ref, prev_ref, KV0 + HD * (NKV + h)).astype(BF16) for h in kvs]
            v_toks = [vb.T for vb in vbs]
            qs = [cur_ref[HD * hq:HD * (hq + 1), :] for hq in heads]
            rqs = [_rstd_rows(q) for q in qs]
            qhats = [q * rq for q, rq in zip(qs, rqs)]
            qns = [(qhat * qg_ref[...]).astype(BF16) for qhat in qhats]
            probs = [_attn_probs(kn_toks[hq // KVG], qns[hq], bias_ref[hq], valid, sink_ref[hq]) for hq in heads]
            dohs = [do_ref[HD * hq:HD * (hq + 1), :] for hq in heads]
            dps = [_dot(v_toks[hq // KVG], dohs[hq]) for hq in heads]
            dsums = [jnp.sum(p * dp, axis=0, keepdims=True) for (p, _), dp in zip(probs, dps)]
            dss = [p * (dp - dsum) for (p, _), dp, dsum in zip(probs, dps, dsums)]
            for hq in heads:
                dsk_ref[hq:hq + 1, :] -= probs[hq][1] * dsums[hq]
                dbias_ref[hq] += dss[hq]
            dvs = [sum(_dot_nt(dohs[hq], probs[hq][0].astype(BF16)) for hq in group(h)) for h in kvs]
            dscs = [(ds * (HD ** -0.5)).astype(BF16) for ds in dss]
            dqns = [_dot(kns[hq // KVG], dscs[hq]) for hq in heads]
            dkns = [sum(_dot_nt(qns[hq], dscs[hq]) for hq in group(h)) for h in kvs]
            dqg_ref[...] += sum(dqn * qhat for dqn, qhat in zip(dqns, qhats))
            for hq in heads:
                gq = dqns[hq] * qg_ref[...]
                carry[HD * hq:HD * (hq + 1), :] = rqs[hq] * gq - qs[hq] * (
                    rqs[hq] * rqs[hq] * rqs[hq] * jnp.mean(gq * qs[hq], axis=0, keepdims=True))
            dkg_ref[...] += sum(dkn * khat for dkn, khat in zip(dkns, khats))
            for h in kvs:
                krow, vrow = KV0 + HD * h, KV0 + HD * (NKV + h)
                gk = dkns[h] * kg_ref[...]
                dk = rks[h] * gk - ks[h] * (rks[h] * rks[h] * rks[h] * jnp.mean(gk * ks[h], axis=0, keepdims=True))
                o_ref[krow:krow + HD, :] = (carry[krow:krow + HD, :] + dk[:, :CHUNK]).astype(BF16)
                o_ref[vrow:vrow + HD, :] = (carry[vrow:vrow + HD, :] + dvs[h][:, :CHUNK]).astype(BF16)
                carry[krow:krow + HD, :] = dk[:, CHUNK:]
                carry[vrow:vrow + HD, :] = dvs[h][:, CHUNK:]

        @pl.when(n == nb)
        def _():
            o_ref[...] = carry[...].astype(BF16)
            dqg_out[...] = jnp.sum(dqg_ref[...], axis=1, keepdims=True)
            dkg_out[...] = jnp.sum(dkg_ref[...], axis=1, keepdims=True)
            dsk_out[...] = jnp.sum(dsk_ref[...], axis=1, keepdims=True)

    cur = lambda n: (0, jnp.minimum(n, nb - 1))
    col = pl.BlockSpec((HD, 1), lambda n: (0, 0))
    whole = lambda shape: pl.BlockSpec(shape, lambda n: (0,) * len(shape))
    return _run(
        body, [qkv_t, qkv_t, do_t, qg, kg, sinks, bias], hook, grid=(nb + 1,), name=name, semantics=("arbitrary",),
        in_specs=[pl.BlockSpec((QKV, CHUNK), cur),
                  pl.BlockSpec((QKV - KV0, CHUNK), lambda n: (KV0 // (QKV - KV0), jnp.clip(n - 1, 0, nb - 1))),
                  pl.BlockSpec((D, CHUNK), cur), col, col, pl.BlockSpec(memory_space=pltpu.SMEM), whole((NH, 2 * CHUNK, CHUNK))],
        out_specs=[pl.BlockSpec((QKV, CHUNK), lambda n: (0, jnp.maximum(n - 1, 0))), whole((HD, 1)), whole((HD, 1)),
                   whole((NH, 1)), whole((NH, 2 * CHUNK, CHUNK))],
        out_shape=[S((QKV, t), BF16), S((HD, 1), F32), S((HD, 1), F32), S((NH, 1), F32), S((NH, 2 * CHUNK, CHUNK), F32)],
        scratch_shapes=[pltpu.VMEM((QKV, CHUNK), F32), pltpu.VMEM((HD, CHUNK), F32), pltpu.VMEM((HD, 2 * CHUNK), F32),
                        pltpu.VMEM((NH, CHUNK), F32)])


class _Plain:
    def __init__(self, wg):
        self.full, self.grads = wg, {}

    def w(self, n):
        return self.full[n]

    def hook(self, host):
        return None

    def grad(self, n, pair):
        self.grads[n] = pair

    def small(self, g_rep):
        pass


def _local_step(x, target, rep, sch):
    bucket_row = jnp.asarray(_rel_tables().T.reshape(1, -1))
    bias = _relbias_fwd(rep["rel_bias"].T, bucket_row, "relbias_fwd").reshape(NH, 2 * CHUNK, CHUNK)
    bst = rep["sgu_b_s"][0].T
    ws = rep["sgu_w_s"][0]
    vgain = rep["sgu_v_gain"]
    qg, kg, sinks = rep["attn_q_gain"].reshape(HD, 1), rep["attn_k_gain"].reshape(HD, 1), rep["attn_sinks"][0]
    w_down = lambda l: sch.w("ffn_w_down%d" % l).reshape(D_FF, D)
    w_up = lambda l: sch.w("ffn_w_up%d" % l)
    cw = [sch.w("ffn_conv_w")[:, 3 * l:3 * l + 3] for l in range(2)]
    cb = [rep["ffn_conv_b"][l].reshape(8, 1, -1) for l in range(2)]
    mixg = [rep["mix_norm"][l:l + 1] for l in range(2)]
    ffng = [rep["ffn_norm"][l:l + 1] for l in range(2)]
    rows = lambda pair: tuple(g.reshape(N_DEV, -1, D) for g in pair)
    hk = sch.hook

    hn0 = _rmsnorm(x, mixg[0], "norm0")
    a0 = _mm_slot(hn0, sch.w("sgu_w_in"), BF16, "sgu_in", hk("sgu_in"))
    gated = _sgu_gate_fwd(a0, vgain, ws, bst, "sgu_gate", hk("sgu_gate"))
    h1, hn1 = _resid_mm(gated, sch.w("sgu_w_out").reshape(SGU_W, D), x, ffng[0], "norm", "sgu_out", hk("sgu_out"))
    a_ff0, c_ff0, h2, hn2 = _ffn_fwd(hn1, h1, w_up(0), w_down(0), cw[0], cb[0], mixg[1], "norm", "ffn0_fwd", hk("ffn0_fwd"))
    qkv = _mm_t(hn2, sch.w("attn_w_qkv"), "qkv", hk("qkv"))
    o = _attn_fwd(qkv, qg, kg, sinks, bias, "attn", hk("attn"))
    h3, hn3 = _resid_mm(o, sch.w("attn_w_o").reshape(D, D), h2, ffng[1], "norm", "attn_out", hk("attn_out"), fm=True)
    a_ff1, c_ff1, dy, sq = _ffn_fwd(hn3, h3, w_up(1), w_down(1), cw[1], cb[1], target, "loss", "ffn1_fwd_loss", hk("ffn1_fwd_loss"))
    loss = (0.5 / D) * jnp.sum(sq[:, 0, 0])

    def ffn_bwd(dh, h_in, hn, a, c, l, tag):
        da, dh_new, dgain, g_cw, g_cb = _ffn_bwd(dh, c, a, w_down(l), w_up(l), cw[l], h_in, ffng[l], tag + "_bwd", hk(tag + "_bwd"))
        sch.grad("ffn_w_down%d" % l, rows(_ffn_dw_down(dh, c, tag + "_dw_down", hk(tag + "_dw_down"))))
        sch.grad("ffn_w_up%d" % l, _dw_slot(hn, da, tag + "_dw_up", hk(tag + "_dw_up")))
        return dh_new, dgain, g_cw, g_cb.reshape(-1)

    dh, d_ffng1, g_cw1, g_cb1 = ffn_bwd(dy, h3, hn3, a_ff1, c_ff1, 1, "ffn1")
    do = _dx_rows_t(dh, sch.w("attn_w_o").reshape(D, D), "attn_do", hk("attn_do"))
    sch.grad("attn_w_o", rows(_dw_rows(o, dh, "dw_o", hk("dw_o"), fm=True)))
    dqkv, d_qg, d_kg, d_sk, d_bias = _attn_bwd(qkv, do, qg, kg, sinks, bias, "attn_bwd", hk("attn_bwd"))
    sch.grad("attn_w_qkv", tuple(g.reshape(N_DEV, -1, D) for g in _dw_rows(dqkv, hn2, "dw_qkv", hk("dw_qkv"), fm=True)))
    dh, d_mixg1 = _dx_slot_normbwd(dqkv, sch.w("attn_w_qkv").reshape(QKV, D), h2, mixg[1], dh, "dx_qkv", hk("dx_qkv"), fm=True,
                                   out_dtype=DH)
    d_relb = _relbias_bwd(d_bias.reshape(NH, -1), bucket_row, "relbias_bwd").T
    g_rep = {"attn_q_gain": d_qg.reshape(1, HD), "attn_k_gain": d_kg.reshape(1, HD), "attn_sinks": d_sk.reshape(1, NH),
             "rel_bias": d_relb}
    sch.small(g_rep)
    dh, d_ffng0, g_cw0, g_cb0 = ffn_bwd(dh, h1, hn1, a_ff0, c_ff0, 0, "ffn0")
    g_cw = jnp.concatenate([g_cw0, g_cw1], axis=1)
    sch.grad("ffn_conv_w", (g_cw, g_cw.astype(BF16)))
    g_ffn = {"ffn_norm": jnp.concatenate([d_ffng0, d_ffng1], axis=0), "ffn_conv_b": jnp.stack([g_cb0, g_cb1], axis=0)}
    sch.small(g_ffn)
    dgated = _dx_rows(dh, sch.w("sgu_w_out").reshape(SGU_W, D), SGU_W // 4, BF16, "sgu_dgated", hk("sgu_dgated"))
    sch.grad("sgu_w_out", rows(_dw_rows(gated, dh, "dw_sgu_out", hk("dw_sgu_out"))))
    da0, d_ws, d_bst, d_vgain = _sgu_gate_bwd(a0, dgated, vgain, ws, bst, "sgu_gate_bwd", hk("sgu_gate_bwd"))
    grad_x, d_mixg0 = _dx_slot_normbwd(da0, sch.w("sgu_w_in"), x, mixg[0], dh, "dx_sgu_in")
    g_sgu = {"sgu_v_gain": d_vgain, "sgu_w_s": d_ws[None], "sgu_b_s": d_bst[:, :SGU_G].T[None],
             "mix_norm": jnp.concatenate([d_mixg0, d_mixg1], axis=0)}
    sch.small(g_sgu)
    g_rep.update(g_ffn)
    g_rep.update(g_sgu)
    sch.grad("sgu_w_in", _dw_slot(hn0, da0, "dw_sgu_in", hk("dw_sgu_in")))
    return loss, grad_x, g_rep


def _allgather(xs, name):
    nt = len(xs)

    def body(*refs):
        x_refs, o_refs = refs[:nt], refs[nt:2 * nt]
        send_sems, recv_sems, local_sems = refs[2 * nt:]
        x, y, c, chips = _place()
        me, sibling = (x, y, c), (x, y, 1 - c)

        def copy(t, k, block, to, src=None):
            px, py, pc = block
            dst = o_refs[t].at[4 * px + 2 * py + pc]
            return pltpu.make_async_remote_copy(
                src_ref=dst if src is None else src, dst_ref=dst, send_sem=send_sems.at[t, k], recv_sem=recv_sems.at[t, k],
                device_id=to, device_id_type=MESH)

        mine = [pltpu.make_async_copy(x_refs[t], o_refs[t].at[4 * x + 2 * y + c], local_sems.at[t]) for t in range(nt)]
        for cp in mine:
            cp.start()
        first = []
        for t in range(nt):
            first.append(copy(t, 0, me, sibling, src=x_refs[t]))
            first += [copy(t, 1 + j, me, (*chip, c), src=x_refs[t]) for j, chip in enumerate(chips)]
        for cp in first:
            cp.start()
        passed = []
        for j, chip in enumerate(chips):
            for t in range(nt):
                copy(t, 1 + j, (*chip, c), me).wait_recv()
                fwd = copy(t, 4 + j, (*chip, c), sibling)
                fwd.start()
                passed.append(fwd)
        for t in range(nt):
            copy(t, 0, sibling, me).wait_recv()
            for j, chip in enumerate(chips):
                copy(t, 4 + j, (*chip, 1 - c), me).wait_recv()
        for cp in first + passed:
            cp.wait_send()
        for cp in mine:
            cp.wait()

    return pl.pallas_call(
        body, name=name, in_specs=[ANY] * nt, out_specs=[ANY] * nt,
        out_shape=[S((N_DEV,) + a.shape, a.dtype) for a in xs],
        scratch_shapes=[pltpu.SemaphoreType.DMA((nt, 7)), pltpu.SemaphoreType.DMA((nt, 7)), pltpu.SemaphoreType.DMA((nt,))],
        compiler_params=pltpu.CompilerParams(has_side_effects=True))(*xs)


def _exchange(hook, name):
    comm = hook()
    ci, co = len(comm.inputs), len(comm.out_shapes)

    def body(*refs):
        cins, couts = refs[:ci], refs[ci:ci + co]
        send, recv = refs[-2:]
        comm.start(cins, couts, send, recv)
        comm.finish(cins, couts, send, recv)

    res = pl.pallas_call(
        body, name=name, in_specs=[ANY] * ci, out_specs=[ANY] * co, out_shape=comm.out_shapes,
        scratch_shapes=[pltpu.SemaphoreType.DMA((comm.n_sems,)), pltpu.SemaphoreType.DMA((comm.n_sems,))],
        input_output_aliases=dict(comm.aliases),
        compiler_params=pltpu.CompilerParams(has_side_effects=True))(*comm.inputs)
    hook(res)


def _row_tile(r):
    tr = r if r <= ROW_TILE or r % ROW_TILE else ROW_TILE
    assert r % tr == 0
    return tr


def _rs_partial(g32, sib, place, name):
    _, r, cdim = g32.shape
    tr = _row_tile(r)

    def body(place_ref, g_ref, s_ref, p_ref, own_ref):
        k = pl.program_id(1)
        tot = g_ref[...] + s_ref[...].astype(F32)
        p_ref[...] = tot.astype(BF16)

        @pl.when(k == place_ref[1])
        def _():
            own_ref[...] = tot

    grid_spec = pltpu.PrefetchScalarGridSpec(
        num_scalar_prefetch=1, grid=(r // tr, 4),
        in_specs=[pl.BlockSpec((None, None, tr, cdim), lambda i, k, pr: (k, pr[0], i, 0)),
                  pl.BlockSpec((None, tr, cdim), lambda i, k, pr: (k, i, 0))],
        out_specs=[pl.BlockSpec((None, tr, cdim), lambda i, k, pr: (k, i, 0)), pl.BlockSpec((tr, cdim), lambda i, k, pr: (i, 0))])
    return pl.pallas_call(
        body, grid_spec=grid_spec, name=name,
        out_shape=[S((4, r, cdim), BF16), S((r, cdim), F32)],
        compiler_params=_cp("parallel", "arbitrary"))(place, g32.reshape(4, 2, r, cdim), sib)


def _adamw_math(w, g, m, v):
    m = ADAM_B1 * m + (1.0 - ADAM_B1) * g
    v = ADAM_B2 * v + (1.0 - ADAM_B2) * (g * g)
    m_hat = m / (1.0 - ADAM_B1 ** ADAM_STEP)
    v_hat = v / (1.0 - ADAM_B2 ** ADAM_STEP)
    delta = -ADAM_LR * (m_hat / (jnp.sqrt(v_hat) + ADAM_EPS) + ADAM_WD * w)
    return delta, m, v


def _adamw_shard(owns, recvs, w, m, v, name, flipped=False):
    nl = w.shape[0]
    r, cdim = owns[0].shape
    tr = _row_tile(r)
    nr = r // tr

    def body(*refs):
        own_refs, recv_refs = refs[:nl], refs[nl:2 * nl]
        w_ref, m_ref, v_ref, g_out, d_out, m_out, v_out = refs[2 * nl:]
        layer = pl.program_id(0)
        g = None
        for l in range(nl):
            gl = own_refs[l][...] + recv_refs[l][0].astype(F32) + recv_refs[l][1].astype(F32) + recv_refs[l][2].astype(F32)
            g = gl if g is None else jnp.where(layer == l, gl, g)
        if flipped:
            g = g.T
        g_out[...] = g
        d_out[...], m_out[...], v_out[...] = _adamw_math(w_ref[...], g, m_ref[...], v_ref[...])

    park = lambda l: (lambda layer, i: (jnp.where(layer == l, i, jnp.where(layer < l, 0, nr - 1)), 0))
    park3 = lambda l: (lambda layer, i: (0, jnp.where(layer == l, i, jnp.where(layer < l, 0, nr - 1)), 0))
    if flipped:
        row = pl.BlockSpec((None, cdim, tr), lambda layer, i: (layer, 0, i))
    else:
        row = pl.BlockSpec((None, tr, cdim), lambda layer, i: (layer, i, 0))
    return pl.pallas_call(
        body, grid=(nl, nr), name=name,
        in_specs=[pl.BlockSpec((tr, cdim), park(l)) for l in range(nl)] + [pl.BlockSpec((3, tr, cdim), park3(l)) for l in range(nl)]
        + [row, row, row],
        out_specs=[row] * 4, out_shape=[S(w.shape, F32)] * 4,
        compiler_params=_cp("arbitrary", "arbitrary"))(*owns, *recvs, w, m, v)


def _adamw_small(galls, ws, ms, vs, name):
    n = len(galls)

    def body(*refs):
        g_refs, w_refs, m_refs, v_refs, outs = refs[:n], refs[n:2 * n], refs[2 * n:3 * n], refs[3 * n:4 * n], refs[4 * n:]
        for i in range(n):
            g = g_refs[i][0].astype(F32)
            for s in range(1, N_DEV):
                g = g + g_refs[i][s].astype(F32)
            outs[i][...] = g
            outs[n + i][...], outs[2 * n + i][...], outs[3 * n + i][...] = _adamw_math(w_refs[i][...], g, m_refs[i][...], v_refs[i][...])

    res = pl.pallas_call(body, out_shape=[S(a.shape, F32) for a in ws] * 4, name=name)(*galls, *ws, *ms, *vs)
    return [res[k * n:(k + 1) * n] for k in range(4)]


REPLICATED = ["mix_norm", "ffn_norm", "sgu_v_gain", "sgu_w_s", "sgu_b_s", "attn_q_gain", "attn_k_gain", "attn_sinks", "rel_bias",
              "ffn_conv_b"]
WEIGHTS = ["mix_norm", "ffn_norm", "sgu_w_in", "sgu_v_gain", "sgu_w_s", "sgu_b_s", "sgu_w_out", "attn_w_qkv", "attn_q_gain",
           "attn_k_gain", "attn_sinks", "attn_w_o", "rel_bias", "ffn_w_up", "ffn_conv_w", "ffn_conv_b", "ffn_w_down"]
SMALL = ["g_" + n for n in REPLICATED]
BF16_TRANSIT = {"sgu_w_s"}
SMALL_ATTN = ["g_attn_q_gain", "g_attn_k_gain", "g_attn_sinks", "g_rel_bias"]
SMALL_FFN = ["g_ffn_norm", "g_ffn_conv_b"]
SMALL_LATE = [n for n in SMALL if n not in SMALL_ATTN + SMALL_FFN]

GATHER_FIRST = ["sgu_w_in", "ffn_conv_w"]
PLAN = {
    "sgu_in": [("ag1", "sgu_w_out"), ("ag1", "ffn_w_down0")],
    "sgu_gate": [("ag2", "sgu_w_out"), ("ag2", "ffn_w_down0"), ("ag1", "ffn_w_up0")],
    "sgu_out": [("ag2", "ffn_w_up0"), ("ag1", "attn_w_qkv")],
    "ffn0_fwd": [("ag2", "attn_w_qkv"), ("ag1", "attn_w_o"), ("ag1", "ffn_w_up1")],
    "qkv": [("ag2", "attn_w_o"), ("ag2", "ffn_w_up1")],
    "attn": [("ag1", "ffn_w_down1")],
    "attn_out": [("ag2", "ffn_w_down1")],
    "ffn1_dw_up": [("rs1", "ffn_w_down1")],
    "attn_do": [("rs1", "ffn_w_up1")],
    "attn_bwd": [("rs2", "ffn_w_down1"), ("rs1", "attn_w_o")],
    "dw_qkv": [("rs2", "attn_w_o")],
    "dx_qkv": [("rs1", "attn_w_qkv")],
    "ffn0_bwd": [("rs2", "ffn_w_up1"), ("rs2", "attn_w_qkv")] + [("ag1", n) for n in SMALL_ATTN],
    "ffn0_dw_down": [("ag2", n) for n in SMALL_ATTN],
    "ffn0_dw_up": [("rs1", "ffn_w_down0")],
    "sgu_dgated": [("rs1", "ffn_w_up0")] + [("ag1", n) for n in SMALL_FFN],
    "dw_sgu_out": [("rs2", "ffn_w_down0")] + [("ag2", n) for n in SMALL_FFN],
    "sgu_gate_bwd": [("rs2", "ffn_w_up0"), ("rs1", "sgu_w_out")],
    "dw_sgu_in": [("rs2", "sgu_w_out")] + [("ag1", n) for n in SMALL_LATE],
    "last_a": [("rs1", "sgu_w_in"), ("rs1", "ffn_conv_w")] + [("ag2", n) for n in SMALL_LATE],
    "last_b": [("rs2", "sgu_w_in"), ("rs2", "ffn_conv_w")],
}


class _Overlap:
    def __init__(self, shard, place):
        self.shard, self.place = shard, place
        self.part, self.full = {}, {}
        self.grads, self.sib, self.own, self.recv = {}, {}, {}, {}

    def w(self, n):
        return self.full[n]

    def grad(self, n, pair):
        self.grads[n] = pair

    def small(self, g_rep):
        self.shard.update(("g_" + n, a.astype(BF16) if n in BF16_TRANSIT else a) for n, a in _views2d(g_rep).items())

    def chip_sums(self, n):
        sums, self.own[n] = _rs_partial(self.grads[n][0], self.sib.pop(n), self.place, "rs_partial_" + n)
        return sums

    def hook(self, host):
        ops = PLAN.get(host)
        if not ops:
            return None
        where = {"ag1": self.part, "ag2": self.full, "rs1": self.sib, "rs2": self.recv}
        idx = []

        def hook(results=None):
            if results is not None:
                for (kind, n), i in zip(ops, idx):
                    where[kind][n] = results[i]
                return None
            comm = _Comm()
            for kind, n in ops:
                arr = {"ag1": lambda: self.shard[n], "ag2": lambda: self.part.pop(n), "rs1": lambda: self.grads[n][1],
                       "rs2": lambda: self.chip_sums(n)}[kind]()
                idx.append(comm.add(kind, arr))
            return comm

        return hook


TRANSPOSED = {"attn_w_qkv"}
PHYSICAL_T = {"ffn_w_up"}
SHARDED = {
    "sgu_w_in": ["sgu_w_in"], "sgu_w_out": ["sgu_w_out"], "attn_w_qkv": ["attn_w_qkv"], "attn_w_o": ["attn_w_o"],
    "ffn_w_up": ["ffn_w_up0", "ffn_w_up1"], "ffn_w_down": ["ffn_w_down0", "ffn_w_down1"], "ffn_conv_w": ["ffn_conv_w"],
}


def _send_views(w):
    out = {"ffn_conv_w": w["ffn_conv_w"].reshape(6, -1)}
    for name, parts in SHARDED.items():
        if name != "ffn_conv_w":
            out.update((p, (w[name][l].T if name in TRANSPOSED else w[name][l]).astype(BF16)) for l, p in enumerate(parts))
    return out


def _views2d(d):
    return {n: d[n].reshape(-1, d[n].shape[-1]) for n in REPLICATED if n in d}


def kernel(x, mix_norm, ffn_norm, sgu_w_in, sgu_v_gain, sgu_w_s, sgu_b_s, sgu_w_out, attn_w_qkv, attn_q_gain, attn_k_gain, attn_sinks, attn_w_o, rel_bias, ffn_w_up, ffn_conv_w, ffn_conv_b, ffn_w_down, loss_target, m_mix_norm, m_ffn_norm, m_sgu_w_in, m_sgu_v_gain, m_sgu_w_s, m_sgu_b_s, m_sgu_w_out, m_attn_w_qkv, m_attn_q_gain, m_attn_k_gain, m_attn_sinks, m_attn_w_o, m_rel_bias, m_ffn_w_up, m_ffn_conv_w, m_ffn_conv_b, m_ffn_w_down, v_mix_norm, v_ffn_norm, v_sgu_w_in, v_sgu_v_gain, v_sgu_w_s, v_sgu_b_s, v_sgu_w_out, v_attn_w_qkv, v_attn_q_gain, v_attn_k_gain, v_attn_sinks, v_attn_w_o, v_rel_bias, v_ffn_w_up, v_ffn_conv_w, v_ffn_conv_b, v_ffn_w_down):
    w = dict(zip(WEIGHTS, (mix_norm, ffn_norm, sgu_w_in, sgu_v_gain, sgu_w_s, sgu_b_s, sgu_w_out, attn_w_qkv, attn_q_gain, attn_k_gain,
                           attn_sinks, attn_w_o, rel_bias, ffn_w_up, ffn_conv_w, ffn_conv_b, ffn_w_down)))
    m = dict(zip(WEIGHTS, (m_mix_norm, m_ffn_norm, m_sgu_w_in, m_sgu_v_gain, m_sgu_w_s, m_sgu_b_s, m_sgu_w_out, m_attn_w_qkv, m_attn_q_gain,
                           m_attn_k_gain, m_attn_sinks, m_attn_w_o, m_rel_bias, m_ffn_w_up, m_ffn_conv_w, m_ffn_conv_b, m_ffn_w_down)))
    v = dict(zip(WEIGHTS, (v_mix_norm, v_ffn_norm, v_sgu_w_in, v_sgu_v_gain, v_sgu_w_s, v_sgu_b_s, v_sgu_w_out, v_attn_w_qkv, v_attn_q_gain,
                           v_attn_k_gain, v_attn_sinks, v_attn_w_o, v_rel_bias, v_ffn_w_up, v_ffn_conv_w, v_ffn_conv_b, v_ffn_w_down)))
    rep = {n: w[n] for n in REPLICATED}

    xi, yi, ci = lax.axis_index("x"), lax.axis_index("y"), lax.axis_index("c")
    place = jnp.stack([ci, 2 * xi + yi]).astype(jnp.int32)
    sch = _Overlap(_send_views(w), place)
    sch.full.update(zip(GATHER_FIRST, _allgather([sch.shard[n] for n in GATHER_FIRST], "gather_first")))

    loss, grad_x, g_rep = _local_step(x[0], loss_target[0], rep, sch)
    loss = lax.psum(loss, ("x", "y", "c"))
    _exchange(sch.hook("last_a"), "last_a")
    _exchange(sch.hook("last_b"), "last_b")

    out = [{}, {}, {}, {}]
    for name, parts in SHARDED.items():
        flip = (lambda a: jnp.swapaxes(a, -1, -2)) if name in TRANSPOSED | PHYSICAL_T else (lambda a: a)
        shape = flip(w[name]).shape
        as3d = lambda a: flip(a).reshape(len(parts), -1, shape[-1])
        res = _adamw_shard([sch.own[p] for p in parts], [sch.recv[p] for p in parts], as3d(w[name]), as3d(m[name]), as3d(v[name]),
                           "adamw_" + name, flipped=name in PHYSICAL_T)
        for o, r in zip(out, res):
            o[name] = flip(r.reshape(shape))
    small = _adamw_small([sch.full[n] for n in SMALL], *[list(_views2d(d).values()) for d in (rep, m, v)], "adamw_small")
    for o, res in zip(out, small):
        o.update((n, r.reshape(w[n].shape)) for n, r in zip(REPLICATED, res))

    return (loss, grad_x[None], *[out[0][n] for n in WEIGHTS], *[out[1][n] for n in WEIGHTS],
            *[out[2][n] for n in WEIGHTS], *[out[3][n] for n in WEIGHTS])
```

```python
import functools
import math

import numpy as np
import jax
import jax.numpy as jnp
from jax import lax
from jax.experimental import pallas as pl
from jax.experimental.pallas import tpu as pltpu

F32 = jnp.float32
BF16 = jnp.bfloat16
DH = jnp.bfloat16
S = jax.ShapeDtypeStruct

D = 1024
CHUNK = 128
SGU_W = 2048
SGU_G = 16
HD = 64
NH = 16
NKV = 4
KVG = 4
D_FF = 2816
REL_BUCKETS = 32
REL_MAX_DIST = 128
EPS = 1e-6
N_DEV = 8
MESH = pl.DeviceIdType.MESH

ADAM_LR = 0.001
ADAM_B1 = 0.9
ADAM_B2 = 0.999
ADAM_EPS = 1e-08
ADAM_WD = 0.01
ADAM_STEP = 10

ROW_TILE = 512
HALO = 8
FFN_ROWS = 256


def _tm(t):
    return min(ROW_TILE, t)


def _cp(*sem):
    return pltpu.CompilerParams(dimension_semantics=sem)


ANY = pl.BlockSpec(memory_space=pl.ANY)


def _place():
    x, y, c = lax.axis_index("x"), lax.axis_index("y"), lax.axis_index("c")
    return x, y, c, [(1 - x, y), (x, 1 - y), (1 - x, 1 - y)]


class _Comm:
    SEMS = {"ag1": 5, "ag2": 3, "rs1": 4, "rs2": 3}

    def __init__(self):
        self.inputs, self.out_shapes, self.aliases, self.ops, self.n_sems = [], [], {}, [], 0

    def add(self, kind, arr):
        lead = {"ag1": N_DEV, "ag2": None, "rs1": 4, "rs2": 3}[kind]
        shape = arr.shape if lead is None else (lead,) + arr.shape[(0 if kind == "ag1" else 1):]
        if kind == "ag2":
            self.aliases[len(self.inputs)] = len(self.out_shapes)
        self.ops.append((kind, len(self.inputs), len(self.out_shapes), self.n_sems))
        self.inputs.append(arr)
        self.out_shapes.append(S(shape, arr.dtype))
        self.n_sems += self.SEMS[kind]
        return len(self.out_shapes) - 1

    def _copies(self, ins, outs, send, recv):
        x, y, c, chips = _place()
        me, sibling = (x, y, c), (x, y, 1 - c)
        slot = lambda px, py, pc: 4 * px + 2 * py + pc
        sends, recvs, local = [], [], []

        def rc(src, dst, k, to):
            return lambda: pltpu.make_async_remote_copy(src_ref=src(), dst_ref=dst(), send_sem=send.at[k], recv_sem=recv.at[k],
                                                        device_id=to, device_id_type=MESH)

        for kind, ii, oi, b in self.ops:
            src, dst = ins[ii], outs[oi]
            at = lambda ref, i: (lambda: ref.at[i])
            if kind == "ag1":
                whole, mine = (lambda s=src: s), at(dst, slot(*me))
                sends.append(rc(whole, mine, b, sibling))
                recvs.append(rc(whole, at(dst, slot(x, y, 1 - c)), b, me))
                for j, chip in enumerate(chips):
                    sends.append(rc(whole, mine, b + 1 + j, (*chip, c)))
                    recvs.append(rc(whole, at(dst, slot(*chip, c)), b + 1 + j, me))
                local.append(lambda s=src, m=mine, k=b + 4: pltpu.make_async_copy(s, m(), send.at[k]))
            elif kind == "ag2":
                for j, chip in enumerate(chips):
                    sends.append(rc(at(dst, slot(*chip, c)), at(dst, slot(*chip, c)), b + j, sibling))
                    recvs.append(rc(at(dst, slot(*chip, 1 - c)), at(dst, slot(*chip, 1 - c)), b + j, me))
            elif kind == "rs1":
                for k in range(4):
                    sends.append(rc(at(src, 2 * k + (1 - c)), at(dst, k), b + k, sibling))
                    recvs.append(rc(at(src, 2 * k + c), at(dst, k), b + k, me))
            else:
                for j, (px, py) in enumerate(chips):
                    sends.append(rc(at(src, 2 * px + py), at(dst, j), b + j, (px, py, c)))
                    recvs.append(rc(at(src, 2 * px + py), at(dst, j), b + j, me))
        return sends, recvs, local

    def start(self, ins, outs, send, recv):
        sends, _, local = self._copies(ins, outs, send, recv)
        for make in local + sends:
            make().start()

    def finish(self, ins, outs, send, recv):
        sends, recvs, local = self._copies(ins, outs, send, recv)
        for make in recvs:
            make().wait_recv()
        for make in sends:
            make().wait_send()
        for make in local:
            make().wait()


def _run(body, args, hook, *, grid, in_specs, out_specs, out_shape, name, semantics, scratch_shapes=(), aliases=None):
    comm = hook() if hook is not None else None
    aliases = dict(aliases or {})
    if comm is None:
        return pl.pallas_call(body, grid=grid, in_specs=in_specs, out_specs=out_specs, out_shape=out_shape, name=name,
                              scratch_shapes=list(scratch_shapes), input_output_aliases=aliases,
                              compiler_params=_cp(*semantics))(*args)
    single = not isinstance(out_shape, (list, tuple))
    out_shapes = [out_shape] if single else list(out_shape)
    out_specs_l = [out_specs] if single else list(out_specs)
    n_in, n_out, n_scr, ci, co = len(args), len(out_shapes), len(scratch_shapes), len(comm.inputs), len(comm.out_shapes)

    def wrapped(*refs):
        ins, cins = refs[:n_in], refs[n_in:n_in + ci]
        outs, couts = refs[n_in + ci:n_in + ci + n_out], refs[n_in + ci + n_out:n_in + ci + n_out + co]
        scr = refs[n_in + ci + n_out + co:n_in + ci + n_out + co + n_scr]
        send, recv = refs[-2:]
        first = functools.reduce(lambda a, b: a & b, [pl.program_id(a) == 0 for a in range(len(grid))])
        last = functools.reduce(lambda a, b: a & b, [pl.program_id(a) == g - 1 for a, g in enumerate(grid)])

        @pl.when(first)
        def _():
            comm.start(cins, couts, send, recv)

        body(*ins, *outs, *scr)

        @pl.when(last)
        def _():
            comm.finish(cins, couts, send, recv)

    res = pl.pallas_call(
        wrapped, grid=grid, in_specs=list(in_specs) + [ANY] * ci, out_specs=out_specs_l + [ANY] * co,
        out_shape=out_shapes + comm.out_shapes, name=name,
        scratch_shapes=list(scratch_shapes) + [pltpu.SemaphoreType.DMA((comm.n_sems,)), pltpu.SemaphoreType.DMA((comm.n_sems,))],
        input_output_aliases={**aliases, **{n_in + k: n_out + v for k, v in comm.aliases.items()}},
        compiler_params=pltpu.CompilerParams(dimension_semantics=("arbitrary",) * len(grid), has_side_effects=True))(*args, *comm.inputs)
    hook(res[n_out:])
    return res[0] if single else list(res[:n_out])


def _dot(a, b):
    return jnp.dot(a, b, preferred_element_type=F32)


def _dot_nt(a, b):
    return lax.dot_general(a, b, (((1,), (1,)), ((), ())), preferred_element_type=F32)


def _dot_tn(a, b):
    return lax.dot_general(a, b, (((0,), (0,)), ((), ())), preferred_element_type=F32)


def _gelu(x):
    return 0.5 * x * (1.0 + lax.erf(x * (2.0 ** -0.5)))


def _gelu_and_grad(x):
    cdf = 0.5 * (1.0 + lax.erf(x * (2.0 ** -0.5)))
    return x * cdf, cdf + x * jnp.exp(-0.5 * x * x) * (1.0 / math.sqrt(2.0 * math.pi))


def _sigmoid(x):
    return 1.0 / (1.0 + jnp.exp(-x))


def _rstd(x):
    return lax.rsqrt(jnp.mean(x * x, axis=-1, keepdims=True) + EPS)


def _rel_tables():
    q = np.arange(CHUNK)[:, None] + CHUNK
    k = np.arange(2 * CHUNK)[None, :]
    dist = q - k
    n = np.maximum(dist, 0)
    max_exact = REL_BUCKETS // 2
    large = max_exact + (np.log(np.maximum(n, 1).astype(np.float32) / max_exact)
                         / math.log(REL_MAX_DIST / max_exact) * (REL_BUCKETS - max_exact)).astype(np.int32)
    large = np.minimum(large, REL_BUCKETS - 1)
    return np.where(n < max_exact, n, large).astype(np.int32)


def _rmsnorm(x, gain, name):
    t = x.shape[0]
    tm = _tm(t)

    def body(x_ref, g_ref, o_ref):
        xv = x_ref[...]
        o_ref[...] = (xv * _rstd(xv) * g_ref[...]).astype(BF16)

    return pl.pallas_call(
        body, grid=(t // tm,), name=name,
        in_specs=[pl.BlockSpec((tm, D), lambda i: (i, 0)), pl.BlockSpec((1, D), lambda i: (0, 0))],
        out_specs=pl.BlockSpec((tm, D), lambda i: (i, 0)),
        out_shape=S((t, D), BF16), compiler_params=_cp("parallel"))(x, gain)


def _resident(shape):
    zeros = (0,) * len(shape)
    return pl.BlockSpec(shape, lambda *_: zeros, pipeline_mode=pl.Buffered(1))


def _mm_slot(hn, wg, out_dtype, name, hook=None):
    t, k = hn.shape
    ns, _, n = wg.shape
    tm = _tm(t)

    def body(a_ref, w_ref, o_ref):
        a = a_ref[...]
        for s in range(ns):
            o_ref[s] = _dot(a, w_ref[s]).astype(out_dtype)

    return _run(
        body, [hn, wg], hook, grid=(t // tm,), name=name, semantics=("parallel",),
        in_specs=[pl.BlockSpec((tm, k), lambda i: (i, 0)), _resident(wg.shape)],
        out_specs=pl.BlockSpec((ns, tm, n), lambda i: (0, i, 0)), out_shape=S((ns, t, n), out_dtype))


def _mm_t(hn, wt, name, hook=None):
    t, k = hn.shape
    ns, n, _ = wt.shape
    tm = _tm(t)

    def body(a_ref, w_ref, o_ref):
        a = a_ref[...]
        for s in range(ns):
            o_ref[s * n:(s + 1) * n, :] = _dot_nt(w_ref[s], a)

    return _run(
        body, [hn, wt], hook, grid=(t // tm,), name=name, semantics=("parallel",),
        in_specs=[pl.BlockSpec((tm, k), lambda i: (i, 0)), _resident(wt.shape)],
        out_specs=pl.BlockSpec((ns * n, tm), lambda i: (0, i)), out_shape=S((ns * n, t), F32))


def _conv3(a, prev, cw, cb, tm):
    ext = jnp.concatenate([prev, a], axis=0)
    return cw[2:3] * a + cw[1:2] * ext[HALO - 1:HALO - 1 + tm] + cw[0:1] * ext[HALO - 2:HALO - 2 + tm] + cb


def _ffn_fwd(hn, h, wup, wdown, cw, cb, extra, mode, name, hook=None):
    t, k = hn.shape
    n = wup.shape[-1]
    nh = wup.shape[0] // 2
    tm = min(FFN_ROWS, t)
    ni = t // tm

    def body(a_ref, h_ref, wu_ref, wd_ref, cw_ref, cb_ref, e_ref, as_ref, cs_ref, o1_ref, o2_ref, carry):
        i = pl.program_id(0)

        @pl.when(i == 0)
        def _():
            carry[...] = jnp.zeros_like(carry)

        a = a_ref[...]
        acc = h_ref[...]
        nxt = (_dot(a, wu_ref[0]), _dot(a, wu_ref[nh]))
        for j in range(nh):
            ag, av = nxt
            if j + 1 < nh:
                nxt = (_dot(a, wu_ref[j + 1]), _dot(a, wu_ref[nh + j + 1]))
            as_ref[j] = ag.astype(BF16)
            as_ref[nh + j] = av.astype(BF16)
            cg = _conv3(ag, carry[j], cw_ref[j], cb_ref[j], tm)
            cv = _conv3(av, carry[nh + j], cw_ref[nh + j], cb_ref[nh + j], tm)
            carry[j] = ag[tm - HALO:]
            carry[nh + j] = av[tm - HALO:]
            cs_ref[j] = cg.astype(BF16)
            cs_ref[nh + j] = cv.astype(BF16)
            act = (cg * _sigmoid(cg) * cv).astype(BF16)
            acc = acc + _dot(act, wd_ref[j * n:(j + 1) * n, :])
        if mode == "norm":
            o1_ref[...] = acc
            o2_ref[...] = (acc * _rstd(acc) * e_ref[...]).astype(BF16)
        else:
            err = acc - e_ref[...]
            o1_ref[...] = (err * (1.0 / D)).astype(o1_ref.dtype)
            o2_ref[...] = jnp.full(o2_ref.shape, jnp.sum(err * err), F32)

    row = pl.BlockSpec((tm, D), lambda i: (i, 0))
    if mode == "norm":
        e_spec, o2_spec, o2_shape = pl.BlockSpec((1, D), lambda i: (0, 0)), row, S((t, D), BF16)
    else:
        e_spec, o2_spec, o2_shape = row, pl.BlockSpec((None, 8, 128), lambda i: (i, 0, 0)), S((ni, 8, 128), F32)
    aspec = pl.BlockSpec((2 * nh, tm, n), lambda i: (0, i, 0))
    return _run(
        body, [hn, h, wup, wdown, cw, cb, extra], hook, grid=(ni,), name=name, semantics=("arbitrary",),
        in_specs=[pl.BlockSpec((tm, k), lambda i: (i, 0)), row, _resident(wup.shape), _resident(wdown.shape),
                  _resident(cw.shape), _resident(cb.shape), e_spec],
        out_specs=[aspec, aspec, row, o2_spec],
        out_shape=[S((2 * nh, t, n), BF16), S((2 * nh, t, n), BF16), S((t, D), F32 if mode == "norm" else DH), o2_shape],
        scratch_shapes=[pltpu.VMEM((2 * nh, HALO, n), F32)])


def _tril_mask():
    r = lax.broadcasted_iota(jnp.int32, (CHUNK, CHUNK), 0)
    c = lax.broadcasted_iota(jnp.int32, (CHUNK, CHUNK), 1)
    return r >= c


def _sgu_gate_fwd(a_s, vgain, ws, bst, name, hook=None):
    t = a_s.shape[1]
    sw = a_s.shape[2]
    gps = sw // CHUNK

    def body(a_ref, vg_ref, ws_ref, b_ref, o_ref):
        v = _gelu(jnp.concatenate([a_ref[4 + s].astype(F32) for s in range(4)], axis=1))
        vn = (v * _rstd(v) * vg_ref[...]).astype(BF16)
        tri = _tril_mask()
        for g in range(SGU_G):
            w = jnp.where(tri, ws_ref[g], 0.0).astype(BF16)
            sg = _dot(w, vn[:, g * CHUNK:(g + 1) * CHUNK]) + b_ref[:, g:g + 1]
            lo = (g % gps) * CHUNK
            u = _gelu(a_ref[g // gps, :, lo:lo + CHUNK].astype(F32))
            o_ref[g // gps, :, lo:lo + CHUNK] = (u * sg).astype(BF16)

    return _run(
        body, [a_s, vgain, ws, bst], hook, grid=(t // CHUNK,), name=name, semantics=("parallel",),
        in_specs=[pl.BlockSpec((8, CHUNK, sw), lambda n: (0, n, 0)), pl.BlockSpec((1, SGU_W), lambda n: (0, 0)),
                  pl.BlockSpec((SGU_G, CHUNK, CHUNK), lambda n: (0, 0, 0)), pl.BlockSpec((CHUNK, SGU_G), lambda n: (0, 0))],
        out_specs=pl.BlockSpec((4, CHUNK, sw), lambda n: (0, n, 0)), out_shape=S((4, t, sw), BF16))


def _resid_mm(a_s, w, resid, extra, mode, name, hook=None, fm=False):
    nk, t, kc = (1, a_s.shape[1], a_s.shape[0]) if fm else a_s.shape
    tm = _tm(t)
    ni = t // tm

    def body(a_ref, w_ref, r_ref, e_ref, o1_ref, o2_ref):
        h = r_ref[...]
        if fm:
            h = h + _dot_tn(a_ref[...], w_ref[...])
        for j in range(0 if fm else nk):
            h = h + _dot(a_ref[j], w_ref[j * kc:(j + 1) * kc, :])
        if mode == "norm":
            o1_ref[...] = h
            o2_ref[...] = (h * _rstd(h) * e_ref[...]).astype(BF16)
        else:
            err = h - e_ref[...]
            o1_ref[...] = (err * (1.0 / D)).astype(o1_ref.dtype)
            o2_ref[...] = jnp.full(o2_ref.shape, jnp.sum(err * err), F32)

    row = pl.BlockSpec((tm, D), lambda i: (i, 0))
    if mode == "norm":
        e_spec, o2_spec, o2_shape = pl.BlockSpec((1, D), lambda i: (0, 0)), row, S((t, D), BF16)
    else:
        e_spec, o2_spec, o2_shape = row, pl.BlockSpec((None, 8, 128), lambda i: (i, 0, 0)), S((ni, 8, 128), F32)
    return _run(
        body, [a_s, w, resid, extra], hook, grid=(ni,), name=name, semantics=("parallel",),
        in_specs=[pl.BlockSpec((kc, tm), lambda i: (0, i)) if fm else pl.BlockSpec((nk, tm, kc), lambda i: (0, i, 0)),
                  _resident(w.shape), row, e_spec],
        out_specs=[row, o2_spec], out_shape=[S((t, D), F32 if mode == "norm" else DH), o2_shape])


def _relbias_fwd(rel_bias_t, bucket_row, name):
    nb = bucket_row.shape[1]

    def body(rb_ref, bk_ref, o_ref):
        onehot = (lax.broadcasted_iota(jnp.int32, (REL_BUCKETS, nb), 0) == bk_ref[...]).astype(F32)
        o_ref[...] = jnp.dot(rb_ref[...], onehot, precision=lax.Precision.HIGHEST, preferred_element_type=F32)

    return pl.pallas_call(body, out_shape=S((NH, nb), F32), name=name)(rel_bias_t, bucket_row)


def _relbias_bwd(dbias, bucket_row, name):
    nb = bucket_row.shape[1]

    def body(db_ref, bk_ref, o_ref):
        onehot = (lax.broadcasted_iota(jnp.int32, (REL_BUCKETS, nb), 0) == bk_ref[...]).astype(F32)
        o_ref[...] = lax.dot_general(db_ref[...], onehot, (((1,), (1,)), ((), ())),
                                     precision=lax.Precision.HIGHEST, preferred_element_type=F32)

    return pl.pallas_call(body, out_shape=S((NH, REL_BUCKETS), F32), name=name)(dbias, bucket_row)


QKV = D + 2 * NKV * HD
KV0 = D


def _rstd_rows(x):
    return lax.rsqrt(jnp.mean(x * x, axis=0, keepdims=True) + EPS)


def _attn_valid(n):
    kj = lax.broadcasted_iota(jnp.int32, (2 * CHUNK, CHUNK), 0)
    qi = lax.broadcasted_iota(jnp.int32, (2 * CHUNK, CHUNK), 1)
    dist = qi + CHUNK - kj
    return (dist >= 0) & (dist < CHUNK) & ((n > 0) | (kj >= CHUNK))


def _attn_band(cur_ref, prev_ref, row):
    return jnp.concatenate([prev_ref[row - KV0:row - KV0 + HD, :], cur_ref[row:row + HD, :]], axis=1)


def _attn_probs(kn_tok, qn, bias, valid, sink):
    s = _dot(kn_tok, qn) * (HD ** -0.5) + bias
    s = jnp.where(valid, s, -jnp.inf)
    m = jnp.maximum(jnp.max(s, axis=0, keepdims=True), sink)
    p = jnp.exp(s - m)
    psink = jnp.exp(sink - m)
    inv = 1.0 / (jnp.sum(p, axis=0, keepdims=True) + psink)
    return p * inv, psink * inv


def _attn_fwd(qkv_t, qg, kg, sinks, bias, name, hook=None):
    t = qkv_t.shape[1]

    def body(cur_ref, prev_ref, qg_ref, kg_ref, sink_ref, bias_ref, o_ref):
        n = pl.program_id(0)
        valid = _attn_valid(n)
        ks = [_attn_band(cur_ref, prev_ref, KV0 + HD * h) for h in range(NKV)]
        kn_toks = [(k * _rstd_rows(k) * kg_ref[...]).astype(BF16).T for k in ks]
        vbs = [_attn_band(cur_ref, prev_ref, KV0 + HD * (NKV + h)).astype(BF16) for h in range(NKV)]
        qs = [cur_ref[HD * hq:HD * (hq + 1), :] for hq in range(NH)]
        qns = [(q * _rstd_rows(q) * qg_ref[...]).astype(BF16) for q in qs]
        ps = [_attn_probs(kn_toks[hq // KVG], qns[hq], bias_ref[hq], valid, sink_ref[hq])[0] for hq in range(NH)]
        for hq in range(NH):
            o_ref[HD * hq:HD * (hq + 1), :] = _dot(vbs[hq // KVG], ps[hq].astype(BF16)).astype(BF16)

    col = pl.BlockSpec((HD, 1), lambda n: (0, 0))
    return _run(
        body, [qkv_t, qkv_t, qg, kg, sinks, bias], hook, grid=(t // CHUNK,), name=name, semantics=("parallel",),
        in_specs=[pl.BlockSpec((QKV, CHUNK), lambda n: (0, n)),
                  pl.BlockSpec((QKV - KV0, CHUNK), lambda n: (KV0 // (QKV - KV0), jnp.maximum(n - 1, 0))),
                  col, col, pl.BlockSpec(memory_space=pltpu.SMEM), pl.BlockSpec((NH, 2 * CHUNK, CHUNK), lambda n: (0, 0, 0))],
        out_specs=pl.BlockSpec((D, CHUNK), lambda n: (0, n)), out_shape=S((D, t), BF16))


def _dx_rows(dh, w, kc, out_dtype, name, hook=None):
    t = dh.shape[0]
    nk = w.shape[0] // kc
    tm = _tm(t)

    def body(d_ref, w_ref, o_ref):
        dhb = d_ref[...].astype(BF16)
        for j in range(nk):
            o_ref[j] = _dot_nt(dhb, w_ref[j * kc:(j + 1) * kc, :]).astype(out_dtype)

    return _run(
        body, [dh, w], hook, grid=(t // tm,), name=name, semantics=("parallel",),
        in_specs=[pl.BlockSpec((tm, D), lambda i: (i, 0)), _resident(w.shape)],
        out_specs=pl.BlockSpec((nk, tm, kc), lambda i: (0, i, 0)), out_shape=S((nk, t, kc), out_dtype))


def _dx_rows_t(dh, w, name, hook=None):
    t = dh.shape[0]
    k = w.shape[0]
    tm = _tm(t)

    def body(d_ref, w_ref, o_ref):
        o_ref[...] = _dot_nt(w_ref[...], d_ref[...].astype(BF16)).astype(BF16)

    return _run(
        body, [dh, w], hook, grid=(t // tm,), name=name, semantics=("parallel",),
        in_specs=[pl.BlockSpec((tm, D), lambda i: (i, 0)), _resident(w.shape)],
        out_specs=pl.BlockSpec((k, tm), lambda i: (0, i)), out_shape=S((k, t), BF16))


def _ffn_bwd1(dh, c, wdown, name, hook=None):
    ns, t, n = c.shape
    nh = ns // 2
    tm = min(FFN_ROWS, t)
    ni = t // tm

    def body(d_ref, c_ref, wd_ref, dc_ref, dw_hbm, dwb_hbm, acc, stage):
        i = pl.program_id(0)

        @pl.when(i == 0)
        def _():
            acc[...] = jnp.zeros_like(acc)

        dhb = d_ref[...].astype(BF16)
        for j in range(nh):
            dact = _dot_nt(dhb, wd_ref[j * n:(j + 1) * n, :])
            cg = c_ref[j].astype(F32)
            cv = c_ref[nh + j].astype(F32)
            sg = _sigmoid(cg)
            gs = cg * sg
            acc[j * n:(j + 1) * n, :] += _dot_tn((gs * cv).astype(BF16), dhb)
            dc_ref[j] = (dact * cv * (sg + gs * (1.0 - sg))).astype(BF16)
            dc_ref[nh + j] = (dact * gs).astype(BF16)

        @pl.when(i == ni - 1)
        def _():
            pltpu.sync_copy(acc, dw_hbm)
            for j in range(nh):
                stage[...] = acc[j * n:(j + 1) * n, :].astype(BF16)
                pltpu.sync_copy(stage, dwb_hbm.at[pl.ds(j * n, n), :])

    slab = pl.BlockSpec((ns, tm, n), lambda i: (0, i, 0))
    return _run(
        body, [dh, c, wdown], hook, grid=(ni,), name=name, semantics=("arbitrary",),
        in_specs=[pl.BlockSpec((tm, D), lambda i: (i, 0)), slab, _resident(wdown.shape)],
        out_specs=[slab, ANY, ANY], out_shape=[S((ns, t, n), BF16), S(wdown.shape, F32), S(wdown.shape, BF16)],
        scratch_shapes=[pltpu.VMEM(wdown.shape, F32), pltpu.VMEM((n, D), BF16)])


def _ffn_bwd2(dc, a, wup, cw, h, gain, dh_in, name, hook=None):
    ns, t, n = dc.shape
    tm = min(FFN_ROWS, t)
    ni = t // tm

    def body(dc_ref, a_ref, wu_ref, cw_ref, h_ref, g_ref, di_ref, da_ref, o_ref, dg_ref, dcw_ref, dcb_ref, carry, keep):
        i = pl.program_id(0)

        @pl.when(i == 0)
        def _():
            carry[...] = jnp.zeros_like(carry)
            dg_ref[...] = jnp.zeros_like(dg_ref)
            dcw_ref[...] = jnp.zeros_like(dcw_ref)
            dcb_ref[...] = jnp.zeros_like(dcb_ref)

        rsum = lambda v: jnp.sum(v, axis=0, keepdims=True)
        acc = jnp.zeros((tm, D), F32)
        for s in range(ns):
            x = dc_ref[s].astype(F32)
            ext = jnp.concatenate([x, carry[s]], axis=0)
            keep[0] = ext[1:1 + tm]
            keep[1] = ext[2:2 + tm]
            x1, x2 = keep[0], keep[1]
            cwv = cw_ref[s]
            da = (cwv[2:3] * x + cwv[1:2] * x1 + cwv[0:1] * x2).astype(BF16)
            carry[s] = x[:HALO]
            da_ref[s] = da
            acc = acc + _dot_nt(da, wu_ref[s])
            av = a_ref[s].astype(F32)
            dcw_ref[s] += jnp.concatenate([rsum(x2 * av), rsum(x1 * av), rsum(x * av)], axis=0)
            dcb_ref[s] += rsum(x)
        hv = h_ref[...]
        r = _rstd(hv)
        gg = acc * g_ref[...]
        dh_new = di_ref[...].astype(F32) + r * gg - hv * (r * r * r * jnp.mean(gg * hv, axis=-1, keepdims=True))
        o_ref[...] = dh_new.astype(o_ref.dtype)
        dg_ref[...] += jnp.sum(acc * hv * r, axis=0, keepdims=True)

    slab = pl.BlockSpec((ns, tm, n), lambda i: (0, ni - 1 - i, 0))
    row = pl.BlockSpec((tm, D), lambda i: (ni - 1 - i, 0))
    vec = pl.BlockSpec((1, D), lambda i: (0, 0))
    whole = lambda shape: pl.BlockSpec(shape, lambda i: (0,) * len(shape))
    return _run(
        body, [dc, a, wup, cw, h, gain, dh_in], hook, grid=(ni,), name=name, semantics=("arbitrary",),
        in_specs=[slab, slab, _resident(wup.shape), _resident(cw.shape), row, vec, row],
        out_specs=[slab, row, vec, whole((ns, 3, n)), whole((ns, 1, n))],
        out_shape=[S((ns, t, n), BF16), S((t, D), DH), S((1, D), F32), S((ns, 3, n), F32), S((ns, 1, n), F32)],
        scratch_shapes=[pltpu.VMEM((ns, HALO, n), F32), pltpu.VMEM((2, tm, n), F32)])


def _dw_slot(hn, dy_s, name, hook=None):
    t, k = hn.shape
    ns, _, n = dy_s.shape
    tm = _tm(t)

    def body(a_ref, b_ref, o_ref, ob_ref, at_ref):
        @pl.when(pl.program_id(0) == 0)
        def _():
            for i in range(t // tm):
                at_ref[:, i * tm:(i + 1) * tm] = a_ref[i * tm:(i + 1) * tm, :].T

        acc = _dot(at_ref[...], b_ref[...])
        o_ref[...] = acc
        ob_ref[...] = acc.astype(BF16)

    ospec = pl.BlockSpec((None, k, n), lambda j: (j, 0, 0))
    return _run(
        body, [hn, dy_s], hook, grid=(ns,), name=name, semantics=("arbitrary",),
        in_specs=[_resident(hn.shape), pl.BlockSpec((None, t, n), lambda j: (j, 0, 0))],
        out_specs=[ospec, ospec], out_shape=[S((ns, k, n), F32), S((ns, k, n), BF16)],
        scratch_shapes=[pltpu.VMEM((k, t), BF16)])


def _dw_rows(a_s, dh, name, hook=None, fm=False):
    nk, t, kc = (1, a_s.shape[1], a_s.shape[0]) if fm else a_s.shape
    tm = _tm(t)
    ni = t // tm

    def body(a_ref, d_ref, o_ref, ob_ref):
        i = pl.program_id(0)
        dhb = d_ref[...].astype(BF16)

        @pl.when(i == 0)
        def _():
            o_ref[...] = jnp.zeros_like(o_ref)

        if fm:
            o_ref[...] += _dot(a_ref[...], dhb)
        for j in range(0 if fm else nk):
            o_ref[j * kc:(j + 1) * kc, :] += _dot_tn(a_ref[j], dhb)

        @pl.when(i == ni - 1)
        def _():
            ob_ref[...] = o_ref[...].astype(BF16)

    ospec = pl.BlockSpec((nk * kc, D), lambda i: (0, 0))
    return _run(
        body, [a_s, dh], hook, grid=(ni,), name=name, semantics=("arbitrary",),
        in_specs=[pl.BlockSpec((kc, tm), lambda i: (0, i)) if fm else pl.BlockSpec((nk, tm, kc), lambda i: (0, i, 0)),
                  pl.BlockSpec((tm, D), lambda i: (i, 0))],
        out_specs=[ospec, ospec], out_shape=[S((nk * kc, D), F32), S((nk * kc, D), BF16)])


def _dx_slot_normbwd(dy_s, wg, h, gain, dh_in, name, hook=None, fm=False, out_dtype=F32):
    ns, t, n = (1, dy_s.shape[1], dy_s.shape[0]) if fm else dy_s.shape
    tm = _tm(t)

    def body(dy_ref, w_ref, h_ref, g_ref, di_ref, o_ref, dg_ref):
        i = pl.program_id(0)

        @pl.when(i == 0)
        def _():
            dg_ref[...] = jnp.zeros_like(dg_ref)

        g = _dot_tn(dy_ref[...], w_ref[...]) if fm else _dot_nt(dy_ref[0], w_ref[0])
        for s in range(1, ns):
            g = g + _dot_nt(dy_ref[s], w_ref[s])
        hv = h_ref[...]
        r = _rstd(hv)
        gg = g * g_ref[...]
        dh_new = di_ref[...].astype(F32) + r * gg - hv * (r * r * r * jnp.mean(gg * hv, axis=-1, keepdims=True))
        o_ref[...] = dh_new.astype(o_ref.dtype)
        dg_ref[...] += jnp.sum(g * hv * r, axis=0, keepdims=True)

    row = pl.BlockSpec((tm, D), lambda i: (i, 0))
    vec = pl.BlockSpec((1, D), lambda i: (0, 0))
    return _run(
        body, [dy_s, wg, h, gain, dh_in], hook, grid=(t // tm,), name=name, semantics=("arbitrary",),
        in_specs=[pl.BlockSpec((n, tm), lambda i: (0, i)) if fm else pl.BlockSpec((ns, tm, n), lambda i: (0, i, 0)),
                  _resident(wg.shape), row, vec, row],
        out_specs=[row, vec], out_shape=[S((t, D), out_dtype), S((1, D), F32)])


def _sgu_gate_bwd(a_s, dg_s, vgain, ws, bst, name, hook=None):
    t = a_s.shape[1]
    sw = a_s.shape[2]
    gps = sw // CHUNK

    def body(a_ref, dg_ref, vg_ref, ws_ref, b_ref, da_ref, dws_ref, dbt_ref, dvg_ref, dvn_ref):
        n = pl.program_id(0)

        @pl.when(n == 0)
        def _():
            dws_ref[...] = jnp.zeros_like(dws_ref)
            dbt_ref[...] = jnp.zeros_like(dbt_ref)
            dvg_ref[...] = jnp.zeros_like(dvg_ref)

        vpre = jnp.concatenate([a_ref[4 + s].astype(F32) for s in range(4)], axis=1)
        v, v_grad = _gelu_and_grad(vpre)
        r = _rstd(v)
        vhat = v * r
        vn = (vhat * vg_ref[...]).astype(BF16)
        tri = _tril_mask()
        lane = lax.broadcasted_iota(jnp.int32, (CHUNK, CHUNK), 1)
        dbt = jnp.zeros((CHUNK, CHUNK), F32)
        for g in range(SGU_G):
            w = jnp.where(tri, ws_ref[g], 0.0).astype(BF16)
            vng = vn[:, g * CHUNK:(g + 1) * CHUNK]
            sg = _dot(w, vng) + b_ref[:, g:g + 1]
            lo = (g % gps) * CHUNK
            u, u_grad = _gelu_and_grad(a_ref[g // gps, :, lo:lo + CHUNK].astype(F32))
            dgate = dg_ref[g // gps, :, lo:lo + CHUNK].astype(F32)
            da_ref[g // gps, :, lo:lo + CHUNK] = (dgate * sg * u_grad).astype(BF16)
            ds = dgate * u
            dsb = ds.astype(BF16)
            dvn_ref[:, g * CHUNK:(g + 1) * CHUNK] = _dot_tn(w, dsb)
            dws_ref[g] += jnp.where(tri, _dot_nt(dsb, vng), 0.0)
            dbt = dbt + jnp.where(lane == g, jnp.sum(ds, axis=-1, keepdims=True), 0.0)
        dbt_ref[...] += dbt
        dvn = dvn_ref[...]
        dvg_ref[...] += jnp.sum(dvn * vhat, axis=0, keepdims=True)
        gg = dvn * vg_ref[...]
        dv = r * gg - v * (r * r * r * jnp.mean(gg * v, axis=-1, keepdims=True))
        dav = (dv * v_grad).astype(BF16)
        for s in range(4):
            da_ref[4 + s] = dav[:, s * sw:(s + 1) * sw]

    return _run(
        body, [a_s, dg_s, vgain, ws, bst], hook, grid=(t // CHUNK,), name=name, semantics=("arbitrary",),
        in_specs=[pl.BlockSpec((8, CHUNK, sw), lambda n: (0, n, 0)), pl.BlockSpec((4, CHUNK, sw), lambda n: (0, n, 0)),
                  pl.BlockSpec((1, SGU_W), lambda n: (0, 0)), pl.BlockSpec((SGU_G, CHUNK, CHUNK), lambda n: (0, 0, 0)),
                  pl.BlockSpec((CHUNK, SGU_G), lambda n: (0, 0))],
        out_specs=[pl.BlockSpec((8, CHUNK, sw), lambda n: (0, n, 0)), pl.BlockSpec((SGU_G, CHUNK, CHUNK), lambda n: (0, 0, 0)),
                   pl.BlockSpec((CHUNK, CHUNK), lambda n: (0, 0)), pl.BlockSpec((1, SGU_W), lambda n: (0, 0))],
        out_shape=[S((8, t, sw), BF16), S((SGU_G, CHUNK, CHUNK), F32), S((CHUNK, CHUNK), F32), S((1, SGU_W), F32)],
        scratch_shapes=[pltpu.VMEM((CHUNK, SGU_W), F32)])


def _attn_bwd(qkv_t, do_t, qg, kg, sinks, bias, name, hook=None):
    t = qkv_t.shape[1]
    nb = t // CHUNK

    def body(cur_ref, prev_ref, do_ref, qg_ref, kg_ref, sink_ref, bias_ref,
             o_ref, dqg_out, dkg_out, dsk_out, dbias_ref, carry, dqg_ref, dkg_ref, dsk_ref):
        n = pl.program_id(0)

        @pl.when(n == 0)
        def _():
            carry[...] = jnp.zeros_like(carry)
            dqg_ref[...] = jnp.zeros_like(dqg_ref)
            dkg_ref[...] = jnp.zeros_like(dkg_ref)
            dsk_ref[...] = jnp.zeros_like(dsk_ref)
            dbias_ref[...] = jnp.zeros_like(dbias_ref)

        @pl.when(n < nb)
        def _():
            valid = _attn_valid(n)
            o_ref[0:KV0, :] = carry[0:KV0, :].astype(BF16)
            kvs, heads = range(NKV), range(NH)
            group = lambda h: range(KVG * h, KVG * (h + 1))
            ks = [_attn_band(cur_ref, prev_ref, KV0 + HD * h) for h in kvs]
            rks = [_rstd_rows(k) for k in ks]
            khats = [k * rk for k, rk in zip(ks, rks)]
            kns = [(khat * kg_ref[...]).astype(BF16) for khat in khats]
            kn_toks = [kn.T for kn in kns]
            vbs = [_attn_band(cur_ref, prev_ref, KV0 + HD * (NKV + h)).astype(BF16) for h in kvs]
            v_toks = [vb.T for vb in vbs]
            qs = [cur_ref[HD * hq:HD * (hq + 1), :] for hq in heads]
            rqs = [_rstd_rows(q) for q in qs]
            qhats = [q * rq for q, rq in zip(qs, rqs)]
            qns = [(qhat * qg_ref[...]).astype(BF16) for qhat in qhats]
            probs = [_attn_probs(kn_toks[hq // KVG], qns[hq], bias_ref[hq], valid, sink_ref[hq]) for hq in heads]
            dohs = [do_ref[HD * hq:HD * (hq + 1), :] for hq in heads]
            dps = [_dot(v_toks[hq // KVG], dohs[hq]) for hq in heads]
            dsums = [jnp.sum(p * dp, axis=0, keepdims=True) for (p, _), dp in zip(probs, dps)]
            dss = [p * (dp - dsum) for (p, _), dp, dsum in zip(probs, dps, dsums)]
            for hq in heads:
                dsk_ref[hq:hq + 1, :] -= probs[hq][1] * dsums[hq]
                dbias_ref[hq] += dss[hq]
            dvs = [sum(_dot_nt(dohs[hq], probs[hq][0].astype(BF16)) for hq in group(h)) for h in kvs]
            dscs = [(ds * (HD ** -0.5)).astype(BF16) for ds in dss]
            dqns = [_dot(kns[hq // KVG], dscs[hq]) for hq in heads]
            dkns = [sum(_dot_nt(qns[hq], dscs[hq]) for hq in group(h)) for h in kvs]
            dqg_ref[...] += sum(dqn * qhat for dqn, qhat in zip(dqns, qhats))
            for hq in heads:
                gq = dqns[hq] * qg_ref[...]
                carry[HD * hq:HD * (hq + 1), :] = rqs[hq] * gq - qs[hq] * (
                    rqs[hq] * rqs[hq] * rqs[hq] * jnp.mean(gq * qs[hq], axis=0, keepdims=True))
            dkg_ref[...] += sum(dkn * khat for dkn, khat in zip(dkns, khats))
            for h in kvs:
                krow, vrow = KV0 + HD * h, KV0 + HD * (NKV + h)
                gk = dkns[h] * kg_ref[...]
                dk = rks[h] * gk - ks[h] * (rks[h] * rks[h] * rks[h] * jnp.mean(gk * ks[h], axis=0, keepdims=True))
                o_ref[krow:krow + HD, :] = (carry[krow:krow + HD, :] + dk[:, :CHUNK]).astype(BF16)
                o_ref[vrow:vrow + HD, :] = (carry[vrow:vrow + HD, :] + dvs[h][:, :CHUNK]).astype(BF16)
                carry[krow:krow + HD, :] = dk[:, CHUNK:]
                carry[vrow:vrow + HD, :] = dvs[h][:, CHUNK:]

        @pl.when(n == nb)
        def _():
            o_ref[...] = carry[...].astype(BF16)
            dqg_out[...] = jnp.sum(dqg_ref[...], axis=1, keepdims=True)
            dkg_out[...] = jnp.sum(dkg_ref[...], axis=1, keepdims=True)
            dsk_out[...] = jnp.sum(dsk_ref[...], axis=1, keepdims=True)

    cur = lambda n: (0, jnp.minimum(n, nb - 1))
    col = pl.BlockSpec((HD, 1), lambda n: (0, 0))
    whole = lambda shape: pl.BlockSpec(shape, lambda n: (0,) * len(shape))
    return _run(
        body, [qkv_t, qkv_t, do_t, qg, kg, sinks, bias], hook, grid=(nb + 1,), name=name, semantics=("arbitrary",),
        in_specs=[pl.BlockSpec((QKV, CHUNK), cur),
                  pl.BlockSpec((QKV - KV0, CHUNK), lambda n: (KV0 // (QKV - KV0), jnp.clip(n - 1, 0, nb - 1))),
                  pl.BlockSpec((D, CHUNK), cur), col, col, pl.BlockSpec(memory_space=pltpu.SMEM), whole((NH, 2 * CHUNK, CHUNK))],
        out_specs=[pl.BlockSpec((QKV, CHUNK), lambda n: (0, jnp.maximum(n - 1, 0))), whole((HD, 1)), whole((HD, 1)),
                   whole((NH, 1)), whole((NH, 2 * CHUNK, CHUNK))],
        out_shape=[S((QKV, t), BF16), S((HD, 1), F32), S((HD, 1), F32), S((NH, 1), F32), S((NH, 2 * CHUNK, CHUNK), F32)],
        scratch_shapes=[pltpu.VMEM((QKV, CHUNK), F32), pltpu.VMEM((HD, CHUNK), F32), pltpu.VMEM((HD, 2 * CHUNK), F32),
                        pltpu.VMEM((NH, CHUNK), F32)])


class _Plain:
    def __init__(self, wg):
        self.full, self.grads = wg, {}

    def w(self, n):
        return self.full[n]

    def hook(self, host):
        return None

    def grad(self, n, pair):
        self.grads[n] = pair

    def small(self, g_rep):
        pass


def _local_step(x, target, rep, sch):
    bucket_row = jnp.asarray(_rel_tables().T.reshape(1, -1))
    bias = _relbias_fwd(rep["rel_bias"].T, bucket_row, "relbias_fwd").reshape(NH, 2 * CHUNK, CHUNK)
    bst = rep["sgu_b_s"][0].T
    ws = rep["sgu_w_s"][0]
    vgain = rep["sgu_v_gain"]
    qg, kg, sinks = rep["attn_q_gain"].reshape(HD, 1), rep["attn_k_gain"].reshape(HD, 1), rep["attn_sinks"][0]
    w_down = lambda l: sch.w("ffn_w_down%d" % l).reshape(D_FF, D)
    w_up = lambda l: sch.w("ffn_w_up%d" % l)
    cw = [sch.w("ffn_conv_w")[:, 3 * l:3 * l + 3] for l in range(2)]
    cb = [rep["ffn_conv_b"][l].reshape(8, 1, -1) for l in range(2)]
    mixg = [rep["mix_norm"][l:l + 1] for l in range(2)]
    ffng = [rep["ffn_norm"][l:l + 1] for l in range(2)]
    rows = lambda pair: tuple(g.reshape(N_DEV, -1, D) for g in pair)
    hk = sch.hook

    hn0 = _rmsnorm(x, mixg[0], "norm0")
    a0 = _mm_slot(hn0, sch.w("sgu_w_in"), BF16, "sgu_in", hk("sgu_in"))
    gated = _sgu_gate_fwd(a0, vgain, ws, bst, "sgu_gate", hk("sgu_gate"))
    h1, hn1 = _resid_mm(gated, sch.w("sgu_w_out").reshape(SGU_W, D), x, ffng[0], "norm", "sgu_out", hk("sgu_out"))
    a_ff0, c_ff0, h2, hn2 = _ffn_fwd(hn1, h1, w_up(0), w_down(0), cw[0], cb[0], mixg[1], "norm", "ffn0_fwd", hk("ffn0_fwd"))
    qkv = _mm_t(hn2, sch.w("attn_w_qkv"), "qkv", hk("qkv"))
    o = _attn_fwd(qkv, qg, kg, sinks, bias, "attn", hk("attn"))
    h3, hn3 = _resid_mm(o, sch.w("attn_w_o").reshape(D, D), h2, ffng[1], "norm", "attn_out", hk("attn_out"), fm=True)
    a_ff1, c_ff1, dy, sq = _ffn_fwd(hn3, h3, w_up(1), w_down(1), cw[1], cb[1], target, "loss", "ffn1_fwd_loss", hk("ffn1_fwd_loss"))
    loss = (0.5 / D) * jnp.sum(sq[:, 0, 0])

    def ffn_bwd(dh, h_in, hn, a, c, l, tag):
        dc, g_down, g_down_b = _ffn_bwd1(dh, c, w_down(l), tag + "_bwd1", hk(tag + "_bwd1"))
        sch.grad("ffn_w_down%d" % l, rows((g_down, g_down_b)))
        da, dh_new, dgain, g_cw, g_cb = _ffn_bwd2(dc, a, w_up(l), cw[l], h_in, ffng[l], dh, tag + "_bwd2", hk(tag + "_bwd2"))
        sch.grad("ffn_w_up%d" % l, _dw_slot(hn, da, tag + "_dw_up", hk(tag + "_dw_up")))
        return dh_new, dgain, g_cw, g_cb.reshape(-1)

    dh, d_ffng1, g_cw1, g_cb1 = ffn_bwd(dy, h3, hn3, a_ff1, c_ff1, 1, "ffn1")
    do = _dx_rows_t(dh, sch.w("attn_w_o").reshape(D, D), "attn_do", hk("attn_do"))
    sch.grad("attn_w_o", rows(_dw_rows(o, dh, "dw_o", hk("dw_o"), fm=True)))
    dqkv, d_qg, d_kg, d_sk, d_bias = _attn_bwd(qkv, do, qg, kg, sinks, bias, "attn_bwd", hk("attn_bwd"))
    sch.grad("attn_w_qkv", tuple(g.reshape(N_DEV, -1, D) for g in _dw_rows(dqkv, hn2, "dw_qkv", hk("dw_qkv"), fm=True)))
    dh, d_mixg1 = _dx_slot_normbwd(dqkv, sch.w("attn_w_qkv").reshape(QKV, D), h2, mixg[1], dh, "dx_qkv", hk("dx_qkv"), fm=True,
                                   out_dtype=DH)
    d_relb = _relbias_bwd(d_bias.reshape(NH, -1), bucket_row, "relbias_bwd").T
    g_rep = {"attn_q_gain": d_qg.reshape(1, HD), "attn_k_gain": d_kg.reshape(1, HD), "attn_sinks": d_sk.reshape(1, NH),
             "rel_bias": d_relb}
    sch.small(g_rep)
    dh, d_ffng0, g_cw0, g_cb0 = ffn_bwd(dh, h1, hn1, a_ff0, c_ff0, 0, "ffn0")
    g_cw = jnp.concatenate([g_cw0, g_cw1], axis=1)
    sch.grad("ffn_conv_w", (g_cw, g_cw.astype(BF16)))
    g_ffn = {"ffn_norm": jnp.concatenate([d_ffng0, d_ffng1], axis=0), "ffn_conv_b": jnp.stack([g_cb0, g_cb1], axis=0)}
    sch.small(g_ffn)
    dgated = _dx_rows(dh, sch.w("sgu_w_out").reshape(SGU_W, D), SGU_W // 4, BF16, "sgu_dgated", hk("sgu_dgated"))
    sch.grad("sgu_w_out", rows(_dw_rows(gated, dh, "dw_sgu_out", hk("dw_sgu_out"))))
    da0, d_ws, d_bst, d_vgain = _sgu_gate_bwd(a0, dgated, vgain, ws, bst, "sgu_gate_bwd", hk("sgu_gate_bwd"))
    grad_x, d_mixg0 = _dx_slot_normbwd(da0, sch.w("sgu_w_in"), x, mixg[0], dh, "dx_sgu_in")
    g_sgu = {"sgu_v_gain": d_vgain, "sgu_w_s": d_ws[None], "sgu_b_s": d_bst[:, :SGU_G].T[None],
             "mix_norm": jnp.concatenate([d_mixg0, d_mixg1], axis=0)}
    sch.small(g_sgu)
    g_rep.update(g_ffn)
    g_rep.update(g_sgu)
    sch.grad("sgu_w_in", _dw_slot(hn0, da0, "dw_sgu_in", hk("dw_sgu_in")))
    return loss, grad_x, g_rep


def _allgather(xs, name):
    nt = len(xs)

    def body(*refs):
        x_refs, o_refs = refs[:nt], refs[nt:2 * nt]
        send_sems, recv_sems, local_sems = refs[2 * nt:]
        x, y, c, chips = _place()
        me, sibling = (x, y, c), (x, y, 1 - c)

        def copy(t, k, block, to, src=None):
            px, py, pc = block
            dst = o_refs[t].at[4 * px + 2 * py + pc]
            return pltpu.make_async_remote_copy(
                src_ref=dst if src is None else src, dst_ref=dst, send_sem=send_sems.at[t, k], recv_sem=recv_sems.at[t, k],
                device_id=to, device_id_type=MESH)

        mine = [pltpu.make_async_copy(x_refs[t], o_refs[t].at[4 * x + 2 * y + c], local_sems.at[t]) for t in range(nt)]
        for cp in mine:
            cp.start()
        first = []
        for t in range(nt):
            first.append(copy(t, 0, me, sibling, src=x_refs[t]))
            first += [copy(t, 1 + j, me, (*chip, c), src=x_refs[t]) for j, chip in enumerate(chips)]
        for cp in first:
            cp.start()
        passed = []
        for j, chip in enumerate(chips):
            for t in range(nt):
                copy(t, 1 + j, (*chip, c), me).wait_recv()
                fwd = copy(t, 4 + j, (*chip, c), sibling)
                fwd.start()
                passed.append(fwd)
        for t in range(nt):
            copy(t, 0, sibling, me).wait_recv()
            for j, chip in enumerate(chips):
                copy(t, 4 + j, (*chip, 1 - c), me).wait_recv()
        for cp in first + passed:
            cp.wait_send()
        for cp in mine:
            cp.wait()

    return pl.pallas_call(
        body, name=name, in_specs=[ANY] * nt, out_specs=[ANY] * nt,
        out_shape=[S((N_DEV,) + a.shape, a.dtype) for a in xs],
        scratch_shapes=[pltpu.SemaphoreType.DMA((nt, 7)), pltpu.SemaphoreType.DMA((nt, 7)), pltpu.SemaphoreType.DMA((nt,))],
        compiler_params=pltpu.CompilerParams(has_side_effects=True))(*xs)


def _exchange(hook, name):
    comm = hook()
    ci, co = len(comm.inputs), len(comm.out_shapes)

    def body(*refs):
        cins, couts = refs[:ci], refs[ci:ci + co]
        send, recv = refs[-2:]
        comm.start(cins, couts, send, recv)
        comm.finish(cins, couts, send, recv)

    res = pl.pallas_call(
        body, name=name, in_specs=[ANY] * ci, out_specs=[ANY] * co, out_shape=comm.out_shapes,
        scratch_shapes=[pltpu.SemaphoreType.DMA((comm.n_sems,)), pltpu.SemaphoreType.DMA((comm.n_sems,))],
        input_output_aliases=dict(comm.aliases),
        compiler_params=pltpu.CompilerParams(has_side_effects=True))(*comm.inputs)
    hook(res)


def _row_tile(r):
    tr = r if r <= ROW_TILE or r % ROW_TILE else ROW_TILE
    assert r % tr == 0
    return tr


def _rs_partial(g32, sib, place, name):
    _, r, cdim = g32.shape
    tr = _row_tile(r)

    def body(place_ref, g_ref, s_ref, p_ref, own_ref):
        k = pl.program_id(1)
        tot = g_ref[...] + s_ref[...].astype(F32)
        p_ref[...] = tot.astype(BF16)

        @pl.when(k == place_ref[1])
        def _():
            own_ref[...] = tot

    grid_spec = pltpu.PrefetchScalarGridSpec(
        num_scalar_prefetch=1, grid=(r // tr, 4),
        in_specs=[pl.BlockSpec((None, None, tr, cdim), lambda i, k, pr: (k, pr[0], i, 0)),
                  pl.BlockSpec((None, tr, cdim), lambda i, k, pr: (k, i, 0))],
        out_specs=[pl.BlockSpec((None, tr, cdim), lambda i, k, pr: (k, i, 0)), pl.BlockSpec((tr, cdim), lambda i, k, pr: (i, 0))])
    return pl.pallas_call(
        body, grid_spec=grid_spec, name=name,
        out_shape=[S((4, r, cdim), BF16), S((r, cdim), F32)],
        compiler_params=_cp("parallel", "arbitrary"))(place, g32.reshape(4, 2, r, cdim), sib)


def _adamw_math(w, g, m, v):
    m = ADAM_B1 * m + (1.0 - ADAM_B1) * g
    v = ADAM_B2 * v + (1.0 - ADAM_B2) * (g * g)
    m_hat = m / (1.0 - ADAM_B1 ** ADAM_STEP)
    v_hat = v / (1.0 - ADAM_B2 ** ADAM_STEP)
    delta = -ADAM_LR * (m_hat / (jnp.sqrt(v_hat) + ADAM_EPS) + ADAM_WD * w)
    return delta, m, v


def _adamw_shard(owns, recvs, w, m, v, name, flipped=False):
    nl = w.shape[0]
    r, cdim = owns[0].shape
    tr = _row_tile(r)
    nr = r // tr

    def body(*refs):
        own_refs, recv_refs = refs[:nl], refs[nl:2 * nl]
        w_ref, m_ref, v_ref, g_out, d_out, m_out, v_out = refs[2 * nl:]
        layer = pl.program_id(0)
        g = None
        for l in range(nl):
            gl = own_refs[l][...] + recv_refs[l][0].astype(F32) + recv_refs[l][1].astype(F32) + recv_refs[l][2].astype(F32)
            g = gl if g is None else jnp.where(layer == l, gl, g)
        if flipped:
            g = g.T
        g_out[...] = g
        d_out[...], m_out[...], v_out[...] = _adamw_math(w_ref[...], g, m_ref[...], v_ref[...])

    park = lambda l: (lambda layer, i: (jnp.where(layer == l, i, jnp.where(layer < l, 0, nr - 1)), 0))
    park3 = lambda l: (lambda layer, i: (0, jnp.where(layer == l, i, jnp.where(layer < l, 0, nr - 1)), 0))
    if flipped:
        row = pl.BlockSpec((None, cdim, tr), lambda layer, i: (layer, 0, i))
    else:
        row = pl.BlockSpec((None, tr, cdim), lambda layer, i: (layer, i, 0))
    return pl.pallas_call(
        body, grid=(nl, nr), name=name,
        in_specs=[pl.BlockSpec((tr, cdim), park(l)) for l in range(nl)] + [pl.BlockSpec((3, tr, cdim), park3(l)) for l in range(nl)]
        + [row, row, row],
        out_specs=[row] * 4, out_shape=[S(w.shape, F32)] * 4,
        compiler_params=_cp("arbitrary", "arbitrary"))(*owns, *recvs, w, m, v)


def _adamw_small(galls, ws, ms, vs, name):
    n = len(galls)

    def body(*refs):
        g_refs, w_refs, m_refs, v_refs, outs = refs[:n], refs[n:2 * n], refs[2 * n:3 * n], refs[3 * n:4 * n], refs[4 * n:]
        for i in range(n):
            g = g_refs[i][0].astype(F32)
            for s in range(1, N_DEV):
                g = g + g_refs[i][s].astype(F32)
            outs[i][...] = g
            outs[n + i][...], outs[2 * n + i][...], outs[3 * n + i][...] = _adamw_math(w_refs[i][...], g, m_refs[i][...], v_refs[i][...])

    res = pl.pallas_call(body, out_shape=[S(a.shape, F32) for a in ws] * 4, name=name)(*galls, *ws, *ms, *vs)
    return [res[k * n:(k + 1) * n] for k in range(4)]


REPLICATED = ["mix_norm", "ffn_norm", "sgu_v_gain", "sgu_w_s", "sgu_b_s", "attn_q_gain", "attn_k_gain", "attn_sinks", "rel_bias",
              "ffn_conv_b"]
WEIGHTS = ["mix_norm", "ffn_norm", "sgu_w_in", "sgu_v_gain", "sgu_w_s", "sgu_b_s", "sgu_w_out", "attn_w_qkv", "attn_q_gain",
           "attn_k_gain", "attn_sinks", "attn_w_o", "rel_bias", "ffn_w_up", "ffn_conv_w", "ffn_conv_b", "ffn_w_down"]
SMALL = ["g_" + n for n in REPLICATED]
BF16_TRANSIT = {"sgu_w_s"}
SMALL_ATTN = ["g_attn_q_gain", "g_attn_k_gain", "g_attn_sinks", "g_rel_bias"]
SMALL_FFN = ["g_ffn_norm", "g_ffn_conv_b"]
SMALL_LATE = [n for n in SMALL if n not in SMALL_ATTN + SMALL_FFN]

GATHER_FIRST = ["sgu_w_in", "ffn_conv_w"]
PLAN = {
    "sgu_in": [("ag1", "sgu_w_out"), ("ag1", "ffn_w_down0")],
    "sgu_gate": [("ag2", "sgu_w_out"), ("ag2", "ffn_w_down0"), ("ag1", "ffn_w_up0")],
    "sgu_out": [("ag2", "ffn_w_up0"), ("ag1", "attn_w_qkv")],
    "ffn0_fwd": [("ag2", "attn_w_qkv"), ("ag1", "attn_w_o"), ("ag1", "ffn_w_up1")],
    "qkv": [("ag2", "attn_w_o"), ("ag2", "ffn_w_up1")],
    "attn": [("ag1", "ffn_w_down1")],
    "attn_out": [("ag2", "ffn_w_down1")],
    "ffn1_bwd2": [("rs1", "ffn_w_down1")],
    "ffn1_dw_up": [("rs2", "ffn_w_down1")],
    "attn_do": [("rs1", "ffn_w_up1")],
    "attn_bwd": [("rs2", "ffn_w_up1"), ("rs1", "attn_w_o")],
    "dw_qkv": [("rs2", "attn_w_o")],
    "dx_qkv": [("rs1", "attn_w_qkv")],
    "ffn0_bwd1": [("rs2", "attn_w_qkv")] + [("ag1", n) for n in SMALL_ATTN],
    "ffn0_bwd2": [("rs1", "ffn_w_down0")] + [("ag2", n) for n in SMALL_ATTN],
    "ffn0_dw_up": [("rs2", "ffn_w_down0")],
    "sgu_dgated": [("rs1", "ffn_w_up0")] + [("ag1", n) for n in SMALL_FFN],
    "dw_sgu_out": [("ag2", n) for n in SMALL_FFN],
    "sgu_gate_bwd": [("rs2", "ffn_w_up0"), ("rs1", "sgu_w_out")],
    "dw_sgu_in": [("rs2", "sgu_w_out")] + [("ag1", n) for n in SMALL_LATE],
    "last_a": [("rs1", "sgu_w_in"), ("rs1", "ffn_conv_w")] + [("ag2", n) for n in SMALL_LATE],
    "last_b": [("rs2", "sgu_w_in"), ("rs2", "ffn_conv_w")],
}


class _Overlap:
    def __init__(self, shard, place):
        self.shard, self.place = shard, place
        self.part, self.full = {}, {}
        self.grads, self.sib, self.own, self.recv = {}, {}, {}, {}

    def w(self, n):
        return self.full[n]

    def grad(self, n, pair):
        self.grads[n] = pair

    def small(self, g_rep):
        self.shard.update(("g_" + n, a.astype(BF16) if n in BF16_TRANSIT else a) for n, a in _views2d(g_rep).items())

    def chip_sums(self, n):
        sums, self.own[n] = _rs_partial(self.grads[n][0], self.sib.pop(n), self.place, "rs_partial_" + n)
        return sums

    def hook(self, host):
        ops = PLAN.get(host)
        if not ops:
            return None
        where = {"ag1": self.part, "ag2": self.full, "rs1": self.sib, "rs2": self.recv}
        idx = []

        def hook(results=None):
            if results is not None:
                for (kind, n), i in zip(ops, idx):
                    where[kind][n] = results[i]
                return None
            comm = _Comm()
            for kind, n in ops:
                arr = {"ag1": lambda: self.shard[n], "ag2": lambda: self.part.pop(n), "rs1": lambda: self.grads[n][1],
                       "rs2": lambda: self.chip_sums(n)}[kind]()
                idx.append(comm.add(kind, arr))
            return comm

        return hook


TRANSPOSED = {"attn_w_qkv"}
PHYSICAL_T = {"ffn_w_up"}
SHARDED = {
    "sgu_w_in": ["sgu_w_in"], "sgu_w_out": ["sgu_w_out"], "attn_w_qkv": ["attn_w_qkv"], "attn_w_o": ["attn_w_o"],
    "ffn_w_up": ["ffn_w_up0", "ffn_w_up1"], "ffn_w_down": ["ffn_w_down0", "ffn_w_down1"], "ffn_conv_w": ["ffn_conv_w"],
}


def _send_views(w):
    out = {"ffn_conv_w": w["ffn_conv_w"].reshape(6, -1)}
    for name, parts in SHARDED.items():
        if name != "ffn_conv_w":
            out.update((p, (w[name][l].T if name in TRANSPOSED else w[name][l]).astype(BF16)) for l, p in enumerate(parts))
    return out


def _views2d(d):
    return {n: d[n].reshape(-1, d[n].shape[-1]) for n in REPLICATED if n in d}


def kernel(x, mix_norm, ffn_norm, sgu_w_in, sgu_v_gain, sgu_w_s, sgu_b_s, sgu_w_out, attn_w_qkv, attn_q_gain, attn_k_gain, attn_sinks, attn_w_o, rel_bias, ffn_w_up, ffn_conv_w, ffn_conv_b, ffn_w_down, loss_target, m_mix_norm, m_ffn_norm, m_sgu_w_in, m_sgu_v_gain, m_sgu_w_s, m_sgu_b_s, m_sgu_w_out, m_attn_w_qkv, m_attn_q_gain, m_attn_k_gain, m_attn_sinks, m_attn_w_o, m_rel_bias, m_ffn_w_up, m_ffn_conv_w, m_ffn_conv_b, m_ffn_w_down, v_mix_norm, v_ffn_norm, v_sgu_w_in, v_sgu_v_gain, v_sgu_w_s, v_sgu_b_s, v_sgu_w_out, v_attn_w_qkv, v_attn_q_gain, v_attn_k_gain, v_attn_sinks, v_attn_w_o, v_rel_bias, v_ffn_w_up, v_ffn_conv_w, v_ffn_conv_b, v_ffn_w_down):
    w = dict(zip(WEIGHTS, (mix_norm, ffn_norm, sgu_w_in, sgu_v_gain, sgu_w_s, sgu_b_s, sgu_w_out, attn_w_qkv, attn_q_gain, attn_k_gain,
                           attn_sinks, attn_w_o, rel_bias, ffn_w_up, ffn_conv_w, ffn_conv_b, ffn_w_down)))
    m = dict(zip(WEIGHTS, (m_mix_norm, m_ffn_norm, m_sgu_w_in, m_sgu_v_gain, m_sgu_w_s, m_sgu_b_s, m_sgu_w_out, m_attn_w_qkv, m_attn_q_gain,
                           m_attn_k_gain, m_attn_sinks, m_attn_w_o, m_rel_bias, m_ffn_w_up, m_ffn_conv_w, m_ffn_conv_b, m_ffn_w_down)))
    v = dict(zip(WEIGHTS, (v_mix_norm, v_ffn_norm, v_sgu_w_in, v_sgu_v_gain, v_sgu_w_s, v_sgu_b_s, v_sgu_w_out, v_attn_w_qkv, v_attn_q_gain,
                           v_attn_k_gain, v_attn_sinks, v_attn_w_o, v_rel_bias, v_ffn_w_up, v_ffn_conv_w, v_ffn_conv_b, v_ffn_w_down)))
    rep = {n: w[n] for n in REPLICATED}

    xi, yi, ci = lax.axis_index("x"), lax.axis_index("y"), lax.axis_index("c")
    place = jnp.stack([ci, 2 * xi + yi]).astype(jnp.int32)
    sch = _Overlap(_send_views(w), place)
    sch.full.update(zip(GATHER_FIRST, _allgather([sch.shard[n] for n in GATHER_FIRST], "gather_first")))

    loss, grad_x, g_rep = _local_step(x[0], loss_target[0], rep, sch)
    loss = lax.psum(loss, ("x", "y", "c"))
    _exchange(sch.hook("last_a"), "last_a")
    _exchange(sch.hook("last_b"), "last_b")

    out = [{}, {}, {}, {}]
    for name, parts in SHARDED.items():
        flip = (lambda a: jnp.swapaxes(a, -1, -2)) if name in TRANSPOSED | PHYSICAL_T else (lambda a: a)
        shape = flip(w[name]).shape
        as3d = lambda a: flip(a).reshape(len(parts), -1, shape[-1])
        res = _adamw_shard([sch.own[p] for p in parts], [sch.recv[p] for p in parts], as3d(w[name]), as3d(m[name]), as3d(v[name]),
                           "adamw_" + name, flipped=name in PHYSICAL_T)
        for o, r in zip(out, res):
            o[name] = flip(r.reshape(shape))
    small = _adamw_small([sch.full[n] for n in SMALL], *[list(_views2d(d).values()) for d in (rep, m, v)], "adamw_small")
    for o, res in zip(out, small):
        o.update((n, r.reshape(w[n].shape)) for n, r in zip(REPLICATED, res))

    return (loss, grad_x[None], *[out[0][n] for n in WEIGHTS], *[out[1][n] for n in WEIGHTS],
            *[out[2][n] for n in WEIGHTS], *[out[3][n] for n in WEIGHTS])
```

```python
import functools
import math

import numpy as np
import jax
import jax.numpy as jnp
from jax import lax
from jax.experimental import pallas as pl
from jax.experimental.pallas import tpu as pltpu

F32 = jnp.float32
BF16 = jnp.bfloat16
DH = jnp.bfloat16
S = jax.ShapeDtypeStruct

D = 1024
CHUNK = 128
SGU_W = 2048
SGU_G = 16
HD = 64
NH = 16
NKV = 4
KVG = 4
D_FF = 2816
REL_BUCKETS = 32
REL_MAX_DIST = 128
EPS = 1e-6
N_DEV = 8
MESH = pl.DeviceIdType.MESH

ADAM_LR = 0.001
ADAM_B1 = 0.9
ADAM_B2 = 0.999
ADAM_EPS = 1e-08
ADAM_WD = 0.01
ADAM_STEP = 10

ROW_TILE = 512
HALO = 8
FFN_ROWS = 256


def _tm(t):
    return min(ROW_TILE, t)


def _cp(*sem):
    return pltpu.CompilerParams(dimension_semantics=sem)


ANY = pl.BlockSpec(memory_space=pl.ANY)


def _place():
    x, y, c = lax.axis_index("x"), lax.axis_index("y"), lax.axis_index("c")
    return x, y, c, [(1 - x, y), (x, 1 - y), (1 - x, 1 - y)]


class _Comm:
    SEMS = {"ag1": 5, "ag2": 3, "rs1": 4, "rs2": 3, "agd": 8}

    def __init__(self):
        self.inputs, self.out_shapes, self.aliases, self.ops, self.n_sems = [], [], {}, [], 0

    def add(self, kind, arr):
        lead = {"ag1": N_DEV, "agd": N_DEV, "ag2": None, "rs1": 4, "rs2": 3}[kind]
        shape = arr.shape if lead is None else (lead,) + arr.shape[(0 if kind in ("ag1", "agd") else 1):]
        if kind == "ag2":
            self.aliases[len(self.inputs)] = len(self.out_shapes)
        self.ops.append((kind, len(self.inputs), len(self.out_shapes), self.n_sems))
        self.inputs.append(arr)
        self.out_shapes.append(S(shape, arr.dtype))
        self.n_sems += self.SEMS[kind]
        return len(self.out_shapes) - 1

    def _copies(self, ins, outs, send, recv):
        x, y, c, chips = _place()
        me, sibling = (x, y, c), (x, y, 1 - c)
        slot = lambda px, py, pc: 4 * px + 2 * py + pc
        sends, recvs, local = [], [], []

        def rc(src, dst, k, to):
            return lambda: pltpu.make_async_remote_copy(src_ref=src(), dst_ref=dst(), send_sem=send.at[k], recv_sem=recv.at[k],
                                                        device_id=to, device_id_type=MESH)

        for kind, ii, oi, b in self.ops:
            src, dst = ins[ii], outs[oi]
            at = lambda ref, i: (lambda: ref.at[i])
            if kind == "ag1":
                whole, mine = (lambda s=src: s), at(dst, slot(*me))
                sends.append(rc(whole, mine, b, sibling))
                recvs.append(rc(whole, at(dst, slot(x, y, 1 - c)), b, me))
                for j, chip in enumerate(chips):
                    sends.append(rc(whole, mine, b + 1 + j, (*chip, c)))
                    recvs.append(rc(whole, at(dst, slot(*chip, c)), b + 1 + j, me))
                local.append(lambda s=src, m=mine, k=b + 4: pltpu.make_async_copy(s, m(), send.at[k]))
            elif kind == "ag2":
                for j, chip in enumerate(chips):
                    sends.append(rc(at(dst, slot(*chip, c)), at(dst, slot(*chip, c)), b + j, sibling))
                    recvs.append(rc(at(dst, slot(*chip, 1 - c)), at(dst, slot(*chip, 1 - c)), b + j, me))
            elif kind == "agd":
                whole, mine = (lambda s=src: s), at(dst, slot(*me))
                flip = lambda v, bit: 1 - v if bit else v
                for k in range(1, N_DEV):
                    peer = (flip(x, k >> 2), flip(y, (k >> 1) & 1), flip(c, k & 1))
                    sends.append(rc(whole, mine, b + k - 1, peer))
                    recvs.append(rc(whole, at(dst, slot(*peer)), b + k - 1, me))
                local.append(lambda s=src, m=mine, k=b + 7: pltpu.make_async_copy(s, m(), send.at[k]))
            elif kind == "rs1":
                for k in range(4):
                    sends.append(rc(at(src, 2 * k + (1 - c)), at(dst, k), b + k, sibling))
                    recvs.append(rc(at(src, 2 * k + c), at(dst, k), b + k, me))
            else:
                for j, (px, py) in enumerate(chips):
                    sends.append(rc(at(src, 2 * px + py), at(dst, j), b + j, (px, py, c)))
                    recvs.append(rc(at(src, 2 * px + py), at(dst, j), b + j, me))
        return sends, recvs, local

    def start(self, ins, outs, send, recv):
        sends, _, local = self._copies(ins, outs, send, recv)
        for make in local + sends:
            make().start()

    def finish(self, ins, outs, send, recv):
        sends, recvs, local = self._copies(ins, outs, send, recv)
        for make in recvs:
            make().wait_recv()
        for make in sends:
            make().wait_send()
        for make in local:
            make().wait()


def _run(body, args, hook, *, grid, in_specs, out_specs, out_shape, name, semantics, scratch_shapes=(), aliases=None):
    comm = hook() if hook is not None else None
    aliases = dict(aliases or {})
    if comm is None:
        return pl.pallas_call(body, grid=grid, in_specs=in_specs, out_specs=out_specs, out_shape=out_shape, name=name,
                              scratch_shapes=list(scratch_shapes), input_output_aliases=aliases,
                              compiler_params=_cp(*semantics))(*args)
    single = not isinstance(out_shape, (list, tuple))
    out_shapes = [out_shape] if single else list(out_shape)
    out_specs_l = [out_specs] if single else list(out_specs)
    n_in, n_out, n_scr, ci, co = len(args), len(out_shapes), len(scratch_shapes), len(comm.inputs), len(comm.out_shapes)

    def wrapped(*refs):
        ins, cins = refs[:n_in], refs[n_in:n_in + ci]
        outs, couts = refs[n_in + ci:n_in + ci + n_out], refs[n_in + ci + n_out:n_in + ci + n_out + co]
        scr = refs[n_in + ci + n_out + co:n_in + ci + n_out + co + n_scr]
        send, recv = refs[-2:]
        first = functools.reduce(lambda a, b: a & b, [pl.program_id(a) == 0 for a in range(len(grid))])
        last = functools.reduce(lambda a, b: a & b, [pl.program_id(a) == g - 1 for a, g in enumerate(grid)])

        @pl.when(first)
        def _():
            comm.start(cins, couts, send, recv)

        body(*ins, *outs, *scr)

        @pl.when(last)
        def _():
            comm.finish(cins, couts, send, recv)

    res = pl.pallas_call(
        wrapped, grid=grid, in_specs=list(in_specs) + [ANY] * ci, out_specs=out_specs_l + [ANY] * co,
        out_shape=out_shapes + comm.out_shapes, name=name,
        scratch_shapes=list(scratch_shapes) + [pltpu.SemaphoreType.DMA((comm.n_sems,)), pltpu.SemaphoreType.DMA((comm.n_sems,))],
        input_output_aliases={**aliases, **{n_in + k: n_out + v for k, v in comm.aliases.items()}},
        compiler_params=pltpu.CompilerParams(dimension_semantics=("arbitrary",) * len(grid), has_side_effects=True))(*args, *comm.inputs)
    hook(res[n_out:])
    return res[0] if single else list(res[:n_out])


def _dot(a, b):
    return jnp.dot(a, b, preferred_element_type=F32)


def _dot_nt(a, b):
    return lax.dot_general(a, b, (((1,), (1,)), ((), ())), preferred_element_type=F32)


def _dot_tn(a, b):
    return lax.dot_general(a, b, (((0,), (0,)), ((), ())), preferred_element_type=F32)


def _gelu(x):
    return 0.5 * x * (1.0 + lax.erf(x * (2.0 ** -0.5)))


def _gelu_and_grad(x):
    cdf = 0.5 * (1.0 + lax.erf(x * (2.0 ** -0.5)))
    return x * cdf, cdf + x * jnp.exp(-0.5 * x * x) * (1.0 / math.sqrt(2.0 * math.pi))


def _sigmoid(x):
    return 1.0 / (1.0 + jnp.exp(-x))


def _rstd(x):
    return lax.rsqrt(jnp.mean(x * x, axis=-1, keepdims=True) + EPS)


def _rel_tables():
    q = np.arange(CHUNK)[:, None] + CHUNK
    k = np.arange(2 * CHUNK)[None, :]
    dist = q - k
    n = np.maximum(dist, 0)
    max_exact = REL_BUCKETS // 2
    large = max_exact + (np.log(np.maximum(n, 1).astype(np.float32) / max_exact)
                         / math.log(REL_MAX_DIST / max_exact) * (REL_BUCKETS - max_exact)).astype(np.int32)
    large = np.minimum(large, REL_BUCKETS - 1)
    return np.where(n < max_exact, n, large).astype(np.int32)


def _rmsnorm(x, gain, name):
    t = x.shape[0]
    tm = _tm(t)

    def body(x_ref, g_ref, o_ref):
        xv = x_ref[...]
        o_ref[...] = (xv * _rstd(xv) * g_ref[...]).astype(BF16)

    return pl.pallas_call(
        body, grid=(t // tm,), name=name,
        in_specs=[pl.BlockSpec((tm, D), lambda i: (i, 0)), pl.BlockSpec((1, D), lambda i: (0, 0))],
        out_specs=pl.BlockSpec((tm, D), lambda i: (i, 0)),
        out_shape=S((t, D), BF16), compiler_params=_cp("parallel"))(x, gain)


def _resident(shape):
    zeros = (0,) * len(shape)
    return pl.BlockSpec(shape, lambda *_: zeros, pipeline_mode=pl.Buffered(1))


def _mm_slot(hn, wg, out_dtype, name, hook=None):
    t, k = hn.shape
    ns, _, n = wg.shape
    tm = _tm(t)

    def body(a_ref, w_ref, o_ref):
        a = a_ref[...]
        for s in range(ns):
            o_ref[s] = _dot(a, w_ref[s]).astype(out_dtype)

    return _run(
        body, [hn, wg], hook, grid=(t // tm,), name=name, semantics=("parallel",),
        in_specs=[pl.BlockSpec((tm, k), lambda i: (i, 0)), _resident(wg.shape)],
        out_specs=pl.BlockSpec((ns, tm, n), lambda i: (0, i, 0)), out_shape=S((ns, t, n), out_dtype))


def _mm_t(hn, wt, name, hook=None):
    t, k = hn.shape
    ns, n, _ = wt.shape
    tm = _tm(t)

    def body(a_ref, w_ref, o_ref):
        a = a_ref[...]
        for s in range(ns):
            o_ref[s * n:(s + 1) * n, :] = _dot_nt(w_ref[s], a)

    return _run(
        body, [hn, wt], hook, grid=(t // tm,), name=name, semantics=("parallel",),
        in_specs=[pl.BlockSpec((tm, k), lambda i: (i, 0)), _resident(wt.shape)],
        out_specs=pl.BlockSpec((ns * n, tm), lambda i: (0, i)), out_shape=S((ns * n, t), F32))


def _conv3(a, prev, cw, cb, tm):
    ext = jnp.concatenate([prev, a], axis=0)
    return cw[2:3] * a + cw[1:2] * ext[HALO - 1:HALO - 1 + tm] + cw[0:1] * ext[HALO - 2:HALO - 2 + tm] + cb


def _ffn_fwd(hn, h, wup, wdown, cw, cb, extra, mode, name, hook=None):
    t, k = hn.shape
    n = wup.shape[-1]
    nh = wup.shape[0] // 2
    tm = min(FFN_ROWS, t)
    ni = t // tm

    def body(a_ref, h_ref, wu_ref, wd_ref, cw_ref, cb_ref, e_ref, as_ref, cs_ref, o1_ref, o2_ref, carry):
        i = pl.program_id(0)

        @pl.when(i == 0)
        def _():
            carry[...] = jnp.zeros_like(carry)

        a = a_ref[...]
        acc = h_ref[...]
        nxt = (_dot(a, wu_ref[0]), _dot(a, wu_ref[nh]))
        for j in range(nh):
            ag, av = nxt
            if j + 1 < nh:
                nxt = (_dot(a, wu_ref[j + 1]), _dot(a, wu_ref[nh + j + 1]))
            as_ref[j] = ag.astype(BF16)
            as_ref[nh + j] = av.astype(BF16)
            cg = _conv3(ag, carry[j], cw_ref[j], cb_ref[j], tm)
            cv = _conv3(av, carry[nh + j], cw_ref[nh + j], cb_ref[nh + j], tm)
            carry[j] = ag[tm - HALO:]
            carry[nh + j] = av[tm - HALO:]
            cs_ref[j] = cg.astype(BF16)
            cs_ref[nh + j] = cv.astype(BF16)
            act = (cg * _sigmoid(cg) * cv).astype(BF16)
            acc = acc + _dot(act, wd_ref[j * n:(j + 1) * n, :])
        if mode == "norm":
            o1_ref[...] = acc
            o2_ref[...] = (acc * _rstd(acc) * e_ref[...]).astype(BF16)
        else:
            err = acc - e_ref[...]
            o1_ref[...] = (err * (1.0 / D)).astype(o1_ref.dtype)
            o2_ref[...] = jnp.full(o2_ref.shape, jnp.sum(err * err), F32)

    row = pl.BlockSpec((tm, D), lambda i: (i, 0))
    if mode == "norm":
        e_spec, o2_spec, o2_shape = pl.BlockSpec((1, D), lambda i: (0, 0)), row, S((t, D), BF16)
    else:
        e_spec, o2_spec, o2_shape = row, pl.BlockSpec((None, 8, 128), lambda i: (i, 0, 0)), S((ni, 8, 128), F32)
    aspec = pl.BlockSpec((2 * nh, tm, n), lambda i: (0, i, 0))
    return _run(
        body, [hn, h, wup, wdown, cw, cb, extra], hook, grid=(ni,), name=name, semantics=("arbitrary",),
        in_specs=[pl.BlockSpec((tm, k), lambda i: (i, 0)), row, _resident(wup.shape), _resident(wdown.shape),
                  _resident(cw.shape), _resident(cb.shape), e_spec],
        out_specs=[aspec, aspec, row, o2_spec],
        out_shape=[S((2 * nh, t, n), BF16), S((2 * nh, t, n), BF16), S((t, D), F32 if mode == "norm" else DH), o2_shape],
        scratch_shapes=[pltpu.VMEM((2 * nh, HALO, n), F32)])


def _tril_mask():
    r = lax.broadcasted_iota(jnp.int32, (CHUNK, CHUNK), 0)
    c = lax.broadcasted_iota(jnp.int32, (CHUNK, CHUNK), 1)
    return r >= c


def _sgu_gate_fwd(a_s, vgain, ws, bst, name, hook=None):
    t = a_s.shape[1]
    sw = a_s.shape[2]
    gps = sw // CHUNK

    def body(a_ref, vg_ref, ws_ref, b_ref, o_ref):
        v = _gelu(jnp.concatenate([a_ref[4 + s].astype(F32) for s in range(4)], axis=1))
        vn = (v * _rstd(v) * vg_ref[...]).astype(BF16)
        tri = _tril_mask()
        for g in range(SGU_G):
            w = jnp.where(tri, ws_ref[g], 0.0).astype(BF16)
            sg = _dot(w, vn[:, g * CHUNK:(g + 1) * CHUNK]) + b_ref[:, g:g + 1]
            lo = (g % gps) * CHUNK
            u = _gelu(a_ref[g // gps, :, lo:lo + CHUNK].astype(F32))
            o_ref[g // gps, :, lo:lo + CHUNK] = (u * sg).astype(BF16)

    return _run(
        body, [a_s, vgain, ws, bst], hook, grid=(t // CHUNK,), name=name, semantics=("parallel",),
        in_specs=[pl.BlockSpec((8, CHUNK, sw), lambda n: (0, n, 0)), pl.BlockSpec((1, SGU_W), lambda n: (0, 0)),
                  pl.BlockSpec((SGU_G, CHUNK, CHUNK), lambda n: (0, 0, 0)), pl.BlockSpec((CHUNK, SGU_G), lambda n: (0, 0))],
        out_specs=pl.BlockSpec((4, CHUNK, sw), lambda n: (0, n, 0)), out_shape=S((4, t, sw), BF16))


def _resid_mm(a_s, w, resid, extra, mode, name, hook=None, fm=False):
    nk, t, kc = (1, a_s.shape[1], a_s.shape[0]) if fm else a_s.shape
    tm = _tm(t)
    ni = t // tm

    def body(a_ref, w_ref, r_ref, e_ref, o1_ref, o2_ref):
        h = r_ref[...]
        if fm:
            h = h + _dot_tn(a_ref[...], w_ref[...])
        for j in range(0 if fm else nk):
            h = h + _dot(a_ref[j], w_ref[j * kc:(j + 1) * kc, :])
        if mode == "norm":
            o1_ref[...] = h
            o2_ref[...] = (h * _rstd(h) * e_ref[...]).astype(BF16)
        else:
            err = h - e_ref[...]
            o1_ref[...] = (err * (1.0 / D)).astype(o1_ref.dtype)
            o2_ref[...] = jnp.full(o2_ref.shape, jnp.sum(err * err), F32)

    row = pl.BlockSpec((tm, D), lambda i: (i, 0))
    if mode == "norm":
        e_spec, o2_spec, o2_shape = pl.BlockSpec((1, D), lambda i: (0, 0)), row, S((t, D), BF16)
    else:
        e_spec, o2_spec, o2_shape = row, pl.BlockSpec((None, 8, 128), lambda i: (i, 0, 0)), S((ni, 8, 128), F32)
    return _run(
        body, [a_s, w, resid, extra], hook, grid=(ni,), name=name, semantics=("parallel",),
        in_specs=[pl.BlockSpec((kc, tm), lambda i: (0, i)) if fm else pl.BlockSpec((nk, tm, kc), lambda i: (0, i, 0)),
                  _resident(w.shape), row, e_spec],
        out_specs=[row, o2_spec], out_shape=[S((t, D), F32 if mode == "norm" else DH), o2_shape])


def _relbias_fwd(rel_bias_t, bucket_row, name):
    nb = bucket_row.shape[1]

    def body(rb_ref, bk_ref, o_ref):
        onehot = (lax.broadcasted_iota(jnp.int32, (REL_BUCKETS, nb), 0) == bk_ref[...]).astype(F32)
        o_ref[...] = jnp.dot(rb_ref[...], onehot, precision=lax.Precision.HIGHEST, preferred_element_type=F32)

    return pl.pallas_call(body, out_shape=S((NH, nb), F32), name=name)(rel_bias_t, bucket_row)


def _relbias_bwd(dbias, bucket_row, name):
    nb = bucket_row.shape[1]

    def body(db_ref, bk_ref, o_ref):
        onehot = (lax.broadcasted_iota(jnp.int32, (REL_BUCKETS, nb), 0) == bk_ref[...]).astype(F32)
        o_ref[...] = lax.dot_general(db_ref[...], onehot, (((1,), (1,)), ((), ())),
                                     precision=lax.Precision.HIGHEST, preferred_element_type=F32)

    return pl.pallas_call(body, out_shape=S((NH, REL_BUCKETS), F32), name=name)(dbias, bucket_row)


QKV = D + 2 * NKV * HD
KV0 = D


def _rstd_rows(x):
    return lax.rsqrt(jnp.mean(x * x, axis=0, keepdims=True) + EPS)


def _attn_valid(n):
    kj = lax.broadcasted_iota(jnp.int32, (2 * CHUNK, CHUNK), 0)
    qi = lax.broadcasted_iota(jnp.int32, (2 * CHUNK, CHUNK), 1)
    dist = qi + CHUNK - kj
    return (dist >= 0) & (dist < CHUNK) & ((n > 0) | (kj >= CHUNK))


def _attn_band(cur_ref, prev_ref, row):
    return jnp.concatenate([prev_ref[row - KV0:row - KV0 + HD, :], cur_ref[row:row + HD, :]], axis=1)


def _attn_probs(kn_tok, qn, bias, valid, sink):
    s = _dot(kn_tok, qn) * (HD ** -0.5) + bias
    s = jnp.where(valid, s, -jnp.inf)
    m = jnp.maximum(jnp.max(s, axis=0, keepdims=True), sink)
    p = jnp.exp(s - m)
    psink = jnp.exp(sink - m)
    inv = 1.0 / (jnp.sum(p, axis=0, keepdims=True) + psink)
    return p * inv, psink * inv


def _attn_fwd(qkv_t, qg, kg, sinks, bias, name, hook=None):
    t = qkv_t.shape[1]

    def body(cur_ref, prev_ref, qg_ref, kg_ref, sink_ref, bias_ref, o_ref):
        n = pl.program_id(0)
        valid = _attn_valid(n)
        ks = [_attn_band(cur_ref, prev_ref, KV0 + HD * h) for h in range(NKV)]
        kn_toks = [(k * _rstd_rows(k) * kg_ref[...]).astype(BF16).T for k in ks]
        vbs = [_attn_band(cur_ref, prev_ref, KV0 + HD * (NKV + h)).astype(BF16) for h in range(NKV)]
        qs = [cur_ref[HD * hq:HD * (hq + 1), :] for hq in range(NH)]
        qns = [(q * _rstd_rows(q) * qg_ref[...]).astype(BF16) for q in qs]
        ps = [_attn_probs(kn_toks[hq // KVG], qns[hq], bias_ref[hq], valid, sink_ref[hq])[0] for hq in range(NH)]
        for hq in range(NH):
            o_ref[HD * hq:HD * (hq + 1), :] = _dot(vbs[hq // KVG], ps[hq].astype(BF16)).astype(BF16)

    col = pl.BlockSpec((HD, 1), lambda n: (0, 0))
    return _run(
        body, [qkv_t, qkv_t, qg, kg, sinks, bias], hook, grid=(t // CHUNK,), name=name, semantics=("parallel",),
        in_specs=[pl.BlockSpec((QKV, CHUNK), lambda n: (0, n)),
                  pl.BlockSpec((QKV - KV0, CHUNK), lambda n: (KV0 // (QKV - KV0), jnp.maximum(n - 1, 0))),
                  col, col, pl.BlockSpec(memory_space=pltpu.SMEM), pl.BlockSpec((NH, 2 * CHUNK, CHUNK), lambda n: (0, 0, 0))],
        out_specs=pl.BlockSpec((D, CHUNK), lambda n: (0, n)), out_shape=S((D, t), BF16))


def _dx_rows(dh, w, kc, out_dtype, name, hook=None):
    t = dh.shape[0]
    nk = w.shape[0] // kc
    tm = _tm(t)

    def body(d_ref, w_ref, o_ref):
        dhb = d_ref[...].astype(BF16)
        for j in range(nk):
            o_ref[j] = _dot_nt(dhb, w_ref[j * kc:(j + 1) * kc, :]).astype(out_dtype)

    return _run(
        body, [dh, w], hook, grid=(t // tm,), name=name, semantics=("parallel",),
        in_specs=[pl.BlockSpec((tm, D), lambda i: (i, 0)), _resident(w.shape)],
        out_specs=pl.BlockSpec((nk, tm, kc), lambda i: (0, i, 0)), out_shape=S((nk, t, kc), out_dtype))


def _dx_rows_t(dh, w, name, hook=None):
    t = dh.shape[0]
    k = w.shape[0]
    tm = _tm(t)

    def body(d_ref, w_ref, o_ref):
        o_ref[...] = _dot_nt(w_ref[...], d_ref[...].astype(BF16)).astype(BF16)

    return _run(
        body, [dh, w], hook, grid=(t // tm,), name=name, semantics=("parallel",),
        in_specs=[pl.BlockSpec((tm, D), lambda i: (i, 0)), _resident(w.shape)],
        out_specs=pl.BlockSpec((k, tm), lambda i: (0, i)), out_shape=S((k, t), BF16))


def _ffn_bwd1(dh, c, wdown, name, hook=None):
    ns, t, n = c.shape
    nh = ns // 2
    tm = min(FFN_ROWS, t)
    ni = t // tm

    def body(d_ref, c_ref, wd_ref, dc_ref, dw_hbm, dwb_hbm, acc, stage):
        i = pl.program_id(0)

        @pl.when(i == 0)
        def _():
            acc[...] = jnp.zeros_like(acc)

        dhb = d_ref[...].astype(BF16)
        for j in range(nh):
            dact = _dot_nt(dhb, wd_ref[j * n:(j + 1) * n, :])
            cg = c_ref[j].astype(F32)
            cv = c_ref[nh + j].astype(F32)
            sg = _sigmoid(cg)
            gs = cg * sg
            acc[j * n:(j + 1) * n, :] += _dot_tn((gs * cv).astype(BF16), dhb)
            dc_ref[j] = (dact * cv * (sg + gs * (1.0 - sg))).astype(BF16)
            dc_ref[nh + j] = (dact * gs).astype(BF16)

        @pl.when(i == ni - 1)
        def _():
            pltpu.sync_copy(acc, dw_hbm)
            for j in range(nh):
                stage[...] = acc[j * n:(j + 1) * n, :].astype(BF16)
                pltpu.sync_copy(stage, dwb_hbm.at[pl.ds(j * n, n), :])

    slab = pl.BlockSpec((ns, tm, n), lambda i: (0, i, 0))
    return _run(
        body, [dh, c, wdown], hook, grid=(ni,), name=name, semantics=("arbitrary",),
        in_specs=[pl.BlockSpec((tm, D), lambda i: (i, 0)), slab, _resident(wdown.shape)],
        out_specs=[slab, ANY, ANY], out_shape=[S((ns, t, n), BF16), S(wdown.shape, F32), S(wdown.shape, BF16)],
        scratch_shapes=[pltpu.VMEM(wdown.shape, F32), pltpu.VMEM((n, D), BF16)])


def _ffn_bwd2(dc, a, wup, cw, h, gain, dh_in, name, hook=None):
    ns, t, n = dc.shape
    tm = min(FFN_ROWS, t)
    ni = t // tm

    def body(dc_ref, a_ref, wu_ref, cw_ref, h_ref, g_ref, di_ref, da_ref, o_ref, dg_ref, dcw_ref, dcb_ref, carry, keep):
        i = pl.program_id(0)

        @pl.when(i == 0)
        def _():
            carry[...] = jnp.zeros_like(carry)
            dg_ref[...] = jnp.zeros_like(dg_ref)
            dcw_ref[...] = jnp.zeros_like(dcw_ref)
            dcb_ref[...] = jnp.zeros_like(dcb_ref)

        rsum = lambda v: jnp.sum(v, axis=0, keepdims=True)
        acc = jnp.zeros((tm, D), F32)
        for s in range(ns):
            x = dc_ref[s].astype(F32)
            ext = jnp.concatenate([x, carry[s]], axis=0)
            keep[0] = ext[1:1 + tm]
            keep[1] = ext[2:2 + tm]
            x1, x2 = keep[0], keep[1]
            cwv = cw_ref[s]
            da = (cwv[2:3] * x + cwv[1:2] * x1 + cwv[0:1] * x2).astype(BF16)
            carry[s] = x[:HALO]
            da_ref[s] = da
            acc = acc + _dot_nt(da, wu_ref[s])
            av = a_ref[s].astype(F32)
            dcw_ref[s] += jnp.concatenate([rsum(x2 * av), rsum(x1 * av), rsum(x * av)], axis=0)
            dcb_ref[s] += rsum(x)
        hv = h_ref[...]
        r = _rstd(hv)
        gg = acc * g_ref[...]
        dh_new = di_ref[...].astype(F32) + r * gg - hv * (r * r * r * jnp.mean(gg * hv, axis=-1, keepdims=True))
        o_ref[...] = dh_new.astype(o_ref.dtype)
        dg_ref[...] += jnp.sum(acc * hv * r, axis=0, keepdims=True)

    slab = pl.BlockSpec((ns, tm, n), lambda i: (0, ni - 1 - i, 0))
    row = pl.BlockSpec((tm, D), lambda i: (ni - 1 - i, 0))
    vec = pl.BlockSpec((1, D), lambda i: (0, 0))
    whole = lambda shape: pl.BlockSpec(shape, lambda i: (0,) * len(shape))
    return _run(
        body, [dc, a, wup, cw, h, gain, dh_in], hook, grid=(ni,), name=name, semantics=("arbitrary",),
        in_specs=[slab, slab, _resident(wup.shape), _resident(cw.shape), row, vec, row],
        out_specs=[slab, row, vec, whole((ns, 3, n)), whole((ns, 1, n))],
        out_shape=[S((ns, t, n), BF16), S((t, D), DH), S((1, D), F32), S((ns, 3, n), F32), S((ns, 1, n), F32)],
        scratch_shapes=[pltpu.VMEM((ns, HALO, n), F32), pltpu.VMEM((2, tm, n), F32)])


def _dw_slot(hn, dy_s, name, hook=None):
    t, k = hn.shape
    ns, _, n = dy_s.shape
    tm = _tm(t)

    def body(a_ref, b_ref, o_ref, ob_ref, at_ref):
        @pl.when(pl.program_id(0) == 0)
        def _():
            for i in range(t // tm):
                at_ref[:, i * tm:(i + 1) * tm] = a_ref[i * tm:(i + 1) * tm, :].T

        acc = _dot(at_ref[...], b_ref[...])
        o_ref[...] = acc
        ob_ref[...] = acc.astype(BF16)

    ospec = pl.BlockSpec((None, k, n), lambda j: (j, 0, 0))
    return _run(
        body, [hn, dy_s], hook, grid=(ns,), name=name, semantics=("arbitrary",),
        in_specs=[_resident(hn.shape), pl.BlockSpec((None, t, n), lambda j: (j, 0, 0))],
        out_specs=[ospec, ospec], out_shape=[S((ns, k, n), F32), S((ns, k, n), BF16)],
        scratch_shapes=[pltpu.VMEM((k, t), BF16)])


def _dw_rows(a_s, dh, name, hook=None, fm=False):
    nk, t, kc = (1, a_s.shape[1], a_s.shape[0]) if fm else a_s.shape
    tm = _tm(t)
    ni = t // tm

    def body(a_ref, d_ref, o_ref, ob_ref):
        i = pl.program_id(0)
        dhb = d_ref[...].astype(BF16)

        @pl.when(i == 0)
        def _():
            o_ref[...] = jnp.zeros_like(o_ref)

        if fm:
            o_ref[...] += _dot(a_ref[...], dhb)
        for j in range(0 if fm else nk):
            o_ref[j * kc:(j + 1) * kc, :] += _dot_tn(a_ref[j], dhb)

        @pl.when(i == ni - 1)
        def _():
            ob_ref[...] = o_ref[...].astype(BF16)

    ospec = pl.BlockSpec((nk * kc, D), lambda i: (0, 0))
    return _run(
        body, [a_s, dh], hook, grid=(ni,), name=name, semantics=("arbitrary",),
        in_specs=[pl.BlockSpec((kc, tm), lambda i: (0, i)) if fm else pl.BlockSpec((nk, tm, kc), lambda i: (0, i, 0)),
                  pl.BlockSpec((tm, D), lambda i: (i, 0))],
        out_specs=[ospec, ospec], out_shape=[S((nk * kc, D), F32), S((nk * kc, D), BF16)])


def _dx_slot_normbwd(dy_s, wg, h, gain, dh_in, name, hook=None, fm=False, out_dtype=F32):
    ns, t, n = (1, dy_s.shape[1], dy_s.shape[0]) if fm else dy_s.shape
    tm = _tm(t)

    def body(dy_ref, w_ref, h_ref, g_ref, di_ref, o_ref, dg_ref):
        i = pl.program_id(0)

        @pl.when(i == 0)
        def _():
            dg_ref[...] = jnp.zeros_like(dg_ref)

        g = _dot_tn(dy_ref[...], w_ref[...]) if fm else _dot_nt(dy_ref[0], w_ref[0])
        for s in range(1, ns):
            g = g + _dot_nt(dy_ref[s], w_ref[s])
        hv = h_ref[...]
        r = _rstd(hv)
        gg = g * g_ref[...]
        dh_new = di_ref[...].astype(F32) + r * gg - hv * (r * r * r * jnp.mean(gg * hv, axis=-1, keepdims=True))
        o_ref[...] = dh_new.astype(o_ref.dtype)
        dg_ref[...] += jnp.sum(g * hv * r, axis=0, keepdims=True)

    row = pl.BlockSpec((tm, D), lambda i: (i, 0))
    vec = pl.BlockSpec((1, D), lambda i: (0, 0))
    return _run(
        body, [dy_s, wg, h, gain, dh_in], hook, grid=(t // tm,), name=name, semantics=("arbitrary",),
        in_specs=[pl.BlockSpec((n, tm), lambda i: (0, i)) if fm else pl.BlockSpec((ns, tm, n), lambda i: (0, i, 0)),
                  _resident(wg.shape), row, vec, row],
        out_specs=[row, vec], out_shape=[S((t, D), out_dtype), S((1, D), F32)])


def _sgu_gate_bwd(a_s, dg_s, vgain, ws, bst, name, hook=None):
    t = a_s.shape[1]
    sw = a_s.shape[2]
    gps = sw // CHUNK

    def body(a_ref, dg_ref, vg_ref, ws_ref, b_ref, da_ref, dws_ref, dbt_ref, dvg_ref, dvn_ref):
        n = pl.program_id(0)

        @pl.when(n == 0)
        def _():
            dws_ref[...] = jnp.zeros_like(dws_ref)
            dbt_ref[...] = jnp.zeros_like(dbt_ref)
            dvg_ref[...] = jnp.zeros_like(dvg_ref)

        vpre = jnp.concatenate([a_ref[4 + s].astype(F32) for s in range(4)], axis=1)
        v, v_grad = _gelu_and_grad(vpre)
        r = _rstd(v)
        vhat = v * r
        vn = (vhat * vg_ref[...]).astype(BF16)
        tri = _tril_mask()
        lane = lax.broadcasted_iota(jnp.int32, (CHUNK, CHUNK), 1)
        dbt = jnp.zeros((CHUNK, CHUNK), F32)
        for g in range(SGU_G):
            w = jnp.where(tri, ws_ref[g], 0.0).astype(BF16)
            vng = vn[:, g * CHUNK:(g + 1) * CHUNK]
            sg = _dot(w, vng) + b_ref[:, g:g + 1]
            lo = (g % gps) * CHUNK
            u, u_grad = _gelu_and_grad(a_ref[g // gps, :, lo:lo + CHUNK].astype(F32))
            dgate = dg_ref[g // gps, :, lo:lo + CHUNK].astype(F32)
            da_ref[g // gps, :, lo:lo + CHUNK] = (dgate * sg * u_grad).astype(BF16)
            ds = dgate * u
            dsb = ds.astype(BF16)
            dvn_ref[:, g * CHUNK:(g + 1) * CHUNK] = _dot_tn(w, dsb)
            dws_ref[g] += jnp.where(tri, _dot_nt(dsb, vng), 0.0)
            dbt = dbt + jnp.where(lane == g, jnp.sum(ds, axis=-1, keepdims=True), 0.0)
        dbt_ref[...] += dbt
        dvn = dvn_ref[...]
        dvg_ref[...] += jnp.sum(dvn * vhat, axis=0, keepdims=True)
        gg = dvn * vg_ref[...]
        dv = r * gg - v * (r * r * r * jnp.mean(gg * v, axis=-1, keepdims=True))
        dav = (dv * v_grad).astype(BF16)
        for s in range(4):
            da_ref[4 + s] = dav[:, s * sw:(s + 1) * sw]

    return _run(
        body, [a_s, dg_s, vgain, ws, bst], hook, grid=(t // CHUNK,), name=name, semantics=("arbitrary",),
        in_specs=[pl.BlockSpec((8, CHUNK, sw), lambda n: (0, n, 0)), pl.BlockSpec((4, CHUNK, sw), lambda n: (0, n, 0)),
                  pl.BlockSpec((1, SGU_W), lambda n: (0, 0)), pl.BlockSpec((SGU_G, CHUNK, CHUNK), lambda n: (0, 0, 0)),
                  pl.BlockSpec((CHUNK, SGU_G), lambda n: (0, 0))],
        out_specs=[pl.BlockSpec((8, CHUNK, sw), lambda n: (0, n, 0)), pl.BlockSpec((SGU_G, CHUNK, CHUNK), lambda n: (0, 0, 0)),
                   pl.BlockSpec((CHUNK, CHUNK), lambda n: (0, 0)), pl.BlockSpec((1, SGU_W), lambda n: (0, 0))],
        out_shape=[S((8, t, sw), BF16), S((SGU_G, CHUNK, CHUNK), F32), S((CHUNK, CHUNK), F32), S((1, SGU_W), F32)],
        scratch_shapes=[pltpu.VMEM((CHUNK, SGU_W), F32)])


def _attn_bwd(qkv_t, do_t, qg, kg, sinks, bias, name, hook=None):
    t = qkv_t.shape[1]
    nb = t // CHUNK

    def body(cur_ref, prev_ref, do_ref, qg_ref, kg_ref, sink_ref, bias_ref,
             o_ref, dqg_out, dkg_out, dsk_out, dbias_ref, carry, dqg_ref, dkg_ref, dsk_ref):
        n = pl.program_id(0)

        @pl.when(n == 0)
        def _():
            carry[...] = jnp.zeros_like(carry)
            dqg_ref[...] = jnp.zeros_like(dqg_ref)
            dkg_ref[...] = jnp.zeros_like(dkg_ref)
            dsk_ref[...] = jnp.zeros_like(dsk_ref)
            dbias_ref[...] = jnp.zeros_like(dbias_ref)

        @pl.when(n < nb)
        def _():
            valid = _attn_valid(n)
            o_ref[0:KV0, :] = carry[0:KV0, :].astype(BF16)
            kvs, heads = range(NKV), range(NH)
            group = lambda h: range(KVG * h, KVG * (h + 1))
            ks = [_attn_band(cur_ref, prev_ref, KV0 + HD * h) for h in kvs]
            rks = [_rstd_rows(k) for k in ks]
            khats = [k * rk for k, rk in zip(ks, rks)]
            kns = [(khat * kg_ref[...]).astype(BF16) for khat in khats]
            kn_toks = [kn.T for kn in kns]
            vbs = [_attn_band(cur_ref, prev_ref, KV0 + HD * (NKV + h)).astype(BF16) for h in kvs]
            v_toks = [vb.T for vb in vbs]
            qs = [cur_ref[HD * hq:HD * (hq + 1), :] for hq in heads]
            rqs = [_rstd_rows(q) for q in qs]
            qhats = [q * rq for q, rq in zip(qs, rqs)]
            qns = [(qhat * qg_ref[...]).astype(BF16) for qhat in qhats]
            probs = [_attn_probs(kn_toks[hq // KVG], qns[hq], bias_ref[hq], valid, sink_ref[hq]) for hq in heads]
            dohs = [do_ref[HD * hq:HD * (hq + 1), :] for hq in heads]
            dps = [_dot(v_toks[hq // KVG], dohs[hq]) for hq in heads]
            dsums = [jnp.sum(p * dp, axis=0, keepdims=True) for (p, _), dp in zip(probs, dps)]
            dss = [p * (dp - dsum) for (p, _), dp, dsum in zip(probs, dps, dsums)]
            for hq in heads:
                dsk_ref[hq:hq + 1, :] -= probs[hq][1] * dsums[hq]
                dbias_ref[hq] += dss[hq]
            dvs = [sum(_dot_nt(dohs[hq], probs[hq][0].astype(BF16)) for hq in group(h)) for h in kvs]
            dscs = [(ds * (HD ** -0.5)).astype(BF16) for ds in dss]
            dqns = [_dot(kns[hq // KVG], dscs[hq]) for hq in heads]
            dkns = [sum(_dot_nt(qns[hq], dscs[hq]) for hq in group(h)) for h in kvs]
            dqg_ref[...] += sum(dqn * qhat for dqn, qhat in zip(dqns, qhats))
            for hq in heads:
                gq = dqns[hq] * qg_ref[...]
                carry[HD * hq:HD * (hq + 1), :] = rqs[hq] * gq - qs[hq] * (
                    rqs[hq] * rqs[hq] * rqs[hq] * jnp.mean(gq * qs[hq], axis=0, keepdims=True))
            dkg_ref[...] += sum(dkn * khat for dkn, khat in zip(dkns, khats))
            for h in kvs:
                krow, vrow = KV0 + HD * h, KV0 + HD * (NKV + h)
                gk = dkns[h] * kg_ref[...]
                dk = rks[h] * gk - ks[h] * (rks[h] * rks[h] * rks[h] * jnp.mean(gk * ks[h], axis=0, keepdims=True))
                o_ref[krow:krow + HD, :] = (carry[krow:krow + HD, :] + dk[:, :CHUNK]).astype(BF16)
                o_ref[vrow:vrow + HD, :] = (carry[vrow:vrow + HD, :] + dvs[h][:, :CHUNK]).astype(BF16)
                carry[krow:krow + HD, :] = dk[:, CHUNK:]
                carry[vrow:vrow + HD, :] = dvs[h][:, CHUNK:]

        @pl.when(n == nb)
        def _():
            o_ref[...] = carry[...].astype(BF16)
            dqg_out[...] = jnp.sum(dqg_ref[...], axis=1, keepdims=True)
            dkg_out[...] = jnp.sum(dkg_ref[...], axis=1, keepdims=True)
            dsk_out[...] = jnp.sum(dsk_ref[...], axis=1, keepdims=True)

    cur = lambda n: (0, jnp.minimum(n, nb - 1))
    col = pl.BlockSpec((HD, 1), lambda n: (0, 0))
    whole = lambda shape: pl.BlockSpec(shape, lambda n: (0,) * len(shape))
    return _run(
        body, [qkv_t, qkv_t, do_t, qg, kg, sinks, bias], hook, grid=(nb + 1,), name=name, semantics=("arbitrary",),
        in_specs=[pl.BlockSpec((QKV, CHUNK), cur),
                  pl.BlockSpec((QKV - KV0, CHUNK), lambda n: (KV0 // (QKV - KV0), jnp.clip(n - 1, 0, nb - 1))),
                  pl.BlockSpec((D, CHUNK), cur), col, col, pl.BlockSpec(memory_space=pltpu.SMEM), whole((NH, 2 * CHUNK, CHUNK))],
        out_specs=[pl.BlockSpec((QKV, CHUNK), lambda n: (0, jnp.maximum(n - 1, 0))), whole((HD, 1)), whole((HD, 1)),
                   whole((NH, 1)), whole((NH, 2 * CHUNK, CHUNK))],
        out_shape=[S((QKV, t), BF16), S((HD, 1), F32), S((HD, 1), F32), S((NH, 1), F32), S((NH, 2 * CHUNK, CHUNK), F32)],
        scratch_shapes=[pltpu.VMEM((QKV, CHUNK), F32), pltpu.VMEM((HD, CHUNK), F32), pltpu.VMEM((HD, 2 * CHUNK), F32),
                        pltpu.VMEM((NH, CHUNK), F32)])


class _Plain:
    def __init__(self, wg):
        self.full, self.grads = wg, {}

    def w(self, n):
        return self.full[n]

    def hook(self, host):
        return None

    def grad(self, n, pair):
        self.grads[n] = pair

    def small(self, g_rep):
        pass

    def sync(self, point):
        pass


def _local_step(x, target, rep, sch):
    bucket_row = jnp.asarray(_rel_tables().T.reshape(1, -1))
    bias = _relbias_fwd(rep["rel_bias"].T, bucket_row, "relbias_fwd").reshape(NH, 2 * CHUNK, CHUNK)
    bst = rep["sgu_b_s"][0].T
    ws = rep["sgu_w_s"][0]
    vgain = rep["sgu_v_gain"]
    qg, kg, sinks = rep["attn_q_gain"].reshape(HD, 1), rep["attn_k_gain"].reshape(HD, 1), rep["attn_sinks"][0]
    w_down = lambda l: sch.w("ffn_w_down%d" % l).reshape(D_FF, D)
    w_up = lambda l: sch.w("ffn_w_up%d" % l)
    cw = [sch.w("ffn_conv_w")[:, 3 * l:3 * l + 3] for l in range(2)]
    cb = [rep["ffn_conv_b"][l].reshape(8, 1, -1) for l in range(2)]
    mixg = [rep["mix_norm"][l:l + 1] for l in range(2)]
    ffng = [rep["ffn_norm"][l:l + 1] for l in range(2)]
    rows = lambda pair: tuple(g.reshape(N_DEV, -1, D) for g in pair)
    hk = sch.hook

    hn0 = _rmsnorm(x, mixg[0], "norm0")
    a0 = _mm_slot(hn0, sch.w("sgu_w_in"), BF16, "sgu_in", hk("sgu_in"))
    gated = _sgu_gate_fwd(a0, vgain, ws, bst, "sgu_gate", hk("sgu_gate"))
    h1, hn1 = _resid_mm(gated, sch.w("sgu_w_out").reshape(SGU_W, D), x, ffng[0], "norm", "sgu_out", hk("sgu_out"))
    a_ff0, c_ff0, h2, hn2 = _ffn_fwd(hn1, h1, w_up(0), w_down(0), cw[0], cb[0], mixg[1], "norm", "ffn0_fwd", hk("ffn0_fwd"))
    qkv = _mm_t(hn2, sch.w("attn_w_qkv"), "qkv", hk("qkv"))
    o = _attn_fwd(qkv, qg, kg, sinks, bias, "attn", hk("attn"))
    h3, hn3 = _resid_mm(o, sch.w("attn_w_o").reshape(D, D), h2, ffng[1], "norm", "attn_out", hk("attn_out"), fm=True)
    a_ff1, c_ff1, dy, sq = _ffn_fwd(hn3, h3, w_up(1), w_down(1), cw[1], cb[1], target, "loss", "ffn1_fwd_loss", hk("ffn1_fwd_loss"))
    loss = (0.5 / D) * jnp.sum(sq[:, 0, 0])

    def ffn_bwd(dh, h_in, hn, a, c, l, tag):
        dc, g_down, g_down_b = _ffn_bwd1(dh, c, w_down(l), tag + "_bwd1", hk(tag + "_bwd1"))
        sch.grad("ffn_w_down%d" % l, rows((g_down, g_down_b)))
        da, dh_new, dgain, g_cw, g_cb = _ffn_bwd2(dc, a, w_up(l), cw[l], h_in, ffng[l], dh, tag + "_bwd2", hk(tag + "_bwd2"))
        sch.grad("ffn_w_up%d" % l, _dw_slot(hn, da, tag + "_dw_up", hk(tag + "_dw_up")))
        return dh_new, dgain, g_cw, g_cb.reshape(-1)

    dh, d_ffng1, g_cw1, g_cb1 = ffn_bwd(dy, h3, hn3, a_ff1, c_ff1, 1, "ffn1")
    do = _dx_rows_t(dh, sch.w("attn_w_o").reshape(D, D), "attn_do", hk("attn_do"))
    sch.grad("attn_w_o", rows(_dw_rows(o, dh, "dw_o", hk("dw_o"), fm=True)))
    dqkv, d_qg, d_kg, d_sk, d_bias = _attn_bwd(qkv, do, qg, kg, sinks, bias, "attn_bwd", hk("attn_bwd"))
    sch.grad("attn_w_qkv", tuple(g.reshape(N_DEV, -1, D) for g in _dw_rows(dqkv, hn2, "dw_qkv", hk("dw_qkv"), fm=True)))
    dh, d_mixg1 = _dx_slot_normbwd(dqkv, sch.w("attn_w_qkv").reshape(QKV, D), h2, mixg[1], dh, "dx_qkv", hk("dx_qkv"), fm=True,
                                   out_dtype=DH)
    d_relb = _relbias_bwd(d_bias.reshape(NH, -1), bucket_row, "relbias_bwd").T
    g_rep = {"attn_q_gain": d_qg.reshape(1, HD), "attn_k_gain": d_kg.reshape(1, HD), "attn_sinks": d_sk.reshape(1, NH),
             "rel_bias": d_relb}
    sch.small(g_rep)
    dh, d_ffng0, g_cw0, g_cb0 = ffn_bwd(dh, h1, hn1, a_ff0, c_ff0, 0, "ffn0")
    g_cw = jnp.concatenate([g_cw0, g_cw1], axis=1)
    sch.grad("ffn_conv_w", (g_cw, g_cw.astype(BF16)))
    g_ffn = {"ffn_norm": jnp.concatenate([d_ffng0, d_ffng1], axis=0), "ffn_conv_b": jnp.stack([g_cb0, g_cb1], axis=0)}
    sch.small(g_ffn)
    dgated = _dx_rows(dh, sch.w("sgu_w_out").reshape(SGU_W, D), SGU_W // 4, BF16, "sgu_dgated", hk("sgu_dgated"))
    sch.grad("sgu_w_out", rows(_dw_rows(gated, dh, "dw_sgu_out", hk("dw_sgu_out"))))
    da0, d_ws, d_bst, d_vgain = _sgu_gate_bwd(a0, dgated, vgain, ws, bst, "sgu_gate_bwd", hk("sgu_gate_bwd"))
    g_sgu = {"sgu_v_gain": d_vgain, "sgu_w_s": d_ws[None], "sgu_b_s": d_bst[:, :SGU_G].T[None]}
    sch.small(g_sgu)
    sch.grad("sgu_w_in", _dw_slot(hn0, da0, "dw_sgu_in", hk("dw_sgu_in")))
    sch.sync("after_dw")
    grad_x, d_mixg0 = _dx_slot_normbwd(da0, sch.w("sgu_w_in"), x, mixg[0], dh, "dx_sgu_in", hk("dx_sgu_in"))
    g_mix = {"mix_norm": jnp.concatenate([d_mixg0, d_mixg1], axis=0)}
    sch.small(g_mix)
    for g in (g_ffn, g_sgu, g_mix):
        g_rep.update(g)
    return loss, grad_x, g_rep


def _allgather(xs, name):
    nt = len(xs)

    def body(*refs):
        x_refs, o_refs = refs[:nt], refs[nt:2 * nt]
        send_sems, recv_sems, local_sems = refs[2 * nt:]
        x, y, c, chips = _place()
        me, sibling = (x, y, c), (x, y, 1 - c)

        def copy(t, k, block, to, src=None):
            px, py, pc = block
            dst = o_refs[t].at[4 * px + 2 * py + pc]
            return pltpu.make_async_remote_copy(
                src_ref=dst if src is None else src, dst_ref=dst, send_sem=send_sems.at[t, k], recv_sem=recv_sems.at[t, k],
                device_id=to, device_id_type=MESH)

        mine = [pltpu.make_async_copy(x_refs[t], o_refs[t].at[4 * x + 2 * y + c], local_sems.at[t]) for t in range(nt)]
        for cp in mine:
            cp.start()
        first = []
        for t in range(nt):
            first.append(copy(t, 0, me, sibling, src=x_refs[t]))
            first += [copy(t, 1 + j, me, (*chip, c), src=x_refs[t]) for j, chip in enumerate(chips)]
        for cp in first:
            cp.start()
        passed = []
        for j, chip in enumerate(chips):
            for t in range(nt):
                copy(t, 1 + j, (*chip, c), me).wait_recv()
                fwd = copy(t, 4 + j, (*chip, c), sibling)
                fwd.start()
                passed.append(fwd)
        for t in range(nt):
            copy(t, 0, sibling, me).wait_recv()
            for j, chip in enumerate(chips):
                copy(t, 4 + j, (*chip, 1 - c), me).wait_recv()
        for cp in first + passed:
            cp.wait_send()
        for cp in mine:
            cp.wait()

    return pl.pallas_call(
        body, name=name, in_specs=[ANY] * nt, out_specs=[ANY] * nt,
        out_shape=[S((N_DEV,) + a.shape, a.dtype) for a in xs],
        scratch_shapes=[pltpu.SemaphoreType.DMA((nt, 7)), pltpu.SemaphoreType.DMA((nt, 7)), pltpu.SemaphoreType.DMA((nt,))],
        compiler_params=pltpu.CompilerParams(has_side_effects=True))(*xs)


def _exchange(hook, name):
    comm = hook()
    ci, co = len(comm.inputs), len(comm.out_shapes)

    def body(*refs):
        cins, couts = refs[:ci], refs[ci:ci + co]
        send, recv = refs[-2:]
        comm.start(cins, couts, send, recv)
        comm.finish(cins, couts, send, recv)

    res = pl.pallas_call(
        body, name=name, in_specs=[ANY] * ci, out_specs=[ANY] * co, out_shape=comm.out_shapes,
        scratch_shapes=[pltpu.SemaphoreType.DMA((comm.n_sems,)), pltpu.SemaphoreType.DMA((comm.n_sems,))],
        input_output_aliases=dict(comm.aliases),
        compiler_params=pltpu.CompilerParams(has_side_effects=True))(*comm.inputs)
    hook(res)


def _row_tile(r):
    tr = r if r <= ROW_TILE or r % ROW_TILE else ROW_TILE
    assert r % tr == 0
    return tr


def _rs_partial(g32, sib, place, name):
    _, r, cdim = g32.shape
    tr = _row_tile(r)

    def body(place_ref, g_ref, s_ref, p_ref, own_ref):
        k = pl.program_id(1)
        tot = g_ref[...] + s_ref[...].astype(F32)
        p_ref[...] = tot.astype(BF16)

        @pl.when(k == place_ref[1])
        def _():
            own_ref[...] = tot

    grid_spec = pltpu.PrefetchScalarGridSpec(
        num_scalar_prefetch=1, grid=(r // tr, 4),
        in_specs=[pl.BlockSpec((None, None, tr, cdim), lambda i, k, pr: (k, pr[0], i, 0)),
                  pl.BlockSpec((None, tr, cdim), lambda i, k, pr: (k, i, 0))],
        out_specs=[pl.BlockSpec((None, tr, cdim), lambda i, k, pr: (k, i, 0)), pl.BlockSpec((tr, cdim), lambda i, k, pr: (i, 0))])
    return pl.pallas_call(
        body, grid_spec=grid_spec, name=name,
        out_shape=[S((4, r, cdim), BF16), S((r, cdim), F32)],
        compiler_params=_cp("parallel", "arbitrary"))(place, g32.reshape(4, 2, r, cdim), sib)


def _adamw_math(w, g, m, v):
    m = ADAM_B1 * m + (1.0 - ADAM_B1) * g
    v = ADAM_B2 * v + (1.0 - ADAM_B2) * (g * g)
    m_hat = m / (1.0 - ADAM_B1 ** ADAM_STEP)
    v_hat = v / (1.0 - ADAM_B2 ** ADAM_STEP)
    delta = -ADAM_LR * (m_hat / (jnp.sqrt(v_hat) + ADAM_EPS) + ADAM_WD * w)
    return delta, m, v


def _adamw_shard(owns, recvs, w, m, v, name, flipped=False):
    nl = w.shape[0]
    r, cdim = owns[0].shape
    tr = _row_tile(r)
    nr = r // tr

    def body(*refs):
        own_refs, recv_refs = refs[:nl], refs[nl:2 * nl]
        w_ref, m_ref, v_ref, g_out, d_out, m_out, v_out = refs[2 * nl:]
        layer = pl.program_id(0)
        g = None
        for l in range(nl):
            gl = own_refs[l][...] + recv_refs[l][0].astype(F32) + recv_refs[l][1].astype(F32) + recv_refs[l][2].astype(F32)
            g = gl if g is None else jnp.where(layer == l, gl, g)
        if flipped:
            g = g.T
        g_out[...] = g
        d_out[...], m_out[...], v_out[...] = _adamw_math(w_ref[...], g, m_ref[...], v_ref[...])

    park = lambda l: (lambda layer, i: (jnp.where(layer == l, i, jnp.where(layer < l, 0, nr - 1)), 0))
    park3 = lambda l: (lambda layer, i: (0, jnp.where(layer == l, i, jnp.where(layer < l, 0, nr - 1)), 0))
    if flipped:
        row = pl.BlockSpec((None, cdim, tr), lambda layer, i: (layer, 0, i))
    else:
        row = pl.BlockSpec((None, tr, cdim), lambda layer, i: (layer, i, 0))
    return pl.pallas_call(
        body, grid=(nl, nr), name=name,
        in_specs=[pl.BlockSpec((tr, cdim), park(l)) for l in range(nl)] + [pl.BlockSpec((3, tr, cdim), park3(l)) for l in range(nl)]
        + [row, row, row],
        out_specs=[row] * 4, out_shape=[S(w.shape, F32)] * 4,
        compiler_params=_cp("arbitrary", "arbitrary"))(*owns, *recvs, w, m, v)


def _adamw_small(galls, ws, ms, vs, name):
    n = len(galls)

    def body(*refs):
        g_refs, w_refs, m_refs, v_refs, outs = refs[:n], refs[n:2 * n], refs[2 * n:3 * n], refs[3 * n:4 * n], refs[4 * n:]
        for i in range(n):
            g = g_refs[i][0].astype(F32)
            for s in range(1, N_DEV):
                g = g + g_refs[i][s].astype(F32)
            outs[i][...] = g
            outs[n + i][...], outs[2 * n + i][...], outs[3 * n + i][...] = _adamw_math(w_refs[i][...], g, m_refs[i][...], v_refs[i][...])

    res = pl.pallas_call(body, out_shape=[S(a.shape, F32) for a in ws] * 4, name=name)(*galls, *ws, *ms, *vs)
    return [res[k * n:(k + 1) * n] for k in range(4)]


REPLICATED = ["mix_norm", "ffn_norm", "sgu_v_gain", "sgu_w_s", "sgu_b_s", "attn_q_gain", "attn_k_gain", "attn_sinks", "rel_bias",
              "ffn_conv_b"]
WEIGHTS = ["mix_norm", "ffn_norm", "sgu_w_in", "sgu_v_gain", "sgu_w_s", "sgu_b_s", "sgu_w_out", "attn_w_qkv", "attn_q_gain",
           "attn_k_gain", "attn_sinks", "attn_w_o", "rel_bias", "ffn_w_up", "ffn_conv_w", "ffn_conv_b", "ffn_w_down"]
SMALL = ["g_" + n for n in REPLICATED]
BF16_TRANSIT = {"sgu_w_s"}
SMALL_ATTN = ["g_attn_q_gain", "g_attn_k_gain", "g_attn_sinks", "g_rel_bias"]
SMALL_FFN = ["g_ffn_norm", "g_ffn_conv_b"]
SMALL_SGU = ["g_sgu_v_gain", "g_sgu_w_s", "g_sgu_b_s"]

GATHER_FIRST = ["sgu_w_in", "ffn_conv_w"]
PLAN = {
    "sgu_in": [("ag1", "sgu_w_out"), ("ag1", "ffn_w_down0")],
    "sgu_gate": [("ag2", "sgu_w_out"), ("ag2", "ffn_w_down0"), ("ag1", "ffn_w_up0")],
    "sgu_out": [("ag2", "ffn_w_up0"), ("ag1", "attn_w_qkv")],
    "ffn0_fwd": [("ag2", "attn_w_qkv"), ("ag1", "attn_w_o"), ("ag1", "ffn_w_up1")],
    "qkv": [("ag2", "attn_w_o"), ("ag2", "ffn_w_up1")],
    "attn": [("ag1", "ffn_w_down1")],
    "attn_out": [("ag2", "ffn_w_down1")],
    "ffn1_bwd2": [("rs1", "ffn_w_down1")],
    "ffn1_dw_up": [("rs2", "ffn_w_down1")],
    "attn_do": [("rs1", "ffn_w_up1")],
    "attn_bwd": [("rs2", "ffn_w_up1"), ("rs1", "attn_w_o")],
    "dw_qkv": [("rs2", "attn_w_o")],
    "dx_qkv": [("rs1", "attn_w_qkv")],
    "ffn0_bwd1": [("rs2", "attn_w_qkv")] + [("ag1", n) for n in SMALL_ATTN],
    "ffn0_bwd2": [("rs1", "ffn_w_down0")] + [("ag2", n) for n in SMALL_ATTN],
    "ffn0_dw_up": [("rs2", "ffn_w_down0")],
    "sgu_dgated": [("rs1", "ffn_w_up0")] + [("ag1", n) for n in SMALL_FFN],
    "dw_sgu_out": [("ag2", n) for n in SMALL_FFN],
    "sgu_gate_bwd": [("rs2", "ffn_w_up0"), ("rs1", "sgu_w_out")],
    "dw_sgu_in": [("rs2", "sgu_w_out")] + [("ag1", n) for n in SMALL_SGU],
    "after_dw": [("rs1", "sgu_w_in"), ("rs1", "ffn_conv_w")] + [("ag2", n) for n in SMALL_SGU],
    "dx_sgu_in": [("rs2", "sgu_w_in"), ("rs2", "ffn_conv_w")],
    "last": [("agd", "g_mix_norm")],
}


class _Overlap:
    def __init__(self, shard, place):
        self.shard, self.place = shard, place
        self.part, self.full = {}, {}
        self.grads, self.sib, self.own, self.recv = {}, {}, {}, {}

    def w(self, n):
        return self.full[n]

    def grad(self, n, pair):
        self.grads[n] = pair

    def small(self, g_rep):
        self.shard.update(("g_" + n, a.astype(BF16) if n in BF16_TRANSIT else a) for n, a in _views2d(g_rep).items())

    def sync(self, point):
        _exchange(self.hook(point), point)

    def chip_sums(self, n):
        sums, self.own[n] = _rs_partial(self.grads[n][0], self.sib.pop(n), self.place, "rs_partial_" + n)
        return sums

    def hook(self, host):
        ops = PLAN.get(host)
        if not ops:
            return None
        where = {"ag1": self.part, "ag2": self.full, "agd": self.full, "rs1": self.sib, "rs2": self.recv}
        idx = []

        def hook(results=None):
            if results is not None:
                for (kind, n), i in zip(ops, idx):
                    where[kind][n] = results[i]
                return None
            comm = _Comm()
            for kind, n in ops:
                arr = {"ag1": lambda: self.shard[n], "agd": lambda: self.shard[n], "ag2": lambda: self.part.pop(n),
                       "rs1": lambda: self.grads[n][1], "rs2": lambda: self.chip_sums(n)}[kind]()
                idx.append(comm.add(kind, arr))
            return comm

        return hook


TRANSPOSED = {"attn_w_qkv"}
PHYSICAL_T = {"ffn_w_up"}
SHARDED = {
    "sgu_w_in": ["sgu_w_in"], "sgu_w_out": ["sgu_w_out"], "attn_w_qkv": ["attn_w_qkv"], "attn_w_o": ["attn_w_o"],
    "ffn_w_up": ["ffn_w_up0", "ffn_w_up1"], "ffn_w_down": ["ffn_w_down0", "ffn_w_down1"], "ffn_conv_w": ["ffn_conv_w"],
}


def _send_views(w):
    out = {"ffn_conv_w": w["ffn_conv_w"].reshape(6, -1)}
    for name, parts in SHARDED.items():
        if name != "ffn_conv_w":
            out.update((p, (w[name][l].T if name in TRANSPOSED else w[name][l]).astype(BF16)) for l, p in enumerate(parts))
    return out


def _views2d(d):
    return {n: d[n].reshape(-1, d[n].shape[-1]) for n in REPLICATED if n in d}


def kernel(x, mix_norm, ffn_norm, sgu_w_in, sgu_v_gain, sgu_w_s, sgu_b_s, sgu_w_out, attn_w_qkv, attn_q_gain, attn_k_gain, attn_sinks, attn_w_o, rel_bias, ffn_w_up, ffn_conv_w, ffn_conv_b, ffn_w_down, loss_target, m_mix_norm, m_ffn_norm, m_sgu_w_in, m_sgu_v_gain, m_sgu_w_s, m_sgu_b_s, m_sgu_w_out, m_attn_w_qkv, m_attn_q_gain, m_attn_k_gain, m_attn_sinks, m_attn_w_o, m_rel_bias, m_ffn_w_up, m_ffn_conv_w, m_ffn_conv_b, m_ffn_w_down, v_mix_norm, v_ffn_norm, v_sgu_w_in, v_sgu_v_gain, v_sgu_w_s, v_sgu_b_s, v_sgu_w_out, v_attn_w_qkv, v_attn_q_gain, v_attn_k_gain, v_attn_sinks, v_attn_w_o, v_rel_bias, v_ffn_w_up, v_ffn_conv_w, v_ffn_conv_b, v_ffn_w_down):
    w = dict(zip(WEIGHTS, (mix_norm, ffn_norm, sgu_w_in, sgu_v_gain, sgu_w_s, sgu_b_s, sgu_w_out, attn_w_qkv, attn_q_gain, attn_k_gain,
                           attn_sinks, attn_w_o, rel_bias, ffn_w_up, ffn_conv_w, ffn_conv_b, ffn_w_down)))
    m = dict(zip(WEIGHTS, (m_mix_norm, m_ffn_norm, m_sgu_w_in, m_sgu_v_gain, m_sgu_w_s, m_sgu_b_s, m_sgu_w_out, m_attn_w_qkv, m_attn_q_gain,
                           m_attn_k_gain, m_attn_sinks, m_attn_w_o, m_rel_bias, m_ffn_w_up, m_ffn_conv_w, m_ffn_conv_b, m_ffn_w_down)))
    v = dict(zip(WEIGHTS, (v_mix_norm, v_ffn_norm, v_sgu_w_in, v_sgu_v_gain, v_sgu_w_s, v_sgu_b_s, v_sgu_w_out, v_attn_w_qkv, v_attn_q_gain,
                           v_attn_k_gain, v_attn_sinks, v_attn_w_o, v_rel_bias, v_ffn_w_up, v_ffn_conv_w, v_ffn_conv_b, v_ffn_w_down)))
    rep = {n: w[n] for n in REPLICATED}

    xi, yi, ci = lax.axis_index("x"), lax.axis_index("y"), lax.axis_index("c")
    place = jnp.stack([ci, 2 * xi + yi]).astype(jnp.int32)
    sch = _Overlap(_send_views(w), place)
    sch.full.update(zip(GATHER_FIRST, _allgather([sch.shard[n] for n in GATHER_FIRST], "gather_first")))

    loss, grad_x, g_rep = _local_step(x[0], loss_target[0], rep, sch)
    loss = lax.psum(loss, ("x", "y", "c"))
    sch.sync("last")

    out = [{}, {}, {}, {}]
    for name, parts in SHARDED.items():
        flip = (lambda a: jnp.swapaxes(a, -1, -2)) if name in TRANSPOSED | PHYSICAL_T else (lambda a: a)
        shape = flip(w[name]).shape
        as3d = lambda a: flip(a).reshape(len(parts), -1, shape[-1])
        res = _adamw_shard([sch.own[p] for p in parts], [sch.recv[p] for p in parts], as3d(w[name]), as3d(m[name]), as3d(v[name]),
                           "adamw_" + name, flipped=name in PHYSICAL_T)
        for o, r in zip(out, res):
            o[name] = flip(r.reshape(shape))
    small = _adamw_small([sch.full[n] for n in SMALL], *[list(_views2d(d).values()) for d in (rep, m, v)], "adamw_small")
    for o, res in zip(out, small):
        o.update((n, r.reshape(w[n].shape)) for n, r in zip(REPLICATED, res))

    return (loss, grad_x[None], *[out[0][n] for n in WEIGHTS], *[out[1][n] for n in WEIGHTS],
            *[out[2][n] for n in WEIGHTS], *[out[3][n] for n in WEIGHTS])
```

```python
import functools
import math

import numpy as np
import jax
import jax.numpy as jnp
from jax import lax
from jax.experimental import pallas as pl
from jax.experimental.pallas import tpu as pltpu

F32 = jnp.float32
BF16 = jnp.bfloat16
DH = jnp.bfloat16
S = jax.ShapeDtypeStruct

D = 1024
CHUNK = 128
SGU_W = 2048
SGU_G = 16
HD = 64
NH = 16
NKV = 4
KVG = 4
D_FF = 2816
REL_BUCKETS = 32
REL_MAX_DIST = 128
EPS = 1e-6
N_DEV = 8
MESH = pl.DeviceIdType.MESH

ADAM_LR = 0.001
ADAM_B1 = 0.9
ADAM_B2 = 0.999
ADAM_EPS = 1e-08
ADAM_WD = 0.01
ADAM_STEP = 10

ROW_TILE = 512
HALO = 8
FFN_ROWS = 256


def _tm(t):
    return min(ROW_TILE, t)


def _cp(*sem):
    return pltpu.CompilerParams(dimension_semantics=sem)


ANY = pl.BlockSpec(memory_space=pl.ANY)


def _place():
    x, y, c = lax.axis_index("x"), lax.axis_index("y"), lax.axis_index("c")
    return x, y, c, [(1 - x, y), (x, 1 - y), (1 - x, 1 - y)]


class _Comm:
    SEMS = {"ag1": 5, "ag2": 3, "rs1": 4, "rs2": 3, "agd": 8}

    def __init__(self):
        self.inputs, self.out_shapes, self.aliases, self.ops, self.n_sems = [], [], {}, [], 0

    def add(self, kind, arr, rows=None, into=None):
        lead = {"ag1": N_DEV, "agd": N_DEV, "ag2": None, "rs1": 4, "rs2": 3}[kind]
        shape = arr.shape if lead is None else (lead,) + arr.shape[(0 if kind in ("ag1", "agd") else 1):]
        if kind == "ag2":
            self.aliases[len(self.inputs)] = len(self.out_shapes)
        self.ops.append((kind, len(self.inputs), len(self.out_shapes), self.n_sems, rows))
        self.inputs.append(arr)
        if into is not None:
            self.aliases[len(self.inputs)] = len(self.out_shapes)
            self.inputs.append(into)
        self.out_shapes.append(S(shape, arr.dtype))
        self.n_sems += self.SEMS[kind]
        return len(self.out_shapes) - 1

    def _copies(self, ins, outs, send, recv):
        x, y, c, chips = _place()
        me, sibling = (x, y, c), (x, y, 1 - c)
        slot = lambda px, py, pc: 4 * px + 2 * py + pc
        sends, recvs, local = [], [], []

        def rc(src, dst, k, to):
            return lambda: pltpu.make_async_remote_copy(src_ref=src(), dst_ref=dst(), send_sem=send.at[k], recv_sem=recv.at[k],
                                                        device_id=to, device_id_type=MESH)

        for kind, ii, oi, b, rows in self.ops:
            src, dst = ins[ii], outs[oi]
            at = lambda ref, i: (lambda: ref.at[i])
            if kind == "ag1":
                part = slice(None) if rows is None else pl.ds(rows[0], rows[1] - rows[0])
                to = lambda i, d=dst, p=part: (lambda: d.at[i, p])
                whole, mine = (lambda s=src, p=part: s.at[p]), to(slot(*me))
                sends.append(rc(whole, mine, b, sibling))
                recvs.append(rc(whole, to(slot(x, y, 1 - c)), b, me))
                for j, chip in enumerate(chips):
                    sends.append(rc(whole, mine, b + 1 + j, (*chip, c)))
                    recvs.append(rc(whole, to(slot(*chip, c)), b + 1 + j, me))
                local.append(lambda s=whole, m=mine, k=b + 4: pltpu.make_async_copy(s(), m(), send.at[k]))
            elif kind == "ag2":
                for j, chip in enumerate(chips):
                    sends.append(rc(at(dst, slot(*chip, c)), at(dst, slot(*chip, c)), b + j, sibling))
                    recvs.append(rc(at(dst, slot(*chip, 1 - c)), at(dst, slot(*chip, 1 - c)), b + j, me))
            elif kind == "agd":
                whole, mine = (lambda s=src: s), at(dst, slot(*me))
                flip = lambda v, bit: 1 - v if bit else v
                for k in range(1, N_DEV):
                    peer = (flip(x, k >> 2), flip(y, (k >> 1) & 1), flip(c, k & 1))
                    sends.append(rc(whole, mine, b + k - 1, peer))
                    recvs.append(rc(whole, at(dst, slot(*peer)), b + k - 1, me))
                local.append(lambda s=src, m=mine, k=b + 7: pltpu.make_async_copy(s, m(), send.at[k]))
            elif kind == "rs1":
                for k in range(4):
                    sends.append(rc(at(src, 2 * k + (1 - c)), at(dst, k), b + k, sibling))
                    recvs.append(rc(at(src, 2 * k + c), at(dst, k), b + k, me))
            else:
                for j, (px, py) in enumerate(chips):
                    sends.append(rc(at(src, 2 * px + py), at(dst, j), b + j, (px, py, c)))
                    recvs.append(rc(at(src, 2 * px + py), at(dst, j), b + j, me))
        return sends, recvs, local

    def start(self, ins, outs, send, recv):
        sends, _, local = self._copies(ins, outs, send, recv)
        for make in local + sends:
            make().start()

    def finish(self, ins, outs, send, recv):
        sends, recvs, local = self._copies(ins, outs, send, recv)
        for make in recvs:
            make().wait_recv()
        for make in sends:
            make().wait_send()
        for make in local:
            make().wait()


def _run(body, args, hook, *, grid, in_specs, out_specs, out_shape, name, semantics, scratch_shapes=(), aliases=None):
    comm = hook() if hook is not None else None
    aliases = dict(aliases or {})
    if comm is None:
        return pl.pallas_call(body, grid=grid, in_specs=in_specs, out_specs=out_specs, out_shape=out_shape, name=name,
                              scratch_shapes=list(scratch_shapes), input_output_aliases=aliases,
                              compiler_params=_cp(*semantics))(*args)
    single = not isinstance(out_shape, (list, tuple))
    out_shapes = [out_shape] if single else list(out_shape)
    out_specs_l = [out_specs] if single else list(out_specs)
    n_in, n_out, n_scr, ci, co = len(args), len(out_shapes), len(scratch_shapes), len(comm.inputs), len(comm.out_shapes)

    def wrapped(*refs):
        ins, cins = refs[:n_in], refs[n_in:n_in + ci]
        outs, couts = refs[n_in + ci:n_in + ci + n_out], refs[n_in + ci + n_out:n_in + ci + n_out + co]
        scr = refs[n_in + ci + n_out + co:n_in + ci + n_out + co + n_scr]
        send, recv = refs[-2:]
        first = functools.reduce(lambda a, b: a & b, [pl.program_id(a) == 0 for a in range(len(grid))])
        last = functools.reduce(lambda a, b: a & b, [pl.program_id(a) == g - 1 for a, g in enumerate(grid)])

        @pl.when(first)
        def _():
            comm.start(cins, couts, send, recv)

        body(*ins, *outs, *scr)

        @pl.when(last)
        def _():
            comm.finish(cins, couts, send, recv)

    res = pl.pallas_call(
        wrapped, grid=grid, in_specs=list(in_specs) + [ANY] * ci, out_specs=out_specs_l + [ANY] * co,
        out_shape=out_shapes + comm.out_shapes, name=name,
        scratch_shapes=list(scratch_shapes) + [pltpu.SemaphoreType.DMA((comm.n_sems,)), pltpu.SemaphoreType.DMA((comm.n_sems,))],
        input_output_aliases={**aliases, **{n_in + k: n_out + v for k, v in comm.aliases.items()}},
        compiler_params=pltpu.CompilerParams(dimension_semantics=("arbitrary",) * len(grid), has_side_effects=True))(*args, *comm.inputs)
    hook(res[n_out:])
    return res[0] if single else list(res[:n_out])


def _dot(a, b):
    return jnp.dot(a, b, preferred_element_type=F32)


def _dot_nt(a, b):
    return lax.dot_general(a, b, (((1,), (1,)), ((), ())), preferred_element_type=F32)


def _dot_tn(a, b):
    return lax.dot_general(a, b, (((0,), (0,)), ((), ())), preferred_element_type=F32)


def _gelu(x):
    return 0.5 * x * (1.0 + lax.erf(x * (2.0 ** -0.5)))


def _gelu_and_grad(x):
    cdf = 0.5 * (1.0 + lax.erf(x * (2.0 ** -0.5)))
    return x * cdf, cdf + x * jnp.exp(-0.5 * x * x) * (1.0 / math.sqrt(2.0 * math.pi))


def _sigmoid(x):
    return 1.0 / (1.0 + jnp.exp(-x))


def _rstd(x):
    return lax.rsqrt(jnp.mean(x * x, axis=-1, keepdims=True) + EPS)


def _rel_tables():
    q = np.arange(CHUNK)[:, None] + CHUNK
    k = np.arange(2 * CHUNK)[None, :]
    dist = q - k
    n = np.maximum(dist, 0)
    max_exact = REL_BUCKETS // 2
    large = max_exact + (np.log(np.maximum(n, 1).astype(np.float32) / max_exact)
                         / math.log(REL_MAX_DIST / max_exact) * (REL_BUCKETS - max_exact)).astype(np.int32)
    large = np.minimum(large, REL_BUCKETS - 1)
    return np.where(n < max_exact, n, large).astype(np.int32)


def _rmsnorm(x, gain, name):
    t = x.shape[0]
    tm = _tm(t)

    def body(x_ref, g_ref, o_ref):
        xv = x_ref[...]
        o_ref[...] = (xv * _rstd(xv) * g_ref[...]).astype(BF16)

    return pl.pallas_call(
        body, grid=(t // tm,), name=name,
        in_specs=[pl.BlockSpec((tm, D), lambda i: (i, 0)), pl.BlockSpec((1, D), lambda i: (0, 0))],
        out_specs=pl.BlockSpec((tm, D), lambda i: (i, 0)),
        out_shape=S((t, D), BF16), compiler_params=_cp("parallel"))(x, gain)


def _resident(shape):
    zeros = (0,) * len(shape)
    return pl.BlockSpec(shape, lambda *_: zeros, pipeline_mode=pl.Buffered(1))


def _mm_slot(hn, wg, out_dtype, name, hook=None):
    t, k = hn.shape
    ns, _, n = wg.shape
    tm = _tm(t)

    def body(a_ref, w_ref, o_ref):
        a = a_ref[...]
        for s in range(ns):
            o_ref[s] = _dot(a, w_ref[s]).astype(out_dtype)

    return _run(
        body, [hn, wg], hook, grid=(t // tm,), name=name, semantics=("parallel",),
        in_specs=[pl.BlockSpec((tm, k), lambda i: (i, 0)), _resident(wg.shape)],
        out_specs=pl.BlockSpec((ns, tm, n), lambda i: (0, i, 0)), out_shape=S((ns, t, n), out_dtype))


def _mm_t(hn, wt, name, hook=None):
    t, k = hn.shape
    ns, n, _ = wt.shape
    tm = _tm(t)

    def body(a_ref, w_ref, o_ref):
        a = a_ref[...]
        for s in range(ns):
            o_ref[s * n:(s + 1) * n, :] = _dot_nt(w_ref[s], a)

    return _run(
        body, [hn, wt], hook, grid=(t // tm,), name=name, semantics=("parallel",),
        in_specs=[pl.BlockSpec((tm, k), lambda i: (i, 0)), _resident(wt.shape)],
        out_specs=pl.BlockSpec((ns * n, tm), lambda i: (0, i)), out_shape=S((ns * n, t), F32))


def _conv3(a, prev, cw, cb, tm):
    ext = jnp.concatenate([prev, a], axis=0)
    return cw[2:3] * a + cw[1:2] * ext[HALO - 1:HALO - 1 + tm] + cw[0:1] * ext[HALO - 2:HALO - 2 + tm] + cb


def _ffn_fwd(hn, h, wup, wdown, cw, cb, extra, mode, name, hook=None):
    t, k = hn.shape
    n = wup.shape[-1]
    nh = wup.shape[0] // 2
    tm = min(FFN_ROWS, t)
    ni = t // tm

    def body(a_ref, h_ref, wu_ref, wd_ref, cw_ref, cb_ref, e_ref, as_ref, cs_ref, o1_ref, o2_ref, carry):
        i = pl.program_id(0)

        @pl.when(i == 0)
        def _():
            carry[...] = jnp.zeros_like(carry)

        a = a_ref[...]
        acc = h_ref[...]
        nxt = (_dot(a, wu_ref[0]), _dot(a, wu_ref[nh]))
        for j in range(nh):
            ag, av = nxt
            if j + 1 < nh:
                nxt = (_dot(a, wu_ref[j + 1]), _dot(a, wu_ref[nh + j + 1]))
            as_ref[j] = ag.astype(BF16)
            as_ref[nh + j] = av.astype(BF16)
            cg = _conv3(ag, carry[j], cw_ref[j], cb_ref[j], tm)
            cv = _conv3(av, carry[nh + j], cw_ref[nh + j], cb_ref[nh + j], tm)
            carry[j] = ag[tm - HALO:]
            carry[nh + j] = av[tm - HALO:]
            cs_ref[j] = cg.astype(BF16)
            cs_ref[nh + j] = cv.astype(BF16)
            act = (cg * _sigmoid(cg) * cv).astype(BF16)
            acc = acc + _dot(act, wd_ref[j * n:(j + 1) * n, :])
        if mode == "norm":
            o1_ref[...] = acc
            o2_ref[...] = (acc * _rstd(acc) * e_ref[...]).astype(BF16)
        else:
            err = acc - e_ref[...]
            o1_ref[...] = (err * (1.0 / D)).astype(o1_ref.dtype)
            o2_ref[...] = jnp.full(o2_ref.shape, jnp.sum(err * err), F32)

    row = pl.BlockSpec((tm, D), lambda i: (i, 0))
    if mode == "norm":
        e_spec, o2_spec, o2_shape = pl.BlockSpec((1, D), lambda i: (0, 0)), row, S((t, D), BF16)
    else:
        e_spec, o2_spec, o2_shape = row, pl.BlockSpec((None, 8, 128), lambda i: (i, 0, 0)), S((ni, 8, 128), F32)
    aspec = pl.BlockSpec((2 * nh, tm, n), lambda i: (0, i, 0))
    return _run(
        body, [hn, h, wup, wdown, cw, cb, extra], hook, grid=(ni,), name=name, semantics=("arbitrary",),
        in_specs=[pl.BlockSpec((tm, k), lambda i: (i, 0)), row, _resident(wup.shape), _resident(wdown.shape),
                  _resident(cw.shape), _resident(cb.shape), e_spec],
        out_specs=[aspec, aspec, row, o2_spec],
        out_shape=[S((2 * nh, t, n), BF16), S((2 * nh, t, n), BF16), S((t, D), F32 if mode == "norm" else DH), o2_shape],
        scratch_shapes=[pltpu.VMEM((2 * nh, HALO, n), F32)])


def _tril_mask():
    r = lax.broadcasted_iota(jnp.int32, (CHUNK, CHUNK), 0)
    c = lax.broadcasted_iota(jnp.int32, (CHUNK, CHUNK), 1)
    return r >= c


def _sgu_gate_fwd(a_s, vgain, ws, bst, name, hook=None):
    t = a_s.shape[1]
    sw = a_s.shape[2]
    gps = sw // CHUNK

    def body(a_ref, vg_ref, ws_ref, b_ref, o_ref):
        v = _gelu(jnp.concatenate([a_ref[4 + s].astype(F32) for s in range(4)], axis=1))
        vn = (v * _rstd(v) * vg_ref[...]).astype(BF16)
        tri = _tril_mask()
        for g in range(SGU_G):
            w = jnp.where(tri, ws_ref[g], 0.0).astype(BF16)
            sg = _dot(w, vn[:, g * CHUNK:(g + 1) * CHUNK]) + b_ref[:, g:g + 1]
            lo = (g % gps) * CHUNK
            u = _gelu(a_ref[g // gps, :, lo:lo + CHUNK].astype(F32))
            o_ref[g // gps, :, lo:lo + CHUNK] = (u * sg).astype(BF16)

    return _run(
        body, [a_s, vgain, ws, bst], hook, grid=(t // CHUNK,), name=name, semantics=("parallel",),
        in_specs=[pl.BlockSpec((8, CHUNK, sw), lambda n: (0, n, 0)), pl.BlockSpec((1, SGU_W), lambda n: (0, 0)),
                  pl.BlockSpec((SGU_G, CHUNK, CHUNK), lambda n: (0, 0, 0)), pl.BlockSpec((CHUNK, SGU_G), lambda n: (0, 0))],
        out_specs=pl.BlockSpec((4, CHUNK, sw), lambda n: (0, n, 0)), out_shape=S((4, t, sw), BF16))


def _resid_mm(a_s, w, resid, extra, mode, name, hook=None, fm=False):
    nk, t, kc = (1, a_s.shape[1], a_s.shape[0]) if fm else a_s.shape
    tm = _tm(t)
    ni = t // tm

    def body(a_ref, w_ref, r_ref, e_ref, o1_ref, o2_ref):
        h = r_ref[...]
        if fm:
            h = h + _dot_tn(a_ref[...], w_ref[...])
        for j in range(0 if fm else nk):
            h = h + _dot(a_ref[j], w_ref[j * kc:(j + 1) * kc, :])
        if mode == "norm":
            o1_ref[...] = h
            o2_ref[...] = (h * _rstd(h) * e_ref[...]).astype(BF16)
        else:
            err = h - e_ref[...]
            o1_ref[...] = (err * (1.0 / D)).astype(o1_ref.dtype)
            o2_ref[...] = jnp.full(o2_ref.shape, jnp.sum(err * err), F32)

    row = pl.BlockSpec((tm, D), lambda i: (i, 0))
    if mode == "norm":
        e_spec, o2_spec, o2_shape = pl.BlockSpec((1, D), lambda i: (0, 0)), row, S((t, D), BF16)
    else:
        e_spec, o2_spec, o2_shape = row, pl.BlockSpec((None, 8, 128), lambda i: (i, 0, 0)), S((ni, 8, 128), F32)
    return _run(
        body, [a_s, w, resid, extra], hook, grid=(ni,), name=name, semantics=("parallel",),
        in_specs=[pl.BlockSpec((kc, tm), lambda i: (0, i)) if fm else pl.BlockSpec((nk, tm, kc), lambda i: (0, i, 0)),
                  _resident(w.shape), row, e_spec],
        out_specs=[row, o2_spec], out_shape=[S((t, D), F32 if mode == "norm" else DH), o2_shape])


def _relbias_fwd(rel_bias_t, bucket_row, name):
    nb = bucket_row.shape[1]

    def body(rb_ref, bk_ref, o_ref):
        onehot = (lax.broadcasted_iota(jnp.int32, (REL_BUCKETS, nb), 0) == bk_ref[...]).astype(F32)
        o_ref[...] = jnp.dot(rb_ref[...], onehot, precision=lax.Precision.HIGHEST, preferred_element_type=F32)

    return pl.pallas_call(body, out_shape=S((NH, nb), F32), name=name)(rel_bias_t, bucket_row)


def _relbias_bwd(dbias, bucket_row, name):
    nb = bucket_row.shape[1]

    def body(db_ref, bk_ref, o_ref):
        onehot = (lax.broadcasted_iota(jnp.int32, (REL_BUCKETS, nb), 0) == bk_ref[...]).astype(F32)
        o_ref[...] = lax.dot_general(db_ref[...], onehot, (((1,), (1,)), ((), ())),
                                     precision=lax.Precision.HIGHEST, preferred_element_type=F32)

    return pl.pallas_call(body, out_shape=S((NH, REL_BUCKETS), F32), name=name)(dbias, bucket_row)


QKV = D + 2 * NKV * HD
KV0 = D


def _rstd_rows(x):
    return lax.rsqrt(jnp.mean(x * x, axis=0, keepdims=True) + EPS)


def _attn_valid(n):
    kj = lax.broadcasted_iota(jnp.int32, (2 * CHUNK, CHUNK), 0)
    qi = lax.broadcasted_iota(jnp.int32, (2 * CHUNK, CHUNK), 1)
    dist = qi + CHUNK - kj
    return (dist >= 0) & (dist < CHUNK) & ((n > 0) | (kj >= CHUNK))


def _attn_band(cur_ref, prev_ref, row):
    return jnp.concatenate([prev_ref[row - KV0:row - KV0 + HD, :], cur_ref[row:row + HD, :]], axis=1)


def _attn_probs(kn_tok, qn, bias, valid, sink):
    s = _dot(kn_tok, qn) * (HD ** -0.5) + bias
    s = jnp.where(valid, s, -jnp.inf)
    m = jnp.maximum(jnp.max(s, axis=0, keepdims=True), sink)
    p = jnp.exp(s - m)
    psink = jnp.exp(sink - m)
    inv = 1.0 / (jnp.sum(p, axis=0, keepdims=True) + psink)
    return p * inv, psink * inv


def _attn_fwd(qkv_t, qg, kg, sinks, bias, name, hook=None):
    t = qkv_t.shape[1]

    def body(cur_ref, prev_ref, qg_ref, kg_ref, sink_ref, bias_ref, o_ref):
        n = pl.program_id(0)
        valid = _attn_valid(n)
        ks = [_attn_band(cur_ref, prev_ref, KV0 + HD * h) for h in range(NKV)]
        kn_toks = [(k * _rstd_rows(k) * kg_ref[...]).astype(BF16).T for k in ks]
        vbs = [_attn_band(cur_ref, prev_ref, KV0 + HD * (NKV + h)).astype(BF16) for h in range(NKV)]
        qs = [cur_ref[HD * hq:HD * (hq + 1), :] for hq in range(NH)]
        qns = [(q * _rstd_rows(q) * qg_ref[...]).astype(BF16) for q in qs]
        ps = [_attn_probs(kn_toks[hq // KVG], qns[hq], bias_ref[hq], valid, sink_ref[hq])[0] for hq in range(NH)]
        for hq in range(NH):
            o_ref[HD * hq:HD * (hq + 1), :] = _dot(vbs[hq // KVG], ps[hq].astype(BF16)).astype(BF16)

    col = pl.BlockSpec((HD, 1), lambda n: (0, 0))
    return _run(
        body, [qkv_t, qkv_t, qg, kg, sinks, bias], hook, grid=(t // CHUNK,), name=name, semantics=("parallel",),
        in_specs=[pl.BlockSpec((QKV, CHUNK), lambda n: (0, n)),
                  pl.BlockSpec((QKV - KV0, CHUNK), lambda n: (KV0 // (QKV - KV0), jnp.maximum(n - 1, 0))),
                  col, col, pl.BlockSpec(memory_space=pltpu.SMEM), pl.BlockSpec((NH, 2 * CHUNK, CHUNK), lambda n: (0, 0, 0))],
        out_specs=pl.BlockSpec((D, CHUNK), lambda n: (0, n)), out_shape=S((D, t), BF16))


def _dx_rows(dh, w, kc, out_dtype, name, hook=None):
    t = dh.shape[0]
    nk = w.shape[0] // kc
    tm = _tm(t)

    def body(d_ref, w_ref, o_ref):
        dhb = d_ref[...].astype(BF16)
        for j in range(nk):
            o_ref[j] = _dot_nt(dhb, w_ref[j * kc:(j + 1) * kc, :]).astype(out_dtype)

    return _run(
        body, [dh, w], hook, grid=(t // tm,), name=name, semantics=("parallel",),
        in_specs=[pl.BlockSpec((tm, D), lambda i: (i, 0)), _resident(w.shape)],
        out_specs=pl.BlockSpec((nk, tm, kc), lambda i: (0, i, 0)), out_shape=S((nk, t, kc), out_dtype))


def _dx_rows_t(dh, w, name, hook=None):
    t = dh.shape[0]
    k = w.shape[0]
    tm = _tm(t)

    def body(d_ref, w_ref, o_ref):
        o_ref[...] = _dot_nt(w_ref[...], d_ref[...].astype(BF16)).astype(BF16)

    return _run(
        body, [dh, w], hook, grid=(t // tm,), name=name, semantics=("parallel",),
        in_specs=[pl.BlockSpec((tm, D), lambda i: (i, 0)), _resident(w.shape)],
        out_specs=pl.BlockSpec((k, tm), lambda i: (0, i)), out_shape=S((k, t), BF16))


def _ffn_bwd1(dh, c, wdown, name, hook=None):
    ns, t, n = c.shape
    nh = ns // 2
    tm = min(FFN_ROWS, t)
    ni = t // tm

    def body(d_ref, c_ref, wd_ref, dc_ref, dw_hbm, dwb_hbm, acc, stage):
        i = pl.program_id(0)

        @pl.when(i == 0)
        def _():
            acc[...] = jnp.zeros_like(acc)

        dhb = d_ref[...].astype(BF16)
        for j in range(nh):
            dact = _dot_nt(dhb, wd_ref[j * n:(j + 1) * n, :])
            cg = c_ref[j].astype(F32)
            cv = c_ref[nh + j].astype(F32)
            sg = _sigmoid(cg)
            gs = cg * sg
            acc[j * n:(j + 1) * n, :] += _dot_tn((gs * cv).astype(BF16), dhb)
            dc_ref[j] = (dact * cv * (sg + gs * (1.0 - sg))).astype(BF16)
            dc_ref[nh + j] = (dact * gs).astype(BF16)

        @pl.when(i == ni - 1)
        def _():
            pltpu.sync_copy(acc, dw_hbm)
            for j in range(nh):
                stage[...] = acc[j * n:(j + 1) * n, :].astype(BF16)
                pltpu.sync_copy(stage, dwb_hbm.at[pl.ds(j * n, n), :])

    slab = pl.BlockSpec((ns, tm, n), lambda i: (0, i, 0))
    return _run(
        body, [dh, c, wdown], hook, grid=(ni,), name=name, semantics=("arbitrary",),
        in_specs=[pl.BlockSpec((tm, D), lambda i: (i, 0)), slab, _resident(wdown.shape)],
        out_specs=[slab, ANY, ANY], out_shape=[S((ns, t, n), BF16), S(wdown.shape, F32), S(wdown.shape, BF16)],
        scratch_shapes=[pltpu.VMEM(wdown.shape, F32), pltpu.VMEM((n, D), BF16)])


def _ffn_bwd2(dc, a, wup, cw, h, gain, dh_in, name, hook=None):
    ns, t, n = dc.shape
    tm = min(FFN_ROWS, t)
    ni = t // tm

    def body(dc_ref, a_ref, wu_ref, cw_ref, h_ref, g_ref, di_ref, da_ref, o_ref, dg_ref, dcw_ref, dcb_ref, carry, keep):
        i = pl.program_id(0)

        @pl.when(i == 0)
        def _():
            carry[...] = jnp.zeros_like(carry)
            dg_ref[...] = jnp.zeros_like(dg_ref)
            dcw_ref[...] = jnp.zeros_like(dcw_ref)
            dcb_ref[...] = jnp.zeros_like(dcb_ref)

        rsum = lambda v: jnp.sum(v, axis=0, keepdims=True)
        acc = jnp.zeros((tm, D), F32)
        for s in range(ns):
            x = dc_ref[s].astype(F32)
            ext = jnp.concatenate([x, carry[s]], axis=0)
            keep[0] = ext[1:1 + tm]
            keep[1] = ext[2:2 + tm]
            x1, x2 = keep[0], keep[1]
            cwv = cw_ref[s]
            da = (cwv[2:3] * x + cwv[1:2] * x1 + cwv[0:1] * x2).astype(BF16)
            carry[s] = x[:HALO]
            da_ref[s] = da
            acc = acc + _dot_nt(da, wu_ref[s])
            av = a_ref[s].astype(F32)
            dcw_ref[s] += jnp.concatenate([rsum(x2 * av), rsum(x1 * av), rsum(x * av)], axis=0)
            dcb_ref[s] += rsum(x)
        hv = h_ref[...]
        r = _rstd(hv)
        gg = acc * g_ref[...]
        dh_new = di_ref[...].astype(F32) + r * gg - hv * (r * r * r * jnp.mean(gg * hv, axis=-1, keepdims=True))
        o_ref[...] = dh_new.astype(o_ref.dtype)
        dg_ref[...] += jnp.sum(acc * hv * r, axis=0, keepdims=True)

    slab = pl.BlockSpec((ns, tm, n), lambda i: (0, ni - 1 - i, 0))
    row = pl.BlockSpec((tm, D), lambda i: (ni - 1 - i, 0))
    vec = pl.BlockSpec((1, D), lambda i: (0, 0))
    whole = lambda shape: pl.BlockSpec(shape, lambda i: (0,) * len(shape))
    return _run(
        body, [dc, a, wup, cw, h, gain, dh_in], hook, grid=(ni,), name=name, semantics=("arbitrary",),
        in_specs=[slab, slab, _resident(wup.shape), _resident(cw.shape), row, vec, row],
        out_specs=[slab, row, vec, whole((ns, 3, n)), whole((ns, 1, n))],
        out_shape=[S((ns, t, n), BF16), S((t, D), DH), S((1, D), F32), S((ns, 3, n), F32), S((ns, 1, n), F32)],
        scratch_shapes=[pltpu.VMEM((ns, HALO, n), F32), pltpu.VMEM((2, tm, n), F32)])


def _dw_slot(hn, dy_s, name, hook=None):
    t, k = hn.shape
    ns, _, n = dy_s.shape
    tm = _tm(t)

    def body(a_ref, b_ref, o_ref, ob_ref, at_ref):
        @pl.when(pl.program_id(0) == 0)
        def _():
            for i in range(t // tm):
                at_ref[:, i * tm:(i + 1) * tm] = a_ref[i * tm:(i + 1) * tm, :].T

        acc = _dot(at_ref[...], b_ref[...])
        o_ref[...] = acc
        ob_ref[...] = acc.astype(BF16)

    ospec = pl.BlockSpec((None, k, n), lambda j: (j, 0, 0))
    return _run(
        body, [hn, dy_s], hook, grid=(ns,), name=name, semantics=("arbitrary",),
        in_specs=[_resident(hn.shape), pl.BlockSpec((None, t, n), lambda j: (j, 0, 0))],
        out_specs=[ospec, ospec], out_shape=[S((ns, k, n), F32), S((ns, k, n), BF16)],
        scratch_shapes=[pltpu.VMEM((k, t), BF16)])


def _dw_rows(a_s, dh, name, hook=None, fm=False):
    nk, t, kc = (1, a_s.shape[1], a_s.shape[0]) if fm else a_s.shape
    tm = _tm(t)
    ni = t // tm

    def body(a_ref, d_ref, o_ref, ob_ref):
        i = pl.program_id(0)
        dhb = d_ref[...].astype(BF16)

        @pl.when(i == 0)
        def _():
            o_ref[...] = jnp.zeros_like(o_ref)

        if fm:
            o_ref[...] += _dot(a_ref[...], dhb)
        for j in range(0 if fm else nk):
            o_ref[j * kc:(j + 1) * kc, :] += _dot_tn(a_ref[j], dhb)

        @pl.when(i == ni - 1)
        def _():
            ob_ref[...] = o_ref[...].astype(BF16)

    ospec = pl.BlockSpec((nk * kc, D), lambda i: (0, 0))
    return _run(
        body, [a_s, dh], hook, grid=(ni,), name=name, semantics=("arbitrary",),
        in_specs=[pl.BlockSpec((kc, tm), lambda i: (0, i)) if fm else pl.BlockSpec((nk, tm, kc), lambda i: (0, i, 0)),
                  pl.BlockSpec((tm, D), lambda i: (i, 0))],
        out_specs=[ospec, ospec], out_shape=[S((nk * kc, D), F32), S((nk * kc, D), BF16)])


def _dx_slot_normbwd(dy_s, wg, h, gain, dh_in, name, hook=None, fm=False, out_dtype=F32):
    ns, t, n = (1, dy_s.shape[1], dy_s.shape[0]) if fm else dy_s.shape
    tm = _tm(t)

    def body(dy_ref, w_ref, h_ref, g_ref, di_ref, o_ref, dg_ref):
        i = pl.program_id(0)

        @pl.when(i == 0)
        def _():
            dg_ref[...] = jnp.zeros_like(dg_ref)

        g = _dot_tn(dy_ref[...], w_ref[...]) if fm else _dot_nt(dy_ref[0], w_ref[0])
        for s in range(1, ns):
            g = g + _dot_nt(dy_ref[s], w_ref[s])
        hv = h_ref[...]
        r = _rstd(hv)
        gg = g * g_ref[...]
        dh_new = di_ref[...].astype(F32) + r * gg - hv * (r * r * r * jnp.mean(gg * hv, axis=-1, keepdims=True))
        o_ref[...] = dh_new.astype(o_ref.dtype)
        dg_ref[...] += jnp.sum(g * hv * r, axis=0, keepdims=True)

    row = pl.BlockSpec((tm, D), lambda i: (i, 0))
    vec = pl.BlockSpec((1, D), lambda i: (0, 0))
    return _run(
        body, [dy_s, wg, h, gain, dh_in], hook, grid=(t // tm,), name=name, semantics=("arbitrary",),
        in_specs=[pl.BlockSpec((n, tm), lambda i: (0, i)) if fm else pl.BlockSpec((ns, tm, n), lambda i: (0, i, 0)),
                  _resident(wg.shape), row, vec, row],
        out_specs=[row, vec], out_shape=[S((t, D), out_dtype), S((1, D), F32)])


def _sgu_gate_bwd(a_s, dg_s, vgain, ws, bst, name, hook=None):
    t = a_s.shape[1]
    sw = a_s.shape[2]
    gps = sw // CHUNK

    def body(a_ref, dg_ref, vg_ref, ws_ref, b_ref, da_ref, dws_ref, dbt_ref, dvg_ref, dvn_ref):
        n = pl.program_id(0)

        @pl.when(n == 0)
        def _():
            dws_ref[...] = jnp.zeros_like(dws_ref)
            dbt_ref[...] = jnp.zeros_like(dbt_ref)
            dvg_ref[...] = jnp.zeros_like(dvg_ref)

        vpre = jnp.concatenate([a_ref[4 + s].astype(F32) for s in range(4)], axis=1)
        v, v_grad = _gelu_and_grad(vpre)
        r = _rstd(v)
        vhat = v * r
        vn = (vhat * vg_ref[...]).astype(BF16)
        tri = _tril_mask()
        lane = lax.broadcasted_iota(jnp.int32, (CHUNK, CHUNK), 1)
        dbt = jnp.zeros((CHUNK, CHUNK), F32)
        for g in range(SGU_G):
            w = jnp.where(tri, ws_ref[g], 0.0).astype(BF16)
            vng = vn[:, g * CHUNK:(g + 1) * CHUNK]
            sg = _dot(w, vng) + b_ref[:, g:g + 1]
            lo = (g % gps) * CHUNK
            u, u_grad = _gelu_and_grad(a_ref[g // gps, :, lo:lo + CHUNK].astype(F32))
            dgate = dg_ref[g // gps, :, lo:lo + CHUNK].astype(F32)
            da_ref[g // gps, :, lo:lo + CHUNK] = (dgate * sg * u_grad).astype(BF16)
            ds = dgate * u
            dsb = ds.astype(BF16)
            dvn_ref[:, g * CHUNK:(g + 1) * CHUNK] = _dot_tn(w, dsb)
            dws_ref[g] += jnp.where(tri, _dot_nt(dsb, vng), 0.0)
            dbt = dbt + jnp.where(lane == g, jnp.sum(ds, axis=-1, keepdims=True), 0.0)
        dbt_ref[...] += dbt
        dvn = dvn_ref[...]
        dvg_ref[...] += jnp.sum(dvn * vhat, axis=0, keepdims=True)
        gg = dvn * vg_ref[...]
        dv = r * gg - v * (r * r * r * jnp.mean(gg * v, axis=-1, keepdims=True))
        dav = (dv * v_grad).astype(BF16)
        for s in range(4):
            da_ref[4 + s] = dav[:, s * sw:(s + 1) * sw]

    return _run(
        body, [a_s, dg_s, vgain, ws, bst], hook, grid=(t // CHUNK,), name=name, semantics=("arbitrary",),
        in_specs=[pl.BlockSpec((8, CHUNK, sw), lambda n: (0, n, 0)), pl.BlockSpec((4, CHUNK, sw), lambda n: (0, n, 0)),
                  pl.BlockSpec((1, SGU_W), lambda n: (0, 0)), pl.BlockSpec((SGU_G, CHUNK, CHUNK), lambda n: (0, 0, 0)),
                  pl.BlockSpec((CHUNK, SGU_G), lambda n: (0, 0))],
        out_specs=[pl.BlockSpec((8, CHUNK, sw), lambda n: (0, n, 0)), pl.BlockSpec((SGU_G, CHUNK, CHUNK), lambda n: (0, 0, 0)),
                   pl.BlockSpec((CHUNK, CHUNK), lambda n: (0, 0)), pl.BlockSpec((1, SGU_W), lambda n: (0, 0))],
        out_shape=[S((8, t, sw), BF16), S((SGU_G, CHUNK, CHUNK), F32), S((CHUNK, CHUNK), F32), S((1, SGU_W), F32)],
        scratch_shapes=[pltpu.VMEM((CHUNK, SGU_W), F32)])


def _attn_bwd(qkv_t, do_t, qg, kg, sinks, bias, name, hook=None):
    t = qkv_t.shape[1]
    nb = t // CHUNK

    def body(cur_ref, prev_ref, do_ref, qg_ref, kg_ref, sink_ref, bias_ref,
             o_ref, dqg_out, dkg_out, dsk_out, dbias_ref, carry, dqg_ref, dkg_ref, dsk_ref):
        n = pl.program_id(0)

        @pl.when(n == 0)
        def _():
            carry[...] = jnp.zeros_like(carry)
            dqg_ref[...] = jnp.zeros_like(dqg_ref)
            dkg_ref[...] = jnp.zeros_like(dkg_ref)
            dsk_ref[...] = jnp.zeros_like(dsk_ref)
            dbias_ref[...] = jnp.zeros_like(dbias_ref)

        @pl.when(n < nb)
        def _():
            valid = _attn_valid(n)
            o_ref[0:KV0, :] = carry[0:KV0, :].astype(BF16)
            kvs, heads = range(NKV), range(NH)
            group = lambda h: range(KVG * h, KVG * (h + 1))
            ks = [_attn_band(cur_ref, prev_ref, KV0 + HD * h) for h in kvs]
            rks = [_rstd_rows(k) for k in ks]
            khats = [k * rk for k, rk in zip(ks, rks)]
            kns = [(khat * kg_ref[...]).astype(BF16) for khat in khats]
            kn_toks = [kn.T for kn in kns]
            vbs = [_attn_band(cur_ref, prev_ref, KV0 + HD * (NKV + h)).astype(BF16) for h in kvs]
            v_toks = [vb.T for vb in vbs]
            qs = [cur_ref[HD * hq:HD * (hq + 1), :] for hq in heads]
            rqs = [_rstd_rows(q) for q in qs]
            qhats = [q * rq for q, rq in zip(qs, rqs)]
            qns = [(qhat * qg_ref[...]).astype(BF16) for qhat in qhats]
            probs = [_attn_probs(kn_toks[hq // KVG], qns[hq], bias_ref[hq], valid, sink_ref[hq]) for hq in heads]
            dohs = [do_ref[HD * hq:HD * (hq + 1), :] for hq in heads]
            dps = [_dot(v_toks[hq // KVG], dohs[hq]) for hq in heads]
            dsums = [jnp.sum(p * dp, axis=0, keepdims=True) for (p, _), dp in zip(probs, dps)]
            dss = [p * (dp - dsum) for (p, _), dp, dsum in zip(probs, dps, dsums)]
            for hq in heads:
                dsk_ref[hq:hq + 1, :] -= probs[hq][1] * dsums[hq]
                dbias_ref[hq] += dss[hq]
            dvs = [sum(_dot_nt(dohs[hq], probs[hq][0].astype(BF16)) for hq in group(h)) for h in kvs]
            dscs = [(ds * (HD ** -0.5)).astype(BF16) for ds in dss]
            dqns = [_dot(kns[hq // KVG], dscs[hq]) for hq in heads]
            dkns = [sum(_dot_nt(qns[hq], dscs[hq]) for hq in group(h)) for h in kvs]
            dqg_ref[...] += sum(dqn * qhat for dqn, qhat in zip(dqns, qhats))
            for hq in heads:
                gq = dqns[hq] * qg_ref[...]
                carry[HD * hq:HD * (hq + 1), :] = rqs[hq] * gq - qs[hq] * (
                    rqs[hq] * rqs[hq] * rqs[hq] * jnp.mean(gq * qs[hq], axis=0, keepdims=True))
            dkg_ref[...] += sum(dkn * khat for dkn, khat in zip(dkns, khats))
            for h in kvs:
                krow, vrow = KV0 + HD * h, KV0 + HD * (NKV + h)
                gk = dkns[h] * kg_ref[...]
                dk = rks[h] * gk - ks[h] * (rks[h] * rks[h] * rks[h] * jnp.mean(gk * ks[h], axis=0, keepdims=True))
                o_ref[krow:krow + HD, :] = (carry[krow:krow + HD, :] + dk[:, :CHUNK]).astype(BF16)
                o_ref[vrow:vrow + HD, :] = (carry[vrow:vrow + HD, :] + dvs[h][:, :CHUNK]).astype(BF16)
                carry[krow:krow + HD, :] = dk[:, CHUNK:]
                carry[vrow:vrow + HD, :] = dvs[h][:, CHUNK:]

        @pl.when(n == nb)
        def _():
            o_ref[...] = carry[...].astype(BF16)
            dqg_out[...] = jnp.sum(dqg_ref[...], axis=1, keepdims=True)
            dkg_out[...] = jnp.sum(dkg_ref[...], axis=1, keepdims=True)
            dsk_out[...] = jnp.sum(dsk_ref[...], axis=1, keepdims=True)

    cur = lambda n: (0, jnp.minimum(n, nb - 1))
    col = pl.BlockSpec((HD, 1), lambda n: (0, 0))
    whole = lambda shape: pl.BlockSpec(shape, lambda n: (0,) * len(shape))
    return _run(
        body, [qkv_t, qkv_t, do_t, qg, kg, sinks, bias], hook, grid=(nb + 1,), name=name, semantics=("arbitrary",),
        in_specs=[pl.BlockSpec((QKV, CHUNK), cur),
                  pl.BlockSpec((QKV - KV0, CHUNK), lambda n: (KV0 // (QKV - KV0), jnp.clip(n - 1, 0, nb - 1))),
                  pl.BlockSpec((D, CHUNK), cur), col, col, pl.BlockSpec(memory_space=pltpu.SMEM), whole((NH, 2 * CHUNK, CHUNK))],
        out_specs=[pl.BlockSpec((QKV, CHUNK), lambda n: (0, jnp.maximum(n - 1, 0))), whole((HD, 1)), whole((HD, 1)),
                   whole((NH, 1)), whole((NH, 2 * CHUNK, CHUNK))],
        out_shape=[S((QKV, t), BF16), S((HD, 1), F32), S((HD, 1), F32), S((NH, 1), F32), S((NH, 2 * CHUNK, CHUNK), F32)],
        scratch_shapes=[pltpu.VMEM((QKV, CHUNK), F32), pltpu.VMEM((HD, CHUNK), F32), pltpu.VMEM((HD, 2 * CHUNK), F32),
                        pltpu.VMEM((NH, CHUNK), F32)])


class _Plain:
    def __init__(self, wg):
        self.full, self.grads = wg, {}

    def w(self, n):
        return self.full[n]

    def hook(self, host):
        return None

    def grad(self, n, pair):
        self.grads[n] = pair

    def small(self, g_rep):
        pass

    def sync(self, point):
        pass


def _local_step(x, target, rep, sch):
    bucket_row = jnp.asarray(_rel_tables().T.reshape(1, -1))
    bias = _relbias_fwd(rep["rel_bias"].T, bucket_row, "relbias_fwd").reshape(NH, 2 * CHUNK, CHUNK)
    bst = rep["sgu_b_s"][0].T
    ws = rep["sgu_w_s"][0]
    vgain = rep["sgu_v_gain"]
    qg, kg, sinks = rep["attn_q_gain"].reshape(HD, 1), rep["attn_k_gain"].reshape(HD, 1), rep["attn_sinks"][0]
    w_down = lambda l: sch.w("ffn_w_down%d" % l).reshape(D_FF, D)
    w_up = lambda l: sch.w("ffn_w_up%d" % l)
    cw = [sch.w("ffn_conv_w")[:, 3 * l:3 * l + 3] for l in range(2)]
    cb = [rep["ffn_conv_b"][l].reshape(8, 1, -1) for l in range(2)]
    mixg = [rep["mix_norm"][l:l + 1] for l in range(2)]
    ffng = [rep["ffn_norm"][l:l + 1] for l in range(2)]
    rows = lambda pair: tuple(g.reshape(N_DEV, -1, D) for g in pair)
    hk = sch.hook

    hn0 = _rmsnorm(x, mixg[0], "norm0")
    a0 = _mm_slot(hn0, sch.w("sgu_w_in"), BF16, "sgu_in", hk("sgu_in"))
    gated = _sgu_gate_fwd(a0, vgain, ws, bst, "sgu_gate", hk("sgu_gate"))
    h1, hn1 = _resid_mm(gated, sch.w("sgu_w_out").reshape(SGU_W, D), x, ffng[0], "norm", "sgu_out", hk("sgu_out"))
    sch.sync("before_ffn0")
    a_ff0, c_ff0, h2, hn2 = _ffn_fwd(hn1, h1, w_up(0), w_down(0), cw[0], cb[0], mixg[1], "norm", "ffn0_fwd", hk("ffn0_fwd"))
    qkv = _mm_t(hn2, sch.w("attn_w_qkv"), "qkv", hk("qkv"))
    o = _attn_fwd(qkv, qg, kg, sinks, bias, "attn", hk("attn"))
    h3, hn3 = _resid_mm(o, sch.w("attn_w_o").reshape(D, D), h2, ffng[1], "norm", "attn_out", hk("attn_out"), fm=True)
    a_ff1, c_ff1, dy, sq = _ffn_fwd(hn3, h3, w_up(1), w_down(1), cw[1], cb[1], target, "loss", "ffn1_fwd_loss", hk("ffn1_fwd_loss"))
    loss = (0.5 / D) * jnp.sum(sq[:, 0, 0])

    def ffn_bwd(dh, h_in, hn, a, c, l, tag):
        dc, g_down, g_down_b = _ffn_bwd1(dh, c, w_down(l), tag + "_bwd1", hk(tag + "_bwd1"))
        sch.grad("ffn_w_down%d" % l, rows((g_down, g_down_b)))
        da, dh_new, dgain, g_cw, g_cb = _ffn_bwd2(dc, a, w_up(l), cw[l], h_in, ffng[l], dh, tag + "_bwd2", hk(tag + "_bwd2"))
        sch.grad("ffn_w_up%d" % l, _dw_slot(hn, da, tag + "_dw_up", hk(tag + "_dw_up")))
        return dh_new, dgain, g_cw, g_cb.reshape(-1)

    dh, d_ffng1, g_cw1, g_cb1 = ffn_bwd(dy, h3, hn3, a_ff1, c_ff1, 1, "ffn1")
    do = _dx_rows_t(dh, sch.w("attn_w_o").reshape(D, D), "attn_do", hk("attn_do"))
    sch.grad("attn_w_o", rows(_dw_rows(o, dh, "dw_o", hk("dw_o"), fm=True)))
    dqkv, d_qg, d_kg, d_sk, d_bias = _attn_bwd(qkv, do, qg, kg, sinks, bias, "attn_bwd", hk("attn_bwd"))
    sch.grad("attn_w_qkv", tuple(g.reshape(N_DEV, -1, D) for g in _dw_rows(dqkv, hn2, "dw_qkv", hk("dw_qkv"), fm=True)))
    dh, d_mixg1 = _dx_slot_normbwd(dqkv, sch.w("attn_w_qkv").reshape(QKV, D), h2, mixg[1], dh, "dx_qkv", hk("dx_qkv"), fm=True,
                                   out_dtype=DH)
    d_relb = _relbias_bwd(d_bias.reshape(NH, -1), bucket_row, "relbias_bwd").T
    g_rep = {"attn_q_gain": d_qg.reshape(1, HD), "attn_k_gain": d_kg.reshape(1, HD), "attn_sinks": d_sk.reshape(1, NH),
             "rel_bias": d_relb}
    sch.small(g_rep)
    dh, d_ffng0, g_cw0, g_cb0 = ffn_bwd(dh, h1, hn1, a_ff0, c_ff0, 0, "ffn0")
    g_cw = jnp.concatenate([g_cw0, g_cw1], axis=1)
    sch.grad("ffn_conv_w", (g_cw, g_cw.astype(BF16)))
    g_ffn = {"ffn_norm": jnp.concatenate([d_ffng0, d_ffng1], axis=0), "ffn_conv_b": jnp.stack([g_cb0, g_cb1], axis=0)}
    sch.small(g_ffn)
    dgated = _dx_rows(dh, sch.w("sgu_w_out").reshape(SGU_W, D), SGU_W // 4, BF16, "sgu_dgated", hk("sgu_dgated"))
    sch.grad("sgu_w_out", rows(_dw_rows(gated, dh, "dw_sgu_out", hk("dw_sgu_out"))))
    da0, d_ws, d_bst, d_vgain = _sgu_gate_bwd(a0, dgated, vgain, ws, bst, "sgu_gate_bwd", hk("sgu_gate_bwd"))
    g_sgu = {"sgu_v_gain": d_vgain, "sgu_w_s": d_ws[None], "sgu_b_s": d_bst[:, :SGU_G].T[None]}
    sch.small(g_sgu)
    sch.grad("sgu_w_in", _dw_slot(hn0, da0, "dw_sgu_in", hk("dw_sgu_in")))
    sch.sync("after_dw")
    grad_x, d_mixg0 = _dx_slot_normbwd(da0, sch.w("sgu_w_in"), x, mixg[0], dh, "dx_sgu_in", hk("dx_sgu_in"))
    g_mix = {"mix_norm": jnp.concatenate([d_mixg0, d_mixg1], axis=0)}
    sch.small(g_mix)
    for g in (g_ffn, g_sgu, g_mix):
        g_rep.update(g)
    return loss, grad_x, g_rep


def _allgather(xs, name):
    nt = len(xs)

    def body(*refs):
        x_refs, o_refs = refs[:nt], refs[nt:2 * nt]
        send_sems, recv_sems, local_sems = refs[2 * nt:]
        x, y, c, chips = _place()
        me, sibling = (x, y, c), (x, y, 1 - c)

        def copy(t, k, block, to, src=None):
            px, py, pc = block
            dst = o_refs[t].at[4 * px + 2 * py + pc]
            return pltpu.make_async_remote_copy(
                src_ref=dst if src is None else src, dst_ref=dst, send_sem=send_sems.at[t, k], recv_sem=recv_sems.at[t, k],
                device_id=to, device_id_type=MESH)

        mine = [pltpu.make_async_copy(x_refs[t], o_refs[t].at[4 * x + 2 * y + c], local_sems.at[t]) for t in range(nt)]
        for cp in mine:
            cp.start()
        first = []
        for t in range(nt):
            first.append(copy(t, 0, me, sibling, src=x_refs[t]))
            first += [copy(t, 1 + j, me, (*chip, c), src=x_refs[t]) for j, chip in enumerate(chips)]
        for cp in first:
            cp.start()
        passed = []
        for j, chip in enumerate(chips):
            for t in range(nt):
                copy(t, 1 + j, (*chip, c), me).wait_recv()
                fwd = copy(t, 4 + j, (*chip, c), sibling)
                fwd.start()
                passed.append(fwd)
        for t in range(nt):
            copy(t, 0, sibling, me).wait_recv()
            for j, chip in enumerate(chips):
                copy(t, 4 + j, (*chip, 1 - c), me).wait_recv()
        for cp in first + passed:
            cp.wait_send()
        for cp in mine:
            cp.wait()

    return pl.pallas_call(
        body, name=name, in_specs=[ANY] * nt, out_specs=[ANY] * nt,
        out_shape=[S((N_DEV,) + a.shape, a.dtype) for a in xs],
        scratch_shapes=[pltpu.SemaphoreType.DMA((nt, 7)), pltpu.SemaphoreType.DMA((nt, 7)), pltpu.SemaphoreType.DMA((nt,))],
        compiler_params=pltpu.CompilerParams(has_side_effects=True))(*xs)


def _exchange(hook, name):
    comm = hook()
    ci, co = len(comm.inputs), len(comm.out_shapes)

    def body(*refs):
        cins, couts = refs[:ci], refs[ci:ci + co]
        send, recv = refs[-2:]
        comm.start(cins, couts, send, recv)
        comm.finish(cins, couts, send, recv)

    res = pl.pallas_call(
        body, name=name, in_specs=[ANY] * ci, out_specs=[ANY] * co, out_shape=comm.out_shapes,
        scratch_shapes=[pltpu.SemaphoreType.DMA((comm.n_sems,)), pltpu.SemaphoreType.DMA((comm.n_sems,))],
        input_output_aliases=dict(comm.aliases),
        compiler_params=pltpu.CompilerParams(has_side_effects=True))(*comm.inputs)
    hook(res)


def _row_tile(r):
    tr = r if r <= ROW_TILE or r % ROW_TILE else ROW_TILE
    assert r % tr == 0
    return tr


def _rs_partial(g32, sib, place, name):
    _, r, cdim = g32.shape
    tr = _row_tile(r)

    def body(place_ref, g_ref, s_ref, p_ref, own_ref):
        k = pl.program_id(1)
        tot = g_ref[...] + s_ref[...].astype(F32)
        p_ref[...] = tot.astype(BF16)

        @pl.when(k == place_ref[1])
        def _():
            own_ref[...] = tot

    grid_spec = pltpu.PrefetchScalarGridSpec(
        num_scalar_prefetch=1, grid=(r // tr, 4),
        in_specs=[pl.BlockSpec((None, None, tr, cdim), lambda i, k, pr: (k, pr[0], i, 0)),
                  pl.BlockSpec((None, tr, cdim), lambda i, k, pr: (k, i, 0))],
        out_specs=[pl.BlockSpec((None, tr, cdim), lambda i, k, pr: (k, i, 0)), pl.BlockSpec((tr, cdim), lambda i, k, pr: (i, 0))])
    return pl.pallas_call(
        body, grid_spec=grid_spec, name=name,
        out_shape=[S((4, r, cdim), BF16), S((r, cdim), F32)],
        compiler_params=_cp("parallel", "arbitrary"))(place, g32.reshape(4, 2, r, cdim), sib)


def _adamw_math(w, g, m, v):
    m = ADAM_B1 * m + (1.0 - ADAM_B1) * g
    v = ADAM_B2 * v + (1.0 - ADAM_B2) * (g * g)
    m_hat = m / (1.0 - ADAM_B1 ** ADAM_STEP)
    v_hat = v / (1.0 - ADAM_B2 ** ADAM_STEP)
    delta = -ADAM_LR * (m_hat / (jnp.sqrt(v_hat) + ADAM_EPS) + ADAM_WD * w)
    return delta, m, v


def _adamw_shard(owns, recvs, w, m, v, name, flipped=False):
    nl = w.shape[0]
    r, cdim = owns[0].shape
    tr = _row_tile(r)
    nr = r // tr

    def body(*refs):
        own_refs, recv_refs = refs[:nl], refs[nl:2 * nl]
        w_ref, m_ref, v_ref, g_out, d_out, m_out, v_out = refs[2 * nl:]
        layer = pl.program_id(0)
        g = None
        for l in range(nl):
            gl = own_refs[l][...] + recv_refs[l][0].astype(F32) + recv_refs[l][1].astype(F32) + recv_refs[l][2].astype(F32)
            g = gl if g is None else jnp.where(layer == l, gl, g)
        if flipped:
            g = g.T
        g_out[...] = g
        d_out[...], m_out[...], v_out[...] = _adamw_math(w_ref[...], g, m_ref[...], v_ref[...])

    park = lambda l: (lambda layer, i: (jnp.where(layer == l, i, jnp.where(layer < l, 0, nr - 1)), 0))
    park3 = lambda l: (lambda layer, i: (0, jnp.where(layer == l, i, jnp.where(layer < l, 0, nr - 1)), 0))
    if flipped:
        row = pl.BlockSpec((None, cdim, tr), lambda layer, i: (layer, 0, i))
    else:
        row = pl.BlockSpec((None, tr, cdim), lambda layer, i: (layer, i, 0))
    return pl.pallas_call(
        body, grid=(nl, nr), name=name,
        in_specs=[pl.BlockSpec((tr, cdim), park(l)) for l in range(nl)] + [pl.BlockSpec((3, tr, cdim), park3(l)) for l in range(nl)]
        + [row, row, row],
        out_specs=[row] * 4, out_shape=[S(w.shape, F32)] * 4,
        compiler_params=_cp("arbitrary", "arbitrary"))(*owns, *recvs, w, m, v)


def _adamw_small(galls, ws, ms, vs, name):
    n = len(galls)

    def body(*refs):
        g_refs, w_refs, m_refs, v_refs, outs = refs[:n], refs[n:2 * n], refs[2 * n:3 * n], refs[3 * n:4 * n], refs[4 * n:]
        for i in range(n):
            g = g_refs[i][0].astype(F32)
            for s in range(1, N_DEV):
                g = g + g_refs[i][s].astype(F32)
            outs[i][...] = g
            outs[n + i][...], outs[2 * n + i][...], outs[3 * n + i][...] = _adamw_math(w_refs[i][...], g, m_refs[i][...], v_refs[i][...])

    res = pl.pallas_call(body, out_shape=[S(a.shape, F32) for a in ws] * 4, name=name)(*galls, *ws, *ms, *vs)
    return [res[k * n:(k + 1) * n] for k in range(4)]


REPLICATED = ["mix_norm", "ffn_norm", "sgu_v_gain", "sgu_w_s", "sgu_b_s", "attn_q_gain", "attn_k_gain", "attn_sinks", "rel_bias",
              "ffn_conv_b"]
WEIGHTS = ["mix_norm", "ffn_norm", "sgu_w_in", "sgu_v_gain", "sgu_w_s", "sgu_b_s", "sgu_w_out", "attn_w_qkv", "attn_q_gain",
           "attn_k_gain", "attn_sinks", "attn_w_o", "rel_bias", "ffn_w_up", "ffn_conv_w", "ffn_conv_b", "ffn_w_down"]
SMALL = ["g_" + n for n in REPLICATED]
BF16_TRANSIT = {"sgu_w_s"}
SMALL_ATTN = ["g_attn_q_gain", "g_attn_k_gain", "g_attn_sinks", "g_rel_bias"]
SMALL_FFN = ["g_ffn_norm", "g_ffn_conv_b"]
SMALL_SGU = ["g_sgu_v_gain", "g_sgu_w_s", "g_sgu_b_s"]

GATHER_FIRST = ["sgu_w_in", "ffn_conv_w"]
UP0_SPLIT = 352
PLAN = {
    "sgu_in": [("ag1", "sgu_w_out"), ("ag1", "ffn_w_up0", (0, UP0_SPLIT))],
    "sgu_gate": [("ag2", "sgu_w_out"), ("ag1", "ffn_w_up0", (UP0_SPLIT, D))],
    "sgu_out": [("ag2", "ffn_w_up0"), ("ag1", "ffn_w_down0")],
    "before_ffn0": [("ag2", "ffn_w_down0")],
    "ffn0_fwd": [("agd", "attn_w_qkv"), ("ag1", "attn_w_o"), ("ag1", "ffn_w_up1")],
    "qkv": [("ag2", "attn_w_o"), ("ag2", "ffn_w_up1")],
    "attn": [("ag1", "ffn_w_down1")],
    "attn_out": [("ag2", "ffn_w_down1")],
    "ffn1_bwd2": [("rs1", "ffn_w_down1")],
    "ffn1_dw_up": [("rs2", "ffn_w_down1")],
    "attn_do": [("rs1", "ffn_w_up1")],
    "attn_bwd": [("rs2", "ffn_w_up1"), ("rs1", "attn_w_o")],
    "dw_qkv": [("rs2", "attn_w_o")],
    "dx_qkv": [("rs1", "attn_w_qkv")],
    "ffn0_bwd1": [("rs2", "attn_w_qkv")] + [("ag1", n) for n in SMALL_ATTN],
    "ffn0_bwd2": [("rs1", "ffn_w_down0")] + [("ag2", n) for n in SMALL_ATTN],
    "ffn0_dw_up": [("rs2", "ffn_w_down0")],
    "sgu_dgated": [("rs1", "ffn_w_up0")] + [("ag1", n) for n in SMALL_FFN],
    "dw_sgu_out": [("ag2", n) for n in SMALL_FFN],
    "sgu_gate_bwd": [("rs2", "ffn_w_up0"), ("rs1", "sgu_w_out")],
    "dw_sgu_in": [("rs2", "sgu_w_out")] + [("ag1", n) for n in SMALL_SGU],
    "after_dw": [("rs1", "sgu_w_in"), ("rs1", "ffn_conv_w")] + [("ag2", n) for n in SMALL_SGU],
    "dx_sgu_in": [("rs2", "sgu_w_in"), ("rs2", "ffn_conv_w")],
    "last": [("agd", "g_mix_norm")],
}


class _Overlap:
    def __init__(self, shard, place):
        self.shard, self.place = shard, place
        self.part, self.full = {}, {}
        self.grads, self.sib, self.own, self.recv = {}, {}, {}, {}

    def w(self, n):
        return self.full[n]

    def grad(self, n, pair):
        self.grads[n] = pair

    def small(self, g_rep):
        self.shard.update(("g_" + n, a.astype(BF16) if n in BF16_TRANSIT else a) for n, a in _views2d(g_rep).items())

    def sync(self, point):
        _exchange(self.hook(point), point)

    def chip_sums(self, n):
        sums, self.own[n] = _rs_partial(self.grads[n][0], self.sib.pop(n), self.place, "rs_partial_" + n)
        return sums

    def hook(self, host):
        ops = PLAN.get(host)
        if not ops:
            return None
        where = {"ag1": self.part, "ag2": self.full, "agd": self.full, "rs1": self.sib, "rs2": self.recv}
        idx = []

        def hook(results=None):
            if results is not None:
                for (kind, n, *_), i in zip(ops, idx):
                    where[kind][n] = results[i]
                return None
            comm = _Comm()
            for kind, n, *rows in ops:
                arr = {"ag1": lambda: self.shard[n], "agd": lambda: self.shard[n], "ag2": lambda: self.part.pop(n),
                       "rs1": lambda: self.grads[n][1], "rs2": lambda: self.chip_sums(n)}[kind]()
                idx.append(comm.add(kind, arr, *rows, into=self.part.pop(n) if rows and rows[0][0] else None))
            return comm

        return hook


TRANSPOSED = {"attn_w_qkv"}
PHYSICAL_T = {"ffn_w_up"}
SHARDED = {
    "sgu_w_in": ["sgu_w_in"], "sgu_w_out": ["sgu_w_out"], "attn_w_qkv": ["attn_w_qkv"], "attn_w_o": ["attn_w_o"],
    "ffn_w_up": ["ffn_w_up0", "ffn_w_up1"], "ffn_w_down": ["ffn_w_down0", "ffn_w_down1"], "ffn_conv_w": ["ffn_conv_w"],
}


def _send_views(w):
    out = {"ffn_conv_w": w["ffn_conv_w"].reshape(6, -1)}
    for name, parts in SHARDED.items():
        if name != "ffn_conv_w":
            out.update((p, (w[name][l].T if name in TRANSPOSED else w[name][l]).astype(BF16)) for l, p in enumerate(parts))
    return out


def _views2d(d):
    return {n: d[n].reshape(-1, d[n].shape[-1]) for n in REPLICATED if n in d}


def kernel(x, mix_norm, ffn_norm, sgu_w_in, sgu_v_gain, sgu_w_s, sgu_b_s, sgu_w_out, attn_w_qkv, attn_q_gain, attn_k_gain, attn_sinks, attn_w_o, rel_bias, ffn_w_up, ffn_conv_w, ffn_conv_b, ffn_w_down, loss_target, m_mix_norm, m_ffn_norm, m_sgu_w_in, m_sgu_v_gain, m_sgu_w_s, m_sgu_b_s, m_sgu_w_out, m_attn_w_qkv, m_attn_q_gain, m_attn_k_gain, m_attn_sinks, m_attn_w_o, m_rel_bias, m_ffn_w_up, m_ffn_conv_w, m_ffn_conv_b, m_ffn_w_down, v_mix_norm, v_ffn_norm, v_sgu_w_in, v_sgu_v_gain, v_sgu_w_s, v_sgu_b_s, v_sgu_w_out, v_attn_w_qkv, v_attn_q_gain, v_attn_k_gain, v_attn_sinks, v_attn_w_o, v_rel_bias, v_ffn_w_up, v_ffn_conv_w, v_ffn_conv_b, v_ffn_w_down):
    w = dict(zip(WEIGHTS, (mix_norm, ffn_norm, sgu_w_in, sgu_v_gain, sgu_w_s, sgu_b_s, sgu_w_out, attn_w_qkv, attn_q_gain, attn_k_gain,
                           attn_sinks, attn_w_o, rel_bias, ffn_w_up, ffn_conv_w, ffn_conv_b, ffn_w_down)))
    m = dict(zip(WEIGHTS, (m_mix_norm, m_ffn_norm, m_sgu_w_in, m_sgu_v_gain, m_sgu_w_s, m_sgu_b_s, m_sgu_w_out, m_attn_w_qkv, m_attn_q_gain,
                           m_attn_k_gain, m_attn_sinks, m_attn_w_o, m_rel_bias, m_ffn_w_up, m_ffn_conv_w, m_ffn_conv_b, m_ffn_w_down)))
    v = dict(zip(WEIGHTS, (v_mix_norm, v_ffn_norm, v_sgu_w_in, v_sgu_v_gain, v_sgu_w_s, v_sgu_b_s, v_sgu_w_out, v_attn_w_qkv, v_attn_q_gain,
                           v_attn_k_gain, v_attn_sinks, v_attn_w_o, v_rel_bias, v_ffn_w_up, v_ffn_conv_w, v_ffn_conv_b, v_ffn_w_down)))
    rep = {n: w[n] for n in REPLICATED}

    xi, yi, ci = lax.axis_index("x"), lax.axis_index("y"), lax.axis_index("c")
    place = jnp.stack([ci, 2 * xi + yi]).astype(jnp.int32)
    sch = _Overlap(_send_views(w), place)
    sch.full.update(zip(GATHER_FIRST, _allgather([sch.shard[n] for n in GATHER_FIRST], "gather_first")))

    loss, grad_x, g_rep = _local_step(x[0], loss_target[0], rep, sch)
    loss = lax.psum(loss, ("x", "y", "c"))
    sch.sync("last")

    out = [{}, {}, {}, {}]
    for name, parts in SHARDED.items():
        flip = (lambda a: jnp.swapaxes(a, -1, -2)) if name in TRANSPOSED | PHYSICAL_T else (lambda a: a)
        shape = flip(w[name]).shape
        as3d = lambda a: flip(a).reshape(len(parts), -1, shape[-1])
        res = _adamw_shard([sch.own[p] for p in parts], [sch.recv[p] for p in parts], as3d(w[name]), as3d(m[name]), as3d(v[name]),
                           "adamw_" + name, flipped=name in PHYSICAL_T)
        for o, r in zip(out, res):
            o[name] = flip(r.reshape(shape))
    small = _adamw_small([sch.full[n] for n in SMALL], *[list(_views2d(d).values()) for d in (rep, m, v)], "adamw_small")
    for o, res in zip(out, small):
        o.update((n, r.reshape(w[n].shape)) for n, r in zip(REPLICATED, res))

    return (loss, grad_x[None], *[out[0][n] for n in WEIGHTS], *[out[1][n] for n in WEIGHTS],
            *[out[2][n] for n in WEIGHTS], *[out[3][n] for n in WEIGHTS])
```

```python
import functools
import math

import numpy as np
import jax
import jax.numpy as jnp
from jax import lax
from jax.experimental import pallas as pl
from jax.experimental.pallas import tpu as pltpu

F32 = jnp.float32
BF16 = jnp.bfloat16
DH = jnp.bfloat16
S = jax.ShapeDtypeStruct

D = 1024
CHUNK = 128
SGU_W = 2048
SGU_G = 16
HD = 64
NH = 16
NKV = 4
KVG = 4
D_FF = 2816
REL_BUCKETS = 32
REL_MAX_DIST = 128
EPS = 1e-6
N_DEV = 8
MESH = pl.DeviceIdType.MESH

ADAM_LR = 0.001
ADAM_B1 = 0.9
ADAM_B2 = 0.999
ADAM_EPS = 1e-08
ADAM_WD = 0.01
ADAM_STEP = 10

ROW_TILE = 512
HALO = 8
FFN_ROWS = 256


def _tm(t):
    return min(ROW_TILE, t)


def _cp(*sem):
    return pltpu.CompilerParams(dimension_semantics=sem)


ANY = pl.BlockSpec(memory_space=pl.ANY)


def _place():
    x, y, c = lax.axis_index("x"), lax.axis_index("y"), lax.axis_index("c")
    return x, y, c, [(1 - x, y), (x, 1 - y), (1 - x, 1 - y)]


class _Comm:
    SEMS = {"ag1": 5, "ag2": 3, "rs1": 4, "rs2": 3, "agd": 8}

    def __init__(self):
        self.inputs, self.out_shapes, self.aliases, self.ops, self.n_sems = [], [], {}, [], 0

    def add(self, kind, arr, rows=None, into=None):
        lead = {"ag1": N_DEV, "agd": N_DEV, "ag2": None, "rs1": 4, "rs2": 3}[kind]
        shape = arr.shape if lead is None else (lead,) + arr.shape[(0 if kind in ("ag1", "agd") else 1):]
        if kind == "ag2":
            self.aliases[len(self.inputs)] = len(self.out_shapes)
        self.ops.append((kind, len(self.inputs), len(self.out_shapes), self.n_sems, rows))
        self.inputs.append(arr)
        if into is not None:
            self.aliases[len(self.inputs)] = len(self.out_shapes)
            self.inputs.append(into)
        self.out_shapes.append(S(shape, arr.dtype))
        self.n_sems += self.SEMS[kind]
        return len(self.out_shapes) - 1

    def _copies(self, ins, outs, send, recv):
        x, y, c, chips = _place()
        me, sibling = (x, y, c), (x, y, 1 - c)
        slot = lambda px, py, pc: 4 * px + 2 * py + pc
        sends, recvs, local = [], [], []

        def rc(src, dst, k, to):
            return lambda: pltpu.make_async_remote_copy(src_ref=src(), dst_ref=dst(), send_sem=send.at[k], recv_sem=recv.at[k],
                                                        device_id=to, device_id_type=MESH)

        for kind, ii, oi, b, rows in self.ops:
            src, dst = ins[ii], outs[oi]
            at = lambda ref, i: (lambda: ref.at[i])
            if kind == "ag1":
                part = slice(None) if rows is None else pl.ds(rows[0], rows[1] - rows[0])
                to = lambda i, d=dst, p=part: (lambda: d.at[i, p])
                whole, mine = (lambda s=src, p=part: s.at[p]), to(slot(*me))
                sends.append(rc(whole, mine, b, sibling))
                recvs.append(rc(whole, to(slot(x, y, 1 - c)), b, me))
                for j, chip in enumerate(chips):
                    sends.append(rc(whole, mine, b + 1 + j, (*chip, c)))
                    recvs.append(rc(whole, to(slot(*chip, c)), b + 1 + j, me))
                local.append(lambda s=whole, m=mine, k=b + 4: pltpu.make_async_copy(s(), m(), send.at[k]))
            elif kind == "ag2":
                for j, chip in enumerate(chips):
                    sends.append(rc(at(dst, slot(*chip, c)), at(dst, slot(*chip, c)), b + j, sibling))
                    recvs.append(rc(at(dst, slot(*chip, 1 - c)), at(dst, slot(*chip, 1 - c)), b + j, me))
            elif kind == "agd":
                whole, mine = (lambda s=src: s), at(dst, slot(*me))
                flip = lambda v, bit: 1 - v if bit else v
                for k in range(1, N_DEV):
                    peer = (flip(x, k >> 2), flip(y, (k >> 1) & 1), flip(c, k & 1))
                    sends.append(rc(whole, mine, b + k - 1, peer))
                    recvs.append(rc(whole, at(dst, slot(*peer)), b + k - 1, me))
                local.append(lambda s=src, m=mine, k=b + 7: pltpu.make_async_copy(s, m(), send.at[k]))
            elif kind == "rs1":
                for k in range(4):
                    sends.append(rc(at(src, 2 * k + (1 - c)), at(dst, k), b + k, sibling))
                    recvs.append(rc(at(src, 2 * k + c), at(dst, k), b + k, me))
            else:
                for j, (px, py) in enumerate(chips):
                    sends.append(rc(at(src, 2 * px + py), at(dst, j), b + j, (px, py, c)))
                    recvs.append(rc(at(src, 2 * px + py), at(dst, j), b + j, me))
        return sends, recvs, local

    def start(self, ins, outs, send, recv):
        sends, _, local = self._copies(ins, outs, send, recv)
        for make in local + sends:
            make().start()

    def finish(self, ins, outs, send, recv):
        sends, recvs, local = self._copies(ins, outs, send, recv)
        for make in recvs:
            make().wait_recv()
        for make in sends:
            make().wait_send()
        for make in local:
            make().wait()


def _run(body, args, hook, *, grid, in_specs, out_specs, out_shape, name, semantics, scratch_shapes=(), aliases=None):
    comm = hook() if hook is not None else None
    aliases = dict(aliases or {})
    if comm is None:
        return pl.pallas_call(body, grid=grid, in_specs=in_specs, out_specs=out_specs, out_shape=out_shape, name=name,
                              scratch_shapes=list(scratch_shapes), input_output_aliases=aliases,
                              compiler_params=_cp(*semantics))(*args)
    single = not isinstance(out_shape, (list, tuple))
    out_shapes = [out_shape] if single else list(out_shape)
    out_specs_l = [out_specs] if single else list(out_specs)
    n_in, n_out, n_scr, ci, co = len(args), len(out_shapes), len(scratch_shapes), len(comm.inputs), len(comm.out_shapes)

    def wrapped(*refs):
        ins, cins = refs[:n_in], refs[n_in:n_in + ci]
        outs, couts = refs[n_in + ci:n_in + ci + n_out], refs[n_in + ci + n_out:n_in + ci + n_out + co]
        scr = refs[n_in + ci + n_out + co:n_in + ci + n_out + co + n_scr]
        send, recv = refs[-2:]
        first = functools.reduce(lambda a, b: a & b, [pl.program_id(a) == 0 for a in range(len(grid))])
        last = functools.reduce(lambda a, b: a & b, [pl.program_id(a) == g - 1 for a, g in enumerate(grid)])

        @pl.when(first)
        def _():
            comm.start(cins, couts, send, recv)

        body(*ins, *outs, *scr)

        @pl.when(last)
        def _():
            comm.finish(cins, couts, send, recv)

    res = pl.pallas_call(
        wrapped, grid=grid, in_specs=list(in_specs) + [ANY] * ci, out_specs=out_specs_l + [ANY] * co,
        out_shape=out_shapes + comm.out_shapes, name=name,
        scratch_shapes=list(scratch_shapes) + [pltpu.SemaphoreType.DMA((comm.n_sems,)), pltpu.SemaphoreType.DMA((comm.n_sems,))],
        input_output_aliases={**aliases, **{n_in + k: n_out + v for k, v in comm.aliases.items()}},
        compiler_params=pltpu.CompilerParams(dimension_semantics=("arbitrary",) * len(grid), has_side_effects=True))(*args, *comm.inputs)
    hook(res[n_out:])
    return res[0] if single else list(res[:n_out])


def _dot(a, b):
    return jnp.dot(a, b, preferred_element_type=F32)


def _dot_nt(a, b):
    return lax.dot_general(a, b, (((1,), (1,)), ((), ())), preferred_element_type=F32)


def _dot_tn(a, b):
    return lax.dot_general(a, b, (((0,), (0,)), ((), ())), preferred_element_type=F32)


def _gelu(x):
    return 0.5 * x * (1.0 + lax.erf(x * (2.0 ** -0.5)))


def _gelu_and_grad(x):
    cdf = 0.5 * (1.0 + lax.erf(x * (2.0 ** -0.5)))
    return x * cdf, cdf + x * jnp.exp(-0.5 * x * x) * (1.0 / math.sqrt(2.0 * math.pi))


def _sigmoid(x):
    return 1.0 / (1.0 + jnp.exp(-x))


def _rstd(x):
    return lax.rsqrt(jnp.mean(x * x, axis=-1, keepdims=True) + EPS)


def _rel_tables():
    q = np.arange(CHUNK)[:, None] + CHUNK
    k = np.arange(2 * CHUNK)[None, :]
    dist = q - k
    n = np.maximum(dist, 0)
    max_exact = REL_BUCKETS // 2
    large = max_exact + (np.log(np.maximum(n, 1).astype(np.float32) / max_exact)
                         / math.log(REL_MAX_DIST / max_exact) * (REL_BUCKETS - max_exact)).astype(np.int32)
    large = np.minimum(large, REL_BUCKETS - 1)
    return np.where(n < max_exact, n, large).astype(np.int32)


def _rmsnorm(x, gain, name):
    t = x.shape[0]
    tm = _tm(t)

    def body(x_ref, g_ref, o_ref):
        xv = x_ref[...]
        o_ref[...] = (xv * _rstd(xv) * g_ref[...]).astype(BF16)

    return pl.pallas_call(
        body, grid=(t // tm,), name=name,
        in_specs=[pl.BlockSpec((tm, D), lambda i: (i, 0)), pl.BlockSpec((1, D), lambda i: (0, 0))],
        out_specs=pl.BlockSpec((tm, D), lambda i: (i, 0)),
        out_shape=S((t, D), BF16), compiler_params=_cp("parallel"))(x, gain)


def _resident(shape):
    zeros = (0,) * len(shape)
    return pl.BlockSpec(shape, lambda *_: zeros, pipeline_mode=pl.Buffered(1))


def _mm_slot(hn, wg, out_dtype, name, hook=None):
    t, k = hn.shape
    ns, _, n = wg.shape
    tm = _tm(t)

    def body(a_ref, w_ref, o_ref):
        a = a_ref[...]
        for s in range(ns):
            o_ref[s] = _dot(a, w_ref[s]).astype(out_dtype)

    return _run(
        body, [hn, wg], hook, grid=(t // tm,), name=name, semantics=("parallel",),
        in_specs=[pl.BlockSpec((tm, k), lambda i: (i, 0)), _resident(wg.shape)],
        out_specs=pl.BlockSpec((ns, tm, n), lambda i: (0, i, 0)), out_shape=S((ns, t, n), out_dtype))


def _mm_t(hn, wt, name, hook=None):
    t, k = hn.shape
    ns, n, _ = wt.shape
    tm = _tm(t)

    def body(a_ref, w_ref, o_ref):
        a = a_ref[...]
        for s in range(ns):
            o_ref[s * n:(s + 1) * n, :] = _dot_nt(w_ref[s], a)

    return _run(
        body, [hn, wt], hook, grid=(t // tm,), name=name, semantics=("parallel",),
        in_specs=[pl.BlockSpec((tm, k), lambda i: (i, 0)), _resident(wt.shape)],
        out_specs=pl.BlockSpec((ns * n, tm), lambda i: (0, i)), out_shape=S((ns * n, t), F32))


def _conv3(a, prev, cw, cb, tm):
    ext = jnp.concatenate([prev, a], axis=0)
    return cw[2:3] * a + cw[1:2] * ext[HALO - 1:HALO - 1 + tm] + cw[0:1] * ext[HALO - 2:HALO - 2 + tm] + cb


def _ffn_fwd(hn, h, wup, wdown, cw, cb, extra, mode, name, hook=None):
    t, k = hn.shape
    n = wup.shape[-1]
    nh = wup.shape[0] // 2
    tm = min(FFN_ROWS, t)
    ni = t // tm

    def body(a_ref, h_ref, wu_ref, wd_ref, cw_ref, cb_ref, e_ref, as_ref, cs_ref, o1_ref, o2_ref, carry):
        i = pl.program_id(0)

        @pl.when(i == 0)
        def _():
            carry[...] = jnp.zeros_like(carry)

        a = a_ref[...]
        acc = h_ref[...]
        nxt = (_dot(a, wu_ref[0]), _dot(a, wu_ref[nh]))
        for j in range(nh):
            ag, av = nxt
            if j + 1 < nh:
                nxt = (_dot(a, wu_ref[j + 1]), _dot(a, wu_ref[nh + j + 1]))
            as_ref[j] = ag.astype(BF16)
            as_ref[nh + j] = av.astype(BF16)
            cg = _conv3(ag, carry[j], cw_ref[j], cb_ref[j], tm)
            cv = _conv3(av, carry[nh + j], cw_ref[nh + j], cb_ref[nh + j], tm)
            carry[j] = ag[tm - HALO:]
            carry[nh + j] = av[tm - HALO:]
            cs_ref[j] = cg.astype(BF16)
            cs_ref[nh + j] = cv.astype(BF16)
            act = (cg * _sigmoid(cg) * cv).astype(BF16)
            acc = acc + _dot(act, wd_ref[j * n:(j + 1) * n, :])
        if mode == "norm":
            o1_ref[...] = acc
            o2_ref[...] = (acc * _rstd(acc) * e_ref[...]).astype(BF16)
        else:
            err = acc - e_ref[...]
            o1_ref[...] = (err * (1.0 / D)).astype(o1_ref.dtype)
            o2_ref[...] = jnp.full(o2_ref.shape, jnp.sum(err * err), F32)

    row = pl.BlockSpec((tm, D), lambda i: (i, 0))
    if mode == "norm":
        e_spec, o2_spec, o2_shape = pl.BlockSpec((1, D), lambda i: (0, 0)), row, S((t, D), BF16)
    else:
        e_spec, o2_spec, o2_shape = row, pl.BlockSpec((None, 8, 128), lambda i: (i, 0, 0)), S((ni, 8, 128), F32)
    aspec = pl.BlockSpec((2 * nh, tm, n), lambda i: (0, i, 0))
    return _run(
        body, [hn, h, wup, wdown, cw, cb, extra], hook, grid=(ni,), name=name, semantics=("arbitrary",),
        in_specs=[pl.BlockSpec((tm, k), lambda i: (i, 0)), row, _resident(wup.shape), _resident(wdown.shape),
                  _resident(cw.shape), _resident(cb.shape), e_spec],
        out_specs=[aspec, aspec, row, o2_spec],
        out_shape=[S((2 * nh, t, n), BF16), S((2 * nh, t, n), BF16), S((t, D), F32 if mode == "norm" else DH), o2_shape],
        scratch_shapes=[pltpu.VMEM((2 * nh, HALO, n), F32)])


def _tril_mask():
    r = lax.broadcasted_iota(jnp.int32, (CHUNK, CHUNK), 0)
    c = lax.broadcasted_iota(jnp.int32, (CHUNK, CHUNK), 1)
    return r >= c


def _sgu_gate_fwd(a_s, vgain, ws, bst, name, hook=None):
    t = a_s.shape[1]
    sw = a_s.shape[2]
    gps = sw // CHUNK

    def body(a_ref, vg_ref, ws_ref, b_ref, o_ref):
        v = _gelu(jnp.concatenate([a_ref[4 + s].astype(F32) for s in range(4)], axis=1))
        vn = (v * _rstd(v) * vg_ref[...]).astype(BF16)
        tri = _tril_mask()
        for g in range(SGU_G):
            w = jnp.where(tri, ws_ref[g], 0.0).astype(BF16)
            sg = _dot(w, vn[:, g * CHUNK:(g + 1) * CHUNK]) + b_ref[:, g:g + 1]
            lo = (g % gps) * CHUNK
            u = _gelu(a_ref[g // gps, :, lo:lo + CHUNK].astype(F32))
            o_ref[g // gps, :, lo:lo + CHUNK] = (u * sg).astype(BF16)

    return _run(
        body, [a_s, vgain, ws, bst], hook, grid=(t // CHUNK,), name=name, semantics=("parallel",),
        in_specs=[pl.BlockSpec((8, CHUNK, sw), lambda n: (0, n, 0)), pl.BlockSpec((1, SGU_W), lambda n: (0, 0)),
                  pl.BlockSpec((SGU_G, CHUNK, CHUNK), lambda n: (0, 0, 0)), pl.BlockSpec((CHUNK, SGU_G), lambda n: (0, 0))],
        out_specs=pl.BlockSpec((4, CHUNK, sw), lambda n: (0, n, 0)), out_shape=S((4, t, sw), BF16))


def _resid_mm(a_s, w, resid, extra, mode, name, hook=None, fm=False):
    nk, t, kc = (1, a_s.shape[1], a_s.shape[0]) if fm else a_s.shape
    tm = _tm(t)
    ni = t // tm

    def body(a_ref, w_ref, r_ref, e_ref, o1_ref, o2_ref):
        h = r_ref[...]
        if fm:
            h = h + _dot_tn(a_ref[...], w_ref[...])
        for j in range(0 if fm else nk):
            h = h + _dot(a_ref[j], w_ref[j * kc:(j + 1) * kc, :])
        if mode == "norm":
            o1_ref[...] = h
            o2_ref[...] = (h * _rstd(h) * e_ref[...]).astype(BF16)
        else:
            err = h - e_ref[...]
            o1_ref[...] = (err * (1.0 / D)).astype(o1_ref.dtype)
            o2_ref[...] = jnp.full(o2_ref.shape, jnp.sum(err * err), F32)

    row = pl.BlockSpec((tm, D), lambda i: (i, 0))
    if mode == "norm":
        e_spec, o2_spec, o2_shape = pl.BlockSpec((1, D), lambda i: (0, 0)), row, S((t, D), BF16)
    else:
        e_spec, o2_spec, o2_shape = row, pl.BlockSpec((None, 8, 128), lambda i: (i, 0, 0)), S((ni, 8, 128), F32)
    return _run(
        body, [a_s, w, resid, extra], hook, grid=(ni,), name=name, semantics=("parallel",),
        in_specs=[pl.BlockSpec((kc, tm), lambda i: (0, i)) if fm else pl.BlockSpec((nk, tm, kc), lambda i: (0, i, 0)),
                  _resident(w.shape), row, e_spec],
        out_specs=[row, o2_spec], out_shape=[S((t, D), F32 if mode == "norm" else DH), o2_shape])


def _relbias_fwd(rel_bias_t, bucket_row, name):
    nb = bucket_row.shape[1]

    def body(rb_ref, bk_ref, o_ref):
        onehot = (lax.broadcasted_iota(jnp.int32, (REL_BUCKETS, nb), 0) == bk_ref[...]).astype(F32)
        o_ref[...] = jnp.dot(rb_ref[...], onehot, precision=lax.Precision.HIGHEST, preferred_element_type=F32)

    return pl.pallas_call(body, out_shape=S((NH, nb), F32), name=name)(rel_bias_t, bucket_row)


def _relbias_bwd(dbias, bucket_row, name):
    nb = bucket_row.shape[1]

    def body(db_ref, bk_ref, o_ref):
        onehot = (lax.broadcasted_iota(jnp.int32, (REL_BUCKETS, nb), 0) == bk_ref[...]).astype(F32)
        o_ref[...] = lax.dot_general(db_ref[...], onehot, (((1,), (1,)), ((), ())),
                                     precision=lax.Precision.HIGHEST, preferred_element_type=F32)

    return pl.pallas_call(body, out_shape=S((NH, REL_BUCKETS), F32), name=name)(dbias, bucket_row)


QKV = D + 2 * NKV * HD
KV0 = D


def _rstd_rows(x):
    return lax.rsqrt(jnp.mean(x * x, axis=0, keepdims=True) + EPS)


def _attn_valid(n):
    kj = lax.broadcasted_iota(jnp.int32, (2 * CHUNK, CHUNK), 0)
    qi = lax.broadcasted_iota(jnp.int32, (2 * CHUNK, CHUNK), 1)
    dist = qi + CHUNK - kj
    return (dist >= 0) & (dist < CHUNK) & ((n > 0) | (kj >= CHUNK))


def _attn_band(cur_ref, prev_ref, row):
    return jnp.concatenate([prev_ref[row - KV0:row - KV0 + HD, :], cur_ref[row:row + HD, :]], axis=1)


def _attn_probs(kn_tok, qn, bias, valid, sink):
    s = _dot(kn_tok, qn) * (HD ** -0.5) + bias
    s = jnp.where(valid, s, -jnp.inf)
    m = jnp.maximum(jnp.max(s, axis=0, keepdims=True), sink)
    p = jnp.exp(s - m)
    psink = jnp.exp(sink - m)
    inv = 1.0 / (jnp.sum(p, axis=0, keepdims=True) + psink)
    return p * inv, psink * inv


def _attn_fwd(qkv_t, qg, kg, sinks, bias, name, hook=None):
    t = qkv_t.shape[1]

    def body(cur_ref, prev_ref, qg_ref, kg_ref, sink_ref, bias_ref, o_ref):
        n = pl.program_id(0)
        valid = _attn_valid(n)
        ks = [_attn_band(cur_ref, prev_ref, KV0 + HD * h) for h in range(NKV)]
        kn_toks = [(k * _rstd_rows(k) * kg_ref[...]).astype(BF16).T for k in ks]
        vbs = [_attn_band(cur_ref, prev_ref, KV0 + HD * (NKV + h)).astype(BF16) for h in range(NKV)]
        qs = [cur_ref[HD * hq:HD * (hq + 1), :] for hq in range(NH)]
        qns = [(q * _rstd_rows(q) * qg_ref[...]).astype(BF16) for q in qs]
        ps = [_attn_probs(kn_toks[hq // KVG], qns[hq], bias_ref[hq], valid, sink_ref[hq])[0] for hq in range(NH)]
        for hq in range(NH):
            o_ref[HD * hq:HD * (hq + 1), :] = _dot(vbs[hq // KVG], ps[hq].astype(BF16)).astype(BF16)

    col = pl.BlockSpec((HD, 1), lambda n: (0, 0))
    return _run(
        body, [qkv_t, qkv_t, qg, kg, sinks, bias], hook, grid=(t // CHUNK,), name=name, semantics=("parallel",),
        in_specs=[pl.BlockSpec((QKV, CHUNK), lambda n: (0, n)),
                  pl.BlockSpec((QKV - KV0, CHUNK), lambda n: (KV0 // (QKV - KV0), jnp.maximum(n - 1, 0))),
                  col, col, pl.BlockSpec(memory_space=pltpu.SMEM), pl.BlockSpec((NH, 2 * CHUNK, CHUNK), lambda n: (0, 0, 0))],
        out_specs=pl.BlockSpec((D, CHUNK), lambda n: (0, n)), out_shape=S((D, t), BF16))


def _dx_rows(dh, w, kc, out_dtype, name, hook=None):
    t = dh.shape[0]
    nk = w.shape[0] // kc
    tm = _tm(t)

    def body(d_ref, w_ref, o_ref):
        dhb = d_ref[...].astype(BF16)
        for j in range(nk):
            o_ref[j] = _dot_nt(dhb, w_ref[j * kc:(j + 1) * kc, :]).astype(out_dtype)

    return _run(
        body, [dh, w], hook, grid=(t // tm,), name=name, semantics=("parallel",),
        in_specs=[pl.BlockSpec((tm, D), lambda i: (i, 0)), _resident(w.shape)],
        out_specs=pl.BlockSpec((nk, tm, kc), lambda i: (0, i, 0)), out_shape=S((nk, t, kc), out_dtype))


def _dx_rows_t(dh, w, name, hook=None):
    t = dh.shape[0]
    k = w.shape[0]
    tm = _tm(t)

    def body(d_ref, w_ref, o_ref):
        o_ref[...] = _dot_nt(w_ref[...], d_ref[...].astype(BF16)).astype(BF16)

    return _run(
        body, [dh, w], hook, grid=(t // tm,), name=name, semantics=("parallel",),
        in_specs=[pl.BlockSpec((tm, D), lambda i: (i, 0)), _resident(w.shape)],
        out_specs=pl.BlockSpec((k, tm), lambda i: (0, i)), out_shape=S((k, t), BF16))


def _ffn_bwd1(dh, c, wdown, name, hook=None):
    ns, t, n = c.shape
    nh = ns // 2
    tm = min(FFN_ROWS, t)
    ni = t // tm

    def body(d_ref, c_ref, wd_ref, dc_ref, dw_hbm, dwb_hbm, acc, stage):
        i = pl.program_id(0)

        @pl.when(i == 0)
        def _():
            acc[...] = jnp.zeros_like(acc)

        dhb = d_ref[...].astype(BF16)
        for j in range(nh):
            dact = _dot_nt(dhb, wd_ref[j * n:(j + 1) * n, :])
            cg = c_ref[j].astype(F32)
            cv = c_ref[nh + j].astype(F32)
            sg = _sigmoid(cg)
            gs = cg * sg
            acc[j * n:(j + 1) * n, :] += _dot_tn((gs * cv).astype(BF16), dhb)
            dc_ref[j] = (dact * cv * (sg + gs * (1.0 - sg))).astype(BF16)
            dc_ref[nh + j] = (dact * gs).astype(BF16)

        @pl.when(i == ni - 1)
        def _():
            pltpu.sync_copy(acc, dw_hbm)
            for j in range(nh):
                stage[...] = acc[j * n:(j + 1) * n, :].astype(BF16)
                pltpu.sync_copy(stage, dwb_hbm.at[pl.ds(j * n, n), :])

    slab = pl.BlockSpec((ns, tm, n), lambda i: (0, i, 0))
    return _run(
        body, [dh, c, wdown], hook, grid=(ni,), name=name, semantics=("arbitrary",),
        in_specs=[pl.BlockSpec((tm, D), lambda i: (i, 0)), slab, _resident(wdown.shape)],
        out_specs=[slab, ANY, ANY], out_shape=[S((ns, t, n), BF16), S(wdown.shape, F32), S(wdown.shape, BF16)],
        scratch_shapes=[pltpu.VMEM(wdown.shape, F32), pltpu.VMEM((n, D), BF16)])


def _ffn_bwd2(dc, a, wup, cw, h, gain, dh_in, name, hook=None):
    ns, t, n = dc.shape
    tm = min(FFN_ROWS, t)
    ni = t // tm

    def body(dc_ref, a_ref, wu_ref, cw_ref, h_ref, g_ref, di_ref, da_ref, o_ref, dg_ref, dcw_ref, dcb_ref, carry, keep):
        i = pl.program_id(0)

        @pl.when(i == 0)
        def _():
            carry[...] = jnp.zeros_like(carry)
            dg_ref[...] = jnp.zeros_like(dg_ref)
            dcw_ref[...] = jnp.zeros_like(dcw_ref)
            dcb_ref[...] = jnp.zeros_like(dcb_ref)

        rsum = lambda v: jnp.sum(v, axis=0, keepdims=True)
        acc = jnp.zeros((tm, D), F32)
        for s in range(ns):
            x = dc_ref[s].astype(F32)
            ext = jnp.concatenate([x, carry[s]], axis=0)
            keep[0] = ext[1:1 + tm]
            keep[1] = ext[2:2 + tm]
            x1, x2 = keep[0], keep[1]
            cwv = cw_ref[s]
            da = (cwv[2:3] * x + cwv[1:2] * x1 + cwv[0:1] * x2).astype(BF16)
            carry[s] = x[:HALO]
            da_ref[s] = da
            acc = acc + _dot_nt(da, wu_ref[s])
            av = a_ref[s].astype(F32)
            dcw_ref[s] += jnp.concatenate([rsum(x2 * av), rsum(x1 * av), rsum(x * av)], axis=0)
            dcb_ref[s] += rsum(x)
        hv = h_ref[...]
        r = _rstd(hv)
        gg = acc * g_ref[...]
        dh_new = di_ref[...].astype(F32) + r * gg - hv * (r * r * r * jnp.mean(gg * hv, axis=-1, keepdims=True))
        o_ref[...] = dh_new.astype(o_ref.dtype)
        dg_ref[...] += jnp.sum(acc * hv * r, axis=0, keepdims=True)

    slab = pl.BlockSpec((ns, tm, n), lambda i: (0, ni - 1 - i, 0))
    row = pl.BlockSpec((tm, D), lambda i: (ni - 1 - i, 0))
    vec = pl.BlockSpec((1, D), lambda i: (0, 0))
    whole = lambda shape: pl.BlockSpec(shape, lambda i: (0,) * len(shape))
    return _run(
        body, [dc, a, wup, cw, h, gain, dh_in], hook, grid=(ni,), name=name, semantics=("arbitrary",),
        in_specs=[slab, slab, _resident(wup.shape), _resident(cw.shape), row, vec, row],
        out_specs=[slab, row, vec, whole((ns, 3, n)), whole((ns, 1, n))],
        out_shape=[S((ns, t, n), BF16), S((t, D), DH), S((1, D), F32), S((ns, 3, n), F32), S((ns, 1, n), F32)],
        scratch_shapes=[pltpu.VMEM((ns, HALO, n), F32), pltpu.VMEM((2, tm, n), F32)])


def _dw_slot(hn, dy_s, name, hook=None):
    t, k = hn.shape
    ns, _, n = dy_s.shape
    tm = _tm(t)

    def body(a_ref, b_ref, o_ref, ob_ref, at_ref):
        @pl.when(pl.program_id(0) == 0)
        def _():
            for i in range(t // tm):
                at_ref[:, i * tm:(i + 1) * tm] = a_ref[i * tm:(i + 1) * tm, :].T

        acc = _dot(at_ref[...], b_ref[...])
        o_ref[...] = acc
        ob_ref[...] = acc.astype(BF16)

    ospec = pl.BlockSpec((None, k, n), lambda j: (j, 0, 0))
    return _run(
        body, [hn, dy_s], hook, grid=(ns,), name=name, semantics=("arbitrary",),
        in_specs=[_resident(hn.shape), pl.BlockSpec((None, t, n), lambda j: (j, 0, 0))],
        out_specs=[ospec, ospec], out_shape=[S((ns, k, n), F32), S((ns, k, n), BF16)],
        scratch_shapes=[pltpu.VMEM((k, t), BF16)])


def _dw_rows(a_s, dh, name, hook=None, fm=False):
    nk, t, kc = (1, a_s.shape[1], a_s.shape[0]) if fm else a_s.shape
    tm = _tm(t)
    ni = t // tm

    def body(a_ref, d_ref, o_ref, ob_ref):
        i = pl.program_id(0)
        dhb = d_ref[...].astype(BF16)

        @pl.when(i == 0)
        def _():
            o_ref[...] = jnp.zeros_like(o_ref)

        if fm:
            o_ref[...] += _dot(a_ref[...], dhb)
        for j in range(0 if fm else nk):
            o_ref[j * kc:(j + 1) * kc, :] += _dot_tn(a_ref[j], dhb)

        @pl.when(i == ni - 1)
        def _():
            ob_ref[...] = o_ref[...].astype(BF16)

    ospec = pl.BlockSpec((nk * kc, D), lambda i: (0, 0))
    return _run(
        body, [a_s, dh], hook, grid=(ni,), name=name, semantics=("arbitrary",),
        in_specs=[pl.BlockSpec((kc, tm), lambda i: (0, i)) if fm else pl.BlockSpec((nk, tm, kc), lambda i: (0, i, 0)),
                  pl.BlockSpec((tm, D), lambda i: (i, 0))],
        out_specs=[ospec, ospec], out_shape=[S((nk * kc, D), F32), S((nk * kc, D), BF16)])


def _dx_slot_normbwd(dy_s, wg, h, gain, dh_in, name, hook=None, fm=False, out_dtype=F32):
    ns, t, n = (1, dy_s.shape[1], dy_s.shape[0]) if fm else dy_s.shape
    tm = _tm(t)

    def body(dy_ref, w_ref, h_ref, g_ref, di_ref, o_ref, dg_ref):
        i = pl.program_id(0)

        @pl.when(i == 0)
        def _():
            dg_ref[...] = jnp.zeros_like(dg_ref)

        g = _dot_tn(dy_ref[...], w_ref[...]) if fm else _dot_nt(dy_ref[0], w_ref[0])
        for s in range(1, ns):
            g = g + _dot_nt(dy_ref[s], w_ref[s])
        hv = h_ref[...]
        r = _rstd(hv)
        gg = g * g_ref[...]
        dh_new = di_ref[...].astype(F32) + r * gg - hv * (r * r * r * jnp.mean(gg * hv, axis=-1, keepdims=True))
        o_ref[...] = dh_new.astype(o_ref.dtype)
        dg_ref[...] += jnp.sum(g * hv * r, axis=0, keepdims=True)

    row = pl.BlockSpec((tm, D), lambda i: (i, 0))
    vec = pl.BlockSpec((1, D), lambda i: (0, 0))
    return _run(
        body, [dy_s, wg, h, gain, dh_in], hook, grid=(t // tm,), name=name, semantics=("arbitrary",),
        in_specs=[pl.BlockSpec((n, tm), lambda i: (0, i)) if fm else pl.BlockSpec((ns, tm, n), lambda i: (0, i, 0)),
                  _resident(wg.shape), row, vec, row],
        out_specs=[row, vec], out_shape=[S((t, D), out_dtype), S((1, D), F32)])


def _sgu_gate_bwd(a_s, dg_s, vgain, ws, bst, name, hook=None):
    t = a_s.shape[1]
    sw = a_s.shape[2]
    gps = sw // CHUNK

    def body(a_ref, dg_ref, vg_ref, ws_ref, b_ref, da_ref, dws_ref, dbt_ref, dvg_ref, dvn_ref):
        n = pl.program_id(0)

        @pl.when(n == 0)
        def _():
            dws_ref[...] = jnp.zeros_like(dws_ref)
            dbt_ref[...] = jnp.zeros_like(dbt_ref)
            dvg_ref[...] = jnp.zeros_like(dvg_ref)

        vpre = jnp.concatenate([a_ref[4 + s].astype(F32) for s in range(4)], axis=1)
        v, v_grad = _gelu_and_grad(vpre)
        r = _rstd(v)
        vhat = v * r
        vn = (vhat * vg_ref[...]).astype(BF16)
        tri = _tril_mask()
        lane = lax.broadcasted_iota(jnp.int32, (CHUNK, CHUNK), 1)
        dbt = jnp.zeros((CHUNK, CHUNK), F32)
        for g in range(SGU_G):
            w = jnp.where(tri, ws_ref[g], 0.0).astype(BF16)
            vng = vn[:, g * CHUNK:(g + 1) * CHUNK]
            sg = _dot(w, vng) + b_ref[:, g:g + 1]
            lo = (g % gps) * CHUNK
            u, u_grad = _gelu_and_grad(a_ref[g // gps, :, lo:lo + CHUNK].astype(F32))
            dgate = dg_ref[g // gps, :, lo:lo + CHUNK].astype(F32)
            da_ref[g // gps, :, lo:lo + CHUNK] = (dgate * sg * u_grad).astype(BF16)
            ds = dgate * u
            dsb = ds.astype(BF16)
            dvn_ref[:, g * CHUNK:(g + 1) * CHUNK] = _dot_tn(w, dsb)
            dws_ref[g] += jnp.where(tri, _dot_nt(dsb, vng), 0.0)
            dbt = dbt + jnp.where(lane == g, jnp.sum(ds, axis=-1, keepdims=True), 0.0)
        dbt_ref[...] += dbt
        dvn = dvn_ref[...]
        dvg_ref[...] += jnp.sum(dvn * vhat, axis=0, keepdims=True)
        gg = dvn * vg_ref[...]
        dv = r * gg - v * (r * r * r * jnp.mean(gg * v, axis=-1, keepdims=True))
        dav = (dv * v_grad).astype(BF16)
        for s in range(4):
            da_ref[4 + s] = dav[:, s * sw:(s + 1) * sw]

    return _run(
        body, [a_s, dg_s, vgain, ws, bst], hook, grid=(t // CHUNK,), name=name, semantics=("arbitrary",),
        in_specs=[pl.BlockSpec((8, CHUNK, sw), lambda n: (0, n, 0)), pl.BlockSpec((4, CHUNK, sw), lambda n: (0, n, 0)),
                  pl.BlockSpec((1, SGU_W), lambda n: (0, 0)), pl.BlockSpec((SGU_G, CHUNK, CHUNK), lambda n: (0, 0, 0)),
                  pl.BlockSpec((CHUNK, SGU_G), lambda n: (0, 0))],
        out_specs=[pl.BlockSpec((8, CHUNK, sw), lambda n: (0, n, 0)), pl.BlockSpec((SGU_G, CHUNK, CHUNK), lambda n: (0, 0, 0)),
                   pl.BlockSpec((CHUNK, CHUNK), lambda n: (0, 0)), pl.BlockSpec((1, SGU_W), lambda n: (0, 0))],
        out_shape=[S((8, t, sw), BF16), S((SGU_G, CHUNK, CHUNK), F32), S((CHUNK, CHUNK), F32), S((1, SGU_W), F32)],
        scratch_shapes=[pltpu.VMEM((CHUNK, SGU_W), F32)])


def _attn_bwd(qkv_t, do_t, qg, kg, sinks, bias, name, hook=None):
    t = qkv_t.shape[1]
    nb = t // CHUNK

    def body(cur_ref, prev_ref, do_ref, qg_ref, kg_ref, sink_ref, bias_ref,
             o_ref, dqg_out, dkg_out, dsk_out, dbias_ref, carry, dqg_ref, dkg_ref, dsk_ref):
        n = pl.program_id(0)

        @pl.when(n == 0)
        def _():
            carry[...] = jnp.zeros_like(carry)
            dqg_ref[...] = jnp.zeros_like(dqg_ref)
            dkg_ref[...] = jnp.zeros_like(dkg_ref)
            dsk_ref[...] = jnp.zeros_like(dsk_ref)
            dbias_ref[...] = jnp.zeros_like(dbias_ref)

        @pl.when(n < nb)
        def _():
            valid = _attn_valid(n)
            o_ref[0:KV0, :] = carry[0:KV0, :].astype(BF16)
            kvs, heads = range(NKV), range(NH)
            group = lambda h: range(KVG * h, KVG * (h + 1))
            ks = [_attn_band(cur_ref, prev_ref, KV0 + HD * h) for h in kvs]
            rks = [_rstd_rows(k) for k in ks]
            khats = [k * rk for k, rk in zip(ks, rks)]
            kns = [(khat * kg_ref[...]).astype(BF16) for khat in khats]
            kn_toks = [kn.T for kn in kns]
            vbs = [_attn_band(cur_ref, prev_ref, KV0 + HD * (NKV + h)).astype(BF16) for h in kvs]
            v_toks = [vb.T for vb in vbs]
            qs = [cur_ref[HD * hq:HD * (hq + 1), :] for hq in heads]
            rqs = [_rstd_rows(q) for q in qs]
            qhats = [q * rq for q, rq in zip(qs, rqs)]
            qns = [(qhat * qg_ref[...]).astype(BF16) for qhat in qhats]
            probs = [_attn_probs(kn_toks[hq // KVG], qns[hq], bias_ref[hq], valid, sink_ref[hq]) for hq in heads]
            dohs = [do_ref[HD * hq:HD * (hq + 1), :] for hq in heads]
            dps = [_dot(v_toks[hq // KVG], dohs[hq]) for hq in heads]
            dsums = [jnp.sum(p * dp, axis=0, keepdims=True) for (p, _), dp in zip(probs, dps)]
            dss = [p * (dp - dsum) for (p, _), dp, dsum in zip(probs, dps, dsums)]
            for hq in heads:
                dsk_ref[hq:hq + 1, :] -= probs[hq][1] * dsums[hq]
                dbias_ref[hq] += dss[hq]
            dvs = [sum(_dot_nt(dohs[hq], probs[hq][0].astype(BF16)) for hq in group(h)) for h in kvs]
            dscs = [(ds * (HD ** -0.5)).astype(BF16) for ds in dss]
            dqns = [_dot(kns[hq // KVG], dscs[hq]) for hq in heads]
            dkns = [sum(_dot_nt(qns[hq], dscs[hq]) for hq in group(h)) for h in kvs]
            dqg_ref[...] += sum(dqn * qhat for dqn, qhat in zip(dqns, qhats))
            for hq in heads:
                gq = dqns[hq] * qg_ref[...]
                carry[HD * hq:HD * (hq + 1), :] = rqs[hq] * gq - qs[hq] * (
                    rqs[hq] * rqs[hq] * rqs[hq] * jnp.mean(gq * qs[hq], axis=0, keepdims=True))
            dkg_ref[...] += sum(dkn * khat for dkn, khat in zip(dkns, khats))
            for h in kvs:
                krow, vrow = KV0 + HD * h, KV0 + HD * (NKV + h)
                gk = dkns[h] * kg_ref[...]
                dk = rks[h] * gk - ks[h] * (rks[h] * rks[h] * rks[h] * jnp.mean(gk * ks[h], axis=0, keepdims=True))
                o_ref[krow:krow + HD, :] = (carry[krow:krow + HD, :] + dk[:, :CHUNK]).astype(BF16)
                o_ref[vrow:vrow + HD, :] = (carry[vrow:vrow + HD, :] + dvs[h][:, :CHUNK]).astype(BF16)
                carry[krow:krow + HD, :] = dk[:, CHUNK:]
                carry[vrow:vrow + HD, :] = dvs[h][:, CHUNK:]

        @pl.when(n == nb)
        def _():
            o_ref[...] = carry[...].astype(BF16)
            dqg_out[...] = jnp.sum(dqg_ref[...], axis=1, keepdims=True)
            dkg_out[...] = jnp.sum(dkg_ref[...], axis=1, keepdims=True)
            dsk_out[...] = jnp.sum(dsk_ref[...], axis=1, keepdims=True)

    cur = lambda n: (0, jnp.minimum(n, nb - 1))
    col = pl.BlockSpec((HD, 1), lambda n: (0, 0))
    whole = lambda shape: pl.BlockSpec(shape, lambda n: (0,) * len(shape))
    return _run(
        body, [qkv_t, qkv_t, do_t, qg, kg, sinks, bias], hook, grid=(nb + 1,), name=name, semantics=("arbitrary",),
        in_specs=[pl.BlockSpec((QKV, CHUNK), cur),
                  pl.BlockSpec((QKV - KV0, CHUNK), lambda n: (KV0 // (QKV - KV0), jnp.clip(n - 1, 0, nb - 1))),
                  pl.BlockSpec((D, CHUNK), cur), col, col, pl.BlockSpec(memory_space=pltpu.SMEM), whole((NH, 2 * CHUNK, CHUNK))],
        out_specs=[pl.BlockSpec((QKV, CHUNK), lambda n: (0, jnp.maximum(n - 1, 0))), whole((HD, 1)), whole((HD, 1)),
                   whole((NH, 1)), whole((NH, 2 * CHUNK, CHUNK))],
        out_shape=[S((QKV, t), BF16), S((HD, 1), F32), S((HD, 1), F32), S((NH, 1), F32), S((NH, 2 * CHUNK, CHUNK), F32)],
        scratch_shapes=[pltpu.VMEM((QKV, CHUNK), F32), pltpu.VMEM((HD, CHUNK), F32), pltpu.VMEM((HD, 2 * CHUNK), F32),
                        pltpu.VMEM((NH, CHUNK), F32)])


class _Plain:
    def __init__(self, wg):
        self.full, self.grads = wg, {}

    def w(self, n):
        return self.full[n]

    def hook(self, host):
        return None

    def grad(self, n, pair):
        self.grads[n] = pair

    def small(self, g_rep):
        pass

    def sync(self, point):
        pass


def _local_step(x, target, rep, sch):
    bucket_row = jnp.asarray(_rel_tables().T.reshape(1, -1))
    bias = _relbias_fwd(rep["rel_bias"].T, bucket_row, "relbias_fwd").reshape(NH, 2 * CHUNK, CHUNK)
    bst = rep["sgu_b_s"][0].T
    ws = rep["sgu_w_s"][0]
    vgain = rep["sgu_v_gain"]
    qg, kg, sinks = rep["attn_q_gain"].reshape(HD, 1), rep["attn_k_gain"].reshape(HD, 1), rep["attn_sinks"][0]
    w_down = lambda l: sch.w("ffn_w_down%d" % l).reshape(D_FF, D)
    w_up = lambda l: sch.w("ffn_w_up%d" % l)
    cw = [sch.w("ffn_conv_w")[:, 3 * l:3 * l + 3] for l in range(2)]
    cb = [rep["ffn_conv_b"][l].reshape(8, 1, -1) for l in range(2)]
    mixg = [rep["mix_norm"][l:l + 1] for l in range(2)]
    ffng = [rep["ffn_norm"][l:l + 1] for l in range(2)]
    rows = lambda pair: tuple(g.reshape(N_DEV, -1, D) for g in pair)
    hk = sch.hook

    hn0 = _rmsnorm(x, mixg[0], "norm0")
    a0 = _mm_slot(hn0, sch.w("sgu_w_in"), BF16, "sgu_in", hk("sgu_in"))
    gated = _sgu_gate_fwd(a0, vgain, ws, bst, "sgu_gate", hk("sgu_gate"))
    h1, hn1 = _resid_mm(gated, sch.w("sgu_w_out").reshape(SGU_W, D), x, ffng[0], "norm", "sgu_out", hk("sgu_out"))
    sch.sync("before_ffn0")
    a_ff0, c_ff0, h2, hn2 = _ffn_fwd(hn1, h1, w_up(0), w_down(0), cw[0], cb[0], mixg[1], "norm", "ffn0_fwd", hk("ffn0_fwd"))
    qkv = _mm_t(hn2, sch.w("attn_w_qkv"), "qkv", hk("qkv"))
    o = _attn_fwd(qkv, qg, kg, sinks, bias, "attn", hk("attn"))
    h3, hn3 = _resid_mm(o, sch.w("attn_w_o").reshape(D, D), h2, ffng[1], "norm", "attn_out", hk("attn_out"), fm=True)
    a_ff1, c_ff1, dy, sq = _ffn_fwd(hn3, h3, w_up(1), w_down(1), cw[1], cb[1], target, "loss", "ffn1_fwd_loss", hk("ffn1_fwd_loss"))
    loss = (0.5 / D) * jnp.sum(sq[:, 0, 0])

    def ffn_bwd(dh, h_in, hn, a, c, l, tag):
        dc, g_down, g_down_b = _ffn_bwd1(dh, c, w_down(l), tag + "_bwd1", hk(tag + "_bwd1"))
        sch.grad("ffn_w_down%d" % l, rows((g_down, g_down_b)))
        da, dh_new, dgain, g_cw, g_cb = _ffn_bwd2(dc, a, w_up(l), cw[l], h_in, ffng[l], dh, tag + "_bwd2", hk(tag + "_bwd2"))
        sch.grad("ffn_w_up%d" % l, _dw_slot(hn, da, tag + "_dw_up", hk(tag + "_dw_up")))
        return dh_new, dgain, g_cw, g_cb.reshape(-1)

    dh, d_ffng1, g_cw1, g_cb1 = ffn_bwd(dy, h3, hn3, a_ff1, c_ff1, 1, "ffn1")
    do = _dx_rows_t(dh, sch.w("attn_w_o").reshape(D, D), "attn_do", hk("attn_do"))
    sch.grad("attn_w_o", rows(_dw_rows(o, dh, "dw_o", hk("dw_o"), fm=True)))
    dqkv, d_qg, d_kg, d_sk, d_bias = _attn_bwd(qkv, do, qg, kg, sinks, bias, "attn_bwd", hk("attn_bwd"))
    sch.grad("attn_w_qkv", tuple(g.reshape(N_DEV, -1, D) for g in _dw_rows(dqkv, hn2, "dw_qkv", hk("dw_qkv"), fm=True)))
    dh, d_mixg1 = _dx_slot_normbwd(dqkv, sch.w("attn_w_qkv").reshape(QKV, D), h2, mixg[1], dh, "dx_qkv", hk("dx_qkv"), fm=True,
                                   out_dtype=DH)
    d_relb = _relbias_bwd(d_bias.reshape(NH, -1), bucket_row, "relbias_bwd").T
    g_rep = {"attn_q_gain": d_qg.reshape(1, HD), "attn_k_gain": d_kg.reshape(1, HD), "attn_sinks": d_sk.reshape(1, NH),
             "rel_bias": d_relb}
    sch.small(g_rep)
    dh, d_ffng0, g_cw0, g_cb0 = ffn_bwd(dh, h1, hn1, a_ff0, c_ff0, 0, "ffn0")
    g_cw = jnp.concatenate([g_cw0, g_cw1], axis=1)
    sch.grad("ffn_conv_w", (g_cw, g_cw.astype(BF16)))
    g_ffn = {"ffn_norm": jnp.concatenate([d_ffng0, d_ffng1], axis=0), "ffn_conv_b": jnp.stack([g_cb0, g_cb1], axis=0)}
    sch.small(g_ffn)
    dgated = _dx_rows(dh, sch.w("sgu_w_out").reshape(SGU_W, D), SGU_W // 4, BF16, "sgu_dgated", hk("sgu_dgated"))
    sch.grad("sgu_w_out", rows(_dw_rows(gated, dh, "dw_sgu_out", hk("dw_sgu_out"))))
    da0, d_ws, d_bst, d_vgain = _sgu_gate_bwd(a0, dgated, vgain, ws, bst, "sgu_gate_bwd", hk("sgu_gate_bwd"))
    g_sgu = {"sgu_v_gain": d_vgain, "sgu_w_s": d_ws[None], "sgu_b_s": d_bst[:, :SGU_G].T[None]}
    sch.small(g_sgu)
    sch.grad("sgu_w_in", _dw_slot(hn0, da0, "dw_sgu_in", hk("dw_sgu_in")))
    sch.sync("after_dw")
    grad_x, d_mixg0 = _dx_slot_normbwd(da0, sch.w("sgu_w_in"), x, mixg[0], dh, "dx_sgu_in", hk("dx_sgu_in"))
    g_mix = {"mix_norm": jnp.concatenate([d_mixg0, d_mixg1], axis=0)}
    sch.small(g_mix)
    for g in (g_ffn, g_sgu, g_mix):
        g_rep.update(g)
    return loss, grad_x, g_rep


def _allgather(xs, name):
    nt = len(xs)

    def body(*refs):
        x_refs, o_refs = refs[:nt], refs[nt:2 * nt]
        send_sems, recv_sems, local_sems = refs[2 * nt:]
        x, y, c, chips = _place()
        me, sibling = (x, y, c), (x, y, 1 - c)

        def copy(t, k, block, to, src=None):
            px, py, pc = block
            dst = o_refs[t].at[4 * px + 2 * py + pc]
            return pltpu.make_async_remote_copy(
                src_ref=dst if src is None else src, dst_ref=dst, send_sem=send_sems.at[t, k], recv_sem=recv_sems.at[t, k],
                device_id=to, device_id_type=MESH)

        mine = [pltpu.make_async_copy(x_refs[t], o_refs[t].at[4 * x + 2 * y + c], local_sems.at[t]) for t in range(nt)]
        for cp in mine:
            cp.start()
        first = []
        for t in range(nt):
            first.append(copy(t, 0, me, sibling, src=x_refs[t]))
            first += [copy(t, 1 + j, me, (*chip, c), src=x_refs[t]) for j, chip in enumerate(chips)]
        for cp in first:
            cp.start()
        passed = []
        for j, chip in enumerate(chips):
            for t in range(nt):
                copy(t, 1 + j, (*chip, c), me).wait_recv()
                fwd = copy(t, 4 + j, (*chip, c), sibling)
                fwd.start()
                passed.append(fwd)
        for t in range(nt):
            copy(t, 0, sibling, me).wait_recv()
            for j, chip in enumerate(chips):
                copy(t, 4 + j, (*chip, 1 - c), me).wait_recv()
        for cp in first + passed:
            cp.wait_send()
        for cp in mine:
            cp.wait()

    return pl.pallas_call(
        body, name=name, in_specs=[ANY] * nt, out_specs=[ANY] * nt,
        out_shape=[S((N_DEV,) + a.shape, a.dtype) for a in xs],
        scratch_shapes=[pltpu.SemaphoreType.DMA((nt, 7)), pltpu.SemaphoreType.DMA((nt, 7)), pltpu.SemaphoreType.DMA((nt,))],
        compiler_params=pltpu.CompilerParams(has_side_effects=True))(*xs)


def _exchange(hook, name):
    comm = hook()
    ci, co = len(comm.inputs), len(comm.out_shapes)

    def body(*refs):
        cins, couts = refs[:ci], refs[ci:ci + co]
        send, recv = refs[-2:]
        comm.start(cins, couts, send, recv)
        comm.finish(cins, couts, send, recv)

    res = pl.pallas_call(
        body, name=name, in_specs=[ANY] * ci, out_specs=[ANY] * co, out_shape=comm.out_shapes,
        scratch_shapes=[pltpu.SemaphoreType.DMA((comm.n_sems,)), pltpu.SemaphoreType.DMA((comm.n_sems,))],
        input_output_aliases=dict(comm.aliases),
        compiler_params=pltpu.CompilerParams(has_side_effects=True))(*comm.inputs)
    hook(res)


def _row_tile(r):
    tr = r if r <= ROW_TILE or r % ROW_TILE else ROW_TILE
    assert r % tr == 0
    return tr


def _rs_partial(g32, sib, place, name):
    _, r, cdim = g32.shape
    tr = _row_tile(r)

    def body(place_ref, g_ref, s_ref, p_ref, own_ref):
        k = pl.program_id(1)
        tot = g_ref[...] + s_ref[...].astype(F32)
        p_ref[...] = tot.astype(BF16)

        @pl.when(k == place_ref[1])
        def _():
            own_ref[...] = tot

    grid_spec = pltpu.PrefetchScalarGridSpec(
        num_scalar_prefetch=1, grid=(r // tr, 4),
        in_specs=[pl.BlockSpec((None, None, tr, cdim), lambda i, k, pr: (k, pr[0], i, 0)),
                  pl.BlockSpec((None, tr, cdim), lambda i, k, pr: (k, i, 0))],
        out_specs=[pl.BlockSpec((None, tr, cdim), lambda i, k, pr: (k, i, 0)), pl.BlockSpec((tr, cdim), lambda i, k, pr: (i, 0))])
    return pl.pallas_call(
        body, grid_spec=grid_spec, name=name,
        out_shape=[S((4, r, cdim), BF16), S((r, cdim), F32)],
        compiler_params=_cp("parallel", "arbitrary"))(place, g32.reshape(4, 2, r, cdim), sib)


def _adamw_math(w, g, m, v):
    m = ADAM_B1 * m + (1.0 - ADAM_B1) * g
    v = ADAM_B2 * v + (1.0 - ADAM_B2) * (g * g)
    m_hat = m / (1.0 - ADAM_B1 ** ADAM_STEP)
    v_hat = v / (1.0 - ADAM_B2 ** ADAM_STEP)
    delta = -ADAM_LR * (m_hat / (jnp.sqrt(v_hat) + ADAM_EPS) + ADAM_WD * w)
    return delta, m, v


def _adamw_shard(owns, recvs, w, m, v, name, flipped=False):
    nl = w.shape[0]
    r, cdim = owns[0].shape
    tr = _row_tile(r)
    nr = r // tr

    def body(*refs):
        own_refs, recv_refs = refs[:nl], refs[nl:2 * nl]
        w_ref, m_ref, v_ref, g_out, d_out, m_out, v_out = refs[2 * nl:]
        layer = pl.program_id(0)
        g = None
        for l in range(nl):
            gl = own_refs[l][...] + recv_refs[l][0].astype(F32) + recv_refs[l][1].astype(F32) + recv_refs[l][2].astype(F32)
            g = gl if g is None else jnp.where(layer == l, gl, g)
        if flipped:
            g = g.T
        g_out[...] = g
        d_out[...], m_out[...], v_out[...] = _adamw_math(w_ref[...], g, m_ref[...], v_ref[...])

    park = lambda l: (lambda layer, i: (jnp.where(layer == l, i, jnp.where(layer < l, 0, nr - 1)), 0))
    park3 = lambda l: (lambda layer, i: (0, jnp.where(layer == l, i, jnp.where(layer < l, 0, nr - 1)), 0))
    if flipped:
        row = pl.BlockSpec((None, cdim, tr), lambda layer, i: (layer, 0, i))
    else:
        row = pl.BlockSpec((None, tr, cdim), lambda layer, i: (layer, i, 0))
    return pl.pallas_call(
        body, grid=(nl, nr), name=name,
        in_specs=[pl.BlockSpec((tr, cdim), park(l)) for l in range(nl)] + [pl.BlockSpec((3, tr, cdim), park3(l)) for l in range(nl)]
        + [row, row, row],
        out_specs=[row] * 4, out_shape=[S(w.shape, F32)] * 4,
        compiler_params=_cp("arbitrary", "arbitrary"))(*owns, *recvs, w, m, v)


def _adamw_small(galls, ws, ms, vs, name):
    n = len(galls)

    def body(*refs):
        g_refs, w_refs, m_refs, v_refs, outs = refs[:n], refs[n:2 * n], refs[2 * n:3 * n], refs[3 * n:4 * n], refs[4 * n:]
        for i in range(n):
            g = g_refs[i][0].astype(F32)
            for s in range(1, N_DEV):
                g = g + g_refs[i][s].astype(F32)
            outs[i][...] = g
            outs[n + i][...], outs[2 * n + i][...], outs[3 * n + i][...] = _adamw_math(w_refs[i][...], g, m_refs[i][...], v_refs[i][...])

    res = pl.pallas_call(body, out_shape=[S(a.shape, F32) for a in ws] * 4, name=name)(*galls, *ws, *ms, *vs)
    return [res[k * n:(k + 1) * n] for k in range(4)]


REPLICATED = ["mix_norm", "ffn_norm", "sgu_v_gain", "sgu_w_s", "sgu_b_s", "attn_q_gain", "attn_k_gain", "attn_sinks", "rel_bias",
              "ffn_conv_b"]
WEIGHTS = ["mix_norm", "ffn_norm", "sgu_w_in", "sgu_v_gain", "sgu_w_s", "sgu_b_s", "sgu_w_out", "attn_w_qkv", "attn_q_gain",
           "attn_k_gain", "attn_sinks", "attn_w_o", "rel_bias", "ffn_w_up", "ffn_conv_w", "ffn_conv_b", "ffn_w_down"]
SMALL = ["g_" + n for n in REPLICATED]
BF16_TRANSIT = {"sgu_w_s"}
SMALL_ATTN = ["g_attn_q_gain", "g_attn_k_gain", "g_attn_sinks", "g_rel_bias"]
SMALL_FFN = ["g_ffn_norm", "g_ffn_conv_b"]
SMALL_SGU = ["g_sgu_v_gain", "g_sgu_w_s", "g_sgu_b_s"]

GATHER_FIRST = ["sgu_w_in", "ffn_conv_w"]
UP0_SPLIT, UP1_SPLIT = 352, 304
PLAN = {
    "sgu_in": [("ag1", "sgu_w_out"), ("ag1", "ffn_w_up0", (0, UP0_SPLIT))],
    "sgu_gate": [("ag2", "sgu_w_out"), ("ag1", "ffn_w_up0", (UP0_SPLIT, D))],
    "sgu_out": [("ag2", "ffn_w_up0"), ("ag1", "ffn_w_down0")],
    "before_ffn0": [("ag2", "ffn_w_down0")],
    "ffn0_fwd": [("agd", "attn_w_qkv"), ("ag1", "attn_w_o"), ("ag1", "ffn_w_down1")],
    "qkv": [("ag2", "attn_w_o"), ("ag2", "ffn_w_down1"), ("ag1", "ffn_w_up1", (0, UP1_SPLIT))],
    "attn": [("ag1", "ffn_w_up1", (UP1_SPLIT, D))],
    "attn_out": [("ag2", "ffn_w_up1")],
    "ffn1_bwd2": [("rs1", "ffn_w_down1")],
    "ffn1_dw_up": [("rs2", "ffn_w_down1")],
    "attn_do": [("rs1", "ffn_w_up1")],
    "attn_bwd": [("rs2", "ffn_w_up1"), ("rs1", "attn_w_o")],
    "dw_qkv": [("rs2", "attn_w_o")],
    "dx_qkv": [("rs1", "attn_w_qkv")],
    "ffn0_bwd1": [("rs2", "attn_w_qkv")] + [("ag1", n) for n in SMALL_ATTN],
    "ffn0_bwd2": [("rs1", "ffn_w_down0")] + [("ag2", n) for n in SMALL_ATTN],
    "ffn0_dw_up": [("rs2", "ffn_w_down0")],
    "sgu_dgated": [("rs1", "ffn_w_up0")] + [("ag1", n) for n in SMALL_FFN],
    "dw_sgu_out": [("ag2", n) for n in SMALL_FFN],
    "sgu_gate_bwd": [("rs2", "ffn_w_up0"), ("rs1", "sgu_w_out")],
    "dw_sgu_in": [("rs2", "sgu_w_out")] + [("ag1", n) for n in SMALL_SGU],
    "after_dw": [("rs1", "sgu_w_in"), ("rs1", "ffn_conv_w")] + [("ag2", n) for n in SMALL_SGU],
    "dx_sgu_in": [("rs2", "sgu_w_in"), ("rs2", "ffn_conv_w")],
    "last": [("agd", "g_mix_norm")],
}


class _Overlap:
    def __init__(self, shard, place):
        self.shard, self.place = shard, place
        self.part, self.full = {}, {}
        self.grads, self.sib, self.own, self.recv = {}, {}, {}, {}

    def w(self, n):
        return self.full[n]

    def grad(self, n, pair):
        self.grads[n] = pair

    def small(self, g_rep):
        self.shard.update(("g_" + n, a.astype(BF16) if n in BF16_TRANSIT else a) for n, a in _views2d(g_rep).items())

    def sync(self, point):
        _exchange(self.hook(point), point)

    def chip_sums(self, n):
        sums, self.own[n] = _rs_partial(self.grads[n][0], self.sib.pop(n), self.place, "rs_partial_" + n)
        return sums

    def hook(self, host):
        ops = PLAN.get(host)
        if not ops:
            return None
        where = {"ag1": self.part, "ag2": self.full, "agd": self.full, "rs1": self.sib, "rs2": self.recv}
        idx = []

        def hook(results=None):
            if results is not None:
                for (kind, n, *_), i in zip(ops, idx):
                    where[kind][n] = results[i]
                return None
            comm = _Comm()
            for kind, n, *rows in ops:
                arr = {"ag1": lambda: self.shard[n], "agd": lambda: self.shard[n], "ag2": lambda: self.part.pop(n),
                       "rs1": lambda: self.grads[n][1], "rs2": lambda: self.chip_sums(n)}[kind]()
                idx.append(comm.add(kind, arr, *rows, into=self.part.pop(n) if rows and rows[0][0] else None))
            return comm

        return hook


TRANSPOSED = {"attn_w_qkv"}
PHYSICAL_T = {"ffn_w_up"}
SHARDED = {
    "sgu_w_in": ["sgu_w_in"], "sgu_w_out": ["sgu_w_out"], "attn_w_qkv": ["attn_w_qkv"], "attn_w_o": ["attn_w_o"],
    "ffn_w_up": ["ffn_w_up0", "ffn_w_up1"], "ffn_w_down": ["ffn_w_down0", "ffn_w_down1"], "ffn_conv_w": ["ffn_conv_w"],
}


def _send_views(w):
    out = {"ffn_conv_w": w["ffn_conv_w"].reshape(6, -1)}
    for name, parts in SHARDED.items():
        if name != "ffn_conv_w":
            out.update((p, (w[name][l].T if name in TRANSPOSED else w[name][l]).astype(BF16)) for l, p in enumerate(parts))
    return out


def _views2d(d):
    return {n: d[n].reshape(-1, d[n].shape[-1]) for n in REPLICATED if n in d}


def kernel(x, mix_norm, ffn_norm, sgu_w_in, sgu_v_gain, sgu_w_s, sgu_b_s, sgu_w_out, attn_w_qkv, attn_q_gain, attn_k_gain, attn_sinks, attn_w_o, rel_bias, ffn_w_up, ffn_conv_w, ffn_conv_b, ffn_w_down, loss_target, m_mix_norm, m_ffn_norm, m_sgu_w_in, m_sgu_v_gain, m_sgu_w_s, m_sgu_b_s, m_sgu_w_out, m_attn_w_qkv, m_attn_q_gain, m_attn_k_gain, m_attn_sinks, m_attn_w_o, m_rel_bias, m_ffn_w_up, m_ffn_conv_w, m_ffn_conv_b, m_ffn_w_down, v_mix_norm, v_ffn_norm, v_sgu_w_in, v_sgu_v_gain, v_sgu_w_s, v_sgu_b_s, v_sgu_w_out, v_attn_w_qkv, v_attn_q_gain, v_attn_k_gain, v_attn_sinks, v_attn_w_o, v_rel_bias, v_ffn_w_up, v_ffn_conv_w, v_ffn_conv_b, v_ffn_w_down):
    w = dict(zip(WEIGHTS, (mix_norm, ffn_norm, sgu_w_in, sgu_v_gain, sgu_w_s, sgu_b_s, sgu_w_out, attn_w_qkv, attn_q_gain, attn_k_gain,
                           attn_sinks, attn_w_o, rel_bias, ffn_w_up, ffn_conv_w, ffn_conv_b, ffn_w_down)))
    m = dict(zip(WEIGHTS, (m_mix_norm, m_ffn_norm, m_sgu_w_in, m_sgu_v_gain, m_sgu_w_s, m_sgu_b_s, m_sgu_w_out, m_attn_w_qkv, m_attn_q_gain,
                           m_attn_k_gain, m_attn_sinks, m_attn_w_o, m_rel_bias, m_ffn_w_up, m_ffn_conv_w, m_ffn_conv_b, m_ffn_w_down)))
    v = dict(zip(WEIGHTS, (v_mix_norm, v_ffn_norm, v_sgu_w_in, v_sgu_v_gain, v_sgu_w_s, v_sgu_b_s, v_sgu_w_out, v_attn_w_qkv, v_attn_q_gain,
                           v_attn_k_gain, v_attn_sinks, v_attn_w_o, v_rel_bias, v_ffn_w_up, v_ffn_conv_w, v_ffn_conv_b, v_ffn_w_down)))
    rep = {n: w[n] for n in REPLICATED}

    xi, yi, ci = lax.axis_index("x"), lax.axis_index("y"), lax.axis_index("c")
    place = jnp.stack([ci, 2 * xi + yi]).astype(jnp.int32)
    sch = _Overlap(_send_views(w), place)
    sch.full.update(zip(GATHER_FIRST, _allgather([sch.shard[n] for n in GATHER_FIRST], "gather_first")))

    loss, grad_x, g_rep = _local_step(x[0], loss_target[0], rep, sch)
    loss = lax.psum(loss, ("x", "y", "c"))
    sch.sync("last")

    out = [{}, {}, {}, {}]
    for name, parts in SHARDED.items():
        flip = (lambda a: jnp.swapaxes(a, -1, -2)) if name in TRANSPOSED | PHYSICAL_T else (lambda a: a)
        shape = flip(w[name]).shape
        as3d = lambda a: flip(a).reshape(len(parts), -1, shape[-1])
        res = _adamw_shard([sch.own[p] for p in parts], [sch.recv[p] for p in parts], as3d(w[name]), as3d(m[name]), as3d(v[name]),
                           "adamw_" + name, flipped=name in PHYSICAL_T)
        for o, r in zip(out, res):
            o[name] = flip(r.reshape(shape))
    small = _adamw_small([sch.full[n] for n in SMALL], *[list(_views2d(d).values()) for d in (rep, m, v)], "adamw_small")
    for o, res in zip(out, small):
        o.update((n, r.reshape(w[n].shape)) for n, r in zip(REPLICATED, res))

    return (loss, grad_x[None], *[out[0][n] for n in WEIGHTS], *[out[1][n] for n in WEIGHTS],
            *[out[2][n] for n in WEIGHTS], *[out[3][n] for n in WEIGHTS])
```

```python
import functools
import math

import numpy as np
import jax
import jax.numpy as jnp
from jax import lax
from jax.experimental import pallas as pl
from jax.experimental.pallas import tpu as pltpu

F32 = jnp.float32
BF16 = jnp.bfloat16
DH = jnp.bfloat16
S = jax.ShapeDtypeStruct

D = 1024
CHUNK = 128
SGU_W = 2048
SGU_G = 16
HD = 64
NH = 16
NKV = 4
KVG = 4
D_FF = 2816
REL_BUCKETS = 32
REL_MAX_DIST = 128
EPS = 1e-6
N_DEV = 8
MESH = pl.DeviceIdType.MESH

ADAM_LR = 0.001
ADAM_B1 = 0.9
ADAM_B2 = 0.999
ADAM_EPS = 1e-08
ADAM_WD = 0.01
ADAM_STEP = 10

ROW_TILE = 512
HALO = 8
FFN_ROWS = 256


def _tm(t):
    return min(ROW_TILE, t)


def _cp(*sem):
    return pltpu.CompilerParams(dimension_semantics=sem)


ANY = pl.BlockSpec(memory_space=pl.ANY)


def _place():
    x, y, c = lax.axis_index("x"), lax.axis_index("y"), lax.axis_index("c")
    return x, y, c, [(1 - x, y), (x, 1 - y), (1 - x, 1 - y)]


class _Comm:
    SEMS = {"ag1": 5, "ag2": 3, "rs1": 4, "rs2": 3, "agd": 8}

    def __init__(self):
        self.inputs, self.out_shapes, self.aliases, self.ops, self.n_sems = [], [], {}, [], 0

    def add(self, kind, arr, rows=None, into=None):
        lead = {"ag1": N_DEV, "agd": N_DEV, "ag2": None, "rs1": 4, "rs2": 3}[kind]
        shape = arr.shape if lead is None else (lead,) + arr.shape[(0 if kind in ("ag1", "agd") else 1):]
        if kind == "ag2":
            self.aliases[len(self.inputs)] = len(self.out_shapes)
        self.ops.append((kind, len(self.inputs), len(self.out_shapes), self.n_sems, rows))
        self.inputs.append(arr)
        if into is not None:
            self.aliases[len(self.inputs)] = len(self.out_shapes)
            self.inputs.append(into)
        self.out_shapes.append(S(shape, arr.dtype))
        self.n_sems += self.SEMS[kind]
        return len(self.out_shapes) - 1

    def _copies(self, ins, outs, send, recv):
        x, y, c, chips = _place()
        me, sibling = (x, y, c), (x, y, 1 - c)
        slot = lambda px, py, pc: 4 * px + 2 * py + pc
        sends, recvs, local = [], [], []

        def rc(src, dst, k, to):
            return lambda: pltpu.make_async_remote_copy(src_ref=src(), dst_ref=dst(), send_sem=send.at[k], recv_sem=recv.at[k],
                                                        device_id=to, device_id_type=MESH)

        for kind, ii, oi, b, rows in self.ops:
            src, dst = ins[ii], outs[oi]
            at = lambda ref, i: (lambda: ref.at[i])
            if kind == "ag1":
                part = slice(None) if rows is None else pl.ds(rows[0], rows[1] - rows[0])
                to = lambda i, d=dst, p=part: (lambda: d.at[i, p])
                whole, mine = (lambda s=src, p=part: s.at[p]), to(slot(*me))
                sends.append(rc(whole, mine, b, sibling))
                recvs.append(rc(whole, to(slot(x, y, 1 - c)), b, me))
                for j, chip in enumerate(chips):
                    sends.append(rc(whole, mine, b + 1 + j, (*chip, c)))
                    recvs.append(rc(whole, to(slot(*chip, c)), b + 1 + j, me))
                local.append(lambda s=whole, m=mine, k=b + 4: pltpu.make_async_copy(s(), m(), send.at[k]))
            elif kind == "ag2":
                for j, chip in enumerate(chips):
                    sends.append(rc(at(dst, slot(*chip, c)), at(dst, slot(*chip, c)), b + j, sibling))
                    recvs.append(rc(at(dst, slot(*chip, 1 - c)), at(dst, slot(*chip, 1 - c)), b + j, me))
            elif kind == "agd":
                whole, mine = (lambda s=src: s), at(dst, slot(*me))
                flip = lambda v, bit: 1 - v if bit else v
                for k in range(1, N_DEV):
                    peer = (flip(x, k >> 2), flip(y, (k >> 1) & 1), flip(c, k & 1))
                    sends.append(rc(whole, mine, b + k - 1, peer))
                    recvs.append(rc(whole, at(dst, slot(*peer)), b + k - 1, me))
                local.append(lambda s=src, m=mine, k=b + 7: pltpu.make_async_copy(s, m(), send.at[k]))
            elif kind == "rs1":
                for k in range(4):
                    sends.append(rc(at(src, 2 * k + (1 - c)), at(dst, k), b + k, sibling))
                    recvs.append(rc(at(src, 2 * k + c), at(dst, k), b + k, me))
            else:
                for j, (px, py) in enumerate(chips):
                    sends.append(rc(at(src, 2 * px + py), at(dst, j), b + j, (px, py, c)))
                    recvs.append(rc(at(src, 2 * px + py), at(dst, j), b + j, me))
        return sends, recvs, local

    def start(self, ins, outs, send, recv):
        sends, _, local = self._copies(ins, outs, send, recv)
        for make in local + sends:
            make().start()

    def finish(self, ins, outs, send, recv):
        sends, recvs, local = self._copies(ins, outs, send, recv)
        for make in recvs:
            make().wait_recv()
        for make in sends:
            make().wait_send()
        for make in local:
            make().wait()


def _run(body, args, hook, *, grid, in_specs, out_specs, out_shape, name, semantics, scratch_shapes=(), aliases=None):
    comm = hook() if hook is not None else None
    aliases = dict(aliases or {})
    if comm is None:
        return pl.pallas_call(body, grid=grid, in_specs=in_specs, out_specs=out_specs, out_shape=out_shape, name=name,
                              scratch_shapes=list(scratch_shapes), input_output_aliases=aliases,
                              compiler_params=_cp(*semantics))(*args)
    single = not isinstance(out_shape, (list, tuple))
    out_shapes = [out_shape] if single else list(out_shape)
    out_specs_l = [out_specs] if single else list(out_specs)
    n_in, n_out, n_scr, ci, co = len(args), len(out_shapes), len(scratch_shapes), len(comm.inputs), len(comm.out_shapes)

    def wrapped(*refs):
        ins, cins = refs[:n_in], refs[n_in:n_in + ci]
        outs, couts = refs[n_in + ci:n_in + ci + n_out], refs[n_in + ci + n_out:n_in + ci + n_out + co]
        scr = refs[n_in + ci + n_out + co:n_in + ci + n_out + co + n_scr]
        send, recv = refs[-2:]
        first = functools.reduce(lambda a, b: a & b, [pl.program_id(a) == 0 for a in range(len(grid))])
        last = functools.reduce(lambda a, b: a & b, [pl.program_id(a) == g - 1 for a, g in enumerate(grid)])

        @pl.when(first)
        def _():
            comm.start(cins, couts, send, recv)

        body(*ins, *outs, *scr)

        @pl.when(last)
        def _():
            comm.finish(cins, couts, send, recv)

    res = pl.pallas_call(
        wrapped, grid=grid, in_specs=list(in_specs) + [ANY] * ci, out_specs=out_specs_l + [ANY] * co,
        out_shape=out_shapes + comm.out_shapes, name=name,
        scratch_shapes=list(scratch_shapes) + [pltpu.SemaphoreType.DMA((comm.n_sems,)), pltpu.SemaphoreType.DMA((comm.n_sems,))],
        input_output_aliases={**aliases, **{n_in + k: n_out + v for k, v in comm.aliases.items()}},
        compiler_params=pltpu.CompilerParams(dimension_semantics=("arbitrary",) * len(grid), has_side_effects=True))(*args, *comm.inputs)
    hook(res[n_out:])
    return res[0] if single else list(res[:n_out])


def _dot(a, b):
    return jnp.dot(a, b, preferred_element_type=F32)


def _dot_nt(a, b):
    return lax.dot_general(a, b, (((1,), (1,)), ((), ())), preferred_element_type=F32)


def _dot_tn(a, b):
    return lax.dot_general(a, b, (((0,), (0,)), ((), ())), preferred_element_type=F32)


def _gelu(x):
    return 0.5 * x * (1.0 + lax.erf(x * (2.0 ** -0.5)))


def _gelu_and_grad(x):
    cdf = 0.5 * (1.0 + lax.erf(x * (2.0 ** -0.5)))
    return x * cdf, cdf + x * jnp.exp(-0.5 * x * x) * (1.0 / math.sqrt(2.0 * math.pi))


def _sigmoid(x):
    return 1.0 / (1.0 + jnp.exp(-x))


def _rstd(x):
    return lax.rsqrt(jnp.mean(x * x, axis=-1, keepdims=True) + EPS)


def _rel_tables():
    q = np.arange(CHUNK)[:, None] + CHUNK
    k = np.arange(2 * CHUNK)[None, :]
    dist = q - k
    n = np.maximum(dist, 0)
    max_exact = REL_BUCKETS // 2
    large = max_exact + (np.log(np.maximum(n, 1).astype(np.float32) / max_exact)
                         / math.log(REL_MAX_DIST / max_exact) * (REL_BUCKETS - max_exact)).astype(np.int32)
    large = np.minimum(large, REL_BUCKETS - 1)
    return np.where(n < max_exact, n, large).astype(np.int32)


def _rmsnorm(x, gain, name):
    t = x.shape[0]
    tm = _tm(t)

    def body(x_ref, g_ref, o_ref):
        xv = x_ref[...]
        o_ref[...] = (xv * _rstd(xv) * g_ref[...]).astype(BF16)

    return pl.pallas_call(
        body, grid=(t // tm,), name=name,
        in_specs=[pl.BlockSpec((tm, D), lambda i: (i, 0)), pl.BlockSpec((1, D), lambda i: (0, 0))],
        out_specs=pl.BlockSpec((tm, D), lambda i: (i, 0)),
        out_shape=S((t, D), BF16), compiler_params=_cp("parallel"))(x, gain)


def _resident(shape):
    zeros = (0,) * len(shape)
    return pl.BlockSpec(shape, lambda *_: zeros, pipeline_mode=pl.Buffered(1))


def _mm_slot(hn, wg, out_dtype, name, hook=None):
    t, k = hn.shape
    ns, _, n = wg.shape
    tm = _tm(t)

    def body(a_ref, w_ref, o_ref):
        a = a_ref[...]
        for s in range(ns):
            o_ref[s] = _dot(a, w_ref[s]).astype(out_dtype)

    return _run(
        body, [hn, wg], hook, grid=(t // tm,), name=name, semantics=("parallel",),
        in_specs=[pl.BlockSpec((tm, k), lambda i: (i, 0)), _resident(wg.shape)],
        out_specs=pl.BlockSpec((ns, tm, n), lambda i: (0, i, 0)), out_shape=S((ns, t, n), out_dtype))


def _mm_t(hn, wt, name, hook=None):
    t, k = hn.shape
    ns, n, _ = wt.shape
    tm = _tm(t)

    def body(a_ref, w_ref, o_ref):
        a = a_ref[...]
        for s in range(ns):
            o_ref[s * n:(s + 1) * n, :] = _dot_nt(w_ref[s], a)

    return _run(
        body, [hn, wt], hook, grid=(t // tm,), name=name, semantics=("parallel",),
        in_specs=[pl.BlockSpec((tm, k), lambda i: (i, 0)), _resident(wt.shape)],
        out_specs=pl.BlockSpec((ns * n, tm), lambda i: (0, i)), out_shape=S((ns * n, t), F32))


def _conv3(a, prev, cw, cb, tm):
    ext = jnp.concatenate([prev, a], axis=0)
    return cw[2:3] * a + cw[1:2] * ext[HALO - 1:HALO - 1 + tm] + cw[0:1] * ext[HALO - 2:HALO - 2 + tm] + cb


def _ffn_fwd(hn, h, wup, wdown, cw, cb, extra, mode, name, hook=None):
    t, k = hn.shape
    n = wup.shape[-1]
    nh = wup.shape[0] // 2
    tm = min(FFN_ROWS, t)
    ni = t // tm

    def body(a_ref, h_ref, wu_ref, wd_ref, cw_ref, cb_ref, e_ref, as_ref, cs_ref, o1_ref, o2_ref, carry):
        i = pl.program_id(0)

        @pl.when(i == 0)
        def _():
            carry[...] = jnp.zeros_like(carry)

        a = a_ref[...]
        acc = h_ref[...]
        nxt = (_dot(a, wu_ref[0]), _dot(a, wu_ref[nh]))
        for j in range(nh):
            ag, av = nxt
            if j + 1 < nh:
                nxt = (_dot(a, wu_ref[j + 1]), _dot(a, wu_ref[nh + j + 1]))
            as_ref[j] = ag.astype(BF16)
            as_ref[nh + j] = av.astype(BF16)
            cg = _conv3(ag, carry[j], cw_ref[j], cb_ref[j], tm)
            cv = _conv3(av, carry[nh + j], cw_ref[nh + j], cb_ref[nh + j], tm)
            carry[j] = ag[tm - HALO:]
            carry[nh + j] = av[tm - HALO:]
            cs_ref[j] = cg.astype(BF16)
            cs_ref[nh + j] = cv.astype(BF16)
            act = (cg * _sigmoid(cg) * cv).astype(BF16)
            acc = acc + _dot(act, wd_ref[j * n:(j + 1) * n, :])
        if mode == "norm":
            o1_ref[...] = acc
            o2_ref[...] = (acc * _rstd(acc) * e_ref[...]).astype(BF16)
        else:
            err = acc - e_ref[...]
            o1_ref[...] = (err * (1.0 / D)).astype(o1_ref.dtype)
            o2_ref[...] = jnp.full(o2_ref.shape, jnp.sum(err * err), F32)

    row = pl.BlockSpec((tm, D), lambda i: (i, 0))
    if mode == "norm":
        e_spec, o2_spec, o2_shape = pl.BlockSpec((1, D), lambda i: (0, 0)), row, S((t, D), BF16)
    else:
        e_spec, o2_spec, o2_shape = row, pl.BlockSpec((None, 8, 128), lambda i: (i, 0, 0)), S((ni, 8, 128), F32)
    aspec = pl.BlockSpec((2 * nh, tm, n), lambda i: (0, i, 0))
    return _run(
        body, [hn, h, wup, wdown, cw, cb, extra], hook, grid=(ni,), name=name, semantics=("arbitrary",),
        in_specs=[pl.BlockSpec((tm, k), lambda i: (i, 0)), row, _resident(wup.shape), _resident(wdown.shape),
                  _resident(cw.shape), _resident(cb.shape), e_spec],
        out_specs=[aspec, aspec, row, o2_spec],
        out_shape=[S((2 * nh, t, n), BF16), S((2 * nh, t, n), BF16), S((t, D), F32 if mode == "norm" else DH), o2_shape],
        scratch_shapes=[pltpu.VMEM((2 * nh, HALO, n), F32)])


def _tril_mask():
    r = lax.broadcasted_iota(jnp.int32, (CHUNK, CHUNK), 0)
    c = lax.broadcasted_iota(jnp.int32, (CHUNK, CHUNK), 1)
    return r >= c


def _sgu_gate_fwd(a_s, vgain, ws, bst, name, hook=None):
    t = a_s.shape[1]
    sw = a_s.shape[2]
    gps = sw // CHUNK

    def body(a_ref, vg_ref, ws_ref, b_ref, o_ref):
        v = _gelu(jnp.concatenate([a_ref[4 + s].astype(F32) for s in range(4)], axis=1))
        vn = (v * _rstd(v) * vg_ref[...]).astype(BF16)
        tri = _tril_mask()
        for g in range(SGU_G):
            w = jnp.where(tri, ws_ref[g], 0.0).astype(BF16)
            sg = _dot(w, vn[:, g * CHUNK:(g + 1) * CHUNK]) + b_ref[:, g:g + 1]
            lo = (g % gps) * CHUNK
            u = _gelu(a_ref[g // gps, :, lo:lo + CHUNK].astype(F32))
            o_ref[g // gps, :, lo:lo + CHUNK] = (u * sg).astype(BF16)

    return _run(
        body, [a_s, vgain, ws, bst], hook, grid=(t // CHUNK,), name=name, semantics=("parallel",),
        in_specs=[pl.BlockSpec((8, CHUNK, sw), lambda n: (0, n, 0)), pl.BlockSpec((1, SGU_W), lambda n: (0, 0)),
                  pl.BlockSpec((SGU_G, CHUNK, CHUNK), lambda n: (0, 0, 0)), pl.BlockSpec((CHUNK, SGU_G), lambda n: (0, 0))],
        out_specs=pl.BlockSpec((4, CHUNK, sw), lambda n: (0, n, 0)), out_shape=S((4, t, sw), BF16))


def _resid_mm(a_s, w, resid, extra, mode, name, hook=None, fm=False):
    nk, t, kc = (1, a_s.shape[1], a_s.shape[0]) if fm else a_s.shape
    tm = _tm(t)
    ni = t // tm

    def body(a_ref, w_ref, r_ref, e_ref, o1_ref, o2_ref):
        h = r_ref[...]
        if fm:
            h = h + _dot_tn(a_ref[...], w_ref[...])
        for j in range(0 if fm else nk):
            h = h + _dot(a_ref[j], w_ref[j * kc:(j + 1) * kc, :])
        if mode == "norm":
            o1_ref[...] = h
            o2_ref[...] = (h * _rstd(h) * e_ref[...]).astype(BF16)
        else:
            err = h - e_ref[...]
            o1_ref[...] = (err * (1.0 / D)).astype(o1_ref.dtype)
            o2_ref[...] = jnp.full(o2_ref.shape, jnp.sum(err * err), F32)

    row = pl.BlockSpec((tm, D), lambda i: (i, 0))
    if mode == "norm":
        e_spec, o2_spec, o2_shape = pl.BlockSpec((1, D), lambda i: (0, 0)), row, S((t, D), BF16)
    else:
        e_spec, o2_spec, o2_shape = row, pl.BlockSpec((None, 8, 128), lambda i: (i, 0, 0)), S((ni, 8, 128), F32)
    return _run(
        body, [a_s, w, resid, extra], hook, grid=(ni,), name=name, semantics=("parallel",),
        in_specs=[pl.BlockSpec((kc, tm), lambda i: (0, i)) if fm else pl.BlockSpec((nk, tm, kc), lambda i: (0, i, 0)),
                  _resident(w.shape), row, e_spec],
        out_specs=[row, o2_spec], out_shape=[S((t, D), F32 if mode == "norm" else DH), o2_shape])


def _relbias_fwd(rel_bias_t, bucket_row, name):
    nb = bucket_row.shape[1]

    def body(rb_ref, bk_ref, o_ref):
        onehot = (lax.broadcasted_iota(jnp.int32, (REL_BUCKETS, nb), 0) == bk_ref[...]).astype(F32)
        o_ref[...] = jnp.dot(rb_ref[...], onehot, precision=lax.Precision.HIGHEST, preferred_element_type=F32)

    return pl.pallas_call(body, out_shape=S((NH, nb), F32), name=name)(rel_bias_t, bucket_row)


def _relbias_bwd(dbias, bucket_row, name):
    nb = bucket_row.shape[1]

    def body(db_ref, bk_ref, o_ref):
        onehot = (lax.broadcasted_iota(jnp.int32, (REL_BUCKETS, nb), 0) == bk_ref[...]).astype(F32)
        o_ref[...] = lax.dot_general(db_ref[...], onehot, (((1,), (1,)), ((), ())),
                                     precision=lax.Precision.HIGHEST, preferred_element_type=F32)

    return pl.pallas_call(body, out_shape=S((NH, REL_BUCKETS), F32), name=name)(dbias, bucket_row)


QKV = D + 2 * NKV * HD
KV0 = D


def _rstd_rows(x):
    return lax.rsqrt(jnp.mean(x * x, axis=0, keepdims=True) + EPS)


def _attn_valid(n):
    kj = lax.broadcasted_iota(jnp.int32, (2 * CHUNK, CHUNK), 0)
    qi = lax.broadcasted_iota(jnp.int32, (2 * CHUNK, CHUNK), 1)
    dist = qi + CHUNK - kj
    return (dist >= 0) & (dist < CHUNK) & ((n > 0) | (kj >= CHUNK))


def _attn_band(cur_ref, prev_ref, row):
    return jnp.concatenate([prev_ref[row - KV0:row - KV0 + HD, :], cur_ref[row:row + HD, :]], axis=1)


def _attn_probs(kn_tok, qn, bias, valid, sink):
    s = _dot(kn_tok, qn) * (HD ** -0.5) + bias
    s = jnp.where(valid, s, -jnp.inf)
    m = jnp.maximum(jnp.max(s, axis=0, keepdims=True), sink)
    p = jnp.exp(s - m)
    psink = jnp.exp(sink - m)
    inv = 1.0 / (jnp.sum(p, axis=0, keepdims=True) + psink)
    return p * inv, psink * inv


def _attn_fwd(qkv_t, qg, kg, sinks, bias, name, hook=None):
    t = qkv_t.shape[1]

    def body(cur_ref, prev_ref, qg_ref, kg_ref, sink_ref, bias_ref, o_ref):
        n = pl.program_id(0)
        valid = _attn_valid(n)
        ks = [_attn_band(cur_ref, prev_ref, KV0 + HD * h) for h in range(NKV)]
        kn_toks = [(k * _rstd_rows(k) * kg_ref[...]).astype(BF16).T for k in ks]
        vbs = [_attn_band(cur_ref, prev_ref, KV0 + HD * (NKV + h)).astype(BF16) for h in range(NKV)]
        qs = [cur_ref[HD * hq:HD * (hq + 1), :] for hq in range(NH)]
        qns = [(q * _rstd_rows(q) * qg_ref[...]).astype(BF16) for q in qs]
        ps = [_attn_probs(kn_toks[hq // KVG], qns[hq], bias_ref[hq], valid, sink_ref[hq])[0] for hq in range(NH)]
        for hq in range(NH):
            o_ref[HD * hq:HD * (hq + 1), :] = _dot(vbs[hq // KVG], ps[hq].astype(BF16)).astype(BF16)

    col = pl.BlockSpec((HD, 1), lambda n: (0, 0))
    return _run(
        body, [qkv_t, qkv_t, qg, kg, sinks, bias], hook, grid=(t // CHUNK,), name=name, semantics=("parallel",),
        in_specs=[pl.BlockSpec((QKV, CHUNK), lambda n: (0, n)),
                  pl.BlockSpec((QKV - KV0, CHUNK), lambda n: (KV0 // (QKV - KV0), jnp.maximum(n - 1, 0))),
                  col, col, pl.BlockSpec(memory_space=pltpu.SMEM), pl.BlockSpec((NH, 2 * CHUNK, CHUNK), lambda n: (0, 0, 0))],
        out_specs=pl.BlockSpec((D, CHUNK), lambda n: (0, n)), out_shape=S((D, t), BF16))


def _dx_rows(dh, w, kc, out_dtype, name, hook=None):
    t = dh.shape[0]
    nk = w.shape[0] // kc
    tm = _tm(t)

    def body(d_ref, w_ref, o_ref):
        dhb = d_ref[...].astype(BF16)
        for j in range(nk):
            o_ref[j] = _dot_nt(dhb, w_ref[j * kc:(j + 1) * kc, :]).astype(out_dtype)

    return _run(
        body, [dh, w], hook, grid=(t // tm,), name=name, semantics=("parallel",),
        in_specs=[pl.BlockSpec((tm, D), lambda i: (i, 0)), _resident(w.shape)],
        out_specs=pl.BlockSpec((nk, tm, kc), lambda i: (0, i, 0)), out_shape=S((nk, t, kc), out_dtype))


def _dx_rows_t(dh, w, name, hook=None):
    t = dh.shape[0]
    k = w.shape[0]
    tm = _tm(t)

    def body(d_ref, w_ref, o_ref):
        o_ref[...] = _dot_nt(w_ref[...], d_ref[...].astype(BF16)).astype(BF16)

    return _run(
        body, [dh, w], hook, grid=(t // tm,), name=name, semantics=("parallel",),
        in_specs=[pl.BlockSpec((tm, D), lambda i: (i, 0)), _resident(w.shape)],
        out_specs=pl.BlockSpec((k, tm), lambda i: (0, i)), out_shape=S((k, t), BF16))


def _ffn_bwd1(dh, c, wdown, name, hook=None):
    ns, t, n = c.shape
    nh = ns // 2
    tm = min(FFN_ROWS, t)
    ni = t // tm

    def body(d_ref, c_ref, wd_ref, dc_ref, dw_hbm, dwb_hbm, acc, stage):
        i = pl.program_id(0)

        @pl.when(i == 0)
        def _():
            acc[...] = jnp.zeros_like(acc)

        dhb = d_ref[...].astype(BF16)
        for j in range(nh):
            dact = _dot_nt(dhb, wd_ref[j * n:(j + 1) * n, :])
            cg = c_ref[j].astype(F32)
            cv = c_ref[nh + j].astype(F32)
            sg = _sigmoid(cg)
            gs = cg * sg
            acc[j * n:(j + 1) * n, :] += _dot_tn((gs * cv).astype(BF16), dhb)
            dc_ref[j] = (dact * cv * (sg + gs * (1.0 - sg))).astype(BF16)
            dc_ref[nh + j] = (dact * gs).astype(BF16)

        @pl.when(i == ni - 1)
        def _():
            pltpu.sync_copy(acc, dw_hbm)
            for j in range(nh):
                stage[...] = acc[j * n:(j + 1) * n, :].astype(BF16)
                pltpu.sync_copy(stage, dwb_hbm.at[pl.ds(j * n, n), :])

    slab = pl.BlockSpec((ns, tm, n), lambda i: (0, i, 0))
    return _run(
        body, [dh, c, wdown], hook, grid=(ni,), name=name, semantics=("arbitrary",),
        in_specs=[pl.BlockSpec((tm, D), lambda i: (i, 0)), slab, _resident(wdown.shape)],
        out_specs=[slab, ANY, ANY], out_shape=[S((ns, t, n), BF16), S(wdown.shape, F32), S(wdown.shape, BF16)],
        scratch_shapes=[pltpu.VMEM(wdown.shape, F32), pltpu.VMEM((n, D), BF16)])


def _ffn_bwd2(dc, a, wup, cw, h, gain, dh_in, name, hook=None):
    ns, t, n = dc.shape
    tm = min(FFN_ROWS, t)
    ni = t // tm

    def body(dc_ref, a_ref, wu_ref, cw_ref, h_ref, g_ref, di_ref, da_ref, o_ref, dg_ref, dcw_ref, dcb_ref, carry, keep):
        i = pl.program_id(0)

        @pl.when(i == 0)
        def _():
            carry[...] = jnp.zeros_like(carry)
            dg_ref[...] = jnp.zeros_like(dg_ref)
            dcw_ref[...] = jnp.zeros_like(dcw_ref)
            dcb_ref[...] = jnp.zeros_like(dcb_ref)

        rsum = lambda v: jnp.sum(v, axis=0, keepdims=True)
        acc = jnp.zeros((tm, D), F32)
        for s in range(ns):
            x = dc_ref[s].astype(F32)
            ext = jnp.concatenate([x, carry[s]], axis=0)
            keep[0] = ext[1:1 + tm]
            keep[1] = ext[2:2 + tm]
            x1, x2 = keep[0], keep[1]
            cwv = cw_ref[s]
            da = (cwv[2:3] * x + cwv[1:2] * x1 + cwv[0:1] * x2).astype(BF16)
            carry[s] = x[:HALO]
            da_ref[s] = da
            acc = acc + _dot_nt(da, wu_ref[s])
            av = a_ref[s].astype(F32)
            dcw_ref[s] += jnp.concatenate([rsum(x2 * av), rsum(x1 * av), rsum(x * av)], axis=0)
            dcb_ref[s] += rsum(x)
        hv = h_ref[...]
        r = _rstd(hv)
        gg = acc * g_ref[...]
        dh_new = di_ref[...].astype(F32) + r * gg - hv * (r * r * r * jnp.mean(gg * hv, axis=-1, keepdims=True))
        o_ref[...] = dh_new.astype(o_ref.dtype)
        dg_ref[...] += jnp.sum(acc * hv * r, axis=0, keepdims=True)

    slab = pl.BlockSpec((ns, tm, n), lambda i: (0, ni - 1 - i, 0))
    row = pl.BlockSpec((tm, D), lambda i: (ni - 1 - i, 0))
    vec = pl.BlockSpec((1, D), lambda i: (0, 0))
    whole = lambda shape: pl.BlockSpec(shape, lambda i: (0,) * len(shape))
    return _run(
        body, [dc, a, wup, cw, h, gain, dh_in], hook, grid=(ni,), name=name, semantics=("arbitrary",),
        in_specs=[slab, slab, _resident(wup.shape), _resident(cw.shape), row, vec, row],
        out_specs=[slab, row, vec, whole((ns, 3, n)), whole((ns, 1, n))],
        out_shape=[S((ns, t, n), BF16), S((t, D), DH), S((1, D), F32), S((ns, 3, n), F32), S((ns, 1, n), F32)],
        scratch_shapes=[pltpu.VMEM((ns, HALO, n), F32), pltpu.VMEM((2, tm, n), F32)])


def _dw_slot(hn, dy_s, name, hook=None):
    t, k = hn.shape
    ns, _, n = dy_s.shape
    tm = _tm(t)

    def body(a_ref, b_ref, o_ref, ob_ref, at_ref):
        @pl.when(pl.program_id(0) == 0)
        def _():
            for i in range(t // tm):
                at_ref[:, i * tm:(i + 1) * tm] = a_ref[i * tm:(i + 1) * tm, :].T

        acc = _dot(at_ref[...], b_ref[...])
        o_ref[...] = acc
        ob_ref[...] = acc.astype(BF16)

    ospec = pl.BlockSpec((None, k, n), lambda j: (j, 0, 0))
    return _run(
        body, [hn, dy_s], hook, grid=(ns,), name=name, semantics=("arbitrary",),
        in_specs=[_resident(hn.shape), pl.BlockSpec((None, t, n), lambda j: (j, 0, 0))],
        out_specs=[ospec, ospec], out_shape=[S((ns, k, n), F32), S((ns, k, n), BF16)],
        scratch_shapes=[pltpu.VMEM((k, t), BF16)])


def _dw_rows(a_s, dh, name, hook=None, fm=False):
    nk, t, kc = (1, a_s.shape[1], a_s.shape[0]) if fm else a_s.shape
    tm = _tm(t)
    ni = t // tm

    def body(a_ref, d_ref, o_ref, ob_ref):
        i = pl.program_id(0)
        dhb = d_ref[...].astype(BF16)

        @pl.when(i == 0)
        def _():
            o_ref[...] = jnp.zeros_like(o_ref)

        if fm:
            o_ref[...] += _dot(a_ref[...], dhb)
        for j in range(0 if fm else nk):
            o_ref[j * kc:(j + 1) * kc, :] += _dot_tn(a_ref[j], dhb)

        @pl.when(i == ni - 1)
        def _():
            ob_ref[...] = o_ref[...].astype(BF16)

    ospec = pl.BlockSpec((nk * kc, D), lambda i: (0, 0))
    return _run(
        body, [a_s, dh], hook, grid=(ni,), name=name, semantics=("arbitrary",),
        in_specs=[pl.BlockSpec((kc, tm), lambda i: (0, i)) if fm else pl.BlockSpec((nk, tm, kc), lambda i: (0, i, 0)),
                  pl.BlockSpec((tm, D), lambda i: (i, 0))],
        out_specs=[ospec, ospec], out_shape=[S((nk * kc, D), F32), S((nk * kc, D), BF16)])


def _dx_slot_normbwd(dy_s, wg, h, gain, dh_in, name, hook=None, fm=False, out_dtype=F32):
    ns, t, n = (1, dy_s.shape[1], dy_s.shape[0]) if fm else dy_s.shape
    tm = _tm(t)

    def body(dy_ref, w_ref, h_ref, g_ref, di_ref, o_ref, dg_ref):
        i = pl.program_id(0)

        @pl.when(i == 0)
        def _():
            dg_ref[...] = jnp.zeros_like(dg_ref)

        g = _dot_tn(dy_ref[...], w_ref[...]) if fm else _dot_nt(dy_ref[0], w_ref[0])
        for s in range(1, ns):
            g = g + _dot_nt(dy_ref[s], w_ref[s])
        hv = h_ref[...]
        r = _rstd(hv)
        gg = g * g_ref[...]
        dh_new = di_ref[...].astype(F32) + r * gg - hv * (r * r * r * jnp.mean(gg * hv, axis=-1, keepdims=True))
        o_ref[...] = dh_new.astype(o_ref.dtype)
        dg_ref[...] += jnp.sum(g * hv * r, axis=0, keepdims=True)

    row = pl.BlockSpec((tm, D), lambda i: (i, 0))
    vec = pl.BlockSpec((1, D), lambda i: (0, 0))
    return _run(
        body, [dy_s, wg, h, gain, dh_in], hook, grid=(t // tm,), name=name, semantics=("arbitrary",),
        in_specs=[pl.BlockSpec((n, tm), lambda i: (0, i)) if fm else pl.BlockSpec((ns, tm, n), lambda i: (0, i, 0)),
                  _resident(wg.shape), row, vec, row],
        out_specs=[row, vec], out_shape=[S((t, D), out_dtype), S((1, D), F32)])


def _sgu_gate_bwd(a_s, dg_s, vgain, ws, bst, name, hook=None):
    t = a_s.shape[1]
    sw = a_s.shape[2]
    gps = sw // CHUNK

    def body(a_ref, dg_ref, vg_ref, ws_ref, b_ref, da_ref, dws_ref, dbt_ref, dvg_ref, dvn_ref):
        n = pl.program_id(0)

        @pl.when(n == 0)
        def _():
            dws_ref[...] = jnp.zeros_like(dws_ref)
            dbt_ref[...] = jnp.zeros_like(dbt_ref)
            dvg_ref[...] = jnp.zeros_like(dvg_ref)

        vpre = jnp.concatenate([a_ref[4 + s].astype(F32) for s in range(4)], axis=1)
        v, v_grad = _gelu_and_grad(vpre)
        r = _rstd(v)
        vhat = v * r
        vn = (vhat * vg_ref[...]).astype(BF16)
        tri = _tril_mask()
        lane = lax.broadcasted_iota(jnp.int32, (CHUNK, CHUNK), 1)
        dbt = jnp.zeros((CHUNK, CHUNK), F32)
        for g in range(SGU_G):
            w = jnp.where(tri, ws_ref[g], 0.0).astype(BF16)
            vng = vn[:, g * CHUNK:(g + 1) * CHUNK]
            sg = _dot(w, vng) + b_ref[:, g:g + 1]
            lo = (g % gps) * CHUNK
            u, u_grad = _gelu_and_grad(a_ref[g // gps, :, lo:lo + CHUNK].astype(F32))
            dgate = dg_ref[g // gps, :, lo:lo + CHUNK].astype(F32)
            da_ref[g // gps, :, lo:lo + CHUNK] = (dgate * sg * u_grad).astype(BF16)
            ds = dgate * u
            dsb = ds.astype(BF16)
            dvn_ref[:, g * CHUNK:(g + 1) * CHUNK] = _dot_tn(w, dsb)
            dws_ref[g] += jnp.where(tri, _dot_nt(dsb, vng), 0.0)
            dbt = dbt + jnp.where(lane == g, jnp.sum(ds, axis=-1, keepdims=True), 0.0)
        dbt_ref[...] += dbt
        dvn = dvn_ref[...]
        dvg_ref[...] += jnp.sum(dvn * vhat, axis=0, keepdims=True)
        gg = dvn * vg_ref[...]
        dv = r * gg - v * (r * r * r * jnp.mean(gg * v, axis=-1, keepdims=True))
        dav = (dv * v_grad).astype(BF16)
        for s in range(4):
            da_ref[4 + s] = dav[:, s * sw:(s + 1) * sw]

    return _run(
        body, [a_s, dg_s, vgain, ws, bst], hook, grid=(t // CHUNK,), name=name, semantics=("arbitrary",),
        in_specs=[pl.BlockSpec((8, CHUNK, sw), lambda n: (0, n, 0)), pl.BlockSpec((4, CHUNK, sw), lambda n: (0, n, 0)),
                  pl.BlockSpec((1, SGU_W), lambda n: (0, 0)), pl.BlockSpec((SGU_G, CHUNK, CHUNK), lambda n: (0, 0, 0)),
                  pl.BlockSpec((CHUNK, SGU_G), lambda n: (0, 0))],
        out_specs=[pl.BlockSpec((8, CHUNK, sw), lambda n: (0, n, 0)), pl.BlockSpec((SGU_G, CHUNK, CHUNK), lambda n: (0, 0, 0)),
                   pl.BlockSpec((CHUNK, CHUNK), lambda n: (0, 0)), pl.BlockSpec((1, SGU_W), lambda n: (0, 0))],
        out_shape=[S((8, t, sw), BF16), S((SGU_G, CHUNK, CHUNK), F32), S((CHUNK, CHUNK), F32), S((1, SGU_W), F32)],
        scratch_shapes=[pltpu.VMEM((CHUNK, SGU_W), F32)])


def _attn_bwd(qkv_t, do_t, qg, kg, sinks, bias, name, hook=None):
    t = qkv_t.shape[1]
    nb = t // CHUNK

    def body(cur_ref, prev_ref, do_ref, qg_ref, kg_ref, sink_ref, bias_ref,
             o_ref, dqg_out, dkg_out, dsk_out, dbias_ref, carry, dqg_ref, dkg_ref, dsk_ref):
        n = pl.program_id(0)

        @pl.when(n == 0)
        def _():
            carry[...] = jnp.zeros_like(carry)
            dqg_ref[...] = jnp.zeros_like(dqg_ref)
            dkg_ref[...] = jnp.zeros_like(dkg_ref)
            dsk_ref[...] = jnp.zeros_like(dsk_ref)
            dbias_ref[...] = jnp.zeros_like(dbias_ref)

        @pl.when(n < nb)
        def _():
            valid = _attn_valid(n)
            o_ref[0:KV0, :] = carry[0:KV0, :].astype(BF16)
            kvs, heads = range(NKV), range(NH)
            group = lambda h: range(KVG * h, KVG * (h + 1))
            ks = [_attn_band(cur_ref, prev_ref, KV0 + HD * h) for h in kvs]
            rks = [_rstd_rows(k) for k in ks]
            khats = [k * rk for k, rk in zip(ks, rks)]
            kns = [(khat * kg_ref[...]).astype(BF16) for khat in khats]
            kn_toks = [kn.T for kn in kns]
            vbs = [_attn_band(cur_ref, prev_ref, KV0 + HD * (NKV + h)).astype(BF16) for h in kvs]
            v_toks = [vb.T for vb in vbs]
            qs = [cur_ref[HD * hq:HD * (hq + 1), :] for hq in heads]
            rqs = [_rstd_rows(q) for q in qs]
            qhats = [q * rq for q, rq in zip(qs, rqs)]
            qns = [(qhat * qg_ref[...]).astype(BF16) for qhat in qhats]
            probs = [_attn_probs(kn_toks[hq // KVG], qns[hq], bias_ref[hq], valid, sink_ref[hq]) for hq in heads]
            dohs = [do_ref[HD * hq:HD * (hq + 1), :] for hq in heads]
            dps = [_dot(v_toks[hq // KVG], dohs[hq]) for hq in heads]
            dsums = [jnp.sum(p * dp, axis=0, keepdims=True) for (p, _), dp in zip(probs, dps)]
            dss = [p * (dp - dsum) for (p, _), dp, dsum in zip(probs, dps, dsums)]
            for hq in heads:
                dsk_ref[hq:hq + 1, :] -= probs[hq][1] * dsums[hq]
                dbias_ref[hq] += dss[hq]
            dvs = [sum(_dot_nt(dohs[hq], probs[hq][0].astype(BF16)) for hq in group(h)) for h in kvs]
            dscs = [(ds * (HD ** -0.5)).astype(BF16) for ds in dss]
            dqns = [_dot(kns[hq // KVG], dscs[hq]) for hq in heads]
            dkns = [sum(_dot_nt(qns[hq], dscs[hq]) for hq in group(h)) for h in kvs]
            dqg_ref[...] += sum(dqn * qhat for dqn, qhat in zip(dqns, qhats))
            for hq in heads:
                gq = dqns[hq] * qg_ref[...]
                carry[HD * hq:HD * (hq + 1), :] = rqs[hq] * gq - qs[hq] * (
                    rqs[hq] * rqs[hq] * rqs[hq] * jnp.mean(gq * qs[hq], axis=0, keepdims=True))
            dkg_ref[...] += sum(dkn * khat for dkn, khat in zip(dkns, khats))
            for h in kvs:
                krow, vrow = KV0 + HD * h, KV0 + HD * (NKV + h)
                gk = dkns[h] * kg_ref[...]
                dk = rks[h] * gk - ks[h] * (rks[h] * rks[h] * rks[h] * jnp.mean(gk * ks[h], axis=0, keepdims=True))
                o_ref[krow:krow + HD, :] = (carry[krow:krow + HD, :] + dk[:, :CHUNK]).astype(BF16)
                o_ref[vrow:vrow + HD, :] = (carry[vrow:vrow + HD, :] + dvs[h][:, :CHUNK]).astype(BF16)
                carry[krow:krow + HD, :] = dk[:, CHUNK:]
                carry[vrow:vrow + HD, :] = dvs[h][:, CHUNK:]

        @pl.when(n == nb)
        def _():
            o_ref[...] = carry[...].astype(BF16)
            dqg_out[...] = jnp.sum(dqg_ref[...], axis=1, keepdims=True)
            dkg_out[...] = jnp.sum(dkg_ref[...], axis=1, keepdims=True)
            dsk_out[...] = jnp.sum(dsk_ref[...], axis=1, keepdims=True)

    cur = lambda n: (0, jnp.minimum(n, nb - 1))
    col = pl.BlockSpec((HD, 1), lambda n: (0, 0))
    whole = lambda shape: pl.BlockSpec(shape, lambda n: (0,) * len(shape))
    return _run(
        body, [qkv_t, qkv_t, do_t, qg, kg, sinks, bias], hook, grid=(nb + 1,), name=name, semantics=("arbitrary",),
        in_specs=[pl.BlockSpec((QKV, CHUNK), cur),
                  pl.BlockSpec((QKV - KV0, CHUNK), lambda n: (KV0 // (QKV - KV0), jnp.clip(n - 1, 0, nb - 1))),
                  pl.BlockSpec((D, CHUNK), cur), col, col, pl.BlockSpec(memory_space=pltpu.SMEM), whole((NH, 2 * CHUNK, CHUNK))],
        out_specs=[pl.BlockSpec((QKV, CHUNK), lambda n: (0, jnp.maximum(n - 1, 0))), whole((HD, 1)), whole((HD, 1)),
                   whole((NH, 1)), whole((NH, 2 * CHUNK, CHUNK))],
        out_shape=[S((QKV, t), BF16), S((HD, 1), F32), S((HD, 1), F32), S((NH, 1), F32), S((NH, 2 * CHUNK, CHUNK), F32)],
        scratch_shapes=[pltpu.VMEM((QKV, CHUNK), F32), pltpu.VMEM((HD, CHUNK), F32), pltpu.VMEM((HD, 2 * CHUNK), F32),
                        pltpu.VMEM((NH, CHUNK), F32)])


class _Plain:
    def __init__(self, wg):
        self.full, self.grads = wg, {}

    def w(self, n):
        return self.full[n]

    def hook(self, host):
        return None

    def grad(self, n, pair):
        self.grads[n] = pair

    def small(self, g_rep):
        pass

    def sync(self, point):
        pass


def _local_step(x, target, rep, sch):
    bucket_row = jnp.asarray(_rel_tables().T.reshape(1, -1))
    bias = _relbias_fwd(rep["rel_bias"].T, bucket_row, "relbias_fwd").reshape(NH, 2 * CHUNK, CHUNK)
    bst = rep["sgu_b_s"][0].T
    ws = rep["sgu_w_s"][0]
    vgain = rep["sgu_v_gain"]
    qg, kg, sinks = rep["attn_q_gain"].reshape(HD, 1), rep["attn_k_gain"].reshape(HD, 1), rep["attn_sinks"][0]
    w_down = lambda l: sch.w("ffn_w_down%d" % l).reshape(D_FF, D)
    w_up = lambda l: sch.w("ffn_w_up%d" % l)
    cw = [sch.w("ffn_conv_w")[:, 3 * l:3 * l + 3] for l in range(2)]
    cb = [rep["ffn_conv_b"][l].reshape(8, 1, -1) for l in range(2)]
    mixg = [rep["mix_norm"][l:l + 1] for l in range(2)]
    ffng = [rep["ffn_norm"][l:l + 1] for l in range(2)]
    rows = lambda pair: tuple(g.reshape(N_DEV, -1, D) for g in pair)
    hk = sch.hook

    hn0 = _rmsnorm(x, mixg[0], "norm0")
    a0 = _mm_slot(hn0, sch.w("sgu_w_in"), BF16, "sgu_in", hk("sgu_in"))
    gated = _sgu_gate_fwd(a0, vgain, ws, bst, "sgu_gate", hk("sgu_gate"))
    h1, hn1 = _resid_mm(gated, sch.w("sgu_w_out").reshape(SGU_W, D), x, ffng[0], "norm", "sgu_out", hk("sgu_out"))
    sch.sync("before_ffn0")
    a_ff0, c_ff0, h2, hn2 = _ffn_fwd(hn1, h1, w_up(0), w_down(0), cw[0], cb[0], mixg[1], "norm", "ffn0_fwd", hk("ffn0_fwd"))
    qkv = _mm_t(hn2, sch.w("attn_w_qkv"), "qkv", hk("qkv"))
    o = _attn_fwd(qkv, qg, kg, sinks, bias, "attn", hk("attn"))
    h3, hn3 = _resid_mm(o, sch.w("attn_w_o").reshape(D, D), h2, ffng[1], "norm", "attn_out", hk("attn_out"), fm=True)
    a_ff1, c_ff1, dy, sq = _ffn_fwd(hn3, h3, w_up(1), w_down(1), cw[1], cb[1], target, "loss", "ffn1_fwd_loss", hk("ffn1_fwd_loss"))
    loss = (0.5 / D) * jnp.sum(sq[:, 0, 0])

    def ffn_bwd(dh, h_in, hn, a, c, l, tag):
        dc, g_down, g_down_b = _ffn_bwd1(dh, c, w_down(l), tag + "_bwd1", hk(tag + "_bwd1"))
        sch.grad("ffn_w_down%d" % l, rows((g_down, g_down_b)))
        da, dh_new, dgain, g_cw, g_cb = _ffn_bwd2(dc, a, w_up(l), cw[l], h_in, ffng[l], dh, tag + "_bwd2", hk(tag + "_bwd2"))
        sch.grad("ffn_w_up%d" % l, _dw_slot(hn, da, tag + "_dw_up", hk(tag + "_dw_up")))
        return dh_new, dgain, g_cw, g_cb.reshape(-1)

    dh, d_ffng1, g_cw1, g_cb1 = ffn_bwd(dy, h3, hn3, a_ff1, c_ff1, 1, "ffn1")
    do = _dx_rows_t(dh, sch.w("attn_w_o").reshape(D, D), "attn_do", hk("attn_do"))
    sch.grad("attn_w_o", rows(_dw_rows(o, dh, "dw_o", hk("dw_o"), fm=True)))
    dqkv, d_qg, d_kg, d_sk, d_bias = _attn_bwd(qkv, do, qg, kg, sinks, bias, "attn_bwd", hk("attn_bwd"))
    sch.grad("attn_w_qkv", tuple(g.reshape(N_DEV, -1, D) for g in _dw_rows(dqkv, hn2, "dw_qkv", hk("dw_qkv"), fm=True)))
    dh, d_mixg1 = _dx_slot_normbwd(dqkv, sch.w("attn_w_qkv").reshape(QKV, D), h2, mixg[1], dh, "dx_qkv", hk("dx_qkv"), fm=True,
                                   out_dtype=DH)
    d_relb = _relbias_bwd(d_bias.reshape(NH, -1), bucket_row, "relbias_bwd").T
    g_rep = {"attn_q_gain": d_qg.reshape(1, HD), "attn_k_gain": d_kg.reshape(1, HD), "attn_sinks": d_sk.reshape(1, NH),
             "rel_bias": d_relb}
    sch.small(g_rep)
    dh, d_ffng0, g_cw0, g_cb0 = ffn_bwd(dh, h1, hn1, a_ff0, c_ff0, 0, "ffn0")
    g_cw = jnp.concatenate([g_cw0, g_cw1], axis=1)
    sch.grad("ffn_conv_w", (g_cw, g_cw.astype(BF16)))
    g_ffn = {"ffn_norm": jnp.concatenate([d_ffng0, d_ffng1], axis=0), "ffn_conv_b": jnp.stack([g_cb0, g_cb1], axis=0)}
    sch.small(g_ffn)
    dgated = _dx_rows(dh, sch.w("sgu_w_out").reshape(SGU_W, D), SGU_W // 4, BF16, "sgu_dgated", hk("sgu_dgated"))
    sch.grad("sgu_w_out", rows(_dw_rows(gated, dh, "dw_sgu_out", hk("dw_sgu_out"))))
    da0, d_ws, d_bst, d_vgain = _sgu_gate_bwd(a0, dgated, vgain, ws, bst, "sgu_gate_bwd", hk("sgu_gate_bwd"))
    g_sgu = {"sgu_v_gain": d_vgain, "sgu_w_s": d_ws[None], "sgu_b_s": d_bst[:, :SGU_G].T[None]}
    sch.small(g_sgu)
    sch.grad("sgu_w_in", _dw_slot(hn0, da0, "dw_sgu_in", hk("dw_sgu_in")))
    sch.sync("after_dw")
    grad_x, d_mixg0 = _dx_slot_normbwd(da0, sch.w("sgu_w_in"), x, mixg[0], dh, "dx_sgu_in", hk("dx_sgu_in"))
    g_mix = {"mix_norm": jnp.concatenate([d_mixg0, d_mixg1], axis=0)}
    sch.small(g_mix)
    for g in (g_ffn, g_sgu, g_mix):
        g_rep.update(g)
    return loss, grad_x, g_rep


def _allgather(xs, name):
    nt = len(xs)

    def body(*refs):
        x_refs, o_refs = refs[:nt], refs[nt:2 * nt]
        send_sems, recv_sems, local_sems = refs[2 * nt:]
        x, y, c, chips = _place()
        me, sibling = (x, y, c), (x, y, 1 - c)

        def copy(t, k, block, to, src=None):
            px, py, pc = block
            dst = o_refs[t].at[4 * px + 2 * py + pc]
            return pltpu.make_async_remote_copy(
                src_ref=dst if src is None else src, dst_ref=dst, send_sem=send_sems.at[t, k], recv_sem=recv_sems.at[t, k],
                device_id=to, device_id_type=MESH)

        mine = [pltpu.make_async_copy(x_refs[t], o_refs[t].at[4 * x + 2 * y + c], local_sems.at[t]) for t in range(nt)]
        for cp in mine:
            cp.start()
        first = []
        for t in range(nt):
            first.append(copy(t, 0, me, sibling, src=x_refs[t]))
            first += [copy(t, 1 + j, me, (*chip, c), src=x_refs[t]) for j, chip in enumerate(chips)]
        for cp in first:
            cp.start()
        passed = []
        for j, chip in enumerate(chips):
            for t in range(nt):
                copy(t, 1 + j, (*chip, c), me).wait_recv()
                fwd = copy(t, 4 + j, (*chip, c), sibling)
                fwd.start()
                passed.append(fwd)
        for t in range(nt):
            copy(t, 0, sibling, me).wait_recv()
            for j, chip in enumerate(chips):
                copy(t, 4 + j, (*chip, 1 - c), me).wait_recv()
        for cp in first + passed:
            cp.wait_send()
        for cp in mine:
            cp.wait()

    return pl.pallas_call(
        body, name=name, in_specs=[ANY] * nt, out_specs=[ANY] * nt,
        out_shape=[S((N_DEV,) + a.shape, a.dtype) for a in xs],
        scratch_shapes=[pltpu.SemaphoreType.DMA((nt, 7)), pltpu.SemaphoreType.DMA((nt, 7)), pltpu.SemaphoreType.DMA((nt,))],
        compiler_params=pltpu.CompilerParams(has_side_effects=True))(*xs)


def _exchange(hook, name):
    comm = hook()
    ci, co = len(comm.inputs), len(comm.out_shapes)

    def body(*refs):
        cins, couts = refs[:ci], refs[ci:ci + co]
        send, recv = refs[-2:]
        comm.start(cins, couts, send, recv)
        comm.finish(cins, couts, send, recv)

    res = pl.pallas_call(
        body, name=name, in_specs=[ANY] * ci, out_specs=[ANY] * co, out_shape=comm.out_shapes,
        scratch_shapes=[pltpu.SemaphoreType.DMA((comm.n_sems,)), pltpu.SemaphoreType.DMA((comm.n_sems,))],
        input_output_aliases=dict(comm.aliases),
        compiler_params=pltpu.CompilerParams(has_side_effects=True))(*comm.inputs)
    hook(res)


def _row_tile(r):
    tr = r if r <= ROW_TILE or r % ROW_TILE else ROW_TILE
    assert r % tr == 0
    return tr


def _rs_partial(g32, sib, place, name):
    _, r, cdim = g32.shape
    tr = _row_tile(r)

    def body(place_ref, g_ref, s_ref, p_ref, own_ref):
        k = pl.program_id(1)
        tot = g_ref[...] + s_ref[...].astype(F32)
        p_ref[...] = tot.astype(BF16)

        @pl.when(k == place_ref[1])
        def _():
            own_ref[...] = tot

    grid_spec = pltpu.PrefetchScalarGridSpec(
        num_scalar_prefetch=1, grid=(r // tr, 4),
        in_specs=[pl.BlockSpec((None, None, tr, cdim), lambda i, k, pr: (k, pr[0], i, 0)),
                  pl.BlockSpec((None, tr, cdim), lambda i, k, pr: (k, i, 0))],
        out_specs=[pl.BlockSpec((None, tr, cdim), lambda i, k, pr: (k, i, 0)), pl.BlockSpec((tr, cdim), lambda i, k, pr: (i, 0))])
    return pl.pallas_call(
        body, grid_spec=grid_spec, name=name,
        out_shape=[S((4, r, cdim), BF16), S((r, cdim), F32)],
        compiler_params=_cp("parallel", "arbitrary"))(place, g32.reshape(4, 2, r, cdim), sib)


def _adamw_math(w, g, m, v):
    m = ADAM_B1 * m + (1.0 - ADAM_B1) * g
    v = ADAM_B2 * v + (1.0 - ADAM_B2) * (g * g)
    m_hat = m / (1.0 - ADAM_B1 ** ADAM_STEP)
    v_hat = v / (1.0 - ADAM_B2 ** ADAM_STEP)
    delta = -ADAM_LR * (m_hat / (jnp.sqrt(v_hat) + ADAM_EPS) + ADAM_WD * w)
    return delta, m, v


def _adamw_shard(owns, recvs, w, m, v, name, flipped=False):
    nl = w.shape[0]
    r, cdim = owns[0].shape
    tr = _row_tile(r)
    nr = r // tr

    def body(*refs):
        own_refs, recv_refs = refs[:nl], refs[nl:2 * nl]
        w_ref, m_ref, v_ref, g_out, d_out, m_out, v_out = refs[2 * nl:]
        layer = pl.program_id(0)
        g = None
        for l in range(nl):
            gl = own_refs[l][...] + recv_refs[l][0].astype(F32) + recv_refs[l][1].astype(F32) + recv_refs[l][2].astype(F32)
            g = gl if g is None else jnp.where(layer == l, gl, g)
        if flipped:
            g = g.T
        g_out[...] = g
        d_out[...], m_out[...], v_out[...] = _adamw_math(w_ref[...], g, m_ref[...], v_ref[...])

    park = lambda l: (lambda layer, i: (jnp.where(layer == l, i, jnp.where(layer < l, 0, nr - 1)), 0))
    park3 = lambda l: (lambda layer, i: (0, jnp.where(layer == l, i, jnp.where(layer < l, 0, nr - 1)), 0))
    if flipped:
        row = pl.BlockSpec((None, cdim, tr), lambda layer, i: (layer, 0, i))
    else:
        row = pl.BlockSpec((None, tr, cdim), lambda layer, i: (layer, i, 0))
    return pl.pallas_call(
        body, grid=(nl, nr), name=name,
        in_specs=[pl.BlockSpec((tr, cdim), park(l)) for l in range(nl)] + [pl.BlockSpec((3, tr, cdim), park3(l)) for l in range(nl)]
        + [row, row, row],
        out_specs=[row] * 4, out_shape=[S(w.shape, F32)] * 4,
        compiler_params=_cp("arbitrary", "arbitrary"))(*owns, *recvs, w, m, v)


def _adamw_small(galls, ws, ms, vs, name):
    n = len(galls)

    def body(*refs):
        g_refs, w_refs, m_refs, v_refs, outs = refs[:n], refs[n:2 * n], refs[2 * n:3 * n], refs[3 * n:4 * n], refs[4 * n:]
        for i in range(n):
            g = g_refs[i][0].astype(F32)
            for s in range(1, N_DEV):
                g = g + g_refs[i][s].astype(F32)
            outs[i][...] = g
            outs[n + i][...], outs[2 * n + i][...], outs[3 * n + i][...] = _adamw_math(w_refs[i][...], g, m_refs[i][...], v_refs[i][...])

    res = pl.pallas_call(body, out_shape=[S(a.shape, F32) for a in ws] * 4, name=name)(*galls, *ws, *ms, *vs)
    return [res[k * n:(k + 1) * n] for k in range(4)]


REPLICATED = ["mix_norm", "ffn_norm", "sgu_v_gain", "sgu_w_s", "sgu_b_s", "attn_q_gain", "attn_k_gain", "attn_sinks", "rel_bias",
              "ffn_conv_b"]
WEIGHTS = ["mix_norm", "ffn_norm", "sgu_w_in", "sgu_v_gain", "sgu_w_s", "sgu_b_s", "sgu_w_out", "attn_w_qkv", "attn_q_gain",
           "attn_k_gain", "attn_sinks", "attn_w_o", "rel_bias", "ffn_w_up", "ffn_conv_w", "ffn_conv_b", "ffn_w_down"]
SMALL = ["g_" + n for n in REPLICATED]
BF16_TRANSIT = {"sgu_w_s"}
SMALL_ATTN = ["g_attn_q_gain", "g_attn_k_gain", "g_attn_sinks", "g_rel_bias"]
SMALL_FFN = ["g_ffn_norm", "g_ffn_conv_b"]
SMALL_SGU = ["g_sgu_v_gain", "g_sgu_w_s", "g_sgu_b_s"]

GATHER_FIRST = ["sgu_w_in", "ffn_conv_w"]
UP0_SPLIT, UP1_SPLITS = 352, (256, 512)
PLAN = {
    "sgu_in": [("ag1", "sgu_w_out"), ("ag1", "ffn_w_up0", (0, UP0_SPLIT))],
    "sgu_gate": [("ag2", "sgu_w_out"), ("ag1", "ffn_w_up0", (UP0_SPLIT, D))],
    "sgu_out": [("ag2", "ffn_w_up0"), ("ag1", "ffn_w_down0")],
    "before_ffn0": [("ag2", "ffn_w_down0")],
    "ffn0_fwd": [("agd", "attn_w_qkv"), ("ag1", "attn_w_o"), ("ag1", "ffn_w_down1"), ("ag1", "ffn_w_up1", (0, UP1_SPLITS[0]))],
    "qkv": [("ag2", "attn_w_o"), ("ag2", "ffn_w_down1"), ("ag1", "ffn_w_up1", UP1_SPLITS)],
    "attn": [("ag1", "ffn_w_up1", (UP1_SPLITS[1], D))],
    "attn_out": [("ag2", "ffn_w_up1")],
    "ffn1_bwd2": [("rs1", "ffn_w_down1")],
    "ffn1_dw_up": [("rs2", "ffn_w_down1")],
    "attn_do": [("rs1", "ffn_w_up1")],
    "attn_bwd": [("rs2", "ffn_w_up1"), ("rs1", "attn_w_o")],
    "dw_qkv": [("rs2", "attn_w_o")],
    "dx_qkv": [("rs1", "attn_w_qkv")],
    "ffn0_bwd1": [("rs2", "attn_w_qkv")] + [("ag1", n) for n in SMALL_ATTN],
    "ffn0_bwd2": [("rs1", "ffn_w_down0")] + [("ag2", n) for n in SMALL_ATTN],
    "ffn0_dw_up": [("rs2", "ffn_w_down0")],
    "sgu_dgated": [("rs1", "ffn_w_up0")] + [("ag1", n) for n in SMALL_FFN],
    "dw_sgu_out": [("ag2", n) for n in SMALL_FFN],
    "sgu_gate_bwd": [("rs2", "ffn_w_up0"), ("rs1", "sgu_w_out")],
    "dw_sgu_in": [("rs2", "sgu_w_out")] + [("ag1", n) for n in SMALL_SGU],
    "after_dw": [("rs1", "sgu_w_in"), ("rs1", "ffn_conv_w")] + [("ag2", n) for n in SMALL_SGU],
    "dx_sgu_in": [("rs2", "sgu_w_in"), ("rs2", "ffn_conv_w")],
    "last": [("agd", "g_mix_norm")],
}


class _Overlap:
    def __init__(self, shard, place):
        self.shard, self.place = shard, place
        self.part, self.full = {}, {}
        self.grads, self.sib, self.own, self.recv = {}, {}, {}, {}

    def w(self, n):
        return self.full[n]

    def grad(self, n, pair):
        self.grads[n] = pair

    def small(self, g_rep):
        self.shard.update(("g_" + n, a.astype(BF16) if n in BF16_TRANSIT else a) for n, a in _views2d(g_rep).items())

    def sync(self, point):
        _exchange(self.hook(point), point)

    def chip_sums(self, n):
        sums, self.own[n] = _rs_partial(self.grads[n][0], self.sib.pop(n), self.place, "rs_partial_" + n)
        return sums

    def hook(self, host):
        ops = PLAN.get(host)
        if not ops:
            return None
        where = {"ag1": self.part, "ag2": self.full, "agd": self.full, "rs1": self.sib, "rs2": self.recv}
        idx = []

        def hook(results=None):
            if results is not None:
                for (kind, n, *_), i in zip(ops, idx):
                    where[kind][n] = results[i]
                return None
            comm = _Comm()
            for kind, n, *rows in ops:
                arr = {"ag1": lambda: self.shard[n], "agd": lambda: self.shard[n], "ag2": lambda: self.part.pop(n),
                       "rs1": lambda: self.grads[n][1], "rs2": lambda: self.chip_sums(n)}[kind]()
                idx.append(comm.add(kind, arr, *rows, into=self.part.pop(n) if rows and rows[0][0] else None))
            return comm

        return hook


TRANSPOSED = {"attn_w_qkv"}
PHYSICAL_T = {"ffn_w_up"}
SHARDED = {
    "sgu_w_in": ["sgu_w_in"], "sgu_w_out": ["sgu_w_out"], "attn_w_qkv": ["attn_w_qkv"], "attn_w_o": ["attn_w_o"],
    "ffn_w_up": ["ffn_w_up0", "ffn_w_up1"], "ffn_w_down": ["ffn_w_down0", "ffn_w_down1"], "ffn_conv_w": ["ffn_conv_w"],
}


def _send_views(w):
    out = {"ffn_conv_w": w["ffn_conv_w"].reshape(6, -1)}
    for name, parts in SHARDED.items():
        if name != "ffn_conv_w":
            out.update((p, (w[name][l].T if name in TRANSPOSED else w[name][l]).astype(BF16)) for l, p in enumerate(parts))
    return out


def _views2d(d):
    return {n: d[n].reshape(-1, d[n].shape[-1]) for n in REPLICATED if n in d}


def kernel(x, mix_norm, ffn_norm, sgu_w_in, sgu_v_gain, sgu_w_s, sgu_b_s, sgu_w_out, attn_w_qkv, attn_q_gain, attn_k_gain, attn_sinks, attn_w_o, rel_bias, ffn_w_up, ffn_conv_w, ffn_conv_b, ffn_w_down, loss_target, m_mix_norm, m_ffn_norm, m_sgu_w_in, m_sgu_v_gain, m_sgu_w_s, m_sgu_b_s, m_sgu_w_out, m_attn_w_qkv, m_attn_q_gain, m_attn_k_gain, m_attn_sinks, m_attn_w_o, m_rel_bias, m_ffn_w_up, m_ffn_conv_w, m_ffn_conv_b, m_ffn_w_down, v_mix_norm, v_ffn_norm, v_sgu_w_in, v_sgu_v_gain, v_sgu_w_s, v_sgu_b_s, v_sgu_w_out, v_attn_w_qkv, v_attn_q_gain, v_attn_k_gain, v_attn_sinks, v_attn_w_o, v_rel_bias, v_ffn_w_up, v_ffn_conv_w, v_ffn_conv_b, v_ffn_w_down):
    w = dict(zip(WEIGHTS, (mix_norm, ffn_norm, sgu_w_in, sgu_v_gain, sgu_w_s, sgu_b_s, sgu_w_out, attn_w_qkv, attn_q_gain, attn_k_gain,
                           attn_sinks, attn_w_o, rel_bias, ffn_w_up, ffn_conv_w, ffn_conv_b, ffn_w_down)))
    m = dict(zip(WEIGHTS, (m_mix_norm, m_ffn_norm, m_sgu_w_in, m_sgu_v_gain, m_sgu_w_s, m_sgu_b_s, m_sgu_w_out, m_attn_w_qkv, m_attn_q_gain,
                           m_attn_k_gain, m_attn_sinks, m_attn_w_o, m_rel_bias, m_ffn_w_up, m_ffn_conv_w, m_ffn_conv_b, m_ffn_w_down)))
    v = dict(zip(WEIGHTS, (v_mix_norm, v_ffn_norm, v_sgu_w_in, v_sgu_v_gain, v_sgu_w_s, v_sgu_b_s, v_sgu_w_out, v_attn_w_qkv, v_attn_q_gain,
                           v_attn_k_gain, v_attn_sinks, v_attn_w_o, v_rel_bias, v_ffn_w_up, v_ffn_conv_w, v_ffn_conv_b, v_ffn_w_down)))
    rep = {n: w[n] for n in REPLICATED}

    xi, yi, ci = lax.axis_index("x"), lax.axis_index("y"), lax.axis_index("c")
    place = jnp.stack([ci, 2 * xi + yi]).astype(jnp.int32)
    sch = _Overlap(_send_views(w), place)
    sch.full.update(zip(GATHER_FIRST, _allgather([sch.shard[n] for n in GATHER_FIRST], "gather_first")))

    loss, grad_x, g_rep = _local_step(x[0], loss_target[0], rep, sch)
    loss = lax.psum(loss, ("x", "y", "c"))
    sch.sync("last")

    out = [{}, {}, {}, {}]
    for name, parts in SHARDED.items():
        flip = (lambda a: jnp.swapaxes(a, -1, -2)) if name in TRANSPOSED | PHYSICAL_T else (lambda a: a)
        shape = flip(w[name]).shape
        as3d = lambda a: flip(a).reshape(len(parts), -1, shape[-1])
        res = _adamw_shard([sch.own[p] for p in parts], [sch.recv[p] for p in parts], as3d(w[name]), as3d(m[name]), as3d(v[name]),
                           "adamw_" + name, flipped=name in PHYSICAL_T)
        for o, r in zip(out, res):
            o[name] = flip(r.reshape(shape))
    small = _adamw_small([sch.full[n] for n in SMALL], *[list(_views2d(d).values()) for d in (rep, m, v)], "adamw_small")
    for o, res in zip(out, small):
        o.update((n, r.reshape(w[n].shape)) for n, r in zip(REPLICATED, res))

    return (loss, grad_x[None], *[out[0][n] for n in WEIGHTS], *[out[1][n] for n in WEIGHTS],
            *[out[2][n] for n in WEIGHTS], *[out[3][n] for n in WEIGHTS])
```

```python
import functools
import math

import numpy as np
import jax
import jax.numpy as jnp
from jax import lax
from jax.experimental import pallas as pl
from jax.experimental.pallas import tpu as pltpu

F32 = jnp.float32
BF16 = jnp.bfloat16
DH = jnp.bfloat16
S = jax.ShapeDtypeStruct

D = 1024
CHUNK = 128
SGU_W = 2048
SGU_G = 16
HD = 64
NH = 16
NKV = 4
KVG = 4
D_FF = 2816
REL_BUCKETS = 32
REL_MAX_DIST = 128
EPS = 1e-6
N_DEV = 8
MESH = pl.DeviceIdType.MESH

ADAM_LR = 0.001
ADAM_B1 = 0.9
ADAM_B2 = 0.999
ADAM_EPS = 1e-08
ADAM_WD = 0.01
ADAM_STEP = 10

ROW_TILE = 512
HALO = 8
FFN_ROWS = 256


def _tm(t):
    return min(ROW_TILE, t)


def _cp(*sem):
    return pltpu.CompilerParams(dimension_semantics=sem)


ANY = pl.BlockSpec(memory_space=pl.ANY)


def _place():
    x, y, c = lax.axis_index("x"), lax.axis_index("y"), lax.axis_index("c")
    return x, y, c, [(1 - x, y), (x, 1 - y), (1 - x, 1 - y)]


class _Comm:
    SEMS = {"ag1": 5, "ag2": 3, "rs1": 4, "rs2": 3, "agd": 8}

    def __init__(self):
        self.inputs, self.out_shapes, self.aliases, self.ops, self.n_sems = [], [], {}, [], 0

    def add(self, kind, arr, rows=None, into=None):
        lead = {"ag1": N_DEV, "agd": N_DEV, "ag2": None, "rs1": 4, "rs2": 3}[kind]
        shape = arr.shape if lead is None else (lead,) + arr.shape[(0 if kind in ("ag1", "agd") else 1):]
        if kind == "ag2":
            self.aliases[len(self.inputs)] = len(self.out_shapes)
        self.ops.append((kind, len(self.inputs), len(self.out_shapes), self.n_sems, rows))
        self.inputs.append(arr)
        if into is not None:
            self.aliases[len(self.inputs)] = len(self.out_shapes)
            self.inputs.append(into)
        self.out_shapes.append(S(shape, arr.dtype))
        self.n_sems += self.SEMS[kind]
        return len(self.out_shapes) - 1

    def _copies(self, ins, outs, send, recv):
        x, y, c, chips = _place()
        me, sibling = (x, y, c), (x, y, 1 - c)
        slot = lambda px, py, pc: 4 * px + 2 * py + pc
        sends, recvs, local = [], [], []

        def rc(src, dst, k, to):
            return lambda: pltpu.make_async_remote_copy(src_ref=src(), dst_ref=dst(), send_sem=send.at[k], recv_sem=recv.at[k],
                                                        device_id=to, device_id_type=MESH)

        for kind, ii, oi, b, rows in self.ops:
            src, dst = ins[ii], outs[oi]
            at = lambda ref, i: (lambda: ref.at[i])
            if kind == "ag1":
                part = slice(None) if rows is None else pl.ds(rows[0], rows[1] - rows[0])
                to = lambda i, d=dst, p=part: (lambda: d.at[i, p])
                whole, mine = (lambda s=src, p=part: s.at[p]), to(slot(*me))
                sends.append(rc(whole, mine, b, sibling))
                recvs.append(rc(whole, to(slot(x, y, 1 - c)), b, me))
                for j, chip in enumerate(chips):
                    sends.append(rc(whole, mine, b + 1 + j, (*chip, c)))
                    recvs.append(rc(whole, to(slot(*chip, c)), b + 1 + j, me))
                local.append(lambda s=whole, m=mine, k=b + 4: pltpu.make_async_copy(s(), m(), send.at[k]))
            elif kind == "ag2":
                for j, chip in enumerate(chips):
                    sends.append(rc(at(dst, slot(*chip, c)), at(dst, slot(*chip, c)), b + j, sibling))
                    recvs.append(rc(at(dst, slot(*chip, 1 - c)), at(dst, slot(*chip, 1 - c)), b + j, me))
            elif kind == "agd":
                whole, mine = (lambda s=src: s), at(dst, slot(*me))
                flip = lambda v, bit: 1 - v if bit else v
                for k in range(1, N_DEV):
                    peer = (flip(x, k >> 2), flip(y, (k >> 1) & 1), flip(c, k & 1))
                    sends.append(rc(whole, mine, b + k - 1, peer))
                    recvs.append(rc(whole, at(dst, slot(*peer)), b + k - 1, me))
                local.append(lambda s=src, m=mine, k=b + 7: pltpu.make_async_copy(s, m(), send.at[k]))
            elif kind == "rs1":
                for k in range(4):
                    sends.append(rc(at(src, 2 * k + (1 - c)), at(dst, k), b + k, sibling))
                    recvs.append(rc(at(src, 2 * k + c), at(dst, k), b + k, me))
            else:
                for j, (px, py) in enumerate(chips):
                    sends.append(rc(at(src, 2 * px + py), at(dst, j), b + j, (px, py, c)))
                    recvs.append(rc(at(src, 2 * px + py), at(dst, j), b + j, me))
        return sends, recvs, local

    def start(self, ins, outs, send, recv):
        sends, _, local = self._copies(ins, outs, send, recv)
        for make in local + sends:
            make().start()

    def finish(self, ins, outs, send, recv):
        sends, recvs, local = self._copies(ins, outs, send, recv)
        for make in recvs:
            make().wait_recv()
        for make in sends:
            make().wait_send()
        for make in local:
            make().wait()


def _run(body, args, hook, *, grid, in_specs, out_specs, out_shape, name, semantics, scratch_shapes=(), aliases=None):
    comm = hook() if hook is not None else None
    aliases = dict(aliases or {})
    if comm is None:
        return pl.pallas_call(body, grid=grid, in_specs=in_specs, out_specs=out_specs, out_shape=out_shape, name=name,
                              scratch_shapes=list(scratch_shapes), input_output_aliases=aliases,
                              compiler_params=_cp(*semantics))(*args)
    single = not isinstance(out_shape, (list, tuple))
    out_shapes = [out_shape] if single else list(out_shape)
    out_specs_l = [out_specs] if single else list(out_specs)
    n_in, n_out, n_scr, ci, co = len(args), len(out_shapes), len(scratch_shapes), len(comm.inputs), len(comm.out_shapes)

    def wrapped(*refs):
        ins, cins = refs[:n_in], refs[n_in:n_in + ci]
        outs, couts = refs[n_in + ci:n_in + ci + n_out], refs[n_in + ci + n_out:n_in + ci + n_out + co]
        scr = refs[n_in + ci + n_out + co:n_in + ci + n_out + co + n_scr]
        send, recv = refs[-2:]
        first = functools.reduce(lambda a, b: a & b, [pl.program_id(a) == 0 for a in range(len(grid))])
        last = functools.reduce(lambda a, b: a & b, [pl.program_id(a) == g - 1 for a, g in enumerate(grid)])

        @pl.when(first)
        def _():
            comm.start(cins, couts, send, recv)

        body(*ins, *outs, *scr)

        @pl.when(last)
        def _():
            comm.finish(cins, couts, send, recv)

    res = pl.pallas_call(
        wrapped, grid=grid, in_specs=list(in_specs) + [ANY] * ci, out_specs=out_specs_l + [ANY] * co,
        out_shape=out_shapes + comm.out_shapes, name=name,
        scratch_shapes=list(scratch_shapes) + [pltpu.SemaphoreType.DMA((comm.n_sems,)), pltpu.SemaphoreType.DMA((comm.n_sems,))],
        input_output_aliases={**aliases, **{n_in + k: n_out + v for k, v in comm.aliases.items()}},
        compiler_params=pltpu.CompilerParams(dimension_semantics=("arbitrary",) * len(grid), has_side_effects=True))(*args, *comm.inputs)
    hook(res[n_out:])
    return res[0] if single else list(res[:n_out])


def _dot(a, b):
    return jnp.dot(a, b, preferred_element_type=F32)


def _dot_nt(a, b):
    return lax.dot_general(a, b, (((1,), (1,)), ((), ())), preferred_element_type=F32)


def _dot_tn(a, b):
    return lax.dot_general(a, b, (((0,), (0,)), ((), ())), preferred_element_type=F32)


def _gelu(x):
    return 0.5 * x * (1.0 + lax.erf(x * (2.0 ** -0.5)))


def _gelu_and_grad(x):
    cdf = 0.5 * (1.0 + lax.erf(x * (2.0 ** -0.5)))
    return x * cdf, cdf + x * jnp.exp(-0.5 * x * x) * (1.0 / math.sqrt(2.0 * math.pi))


def _sigmoid(x):
    return 1.0 / (1.0 + jnp.exp(-x))


def _rstd(x):
    return lax.rsqrt(jnp.mean(x * x, axis=-1, keepdims=True) + EPS)


def _rel_tables():
    q = np.arange(CHUNK)[:, None] + CHUNK
    k = np.arange(2 * CHUNK)[None, :]
    dist = q - k
    n = np.maximum(dist, 0)
    max_exact = REL_BUCKETS // 2
    large = max_exact + (np.log(np.maximum(n, 1).astype(np.float32) / max_exact)
                         / math.log(REL_MAX_DIST / max_exact) * (REL_BUCKETS - max_exact)).astype(np.int32)
    large = np.minimum(large, REL_BUCKETS - 1)
    return np.where(n < max_exact, n, large).astype(np.int32)


def _rmsnorm(x, gain, name):
    t = x.shape[0]
    tm = _tm(t)

    def body(x_ref, g_ref, o_ref):
        xv = x_ref[...]
        o_ref[...] = (xv * _rstd(xv) * g_ref[...]).astype(BF16)

    return pl.pallas_call(
        body, grid=(t // tm,), name=name,
        in_specs=[pl.BlockSpec((tm, D), lambda i: (i, 0)), pl.BlockSpec((1, D), lambda i: (0, 0))],
        out_specs=pl.BlockSpec((tm, D), lambda i: (i, 0)),
        out_shape=S((t, D), BF16), compiler_params=_cp("parallel"))(x, gain)


def _resident(shape):
    zeros = (0,) * len(shape)
    return pl.BlockSpec(shape, lambda *_: zeros, pipeline_mode=pl.Buffered(1))


def _mm_slot(hn, wg, out_dtype, name, hook=None):
    t, k = hn.shape
    ns, _, n = wg.shape
    tm = _tm(t)

    def body(a_ref, w_ref, o_ref):
        a = a_ref[...]
        for s in range(ns):
            o_ref[s] = _dot(a, w_ref[s]).astype(out_dtype)

    return _run(
        body, [hn, wg], hook, grid=(t // tm,), name=name, semantics=("parallel",),
        in_specs=[pl.BlockSpec((tm, k), lambda i: (i, 0)), _resident(wg.shape)],
        out_specs=pl.BlockSpec((ns, tm, n), lambda i: (0, i, 0)), out_shape=S((ns, t, n), out_dtype))


def _mm_t(hn, wt, name, hook=None):
    t, k = hn.shape
    ns, n, _ = wt.shape
    tm = _tm(t)

    def body(a_ref, w_ref, o_ref):
        a = a_ref[...]
        for s in range(ns):
            o_ref[s * n:(s + 1) * n, :] = _dot_nt(w_ref[s], a)

    return _run(
        body, [hn, wt], hook, grid=(t // tm,), name=name, semantics=("parallel",),
        in_specs=[pl.BlockSpec((tm, k), lambda i: (i, 0)), _resident(wt.shape)],
        out_specs=pl.BlockSpec((ns * n, tm), lambda i: (0, i)), out_shape=S((ns * n, t), F32))


def _conv3(a, prev, cw, cb, tm):
    ext = jnp.concatenate([prev, a], axis=0)
    return cw[2:3] * a + cw[1:2] * ext[HALO - 1:HALO - 1 + tm] + cw[0:1] * ext[HALO - 2:HALO - 2 + tm] + cb


def _ffn_fwd(hn, h, wup, wdown, cw, cb, extra, mode, name, hook=None):
    t, k = hn.shape
    n = wup.shape[-1]
    nh = wup.shape[0] // 2
    tm = min(FFN_ROWS, t)
    ni = t // tm

    def body(a_ref, h_ref, wu_ref, wd_ref, cw_ref, cb_ref, e_ref, as_ref, cs_ref, o1_ref, o2_ref, carry):
        i = pl.program_id(0)

        @pl.when(i == 0)
        def _():
            carry[...] = jnp.zeros_like(carry)

        a = a_ref[...]
        acc = h_ref[...]
        nxt = (_dot(a, wu_ref[0]), _dot(a, wu_ref[nh]))
        for j in range(nh):
            ag, av = nxt
            if j + 1 < nh:
                nxt = (_dot(a, wu_ref[j + 1]), _dot(a, wu_ref[nh + j + 1]))
            as_ref[j] = ag.astype(BF16)
            as_ref[nh + j] = av.astype(BF16)
            cg = _conv3(ag, carry[j], cw_ref[j], cb_ref[j], tm)
            cv = _conv3(av, carry[nh + j], cw_ref[nh + j], cb_ref[nh + j], tm)
            carry[j] = ag[tm - HALO:]
            carry[nh + j] = av[tm - HALO:]
            cs_ref[j] = cg.astype(BF16)
            cs_ref[nh + j] = cv.astype(BF16)
            act = (cg * _sigmoid(cg) * cv).astype(BF16)
            acc = acc + _dot(act, wd_ref[j * n:(j + 1) * n, :])
        if mode == "norm":
            o1_ref[...] = acc
            o2_ref[...] = (acc * _rstd(acc) * e_ref[...]).astype(BF16)
        else:
            err = acc - e_ref[...]
            o1_ref[...] = (err * (1.0 / D)).astype(o1_ref.dtype)
            o2_ref[...] = jnp.full(o2_ref.shape, jnp.sum(err * err), F32)

    row = pl.BlockSpec((tm, D), lambda i: (i, 0))
    if mode == "norm":
        e_spec, o2_spec, o2_shape = pl.BlockSpec((1, D), lambda i: (0, 0)), row, S((t, D), BF16)
    else:
        e_spec, o2_spec, o2_shape = row, pl.BlockSpec((None, 8, 128), lambda i: (i, 0, 0)), S((ni, 8, 128), F32)
    aspec = pl.BlockSpec((2 * nh, tm, n), lambda i: (0, i, 0))
    return _run(
        body, [hn, h, wup, wdown, cw, cb, extra], hook, grid=(ni,), name=name, semantics=("arbitrary",),
        in_specs=[pl.BlockSpec((tm, k), lambda i: (i, 0)), row, _resident(wup.shape), _resident(wdown.shape),
                  _resident(cw.shape), _resident(cb.shape), e_spec],
        out_specs=[aspec, aspec, row, o2_spec],
        out_shape=[S((2 * nh, t, n), BF16), S((2 * nh, t, n), BF16), S((t, D), F32 if mode == "norm" else DH), o2_shape],
        scratch_shapes=[pltpu.VMEM((2 * nh, HALO, n), F32)])


def _tril_mask():
    r = lax.broadcasted_iota(jnp.int32, (CHUNK, CHUNK), 0)
    c = lax.broadcasted_iota(jnp.int32, (CHUNK, CHUNK), 1)
    return r >= c


def _sgu_gate_fwd(a_s, vgain, ws, bst, name, hook=None):
    t = a_s.shape[1]
    sw = a_s.shape[2]
    gps = sw // CHUNK

    def body(a_ref, vg_ref, ws_ref, b_ref, o_ref):
        v = _gelu(jnp.concatenate([a_ref[4 + s].astype(F32) for s in range(4)], axis=1))
        vn = (v * _rstd(v) * vg_ref[...]).astype(BF16)
        tri = _tril_mask()
        for g in range(SGU_G):
            w = jnp.where(tri, ws_ref[g], 0.0).astype(BF16)
            sg = _dot(w, vn[:, g * CHUNK:(g + 1) * CHUNK]) + b_ref[:, g:g + 1]
            lo = (g % gps) * CHUNK
            u = _gelu(a_ref[g // gps, :, lo:lo + CHUNK].astype(F32))
            o_ref[g // gps, :, lo:lo + CHUNK] = (u * sg).astype(BF16)

    return _run(
        body, [a_s, vgain, ws, bst], hook, grid=(t // CHUNK,), name=name, semantics=("parallel",),
        in_specs=[pl.BlockSpec((8, CHUNK, sw), lambda n: (0, n, 0)), pl.BlockSpec((1, SGU_W), lambda n: (0, 0)),
                  pl.BlockSpec((SGU_G, CHUNK, CHUNK), lambda n: (0, 0, 0)), pl.BlockSpec((CHUNK, SGU_G), lambda n: (0, 0))],
        out_specs=pl.BlockSpec((4, CHUNK, sw), lambda n: (0, n, 0)), out_shape=S((4, t, sw), BF16))


def _resid_mm(a_s, w, resid, extra, mode, name, hook=None, fm=False):
    nk, t, kc = (1, a_s.shape[1], a_s.shape[0]) if fm else a_s.shape
    tm = _tm(t)
    ni = t // tm

    def body(a_ref, w_ref, r_ref, e_ref, o1_ref, o2_ref):
        h = r_ref[...]
        if fm:
            h = h + _dot_tn(a_ref[...], w_ref[...])
        for j in range(0 if fm else nk):
            h = h + _dot(a_ref[j], w_ref[j * kc:(j + 1) * kc, :])
        if mode == "norm":
            o1_ref[...] = h
            o2_ref[...] = (h * _rstd(h) * e_ref[...]).astype(BF16)
        else:
            err = h - e_ref[...]
            o1_ref[...] = (err * (1.0 / D)).astype(o1_ref.dtype)
            o2_ref[...] = jnp.full(o2_ref.shape, jnp.sum(err * err), F32)

    row = pl.BlockSpec((tm, D), lambda i: (i, 0))
    if mode == "norm":
        e_spec, o2_spec, o2_shape = pl.BlockSpec((1, D), lambda i: (0, 0)), row, S((t, D), BF16)
    else:
        e_spec, o2_spec, o2_shape = row, pl.BlockSpec((None, 8, 128), lambda i: (i, 0, 0)), S((ni, 8, 128), F32)
    return _run(
        body, [a_s, w, resid, extra], hook, grid=(ni,), name=name, semantics=("parallel",),
        in_specs=[pl.BlockSpec((kc, tm), lambda i: (0, i)) if fm else pl.BlockSpec((nk, tm, kc), lambda i: (0, i, 0)),
                  _resident(w.shape), row, e_spec],
        out_specs=[row, o2_spec], out_shape=[S((t, D), F32 if mode == "norm" else DH), o2_shape])


def _relbias_fwd(rel_bias_t, bucket_row, name):
    nb = bucket_row.shape[1]

    def body(rb_ref, bk_ref, o_ref):
        onehot = (lax.broadcasted_iota(jnp.int32, (REL_BUCKETS, nb), 0) == bk_ref[...]).astype(F32)
        o_ref[...] = jnp.dot(rb_ref[...], onehot, precision=lax.Precision.HIGHEST, preferred_element_type=F32)

    return pl.pallas_call(body, out_shape=S((NH, nb), F32), name=name)(rel_bias_t, bucket_row)


def _relbias_bwd(dbias, bucket_row, name):
    nb = bucket_row.shape[1]

    def body(db_ref, bk_ref, o_ref):
        onehot = (lax.broadcasted_iota(jnp.int32, (REL_BUCKETS, nb), 0) == bk_ref[...]).astype(F32)
        o_ref[...] = lax.dot_general(db_ref[...], onehot, (((1,), (1,)), ((), ())),
                                     precision=lax.Precision.HIGHEST, preferred_element_type=F32)

    return pl.pallas_call(body, out_shape=S((NH, REL_BUCKETS), F32), name=name)(dbias, bucket_row)


QKV = D + 2 * NKV * HD
KV0 = D


def _rstd_rows(x):
    return lax.rsqrt(jnp.mean(x * x, axis=0, keepdims=True) + EPS)


def _attn_valid(n):
    kj = lax.broadcasted_iota(jnp.int32, (2 * CHUNK, CHUNK), 0)
    qi = lax.broadcasted_iota(jnp.int32, (2 * CHUNK, CHUNK), 1)
    dist = qi + CHUNK - kj
    return (dist >= 0) & (dist < CHUNK) & ((n > 0) | (kj >= CHUNK))


def _attn_band(cur_ref, prev_ref, row):
    return jnp.concatenate([prev_ref[row - KV0:row - KV0 + HD, :], cur_ref[row:row + HD, :]], axis=1)


def _attn_probs(kn_tok, qn, bias, valid, sink):
    s = _dot(kn_tok, qn) * (HD ** -0.5) + bias
    s = jnp.where(valid, s, -jnp.inf)
    m = jnp.maximum(jnp.max(s, axis=0, keepdims=True), sink)
    p = jnp.exp(s - m)
    psink = jnp.exp(sink - m)
    inv = 1.0 / (jnp.sum(p, axis=0, keepdims=True) + psink)
    return p * inv, psink * inv


def _attn_fwd(qkv_t, qg, kg, sinks, bias, name, hook=None):
    t = qkv_t.shape[1]

    def body(cur_ref, prev_ref, qg_ref, kg_ref, sink_ref, bias_ref, o_ref):
        n = pl.program_id(0)
        valid = _attn_valid(n)
        ks = [_attn_band(cur_ref, prev_ref, KV0 + HD * h) for h in range(NKV)]
        kn_toks = [(k * _rstd_rows(k) * kg_ref[...]).astype(BF16).T for k in ks]
        vbs = [_attn_band(cur_ref, prev_ref, KV0 + HD * (NKV + h)).astype(BF16) for h in range(NKV)]
        qs = [cur_ref[HD * hq:HD * (hq + 1), :] for hq in range(NH)]
        qns = [(q * _rstd_rows(q) * qg_ref[...]).astype(BF16) for q in qs]
        ps = [_attn_probs(kn_toks[hq // KVG], qns[hq], bias_ref[hq], valid, sink_ref[hq])[0] for hq in range(NH)]
        for hq in range(NH):
            o_ref[HD * hq:HD * (hq + 1), :] = _dot(vbs[hq // KVG], ps[hq].astype(BF16)).astype(BF16)

    col = pl.BlockSpec((HD, 1), lambda n: (0, 0))
    return _run(
        body, [qkv_t, qkv_t, qg, kg, sinks, bias], hook, grid=(t // CHUNK,), name=name, semantics=("parallel",),
        in_specs=[pl.BlockSpec((QKV, CHUNK), lambda n: (0, n)),
                  pl.BlockSpec((QKV - KV0, CHUNK), lambda n: (KV0 // (QKV - KV0), jnp.maximum(n - 1, 0))),
                  col, col, pl.BlockSpec(memory_space=pltpu.SMEM), pl.BlockSpec((NH, 2 * CHUNK, CHUNK), lambda n: (0, 0, 0))],
        out_specs=pl.BlockSpec((D, CHUNK), lambda n: (0, n)), out_shape=S((D, t), BF16))


def _dx_rows(dh, w, kc, out_dtype, name, hook=None):
    t = dh.shape[0]
    nk = w.shape[0] // kc
    tm = _tm(t)

    def body(d_ref, w_ref, o_ref):
        dhb = d_ref[...].astype(BF16)
        for j in range(nk):
            o_ref[j] = _dot_nt(dhb, w_ref[j * kc:(j + 1) * kc, :]).astype(out_dtype)

    return _run(
        body, [dh, w], hook, grid=(t // tm,), name=name, semantics=("parallel",),
        in_specs=[pl.BlockSpec((tm, D), lambda i: (i, 0)), _resident(w.shape)],
        out_specs=pl.BlockSpec((nk, tm, kc), lambda i: (0, i, 0)), out_shape=S((nk, t, kc), out_dtype))


def _dx_rows_t(dh, w, name, hook=None):
    t = dh.shape[0]
    k = w.shape[0]
    tm = _tm(t)

    def body(d_ref, w_ref, o_ref):
        o_ref[...] = _dot_nt(w_ref[...], d_ref[...].astype(BF16)).astype(BF16)

    return _run(
        body, [dh, w], hook, grid=(t // tm,), name=name, semantics=("parallel",),
        in_specs=[pl.BlockSpec((tm, D), lambda i: (i, 0)), _resident(w.shape)],
        out_specs=pl.BlockSpec((k, tm), lambda i: (0, i)), out_shape=S((k, t), BF16))


def _ffn_bwd1(dh, c, wdown, name, hook=None):
    ns, t, n = c.shape
    nh = ns // 2
    tm = min(FFN_ROWS, t)
    ni = t // tm

    def body(d_ref, c_ref, wd_ref, dc_ref, dw_hbm, dwb_hbm, acc, stage):
        i = pl.program_id(0)

        @pl.when(i == 0)
        def _():
            acc[...] = jnp.zeros_like(acc)

        dhb = d_ref[...].astype(BF16)
        for j in range(nh):
            dact = _dot_nt(dhb, wd_ref[j * n:(j + 1) * n, :])
            cg = c_ref[j].astype(F32)
            cv = c_ref[nh + j].astype(F32)
            sg = _sigmoid(cg)
            gs = cg * sg
            acc[j * n:(j + 1) * n, :] += _dot_tn((gs * cv).astype(BF16), dhb)
            dc_ref[j] = (dact * cv * (sg + gs * (1.0 - sg))).astype(BF16)
            dc_ref[nh + j] = (dact * gs).astype(BF16)

        @pl.when(i == ni - 1)
        def _():
            pltpu.sync_copy(acc, dw_hbm)
            for j in range(nh):
                stage[...] = acc[j * n:(j + 1) * n, :].astype(BF16)
                pltpu.sync_copy(stage, dwb_hbm.at[pl.ds(j * n, n), :])

    slab = pl.BlockSpec((ns, tm, n), lambda i: (0, i, 0))
    return _run(
        body, [dh, c, wdown], hook, grid=(ni,), name=name, semantics=("arbitrary",),
        in_specs=[pl.BlockSpec((tm, D), lambda i: (i, 0)), slab, _resident(wdown.shape)],
        out_specs=[slab, ANY, ANY], out_shape=[S((ns, t, n), BF16), S(wdown.shape, F32), S(wdown.shape, BF16)],
        scratch_shapes=[pltpu.VMEM(wdown.shape, F32), pltpu.VMEM((n, D), BF16)])


def _ffn_bwd2(dc, a, wup, cw, h, gain, dh_in, name, hook=None):
    ns, t, n = dc.shape
    tm = min(FFN_ROWS, t)
    ni = t // tm

    def body(dc_ref, a_ref, wu_ref, cw_ref, h_ref, g_ref, di_ref, da_ref, o_ref, dg_ref, dcw_ref, dcb_ref, carry, keep):
        i = pl.program_id(0)

        @pl.when(i == 0)
        def _():
            carry[...] = jnp.zeros_like(carry)
            dg_ref[...] = jnp.zeros_like(dg_ref)
            dcw_ref[...] = jnp.zeros_like(dcw_ref)
            dcb_ref[...] = jnp.zeros_like(dcb_ref)

        rsum = lambda v: jnp.sum(v, axis=0, keepdims=True)
        acc = jnp.zeros((tm, D), F32)
        for s in range(ns):
            x = dc_ref[s].astype(F32)
            ext = jnp.concatenate([x, carry[s]], axis=0)
            keep[0] = ext[1:1 + tm]
            keep[1] = ext[2:2 + tm]
            x1, x2 = keep[0], keep[1]
            cwv = cw_ref[s]
            da = (cwv[2:3] * x + cwv[1:2] * x1 + cwv[0:1] * x2).astype(BF16)
            carry[s] = x[:HALO]
            da_ref[s] = da
            acc = acc + _dot_nt(da, wu_ref[s])
            av = a_ref[s].astype(F32)
            dcw_ref[s] += jnp.concatenate([rsum(x2 * av), rsum(x1 * av), rsum(x * av)], axis=0)
            dcb_ref[s] += rsum(x)
        hv = h_ref[...]
        r = _rstd(hv)
        gg = acc * g_ref[...]
        dh_new = di_ref[...].astype(F32) + r * gg - hv * (r * r * r * jnp.mean(gg * hv, axis=-1, keepdims=True))
        o_ref[...] = dh_new.astype(o_ref.dtype)
        dg_ref[...] += jnp.sum(acc * hv * r, axis=0, keepdims=True)

    slab = pl.BlockSpec((ns, tm, n), lambda i: (0, ni - 1 - i, 0))
    row = pl.BlockSpec((tm, D), lambda i: (ni - 1 - i, 0))
    vec = pl.BlockSpec((1, D), lambda i: (0, 0))
    whole = lambda shape: pl.BlockSpec(shape, lambda i: (0,) * len(shape))
    return _run(
        body, [dc, a, wup, cw, h, gain, dh_in], hook, grid=(ni,), name=name, semantics=("arbitrary",),
        in_specs=[slab, slab, _resident(wup.shape), _resident(cw.shape), row, vec, row],
        out_specs=[slab, row, vec, whole((ns, 3, n)), whole((ns, 1, n))],
        out_shape=[S((ns, t, n), BF16), S((t, D), DH), S((1, D), F32), S((ns, 3, n), F32), S((ns, 1, n), F32)],
        scratch_shapes=[pltpu.VMEM((ns, HALO, n), F32), pltpu.VMEM((2, tm, n), F32)])


def _dw_slot(hn, dy_s, name, hook=None):
    t, k = hn.shape
    ns, _, n = dy_s.shape
    tm = _tm(t)

    def body(a_ref, b_ref, o_ref, ob_ref, at_ref):
        @pl.when(pl.program_id(0) == 0)
        def _():
            for i in range(t // tm):
                at_ref[:, i * tm:(i + 1) * tm] = a_ref[i * tm:(i + 1) * tm, :].T

        acc = _dot(at_ref[...], b_ref[...])
        o_ref[...] = acc
        ob_ref[...] = acc.astype(BF16)

    ospec = pl.BlockSpec((None, k, n), lambda j: (j, 0, 0))
    return _run(
        body, [hn, dy_s], hook, grid=(ns,), name=name, semantics=("arbitrary",),
        in_specs=[_resident(hn.shape), pl.BlockSpec((None, t, n), lambda j: (j, 0, 0))],
        out_specs=[ospec, ospec], out_shape=[S((ns, k, n), F32), S((ns, k, n), BF16)],
        scratch_shapes=[pltpu.VMEM((k, t), BF16)])


def _dw_rows(a_s, dh, name, hook=None, fm=False):
    nk, t, kc = (1, a_s.shape[1], a_s.shape[0]) if fm else a_s.shape
    tm = _tm(t)
    ni = t // tm

    def body(a_ref, d_ref, o_ref, ob_ref):
        i = pl.program_id(0)
        dhb = d_ref[...].astype(BF16)

        @pl.when(i == 0)
        def _():
            o_ref[...] = jnp.zeros_like(o_ref)

        if fm:
            o_ref[...] += _dot(a_ref[...], dhb)
        for j in range(0 if fm else nk):
            o_ref[j * kc:(j + 1) * kc, :] += _dot_tn(a_ref[j], dhb)

        @pl.when(i == ni - 1)
        def _():
            ob_ref[...] = o_ref[...].astype(BF16)

    ospec = pl.BlockSpec((nk * kc, D), lambda i: (0, 0))
    return _run(
        body, [a_s, dh], hook, grid=(ni,), name=name, semantics=("arbitrary",),
        in_specs=[pl.BlockSpec((kc, tm), lambda i: (0, i)) if fm else pl.BlockSpec((nk, tm, kc), lambda i: (0, i, 0)),
                  pl.BlockSpec((tm, D), lambda i: (i, 0))],
        out_specs=[ospec, ospec], out_shape=[S((nk * kc, D), F32), S((nk * kc, D), BF16)])


def _dx_slot_normbwd(dy_s, wg, h, gain, dh_in, name, hook=None, fm=False, out_dtype=F32):
    ns, t, n = (1, dy_s.shape[1], dy_s.shape[0]) if fm else dy_s.shape
    tm = _tm(t)

    def body(dy_ref, w_ref, h_ref, g_ref, di_ref, o_ref, dg_ref):
        i = pl.program_id(0)

        @pl.when(i == 0)
        def _():
            dg_ref[...] = jnp.zeros_like(dg_ref)

        g = _dot_tn(dy_ref[...], w_ref[...]) if fm else _dot_nt(dy_ref[0], w_ref[0])
        for s in range(1, ns):
            g = g + _dot_nt(dy_ref[s], w_ref[s])
        hv = h_ref[...]
        r = _rstd(hv)
        gg = g * g_ref[...]
        dh_new = di_ref[...].astype(F32) + r * gg - hv * (r * r * r * jnp.mean(gg * hv, axis=-1, keepdims=True))
        o_ref[...] = dh_new.astype(o_ref.dtype)
        dg_ref[...] += jnp.sum(g * hv * r, axis=0, keepdims=True)

    row = pl.BlockSpec((tm, D), lambda i: (i, 0))
    vec = pl.BlockSpec((1, D), lambda i: (0, 0))
    return _run(
        body, [dy_s, wg, h, gain, dh_in], hook, grid=(t // tm,), name=name, semantics=("arbitrary",),
        in_specs=[pl.BlockSpec((n, tm), lambda i: (0, i)) if fm else pl.BlockSpec((ns, tm, n), lambda i: (0, i, 0)),
                  _resident(wg.shape), row, vec, row],
        out_specs=[row, vec], out_shape=[S((t, D), out_dtype), S((1, D), F32)])


def _sgu_gate_bwd(a_s, dg_s, vgain, ws, bst, name, hook=None):
    t = a_s.shape[1]
    sw = a_s.shape[2]
    gps = sw // CHUNK

    def body(a_ref, dg_ref, vg_ref, ws_ref, b_ref, da_ref, dws_ref, dbt_ref, dvg_ref, dvn_ref):
        n = pl.program_id(0)

        @pl.when(n == 0)
        def _():
            dws_ref[...] = jnp.zeros_like(dws_ref)
            dbt_ref[...] = jnp.zeros_like(dbt_ref)
            dvg_ref[...] = jnp.zeros_like(dvg_ref)

        vpre = jnp.concatenate([a_ref[4 + s].astype(F32) for s in range(4)], axis=1)
        v, v_grad = _gelu_and_grad(vpre)
        r = _rstd(v)
        vhat = v * r
        vn = (vhat * vg_ref[...]).astype(BF16)
        tri = _tril_mask()
        lane = lax.broadcasted_iota(jnp.int32, (CHUNK, CHUNK), 1)
        dbt = jnp.zeros((CHUNK, CHUNK), F32)
        for g in range(SGU_G):
            w = jnp.where(tri, ws_ref[g], 0.0).astype(BF16)
            vng = vn[:, g * CHUNK:(g + 1) * CHUNK]
            sg = _dot(w, vng) + b_ref[:, g:g + 1]
            lo = (g % gps) * CHUNK
            u, u_grad = _gelu_and_grad(a_ref[g // gps, :, lo:lo + CHUNK].astype(F32))
            dgate = dg_ref[g // gps, :, lo:lo + CHUNK].astype(F32)
            da_ref[g // gps, :, lo:lo + CHUNK] = (dgate * sg * u_grad).astype(BF16)
            ds = dgate * u
            dsb = ds.astype(BF16)
            dvn_ref[:, g * CHUNK:(g + 1) * CHUNK] = _dot_tn(w, dsb)
            dws_ref[g] += jnp.where(tri, _dot_nt(dsb, vng), 0.0)
            dbt = dbt + jnp.where(lane == g, jnp.sum(ds, axis=-1, keepdims=True), 0.0)
        dbt_ref[...] += dbt
        dvn = dvn_ref[...]
        dvg_ref[...] += jnp.sum(dvn * vhat, axis=0, keepdims=True)
        gg = dvn * vg_ref[...]
        dv = r * gg - v * (r * r * r * jnp.mean(gg * v, axis=-1, keepdims=True))
        dav = (dv * v_grad).astype(BF16)
        for s in range(4):
            da_ref[4 + s] = dav[:, s * sw:(s + 1) * sw]

    return _run(
        body, [a_s, dg_s, vgain, ws, bst], hook, grid=(t // CHUNK,), name=name, semantics=("arbitrary",),
        in_specs=[pl.BlockSpec((8, CHUNK, sw), lambda n: (0, n, 0)), pl.BlockSpec((4, CHUNK, sw), lambda n: (0, n, 0)),
                  pl.BlockSpec((1, SGU_W), lambda n: (0, 0)), pl.BlockSpec((SGU_G, CHUNK, CHUNK), lambda n: (0, 0, 0)),
                  pl.BlockSpec((CHUNK, SGU_G), lambda n: (0, 0))],
        out_specs=[pl.BlockSpec((8, CHUNK, sw), lambda n: (0, n, 0)), pl.BlockSpec((SGU_G, CHUNK, CHUNK), lambda n: (0, 0, 0)),
                   pl.BlockSpec((CHUNK, CHUNK), lambda n: (0, 0)), pl.BlockSpec((1, SGU_W), lambda n: (0, 0))],
        out_shape=[S((8, t, sw), BF16), S((SGU_G, CHUNK, CHUNK), F32), S((CHUNK, CHUNK), F32), S((1, SGU_W), F32)],
        scratch_shapes=[pltpu.VMEM((CHUNK, SGU_W), F32)])


def _attn_bwd(qkv_t, do_t, qg, kg, sinks, bias, name, hook=None):
    t = qkv_t.shape[1]
    nb = t // CHUNK

    def body(cur_ref, prev_ref, do_ref, qg_ref, kg_ref, sink_ref, bias_ref,
             o_ref, dqg_out, dkg_out, dsk_out, dbias_ref, carry, dqg_ref, dkg_ref, dsk_ref):
        n = pl.program_id(0)

        @pl.when(n == 0)
        def _():
            carry[...] = jnp.zeros_like(carry)
            dqg_ref[...] = jnp.zeros_like(dqg_ref)
            dkg_ref[...] = jnp.zeros_like(dkg_ref)
            dsk_ref[...] = jnp.zeros_like(dsk_ref)
            dbias_ref[...] = jnp.zeros_like(dbias_ref)

        @pl.when(n < nb)
        def _():
            valid = _attn_valid(n)
            o_ref[0:KV0, :] = carry[0:KV0, :].astype(BF16)
            kvs, heads = range(NKV), range(NH)
            group = lambda h: range(KVG * h, KVG * (h + 1))
            ks = [_attn_band(cur_ref, prev_ref, KV0 + HD * h) for h in kvs]
            rks = [_rstd_rows(k) for k in ks]
            khats = [k * rk for k, rk in zip(ks, rks)]
            kns = [(khat * kg_ref[...]).astype(BF16) for khat in khats]
            kn_toks = [kn.T for kn in kns]
            vbs = [_attn_band(cur_ref, prev_ref, KV0 + HD * (NKV + h)).astype(BF16) for h in kvs]
            v_toks = [vb.T for vb in vbs]
            qs = [cur_ref[HD * hq:HD * (hq + 1), :] for hq in heads]
            rqs = [_rstd_rows(q) for q in qs]
            qhats = [q * rq for q, rq in zip(qs, rqs)]
            qns = [(qhat * qg_ref[...]).astype(BF16) for qhat in qhats]
            probs = [_attn_probs(kn_toks[hq // KVG], qns[hq], bias_ref[hq], valid, sink_ref[hq]) for hq in heads]
            dohs = [do_ref[HD * hq:HD * (hq + 1), :] for hq in heads]
            dps = [_dot(v_toks[hq // KVG], dohs[hq]) for hq in heads]
            dsums = [jnp.sum(p * dp, axis=0, keepdims=True) for (p, _), dp in zip(probs, dps)]
            dss = [p * (dp - dsum) for (p, _), dp, dsum in zip(probs, dps, dsums)]
            for hq in heads:
                dsk_ref[hq:hq + 1, :] -= probs[hq][1] * dsums[hq]
                dbias_ref[hq] += dss[hq]
            dvs = [sum(_dot_nt(dohs[hq], probs[hq][0].astype(BF16)) for hq in group(h)) for h in kvs]
            dscs = [(ds * (HD ** -0.5)).astype(BF16) for ds in dss]
            dqns = [_dot(kns[hq // KVG], dscs[hq]) for hq in heads]
            dkns = [sum(_dot_nt(qns[hq], dscs[hq]) for hq in group(h)) for h in kvs]
            dqg_ref[...] += sum(dqn * qhat for dqn, qhat in zip(dqns, qhats))
            for hq in heads:
                gq = dqns[hq] * qg_ref[...]
                carry[HD * hq:HD * (hq + 1), :] = rqs[hq] * gq - qs[hq] * (
                    rqs[hq] * rqs[hq] * rqs[hq] * jnp.mean(gq * qs[hq], axis=0, keepdims=True))
            dkg_ref[...] += sum(dkn * khat for dkn, khat in zip(dkns, khats))
            for h in kvs:
                krow, vrow = KV0 + HD * h, KV0 + HD * (NKV + h)
                gk = dkns[h] * kg_ref[...]
                dk = rks[h] * gk - ks[h] * (rks[h] * rks[h] * rks[h] * jnp.mean(gk * ks[h], axis=0, keepdims=True))
                o_ref[krow:krow + HD, :] = (carry[krow:krow + HD, :] + dk[:, :CHUNK]).astype(BF16)
                o_ref[vrow:vrow + HD, :] = (carry[vrow:vrow + HD, :] + dvs[h][:, :CHUNK]).astype(BF16)
                carry[krow:krow + HD, :] = dk[:, CHUNK:]
                carry[vrow:vrow + HD, :] = dvs[h][:, CHUNK:]

        @pl.when(n == nb)
        def _():
            o_ref[...] = carry[...].astype(BF16)
            dqg_out[...] = jnp.sum(dqg_ref[...], axis=1, keepdims=True)
            dkg_out[...] = jnp.sum(dkg_ref[...], axis=1, keepdims=True)
            dsk_out[...] = jnp.sum(dsk_ref[...], axis=1, keepdims=True)

    cur = lambda n: (0, jnp.minimum(n, nb - 1))
    col = pl.BlockSpec((HD, 1), lambda n: (0, 0))
    whole = lambda shape: pl.BlockSpec(shape, lambda n: (0,) * len(shape))
    return _run(
        body, [qkv_t, qkv_t, do_t, qg, kg, sinks, bias], hook, grid=(nb + 1,), name=name, semantics=("arbitrary",),
        in_specs=[pl.BlockSpec((QKV, CHUNK), cur),
                  pl.BlockSpec((QKV - KV0, CHUNK), lambda n: (KV0 // (QKV - KV0), jnp.clip(n - 1, 0, nb - 1))),
                  pl.BlockSpec((D, CHUNK), cur), col, col, pl.BlockSpec(memory_space=pltpu.SMEM), whole((NH, 2 * CHUNK, CHUNK))],
        out_specs=[pl.BlockSpec((QKV, CHUNK), lambda n: (0, jnp.maximum(n - 1, 0))), whole((HD, 1)), whole((HD, 1)),
                   whole((NH, 1)), whole((NH, 2 * CHUNK, CHUNK))],
        out_shape=[S((QKV, t), BF16), S((HD, 1), F32), S((HD, 1), F32), S((NH, 1), F32), S((NH, 2 * CHUNK, CHUNK), F32)],
        scratch_shapes=[pltpu.VMEM((QKV, CHUNK), F32), pltpu.VMEM((HD, CHUNK), F32), pltpu.VMEM((HD, 2 * CHUNK), F32),
                        pltpu.VMEM((NH, CHUNK), F32)])


class _Plain:
    def __init__(self, wg):
        self.full, self.grads = wg, {}

    def w(self, n):
        return self.full[n]

    def hook(self, host):
        return None

    def grad(self, n, pair):
        self.grads[n] = pair

    def small(self, g_rep):
        pass

    def sync(self, point):
        pass


def _local_step(x, target, rep, sch):
    bucket_row = jnp.asarray(_rel_tables().T.reshape(1, -1))
    bias = _relbias_fwd(rep["rel_bias"].T, bucket_row, "relbias_fwd").reshape(NH, 2 * CHUNK, CHUNK)
    bst = rep["sgu_b_s"][0].T
    ws = rep["sgu_w_s"][0]
    vgain = rep["sgu_v_gain"]
    qg, kg, sinks = rep["attn_q_gain"].reshape(HD, 1), rep["attn_k_gain"].reshape(HD, 1), rep["attn_sinks"][0]
    w_down = lambda l: sch.w("ffn_w_down%d" % l).reshape(D_FF, D)
    w_up = lambda l: sch.w("ffn_w_up%d" % l)
    cw = [sch.w("ffn_conv_w")[:, 3 * l:3 * l + 3] for l in range(2)]
    cb = [rep["ffn_conv_b"][l].reshape(8, 1, -1) for l in range(2)]
    mixg = [rep["mix_norm"][l:l + 1] for l in range(2)]
    ffng = [rep["ffn_norm"][l:l + 1] for l in range(2)]
    rows = lambda pair: tuple(g.reshape(N_DEV, -1, D) for g in pair)
    hk = sch.hook

    hn0 = _rmsnorm(x, mixg[0], "norm0")
    a0 = _mm_slot(hn0, sch.w("sgu_w_in"), BF16, "sgu_in", hk("sgu_in"))
    gated = _sgu_gate_fwd(a0, vgain, ws, bst, "sgu_gate", hk("sgu_gate"))
    h1, hn1 = _resid_mm(gated, sch.w("sgu_w_out").reshape(SGU_W, D), x, ffng[0], "norm", "sgu_out", hk("sgu_out"))
    sch.sync("before_ffn0")
    a_ff0, c_ff0, h2, hn2 = _ffn_fwd(hn1, h1, w_up(0), w_down(0), cw[0], cb[0], mixg[1], "norm", "ffn0_fwd", hk("ffn0_fwd"))
    qkv = _mm_t(hn2, sch.w("attn_w_qkv"), "qkv", hk("qkv"))
    o = _attn_fwd(qkv, qg, kg, sinks, bias, "attn", hk("attn"))
    h3, hn3 = _resid_mm(o, sch.w("attn_w_o").reshape(D, D), h2, ffng[1], "norm", "attn_out", hk("attn_out"), fm=True)
    a_ff1, c_ff1, dy, sq = _ffn_fwd(hn3, h3, w_up(1), w_down(1), cw[1], cb[1], target, "loss", "ffn1_fwd_loss", hk("ffn1_fwd_loss"))
    loss = (0.5 / D) * jnp.sum(sq[:, 0, 0])

    def ffn_bwd(dh, h_in, hn, a, c, l, tag):
        dc, g_down, g_down_b = _ffn_bwd1(dh, c, w_down(l), tag + "_bwd1", hk(tag + "_bwd1"))
        sch.grad("ffn_w_down%d" % l, rows((g_down, g_down_b)))
        da, dh_new, dgain, g_cw, g_cb = _ffn_bwd2(dc, a, w_up(l), cw[l], h_in, ffng[l], dh, tag + "_bwd2", hk(tag + "_bwd2"))
        sch.grad("ffn_w_up%d" % l, _dw_slot(hn, da, tag + "_dw_up", hk(tag + "_dw_up")))
        return dh_new, dgain, g_cw, g_cb.reshape(-1)

    dh, d_ffng1, g_cw1, g_cb1 = ffn_bwd(dy, h3, hn3, a_ff1, c_ff1, 1, "ffn1")
    do = _dx_rows_t(dh, sch.w("attn_w_o").reshape(D, D), "attn_do", hk("attn_do"))
    sch.grad("attn_w_o", rows(_dw_rows(o, dh, "dw_o", hk("dw_o"), fm=True)))
    dqkv, d_qg, d_kg, d_sk, d_bias = _attn_bwd(qkv, do, qg, kg, sinks, bias, "attn_bwd", hk("attn_bwd"))
    sch.grad("attn_w_qkv", tuple(g.reshape(N_DEV, -1, D) for g in _dw_rows(dqkv, hn2, "dw_qkv", hk("dw_qkv"), fm=True)))
    dh, d_mixg1 = _dx_slot_normbwd(dqkv, sch.w("attn_w_qkv").reshape(QKV, D), h2, mixg[1], dh, "dx_qkv", hk("dx_qkv"), fm=True,
                                   out_dtype=DH)
    d_relb = _relbias_bwd(d_bias.reshape(NH, -1), bucket_row, "relbias_bwd").T
    g_rep = {"attn_q_gain": d_qg.reshape(1, HD), "attn_k_gain": d_kg.reshape(1, HD), "attn_sinks": d_sk.reshape(1, NH),
             "rel_bias": d_relb}
    sch.small(g_rep)
    dh, d_ffng0, g_cw0, g_cb0 = ffn_bwd(dh, h1, hn1, a_ff0, c_ff0, 0, "ffn0")
    g_cw = jnp.concatenate([g_cw0, g_cw1], axis=1)
    sch.grad("ffn_conv_w", (g_cw, g_cw.astype(BF16)))
    g_ffn = {"ffn_norm": jnp.concatenate([d_ffng0, d_ffng1], axis=0), "ffn_conv_b": jnp.stack([g_cb0, g_cb1], axis=0)}
    sch.small(g_ffn)
    dgated = _dx_rows(dh, sch.w("sgu_w_out").reshape(SGU_W, D), SGU_W // 4, BF16, "sgu_dgated", hk("sgu_dgated"))
    sch.grad("sgu_w_out", rows(_dw_rows(gated, dh, "dw_sgu_out", hk("dw_sgu_out"))))
    da0, d_ws, d_bst, d_vgain = _sgu_gate_bwd(a0, dgated, vgain, ws, bst, "sgu_gate_bwd", hk("sgu_gate_bwd"))
    g_sgu = {"sgu_v_gain": d_vgain, "sgu_w_s": d_ws[None], "sgu_b_s": d_bst[:, :SGU_G].T[None]}
    sch.small(g_sgu)
    sch.grad("sgu_w_in", _dw_slot(hn0, da0, "dw_sgu_in", hk("dw_sgu_in")))
    sch.sync("after_dw")
    grad_x, d_mixg0 = _dx_slot_normbwd(da0, sch.w("sgu_w_in"), x, mixg[0], dh, "dx_sgu_in", hk("dx_sgu_in"))
    g_mix = {"mix_norm": jnp.concatenate([d_mixg0, d_mixg1], axis=0)}
    sch.small(g_mix)
    for g in (g_ffn, g_sgu, g_mix):
        g_rep.update(g)
    return loss, grad_x, g_rep


def _allgather(xs, name):
    nt = len(xs)

    def body(*refs):
        x_refs, o_refs = refs[:nt], refs[nt:2 * nt]
        send_sems, recv_sems, local_sems = refs[2 * nt:]
        x, y, c, chips = _place()
        me, sibling = (x, y, c), (x, y, 1 - c)

        def copy(t, k, block, to, src=None):
            px, py, pc = block
            dst = o_refs[t].at[4 * px + 2 * py + pc]
            return pltpu.make_async_remote_copy(
                src_ref=dst if src is None else src, dst_ref=dst, send_sem=send_sems.at[t, k], recv_sem=recv_sems.at[t, k],
                device_id=to, device_id_type=MESH)

        mine = [pltpu.make_async_copy(x_refs[t], o_refs[t].at[4 * x + 2 * y + c], local_sems.at[t]) for t in range(nt)]
        for cp in mine:
            cp.start()
        first = []
        for t in range(nt):
            first.append(copy(t, 0, me, sibling, src=x_refs[t]))
            first += [copy(t, 1 + j, me, (*chip, c), src=x_refs[t]) for j, chip in enumerate(chips)]
        for cp in first:
            cp.start()
        passed = []
        for j, chip in enumerate(chips):
            for t in range(nt):
                copy(t, 1 + j, (*chip, c), me).wait_recv()
                fwd = copy(t, 4 + j, (*chip, c), sibling)
                fwd.start()
                passed.append(fwd)
        for t in range(nt):
            copy(t, 0, sibling, me).wait_recv()
            for j, chip in enumerate(chips):
                copy(t, 4 + j, (*chip, 1 - c), me).wait_recv()
        for cp in first + passed:
            cp.wait_send()
        for cp in mine:
            cp.wait()

    return pl.pallas_call(
        body, name=name, in_specs=[ANY] * nt, out_specs=[ANY] * nt,
        out_shape=[S((N_DEV,) + a.shape, a.dtype) for a in xs],
        scratch_shapes=[pltpu.SemaphoreType.DMA((nt, 7)), pltpu.SemaphoreType.DMA((nt, 7)), pltpu.SemaphoreType.DMA((nt,))],
        compiler_params=pltpu.CompilerParams(has_side_effects=True))(*xs)


def _exchange(hook, name):
    comm = hook()
    ci, co = len(comm.inputs), len(comm.out_shapes)

    def body(*refs):
        cins, couts = refs[:ci], refs[ci:ci + co]
        send, recv = refs[-2:]
        comm.start(cins, couts, send, recv)
        comm.finish(cins, couts, send, recv)

    res = pl.pallas_call(
        body, name=name, in_specs=[ANY] * ci, out_specs=[ANY] * co, out_shape=comm.out_shapes,
        scratch_shapes=[pltpu.SemaphoreType.DMA((comm.n_sems,)), pltpu.SemaphoreType.DMA((comm.n_sems,))],
        input_output_aliases=dict(comm.aliases),
        compiler_params=pltpu.CompilerParams(has_side_effects=True))(*comm.inputs)
    hook(res)


def _row_tile(r):
    tr = r if r <= ROW_TILE or r % ROW_TILE else ROW_TILE
    assert r % tr == 0
    return tr


def _rs_partial(g32, sib, place, name):
    _, r, cdim = g32.shape
    tr = _row_tile(r)

    def body(place_ref, g_ref, s_ref, p_ref, own_ref):
        k = pl.program_id(1)
        tot = g_ref[...] + s_ref[...].astype(F32)
        p_ref[...] = tot.astype(BF16)

        @pl.when(k == place_ref[1])
        def _():
            own_ref[...] = tot

    grid_spec = pltpu.PrefetchScalarGridSpec(
        num_scalar_prefetch=1, grid=(r // tr, 4),
        in_specs=[pl.BlockSpec((None, None, tr, cdim), lambda i, k, pr: (k, pr[0], i, 0)),
                  pl.BlockSpec((None, tr, cdim), lambda i, k, pr: (k, i, 0))],
        out_specs=[pl.BlockSpec((None, tr, cdim), lambda i, k, pr: (k, i, 0)), pl.BlockSpec((tr, cdim), lambda i, k, pr: (i, 0))])
    return pl.pallas_call(
        body, grid_spec=grid_spec, name=name,
        out_shape=[S((4, r, cdim), BF16), S((r, cdim), F32)],
        compiler_params=_cp("parallel", "arbitrary"))(place, g32.reshape(4, 2, r, cdim), sib)


def _adamw_math(w, g, m, v):
    m = ADAM_B1 * m + (1.0 - ADAM_B1) * g
    v = ADAM_B2 * v + (1.0 - ADAM_B2) * (g * g)
    m_hat = m / (1.0 - ADAM_B1 ** ADAM_STEP)
    v_hat = v / (1.0 - ADAM_B2 ** ADAM_STEP)
    delta = -ADAM_LR * (m_hat / (jnp.sqrt(v_hat) + ADAM_EPS) + ADAM_WD * w)
    return delta, m, v


def _adamw_shard(owns, recvs, w, m, v, name, flipped=False):
    nl = w.shape[0]
    r, cdim = owns[0].shape
    tr = _row_tile(r)
    nr = r // tr

    def body(*refs):
        own_refs, recv_refs = refs[:nl], refs[nl:2 * nl]
        w_ref, m_ref, v_ref, g_out, d_out, m_out, v_out = refs[2 * nl:]
        layer = pl.program_id(0)
        g = None
        for l in range(nl):
            gl = own_refs[l][...] + recv_refs[l][0].astype(F32) + recv_refs[l][1].astype(F32) + recv_refs[l][2].astype(F32)
            g = gl if g is None else jnp.where(layer == l, gl, g)
        if flipped:
            g = g.T
        g_out[...] = g
        d_out[...], m_out[...], v_out[...] = _adamw_math(w_ref[...], g, m_ref[...], v_ref[...])

    park = lambda l: (lambda layer, i: (jnp.where(layer == l, i, jnp.where(layer < l, 0, nr - 1)), 0))
    park3 = lambda l: (lambda layer, i: (0, jnp.where(layer == l, i, jnp.where(layer < l, 0, nr - 1)), 0))
    if flipped:
        row = pl.BlockSpec((None, cdim, tr), lambda layer, i: (layer, 0, i))
    else:
        row = pl.BlockSpec((None, tr, cdim), lambda layer, i: (layer, i, 0))
    return pl.pallas_call(
        body, grid=(nl, nr), name=name,
        in_specs=[pl.BlockSpec((tr, cdim), park(l)) for l in range(nl)] + [pl.BlockSpec((3, tr, cdim), park3(l)) for l in range(nl)]
        + [row, row, row],
        out_specs=[row] * 4, out_shape=[S(w.shape, F32)] * 4,
        compiler_params=_cp("arbitrary", "arbitrary"))(*owns, *recvs, w, m, v)


def _adamw_small(galls, ws, ms, vs, name):
    n = len(galls)

    def body(*refs):
        g_refs, w_refs, m_refs, v_refs, outs = refs[:n], refs[n:2 * n], refs[2 * n:3 * n], refs[3 * n:4 * n], refs[4 * n:]
        for i in range(n):
            g = g_refs[i][0].astype(F32)
            for s in range(1, N_DEV):
                g = g + g_refs[i][s].astype(F32)
            outs[i][...] = g
            outs[n + i][...], outs[2 * n + i][...], outs[3 * n + i][...] = _adamw_math(w_refs[i][...], g, m_refs[i][...], v_refs[i][...])

    res = pl.pallas_call(body, out_shape=[S(a.shape, F32) for a in ws] * 4, name=name)(*galls, *ws, *ms, *vs)
    return [res[k * n:(k + 1) * n] for k in range(4)]


REPLICATED = ["mix_norm", "ffn_norm", "sgu_v_gain", "sgu_w_s", "sgu_b_s", "attn_q_gain", "attn_k_gain", "attn_sinks", "rel_bias",
              "ffn_conv_b"]
WEIGHTS = ["mix_norm", "ffn_norm", "sgu_w_in", "sgu_v_gain", "sgu_w_s", "sgu_b_s", "sgu_w_out", "attn_w_qkv", "attn_q_gain",
           "attn_k_gain", "attn_sinks", "attn_w_o", "rel_bias", "ffn_w_up", "ffn_conv_w", "ffn_conv_b", "ffn_w_down"]
SMALL = ["g_" + n for n in REPLICATED]
BF16_TRANSIT = {"sgu_w_s"}
SMALL_ATTN = ["g_attn_q_gain", "g_attn_k_gain", "g_attn_sinks", "g_rel_bias"]
SMALL_FFN = ["g_ffn_norm", "g_ffn_conv_b"]
SMALL_SGU = ["g_sgu_v_gain", "g_sgu_w_s", "g_sgu_b_s"]

GATHER_FIRST = ["sgu_w_in", "ffn_conv_w"]
UP0_SPLIT, UP1_SPLIT = 352, 304
PLAN = {
    "sgu_in": [("ag1", "sgu_w_out"), ("ag1", "ffn_w_up0", (0, UP0_SPLIT))],
    "sgu_gate": [("ag2", "sgu_w_out"), ("ag1", "ffn_w_up0", (UP0_SPLIT, D))],
    "sgu_out": [("ag2", "ffn_w_up0"), ("ag1", "ffn_w_down0")],
    "before_ffn0": [("ag2", "ffn_w_down0")],
    "ffn0_fwd": [("agd", "attn_w_qkv"), ("ag1", "attn_w_o"), ("ag1", "ffn_w_down1")],
    "qkv": [("ag2", "attn_w_o"), ("ag2", "ffn_w_down1"), ("ag1", "ffn_w_up1", (0, UP1_SPLIT))],
    "attn": [("ag1", "ffn_w_up1", (UP1_SPLIT, D))],
    "attn_out": [("ag2", "ffn_w_up1")],
    "ffn1_bwd2": [("rs1", "ffn_w_down1")],
    "ffn1_dw_up": [("rs2", "ffn_w_down1")],
    "attn_do": [("rs1", "ffn_w_up1")],
    "attn_bwd": [("rs2", "ffn_w_up1"), ("rs1", "attn_w_o")],
    "dx_qkv": [("rs2", "attn_w_o"), ("rs1", "attn_w_qkv")],
    "ffn0_bwd1": [("rs2", "attn_w_qkv")] + [("ag1", n) for n in SMALL_ATTN],
    "ffn0_bwd2": [("rs1", "ffn_w_down0")] + [("ag2", n) for n in SMALL_ATTN],
    "ffn0_dw_up": [("rs2", "ffn_w_down0")],
    "sgu_dgated": [("rs1", "ffn_w_up0")] + [("agd", n) for n in SMALL_FFN],
    "sgu_gate_bwd": [("rs2", "ffn_w_up0"), ("rs1", "sgu_w_out")],
    "dw_sgu_in": [("rs2", "sgu_w_out")] + [("ag1", n) for n in SMALL_SGU],
    "after_dw": [("rs1", "sgu_w_in"), ("rs1", "ffn_conv_w")] + [("ag2", n) for n in SMALL_SGU],
    "dx_sgu_in": [("rs2", "sgu_w_in"), ("rs2", "ffn_conv_w")],
    "last": [("agd", "g_mix_norm")],
}


class _Overlap:
    def __init__(self, shard, place):
        self.shard, self.place = shard, place
        self.part, self.full = {}, {}
        self.grads, self.sib, self.own, self.recv = {}, {}, {}, {}

    def w(self, n):
        return self.full[n]

    def grad(self, n, pair):
        self.grads[n] = pair

    def small(self, g_rep):
        self.shard.update(("g_" + n, a.astype(BF16) if n in BF16_TRANSIT else a) for n, a in _views2d(g_rep).items())

    def sync(self, point):
        _exchange(self.hook(point), point)

    def chip_sums(self, n):
        sums, self.own[n] = _rs_partial(self.grads[n][0], self.sib.pop(n), self.place, "rs_partial_" + n)
        return sums

    def hook(self, host):
        ops = PLAN.get(host)
        if not ops:
            return None
        where = {"ag1": self.part, "ag2": self.full, "agd": self.full, "rs1": self.sib, "rs2": self.recv}
        idx = []

        def hook(results=None):
            if results is not None:
                for (kind, n, *_), i in zip(ops, idx):
                    where[kind][n] = results[i]
                return None
            comm = _Comm()
            for kind, n, *rows in ops:
                arr = {"ag1": lambda: self.shard[n], "agd": lambda: self.shard[n], "ag2": lambda: self.part.pop(n),
                       "rs1": lambda: self.grads[n][1], "rs2": lambda: self.chip_sums(n)}[kind]()
                idx.append(comm.add(kind, arr, *rows, into=self.part.pop(n) if rows and rows[0][0] else None))
            return comm

        return hook


TRANSPOSED = {"attn_w_qkv"}
PHYSICAL_T = {"ffn_w_up"}
SHARDED = {
    "sgu_w_in": ["sgu_w_in"], "sgu_w_out": ["sgu_w_out"], "attn_w_qkv": ["attn_w_qkv"], "attn_w_o": ["attn_w_o"],
    "ffn_w_up": ["ffn_w_up0", "ffn_w_up1"], "ffn_w_down": ["ffn_w_down0", "ffn_w_down1"], "ffn_conv_w": ["ffn_conv_w"],
}


def _send_views(w):
    out = {"ffn_conv_w": w["ffn_conv_w"].reshape(6, -1)}
    for name, parts in SHARDED.items():
        if name != "ffn_conv_w":
            out.update((p, (w[name][l].T if name in TRANSPOSED else w[name][l]).astype(BF16)) for l, p in enumerate(parts))
    return out


def _views2d(d):
    return {n: d[n].reshape(-1, d[n].shape[-1]) for n in REPLICATED if n in d}


def kernel(x, mix_norm, ffn_norm, sgu_w_in, sgu_v_gain, sgu_w_s, sgu_b_s, sgu_w_out, attn_w_qkv, attn_q_gain, attn_k_gain, attn_sinks, attn_w_o, rel_bias, ffn_w_up, ffn_conv_w, ffn_conv_b, ffn_w_down, loss_target, m_mix_norm, m_ffn_norm, m_sgu_w_in, m_sgu_v_gain, m_sgu_w_s, m_sgu_b_s, m_sgu_w_out, m_attn_w_qkv, m_attn_q_gain, m_attn_k_gain, m_attn_sinks, m_attn_w_o, m_rel_bias, m_ffn_w_up, m_ffn_conv_w, m_ffn_conv_b, m_ffn_w_down, v_mix_norm, v_ffn_norm, v_sgu_w_in, v_sgu_v_gain, v_sgu_w_s, v_sgu_b_s, v_sgu_w_out, v_attn_w_qkv, v_attn_q_gain, v_attn_k_gain, v_attn_sinks, v_attn_w_o, v_rel_bias, v_ffn_w_up, v_ffn_conv_w, v_ffn_conv_b, v_ffn_w_down):
    w = dict(zip(WEIGHTS, (mix_norm, ffn_norm, sgu_w_in, sgu_v_gain, sgu_w_s, sgu_b_s, sgu_w_out, attn_w_qkv, attn_q_gain, attn_k_gain,
                           attn_sinks, attn_w_o, rel_bias, ffn_w_up, ffn_conv_w, ffn_conv_b, ffn_w_down)))
    m = dict(zip(WEIGHTS, (m_mix_norm, m_ffn_norm, m_sgu_w_in, m_sgu_v_gain, m_sgu_w_s, m_sgu_b_s, m_sgu_w_out, m_attn_w_qkv, m_attn_q_gain,
                           m_attn_k_gain, m_attn_sinks, m_attn_w_o, m_rel_bias, m_ffn_w_up, m_ffn_conv_w, m_ffn_conv_b, m_ffn_w_down)))
    v = dict(zip(WEIGHTS, (v_mix_norm, v_ffn_norm, v_sgu_w_in, v_sgu_v_gain, v_sgu_w_s, v_sgu_b_s, v_sgu_w_out, v_attn_w_qkv, v_attn_q_gain,
                           v_attn_k_gain, v_attn_sinks, v_attn_w_o, v_rel_bias, v_ffn_w_up, v_ffn_conv_w, v_ffn_conv_b, v_ffn_w_down)))
    rep = {n: w[n] for n in REPLICATED}

    xi, yi, ci = lax.axis_index("x"), lax.axis_index("y"), lax.axis_index("c")
    place = jnp.stack([ci, 2 * xi + yi]).astype(jnp.int32)
    sch = _Overlap(_send_views(w), place)
    sch.full.update(zip(GATHER_FIRST, _allgather([sch.shard[n] for n in GATHER_FIRST], "gather_first")))

    loss, grad_x, g_rep = _local_step(x[0], loss_target[0], rep, sch)
    loss = lax.psum(loss, ("x", "y", "c"))
    sch.sync("last")

    out = [{}, {}, {}, {}]
    for name, parts in SHARDED.items():
        flip = (lambda a: jnp.swapaxes(a, -1, -2)) if name in TRANSPOSED | PHYSICAL_T else (lambda a: a)
        shape = flip(w[name]).shape
        as3d = lambda a: flip(a).reshape(len(parts), -1, shape[-1])
        res = _adamw_shard([sch.own[p] for p in parts], [sch.recv[p] for p in parts], as3d(w[name]), as3d(m[name]), as3d(v[name]),
                           "adamw_" + name, flipped=name in PHYSICAL_T)
        for o, r in zip(out, res):
            o[name] = flip(r.reshape(shape))
    small = _adamw_small([sch.full[n] for n in SMALL], *[list(_views2d(d).values()) for d in (rep, m, v)], "adamw_small")
    for o, res in zip(out, small):
        o.update((n, r.reshape(w[n].shape)) for n, r in zip(REPLICATED, res))

    return (loss, grad_x[None], *[out[0][n] for n in WEIGHTS], *[out[1][n] for n in WEIGHTS],
            *[out[2][n] for n in WEIGHTS], *[out[3][n] for n in WEIGHTS])
```

```python
import functools
import math

import numpy as np
import jax
import jax.numpy as jnp
from jax import lax
from jax.experimental import pallas as pl
from jax.experimental.pallas import tpu as pltpu

F32 = jnp.float32
BF16 = jnp.bfloat16
DH = jnp.bfloat16
S = jax.ShapeDtypeStruct

D = 1024
CHUNK = 128
SGU_W = 2048
SGU_G = 16
HD = 64
NH = 16
NKV = 4
KVG = 4
D_FF = 2816
REL_BUCKETS = 32
REL_MAX_DIST = 128
EPS = 1e-6
N_DEV = 8
MESH = pl.DeviceIdType.MESH

ADAM_LR = 0.001
ADAM_B1 = 0.9
ADAM_B2 = 0.999
ADAM_EPS = 1e-08
ADAM_WD = 0.01
ADAM_STEP = 10

ROW_TILE = 512
HALO = 8
FFN_ROWS = 256


def _tm(t):
    return min(ROW_TILE, t)


def _cp(*sem):
    return pltpu.CompilerParams(dimension_semantics=sem)


ANY = pl.BlockSpec(memory_space=pl.ANY)


def _place():
    x, y, c = lax.axis_index("x"), lax.axis_index("y"), lax.axis_index("c")
    return x, y, c, [(1 - x, y), (x, 1 - y), (1 - x, 1 - y)]


class _Comm:
    SEMS = {"ag1": 5, "ag2": 3, "rs1": 4, "rs2": 3, "agd": 8}

    def __init__(self):
        self.inputs, self.out_shapes, self.aliases, self.ops, self.n_sems = [], [], {}, [], 0

    def add(self, kind, arr, rows=None, into=None):
        lead = {"ag1": N_DEV, "agd": N_DEV, "ag2": None, "rs1": 4, "rs2": 3}[kind]
        shape = arr.shape if lead is None else (lead,) + arr.shape[(0 if kind in ("ag1", "agd") else 1):]
        if kind == "ag2":
            self.aliases[len(self.inputs)] = len(self.out_shapes)
        self.ops.append((kind, len(self.inputs), len(self.out_shapes), self.n_sems, rows))
        self.inputs.append(arr)
        if into is not None:
            self.aliases[len(self.inputs)] = len(self.out_shapes)
            self.inputs.append(into)
        self.out_shapes.append(S(shape, arr.dtype))
        self.n_sems += self.SEMS[kind]
        return len(self.out_shapes) - 1

    def _copies(self, ins, outs, send, recv):
        x, y, c, chips = _place()
        me, sibling = (x, y, c), (x, y, 1 - c)
        slot = lambda px, py, pc: 4 * px + 2 * py + pc
        sends, recvs, local = [], [], []

        def rc(src, dst, k, to):
            return lambda: pltpu.make_async_remote_copy(src_ref=src(), dst_ref=dst(), send_sem=send.at[k], recv_sem=recv.at[k],
                                                        device_id=to, device_id_type=MESH)

        for kind, ii, oi, b, rows in self.ops:
            src, dst = ins[ii], outs[oi]
            at = lambda ref, i: (lambda: ref.at[i])
            if kind == "ag1":
                part = slice(None) if rows is None else pl.ds(rows[0], rows[1] - rows[0])
                to = lambda i, d=dst, p=part: (lambda: d.at[i, p])
                whole, mine = (lambda s=src, p=part: s.at[p]), to(slot(*me))
                sends.append(rc(whole, mine, b, sibling))
                recvs.append(rc(whole, to(slot(x, y, 1 - c)), b, me))
                for j, chip in enumerate(chips):
                    sends.append(rc(whole, mine, b + 1 + j, (*chip, c)))
                    recvs.append(rc(whole, to(slot(*chip, c)), b + 1 + j, me))
                local.append(lambda s=whole, m=mine, k=b + 4: pltpu.make_async_copy(s(), m(), send.at[k]))
            elif kind == "ag2":
                for j, chip in enumerate(chips):
                    sends.append(rc(at(dst, slot(*chip, c)), at(dst, slot(*chip, c)), b + j, sibling))
                    recvs.append(rc(at(dst, slot(*chip, 1 - c)), at(dst, slot(*chip, 1 - c)), b + j, me))
            elif kind == "agd":
                whole, mine = (lambda s=src: s), at(dst, slot(*me))
                flip = lambda v, bit: 1 - v if bit else v
                for k in range(1, N_DEV):
                    peer = (flip(x, k >> 2), flip(y, (k >> 1) & 1), flip(c, k & 1))
                    sends.append(rc(whole, mine, b + k - 1, peer))
                    recvs.append(rc(whole, at(dst, slot(*peer)), b + k - 1, me))
                local.append(lambda s=src, m=mine, k=b + 7: pltpu.make_async_copy(s, m(), send.at[k]))
            elif kind == "rs1":
                for k in range(4):
                    sends.append(rc(at(src, 2 * k + (1 - c)), at(dst, k), b + k, sibling))
                    recvs.append(rc(at(src, 2 * k + c), at(dst, k), b + k, me))
            else:
                for j, (px, py) in enumerate(chips):
                    sends.append(rc(at(src, 2 * px + py), at(dst, j), b + j, (px, py, c)))
                    recvs.append(rc(at(src, 2 * px + py), at(dst, j), b + j, me))
        return sends, recvs, local

    def start(self, ins, outs, send, recv):
        sends, _, local = self._copies(ins, outs, send, recv)
        for make in local + sends:
            make().start()

    def finish(self, ins, outs, send, recv):
        sends, recvs, local = self._copies(ins, outs, send, recv)
        for make in recvs:
            make().wait_recv()
        for make in sends:
            make().wait_send()
        for make in local:
            make().wait()


def _run(body, args, hook, *, grid, in_specs, out_specs, out_shape, name, semantics, scratch_shapes=(), aliases=None):
    comm = hook() if hook is not None else None
    aliases = dict(aliases or {})
    if comm is None:
        return pl.pallas_call(body, grid=grid, in_specs=in_specs, out_specs=out_specs, out_shape=out_shape, name=name,
                              scratch_shapes=list(scratch_shapes), input_output_aliases=aliases,
                              compiler_params=_cp(*semantics))(*args)
    single = not isinstance(out_shape, (list, tuple))
    out_shapes = [out_shape] if single else list(out_shape)
    out_specs_l = [out_specs] if single else list(out_specs)
    n_in, n_out, n_scr, ci, co = len(args), len(out_shapes), len(scratch_shapes), len(comm.inputs), len(comm.out_shapes)

    def wrapped(*refs):
        ins, cins = refs[:n_in], refs[n_in:n_in + ci]
        outs, couts = refs[n_in + ci:n_in + ci + n_out], refs[n_in + ci + n_out:n_in + ci + n_out + co]
        scr = refs[n_in + ci + n_out + co:n_in + ci + n_out + co + n_scr]
        send, recv = refs[-2:]
        first = functools.reduce(lambda a, b: a & b, [pl.program_id(a) == 0 for a in range(len(grid))])
        last = functools.reduce(lambda a, b: a & b, [pl.program_id(a) == g - 1 for a, g in enumerate(grid)])

        @pl.when(first)
        def _():
            comm.start(cins, couts, send, recv)

        body(*ins, *outs, *scr)

        @pl.when(last)
        def _():
            comm.finish(cins, couts, send, recv)

    res = pl.pallas_call(
        wrapped, grid=grid, in_specs=list(in_specs) + [ANY] * ci, out_specs=out_specs_l + [ANY] * co,
        out_shape=out_shapes + comm.out_shapes, name=name,
        scratch_shapes=list(scratch_shapes) + [pltpu.SemaphoreType.DMA((comm.n_sems,)), pltpu.SemaphoreType.DMA((comm.n_sems,))],
        input_output_aliases={**aliases, **{n_in + k: n_out + v for k, v in comm.aliases.items()}},
        compiler_params=pltpu.CompilerParams(dimension_semantics=("arbitrary",) * len(grid), has_side_effects=True))(*args, *comm.inputs)
    hook(res[n_out:])
    return res[0] if single else list(res[:n_out])


def _dot(a, b):
    return jnp.dot(a, b, preferred_element_type=F32)


def _dot_nt(a, b):
    return lax.dot_general(a, b, (((1,), (1,)), ((), ())), preferred_element_type=F32)


def _dot_tn(a, b):
    return lax.dot_general(a, b, (((0,), (0,)), ((), ())), preferred_element_type=F32)


def _gelu(x):
    return 0.5 * x * (1.0 + lax.erf(x * (2.0 ** -0.5)))


def _gelu_and_grad(x):
    cdf = 0.5 * (1.0 + lax.erf(x * (2.0 ** -0.5)))
    return x * cdf, cdf + x * jnp.exp(-0.5 * x * x) * (1.0 / math.sqrt(2.0 * math.pi))


def _sigmoid(x):
    return 1.0 / (1.0 + jnp.exp(-x))


def _rstd(x):
    return lax.rsqrt(jnp.mean(x * x, axis=-1, keepdims=True) + EPS)


def _rel_tables():
    q = np.arange(CHUNK)[:, None] + CHUNK
    k = np.arange(2 * CHUNK)[None, :]
    dist = q - k
    n = np.maximum(dist, 0)
    max_exact = REL_BUCKETS // 2
    large = max_exact + (np.log(np.maximum(n, 1).astype(np.float32) / max_exact)
                         / math.log(REL_MAX_DIST / max_exact) * (REL_BUCKETS - max_exact)).astype(np.int32)
    large = np.minimum(large, REL_BUCKETS - 1)
    return np.where(n < max_exact, n, large).astype(np.int32)


def _rmsnorm(x, gain, name):
    t = x.shape[0]
    tm = _tm(t)

    def body(x_ref, g_ref, o_ref):
        xv = x_ref[...]
        o_ref[...] = (xv * _rstd(xv) * g_ref[...]).astype(BF16)

    return pl.pallas_call(
        body, grid=(t // tm,), name=name,
        in_specs=[pl.BlockSpec((tm, D), lambda i: (i, 0)), pl.BlockSpec((1, D), lambda i: (0, 0))],
        out_specs=pl.BlockSpec((tm, D), lambda i: (i, 0)),
        out_shape=S((t, D), BF16), compiler_params=_cp("parallel"))(x, gain)


def _resident(shape):
    zeros = (0,) * len(shape)
    return pl.BlockSpec(shape, lambda *_: zeros, pipeline_mode=pl.Buffered(1))


def _mm_slot(hn, wg, out_dtype, name, hook=None):
    t, k = hn.shape
    ns, _, n = wg.shape
    tm = _tm(t)

    def body(a_ref, w_ref, o_ref):
        a = a_ref[...]
        for s in range(ns):
            o_ref[s] = _dot(a, w_ref[s]).astype(out_dtype)

    return _run(
        body, [hn, wg], hook, grid=(t // tm,), name=name, semantics=("parallel",),
        in_specs=[pl.BlockSpec((tm, k), lambda i: (i, 0)), _resident(wg.shape)],
        out_specs=pl.BlockSpec((ns, tm, n), lambda i: (0, i, 0)), out_shape=S((ns, t, n), out_dtype))


def _mm_t(hn, wt, name, hook=None):
    t, k = hn.shape
    ns, n, _ = wt.shape
    tm = _tm(t)

    def body(a_ref, w_ref, o_ref):
        a = a_ref[...]
        for s in range(ns):
            o_ref[s * n:(s + 1) * n, :] = _dot_nt(w_ref[s], a)

    return _run(
        body, [hn, wt], hook, grid=(t // tm,), name=name, semantics=("parallel",),
        in_specs=[pl.BlockSpec((tm, k), lambda i: (i, 0)), _resident(wt.shape)],
        out_specs=pl.BlockSpec((ns * n, tm), lambda i: (0, i)), out_shape=S((ns * n, t), F32))


def _conv3(a, prev, cw, cb, tm):
    ext = jnp.concatenate([prev, a], axis=0)
    return cw[2:3] * a + cw[1:2] * ext[HALO - 1:HALO - 1 + tm] + cw[0:1] * ext[HALO - 2:HALO - 2 + tm] + cb


def _ffn_fwd(hn, h, wup, wdown, cw, cb, extra, mode, name, hook=None):
    t, k = hn.shape
    n = wup.shape[-1]
    nh = wup.shape[0] // 2
    tm = min(FFN_ROWS, t)
    ni = t // tm

    def body(a_ref, h_ref, wu_ref, wd_ref, cw_ref, cb_ref, e_ref, as_ref, cs_ref, o1_ref, o2_ref, carry):
        i = pl.program_id(0)

        @pl.when(i == 0)
        def _():
            carry[...] = jnp.zeros_like(carry)

        a = a_ref[...]
        acc = h_ref[...]
        nxt = (_dot(a, wu_ref[0]), _dot(a, wu_ref[nh]))
        for j in range(nh):
            ag, av = nxt
            if j + 1 < nh:
                nxt = (_dot(a, wu_ref[j + 1]), _dot(a, wu_ref[nh + j + 1]))
            as_ref[j] = ag.astype(BF16)
            as_ref[nh + j] = av.astype(BF16)
            cg = _conv3(ag, carry[j], cw_ref[j], cb_ref[j], tm)
            cv = _conv3(av, carry[nh + j], cw_ref[nh + j], cb_ref[nh + j], tm)
            carry[j] = ag[tm - HALO:]
            carry[nh + j] = av[tm - HALO:]
            cs_ref[j] = cg.astype(BF16)
            cs_ref[nh + j] = cv.astype(BF16)
            act = (cg * _sigmoid(cg) * cv).astype(BF16)
            acc = acc + _dot(act, wd_ref[j * n:(j + 1) * n, :])
        if mode == "norm":
            o1_ref[...] = acc
            o2_ref[...] = (acc * _rstd(acc) * e_ref[...]).astype(BF16)
        else:
            err = acc - e_ref[...]
            o1_ref[...] = (err * (1.0 / D)).astype(o1_ref.dtype)
            o2_ref[...] = jnp.full(o2_ref.shape, jnp.sum(err * err), F32)

    row = pl.BlockSpec((tm, D), lambda i: (i, 0))
    if mode == "norm":
        e_spec, o2_spec, o2_shape = pl.BlockSpec((1, D), lambda i: (0, 0)), row, S((t, D), BF16)
    else:
        e_spec, o2_spec, o2_shape = row, pl.BlockSpec((None, 8, 128), lambda i: (i, 0, 0)), S((ni, 8, 128), F32)
    aspec = pl.BlockSpec((2 * nh, tm, n), lambda i: (0, i, 0))
    return _run(
        body, [hn, h, wup, wdown, cw, cb, extra], hook, grid=(ni,), name=name, semantics=("arbitrary",),
        in_specs=[pl.BlockSpec((tm, k), lambda i: (i, 0)), row, _resident(wup.shape), _resident(wdown.shape),
                  _resident(cw.shape), _resident(cb.shape), e_spec],
        out_specs=[aspec, aspec, row, o2_spec],
        out_shape=[S((2 * nh, t, n), BF16), S((2 * nh, t, n), BF16), S((t, D), F32 if mode == "norm" else DH), o2_shape],
        scratch_shapes=[pltpu.VMEM((2 * nh, HALO, n), F32)])


def _tril_mask():
    r = lax.broadcasted_iota(jnp.int32, (CHUNK, CHUNK), 0)
    c = lax.broadcasted_iota(jnp.int32, (CHUNK, CHUNK), 1)
    return r >= c


def _sgu_gate_fwd(a_s, vgain, ws, bst, name, hook=None):
    t = a_s.shape[1]
    sw = a_s.shape[2]
    gps = sw // CHUNK

    def body(a_ref, vg_ref, ws_ref, b_ref, o_ref):
        v = _gelu(jnp.concatenate([a_ref[4 + s].astype(F32) for s in range(4)], axis=1))
        vn = (v * _rstd(v) * vg_ref[...]).astype(BF16)
        tri = _tril_mask()
        for g in range(SGU_G):
            w = jnp.where(tri, ws_ref[g], 0.0).astype(BF16)
            sg = _dot(w, vn[:, g * CHUNK:(g + 1) * CHUNK]) + b_ref[:, g:g + 1]
            lo = (g % gps) * CHUNK
            u = _gelu(a_ref[g // gps, :, lo:lo + CHUNK].astype(F32))
            o_ref[g // gps, :, lo:lo + CHUNK] = (u * sg).astype(BF16)

    return _run(
        body, [a_s, vgain, ws, bst], hook, grid=(t // CHUNK,), name=name, semantics=("parallel",),
        in_specs=[pl.BlockSpec((8, CHUNK, sw), lambda n: (0, n, 0)), pl.BlockSpec((1, SGU_W), lambda n: (0, 0)),
                  pl.BlockSpec((SGU_G, CHUNK, CHUNK), lambda n: (0, 0, 0)), pl.BlockSpec((CHUNK, SGU_G), lambda n: (0, 0))],
        out_specs=pl.BlockSpec((4, CHUNK, sw), lambda n: (0, n, 0)), out_shape=S((4, t, sw), BF16))


def _resid_mm(a_s, w, resid, extra, mode, name, hook=None, fm=False):
    nk, t, kc = (1, a_s.shape[1], a_s.shape[0]) if fm else a_s.shape
    tm = _tm(t)
    ni = t // tm

    def body(a_ref, w_ref, r_ref, e_ref, o1_ref, o2_ref):
        h = r_ref[...]
        if fm:
            h = h + _dot_tn(a_ref[...], w_ref[...])
        for j in range(0 if fm else nk):
            h = h + _dot(a_ref[j], w_ref[j * kc:(j + 1) * kc, :])
        if mode == "norm":
            o1_ref[...] = h
            o2_ref[...] = (h * _rstd(h) * e_ref[...]).astype(BF16)
        else:
            err = h - e_ref[...]
            o1_ref[...] = (err * (1.0 / D)).astype(o1_ref.dtype)
            o2_ref[...] = jnp.full(o2_ref.shape, jnp.sum(err * err), F32)

    row = pl.BlockSpec((tm, D), lambda i: (i, 0))
    if mode == "norm":
        e_spec, o2_spec, o2_shape = pl.BlockSpec((1, D), lambda i: (0, 0)), row, S((t, D), BF16)
    else:
        e_spec, o2_spec, o2_shape = row, pl.BlockSpec((None, 8, 128), lambda i: (i, 0, 0)), S((ni, 8, 128), F32)
    return _run(
        body, [a_s, w, resid, extra], hook, grid=(ni,), name=name, semantics=("parallel",),
        in_specs=[pl.BlockSpec((kc, tm), lambda i: (0, i)) if fm else pl.BlockSpec((nk, tm, kc), lambda i: (0, i, 0)),
                  _resident(w.shape), row, e_spec],
        out_specs=[row, o2_spec], out_shape=[S((t, D), F32 if mode == "norm" else DH), o2_shape])


def _relbias_fwd(rel_bias_t, bucket_row, name):
    nb = bucket_row.shape[1]

    def body(rb_ref, bk_ref, o_ref):
        onehot = (lax.broadcasted_iota(jnp.int32, (REL_BUCKETS, nb), 0) == bk_ref[...]).astype(F32)
        o_ref[...] = jnp.dot(rb_ref[...], onehot, precision=lax.Precision.HIGHEST, preferred_element_type=F32)

    return pl.pallas_call(body, out_shape=S((NH, nb), F32), name=name)(rel_bias_t, bucket_row)


def _relbias_bwd(dbias, bucket_row, name):
    nb = bucket_row.shape[1]

    def body(db_ref, bk_ref, o_ref):
        onehot = (lax.broadcasted_iota(jnp.int32, (REL_BUCKETS, nb), 0) == bk_ref[...]).astype(F32)
        o_ref[...] = lax.dot_general(db_ref[...], onehot, (((1,), (1,)), ((), ())),
                                     precision=lax.Precision.HIGHEST, preferred_element_type=F32)

    return pl.pallas_call(body, out_shape=S((NH, REL_BUCKETS), F32), name=name)(dbias, bucket_row)


QKV = D + 2 * NKV * HD
KV0 = D


def _rstd_rows(x):
    return lax.rsqrt(jnp.mean(x * x, axis=0, keepdims=True) + EPS)


def _attn_valid(n):
    kj = lax.broadcasted_iota(jnp.int32, (2 * CHUNK, CHUNK), 0)
    qi = lax.broadcasted_iota(jnp.int32, (2 * CHUNK, CHUNK), 1)
    dist = qi + CHUNK - kj
    return (dist >= 0) & (dist < CHUNK) & ((n > 0) | (kj >= CHUNK))


def _attn_band(cur_ref, prev_ref, row):
    return jnp.concatenate([prev_ref[row - KV0:row - KV0 + HD, :], cur_ref[row:row + HD, :]], axis=1)


def _attn_probs(kn_tok, qn, bias, valid, sink):
    s = _dot(kn_tok, qn) * (HD ** -0.5) + bias
    s = jnp.where(valid, s, -jnp.inf)
    m = jnp.maximum(jnp.max(s, axis=0, keepdims=True), sink)
    p = jnp.exp(s - m)
    psink = jnp.exp(sink - m)
    inv = 1.0 / (jnp.sum(p, axis=0, keepdims=True) + psink)
    return p * inv, psink * inv


def _attn_fwd(qkv_t, qg, kg, sinks, bias, name, hook=None):
    t = qkv_t.shape[1]

    def body(cur_ref, prev_ref, qg_ref, kg_ref, sink_ref, bias_ref, o_ref):
        n = pl.program_id(0)
        valid = _attn_valid(n)
        ks = [_attn_band(cur_ref, prev_ref, KV0 + HD * h) for h in range(NKV)]
        kn_toks = [(k * _rstd_rows(k) * kg_ref[...]).astype(BF16).T for k in ks]
        vbs = [_attn_band(cur_ref, prev_ref, KV0 + HD * (NKV + h)).astype(BF16) for h in range(NKV)]
        qs = [cur_ref[HD * hq:HD * (hq + 1), :] for hq in range(NH)]
        qns = [(q * _rstd_rows(q) * qg_ref[...]).astype(BF16) for q in qs]
        ps = [_attn_probs(kn_toks[hq // KVG], qns[hq], bias_ref[hq], valid, sink_ref[hq])[0] for hq in range(NH)]
        for hq in range(NH):
            o_ref[HD * hq:HD * (hq + 1), :] = _dot(vbs[hq // KVG], ps[hq].astype(BF16)).astype(BF16)

    col = pl.BlockSpec((HD, 1), lambda n: (0, 0))
    return _run(
        body, [qkv_t, qkv_t, qg, kg, sinks, bias], hook, grid=(t // CHUNK,), name=name, semantics=("parallel",),
        in_specs=[pl.BlockSpec((QKV, CHUNK), lambda n: (0, n)),
                  pl.BlockSpec((QKV - KV0, CHUNK), lambda n: (KV0 // (QKV - KV0), jnp.maximum(n - 1, 0))),
                  col, col, pl.BlockSpec(memory_space=pltpu.SMEM), pl.BlockSpec((NH, 2 * CHUNK, CHUNK), lambda n: (0, 0, 0))],
        out_specs=pl.BlockSpec((D, CHUNK), lambda n: (0, n)), out_shape=S((D, t), BF16))


def _dx_rows(dh, w, kc, out_dtype, name, hook=None):
    t = dh.shape[0]
    nk = w.shape[0] // kc
    tm = _tm(t)

    def body(d_ref, w_ref, o_ref):
        dhb = d_ref[...].astype(BF16)
        for j in range(nk):
            o_ref[j] = _dot_nt(dhb, w_ref[j * kc:(j + 1) * kc, :]).astype(out_dtype)

    return _run(
        body, [dh, w], hook, grid=(t // tm,), name=name, semantics=("parallel",),
        in_specs=[pl.BlockSpec((tm, D), lambda i: (i, 0)), _resident(w.shape)],
        out_specs=pl.BlockSpec((nk, tm, kc), lambda i: (0, i, 0)), out_shape=S((nk, t, kc), out_dtype))


def _dx_rows_t(dh, w, name, hook=None):
    t = dh.shape[0]
    k = w.shape[0]
    tm = _tm(t)

    def body(d_ref, w_ref, o_ref):
        o_ref[...] = _dot_nt(w_ref[...], d_ref[...].astype(BF16)).astype(BF16)

    return _run(
        body, [dh, w], hook, grid=(t // tm,), name=name, semantics=("parallel",),
        in_specs=[pl.BlockSpec((tm, D), lambda i: (i, 0)), _resident(w.shape)],
        out_specs=pl.BlockSpec((k, tm), lambda i: (0, i)), out_shape=S((k, t), BF16))


def _ffn_bwd1(dh, c, wdown, name, hook=None):
    ns, t, n = c.shape
    nh = ns // 2
    tm = min(FFN_ROWS, t)
    ni = t // tm

    def body(d_ref, c_ref, wd_ref, dc_ref, dw_hbm, dwb_hbm, acc, stage):
        i = pl.program_id(0)

        @pl.when(i == 0)
        def _():
            acc[...] = jnp.zeros_like(acc)

        dhb = d_ref[...].astype(BF16)
        for j in range(nh):
            dact = _dot_nt(dhb, wd_ref[j * n:(j + 1) * n, :])
            cg = c_ref[j].astype(F32)
            cv = c_ref[nh + j].astype(F32)
            sg = _sigmoid(cg)
            gs = cg * sg
            acc[j * n:(j + 1) * n, :] += _dot_tn((gs * cv).astype(BF16), dhb)
            dc_ref[j] = (dact * cv * (sg + gs * (1.0 - sg))).astype(BF16)
            dc_ref[nh + j] = (dact * gs).astype(BF16)

        @pl.when(i == ni - 1)
        def _():
            pltpu.sync_copy(acc, dw_hbm)
            for j in range(nh):
                stage[...] = acc[j * n:(j + 1) * n, :].astype(BF16)
                pltpu.sync_copy(stage, dwb_hbm.at[pl.ds(j * n, n), :])

    slab = pl.BlockSpec((ns, tm, n), lambda i: (0, i, 0))
    return _run(
        body, [dh, c, wdown], hook, grid=(ni,), name=name, semantics=("arbitrary",),
        in_specs=[pl.BlockSpec((tm, D), lambda i: (i, 0)), slab, _resident(wdown.shape)],
        out_specs=[slab, ANY, ANY], out_shape=[S((ns, t, n), BF16), S(wdown.shape, F32), S(wdown.shape, BF16)],
        scratch_shapes=[pltpu.VMEM(wdown.shape, F32), pltpu.VMEM((n, D), BF16)])


def _ffn_bwd2(dc, a, wup, cw, h, gain, dh_in, name, hook=None):
    ns, t, n = dc.shape
    tm = min(FFN_ROWS, t)
    ni = t // tm

    def body(dc_ref, a_ref, wu_ref, cw_ref, h_ref, g_ref, di_ref, da_ref, o_ref, dg_ref, dcw_ref, dcb_ref, carry, keep):
        i = pl.program_id(0)

        @pl.when(i == 0)
        def _():
            carry[...] = jnp.zeros_like(carry)
            dg_ref[...] = jnp.zeros_like(dg_ref)
            dcw_ref[...] = jnp.zeros_like(dcw_ref)
            dcb_ref[...] = jnp.zeros_like(dcb_ref)

        rsum = lambda v: jnp.sum(v, axis=0, keepdims=True)
        acc = jnp.zeros((tm, D), F32)
        for s in range(ns):
            x = dc_ref[s].astype(F32)
            ext = jnp.concatenate([x, carry[s]], axis=0)
            keep[0] = ext[1:1 + tm]
            keep[1] = ext[2:2 + tm]
            x1, x2 = keep[0], keep[1]
            cwv = cw_ref[s]
            da = (cwv[2:3] * x + cwv[1:2] * x1 + cwv[0:1] * x2).astype(BF16)
            carry[s] = x[:HALO]
            da_ref[s] = da
            acc = acc + _dot_nt(da, wu_ref[s])
            av = a_ref[s].astype(F32)
            dcw_ref[s] += jnp.concatenate([rsum(x2 * av), rsum(x1 * av), rsum(x * av)], axis=0)
            dcb_ref[s] += rsum(x)
        hv = h_ref[...]
        r = _rstd(hv)
        gg = acc * g_ref[...]
        dh_new = di_ref[...].astype(F32) + r * gg - hv * (r * r * r * jnp.mean(gg * hv, axis=-1, keepdims=True))
        o_ref[...] = dh_new.astype(o_ref.dtype)
        dg_ref[...] += jnp.sum(acc * hv * r, axis=0, keepdims=True)

    slab = pl.BlockSpec((ns, tm, n), lambda i: (0, ni - 1 - i, 0))
    row = pl.BlockSpec((tm, D), lambda i: (ni - 1 - i, 0))
    vec = pl.BlockSpec((1, D), lambda i: (0, 0))
    whole = lambda shape: pl.BlockSpec(shape, lambda i: (0,) * len(shape))
    return _run(
        body, [dc, a, wup, cw, h, gain, dh_in], hook, grid=(ni,), name=name, semantics=("arbitrary",),
        in_specs=[slab, slab, _resident(wup.shape), _resident(cw.shape), row, vec, row],
        out_specs=[slab, row, vec, whole((ns, 3, n)), whole((ns, 1, n))],
        out_shape=[S((ns, t, n), BF16), S((t, D), DH), S((1, D), F32), S((ns, 3, n), F32), S((ns, 1, n), F32)],
        scratch_shapes=[pltpu.VMEM((ns, HALO, n), F32), pltpu.VMEM((2, tm, n), F32)])


def _dw_slot(hn, dy_s, name, hook=None):
    t, k = hn.shape
    ns, _, n = dy_s.shape
    tm = _tm(t)

    def body(a_ref, b_ref, o_ref, ob_ref, at_ref):
        @pl.when(pl.program_id(0) == 0)
        def _():
            for i in range(t // tm):
                at_ref[:, i * tm:(i + 1) * tm] = a_ref[i * tm:(i + 1) * tm, :].T

        acc = _dot(at_ref[...], b_ref[...])
        o_ref[...] = acc
        ob_ref[...] = acc.astype(BF16)

    ospec = pl.BlockSpec((None, k, n), lambda j: (j, 0, 0))
    return _run(
        body, [hn, dy_s], hook, grid=(ns,), name=name, semantics=("arbitrary",),
        in_specs=[_resident(hn.shape), pl.BlockSpec((None, t, n), lambda j: (j, 0, 0))],
        out_specs=[ospec, ospec], out_shape=[S((ns, k, n), F32), S((ns, k, n), BF16)],
        scratch_shapes=[pltpu.VMEM((k, t), BF16)])


def _dw_rows(a_s, dh, name, hook=None, fm=False):
    nk, t, kc = (1, a_s.shape[1], a_s.shape[0]) if fm else a_s.shape
    tm = _tm(t)
    ni = t // tm

    def body(a_ref, d_ref, o_ref, ob_ref):
        i = pl.program_id(0)
        dhb = d_ref[...].astype(BF16)

        @pl.when(i == 0)
        def _():
            o_ref[...] = jnp.zeros_like(o_ref)

        if fm:
            o_ref[...] += _dot(a_ref[...], dhb)
        for j in range(0 if fm else nk):
            o_ref[j * kc:(j + 1) * kc, :] += _dot_tn(a_ref[j], dhb)

        @pl.when(i == ni - 1)
        def _():
            ob_ref[...] = o_ref[...].astype(BF16)

    ospec = pl.BlockSpec((nk * kc, D), lambda i: (0, 0))
    return _run(
        body, [a_s, dh], hook, grid=(ni,), name=name, semantics=("arbitrary",),
        in_specs=[pl.BlockSpec((kc, tm), lambda i: (0, i)) if fm else pl.BlockSpec((nk, tm, kc), lambda i: (0, i, 0)),
                  pl.BlockSpec((tm, D), lambda i: (i, 0))],
        out_specs=[ospec, ospec], out_shape=[S((nk * kc, D), F32), S((nk * kc, D), BF16)])


def _dx_slot_normbwd(dy_s, wg, h, gain, dh_in, name, hook=None, fm=False, out_dtype=F32):
    ns, t, n = (1, dy_s.shape[1], dy_s.shape[0]) if fm else dy_s.shape
    tm = _tm(t)

    def body(dy_ref, w_ref, h_ref, g_ref, di_ref, o_ref, dg_ref):
        i = pl.program_id(0)

        @pl.when(i == 0)
        def _():
            dg_ref[...] = jnp.zeros_like(dg_ref)

        g = _dot_tn(dy_ref[...], w_ref[...]) if fm else _dot_nt(dy_ref[0], w_ref[0])
        for s in range(1, ns):
            g = g + _dot_nt(dy_ref[s], w_ref[s])
        hv = h_ref[...]
        r = _rstd(hv)
        gg = g * g_ref[...]
        dh_new = di_ref[...].astype(F32) + r * gg - hv * (r * r * r * jnp.mean(gg * hv, axis=-1, keepdims=True))
        o_ref[...] = dh_new.astype(o_ref.dtype)
        dg_ref[...] += jnp.sum(g * hv * r, axis=0, keepdims=True)

    row = pl.BlockSpec((tm, D), lambda i: (i, 0))
    vec = pl.BlockSpec((1, D), lambda i: (0, 0))
    return _run(
        body, [dy_s, wg, h, gain, dh_in], hook, grid=(t // tm,), name=name, semantics=("arbitrary",),
        in_specs=[pl.BlockSpec((n, tm), lambda i: (0, i)) if fm else pl.BlockSpec((ns, tm, n), lambda i: (0, i, 0)),
                  _resident(wg.shape), row, vec, row],
        out_specs=[row, vec], out_shape=[S((t, D), out_dtype), S((1, D), F32)])


def _sgu_gate_bwd(a_s, dg_s, vgain, ws, bst, name, hook=None):
    t = a_s.shape[1]
    sw = a_s.shape[2]
    gps = sw // CHUNK

    def body(a_ref, dg_ref, vg_ref, ws_ref, b_ref, da_ref, dws_ref, dbt_ref, dvg_ref, dvn_ref):
        n = pl.program_id(0)

        @pl.when(n == 0)
        def _():
            dws_ref[...] = jnp.zeros_like(dws_ref)
            dbt_ref[...] = jnp.zeros_like(dbt_ref)
            dvg_ref[...] = jnp.zeros_like(dvg_ref)

        vpre = jnp.concatenate([a_ref[4 + s].astype(F32) for s in range(4)], axis=1)
        v, v_grad = _gelu_and_grad(vpre)
        r = _rstd(v)
        vhat = v * r
        vn = (vhat * vg_ref[...]).astype(BF16)
        tri = _tril_mask()
        lane = lax.broadcasted_iota(jnp.int32, (CHUNK, CHUNK), 1)
        dbt = jnp.zeros((CHUNK, CHUNK), F32)
        for g in range(SGU_G):
            w = jnp.where(tri, ws_ref[g], 0.0).astype(BF16)
            vng = vn[:, g * CHUNK:(g + 1) * CHUNK]
            sg = _dot(w, vng) + b_ref[:, g:g + 1]
            lo = (g % gps) * CHUNK
            u, u_grad = _gelu_and_grad(a_ref[g // gps, :, lo:lo + CHUNK].astype(F32))
            dgate = dg_ref[g // gps, :, lo:lo + CHUNK].astype(F32)
            da_ref[g // gps, :, lo:lo + CHUNK] = (dgate * sg * u_grad).astype(BF16)
            ds = dgate * u
            dsb = ds.astype(BF16)
            dvn_ref[:, g * CHUNK:(g + 1) * CHUNK] = _dot_tn(w, dsb)
            dws_ref[g] += jnp.where(tri, _dot_nt(dsb, vng), 0.0)
            dbt = dbt + jnp.where(lane == g, jnp.sum(ds, axis=-1, keepdims=True), 0.0)
        dbt_ref[...] += dbt
        dvn = dvn_ref[...]
        dvg_ref[...] += jnp.sum(dvn * vhat, axis=0, keepdims=True)
        gg = dvn * vg_ref[...]
        dv = r * gg - v * (r * r * r * jnp.mean(gg * v, axis=-1, keepdims=True))
        dav = (dv * v_grad).astype(BF16)
        for s in range(4):
            da_ref[4 + s] = dav[:, s * sw:(s + 1) * sw]

    return _run(
        body, [a_s, dg_s, vgain, ws, bst], hook, grid=(t // CHUNK,), name=name, semantics=("arbitrary",),
        in_specs=[pl.BlockSpec((8, CHUNK, sw), lambda n: (0, n, 0)), pl.BlockSpec((4, CHUNK, sw), lambda n: (0, n, 0)),
                  pl.BlockSpec((1, SGU_W), lambda n: (0, 0)), pl.BlockSpec((SGU_G, CHUNK, CHUNK), lambda n: (0, 0, 0)),
                  pl.BlockSpec((CHUNK, SGU_G), lambda n: (0, 0))],
        out_specs=[pl.BlockSpec((8, CHUNK, sw), lambda n: (0, n, 0)), pl.BlockSpec((SGU_G, CHUNK, CHUNK), lambda n: (0, 0, 0)),
                   pl.BlockSpec((CHUNK, CHUNK), lambda n: (0, 0)), pl.BlockSpec((1, SGU_W), lambda n: (0, 0))],
        out_shape=[S((8, t, sw), BF16), S((SGU_G, CHUNK, CHUNK), F32), S((CHUNK, CHUNK), F32), S((1, SGU_W), F32)],
        scratch_shapes=[pltpu.VMEM((CHUNK, SGU_W), F32)])


def _attn_bwd(qkv_t, do_t, qg, kg, sinks, bias, name, hook=None):
    t = qkv_t.shape[1]
    nb = t // CHUNK

    def body(cur_ref, prev_ref, do_ref, qg_ref, kg_ref, sink_ref, bias_ref,
             o_ref, dqg_out, dkg_out, dsk_out, dbias_ref, carry, dqg_ref, dkg_ref, dsk_ref):
        n = pl.program_id(0)

        @pl.when(n == 0)
        def _():
            carry[...] = jnp.zeros_like(carry)
            dqg_ref[...] = jnp.zeros_like(dqg_ref)
            dkg_ref[...] = jnp.zeros_like(dkg_ref)
            dsk_ref[...] = jnp.zeros_like(dsk_ref)
            dbias_ref[...] = jnp.zeros_like(dbias_ref)

        @pl.when(n < nb)
        def _():
            valid = _attn_valid(n)
            o_ref[0:KV0, :] = carry[0:KV0, :].astype(BF16)
            kvs, heads = range(NKV), range(NH)
            group = lambda h: range(KVG * h, KVG * (h + 1))
            ks = [_attn_band(cur_ref, prev_ref, KV0 + HD * h) for h in kvs]
            rks = [_rstd_rows(k) for k in ks]
            khats = [k * rk for k, rk in zip(ks, rks)]
            kns = [(khat * kg_ref[...]).astype(BF16) for khat in khats]
            kn_toks = [kn.T for kn in kns]
            vbs = [_attn_band(cur_ref, prev_ref, KV0 + HD * (NKV + h)).astype(BF16) for h in kvs]
            v_toks = [vb.T for vb in vbs]
            qs = [cur_ref[HD * hq:HD * (hq + 1), :] for hq in heads]
            rqs = [_rstd_rows(q) for q in qs]
            qhats = [q * rq for q, rq in zip(qs, rqs)]
            qns = [(qhat * qg_ref[...]).astype(BF16) for qhat in qhats]
            probs = [_attn_probs(kn_toks[hq // KVG], qns[hq], bias_ref[hq], valid, sink_ref[hq]) for hq in heads]
            dohs = [do_ref[HD * hq:HD * (hq + 1), :] for hq in heads]
            dps = [_dot(v_toks[hq // KVG], dohs[hq]) for hq in heads]
            dsums = [jnp.sum(p * dp, axis=0, keepdims=True) for (p, _), dp in zip(probs, dps)]
            dss = [p * (dp - dsum) for (p, _), dp, dsum in zip(probs, dps, dsums)]
            for hq in heads:
                dsk_ref[hq:hq + 1, :] -= probs[hq][1] * dsums[hq]
                dbias_ref[hq] += dss[hq]
            dvs = [sum(_dot_nt(dohs[hq], probs[hq][0].astype(BF16)) for hq in group(h)) for h in kvs]
            dscs = [(ds * (HD ** -0.5)).astype(BF16) for ds in dss]
            dqns = [_dot(kns[hq // KVG], dscs[hq]) for hq in heads]
            dkns = [sum(_dot_nt(qns[hq], dscs[hq]) for hq in group(h)) for h in kvs]
            dqg_ref[...] += sum(dqn * qhat for dqn, qhat in zip(dqns, qhats))
            for hq in heads:
                gq = dqns[hq] * qg_ref[...]
                carry[HD * hq:HD * (hq + 1), :] = rqs[hq] * gq - qs[hq] * (
                    rqs[hq] * rqs[hq] * rqs[hq] * jnp.mean(gq * qs[hq], axis=0, keepdims=True))
            dkg_ref[...] += sum(dkn * khat for dkn, khat in zip(dkns, khats))
            for h in kvs:
                krow, vrow = KV0 + HD * h, KV0 + HD * (NKV + h)
                gk = dkns[h] * kg_ref[...]
                dk = rks[h] * gk - ks[h] * (rks[h] * rks[h] * rks[h] * jnp.mean(gk * ks[h], axis=0, keepdims=True))
                o_ref[krow:krow + HD, :] = (carry[krow:krow + HD, :] + dk[:, :CHUNK]).astype(BF16)
                o_ref[vrow:vrow + HD, :] = (carry[vrow:vrow + HD, :] + dvs[h][:, :CHUNK]).astype(BF16)
                carry[krow:krow + HD, :] = dk[:, CHUNK:]
                carry[vrow:vrow + HD, :] = dvs[h][:, CHUNK:]

        @pl.when(n == nb)
        def _():
            o_ref[...] = carry[...].astype(BF16)
            dqg_out[...] = jnp.sum(dqg_ref[...], axis=1, keepdims=True)
            dkg_out[...] = jnp.sum(dkg_ref[...], axis=1, keepdims=True)
            dsk_out[...] = jnp.sum(dsk_ref[...], axis=1, keepdims=True)

    cur = lambda n: (0, jnp.minimum(n, nb - 1))
    col = pl.BlockSpec((HD, 1), lambda n: (0, 0))
    whole = lambda shape: pl.BlockSpec(shape, lambda n: (0,) * len(shape))
    return _run(
        body, [qkv_t, qkv_t, do_t, qg, kg, sinks, bias], hook, grid=(nb + 1,), name=name, semantics=("arbitrary",),
        in_specs=[pl.BlockSpec((QKV, CHUNK), cur),
                  pl.BlockSpec((QKV - KV0, CHUNK), lambda n: (KV0 // (QKV - KV0), jnp.clip(n - 1, 0, nb - 1))),
                  pl.BlockSpec((D, CHUNK), cur), col, col, pl.BlockSpec(memory_space=pltpu.SMEM), whole((NH, 2 * CHUNK, CHUNK))],
        out_specs=[pl.BlockSpec((QKV, CHUNK), lambda n: (0, jnp.maximum(n - 1, 0))), whole((HD, 1)), whole((HD, 1)),
                   whole((NH, 1)), whole((NH, 2 * CHUNK, CHUNK))],
        out_shape=[S((QKV, t), BF16), S((HD, 1), F32), S((HD, 1), F32), S((NH, 1), F32), S((NH, 2 * CHUNK, CHUNK), F32)],
        scratch_shapes=[pltpu.VMEM((QKV, CHUNK), F32), pltpu.VMEM((HD, CHUNK), F32), pltpu.VMEM((HD, 2 * CHUNK), F32),
                        pltpu.VMEM((NH, CHUNK), F32)])


class _Plain:
    def __init__(self, wg):
        self.full, self.grads = wg, {}

    def w(self, n):
        return self.full[n]

    def hook(self, host):
        return None

    def grad(self, n, pair):
        self.grads[n] = pair

    def small(self, g_rep):
        pass

    def sync(self, point):
        pass


def _local_step(x, target, rep, sch):
    bucket_row = jnp.asarray(_rel_tables().T.reshape(1, -1))
    bias = _relbias_fwd(rep["rel_bias"].T, bucket_row, "relbias_fwd").reshape(NH, 2 * CHUNK, CHUNK)
    bst = rep["sgu_b_s"][0].T
    ws = rep["sgu_w_s"][0]
    vgain = rep["sgu_v_gain"]
    qg, kg, sinks = rep["attn_q_gain"].reshape(HD, 1), rep["attn_k_gain"].reshape(HD, 1), rep["attn_sinks"][0]
    w_down = lambda l: sch.w("ffn_w_down%d" % l).reshape(D_FF, D)
    w_up = lambda l: sch.w("ffn_w_up%d" % l)
    cw = [sch.w("ffn_conv_w")[:, 3 * l:3 * l + 3] for l in range(2)]
    cb = [rep["ffn_conv_b"][l].reshape(8, 1, -1) for l in range(2)]
    mixg = [rep["mix_norm"][l:l + 1] for l in range(2)]
    ffng = [rep["ffn_norm"][l:l + 1] for l in range(2)]
    rows = lambda pair: tuple(g.reshape(N_DEV, -1, D) for g in pair)
    hk = sch.hook

    hn0 = _rmsnorm(x, mixg[0], "norm0")
    a0 = _mm_slot(hn0, sch.w("sgu_w_in"), BF16, "sgu_in", hk("sgu_in"))
    gated = _sgu_gate_fwd(a0, vgain, ws, bst, "sgu_gate", hk("sgu_gate"))
    h1, hn1 = _resid_mm(gated, sch.w("sgu_w_out").reshape(SGU_W, D), x, ffng[0], "norm", "sgu_out", hk("sgu_out"))
    sch.sync("before_ffn0")
    a_ff0, c_ff0, h2, hn2 = _ffn_fwd(hn1, h1, w_up(0), w_down(0), cw[0], cb[0], mixg[1], "norm", "ffn0_fwd", hk("ffn0_fwd"))
    qkv = _mm_t(hn2, sch.w("attn_w_qkv"), "qkv", hk("qkv"))
    o = _attn_fwd(qkv, qg, kg, sinks, bias, "attn", hk("attn"))
    h3, hn3 = _resid_mm(o, sch.w("attn_w_o").reshape(D, D), h2, ffng[1], "norm", "attn_out", hk("attn_out"), fm=True)
    a_ff1, c_ff1, dy, sq = _ffn_fwd(hn3, h3, w_up(1), w_down(1), cw[1], cb[1], target, "loss", "ffn1_fwd_loss", hk("ffn1_fwd_loss"))
    loss = (0.5 / D) * jnp.sum(sq[:, 0, 0])

    def ffn_bwd(dh, h_in, hn, a, c, l, tag):
        dc, g_down, g_down_b = _ffn_bwd1(dh, c, w_down(l), tag + "_bwd1", hk(tag + "_bwd1"))
        sch.grad("ffn_w_down%d" % l, rows((g_down, g_down_b)))
        da, dh_new, dgain, g_cw, g_cb = _ffn_bwd2(dc, a, w_up(l), cw[l], h_in, ffng[l], dh, tag + "_bwd2", hk(tag + "_bwd2"))
        sch.grad("ffn_w_up%d" % l, _dw_slot(hn, da, tag + "_dw_up", hk(tag + "_dw_up")))
        return dh_new, dgain, g_cw, g_cb.reshape(-1)

    dh, d_ffng1, g_cw1, g_cb1 = ffn_bwd(dy, h3, hn3, a_ff1, c_ff1, 1, "ffn1")
    do = _dx_rows_t(dh, sch.w("attn_w_o").reshape(D, D), "attn_do", hk("attn_do"))
    sch.grad("attn_w_o", rows(_dw_rows(o, dh, "dw_o", hk("dw_o"), fm=True)))
    dqkv, d_qg, d_kg, d_sk, d_bias = _attn_bwd(qkv, do, qg, kg, sinks, bias, "attn_bwd", hk("attn_bwd"))
    sch.grad("attn_w_qkv", tuple(g.reshape(N_DEV, -1, D) for g in _dw_rows(dqkv, hn2, "dw_qkv", hk("dw_qkv"), fm=True)))
    dh, d_mixg1 = _dx_slot_normbwd(dqkv, sch.w("attn_w_qkv").reshape(QKV, D), h2, mixg[1], dh, "dx_qkv", hk("dx_qkv"), fm=True,
                                   out_dtype=DH)
    d_relb = _relbias_bwd(d_bias.reshape(NH, -1), bucket_row, "relbias_bwd").T
    g_rep = {"attn_q_gain": d_qg.reshape(1, HD), "attn_k_gain": d_kg.reshape(1, HD), "attn_sinks": d_sk.reshape(1, NH),
             "rel_bias": d_relb}
    sch.small(g_rep)
    dh, d_ffng0, g_cw0, g_cb0 = ffn_bwd(dh, h1, hn1, a_ff0, c_ff0, 0, "ffn0")
    g_cw = jnp.concatenate([g_cw0, g_cw1], axis=1)
    sch.grad("ffn_conv_w", (g_cw, g_cw.astype(BF16)))
    g_ffn = {"ffn_norm": jnp.concatenate([d_ffng0, d_ffng1], axis=0), "ffn_conv_b": jnp.stack([g_cb0, g_cb1], axis=0)}
    sch.small(g_ffn)
    dgated = _dx_rows(dh, sch.w("sgu_w_out").reshape(SGU_W, D), SGU_W // 4, BF16, "sgu_dgated", hk("sgu_dgated"))
    sch.grad("sgu_w_out", rows(_dw_rows(gated, dh, "dw_sgu_out", hk("dw_sgu_out"))))
    da0, d_ws, d_bst, d_vgain = _sgu_gate_bwd(a0, dgated, vgain, ws, bst, "sgu_gate_bwd", hk("sgu_gate_bwd"))
    g_sgu = {"sgu_v_gain": d_vgain, "sgu_w_s": d_ws[None], "sgu_b_s": d_bst[:, :SGU_G].T[None]}
    sch.small(g_sgu)
    sch.grad("sgu_w_in", _dw_slot(hn0, da0, "dw_sgu_in", hk("dw_sgu_in")))
    sch.sync("after_dw")
    grad_x, d_mixg0 = _dx_slot_normbwd(da0, sch.w("sgu_w_in"), x, mixg[0], dh, "dx_sgu_in", hk("dx_sgu_in"))
    g_mix = {"mix_norm": jnp.concatenate([d_mixg0, d_mixg1], axis=0)}
    sch.small(g_mix)
    for g in (g_ffn, g_sgu, g_mix):
        g_rep.update(g)
    return loss, grad_x, g_rep


def _allgather(xs, name):
    nt = len(xs)

    def body(*refs):
        x_refs, o_refs = refs[:nt], refs[nt:2 * nt]
        send_sems, recv_sems, local_sems = refs[2 * nt:]
        x, y, c, chips = _place()
        me, sibling = (x, y, c), (x, y, 1 - c)

        def copy(t, k, block, to, src=None):
            px, py, pc = block
            dst = o_refs[t].at[4 * px + 2 * py + pc]
            return pltpu.make_async_remote_copy(
                src_ref=dst if src is None else src, dst_ref=dst, send_sem=send_sems.at[t, k], recv_sem=recv_sems.at[t, k],
                device_id=to, device_id_type=MESH)

        mine = [pltpu.make_async_copy(x_refs[t], o_refs[t].at[4 * x + 2 * y + c], local_sems.at[t]) for t in range(nt)]
        for cp in mine:
            cp.start()
        first = []
        for t in range(nt):
            first.append(copy(t, 0, me, sibling, src=x_refs[t]))
            first += [copy(t, 1 + j, me, (*chip, c), src=x_refs[t]) for j, chip in enumerate(chips)]
        for cp in first:
            cp.start()
        passed = []
        for j, chip in enumerate(chips):
            for t in range(nt):
                copy(t, 1 + j, (*chip, c), me).wait_recv()
                fwd = copy(t, 4 + j, (*chip, c), sibling)
                fwd.start()
                passed.append(fwd)
        for t in range(nt):
            copy(t, 0, sibling, me).wait_recv()
            for j, chip in enumerate(chips):
                copy(t, 4 + j, (*chip, 1 - c), me).wait_recv()
        for cp in first + passed:
            cp.wait_send()
        for cp in mine:
            cp.wait()

    return pl.pallas_call(
        body, name=name, in_specs=[ANY] * nt, out_specs=[ANY] * nt,
        out_shape=[S((N_DEV,) + a.shape, a.dtype) for a in xs],
        scratch_shapes=[pltpu.SemaphoreType.DMA((nt, 7)), pltpu.SemaphoreType.DMA((nt, 7)), pltpu.SemaphoreType.DMA((nt,))],
        compiler_params=pltpu.CompilerParams(has_side_effects=True))(*xs)


def _exchange(hook, name):
    comm = hook()
    ci, co = len(comm.inputs), len(comm.out_shapes)

    def body(*refs):
        cins, couts = refs[:ci], refs[ci:ci + co]
        send, recv = refs[-2:]
        comm.start(cins, couts, send, recv)
        comm.finish(cins, couts, send, recv)

    res = pl.pallas_call(
        body, name=name, in_specs=[ANY] * ci, out_specs=[ANY] * co, out_shape=comm.out_shapes,
        scratch_shapes=[pltpu.SemaphoreType.DMA((comm.n_sems,)), pltpu.SemaphoreType.DMA((comm.n_sems,))],
        input_output_aliases=dict(comm.aliases),
        compiler_params=pltpu.CompilerParams(has_side_effects=True))(*comm.inputs)
    hook(res)


def _row_tile(r):
    tr = r if r <= ROW_TILE or r % ROW_TILE else ROW_TILE
    assert r % tr == 0
    return tr


def _rs_partial(g32, sib, place, name):
    _, r, cdim = g32.shape
    tr = _row_tile(r)

    def body(place_ref, g_ref, s_ref, p_ref, own_ref):
        k = pl.program_id(1)
        tot = g_ref[...] + s_ref[...].astype(F32)
        p_ref[...] = tot.astype(BF16)

        @pl.when(k == place_ref[1])
        def _():
            own_ref[...] = tot

    grid_spec = pltpu.PrefetchScalarGridSpec(
        num_scalar_prefetch=1, grid=(r // tr, 4),
        in_specs=[pl.BlockSpec((None, None, tr, cdim), lambda i, k, pr: (k, pr[0], i, 0)),
                  pl.BlockSpec((None, tr, cdim), lambda i, k, pr: (k, i, 0))],
        out_specs=[pl.BlockSpec((None, tr, cdim), lambda i, k, pr: (k, i, 0)), pl.BlockSpec((tr, cdim), lambda i, k, pr: (i, 0))])
    return pl.pallas_call(
        body, grid_spec=grid_spec, name=name,
        out_shape=[S((4, r, cdim), BF16), S((r, cdim), F32)],
        compiler_params=_cp("parallel", "arbitrary"))(place, g32.reshape(4, 2, r, cdim), sib)


def _adamw_math(w, g, m, v):
    m = ADAM_B1 * m + (1.0 - ADAM_B1) * g
    v = ADAM_B2 * v + (1.0 - ADAM_B2) * (g * g)
    m_hat = m / (1.0 - ADAM_B1 ** ADAM_STEP)
    v_hat = v / (1.0 - ADAM_B2 ** ADAM_STEP)
    delta = -ADAM_LR * (m_hat / (jnp.sqrt(v_hat) + ADAM_EPS) + ADAM_WD * w)
    return delta, m, v


def _adamw_shard(owns, recvs, w, m, v, name, flipped=False):
    nl = w.shape[0]
    r, cdim = owns[0].shape
    tr = _row_tile(r)
    nr = r // tr

    def body(*refs):
        own_refs, recv_refs = refs[:nl], refs[nl:2 * nl]
        w_ref, m_ref, v_ref, g_out, d_out, m_out, v_out = refs[2 * nl:]
        layer = pl.program_id(0)
        g = None
        for l in range(nl):
            gl = own_refs[l][...] + recv_refs[l][0].astype(F32) + recv_refs[l][1].astype(F32) + recv_refs[l][2].astype(F32)
            g = gl if g is None else jnp.where(layer == l, gl, g)
        if flipped:
            g = g.T
        g_out[...] = g
        d_out[...], m_out[...], v_out[...] = _adamw_math(w_ref[...], g, m_ref[...], v_ref[...])

    park = lambda l: (lambda layer, i: (jnp.where(layer == l, i, jnp.where(layer < l, 0, nr - 1)), 0))
    park3 = lambda l: (lambda layer, i: (0, jnp.where(layer == l, i, jnp.where(layer < l, 0, nr - 1)), 0))
    if flipped:
        row = pl.BlockSpec((None, cdim, tr), lambda layer, i: (layer, 0, i))
    else:
        row = pl.BlockSpec((None, tr, cdim), lambda layer, i: (layer, i, 0))
    return pl.pallas_call(
        body, grid=(nl, nr), name=name,
        in_specs=[pl.BlockSpec((tr, cdim), park(l)) for l in range(nl)] + [pl.BlockSpec((3, tr, cdim), park3(l)) for l in range(nl)]
        + [row, row, row],
        out_specs=[row] * 4, out_shape=[S(w.shape, F32)] * 4,
        compiler_params=_cp("arbitrary", "arbitrary"))(*owns, *recvs, w, m, v)


def _adamw_small(galls, ws, ms, vs, name):
    n = len(galls)

    def body(*refs):
        g_refs, w_refs, m_refs, v_refs, outs = refs[:n], refs[n:2 * n], refs[2 * n:3 * n], refs[3 * n:4 * n], refs[4 * n:]
        for i in range(n):
            g = g_refs[i][0].astype(F32)
            for s in range(1, N_DEV):
                g = g + g_refs[i][s].astype(F32)
            outs[i][...] = g
            outs[n + i][...], outs[2 * n + i][...], outs[3 * n + i][...] = _adamw_math(w_refs[i][...], g, m_refs[i][...], v_refs[i][...])

    res = pl.pallas_call(body, out_shape=[S(a.shape, F32) for a in ws] * 4, name=name)(*galls, *ws, *ms, *vs)
    return [res[k * n:(k + 1) * n] for k in range(4)]


REPLICATED = ["mix_norm", "ffn_norm", "sgu_v_gain", "sgu_w_s", "sgu_b_s", "attn_q_gain", "attn_k_gain", "attn_sinks", "rel_bias",
              "ffn_conv_b"]
WEIGHTS = ["mix_norm", "ffn_norm", "sgu_w_in", "sgu_v_gain", "sgu_w_s", "sgu_b_s", "sgu_w_out", "attn_w_qkv", "attn_q_gain",
           "attn_k_gain", "attn_sinks", "attn_w_o", "rel_bias", "ffn_w_up", "ffn_conv_w", "ffn_conv_b", "ffn_w_down"]
SMALL = ["g_" + n for n in REPLICATED]
BF16_TRANSIT = {"sgu_w_s"}
SMALL_ATTN = ["g_attn_q_gain", "g_attn_k_gain", "g_attn_sinks", "g_rel_bias"]
SMALL_FFN = ["g_ffn_norm", "g_ffn_conv_b"]
SMALL_SGU = ["g_sgu_v_gain", "g_sgu_w_s", "g_sgu_b_s"]

GATHER_FIRST = ["sgu_w_in", "ffn_conv_w"]
UP0_SPLIT, UP1_SPLIT = 352, 304
PLAN = {
    "sgu_in": [("ag1", "sgu_w_out"), ("ag1", "ffn_w_up0", (0, UP0_SPLIT))],
    "sgu_gate": [("ag2", "sgu_w_out"), ("ag1", "ffn_w_up0", (UP0_SPLIT, D))],
    "sgu_out": [("ag2", "ffn_w_up0"), ("ag1", "ffn_w_down0")],
    "before_ffn0": [("ag2", "ffn_w_down0")],
    "ffn0_fwd": [("agd", "attn_w_qkv"), ("ag1", "attn_w_o"), ("ag1", "ffn_w_down1")],
    "qkv": [("ag2", "attn_w_o"), ("ag2", "ffn_w_down1"), ("ag1", "ffn_w_up1", (0, UP1_SPLIT))],
    "attn": [("ag1", "ffn_w_up1", (UP1_SPLIT, D))],
    "attn_out": [("ag2", "ffn_w_up1")],
    "ffn1_dw_up": [("rs1", "ffn_w_down1")],
    "attn_bwd": [("rs2", "ffn_w_down1"), ("rs1", "ffn_w_up1"), ("rs1", "attn_w_o")],
    "ffn0_bwd1": [("rs2", "ffn_w_up1"), ("rs2", "attn_w_o"), ("rs1", "attn_w_qkv")] + [("ag1", n) for n in SMALL_ATTN],
    "ffn0_bwd2": [("rs2", "attn_w_qkv"), ("rs1", "ffn_w_down0")] + [("ag2", n) for n in SMALL_ATTN],
    "ffn0_dw_up": [("rs2", "ffn_w_down0")],
    "sgu_dgated": [("rs1", "ffn_w_up0")] + [("agd", n) for n in SMALL_FFN],
    "sgu_gate_bwd": [("rs2", "ffn_w_up0"), ("rs1", "sgu_w_out")],
    "dw_sgu_in": [("rs2", "sgu_w_out")] + [("ag1", n) for n in SMALL_SGU],
    "after_dw": [("rs1", "sgu_w_in"), ("rs1", "ffn_conv_w")] + [("ag2", n) for n in SMALL_SGU],
    "dx_sgu_in": [("rs2", "sgu_w_in"), ("rs2", "ffn_conv_w")],
    "last": [("agd", "g_mix_norm")],
}


class _Overlap:
    def __init__(self, shard, place):
        self.shard, self.place = shard, place
        self.part, self.full = {}, {}
        self.grads, self.sib, self.own, self.recv = {}, {}, {}, {}

    def w(self, n):
        return self.full[n]

    def grad(self, n, pair):
        self.grads[n] = pair

    def small(self, g_rep):
        self.shard.update(("g_" + n, a.astype(BF16) if n in BF16_TRANSIT else a) for n, a in _views2d(g_rep).items())

    def sync(self, point):
        _exchange(self.hook(point), point)

    def chip_sums(self, n):
        sums, self.own[n] = _rs_partial(self.grads[n][0], self.sib.pop(n), self.place, "rs_partial_" + n)
        return sums

    def hook(self, host):
        ops = PLAN.get(host)
        if not ops:
            return None
        where = {"ag1": self.part, "ag2": self.full, "agd": self.full, "rs1": self.sib, "rs2": self.recv}
        idx = []

        def hook(results=None):
            if results is not None:
                for (kind, n, *_), i in zip(ops, idx):
                    where[kind][n] = results[i]
                return None
            comm = _Comm()
            for kind, n, *rows in ops:
                arr = {"ag1": lambda: self.shard[n], "agd": lambda: self.shard[n], "ag2": lambda: self.part.pop(n),
                       "rs1": lambda: self.grads[n][1], "rs2": lambda: self.chip_sums(n)}[kind]()
                idx.append(comm.add(kind, arr, *rows, into=self.part.pop(n) if rows and rows[0][0] else None))
            return comm

        return hook


TRANSPOSED = {"attn_w_qkv"}
PHYSICAL_T = {"ffn_w_up"}
SHARDED = {
    "sgu_w_in": ["sgu_w_in"], "sgu_w_out": ["sgu_w_out"], "attn_w_qkv": ["attn_w_qkv"], "attn_w_o": ["attn_w_o"],
    "ffn_w_up": ["ffn_w_up0", "ffn_w_up1"], "ffn_w_down": ["ffn_w_down0", "ffn_w_down1"], "ffn_conv_w": ["ffn_conv_w"],
}


def _send_views(w):
    out = {"ffn_conv_w": w["ffn_conv_w"].reshape(6, -1)}
    for name, parts in SHARDED.items():
        if name != "ffn_conv_w":
            out.update((p, (w[name][l].T if name in TRANSPOSED else w[name][l]).astype(BF16)) for l, p in enumerate(parts))
    return out


def _views2d(d):
    return {n: d[n].reshape(-1, d[n].shape[-1]) for n in REPLICATED if n in d}


def kernel(x, mix_norm, ffn_norm, sgu_w_in, sgu_v_gain, sgu_w_s, sgu_b_s, sgu_w_out, attn_w_qkv, attn_q_gain, attn_k_gain, attn_sinks, attn_w_o, rel_bias, ffn_w_up, ffn_conv_w, ffn_conv_b, ffn_w_down, loss_target, m_mix_norm, m_ffn_norm, m_sgu_w_in, m_sgu_v_gain, m_sgu_w_s, m_sgu_b_s, m_sgu_w_out, m_attn_w_qkv, m_attn_q_gain, m_attn_k_gain, m_attn_sinks, m_attn_w_o, m_rel_bias, m_ffn_w_up, m_ffn_conv_w, m_ffn_conv_b, m_ffn_w_down, v_mix_norm, v_ffn_norm, v_sgu_w_in, v_sgu_v_gain, v_sgu_w_s, v_sgu_b_s, v_sgu_w_out, v_attn_w_qkv, v_attn_q_gain, v_attn_k_gain, v_attn_sinks, v_attn_w_o, v_rel_bias, v_ffn_w_up, v_ffn_conv_w, v_ffn_conv_b, v_ffn_w_down):
    w = dict(zip(WEIGHTS, (mix_norm, ffn_norm, sgu_w_in, sgu_v_gain, sgu_w_s, sgu_b_s, sgu_w_out, attn_w_qkv, attn_q_gain, attn_k_gain,
                           attn_sinks, attn_w_o, rel_bias, ffn_w_up, ffn_conv_w, ffn_conv_b, ffn_w_down)))
    m = dict(zip(WEIGHTS, (m_mix_norm, m_ffn_norm, m_sgu_w_in, m_sgu_v_gain, m_sgu_w_s, m_sgu_b_s, m_sgu_w_out, m_attn_w_qkv, m_attn_q_gain,
                           m_attn_k_gain, m_attn_sinks, m_attn_w_o, m_rel_bias, m_ffn_w_up, m_ffn_conv_w, m_ffn_conv_b, m_ffn_w_down)))
    v = dict(zip(WEIGHTS, (v_mix_norm, v_ffn_norm, v_sgu_w_in, v_sgu_v_gain, v_sgu_w_s, v_sgu_b_s, v_sgu_w_out, v_attn_w_qkv, v_attn_q_gain,
                           v_attn_k_gain, v_attn_sinks, v_attn_w_o, v_rel_bias, v_ffn_w_up, v_ffn_conv_w, v_ffn_conv_b, v_ffn_w_down)))
    rep = {n: w[n] for n in REPLICATED}

    xi, yi, ci = lax.axis_index("x"), lax.axis_index("y"), lax.axis_index("c")
    place = jnp.stack([ci, 2 * xi + yi]).astype(jnp.int32)
    sch = _Overlap(_send_views(w), place)
    sch.full.update(zip(GATHER_FIRST, _allgather([sch.shard[n] for n in GATHER_FIRST], "gather_first")))

    loss, grad_x, g_rep = _local_step(x[0], loss_target[0], rep, sch)
    loss = lax.psum(loss, ("x", "y", "c"))
    sch.sync("last")

    out = [{}, {}, {}, {}]
    for name, parts in SHARDED.items():
        flip = (lambda a: jnp.swapaxes(a, -1, -2)) if name in TRANSPOSED | PHYSICAL_T else (lambda a: a)
        shape = flip(w[name]).shape
        as3d = lambda a: flip(a).reshape(len(parts), -1, shape[-1])
        res = _adamw_shard([sch.own[p] for p in parts], [sch.recv[p] for p in parts], as3d(w[name]), as3d(m[name]), as3d(v[name]),
                           "adamw_" + name, flipped=name in PHYSICAL_T)
        for o, r in zip(out, res):
            o[name] = flip(r.reshape(shape))
    small = _adamw_small([sch.full[n] for n in SMALL], *[list(_views2d(d).values()) for d in (rep, m, v)], "adamw_small")
    for o, res in zip(out, small):
        o.update((n, r.reshape(w[n].shape)) for n, r in zip(REPLICATED, res))

    return (loss, grad_x[None], *[out[0][n] for n in WEIGHTS], *[out[1][n] for n in WEIGHTS],
            *[out[2][n] for n in WEIGHTS], *[out[3][n] for n in WEIGHTS])
```

```python
import functools
import math

import numpy as np
import jax
import jax.numpy as jnp
from jax import lax
from jax.experimental import pallas as pl
from jax.experimental.pallas import tpu as pltpu

F32 = jnp.float32
BF16 = jnp.bfloat16
DH = jnp.bfloat16
S = jax.ShapeDtypeStruct

D = 1024
CHUNK = 128
SGU_W = 2048
SGU_G = 16
HD = 64
NH = 16
NKV = 4
KVG = 4
D_FF = 2816
REL_BUCKETS = 32
REL_MAX_DIST = 128
EPS = 1e-6
N_DEV = 8
MESH = pl.DeviceIdType.MESH

ADAM_LR = 0.001
ADAM_B1 = 0.9
ADAM_B2 = 0.999
ADAM_EPS = 1e-08
ADAM_WD = 0.01
ADAM_STEP = 10

ROW_TILE = 512
HALO = 8
FFN_ROWS = 256


def _tm(t):
    return min(ROW_TILE, t)


def _cp(*sem):
    return pltpu.CompilerParams(dimension_semantics=sem)


ANY = pl.BlockSpec(memory_space=pl.ANY)


def _place():
    x, y, c = lax.axis_index("x"), lax.axis_index("y"), lax.axis_index("c")
    return x, y, c, [(1 - x, y), (x, 1 - y), (1 - x, 1 - y)]


class _Comm:
    SEMS = {"ag1": 5, "ag2": 3, "rs1": 4, "rs2": 3, "agd": 8}

    def __init__(self):
        self.inputs, self.out_shapes, self.aliases, self.ops, self.n_sems = [], [], {}, [], 0

    def add(self, kind, arr, rows=None, into=None):
        lead = {"ag1": N_DEV, "agd": N_DEV, "ag2": None, "rs1": 4, "rs2": 3}[kind]
        shape = arr.shape if lead is None else (lead,) + arr.shape[(0 if kind in ("ag1", "agd") else 1):]
        if kind == "ag2":
            self.aliases[len(self.inputs)] = len(self.out_shapes)
        self.ops.append((kind, len(self.inputs), len(self.out_shapes), self.n_sems, rows))
        self.inputs.append(arr)
        if into is not None:
            self.aliases[len(self.inputs)] = len(self.out_shapes)
            self.inputs.append(into)
        self.out_shapes.append(S(shape, arr.dtype))
        self.n_sems += self.SEMS[kind]
        return len(self.out_shapes) - 1

    def _copies(self, ins, outs, send, recv):
        x, y, c, chips = _place()
        me, sibling = (x, y, c), (x, y, 1 - c)
        slot = lambda px, py, pc: 4 * px + 2 * py + pc
        sends, recvs, local = [], [], []

        def rc(src, dst, k, to):
            return lambda: pltpu.make_async_remote_copy(src_ref=src(), dst_ref=dst(), send_sem=send.at[k], recv_sem=recv.at[k],
                                                        device_id=to, device_id_type=MESH)

        for kind, ii, oi, b, rows in self.ops:
            src, dst = ins[ii], outs[oi]
            at = lambda ref, i: (lambda: ref.at[i])
            if kind == "ag1":
                part = slice(None) if rows is None else pl.ds(rows[0], rows[1] - rows[0])
                to = lambda i, d=dst, p=part: (lambda: d.at[i, p])
                whole, mine = (lambda s=src, p=part: s.at[p]), to(slot(*me))
                sends.append(rc(whole, mine, b, sibling))
                recvs.append(rc(whole, to(slot(x, y, 1 - c)), b, me))
                for j, chip in enumerate(chips):
                    sends.append(rc(whole, mine, b + 1 + j, (*chip, c)))
                    recvs.append(rc(whole, to(slot(*chip, c)), b + 1 + j, me))
                local.append(lambda s=whole, m=mine, k=b + 4: pltpu.make_async_copy(s(), m(), send.at[k]))
            elif kind == "ag2":
                for j, chip in enumerate(chips):
                    sends.append(rc(at(dst, slot(*chip, c)), at(dst, slot(*chip, c)), b + j, sibling))
                    recvs.append(rc(at(dst, slot(*chip, 1 - c)), at(dst, slot(*chip, 1 - c)), b + j, me))
            elif kind == "agd":
                whole, mine = (lambda s=src: s), at(dst, slot(*me))
                flip = lambda v, bit: 1 - v if bit else v
                for k in range(1, N_DEV):
                    peer = (flip(x, k >> 2), flip(y, (k >> 1) & 1), flip(c, k & 1))
                    sends.append(rc(whole, mine, b + k - 1, peer))
                    recvs.append(rc(whole, at(dst, slot(*peer)), b + k - 1, me))
                local.append(lambda s=src, m=mine, k=b + 7: pltpu.make_async_copy(s, m(), send.at[k]))
            elif kind == "rs1":
                for k in range(4):
                    sends.append(rc(at(src, 2 * k + (1 - c)), at(dst, k), b + k, sibling))
                    recvs.append(rc(at(src, 2 * k + c), at(dst, k), b + k, me))
            else:
                for j, (px, py) in enumerate(chips):
                    sends.append(rc(at(src, 2 * px + py), at(dst, j), b + j, (px, py, c)))
                    recvs.append(rc(at(src, 2 * px + py), at(dst, j), b + j, me))
        return sends, recvs, local

    def start(self, ins, outs, send, recv):
        sends, _, local = self._copies(ins, outs, send, recv)
        for make in local + sends:
            make().start()

    def finish(self, ins, outs, send, recv):
        sends, recvs, local = self._copies(ins, outs, send, recv)
        for make in recvs:
            make().wait_recv()
        for make in sends:
            make().wait_send()
        for make in local:
            make().wait()


def _run(body, args, hook, *, grid, in_specs, out_specs, out_shape, name, semantics, scratch_shapes=(), aliases=None):
    comm = hook() if hook is not None else None
    aliases = dict(aliases or {})
    if comm is None:
        return pl.pallas_call(body, grid=grid, in_specs=in_specs, out_specs=out_specs, out_shape=out_shape, name=name,
                              scratch_shapes=list(scratch_shapes), input_output_aliases=aliases,
                              compiler_params=_cp(*semantics))(*args)
    single = not isinstance(out_shape, (list, tuple))
    out_shapes = [out_shape] if single else list(out_shape)
    out_specs_l = [out_specs] if single else list(out_specs)
    n_in, n_out, n_scr, ci, co = len(args), len(out_shapes), len(scratch_shapes), len(comm.inputs), len(comm.out_shapes)

    def wrapped(*refs):
        ins, cins = refs[:n_in], refs[n_in:n_in + ci]
        outs, couts = refs[n_in + ci:n_in + ci + n_out], refs[n_in + ci + n_out:n_in + ci + n_out + co]
        scr = refs[n_in + ci + n_out + co:n_in + ci + n_out + co + n_scr]
        send, recv = refs[-2:]
        first = functools.reduce(lambda a, b: a & b, [pl.program_id(a) == 0 for a in range(len(grid))])
        last = functools.reduce(lambda a, b: a & b, [pl.program_id(a) == g - 1 for a, g in enumerate(grid)])

        @pl.when(first)
        def _():
            comm.start(cins, couts, send, recv)

        body(*ins, *outs, *scr)

        @pl.when(last)
        def _():
            comm.finish(cins, couts, send, recv)

    res = pl.pallas_call(
        wrapped, grid=grid, in_specs=list(in_specs) + [ANY] * ci, out_specs=out_specs_l + [ANY] * co,
        out_shape=out_shapes + comm.out_shapes, name=name,
        scratch_shapes=list(scratch_shapes) + [pltpu.SemaphoreType.DMA((comm.n_sems,)), pltpu.SemaphoreType.DMA((comm.n_sems,))],
        input_output_aliases={**aliases, **{n_in + k: n_out + v for k, v in comm.aliases.items()}},
        compiler_params=pltpu.CompilerParams(dimension_semantics=("arbitrary",) * len(grid), has_side_effects=True))(*args, *comm.inputs)
    hook(res[n_out:])
    return res[0] if single else list(res[:n_out])


def _dot(a, b):
    return jnp.dot(a, b, preferred_element_type=F32)


def _dot_nt(a, b):
    return lax.dot_general(a, b, (((1,), (1,)), ((), ())), preferred_element_type=F32)


def _dot_tn(a, b):
    return lax.dot_general(a, b, (((0,), (0,)), ((), ())), preferred_element_type=F32)


def _gelu(x):
    return 0.5 * x * (1.0 + lax.erf(x * (2.0 ** -0.5)))


def _gelu_and_grad(x):
    cdf = 0.5 * (1.0 + lax.erf(x * (2.0 ** -0.5)))
    return x * cdf, cdf + x * jnp.exp(-0.5 * x * x) * (1.0 / math.sqrt(2.0 * math.pi))


def _sigmoid(x):
    return 1.0 / (1.0 + jnp.exp(-x))


def _rstd(x):
    return lax.rsqrt(jnp.mean(x * x, axis=-1, keepdims=True) + EPS)


def _rel_tables():
    q = np.arange(CHUNK)[:, None] + CHUNK
    k = np.arange(2 * CHUNK)[None, :]
    dist = q - k
    n = np.maximum(dist, 0)
    max_exact = REL_BUCKETS // 2
    large = max_exact + (np.log(np.maximum(n, 1).astype(np.float32) / max_exact)
                         / math.log(REL_MAX_DIST / max_exact) * (REL_BUCKETS - max_exact)).astype(np.int32)
    large = np.minimum(large, REL_BUCKETS - 1)
    return np.where(n < max_exact, n, large).astype(np.int32)


def _rmsnorm(x, gain, name):
    t = x.shape[0]
    tm = _tm(t)

    def body(x_ref, g_ref, o_ref):
        xv = x_ref[...]
        o_ref[...] = (xv * _rstd(xv) * g_ref[...]).astype(BF16)

    return pl.pallas_call(
        body, grid=(t // tm,), name=name,
        in_specs=[pl.BlockSpec((tm, D), lambda i: (i, 0)), pl.BlockSpec((1, D), lambda i: (0, 0))],
        out_specs=pl.BlockSpec((tm, D), lambda i: (i, 0)),
        out_shape=S((t, D), BF16), compiler_params=_cp("parallel"))(x, gain)


def _resident(shape):
    zeros = (0,) * len(shape)
    return pl.BlockSpec(shape, lambda *_: zeros, pipeline_mode=pl.Buffered(1))


def _mm_slot(hn, wg, out_dtype, name, hook=None):
    t, k = hn.shape
    ns, _, n = wg.shape
    tm = _tm(t)

    def body(a_ref, w_ref, o_ref):
        a = a_ref[...]
        for s in range(ns):
            o_ref[s] = _dot(a, w_ref[s]).astype(out_dtype)

    return _run(
        body, [hn, wg], hook, grid=(t // tm,), name=name, semantics=("parallel",),
        in_specs=[pl.BlockSpec((tm, k), lambda i: (i, 0)), _resident(wg.shape)],
        out_specs=pl.BlockSpec((ns, tm, n), lambda i: (0, i, 0)), out_shape=S((ns, t, n), out_dtype))


def _mm_t(hn, wt, name, hook=None):
    t, k = hn.shape
    ns, n, _ = wt.shape
    tm = _tm(t)

    def body(a_ref, w_ref, o_ref):
        a = a_ref[...]
        for s in range(ns):
            o_ref[s * n:(s + 1) * n, :] = _dot_nt(w_ref[s], a)

    return _run(
        body, [hn, wt], hook, grid=(t // tm,), name=name, semantics=("parallel",),
        in_specs=[pl.BlockSpec((tm, k), lambda i: (i, 0)), _resident(wt.shape)],
        out_specs=pl.BlockSpec((ns * n, tm), lambda i: (0, i)), out_shape=S((ns * n, t), F32))


def _conv3(a, prev, cw, cb, tm):
    ext = jnp.concatenate([prev, a], axis=0)
    return cw[2:3] * a + cw[1:2] * ext[HALO - 1:HALO - 1 + tm] + cw[0:1] * ext[HALO - 2:HALO - 2 + tm] + cb


def _ffn_fwd(hn, h, wup, wdown, cw, cb, extra, mode, name, hook=None):
    t, k = hn.shape
    n = wup.shape[-1]
    nh = wup.shape[0] // 2
    tm = min(FFN_ROWS, t)
    ni = t // tm

    def body(a_ref, h_ref, wu_ref, wd_ref, cw_ref, cb_ref, e_ref, as_ref, cs_ref, o1_ref, o2_ref, carry):
        i = pl.program_id(0)

        @pl.when(i == 0)
        def _():
            carry[...] = jnp.zeros_like(carry)

        a = a_ref[...]
        acc = h_ref[...]
        nxt = (_dot(a, wu_ref[0]), _dot(a, wu_ref[nh]))
        for j in range(nh):
            ag, av = nxt
            if j + 1 < nh:
                nxt = (_dot(a, wu_ref[j + 1]), _dot(a, wu_ref[nh + j + 1]))
            as_ref[j] = ag.astype(BF16)
            as_ref[nh + j] = av.astype(BF16)
            cg = _conv3(ag, carry[j], cw_ref[j], cb_ref[j], tm)
            cv = _conv3(av, carry[nh + j], cw_ref[nh + j], cb_ref[nh + j], tm)
            carry[j] = ag[tm - HALO:]
            carry[nh + j] = av[tm - HALO:]
            cs_ref[j] = cg.astype(BF16)
            cs_ref[nh + j] = cv.astype(BF16)
            act = (cg * _sigmoid(cg) * cv).astype(BF16)
            acc = acc + _dot(act, wd_ref[j * n:(j + 1) * n, :])
        if mode == "norm":
            o1_ref[...] = acc
            o2_ref[...] = (acc * _rstd(acc) * e_ref[...]).astype(BF16)
        else:
            err = acc - e_ref[...]
            o1_ref[...] = (err * (1.0 / D)).astype(o1_ref.dtype)
            o2_ref[...] = jnp.full(o2_ref.shape, jnp.sum(err * err), F32)

    row = pl.BlockSpec((tm, D), lambda i: (i, 0))
    if mode == "norm":
        e_spec, o2_spec, o2_shape = pl.BlockSpec((1, D), lambda i: (0, 0)), row, S((t, D), BF16)
    else:
        e_spec, o2_spec, o2_shape = row, pl.BlockSpec((None, 8, 128), lambda i: (i, 0, 0)), S((ni, 8, 128), F32)
    aspec = pl.BlockSpec((2 * nh, tm, n), lambda i: (0, i, 0))
    return _run(
        body, [hn, h, wup, wdown, cw, cb, extra], hook, grid=(ni,), name=name, semantics=("arbitrary",),
        in_specs=[pl.BlockSpec((tm, k), lambda i: (i, 0)), row, _resident(wup.shape), _resident(wdown.shape),
                  _resident(cw.shape), _resident(cb.shape), e_spec],
        out_specs=[aspec, aspec, row, o2_spec],
        out_shape=[S((2 * nh, t, n), BF16), S((2 * nh, t, n), BF16), S((t, D), F32 if mode == "norm" else DH), o2_shape],
        scratch_shapes=[pltpu.VMEM((2 * nh, HALO, n), F32)])


def _tril_mask():
    r = lax.broadcasted_iota(jnp.int32, (CHUNK, CHUNK), 0)
    c = lax.broadcasted_iota(jnp.int32, (CHUNK, CHUNK), 1)
    return r >= c


def _sgu_gate_fwd(a_s, vgain, ws, bst, name, hook=None):
    t = a_s.shape[1]
    sw = a_s.shape[2]
    gps = sw // CHUNK

    def body(a_ref, vg_ref, ws_ref, b_ref, o_ref):
        v = _gelu(jnp.concatenate([a_ref[4 + s].astype(F32) for s in range(4)], axis=1))
        vn = (v * _rstd(v) * vg_ref[...]).astype(BF16)
        tri = _tril_mask()
        for g in range(SGU_G):
            w = jnp.where(tri, ws_ref[g], 0.0).astype(BF16)
            sg = _dot(w, vn[:, g * CHUNK:(g + 1) * CHUNK]) + b_ref[:, g:g + 1]
            lo = (g % gps) * CHUNK
            u = _gelu(a_ref[g // gps, :, lo:lo + CHUNK].astype(F32))
            o_ref[g // gps, :, lo:lo + CHUNK] = (u * sg).astype(BF16)

    return _run(
        body, [a_s, vgain, ws, bst], hook, grid=(t // CHUNK,), name=name, semantics=("parallel",),
        in_specs=[pl.BlockSpec((8, CHUNK, sw), lambda n: (0, n, 0)), pl.BlockSpec((1, SGU_W), lambda n: (0, 0)),
                  pl.BlockSpec((SGU_G, CHUNK, CHUNK), lambda n: (0, 0, 0)), pl.BlockSpec((CHUNK, SGU_G), lambda n: (0, 0))],
        out_specs=pl.BlockSpec((4, CHUNK, sw), lambda n: (0, n, 0)), out_shape=S((4, t, sw), BF16))


def _resid_mm(a_s, w, resid, extra, mode, name, hook=None, fm=False):
    nk, t, kc = (1, a_s.shape[1], a_s.shape[0]) if fm else a_s.shape
    tm = _tm(t)
    ni = t // tm

    def body(a_ref, w_ref, r_ref, e_ref, o1_ref, o2_ref):
        h = r_ref[...]
        if fm:
            h = h + _dot_tn(a_ref[...], w_ref[...])
        for j in range(0 if fm else nk):
            h = h + _dot(a_ref[j], w_ref[j * kc:(j + 1) * kc, :])
        if mode == "norm":
            o1_ref[...] = h
            o2_ref[...] = (h * _rstd(h) * e_ref[...]).astype(BF16)
        else:
            err = h - e_ref[...]
            o1_ref[...] = (err * (1.0 / D)).astype(o1_ref.dtype)
            o2_ref[...] = jnp.full(o2_ref.shape, jnp.sum(err * err), F32)

    row = pl.BlockSpec((tm, D), lambda i: (i, 0))
    if mode == "norm":
        e_spec, o2_spec, o2_shape = pl.BlockSpec((1, D), lambda i: (0, 0)), row, S((t, D), BF16)
    else:
        e_spec, o2_spec, o2_shape = row, pl.BlockSpec((None, 8, 128), lambda i: (i, 0, 0)), S((ni, 8, 128), F32)
    return _run(
        body, [a_s, w, resid, extra], hook, grid=(ni,), name=name, semantics=("parallel",),
        in_specs=[pl.BlockSpec((kc, tm), lambda i: (0, i)) if fm else pl.BlockSpec((nk, tm, kc), lambda i: (0, i, 0)),
                  _resident(w.shape), row, e_spec],
        out_specs=[row, o2_spec], out_shape=[S((t, D), F32 if mode == "norm" else DH), o2_shape])


def _relbias_fwd(rel_bias_t, bucket_row, name):
    nb = bucket_row.shape[1]

    def body(rb_ref, bk_ref, o_ref):
        onehot = (lax.broadcasted_iota(jnp.int32, (REL_BUCKETS, nb), 0) == bk_ref[...]).astype(F32)
        o_ref[...] = jnp.dot(rb_ref[...], onehot, precision=lax.Precision.HIGHEST, preferred_element_type=F32)

    return pl.pallas_call(body, out_shape=S((NH, nb), F32), name=name)(rel_bias_t, bucket_row)


def _relbias_bwd(dbias, bucket_row, name):
    nb = bucket_row.shape[1]

    def body(db_ref, bk_ref, o_ref):
        onehot = (lax.broadcasted_iota(jnp.int32, (REL_BUCKETS, nb), 0) == bk_ref[...]).astype(F32)
        o_ref[...] = lax.dot_general(db_ref[...], onehot, (((1,), (1,)), ((), ())),
                                     precision=lax.Precision.HIGHEST, preferred_element_type=F32)

    return pl.pallas_call(body, out_shape=S((NH, REL_BUCKETS), F32), name=name)(dbias, bucket_row)


QKV = D + 2 * NKV * HD
KV0 = D


def _rstd_rows(x):
    return lax.rsqrt(jnp.mean(x * x, axis=0, keepdims=True) + EPS)


def _attn_valid(n):
    kj = lax.broadcasted_iota(jnp.int32, (2 * CHUNK, CHUNK), 0)
    qi = lax.broadcasted_iota(jnp.int32, (2 * CHUNK, CHUNK), 1)
    dist = qi + CHUNK - kj
    return (dist >= 0) & (dist < CHUNK) & ((n > 0) | (kj >= CHUNK))


def _attn_band(cur_ref, prev_ref, row):
    return jnp.concatenate([prev_ref[row - KV0:row - KV0 + HD, :], cur_ref[row:row + HD, :]], axis=1)


def _attn_probs(kn_tok, qn, bias, valid, sink):
    s = _dot(kn_tok, qn) * (HD ** -0.5) + bias
    s = jnp.where(valid, s, -jnp.inf)
    m = jnp.maximum(jnp.max(s, axis=0, keepdims=True), sink)
    p = jnp.exp(s - m)
    psink = jnp.exp(sink - m)
    inv = 1.0 / (jnp.sum(p, axis=0, keepdims=True) + psink)
    return p * inv, psink * inv


def _attn_fwd(qkv_t, qg, kg, sinks, bias, name, hook=None):
    t = qkv_t.shape[1]

    def body(cur_ref, prev_ref, qg_ref, kg_ref, sink_ref, bias_ref, o_ref):
        n = pl.program_id(0)
        valid = _attn_valid(n)
        ks = [_attn_band(cur_ref, prev_ref, KV0 + HD * h) for h in range(NKV)]
        kn_toks = [(k * _rstd_rows(k) * kg_ref[...]).astype(BF16).T for k in ks]
        vbs = [_attn_band(cur_ref, prev_ref, KV0 + HD * (NKV + h)).astype(BF16) for h in range(NKV)]
        qs = [cur_ref[HD * hq:HD * (hq + 1), :] for hq in range(NH)]
        qns = [(q * _rstd_rows(q) * qg_ref[...]).astype(BF16) for q in qs]
        ps = [_attn_probs(kn_toks[hq // KVG], qns[hq], bias_ref[hq], valid, sink_ref[hq])[0] for hq in range(NH)]
        for hq in range(NH):
            o_ref[HD * hq:HD * (hq + 1), :] = _dot(vbs[hq // KVG], ps[hq].astype(BF16)).astype(BF16)

    col = pl.BlockSpec((HD, 1), lambda n: (0, 0))
    return _run(
        body, [qkv_t, qkv_t, qg, kg, sinks, bias], hook, grid=(t // CHUNK,), name=name, semantics=("parallel",),
        in_specs=[pl.BlockSpec((QKV, CHUNK), lambda n: (0, n)),
                  pl.BlockSpec((QKV - KV0, CHUNK), lambda n: (KV0 // (QKV - KV0), jnp.maximum(n - 1, 0))),
                  col, col, pl.BlockSpec(memory_space=pltpu.SMEM), pl.BlockSpec((NH, 2 * CHUNK, CHUNK), lambda n: (0, 0, 0))],
        out_specs=pl.BlockSpec((D, CHUNK), lambda n: (0, n)), out_shape=S((D, t), BF16))


def _dx_rows(dh, w, kc, out_dtype, name, hook=None):
    t = dh.shape[0]
    nk = w.shape[0] // kc
    tm = _tm(t)

    def body(d_ref, w_ref, o_ref):
        dhb = d_ref[...].astype(BF16)
        for j in range(nk):
            o_ref[j] = _dot_nt(dhb, w_ref[j * kc:(j + 1) * kc, :]).astype(out_dtype)

    return _run(
        body, [dh, w], hook, grid=(t // tm,), name=name, semantics=("parallel",),
        in_specs=[pl.BlockSpec((tm, D), lambda i: (i, 0)), _resident(w.shape)],
        out_specs=pl.BlockSpec((nk, tm, kc), lambda i: (0, i, 0)), out_shape=S((nk, t, kc), out_dtype))


def _dx_rows_t(dh, w, name, hook=None):
    t = dh.shape[0]
    k = w.shape[0]
    tm = _tm(t)

    def body(d_ref, w_ref, o_ref):
        o_ref[...] = _dot_nt(w_ref[...], d_ref[...].astype(BF16)).astype(BF16)

    return _run(
        body, [dh, w], hook, grid=(t // tm,), name=name, semantics=("parallel",),
        in_specs=[pl.BlockSpec((tm, D), lambda i: (i, 0)), _resident(w.shape)],
        out_specs=pl.BlockSpec((k, tm), lambda i: (0, i)), out_shape=S((k, t), BF16))


def _ffn_bwd1(dh, c, wdown, name, hook=None):
    ns, t, n = c.shape
    nh = ns // 2
    tm = min(FFN_ROWS, t)
    ni = t // tm

    def body(d_ref, c_ref, wd_ref, dc_ref, dw_hbm, dwb_hbm, acc, stage):
        i = pl.program_id(0)

        @pl.when(i == 0)
        def _():
            acc[...] = jnp.zeros_like(acc)

        dhb = d_ref[...].astype(BF16)
        for j in range(nh):
            dact = _dot_nt(dhb, wd_ref[j * n:(j + 1) * n, :])
            cg = c_ref[j].astype(F32)
            cv = c_ref[nh + j].astype(F32)
            sg = _sigmoid(cg)
            gs = cg * sg
            acc[j * n:(j + 1) * n, :] += _dot_tn((gs * cv).astype(BF16), dhb)
            dc_ref[j] = (dact * cv * (sg + gs * (1.0 - sg))).astype(BF16)
            dc_ref[nh + j] = (dact * gs).astype(BF16)

        @pl.when(i == ni - 1)
        def _():
            pltpu.sync_copy(acc, dw_hbm)
            for j in range(nh):
                stage[...] = acc[j * n:(j + 1) * n, :].astype(BF16)
                pltpu.sync_copy(stage, dwb_hbm.at[pl.ds(j * n, n), :])

    slab = pl.BlockSpec((ns, tm, n), lambda i: (0, i, 0))
    return _run(
        body, [dh, c, wdown], hook, grid=(ni,), name=name, semantics=("arbitrary",),
        in_specs=[pl.BlockSpec((tm, D), lambda i: (i, 0)), slab, _resident(wdown.shape)],
        out_specs=[slab, ANY, ANY], out_shape=[S((ns, t, n), BF16), S(wdown.shape, F32), S(wdown.shape, BF16)],
        scratch_shapes=[pltpu.VMEM(wdown.shape, F32), pltpu.VMEM((n, D), BF16)])


def _ffn_bwd2(dc, a, wup, cw, h, gain, dh_in, name, hook=None):
    ns, t, n = dc.shape
    tm = min(FFN_ROWS, t)
    ni = t // tm

    def body(dc_ref, a_ref, wu_ref, cw_ref, h_ref, g_ref, di_ref, da_ref, o_ref, dg_ref, dcw_ref, dcb_ref, carry, keep):
        i = pl.program_id(0)

        @pl.when(i == 0)
        def _():
            carry[...] = jnp.zeros_like(carry)
            dg_ref[...] = jnp.zeros_like(dg_ref)
            dcw_ref[...] = jnp.zeros_like(dcw_ref)
            dcb_ref[...] = jnp.zeros_like(dcb_ref)

        rsum = lambda v: jnp.sum(v, axis=0, keepdims=True)
        acc = jnp.zeros((tm, D), F32)
        for s in range(ns):
            x = dc_ref[s].astype(F32)
            ext = jnp.concatenate([x, carry[s]], axis=0)
            keep[0] = ext[1:1 + tm]
            keep[1] = ext[2:2 + tm]
            x1, x2 = keep[0], keep[1]
            cwv = cw_ref[s]
            da = (cwv[2:3] * x + cwv[1:2] * x1 + cwv[0:1] * x2).astype(BF16)
            carry[s] = x[:HALO]
            da_ref[s] = da
            acc = acc + _dot_nt(da, wu_ref[s])
            av = a_ref[s].astype(F32)
            dcw_ref[s] += jnp.concatenate([rsum(x2 * av), rsum(x1 * av), rsum(x * av)], axis=0)
            dcb_ref[s] += rsum(x)
        hv = h_ref[...]
        r = _rstd(hv)
        gg = acc * g_ref[...]
        dh_new = di_ref[...].astype(F32) + r * gg - hv * (r * r * r * jnp.mean(gg * hv, axis=-1, keepdims=True))
        o_ref[...] = dh_new.astype(o_ref.dtype)
        dg_ref[...] += jnp.sum(acc * hv * r, axis=0, keepdims=True)

    slab = pl.BlockSpec((ns, tm, n), lambda i: (0, ni - 1 - i, 0))
    row = pl.BlockSpec((tm, D), lambda i: (ni - 1 - i, 0))
    vec = pl.BlockSpec((1, D), lambda i: (0, 0))
    whole = lambda shape: pl.BlockSpec(shape, lambda i: (0,) * len(shape))
    return _run(
        body, [dc, a, wup, cw, h, gain, dh_in], hook, grid=(ni,), name=name, semantics=("arbitrary",),
        in_specs=[slab, slab, _resident(wup.shape), _resident(cw.shape), row, vec, row],
        out_specs=[slab, row, vec, whole((ns, 3, n)), whole((ns, 1, n))],
        out_shape=[S((ns, t, n), BF16), S((t, D), DH), S((1, D), F32), S((ns, 3, n), F32), S((ns, 1, n), F32)],
        scratch_shapes=[pltpu.VMEM((ns, HALO, n), F32), pltpu.VMEM((2, tm, n), F32)])


def _dw_slot(hn, dy_s, name, hook=None):
    t, k = hn.shape
    ns, _, n = dy_s.shape
    tm = _tm(t)

    def body(a_ref, b_ref, o_ref, ob_ref, at_ref):
        @pl.when(pl.program_id(0) == 0)
        def _():
            for i in range(t // tm):
                at_ref[:, i * tm:(i + 1) * tm] = a_ref[i * tm:(i + 1) * tm, :].T

        acc = _dot(at_ref[...], b_ref[...])
        o_ref[...] = acc
        ob_ref[...] = acc.astype(BF16)

    ospec = pl.BlockSpec((None, k, n), lambda j: (j, 0, 0))
    return _run(
        body, [hn, dy_s], hook, grid=(ns,), name=name, semantics=("arbitrary",),
        in_specs=[_resident(hn.shape), pl.BlockSpec((None, t, n), lambda j: (j, 0, 0))],
        out_specs=[ospec, ospec], out_shape=[S((ns, k, n), F32), S((ns, k, n), BF16)],
        scratch_shapes=[pltpu.VMEM((k, t), BF16)])


def _dw_rows(a_s, dh, name, hook=None, fm=False):
    nk, t, kc = (1, a_s.shape[1], a_s.shape[0]) if fm else a_s.shape
    tm = _tm(t)
    ni = t // tm

    def body(a_ref, d_ref, o_ref, ob_ref):
        i = pl.program_id(0)
        dhb = d_ref[...].astype(BF16)

        @pl.when(i == 0)
        def _():
            o_ref[...] = jnp.zeros_like(o_ref)

        if fm:
            o_ref[...] += _dot(a_ref[...], dhb)
        for j in range(0 if fm else nk):
            o_ref[j * kc:(j + 1) * kc, :] += _dot_tn(a_ref[j], dhb)

        @pl.when(i == ni - 1)
        def _():
            ob_ref[...] = o_ref[...].astype(BF16)

    ospec = pl.BlockSpec((nk * kc, D), lambda i: (0, 0))
    return _run(
        body, [a_s, dh], hook, grid=(ni,), name=name, semantics=("arbitrary",),
        in_specs=[pl.BlockSpec((kc, tm), lambda i: (0, i)) if fm else pl.BlockSpec((nk, tm, kc), lambda i: (0, i, 0)),
                  pl.BlockSpec((tm, D), lambda i: (i, 0))],
        out_specs=[ospec, ospec], out_shape=[S((nk * kc, D), F32), S((nk * kc, D), BF16)])


def _dx_slot_normbwd(dy_s, wg, h, gain, dh_in, name, hook=None, fm=False, out_dtype=F32):
    ns, t, n = (1, dy_s.shape[1], dy_s.shape[0]) if fm else dy_s.shape
    tm = _tm(t)

    def body(dy_ref, w_ref, h_ref, g_ref, di_ref, o_ref, dg_ref):
        i = pl.program_id(0)

        @pl.when(i == 0)
        def _():
            dg_ref[...] = jnp.zeros_like(dg_ref)

        g = _dot_tn(dy_ref[...], w_ref[...]) if fm else _dot_nt(dy_ref[0], w_ref[0])
        for s in range(1, ns):
            g = g + _dot_nt(dy_ref[s], w_ref[s])
        hv = h_ref[...]
        r = _rstd(hv)
        gg = g * g_ref[...]
        dh_new = di_ref[...].astype(F32) + r * gg - hv * (r * r * r * jnp.mean(gg * hv, axis=-1, keepdims=True))
        o_ref[...] = dh_new.astype(o_ref.dtype)
        dg_ref[...] += jnp.sum(g * hv * r, axis=0, keepdims=True)

    row = pl.BlockSpec((tm, D), lambda i: (i, 0))
    vec = pl.BlockSpec((1, D), lambda i: (0, 0))
    return _run(
        body, [dy_s, wg, h, gain, dh_in], hook, grid=(t // tm,), name=name, semantics=("arbitrary",),
        in_specs=[pl.BlockSpec((n, tm), lambda i: (0, i)) if fm else pl.BlockSpec((ns, tm, n), lambda i: (0, i, 0)),
                  _resident(wg.shape), row, vec, row],
        out_specs=[row, vec], out_shape=[S((t, D), out_dtype), S((1, D), F32)])


def _sgu_gate_bwd(a_s, dg_s, vgain, ws, bst, name, hook=None):
    t = a_s.shape[1]
    sw = a_s.shape[2]
    gps = sw // CHUNK

    def body(a_ref, dg_ref, vg_ref, ws_ref, b_ref, da_ref, dws_ref, dbt_ref, dvg_ref, dvn_ref):
        n = pl.program_id(0)

        @pl.when(n == 0)
        def _():
            dws_ref[...] = jnp.zeros_like(dws_ref)
            dbt_ref[...] = jnp.zeros_like(dbt_ref)
            dvg_ref[...] = jnp.zeros_like(dvg_ref)

        vpre = jnp.concatenate([a_ref[4 + s].astype(F32) for s in range(4)], axis=1)
        v, v_grad = _gelu_and_grad(vpre)
        r = _rstd(v)
        vhat = v * r
        vn = (vhat * vg_ref[...]).astype(BF16)
        tri = _tril_mask()
        lane = lax.broadcasted_iota(jnp.int32, (CHUNK, CHUNK), 1)
        dbt = jnp.zeros((CHUNK, CHUNK), F32)
        for g in range(SGU_G):
            w = jnp.where(tri, ws_ref[g], 0.0).astype(BF16)
            vng = vn[:, g * CHUNK:(g + 1) * CHUNK]
            sg = _dot(w, vng) + b_ref[:, g:g + 1]
            lo = (g % gps) * CHUNK
            u, u_grad = _gelu_and_grad(a_ref[g // gps, :, lo:lo + CHUNK].astype(F32))
            dgate = dg_ref[g // gps, :, lo:lo + CHUNK].astype(F32)
            da_ref[g // gps, :, lo:lo + CHUNK] = (dgate * sg * u_grad).astype(BF16)
            ds = dgate * u
            dsb = ds.astype(BF16)
            dvn_ref[:, g * CHUNK:(g + 1) * CHUNK] = _dot_tn(w, dsb)
            dws_ref[g] += jnp.where(tri, _dot_nt(dsb, vng), 0.0)
            dbt = dbt + jnp.where(lane == g, jnp.sum(ds, axis=-1, keepdims=True), 0.0)
        dbt_ref[...] += dbt
        dvn = dvn_ref[...]
        dvg_ref[...] += jnp.sum(dvn * vhat, axis=0, keepdims=True)
        gg = dvn * vg_ref[...]
        dv = r * gg - v * (r * r * r * jnp.mean(gg * v, axis=-1, keepdims=True))
        dav = (dv * v_grad).astype(BF16)
        for s in range(4):
            da_ref[4 + s] = dav[:, s * sw:(s + 1) * sw]

    return _run(
        body, [a_s, dg_s, vgain, ws, bst], hook, grid=(t // CHUNK,), name=name, semantics=("arbitrary",),
        in_specs=[pl.BlockSpec((8, CHUNK, sw), lambda n: (0, n, 0)), pl.BlockSpec((4, CHUNK, sw), lambda n: (0, n, 0)),
                  pl.BlockSpec((1, SGU_W), lambda n: (0, 0)), pl.BlockSpec((SGU_G, CHUNK, CHUNK), lambda n: (0, 0, 0)),
                  pl.BlockSpec((CHUNK, SGU_G), lambda n: (0, 0))],
        out_specs=[pl.BlockSpec((8, CHUNK, sw), lambda n: (0, n, 0)), pl.BlockSpec((SGU_G, CHUNK, CHUNK), lambda n: (0, 0, 0)),
                   pl.BlockSpec((CHUNK, CHUNK), lambda n: (0, 0)), pl.BlockSpec((1, SGU_W), lambda n: (0, 0))],
        out_shape=[S((8, t, sw), BF16), S((SGU_G, CHUNK, CHUNK), F32), S((CHUNK, CHUNK), F32), S((1, SGU_W), F32)],
        scratch_shapes=[pltpu.VMEM((CHUNK, SGU_W), F32)])


def _attn_bwd(qkv_t, do_t, qg, kg, sinks, bias, name, hook=None):
    t = qkv_t.shape[1]
    nb = t // CHUNK

    def body(cur_ref, prev_ref, do_ref, qg_ref, kg_ref, sink_ref, bias_ref,
             o_ref, dqg_out, dkg_out, dsk_out, dbias_ref, carry, dqg_ref, dkg_ref, dsk_ref):
        n = pl.program_id(0)

        @pl.when(n == 0)
        def _():
            carry[...] = jnp.zeros_like(carry)
            dqg_ref[...] = jnp.zeros_like(dqg_ref)
            dkg_ref[...] = jnp.zeros_like(dkg_ref)
            dsk_ref[...] = jnp.zeros_like(dsk_ref)
            dbias_ref[...] = jnp.zeros_like(dbias_ref)

        @pl.when(n < nb)
        def _():
            valid = _attn_valid(n)
            o_ref[0:KV0, :] = carry[0:KV0, :].astype(BF16)
            kvs, heads = range(NKV), range(NH)
            group = lambda h: range(KVG * h, KVG * (h + 1))
            ks = [_attn_band(cur_ref, prev_ref, KV0 + HD * h) for h in kvs]
            rks = [_rstd_rows(k) for k in ks]
            khats = [k * rk for k, rk in zip(ks, rks)]
            kns = [(khat * kg_ref[...]).astype(BF16) for khat in khats]
            kn_toks = [kn.T for kn in kns]
            vbs = [_attn_band(cur_ref, prev_ref, KV0 + HD * (NKV + h)).astype(BF16) for h in kvs]
            v_toks = [vb.T for vb in vbs]
            qs = [cur_ref[HD * hq:HD * (hq + 1), :] for hq in heads]
            rqs = [_rstd_rows(q) for q in qs]
            qhats = [q * rq for q, rq in zip(qs, rqs)]
            qns = [(qhat * qg_ref[...]).astype(BF16) for qhat in qhats]
            probs = [_attn_probs(kn_toks[hq // KVG], qns[hq], bias_ref[hq], valid, sink_ref[hq]) for hq in heads]
            dohs = [do_ref[HD * hq:HD * (hq + 1), :] for hq in heads]
            dps = [_dot(v_toks[hq // KVG], dohs[hq]) for hq in heads]
            dsums = [jnp.sum(p * dp, axis=0, keepdims=True) for (p, _), dp in zip(probs, dps)]
            dss = [p * (dp - dsum) for (p, _), dp, dsum in zip(probs, dps, dsums)]
            for hq in heads:
                dsk_ref[hq:hq + 1, :] -= probs[hq][1] * dsums[hq]
                dbias_ref[hq] += dss[hq]
            dvs = [sum(_dot_nt(dohs[hq], probs[hq][0].astype(BF16)) for hq in group(h)) for h in kvs]
            dscs = [(ds * (HD ** -0.5)).astype(BF16) for ds in dss]
            dqns = [_dot(kns[hq // KVG], dscs[hq]) for hq in heads]
            dkns = [sum(_dot_nt(qns[hq], dscs[hq]) for hq in group(h)) for h in kvs]
            dqg_ref[...] += sum(dqn * qhat for dqn, qhat in zip(dqns, qhats))
            for hq in heads:
                gq = dqns[hq] * qg_ref[...]
                carry[HD * hq:HD * (hq + 1), :] = rqs[hq] * gq - qs[hq] * (
                    rqs[hq] * rqs[hq] * rqs[hq] * jnp.mean(gq * qs[hq], axis=0, keepdims=True))
            dkg_ref[...] += sum(dkn * khat for dkn, khat in zip(dkns, khats))
            for h in kvs:
                krow, vrow = KV0 + HD * h, KV0 + HD * (NKV + h)
                gk = dkns[h] * kg_ref[...]
                dk = rks[h] * gk - ks[h] * (rks[h] * rks[h] * rks[h] * jnp.mean(gk * ks[h], axis=0, keepdims=True))
                o_ref[krow:krow + HD, :] = (carry[krow:krow + HD, :] + dk[:, :CHUNK]).astype(BF16)
                o_ref[vrow:vrow + HD, :] = (carry[vrow:vrow + HD, :] + dvs[h][:, :CHUNK]).astype(BF16)
                carry[krow:krow + HD, :] = dk[:, CHUNK:]
                carry[vrow:vrow + HD, :] = dvs[h][:, CHUNK:]

        @pl.when(n == nb)
        def _():
            o_ref[...] = carry[...].astype(BF16)
            dqg_out[...] = jnp.sum(dqg_ref[...], axis=1, keepdims=True)
            dkg_out[...] = jnp.sum(dkg_ref[...], axis=1, keepdims=True)
            dsk_out[...] = jnp.sum(dsk_ref[...], axis=1, keepdims=True)

    cur = lambda n: (0, jnp.minimum(n, nb - 1))
    col = pl.BlockSpec((HD, 1), lambda n: (0, 0))
    whole = lambda shape: pl.BlockSpec(shape, lambda n: (0,) * len(shape))
    return _run(
        body, [qkv_t, qkv_t, do_t, qg, kg, sinks, bias], hook, grid=(nb + 1,), name=name, semantics=("arbitrary",),
        in_specs=[pl.BlockSpec((QKV, CHUNK), cur),
                  pl.BlockSpec((QKV - KV0, CHUNK), lambda n: (KV0 // (QKV - KV0), jnp.clip(n - 1, 0, nb - 1))),
                  pl.BlockSpec((D, CHUNK), cur), col, col, pl.BlockSpec(memory_space=pltpu.SMEM), whole((NH, 2 * CHUNK, CHUNK))],
        out_specs=[pl.BlockSpec((QKV, CHUNK), lambda n: (0, jnp.maximum(n - 1, 0))), whole((HD, 1)), whole((HD, 1)),
                   whole((NH, 1)), whole((NH, 2 * CHUNK, CHUNK))],
        out_shape=[S((QKV, t), BF16), S((HD, 1), F32), S((HD, 1), F32), S((NH, 1), F32), S((NH, 2 * CHUNK, CHUNK), F32)],
        scratch_shapes=[pltpu.VMEM((QKV, CHUNK), F32), pltpu.VMEM((HD, CHUNK), F32), pltpu.VMEM((HD, 2 * CHUNK), F32),
                        pltpu.VMEM((NH, CHUNK), F32)])


class _Plain:
    def __init__(self, wg):
        self.full, self.grads = wg, {}

    def w(self, n):
        return self.full[n]

    def hook(self, host):
        return None

    def grad(self, n, pair):
        self.grads[n] = pair

    def small(self, g_rep):
        pass

    def sync(self, point):
        pass


def _local_step(x, target, rep, sch):
    bucket_row = jnp.asarray(_rel_tables().T.reshape(1, -1))
    bias = _relbias_fwd(rep["rel_bias"].T, bucket_row, "relbias_fwd").reshape(NH, 2 * CHUNK, CHUNK)
    bst = rep["sgu_b_s"][0].T
    ws = rep["sgu_w_s"][0]
    vgain = rep["sgu_v_gain"]
    qg, kg, sinks = rep["attn_q_gain"].reshape(HD, 1), rep["attn_k_gain"].reshape(HD, 1), rep["attn_sinks"][0]
    w_down = lambda l: sch.w("ffn_w_down%d" % l).reshape(D_FF, D)
    w_up = lambda l: sch.w("ffn_w_up%d" % l)
    cw = [sch.w("ffn_conv_w")[:, 3 * l:3 * l + 3] for l in range(2)]
    cb = [rep["ffn_conv_b"][l].reshape(8, 1, -1) for l in range(2)]
    mixg = [rep["mix_norm"][l:l + 1] for l in range(2)]
    ffng = [rep["ffn_norm"][l:l + 1] for l in range(2)]
    rows = lambda pair: tuple(g.reshape(N_DEV, -1, D) for g in pair)
    hk = sch.hook

    hn0 = _rmsnorm(x, mixg[0], "norm0")
    a0 = _mm_slot(hn0, sch.w("sgu_w_in"), BF16, "sgu_in", hk("sgu_in"))
    gated = _sgu_gate_fwd(a0, vgain, ws, bst, "sgu_gate", hk("sgu_gate"))
    h1, hn1 = _resid_mm(gated, sch.w("sgu_w_out").reshape(SGU_W, D), x, ffng[0], "norm", "sgu_out", hk("sgu_out"))
    sch.sync("before_ffn0")
    a_ff0, c_ff0, h2, hn2 = _ffn_fwd(hn1, h1, w_up(0), w_down(0), cw[0], cb[0], mixg[1], "norm", "ffn0_fwd", hk("ffn0_fwd"))
    qkv = _mm_t(hn2, sch.w("attn_w_qkv"), "qkv", hk("qkv"))
    o = _attn_fwd(qkv, qg, kg, sinks, bias, "attn", hk("attn"))
    h3, hn3 = _resid_mm(o, sch.w("attn_w_o").reshape(D, D), h2, ffng[1], "norm", "attn_out", hk("attn_out"), fm=True)
    a_ff1, c_ff1, dy, sq = _ffn_fwd(hn3, h3, w_up(1), w_down(1), cw[1], cb[1], target, "loss", "ffn1_fwd_loss", hk("ffn1_fwd_loss"))
    loss = (0.5 / D) * jnp.sum(sq[:, 0, 0])

    def ffn_bwd(dh, h_in, hn, a, c, l, tag):
        dc, g_down, g_down_b = _ffn_bwd1(dh, c, w_down(l), tag + "_bwd1", hk(tag + "_bwd1"))
        sch.grad("ffn_w_down%d" % l, rows((g_down, g_down_b)))
        da, dh_new, dgain, g_cw, g_cb = _ffn_bwd2(dc, a, w_up(l), cw[l], h_in, ffng[l], dh, tag + "_bwd2", hk(tag + "_bwd2"))
        sch.grad("ffn_w_up%d" % l, _dw_slot(hn, da, tag + "_dw_up", hk(tag + "_dw_up")))
        return dh_new, dgain, g_cw, g_cb.reshape(-1)

    dh, d_ffng1, g_cw1, g_cb1 = ffn_bwd(dy, h3, hn3, a_ff1, c_ff1, 1, "ffn1")
    do = _dx_rows_t(dh, sch.w("attn_w_o").reshape(D, D), "attn_do", hk("attn_do"))
    sch.grad("attn_w_o", rows(_dw_rows(o, dh, "dw_o", hk("dw_o"), fm=True)))
    dqkv, d_qg, d_kg, d_sk, d_bias = _attn_bwd(qkv, do, qg, kg, sinks, bias, "attn_bwd", hk("attn_bwd"))
    sch.grad("attn_w_qkv", tuple(g.reshape(N_DEV, -1, D) for g in _dw_rows(dqkv, hn2, "dw_qkv", hk("dw_qkv"), fm=True)))
    dh, d_mixg1 = _dx_slot_normbwd(dqkv, sch.w("attn_w_qkv").reshape(QKV, D), h2, mixg[1], dh, "dx_qkv", hk("dx_qkv"), fm=True,
                                   out_dtype=DH)
    d_relb = _relbias_bwd(d_bias.reshape(NH, -1), bucket_row, "relbias_bwd").T
    g_rep = {"attn_q_gain": d_qg.reshape(1, HD), "attn_k_gain": d_kg.reshape(1, HD), "attn_sinks": d_sk.reshape(1, NH),
             "rel_bias": d_relb}
    sch.small(g_rep)
    dh, d_ffng0, g_cw0, g_cb0 = ffn_bwd(dh, h1, hn1, a_ff0, c_ff0, 0, "ffn0")
    g_cw = jnp.concatenate([g_cw0, g_cw1], axis=1)
    sch.grad("ffn_conv_w", (g_cw, g_cw.astype(BF16)))
    g_ffn = {"ffn_norm": jnp.concatenate([d_ffng0, d_ffng1], axis=0), "ffn_conv_b": jnp.stack([g_cb0, g_cb1], axis=0)}
    sch.small(g_ffn)
    dgated = _dx_rows(dh, sch.w("sgu_w_out").reshape(SGU_W, D), SGU_W // 4, BF16, "sgu_dgated", hk("sgu_dgated"))
    sch.grad("sgu_w_out", rows(_dw_rows(gated, dh, "dw_sgu_out", hk("dw_sgu_out"))))
    da0, d_ws, d_bst, d_vgain = _sgu_gate_bwd(a0, dgated, vgain, ws, bst, "sgu_gate_bwd", hk("sgu_gate_bwd"))
    g_sgu = {"sgu_v_gain": d_vgain, "sgu_w_s": d_ws[None], "sgu_b_s": d_bst[:, :SGU_G].T[None]}
    sch.small(g_sgu)
    sch.grad("sgu_w_in", _dw_slot(hn0, da0, "dw_sgu_in", hk("dw_sgu_in")))
    sch.sync("after_dw")
    grad_x, d_mixg0 = _dx_slot_normbwd(da0, sch.w("sgu_w_in"), x, mixg[0], dh, "dx_sgu_in", hk("dx_sgu_in"))
    g_mix = {"mix_norm": jnp.concatenate([d_mixg0, d_mixg1], axis=0)}
    sch.small(g_mix)
    for g in (g_ffn, g_sgu, g_mix):
        g_rep.update(g)
    return loss, grad_x, g_rep


def _allgather(xs, name):
    nt = len(xs)

    def body(*refs):
        x_refs, o_refs = refs[:nt], refs[nt:2 * nt]
        send_sems, recv_sems, local_sems = refs[2 * nt:]
        x, y, c, chips = _place()
        me, sibling = (x, y, c), (x, y, 1 - c)

        def copy(t, k, block, to, src=None):
            px, py, pc = block
            dst = o_refs[t].at[4 * px + 2 * py + pc]
            return pltpu.make_async_remote_copy(
                src_ref=dst if src is None else src, dst_ref=dst, send_sem=send_sems.at[t, k], recv_sem=recv_sems.at[t, k],
                device_id=to, device_id_type=MESH)

        mine = [pltpu.make_async_copy(x_refs[t], o_refs[t].at[4 * x + 2 * y + c], local_sems.at[t]) for t in range(nt)]
        for cp in mine:
            cp.start()
        first = []
        for t in range(nt):
            first.append(copy(t, 0, me, sibling, src=x_refs[t]))
            first += [copy(t, 1 + j, me, (*chip, c), src=x_refs[t]) for j, chip in enumerate(chips)]
        for cp in first:
            cp.start()
        passed = []
        for j, chip in enumerate(chips):
            for t in range(nt):
                copy(t, 1 + j, (*chip, c), me).wait_recv()
                fwd = copy(t, 4 + j, (*chip, c), sibling)
                fwd.start()
                passed.append(fwd)
        for t in range(nt):
            copy(t, 0, sibling, me).wait_recv()
            for j, chip in enumerate(chips):
                copy(t, 4 + j, (*chip, 1 - c), me).wait_recv()
        for cp in first + passed:
            cp.wait_send()
        for cp in mine:
            cp.wait()

    return pl.pallas_call(
        body, name=name, in_specs=[ANY] * nt, out_specs=[ANY] * nt,
        out_shape=[S((N_DEV,) + a.shape, a.dtype) for a in xs],
        scratch_shapes=[pltpu.SemaphoreType.DMA((nt, 7)), pltpu.SemaphoreType.DMA((nt, 7)), pltpu.SemaphoreType.DMA((nt,))],
        compiler_params=pltpu.CompilerParams(has_side_effects=True))(*xs)


def _exchange(hook, name):
    comm = hook()
    ci, co = len(comm.inputs), len(comm.out_shapes)

    def body(*refs):
        cins, couts = refs[:ci], refs[ci:ci + co]
        send, recv = refs[-2:]
        comm.start(cins, couts, send, recv)
        comm.finish(cins, couts, send, recv)

    res = pl.pallas_call(
        body, name=name, in_specs=[ANY] * ci, out_specs=[ANY] * co, out_shape=comm.out_shapes,
        scratch_shapes=[pltpu.SemaphoreType.DMA((comm.n_sems,)), pltpu.SemaphoreType.DMA((comm.n_sems,))],
        input_output_aliases=dict(comm.aliases),
        compiler_params=pltpu.CompilerParams(has_side_effects=True))(*comm.inputs)
    hook(res)


def _row_tile(r):
    tr = r if r <= ROW_TILE or r % ROW_TILE else ROW_TILE
    assert r % tr == 0
    return tr


def _rs_partial(g32, sib, place, name):
    _, r, cdim = g32.shape
    tr = _row_tile(r)

    def body(place_ref, g_ref, s_ref, p_ref, own_ref):
        k = pl.program_id(1)
        tot = g_ref[...] + s_ref[...].astype(F32)
        p_ref[...] = tot.astype(BF16)

        @pl.when(k == place_ref[1])
        def _():
            own_ref[...] = tot

    grid_spec = pltpu.PrefetchScalarGridSpec(
        num_scalar_prefetch=1, grid=(r // tr, 4),
        in_specs=[pl.BlockSpec((None, None, tr, cdim), lambda i, k, pr: (k, pr[0], i, 0)),
                  pl.BlockSpec((None, tr, cdim), lambda i, k, pr: (k, i, 0))],
        out_specs=[pl.BlockSpec((None, tr, cdim), lambda i, k, pr: (k, i, 0)), pl.BlockSpec((tr, cdim), lambda i, k, pr: (i, 0))])
    return pl.pallas_call(
        body, grid_spec=grid_spec, name=name,
        out_shape=[S((4, r, cdim), BF16), S((r, cdim), F32)],
        compiler_params=_cp("parallel", "arbitrary"))(place, g32.reshape(4, 2, r, cdim), sib)


def _adamw_math(w, g, m, v):
    m = ADAM_B1 * m + (1.0 - ADAM_B1) * g
    v = ADAM_B2 * v + (1.0 - ADAM_B2) * (g * g)
    m_hat = m / (1.0 - ADAM_B1 ** ADAM_STEP)
    v_hat = v / (1.0 - ADAM_B2 ** ADAM_STEP)
    delta = -ADAM_LR * (m_hat / (jnp.sqrt(v_hat) + ADAM_EPS) + ADAM_WD * w)
    return delta, m, v


def _adamw_shard(owns, recvs, w, m, v, name, flipped=False):
    nl = w.shape[0]
    r, cdim = owns[0].shape
    tr = _row_tile(r)
    nr = r // tr

    def body(*refs):
        own_refs, recv_refs = refs[:nl], refs[nl:2 * nl]
        w_ref, m_ref, v_ref, g_out, d_out, m_out, v_out = refs[2 * nl:]
        layer = pl.program_id(0)
        g = None
        for l in range(nl):
            gl = own_refs[l][...] + recv_refs[l][0].astype(F32) + recv_refs[l][1].astype(F32) + recv_refs[l][2].astype(F32)
            g = gl if g is None else jnp.where(layer == l, gl, g)
        if flipped:
            g = g.T
        g_out[...] = g
        d_out[...], m_out[...], v_out[...] = _adamw_math(w_ref[...], g, m_ref[...], v_ref[...])

    park = lambda l: (lambda layer, i: (jnp.where(layer == l, i, jnp.where(layer < l, 0, nr - 1)), 0))
    park3 = lambda l: (lambda layer, i: (0, jnp.where(layer == l, i, jnp.where(layer < l, 0, nr - 1)), 0))
    if flipped:
        row = pl.BlockSpec((None, cdim, tr), lambda layer, i: (layer, 0, i))
    else:
        row = pl.BlockSpec((None, tr, cdim), lambda layer, i: (layer, i, 0))
    return pl.pallas_call(
        body, grid=(nl, nr), name=name,
        in_specs=[pl.BlockSpec((tr, cdim), park(l)) for l in range(nl)] + [pl.BlockSpec((3, tr, cdim), park3(l)) for l in range(nl)]
        + [row, row, row],
        out_specs=[row] * 4, out_shape=[S(w.shape, F32)] * 4,
        compiler_params=_cp("arbitrary", "arbitrary"))(*owns, *recvs, w, m, v)


def _adamw_small(galls, ws, ms, vs, name):
    n = len(galls)

    def body(*refs):
        g_refs, w_refs, m_refs, v_refs, outs = refs[:n], refs[n:2 * n], refs[2 * n:3 * n], refs[3 * n:4 * n], refs[4 * n:]
        for i in range(n):
            g = g_refs[i][0].astype(F32)
            for s in range(1, N_DEV):
                g = g + g_refs[i][s].astype(F32)
            outs[i][...] = g
            outs[n + i][...], outs[2 * n + i][...], outs[3 * n + i][...] = _adamw_math(w_refs[i][...], g, m_refs[i][...], v_refs[i][...])

    res = pl.pallas_call(body, out_shape=[S(a.shape, F32) for a in ws] * 4, name=name)(*galls, *ws, *ms, *vs)
    return [res[k * n:(k + 1) * n] for k in range(4)]


REPLICATED = ["mix_norm", "ffn_norm", "sgu_v_gain", "sgu_w_s", "sgu_b_s", "attn_q_gain", "attn_k_gain", "attn_sinks", "rel_bias",
              "ffn_conv_b"]
WEIGHTS = ["mix_norm", "ffn_norm", "sgu_w_in", "sgu_v_gain", "sgu_w_s", "sgu_b_s", "sgu_w_out", "attn_w_qkv", "attn_q_gain",
           "attn_k_gain", "attn_sinks", "attn_w_o", "rel_bias", "ffn_w_up", "ffn_conv_w", "ffn_conv_b", "ffn_w_down"]
SMALL = ["g_" + n for n in REPLICATED]
BF16_TRANSIT = {"sgu_w_s"}
SMALL_ATTN = ["g_attn_q_gain", "g_attn_k_gain", "g_attn_sinks", "g_rel_bias"]
SMALL_FFN = ["g_ffn_norm", "g_ffn_conv_b"]
SMALL_SGU = ["g_sgu_v_gain", "g_sgu_w_s", "g_sgu_b_s"]

GATHER_FIRST = ["sgu_w_in", "ffn_conv_w"]
UP0_SPLIT, UP1_SPLIT = 352, 400
PLAN = {
    "sgu_in": [("ag1", "sgu_w_out"), ("ag1", "ffn_w_up0", (0, UP0_SPLIT))],
    "sgu_gate": [("ag2", "sgu_w_out"), ("ag1", "ffn_w_up0", (UP0_SPLIT, D))],
    "sgu_out": [("ag2", "ffn_w_up0"), ("ag1", "ffn_w_down0")],
    "before_ffn0": [("ag2", "ffn_w_down0")],
    "ffn0_fwd": [("agd", "attn_w_qkv"), ("ag1", "attn_w_o"), ("ag1", "ffn_w_down1"), ("ag1", "ffn_w_up1", (0, UP1_SPLIT))],
    "attn": [("ag2", "attn_w_o"), ("ag2", "ffn_w_down1"), ("ag1", "ffn_w_up1", (UP1_SPLIT, D))],
    "attn_out": [("ag2", "ffn_w_up1")],
    "attn_bwd": [("rs1", "ffn_w_down1"), ("rs1", "ffn_w_up1"), ("rs1", "attn_w_o")],
    "ffn0_bwd1": [("rs2", "ffn_w_down1"), ("rs2", "attn_w_o"), ("rs1", "attn_w_qkv")] + [("ag1", n) for n in SMALL_ATTN],
    "ffn0_bwd2": [("rs2", "ffn_w_up1"), ("rs2", "attn_w_qkv"), ("rs1", "ffn_w_down0")] + [("ag2", n) for n in SMALL_ATTN],
    "ffn0_dw_up": [("rs2", "ffn_w_down0")],
    "sgu_dgated": [("rs1", "ffn_w_up0")] + [("agd", n) for n in SMALL_FFN],
    "sgu_gate_bwd": [("rs2", "ffn_w_up0"), ("rs1", "sgu_w_out")],
    "dw_sgu_in": [("rs2", "sgu_w_out")] + [("ag1", n) for n in SMALL_SGU],
    "after_dw": [("rs1", "sgu_w_in"), ("rs1", "ffn_conv_w")] + [("ag2", n) for n in SMALL_SGU],
    "dx_sgu_in": [("rs2", "sgu_w_in"), ("rs2", "ffn_conv_w")],
    "last": [("agd", "g_mix_norm")],
}


class _Overlap:
    def __init__(self, shard, place):
        self.shard, self.place = shard, place
        self.part, self.full = {}, {}
        self.grads, self.sib, self.own, self.recv = {}, {}, {}, {}

    def w(self, n):
        return self.full[n]

    def grad(self, n, pair):
        self.grads[n] = pair

    def small(self, g_rep):
        self.shard.update(("g_" + n, a.astype(BF16) if n in BF16_TRANSIT else a) for n, a in _views2d(g_rep).items())

    def sync(self, point):
        _exchange(self.hook(point), point)

    def chip_sums(self, n):
        sums, self.own[n] = _rs_partial(self.grads[n][0], self.sib.pop(n), self.place, "rs_partial_" + n)
        return sums

    def hook(self, host):
        ops = PLAN.get(host)
        if not ops:
            return None
        where = {"ag1": self.part, "ag2": self.full, "agd": self.full, "rs1": self.sib, "rs2": self.recv}
        idx = []

        def hook(results=None):
            if results is not None:
                for (kind, n, *_), i in zip(ops, idx):
                    where[kind][n] = results[i]
                return None
            comm = _Comm()
            for kind, n, *rows in ops:
                arr = {"ag1": lambda: self.shard[n], "agd": lambda: self.shard[n], "ag2": lambda: self.part.pop(n),
                       "rs1": lambda: self.grads[n][1], "rs2": lambda: self.chip_sums(n)}[kind]()
                idx.append(comm.add(kind, arr, *rows, into=self.part.pop(n) if rows and rows[0][0] else None))
            return comm

        return hook


TRANSPOSED = {"attn_w_qkv"}
PHYSICAL_T = {"ffn_w_up"}
SHARDED = {
    "sgu_w_in": ["sgu_w_in"], "sgu_w_out": ["sgu_w_out"], "attn_w_qkv": ["attn_w_qkv"], "attn_w_o": ["attn_w_o"],
    "ffn_w_up": ["ffn_w_up0", "ffn_w_up1"], "ffn_w_down": ["ffn_w_down0", "ffn_w_down1"], "ffn_conv_w": ["ffn_conv_w"],
}


def _send_views(w):
    out = {"ffn_conv_w": w["ffn_conv_w"].reshape(6, -1)}
    for name, parts in SHARDED.items():
        if name != "ffn_conv_w":
            out.update((p, (w[name][l].T if name in TRANSPOSED else w[name][l]).astype(BF16)) for l, p in enumerate(parts))
    return out


def _views2d(d):
    return {n: d[n].reshape(-1, d[n].shape[-1]) for n in REPLICATED if n in d}


def kernel(x, mix_norm, ffn_norm, sgu_w_in, sgu_v_gain, sgu_w_s, sgu_b_s, sgu_w_out, attn_w_qkv, attn_q_gain, attn_k_gain, attn_sinks, attn_w_o, rel_bias, ffn_w_up, ffn_conv_w, ffn_conv_b, ffn_w_down, loss_target, m_mix_norm, m_ffn_norm, m_sgu_w_in, m_sgu_v_gain, m_sgu_w_s, m_sgu_b_s, m_sgu_w_out, m_attn_w_qkv, m_attn_q_gain, m_attn_k_gain, m_attn_sinks, m_attn_w_o, m_rel_bias, m_ffn_w_up, m_ffn_conv_w, m_ffn_conv_b, m_ffn_w_down, v_mix_norm, v_ffn_norm, v_sgu_w_in, v_sgu_v_gain, v_sgu_w_s, v_sgu_b_s, v_sgu_w_out, v_attn_w_qkv, v_attn_q_gain, v_attn_k_gain, v_attn_sinks, v_attn_w_o, v_rel_bias, v_ffn_w_up, v_ffn_conv_w, v_ffn_conv_b, v_ffn_w_down):
    w = dict(zip(WEIGHTS, (mix_norm, ffn_norm, sgu_w_in, sgu_v_gain, sgu_w_s, sgu_b_s, sgu_w_out, attn_w_qkv, attn_q_gain, attn_k_gain,
                           attn_sinks, attn_w_o, rel_bias, ffn_w_up, ffn_conv_w, ffn_conv_b, ffn_w_down)))
    m = dict(zip(WEIGHTS, (m_mix_norm, m_ffn_norm, m_sgu_w_in, m_sgu_v_gain, m_sgu_w_s, m_sgu_b_s, m_sgu_w_out, m_attn_w_qkv, m_attn_q_gain,
                           m_attn_k_gain, m_attn_sinks, m_attn_w_o, m_rel_bias, m_ffn_w_up, m_ffn_conv_w, m_ffn_conv_b, m_ffn_w_down)))
    v = dict(zip(WEIGHTS, (v_mix_norm, v_ffn_norm, v_sgu_w_in, v_sgu_v_gain, v_sgu_w_s, v_sgu_b_s, v_sgu_w_out, v_attn_w_qkv, v_attn_q_gain,
                           v_attn_k_gain, v_attn_sinks, v_attn_w_o, v_rel_bias, v_ffn_w_up, v_ffn_conv_w, v_ffn_conv_b, v_ffn_w_down)))
    rep = {n: w[n] for n in REPLICATED}

    xi, yi, ci = lax.axis_index("x"), lax.axis_index("y"), lax.axis_index("c")
    place = jnp.stack([ci, 2 * xi + yi]).astype(jnp.int32)
    sch = _Overlap(_send_views(w), place)
    sch.full.update(zip(GATHER_FIRST, _allgather([sch.shard[n] for n in GATHER_FIRST], "gather_first")))

    loss, grad_x, g_rep = _local_step(x[0], loss_target[0], rep, sch)
    loss = lax.psum(loss, ("x", "y", "c"))
    sch.sync("last")

    out = [{}, {}, {}, {}]
    for name, parts in SHARDED.items():
        flip = (lambda a: jnp.swapaxes(a, -1, -2)) if name in TRANSPOSED | PHYSICAL_T else (lambda a: a)
        shape = flip(w[name]).shape
        as3d = lambda a: flip(a).reshape(len(parts), -1, shape[-1])
        res = _adamw_shard([sch.own[p] for p in parts], [sch.recv[p] for p in parts], as3d(w[name]), as3d(m[name]), as3d(v[name]),
                           "adamw_" + name, flipped=name in PHYSICAL_T)
        for o, r in zip(out, res):
            o[name] = flip(r.reshape(shape))
    small = _adamw_small([sch.full[n] for n in SMALL], *[list(_views2d(d).values()) for d in (rep, m, v)], "adamw_small")
    for o, res in zip(out, small):
        o.update((n, r.reshape(w[n].shape)) for n, r in zip(REPLICATED, res))

    return (loss, grad_x[None], *[out[0][n] for n in WEIGHTS], *[out[1][n] for n in WEIGHTS],
            *[out[2][n] for n in WEIGHTS], *[out[3][n] for n in WEIGHTS])
```

```python
import functools
import math

import numpy as np
import jax
import jax.numpy as jnp
from jax import lax
from jax.experimental import pallas as pl
from jax.experimental.pallas import tpu as pltpu

F32 = jnp.float32
BF16 = jnp.bfloat16
DH = jnp.bfloat16
S = jax.ShapeDtypeStruct

D = 1024
CHUNK = 128
SGU_W = 2048
SGU_G = 16
HD = 64
NH = 16
NKV = 4
KVG = 4
D_FF = 2816
REL_BUCKETS = 32
REL_MAX_DIST = 128
EPS = 1e-6
N_DEV = 8
MESH = pl.DeviceIdType.MESH

ADAM_LR = 0.001
ADAM_B1 = 0.9
ADAM_B2 = 0.999
ADAM_EPS = 1e-08
ADAM_WD = 0.01
ADAM_STEP = 10

ROW_TILE = 512
HALO = 8
FFN_ROWS = 256


def _tm(t):
    return min(ROW_TILE, t)


def _cp(*sem):
    return pltpu.CompilerParams(dimension_semantics=sem)


ANY = pl.BlockSpec(memory_space=pl.ANY)


def _place():
    x, y, c = lax.axis_index("x"), lax.axis_index("y"), lax.axis_index("c")
    return x, y, c, [(1 - x, y), (x, 1 - y), (1 - x, 1 - y)]


class _Comm:
    SEMS = {"ag1": 5, "ag2": 3, "rs1": 4, "rs2": 3, "agd": 8}

    def __init__(self):
        self.inputs, self.out_shapes, self.aliases, self.ops, self.n_sems = [], [], {}, [], 0

    def add(self, kind, arr, rows=None, into=None):
        lead = {"ag1": N_DEV, "agd": N_DEV, "ag2": None, "rs1": 4, "rs2": 3}[kind]
        shape = arr.shape if lead is None else (lead,) + arr.shape[(0 if kind in ("ag1", "agd") else 1):]
        if kind == "ag2":
            self.aliases[len(self.inputs)] = len(self.out_shapes)
        self.ops.append((kind, len(self.inputs), len(self.out_shapes), self.n_sems, rows))
        self.inputs.append(arr)
        if into is not None:
            self.aliases[len(self.inputs)] = len(self.out_shapes)
            self.inputs.append(into)
        self.out_shapes.append(S(shape, arr.dtype))
        self.n_sems += self.SEMS[kind]
        return len(self.out_shapes) - 1

    def _copies(self, ins, outs, send, recv):
        x, y, c, chips = _place()
        me, sibling = (x, y, c), (x, y, 1 - c)
        slot = lambda px, py, pc: 4 * px + 2 * py + pc
        sends, recvs, local = [], [], []

        def rc(src, dst, k, to):
            return lambda: pltpu.make_async_remote_copy(src_ref=src(), dst_ref=dst(), send_sem=send.at[k], recv_sem=recv.at[k],
                                                        device_id=to, device_id_type=MESH)

        for kind, ii, oi, b, rows in self.ops:
            src, dst = ins[ii], outs[oi]
            at = lambda ref, i: (lambda: ref.at[i])
            if kind == "ag1":
                part = slice(None) if rows is None else pl.ds(rows[0], rows[1] - rows[0])
                to = lambda i, d=dst, p=part: (lambda: d.at[i, p])
                whole, mine = (lambda s=src, p=part: s.at[p]), to(slot(*me))
                sends.append(rc(whole, mine, b, sibling))
                recvs.append(rc(whole, to(slot(x, y, 1 - c)), b, me))
                for j, chip in enumerate(chips):
                    sends.append(rc(whole, mine, b + 1 + j, (*chip, c)))
                    recvs.append(rc(whole, to(slot(*chip, c)), b + 1 + j, me))
                local.append(lambda s=whole, m=mine, k=b + 4: pltpu.make_async_copy(s(), m(), send.at[k]))
            elif kind == "ag2":
                for j, chip in enumerate(chips):
                    sends.append(rc(at(dst, slot(*chip, c)), at(dst, slot(*chip, c)), b + j, sibling))
                    recvs.append(rc(at(dst, slot(*chip, 1 - c)), at(dst, slot(*chip, 1 - c)), b + j, me))
            elif kind == "agd":
                whole, mine = (lambda s=src: s), at(dst, slot(*me))
                flip = lambda v, bit: 1 - v if bit else v
                for k in range(1, N_DEV):
                    peer = (flip(x, k >> 2), flip(y, (k >> 1) & 1), flip(c, k & 1))
                    sends.append(rc(whole, mine, b + k - 1, peer))
                    recvs.append(rc(whole, at(dst, slot(*peer)), b + k - 1, me))
                local.append(lambda s=src, m=mine, k=b + 7: pltpu.make_async_copy(s, m(), send.at[k]))
            elif kind == "rs1":
                for k in range(4):
                    sends.append(rc(at(src, 2 * k + (1 - c)), at(dst, k), b + k, sibling))
                    recvs.append(rc(at(src, 2 * k + c), at(dst, k), b + k, me))
            else:
                for j, (px, py) in enumerate(chips):
                    sends.append(rc(at(src, 2 * px + py), at(dst, j), b + j, (px, py, c)))
                    recvs.append(rc(at(src, 2 * px + py), at(dst, j), b + j, me))
        return sends, recvs, local

    def start(self, ins, outs, send, recv):
        sends, _, local = self._copies(ins, outs, send, recv)
        for make in local + sends:
            make().start()

    def finish(self, ins, outs, send, recv):
        sends, recvs, local = self._copies(ins, outs, send, recv)
        for make in recvs:
            make().wait_recv()
        for make in sends:
            make().wait_send()
        for make in local:
            make().wait()


def _run(body, args, hook, *, grid, in_specs, out_specs, out_shape, name, semantics, scratch_shapes=(), aliases=None):
    comm = hook() if hook is not None else None
    aliases = dict(aliases or {})
    if comm is None:
        return pl.pallas_call(body, grid=grid, in_specs=in_specs, out_specs=out_specs, out_shape=out_shape, name=name,
                              scratch_shapes=list(scratch_shapes), input_output_aliases=aliases,
                              compiler_params=_cp(*semantics))(*args)
    single = not isinstance(out_shape, (list, tuple))
    out_shapes = [out_shape] if single else list(out_shape)
    out_specs_l = [out_specs] if single else list(out_specs)
    n_in, n_out, n_scr, ci, co = len(args), len(out_shapes), len(scratch_shapes), len(comm.inputs), len(comm.out_shapes)

    def wrapped(*refs):
        ins, cins = refs[:n_in], refs[n_in:n_in + ci]
        outs, couts = refs[n_in + ci:n_in + ci + n_out], refs[n_in + ci + n_out:n_in + ci + n_out + co]
        scr = refs[n_in + ci + n_out + co:n_in + ci + n_out + co + n_scr]
        send, recv = refs[-2:]
        first = functools.reduce(lambda a, b: a & b, [pl.program_id(a) == 0 for a in range(len(grid))])
        last = functools.reduce(lambda a, b: a & b, [pl.program_id(a) == g - 1 for a, g in enumerate(grid)])

        @pl.when(first)
        def _():
            comm.start(cins, couts, send, recv)

        body(*ins, *outs, *scr)

        @pl.when(last)
        def _():
            comm.finish(cins, couts, send, recv)

    res = pl.pallas_call(
        wrapped, grid=grid, in_specs=list(in_specs) + [ANY] * ci, out_specs=out_specs_l + [ANY] * co,
        out_shape=out_shapes + comm.out_shapes, name=name,
        scratch_shapes=list(scratch_shapes) + [pltpu.SemaphoreType.DMA((comm.n_sems,)), pltpu.SemaphoreType.DMA((comm.n_sems,))],
        input_output_aliases={**aliases, **{n_in + k: n_out + v for k, v in comm.aliases.items()}},
        compiler_params=pltpu.CompilerParams(dimension_semantics=("arbitrary",) * len(grid), has_side_effects=True))(*args, *comm.inputs)
    hook(res[n_out:])
    return res[0] if single else list(res[:n_out])


def _dot(a, b):
    return jnp.dot(a, b, preferred_element_type=F32)


def _dot_nt(a, b):
    return lax.dot_general(a, b, (((1,), (1,)), ((), ())), preferred_element_type=F32)


def _dot_tn(a, b):
    return lax.dot_general(a, b, (((0,), (0,)), ((), ())), preferred_element_type=F32)


def _gelu(x):
    return 0.5 * x * (1.0 + lax.erf(x * (2.0 ** -0.5)))


def _gelu_and_grad(x):
    cdf = 0.5 * (1.0 + lax.erf(x * (2.0 ** -0.5)))
    return x * cdf, cdf + x * jnp.exp(-0.5 * x * x) * (1.0 / math.sqrt(2.0 * math.pi))


def _sigmoid(x):
    return 1.0 / (1.0 + jnp.exp(-x))


def _rstd(x):
    return lax.rsqrt(jnp.mean(x * x, axis=-1, keepdims=True) + EPS)


def _rel_tables():
    q = np.arange(CHUNK)[:, None] + CHUNK
    k = np.arange(2 * CHUNK)[None, :]
    dist = q - k
    n = np.maximum(dist, 0)
    max_exact = REL_BUCKETS // 2
    large = max_exact + (np.log(np.maximum(n, 1).astype(np.float32) / max_exact)
                         / math.log(REL_MAX_DIST / max_exact) * (REL_BUCKETS - max_exact)).astype(np.int32)
    large = np.minimum(large, REL_BUCKETS - 1)
    return np.where(n < max_exact, n, large).astype(np.int32)


def _rmsnorm(x, gain, name):
    t = x.shape[0]
    tm = _tm(t)

    def body(x_ref, g_ref, o_ref):
        xv = x_ref[...]
        o_ref[...] = (xv * _rstd(xv) * g_ref[...]).astype(BF16)

    return pl.pallas_call(
        body, grid=(t // tm,), name=name,
        in_specs=[pl.BlockSpec((tm, D), lambda i: (i, 0)), pl.BlockSpec((1, D), lambda i: (0, 0))],
        out_specs=pl.BlockSpec((tm, D), lambda i: (i, 0)),
        out_shape=S((t, D), BF16), compiler_params=_cp("parallel"))(x, gain)


def _resident(shape):
    zeros = (0,) * len(shape)
    return pl.BlockSpec(shape, lambda *_: zeros, pipeline_mode=pl.Buffered(1))


def _mm_slot(hn, wg, out_dtype, name, hook=None):
    t, k = hn.shape
    ns, _, n = wg.shape
    tm = _tm(t)

    def body(a_ref, w_ref, o_ref):
        a = a_ref[...]
        for s in range(ns):
            o_ref[s] = _dot(a, w_ref[s]).astype(out_dtype)

    return _run(
        body, [hn, wg], hook, grid=(t // tm,), name=name, semantics=("parallel",),
        in_specs=[pl.BlockSpec((tm, k), lambda i: (i, 0)), _resident(wg.shape)],
        out_specs=pl.BlockSpec((ns, tm, n), lambda i: (0, i, 0)), out_shape=S((ns, t, n), out_dtype))


def _mm_t(hn, wt, name, hook=None):
    t, k = hn.shape
    ns, n, _ = wt.shape
    tm = _tm(t)

    def body(a_ref, w_ref, o_ref):
        a = a_ref[...]
        for s in range(ns):
            o_ref[s * n:(s + 1) * n, :] = _dot_nt(w_ref[s], a)

    return _run(
        body, [hn, wt], hook, grid=(t // tm,), name=name, semantics=("parallel",),
        in_specs=[pl.BlockSpec((tm, k), lambda i: (i, 0)), _resident(wt.shape)],
        out_specs=pl.BlockSpec((ns * n, tm), lambda i: (0, i)), out_shape=S((ns * n, t), F32))


def _conv3(a, prev, cw, cb, tm):
    ext = jnp.concatenate([prev, a], axis=0)
    return cw[2:3] * a + cw[1:2] * ext[HALO - 1:HALO - 1 + tm] + cw[0:1] * ext[HALO - 2:HALO - 2 + tm] + cb


def _ffn_fwd(hn, h, wup, wdown, cw, cb, extra, mode, name, hook=None):
    t, k = hn.shape
    n = wup.shape[-1]
    nh = wup.shape[0] // 2
    tm = min(FFN_ROWS, t)
    ni = t // tm

    def body(a_ref, h_ref, wu_ref, wd_ref, cw_ref, cb_ref, e_ref, as_ref, cs_ref, o1_ref, o2_ref, carry):
        i = pl.program_id(0)

        @pl.when(i == 0)
        def _():
            carry[...] = jnp.zeros_like(carry)

        a = a_ref[...]
        acc = h_ref[...]
        nxt = (_dot(a, wu_ref[0]), _dot(a, wu_ref[nh]))
        for j in range(nh):
            ag, av = nxt
            if j + 1 < nh:
                nxt = (_dot(a, wu_ref[j + 1]), _dot(a, wu_ref[nh + j + 1]))
            as_ref[j] = ag.astype(BF16)
            as_ref[nh + j] = av.astype(BF16)
            cg = _conv3(ag, carry[j], cw_ref[j], cb_ref[j], tm)
            cv = _conv3(av, carry[nh + j], cw_ref[nh + j], cb_ref[nh + j], tm)
            carry[j] = ag[tm - HALO:]
            carry[nh + j] = av[tm - HALO:]
            cs_ref[j] = cg.astype(BF16)
            cs_ref[nh + j] = cv.astype(BF16)
            act = (cg * _sigmoid(cg) * cv).astype(BF16)
            acc = acc + _dot(act, wd_ref[j * n:(j + 1) * n, :])
        if mode == "norm":
            o1_ref[...] = acc
            o2_ref[...] = (acc * _rstd(acc) * e_ref[...]).astype(BF16)
        else:
            err = acc - e_ref[...]
            o1_ref[...] = (err * (1.0 / D)).astype(o1_ref.dtype)
            o2_ref[...] = jnp.full(o2_ref.shape, jnp.sum(err * err), F32)

    row = pl.BlockSpec((tm, D), lambda i: (i, 0))
    if mode == "norm":
        e_spec, o2_spec, o2_shape = pl.BlockSpec((1, D), lambda i: (0, 0)), row, S((t, D), BF16)
    else:
        e_spec, o2_spec, o2_shape = row, pl.BlockSpec((None, 8, 128), lambda i: (i, 0, 0)), S((ni, 8, 128), F32)
    aspec = pl.BlockSpec((2 * nh, tm, n), lambda i: (0, i, 0))
    return _run(
        body, [hn, h, wup, wdown, cw, cb, extra], hook, grid=(ni,), name=name, semantics=("arbitrary",),
        in_specs=[pl.BlockSpec((tm, k), lambda i: (i, 0)), row, _resident(wup.shape), _resident(wdown.shape),
                  _resident(cw.shape), _resident(cb.shape), e_spec],
        out_specs=[aspec, aspec, row, o2_spec],
        out_shape=[S((2 * nh, t, n), BF16), S((2 * nh, t, n), BF16), S((t, D), F32 if mode == "norm" else DH), o2_shape],
        scratch_shapes=[pltpu.VMEM((2 * nh, HALO, n), F32)])


def _tril_mask():
    r = lax.broadcasted_iota(jnp.int32, (CHUNK, CHUNK), 0)
    c = lax.broadcasted_iota(jnp.int32, (CHUNK, CHUNK), 1)
    return r >= c


def _sgu_gate_fwd(a_s, vgain, ws, bst, name, hook=None):
    t = a_s.shape[1]
    sw = a_s.shape[2]
    gps = sw // CHUNK

    def body(a_ref, vg_ref, ws_ref, b_ref, o_ref):
        v = _gelu(jnp.concatenate([a_ref[4 + s].astype(F32) for s in range(4)], axis=1))
        vn = (v * _rstd(v) * vg_ref[...]).astype(BF16)
        tri = _tril_mask()
        for g in range(SGU_G):
            w = jnp.where(tri, ws_ref[g], 0.0).astype(BF16)
            sg = _dot(w, vn[:, g * CHUNK:(g + 1) * CHUNK]) + b_ref[:, g:g + 1]
            lo = (g % gps) * CHUNK
            u = _gelu(a_ref[g // gps, :, lo:lo + CHUNK].astype(F32))
            o_ref[g // gps, :, lo:lo + CHUNK] = (u * sg).astype(BF16)

    return _run(
        body, [a_s, vgain, ws, bst], hook, grid=(t // CHUNK,), name=name, semantics=("parallel",),
        in_specs=[pl.BlockSpec((8, CHUNK, sw), lambda n: (0, n, 0)), pl.BlockSpec((1, SGU_W), lambda n: (0, 0)),
                  pl.BlockSpec((SGU_G, CHUNK, CHUNK), lambda n: (0, 0, 0)), pl.BlockSpec((CHUNK, SGU_G), lambda n: (0, 0))],
        out_specs=pl.BlockSpec((4, CHUNK, sw), lambda n: (0, n, 0)), out_shape=S((4, t, sw), BF16))


def _resid_mm(a_s, w, resid, extra, mode, name, hook=None, fm=False):
    nk, t, kc = (1, a_s.shape[1], a_s.shape[0]) if fm else a_s.shape
    tm = _tm(t)
    ni = t // tm

    def body(a_ref, w_ref, r_ref, e_ref, o1_ref, o2_ref):
        h = r_ref[...]
        if fm:
            h = h + _dot_tn(a_ref[...], w_ref[...])
        for j in range(0 if fm else nk):
            h = h + _dot(a_ref[j], w_ref[j * kc:(j + 1) * kc, :])
        if mode == "norm":
            o1_ref[...] = h
            o2_ref[...] = (h * _rstd(h) * e_ref[...]).astype(BF16)
        else:
            err = h - e_ref[...]
            o1_ref[...] = (err * (1.0 / D)).astype(o1_ref.dtype)
            o2_ref[...] = jnp.full(o2_ref.shape, jnp.sum(err * err), F32)

    row = pl.BlockSpec((tm, D), lambda i: (i, 0))
    if mode == "norm":
        e_spec, o2_spec, o2_shape = pl.BlockSpec((1, D), lambda i: (0, 0)), row, S((t, D), BF16)
    else:
        e_spec, o2_spec, o2_shape = row, pl.BlockSpec((None, 8, 128), lambda i: (i, 0, 0)), S((ni, 8, 128), F32)
    return _run(
        body, [a_s, w, resid, extra], hook, grid=(ni,), name=name, semantics=("parallel",),
        in_specs=[pl.BlockSpec((kc, tm), lambda i: (0, i)) if fm else pl.BlockSpec((nk, tm, kc), lambda i: (0, i, 0)),
                  _resident(w.shape), row, e_spec],
        out_specs=[row, o2_spec], out_shape=[S((t, D), F32 if mode == "norm" else DH), o2_shape])


def _relbias_fwd(rel_bias_t, bucket_row, name):
    nb = bucket_row.shape[1]

    def body(rb_ref, bk_ref, o_ref):
        onehot = (lax.broadcasted_iota(jnp.int32, (REL_BUCKETS, nb), 0) == bk_ref[...]).astype(F32)
        o_ref[...] = jnp.dot(rb_ref[...], onehot, precision=lax.Precision.HIGHEST, preferred_element_type=F32)

    return pl.pallas_call(body, out_shape=S((NH, nb), F32), name=name)(rel_bias_t, bucket_row)


def _relbias_bwd(dbias, bucket_row, name):
    nb = bucket_row.shape[1]

    def body(db_ref, bk_ref, o_ref):
        onehot = (lax.broadcasted_iota(jnp.int32, (REL_BUCKETS, nb), 0) == bk_ref[...]).astype(F32)
        o_ref[...] = lax.dot_general(db_ref[...], onehot, (((1,), (1,)), ((), ())),
                                     precision=lax.Precision.HIGHEST, preferred_element_type=F32)

    return pl.pallas_call(body, out_shape=S((NH, REL_BUCKETS), F32), name=name)(dbias, bucket_row)


QKV = D + 2 * NKV * HD
KV0 = D


def _rstd_rows(x):
    return lax.rsqrt(jnp.mean(x * x, axis=0, keepdims=True) + EPS)


def _attn_valid(n):
    kj = lax.broadcasted_iota(jnp.int32, (2 * CHUNK, CHUNK), 0)
    qi = lax.broadcasted_iota(jnp.int32, (2 * CHUNK, CHUNK), 1)
    dist = qi + CHUNK - kj
    return (dist >= 0) & (dist < CHUNK) & ((n > 0) | (kj >= CHUNK))


def _attn_band(cur_ref, prev_ref, row):
    return jnp.concatenate([prev_ref[row - KV0:row - KV0 + HD, :], cur_ref[row:row + HD, :]], axis=1)


def _attn_probs(kn_tok, qn, bias, valid, sink):
    s = _dot(kn_tok, qn) * (HD ** -0.5) + bias
    s = jnp.where(valid, s, -jnp.inf)
    m = jnp.maximum(jnp.max(s, axis=0, keepdims=True), sink)
    p = jnp.exp(s - m)
    psink = jnp.exp(sink - m)
    inv = 1.0 / (jnp.sum(p, axis=0, keepdims=True) + psink)
    return p * inv, psink * inv


def _attn_fwd(qkv_t, qg, kg, sinks, bias, name, hook=None):
    t = qkv_t.shape[1]

    def body(cur_ref, prev_ref, qg_ref, kg_ref, sink_ref, bias_ref, o_ref):
        n = pl.program_id(0)
        valid = _attn_valid(n)
        ks = [_attn_band(cur_ref, prev_ref, KV0 + HD * h) for h in range(NKV)]
        kn_toks = [(k * _rstd_rows(k) * kg_ref[...]).astype(BF16).T for k in ks]
        vbs = [_attn_band(cur_ref, prev_ref, KV0 + HD * (NKV + h)).astype(BF16) for h in range(NKV)]
        qs = [cur_ref[HD * hq:HD * (hq + 1), :] for hq in range(NH)]
        qns = [(q * _rstd_rows(q) * qg_ref[...]).astype(BF16) for q in qs]
        ps = [_attn_probs(kn_toks[hq // KVG], qns[hq], bias_ref[hq], valid, sink_ref[hq])[0] for hq in range(NH)]
        for hq in range(NH):
            o_ref[HD * hq:HD * (hq + 1), :] = _dot(vbs[hq // KVG], ps[hq].astype(BF16)).astype(BF16)

    col = pl.BlockSpec((HD, 1), lambda n: (0, 0))
    return _run(
        body, [qkv_t, qkv_t, qg, kg, sinks, bias], hook, grid=(t // CHUNK,), name=name, semantics=("parallel",),
        in_specs=[pl.BlockSpec((QKV, CHUNK), lambda n: (0, n)),
                  pl.BlockSpec((QKV - KV0, CHUNK), lambda n: (KV0 // (QKV - KV0), jnp.maximum(n - 1, 0))),
                  col, col, pl.BlockSpec(memory_space=pltpu.SMEM), pl.BlockSpec((NH, 2 * CHUNK, CHUNK), lambda n: (0, 0, 0))],
        out_specs=pl.BlockSpec((D, CHUNK), lambda n: (0, n)), out_shape=S((D, t), BF16))


def _dx_rows(dh, w, kc, out_dtype, name, hook=None):
    t = dh.shape[0]
    nk = w.shape[0] // kc
    tm = _tm(t)

    def body(d_ref, w_ref, o_ref):
        dhb = d_ref[...].astype(BF16)
        for j in range(nk):
            o_ref[j] = _dot_nt(dhb, w_ref[j * kc:(j + 1) * kc, :]).astype(out_dtype)

    return _run(
        body, [dh, w], hook, grid=(t // tm,), name=name, semantics=("parallel",),
        in_specs=[pl.BlockSpec((tm, D), lambda i: (i, 0)), _resident(w.shape)],
        out_specs=pl.BlockSpec((nk, tm, kc), lambda i: (0, i, 0)), out_shape=S((nk, t, kc), out_dtype))


def _dx_rows_t(dh, w, name, hook=None):
    t = dh.shape[0]
    k = w.shape[0]
    tm = _tm(t)

    def body(d_ref, w_ref, o_ref):
        o_ref[...] = _dot_nt(w_ref[...], d_ref[...].astype(BF16)).astype(BF16)

    return _run(
        body, [dh, w], hook, grid=(t // tm,), name=name, semantics=("parallel",),
        in_specs=[pl.BlockSpec((tm, D), lambda i: (i, 0)), _resident(w.shape)],
        out_specs=pl.BlockSpec((k, tm), lambda i: (0, i)), out_shape=S((k, t), BF16))


def _ffn_bwd1(dh, c, wdown, name, hook=None):
    ns, t, n = c.shape
    nh = ns // 2
    tm = min(FFN_ROWS, t)
    ni = t // tm

    def body(d_ref, c_ref, wd_ref, dc_ref, dw_hbm, dwb_hbm, acc, stage):
        i = pl.program_id(0)

        @pl.when(i == 0)
        def _():
            acc[...] = jnp.zeros_like(acc)

        dhb = d_ref[...].astype(BF16)
        for j in range(nh):
            dact = _dot_nt(dhb, wd_ref[j * n:(j + 1) * n, :])
            cg = c_ref[j].astype(F32)
            cv = c_ref[nh + j].astype(F32)
            sg = _sigmoid(cg)
            gs = cg * sg
            acc[j * n:(j + 1) * n, :] += _dot_tn((gs * cv).astype(BF16), dhb)
            dc_ref[j] = (dact * cv * (sg + gs * (1.0 - sg))).astype(BF16)
            dc_ref[nh + j] = (dact * gs).astype(BF16)

        @pl.when(i == ni - 1)
        def _():
            pltpu.sync_copy(acc, dw_hbm)
            for j in range(nh):
                stage[...] = acc[j * n:(j + 1) * n, :].astype(BF16)
                pltpu.sync_copy(stage, dwb_hbm.at[pl.ds(j * n, n), :])

    slab = pl.BlockSpec((ns, tm, n), lambda i: (0, i, 0))
    return _run(
        body, [dh, c, wdown], hook, grid=(ni,), name=name, semantics=("arbitrary",),
        in_specs=[pl.BlockSpec((tm, D), lambda i: (i, 0)), slab, _resident(wdown.shape)],
        out_specs=[slab, ANY, ANY], out_shape=[S((ns, t, n), BF16), S(wdown.shape, F32), S(wdown.shape, BF16)],
        scratch_shapes=[pltpu.VMEM(wdown.shape, F32), pltpu.VMEM((n, D), BF16)])


def _ffn_bwd2(dc, a, wup, cw, h, gain, dh_in, name, hook=None):
    ns, t, n = dc.shape
    tm = min(FFN_ROWS, t)
    ni = t // tm

    def body(dc_ref, a_ref, wu_ref, cw_ref, h_ref, g_ref, di_ref, da_ref, o_ref, dg_ref, dcw_ref, dcb_ref, carry, keep):
        i = pl.program_id(0)

        @pl.when(i == 0)
        def _():
            carry[...] = jnp.zeros_like(carry)
            dg_ref[...] = jnp.zeros_like(dg_ref)
            dcw_ref[...] = jnp.zeros_like(dcw_ref)
            dcb_ref[...] = jnp.zeros_like(dcb_ref)

        rsum = lambda v: jnp.sum(v, axis=0, keepdims=True)
        acc = jnp.zeros((tm, D), F32)
        for s in range(ns):
            x = dc_ref[s].astype(F32)
            ext = jnp.concatenate([x, carry[s]], axis=0)
            keep[0] = ext[1:1 + tm]
            keep[1] = ext[2:2 + tm]
            x1, x2 = keep[0], keep[1]
            cwv = cw_ref[s]
            da = (cwv[2:3] * x + cwv[1:2] * x1 + cwv[0:1] * x2).astype(BF16)
            carry[s] = x[:HALO]
            da_ref[s] = da
            acc = acc + _dot_nt(da, wu_ref[s])
            av = a_ref[s].astype(F32)
            dcw_ref[s] += jnp.concatenate([rsum(x2 * av), rsum(x1 * av), rsum(x * av)], axis=0)
            dcb_ref[s] += rsum(x)
        hv = h_ref[...]
        r = _rstd(hv)
        gg = acc * g_ref[...]
        dh_new = di_ref[...].astype(F32) + r * gg - hv * (r * r * r * jnp.mean(gg * hv, axis=-1, keepdims=True))
        o_ref[...] = dh_new.astype(o_ref.dtype)
        dg_ref[...] += jnp.sum(acc * hv * r, axis=0, keepdims=True)

    slab = pl.BlockSpec((ns, tm, n), lambda i: (0, ni - 1 - i, 0))
    row = pl.BlockSpec((tm, D), lambda i: (ni - 1 - i, 0))
    vec = pl.BlockSpec((1, D), lambda i: (0, 0))
    whole = lambda shape: pl.BlockSpec(shape, lambda i: (0,) * len(shape))
    return _run(
        body, [dc, a, wup, cw, h, gain, dh_in], hook, grid=(ni,), name=name, semantics=("arbitrary",),
        in_specs=[slab, slab, _resident(wup.shape), _resident(cw.shape), row, vec, row],
        out_specs=[slab, row, vec, whole((ns, 3, n)), whole((ns, 1, n))],
        out_shape=[S((ns, t, n), BF16), S((t, D), DH), S((1, D), F32), S((ns, 3, n), F32), S((ns, 1, n), F32)],
        scratch_shapes=[pltpu.VMEM((ns, HALO, n), F32), pltpu.VMEM((2, tm, n), F32)])


def _dw_slot(hn, dy_s, name, hook=None):
    t, k = hn.shape
    ns, _, n = dy_s.shape
    tm = _tm(t)

    def body(a_ref, b_ref, o_ref, ob_ref, at_ref):
        @pl.when(pl.program_id(0) == 0)
        def _():
            for i in range(t // tm):
                at_ref[:, i * tm:(i + 1) * tm] = a_ref[i * tm:(i + 1) * tm, :].T

        acc = _dot(at_ref[...], b_ref[...])
        o_ref[...] = acc
        ob_ref[...] = acc.astype(BF16)

    ospec = pl.BlockSpec((None, k, n), lambda j: (j, 0, 0))
    return _run(
        body, [hn, dy_s], hook, grid=(ns,), name=name, semantics=("arbitrary",),
        in_specs=[_resident(hn.shape), pl.BlockSpec((None, t, n), lambda j: (j, 0, 0))],
        out_specs=[ospec, ospec], out_shape=[S((ns, k, n), F32), S((ns, k, n), BF16)],
        scratch_shapes=[pltpu.VMEM((k, t), BF16)])


def _dw_rows(a_s, dh, name, hook=None, fm=False):
    nk, t, kc = (1, a_s.shape[1], a_s.shape[0]) if fm else a_s.shape
    tm = _tm(t)
    ni = t // tm

    def body(a_ref, d_ref, o_ref, ob_ref):
        i = pl.program_id(0)
        dhb = d_ref[...].astype(BF16)

        @pl.when(i == 0)
        def _():
            o_ref[...] = jnp.zeros_like(o_ref)

        if fm:
            o_ref[...] += _dot(a_ref[...], dhb)
        for j in range(0 if fm else nk):
            o_ref[j * kc:(j + 1) * kc, :] += _dot_tn(a_ref[j], dhb)

        @pl.when(i == ni - 1)
        def _():
            ob_ref[...] = o_ref[...].astype(BF16)

    ospec = pl.BlockSpec((nk * kc, D), lambda i: (0, 0))
    return _run(
        body, [a_s, dh], hook, grid=(ni,), name=name, semantics=("arbitrary",),
        in_specs=[pl.BlockSpec((kc, tm), lambda i: (0, i)) if fm else pl.BlockSpec((nk, tm, kc), lambda i: (0, i, 0)),
                  pl.BlockSpec((tm, D), lambda i: (i, 0))],
        out_specs=[ospec, ospec], out_shape=[S((nk * kc, D), F32), S((nk * kc, D), BF16)])


def _dx_slot_normbwd(dy_s, wg, h, gain, dh_in, name, hook=None, fm=False, out_dtype=F32):
    ns, t, n = (1, dy_s.shape[1], dy_s.shape[0]) if fm else dy_s.shape
    tm = _tm(t)

    def body(dy_ref, w_ref, h_ref, g_ref, di_ref, o_ref, dg_ref):
        i = pl.program_id(0)

        @pl.when(i == 0)
        def _():
            dg_ref[...] = jnp.zeros_like(dg_ref)

        g = _dot_tn(dy_ref[...], w_ref[...]) if fm else _dot_nt(dy_ref[0], w_ref[0])
        for s in range(1, ns):
            g = g + _dot_nt(dy_ref[s], w_ref[s])
        hv = h_ref[...]
        r = _rstd(hv)
        gg = g * g_ref[...]
        dh_new = di_ref[...].astype(F32) + r * gg - hv * (r * r * r * jnp.mean(gg * hv, axis=-1, keepdims=True))
        o_ref[...] = dh_new.astype(o_ref.dtype)
        dg_ref[...] += jnp.sum(g * hv * r, axis=0, keepdims=True)

    row = pl.BlockSpec((tm, D), lambda i: (i, 0))
    vec = pl.BlockSpec((1, D), lambda i: (0, 0))
    return _run(
        body, [dy_s, wg, h, gain, dh_in], hook, grid=(t // tm,), name=name, semantics=("arbitrary",),
        in_specs=[pl.BlockSpec((n, tm), lambda i: (0, i)) if fm else pl.BlockSpec((ns, tm, n), lambda i: (0, i, 0)),
                  _resident(wg.shape), row, vec, row],
        out_specs=[row, vec], out_shape=[S((t, D), out_dtype), S((1, D), F32)])


def _sgu_gate_bwd(a_s, dg_s, vgain, ws, bst, name, hook=None):
    t = a_s.shape[1]
    sw = a_s.shape[2]
    gps = sw // CHUNK

    def body(a_ref, dg_ref, vg_ref, ws_ref, b_ref, da_ref, dws_ref, dbt_ref, dvg_ref, dvn_ref):
        n = pl.program_id(0)

        @pl.when(n == 0)
        def _():
            dws_ref[...] = jnp.zeros_like(dws_ref)
            dbt_ref[...] = jnp.zeros_like(dbt_ref)
            dvg_ref[...] = jnp.zeros_like(dvg_ref)

        vpre = jnp.concatenate([a_ref[4 + s].astype(F32) for s in range(4)], axis=1)
        v, v_grad = _gelu_and_grad(vpre)
        r = _rstd(v)
        vhat = v * r
        vn = (vhat * vg_ref[...]).astype(BF16)
        tri = _tril_mask()
        lane = lax.broadcasted_iota(jnp.int32, (CHUNK, CHUNK), 1)
        dbt = jnp.zeros((CHUNK, CHUNK), F32)
        for g in range(SGU_G):
            w = jnp.where(tri, ws_ref[g], 0.0).astype(BF16)
            vng = vn[:, g * CHUNK:(g + 1) * CHUNK]
            sg = _dot(w, vng) + b_ref[:, g:g + 1]
            lo = (g % gps) * CHUNK
            u, u_grad = _gelu_and_grad(a_ref[g // gps, :, lo:lo + CHUNK].astype(F32))
            dgate = dg_ref[g // gps, :, lo:lo + CHUNK].astype(F32)
            da_ref[g // gps, :, lo:lo + CHUNK] = (dgate * sg * u_grad).astype(BF16)
            ds = dgate * u
            dsb = ds.astype(BF16)
            dvn_ref[:, g * CHUNK:(g + 1) * CHUNK] = _dot_tn(w, dsb)
            dws_ref[g] += jnp.where(tri, _dot_nt(dsb, vng), 0.0)
            dbt = dbt + jnp.where(lane == g, jnp.sum(ds, axis=-1, keepdims=True), 0.0)
        dbt_ref[...] += dbt
        dvn = dvn_ref[...]
        dvg_ref[...] += jnp.sum(dvn * vhat, axis=0, keepdims=True)
        gg = dvn * vg_ref[...]
        dv = r * gg - v * (r * r * r * jnp.mean(gg * v, axis=-1, keepdims=True))
        dav = (dv * v_grad).astype(BF16)
        for s in range(4):
            da_ref[4 + s] = dav[:, s * sw:(s + 1) * sw]

    return _run(
        body, [a_s, dg_s, vgain, ws, bst], hook, grid=(t // CHUNK,), name=name, semantics=("arbitrary",),
        in_specs=[pl.BlockSpec((8, CHUNK, sw), lambda n: (0, n, 0)), pl.BlockSpec((4, CHUNK, sw), lambda n: (0, n, 0)),
                  pl.BlockSpec((1, SGU_W), lambda n: (0, 0)), pl.BlockSpec((SGU_G, CHUNK, CHUNK), lambda n: (0, 0, 0)),
                  pl.BlockSpec((CHUNK, SGU_G), lambda n: (0, 0))],
        out_specs=[pl.BlockSpec((8, CHUNK, sw), lambda n: (0, n, 0)), pl.BlockSpec((SGU_G, CHUNK, CHUNK), lambda n: (0, 0, 0)),
                   pl.BlockSpec((CHUNK, CHUNK), lambda n: (0, 0)), pl.BlockSpec((1, SGU_W), lambda n: (0, 0))],
        out_shape=[S((8, t, sw), BF16), S((SGU_G, CHUNK, CHUNK), F32), S((CHUNK, CHUNK), F32), S((1, SGU_W), F32)],
        scratch_shapes=[pltpu.VMEM((CHUNK, SGU_W), F32)])


def _attn_bwd(qkv_t, do_t, qg, kg, sinks, bias, name, hook=None):
    t = qkv_t.shape[1]
    nb = t // CHUNK

    def body(cur_ref, prev_ref, do_ref, qg_ref, kg_ref, sink_ref, bias_ref,
             o_ref, dqg_out, dkg_out, dsk_out, dbias_ref, carry, dqg_ref, dkg_ref, dsk_ref):
        n = pl.program_id(0)

        @pl.when(n == 0)
        def _():
            carry[...] = jnp.zeros_like(carry)
            dqg_ref[...] = jnp.zeros_like(dqg_ref)
            dkg_ref[...] = jnp.zeros_like(dkg_ref)
            dsk_ref[...] = jnp.zeros_like(dsk_ref)
            dbias_ref[...] = jnp.zeros_like(dbias_ref)

        @pl.when(n < nb)
        def _():
            valid = _attn_valid(n)
            o_ref[0:KV0, :] = carry[0:KV0, :].astype(BF16)
            kvs, heads = range(NKV), range(NH)
            group = lambda h: range(KVG * h, KVG * (h + 1))
            ks = [_attn_band(cur_ref, prev_ref, KV0 + HD * h) for h in kvs]
            rks = [_rstd_rows(k) for k in ks]
            khats = [k * rk for k, rk in zip(ks, rks)]
            kns = [(khat * kg_ref[...]).astype(BF16) for khat in khats]
            kn_toks = [kn.T for kn in kns]
            vbs = [_attn_band(cur_ref, prev_ref, KV0 + HD * (NKV + h)).astype(BF16) for h in kvs]
            v_toks = [vb.T for vb in vbs]
            qs = [cur_ref[HD * hq:HD * (hq + 1), :] for hq in heads]
            rqs = [_rstd_rows(q) for q in qs]
            qhats = [q * rq for q, rq in zip(qs, rqs)]
            qns = [(qhat * qg_ref[...]).astype(BF16) for qhat in qhats]
            probs = [_attn_probs(kn_toks[hq // KVG], qns[hq], bias_ref[hq], valid, sink_ref[hq]) for hq in heads]
            dohs = [do_ref[HD * hq:HD * (hq + 1), :] for hq in heads]
            dps = [_dot(v_toks[hq // KVG], dohs[hq]) for hq in heads]
            dsums = [jnp.sum(p * dp, axis=0, keepdims=True) for (p, _), dp in zip(probs, dps)]
            dss = [p * (dp - dsum) for (p, _), dp, dsum in zip(probs, dps, dsums)]
            for hq in heads:
                dsk_ref[hq:hq + 1, :] -= probs[hq][1] * dsums[hq]
                dbias_ref[hq] += dss[hq]
            dvs = [sum(_dot_nt(dohs[hq], probs[hq][0].astype(BF16)) for hq in group(h)) for h in kvs]
            dscs = [(ds * (HD ** -0.5)).astype(BF16) for ds in dss]
            dqns = [_dot(kns[hq // KVG], dscs[hq]) for hq in heads]
            dkns = [sum(_dot_nt(qns[hq], dscs[hq]) for hq in group(h)) for h in kvs]
            dqg_ref[...] += sum(dqn * qhat for dqn, qhat in zip(dqns, qhats))
            for hq in heads:
                gq = dqns[hq] * qg_ref[...]
                carry[HD * hq:HD * (hq + 1), :] = rqs[hq] * gq - qs[hq] * (
                    rqs[hq] * rqs[hq] * rqs[hq] * jnp.mean(gq * qs[hq], axis=0, keepdims=True))
            dkg_ref[...] += sum(dkn * khat for dkn, khat in zip(dkns, khats))
            for h in kvs:
                krow, vrow = KV0 + HD * h, KV0 + HD * (NKV + h)
                gk = dkns[h] * kg_ref[...]
                dk = rks[h] * gk - ks[h] * (rks[h] * rks[h] * rks[h] * jnp.mean(gk * ks[h], axis=0, keepdims=True))
                o_ref[krow:krow + HD, :] = (carry[krow:krow + HD, :] + dk[:, :CHUNK]).astype(BF16)
                o_ref[vrow:vrow + HD, :] = (carry[vrow:vrow + HD, :] + dvs[h][:, :CHUNK]).astype(BF16)
                carry[krow:krow + HD, :] = dk[:, CHUNK:]
                carry[vrow:vrow + HD, :] = dvs[h][:, CHUNK:]

        @pl.when(n == nb)
        def _():
            o_ref[...] = carry[...].astype(BF16)
            dqg_out[...] = jnp.sum(dqg_ref[...], axis=1, keepdims=True)
            dkg_out[...] = jnp.sum(dkg_ref[...], axis=1, keepdims=True)
            dsk_out[...] = jnp.sum(dsk_ref[...], axis=1, keepdims=True)

    cur = lambda n: (0, jnp.minimum(n, nb - 1))
    col = pl.BlockSpec((HD, 1), lambda n: (0, 0))
    whole = lambda shape: pl.BlockSpec(shape, lambda n: (0,) * len(shape))
    return _run(
        body, [qkv_t, qkv_t, do_t, qg, kg, sinks, bias], hook, grid=(nb + 1,), name=name, semantics=("arbitrary",),
        in_specs=[pl.BlockSpec((QKV, CHUNK), cur),
                  pl.BlockSpec((QKV - KV0, CHUNK), lambda n: (KV0 // (QKV - KV0), jnp.clip(n - 1, 0, nb - 1))),
                  pl.BlockSpec((D, CHUNK), cur), col, col, pl.BlockSpec(memory_space=pltpu.SMEM), whole((NH, 2 * CHUNK, CHUNK))],
        out_specs=[pl.BlockSpec((QKV, CHUNK), lambda n: (0, jnp.maximum(n - 1, 0))), whole((HD, 1)), whole((HD, 1)),
                   whole((NH, 1)), whole((NH, 2 * CHUNK, CHUNK))],
        out_shape=[S((QKV, t), BF16), S((HD, 1), F32), S((HD, 1), F32), S((NH, 1), F32), S((NH, 2 * CHUNK, CHUNK), F32)],
        scratch_shapes=[pltpu.VMEM((QKV, CHUNK), F32), pltpu.VMEM((HD, CHUNK), F32), pltpu.VMEM((HD, 2 * CHUNK), F32),
                        pltpu.VMEM((NH, CHUNK), F32)])


class _Plain:
    def __init__(self, wg):
        self.full, self.grads = wg, {}

    def w(self, n):
        return self.full[n]

    def hook(self, host):
        return None

    def grad(self, n, pair):
        self.grads[n] = pair

    def small(self, g_rep):
        pass

    def sync(self, point):
        pass

    def first_norm(self, x, gain):
        return _rmsnorm(x, gain, "norm0")


def _local_step(x, target, rep, sch):
    bucket_row = jnp.asarray(_rel_tables().T.reshape(1, -1))
    bias = _relbias_fwd(rep["rel_bias"].T, bucket_row, "relbias_fwd").reshape(NH, 2 * CHUNK, CHUNK)
    bst = rep["sgu_b_s"][0].T
    ws = rep["sgu_w_s"][0]
    vgain = rep["sgu_v_gain"]
    qg, kg, sinks = rep["attn_q_gain"].reshape(HD, 1), rep["attn_k_gain"].reshape(HD, 1), rep["attn_sinks"][0]
    w_down = lambda l: sch.w("ffn_w_down%d" % l).reshape(D_FF, D)
    w_up = lambda l: sch.w("ffn_w_up%d" % l)
    cb = [rep["ffn_conv_b"][l].reshape(8, 1, -1) for l in range(2)]
    mixg = [rep["mix_norm"][l:l + 1] for l in range(2)]
    ffng = [rep["ffn_norm"][l:l + 1] for l in range(2)]
    rows = lambda pair: tuple(g.reshape(N_DEV, -1, D) for g in pair)
    hk = sch.hook

    hn0 = sch.first_norm(x, mixg[0])
    cw = [sch.w("ffn_conv_w")[:, 3 * l:3 * l + 3] for l in range(2)]
    a0 = _mm_slot(hn0, sch.w("sgu_w_in"), BF16, "sgu_in", hk("sgu_in"))
    gated = _sgu_gate_fwd(a0, vgain, ws, bst, "sgu_gate", hk("sgu_gate"))
    h1, hn1 = _resid_mm(gated, sch.w("sgu_w_out").reshape(SGU_W, D), x, ffng[0], "norm", "sgu_out", hk("sgu_out"))
    sch.sync("before_ffn0")
    a_ff0, c_ff0, h2, hn2 = _ffn_fwd(hn1, h1, w_up(0), w_down(0), cw[0], cb[0], mixg[1], "norm", "ffn0_fwd", hk("ffn0_fwd"))
    qkv = _mm_t(hn2, sch.w("attn_w_qkv"), "qkv", hk("qkv"))
    o = _attn_fwd(qkv, qg, kg, sinks, bias, "attn", hk("attn"))
    h3, hn3 = _resid_mm(o, sch.w("attn_w_o").reshape(D, D), h2, ffng[1], "norm", "attn_out", hk("attn_out"), fm=True)
    a_ff1, c_ff1, dy, sq = _ffn_fwd(hn3, h3, w_up(1), w_down(1), cw[1], cb[1], target, "loss", "ffn1_fwd_loss", hk("ffn1_fwd_loss"))
    loss = (0.5 / D) * jnp.sum(sq[:, 0, 0])

    def ffn_bwd(dh, h_in, hn, a, c, l, tag):
        dc, g_down, g_down_b = _ffn_bwd1(dh, c, w_down(l), tag + "_bwd1", hk(tag + "_bwd1"))
        sch.grad("ffn_w_down%d" % l, rows((g_down, g_down_b)))
        da, dh_new, dgain, g_cw, g_cb = _ffn_bwd2(dc, a, w_up(l), cw[l], h_in, ffng[l], dh, tag + "_bwd2", hk(tag + "_bwd2"))
        sch.grad("ffn_w_up%d" % l, _dw_slot(hn, da, tag + "_dw_up", hk(tag + "_dw_up")))
        return dh_new, dgain, g_cw, g_cb.reshape(-1)

    dh, d_ffng1, g_cw1, g_cb1 = ffn_bwd(dy, h3, hn3, a_ff1, c_ff1, 1, "ffn1")
    do = _dx_rows_t(dh, sch.w("attn_w_o").reshape(D, D), "attn_do", hk("attn_do"))
    sch.grad("attn_w_o", rows(_dw_rows(o, dh, "dw_o", hk("dw_o"), fm=True)))
    dqkv, d_qg, d_kg, d_sk, d_bias = _attn_bwd(qkv, do, qg, kg, sinks, bias, "attn_bwd", hk("attn_bwd"))
    sch.grad("attn_w_qkv", tuple(g.reshape(N_DEV, -1, D) for g in _dw_rows(dqkv, hn2, "dw_qkv", hk("dw_qkv"), fm=True)))
    dh, d_mixg1 = _dx_slot_normbwd(dqkv, sch.w("attn_w_qkv").reshape(QKV, D), h2, mixg[1], dh, "dx_qkv", hk("dx_qkv"), fm=True,
                                   out_dtype=DH)
    d_relb = _relbias_bwd(d_bias.reshape(NH, -1), bucket_row, "relbias_bwd").T
    g_rep = {"attn_q_gain": d_qg.reshape(1, HD), "attn_k_gain": d_kg.reshape(1, HD), "attn_sinks": d_sk.reshape(1, NH),
             "rel_bias": d_relb}
    sch.small(g_rep)
    dh, d_ffng0, g_cw0, g_cb0 = ffn_bwd(dh, h1, hn1, a_ff0, c_ff0, 0, "ffn0")
    g_cw = jnp.concatenate([g_cw0, g_cw1], axis=1)
    sch.grad("ffn_conv_w", (g_cw, g_cw.astype(BF16)))
    g_ffn = {"ffn_norm": jnp.concatenate([d_ffng0, d_ffng1], axis=0), "ffn_conv_b": jnp.stack([g_cb0, g_cb1], axis=0)}
    sch.small(g_ffn)
    dgated = _dx_rows(dh, sch.w("sgu_w_out").reshape(SGU_W, D), SGU_W // 4, BF16, "sgu_dgated", hk("sgu_dgated"))
    sch.grad("sgu_w_out", rows(_dw_rows(gated, dh, "dw_sgu_out", hk("dw_sgu_out"))))
    da0, d_ws, d_bst, d_vgain = _sgu_gate_bwd(a0, dgated, vgain, ws, bst, "sgu_gate_bwd", hk("sgu_gate_bwd"))
    g_sgu = {"sgu_v_gain": d_vgain, "sgu_w_s": d_ws[None], "sgu_b_s": d_bst[:, :SGU_G].T[None]}
    sch.small(g_sgu)
    sch.grad("sgu_w_in", _dw_slot(hn0, da0, "dw_sgu_in", hk("dw_sgu_in")))
    sch.sync("after_dw")
    grad_x, d_mixg0 = _dx_slot_normbwd(da0, sch.w("sgu_w_in"), x, mixg[0], dh, "dx_sgu_in", hk("dx_sgu_in"))
    g_mix = {"mix_norm": jnp.concatenate([d_mixg0, d_mixg1], axis=0)}
    sch.small(g_mix)
    for g in (g_ffn, g_sgu, g_mix):
        g_rep.update(g)
    return loss, grad_x, g_rep


def _allgather(xs, x_in, gain, name):
    nt = len(xs)
    t_rows = x_in.shape[0]
    tm = _tm(t_rows)

    def body(xin_ref, g_ref, *refs):
        x_refs, hn_ref, o_refs = refs[:nt], refs[nt], refs[nt + 1:2 * nt + 1]
        send_sems, recv_sems, local_sems = refs[2 * nt + 1:]
        x, y, c, chips = _place()
        me, sibling = (x, y, c), (x, y, 1 - c)

        def copy(t, k, block, to, src=None):
            px, py, pc = block
            dst = o_refs[t].at[4 * px + 2 * py + pc]
            return pltpu.make_async_remote_copy(
                src_ref=dst if src is None else src, dst_ref=dst, send_sem=send_sems.at[t, k], recv_sem=recv_sems.at[t, k],
                device_id=to, device_id_type=MESH)

        mine = lambda: [pltpu.make_async_copy(x_refs[t], o_refs[t].at[4 * x + 2 * y + c], local_sems.at[t]) for t in range(nt)]

        def first():
            out = []
            for t in range(nt):
                out.append(copy(t, 0, me, sibling, src=x_refs[t]))
                out += [copy(t, 1 + j, me, (*chip, c), src=x_refs[t]) for j, chip in enumerate(chips)]
            return out

        @pl.when(pl.program_id(0) == 0)
        def _():
            for cp in mine() + first():
                cp.start()

        xv = xin_ref[...]
        hn_ref[...] = (xv * _rstd(xv) * g_ref[...]).astype(BF16)

        @pl.when(pl.program_id(0) == pl.num_programs(0) - 1)
        def _():
            passed = []
            for j, chip in enumerate(chips):
                for t in range(nt):
                    copy(t, 1 + j, (*chip, c), me).wait_recv()
                    fwd = copy(t, 4 + j, (*chip, c), sibling)
                    fwd.start()
                    passed.append(fwd)
            for t in range(nt):
                copy(t, 0, sibling, me).wait_recv()
                for j, chip in enumerate(chips):
                    copy(t, 4 + j, (*chip, 1 - c), me).wait_recv()
            for cp in first() + passed:
                cp.wait_send()
            for cp in mine():
                cp.wait()

    res = pl.pallas_call(
        body, name=name, grid=(t_rows // tm,),
        in_specs=[pl.BlockSpec((tm, D), lambda i: (i, 0)), pl.BlockSpec((1, D), lambda i: (0, 0))] + [ANY] * nt,
        out_specs=[pl.BlockSpec((tm, D), lambda i: (i, 0))] + [ANY] * nt,
        out_shape=[S((t_rows, D), BF16)] + [S((N_DEV,) + a.shape, a.dtype) for a in xs],
        scratch_shapes=[pltpu.SemaphoreType.DMA((nt, 7)), pltpu.SemaphoreType.DMA((nt, 7)), pltpu.SemaphoreType.DMA((nt,))],
        compiler_params=pltpu.CompilerParams(dimension_semantics=("arbitrary",), has_side_effects=True))(x_in, gain, *xs)
    return res[0], res[1:]


def _exchange(hook, name):
    comm = hook()
    ci, co = len(comm.inputs), len(comm.out_shapes)

    def body(*refs):
        cins, couts = refs[:ci], refs[ci:ci + co]
        send, recv = refs[-2:]
        comm.start(cins, couts, send, recv)
        comm.finish(cins, couts, send, recv)

    res = pl.pallas_call(
        body, name=name, in_specs=[ANY] * ci, out_specs=[ANY] * co, out_shape=comm.out_shapes,
        scratch_shapes=[pltpu.SemaphoreType.DMA((comm.n_sems,)), pltpu.SemaphoreType.DMA((comm.n_sems,))],
        input_output_aliases=dict(comm.aliases),
        compiler_params=pltpu.CompilerParams(has_side_effects=True))(*comm.inputs)
    hook(res)


def _row_tile(r):
    tr = r if r <= ROW_TILE or r % ROW_TILE else ROW_TILE
    assert r % tr == 0
    return tr


def _rs_partial(g32, sib, place, name):
    _, r, cdim = g32.shape
    tr = _row_tile(r)

    def body(place_ref, g_ref, s_ref, p_ref, own_ref):
        k = pl.program_id(1)
        tot = g_ref[...] + s_ref[...].astype(F32)
        p_ref[...] = tot.astype(BF16)

        @pl.when(k == place_ref[1])
        def _():
            own_ref[...] = tot

    grid_spec = pltpu.PrefetchScalarGridSpec(
        num_scalar_prefetch=1, grid=(r // tr, 4),
        in_specs=[pl.BlockSpec((None, None, tr, cdim), lambda i, k, pr: (k, pr[0], i, 0)),
                  pl.BlockSpec((None, tr, cdim), lambda i, k, pr: (k, i, 0))],
        out_specs=[pl.BlockSpec((None, tr, cdim), lambda i, k, pr: (k, i, 0)), pl.BlockSpec((tr, cdim), lambda i, k, pr: (i, 0))])
    return pl.pallas_call(
        body, grid_spec=grid_spec, name=name,
        out_shape=[S((4, r, cdim), BF16), S((r, cdim), F32)],
        compiler_params=_cp("parallel", "arbitrary"))(place, g32.reshape(4, 2, r, cdim), sib)


def _adamw_math(w, g, m, v):
    m = ADAM_B1 * m + (1.0 - ADAM_B1) * g
    v = ADAM_B2 * v + (1.0 - ADAM_B2) * (g * g)
    m_hat = m / (1.0 - ADAM_B1 ** ADAM_STEP)
    v_hat = v / (1.0 - ADAM_B2 ** ADAM_STEP)
    delta = -ADAM_LR * (m_hat / (jnp.sqrt(v_hat) + ADAM_EPS) + ADAM_WD * w)
    return delta, m, v


def _adamw_shard(owns, recvs, w, m, v, name, flipped=False):
    nl = w.shape[0]
    r, cdim = owns[0].shape
    tr = _row_tile(r)
    nr = r // tr

    def body(*refs):
        own_refs, recv_refs = refs[:nl], refs[nl:2 * nl]
        w_ref, m_ref, v_ref, g_out, d_out, m_out, v_out = refs[2 * nl:]
        layer = pl.program_id(0)
        g = None
        for l in range(nl):
            gl = own_refs[l][...] + recv_refs[l][0].astype(F32) + recv_refs[l][1].astype(F32) + recv_refs[l][2].astype(F32)
            g = gl if g is None else jnp.where(layer == l, gl, g)
        if flipped:
            g = g.T
        g_out[...] = g
        d_out[...], m_out[...], v_out[...] = _adamw_math(w_ref[...], g, m_ref[...], v_ref[...])

    park = lambda l: (lambda layer, i: (jnp.where(layer == l, i, jnp.where(layer < l, 0, nr - 1)), 0))
    park3 = lambda l: (lambda layer, i: (0, jnp.where(layer == l, i, jnp.where(layer < l, 0, nr - 1)), 0))
    if flipped:
        row = pl.BlockSpec((None, cdim, tr), lambda layer, i: (layer, 0, i))
    else:
        row = pl.BlockSpec((None, tr, cdim), lambda layer, i: (layer, i, 0))
    return pl.pallas_call(
        body, grid=(nl, nr), name=name,
        in_specs=[pl.BlockSpec((tr, cdim), park(l)) for l in range(nl)] + [pl.BlockSpec((3, tr, cdim), park3(l)) for l in range(nl)]
        + [row, row, row],
        out_specs=[row] * 4, out_shape=[S(w.shape, F32)] * 4,
        compiler_params=_cp("arbitrary", "arbitrary"))(*owns, *recvs, w, m, v)


def _adamw_small(galls, ws, ms, vs, name):
    n = len(galls)

    def body(*refs):
        g_refs, w_refs, m_refs, v_refs, outs = refs[:n], refs[n:2 * n], refs[2 * n:3 * n], refs[3 * n:4 * n], refs[4 * n:]
        for i in range(n):
            g = g_refs[i][0].astype(F32)
            for s in range(1, N_DEV):
                g = g + g_refs[i][s].astype(F32)
            outs[i][...] = g
            outs[n + i][...], outs[2 * n + i][...], outs[3 * n + i][...] = _adamw_math(w_refs[i][...], g, m_refs[i][...], v_refs[i][...])

    res = pl.pallas_call(body, out_shape=[S(a.shape, F32) for a in ws] * 4, name=name)(*galls, *ws, *ms, *vs)
    return [res[k * n:(k + 1) * n] for k in range(4)]


REPLICATED = ["mix_norm", "ffn_norm", "sgu_v_gain", "sgu_w_s", "sgu_b_s", "attn_q_gain", "attn_k_gain", "attn_sinks", "rel_bias",
              "ffn_conv_b"]
WEIGHTS = ["mix_norm", "ffn_norm", "sgu_w_in", "sgu_v_gain", "sgu_w_s", "sgu_b_s", "sgu_w_out", "attn_w_qkv", "attn_q_gain",
           "attn_k_gain", "attn_sinks", "attn_w_o", "rel_bias", "ffn_w_up", "ffn_conv_w", "ffn_conv_b", "ffn_w_down"]
SMALL = ["g_" + n for n in REPLICATED]
BF16_TRANSIT = {"sgu_w_s"}
SMALL_ATTN = ["g_attn_q_gain", "g_attn_k_gain", "g_attn_sinks", "g_rel_bias"]
SMALL_FFN = ["g_ffn_norm", "g_ffn_conv_b"]
SMALL_SGU = ["g_sgu_v_gain", "g_sgu_w_s", "g_sgu_b_s"]

GATHER_FIRST = ["sgu_w_in", "ffn_conv_w"]
UP0_SPLIT, UP1_SPLIT = 352, 400
PLAN = {
    "sgu_in": [("ag1", "sgu_w_out"), ("ag1", "ffn_w_up0", (0, UP0_SPLIT))],
    "sgu_gate": [("ag2", "sgu_w_out"), ("ag1", "ffn_w_up0", (UP0_SPLIT, D))],
    "sgu_out": [("ag2", "ffn_w_up0"), ("ag1", "ffn_w_down0")],
    "before_ffn0": [("ag2", "ffn_w_down0")],
    "ffn0_fwd": [("agd", "attn_w_qkv"), ("ag1", "attn_w_o"), ("ag1", "ffn_w_down1"), ("ag1", "ffn_w_up1", (0, UP1_SPLIT))],
    "attn": [("ag2", "attn_w_o"), ("ag2", "ffn_w_down1"), ("ag1", "ffn_w_up1", (UP1_SPLIT, D))],
    "attn_out": [("ag2", "ffn_w_up1")],
    "attn_bwd": [("rs1", "ffn_w_down1"), ("rs1", "ffn_w_up1"), ("rs1", "attn_w_o")],
    "ffn0_bwd1": [("rs2", "ffn_w_down1"), ("rs2", "attn_w_o"), ("rs1", "attn_w_qkv")] + [("ag1", n) for n in SMALL_ATTN],
    "ffn0_bwd2": [("rs2", "ffn_w_up1"), ("rs2", "attn_w_qkv"), ("rs1", "ffn_w_down0")] + [("ag2", n) for n in SMALL_ATTN],
    "ffn0_dw_up": [("rs2", "ffn_w_down0")],
    "sgu_dgated": [("rs1", "ffn_w_up0")] + [("agd", n) for n in SMALL_FFN],
    "sgu_gate_bwd": [("rs2", "ffn_w_up0"), ("rs1", "sgu_w_out")],
    "dw_sgu_in": [("rs2", "sgu_w_out")] + [("ag1", n) for n in SMALL_SGU],
    "after_dw": [("rs1", "sgu_w_in"), ("rs1", "ffn_conv_w")] + [("ag2", n) for n in SMALL_SGU],
    "dx_sgu_in": [("rs2", "sgu_w_in"), ("rs2", "ffn_conv_w")],
    "last": [("agd", "g_mix_norm")],
}


class _Overlap:
    def __init__(self, shard, place):
        self.shard, self.place = shard, place
        self.part, self.full = {}, {}
        self.grads, self.sib, self.own, self.recv = {}, {}, {}, {}

    def w(self, n):
        return self.full[n]

    def grad(self, n, pair):
        self.grads[n] = pair

    def small(self, g_rep):
        self.shard.update(("g_" + n, a.astype(BF16) if n in BF16_TRANSIT else a) for n, a in _views2d(g_rep).items())

    def sync(self, point):
        _exchange(self.hook(point), point)

    def first_norm(self, x, gain):
        hn, full = _allgather([self.shard[n] for n in GATHER_FIRST], x, gain, "gather_first")
        self.full.update(zip(GATHER_FIRST, full))
        return hn

    def chip_sums(self, n):
        sums, self.own[n] = _rs_partial(self.grads[n][0], self.sib.pop(n), self.place, "rs_partial_" + n)
        return sums

    def hook(self, host):
        ops = PLAN.get(host)
        if not ops:
            return None
        where = {"ag1": self.part, "ag2": self.full, "agd": self.full, "rs1": self.sib, "rs2": self.recv}
        idx = []

        def hook(results=None):
            if results is not None:
                for (kind, n, *_), i in zip(ops, idx):
                    where[kind][n] = results[i]
                return None
            comm = _Comm()
            for kind, n, *rows in ops:
                arr = {"ag1": lambda: self.shard[n], "agd": lambda: self.shard[n], "ag2": lambda: self.part.pop(n),
                       "rs1": lambda: self.grads[n][1], "rs2": lambda: self.chip_sums(n)}[kind]()
                idx.append(comm.add(kind, arr, *rows, into=self.part.pop(n) if rows and rows[0][0] else None))
            return comm

        return hook


TRANSPOSED = {"attn_w_qkv"}
PHYSICAL_T = {"ffn_w_up"}
SHARDED = {
    "sgu_w_in": ["sgu_w_in"], "sgu_w_out": ["sgu_w_out"], "attn_w_qkv": ["attn_w_qkv"], "attn_w_o": ["attn_w_o"],
    "ffn_w_up": ["ffn_w_up0", "ffn_w_up1"], "ffn_w_down": ["ffn_w_down0", "ffn_w_down1"], "ffn_conv_w": ["ffn_conv_w"],
}


def _send_views(w):
    out = {"ffn_conv_w": w["ffn_conv_w"].reshape(6, -1)}
    for name, parts in SHARDED.items():
        if name != "ffn_conv_w":
            out.update((p, (w[name][l].T if name in TRANSPOSED else w[name][l]).astype(BF16)) for l, p in enumerate(parts))
    return out


def _views2d(d):
    return {n: d[n].reshape(-1, d[n].shape[-1]) for n in REPLICATED if n in d}


def kernel(x, mix_norm, ffn_norm, sgu_w_in, sgu_v_gain, sgu_w_s, sgu_b_s, sgu_w_out, attn_w_qkv, attn_q_gain, attn_k_gain, attn_sinks, attn_w_o, rel_bias, ffn_w_up, ffn_conv_w, ffn_conv_b, ffn_w_down, loss_target, m_mix_norm, m_ffn_norm, m_sgu_w_in, m_sgu_v_gain, m_sgu_w_s, m_sgu_b_s, m_sgu_w_out, m_attn_w_qkv, m_attn_q_gain, m_attn_k_gain, m_attn_sinks, m_attn_w_o, m_rel_bias, m_ffn_w_up, m_ffn_conv_w, m_ffn_conv_b, m_ffn_w_down, v_mix_norm, v_ffn_norm, v_sgu_w_in, v_sgu_v_gain, v_sgu_w_s, v_sgu_b_s, v_sgu_w_out, v_attn_w_qkv, v_attn_q_gain, v_attn_k_gain, v_attn_sinks, v_attn_w_o, v_rel_bias, v_ffn_w_up, v_ffn_conv_w, v_ffn_conv_b, v_ffn_w_down):
    w = dict(zip(WEIGHTS, (mix_norm, ffn_norm, sgu_w_in, sgu_v_gain, sgu_w_s, sgu_b_s, sgu_w_out, attn_w_qkv, attn_q_gain, attn_k_gain,
                           attn_sinks, attn_w_o, rel_bias, ffn_w_up, ffn_conv_w, ffn_conv_b, ffn_w_down)))
    m = dict(zip(WEIGHTS, (m_mix_norm, m_ffn_norm, m_sgu_w_in, m_sgu_v_gain, m_sgu_w_s, m_sgu_b_s, m_sgu_w_out, m_attn_w_qkv, m_attn_q_gain,
                           m_attn_k_gain, m_attn_sinks, m_attn_w_o, m_rel_bias, m_ffn_w_up, m_ffn_conv_w, m_ffn_conv_b, m_ffn_w_down)))
    v = dict(zip(WEIGHTS, (v_mix_norm, v_ffn_norm, v_sgu_w_in, v_sgu_v_gain, v_sgu_w_s, v_sgu_b_s, v_sgu_w_out, v_attn_w_qkv, v_attn_q_gain,
                           v_attn_k_gain, v_attn_sinks, v_attn_w_o, v_rel_bias, v_ffn_w_up, v_ffn_conv_w, v_ffn_conv_b, v_ffn_w_down)))
    rep = {n: w[n] for n in REPLICATED}

    xi, yi, ci = lax.axis_index("x"), lax.axis_index("y"), lax.axis_index("c")
    place = jnp.stack([ci, 2 * xi + yi]).astype(jnp.int32)
    sch = _Overlap(_send_views(w), place)

    loss, grad_x, g_rep = _local_step(x[0], loss_target[0], rep, sch)
    loss = lax.psum(loss, ("x", "y", "c"))
    sch.sync("last")

    out = [{}, {}, {}, {}]
    for name, parts in SHARDED.items():
        flip = (lambda a: jnp.swapaxes(a, -1, -2)) if name in TRANSPOSED | PHYSICAL_T else (lambda a: a)
        shape = flip(w[name]).shape
        as3d = lambda a: flip(a).reshape(len(parts), -1, shape[-1])
        res = _adamw_shard([sch.own[p] for p in parts], [sch.recv[p] for p in parts], as3d(w[name]), as3d(m[name]), as3d(v[name]),
                           "adamw_" + name, flipped=name in PHYSICAL_T)
        for o, r in zip(out, res):
            o[name] = flip(r.reshape(shape))
    small = _adamw_small([sch.full[n] for n in SMALL], *[list(_views2d(d).values()) for d in (rep, m, v)], "adamw_small")
    for o, res in zip(out, small):
        o.update((n, r.reshape(w[n].shape)) for n, r in zip(REPLICATED, res))

    return (loss, grad_x[None], *[out[0][n] for n in WEIGHTS], *[out[1][n] for n in WEIGHTS],
            *[out[2][n] for n in WEIGHTS], *[out[3][n] for n in WEIGHTS])
```

```python
import functools
import math

import numpy as np
import jax
import jax.numpy as jnp
from jax import lax
from jax.experimental import pallas as pl
from jax.experimental.pallas import tpu as pltpu

F32 = jnp.float32
BF16 = jnp.bfloat16
DH = jnp.bfloat16
S = jax.ShapeDtypeStruct

D = 1024
CHUNK = 128
SGU_W = 2048
SGU_G = 16
HD = 64
NH = 16
NKV = 4
KVG = 4
D_FF = 2816
REL_BUCKETS = 32
REL_MAX_DIST = 128
EPS = 1e-6
N_DEV = 8
MESH = pl.DeviceIdType.MESH

ADAM_LR = 0.001
ADAM_B1 = 0.9
ADAM_B2 = 0.999
ADAM_EPS = 1e-08
ADAM_WD = 0.01
ADAM_STEP = 10

ROW_TILE = 512
HALO = 8
FFN_ROWS = 256


def _tm(t):
    return min(ROW_TILE, t)


def _cp(*sem):
    return pltpu.CompilerParams(dimension_semantics=sem)


ANY = pl.BlockSpec(memory_space=pl.ANY)


def _place():
    x, y, c = lax.axis_index("x"), lax.axis_index("y"), lax.axis_index("c")
    return x, y, c, [(1 - x, y), (x, 1 - y), (1 - x, 1 - y)]


class _Comm:
    SEMS = {"ag1": 5, "ag2": 3, "rs1": 4, "rs2": 3, "agd": 8}

    def __init__(self):
        self.inputs, self.out_shapes, self.aliases, self.ops, self.n_sems = [], [], {}, [], 0

    def add(self, kind, arr, rows=None, into=None):
        lead = {"ag1": N_DEV, "agd": N_DEV, "ag2": None, "rs1": 4, "rs2": 3}[kind]
        shape = arr.shape if lead is None else (lead,) + arr.shape[(0 if kind in ("ag1", "agd") else 1):]
        if kind == "ag2":
            self.aliases[len(self.inputs)] = len(self.out_shapes)
        self.ops.append((kind, len(self.inputs), len(self.out_shapes), self.n_sems, rows))
        self.inputs.append(arr)
        if into is not None:
            self.aliases[len(self.inputs)] = len(self.out_shapes)
            self.inputs.append(into)
        self.out_shapes.append(S(shape, arr.dtype))
        self.n_sems += self.SEMS[kind]
        return len(self.out_shapes) - 1

    def _copies(self, ins, outs, send, recv):
        x, y, c, chips = _place()
        me, sibling = (x, y, c), (x, y, 1 - c)
        slot = lambda px, py, pc: 4 * px + 2 * py + pc
        sends, recvs, local = [], [], []

        def rc(src, dst, k, to):
            return lambda: pltpu.make_async_remote_copy(src_ref=src(), dst_ref=dst(), send_sem=send.at[k], recv_sem=recv.at[k],
                                                        device_id=to, device_id_type=MESH)

        for kind, ii, oi, b, rows in self.ops:
            src, dst = ins[ii], outs[oi]
            at = lambda ref, i: (lambda: ref.at[i])
            if kind == "ag1":
                part = slice(None) if rows is None else pl.ds(rows[0], rows[1] - rows[0])
                to = lambda i, d=dst, p=part: (lambda: d.at[i, p])
                whole, mine = (lambda s=src, p=part: s.at[p]), to(slot(*me))
                sends.append(rc(whole, mine, b, sibling))
                recvs.append(rc(whole, to(slot(x, y, 1 - c)), b, me))
                for j, chip in enumerate(chips):
                    sends.append(rc(whole, mine, b + 1 + j, (*chip, c)))
                    recvs.append(rc(whole, to(slot(*chip, c)), b + 1 + j, me))
                local.append(lambda s=whole, m=mine, k=b + 4: pltpu.make_async_copy(s(), m(), send.at[k]))
            elif kind == "ag2":
                for j, chip in enumerate(chips):
                    sends.append(rc(at(dst, slot(*chip, c)), at(dst, slot(*chip, c)), b + j, sibling))
                    recvs.append(rc(at(dst, slot(*chip, 1 - c)), at(dst, slot(*chip, 1 - c)), b + j, me))
            elif kind == "agd":
                whole, mine = (lambda s=src: s), at(dst, slot(*me))
                flip = lambda v, bit: 1 - v if bit else v
                for k in range(1, N_DEV):
                    peer = (flip(x, k >> 2), flip(y, (k >> 1) & 1), flip(c, k & 1))
                    sends.append(rc(whole, mine, b + k - 1, peer))
                    recvs.append(rc(whole, at(dst, slot(*peer)), b + k - 1, me))
                local.append(lambda s=src, m=mine, k=b + 7: pltpu.make_async_copy(s, m(), send.at[k]))
            elif kind == "rs1":
                for k in range(4):
                    sends.append(rc(at(src, 2 * k + (1 - c)), at(dst, k), b + k, sibling))
                    recvs.append(rc(at(src, 2 * k + c), at(dst, k), b + k, me))
            else:
                for j, (px, py) in enumerate(chips):
                    sends.append(rc(at(src, 2 * px + py), at(dst, j), b + j, (px, py, c)))
                    recvs.append(rc(at(src, 2 * px + py), at(dst, j), b + j, me))
        return sends, recvs, local

    def start(self, ins, outs, send, recv):
        sends, _, local = self._copies(ins, outs, send, recv)
        for make in local + sends:
            make().start()

    def finish(self, ins, outs, send, recv):
        sends, recvs, local = self._copies(ins, outs, send, recv)
        for make in recvs:
            make().wait_recv()
        for make in sends:
            make().wait_send()
        for make in local:
            make().wait()


def _run(body, args, hook, *, grid, in_specs, out_specs, out_shape, name, semantics, scratch_shapes=(), aliases=None):
    comm = hook() if hook is not None else None
    aliases = dict(aliases or {})
    if comm is None:
        return pl.pallas_call(body, grid=grid, in_specs=in_specs, out_specs=out_specs, out_shape=out_shape, name=name,
                              scratch_shapes=list(scratch_shapes), input_output_aliases=aliases,
                              compiler_params=_cp(*semantics))(*args)
    single = not isinstance(out_shape, (list, tuple))
    out_shapes = [out_shape] if single else list(out_shape)
    out_specs_l = [out_specs] if single else list(out_specs)
    n_in, n_out, n_scr, ci, co = len(args), len(out_shapes), len(scratch_shapes), len(comm.inputs), len(comm.out_shapes)

    def wrapped(*refs):
        ins, cins = refs[:n_in], refs[n_in:n_in + ci]
        outs, couts = refs[n_in + ci:n_in + ci + n_out], refs[n_in + ci + n_out:n_in + ci + n_out + co]
        scr = refs[n_in + ci + n_out + co:n_in + ci + n_out + co + n_scr]
        send, recv = refs[-2:]
        first = functools.reduce(lambda a, b: a & b, [pl.program_id(a) == 0 for a in range(len(grid))])
        last = functools.reduce(lambda a, b: a & b, [pl.program_id(a) == g - 1 for a, g in enumerate(grid)])

        @pl.when(first)
        def _():
            comm.start(cins, couts, send, recv)

        body(*ins, *outs, *scr)

        @pl.when(last)
        def _():
            comm.finish(cins, couts, send, recv)

    res = pl.pallas_call(
        wrapped, grid=grid, in_specs=list(in_specs) + [ANY] * ci, out_specs=out_specs_l + [ANY] * co,
        out_shape=out_shapes + comm.out_shapes, name=name,
        scratch_shapes=list(scratch_shapes) + [pltpu.SemaphoreType.DMA((comm.n_sems,)), pltpu.SemaphoreType.DMA((comm.n_sems,))],
        input_output_aliases={**aliases, **{n_in + k: n_out + v for k, v in comm.aliases.items()}},
        compiler_params=pltpu.CompilerParams(dimension_semantics=("arbitrary",) * len(grid), has_side_effects=True))(*args, *comm.inputs)
    hook(res[n_out:])
    return res[0] if single else list(res[:n_out])


def _dot(a, b):
    return jnp.dot(a, b, preferred_element_type=F32)


def _dot_nt(a, b):
    return lax.dot_general(a, b, (((1,), (1,)), ((), ())), preferred_element_type=F32)


def _dot_tn(a, b):
    return lax.dot_general(a, b, (((0,), (0,)), ((), ())), preferred_element_type=F32)


def _gelu(x):
    return 0.5 * x * (1.0 + lax.erf(x * (2.0 ** -0.5)))


def _gelu_and_grad(x):
    cdf = 0.5 * (1.0 + lax.erf(x * (2.0 ** -0.5)))
    return x * cdf, cdf + x * jnp.exp(-0.5 * x * x) * (1.0 / math.sqrt(2.0 * math.pi))


def _sigmoid(x):
    return 1.0 / (1.0 + jnp.exp(-x))


def _rstd(x):
    return lax.rsqrt(jnp.mean(x * x, axis=-1, keepdims=True) + EPS)


def _rel_tables():
    q = np.arange(CHUNK)[:, None] + CHUNK
    k = np.arange(2 * CHUNK)[None, :]
    dist = q - k
    n = np.maximum(dist, 0)
    max_exact = REL_BUCKETS // 2
    large = max_exact + (np.log(np.maximum(n, 1).astype(np.float32) / max_exact)
                         / math.log(REL_MAX_DIST / max_exact) * (REL_BUCKETS - max_exact)).astype(np.int32)
    large = np.minimum(large, REL_BUCKETS - 1)
    return np.where(n < max_exact, n, large).astype(np.int32)


def _rmsnorm(x, gain, name):
    t = x.shape[0]
    tm = _tm(t)

    def body(x_ref, g_ref, o_ref):
        xv = x_ref[...]
        o_ref[...] = (xv * _rstd(xv) * g_ref[...]).astype(BF16)

    return pl.pallas_call(
        body, grid=(t // tm,), name=name,
        in_specs=[pl.BlockSpec((tm, D), lambda i: (i, 0)), pl.BlockSpec((1, D), lambda i: (0, 0))],
        out_specs=pl.BlockSpec((tm, D), lambda i: (i, 0)),
        out_shape=S((t, D), BF16), compiler_params=_cp("parallel"))(x, gain)


def _resident(shape):
    zeros = (0,) * len(shape)
    return pl.BlockSpec(shape, lambda *_: zeros, pipeline_mode=pl.Buffered(1))


def _mm_slot(hn, wg, out_dtype, name, hook=None):
    t, k = hn.shape
    ns, _, n = wg.shape
    tm = _tm(t)

    def body(a_ref, w_ref, o_ref):
        a = a_ref[...]
        for s in range(ns):
            o_ref[s] = _dot(a, w_ref[s]).astype(out_dtype)

    return _run(
        body, [hn, wg], hook, grid=(t // tm,), name=name, semantics=("parallel",),
        in_specs=[pl.BlockSpec((tm, k), lambda i: (i, 0)), _resident(wg.shape)],
        out_specs=pl.BlockSpec((ns, tm, n), lambda i: (0, i, 0)), out_shape=S((ns, t, n), out_dtype))


def _mm_t(hn, wt, name, hook=None):
    t, k = hn.shape
    ns, n, _ = wt.shape
    tm = _tm(t)

    def body(a_ref, w_ref, o_ref):
        a = a_ref[...]
        for s in range(ns):
            o_ref[s * n:(s + 1) * n, :] = _dot_nt(w_ref[s], a)

    return _run(
        body, [hn, wt], hook, grid=(t // tm,), name=name, semantics=("parallel",),
        in_specs=[pl.BlockSpec((tm, k), lambda i: (i, 0)), _resident(wt.shape)],
        out_specs=pl.BlockSpec((ns * n, tm), lambda i: (0, i)), out_shape=S((ns * n, t), F32))


def _conv3(a, prev, cw, cb, tm):
    ext = jnp.concatenate([prev, a], axis=0)
    return cw[2:3] * a + cw[1:2] * ext[HALO - 1:HALO - 1 + tm] + cw[0:1] * ext[HALO - 2:HALO - 2 + tm] + cb


def _ffn_fwd(hn, h, wup, wdown, cw, cb, extra, mode, name, hook=None):
    t, k = hn.shape
    n = wup.shape[-1]
    nh = wup.shape[0] // 2
    tm = min(FFN_ROWS, t)
    ni = t // tm

    def body(a_ref, h_ref, wu_ref, wd_ref, cw_ref, cb_ref, e_ref, as_ref, cs_ref, o1_ref, o2_ref, carry):
        i = pl.program_id(0)

        @pl.when(i == 0)
        def _():
            carry[...] = jnp.zeros_like(carry)

        a = a_ref[...]
        acc = h_ref[...]
        nxt = (_dot(a, wu_ref[0]), _dot(a, wu_ref[nh]))
        for j in range(nh):
            ag, av = nxt
            if j + 1 < nh:
                nxt = (_dot(a, wu_ref[j + 1]), _dot(a, wu_ref[nh + j + 1]))
            as_ref[j] = ag.astype(BF16)
            as_ref[nh + j] = av.astype(BF16)
            cg = _conv3(ag, carry[j], cw_ref[j], cb_ref[j], tm)
            cv = _conv3(av, carry[nh + j], cw_ref[nh + j], cb_ref[nh + j], tm)
            carry[j] = ag[tm - HALO:]
            carry[nh + j] = av[tm - HALO:]
            cs_ref[j] = cg.astype(BF16)
            cs_ref[nh + j] = cv.astype(BF16)
            act = (cg * _sigmoid(cg) * cv).astype(BF16)
            acc = acc + _dot(act, wd_ref[j * n:(j + 1) * n, :])
        if mode == "norm":
            o1_ref[...] = acc
            o2_ref[...] = (acc * _rstd(acc) * e_ref[...]).astype(BF16)
        else:
            err = acc - e_ref[...]
            o1_ref[...] = (err * (1.0 / D)).astype(o1_ref.dtype)
            o2_ref[...] = jnp.full(o2_ref.shape, jnp.sum(err * err), F32)

    row = pl.BlockSpec((tm, D), lambda i: (i, 0))
    if mode == "norm":
        e_spec, o2_spec, o2_shape = pl.BlockSpec((1, D), lambda i: (0, 0)), row, S((t, D), BF16)
    else:
        e_spec, o2_spec, o2_shape = row, pl.BlockSpec((None, 8, 128), lambda i: (i, 0, 0)), S((ni, 8, 128), F32)
    aspec = pl.BlockSpec((2 * nh, tm, n), lambda i: (0, i, 0))
    return _run(
        body, [hn, h, wup, wdown, cw, cb, extra], hook, grid=(ni,), name=name, semantics=("arbitrary",),
        in_specs=[pl.BlockSpec((tm, k), lambda i: (i, 0)), row, _resident(wup.shape), _resident(wdown.shape),
                  _resident(cw.shape), _resident(cb.shape), e_spec],
        out_specs=[aspec, aspec, row, o2_spec],
        out_shape=[S((2 * nh, t, n), BF16), S((2 * nh, t, n), BF16), S((t, D), F32 if mode == "norm" else DH), o2_shape],
        scratch_shapes=[pltpu.VMEM((2 * nh, HALO, n), F32)])


def _tril_mask():
    r = lax.broadcasted_iota(jnp.int32, (CHUNK, CHUNK), 0)
    c = lax.broadcasted_iota(jnp.int32, (CHUNK, CHUNK), 1)
    return r >= c


def _sgu_gate_fwd(a_s, vgain, ws, bst, name, hook=None):
    t = a_s.shape[1]
    sw = a_s.shape[2]
    gps = sw // CHUNK

    def body(a_ref, vg_ref, ws_ref, b_ref, o_ref):
        v = _gelu(jnp.concatenate([a_ref[4 + s].astype(F32) for s in range(4)], axis=1))
        vn = (v * _rstd(v) * vg_ref[...]).astype(BF16)
        tri = _tril_mask()
        for g in range(SGU_G):
            w = jnp.where(tri, ws_ref[g], 0.0).astype(BF16)
            sg = _dot(w, vn[:, g * CHUNK:(g + 1) * CHUNK]) + b_ref[:, g:g + 1]
            lo = (g % gps) * CHUNK
            u = _gelu(a_ref[g // gps, :, lo:lo + CHUNK].astype(F32))
            o_ref[g // gps, :, lo:lo + CHUNK] = (u * sg).astype(BF16)

    return _run(
        body, [a_s, vgain, ws, bst], hook, grid=(t // CHUNK,), name=name, semantics=("parallel",),
        in_specs=[pl.BlockSpec((8, CHUNK, sw), lambda n: (0, n, 0)), pl.BlockSpec((1, SGU_W), lambda n: (0, 0)),
                  pl.BlockSpec((SGU_G, CHUNK, CHUNK), lambda n: (0, 0, 0)), pl.BlockSpec((CHUNK, SGU_G), lambda n: (0, 0))],
        out_specs=pl.BlockSpec((4, CHUNK, sw), lambda n: (0, n, 0)), out_shape=S((4, t, sw), BF16))


def _resid_mm(a_s, w, resid, extra, mode, name, hook=None, fm=False):
    nk, t, kc = (1, a_s.shape[1], a_s.shape[0]) if fm else a_s.shape
    tm = _tm(t)
    ni = t // tm

    def body(a_ref, w_ref, r_ref, e_ref, o1_ref, o2_ref):
        h = r_ref[...]
        if fm:
            h = h + _dot_tn(a_ref[...], w_ref[...])
        for j in range(0 if fm else nk):
            h = h + _dot(a_ref[j], w_ref[j * kc:(j + 1) * kc, :])
        if mode == "norm":
            o1_ref[...] = h
            o2_ref[...] = (h * _rstd(h) * e_ref[...]).astype(BF16)
        else:
            err = h - e_ref[...]
            o1_ref[...] = (err * (1.0 / D)).astype(o1_ref.dtype)
            o2_ref[...] = jnp.full(o2_ref.shape, jnp.sum(err * err), F32)

    row = pl.BlockSpec((tm, D), lambda i: (i, 0))
    if mode == "norm":
        e_spec, o2_spec, o2_shape = pl.BlockSpec((1, D), lambda i: (0, 0)), row, S((t, D), BF16)
    else:
        e_spec, o2_spec, o2_shape = row, pl.BlockSpec((None, 8, 128), lambda i: (i, 0, 0)), S((ni, 8, 128), F32)
    return _run(
        body, [a_s, w, resid, extra], hook, grid=(ni,), name=name, semantics=("parallel",),
        in_specs=[pl.BlockSpec((kc, tm), lambda i: (0, i)) if fm else pl.BlockSpec((nk, tm, kc), lambda i: (0, i, 0)),
                  _resident(w.shape), row, e_spec],
        out_specs=[row, o2_spec], out_shape=[S((t, D), F32 if mode == "norm" else DH), o2_shape])


def _relbias_fwd(rel_bias_t, bucket_row, name):
    nb = bucket_row.shape[1]

    def body(rb_ref, bk_ref, o_ref):
        onehot = (lax.broadcasted_iota(jnp.int32, (REL_BUCKETS, nb), 0) == bk_ref[...]).astype(F32)
        o_ref[...] = jnp.dot(rb_ref[...], onehot, precision=lax.Precision.HIGHEST, preferred_element_type=F32)

    return pl.pallas_call(body, out_shape=S((NH, nb), F32), name=name)(rel_bias_t, bucket_row)


def _relbias_bwd(dbias, bucket_row, name):
    nb = bucket_row.shape[1]

    def body(db_ref, bk_ref, o_ref):
        onehot = (lax.broadcasted_iota(jnp.int32, (REL_BUCKETS, nb), 0) == bk_ref[...]).astype(F32)
        o_ref[...] = lax.dot_general(db_ref[...], onehot, (((1,), (1,)), ((), ())),
                                     precision=lax.Precision.HIGHEST, preferred_element_type=F32)

    return pl.pallas_call(body, out_shape=S((NH, REL_BUCKETS), F32), name=name)(dbias, bucket_row)


QKV = D + 2 * NKV * HD
KV0 = D


def _rstd_rows(x):
    return lax.rsqrt(jnp.mean(x * x, axis=0, keepdims=True) + EPS)


def _attn_valid(n):
    kj = lax.broadcasted_iota(jnp.int32, (2 * CHUNK, CHUNK), 0)
    qi = lax.broadcasted_iota(jnp.int32, (2 * CHUNK, CHUNK), 1)
    dist = qi + CHUNK - kj
    return (dist >= 0) & (dist < CHUNK) & ((n > 0) | (kj >= CHUNK))


def _attn_band(cur_ref, prev_ref, row):
    return jnp.concatenate([prev_ref[row - KV0:row - KV0 + HD, :], cur_ref[row:row + HD, :]], axis=1)


def _attn_probs(kn_tok, qn, bias, valid, sink):
    s = _dot(kn_tok, qn) * (HD ** -0.5) + bias
    s = jnp.where(valid, s, -jnp.inf)
    m = jnp.maximum(jnp.max(s, axis=0, keepdims=True), sink)
    p = jnp.exp(s - m)
    psink = jnp.exp(sink - m)
    inv = 1.0 / (jnp.sum(p, axis=0, keepdims=True) + psink)
    return p * inv, psink * inv


def _attn_fwd(qkv_t, qg, kg, sinks, bias, name, hook=None):
    t = qkv_t.shape[1]

    def body(cur_ref, prev_ref, qg_ref, kg_ref, sink_ref, bias_ref, o_ref):
        n = pl.program_id(0)
        valid = _attn_valid(n)
        ks = [_attn_band(cur_ref, prev_ref, KV0 + HD * h) for h in range(NKV)]
        kn_toks = [(k * _rstd_rows(k) * kg_ref[...]).astype(BF16).T for k in ks]
        vbs = [_attn_band(cur_ref, prev_ref, KV0 + HD * (NKV + h)).astype(BF16) for h in range(NKV)]
        qs = [cur_ref[HD * hq:HD * (hq + 1), :] for hq in range(NH)]
        qns = [(q * _rstd_rows(q) * qg_ref[...]).astype(BF16) for q in qs]
        ps = [_attn_probs(kn_toks[hq // KVG], qns[hq], bias_ref[hq], valid, sink_ref[hq])[0] for hq in range(NH)]
        for hq in range(NH):
            o_ref[HD * hq:HD * (hq + 1), :] = _dot(vbs[hq // KVG], ps[hq].astype(BF16)).astype(BF16)

    col = pl.BlockSpec((HD, 1), lambda n: (0, 0))
    return _run(
        body, [qkv_t, qkv_t, qg, kg, sinks, bias], hook, grid=(t // CHUNK,), name=name, semantics=("parallel",),
        in_specs=[pl.BlockSpec((QKV, CHUNK), lambda n: (0, n)),
                  pl.BlockSpec((QKV - KV0, CHUNK), lambda n: (KV0 // (QKV - KV0), jnp.maximum(n - 1, 0))),
                  col, col, pl.BlockSpec(memory_space=pltpu.SMEM), pl.BlockSpec((NH, 2 * CHUNK, CHUNK), lambda n: (0, 0, 0))],
        out_specs=pl.BlockSpec((D, CHUNK), lambda n: (0, n)), out_shape=S((D, t), BF16))


def _dx_rows(dh, w, kc, out_dtype, name, hook=None):
    t = dh.shape[0]
    nk = w.shape[0] // kc
    tm = _tm(t)

    def body(d_ref, w_ref, o_ref):
        dhb = d_ref[...].astype(BF16)
        for j in range(nk):
            o_ref[j] = _dot_nt(dhb, w_ref[j * kc:(j + 1) * kc, :]).astype(out_dtype)

    return _run(
        body, [dh, w], hook, grid=(t // tm,), name=name, semantics=("parallel",),
        in_specs=[pl.BlockSpec((tm, D), lambda i: (i, 0)), _resident(w.shape)],
        out_specs=pl.BlockSpec((nk, tm, kc), lambda i: (0, i, 0)), out_shape=S((nk, t, kc), out_dtype))


def _dx_rows_t(dh, w, name, hook=None):
    t = dh.shape[0]
    k = w.shape[0]
    tm = _tm(t)

    def body(d_ref, w_ref, o_ref):
        o_ref[...] = _dot_nt(w_ref[...], d_ref[...].astype(BF16)).astype(BF16)

    return _run(
        body, [dh, w], hook, grid=(t // tm,), name=name, semantics=("parallel",),
        in_specs=[pl.BlockSpec((tm, D), lambda i: (i, 0)), _resident(w.shape)],
        out_specs=pl.BlockSpec((k, tm), lambda i: (0, i)), out_shape=S((k, t), BF16))


def _ffn_bwd1(dh, c, wdown, name, hook=None):
    ns, t, n = c.shape
    nh = ns // 2
    tm = min(FFN_ROWS, t)
    ni = t // tm

    def body(d_ref, c_ref, wd_ref, dc_ref, dw_hbm, dwb_hbm, acc, stage):
        i = pl.program_id(0)

        @pl.when(i == 0)
        def _():
            acc[...] = jnp.zeros_like(acc)

        dhb = d_ref[...].astype(BF16)
        for j in range(nh):
            dact = _dot_nt(dhb, wd_ref[j * n:(j + 1) * n, :])
            cg = c_ref[j].astype(F32)
            cv = c_ref[nh + j].astype(F32)
            sg = _sigmoid(cg)
            gs = cg * sg
            acc[j * n:(j + 1) * n, :] += _dot_tn((gs * cv).astype(BF16), dhb)
            dc_ref[j] = (dact * cv * (sg + gs * (1.0 - sg))).astype(BF16)
            dc_ref[nh + j] = (dact * gs).astype(BF16)

        @pl.when(i == ni - 1)
        def _():
            pltpu.sync_copy(acc, dw_hbm)
            for j in range(nh):
                stage[...] = acc[j * n:(j + 1) * n, :].astype(BF16)
                pltpu.sync_copy(stage, dwb_hbm.at[pl.ds(j * n, n), :])

    slab = pl.BlockSpec((ns, tm, n), lambda i: (0, i, 0))
    return _run(
        body, [dh, c, wdown], hook, grid=(ni,), name=name, semantics=("arbitrary",),
        in_specs=[pl.BlockSpec((tm, D), lambda i: (i, 0)), slab, _resident(wdown.shape)],
        out_specs=[slab, ANY, ANY], out_shape=[S((ns, t, n), BF16), S(wdown.shape, F32), S(wdown.shape, BF16)],
        scratch_shapes=[pltpu.VMEM(wdown.shape, F32), pltpu.VMEM((n, D), BF16)])


def _ffn_bwd2(dc, a, wup, cw, h, gain, dh_in, name, hook=None):
    ns, t, n = dc.shape
    tm = min(FFN_ROWS, t)
    ni = t // tm

    def body(dc_ref, a_ref, wu_ref, cw_ref, h_ref, g_ref, di_ref, da_ref, o_ref, dg_ref, dcw_ref, dcb_ref, carry, keep):
        i = pl.program_id(0)

        @pl.when(i == 0)
        def _():
            carry[...] = jnp.zeros_like(carry)
            dg_ref[...] = jnp.zeros_like(dg_ref)
            dcw_ref[...] = jnp.zeros_like(dcw_ref)
            dcb_ref[...] = jnp.zeros_like(dcb_ref)

        rsum = lambda v: jnp.sum(v, axis=0, keepdims=True)
        acc = jnp.zeros((tm, D), F32)
        for s in range(ns):
            x = dc_ref[s].astype(F32)
            ext = jnp.concatenate([x, carry[s]], axis=0)
            keep[0] = ext[1:1 + tm]
            keep[1] = ext[2:2 + tm]
            x1, x2 = keep[0], keep[1]
            cwv = cw_ref[s]
            da = (cwv[2:3] * x + cwv[1:2] * x1 + cwv[0:1] * x2).astype(BF16)
            carry[s] = x[:HALO]
            da_ref[s] = da
            acc = acc + _dot_nt(da, wu_ref[s])
            av = a_ref[s].astype(F32)
            dcw_ref[s] += jnp.concatenate([rsum(x2 * av), rsum(x1 * av), rsum(x * av)], axis=0)
            dcb_ref[s] += rsum(x)
        hv = h_ref[...]
        r = _rstd(hv)
        gg = acc * g_ref[...]
        dh_new = di_ref[...].astype(F32) + r * gg - hv * (r * r * r * jnp.mean(gg * hv, axis=-1, keepdims=True))
        o_ref[...] = dh_new.astype(o_ref.dtype)
        dg_ref[...] += jnp.sum(acc * hv * r, axis=0, keepdims=True)

    slab = pl.BlockSpec((ns, tm, n), lambda i: (0, ni - 1 - i, 0))
    row = pl.BlockSpec((tm, D), lambda i: (ni - 1 - i, 0))
    vec = pl.BlockSpec((1, D), lambda i: (0, 0))
    whole = lambda shape: pl.BlockSpec(shape, lambda i: (0,) * len(shape))
    return _run(
        body, [dc, a, wup, cw, h, gain, dh_in], hook, grid=(ni,), name=name, semantics=("arbitrary",),
        in_specs=[slab, slab, _resident(wup.shape), _resident(cw.shape), row, vec, row],
        out_specs=[slab, row, vec, whole((ns, 3, n)), whole((ns, 1, n))],
        out_shape=[S((ns, t, n), BF16), S((t, D), DH), S((1, D), F32), S((ns, 3, n), F32), S((ns, 1, n), F32)],
        scratch_shapes=[pltpu.VMEM((ns, HALO, n), F32), pltpu.VMEM((2, tm, n), F32)])


def _dw_slot(hn, dy_s, name, hook=None):
    t, k = hn.shape
    ns, _, n = dy_s.shape
    tm = _tm(t)

    def body(a_ref, b_ref, o_ref, ob_ref, at_ref):
        @pl.when(pl.program_id(0) == 0)
        def _():
            for i in range(t // tm):
                at_ref[:, i * tm:(i + 1) * tm] = a_ref[i * tm:(i + 1) * tm, :].T

        acc = _dot(at_ref[...], b_ref[...])
        o_ref[...] = acc
        ob_ref[...] = acc.astype(BF16)

    ospec = pl.BlockSpec((None, k, n), lambda j: (j, 0, 0))
    return _run(
        body, [hn, dy_s], hook, grid=(ns,), name=name, semantics=("arbitrary",),
        in_specs=[_resident(hn.shape), pl.BlockSpec((None, t, n), lambda j: (j, 0, 0))],
        out_specs=[ospec, ospec], out_shape=[S((ns, k, n), F32), S((ns, k, n), BF16)],
        scratch_shapes=[pltpu.VMEM((k, t), BF16)])


def _dw_rows(a_s, dh, name, hook=None, fm=False):
    nk, t, kc = (1, a_s.shape[1], a_s.shape[0]) if fm else a_s.shape
    tm = _tm(t)
    ni = t // tm

    def body(a_ref, d_ref, o_ref, ob_ref):
        i = pl.program_id(0)
        dhb = d_ref[...].astype(BF16)

        @pl.when(i == 0)
        def _():
            o_ref[...] = jnp.zeros_like(o_ref)

        if fm:
            o_ref[...] += _dot(a_ref[...], dhb)
        for j in range(0 if fm else nk):
            o_ref[j * kc:(j + 1) * kc, :] += _dot_tn(a_ref[j], dhb)

        @pl.when(i == ni - 1)
        def _():
            ob_ref[...] = o_ref[...].astype(BF16)

    ospec = pl.BlockSpec((nk * kc, D), lambda i: (0, 0))
    return _run(
        body, [a_s, dh], hook, grid=(ni,), name=name, semantics=("arbitrary",),
        in_specs=[pl.BlockSpec((kc, tm), lambda i: (0, i)) if fm else pl.BlockSpec((nk, tm, kc), lambda i: (0, i, 0)),
                  pl.BlockSpec((tm, D), lambda i: (i, 0))],
        out_specs=[ospec, ospec], out_shape=[S((nk * kc, D), F32), S((nk * kc, D), BF16)])


def _dx_slot_normbwd(dy_s, wg, h, gain, dh_in, name, hook=None, fm=False, out_dtype=F32):
    ns, t, n = (1, dy_s.shape[1], dy_s.shape[0]) if fm else dy_s.shape
    tm = _tm(t)

    def body(dy_ref, w_ref, h_ref, g_ref, di_ref, o_ref, dg_ref):
        i = pl.program_id(0)

        @pl.when(i == 0)
        def _():
            dg_ref[...] = jnp.zeros_like(dg_ref)

        g = _dot_tn(dy_ref[...], w_ref[...]) if fm else _dot_nt(dy_ref[0], w_ref[0])
        for s in range(1, ns):
            g = g + _dot_nt(dy_ref[s], w_ref[s])
        hv = h_ref[...]
        r = _rstd(hv)
        gg = g * g_ref[...]
        dh_new = di_ref[...].astype(F32) + r * gg - hv * (r * r * r * jnp.mean(gg * hv, axis=-1, keepdims=True))
        o_ref[...] = dh_new.astype(o_ref.dtype)
        dg_ref[...] += jnp.sum(g * hv * r, axis=0, keepdims=True)

    row = pl.BlockSpec((tm, D), lambda i: (i, 0))
    vec = pl.BlockSpec((1, D), lambda i: (0, 0))
    return _run(
        body, [dy_s, wg, h, gain, dh_in], hook, grid=(t // tm,), name=name, semantics=("arbitrary",),
        in_specs=[pl.BlockSpec((n, tm), lambda i: (0, i)) if fm else pl.BlockSpec((ns, tm, n), lambda i: (0, i, 0)),
                  _resident(wg.shape), row, vec, row],
        out_specs=[row, vec], out_shape=[S((t, D), out_dtype), S((1, D), F32)])


def _sgu_gate_bwd(a_s, dg_s, vgain, ws, bst, name, hook=None):
    t = a_s.shape[1]
    sw = a_s.shape[2]
    gps = sw // CHUNK

    def body(a_ref, dg_ref, vg_ref, ws_ref, b_ref, da_ref, dws_ref, dbt_ref, dvg_ref, dvn_ref):
        n = pl.program_id(0)

        @pl.when(n == 0)
        def _():
            dws_ref[...] = jnp.zeros_like(dws_ref)
            dbt_ref[...] = jnp.zeros_like(dbt_ref)
            dvg_ref[...] = jnp.zeros_like(dvg_ref)

        vpre = jnp.concatenate([a_ref[4 + s].astype(F32) for s in range(4)], axis=1)
        v, v_grad = _gelu_and_grad(vpre)
        r = _rstd(v)
        vhat = v * r
        vn = (vhat * vg_ref[...]).astype(BF16)
        tri = _tril_mask()
        lane = lax.broadcasted_iota(jnp.int32, (CHUNK, CHUNK), 1)
        dbt = jnp.zeros((CHUNK, CHUNK), F32)
        for g in range(SGU_G):
            w = jnp.where(tri, ws_ref[g], 0.0).astype(BF16)
            vng = vn[:, g * CHUNK:(g + 1) * CHUNK]
            sg = _dot(w, vng) + b_ref[:, g:g + 1]
            lo = (g % gps) * CHUNK
            u, u_grad = _gelu_and_grad(a_ref[g // gps, :, lo:lo + CHUNK].astype(F32))
            dgate = dg_ref[g // gps, :, lo:lo + CHUNK].astype(F32)
            da_ref[g // gps, :, lo:lo + CHUNK] = (dgate * sg * u_grad).astype(BF16)
            ds = dgate * u
            dsb = ds.astype(BF16)
            dvn_ref[:, g * CHUNK:(g + 1) * CHUNK] = _dot_tn(w, dsb)
            dws_ref[g] += jnp.where(tri, _dot_nt(dsb, vng), 0.0)
            dbt = dbt + jnp.where(lane == g, jnp.sum(ds, axis=-1, keepdims=True), 0.0)
        dbt_ref[...] += dbt
        dvn = dvn_ref[...]
        dvg_ref[...] += jnp.sum(dvn * vhat, axis=0, keepdims=True)
        gg = dvn * vg_ref[...]
        dv = r * gg - v * (r * r * r * jnp.mean(gg * v, axis=-1, keepdims=True))
        dav = (dv * v_grad).astype(BF16)
        for s in range(4):
            da_ref[4 + s] = dav[:, s * sw:(s + 1) * sw]

    return _run(
        body, [a_s, dg_s, vgain, ws, bst], hook, grid=(t // CHUNK,), name=name, semantics=("arbitrary",),
        in_specs=[pl.BlockSpec((8, CHUNK, sw), lambda n: (0, n, 0)), pl.BlockSpec((4, CHUNK, sw), lambda n: (0, n, 0)),
                  pl.BlockSpec((1, SGU_W), lambda n: (0, 0)), pl.BlockSpec((SGU_G, CHUNK, CHUNK), lambda n: (0, 0, 0)),
                  pl.BlockSpec((CHUNK, SGU_G), lambda n: (0, 0))],
        out_specs=[pl.BlockSpec((8, CHUNK, sw), lambda n: (0, n, 0)), pl.BlockSpec((SGU_G, CHUNK, CHUNK), lambda n: (0, 0, 0)),
                   pl.BlockSpec((CHUNK, CHUNK), lambda n: (0, 0)), pl.BlockSpec((1, SGU_W), lambda n: (0, 0))],
        out_shape=[S((8, t, sw), BF16), S((SGU_G, CHUNK, CHUNK), F32), S((CHUNK, CHUNK), F32), S((1, SGU_W), F32)],
        scratch_shapes=[pltpu.VMEM((CHUNK, SGU_W), F32)])


def _attn_bwd(qkv_t, do_t, qg, kg, sinks, bias, name, hook=None):
    t = qkv_t.shape[1]
    nb = t // CHUNK

    def body(cur_ref, prev_ref, do_ref, qg_ref, kg_ref, sink_ref, bias_ref,
             o_ref, dqg_out, dkg_out, dsk_out, dbias_ref, carry, dqg_ref, dkg_ref, dsk_ref):
        n = pl.program_id(0)

        @pl.when(n == 0)
        def _():
            carry[...] = jnp.zeros_like(carry)
            dqg_ref[...] = jnp.zeros_like(dqg_ref)
            dkg_ref[...] = jnp.zeros_like(dkg_ref)
            dsk_ref[...] = jnp.zeros_like(dsk_ref)
            dbias_ref[...] = jnp.zeros_like(dbias_ref)

        @pl.when(n < nb)
        def _():
            valid = _attn_valid(n)
            o_ref[0:KV0, :] = carry[0:KV0, :].astype(BF16)
            kvs, heads = range(NKV), range(NH)
            group = lambda h: range(KVG * h, KVG * (h + 1))
            ks = [_attn_band(cur_ref, prev_ref, KV0 + HD * h) for h in kvs]
            rks = [_rstd_rows(k) for k in ks]
            khats = [k * rk for k, rk in zip(ks, rks)]
            kns = [(khat * kg_ref[...]).astype(BF16) for khat in khats]
            kn_toks = [kn.T for kn in kns]
            vbs = [_attn_band(cur_ref, prev_ref, KV0 + HD * (NKV + h)).astype(BF16) for h in kvs]
            v_toks = [vb.T for vb in vbs]
            qs = [cur_ref[HD * hq:HD * (hq + 1), :] for hq in heads]
            rqs = [_rstd_rows(q) for q in qs]
            qhats = [q * rq for q, rq in zip(qs, rqs)]
            qns = [(qhat * qg_ref[...]).astype(BF16) for qhat in qhats]
            probs = [_attn_probs(kn_toks[hq // KVG], qns[hq], bias_ref[hq], valid, sink_ref[hq]) for hq in heads]
            dohs = [do_ref[HD * hq:HD * (hq + 1), :] for hq in heads]
            dps = [_dot(v_toks[hq // KVG], dohs[hq]) for hq in heads]
            dsums = [jnp.sum(p * dp, axis=0, keepdims=True) for (p, _), dp in zip(probs, dps)]
            dss = [p * (dp - dsum) for (p, _), dp, dsum in zip(probs, dps, dsums)]
            for hq in heads:
                dsk_ref[hq:hq + 1, :] -= probs[hq][1] * dsums[hq]
                dbias_ref[hq] += dss[hq]
            dvs = [sum(_dot_nt(dohs[hq], probs[hq][0].astype(BF16)) for hq in group(h)) for h in kvs]
            dscs = [(ds * (HD ** -0.5)).astype(BF16) for ds in dss]
            dqns = [_dot(kns[hq // KVG], dscs[hq]) for hq in heads]
            dkns = [sum(_dot_nt(qns[hq], dscs[hq]) for hq in group(h)) for h in kvs]
            dqg_ref[...] += sum(dqn * qhat for dqn, qhat in zip(dqns, qhats))
            for hq in heads:
                gq = dqns[hq] * qg_ref[...]
                carry[HD * hq:HD * (hq + 1), :] = rqs[hq] * gq - qs[hq] * (
                    rqs[hq] * rqs[hq] * rqs[hq] * jnp.mean(gq * qs[hq], axis=0, keepdims=True))
            dkg_ref[...] += sum(dkn * khat for dkn, khat in zip(dkns, khats))
            for h in kvs:
                krow, vrow = KV0 + HD * h, KV0 + HD * (NKV + h)
                gk = dkns[h] * kg_ref[...]
                dk = rks[h] * gk - ks[h] * (rks[h] * rks[h] * rks[h] * jnp.mean(gk * ks[h], axis=0, keepdims=True))
                o_ref[krow:krow + HD, :] = (carry[krow:krow + HD, :] + dk[:, :CHUNK]).astype(BF16)
                o_ref[vrow:vrow + HD, :] = (carry[vrow:vrow + HD, :] + dvs[h][:, :CHUNK]).astype(BF16)
                carry[krow:krow + HD, :] = dk[:, CHUNK:]
                carry[vrow:vrow + HD, :] = dvs[h][:, CHUNK:]

        @pl.when(n == nb)
        def _():
            o_ref[...] = carry[...].astype(BF16)
            dqg_out[...] = jnp.sum(dqg_ref[...], axis=1, keepdims=True)
            dkg_out[...] = jnp.sum(dkg_ref[...], axis=1, keepdims=True)
            dsk_out[...] = jnp.sum(dsk_ref[...], axis=1, keepdims=True)

    cur = lambda n: (0, jnp.minimum(n, nb - 1))
    col = pl.BlockSpec((HD, 1), lambda n: (0, 0))
    whole = lambda shape: pl.BlockSpec(shape, lambda n: (0,) * len(shape))
    return _run(
        body, [qkv_t, qkv_t, do_t, qg, kg, sinks, bias], hook, grid=(nb + 1,), name=name, semantics=("arbitrary",),
        in_specs=[pl.BlockSpec((QKV, CHUNK), cur),
                  pl.BlockSpec((QKV - KV0, CHUNK), lambda n: (KV0 // (QKV - KV0), jnp.clip(n - 1, 0, nb - 1))),
                  pl.BlockSpec((D, CHUNK), cur), col, col, pl.BlockSpec(memory_space=pltpu.SMEM), whole((NH, 2 * CHUNK, CHUNK))],
        out_specs=[pl.BlockSpec((QKV, CHUNK), lambda n: (0, jnp.maximum(n - 1, 0))), whole((HD, 1)), whole((HD, 1)),
                   whole((NH, 1)), whole((NH, 2 * CHUNK, CHUNK))],
        out_shape=[S((QKV, t), BF16), S((HD, 1), F32), S((HD, 1), F32), S((NH, 1), F32), S((NH, 2 * CHUNK, CHUNK), F32)],
        scratch_shapes=[pltpu.VMEM((QKV, CHUNK), F32), pltpu.VMEM((HD, CHUNK), F32), pltpu.VMEM((HD, 2 * CHUNK), F32),
                        pltpu.VMEM((NH, CHUNK), F32)])


class _Plain:
    def __init__(self, wg):
        self.full, self.grads = wg, {}

    def w(self, n):
        return self.full[n]

    def hook(self, host):
        return None

    def grad(self, n, pair):
        self.grads[n] = pair

    def small(self, g_rep):
        pass

    def sync(self, point):
        pass

    def first_norm(self, x, gain):
        return _rmsnorm(x, gain, "norm0")


def _local_step(x, target, rep, sch):
    bucket_row = jnp.asarray(_rel_tables().T.reshape(1, -1))
    bias = _relbias_fwd(rep["rel_bias"].T, bucket_row, "relbias_fwd").reshape(NH, 2 * CHUNK, CHUNK)
    bst = rep["sgu_b_s"][0].T
    ws = rep["sgu_w_s"][0]
    vgain = rep["sgu_v_gain"]
    qg, kg, sinks = rep["attn_q_gain"].reshape(HD, 1), rep["attn_k_gain"].reshape(HD, 1), rep["attn_sinks"][0]
    w_down = lambda l: sch.w("ffn_w_down%d" % l).reshape(D_FF, D)
    w_up = lambda l: sch.w("ffn_w_up%d" % l)
    cb = [rep["ffn_conv_b"][l].reshape(8, 1, -1) for l in range(2)]
    mixg = [rep["mix_norm"][l:l + 1] for l in range(2)]
    ffng = [rep["ffn_norm"][l:l + 1] for l in range(2)]
    rows = lambda pair: tuple(g.reshape(N_DEV, -1, D) for g in pair)
    hk = sch.hook

    hn0 = sch.first_norm(x, mixg[0])
    cw = [sch.w("ffn_conv_w")[:, 3 * l:3 * l + 3] for l in range(2)]
    a0 = _mm_slot(hn0, sch.w("sgu_w_in"), BF16, "sgu_in", hk("sgu_in"))
    gated = _sgu_gate_fwd(a0, vgain, ws, bst, "sgu_gate", hk("sgu_gate"))
    h1, hn1 = _resid_mm(gated, sch.w("sgu_w_out").reshape(SGU_W, D), x, ffng[0], "norm", "sgu_out", hk("sgu_out"))
    sch.sync("before_ffn0")
    a_ff0, c_ff0, h2, hn2 = _ffn_fwd(hn1, h1, w_up(0), w_down(0), cw[0], cb[0], mixg[1], "norm", "ffn0_fwd", hk("ffn0_fwd"))
    qkv = _mm_t(hn2, sch.w("attn_w_qkv"), "qkv", hk("qkv"))
    o = _attn_fwd(qkv, qg, kg, sinks, bias, "attn", hk("attn"))
    h3, hn3 = _resid_mm(o, sch.w("attn_w_o").reshape(D, D), h2, ffng[1], "norm", "attn_out", hk("attn_out"), fm=True)
    a_ff1, c_ff1, dy, sq = _ffn_fwd(hn3, h3, w_up(1), w_down(1), cw[1], cb[1], target, "loss", "ffn1_fwd_loss", hk("ffn1_fwd_loss"))
    loss = (0.5 / D) * jnp.sum(sq[:, 0, 0])

    def ffn_bwd(dh, h_in, hn, a, c, l, tag):
        dc, g_down, g_down_b = _ffn_bwd1(dh, c, w_down(l), tag + "_bwd1", hk(tag + "_bwd1"))
        sch.grad("ffn_w_down%d" % l, rows((g_down, g_down_b)))
        da, dh_new, dgain, g_cw, g_cb = _ffn_bwd2(dc, a, w_up(l), cw[l], h_in, ffng[l], dh, tag + "_bwd2", hk(tag + "_bwd2"))
        sch.grad("ffn_w_up%d" % l, _dw_slot(hn, da, tag + "_dw_up", hk(tag + "_dw_up")))
        return dh_new, dgain, g_cw, g_cb.reshape(-1)

    dh, d_ffng1, g_cw1, g_cb1 = ffn_bwd(dy, h3, hn3, a_ff1, c_ff1, 1, "ffn1")
    do = _dx_rows_t(dh, sch.w("attn_w_o").reshape(D, D), "attn_do", hk("attn_do"))
    sch.grad("attn_w_o", rows(_dw_rows(o, dh, "dw_o", hk("dw_o"), fm=True)))
    dqkv, d_qg, d_kg, d_sk, d_bias = _attn_bwd(qkv, do, qg, kg, sinks, bias, "attn_bwd", hk("attn_bwd"))
    sch.grad("attn_w_qkv", tuple(g.reshape(N_DEV, -1, D) for g in _dw_rows(dqkv, hn2, "dw_qkv", hk("dw_qkv"), fm=True)))
    dh, d_mixg1 = _dx_slot_normbwd(dqkv, sch.w("attn_w_qkv").reshape(QKV, D), h2, mixg[1], dh, "dx_qkv", hk("dx_qkv"), fm=True,
                                   out_dtype=DH)
    d_relb = _relbias_bwd(d_bias.reshape(NH, -1), bucket_row, "relbias_bwd").T
    g_rep = {"attn_q_gain": d_qg.reshape(1, HD), "attn_k_gain": d_kg.reshape(1, HD), "attn_sinks": d_sk.reshape(1, NH),
             "rel_bias": d_relb}
    sch.small(g_rep)
    dh, d_ffng0, g_cw0, g_cb0 = ffn_bwd(dh, h1, hn1, a_ff0, c_ff0, 0, "ffn0")
    g_cw = jnp.concatenate([g_cw0, g_cw1], axis=1)
    sch.grad("ffn_conv_w", (g_cw, g_cw.astype(BF16)))
    g_ffn = {"ffn_norm": jnp.concatenate([d_ffng0, d_ffng1], axis=0), "ffn_conv_b": jnp.stack([g_cb0, g_cb1], axis=0)}
    sch.small(g_ffn)
    dgated = _dx_rows(dh, sch.w("sgu_w_out").reshape(SGU_W, D), SGU_W // 4, BF16, "sgu_dgated", hk("sgu_dgated"))
    sch.grad("sgu_w_out", rows(_dw_rows(gated, dh, "dw_sgu_out", hk("dw_sgu_out"))))
    da0, d_ws, d_bst, d_vgain = _sgu_gate_bwd(a0, dgated, vgain, ws, bst, "sgu_gate_bwd", hk("sgu_gate_bwd"))
    g_sgu = {"sgu_v_gain": d_vgain, "sgu_w_s": d_ws[None], "sgu_b_s": d_bst[:, :SGU_G].T[None]}
    sch.small(g_sgu)
    sch.grad("sgu_w_in", _dw_slot(hn0, da0, "dw_sgu_in", hk("dw_sgu_in")))
    sch.sync("after_dw")
    grad_x, d_mixg0 = _dx_slot_normbwd(da0, sch.w("sgu_w_in"), x, mixg[0], dh, "dx_sgu_in", hk("dx_sgu_in"))
    g_mix = {"mix_norm": jnp.concatenate([d_mixg0, d_mixg1], axis=0)}
    sch.small(g_mix)
    for g in (g_ffn, g_sgu, g_mix):
        g_rep.update(g)
    return loss, grad_x, g_rep


def _allgather(xs, x_in, gain, name):
    nt = len(xs)
    t_rows = x_in.shape[0]
    tm = _tm(t_rows)

    def body(xin_ref, g_ref, *refs):
        x_refs, hn_ref, o_refs = refs[:nt], refs[nt], refs[nt + 1:2 * nt + 1]
        send_sems, recv_sems, local_sems = refs[2 * nt + 1:]
        x, y, c, chips = _place()
        me, sibling = (x, y, c), (x, y, 1 - c)

        def copy(t, k, block, to, src=None):
            px, py, pc = block
            dst = o_refs[t].at[4 * px + 2 * py + pc]
            return pltpu.make_async_remote_copy(
                src_ref=dst if src is None else src, dst_ref=dst, send_sem=send_sems.at[t, k], recv_sem=recv_sems.at[t, k],
                device_id=to, device_id_type=MESH)

        mine = lambda: [pltpu.make_async_copy(x_refs[t], o_refs[t].at[4 * x + 2 * y + c], local_sems.at[t]) for t in range(nt)]

        def first():
            out = []
            for t in range(nt):
                out.append(copy(t, 0, me, sibling, src=x_refs[t]))
                out += [copy(t, 1 + j, me, (*chip, c), src=x_refs[t]) for j, chip in enumerate(chips)]
            return out

        @pl.when(pl.program_id(0) == 0)
        def _():
            for cp in mine() + first():
                cp.start()

        xv = xin_ref[...]
        hn_ref[...] = (xv * _rstd(xv) * g_ref[...]).astype(BF16)

        @pl.when(pl.program_id(0) == pl.num_programs(0) - 1)
        def _():
            passed = []
            for j, chip in enumerate(chips):
                for t in range(nt):
                    copy(t, 1 + j, (*chip, c), me).wait_recv()
                    fwd = copy(t, 4 + j, (*chip, c), sibling)
                    fwd.start()
                    passed.append(fwd)
            for t in range(nt):
                copy(t, 0, sibling, me).wait_recv()
                for j, chip in enumerate(chips):
                    copy(t, 4 + j, (*chip, 1 - c), me).wait_recv()
            for cp in first() + passed:
                cp.wait_send()
            for cp in mine():
                cp.wait()

    res = pl.pallas_call(
        body, name=name, grid=(t_rows // tm,),
        in_specs=[pl.BlockSpec((tm, D), lambda i: (i, 0)), pl.BlockSpec((1, D), lambda i: (0, 0))] + [ANY] * nt,
        out_specs=[pl.BlockSpec((tm, D), lambda i: (i, 0))] + [ANY] * nt,
        out_shape=[S((t_rows, D), BF16)] + [S((N_DEV,) + a.shape, a.dtype) for a in xs],
        scratch_shapes=[pltpu.SemaphoreType.DMA((nt, 7)), pltpu.SemaphoreType.DMA((nt, 7)), pltpu.SemaphoreType.DMA((nt,))],
        compiler_params=pltpu.CompilerParams(dimension_semantics=("arbitrary",), has_side_effects=True))(x_in, gain, *xs)
    return res[0], res[1:]


def _exchange(hook, name):
    comm = hook()
    ci, co = len(comm.inputs), len(comm.out_shapes)

    def body(*refs):
        cins, couts = refs[:ci], refs[ci:ci + co]
        send, recv = refs[-2:]
        comm.start(cins, couts, send, recv)
        comm.finish(cins, couts, send, recv)

    res = pl.pallas_call(
        body, name=name, in_specs=[ANY] * ci, out_specs=[ANY] * co, out_shape=comm.out_shapes,
        scratch_shapes=[pltpu.SemaphoreType.DMA((comm.n_sems,)), pltpu.SemaphoreType.DMA((comm.n_sems,))],
        input_output_aliases=dict(comm.aliases),
        compiler_params=pltpu.CompilerParams(has_side_effects=True))(*comm.inputs)
    hook(res)


def _row_tile(r):
    tr = r if r <= ROW_TILE or r % ROW_TILE else ROW_TILE
    assert r % tr == 0
    return tr


def _rs_partial(g32, sib, place, name):
    _, r, cdim = g32.shape
    tr = _row_tile(r)

    def body(place_ref, g_ref, s_ref, p_ref, own_ref):
        k = pl.program_id(1)
        tot = g_ref[...] + s_ref[...].astype(F32)
        p_ref[...] = tot.astype(BF16)

        @pl.when(k == place_ref[1])
        def _():
            own_ref[...] = tot

    grid_spec = pltpu.PrefetchScalarGridSpec(
        num_scalar_prefetch=1, grid=(r // tr, 4),
        in_specs=[pl.BlockSpec((None, None, tr, cdim), lambda i, k, pr: (k, pr[0], i, 0)),
                  pl.BlockSpec((None, tr, cdim), lambda i, k, pr: (k, i, 0))],
        out_specs=[pl.BlockSpec((None, tr, cdim), lambda i, k, pr: (k, i, 0)), pl.BlockSpec((tr, cdim), lambda i, k, pr: (i, 0))])
    return pl.pallas_call(
        body, grid_spec=grid_spec, name=name,
        out_shape=[S((4, r, cdim), BF16), S((r, cdim), F32)],
        compiler_params=_cp("parallel", "arbitrary"))(place, g32.reshape(4, 2, r, cdim), sib)


def _adamw_math(w, g, m, v):
    m = ADAM_B1 * m + (1.0 - ADAM_B1) * g
    v = ADAM_B2 * v + (1.0 - ADAM_B2) * (g * g)
    m_hat = m / (1.0 - ADAM_B1 ** ADAM_STEP)
    v_hat = v / (1.0 - ADAM_B2 ** ADAM_STEP)
    delta = -ADAM_LR * (m_hat / (jnp.sqrt(v_hat) + ADAM_EPS) + ADAM_WD * w)
    return delta, m, v


def _adamw_shard(owns, recvs, w, m, v, name, flipped=False):
    nl = w.shape[0]
    r, cdim = owns[0].shape
    tr = _row_tile(r)
    nr = r // tr

    def body(*refs):
        own_refs, recv_refs = refs[:nl], refs[nl:2 * nl]
        w_ref, m_ref, v_ref, g_out, d_out, m_out, v_out = refs[2 * nl:]
        layer = pl.program_id(0)
        g = None
        for l in range(nl):
            gl = own_refs[l][...] + recv_refs[l][0].astype(F32) + recv_refs[l][1].astype(F32) + recv_refs[l][2].astype(F32)
            g = gl if g is None else jnp.where(layer == l, gl, g)
        if flipped:
            g = g.T
        g_out[...] = g
        d_out[...], m_out[...], v_out[...] = _adamw_math(w_ref[...], g, m_ref[...], v_ref[...])

    park = lambda l: (lambda layer, i: (jnp.where(layer == l, i, jnp.where(layer < l, 0, nr - 1)), 0))
    park3 = lambda l: (lambda layer, i: (0, jnp.where(layer == l, i, jnp.where(layer < l, 0, nr - 1)), 0))
    if flipped:
        row = pl.BlockSpec((None, cdim, tr), lambda layer, i: (layer, 0, i))
    else:
        row = pl.BlockSpec((None, tr, cdim), lambda layer, i: (layer, i, 0))
    return pl.pallas_call(
        body, grid=(nl, nr), name=name,
        in_specs=[pl.BlockSpec((tr, cdim), park(l)) for l in range(nl)] + [pl.BlockSpec((3, tr, cdim), park3(l)) for l in range(nl)]
        + [row, row, row],
        out_specs=[row] * 4, out_shape=[S(w.shape, F32)] * 4,
        compiler_params=_cp("arbitrary", "arbitrary"))(*owns, *recvs, w, m, v)


def _adamw_small(galls, ws, ms, vs, name):
    n = len(galls)

    def body(*refs):
        g_refs, w_refs, m_refs, v_refs, outs = refs[:n], refs[n:2 * n], refs[2 * n:3 * n], refs[3 * n:4 * n], refs[4 * n:]
        for i in range(n):
            g = g_refs[i][0].astype(F32)
            for s in range(1, N_DEV):
                g = g + g_refs[i][s].astype(F32)
            outs[i][...] = g
            outs[n + i][...], outs[2 * n + i][...], outs[3 * n + i][...] = _adamw_math(w_refs[i][...], g, m_refs[i][...], v_refs[i][...])

    res = pl.pallas_call(body, out_shape=[S(a.shape, F32) for a in ws] * 4, name=name)(*galls, *ws, *ms, *vs)
    return [res[k * n:(k + 1) * n] for k in range(4)]


REPLICATED = ["mix_norm", "ffn_norm", "sgu_v_gain", "sgu_w_s", "sgu_b_s", "attn_q_gain", "attn_k_gain", "attn_sinks", "rel_bias",
              "ffn_conv_b"]
WEIGHTS = ["mix_norm", "ffn_norm", "sgu_w_in", "sgu_v_gain", "sgu_w_s", "sgu_b_s", "sgu_w_out", "attn_w_qkv", "attn_q_gain",
           "attn_k_gain", "attn_sinks", "attn_w_o", "rel_bias", "ffn_w_up", "ffn_conv_w", "ffn_conv_b", "ffn_w_down"]
SMALL = ["g_" + n for n in REPLICATED]
BF16_TRANSIT = {"sgu_w_s"}
SMALL_ATTN = ["g_attn_q_gain", "g_attn_k_gain", "g_attn_sinks", "g_rel_bias"]
SMALL_FFN = ["g_ffn_norm", "g_ffn_conv_b"]
SMALL_SGU = ["g_sgu_v_gain", "g_sgu_w_s", "g_sgu_b_s"]

GATHER_FIRST = ["sgu_w_in", "ffn_conv_w"]
UP0_SPLIT, UP1_SPLIT = 352, 352
PLAN = {
    "sgu_in": [("ag1", "sgu_w_out"), ("ag1", "ffn_w_up0", (0, UP0_SPLIT))],
    "sgu_gate": [("ag2", "sgu_w_out"), ("ag1", "ffn_w_up0", (UP0_SPLIT, D))],
    "sgu_out": [("ag2", "ffn_w_up0"), ("ag1", "ffn_w_down0")],
    "before_ffn0": [("ag2", "ffn_w_down0")],
    "ffn0_fwd": [("agd", "attn_w_qkv"), ("ag1", "attn_w_o"), ("ag1", "ffn_w_down1"), ("ag1", "ffn_w_up1", (0, UP1_SPLIT))],
    "attn": [("ag2", "attn_w_o"), ("ag2", "ffn_w_down1"), ("ag1", "ffn_w_up1", (UP1_SPLIT, D))],
    "attn_out": [("ag2", "ffn_w_up1")],
    "attn_bwd": [("rs1", "ffn_w_down1"), ("rs1", "ffn_w_up1"), ("rs1", "attn_w_o")],
    "ffn0_bwd1": [("rs2", "ffn_w_down1"), ("rs2", "attn_w_o"), ("rs1", "attn_w_qkv")] + [("ag1", n) for n in SMALL_ATTN],
    "ffn0_bwd2": [("rs2", "ffn_w_up1"), ("rs2", "attn_w_qkv"), ("rs1", "ffn_w_down0")] + [("ag2", n) for n in SMALL_ATTN],
    "ffn0_dw_up": [("rs2", "ffn_w_down0")],
    "sgu_dgated": [("rs1", "ffn_w_up0")] + [("agd", n) for n in SMALL_FFN],
    "sgu_gate_bwd": [("rs2", "ffn_w_up0"), ("rs1", "sgu_w_out")],
    "dw_sgu_in": [("rs2", "sgu_w_out")] + [("ag1", n) for n in SMALL_SGU],
    "after_dw": [("rs1", "sgu_w_in"), ("rs1", "ffn_conv_w")] + [("ag2", n) for n in SMALL_SGU],
    "dx_sgu_in": [("rs2", "sgu_w_in"), ("rs2", "ffn_conv_w")],
    "last": [("agd", "g_mix_norm")],
}


class _Overlap:
    def __init__(self, shard, place):
        self.shard, self.place = shard, place
        self.part, self.full = {}, {}
        self.grads, self.sib, self.own, self.recv = {}, {}, {}, {}

    def w(self, n):
        return self.full[n]

    def grad(self, n, pair):
        self.grads[n] = pair

    def small(self, g_rep):
        self.shard.update(("g_" + n, a.astype(BF16) if n in BF16_TRANSIT else a) for n, a in _views2d(g_rep).items())

    def sync(self, point):
        _exchange(self.hook(point), point)

    def first_norm(self, x, gain):
        hn, full = _allgather([self.shard[n] for n in GATHER_FIRST], x, gain, "gather_first")
        self.full.update(zip(GATHER_FIRST, full))
        return hn

    def chip_sums(self, n):
        sums, self.own[n] = _rs_partial(self.grads[n][0], self.sib.pop(n), self.place, "rs_partial_" + n)
        return sums

    def hook(self, host):
        ops = PLAN.get(host)
        if not ops:
            return None
        where = {"ag1": self.part, "ag2": self.full, "agd": self.full, "rs1": self.sib, "rs2": self.recv}
        idx = []

        def hook(results=None):
            if results is not None:
                for (kind, n, *_), i in zip(ops, idx):
                    where[kind][n] = results[i]
                return None
            comm = _Comm()
            for kind, n, *rows in ops:
                arr = {"ag1": lambda: self.shard[n], "agd": lambda: self.shard[n], "ag2": lambda: self.part.pop(n),
                       "rs1": lambda: self.grads[n][1], "rs2": lambda: self.chip_sums(n)}[kind]()
                idx.append(comm.add(kind, arr, *rows, into=self.part.pop(n) if rows and rows[0][0] else None))
            return comm

        return hook


TRANSPOSED = {"attn_w_qkv"}
PHYSICAL_T = {"ffn_w_up"}
SHARDED = {
    "sgu_w_in": ["sgu_w_in"], "sgu_w_out": ["sgu_w_out"], "attn_w_qkv": ["attn_w_qkv"], "attn_w_o": ["attn_w_o"],
    "ffn_w_up": ["ffn_w_up0", "ffn_w_up1"], "ffn_w_down": ["ffn_w_down0", "ffn_w_down1"], "ffn_conv_w": ["ffn_conv_w"],
}


def _send_views(w):
    out = {"ffn_conv_w": w["ffn_conv_w"].reshape(6, -1)}
    for name, parts in SHARDED.items():
        if name != "ffn_conv_w":
            out.update((p, (w[name][l].T if name in TRANSPOSED else w[name][l]).astype(BF16)) for l, p in enumerate(parts))
    return out


def _views2d(d):
    return {n: d[n].reshape(-1, d[n].shape[-1]) for n in REPLICATED if n in d}


def kernel(x, mix_norm, ffn_norm, sgu_w_in, sgu_v_gain, sgu_w_s, sgu_b_s, sgu_w_out, attn_w_qkv, attn_q_gain, attn_k_gain, attn_sinks, attn_w_o, rel_bias, ffn_w_up, ffn_conv_w, ffn_conv_b, ffn_w_down, loss_target, m_mix_norm, m_ffn_norm, m_sgu_w_in, m_sgu_v_gain, m_sgu_w_s, m_sgu_b_s, m_sgu_w_out, m_attn_w_qkv, m_attn_q_gain, m_attn_k_gain, m_attn_sinks, m_attn_w_o, m_rel_bias, m_ffn_w_up, m_ffn_conv_w, m_ffn_conv_b, m_ffn_w_down, v_mix_norm, v_ffn_norm, v_sgu_w_in, v_sgu_v_gain, v_sgu_w_s, v_sgu_b_s, v_sgu_w_out, v_attn_w_qkv, v_attn_q_gain, v_attn_k_gain, v_attn_sinks, v_attn_w_o, v_rel_bias, v_ffn_w_up, v_ffn_conv_w, v_ffn_conv_b, v_ffn_w_down):
    w = dict(zip(WEIGHTS, (mix_norm, ffn_norm, sgu_w_in, sgu_v_gain, sgu_w_s, sgu_b_s, sgu_w_out, attn_w_qkv, attn_q_gain, attn_k_gain,
                           attn_sinks, attn_w_o, rel_bias, ffn_w_up, ffn_conv_w, ffn_conv_b, ffn_w_down)))
    m = dict(zip(WEIGHTS, (m_mix_norm, m_ffn_norm, m_sgu_w_in, m_sgu_v_gain, m_sgu_w_s, m_sgu_b_s, m_sgu_w_out, m_attn_w_qkv, m_attn_q_gain,
                           m_attn_k_gain, m_attn_sinks, m_attn_w_o, m_rel_bias, m_ffn_w_up, m_ffn_conv_w, m_ffn_conv_b, m_ffn_w_down)))
    v = dict(zip(WEIGHTS, (v_mix_norm, v_ffn_norm, v_sgu_w_in, v_sgu_v_gain, v_sgu_w_s, v_sgu_b_s, v_sgu_w_out, v_attn_w_qkv, v_attn_q_gain,
                           v_attn_k_gain, v_attn_sinks, v_attn_w_o, v_rel_bias, v_ffn_w_up, v_ffn_conv_w, v_ffn_conv_b, v_ffn_w_down)))
    rep = {n: w[n] for n in REPLICATED}

    xi, yi, ci = lax.axis_index("x"), lax.axis_index("y"), lax.axis_index("c")
    place = jnp.stack([ci, 2 * xi + yi]).astype(jnp.int32)
    sch = _Overlap(_send_views(w), place)

    loss, grad_x, g_rep = _local_step(x[0], loss_target[0], rep, sch)
    loss = lax.psum(loss, ("x", "y", "c"))
    sch.sync("last")

    out = [{}, {}, {}, {}]
    for name, parts in SHARDED.items():
        flip = (lambda a: jnp.swapaxes(a, -1, -2)) if name in TRANSPOSED | PHYSICAL_T else (lambda a: a)
        shape = flip(w[name]).shape
        as3d = lambda a: flip(a).reshape(len(parts), -1, shape[-1])
        res = _adamw_shard([sch.own[p] for p in parts], [sch.recv[p] for p in parts], as3d(w[name]), as3d(m[name]), as3d(v[name]),
                           "adamw_" + name, flipped=name in PHYSICAL_T)
        for o, r in zip(out, res):
            o[name] = flip(r.reshape(shape))
    small = _adamw_small([sch.full[n] for n in SMALL], *[list(_views2d(d).values()) for d in (rep, m, v)], "adamw_small")
    for o, res in zip(out, small):
        o.update((n, r.reshape(w[n].shape)) for n, r in zip(REPLICATED, res))

    return (loss, grad_x[None], *[out[0][n] for n in WEIGHTS], *[out[1][n] for n in WEIGHTS],
            *[out[2][n] for n in WEIGHTS], *[out[3][n] for n in WEIGHTS])
```

```python
import functools
import math

import numpy as np
import jax
import jax.numpy as jnp
from jax import lax
from jax.experimental import pallas as pl
from jax.experimental.pallas import tpu as pltpu

F32 = jnp.float32
BF16 = jnp.bfloat16
DH = jnp.bfloat16
S = jax.ShapeDtypeStruct

D = 1024
CHUNK = 128
SGU_W = 2048
SGU_G = 16
HD = 64
NH = 16
NKV = 4
KVG = 4
D_FF = 2816
REL_BUCKETS = 32
REL_MAX_DIST = 128
EPS = 1e-6
N_DEV = 8
MESH = pl.DeviceIdType.MESH

ADAM_LR = 0.001
ADAM_B1 = 0.9
ADAM_B2 = 0.999
ADAM_EPS = 1e-08
ADAM_WD = 0.01
ADAM_STEP = 10

ROW_TILE = 512
HALO = 8
FFN_ROWS = 256


def _tm(t):
    return min(ROW_TILE, t)


def _cp(*sem):
    return pltpu.CompilerParams(dimension_semantics=sem)


ANY = pl.BlockSpec(memory_space=pl.ANY)


def _place():
    x, y, c = lax.axis_index("x"), lax.axis_index("y"), lax.axis_index("c")
    return x, y, c, [(1 - x, y), (x, 1 - y), (1 - x, 1 - y)]


class _Comm:
    SEMS = {"ag1": 5, "ag2": 3, "rs1": 4, "rs2": 3, "agd": 8}

    def __init__(self):
        self.inputs, self.out_shapes, self.aliases, self.ops, self.n_sems = [], [], {}, [], 0

    def add(self, kind, arr, rows=None, into=None):
        lead = {"ag1": N_DEV, "agd": N_DEV, "ag2": None, "rs1": 4, "rs2": 3}[kind]
        shape = arr.shape if lead is None else (lead,) + arr.shape[(0 if kind in ("ag1", "agd") else 1):]
        if kind == "ag2":
            self.aliases[len(self.inputs)] = len(self.out_shapes)
        self.ops.append((kind, len(self.inputs), len(self.out_shapes), self.n_sems, rows))
        self.inputs.append(arr)
        if into is not None:
            self.aliases[len(self.inputs)] = len(self.out_shapes)
            self.inputs.append(into)
        self.out_shapes.append(S(shape, arr.dtype))
        self.n_sems += self.SEMS[kind]
        return len(self.out_shapes) - 1

    def _copies(self, ins, outs, send, recv):
        x, y, c, chips = _place()
        me, sibling = (x, y, c), (x, y, 1 - c)
        slot = lambda px, py, pc: 4 * px + 2 * py + pc
        sends, recvs, local = [], [], []

        def rc(src, dst, k, to):
            return lambda: pltpu.make_async_remote_copy(src_ref=src(), dst_ref=dst(), send_sem=send.at[k], recv_sem=recv.at[k],
                                                        device_id=to, device_id_type=MESH)

        for kind, ii, oi, b, rows in self.ops:
            src, dst = ins[ii], outs[oi]
            at = lambda ref, i: (lambda: ref.at[i])
            if kind == "ag1":
                part = slice(None) if rows is None else pl.ds(rows[0], rows[1] - rows[0])
                to = lambda i, d=dst, p=part: (lambda: d.at[i, p])
                whole, mine = (lambda s=src, p=part: s.at[p]), to(slot(*me))
                sends.append(rc(whole, mine, b, sibling))
                recvs.append(rc(whole, to(slot(x, y, 1 - c)), b, me))
                for j, chip in enumerate(chips):
                    sends.append(rc(whole, mine, b + 1 + j, (*chip, c)))
                    recvs.append(rc(whole, to(slot(*chip, c)), b + 1 + j, me))
                local.append(lambda s=whole, m=mine, k=b + 4: pltpu.make_async_copy(s(), m(), send.at[k]))
            elif kind == "ag2":
                for j, chip in enumerate(chips):
                    sends.append(rc(at(dst, slot(*chip, c)), at(dst, slot(*chip, c)), b + j, sibling))
                    recvs.append(rc(at(dst, slot(*chip, 1 - c)), at(dst, slot(*chip, 1 - c)), b + j, me))
            elif kind == "agd":
                whole, mine = (lambda s=src: s), at(dst, slot(*me))
                flip = lambda v, bit: 1 - v if bit else v
                for k in range(1, N_DEV):
                    peer = (flip(x, k >> 2), flip(y, (k >> 1) & 1), flip(c, k & 1))
                    sends.append(rc(whole, mine, b + k - 1, peer))
                    recvs.append(rc(whole, at(dst, slot(*peer)), b + k - 1, me))
                local.append(lambda s=src, m=mine, k=b + 7: pltpu.make_async_copy(s, m(), send.at[k]))
            elif kind == "rs1":
                for k in range(4):
                    sends.append(rc(at(src, 2 * k + (1 - c)), at(dst, k), b + k, sibling))
                    recvs.append(rc(at(src, 2 * k + c), at(dst, k), b + k, me))
            else:
                for j, (px, py) in enumerate(chips):
                    sends.append(rc(at(src, 2 * px + py), at(dst, j), b + j, (px, py, c)))
                    recvs.append(rc(at(src, 2 * px + py), at(dst, j), b + j, me))
        return sends, recvs, local

    def start(self, ins, outs, send, recv):
        sends, _, local = self._copies(ins, outs, send, recv)
        for make in local + sends:
            make().start()

    def finish(self, ins, outs, send, recv):
        sends, recvs, local = self._copies(ins, outs, send, recv)
        for make in recvs:
            make().wait_recv()
        for make in sends:
            make().wait_send()
        for make in local:
            make().wait()


def _run(body, args, hook, *, grid, in_specs, out_specs, out_shape, name, semantics, scratch_shapes=(), aliases=None):
    comm = hook() if hook is not None else None
    aliases = dict(aliases or {})
    if comm is None:
        return pl.pallas_call(body, grid=grid, in_specs=in_specs, out_specs=out_specs, out_shape=out_shape, name=name,
                              scratch_shapes=list(scratch_shapes), input_output_aliases=aliases,
                              compiler_params=_cp(*semantics))(*args)
    single = not isinstance(out_shape, (list, tuple))
    out_shapes = [out_shape] if single else list(out_shape)
    out_specs_l = [out_specs] if single else list(out_specs)
    n_in, n_out, n_scr, ci, co = len(args), len(out_shapes), len(scratch_shapes), len(comm.inputs), len(comm.out_shapes)

    def wrapped(*refs):
        ins, cins = refs[:n_in], refs[n_in:n_in + ci]
        outs, couts = refs[n_in + ci:n_in + ci + n_out], refs[n_in + ci + n_out:n_in + ci + n_out + co]
        scr = refs[n_in + ci + n_out + co:n_in + ci + n_out + co + n_scr]
        send, recv = refs[-2:]
        first = functools.reduce(lambda a, b: a & b, [pl.program_id(a) == 0 for a in range(len(grid))])
        last = functools.reduce(lambda a, b: a & b, [pl.program_id(a) == g - 1 for a, g in enumerate(grid)])

        @pl.when(first)
        def _():
            comm.start(cins, couts, send, recv)

        body(*ins, *outs, *scr)

        @pl.when(last)
        def _():
            comm.finish(cins, couts, send, recv)

    res = pl.pallas_call(
        wrapped, grid=grid, in_specs=list(in_specs) + [ANY] * ci, out_specs=out_specs_l + [ANY] * co,
        out_shape=out_shapes + comm.out_shapes, name=name,
        scratch_shapes=list(scratch_shapes) + [pltpu.SemaphoreType.DMA((comm.n_sems,)), pltpu.SemaphoreType.DMA((comm.n_sems,))],
        input_output_aliases={**aliases, **{n_in + k: n_out + v for k, v in comm.aliases.items()}},
        compiler_params=pltpu.CompilerParams(dimension_semantics=("arbitrary",) * len(grid), has_side_effects=True))(*args, *comm.inputs)
    hook(res[n_out:])
    return res[0] if single else list(res[:n_out])


def _dot(a, b):
    return jnp.dot(a, b, preferred_element_type=F32)


def _dot_nt(a, b):
    return lax.dot_general(a, b, (((1,), (1,)), ((), ())), preferred_element_type=F32)


def _dot_tn(a, b):
    return lax.dot_general(a, b, (((0,), (0,)), ((), ())), preferred_element_type=F32)


def _gelu(x):
    return 0.5 * x * (1.0 + lax.erf(x * (2.0 ** -0.5)))


def _gelu_and_grad(x):
    cdf = 0.5 * (1.0 + lax.erf(x * (2.0 ** -0.5)))
    return x * cdf, cdf + x * jnp.exp(-0.5 * x * x) * (1.0 / math.sqrt(2.0 * math.pi))


def _sigmoid(x):
    return 1.0 / (1.0 + jnp.exp(-x))


def _rstd(x):
    return lax.rsqrt(jnp.mean(x * x, axis=-1, keepdims=True) + EPS)


def _rel_tables():
    q = np.arange(CHUNK)[:, None] + CHUNK
    k = np.arange(2 * CHUNK)[None, :]
    dist = q - k
    n = np.maximum(dist, 0)
    max_exact = REL_BUCKETS // 2
    large = max_exact + (np.log(np.maximum(n, 1).astype(np.float32) / max_exact)
                         / math.log(REL_MAX_DIST / max_exact) * (REL_BUCKETS - max_exact)).astype(np.int32)
    large = np.minimum(large, REL_BUCKETS - 1)
    return np.where(n < max_exact, n, large).astype(np.int32)


def _rmsnorm(x, gain, name):
    t = x.shape[0]
    tm = _tm(t)

    def body(x_ref, g_ref, o_ref):
        xv = x_ref[...]
        o_ref[...] = (xv * _rstd(xv) * g_ref[...]).astype(BF16)

    return pl.pallas_call(
        body, grid=(t // tm,), name=name,
        in_specs=[pl.BlockSpec((tm, D), lambda i: (i, 0)), pl.BlockSpec((1, D), lambda i: (0, 0))],
        out_specs=pl.BlockSpec((tm, D), lambda i: (i, 0)),
        out_shape=S((t, D), BF16), compiler_params=_cp("parallel"))(x, gain)


def _resident(shape):
    zeros = (0,) * len(shape)
    return pl.BlockSpec(shape, lambda *_: zeros, pipeline_mode=pl.Buffered(1))


def _mm_slot(hn, wg, out_dtype, name, hook=None):
    t, k = hn.shape
    ns, _, n = wg.shape
    tm = _tm(t)

    def body(a_ref, w_ref, o_ref):
        a = a_ref[...]
        for s in range(ns):
            o_ref[s] = _dot(a, w_ref[s]).astype(out_dtype)

    return _run(
        body, [hn, wg], hook, grid=(t // tm,), name=name, semantics=("parallel",),
        in_specs=[pl.BlockSpec((tm, k), lambda i: (i, 0)), _resident(wg.shape)],
        out_specs=pl.BlockSpec((ns, tm, n), lambda i: (0, i, 0)), out_shape=S((ns, t, n), out_dtype))


def _mm_t(hn, wt, name, hook=None):
    t, k = hn.shape
    ns, n, _ = wt.shape
    tm = _tm(t)

    def body(a_ref, w_ref, o_ref):
        a = a_ref[...]
        for s in range(ns):
            o_ref[s * n:(s + 1) * n, :] = _dot_nt(w_ref[s], a)

    return _run(
        body, [hn, wt], hook, grid=(t // tm,), name=name, semantics=("parallel",),
        in_specs=[pl.BlockSpec((tm, k), lambda i: (i, 0)), _resident(wt.shape)],
        out_specs=pl.BlockSpec((ns * n, tm), lambda i: (0, i)), out_shape=S((ns * n, t), F32))


def _conv3(a, prev, cw, cb, tm):
    ext = jnp.concatenate([prev, a], axis=0)
    return cw[2:3] * a + cw[1:2] * ext[HALO - 1:HALO - 1 + tm] + cw[0:1] * ext[HALO - 2:HALO - 2 + tm] + cb


def _ffn_fwd(hn, h, wup, wdown, cw, cb, extra, mode, name, hook=None):
    t, k = hn.shape
    n = wup.shape[-1]
    nh = wup.shape[0] // 2
    tm = min(FFN_ROWS, t)
    ni = t // tm

    def body(a_ref, h_ref, wu_ref, wd_ref, cw_ref, cb_ref, e_ref, as_ref, cs_ref, o1_ref, o2_ref, carry):
        i = pl.program_id(0)

        @pl.when(i == 0)
        def _():
            carry[...] = jnp.zeros_like(carry)

        a = a_ref[...]
        acc = h_ref[...]
        nxt = (_dot(a, wu_ref[0]), _dot(a, wu_ref[nh]))
        for j in range(nh):
            ag, av = nxt
            if j + 1 < nh:
                nxt = (_dot(a, wu_ref[j + 1]), _dot(a, wu_ref[nh + j + 1]))
            as_ref[j] = ag.astype(BF16)
            as_ref[nh + j] = av.astype(BF16)
            cg = _conv3(ag, carry[j], cw_ref[j], cb_ref[j], tm)
            cv = _conv3(av, carry[nh + j], cw_ref[nh + j], cb_ref[nh + j], tm)
            carry[j] = ag[tm - HALO:]
            carry[nh + j] = av[tm - HALO:]
            cs_ref[j] = cg.astype(BF16)
            cs_ref[nh + j] = cv.astype(BF16)
            act = (cg * _sigmoid(cg) * cv).astype(BF16)
            acc = acc + _dot(act, wd_ref[j * n:(j + 1) * n, :])
        if mode == "norm":
            o1_ref[...] = acc
            o2_ref[...] = (acc * _rstd(acc) * e_ref[...]).astype(BF16)
        else:
            err = acc - e_ref[...]
            o1_ref[...] = (err * (1.0 / D)).astype(o1_ref.dtype)
            o2_ref[...] = jnp.full(o2_ref.shape, jnp.sum(err * err), F32)

    row = pl.BlockSpec((tm, D), lambda i: (i, 0))
    if mode == "norm":
        e_spec, o2_spec, o2_shape = pl.BlockSpec((1, D), lambda i: (0, 0)), row, S((t, D), BF16)
    else:
        e_spec, o2_spec, o2_shape = row, pl.BlockSpec((None, 8, 128), lambda i: (i, 0, 0)), S((ni, 8, 128), F32)
    aspec = pl.BlockSpec((2 * nh, tm, n), lambda i: (0, i, 0))
    return _run(
        body, [hn, h, wup, wdown, cw, cb, extra], hook, grid=(ni,), name=name, semantics=("arbitrary",),
        in_specs=[pl.BlockSpec((tm, k), lambda i: (i, 0)), row, _resident(wup.shape), _resident(wdown.shape),
                  _resident(cw.shape), _resident(cb.shape), e_spec],
        out_specs=[aspec, aspec, row, o2_spec],
        out_shape=[S((2 * nh, t, n), BF16), S((2 * nh, t, n), BF16), S((t, D), F32 if mode == "norm" else DH), o2_shape],
        scratch_shapes=[pltpu.VMEM((2 * nh, HALO, n), F32)])


def _tril_mask():
    r = lax.broadcasted_iota(jnp.int32, (CHUNK, CHUNK), 0)
    c = lax.broadcasted_iota(jnp.int32, (CHUNK, CHUNK), 1)
    return r >= c


def _sgu_gate_fwd(a_s, vgain, ws, bst, name, hook=None):
    t = a_s.shape[1]
    sw = a_s.shape[2]
    gps = sw // CHUNK

    def body(a_ref, vg_ref, ws_ref, b_ref, o_ref):
        v = _gelu(jnp.concatenate([a_ref[4 + s].astype(F32) for s in range(4)], axis=1))
        vn = (v * _rstd(v) * vg_ref[...]).astype(BF16)
        tri = _tril_mask()
        for g in range(SGU_G):
            w = jnp.where(tri, ws_ref[g], 0.0).astype(BF16)
            sg = _dot(w, vn[:, g * CHUNK:(g + 1) * CHUNK]) + b_ref[:, g:g + 1]
            lo = (g % gps) * CHUNK
            u = _gelu(a_ref[g // gps, :, lo:lo + CHUNK].astype(F32))
            o_ref[g // gps, :, lo:lo + CHUNK] = (u * sg).astype(BF16)

    return _run(
        body, [a_s, vgain, ws, bst], hook, grid=(t // CHUNK,), name=name, semantics=("parallel",),
        in_specs=[pl.BlockSpec((8, CHUNK, sw), lambda n: (0, n, 0)), pl.BlockSpec((1, SGU_W), lambda n: (0, 0)),
                  pl.BlockSpec((SGU_G, CHUNK, CHUNK), lambda n: (0, 0, 0)), pl.BlockSpec((CHUNK, SGU_G), lambda n: (0, 0))],
        out_specs=pl.BlockSpec((4, CHUNK, sw), lambda n: (0, n, 0)), out_shape=S((4, t, sw), BF16))


def _resid_mm(a_s, w, resid, extra, mode, name, hook=None, fm=False):
    nk, t, kc = (1, a_s.shape[1], a_s.shape[0]) if fm else a_s.shape
    tm = _tm(t)
    ni = t // tm

    def body(a_ref, w_ref, r_ref, e_ref, o1_ref, o2_ref):
        h = r_ref[...]
        if fm:
            h = h + _dot_tn(a_ref[...], w_ref[...])
        for j in range(0 if fm else nk):
            h = h + _dot(a_ref[j], w_ref[j * kc:(j + 1) * kc, :])
        if mode == "norm":
            o1_ref[...] = h
            o2_ref[...] = (h * _rstd(h) * e_ref[...]).astype(BF16)
        else:
            err = h - e_ref[...]
            o1_ref[...] = (err * (1.0 / D)).astype(o1_ref.dtype)
            o2_ref[...] = jnp.full(o2_ref.shape, jnp.sum(err * err), F32)

    row = pl.BlockSpec((tm, D), lambda i: (i, 0))
    if mode == "norm":
        e_spec, o2_spec, o2_shape = pl.BlockSpec((1, D), lambda i: (0, 0)), row, S((t, D), BF16)
    else:
        e_spec, o2_spec, o2_shape = row, pl.BlockSpec((None, 8, 128), lambda i: (i, 0, 0)), S((ni, 8, 128), F32)
    return _run(
        body, [a_s, w, resid, extra], hook, grid=(ni,), name=name, semantics=("parallel",),
        in_specs=[pl.BlockSpec((kc, tm), lambda i: (0, i)) if fm else pl.BlockSpec((nk, tm, kc), lambda i: (0, i, 0)),
                  _resident(w.shape), row, e_spec],
        out_specs=[row, o2_spec], out_shape=[S((t, D), F32 if mode == "norm" else DH), o2_shape])


def _relbias_fwd(rel_bias_t, bucket_row, name):
    nb = bucket_row.shape[1]

    def body(rb_ref, bk_ref, o_ref):
        onehot = (lax.broadcasted_iota(jnp.int32, (REL_BUCKETS, nb), 0) == bk_ref[...]).astype(F32)
        o_ref[...] = jnp.dot(rb_ref[...], onehot, precision=lax.Precision.HIGHEST, preferred_element_type=F32)

    return pl.pallas_call(body, out_shape=S((NH, nb), F32), name=name)(rel_bias_t, bucket_row)


def _relbias_bwd(dbias, bucket_row, name):
    nb = bucket_row.shape[1]

    def body(db_ref, bk_ref, o_ref):
        onehot = (lax.broadcasted_iota(jnp.int32, (REL_BUCKETS, nb), 0) == bk_ref[...]).astype(F32)
        o_ref[...] = lax.dot_general(db_ref[...], onehot, (((1,), (1,)), ((), ())),
                                     precision=lax.Precision.HIGHEST, preferred_element_type=F32)

    return pl.pallas_call(body, out_shape=S((NH, REL_BUCKETS), F32), name=name)(dbias, bucket_row)


QKV = D + 2 * NKV * HD
KV0 = D


def _rstd_rows(x):
    return lax.rsqrt(jnp.mean(x * x, axis=0, keepdims=True) + EPS)


def _attn_valid(n):
    kj = lax.broadcasted_iota(jnp.int32, (2 * CHUNK, CHUNK), 0)
    qi = lax.broadcasted_iota(jnp.int32, (2 * CHUNK, CHUNK), 1)
    dist = qi + CHUNK - kj
    return (dist >= 0) & (dist < CHUNK) & ((n > 0) | (kj >= CHUNK))


def _attn_band(cur_ref, prev_ref, row):
    return jnp.concatenate([prev_ref[row - KV0:row - KV0 + HD, :], cur_ref[row:row + HD, :]], axis=1)


def _attn_probs(kn_tok, qn, bias, valid, sink):
    s = _dot(kn_tok, qn) * (HD ** -0.5) + bias
    s = jnp.where(valid, s, -jnp.inf)
    m = jnp.maximum(jnp.max(s, axis=0, keepdims=True), sink)
    p = jnp.exp(s - m)
    psink = jnp.exp(sink - m)
    inv = 1.0 / (jnp.sum(p, axis=0, keepdims=True) + psink)
    return p * inv, psink * inv


def _attn_fwd(qkv_t, qg, kg, sinks, bias, name, hook=None):
    t = qkv_t.shape[1]

    def body(cur_ref, prev_ref, qg_ref, kg_ref, sink_ref, bias_ref, o_ref):
        n = pl.program_id(0)
        valid = _attn_valid(n)
        ks = [_attn_band(cur_ref, prev_ref, KV0 + HD * h) for h in range(NKV)]
        kn_toks = [(k * _rstd_rows(k) * kg_ref[...]).astype(BF16).T for k in ks]
        vbs = [_attn_band(cur_ref, prev_ref, KV0 + HD * (NKV + h)).astype(BF16) for h in range(NKV)]
        qs = [cur_ref[HD * hq:HD * (hq + 1), :] for hq in range(NH)]
        qns = [(q * _rstd_rows(q) * qg_ref[...]).astype(BF16) for q in qs]
        ps = [_attn_probs(kn_toks[hq // KVG], qns[hq], bias_ref[hq], valid, sink_ref[hq])[0] for hq in range(NH)]
        for hq in range(NH):
            o_ref[HD * hq:HD * (hq + 1), :] = _dot(vbs[hq // KVG], ps[hq].astype(BF16)).astype(BF16)

    col = pl.BlockSpec((HD, 1), lambda n: (0, 0))
    return _run(
        body, [qkv_t, qkv_t, qg, kg, sinks, bias], hook, grid=(t // CHUNK,), name=name, semantics=("parallel",),
        in_specs=[pl.BlockSpec((QKV, CHUNK), lambda n: (0, n)),
                  pl.BlockSpec((QKV - KV0, CHUNK), lambda n: (KV0 // (QKV - KV0), jnp.maximum(n - 1, 0))),
                  col, col, pl.BlockSpec(memory_space=pltpu.SMEM), pl.BlockSpec((NH, 2 * CHUNK, CHUNK), lambda n: (0, 0, 0))],
        out_specs=pl.BlockSpec((D, CHUNK), lambda n: (0, n)), out_shape=S((D, t), BF16))


def _dx_rows(dh, w, kc, out_dtype, name, hook=None):
    t = dh.shape[0]
    nk = w.shape[0] // kc
    tm = _tm(t)

    def body(d_ref, w_ref, o_ref):
        dhb = d_ref[...].astype(BF16)
        for j in range(nk):
            o_ref[j] = _dot_nt(dhb, w_ref[j * kc:(j + 1) * kc, :]).astype(out_dtype)

    return _run(
        body, [dh, w], hook, grid=(t // tm,), name=name, semantics=("parallel",),
        in_specs=[pl.BlockSpec((tm, D), lambda i: (i, 0)), _resident(w.shape)],
        out_specs=pl.BlockSpec((nk, tm, kc), lambda i: (0, i, 0)), out_shape=S((nk, t, kc), out_dtype))


def _dx_rows_t(dh, w, name, hook=None):
    t = dh.shape[0]
    k = w.shape[0]
    tm = _tm(t)

    def body(d_ref, w_ref, o_ref):
        o_ref[...] = _dot_nt(w_ref[...], d_ref[...].astype(BF16)).astype(BF16)

    return _run(
        body, [dh, w], hook, grid=(t // tm,), name=name, semantics=("parallel",),
        in_specs=[pl.BlockSpec((tm, D), lambda i: (i, 0)), _resident(w.shape)],
        out_specs=pl.BlockSpec((k, tm), lambda i: (0, i)), out_shape=S((k, t), BF16))


def _ffn_bwd1(dh, c, wdown, name, hook=None):
    ns, t, n = c.shape
    nh = ns // 2
    tm = min(FFN_ROWS, t)
    ni = t // tm

    def body(d_ref, c_ref, wd_ref, dc_ref, dw_hbm, dwb_hbm, acc, stage):
        i = pl.program_id(0)

        @pl.when(i == 0)
        def _():
            acc[...] = jnp.zeros_like(acc)

        dhb = d_ref[...].astype(BF16)
        for j in range(nh):
            dact = _dot_nt(dhb, wd_ref[j * n:(j + 1) * n, :])
            cg = c_ref[j].astype(F32)
            cv = c_ref[nh + j].astype(F32)
            sg = _sigmoid(cg)
            gs = cg * sg
            acc[j * n:(j + 1) * n, :] += _dot_tn((gs * cv).astype(BF16), dhb)
            dc_ref[j] = (dact * cv * (sg + gs * (1.0 - sg))).astype(BF16)
            dc_ref[nh + j] = (dact * gs).astype(BF16)

        @pl.when(i == ni - 1)
        def _():
            pltpu.sync_copy(acc, dw_hbm)
            for j in range(nh):
                stage[...] = acc[j * n:(j + 1) * n, :].astype(BF16)
                pltpu.sync_copy(stage, dwb_hbm.at[pl.ds(j * n, n), :])

    slab = pl.BlockSpec((ns, tm, n), lambda i: (0, i, 0))
    return _run(
        body, [dh, c, wdown], hook, grid=(ni,), name=name, semantics=("arbitrary",),
        in_specs=[pl.BlockSpec((tm, D), lambda i: (i, 0)), slab, _resident(wdown.shape)],
        out_specs=[slab, ANY, ANY], out_shape=[S((ns, t, n), BF16), S(wdown.shape, F32), S(wdown.shape, BF16)],
        scratch_shapes=[pltpu.VMEM(wdown.shape, F32), pltpu.VMEM((n, D), BF16)])


def _ffn_bwd2(dc, a, wup, cw, h, gain, dh_in, name, hook=None):
    ns, t, n = dc.shape
    tm = min(FFN_ROWS, t)
    ni = t // tm

    def body(dc_ref, a_ref, wu_ref, cw_ref, h_ref, g_ref, di_ref, da_ref, o_ref, dg_ref, dcw_ref, dcb_ref, carry, keep):
        i = pl.program_id(0)

        @pl.when(i == 0)
        def _():
            carry[...] = jnp.zeros_like(carry)
            dg_ref[...] = jnp.zeros_like(dg_ref)
            dcw_ref[...] = jnp.zeros_like(dcw_ref)
            dcb_ref[...] = jnp.zeros_like(dcb_ref)

        rsum = lambda v: jnp.sum(v, axis=0, keepdims=True)
        acc = jnp.zeros((tm, D), F32)
        for s in range(ns):
            x = dc_ref[s].astype(F32)
            ext = jnp.concatenate([x, carry[s]], axis=0)
            keep[0] = ext[1:1 + tm]
            keep[1] = ext[2:2 + tm]
            x1, x2 = keep[0], keep[1]
            cwv = cw_ref[s]
            da = (cwv[2:3] * x + cwv[1:2] * x1 + cwv[0:1] * x2).astype(BF16)
            carry[s] = x[:HALO]
            da_ref[s] = da
            acc = acc + _dot_nt(da, wu_ref[s])
            av = a_ref[s].astype(F32)
            dcw_ref[s] += jnp.concatenate([rsum(x2 * av), rsum(x1 * av), rsum(x * av)], axis=0)
            dcb_ref[s] += rsum(x)
        hv = h_ref[...]
        r = _rstd(hv)
        gg = acc * g_ref[...]
        dh_new = di_ref[...].astype(F32) + r * gg - hv * (r * r * r * jnp.mean(gg * hv, axis=-1, keepdims=True))
        o_ref[...] = dh_new.astype(o_ref.dtype)
        dg_ref[...] += jnp.sum(acc * hv * r, axis=0, keepdims=True)

    slab = pl.BlockSpec((ns, tm, n), lambda i: (0, ni - 1 - i, 0))
    row = pl.BlockSpec((tm, D), lambda i: (ni - 1 - i, 0))
    vec = pl.BlockSpec((1, D), lambda i: (0, 0))
    whole = lambda shape: pl.BlockSpec(shape, lambda i: (0,) * len(shape))
    return _run(
        body, [dc, a, wup, cw, h, gain, dh_in], hook, grid=(ni,), name=name, semantics=("arbitrary",),
        in_specs=[slab, slab, _resident(wup.shape), _resident(cw.shape), row, vec, row],
        out_specs=[slab, row, vec, whole((ns, 3, n)), whole((ns, 1, n))],
        out_shape=[S((ns, t, n), BF16), S((t, D), DH), S((1, D), F32), S((ns, 3, n), F32), S((ns, 1, n), F32)],
        scratch_shapes=[pltpu.VMEM((ns, HALO, n), F32), pltpu.VMEM((2, tm, n), F32)])


def _dw_slot(hn, dy_s, name, hook=None):
    t, k = hn.shape
    ns, _, n = dy_s.shape
    tm = _tm(t)

    def body(a_ref, b_ref, o_ref, ob_ref, at_ref):
        @pl.when(pl.program_id(0) == 0)
        def _():
            for i in range(t // tm):
                at_ref[:, i * tm:(i + 1) * tm] = a_ref[i * tm:(i + 1) * tm, :].T

        acc = _dot(at_ref[...], b_ref[...])
        o_ref[...] = acc
        ob_ref[...] = acc.astype(BF16)

    ospec = pl.BlockSpec((None, k, n), lambda j: (j, 0, 0))
    return _run(
        body, [hn, dy_s], hook, grid=(ns,), name=name, semantics=("arbitrary",),
        in_specs=[_resident(hn.shape), pl.BlockSpec((None, t, n), lambda j: (j, 0, 0))],
        out_specs=[ospec, ospec], out_shape=[S((ns, k, n), F32), S((ns, k, n), BF16)],
        scratch_shapes=[pltpu.VMEM((k, t), BF16)])


def _dw_rows(a_s, dh, name, hook=None, fm=False):
    nk, t, kc = (1, a_s.shape[1], a_s.shape[0]) if fm else a_s.shape
    tm = _tm(t)
    ni = t // tm

    def body(a_ref, d_ref, o_ref, ob_ref):
        i = pl.program_id(0)
        dhb = d_ref[...].astype(BF16)

        @pl.when(i == 0)
        def _():
            o_ref[...] = jnp.zeros_like(o_ref)

        if fm:
            o_ref[...] += _dot(a_ref[...], dhb)
        for j in range(0 if fm else nk):
            o_ref[j * kc:(j + 1) * kc, :] += _dot_tn(a_ref[j], dhb)

        @pl.when(i == ni - 1)
        def _():
            ob_ref[...] = o_ref[...].astype(BF16)

    ospec = pl.BlockSpec((nk * kc, D), lambda i: (0, 0))
    return _run(
        body, [a_s, dh], hook, grid=(ni,), name=name, semantics=("arbitrary",),
        in_specs=[pl.BlockSpec((kc, tm), lambda i: (0, i)) if fm else pl.BlockSpec((nk, tm, kc), lambda i: (0, i, 0)),
                  pl.BlockSpec((tm, D), lambda i: (i, 0))],
        out_specs=[ospec, ospec], out_shape=[S((nk * kc, D), F32), S((nk * kc, D), BF16)])


def _dx_slot_normbwd(dy_s, wg, h, gain, dh_in, name, hook=None, fm=False, out_dtype=F32):
    ns, t, n = (1, dy_s.shape[1], dy_s.shape[0]) if fm else dy_s.shape
    tm = _tm(t)

    def body(dy_ref, w_ref, h_ref, g_ref, di_ref, o_ref, dg_ref):
        i = pl.program_id(0)

        @pl.when(i == 0)
        def _():
            dg_ref[...] = jnp.zeros_like(dg_ref)

        g = _dot_tn(dy_ref[...], w_ref[...]) if fm else _dot_nt(dy_ref[0], w_ref[0])
        for s in range(1, ns):
            g = g + _dot_nt(dy_ref[s], w_ref[s])
        hv = h_ref[...]
        r = _rstd(hv)
        gg = g * g_ref[...]
        dh_new = di_ref[...].astype(F32) + r * gg - hv * (r * r * r * jnp.mean(gg * hv, axis=-1, keepdims=True))
        o_ref[...] = dh_new.astype(o_ref.dtype)
        dg_ref[...] += jnp.sum(g * hv * r, axis=0, keepdims=True)

    row = pl.BlockSpec((tm, D), lambda i: (i, 0))
    vec = pl.BlockSpec((1, D), lambda i: (0, 0))
    return _run(
        body, [dy_s, wg, h, gain, dh_in], hook, grid=(t // tm,), name=name, semantics=("arbitrary",),
        in_specs=[pl.BlockSpec((n, tm), lambda i: (0, i)) if fm else pl.BlockSpec((ns, tm, n), lambda i: (0, i, 0)),
                  _resident(wg.shape), row, vec, row],
        out_specs=[row, vec], out_shape=[S((t, D), out_dtype), S((1, D), F32)])


def _sgu_gate_bwd(a_s, dg_s, vgain, ws, bst, name, hook=None):
    t = a_s.shape[1]
    sw = a_s.shape[2]
    gps = sw // CHUNK

    def body(a_ref, dg_ref, vg_ref, ws_ref, b_ref, da_ref, dws_ref, dbt_ref, dvg_ref, dvn_ref):
        n = pl.program_id(0)

        @pl.when(n == 0)
        def _():
            dws_ref[...] = jnp.zeros_like(dws_ref)
            dbt_ref[...] = jnp.zeros_like(dbt_ref)
            dvg_ref[...] = jnp.zeros_like(dvg_ref)

        vpre = jnp.concatenate([a_ref[4 + s].astype(F32) for s in range(4)], axis=1)
        v, v_grad = _gelu_and_grad(vpre)
        r = _rstd(v)
        vhat = v * r
        vn = (vhat * vg_ref[...]).astype(BF16)
        tri = _tril_mask()
        lane = lax.broadcasted_iota(jnp.int32, (CHUNK, CHUNK), 1)
        dbt = jnp.zeros((CHUNK, CHUNK), F32)
        for g in range(SGU_G):
            w = jnp.where(tri, ws_ref[g], 0.0).astype(BF16)
            vng = vn[:, g * CHUNK:(g + 1) * CHUNK]
            sg = _dot(w, vng) + b_ref[:, g:g + 1]
            lo = (g % gps) * CHUNK
            u, u_grad = _gelu_and_grad(a_ref[g // gps, :, lo:lo + CHUNK].astype(F32))
            dgate = dg_ref[g // gps, :, lo:lo + CHUNK].astype(F32)
            da_ref[g // gps, :, lo:lo + CHUNK] = (dgate * sg * u_grad).astype(BF16)
            ds = dgate * u
            dsb = ds.astype(BF16)
            dvn_ref[:, g * CHUNK:(g + 1) * CHUNK] = _dot_tn(w, dsb)
            dws_ref[g] += jnp.where(tri, _dot_nt(dsb, vng), 0.0)
            dbt = dbt + jnp.where(lane == g, jnp.sum(ds, axis=-1, keepdims=True), 0.0)
        dbt_ref[...] += dbt
        dvn = dvn_ref[...]
        dvg_ref[...] += jnp.sum(dvn * vhat, axis=0, keepdims=True)
        gg = dvn * vg_ref[...]
        dv = r * gg - v * (r * r * r * jnp.mean(gg * v, axis=-1, keepdims=True))
        dav = (dv * v_grad).astype(BF16)
        for s in range(4):
            da_ref[4 + s] = dav[:, s * sw:(s + 1) * sw]

    return _run(
        body, [a_s, dg_s, vgain, ws, bst], hook, grid=(t // CHUNK,), name=name, semantics=("arbitrary",),
        in_specs=[pl.BlockSpec((8, CHUNK, sw), lambda n: (0, n, 0)), pl.BlockSpec((4, CHUNK, sw), lambda n: (0, n, 0)),
                  pl.BlockSpec((1, SGU_W), lambda n: (0, 0)), pl.BlockSpec((SGU_G, CHUNK, CHUNK), lambda n: (0, 0, 0)),
                  pl.BlockSpec((CHUNK, SGU_G), lambda n: (0, 0))],
        out_specs=[pl.BlockSpec((8, CHUNK, sw), lambda n: (0, n, 0)), pl.BlockSpec((SGU_G, CHUNK, CHUNK), lambda n: (0, 0, 0)),
                   pl.BlockSpec((CHUNK, CHUNK), lambda n: (0, 0)), pl.BlockSpec((1, SGU_W), lambda n: (0, 0))],
        out_shape=[S((8, t, sw), BF16), S((SGU_G, CHUNK, CHUNK), F32), S((CHUNK, CHUNK), F32), S((1, SGU_W), F32)],
        scratch_shapes=[pltpu.VMEM((CHUNK, SGU_W), F32)])


def _attn_bwd(qkv_t, do_t, qg, kg, sinks, bias, name, hook=None):
    t = qkv_t.shape[1]
    nb = t // CHUNK

    def body(cur_ref, prev_ref, do_ref, qg_ref, kg_ref, sink_ref, bias_ref,
             o_ref, dqg_out, dkg_out, dsk_out, dbias_ref, carry, dqg_ref, dkg_ref, dsk_ref):
        n = pl.program_id(0)

        @pl.when(n == 0)
        def _():
            carry[...] = jnp.zeros_like(carry)
            dqg_ref[...] = jnp.zeros_like(dqg_ref)
            dkg_ref[...] = jnp.zeros_like(dkg_ref)
            dsk_ref[...] = jnp.zeros_like(dsk_ref)
            dbias_ref[...] = jnp.zeros_like(dbias_ref)

        @pl.when(n < nb)
        def _():
            valid = _attn_valid(n)
            o_ref[0:KV0, :] = carry[0:KV0, :].astype(BF16)
            kvs, heads = range(NKV), range(NH)
            group = lambda h: range(KVG * h, KVG * (h + 1))
            ks = [_attn_band(cur_ref, prev_ref, KV0 + HD * h) for h in kvs]
            rks = [_rstd_rows(k) for k in ks]
            khats = [k * rk for k, rk in zip(ks, rks)]
            kns = [(khat * kg_ref[...]).astype(BF16) for khat in khats]
            kn_toks = [kn.T for kn in kns]
            vbs = [_attn_band(cur_ref, prev_ref, KV0 + HD * (NKV + h)).astype(BF16) for h in kvs]
            v_toks = [vb.T for vb in vbs]
            qs = [cur_ref[HD * hq:HD * (hq + 1), :] for hq in heads]
            rqs = [_rstd_rows(q) for q in qs]
            qhats = [q * rq for q, rq in zip(qs, rqs)]
            qns = [(qhat * qg_ref[...]).astype(BF16) for qhat in qhats]
            probs = [_attn_probs(kn_toks[hq // KVG], qns[hq], bias_ref[hq], valid, sink_ref[hq]) for hq in heads]
            dohs = [do_ref[HD * hq:HD * (hq + 1), :] for hq in heads]
            dps = [_dot(v_toks[hq // KVG], dohs[hq]) for hq in heads]
            dsums = [jnp.sum(p * dp, axis=0, keepdims=True) for (p, _), dp in zip(probs, dps)]
            dss = [p * (dp - dsum) for (p, _), dp, dsum in zip(probs, dps, dsums)]
            for hq in heads:
                dsk_ref[hq:hq + 1, :] -= probs[hq][1] * dsums[hq]
                dbias_ref[hq] += dss[hq]
            dvs = [sum(_dot_nt(dohs[hq], probs[hq][0].astype(BF16)) for hq in group(h)) for h in kvs]
            dscs = [(ds * (HD ** -0.5)).astype(BF16) for ds in dss]
            dqns = [_dot(kns[hq // KVG], dscs[hq]) for hq in heads]
            dkns = [sum(_dot_nt(qns[hq], dscs[hq]) for hq in group(h)) for h in kvs]
            dqg_ref[...] += sum(dqn * qhat for dqn, qhat in zip(dqns, qhats))
            for hq in heads:
                gq = dqns[hq] * qg_ref[...]
                carry[HD * hq:HD * (hq + 1), :] = rqs[hq] * gq - qs[hq] * (
                    rqs[hq] * rqs[hq] * rqs[hq] * jnp.mean(gq * qs[hq], axis=0, keepdims=True))
            dkg_ref[...] += sum(dkn * khat for dkn, khat in zip(dkns, khats))
            for h in kvs:
                krow, vrow = KV0 + HD * h, KV0 + HD * (NKV + h)
                gk = dkns[h] * kg_ref[...]
                dk = rks[h] * gk - ks[h] * (rks[h] * rks[h] * rks[h] * jnp.mean(gk * ks[h], axis=0, keepdims=True))
                o_ref[krow:krow + HD, :] = (carry[krow:krow + HD, :] + dk[:, :CHUNK]).astype(BF16)
                o_ref[vrow:vrow + HD, :] = (carry[vrow:vrow + HD, :] + dvs[h][:, :CHUNK]).astype(BF16)
                carry[krow:krow + HD, :] = dk[:, CHUNK:]
                carry[vrow:vrow + HD, :] = dvs[h][:, CHUNK:]

        @pl.when(n == nb)
        def _():
            o_ref[...] = carry[...].astype(BF16)
            dqg_out[...] = jnp.sum(dqg_ref[...], axis=1, keepdims=True)
            dkg_out[...] = jnp.sum(dkg_ref[...], axis=1, keepdims=True)
            dsk_out[...] = jnp.sum(dsk_ref[...], axis=1, keepdims=True)

    cur = lambda n: (0, jnp.minimum(n, nb - 1))
    col = pl.BlockSpec((HD, 1), lambda n: (0, 0))
    whole = lambda shape: pl.BlockSpec(shape, lambda n: (0,) * len(shape))
    return _run(
        body, [qkv_t, qkv_t, do_t, qg, kg, sinks, bias], hook, grid=(nb + 1,), name=name, semantics=("arbitrary",),
        in_specs=[pl.BlockSpec((QKV, CHUNK), cur),
                  pl.BlockSpec((QKV - KV0, CHUNK), lambda n: (KV0 // (QKV - KV0), jnp.clip(n - 1, 0, nb - 1))),
                  pl.BlockSpec((D, CHUNK), cur), col, col, pl.BlockSpec(memory_space=pltpu.SMEM), whole((NH, 2 * CHUNK, CHUNK))],
        out_specs=[pl.BlockSpec((QKV, CHUNK), lambda n: (0, jnp.maximum(n - 1, 0))), whole((HD, 1)), whole((HD, 1)),
                   whole((NH, 1)), whole((NH, 2 * CHUNK, CHUNK))],
        out_shape=[S((QKV, t), BF16), S((HD, 1), F32), S((HD, 1), F32), S((NH, 1), F32), S((NH, 2 * CHUNK, CHUNK), F32)],
        scratch_shapes=[pltpu.VMEM((QKV, CHUNK), F32), pltpu.VMEM((HD, CHUNK), F32), pltpu.VMEM((HD, 2 * CHUNK), F32),
                        pltpu.VMEM((NH, CHUNK), F32)])


class _Plain:
    def __init__(self, wg):
        self.full, self.grads = wg, {}

    def w(self, n):
        return self.full[n]

    def hook(self, host):
        return None

    def grad(self, n, pair):
        self.grads[n] = pair

    def small(self, g_rep):
        pass

    def sync(self, point):
        pass

    def first_norm(self, x, gain):
        return _rmsnorm(x, gain, "norm0")


def _local_step(x, target, rep, sch):
    bucket_row = jnp.asarray(_rel_tables().T.reshape(1, -1))
    bias = _relbias_fwd(rep["rel_bias"].T, bucket_row, "relbias_fwd").reshape(NH, 2 * CHUNK, CHUNK)
    bst = rep["sgu_b_s"][0].T
    ws = rep["sgu_w_s"][0]
    vgain = rep["sgu_v_gain"]
    qg, kg, sinks = rep["attn_q_gain"].reshape(HD, 1), rep["attn_k_gain"].reshape(HD, 1), rep["attn_sinks"][0]
    w_down = lambda l: sch.w("ffn_w_down%d" % l).reshape(D_FF, D)
    w_up = lambda l: sch.w("ffn_w_up%d" % l)
    cb = [rep["ffn_conv_b"][l].reshape(8, 1, -1) for l in range(2)]
    mixg = [rep["mix_norm"][l:l + 1] for l in range(2)]
    ffng = [rep["ffn_norm"][l:l + 1] for l in range(2)]
    rows = lambda pair: tuple(g.reshape(N_DEV, -1, D) for g in pair)
    hk = sch.hook

    hn0 = sch.first_norm(x, mixg[0])
    cw = [sch.w("ffn_conv_w")[:, 3 * l:3 * l + 3] for l in range(2)]
    a0 = _mm_slot(hn0, sch.w("sgu_w_in"), BF16, "sgu_in", hk("sgu_in"))
    gated = _sgu_gate_fwd(a0, vgain, ws, bst, "sgu_gate", hk("sgu_gate"))
    h1, hn1 = _resid_mm(gated, sch.w("sgu_w_out").reshape(SGU_W, D), x, ffng[0], "norm", "sgu_out", hk("sgu_out"))
    sch.sync("before_ffn0")
    a_ff0, c_ff0, h2, hn2 = _ffn_fwd(hn1, h1, w_up(0), w_down(0), cw[0], cb[0], mixg[1], "norm", "ffn0_fwd", hk("ffn0_fwd"))
    qkv = _mm_t(hn2, sch.w("attn_w_qkv"), "qkv", hk("qkv"))
    o = _attn_fwd(qkv, qg, kg, sinks, bias, "attn", hk("attn"))
    h3, hn3 = _resid_mm(o, sch.w("attn_w_o").reshape(D, D), h2, ffng[1], "norm", "attn_out", hk("attn_out"), fm=True)
    a_ff1, c_ff1, dy, sq = _ffn_fwd(hn3, h3, w_up(1), w_down(1), cw[1], cb[1], target, "loss", "ffn1_fwd_loss", hk("ffn1_fwd_loss"))
    loss = (0.5 / D) * jnp.sum(sq[:, 0, 0])

    def ffn_bwd(dh, h_in, hn, a, c, l, tag):
        dc, g_down, g_down_b = _ffn_bwd1(dh, c, w_down(l), tag + "_bwd1", hk(tag + "_bwd1"))
        sch.grad("ffn_w_down%d" % l, rows((g_down, g_down_b)))
        da, dh_new, dgain, g_cw, g_cb = _ffn_bwd2(dc, a, w_up(l), cw[l], h_in, ffng[l], dh, tag + "_bwd2", hk(tag + "_bwd2"))
        sch.grad("ffn_w_up%d" % l, _dw_slot(hn, da, tag + "_dw_up", hk(tag + "_dw_up")))
        return dh_new, dgain, g_cw, g_cb.reshape(-1)

    dh, d_ffng1, g_cw1, g_cb1 = ffn_bwd(dy, h3, hn3, a_ff1, c_ff1, 1, "ffn1")
    do = _dx_rows_t(dh, sch.w("attn_w_o").reshape(D, D), "attn_do", hk("attn_do"))
    sch.grad("attn_w_o", rows(_dw_rows(o, dh, "dw_o", hk("dw_o"), fm=True)))
    dqkv, d_qg, d_kg, d_sk, d_bias = _attn_bwd(qkv, do, qg, kg, sinks, bias, "attn_bwd", hk("attn_bwd"))
    sch.grad("attn_w_qkv", tuple(g.reshape(N_DEV, -1, D) for g in _dw_rows(dqkv, hn2, "dw_qkv", hk("dw_qkv"), fm=True)))
    dh, d_mixg1 = _dx_slot_normbwd(dqkv, sch.w("attn_w_qkv").reshape(QKV, D), h2, mixg[1], dh, "dx_qkv", hk("dx_qkv"), fm=True,
                                   out_dtype=DH)
    d_relb = _relbias_bwd(d_bias.reshape(NH, -1), bucket_row, "relbias_bwd").T
    g_rep = {"attn_q_gain": d_qg.reshape(1, HD), "attn_k_gain": d_kg.reshape(1, HD), "attn_sinks": d_sk.reshape(1, NH),
             "rel_bias": d_relb}
    sch.small(g_rep)
    dh, d_ffng0, g_cw0, g_cb0 = ffn_bwd(dh, h1, hn1, a_ff0, c_ff0, 0, "ffn0")
    g_cw = jnp.concatenate([g_cw0, g_cw1], axis=1)
    sch.grad("ffn_conv_w", (g_cw, g_cw.astype(BF16)))
    g_ffn = {"ffn_norm": jnp.concatenate([d_ffng0, d_ffng1], axis=0), "ffn_conv_b": jnp.stack([g_cb0, g_cb1], axis=0)}
    sch.small(g_ffn)
    dgated = _dx_rows(dh, sch.w("sgu_w_out").reshape(SGU_W, D), SGU_W // 4, BF16, "sgu_dgated", hk("sgu_dgated"))
    sch.grad("sgu_w_out", rows(_dw_rows(gated, dh, "dw_sgu_out", hk("dw_sgu_out"))))
    da0, d_ws, d_bst, d_vgain = _sgu_gate_bwd(a0, dgated, vgain, ws, bst, "sgu_gate_bwd", hk("sgu_gate_bwd"))
    g_sgu = {"sgu_v_gain": d_vgain, "sgu_w_s": d_ws[None], "sgu_b_s": d_bst[:, :SGU_G].T[None]}
    sch.small(g_sgu)
    sch.grad("sgu_w_in", _dw_slot(hn0, da0, "dw_sgu_in", hk("dw_sgu_in")))
    sch.sync("after_dw")
    grad_x, d_mixg0 = _dx_slot_normbwd(da0, sch.w("sgu_w_in"), x, mixg[0], dh, "dx_sgu_in", hk("dx_sgu_in"))
    g_mix = {"mix_norm": jnp.concatenate([d_mixg0, d_mixg1], axis=0)}
    sch.small(g_mix)
    for g in (g_ffn, g_sgu, g_mix):
        g_rep.update(g)
    return loss, grad_x, g_rep


def _allgather(xs, x_in, gain, name):
    nt = len(xs)
    t_rows = x_in.shape[0]
    tm = _tm(t_rows)

    def body(xin_ref, g_ref, *refs):
        x_refs, hn_ref, o_refs = refs[:nt], refs[nt], refs[nt + 1:2 * nt + 1]
        send_sems, recv_sems, local_sems = refs[2 * nt + 1:]
        x, y, c, chips = _place()
        me, sibling = (x, y, c), (x, y, 1 - c)

        def copy(t, k, block, to, src=None):
            px, py, pc = block
            dst = o_refs[t].at[4 * px + 2 * py + pc]
            return pltpu.make_async_remote_copy(
                src_ref=dst if src is None else src, dst_ref=dst, send_sem=send_sems.at[t, k], recv_sem=recv_sems.at[t, k],
                device_id=to, device_id_type=MESH)

        mine = lambda: [pltpu.make_async_copy(x_refs[t], o_refs[t].at[4 * x + 2 * y + c], local_sems.at[t]) for t in range(nt)]

        def first():
            out = []
            for t in range(nt):
                out.append(copy(t, 0, me, sibling, src=x_refs[t]))
                out += [copy(t, 1 + j, me, (*chip, c), src=x_refs[t]) for j, chip in enumerate(chips)]
            return out

        @pl.when(pl.program_id(0) == 0)
        def _():
            for cp in mine() + first():
                cp.start()

        xv = xin_ref[...]
        hn_ref[...] = (xv * _rstd(xv) * g_ref[...]).astype(BF16)

        @pl.when(pl.program_id(0) == pl.num_programs(0) - 1)
        def _():
            passed = []
            for j, chip in enumerate(chips):
                for t in range(nt):
                    copy(t, 1 + j, (*chip, c), me).wait_recv()
                    fwd = copy(t, 4 + j, (*chip, c), sibling)
                    fwd.start()
                    passed.append(fwd)
            for t in range(nt):
                copy(t, 0, sibling, me).wait_recv()
                for j, chip in enumerate(chips):
                    copy(t, 4 + j, (*chip, 1 - c), me).wait_recv()
            for cp in first() + passed:
                cp.wait_send()
            for cp in mine():
                cp.wait()

    res = pl.pallas_call(
        body, name=name, grid=(t_rows // tm,),
        in_specs=[pl.BlockSpec((tm, D), lambda i: (i, 0)), pl.BlockSpec((1, D), lambda i: (0, 0))] + [ANY] * nt,
        out_specs=[pl.BlockSpec((tm, D), lambda i: (i, 0))] + [ANY] * nt,
        out_shape=[S((t_rows, D), BF16)] + [S((N_DEV,) + a.shape, a.dtype) for a in xs],
        scratch_shapes=[pltpu.SemaphoreType.DMA((nt, 7)), pltpu.SemaphoreType.DMA((nt, 7)), pltpu.SemaphoreType.DMA((nt,))],
        compiler_params=pltpu.CompilerParams(dimension_semantics=("arbitrary",), has_side_effects=True))(x_in, gain, *xs)
    return res[0], res[1:]


def _exchange(hook, name):
    comm = hook()
    ci, co = len(comm.inputs), len(comm.out_shapes)

    def body(*refs):
        cins, couts = refs[:ci], refs[ci:ci + co]
        send, recv = refs[-2:]
        comm.start(cins, couts, send, recv)
        comm.finish(cins, couts, send, recv)

    res = pl.pallas_call(
        body, name=name, in_specs=[ANY] * ci, out_specs=[ANY] * co, out_shape=comm.out_shapes,
        scratch_shapes=[pltpu.SemaphoreType.DMA((comm.n_sems,)), pltpu.SemaphoreType.DMA((comm.n_sems,))],
        input_output_aliases=dict(comm.aliases),
        compiler_params=pltpu.CompilerParams(has_side_effects=True))(*comm.inputs)
    hook(res)


def _row_tile(r):
    tr = r if r <= ROW_TILE or r % ROW_TILE else ROW_TILE
    assert r % tr == 0
    return tr


def _rs_partial(g32, sib, place, name):
    _, r, cdim = g32.shape
    tr = _row_tile(r)

    def body(place_ref, g_ref, s_ref, p_ref, own_ref):
        k = pl.program_id(1)
        tot = g_ref[...] + s_ref[...].astype(F32)
        p_ref[...] = tot.astype(BF16)

        @pl.when(k == place_ref[1])
        def _():
            own_ref[...] = tot

    grid_spec = pltpu.PrefetchScalarGridSpec(
        num_scalar_prefetch=1, grid=(r // tr, 4),
        in_specs=[pl.BlockSpec((None, None, tr, cdim), lambda i, k, pr: (k, pr[0], i, 0)),
                  pl.BlockSpec((None, tr, cdim), lambda i, k, pr: (k, i, 0))],
        out_specs=[pl.BlockSpec((None, tr, cdim), lambda i, k, pr: (k, i, 0)), pl.BlockSpec((tr, cdim), lambda i, k, pr: (i, 0))])
    return pl.pallas_call(
        body, grid_spec=grid_spec, name=name,
        out_shape=[S((4, r, cdim), BF16), S((r, cdim), F32)],
        compiler_params=_cp("parallel", "arbitrary"))(place, g32.reshape(4, 2, r, cdim), sib)


def _rs1_partial(g32, g16, place, hook, name):
    _, r, cdim = g32.shape
    tr = _row_tile(r)
    nr = r // tr
    comm = hook()
    ci, co = len(comm.inputs), len(comm.out_shapes)

    def body(place_ref, g_ref, g16_ref, *refs):
        cins, refs = refs[:ci], refs[ci:]
        p_ref, own_ref, land_ref = refs[:3]
        couts, (sbuf, obuf, send, recv, lsem, csend, crecv) = refs[3:3 + co], refs[3 + co:]
        k, i = pl.program_id(0), pl.program_id(1)
        x, y, c, _ = _place()

        def rc(j, slot, to):
            return pltpu.make_async_remote_copy(src_ref=g16_ref.at[slot], dst_ref=land_ref.at[j], send_sem=send.at[j],
                                                recv_sem=recv.at[j], device_id=to, device_id_type=MESH)

        @pl.when((k == 0) & (i == 0))
        def _():
            for j in range(4):
                rc(j, 2 * j + (1 - c), (x, y, 1 - c)).start()
            comm.start(cins, couts, csend, crecv)

        for j in range(4):
            @pl.when((k == j) & (i == 0))
            def _(j=j):
                rc(j, 2 * j + c, (x, y, c)).wait_recv()

        rows = pl.ds(pl.multiple_of(i * tr, tr), tr)
        fetch = pltpu.make_async_copy(land_ref.at[k, rows], sbuf, lsem)
        fetch.start()
        fetch.wait()
        tot = g_ref[...] + sbuf[...].astype(F32)
        p_ref[...] = tot.astype(BF16)

        @pl.when(k == place_ref[1])
        def _():
            obuf[...] = tot
            keep = pltpu.make_async_copy(obuf, own_ref.at[rows], lsem)
            keep.start()
            keep.wait()

        @pl.when((k == 3) & (i == nr - 1))
        def _():
            for j in range(4):
                rc(j, 2 * j + (1 - c), (x, y, 1 - c)).wait_send()
            comm.finish(cins, couts, csend, crecv)

    grid_spec = pltpu.PrefetchScalarGridSpec(
        num_scalar_prefetch=1, grid=(4, nr),
        in_specs=[pl.BlockSpec((None, None, tr, cdim), lambda k, i, pr: (k, pr[0], i, 0)), ANY] + [ANY] * ci,
        out_specs=[pl.BlockSpec((None, tr, cdim), lambda k, i, pr: (k, i, 0)), ANY, ANY] + [ANY] * co,
        scratch_shapes=[pltpu.VMEM((tr, cdim), BF16), pltpu.VMEM((tr, cdim), F32), pltpu.SemaphoreType.DMA((4,)),
                        pltpu.SemaphoreType.DMA((4,)), pltpu.SemaphoreType.DMA(()), pltpu.SemaphoreType.DMA((comm.n_sems,)),
                        pltpu.SemaphoreType.DMA((comm.n_sems,))])
    res = pl.pallas_call(
        body, grid_spec=grid_spec, name=name,
        out_shape=[S((4, r, cdim), BF16), S((r, cdim), F32), S((4, r, cdim), BF16)] + comm.out_shapes,
        input_output_aliases={3 + k: 3 + v for k, v in comm.aliases.items()},
        compiler_params=pltpu.CompilerParams(dimension_semantics=("arbitrary", "arbitrary"), has_side_effects=True))(
            place, g32.reshape(4, 2, r, cdim), g16, *comm.inputs)
    hook(res[3:])
    return res[0], res[1]


def _adamw_math(w, g, m, v):
    m = ADAM_B1 * m + (1.0 - ADAM_B1) * g
    v = ADAM_B2 * v + (1.0 - ADAM_B2) * (g * g)
    m_hat = m / (1.0 - ADAM_B1 ** ADAM_STEP)
    v_hat = v / (1.0 - ADAM_B2 ** ADAM_STEP)
    delta = -ADAM_LR * (m_hat / (jnp.sqrt(v_hat) + ADAM_EPS) + ADAM_WD * w)
    return delta, m, v


def _adamw_shard(owns, recvs, w, m, v, name, flipped=False):
    nl = w.shape[0]
    r, cdim = owns[0].shape
    tr = _row_tile(r)
    nr = r // tr

    def body(*refs):
        own_refs, recv_refs = refs[:nl], refs[nl:2 * nl]
        w_ref, m_ref, v_ref, g_out, d_out, m_out, v_out = refs[2 * nl:]
        layer = pl.program_id(0)
        g = None
        for l in range(nl):
            gl = own_refs[l][...] + recv_refs[l][0].astype(F32) + recv_refs[l][1].astype(F32) + recv_refs[l][2].astype(F32)
            g = gl if g is None else jnp.where(layer == l, gl, g)
        if flipped:
            g = g.T
        g_out[...] = g
        d_out[...], m_out[...], v_out[...] = _adamw_math(w_ref[...], g, m_ref[...], v_ref[...])

    park = lambda l: (lambda layer, i: (jnp.where(layer == l, i, jnp.where(layer < l, 0, nr - 1)), 0))
    park3 = lambda l: (lambda layer, i: (0, jnp.where(layer == l, i, jnp.where(layer < l, 0, nr - 1)), 0))
    if flipped:
        row = pl.BlockSpec((None, cdim, tr), lambda layer, i: (layer, 0, i))
    else:
        row = pl.BlockSpec((None, tr, cdim), lambda layer, i: (layer, i, 0))
    return pl.pallas_call(
        body, grid=(nl, nr), name=name,
        in_specs=[pl.BlockSpec((tr, cdim), park(l)) for l in range(nl)] + [pl.BlockSpec((3, tr, cdim), park3(l)) for l in range(nl)]
        + [row, row, row],
        out_specs=[row] * 4, out_shape=[S(w.shape, F32)] * 4,
        compiler_params=_cp("arbitrary", "arbitrary"))(*owns, *recvs, w, m, v)


def _adamw_small(galls, ws, ms, vs, name):
    n = len(galls)

    def body(*refs):
        g_refs, w_refs, m_refs, v_refs, outs = refs[:n], refs[n:2 * n], refs[2 * n:3 * n], refs[3 * n:4 * n], refs[4 * n:]
        for i in range(n):
            g = g_refs[i][0].astype(F32)
            for s in range(1, N_DEV):
                g = g + g_refs[i][s].astype(F32)
            outs[i][...] = g
            outs[n + i][...], outs[2 * n + i][...], outs[3 * n + i][...] = _adamw_math(w_refs[i][...], g, m_refs[i][...], v_refs[i][...])

    res = pl.pallas_call(body, out_shape=[S(a.shape, F32) for a in ws] * 4, name=name)(*galls, *ws, *ms, *vs)
    return [res[k * n:(k + 1) * n] for k in range(4)]


REPLICATED = ["mix_norm", "ffn_norm", "sgu_v_gain", "sgu_w_s", "sgu_b_s", "attn_q_gain", "attn_k_gain", "attn_sinks", "rel_bias",
              "ffn_conv_b"]
WEIGHTS = ["mix_norm", "ffn_norm", "sgu_w_in", "sgu_v_gain", "sgu_w_s", "sgu_b_s", "sgu_w_out", "attn_w_qkv", "attn_q_gain",
           "attn_k_gain", "attn_sinks", "attn_w_o", "rel_bias", "ffn_w_up", "ffn_conv_w", "ffn_conv_b", "ffn_w_down"]
SMALL = ["g_" + n for n in REPLICATED]
BF16_TRANSIT = {"sgu_w_s"}
SMALL_ATTN = ["g_attn_q_gain", "g_attn_k_gain", "g_attn_sinks", "g_rel_bias"]
SMALL_FFN = ["g_ffn_norm", "g_ffn_conv_b"]
SMALL_SGU = ["g_sgu_v_gain", "g_sgu_w_s", "g_sgu_b_s"]

GATHER_FIRST = ["sgu_w_in", "ffn_conv_w"]
UP0_SPLIT, UP1_SPLIT = 352, 352
RS1_WITH_SUMS = {"after_dw": "sgu_w_in"}
PLAN = {
    "sgu_in": [("ag1", "sgu_w_out"), ("ag1", "ffn_w_up0", (0, UP0_SPLIT))],
    "sgu_gate": [("ag2", "sgu_w_out"), ("ag1", "ffn_w_up0", (UP0_SPLIT, D))],
    "sgu_out": [("ag2", "ffn_w_up0"), ("ag1", "ffn_w_down0")],
    "before_ffn0": [("ag2", "ffn_w_down0")],
    "ffn0_fwd": [("agd", "attn_w_qkv"), ("ag1", "attn_w_o"), ("ag1", "ffn_w_down1"), ("ag1", "ffn_w_up1", (0, UP1_SPLIT))],
    "attn": [("ag2", "attn_w_o"), ("ag2", "ffn_w_down1"), ("ag1", "ffn_w_up1", (UP1_SPLIT, D))],
    "attn_out": [("ag2", "ffn_w_up1")],
    "attn_bwd": [("rs1", "ffn_w_down1"), ("rs1", "ffn_w_up1"), ("rs1", "attn_w_o")],
    "ffn0_bwd1": [("rs2", "ffn_w_down1"), ("rs2", "attn_w_o"), ("rs1", "attn_w_qkv")] + [("ag1", n) for n in SMALL_ATTN],
    "ffn0_bwd2": [("rs2", "ffn_w_up1"), ("rs2", "attn_w_qkv"), ("rs1", "ffn_w_down0")] + [("ag2", n) for n in SMALL_ATTN],
    "ffn0_dw_up": [("rs2", "ffn_w_down0")],
    "sgu_dgated": [("rs1", "ffn_w_up0")] + [("agd", n) for n in SMALL_FFN],
    "sgu_gate_bwd": [("rs2", "ffn_w_up0"), ("rs1", "sgu_w_out")],
    "dw_sgu_in": [("rs2", "sgu_w_out")] + [("ag1", n) for n in SMALL_SGU],
    "after_dw": [("rs1", "ffn_conv_w")] + [("ag2", n) for n in SMALL_SGU],
    "dx_sgu_in": [("rs2", "sgu_w_in"), ("rs2", "ffn_conv_w")],
    "last": [("agd", "g_mix_norm")],
}


class _Overlap:
    def __init__(self, shard, place):
        self.shard, self.place = shard, place
        self.part, self.full = {}, {}
        self.grads, self.sib, self.own, self.recv, self.sums = {}, {}, {}, {}, {}

    def w(self, n):
        return self.full[n]

    def grad(self, n, pair):
        self.grads[n] = pair

    def small(self, g_rep):
        self.shard.update(("g_" + n, a.astype(BF16) if n in BF16_TRANSIT else a) for n, a in _views2d(g_rep).items())

    def sync(self, point):
        if point in RS1_WITH_SUMS:
            n = RS1_WITH_SUMS[point]
            self.sums[n], self.own[n] = _rs1_partial(*self.grads[n], self.place, self.hook(point), point)
        else:
            _exchange(self.hook(point), point)

    def first_norm(self, x, gain):
        hn, full = _allgather([self.shard[n] for n in GATHER_FIRST], x, gain, "gather_first")
        self.full.update(zip(GATHER_FIRST, full))
        return hn

    def chip_sums(self, n):
        if n in self.sums:
            return self.sums.pop(n)
        sums, self.own[n] = _rs_partial(self.grads[n][0], self.sib.pop(n), self.place, "rs_partial_" + n)
        return sums

    def hook(self, host):
        ops = PLAN.get(host)
        if not ops:
            return None
        where = {"ag1": self.part, "ag2": self.full, "agd": self.full, "rs1": self.sib, "rs2": self.recv}
        idx = []

        def hook(results=None):
            if results is not None:
                for (kind, n, *_), i in zip(ops, idx):
                    where[kind][n] = results[i]
                return None
            comm = _Comm()
            for kind, n, *rows in ops:
                arr = {"ag1": lambda: self.shard[n], "agd": lambda: self.shard[n], "ag2": lambda: self.part.pop(n),
                       "rs1": lambda: self.grads[n][1], "rs2": lambda: self.chip_sums(n)}[kind]()
                idx.append(comm.add(kind, arr, *rows, into=self.part.pop(n) if rows and rows[0][0] else None))
            return comm

        return hook


TRANSPOSED = {"attn_w_qkv"}
PHYSICAL_T = {"ffn_w_up"}
SHARDED = {
    "sgu_w_in": ["sgu_w_in"], "sgu_w_out": ["sgu_w_out"], "attn_w_qkv": ["attn_w_qkv"], "attn_w_o": ["attn_w_o"],
    "ffn_w_up": ["ffn_w_up0", "ffn_w_up1"], "ffn_w_down": ["ffn_w_down0", "ffn_w_down1"], "ffn_conv_w": ["ffn_conv_w"],
}


def _send_views(w):
    out = {"ffn_conv_w": w["ffn_conv_w"].reshape(6, -1)}
    for name, parts in SHARDED.items():
        if name != "ffn_conv_w":
            out.update((p, (w[name][l].T if name in TRANSPOSED else w[name][l]).astype(BF16)) for l, p in enumerate(parts))
    return out


def _views2d(d):
    return {n: d[n].reshape(-1, d[n].shape[-1]) for n in REPLICATED if n in d}


def kernel(x, mix_norm, ffn_norm, sgu_w_in, sgu_v_gain, sgu_w_s, sgu_b_s, sgu_w_out, attn_w_qkv, attn_q_gain, attn_k_gain, attn_sinks, attn_w_o, rel_bias, ffn_w_up, ffn_conv_w, ffn_conv_b, ffn_w_down, loss_target, m_mix_norm, m_ffn_norm, m_sgu_w_in, m_sgu_v_gain, m_sgu_w_s, m_sgu_b_s, m_sgu_w_out, m_attn_w_qkv, m_attn_q_gain, m_attn_k_gain, m_attn_sinks, m_attn_w_o, m_rel_bias, m_ffn_w_up, m_ffn_conv_w, m_ffn_conv_b, m_ffn_w_down, v_mix_norm, v_ffn_norm, v_sgu_w_in, v_sgu_v_gain, v_sgu_w_s, v_sgu_b_s, v_sgu_w_out, v_attn_w_qkv, v_attn_q_gain, v_attn_k_gain, v_attn_sinks, v_attn_w_o, v_rel_bias, v_ffn_w_up, v_ffn_conv_w, v_ffn_conv_b, v_ffn_w_down):
    w = dict(zip(WEIGHTS, (mix_norm, ffn_norm, sgu_w_in, sgu_v_gain, sgu_w_s, sgu_b_s, sgu_w_out, attn_w_qkv, attn_q_gain, attn_k_gain,
                           attn_sinks, attn_w_o, rel_bias, ffn_w_up, ffn_conv_w, ffn_conv_b, ffn_w_down)))
    m = dict(zip(WEIGHTS, (m_mix_norm, m_ffn_norm, m_sgu_w_in, m_sgu_v_gain, m_sgu_w_s, m_sgu_b_s, m_sgu_w_out, m_attn_w_qkv, m_attn_q_gain,
                           m_attn_k_gain, m_attn_sinks, m_attn_w_o, m_rel_bias, m_ffn_w_up, m_ffn_conv_w, m_ffn_conv_b, m_ffn_w_down)))
    v = dict(zip(WEIGHTS, (v_mix_norm, v_ffn_norm, v_sgu_w_in, v_sgu_v_gain, v_sgu_w_s, v_sgu_b_s, v_sgu_w_out, v_attn_w_qkv, v_attn_q_gain,
                           v_attn_k_gain, v_attn_sinks, v_attn_w_o, v_rel_bias, v_ffn_w_up, v_ffn_conv_w, v_ffn_conv_b, v_ffn_w_down)))
    rep = {n: w[n] for n in REPLICATED}

    xi, yi, ci = lax.axis_index("x"), lax.axis_index("y"), lax.axis_index("c")
    place = jnp.stack([ci, 2 * xi + yi]).astype(jnp.int32)
    sch = _Overlap(_send_views(w), place)

    loss, grad_x, g_rep = _local_step(x[0], loss_target[0], rep, sch)
    loss = lax.psum(loss, ("x", "y", "c"))
    sch.sync("last")

    out = [{}, {}, {}, {}]
    for name, parts in SHARDED.items():
        flip = (lambda a: jnp.swapaxes(a, -1, -2)) if name in TRANSPOSED | PHYSICAL_T else (lambda a: a)
        shape = flip(w[name]).shape
        as3d = lambda a: flip(a).reshape(len(parts), -1, shape[-1])
        res = _adamw_shard([sch.own[p] for p in parts], [sch.recv[p] for p in parts], as3d(w[name]), as3d(m[name]), as3d(v[name]),
                           "adamw_" + name, flipped=name in PHYSICAL_T)
        for o, r in zip(out, res):
            o[name] = flip(r.reshape(shape))
    small = _adamw_small([sch.full[n] for n in SMALL], *[list(_views2d(d).values()) for d in (rep, m, v)], "adamw_small")
    for o, res in zip(out, small):
        o.update((n, r.reshape(w[n].shape)) for n, r in zip(REPLICATED, res))

    return (loss, grad_x[None], *[out[0][n] for n in WEIGHTS], *[out[1][n] for n in WEIGHTS],
            *[out[2][n] for n in WEIGHTS], *[out[3][n] for n in WEIGHTS])
```

```python
import functools
import math

import numpy as np
import jax
import jax.numpy as jnp
from jax import lax
from jax.experimental import pallas as pl
from jax.experimental.pallas import tpu as pltpu

F32 = jnp.float32
BF16 = jnp.bfloat16
DH = jnp.bfloat16
S = jax.ShapeDtypeStruct

D = 1024
CHUNK = 128
SGU_W = 2048
SGU_G = 16
HD = 64
NH = 16
NKV = 4
KVG = 4
D_FF = 2816
REL_BUCKETS = 32
REL_MAX_DIST = 128
EPS = 1e-6
N_DEV = 8
MESH = pl.DeviceIdType.MESH

ADAM_LR = 0.001
ADAM_B1 = 0.9
ADAM_B2 = 0.999
ADAM_EPS = 1e-08
ADAM_WD = 0.01
ADAM_STEP = 10

ROW_TILE = 512
HALO = 8
FFN_ROWS = 256


def _tm(t):
    return min(ROW_TILE, t)


def _cp(*sem):
    return pltpu.CompilerParams(dimension_semantics=sem)


ANY = pl.BlockSpec(memory_space=pl.ANY)


def _place():
    x, y, c = lax.axis_index("x"), lax.axis_index("y"), lax.axis_index("c")
    return x, y, c, [(1 - x, y), (x, 1 - y), (1 - x, 1 - y)]


class _Comm:
    SEMS = {"ag1": 5, "ag2": 3, "rs1": 4, "rs2": 3, "agd": 8}

    def __init__(self):
        self.inputs, self.out_shapes, self.aliases, self.ops, self.n_sems = [], [], {}, [], 0

    def add(self, kind, arr, rows=None, into=None):
        lead = {"ag1": N_DEV, "agd": N_DEV, "ag2": None, "rs1": 4, "rs2": 3}[kind]
        shape = arr.shape if lead is None else (lead,) + arr.shape[(0 if kind in ("ag1", "agd") else 1):]
        if kind == "ag2":
            self.aliases[len(self.inputs)] = len(self.out_shapes)
        self.ops.append((kind, len(self.inputs), len(self.out_shapes), self.n_sems, rows))
        self.inputs.append(arr)
        if into is not None:
            self.aliases[len(self.inputs)] = len(self.out_shapes)
            self.inputs.append(into)
        self.out_shapes.append(S(shape, arr.dtype))
        self.n_sems += self.SEMS[kind]
        return len(self.out_shapes) - 1

    def _copies(self, ins, outs, send, recv):
        x, y, c, chips = _place()
        me, sibling = (x, y, c), (x, y, 1 - c)
        slot = lambda px, py, pc: 4 * px + 2 * py + pc
        sends, recvs, local = [], [], []

        def rc(src, dst, k, to):
            return lambda: pltpu.make_async_remote_copy(src_ref=src(), dst_ref=dst(), send_sem=send.at[k], recv_sem=recv.at[k],
                                                        device_id=to, device_id_type=MESH)

        for kind, ii, oi, b, rows in self.ops:
            src, dst = ins[ii], outs[oi]
            at = lambda ref, i: (lambda: ref.at[i])
            if kind == "ag1":
                part = slice(None) if rows is None else pl.ds(rows[0], rows[1] - rows[0])
                to = lambda i, d=dst, p=part: (lambda: d.at[i, p])
                whole, mine = (lambda s=src, p=part: s.at[p]), to(slot(*me))
                sends.append(rc(whole, mine, b, sibling))
                recvs.append(rc(whole, to(slot(x, y, 1 - c)), b, me))
                for j, chip in enumerate(chips):
                    sends.append(rc(whole, mine, b + 1 + j, (*chip, c)))
                    recvs.append(rc(whole, to(slot(*chip, c)), b + 1 + j, me))
                local.append(lambda s=whole, m=mine, k=b + 4: pltpu.make_async_copy(s(), m(), send.at[k]))
            elif kind == "ag2":
                for j, chip in enumerate(chips):
                    sends.append(rc(at(dst, slot(*chip, c)), at(dst, slot(*chip, c)), b + j, sibling))
                    recvs.append(rc(at(dst, slot(*chip, 1 - c)), at(dst, slot(*chip, 1 - c)), b + j, me))
            elif kind == "agd":
                whole, mine = (lambda s=src: s), at(dst, slot(*me))
                flip = lambda v, bit: 1 - v if bit else v
                for k in range(1, N_DEV):
                    peer = (flip(x, k >> 2), flip(y, (k >> 1) & 1), flip(c, k & 1))
                    sends.append(rc(whole, mine, b + k - 1, peer))
                    recvs.append(rc(whole, at(dst, slot(*peer)), b + k - 1, me))
                local.append(lambda s=src, m=mine, k=b + 7: pltpu.make_async_copy(s, m(), send.at[k]))
            elif kind == "rs1":
                for k in range(4):
                    sends.append(rc(at(src, 2 * k + (1 - c)), at(dst, k), b + k, sibling))
                    recvs.append(rc(at(src, 2 * k + c), at(dst, k), b + k, me))
            else:
                for j, (px, py) in enumerate(chips):
                    sends.append(rc(at(src, 2 * px + py), at(dst, j), b + j, (px, py, c)))
                    recvs.append(rc(at(src, 2 * px + py), at(dst, j), b + j, me))
        return sends, recvs, local

    def start(self, ins, outs, send, recv):
        sends, _, local = self._copies(ins, outs, send, recv)
        for make in local + sends:
            make().start()

    def finish(self, ins, outs, send, recv):
        sends, recvs, local = self._copies(ins, outs, send, recv)
        for make in recvs:
            make().wait_recv()
        for make in sends:
            make().wait_send()
        for make in local:
            make().wait()


def _run(body, args, hook, *, grid, in_specs, out_specs, out_shape, name, semantics, scratch_shapes=(), aliases=None):
    comm = hook() if hook is not None else None
    aliases = dict(aliases or {})
    if comm is None:
        return pl.pallas_call(body, grid=grid, in_specs=in_specs, out_specs=out_specs, out_shape=out_shape, name=name,
                              scratch_shapes=list(scratch_shapes), input_output_aliases=aliases,
                              compiler_params=_cp(*semantics))(*args)
    single = not isinstance(out_shape, (list, tuple))
    out_shapes = [out_shape] if single else list(out_shape)
    out_specs_l = [out_specs] if single else list(out_specs)
    n_in, n_out, n_scr, ci, co = len(args), len(out_shapes), len(scratch_shapes), len(comm.inputs), len(comm.out_shapes)

    def wrapped(*refs):
        ins, cins = refs[:n_in], refs[n_in:n_in + ci]
        outs, couts = refs[n_in + ci:n_in + ci + n_out], refs[n_in + ci + n_out:n_in + ci + n_out + co]
        scr = refs[n_in + ci + n_out + co:n_in + ci + n_out + co + n_scr]
        send, recv = refs[-2:]
        first = functools.reduce(lambda a, b: a & b, [pl.program_id(a) == 0 for a in range(len(grid))])
        last = functools.reduce(lambda a, b: a & b, [pl.program_id(a) == g - 1 for a, g in enumerate(grid)])

        @pl.when(first)
        def _():
            comm.start(cins, couts, send, recv)

        body(*ins, *outs, *scr)

        @pl.when(last)
        def _():
            comm.finish(cins, couts, send, recv)

    res = pl.pallas_call(
        wrapped, grid=grid, in_specs=list(in_specs) + [ANY] * ci, out_specs=out_specs_l + [ANY] * co,
        out_shape=out_shapes + comm.out_shapes, name=name,
        scratch_shapes=list(scratch_shapes) + [pltpu.SemaphoreType.DMA((comm.n_sems,)), pltpu.SemaphoreType.DMA((comm.n_sems,))],
        input_output_aliases={**aliases, **{n_in + k: n_out + v for k, v in comm.aliases.items()}},
        compiler_params=pltpu.CompilerParams(dimension_semantics=("arbitrary",) * len(grid), has_side_effects=True))(*args, *comm.inputs)
    hook(res[n_out:])
    return res[0] if single else list(res[:n_out])


def _dot(a, b):
    return jnp.dot(a, b, preferred_element_type=F32)


def _dot_nt(a, b):
    return lax.dot_general(a, b, (((1,), (1,)), ((), ())), preferred_element_type=F32)


def _dot_tn(a, b):
    return lax.dot_general(a, b, (((0,), (0,)), ((), ())), preferred_element_type=F32)


def _gelu(x):
    return 0.5 * x * (1.0 + lax.erf(x * (2.0 ** -0.5)))


def _gelu_and_grad(x):
    cdf = 0.5 * (1.0 + lax.erf(x * (2.0 ** -0.5)))
    return x * cdf, cdf + x * jnp.exp(-0.5 * x * x) * (1.0 / math.sqrt(2.0 * math.pi))


def _sigmoid(x):
    return 1.0 / (1.0 + jnp.exp(-x))


def _rstd(x):
    return lax.rsqrt(jnp.mean(x * x, axis=-1, keepdims=True) + EPS)


def _rel_tables():
    q = np.arange(CHUNK)[:, None] + CHUNK
    k = np.arange(2 * CHUNK)[None, :]
    dist = q - k
    n = np.maximum(dist, 0)
    max_exact = REL_BUCKETS // 2
    large = max_exact + (np.log(np.maximum(n, 1).astype(np.float32) / max_exact)
                         / math.log(REL_MAX_DIST / max_exact) * (REL_BUCKETS - max_exact)).astype(np.int32)
    large = np.minimum(large, REL_BUCKETS - 1)
    return np.where(n < max_exact, n, large).astype(np.int32)


def _rmsnorm(x, gain, name):
    t = x.shape[0]
    tm = _tm(t)

    def body(x_ref, g_ref, o_ref):
        xv = x_ref[...]
        o_ref[...] = (xv * _rstd(xv) * g_ref[...]).astype(BF16)

    return pl.pallas_call(
        body, grid=(t // tm,), name=name,
        in_specs=[pl.BlockSpec((tm, D), lambda i: (i, 0)), pl.BlockSpec((1, D), lambda i: (0, 0))],
        out_specs=pl.BlockSpec((tm, D), lambda i: (i, 0)),
        out_shape=S((t, D), BF16), compiler_params=_cp("parallel"))(x, gain)


def _resident(shape):
    zeros = (0,) * len(shape)
    return pl.BlockSpec(shape, lambda *_: zeros, pipeline_mode=pl.Buffered(1))


def _mm_slot(hn, wg, out_dtype, name, hook=None):
    t, k = hn.shape
    ns, _, n = wg.shape
    tm = _tm(t)

    def body(a_ref, w_ref, o_ref):
        a = a_ref[...]
        for s in range(ns):
            o_ref[s] = _dot(a, w_ref[s]).astype(out_dtype)

    return _run(
        body, [hn, wg], hook, grid=(t // tm,), name=name, semantics=("parallel",),
        in_specs=[pl.BlockSpec((tm, k), lambda i: (i, 0)), _resident(wg.shape)],
        out_specs=pl.BlockSpec((ns, tm, n), lambda i: (0, i, 0)), out_shape=S((ns, t, n), out_dtype))


def _mm_t(hn, wt, name, hook=None):
    t, k = hn.shape
    ns, n, _ = wt.shape
    tm = _tm(t)

    def body(a_ref, w_ref, o_ref):
        a = a_ref[...]
        for s in range(ns):
            o_ref[s * n:(s + 1) * n, :] = _dot_nt(w_ref[s], a)

    return _run(
        body, [hn, wt], hook, grid=(t // tm,), name=name, semantics=("parallel",),
        in_specs=[pl.BlockSpec((tm, k), lambda i: (i, 0)), _resident(wt.shape)],
        out_specs=pl.BlockSpec((ns * n, tm), lambda i: (0, i)), out_shape=S((ns * n, t), F32))


def _conv3(a, prev, cw, cb, tm):
    ext = jnp.concatenate([prev, a], axis=0)
    return cw[2:3] * a + cw[1:2] * ext[HALO - 1:HALO - 1 + tm] + cw[0:1] * ext[HALO - 2:HALO - 2 + tm] + cb


def _ffn_fwd(hn, h, wup, wdown, cw, cb, extra, mode, name, hook=None):
    t, k = hn.shape
    n = wup.shape[-1]
    nh = wup.shape[0] // 2
    tm = min(FFN_ROWS, t)
    ni = t // tm

    def body(a_ref, h_ref, wu_ref, wd_ref, cw_ref, cb_ref, e_ref, as_ref, cs_ref, o1_ref, o2_ref, carry):
        i = pl.program_id(0)

        @pl.when(i == 0)
        def _():
            carry[...] = jnp.zeros_like(carry)

        a = a_ref[...]
        acc = h_ref[...]
        nxt = (_dot(a, wu_ref[0]), _dot(a, wu_ref[nh]))
        for j in range(nh):
            ag, av = nxt
            if j + 1 < nh:
                nxt = (_dot(a, wu_ref[j + 1]), _dot(a, wu_ref[nh + j + 1]))
            as_ref[j] = ag.astype(BF16)
            as_ref[nh + j] = av.astype(BF16)
            cg = _conv3(ag, carry[j], cw_ref[j], cb_ref[j], tm)
            cv = _conv3(av, carry[nh + j], cw_ref[nh + j], cb_ref[nh + j], tm)
            carry[j] = ag[tm - HALO:]
            carry[nh + j] = av[tm - HALO:]
            cs_ref[j] = cg.astype(BF16)
            cs_ref[nh + j] = cv.astype(BF16)
            act = (cg * _sigmoid(cg) * cv).astype(BF16)
            acc = acc + _dot(act, wd_ref[j * n:(j + 1) * n, :])
        if mode == "norm":
            o1_ref[...] = acc
            o2_ref[...] = (acc * _rstd(acc) * e_ref[...]).astype(BF16)
        else:
            err = acc - e_ref[...]
            o1_ref[...] = (err * (1.0 / D)).astype(o1_ref.dtype)
            o2_ref[...] = jnp.full(o2_ref.shape, jnp.sum(err * err), F32)

    row = pl.BlockSpec((tm, D), lambda i: (i, 0))
    if mode == "norm":
        e_spec, o2_spec, o2_shape = pl.BlockSpec((1, D), lambda i: (0, 0)), row, S((t, D), BF16)
    else:
        e_spec, o2_spec, o2_shape = row, pl.BlockSpec((None, 8, 128), lambda i: (i, 0, 0)), S((ni, 8, 128), F32)
    aspec = pl.BlockSpec((2 * nh, tm, n), lambda i: (0, i, 0))
    return _run(
        body, [hn, h, wup, wdown, cw, cb, extra], hook, grid=(ni,), name=name, semantics=("arbitrary",),
        in_specs=[pl.BlockSpec((tm, k), lambda i: (i, 0)), row, _resident(wup.shape), _resident(wdown.shape),
                  _resident(cw.shape), _resident(cb.shape), e_spec],
        out_specs=[aspec, aspec, row, o2_spec],
        out_shape=[S((2 * nh, t, n), BF16), S((2 * nh, t, n), BF16), S((t, D), F32 if mode == "norm" else DH), o2_shape],
        scratch_shapes=[pltpu.VMEM((2 * nh, HALO, n), F32)])


def _tril_mask():
    r = lax.broadcasted_iota(jnp.int32, (CHUNK, CHUNK), 0)
    c = lax.broadcasted_iota(jnp.int32, (CHUNK, CHUNK), 1)
    return r >= c


def _sgu_gate_fwd(a_s, vgain, ws, bst, name, hook=None):
    t = a_s.shape[1]
    sw = a_s.shape[2]
    gps = sw // CHUNK

    def body(a_ref, vg_ref, ws_ref, b_ref, o_ref):
        v = _gelu(jnp.concatenate([a_ref[4 + s].astype(F32) for s in range(4)], axis=1))
        vn = (v * _rstd(v) * vg_ref[...]).astype(BF16)
        tri = _tril_mask()
        for g in range(SGU_G):
            w = jnp.where(tri, ws_ref[g], 0.0).astype(BF16)
            sg = _dot(w, vn[:, g * CHUNK:(g + 1) * CHUNK]) + b_ref[:, g:g + 1]
            lo = (g % gps) * CHUNK
            u = _gelu(a_ref[g // gps, :, lo:lo + CHUNK].astype(F32))
            o_ref[g // gps, :, lo:lo + CHUNK] = (u * sg).astype(BF16)

    return _run(
        body, [a_s, vgain, ws, bst], hook, grid=(t // CHUNK,), name=name, semantics=("parallel",),
        in_specs=[pl.BlockSpec((8, CHUNK, sw), lambda n: (0, n, 0)), pl.BlockSpec((1, SGU_W), lambda n: (0, 0)),
                  pl.BlockSpec((SGU_G, CHUNK, CHUNK), lambda n: (0, 0, 0)), pl.BlockSpec((CHUNK, SGU_G), lambda n: (0, 0))],
        out_specs=pl.BlockSpec((4, CHUNK, sw), lambda n: (0, n, 0)), out_shape=S((4, t, sw), BF16))


def _resid_mm(a_s, w, resid, extra, mode, name, hook=None, fm=False):
    nk, t, kc = (1, a_s.shape[1], a_s.shape[0]) if fm else a_s.shape
    tm = _tm(t)
    ni = t // tm

    def body(a_ref, w_ref, r_ref, e_ref, o1_ref, o2_ref):
        h = r_ref[...]
        if fm:
            h = h + _dot_tn(a_ref[...], w_ref[...])
        for j in range(0 if fm else nk):
            h = h + _dot(a_ref[j], w_ref[j * kc:(j + 1) * kc, :])
        if mode == "norm":
            o1_ref[...] = h
            o2_ref[...] = (h * _rstd(h) * e_ref[...]).astype(BF16)
        else:
            err = h - e_ref[...]
            o1_ref[...] = (err * (1.0 / D)).astype(o1_ref.dtype)
            o2_ref[...] = jnp.full(o2_ref.shape, jnp.sum(err * err), F32)

    row = pl.BlockSpec((tm, D), lambda i: (i, 0))
    if mode == "norm":
        e_spec, o2_spec, o2_shape = pl.BlockSpec((1, D), lambda i: (0, 0)), row, S((t, D), BF16)
    else:
        e_spec, o2_spec, o2_shape = row, pl.BlockSpec((None, 8, 128), lambda i: (i, 0, 0)), S((ni, 8, 128), F32)
    return _run(
        body, [a_s, w, resid, extra], hook, grid=(ni,), name=name, semantics=("parallel",),
        in_specs=[pl.BlockSpec((kc, tm), lambda i: (0, i)) if fm else pl.BlockSpec((nk, tm, kc), lambda i: (0, i, 0)),
                  _resident(w.shape), row, e_spec],
        out_specs=[row, o2_spec], out_shape=[S((t, D), F32 if mode == "norm" else DH), o2_shape])


def _relbias_fwd(rel_bias_t, bucket_row, name):
    nb = bucket_row.shape[1]

    def body(rb_ref, bk_ref, o_ref):
        onehot = (lax.broadcasted_iota(jnp.int32, (REL_BUCKETS, nb), 0) == bk_ref[...]).astype(F32)
        o_ref[...] = jnp.dot(rb_ref[...], onehot, precision=lax.Precision.HIGHEST, preferred_element_type=F32)

    return pl.pallas_call(body, out_shape=S((NH, nb), F32), name=name)(rel_bias_t, bucket_row)


def _relbias_bwd(dbias, bucket_row, name):
    nb = bucket_row.shape[1]

    def body(db_ref, bk_ref, o_ref):
        onehot = (lax.broadcasted_iota(jnp.int32, (REL_BUCKETS, nb), 0) == bk_ref[...]).astype(F32)
        o_ref[...] = lax.dot_general(db_ref[...], onehot, (((1,), (1,)), ((), ())),
                                     precision=lax.Precision.HIGHEST, preferred_element_type=F32)

    return pl.pallas_call(body, out_shape=S((NH, REL_BUCKETS), F32), name=name)(dbias, bucket_row)


QKV = D + 2 * NKV * HD
KV0 = D


def _rstd_rows(x):
    return lax.rsqrt(jnp.mean(x * x, axis=0, keepdims=True) + EPS)


def _attn_valid(n):
    kj = lax.broadcasted_iota(jnp.int32, (2 * CHUNK, CHUNK), 0)
    qi = lax.broadcasted_iota(jnp.int32, (2 * CHUNK, CHUNK), 1)
    dist = qi + CHUNK - kj
    return (dist >= 0) & (dist < CHUNK) & ((n > 0) | (kj >= CHUNK))


def _attn_band(cur_ref, prev_ref, row):
    return jnp.concatenate([prev_ref[row - KV0:row - KV0 + HD, :], cur_ref[row:row + HD, :]], axis=1)


def _attn_probs(kn_tok, qn, bias, valid, sink):
    s = _dot(kn_tok, qn) * (HD ** -0.5) + bias
    s = jnp.where(valid, s, -jnp.inf)
    m = jnp.maximum(jnp.max(s, axis=0, keepdims=True), sink)
    p = jnp.exp(s - m)
    psink = jnp.exp(sink - m)
    inv = 1.0 / (jnp.sum(p, axis=0, keepdims=True) + psink)
    return p * inv, psink * inv


def _attn_fwd(qkv_t, qg, kg, sinks, bias, name, hook=None):
    t = qkv_t.shape[1]

    def body(cur_ref, prev_ref, qg_ref, kg_ref, sink_ref, bias_ref, o_ref):
        n = pl.program_id(0)
        valid = _attn_valid(n)
        ks = [_attn_band(cur_ref, prev_ref, KV0 + HD * h) for h in range(NKV)]
        kn_toks = [(k * _rstd_rows(k) * kg_ref[...]).astype(BF16).T for k in ks]
        vbs = [_attn_band(cur_ref, prev_ref, KV0 + HD * (NKV + h)).astype(BF16) for h in range(NKV)]
        qs = [cur_ref[HD * hq:HD * (hq + 1), :] for hq in range(NH)]
        qns = [(q * _rstd_rows(q) * qg_ref[...]).astype(BF16) for q in qs]
        ps = [_attn_probs(kn_toks[hq // KVG], qns[hq], bias_ref[hq], valid, sink_ref[hq])[0] for hq in range(NH)]
        for hq in range(NH):
            o_ref[HD * hq:HD * (hq + 1), :] = _dot(vbs[hq // KVG], ps[hq].astype(BF16)).astype(BF16)

    col = pl.BlockSpec((HD, 1), lambda n: (0, 0))
    return _run(
        body, [qkv_t, qkv_t, qg, kg, sinks, bias], hook, grid=(t // CHUNK,), name=name, semantics=("parallel",),
        in_specs=[pl.BlockSpec((QKV, CHUNK), lambda n: (0, n)),
                  pl.BlockSpec((QKV - KV0, CHUNK), lambda n: (KV0 // (QKV - KV0), jnp.maximum(n - 1, 0))),
                  col, col, pl.BlockSpec(memory_space=pltpu.SMEM), pl.BlockSpec((NH, 2 * CHUNK, CHUNK), lambda n: (0, 0, 0))],
        out_specs=pl.BlockSpec((D, CHUNK), lambda n: (0, n)), out_shape=S((D, t), BF16))


def _dx_rows(dh, w, kc, out_dtype, name, hook=None):
    t = dh.shape[0]
    nk = w.shape[0] // kc
    tm = _tm(t)

    def body(d_ref, w_ref, o_ref):
        dhb = d_ref[...].astype(BF16)
        for j in range(nk):
            o_ref[j] = _dot_nt(dhb, w_ref[j * kc:(j + 1) * kc, :]).astype(out_dtype)

    return _run(
        body, [dh, w], hook, grid=(t // tm,), name=name, semantics=("parallel",),
        in_specs=[pl.BlockSpec((tm, D), lambda i: (i, 0)), _resident(w.shape)],
        out_specs=pl.BlockSpec((nk, tm, kc), lambda i: (0, i, 0)), out_shape=S((nk, t, kc), out_dtype))


def _dx_rows_t(dh, w, name, hook=None):
    t = dh.shape[0]
    k = w.shape[0]
    tm = _tm(t)

    def body(d_ref, w_ref, o_ref):
        o_ref[...] = _dot_nt(w_ref[...], d_ref[...].astype(BF16)).astype(BF16)

    return _run(
        body, [dh, w], hook, grid=(t // tm,), name=name, semantics=("parallel",),
        in_specs=[pl.BlockSpec((tm, D), lambda i: (i, 0)), _resident(w.shape)],
        out_specs=pl.BlockSpec((k, tm), lambda i: (0, i)), out_shape=S((k, t), BF16))


def _ffn_bwd1(dh, c, wdown, name, hook=None):
    ns, t, n = c.shape
    nh = ns // 2
    tm = min(FFN_ROWS, t)
    ni = t // tm

    def body(d_ref, c_ref, wd_ref, dc_ref, dw_hbm, dwb_hbm, acc, stage):
        i = pl.program_id(0)

        @pl.when(i == 0)
        def _():
            acc[...] = jnp.zeros_like(acc)

        dhb = d_ref[...].astype(BF16)
        for j in range(nh):
            dact = _dot_nt(dhb, wd_ref[j * n:(j + 1) * n, :])
            cg = c_ref[j].astype(F32)
            cv = c_ref[nh + j].astype(F32)
            sg = _sigmoid(cg)
            gs = cg * sg
            acc[j * n:(j + 1) * n, :] += _dot_tn((gs * cv).astype(BF16), dhb)
            dc_ref[j] = (dact * cv * (sg + gs * (1.0 - sg))).astype(BF16)
            dc_ref[nh + j] = (dact * gs).astype(BF16)

        @pl.when(i == ni - 1)
        def _():
            pltpu.sync_copy(acc, dw_hbm)
            for j in range(nh):
                stage[...] = acc[j * n:(j + 1) * n, :].astype(BF16)
                pltpu.sync_copy(stage, dwb_hbm.at[pl.ds(j * n, n), :])

    slab = pl.BlockSpec((ns, tm, n), lambda i: (0, i, 0))
    return _run(
        body, [dh, c, wdown], hook, grid=(ni,), name=name, semantics=("arbitrary",),
        in_specs=[pl.BlockSpec((tm, D), lambda i: (i, 0)), slab, _resident(wdown.shape)],
        out_specs=[slab, ANY, ANY], out_shape=[S((ns, t, n), BF16), S(wdown.shape, F32), S(wdown.shape, BF16)],
        scratch_shapes=[pltpu.VMEM(wdown.shape, F32), pltpu.VMEM((n, D), BF16)])


def _ffn_bwd2(dc, a, wup, cw, h, gain, dh_in, name, hook=None):
    ns, t, n = dc.shape
    tm = min(FFN_ROWS, t)
    ni = t // tm

    def body(dc_ref, a_ref, wu_ref, cw_ref, h_ref, g_ref, di_ref, da_ref, o_ref, dg_ref, dcw_ref, dcb_ref, carry, keep):
        i = pl.program_id(0)

        @pl.when(i == 0)
        def _():
            carry[...] = jnp.zeros_like(carry)
            dg_ref[...] = jnp.zeros_like(dg_ref)
            dcw_ref[...] = jnp.zeros_like(dcw_ref)
            dcb_ref[...] = jnp.zeros_like(dcb_ref)

        rsum = lambda v: jnp.sum(v, axis=0, keepdims=True)
        acc = jnp.zeros((tm, D), F32)
        for s in range(ns):
            x = dc_ref[s].astype(F32)
            ext = jnp.concatenate([x, carry[s]], axis=0)
            keep[0] = ext[1:1 + tm]
            keep[1] = ext[2:2 + tm]
            x1, x2 = keep[0], keep[1]
            cwv = cw_ref[s]
            da = (cwv[2:3] * x + cwv[1:2] * x1 + cwv[0:1] * x2).astype(BF16)
            carry[s] = x[:HALO]
            da_ref[s] = da
            acc = acc + _dot_nt(da, wu_ref[s])
            av = a_ref[s].astype(F32)
            dcw_ref[s] += jnp.concatenate([rsum(x2 * av), rsum(x1 * av), rsum(x * av)], axis=0)
            dcb_ref[s] += rsum(x)
        hv = h_ref[...]
        r = _rstd(hv)
        gg = acc * g_ref[...]
        dh_new = di_ref[...].astype(F32) + r * gg - hv * (r * r * r * jnp.mean(gg * hv, axis=-1, keepdims=True))
        o_ref[...] = dh_new.astype(o_ref.dtype)
        dg_ref[...] += jnp.sum(acc * hv * r, axis=0, keepdims=True)

    slab = pl.BlockSpec((ns, tm, n), lambda i: (0, ni - 1 - i, 0))
    row = pl.BlockSpec((tm, D), lambda i: (ni - 1 - i, 0))
    vec = pl.BlockSpec((1, D), lambda i: (0, 0))
    whole = lambda shape: pl.BlockSpec(shape, lambda i: (0,) * len(shape))
    return _run(
        body, [dc, a, wup, cw, h, gain, dh_in], hook, grid=(ni,), name=name, semantics=("arbitrary",),
        in_specs=[slab, slab, _resident(wup.shape), _resident(cw.shape), row, vec, row],
        out_specs=[slab, row, vec, whole((ns, 3, n)), whole((ns, 1, n))],
        out_shape=[S((ns, t, n), BF16), S((t, D), DH), S((1, D), F32), S((ns, 3, n), F32), S((ns, 1, n), F32)],
        scratch_shapes=[pltpu.VMEM((ns, HALO, n), F32), pltpu.VMEM((2, tm, n), F32)])


def _dw_slot(hn, dy_s, name, hook=None):
    t, k = hn.shape
    ns, _, n = dy_s.shape
    tm = _tm(t)

    def body(a_ref, b_ref, o_ref, ob_ref, at_ref):
        @pl.when(pl.program_id(0) == 0)
        def _():
            for i in range(t // tm):
                at_ref[:, i * tm:(i + 1) * tm] = a_ref[i * tm:(i + 1) * tm, :].T

        acc = _dot(at_ref[...], b_ref[...])
        o_ref[...] = acc
        ob_ref[...] = acc.astype(BF16)

    ospec = pl.BlockSpec((None, k, n), lambda j: (j, 0, 0))
    return _run(
        body, [hn, dy_s], hook, grid=(ns,), name=name, semantics=("arbitrary",),
        in_specs=[_resident(hn.shape), pl.BlockSpec((None, t, n), lambda j: (j, 0, 0))],
        out_specs=[ospec, ospec], out_shape=[S((ns, k, n), F32), S((ns, k, n), BF16)],
        scratch_shapes=[pltpu.VMEM((k, t), BF16)])


def _dw_rows(a_s, dh, name, hook=None, fm=False):
    nk, t, kc = (1, a_s.shape[1], a_s.shape[0]) if fm else a_s.shape
    tm = _tm(t)
    ni = t // tm

    def body(a_ref, d_ref, o_ref, ob_ref):
        i = pl.program_id(0)
        dhb = d_ref[...].astype(BF16)

        @pl.when(i == 0)
        def _():
            o_ref[...] = jnp.zeros_like(o_ref)

        if fm:
            o_ref[...] += _dot(a_ref[...], dhb)
        for j in range(0 if fm else nk):
            o_ref[j * kc:(j + 1) * kc, :] += _dot_tn(a_ref[j], dhb)

        @pl.when(i == ni - 1)
        def _():
            ob_ref[...] = o_ref[...].astype(BF16)

    ospec = pl.BlockSpec((nk * kc, D), lambda i: (0, 0))
    return _run(
        body, [a_s, dh], hook, grid=(ni,), name=name, semantics=("arbitrary",),
        in_specs=[pl.BlockSpec((kc, tm), lambda i: (0, i)) if fm else pl.BlockSpec((nk, tm, kc), lambda i: (0, i, 0)),
                  pl.BlockSpec((tm, D), lambda i: (i, 0))],
        out_specs=[ospec, ospec], out_shape=[S((nk * kc, D), F32), S((nk * kc, D), BF16)])


def _dx_slot_normbwd(dy_s, wg, h, gain, dh_in, name, hook=None, fm=False, out_dtype=F32):
    ns, t, n = (1, dy_s.shape[1], dy_s.shape[0]) if fm else dy_s.shape
    tm = _tm(t)

    def body(dy_ref, w_ref, h_ref, g_ref, di_ref, o_ref, dg_ref):
        i = pl.program_id(0)

        @pl.when(i == 0)
        def _():
            dg_ref[...] = jnp.zeros_like(dg_ref)

        g = _dot_tn(dy_ref[...], w_ref[...]) if fm else _dot_nt(dy_ref[0], w_ref[0])
        for s in range(1, ns):
            g = g + _dot_nt(dy_ref[s], w_ref[s])
        hv = h_ref[...]
        r = _rstd(hv)
        gg = g * g_ref[...]
        dh_new = di_ref[...].astype(F32) + r * gg - hv * (r * r * r * jnp.mean(gg * hv, axis=-1, keepdims=True))
        o_ref[...] = dh_new.astype(o_ref.dtype)
        dg_ref[...] += jnp.sum(g * hv * r, axis=0, keepdims=True)

    row = pl.BlockSpec((tm, D), lambda i: (i, 0))
    vec = pl.BlockSpec((1, D), lambda i: (0, 0))
    return _run(
        body, [dy_s, wg, h, gain, dh_in], hook, grid=(t // tm,), name=name, semantics=("arbitrary",),
        in_specs=[pl.BlockSpec((n, tm), lambda i: (0, i)) if fm else pl.BlockSpec((ns, tm, n), lambda i: (0, i, 0)),
                  _resident(wg.shape), row, vec, row],
        out_specs=[row, vec], out_shape=[S((t, D), out_dtype), S((1, D), F32)])


def _sgu_gate_bwd(a_s, dg_s, vgain, ws, bst, name, hook=None):
    t = a_s.shape[1]
    sw = a_s.shape[2]
    gps = sw // CHUNK

    def body(a_ref, dg_ref, vg_ref, ws_ref, b_ref, da_ref, dws_ref, dbt_ref, dvg_ref, dvn_ref):
        n = pl.program_id(0)

        @pl.when(n == 0)
        def _():
            dws_ref[...] = jnp.zeros_like(dws_ref)
            dbt_ref[...] = jnp.zeros_like(dbt_ref)
            dvg_ref[...] = jnp.zeros_like(dvg_ref)

        vpre = jnp.concatenate([a_ref[4 + s].astype(F32) for s in range(4)], axis=1)
        v, v_grad = _gelu_and_grad(vpre)
        r = _rstd(v)
        vhat = v * r
        vn = (vhat * vg_ref[...]).astype(BF16)
        tri = _tril_mask()
        lane = lax.broadcasted_iota(jnp.int32, (CHUNK, CHUNK), 1)
        dbt = jnp.zeros((CHUNK, CHUNK), F32)
        for g in range(SGU_G):
            w = jnp.where(tri, ws_ref[g], 0.0).astype(BF16)
            vng = vn[:, g * CHUNK:(g + 1) * CHUNK]
            sg = _dot(w, vng) + b_ref[:, g:g + 1]
            lo = (g % gps) * CHUNK
            u, u_grad = _gelu_and_grad(a_ref[g // gps, :, lo:lo + CHUNK].astype(F32))
            dgate = dg_ref[g // gps, :, lo:lo + CHUNK].astype(F32)
            da_ref[g // gps, :, lo:lo + CHUNK] = (dgate * sg * u_grad).astype(BF16)
            ds = dgate * u
            dsb = ds.astype(BF16)
            dvn_ref[:, g * CHUNK:(g + 1) * CHUNK] = _dot_tn(w, dsb)
            dws_ref[g] += jnp.where(tri, _dot_nt(dsb, vng), 0.0)
            dbt = dbt + jnp.where(lane == g, jnp.sum(ds, axis=-1, keepdims=True), 0.0)
        dbt_ref[...] += dbt
        dvn = dvn_ref[...]
        dvg_ref[...] += jnp.sum(dvn * vhat, axis=0, keepdims=True)
        gg = dvn * vg_ref[...]
        dv = r * gg - v * (r * r * r * jnp.mean(gg * v, axis=-1, keepdims=True))
        dav = (dv * v_grad).astype(BF16)
        for s in range(4):
            da_ref[4 + s] = dav[:, s * sw:(s + 1) * sw]

    return _run(
        body, [a_s, dg_s, vgain, ws, bst], hook, grid=(t // CHUNK,), name=name, semantics=("arbitrary",),
        in_specs=[pl.BlockSpec((8, CHUNK, sw), lambda n: (0, n, 0)), pl.BlockSpec((4, CHUNK, sw), lambda n: (0, n, 0)),
                  pl.BlockSpec((1, SGU_W), lambda n: (0, 0)), pl.BlockSpec((SGU_G, CHUNK, CHUNK), lambda n: (0, 0, 0)),
                  pl.BlockSpec((CHUNK, SGU_G), lambda n: (0, 0))],
        out_specs=[pl.BlockSpec((8, CHUNK, sw), lambda n: (0, n, 0)), pl.BlockSpec((SGU_G, CHUNK, CHUNK), lambda n: (0, 0, 0)),
                   pl.BlockSpec((CHUNK, CHUNK), lambda n: (0, 0)), pl.BlockSpec((1, SGU_W), lambda n: (0, 0))],
        out_shape=[S((8, t, sw), BF16), S((SGU_G, CHUNK, CHUNK), F32), S((CHUNK, CHUNK), F32), S((1, SGU_W), F32)],
        scratch_shapes=[pltpu.VMEM((CHUNK, SGU_W), F32)])


def _attn_bwd(qkv_t, do_t, qg, kg, sinks, bias, name, hook=None):
    t = qkv_t.shape[1]
    nb = t // CHUNK

    def body(cur_ref, prev_ref, do_ref, qg_ref, kg_ref, sink_ref, bias_ref,
             o_ref, dqg_out, dkg_out, dsk_out, dbias_ref, carry, dqg_ref, dkg_ref, dsk_ref):
        n = pl.program_id(0)

        @pl.when(n == 0)
        def _():
            carry[...] = jnp.zeros_like(carry)
            dqg_ref[...] = jnp.zeros_like(dqg_ref)
            dkg_ref[...] = jnp.zeros_like(dkg_ref)
            dsk_ref[...] = jnp.zeros_like(dsk_ref)
            dbias_ref[...] = jnp.zeros_like(dbias_ref)

        @pl.when(n < nb)
        def _():
            valid = _attn_valid(n)
            o_ref[0:KV0, :] = carry[0:KV0, :].astype(BF16)
            kvs, heads = range(NKV), range(NH)
            group = lambda h: range(KVG * h, KVG * (h + 1))
            ks = [_attn_band(cur_ref, prev_ref, KV0 + HD * h) for h in kvs]
            rks = [_rstd_rows(k) for k in ks]
            khats = [k * rk for k, rk in zip(ks, rks)]
            kns = [(khat * kg_ref[...]).astype(BF16) for khat in khats]
            kn_toks = [kn.T for kn in kns]
            vbs = [_attn_band(cur_ref, prev_ref, KV0 + HD * (NKV + h)).astype(BF16) for h in kvs]
            v_toks = [vb.T for vb in vbs]
            qs = [cur_ref[HD * hq:HD * (hq + 1), :] for hq in heads]
            rqs = [_rstd_rows(q) for q in qs]
            qhats = [q * rq for q, rq in zip(qs, rqs)]
            qns = [(qhat * qg_ref[...]).astype(BF16) for qhat in qhats]
            probs = [_attn_probs(kn_toks[hq // KVG], qns[hq], bias_ref[hq], valid, sink_ref[hq]) for hq in heads]
            dohs = [do_ref[HD * hq:HD * (hq + 1), :] for hq in heads]
            dps = [_dot(v_toks[hq // KVG], dohs[hq]) for hq in heads]
            dsums = [jnp.sum(p * dp, axis=0, keepdims=True) for (p, _), dp in zip(probs, dps)]
            dss = [p * (dp - dsum) for (p, _), dp, dsum in zip(probs, dps, dsums)]
            for hq in heads:
                dsk_ref[hq:hq + 1, :] -= probs[hq][1] * dsums[hq]
                dbias_ref[hq] += dss[hq]
            dvs = [sum(_dot_nt(dohs[hq], probs[hq][0].astype(BF16)) for hq in group(h)) for h in kvs]
            dscs = [(ds * (HD ** -0.5)).astype(BF16) for ds in dss]
            dqns = [_dot(kns[hq // KVG], dscs[hq]) for hq in heads]
            dkns = [sum(_dot_nt(qns[hq], dscs[hq]) for hq in group(h)) for h in kvs]
            dqg_ref[...] += sum(dqn * qhat for dqn, qhat in zip(dqns, qhats))
            for hq in heads:
                gq = dqns[hq] * qg_ref[...]
                carry[HD * hq:HD * (hq + 1), :] = rqs[hq] * gq - qs[hq] * (
                    rqs[hq] * rqs[hq] * rqs[hq] * jnp.mean(gq * qs[hq], axis=0, keepdims=True))
            dkg_ref[...] += sum(dkn * khat for dkn, khat in zip(dkns, khats))
            for h in kvs:
                krow, vrow = KV0 + HD * h, KV0 + HD * (NKV + h)
                gk = dkns[h] * kg_ref[...]
                dk = rks[h] * gk - ks[h] * (rks[h] * rks[h] * rks[h] * jnp.mean(gk * ks[h], axis=0, keepdims=True))
                o_ref[krow:krow + HD, :] = (carry[krow:krow + HD, :] + dk[:, :CHUNK]).astype(BF16)
                o_ref[vrow:vrow + HD, :] = (carry[vrow:vrow + HD, :] + dvs[h][:, :CHUNK]).astype(BF16)
                carry[krow:krow + HD, :] = dk[:, CHUNK:]
                carry[vrow:vrow + HD, :] = dvs[h][:, CHUNK:]

        @pl.when(n == nb)
        def _():
            o_ref[...] = carry[...].astype(BF16)
            dqg_out[...] = jnp.sum(dqg_ref[...], axis=1, keepdims=True)
            dkg_out[...] = jnp.sum(dkg_ref[...], axis=1, keepdims=True)
            dsk_out[...] = jnp.sum(dsk_ref[...], axis=1, keepdims=True)

    cur = lambda n: (0, jnp.minimum(n, nb - 1))
    col = pl.BlockSpec((HD, 1), lambda n: (0, 0))
    whole = lambda shape: pl.BlockSpec(shape, lambda n: (0,) * len(shape))
    return _run(
        body, [qkv_t, qkv_t, do_t, qg, kg, sinks, bias], hook, grid=(nb + 1,), name=name, semantics=("arbitrary",),
        in_specs=[pl.BlockSpec((QKV, CHUNK), cur),
                  pl.BlockSpec((QKV - KV0, CHUNK), lambda n: (KV0 // (QKV - KV0), jnp.clip(n - 1, 0, nb - 1))),
                  pl.BlockSpec((D, CHUNK), cur), col, col, pl.BlockSpec(memory_space=pltpu.SMEM), whole((NH, 2 * CHUNK, CHUNK))],
        out_specs=[pl.BlockSpec((QKV, CHUNK), lambda n: (0, jnp.maximum(n - 1, 0))), whole((HD, 1)), whole((HD, 1)),
                   whole((NH, 1)), whole((NH, 2 * CHUNK, CHUNK))],
        out_shape=[S((QKV, t), BF16), S((HD, 1), F32), S((HD, 1), F32), S((NH, 1), F32), S((NH, 2 * CHUNK, CHUNK), F32)],
        scratch_shapes=[pltpu.VMEM((QKV, CHUNK), F32), pltpu.VMEM((HD, CHUNK), F32), pltpu.VMEM((HD, 2 * CHUNK), F32),
                        pltpu.VMEM((NH, CHUNK), F32)])


class _Plain:
    def __init__(self, wg):
        self.full, self.grads = wg, {}

    def w(self, n):
        return self.full[n]

    def hook(self, host):
        return None

    def grad(self, n, pair):
        self.grads[n] = pair

    def small(self, g_rep):
        pass

    def sync(self, point):
        pass

    def first_norm(self, x, gain):
        return _rmsnorm(x, gain, "norm0")


def _local_step(x, target, rep, sch):
    bucket_row = jnp.asarray(_rel_tables().T.reshape(1, -1))
    bias = _relbias_fwd(rep["rel_bias"].T, bucket_row, "relbias_fwd").reshape(NH, 2 * CHUNK, CHUNK)
    bst = rep["sgu_b_s"][0].T
    ws = rep["sgu_w_s"][0]
    vgain = rep["sgu_v_gain"]
    qg, kg, sinks = rep["attn_q_gain"].reshape(HD, 1), rep["attn_k_gain"].reshape(HD, 1), rep["attn_sinks"][0]
    w_down = lambda l: sch.w("ffn_w_down%d" % l).reshape(D_FF, D)
    w_up = lambda l: sch.w("ffn_w_up%d" % l)
    cb = [rep["ffn_conv_b"][l].reshape(8, 1, -1) for l in range(2)]
    mixg = [rep["mix_norm"][l:l + 1] for l in range(2)]
    ffng = [rep["ffn_norm"][l:l + 1] for l in range(2)]
    rows = lambda pair: tuple(g.reshape(N_DEV, -1, D) for g in pair)
    hk = sch.hook

    hn0 = sch.first_norm(x, mixg[0])
    cw = [sch.w("ffn_conv_w")[:, 3 * l:3 * l + 3] for l in range(2)]
    a0 = _mm_slot(hn0, sch.w("sgu_w_in"), BF16, "sgu_in", hk("sgu_in"))
    gated = _sgu_gate_fwd(a0, vgain, ws, bst, "sgu_gate", hk("sgu_gate"))
    h1, hn1 = _resid_mm(gated, sch.w("sgu_w_out").reshape(SGU_W, D), x, ffng[0], "norm", "sgu_out", hk("sgu_out"))
    sch.sync("before_ffn0")
    a_ff0, c_ff0, h2, hn2 = _ffn_fwd(hn1, h1, w_up(0), w_down(0), cw[0], cb[0], mixg[1], "norm", "ffn0_fwd", hk("ffn0_fwd"))
    qkv = _mm_t(hn2, sch.w("attn_w_qkv"), "qkv", hk("qkv"))
    o = _attn_fwd(qkv, qg, kg, sinks, bias, "attn", hk("attn"))
    h3, hn3 = _resid_mm(o, sch.w("attn_w_o").reshape(D, D), h2, ffng[1], "norm", "attn_out", hk("attn_out"), fm=True)
    a_ff1, c_ff1, dy, sq = _ffn_fwd(hn3, h3, w_up(1), w_down(1), cw[1], cb[1], target, "loss", "ffn1_fwd_loss", hk("ffn1_fwd_loss"))
    loss = (0.5 / D) * jnp.sum(sq[:, 0, 0])

    def ffn_bwd(dh, h_in, hn, a, c, l, tag):
        dc, g_down, g_down_b = _ffn_bwd1(dh, c, w_down(l), tag + "_bwd1", hk(tag + "_bwd1"))
        sch.grad("ffn_w_down%d" % l, rows((g_down, g_down_b)))
        da, dh_new, dgain, g_cw, g_cb = _ffn_bwd2(dc, a, w_up(l), cw[l], h_in, ffng[l], dh, tag + "_bwd2", hk(tag + "_bwd2"))
        sch.grad("ffn_w_up%d" % l, _dw_slot(hn, da, tag + "_dw_up", hk(tag + "_dw_up")))
        return dh_new, dgain, g_cw, g_cb.reshape(-1)

    dh, d_ffng1, g_cw1, g_cb1 = ffn_bwd(dy, h3, hn3, a_ff1, c_ff1, 1, "ffn1")
    do = _dx_rows_t(dh, sch.w("attn_w_o").reshape(D, D), "attn_do", hk("attn_do"))
    sch.grad("attn_w_o", rows(_dw_rows(o, dh, "dw_o", hk("dw_o"), fm=True)))
    dqkv, d_qg, d_kg, d_sk, d_bias = _attn_bwd(qkv, do, qg, kg, sinks, bias, "attn_bwd", hk("attn_bwd"))
    sch.grad("attn_w_qkv", tuple(g.reshape(N_DEV, -1, D) for g in _dw_rows(dqkv, hn2, "dw_qkv", hk("dw_qkv"), fm=True)))
    dh, d_mixg1 = _dx_slot_normbwd(dqkv, sch.w("attn_w_qkv").reshape(QKV, D), h2, mixg[1], dh, "dx_qkv", hk("dx_qkv"), fm=True,
                                   out_dtype=DH)
    d_relb = _relbias_bwd(d_bias.reshape(NH, -1), bucket_row, "relbias_bwd").T
    g_rep = {"attn_q_gain": d_qg.reshape(1, HD), "attn_k_gain": d_kg.reshape(1, HD), "attn_sinks": d_sk.reshape(1, NH),
             "rel_bias": d_relb}
    sch.small(g_rep)
    dh, d_ffng0, g_cw0, g_cb0 = ffn_bwd(dh, h1, hn1, a_ff0, c_ff0, 0, "ffn0")
    g_cw = jnp.concatenate([g_cw0, g_cw1], axis=1)
    sch.grad("ffn_conv_w", (g_cw, g_cw.astype(BF16)))
    g_ffn = {"ffn_norm": jnp.concatenate([d_ffng0, d_ffng1], axis=0), "ffn_conv_b": jnp.stack([g_cb0, g_cb1], axis=0)}
    sch.small(g_ffn)
    dgated = _dx_rows(dh, sch.w("sgu_w_out").reshape(SGU_W, D), SGU_W // 4, BF16, "sgu_dgated", hk("sgu_dgated"))
    sch.grad("sgu_w_out", rows(_dw_rows(gated, dh, "dw_sgu_out", hk("dw_sgu_out"))))
    da0, d_ws, d_bst, d_vgain = _sgu_gate_bwd(a0, dgated, vgain, ws, bst, "sgu_gate_bwd", hk("sgu_gate_bwd"))
    g_sgu = {"sgu_v_gain": d_vgain, "sgu_w_s": d_ws[None], "sgu_b_s": d_bst[:, :SGU_G].T[None]}
    sch.small(g_sgu)
    sch.grad("sgu_w_in", _dw_slot(hn0, da0, "dw_sgu_in", hk("dw_sgu_in")))
    sch.sync("after_dw")
    grad_x, d_mixg0 = _dx_slot_normbwd(da0, sch.w("sgu_w_in"), x, mixg[0], dh, "dx_sgu_in", hk("dx_sgu_in"))
    g_mix = {"mix_norm": jnp.concatenate([d_mixg0, d_mixg1], axis=0)}
    sch.small(g_mix)
    for g in (g_ffn, g_sgu, g_mix):
        g_rep.update(g)
    return loss, grad_x, g_rep


def _allgather(xs, x_in, gain, name):
    nt = len(xs)
    t_rows = x_in.shape[0]
    tm = _tm(t_rows)

    def body(xin_ref, g_ref, *refs):
        x_refs, hn_ref, o_refs = refs[:nt], refs[nt], refs[nt + 1:2 * nt + 1]
        send_sems, recv_sems, local_sems = refs[2 * nt + 1:]
        x, y, c, chips = _place()
        me, sibling = (x, y, c), (x, y, 1 - c)

        def copy(t, k, block, to, src=None):
            px, py, pc = block
            dst = o_refs[t].at[4 * px + 2 * py + pc]
            return pltpu.make_async_remote_copy(
                src_ref=dst if src is None else src, dst_ref=dst, send_sem=send_sems.at[t, k], recv_sem=recv_sems.at[t, k],
                device_id=to, device_id_type=MESH)

        mine = lambda: [pltpu.make_async_copy(x_refs[t], o_refs[t].at[4 * x + 2 * y + c], local_sems.at[t]) for t in range(nt)]

        def first():
            out = []
            for t in range(nt):
                out.append(copy(t, 0, me, sibling, src=x_refs[t]))
                out += [copy(t, 1 + j, me, (*chip, c), src=x_refs[t]) for j, chip in enumerate(chips)]
            return out

        @pl.when(pl.program_id(0) == 0)
        def _():
            for cp in mine() + first():
                cp.start()

        xv = xin_ref[...]
        hn_ref[...] = (xv * _rstd(xv) * g_ref[...]).astype(BF16)

        @pl.when(pl.program_id(0) == pl.num_programs(0) - 1)
        def _():
            passed = []
            for j, chip in enumerate(chips):
                for t in range(nt):
                    copy(t, 1 + j, (*chip, c), me).wait_recv()
                    fwd = copy(t, 4 + j, (*chip, c), sibling)
                    fwd.start()
                    passed.append(fwd)
            for t in range(nt):
                copy(t, 0, sibling, me).wait_recv()
                for j, chip in enumerate(chips):
                    copy(t, 4 + j, (*chip, 1 - c), me).wait_recv()
            for cp in first() + passed:
                cp.wait_send()
            for cp in mine():
                cp.wait()

    res = pl.pallas_call(
        body, name=name, grid=(t_rows // tm,),
        in_specs=[pl.BlockSpec((tm, D), lambda i: (i, 0)), pl.BlockSpec((1, D), lambda i: (0, 0))] + [ANY] * nt,
        out_specs=[pl.BlockSpec((tm, D), lambda i: (i, 0))] + [ANY] * nt,
        out_shape=[S((t_rows, D), BF16)] + [S((N_DEV,) + a.shape, a.dtype) for a in xs],
        scratch_shapes=[pltpu.SemaphoreType.DMA((nt, 7)), pltpu.SemaphoreType.DMA((nt, 7)), pltpu.SemaphoreType.DMA((nt,))],
        compiler_params=pltpu.CompilerParams(dimension_semantics=("arbitrary",), has_side_effects=True))(x_in, gain, *xs)
    return res[0], res[1:]


def _exchange(hook, name):
    comm = hook()
    ci, co = len(comm.inputs), len(comm.out_shapes)

    def body(*refs):
        cins, couts = refs[:ci], refs[ci:ci + co]
        send, recv = refs[-2:]
        comm.start(cins, couts, send, recv)
        comm.finish(cins, couts, send, recv)

    res = pl.pallas_call(
        body, name=name, in_specs=[ANY] * ci, out_specs=[ANY] * co, out_shape=comm.out_shapes,
        scratch_shapes=[pltpu.SemaphoreType.DMA((comm.n_sems,)), pltpu.SemaphoreType.DMA((comm.n_sems,))],
        input_output_aliases=dict(comm.aliases),
        compiler_params=pltpu.CompilerParams(has_side_effects=True))(*comm.inputs)
    hook(res)


def _row_tile(r):
    tr = r if r <= ROW_TILE or r % ROW_TILE else ROW_TILE
    assert r % tr == 0
    return tr


def _rs_partial(g32, sib, place, name):
    _, r, cdim = g32.shape
    tr = _row_tile(r)

    def body(place_ref, g_ref, s_ref, p_ref, own_ref):
        k = pl.program_id(1)
        tot = g_ref[...] + s_ref[...].astype(F32)
        p_ref[...] = tot.astype(BF16)

        @pl.when(k == place_ref[1])
        def _():
            own_ref[...] = tot

    grid_spec = pltpu.PrefetchScalarGridSpec(
        num_scalar_prefetch=1, grid=(r // tr, 4),
        in_specs=[pl.BlockSpec((None, None, tr, cdim), lambda i, k, pr: (k, pr[0], i, 0)),
                  pl.BlockSpec((None, tr, cdim), lambda i, k, pr: (k, i, 0))],
        out_specs=[pl.BlockSpec((None, tr, cdim), lambda i, k, pr: (k, i, 0)), pl.BlockSpec((tr, cdim), lambda i, k, pr: (i, 0))])
    return pl.pallas_call(
        body, grid_spec=grid_spec, name=name,
        out_shape=[S((4, r, cdim), BF16), S((r, cdim), F32)],
        compiler_params=_cp("parallel", "arbitrary"))(place, g32.reshape(4, 2, r, cdim), sib)


def _rs1_partial(g32, g16, place, hook, name):
    _, r, cdim = g32.shape
    tr, nr = r, 1
    comm = hook()
    ci, co = len(comm.inputs), len(comm.out_shapes)

    def body(place_ref, g_ref, g16_ref, *refs):
        cins, refs = refs[:ci], refs[ci:]
        p_ref, own_ref, land_ref = refs[:3]
        couts, (sbuf, obuf, send, recv, lsem, csend, crecv) = refs[3:3 + co], refs[3 + co:]
        k, i = pl.program_id(0), pl.program_id(1)
        x, y, c, _ = _place()

        def rc(j, slot, to):
            return pltpu.make_async_remote_copy(src_ref=g16_ref.at[slot], dst_ref=land_ref.at[j], send_sem=send.at[j],
                                                recv_sem=recv.at[j], device_id=to, device_id_type=MESH)

        @pl.when((k == 0) & (i == 0))
        def _():
            for j in range(4):
                rc(j, 2 * j + (1 - c), (x, y, 1 - c)).start()
            comm.start(cins, couts, csend, crecv)

        for j in range(4):
            @pl.when((k == j) & (i == 0))
            def _(j=j):
                rc(j, 2 * j + c, (x, y, c)).wait_recv()

        rows = pl.ds(pl.multiple_of(i * tr, tr), tr)
        fetch = pltpu.make_async_copy(land_ref.at[k, rows], sbuf, lsem)
        fetch.start()
        fetch.wait()
        tot = g_ref[...] + sbuf[...].astype(F32)
        p_ref[...] = tot.astype(BF16)

        @pl.when(k == place_ref[1])
        def _():
            obuf[...] = tot
            keep = pltpu.make_async_copy(obuf, own_ref.at[rows], lsem)
            keep.start()
            keep.wait()

        @pl.when((k == 3) & (i == nr - 1))
        def _():
            for j in range(4):
                rc(j, 2 * j + (1 - c), (x, y, 1 - c)).wait_send()
            comm.finish(cins, couts, csend, crecv)

    grid_spec = pltpu.PrefetchScalarGridSpec(
        num_scalar_prefetch=1, grid=(4, nr),
        in_specs=[pl.BlockSpec((None, None, tr, cdim), lambda k, i, pr: (k, pr[0], i, 0)), ANY] + [ANY] * ci,
        out_specs=[pl.BlockSpec((None, tr, cdim), lambda k, i, pr: (k, i, 0)), ANY, ANY] + [ANY] * co,
        scratch_shapes=[pltpu.VMEM((tr, cdim), BF16), pltpu.VMEM((tr, cdim), F32), pltpu.SemaphoreType.DMA((4,)),
                        pltpu.SemaphoreType.DMA((4,)), pltpu.SemaphoreType.DMA(()), pltpu.SemaphoreType.DMA((comm.n_sems,)),
                        pltpu.SemaphoreType.DMA((comm.n_sems,))])
    res = pl.pallas_call(
        body, grid_spec=grid_spec, name=name,
        out_shape=[S((4, r, cdim), BF16), S((r, cdim), F32), S((4, r, cdim), BF16)] + comm.out_shapes,
        input_output_aliases={3 + k: 3 + v for k, v in comm.aliases.items()},
        compiler_params=pltpu.CompilerParams(dimension_semantics=("arbitrary", "arbitrary"), has_side_effects=True))(
            place, g32.reshape(4, 2, r, cdim), g16, *comm.inputs)
    hook(res[3:])
    return res[0], res[1]


def _adamw_math(w, g, m, v):
    m = ADAM_B1 * m + (1.0 - ADAM_B1) * g
    v = ADAM_B2 * v + (1.0 - ADAM_B2) * (g * g)
    m_hat = m / (1.0 - ADAM_B1 ** ADAM_STEP)
    v_hat = v / (1.0 - ADAM_B2 ** ADAM_STEP)
    delta = -ADAM_LR * (m_hat / (jnp.sqrt(v_hat) + ADAM_EPS) + ADAM_WD * w)
    return delta, m, v


def _adamw_shard(owns, recvs, w, m, v, name, flipped=False):
    nl = w.shape[0]
    r, cdim = owns[0].shape
    tr = _row_tile(r)
    nr = r // tr

    def body(*refs):
        own_refs, recv_refs = refs[:nl], refs[nl:2 * nl]
        w_ref, m_ref, v_ref, g_out, d_out, m_out, v_out = refs[2 * nl:]
        layer = pl.program_id(0)
        g = None
        for l in range(nl):
            gl = own_refs[l][...] + recv_refs[l][0].astype(F32) + recv_refs[l][1].astype(F32) + recv_refs[l][2].astype(F32)
            g = gl if g is None else jnp.where(layer == l, gl, g)
        if flipped:
            g = g.T
        g_out[...] = g
        d_out[...], m_out[...], v_out[...] = _adamw_math(w_ref[...], g, m_ref[...], v_ref[...])

    park = lambda l: (lambda layer, i: (jnp.where(layer == l, i, jnp.where(layer < l, 0, nr - 1)), 0))
    park3 = lambda l: (lambda layer, i: (0, jnp.where(layer == l, i, jnp.where(layer < l, 0, nr - 1)), 0))
    if flipped:
        row = pl.BlockSpec((None, cdim, tr), lambda layer, i: (layer, 0, i))
    else:
        row = pl.BlockSpec((None, tr, cdim), lambda layer, i: (layer, i, 0))
    return pl.pallas_call(
        body, grid=(nl, nr), name=name,
        in_specs=[pl.BlockSpec((tr, cdim), park(l)) for l in range(nl)] + [pl.BlockSpec((3, tr, cdim), park3(l)) for l in range(nl)]
        + [row, row, row],
        out_specs=[row] * 4, out_shape=[S(w.shape, F32)] * 4,
        compiler_params=_cp("arbitrary", "arbitrary"))(*owns, *recvs, w, m, v)


def _adamw_small(galls, ws, ms, vs, name):
    n = len(galls)

    def body(*refs):
        g_refs, w_refs, m_refs, v_refs, outs = refs[:n], refs[n:2 * n], refs[2 * n:3 * n], refs[3 * n:4 * n], refs[4 * n:]
        for i in range(n):
            g = g_refs[i][0].astype(F32)
            for s in range(1, N_DEV):
                g = g + g_refs[i][s].astype(F32)
            outs[i][...] = g
            outs[n + i][...], outs[2 * n + i][...], outs[3 * n + i][...] = _adamw_math(w_refs[i][...], g, m_refs[i][...], v_refs[i][...])

    res = pl.pallas_call(body, out_shape=[S(a.shape, F32) for a in ws] * 4, name=name)(*galls, *ws, *ms, *vs)
    return [res[k * n:(k + 1) * n] for k in range(4)]


REPLICATED = ["mix_norm", "ffn_norm", "sgu_v_gain", "sgu_w_s", "sgu_b_s", "attn_q_gain", "attn_k_gain", "attn_sinks", "rel_bias",
              "ffn_conv_b"]
WEIGHTS = ["mix_norm", "ffn_norm", "sgu_w_in", "sgu_v_gain", "sgu_w_s", "sgu_b_s", "sgu_w_out", "attn_w_qkv", "attn_q_gain",
           "attn_k_gain", "attn_sinks", "attn_w_o", "rel_bias", "ffn_w_up", "ffn_conv_w", "ffn_conv_b", "ffn_w_down"]
SMALL = ["g_" + n for n in REPLICATED]
BF16_TRANSIT = {"sgu_w_s"}
SMALL_ATTN = ["g_attn_q_gain", "g_attn_k_gain", "g_attn_sinks", "g_rel_bias"]
SMALL_FFN = ["g_ffn_norm", "g_ffn_conv_b"]
SMALL_SGU = ["g_sgu_v_gain", "g_sgu_w_s", "g_sgu_b_s"]

GATHER_FIRST = ["sgu_w_in", "ffn_conv_w"]
UP0_SPLIT, UP1_SPLIT = 352, 352
RS1_WITH_SUMS = {"after_dw": "sgu_w_in"}
PLAN = {
    "sgu_in": [("ag1", "sgu_w_out"), ("ag1", "ffn_w_up0", (0, UP0_SPLIT))],
    "sgu_gate": [("ag2", "sgu_w_out"), ("ag1", "ffn_w_up0", (UP0_SPLIT, D))],
    "sgu_out": [("ag2", "ffn_w_up0"), ("ag1", "ffn_w_down0")],
    "before_ffn0": [("ag2", "ffn_w_down0")],
    "ffn0_fwd": [("agd", "attn_w_qkv"), ("ag1", "attn_w_o"), ("ag1", "ffn_w_down1"), ("ag1", "ffn_w_up1", (0, UP1_SPLIT))],
    "attn": [("ag2", "attn_w_o"), ("ag2", "ffn_w_down1"), ("ag1", "ffn_w_up1", (UP1_SPLIT, D))],
    "attn_out": [("ag2", "ffn_w_up1")],
    "attn_bwd": [("rs1", "ffn_w_down1"), ("rs1", "ffn_w_up1"), ("rs1", "attn_w_o")],
    "ffn0_bwd1": [("rs2", "ffn_w_down1"), ("rs2", "attn_w_o"), ("rs1", "attn_w_qkv")] + [("ag1", n) for n in SMALL_ATTN],
    "ffn0_bwd2": [("rs2", "ffn_w_up1"), ("rs2", "attn_w_qkv"), ("rs1", "ffn_w_down0")] + [("ag2", n) for n in SMALL_ATTN],
    "ffn0_dw_up": [("rs2", "ffn_w_down0")],
    "sgu_dgated": [("rs1", "ffn_w_up0")] + [("agd", n) for n in SMALL_FFN],
    "sgu_gate_bwd": [("rs2", "ffn_w_up0"), ("rs1", "sgu_w_out")],
    "dw_sgu_in": [("rs2", "sgu_w_out")] + [("ag1", n) for n in SMALL_SGU],
    "after_dw": [("rs1", "ffn_conv_w")] + [("ag2", n) for n in SMALL_SGU],
    "dx_sgu_in": [("rs2", "sgu_w_in"), ("rs2", "ffn_conv_w")],
    "last": [("agd", "g_mix_norm")],
}


class _Overlap:
    def __init__(self, shard, place):
        self.shard, self.place = shard, place
        self.part, self.full = {}, {}
        self.grads, self.sib, self.own, self.recv, self.sums = {}, {}, {}, {}, {}

    def w(self, n):
        return self.full[n]

    def grad(self, n, pair):
        self.grads[n] = pair

    def small(self, g_rep):
        self.shard.update(("g_" + n, a.astype(BF16) if n in BF16_TRANSIT else a) for n, a in _views2d(g_rep).items())

    def sync(self, point):
        if point in RS1_WITH_SUMS:
            n = RS1_WITH_SUMS[point]
            self.sums[n], self.own[n] = _rs1_partial(*self.grads[n], self.place, self.hook(point), point)
        else:
            _exchange(self.hook(point), point)

    def first_norm(self, x, gain):
        hn, full = _allgather([self.shard[n] for n in GATHER_FIRST], x, gain, "gather_first")
        self.full.update(zip(GATHER_FIRST, full))
        return hn

    def chip_sums(self, n):
        if n in self.sums:
            return self.sums.pop(n)
        sums, self.own[n] = _rs_partial(self.grads[n][0], self.sib.pop(n), self.place, "rs_partial_" + n)
        return sums

    def hook(self, host):
        ops = PLAN.get(host)
        if not ops:
            return None
        where = {"ag1": self.part, "ag2": self.full, "agd": self.full, "rs1": self.sib, "rs2": self.recv}
        idx = []

        def hook(results=None):
            if results is not None:
                for (kind, n, *_), i in zip(ops, idx):
                    where[kind][n] = results[i]
                return None
            comm = _Comm()
            for kind, n, *rows in ops:
                arr = {"ag1": lambda: self.shard[n], "agd": lambda: self.shard[n], "ag2": lambda: self.part.pop(n),
                       "rs1": lambda: self.grads[n][1], "rs2": lambda: self.chip_sums(n)}[kind]()
                idx.append(comm.add(kind, arr, *rows, into=self.part.pop(n) if rows and rows[0][0] else None))
            return comm

        return hook


TRANSPOSED = {"attn_w_qkv"}
PHYSICAL_T = {"ffn_w_up"}
SHARDED = {
    "sgu_w_in": ["sgu_w_in"], "sgu_w_out": ["sgu_w_out"], "attn_w_qkv": ["attn_w_qkv"], "attn_w_o": ["attn_w_o"],
    "ffn_w_up": ["ffn_w_up0", "ffn_w_up1"], "ffn_w_down": ["ffn_w_down0", "ffn_w_down1"], "ffn_conv_w": ["ffn_conv_w"],
}


def _send_views(w):
    out = {"ffn_conv_w": w["ffn_conv_w"].reshape(6, -1)}
    for name, parts in SHARDED.items():
        if name != "ffn_conv_w":
            out.update((p, (w[name][l].T if name in TRANSPOSED else w[name][l]).astype(BF16)) for l, p in enumerate(parts))
    return out


def _views2d(d):
    return {n: d[n].reshape(-1, d[n].shape[-1]) for n in REPLICATED if n in d}


def kernel(x, mix_norm, ffn_norm, sgu_w_in, sgu_v_gain, sgu_w_s, sgu_b_s, sgu_w_out, attn_w_qkv, attn_q_gain, attn_k_gain, attn_sinks, attn_w_o, rel_bias, ffn_w_up, ffn_conv_w, ffn_conv_b, ffn_w_down, loss_target, m_mix_norm, m_ffn_norm, m_sgu_w_in, m_sgu_v_gain, m_sgu_w_s, m_sgu_b_s, m_sgu_w_out, m_attn_w_qkv, m_attn_q_gain, m_attn_k_gain, m_attn_sinks, m_attn_w_o, m_rel_bias, m_ffn_w_up, m_ffn_conv_w, m_ffn_conv_b, m_ffn_w_down, v_mix_norm, v_ffn_norm, v_sgu_w_in, v_sgu_v_gain, v_sgu_w_s, v_sgu_b_s, v_sgu_w_out, v_attn_w_qkv, v_attn_q_gain, v_attn_k_gain, v_attn_sinks, v_attn_w_o, v_rel_bias, v_ffn_w_up, v_ffn_conv_w, v_ffn_conv_b, v_ffn_w_down):
    w = dict(zip(WEIGHTS, (mix_norm, ffn_norm, sgu_w_in, sgu_v_gain, sgu_w_s, sgu_b_s, sgu_w_out, attn_w_qkv, attn_q_gain, attn_k_gain,
                           attn_sinks, attn_w_o, rel_bias, ffn_w_up, ffn_conv_w, ffn_conv_b, ffn_w_down)))
    m = dict(zip(WEIGHTS, (m_mix_norm, m_ffn_norm, m_sgu_w_in, m_sgu_v_gain, m_sgu_w_s, m_sgu_b_s, m_sgu_w_out, m_attn_w_qkv, m_attn_q_gain,
                           m_attn_k_gain, m_attn_sinks, m_attn_w_o, m_rel_bias, m_ffn_w_up, m_ffn_conv_w, m_ffn_conv_b, m_ffn_w_down)))
    v = dict(zip(WEIGHTS, (v_mix_norm, v_ffn_norm, v_sgu_w_in, v_sgu_v_gain, v_sgu_w_s, v_sgu_b_s, v_sgu_w_out, v_attn_w_qkv, v_attn_q_gain,
                           v_attn_k_gain, v_attn_sinks, v_attn_w_o, v_rel_bias, v_ffn_w_up, v_ffn_conv_w, v_ffn_conv_b, v_ffn_w_down)))
    rep = {n: w[n] for n in REPLICATED}

    xi, yi, ci = lax.axis_index("x"), lax.axis_index("y"), lax.axis_index("c")
    place = jnp.stack([ci, 2 * xi + yi]).astype(jnp.int32)
    sch = _Overlap(_send_views(w), place)

    loss, grad_x, g_rep = _local_step(x[0], loss_target[0], rep, sch)
    loss = lax.psum(loss, ("x", "y", "c"))
    sch.sync("last")

    out = [{}, {}, {}, {}]
    for name, parts in SHARDED.items():
        flip = (lambda a: jnp.swapaxes(a, -1, -2)) if name in TRANSPOSED | PHYSICAL_T else (lambda a: a)
        shape = flip(w[name]).shape
        as3d = lambda a: flip(a).reshape(len(parts), -1, shape[-1])
        res = _adamw_shard([sch.own[p] for p in parts], [sch.recv[p] for p in parts], as3d(w[name]), as3d(m[name]), as3d(v[name]),
                           "adamw_" + name, flipped=name in PHYSICAL_T)
        for o, r in zip(out, res):
            o[name] = flip(r.reshape(shape))
    small = _adamw_small([sch.full[n] for n in SMALL], *[list(_views2d(d).values()) for d in (rep, m, v)], "adamw_small")
    for o, res in zip(out, small):
        o.update((n, r.reshape(w[n].shape)) for n, r in zip(REPLICATED, res))

    return (loss, grad_x[None], *[out[0][n] for n in WEIGHTS], *[out[1][n] for n in WEIGHTS],
            *[out[2][n] for n in WEIGHTS], *[out[3][n] for n in WEIGHTS])
```

```python
import functools
import math

import numpy as np
import jax
import jax.numpy as jnp
from jax import lax
from jax.experimental import pallas as pl
from jax.experimental.pallas import tpu as pltpu

F32 = jnp.float32
BF16 = jnp.bfloat16
DH = jnp.bfloat16
S = jax.ShapeDtypeStruct

D = 1024
CHUNK = 128
SGU_W = 2048
SGU_G = 16
HD = 64
NH = 16
NKV = 4
KVG = 4
D_FF = 2816
REL_BUCKETS = 32
REL_MAX_DIST = 128
EPS = 1e-6
N_DEV = 8
MESH = pl.DeviceIdType.MESH

ADAM_LR = 0.001
ADAM_B1 = 0.9
ADAM_B2 = 0.999
ADAM_EPS = 1e-08
ADAM_WD = 0.01
ADAM_STEP = 10

ROW_TILE = 512
HALO = 8
FFN_ROWS = 256


def _tm(t):
    return min(ROW_TILE, t)


def _cp(*sem):
    return pltpu.CompilerParams(dimension_semantics=sem)


ANY = pl.BlockSpec(memory_space=pl.ANY)


def _place():
    x, y, c = lax.axis_index("x"), lax.axis_index("y"), lax.axis_index("c")
    return x, y, c, [(1 - x, y), (x, 1 - y), (1 - x, 1 - y)]


class _Comm:
    SEMS = {"ag1": 5, "ag2": 3, "rs1": 4, "rs2": 3, "agd": 8}

    def __init__(self):
        self.inputs, self.out_shapes, self.aliases, self.ops, self.n_sems = [], [], {}, [], 0

    def add(self, kind, arr, rows=None, into=None):
        lead = {"ag1": N_DEV, "agd": N_DEV, "ag2": None, "rs1": 4, "rs2": 3}[kind]
        shape = arr.shape if lead is None else (lead,) + arr.shape[(0 if kind in ("ag1", "agd") else 1):]
        if kind == "ag2":
            self.aliases[len(self.inputs)] = len(self.out_shapes)
        self.ops.append((kind, len(self.inputs), len(self.out_shapes), self.n_sems, rows))
        self.inputs.append(arr)
        if into is not None:
            self.aliases[len(self.inputs)] = len(self.out_shapes)
            self.inputs.append(into)
        self.out_shapes.append(S(shape, arr.dtype))
        self.n_sems += self.SEMS[kind]
        return len(self.out_shapes) - 1

    def _copies(self, ins, outs, send, recv):
        x, y, c, chips = _place()
        me, sibling = (x, y, c), (x, y, 1 - c)
        slot = lambda px, py, pc: 4 * px + 2 * py + pc
        sends, recvs, local = [], [], []

        def rc(src, dst, k, to):
            return lambda: pltpu.make_async_remote_copy(src_ref=src(), dst_ref=dst(), send_sem=send.at[k], recv_sem=recv.at[k],
                                                        device_id=to, device_id_type=MESH)

        for kind, ii, oi, b, rows in self.ops:
            src, dst = ins[ii], outs[oi]
            at = lambda ref, i: (lambda: ref.at[i])
            if kind == "ag1":
                part = slice(None) if rows is None else pl.ds(rows[0], rows[1] - rows[0])
                to = lambda i, d=dst, p=part: (lambda: d.at[i, p])
                whole, mine = (lambda s=src, p=part: s.at[p]), to(slot(*me))
                sends.append(rc(whole, mine, b, sibling))
                recvs.append(rc(whole, to(slot(x, y, 1 - c)), b, me))
                for j, chip in enumerate(chips):
                    sends.append(rc(whole, mine, b + 1 + j, (*chip, c)))
                    recvs.append(rc(whole, to(slot(*chip, c)), b + 1 + j, me))
                local.append(lambda s=whole, m=mine, k=b + 4: pltpu.make_async_copy(s(), m(), send.at[k]))
            elif kind == "ag2":
                for j, chip in enumerate(chips):
                    sends.append(rc(at(dst, slot(*chip, c)), at(dst, slot(*chip, c)), b + j, sibling))
                    recvs.append(rc(at(dst, slot(*chip, 1 - c)), at(dst, slot(*chip, 1 - c)), b + j, me))
            elif kind == "agd":
                whole, mine = (lambda s=src: s), at(dst, slot(*me))
                flip = lambda v, bit: 1 - v if bit else v
                for k in range(1, N_DEV):
                    peer = (flip(x, k >> 2), flip(y, (k >> 1) & 1), flip(c, k & 1))
                    sends.append(rc(whole, mine, b + k - 1, peer))
                    recvs.append(rc(whole, at(dst, slot(*peer)), b + k - 1, me))
                local.append(lambda s=src, m=mine, k=b + 7: pltpu.make_async_copy(s, m(), send.at[k]))
            elif kind == "rs1":
                for k in range(4):
                    sends.append(rc(at(src, 2 * k + (1 - c)), at(dst, k), b + k, sibling))
                    recvs.append(rc(at(src, 2 * k + c), at(dst, k), b + k, me))
            else:
                for j, (px, py) in enumerate(chips):
                    sends.append(rc(at(src, 2 * px + py), at(dst, j), b + j, (px, py, c)))
                    recvs.append(rc(at(src, 2 * px + py), at(dst, j), b + j, me))
        return sends, recvs, local

    def start(self, ins, outs, send, recv):
        sends, _, local = self._copies(ins, outs, send, recv)
        for make in local + sends:
            make().start()

    def finish(self, ins, outs, send, recv):
        sends, recvs, local = self._copies(ins, outs, send, recv)
        for make in recvs:
            make().wait_recv()
        for make in sends:
            make().wait_send()
        for make in local:
            make().wait()


def _run(body, args, hook, *, grid, in_specs, out_specs, out_shape, name, semantics, scratch_shapes=(), aliases=None):
    comm = hook() if hook is not None else None
    aliases = dict(aliases or {})
    if comm is None:
        return pl.pallas_call(body, grid=grid, in_specs=in_specs, out_specs=out_specs, out_shape=out_shape, name=name,
                              scratch_shapes=list(scratch_shapes), input_output_aliases=aliases,
                              compiler_params=_cp(*semantics))(*args)
    single = not isinstance(out_shape, (list, tuple))
    out_shapes = [out_shape] if single else list(out_shape)
    out_specs_l = [out_specs] if single else list(out_specs)
    n_in, n_out, n_scr, ci, co = len(args), len(out_shapes), len(scratch_shapes), len(comm.inputs), len(comm.out_shapes)

    def wrapped(*refs):
        ins, cins = refs[:n_in], refs[n_in:n_in + ci]
        outs, couts = refs[n_in + ci:n_in + ci + n_out], refs[n_in + ci + n_out:n_in + ci + n_out + co]
        scr = refs[n_in + ci + n_out + co:n_in + ci + n_out + co + n_scr]
        send, recv = refs[-2:]
        first = functools.reduce(lambda a, b: a & b, [pl.program_id(a) == 0 for a in range(len(grid))])
        last = functools.reduce(lambda a, b: a & b, [pl.program_id(a) == g - 1 for a, g in enumerate(grid)])

        @pl.when(first)
        def _():
            comm.start(cins, couts, send, recv)

        body(*ins, *outs, *scr)

        @pl.when(last)
        def _():
            comm.finish(cins, couts, send, recv)

    res = pl.pallas_call(
        wrapped, grid=grid, in_specs=list(in_specs) + [ANY] * ci, out_specs=out_specs_l + [ANY] * co,
        out_shape=out_shapes + comm.out_shapes, name=name,
        scratch_shapes=list(scratch_shapes) + [pltpu.SemaphoreType.DMA((comm.n_sems,)), pltpu.SemaphoreType.DMA((comm.n_sems,))],
        input_output_aliases={**aliases, **{n_in + k: n_out + v for k, v in comm.aliases.items()}},
        compiler_params=pltpu.CompilerParams(dimension_semantics=("arbitrary",) * len(grid), has_side_effects=True))(*args, *comm.inputs)
    hook(res[n_out:])
    return res[0] if single else list(res[:n_out])


def _dot(a, b):
    return jnp.dot(a, b, preferred_element_type=F32)


def _dot_nt(a, b):
    return lax.dot_general(a, b, (((1,), (1,)), ((), ())), preferred_element_type=F32)


def _dot_tn(a, b):
    return lax.dot_general(a, b, (((0,), (0,)), ((), ())), preferred_element_type=F32)


def _gelu(x):
    return 0.5 * x * (1.0 + lax.erf(x * (2.0 ** -0.5)))


def _gelu_and_grad(x):
    cdf = 0.5 * (1.0 + lax.erf(x * (2.0 ** -0.5)))
    return x * cdf, cdf + x * jnp.exp(-0.5 * x * x) * (1.0 / math.sqrt(2.0 * math.pi))


def _sigmoid(x):
    return 1.0 / (1.0 + jnp.exp(-x))


def _rstd(x):
    return lax.rsqrt(jnp.mean(x * x, axis=-1, keepdims=True) + EPS)


def _rel_tables():
    q = np.arange(CHUNK)[:, None] + CHUNK
    k = np.arange(2 * CHUNK)[None, :]
    dist = q - k
    n = np.maximum(dist, 0)
    max_exact = REL_BUCKETS // 2
    large = max_exact + (np.log(np.maximum(n, 1).astype(np.float32) / max_exact)
                         / math.log(REL_MAX_DIST / max_exact) * (REL_BUCKETS - max_exact)).astype(np.int32)
    large = np.minimum(large, REL_BUCKETS - 1)
    return np.where(n < max_exact, n, large).astype(np.int32)


def _rmsnorm(x, gain, name):
    t = x.shape[0]
    tm = _tm(t)

    def body(x_ref, g_ref, o_ref):
        xv = x_ref[...]
        o_ref[...] = (xv * _rstd(xv) * g_ref[...]).astype(BF16)

    return pl.pallas_call(
        body, grid=(t // tm,), name=name,
        in_specs=[pl.BlockSpec((tm, D), lambda i: (i, 0)), pl.BlockSpec((1, D), lambda i: (0, 0))],
        out_specs=pl.BlockSpec((tm, D), lambda i: (i, 0)),
        out_shape=S((t, D), BF16), compiler_params=_cp("parallel"))(x, gain)


def _resident(shape):
    zeros = (0,) * len(shape)
    return pl.BlockSpec(shape, lambda *_: zeros, pipeline_mode=pl.Buffered(1))


def _mm_slot(hn, wg, out_dtype, name, hook=None):
    t, k = hn.shape
    ns, _, n = wg.shape
    tm = _tm(t)

    def body(a_ref, w_ref, o_ref):
        a = a_ref[...]
        for s in range(ns):
            o_ref[s] = _dot(a, w_ref[s]).astype(out_dtype)

    return _run(
        body, [hn, wg], hook, grid=(t // tm,), name=name, semantics=("parallel",),
        in_specs=[pl.BlockSpec((tm, k), lambda i: (i, 0)), _resident(wg.shape)],
        out_specs=pl.BlockSpec((ns, tm, n), lambda i: (0, i, 0)), out_shape=S((ns, t, n), out_dtype))


def _mm_t(hn, wt, name, hook=None):
    t, k = hn.shape
    ns, n, _ = wt.shape
    tm = _tm(t)

    def body(a_ref, w_ref, o_ref):
        a = a_ref[...]
        for s in range(ns):
            o_ref[s * n:(s + 1) * n, :] = _dot_nt(w_ref[s], a)

    return _run(
        body, [hn, wt], hook, grid=(t // tm,), name=name, semantics=("parallel",),
        in_specs=[pl.BlockSpec((tm, k), lambda i: (i, 0)), _resident(wt.shape)],
        out_specs=pl.BlockSpec((ns * n, tm), lambda i: (0, i)), out_shape=S((ns * n, t), F32))


def _conv3(a, prev, cw, cb, tm):
    ext = jnp.concatenate([prev, a], axis=0)
    return cw[2:3] * a + cw[1:2] * ext[HALO - 1:HALO - 1 + tm] + cw[0:1] * ext[HALO - 2:HALO - 2 + tm] + cb


def _ffn_fwd(hn, h, wup, wdown, cw, cb, extra, mode, name, hook=None):
    t, k = hn.shape
    n = wup.shape[-1]
    nh = wup.shape[0] // 2
    tm = min(FFN_ROWS, t)
    ni = t // tm

    def body(a_ref, h_ref, wu_ref, wd_ref, cw_ref, cb_ref, e_ref, as_ref, cs_ref, o1_ref, o2_ref, carry):
        i = pl.program_id(0)

        @pl.when(i == 0)
        def _():
            carry[...] = jnp.zeros_like(carry)

        a = a_ref[...]
        acc = h_ref[...]
        nxt = (_dot(a, wu_ref[0]), _dot(a, wu_ref[nh]))
        for j in range(nh):
            ag, av = nxt
            if j + 1 < nh:
                nxt = (_dot(a, wu_ref[j + 1]), _dot(a, wu_ref[nh + j + 1]))
            as_ref[j] = ag.astype(BF16)
            as_ref[nh + j] = av.astype(BF16)
            cg = _conv3(ag, carry[j], cw_ref[j], cb_ref[j], tm)
            cv = _conv3(av, carry[nh + j], cw_ref[nh + j], cb_ref[nh + j], tm)
            carry[j] = ag[tm - HALO:]
            carry[nh + j] = av[tm - HALO:]
            cs_ref[j] = cg.astype(BF16)
            cs_ref[nh + j] = cv.astype(BF16)
            act = (cg * _sigmoid(cg) * cv).astype(BF16)
            acc = acc + _dot(act, wd_ref[j * n:(j + 1) * n, :])
        if mode == "norm":
            o1_ref[...] = acc
            o2_ref[...] = (acc * _rstd(acc) * e_ref[...]).astype(BF16)
        else:
            err = acc - e_ref[...]
            o1_ref[...] = (err * (1.0 / D)).astype(o1_ref.dtype)
            o2_ref[...] = jnp.full(o2_ref.shape, jnp.sum(err * err), F32)

    row = pl.BlockSpec((tm, D), lambda i: (i, 0))
    if mode == "norm":
        e_spec, o2_spec, o2_shape = pl.BlockSpec((1, D), lambda i: (0, 0)), row, S((t, D), BF16)
    else:
        e_spec, o2_spec, o2_shape = row, pl.BlockSpec((None, 8, 128), lambda i: (i, 0, 0)), S((ni, 8, 128), F32)
    aspec = pl.BlockSpec((2 * nh, tm, n), lambda i: (0, i, 0))
    return _run(
        body, [hn, h, wup, wdown, cw, cb, extra], hook, grid=(ni,), name=name, semantics=("arbitrary",),
        in_specs=[pl.BlockSpec((tm, k), lambda i: (i, 0)), row, _resident(wup.shape), _resident(wdown.shape),
                  _resident(cw.shape), _resident(cb.shape), e_spec],
        out_specs=[aspec, aspec, row, o2_spec],
        out_shape=[S((2 * nh, t, n), BF16), S((2 * nh, t, n), BF16), S((t, D), F32 if mode == "norm" else DH), o2_shape],
        scratch_shapes=[pltpu.VMEM((2 * nh, HALO, n), F32)])


def _tril_mask():
    r = lax.broadcasted_iota(jnp.int32, (CHUNK, CHUNK), 0)
    c = lax.broadcasted_iota(jnp.int32, (CHUNK, CHUNK), 1)
    return r >= c


def _sgu_gate_fwd(a_s, vgain, ws, bst, name, hook=None):
    t = a_s.shape[1]
    sw = a_s.shape[2]
    gps = sw // CHUNK

    def body(a_ref, vg_ref, ws_ref, b_ref, o_ref):
        v = _gelu(jnp.concatenate([a_ref[4 + s].astype(F32) for s in range(4)], axis=1))
        vn = (v * _rstd(v) * vg_ref[...]).astype(BF16)
        tri = _tril_mask()
        for g in range(SGU_G):
            w = jnp.where(tri, ws_ref[g], 0.0).astype(BF16)
            sg = _dot(w, vn[:, g * CHUNK:(g + 1) * CHUNK]) + b_ref[:, g:g + 1]
            lo = (g % gps) * CHUNK
            u = _gelu(a_ref[g // gps, :, lo:lo + CHUNK].astype(F32))
            o_ref[g // gps, :, lo:lo + CHUNK] = (u * sg).astype(BF16)

    return _run(
        body, [a_s, vgain, ws, bst], hook, grid=(t // CHUNK,), name=name, semantics=("parallel",),
        in_specs=[pl.BlockSpec((8, CHUNK, sw), lambda n: (0, n, 0)), pl.BlockSpec((1, SGU_W), lambda n: (0, 0)),
                  pl.BlockSpec((SGU_G, CHUNK, CHUNK), lambda n: (0, 0, 0)), pl.BlockSpec((CHUNK, SGU_G), lambda n: (0, 0))],
        out_specs=pl.BlockSpec((4, CHUNK, sw), lambda n: (0, n, 0)), out_shape=S((4, t, sw), BF16))


def _resid_mm(a_s, w, resid, extra, mode, name, hook=None, fm=False):
    nk, t, kc = (1, a_s.shape[1], a_s.shape[0]) if fm else a_s.shape
    tm = _tm(t)
    ni = t // tm

    def body(a_ref, w_ref, r_ref, e_ref, o1_ref, o2_ref):
        h = r_ref[...]
        if fm:
            h = h + _dot_tn(a_ref[...], w_ref[...])
        for j in range(0 if fm else nk):
            h = h + _dot(a_ref[j], w_ref[j * kc:(j + 1) * kc, :])
        if mode == "norm":
            o1_ref[...] = h
            o2_ref[...] = (h * _rstd(h) * e_ref[...]).astype(BF16)
        else:
            err = h - e_ref[...]
            o1_ref[...] = (err * (1.0 / D)).astype(o1_ref.dtype)
            o2_ref[...] = jnp.full(o2_ref.shape, jnp.sum(err * err), F32)

    row = pl.BlockSpec((tm, D), lambda i: (i, 0))
    if mode == "norm":
        e_spec, o2_spec, o2_shape = pl.BlockSpec((1, D), lambda i: (0, 0)), row, S((t, D), BF16)
    else:
        e_spec, o2_spec, o2_shape = row, pl.BlockSpec((None, 8, 128), lambda i: (i, 0, 0)), S((ni, 8, 128), F32)
    return _run(
        body, [a_s, w, resid, extra], hook, grid=(ni,), name=name, semantics=("parallel",),
        in_specs=[pl.BlockSpec((kc, tm), lambda i: (0, i)) if fm else pl.BlockSpec((nk, tm, kc), lambda i: (0, i, 0)),
                  _resident(w.shape), row, e_spec],
        out_specs=[row, o2_spec], out_shape=[S((t, D), F32 if mode == "norm" else DH), o2_shape])


def _relbias_fwd(rel_bias_t, bucket_row, name):
    nb = bucket_row.shape[1]

    def body(rb_ref, bk_ref, o_ref):
        onehot = (lax.broadcasted_iota(jnp.int32, (REL_BUCKETS, nb), 0) == bk_ref[...]).astype(F32)
        o_ref[...] = jnp.dot(rb_ref[...], onehot, precision=lax.Precision.HIGHEST, preferred_element_type=F32)

    return pl.pallas_call(body, out_shape=S((NH, nb), F32), name=name)(rel_bias_t, bucket_row)


def _relbias_bwd(dbias, bucket_row, name):
    nb = bucket_row.shape[1]

    def body(db_ref, bk_ref, o_ref):
        onehot = (lax.broadcasted_iota(jnp.int32, (REL_BUCKETS, nb), 0) == bk_ref[...]).astype(F32)
        o_ref[...] = lax.dot_general(db_ref[...], onehot, (((1,), (1,)), ((), ())),
                                     precision=lax.Precision.HIGHEST, preferred_element_type=F32)

    return pl.pallas_call(body, out_shape=S((NH, REL_BUCKETS), F32), name=name)(dbias, bucket_row)


QKV = D + 2 * NKV * HD
KV0 = D


def _rstd_rows(x):
    return lax.rsqrt(jnp.mean(x * x, axis=0, keepdims=True) + EPS)


def _attn_valid(n):
    kj = lax.broadcasted_iota(jnp.int32, (2 * CHUNK, CHUNK), 0)
    qi = lax.broadcasted_iota(jnp.int32, (2 * CHUNK, CHUNK), 1)
    dist = qi + CHUNK - kj
    return (dist >= 0) & (dist < CHUNK) & ((n > 0) | (kj >= CHUNK))


def _attn_band(cur_ref, prev_ref, row):
    return jnp.concatenate([prev_ref[row - KV0:row - KV0 + HD, :], cur_ref[row:row + HD, :]], axis=1)


def _attn_probs(kn_tok, qn, bias, valid, sink):
    s = _dot(kn_tok, qn) * (HD ** -0.5) + bias
    s = jnp.where(valid, s, -jnp.inf)
    m = jnp.maximum(jnp.max(s, axis=0, keepdims=True), sink)
    p = jnp.exp(s - m)
    psink = jnp.exp(sink - m)
    inv = 1.0 / (jnp.sum(p, axis=0, keepdims=True) + psink)
    return p * inv, psink * inv


def _attn_fwd(qkv_t, qg, kg, sinks, bias, name, hook=None):
    t = qkv_t.shape[1]

    def body(cur_ref, prev_ref, qg_ref, kg_ref, sink_ref, bias_ref, o_ref):
        n = pl.program_id(0)
        valid = _attn_valid(n)
        ks = [_attn_band(cur_ref, prev_ref, KV0 + HD * h) for h in range(NKV)]
        kn_toks = [(k * _rstd_rows(k) * kg_ref[...]).astype(BF16).T for k in ks]
        vbs = [_attn_band(cur_ref, prev_ref, KV0 + HD * (NKV + h)).astype(BF16) for h in range(NKV)]
        qs = [cur_ref[HD * hq:HD * (hq + 1), :] for hq in range(NH)]
        qns = [(q * _rstd_rows(q) * qg_ref[...]).astype(BF16) for q in qs]
        ps = [_attn_probs(kn_toks[hq // KVG], qns[hq], bias_ref[hq], valid, sink_ref[hq])[0] for hq in range(NH)]
        for hq in range(NH):
            o_ref[HD * hq:HD * (hq + 1), :] = _dot(vbs[hq // KVG], ps[hq].astype(BF16)).astype(BF16)

    col = pl.BlockSpec((HD, 1), lambda n: (0, 0))
    return _run(
        body, [qkv_t, qkv_t, qg, kg, sinks, bias], hook, grid=(t // CHUNK,), name=name, semantics=("parallel",),
        in_specs=[pl.BlockSpec((QKV, CHUNK), lambda n: (0, n)),
                  pl.BlockSpec((QKV - KV0, CHUNK), lambda n: (KV0 // (QKV - KV0), jnp.maximum(n - 1, 0))),
                  col, col, pl.BlockSpec(memory_space=pltpu.SMEM), pl.BlockSpec((NH, 2 * CHUNK, CHUNK), lambda n: (0, 0, 0))],
        out_specs=pl.BlockSpec((D, CHUNK), lambda n: (0, n)), out_shape=S((D, t), BF16))


def _dx_rows(dh, w, kc, out_dtype, name, hook=None):
    t = dh.shape[0]
    nk = w.shape[0] // kc
    tm = _tm(t)

    def body(d_ref, w_ref, o_ref):
        dhb = d_ref[...].astype(BF16)
        for j in range(nk):
            o_ref[j] = _dot_nt(dhb, w_ref[j * kc:(j + 1) * kc, :]).astype(out_dtype)

    return _run(
        body, [dh, w], hook, grid=(t // tm,), name=name, semantics=("parallel",),
        in_specs=[pl.BlockSpec((tm, D), lambda i: (i, 0)), _resident(w.shape)],
        out_specs=pl.BlockSpec((nk, tm, kc), lambda i: (0, i, 0)), out_shape=S((nk, t, kc), out_dtype))


def _dx_rows_t(dh, w, name, hook=None):
    t = dh.shape[0]
    k = w.shape[0]
    tm = _tm(t)

    def body(d_ref, w_ref, o_ref):
        o_ref[...] = _dot_nt(w_ref[...], d_ref[...].astype(BF16)).astype(BF16)

    return _run(
        body, [dh, w], hook, grid=(t // tm,), name=name, semantics=("parallel",),
        in_specs=[pl.BlockSpec((tm, D), lambda i: (i, 0)), _resident(w.shape)],
        out_specs=pl.BlockSpec((k, tm), lambda i: (0, i)), out_shape=S((k, t), BF16))


def _ffn_bwd1(dh, c, wdown, name, hook=None):
    ns, t, n = c.shape
    nh = ns // 2
    tm = min(FFN_ROWS, t)
    ni = t // tm

    def body(d_ref, c_ref, wd_ref, dc_ref, dw_hbm, dwb_hbm, acc, stage):
        i = pl.program_id(0)

        @pl.when(i == 0)
        def _():
            acc[...] = jnp.zeros_like(acc)

        dhb = d_ref[...].astype(BF16)
        for j in range(nh):
            dact = _dot_nt(dhb, wd_ref[j * n:(j + 1) * n, :])
            cg = c_ref[j].astype(F32)
            cv = c_ref[nh + j].astype(F32)
            sg = _sigmoid(cg)
            gs = cg * sg
            acc[j * n:(j + 1) * n, :] += _dot_tn((gs * cv).astype(BF16), dhb)
            dc_ref[j] = (dact * cv * (sg + gs * (1.0 - sg))).astype(BF16)
            dc_ref[nh + j] = (dact * gs).astype(BF16)

        @pl.when(i == ni - 1)
        def _():
            pltpu.sync_copy(acc, dw_hbm)
            for j in range(nh):
                stage[...] = acc[j * n:(j + 1) * n, :].astype(BF16)
                pltpu.sync_copy(stage, dwb_hbm.at[pl.ds(j * n, n), :])

    slab = pl.BlockSpec((ns, tm, n), lambda i: (0, i, 0))
    return _run(
        body, [dh, c, wdown], hook, grid=(ni,), name=name, semantics=("arbitrary",),
        in_specs=[pl.BlockSpec((tm, D), lambda i: (i, 0)), slab, _resident(wdown.shape)],
        out_specs=[slab, ANY, ANY], out_shape=[S((ns, t, n), BF16), S(wdown.shape, F32), S(wdown.shape, BF16)],
        scratch_shapes=[pltpu.VMEM(wdown.shape, F32), pltpu.VMEM((n, D), BF16)])


def _ffn_bwd2(dc, a, wup, cw, h, gain, dh_in, name, hook=None):
    ns, t, n = dc.shape
    tm = min(FFN_ROWS, t)
    ni = t // tm

    def body(dc_ref, a_ref, wu_ref, cw_ref, h_ref, g_ref, di_ref, da_ref, o_ref, dg_ref, dcw_ref, dcb_ref, carry, keep):
        i = pl.program_id(0)

        @pl.when(i == 0)
        def _():
            carry[...] = jnp.zeros_like(carry)
            dg_ref[...] = jnp.zeros_like(dg_ref)
            dcw_ref[...] = jnp.zeros_like(dcw_ref)
            dcb_ref[...] = jnp.zeros_like(dcb_ref)

        rsum = lambda v: jnp.sum(v, axis=0, keepdims=True)
        acc = jnp.zeros((tm, D), F32)
        for s in range(ns):
            x = dc_ref[s].astype(F32)
            ext = jnp.concatenate([x, carry[s]], axis=0)
            keep[0] = ext[1:1 + tm]
            keep[1] = ext[2:2 + tm]
            x1, x2 = keep[0], keep[1]
            cwv = cw_ref[s]
            da = (cwv[2:3] * x + cwv[1:2] * x1 + cwv[0:1] * x2).astype(BF16)
            carry[s] = x[:HALO]
            da_ref[s] = da
            acc = acc + _dot_nt(da, wu_ref[s])
            av = a_ref[s].astype(F32)
            dcw_ref[s] += jnp.concatenate([rsum(x2 * av), rsum(x1 * av), rsum(x * av)], axis=0)
            dcb_ref[s] += rsum(x)
        hv = h_ref[...]
        r = _rstd(hv)
        gg = acc * g_ref[...]
        dh_new = di_ref[...].astype(F32) + r * gg - hv * (r * r * r * jnp.mean(gg * hv, axis=-1, keepdims=True))
        o_ref[...] = dh_new.astype(o_ref.dtype)
        dg_ref[...] += jnp.sum(acc * hv * r, axis=0, keepdims=True)

    slab = pl.BlockSpec((ns, tm, n), lambda i: (0, ni - 1 - i, 0))
    row = pl.BlockSpec((tm, D), lambda i: (ni - 1 - i, 0))
    vec = pl.BlockSpec((1, D), lambda i: (0, 0))
    whole = lambda shape: pl.BlockSpec(shape, lambda i: (0,) * len(shape))
    return _run(
        body, [dc, a, wup, cw, h, gain, dh_in], hook, grid=(ni,), name=name, semantics=("arbitrary",),
        in_specs=[slab, slab, _resident(wup.shape), _resident(cw.shape), row, vec, row],
        out_specs=[slab, row, vec, whole((ns, 3, n)), whole((ns, 1, n))],
        out_shape=[S((ns, t, n), BF16), S((t, D), DH), S((1, D), F32), S((ns, 3, n), F32), S((ns, 1, n), F32)],
        scratch_shapes=[pltpu.VMEM((ns, HALO, n), F32), pltpu.VMEM((2, tm, n), F32)])


def _dw_slot(hn, dy_s, name, hook=None):
    t, k = hn.shape
    ns, _, n = dy_s.shape
    tm = _tm(t)

    def body(a_ref, b_ref, o_ref, ob_ref, at_ref):
        @pl.when(pl.program_id(0) == 0)
        def _():
            for i in range(t // tm):
                at_ref[:, i * tm:(i + 1) * tm] = a_ref[i * tm:(i + 1) * tm, :].T

        acc = _dot(at_ref[...], b_ref[...])
        o_ref[...] = acc
        ob_ref[...] = acc.astype(BF16)

    ospec = pl.BlockSpec((None, k, n), lambda j: (j, 0, 0))
    return _run(
        body, [hn, dy_s], hook, grid=(ns,), name=name, semantics=("arbitrary",),
        in_specs=[_resident(hn.shape), pl.BlockSpec((None, t, n), lambda j: (j, 0, 0))],
        out_specs=[ospec, ospec], out_shape=[S((ns, k, n), F32), S((ns, k, n), BF16)],
        scratch_shapes=[pltpu.VMEM((k, t), BF16)])


def _dw_rows(a_s, dh, name, hook=None, fm=False):
    nk, t, kc = (1, a_s.shape[1], a_s.shape[0]) if fm else a_s.shape
    tm = _tm(t)
    ni = t // tm

    def body(a_ref, d_ref, o_ref, ob_ref):
        i = pl.program_id(0)
        dhb = d_ref[...].astype(BF16)

        @pl.when(i == 0)
        def _():
            o_ref[...] = jnp.zeros_like(o_ref)

        if fm:
            o_ref[...] += _dot(a_ref[...], dhb)
        for j in range(0 if fm else nk):
            o_ref[j * kc:(j + 1) * kc, :] += _dot_tn(a_ref[j], dhb)

        @pl.when(i == ni - 1)
        def _():
            ob_ref[...] = o_ref[...].astype(BF16)

    ospec = pl.BlockSpec((nk * kc, D), lambda i: (0, 0))
    return _run(
        body, [a_s, dh], hook, grid=(ni,), name=name, semantics=("arbitrary",),
        in_specs=[pl.BlockSpec((kc, tm), lambda i: (0, i)) if fm else pl.BlockSpec((nk, tm, kc), lambda i: (0, i, 0)),
                  pl.BlockSpec((tm, D), lambda i: (i, 0))],
        out_specs=[ospec, ospec], out_shape=[S((nk * kc, D), F32), S((nk * kc, D), BF16)])


def _dx_slot_normbwd(dy_s, wg, h, gain, dh_in, name, hook=None, fm=False, out_dtype=F32):
    ns, t, n = (1, dy_s.shape[1], dy_s.shape[0]) if fm else dy_s.shape
    tm = _tm(t)

    def body(dy_ref, w_ref, h_ref, g_ref, di_ref, o_ref, dg_ref):
        i = pl.program_id(0)

        @pl.when(i == 0)
        def _():
            dg_ref[...] = jnp.zeros_like(dg_ref)

        g = _dot_tn(dy_ref[...], w_ref[...]) if fm else _dot_nt(dy_ref[0], w_ref[0])
        for s in range(1, ns):
            g = g + _dot_nt(dy_ref[s], w_ref[s])
        hv = h_ref[...]
        r = _rstd(hv)
        gg = g * g_ref[...]
        dh_new = di_ref[...].astype(F32) + r * gg - hv * (r * r * r * jnp.mean(gg * hv, axis=-1, keepdims=True))
        o_ref[...] = dh_new.astype(o_ref.dtype)
        dg_ref[...] += jnp.sum(g * hv * r, axis=0, keepdims=True)

    row = pl.BlockSpec((tm, D), lambda i: (i, 0))
    vec = pl.BlockSpec((1, D), lambda i: (0, 0))
    return _run(
        body, [dy_s, wg, h, gain, dh_in], hook, grid=(t // tm,), name=name, semantics=("arbitrary",),
        in_specs=[pl.BlockSpec((n, tm), lambda i: (0, i)) if fm else pl.BlockSpec((ns, tm, n), lambda i: (0, i, 0)),
                  _resident(wg.shape), row, vec, row],
        out_specs=[row, vec], out_shape=[S((t, D), out_dtype), S((1, D), F32)])


def _sgu_gate_bwd(a_s, dg_s, vgain, ws, bst, name, hook=None):
    t = a_s.shape[1]
    sw = a_s.shape[2]
    gps = sw // CHUNK

    def body(a_ref, dg_ref, vg_ref, ws_ref, b_ref, da_ref, dws_ref, dbt_ref, dvg_ref, dvn_ref):
        n = pl.program_id(0)

        @pl.when(n == 0)
        def _():
            dws_ref[...] = jnp.zeros_like(dws_ref)
            dbt_ref[...] = jnp.zeros_like(dbt_ref)
            dvg_ref[...] = jnp.zeros_like(dvg_ref)

        vpre = jnp.concatenate([a_ref[4 + s].astype(F32) for s in range(4)], axis=1)
        v, v_grad = _gelu_and_grad(vpre)
        r = _rstd(v)
        vhat = v * r
        vn = (vhat * vg_ref[...]).astype(BF16)
        tri = _tril_mask()
        lane = lax.broadcasted_iota(jnp.int32, (CHUNK, CHUNK), 1)
        dbt = jnp.zeros((CHUNK, CHUNK), F32)
        for g in range(SGU_G):
            w = jnp.where(tri, ws_ref[g], 0.0).astype(BF16)
            vng = vn[:, g * CHUNK:(g + 1) * CHUNK]
            sg = _dot(w, vng) + b_ref[:, g:g + 1]
            lo = (g % gps) * CHUNK
            u, u_grad = _gelu_and_grad(a_ref[g // gps, :, lo:lo + CHUNK].astype(F32))
            dgate = dg_ref[g // gps, :, lo:lo + CHUNK].astype(F32)
            da_ref[g // gps, :, lo:lo + CHUNK] = (dgate * sg * u_grad).astype(BF16)
            ds = dgate * u
            dsb = ds.astype(BF16)
            dvn_ref[:, g * CHUNK:(g + 1) * CHUNK] = _dot_tn(w, dsb)
            dws_ref[g] += jnp.where(tri, _dot_nt(dsb, vng), 0.0)
            dbt = dbt + jnp.where(lane == g, jnp.sum(ds, axis=-1, keepdims=True), 0.0)
        dbt_ref[...] += dbt
        dvn = dvn_ref[...]
        dvg_ref[...] += jnp.sum(dvn * vhat, axis=0, keepdims=True)
        gg = dvn * vg_ref[...]
        dv = r * gg - v * (r * r * r * jnp.mean(gg * v, axis=-1, keepdims=True))
        dav = (dv * v_grad).astype(BF16)
        for s in range(4):
            da_ref[4 + s] = dav[:, s * sw:(s + 1) * sw]

    return _run(
        body, [a_s, dg_s, vgain, ws, bst], hook, grid=(t // CHUNK,), name=name, semantics=("arbitrary",),
        in_specs=[pl.BlockSpec((8, CHUNK, sw), lambda n: (0, n, 0)), pl.BlockSpec((4, CHUNK, sw), lambda n: (0, n, 0)),
                  pl.BlockSpec((1, SGU_W), lambda n: (0, 0)), pl.BlockSpec((SGU_G, CHUNK, CHUNK), lambda n: (0, 0, 0)),
                  pl.BlockSpec((CHUNK, SGU_G), lambda n: (0, 0))],
        out_specs=[pl.BlockSpec((8, CHUNK, sw), lambda n: (0, n, 0)), pl.BlockSpec((SGU_G, CHUNK, CHUNK), lambda n: (0, 0, 0)),
                   pl.BlockSpec((CHUNK, CHUNK), lambda n: (0, 0)), pl.BlockSpec((1, SGU_W), lambda n: (0, 0))],
        out_shape=[S((8, t, sw), BF16), S((SGU_G, CHUNK, CHUNK), F32), S((CHUNK, CHUNK), F32), S((1, SGU_W), F32)],
        scratch_shapes=[pltpu.VMEM((CHUNK, SGU_W), F32)])


def _attn_bwd(qkv_t, do_t, qg, kg, sinks, bias, name, hook=None):
    t = qkv_t.shape[1]
    nb = t // CHUNK

    def body(cur_ref, prev_ref, do_ref, qg_ref, kg_ref, sink_ref, bias_ref,
             o_ref, dqg_out, dkg_out, dsk_out, dbias_ref, carry, dqg_ref, dkg_ref, dsk_ref):
        n = pl.program_id(0)

        @pl.when(n == 0)
        def _():
            carry[...] = jnp.zeros_like(carry)
            dqg_ref[...] = jnp.zeros_like(dqg_ref)
            dkg_ref[...] = jnp.zeros_like(dkg_ref)
            dsk_ref[...] = jnp.zeros_like(dsk_ref)
            dbias_ref[...] = jnp.zeros_like(dbias_ref)

        @pl.when(n < nb)
        def _():
            valid = _attn_valid(n)
            o_ref[0:KV0, :] = carry[0:KV0, :].astype(BF16)
            kvs, heads = range(NKV), range(NH)
            group = lambda h: range(KVG * h, KVG * (h + 1))
            ks = [_attn_band(cur_ref, prev_ref, KV0 + HD * h) for h in kvs]
            rks = [_rstd_rows(k) for k in ks]
            khats = [k * rk for k, rk in zip(ks, rks)]
            kns = [(khat * kg_ref[...]).astype(BF16) for khat in khats]
            kn_toks = [kn.T for kn in kns]
            vbs = [_attn_band(cur_ref, prev_ref, KV0 + HD * (NKV + h)).astype(BF16) for h in kvs]
            v_toks = [vb.T for vb in vbs]
            qs = [cur_ref[HD * hq:HD * (hq + 1), :] for hq in heads]
            rqs = [_rstd_rows(q) for q in qs]
            qhats = [q * rq for q, rq in zip(qs, rqs)]
            qns = [(qhat * qg_ref[...]).astype(BF16) for qhat in qhats]
            probs = [_attn_probs(kn_toks[hq // KVG], qns[hq], bias_ref[hq], valid, sink_ref[hq]) for hq in heads]
            dohs = [do_ref[HD * hq:HD * (hq + 1), :] for hq in heads]
            dps = [_dot(v_toks[hq // KVG], dohs[hq]) for hq in heads]
            dsums = [jnp.sum(p * dp, axis=0, keepdims=True) for (p, _), dp in zip(probs, dps)]
            dss = [p * (dp - dsum) for (p, _), dp, dsum in zip(probs, dps, dsums)]
            for hq in heads:
                dsk_ref[hq:hq + 1, :] -= probs[hq][1] * dsums[hq]
                dbias_ref[hq] += dss[hq]
            dvs = [sum(_dot_nt(dohs[hq], probs[hq][0].astype(BF16)) for hq in group(h)) for h in kvs]
            dscs = [(ds * (HD ** -0.5)).astype(BF16) for ds in dss]
            dqns = [_dot(kns[hq // KVG], dscs[hq]) for hq in heads]
            dkns = [sum(_dot_nt(qns[hq], dscs[hq]) for hq in group(h)) for h in kvs]
            dqg_ref[...] += sum(dqn * qhat for dqn, qhat in zip(dqns, qhats))
            for hq in heads:
                gq = dqns[hq] * qg_ref[...]
                carry[HD * hq:HD * (hq + 1), :] = rqs[hq] * gq - qs[hq] * (
                    rqs[hq] * rqs[hq] * rqs[hq] * jnp.mean(gq * qs[hq], axis=0, keepdims=True))
            dkg_ref[...] += sum(dkn * khat for dkn, khat in zip(dkns, khats))
            for h in kvs:
                krow, vrow = KV0 + HD * h, KV0 + HD * (NKV + h)
                gk = dkns[h] * kg_ref[...]
                dk = rks[h] * gk - ks[h] * (rks[h] * rks[h] * rks[h] * jnp.mean(gk * ks[h], axis=0, keepdims=True))
                o_ref[krow:krow + HD, :] = (carry[krow:krow + HD, :] + dk[:, :CHUNK]).astype(BF16)
                o_ref[vrow:vrow + HD, :] = (carry[vrow:vrow + HD, :] + dvs[h][:, :CHUNK]).astype(BF16)
                carry[krow:krow + HD, :] = dk[:, CHUNK:]
                carry[vrow:vrow + HD, :] = dvs[h][:, CHUNK:]

        @pl.when(n == nb)
        def _():
            o_ref[...] = carry[...].astype(BF16)
            dqg_out[...] = jnp.sum(dqg_ref[...], axis=1, keepdims=True)
            dkg_out[...] = jnp.sum(dkg_ref[...], axis=1, keepdims=True)
            dsk_out[...] = jnp.sum(dsk_ref[...], axis=1, keepdims=True)

    cur = lambda n: (0, jnp.minimum(n, nb - 1))
    col = pl.BlockSpec((HD, 1), lambda n: (0, 0))
    whole = lambda shape: pl.BlockSpec(shape, lambda n: (0,) * len(shape))
    return _run(
        body, [qkv_t, qkv_t, do_t, qg, kg, sinks, bias], hook, grid=(nb + 1,), name=name, semantics=("arbitrary",),
        in_specs=[pl.BlockSpec((QKV, CHUNK), cur),
                  pl.BlockSpec((QKV - KV0, CHUNK), lambda n: (KV0 // (QKV - KV0), jnp.clip(n - 1, 0, nb - 1))),
                  pl.BlockSpec((D, CHUNK), cur), col, col, pl.BlockSpec(memory_space=pltpu.SMEM), whole((NH, 2 * CHUNK, CHUNK))],
        out_specs=[pl.BlockSpec((QKV, CHUNK), lambda n: (0, jnp.maximum(n - 1, 0))), whole((HD, 1)), whole((HD, 1)),
                   whole((NH, 1)), whole((NH, 2 * CHUNK, CHUNK))],
        out_shape=[S((QKV, t), BF16), S((HD, 1), F32), S((HD, 1), F32), S((NH, 1), F32), S((NH, 2 * CHUNK, CHUNK), F32)],
        scratch_shapes=[pltpu.VMEM((QKV, CHUNK), F32), pltpu.VMEM((HD, CHUNK), F32), pltpu.VMEM((HD, 2 * CHUNK), F32),
                        pltpu.VMEM((NH, CHUNK), F32)])


class _Plain:
    def __init__(self, wg):
        self.full, self.grads = wg, {}

    def w(self, n):
        return self.full[n]

    def hook(self, host):
        return None

    def grad(self, n, pair):
        self.grads[n] = pair

    def small(self, g_rep):
        pass

    def sync(self, point):
        pass

    def first_norm(self, x, gain):
        return _rmsnorm(x, gain, "norm0")


def _local_step(x, target, rep, sch):
    bucket_row = jnp.asarray(_rel_tables().T.reshape(1, -1))
    bias = _relbias_fwd(rep["rel_bias"].T, bucket_row, "relbias_fwd").reshape(NH, 2 * CHUNK, CHUNK)
    bst = rep["sgu_b_s"][0].T
    ws = rep["sgu_w_s"][0]
    vgain = rep["sgu_v_gain"]
    qg, kg, sinks = rep["attn_q_gain"].reshape(HD, 1), rep["attn_k_gain"].reshape(HD, 1), rep["attn_sinks"][0]
    w_down = lambda l: sch.w("ffn_w_down%d" % l).reshape(D_FF, D)
    w_up = lambda l: sch.w("ffn_w_up%d" % l)
    cb = [rep["ffn_conv_b"][l].reshape(8, 1, -1) for l in range(2)]
    mixg = [rep["mix_norm"][l:l + 1] for l in range(2)]
    ffng = [rep["ffn_norm"][l:l + 1] for l in range(2)]
    rows = lambda pair: tuple(g.reshape(N_DEV, -1, D) for g in pair)
    hk = sch.hook

    hn0 = sch.first_norm(x, mixg[0])
    cw = [sch.w("ffn_conv_w")[:, 3 * l:3 * l + 3] for l in range(2)]
    a0 = _mm_slot(hn0, sch.w("sgu_w_in"), BF16, "sgu_in", hk("sgu_in"))
    gated = _sgu_gate_fwd(a0, vgain, ws, bst, "sgu_gate", hk("sgu_gate"))
    h1, hn1 = _resid_mm(gated, sch.w("sgu_w_out").reshape(SGU_W, D), x, ffng[0], "norm", "sgu_out", hk("sgu_out"))
    sch.sync("before_ffn0")
    a_ff0, c_ff0, h2, hn2 = _ffn_fwd(hn1, h1, w_up(0), w_down(0), cw[0], cb[0], mixg[1], "norm", "ffn0_fwd", hk("ffn0_fwd"))
    qkv = _mm_t(hn2, sch.w("attn_w_qkv"), "qkv", hk("qkv"))
    o = _attn_fwd(qkv, qg, kg, sinks, bias, "attn", hk("attn"))
    h3, hn3 = _resid_mm(o, sch.w("attn_w_o").reshape(D, D), h2, ffng[1], "norm", "attn_out", hk("attn_out"), fm=True)
    a_ff1, c_ff1, dy, sq = _ffn_fwd(hn3, h3, w_up(1), w_down(1), cw[1], cb[1], target, "loss", "ffn1_fwd_loss", hk("ffn1_fwd_loss"))
    loss = (0.5 / D) * jnp.sum(sq[:, 0, 0])

    def ffn_bwd(dh, h_in, hn, a, c, l, tag):
        dc, g_down, g_down_b = _ffn_bwd1(dh, c, w_down(l), tag + "_bwd1", hk(tag + "_bwd1"))
        sch.grad("ffn_w_down%d" % l, rows((g_down, g_down_b)))
        da, dh_new, dgain, g_cw, g_cb = _ffn_bwd2(dc, a, w_up(l), cw[l], h_in, ffng[l], dh, tag + "_bwd2", hk(tag + "_bwd2"))
        sch.grad("ffn_w_up%d" % l, _dw_slot(hn, da, tag + "_dw_up", hk(tag + "_dw_up")))
        return dh_new, dgain, g_cw, g_cb.reshape(-1)

    dh, d_ffng1, g_cw1, g_cb1 = ffn_bwd(dy, h3, hn3, a_ff1, c_ff1, 1, "ffn1")
    do = _dx_rows_t(dh, sch.w("attn_w_o").reshape(D, D), "attn_do", hk("attn_do"))
    sch.grad("attn_w_o", rows(_dw_rows(o, dh, "dw_o", hk("dw_o"), fm=True)))
    dqkv, d_qg, d_kg, d_sk, d_bias = _attn_bwd(qkv, do, qg, kg, sinks, bias, "attn_bwd", hk("attn_bwd"))
    sch.grad("attn_w_qkv", tuple(g.reshape(N_DEV, -1, D) for g in _dw_rows(dqkv, hn2, "dw_qkv", hk("dw_qkv"), fm=True)))
    dh, d_mixg1 = _dx_slot_normbwd(dqkv, sch.w("attn_w_qkv").reshape(QKV, D), h2, mixg[1], dh, "dx_qkv", hk("dx_qkv"), fm=True,
                                   out_dtype=DH)
    d_relb = _relbias_bwd(d_bias.reshape(NH, -1), bucket_row, "relbias_bwd").T
    g_rep = {"attn_q_gain": d_qg.reshape(1, HD), "attn_k_gain": d_kg.reshape(1, HD), "attn_sinks": d_sk.reshape(1, NH),
             "rel_bias": d_relb}
    sch.small(g_rep)
    dh, d_ffng0, g_cw0, g_cb0 = ffn_bwd(dh, h1, hn1, a_ff0, c_ff0, 0, "ffn0")
    g_cw = jnp.concatenate([g_cw0, g_cw1], axis=1)
    sch.grad("ffn_conv_w", (g_cw, g_cw.astype(BF16)))
    g_ffn = {"ffn_norm": jnp.concatenate([d_ffng0, d_ffng1], axis=0), "ffn_conv_b": jnp.stack([g_cb0, g_cb1], axis=0)}
    sch.small(g_ffn)
    dgated = _dx_rows(dh, sch.w("sgu_w_out").reshape(SGU_W, D), SGU_W // 4, BF16, "sgu_dgated", hk("sgu_dgated"))
    sch.grad("sgu_w_out", rows(_dw_rows(gated, dh, "dw_sgu_out", hk("dw_sgu_out"))))
    da0, d_ws, d_bst, d_vgain = _sgu_gate_bwd(a0, dgated, vgain, ws, bst, "sgu_gate_bwd", hk("sgu_gate_bwd"))
    g_sgu = {"sgu_v_gain": d_vgain, "sgu_w_s": d_ws[None], "sgu_b_s": d_bst[:, :SGU_G].T[None]}
    sch.small(g_sgu)
    sch.grad("sgu_w_in", _dw_slot(hn0, da0, "dw_sgu_in", hk("dw_sgu_in")))
    sch.sync("after_dw")
    grad_x, d_mixg0 = _dx_slot_normbwd(da0, sch.w("sgu_w_in"), x, mixg[0], dh, "dx_sgu_in", hk("dx_sgu_in"))
    g_mix = {"mix_norm": jnp.concatenate([d_mixg0, d_mixg1], axis=0)}
    sch.small(g_mix)
    for g in (g_ffn, g_sgu, g_mix):
        g_rep.update(g)
    return loss, grad_x, g_rep


def _allgather(xs, x_in, gain, name):
    nt = len(xs)
    t_rows = x_in.shape[0]
    tm = _tm(t_rows)

    def body(xin_ref, g_ref, *refs):
        x_refs, hn_ref, o_refs = refs[:nt], refs[nt], refs[nt + 1:2 * nt + 1]
        send_sems, recv_sems, local_sems = refs[2 * nt + 1:]
        x, y, c, chips = _place()
        me, sibling = (x, y, c), (x, y, 1 - c)

        def copy(t, k, block, to, src=None):
            px, py, pc = block
            dst = o_refs[t].at[4 * px + 2 * py + pc]
            return pltpu.make_async_remote_copy(
                src_ref=dst if src is None else src, dst_ref=dst, send_sem=send_sems.at[t, k], recv_sem=recv_sems.at[t, k],
                device_id=to, device_id_type=MESH)

        mine = lambda: [pltpu.make_async_copy(x_refs[t], o_refs[t].at[4 * x + 2 * y + c], local_sems.at[t]) for t in range(nt)]

        def first():
            out = []
            for t in range(nt):
                out.append(copy(t, 0, me, sibling, src=x_refs[t]))
                out += [copy(t, 1 + j, me, (*chip, c), src=x_refs[t]) for j, chip in enumerate(chips)]
            return out

        @pl.when(pl.program_id(0) == 0)
        def _():
            for cp in mine() + first():
                cp.start()

        xv = xin_ref[...]
        hn_ref[...] = (xv * _rstd(xv) * g_ref[...]).astype(BF16)

        @pl.when(pl.program_id(0) == pl.num_programs(0) - 1)
        def _():
            passed = []
            for j, chip in enumerate(chips):
                for t in range(nt):
                    copy(t, 1 + j, (*chip, c), me).wait_recv()
                    fwd = copy(t, 4 + j, (*chip, c), sibling)
                    fwd.start()
                    passed.append(fwd)
            for t in range(nt):
                copy(t, 0, sibling, me).wait_recv()
                for j, chip in enumerate(chips):
                    copy(t, 4 + j, (*chip, 1 - c), me).wait_recv()
            for cp in first() + passed:
                cp.wait_send()
            for cp in mine():
                cp.wait()

    res = pl.pallas_call(
        body, name=name, grid=(t_rows // tm,),
        in_specs=[pl.BlockSpec((tm, D), lambda i: (i, 0)), pl.BlockSpec((1, D), lambda i: (0, 0))] + [ANY] * nt,
        out_specs=[pl.BlockSpec((tm, D), lambda i: (i, 0))] + [ANY] * nt,
        out_shape=[S((t_rows, D), BF16)] + [S((N_DEV,) + a.shape, a.dtype) for a in xs],
        scratch_shapes=[pltpu.SemaphoreType.DMA((nt, 7)), pltpu.SemaphoreType.DMA((nt, 7)), pltpu.SemaphoreType.DMA((nt,))],
        compiler_params=pltpu.CompilerParams(dimension_semantics=("arbitrary",), has_side_effects=True))(x_in, gain, *xs)
    return res[0], res[1:]


def _exchange(hook, name):
    comm = hook()
    ci, co = len(comm.inputs), len(comm.out_shapes)

    def body(*refs):
        cins, couts = refs[:ci], refs[ci:ci + co]
        send, recv = refs[-2:]
        comm.start(cins, couts, send, recv)
        comm.finish(cins, couts, send, recv)

    res = pl.pallas_call(
        body, name=name, in_specs=[ANY] * ci, out_specs=[ANY] * co, out_shape=comm.out_shapes,
        scratch_shapes=[pltpu.SemaphoreType.DMA((comm.n_sems,)), pltpu.SemaphoreType.DMA((comm.n_sems,))],
        input_output_aliases=dict(comm.aliases),
        compiler_params=pltpu.CompilerParams(has_side_effects=True))(*comm.inputs)
    hook(res)


def _row_tile(r):
    tr = r if r <= ROW_TILE or r % ROW_TILE else ROW_TILE
    assert r % tr == 0
    return tr


def _rs_partial(g32, sib, place, name):
    _, r, cdim = g32.shape
    tr = _row_tile(r)

    def body(place_ref, g_ref, s_ref, p_ref, own_ref):
        k = pl.program_id(1)
        tot = g_ref[...] + s_ref[...].astype(F32)
        p_ref[...] = tot.astype(BF16)

        @pl.when(k == place_ref[1])
        def _():
            own_ref[...] = tot

    grid_spec = pltpu.PrefetchScalarGridSpec(
        num_scalar_prefetch=1, grid=(r // tr, 4),
        in_specs=[pl.BlockSpec((None, None, tr, cdim), lambda i, k, pr: (k, pr[0], i, 0)),
                  pl.BlockSpec((None, tr, cdim), lambda i, k, pr: (k, i, 0))],
        out_specs=[pl.BlockSpec((None, tr, cdim), lambda i, k, pr: (k, i, 0)), pl.BlockSpec((tr, cdim), lambda i, k, pr: (i, 0))])
    return pl.pallas_call(
        body, grid_spec=grid_spec, name=name,
        out_shape=[S((4, r, cdim), BF16), S((r, cdim), F32)],
        compiler_params=_cp("parallel", "arbitrary"))(place, g32.reshape(4, 2, r, cdim), sib)


def _rs1_partial(g32, g16, place, hook, name):
    _, r, cdim = g32.shape
    tr, nr = r, 1
    comm = hook()
    ci, co = len(comm.inputs), len(comm.out_shapes)

    def body(place_ref, g_ref, g16_ref, *refs):
        cins, refs = refs[:ci], refs[ci:]
        p_ref, own_ref, land_ref = refs[:3]
        couts, (sbuf, obuf, send, recv, fsem, lsem, csend, crecv) = refs[3:3 + co], refs[3 + co:]
        k, i = pl.program_id(0), pl.program_id(1)
        x, y, c, _ = _place()

        def rc(j, slot, to):
            return pltpu.make_async_remote_copy(src_ref=g16_ref.at[slot], dst_ref=land_ref.at[j], send_sem=send.at[j],
                                                recv_sem=recv.at[j], device_id=to, device_id_type=MESH)

        @pl.when((k == 0) & (i == 0))
        def _():
            for j in range(4):
                rc(j, 2 * j + (1 - c), (x, y, 1 - c)).start()
            comm.start(cins, couts, csend, crecv)

        fetch = lambda j: pltpu.make_async_copy(land_ref.at[j], sbuf.at[j % 2], fsem.at[j % 2])
        for j in range(4):
            @pl.when(k == max(j - 1, 0))
            def _(j=j):
                rc(j, 2 * j + c, (x, y, c)).wait_recv()
                fetch(j).start()

        fetch(k).wait()
        tot = g_ref[...] + sbuf[k % 2].astype(F32)
        p_ref[...] = tot.astype(BF16)

        @pl.when(k == place_ref[1])
        def _():
            obuf[...] = tot
            keep = pltpu.make_async_copy(obuf, own_ref, lsem)
            keep.start()
            keep.wait()

        @pl.when((k == 3) & (i == nr - 1))
        def _():
            for j in range(4):
                rc(j, 2 * j + (1 - c), (x, y, 1 - c)).wait_send()
            comm.finish(cins, couts, csend, crecv)

    grid_spec = pltpu.PrefetchScalarGridSpec(
        num_scalar_prefetch=1, grid=(4, nr),
        in_specs=[pl.BlockSpec((None, None, tr, cdim), lambda k, i, pr: (k, pr[0], i, 0)), ANY] + [ANY] * ci,
        out_specs=[pl.BlockSpec((None, tr, cdim), lambda k, i, pr: (k, i, 0)), ANY, ANY] + [ANY] * co,
        scratch_shapes=[pltpu.VMEM((2, tr, cdim), BF16), pltpu.VMEM((tr, cdim), F32), pltpu.SemaphoreType.DMA((4,)),
                        pltpu.SemaphoreType.DMA((4,)), pltpu.SemaphoreType.DMA((2,)), pltpu.SemaphoreType.DMA(()),
                        pltpu.SemaphoreType.DMA((comm.n_sems,)),
                        pltpu.SemaphoreType.DMA((comm.n_sems,))])
    res = pl.pallas_call(
        body, grid_spec=grid_spec, name=name,
        out_shape=[S((4, r, cdim), BF16), S((r, cdim), F32), S((4, r, cdim), BF16)] + comm.out_shapes,
        input_output_aliases={3 + k: 3 + v for k, v in comm.aliases.items()},
        compiler_params=pltpu.CompilerParams(dimension_semantics=("arbitrary", "arbitrary"), has_side_effects=True))(
            place, g32.reshape(4, 2, r, cdim), g16, *comm.inputs)
    hook(res[3:])
    return res[0], res[1]


def _adamw_math(w, g, m, v):
    m = ADAM_B1 * m + (1.0 - ADAM_B1) * g
    v = ADAM_B2 * v + (1.0 - ADAM_B2) * (g * g)
    m_hat = m / (1.0 - ADAM_B1 ** ADAM_STEP)
    v_hat = v / (1.0 - ADAM_B2 ** ADAM_STEP)
    delta = -ADAM_LR * (m_hat / (jnp.sqrt(v_hat) + ADAM_EPS) + ADAM_WD * w)
    return delta, m, v


def _adamw_shard(owns, recvs, w, m, v, name, flipped=False):
    nl = w.shape[0]
    r, cdim = owns[0].shape
    tr = _row_tile(r)
    nr = r // tr

    def body(*refs):
        own_refs, recv_refs = refs[:nl], refs[nl:2 * nl]
        w_ref, m_ref, v_ref, g_out, d_out, m_out, v_out = refs[2 * nl:]
        layer = pl.program_id(0)
        g = None
        for l in range(nl):
            gl = own_refs[l][...] + recv_refs[l][0].astype(F32) + recv_refs[l][1].astype(F32) + recv_refs[l][2].astype(F32)
            g = gl if g is None else jnp.where(layer == l, gl, g)
        if flipped:
            g = g.T
        g_out[...] = g
        d_out[...], m_out[...], v_out[...] = _adamw_math(w_ref[...], g, m_ref[...], v_ref[...])

    park = lambda l: (lambda layer, i: (jnp.where(layer == l, i, jnp.where(layer < l, 0, nr - 1)), 0))
    park3 = lambda l: (lambda layer, i: (0, jnp.where(layer == l, i, jnp.where(layer < l, 0, nr - 1)), 0))
    if flipped:
        row = pl.BlockSpec((None, cdim, tr), lambda layer, i: (layer, 0, i))
    else:
        row = pl.BlockSpec((None, tr, cdim), lambda layer, i: (layer, i, 0))
    return pl.pallas_call(
        body, grid=(nl, nr), name=name,
        in_specs=[pl.BlockSpec((tr, cdim), park(l)) for l in range(nl)] + [pl.BlockSpec((3, tr, cdim), park3(l)) for l in range(nl)]
        + [row, row, row],
        out_specs=[row] * 4, out_shape=[S(w.shape, F32)] * 4,
        compiler_params=_cp("arbitrary", "arbitrary"))(*owns, *recvs, w, m, v)


def _adamw_small(galls, ws, ms, vs, name):
    n = len(galls)

    def body(*refs):
        g_refs, w_refs, m_refs, v_refs, outs = refs[:n], refs[n:2 * n], refs[2 * n:3 * n], refs[3 * n:4 * n], refs[4 * n:]
        for i in range(n):
            g = g_refs[i][0].astype(F32)
            for s in range(1, N_DEV):
                g = g + g_refs[i][s].astype(F32)
            outs[i][...] = g
            outs[n + i][...], outs[2 * n + i][...], outs[3 * n + i][...] = _adamw_math(w_refs[i][...], g, m_refs[i][...], v_refs[i][...])

    res = pl.pallas_call(body, out_shape=[S(a.shape, F32) for a in ws] * 4, name=name)(*galls, *ws, *ms, *vs)
    return [res[k * n:(k + 1) * n] for k in range(4)]


REPLICATED = ["mix_norm", "ffn_norm", "sgu_v_gain", "sgu_w_s", "sgu_b_s", "attn_q_gain", "attn_k_gain", "attn_sinks", "rel_bias",
              "ffn_conv_b"]
WEIGHTS = ["mix_norm", "ffn_norm", "sgu_w_in", "sgu_v_gain", "sgu_w_s", "sgu_b_s", "sgu_w_out", "attn_w_qkv", "attn_q_gain",
           "attn_k_gain", "attn_sinks", "attn_w_o", "rel_bias", "ffn_w_up", "ffn_conv_w", "ffn_conv_b", "ffn_w_down"]
SMALL = ["g_" + n for n in REPLICATED]
BF16_TRANSIT = {"sgu_w_s"}
SMALL_ATTN = ["g_attn_q_gain", "g_attn_k_gain", "g_attn_sinks", "g_rel_bias"]
SMALL_FFN = ["g_ffn_norm", "g_ffn_conv_b"]
SMALL_SGU = ["g_sgu_v_gain", "g_sgu_w_s", "g_sgu_b_s"]

GATHER_FIRST = ["sgu_w_in", "ffn_conv_w"]
UP0_SPLIT, UP1_SPLIT = 352, 352
RS1_WITH_SUMS = {"after_dw": "sgu_w_in"}
PLAN = {
    "sgu_in": [("ag1", "sgu_w_out"), ("ag1", "ffn_w_up0", (0, UP0_SPLIT))],
    "sgu_gate": [("ag2", "sgu_w_out"), ("ag1", "ffn_w_up0", (UP0_SPLIT, D))],
    "sgu_out": [("ag2", "ffn_w_up0"), ("ag1", "ffn_w_down0")],
    "before_ffn0": [("ag2", "ffn_w_down0")],
    "ffn0_fwd": [("agd", "attn_w_qkv"), ("ag1", "attn_w_o"), ("ag1", "ffn_w_down1"), ("ag1", "ffn_w_up1", (0, UP1_SPLIT))],
    "attn": [("ag2", "attn_w_o"), ("ag2", "ffn_w_down1"), ("ag1", "ffn_w_up1", (UP1_SPLIT, D))],
    "attn_out": [("ag2", "ffn_w_up1")],
    "attn_bwd": [("rs1", "ffn_w_down1"), ("rs1", "ffn_w_up1"), ("rs1", "attn_w_o")],
    "ffn0_bwd1": [("rs2", "ffn_w_down1"), ("rs2", "attn_w_o"), ("rs1", "attn_w_qkv")] + [("ag1", n) for n in SMALL_ATTN],
    "ffn0_bwd2": [("rs2", "ffn_w_up1"), ("rs2", "attn_w_qkv"), ("rs1", "ffn_w_down0")] + [("ag2", n) for n in SMALL_ATTN],
    "ffn0_dw_up": [("rs2", "ffn_w_down0")],
    "sgu_dgated": [("rs1", "ffn_w_up0")] + [("agd", n) for n in SMALL_FFN],
    "sgu_gate_bwd": [("rs2", "ffn_w_up0"), ("rs1", "sgu_w_out")],
    "dw_sgu_in": [("rs2", "sgu_w_out")] + [("ag1", n) for n in SMALL_SGU],
    "after_dw": [("rs1", "ffn_conv_w")] + [("ag2", n) for n in SMALL_SGU],
    "dx_sgu_in": [("rs2", "sgu_w_in"), ("rs2", "ffn_conv_w")],
    "last": [("agd", "g_mix_norm")],
}


class _Overlap:
    def __init__(self, shard, place):
        self.shard, self.place = shard, place
        self.part, self.full = {}, {}
        self.grads, self.sib, self.own, self.recv, self.sums = {}, {}, {}, {}, {}

    def w(self, n):
        return self.full[n]

    def grad(self, n, pair):
        self.grads[n] = pair

    def small(self, g_rep):
        self.shard.update(("g_" + n, a.astype(BF16) if n in BF16_TRANSIT else a) for n, a in _views2d(g_rep).items())

    def sync(self, point):
        if point in RS1_WITH_SUMS:
            n = RS1_WITH_SUMS[point]
            self.sums[n], self.own[n] = _rs1_partial(*self.grads[n], self.place, self.hook(point), point)
        else:
            _exchange(self.hook(point), point)

    def first_norm(self, x, gain):
        hn, full = _allgather([self.shard[n] for n in GATHER_FIRST], x, gain, "gather_first")
        self.full.update(zip(GATHER_FIRST, full))
        return hn

    def chip_sums(self, n):
        if n in self.sums:
            return self.sums.pop(n)
        sums, self.own[n] = _rs_partial(self.grads[n][0], self.sib.pop(n), self.place, "rs_partial_" + n)
        return sums

    def hook(self, host):
        ops = PLAN.get(host)
        if not ops:
            return None
        where = {"ag1": self.part, "ag2": self.full, "agd": self.full, "rs1": self.sib, "rs2": self.recv}
        idx = []

        def hook(results=None):
            if results is not None:
                for (kind, n, *_), i in zip(ops, idx):
                    where[kind][n] = results[i]
                return None
            comm = _Comm()
            for kind, n, *rows in ops:
                arr = {"ag1": lambda: self.shard[n], "agd": lambda: self.shard[n], "ag2": lambda: self.part.pop(n),
                       "rs1": lambda: self.grads[n][1], "rs2": lambda: self.chip_sums(n)}[kind]()
                idx.append(comm.add(kind, arr, *rows, into=self.part.pop(n) if rows and rows[0][0] else None))
            return comm

        return hook


TRANSPOSED = {"attn_w_qkv"}
PHYSICAL_T = {"ffn_w_up"}
SHARDED = {
    "sgu_w_in": ["sgu_w_in"], "sgu_w_out": ["sgu_w_out"], "attn_w_qkv": ["attn_w_qkv"], "attn_w_o": ["attn_w_o"],
    "ffn_w_up": ["ffn_w_up0", "ffn_w_up1"], "ffn_w_down": ["ffn_w_down0", "ffn_w_down1"], "ffn_conv_w": ["ffn_conv_w"],
}


def _send_views(w):
    out = {"ffn_conv_w": w["ffn_conv_w"].reshape(6, -1)}
    for name, parts in SHARDED.items():
        if name != "ffn_conv_w":
            out.update((p, (w[name][l].T if name in TRANSPOSED else w[name][l]).astype(BF16)) for l, p in enumerate(parts))
    return out


def _views2d(d):
    return {n: d[n].reshape(-1, d[n].shape[-1]) for n in REPLICATED if n in d}


def kernel(x, mix_norm, ffn_norm, sgu_w_in, sgu_v_gain, sgu_w_s, sgu_b_s, sgu_w_out, attn_w_qkv, attn_q_gain, attn_k_gain, attn_sinks, attn_w_o, rel_bias, ffn_w_up, ffn_conv_w, ffn_conv_b, ffn_w_down, loss_target, m_mix_norm, m_ffn_norm, m_sgu_w_in, m_sgu_v_gain, m_sgu_w_s, m_sgu_b_s, m_sgu_w_out, m_attn_w_qkv, m_attn_q_gain, m_attn_k_gain, m_attn_sinks, m_attn_w_o, m_rel_bias, m_ffn_w_up, m_ffn_conv_w, m_ffn_conv_b, m_ffn_w_down, v_mix_norm, v_ffn_norm, v_sgu_w_in, v_sgu_v_gain, v_sgu_w_s, v_sgu_b_s, v_sgu_w_out, v_attn_w_qkv, v_attn_q_gain, v_attn_k_gain, v_attn_sinks, v_attn_w_o, v_rel_bias, v_ffn_w_up, v_ffn_conv_w, v_ffn_conv_b, v_ffn_w_down):
    w = dict(zip(WEIGHTS, (mix_norm, ffn_norm, sgu_w_in, sgu_v_gain, sgu_w_s, sgu_b_s, sgu_w_out, attn_w_qkv, attn_q_gain, attn_k_gain,
                           attn_sinks, attn_w_o, rel_bias, ffn_w_up, ffn_conv_w, ffn_conv_b, ffn_w_down)))
    m = dict(zip(WEIGHTS, (m_mix_norm, m_ffn_norm, m_sgu_w_in, m_sgu_v_gain, m_sgu_w_s, m_sgu_b_s, m_sgu_w_out, m_attn_w_qkv, m_attn_q_gain,
                           m_attn_k_gain, m_attn_sinks, m_attn_w_o, m_rel_bias, m_ffn_w_up, m_ffn_conv_w, m_ffn_conv_b, m_ffn_w_down)))
    v = dict(zip(WEIGHTS, (v_mix_norm, v_ffn_norm, v_sgu_w_in, v_sgu_v_gain, v_sgu_w_s, v_sgu_b_s, v_sgu_w_out, v_attn_w_qkv, v_attn_q_gain,
                           v_attn_k_gain, v_attn_sinks, v_attn_w_o, v_rel_bias, v_ffn_w_up, v_ffn_conv_w, v_ffn_conv_b, v_ffn_w_down)))
    rep = {n: w[n] for n in REPLICATED}

    xi, yi, ci = lax.axis_index("x"), lax.axis_index("y"), lax.axis_index("c")
    place = jnp.stack([ci, 2 * xi + yi]).astype(jnp.int32)
    sch = _Overlap(_send_views(w), place)

    loss, grad_x, g_rep = _local_step(x[0], loss_target[0], rep, sch)
    loss = lax.psum(loss, ("x", "y", "c"))
    sch.sync("last")

    out = [{}, {}, {}, {}]
    for name, parts in SHARDED.items():
        flip = (lambda a: jnp.swapaxes(a, -1, -2)) if name in TRANSPOSED | PHYSICAL_T else (lambda a: a)
        shape = flip(w[name]).shape
        as3d = lambda a: flip(a).reshape(len(parts), -1, shape[-1])
        res = _adamw_shard([sch.own[p] for p in parts], [sch.recv[p] for p in parts], as3d(w[name]), as3d(m[name]), as3d(v[name]),
                           "adamw_" + name, flipped=name in PHYSICAL_T)
        for o, r in zip(out, res):
            o[name] = flip(r.reshape(shape))
    small = _adamw_small([sch.full[n] for n in SMALL], *[list(_views2d(d).values()) for d in (rep, m, v)], "adamw_small")
    for o, res in zip(out, small):
        o.update((n, r.reshape(w[n].shape)) for n, r in zip(REPLICATED, res))

    return (loss, grad_x[None], *[out[0][n] for n in WEIGHTS], *[out[1][n] for n in WEIGHTS],
            *[out[2][n] for n in WEIGHTS], *[out[3][n] for n in WEIGHTS])
```
